```python
import math
import jax, jax.numpy as jnp
from jax import lax
import numpy as np

D_MODEL = 1024
BATCH = 16
SEQ = 2048
DEPTH = 1

D_MIX = D_MODEL
D_SSM = D_MIX // 2
D_ATTN = D_MIX // 2
SSM_GROUP_CH = 16
SSM_GROUPS = D_SSM // SSM_GROUP_CH
SSM_STATE = 64
HEAD_DIM = 64
N_HEADS = D_ATTN // HEAD_DIM
KV_HEADS = 2
Q_PER_KV = N_HEADS // KV_HEADS
WINDOW = 128
BLOCK = 128
D_PLE = 256
EPS = 1e-6

SPLIT_SIZES = (D_SSM, D_SSM, N_HEADS * HEAD_DIM, KV_HEADS * HEAD_DIM, KV_HEADS * HEAD_DIM, D_ATTN)
SPLIT_IDX = tuple(int(s) for s in np.cumsum(SPLIT_SIZES)[:-1])
D_IN = sum(SPLIT_SIZES)

kernel_name = "hymba_s5_swa_sink_alibi_layer"


def rms_norm(x, g):
    xf = x.astype(jnp.float32)
    y = xf * lax.rsqrt(jnp.mean(xf * xf, axis=-1, keepdims=True) + EPS)
    return (y * g.astype(jnp.float32)).astype(x.dtype)


def alibi_slopes(n_heads):
    return jnp.exp2(-8.0 * (jnp.arange(n_heads, dtype=jnp.float32) + 1.0) / n_heads)


def s5_branch(u, lam_re, lam_im, log_step, b_re, b_im, c_re, c_im, d, w_glu, b_glu):
    f32 = jnp.float32
    bsz, seq, _ = u.shape
    uf = u.astype(f32).reshape(bsz, seq, SSM_GROUPS, SSM_GROUP_CH)
    lam = lax.complex(lam_re.astype(f32), lam_im.astype(f32))
    step = jnp.exp(log_step.astype(f32))[:, None]
    lam_bar = jnp.exp(lam * step)
    b = lax.complex(b_re.astype(f32), b_im.astype(f32))
    b_bar = ((lam_bar - 1.0) / lam)[..., None] * b
    bu = jnp.einsum('blgp,gnp->blgn', uf.astype(jnp.complex64), b_bar)
    a = jnp.broadcast_to(lam_bar[None, None], (1, seq, SSM_GROUPS, SSM_STATE))

    def combine(left, right):
        a_l, b_l = left
        a_r, b_r = right
        return a_r * a_l, a_r * b_l + b_r

    _, states = lax.associative_scan(combine, (a, bu), axis=1)
    c = lax.complex(c_re.astype(f32), c_im.astype(f32))
    y = jnp.einsum('blgn,gpn->blgp', states, c).real \
        + d.astype(f32).reshape(SSM_GROUPS, SSM_GROUP_CH) * uf
    y = y.reshape(bsz, seq, D_SSM)
    g = jax.nn.gelu(y)
    out = g * jax.nn.sigmoid(g @ w_glu.astype(f32) + b_glu.astype(f32))
    return out.astype(u.dtype)


def swa_branch(q, k, v, sinks):
    f32 = jnp.float32
    bsz, seq = q.shape[:2]
    nb = seq // BLOCK
    qb = q.reshape(bsz, nb, BLOCK, KV_HEADS, Q_PER_KV, HEAD_DIM)

    def band(t):
        tb = t.reshape(bsz, nb, BLOCK, KV_HEADS, HEAD_DIM)
        prev = jnp.concatenate([jnp.zeros_like(tb[:, :1]), tb[:, :-1]], axis=1)
        return jnp.concatenate([prev, tb], axis=2)

    kb, vb = band(k), band(v)
    scale = 1.0 / math.sqrt(HEAD_DIM)
    scores = jnp.einsum('bnqkgd,bnskd->bnkgqs', qb, kb, preferred_element_type=f32) * scale
    q_idx = jnp.arange(BLOCK)[:, None]
    s_idx = jnp.arange(2 * BLOCK)[None, :]
    dist = q_idx + BLOCK - s_idx
    valid = (dist >= 0) & (dist < WINDOW)
    block_ok = (jnp.arange(nb)[:, None] > 0) | (jnp.arange(2 * BLOCK)[None, :] >= BLOCK)
    mask = valid[None, :, :] & block_ok[:, None, :]
    slopes = alibi_slopes(N_HEADS).reshape(KV_HEADS, Q_PER_KV)
    bias = -slopes[:, :, None, None] * dist.astype(f32)[None, None]
    scores = jnp.where(mask[None, :, None, None], scores + bias[None, None], -jnp.inf)
    sink = sinks.astype(f32).reshape(KV_HEADS, Q_PER_KV)[None, None, :, :, None, None]
    m = jnp.maximum(jnp.max(scores, axis=-1, keepdims=True), sink)
    e = jnp.exp(scores - m)
    probs = e / (jnp.sum(e, axis=-1, keepdims=True) + jnp.exp(sink - m))
    out = jnp.einsum('bnkgqs,bnskd->bnqkgd', probs.astype(v.dtype), vb)
    return out.reshape(bsz, seq, N_HEADS * HEAD_DIM)


def _fwd_setup_inputs(seed: int = 0) -> dict:
    key = jax.random.key(seed)
    ks = jax.random.split(key, 24)
    f32 = jnp.float32
    nrm = lambda k, shape, s: jax.random.normal(k, shape, f32) * s
    x = jax.random.normal(ks[0], (BATCH, SEQ, D_MODEL), f32)
    p = jax.random.normal(ks[1], (DEPTH, BATCH, SEQ, D_PLE), f32)
    pre_norm_g = 1.0 + nrm(ks[2], (DEPTH, D_MODEL), 0.02)
    w_in = nrm(ks[3], (DEPTH, D_MODEL, D_IN), D_MODEL ** -0.5)
    n = jnp.arange(SSM_STATE, dtype=f32)
    ssm_lam_re = -0.5 * jnp.exp(nrm(ks[4], (DEPTH, SSM_GROUPS, SSM_STATE), 0.05))
    ssm_lam_im = jnp.pi * n[None, None, :] + nrm(ks[5], (DEPTH, SSM_GROUPS, SSM_STATE), 0.01)
    ssm_log_step = jax.random.uniform(ks[6], (DEPTH, SSM_GROUPS), f32, math.log(1e-3), math.log(1e-1))
    bs = (2.0 * SSM_GROUP_CH) ** -0.5
    ssm_b_re = nrm(ks[7], (DEPTH, SSM_GROUPS, SSM_STATE, SSM_GROUP_CH), bs)
    ssm_b_im = nrm(ks[8], (DEPTH, SSM_GROUPS, SSM_STATE, SSM_GROUP_CH), bs)
    cs = (2.0 * SSM_STATE) ** -0.5
    ssm_c_re = nrm(ks[9], (DEPTH, SSM_GROUPS, SSM_GROUP_CH, SSM_STATE), cs)
    ssm_c_im = nrm(ks[10], (DEPTH, SSM_GROUPS, SSM_GROUP_CH, SSM_STATE), cs)
    ssm_d = nrm(ks[11], (DEPTH, D_SSM), 1.0)
    ssm_w_glu = nrm(ks[12], (DEPTH, D_SSM, D_SSM), D_SSM ** -0.5)
    ssm_b_glu = nrm(ks[13], (DEPTH, D_SSM), 0.01)
    attn_sinks = nrm(ks[14], (DEPTH, N_HEADS), 1.0)
    w_out = nrm(ks[15], (DEPTH, D_MIX, D_MODEL), D_MIX ** -0.5)
    post_norm_g = 1.0 + nrm(ks[16], (DEPTH, D_MODEL), 0.02)
    pl_w_proj = nrm(ks[17], (DEPTH, D_PLE, D_MODEL), D_PLE ** -0.5)
    pl_w_gate = nrm(ks[18], (DEPTH, D_MODEL, D_MODEL), D_MODEL ** -0.5)
    pl_b_gate = nrm(ks[19], (DEPTH, D_MODEL), 0.01)
    return {"x": x, "p": p, "pre_norm_g": pre_norm_g, "w_in": w_in,
            "ssm_lam_re": ssm_lam_re, "ssm_lam_im": ssm_lam_im, "ssm_log_step": ssm_log_step,
            "ssm_b_re": ssm_b_re, "ssm_b_im": ssm_b_im, "ssm_c_re": ssm_c_re, "ssm_c_im": ssm_c_im,
            "ssm_d": ssm_d, "ssm_w_glu": ssm_w_glu, "ssm_b_glu": ssm_b_glu,
            "attn_sinks": attn_sinks, "w_out": w_out, "post_norm_g": post_norm_g,
            "pl_w_proj": pl_w_proj, "pl_w_gate": pl_w_gate, "pl_b_gate": pl_b_gate}


def _fwd_reference(x, p, pre_norm_g, w_in, ssm_lam_re, ssm_lam_im, ssm_log_step, ssm_b_re, ssm_b_im,
              ssm_c_re, ssm_c_im, ssm_d, ssm_w_glu, ssm_b_glu, attn_sinks, w_out, post_norm_g,
              pl_w_proj, pl_w_gate, pl_b_gate):
    bsz, seq, _ = x.shape
    h = x
    for i in range(DEPTH):
        hn = rms_norm(h, pre_norm_g[i])
        proj = hn @ w_in[i]
        u_ssm, z_ssm, q, k, v, z_attn = jnp.split(proj, SPLIT_IDX, axis=-1)
        ssm_out = s5_branch(u_ssm, ssm_lam_re[i], ssm_lam_im[i], ssm_log_step[i],
                            ssm_b_re[i], ssm_b_im[i], ssm_c_re[i], ssm_c_im[i], ssm_d[i],
                            ssm_w_glu[i], ssm_b_glu[i]) * jax.nn.silu(z_ssm)
        attn_out = swa_branch(q.reshape(bsz, seq, N_HEADS, HEAD_DIM),
                              k.reshape(bsz, seq, KV_HEADS, HEAD_DIM),
                              v.reshape(bsz, seq, KV_HEADS, HEAD_DIM),
                              attn_sinks[i]) * jax.nn.silu(z_attn)
        mixed = jnp.concatenate([ssm_out, attn_out], axis=-1) @ w_out[i]
        h = h + rms_norm(mixed, post_norm_g[i])
        gate = jax.nn.sigmoid(h @ pl_w_gate[i] + pl_b_gate[i])
        h = h + gate * (p[i] @ pl_w_proj[i])
    return h


import jax as _jax
import jax.numpy as _jnp

TWIN_FORMAT = 'train_step'
FWD_PARAMS = ['x', 'p', 'pre_norm_g', 'w_in', 'ssm_lam_re', 'ssm_lam_im', 'ssm_log_step', 'ssm_b_re', 'ssm_b_im', 'ssm_c_re', 'ssm_c_im', 'ssm_d', 'ssm_w_glu', 'ssm_b_glu', 'attn_sinks', 'w_out', 'post_norm_g', 'pl_w_proj', 'pl_w_gate', 'pl_b_gate']
TWIN_WEIGHTS = ['pre_norm_g', 'w_in', 'ssm_lam_re', 'ssm_lam_im', 'ssm_log_step', 'ssm_b_re', 'ssm_b_im', 'ssm_c_re', 'ssm_c_im', 'ssm_d', 'ssm_w_glu', 'ssm_b_glu', 'attn_sinks', 'w_out', 'post_norm_g', 'pl_w_proj', 'pl_w_gate', 'pl_b_gate']
TWIN_DIFF_INPUT = 'x'
TWIN_INPUTS = ['x', 'p', 'pre_norm_g', 'w_in', 'ssm_lam_re', 'ssm_lam_im', 'ssm_log_step', 'ssm_b_re', 'ssm_b_im', 'ssm_c_re', 'ssm_c_im', 'ssm_d', 'ssm_w_glu', 'ssm_b_glu', 'attn_sinks', 'w_out', 'post_norm_g', 'pl_w_proj', 'pl_w_gate', 'pl_b_gate', 'loss_target', 'm_pre_norm_g', 'm_w_in', 'm_ssm_lam_re', 'm_ssm_lam_im', 'm_ssm_log_step', 'm_ssm_b_re', 'm_ssm_b_im', 'm_ssm_c_re', 'm_ssm_c_im', 'm_ssm_d', 'm_ssm_w_glu', 'm_ssm_b_glu', 'm_attn_sinks', 'm_w_out', 'm_post_norm_g', 'm_pl_w_proj', 'm_pl_w_gate', 'm_pl_b_gate', 'v_pre_norm_g', 'v_w_in', 'v_ssm_lam_re', 'v_ssm_lam_im', 'v_ssm_log_step', 'v_ssm_b_re', 'v_ssm_b_im', 'v_ssm_c_re', 'v_ssm_c_im', 'v_ssm_d', 'v_ssm_w_glu', 'v_ssm_b_glu', 'v_attn_sinks', 'v_w_out', 'v_post_norm_g', 'v_pl_w_proj', 'v_pl_w_gate', 'v_pl_b_gate']
TWIN_OUTPUTS = ['loss', 'grad_x', 'grad_pre_norm_g', 'grad_w_in', 'grad_ssm_lam_re', 'grad_ssm_lam_im', 'grad_ssm_log_step', 'grad_ssm_b_re', 'grad_ssm_b_im', 'grad_ssm_c_re', 'grad_ssm_c_im', 'grad_ssm_d', 'grad_ssm_w_glu', 'grad_ssm_b_glu', 'grad_attn_sinks', 'grad_w_out', 'grad_post_norm_g', 'grad_pl_w_proj', 'grad_pl_w_gate', 'grad_pl_b_gate', 'delta_pre_norm_g', 'delta_w_in', 'delta_ssm_lam_re', 'delta_ssm_lam_im', 'delta_ssm_log_step', 'delta_ssm_b_re', 'delta_ssm_b_im', 'delta_ssm_c_re', 'delta_ssm_c_im', 'delta_ssm_d', 'delta_ssm_w_glu', 'delta_ssm_b_glu', 'delta_attn_sinks', 'delta_w_out', 'delta_post_norm_g', 'delta_pl_w_proj', 'delta_pl_w_gate', 'delta_pl_b_gate', 'new_m_pre_norm_g', 'new_m_w_in', 'new_m_ssm_lam_re', 'new_m_ssm_lam_im', 'new_m_ssm_log_step', 'new_m_ssm_b_re', 'new_m_ssm_b_im', 'new_m_ssm_c_re', 'new_m_ssm_c_im', 'new_m_ssm_d', 'new_m_ssm_w_glu', 'new_m_ssm_b_glu', 'new_m_attn_sinks', 'new_m_w_out', 'new_m_post_norm_g', 'new_m_pl_w_proj', 'new_m_pl_w_gate', 'new_m_pl_b_gate', 'new_v_pre_norm_g', 'new_v_w_in', 'new_v_ssm_lam_re', 'new_v_ssm_lam_im', 'new_v_ssm_log_step', 'new_v_ssm_b_re', 'new_v_ssm_b_im', 'new_v_ssm_c_re', 'new_v_ssm_c_im', 'new_v_ssm_d', 'new_v_ssm_w_glu', 'new_v_ssm_b_glu', 'new_v_attn_sinks', 'new_v_w_out', 'new_v_post_norm_g', 'new_v_pl_w_proj', 'new_v_pl_w_gate', 'new_v_pl_b_gate']
TWIN_LEAF_KINDS = {'loss': 'loss', 'grad_x': 'grad_x', 'grad_pre_norm_g': 'grad_w', 'grad_w_in': 'grad_w', 'grad_ssm_lam_re': 'grad_w', 'grad_ssm_lam_im': 'grad_w', 'grad_ssm_log_step': 'grad_w', 'grad_ssm_b_re': 'grad_w', 'grad_ssm_b_im': 'grad_w', 'grad_ssm_c_re': 'grad_w', 'grad_ssm_c_im': 'grad_w', 'grad_ssm_d': 'grad_w', 'grad_ssm_w_glu': 'grad_w', 'grad_ssm_b_glu': 'grad_w', 'grad_attn_sinks': 'grad_w', 'grad_w_out': 'grad_w', 'grad_post_norm_g': 'grad_w', 'grad_pl_w_proj': 'grad_w', 'grad_pl_w_gate': 'grad_w', 'grad_pl_b_gate': 'grad_w', 'delta_pre_norm_g': 'delta_w', 'delta_w_in': 'delta_w', 'delta_ssm_lam_re': 'delta_w', 'delta_ssm_lam_im': 'delta_w', 'delta_ssm_log_step': 'delta_w', 'delta_ssm_b_re': 'delta_w', 'delta_ssm_b_im': 'delta_w', 'delta_ssm_c_re': 'delta_w', 'delta_ssm_c_im': 'delta_w', 'delta_ssm_d': 'delta_w', 'delta_ssm_w_glu': 'delta_w', 'delta_ssm_b_glu': 'delta_w', 'delta_attn_sinks': 'delta_w', 'delta_w_out': 'delta_w', 'delta_post_norm_g': 'delta_w', 'delta_pl_w_proj': 'delta_w', 'delta_pl_w_gate': 'delta_w', 'delta_pl_b_gate': 'delta_w', 'new_m_pre_norm_g': 'new_m', 'new_m_w_in': 'new_m', 'new_m_ssm_lam_re': 'new_m', 'new_m_ssm_lam_im': 'new_m', 'new_m_ssm_log_step': 'new_m', 'new_m_ssm_b_re': 'new_m', 'new_m_ssm_b_im': 'new_m', 'new_m_ssm_c_re': 'new_m', 'new_m_ssm_c_im': 'new_m', 'new_m_ssm_d': 'new_m', 'new_m_ssm_w_glu': 'new_m', 'new_m_ssm_b_glu': 'new_m', 'new_m_attn_sinks': 'new_m', 'new_m_w_out': 'new_m', 'new_m_post_norm_g': 'new_m', 'new_m_pl_w_proj': 'new_m', 'new_m_pl_w_gate': 'new_m', 'new_m_pl_b_gate': 'new_m', 'new_v_pre_norm_g': 'new_v', 'new_v_w_in': 'new_v', 'new_v_ssm_lam_re': 'new_v', 'new_v_ssm_lam_im': 'new_v', 'new_v_ssm_log_step': 'new_v', 'new_v_ssm_b_re': 'new_v', 'new_v_ssm_b_im': 'new_v', 'new_v_ssm_c_re': 'new_v', 'new_v_ssm_c_im': 'new_v', 'new_v_ssm_d': 'new_v', 'new_v_ssm_w_glu': 'new_v', 'new_v_ssm_b_glu': 'new_v', 'new_v_attn_sinks': 'new_v', 'new_v_w_out': 'new_v', 'new_v_post_norm_g': 'new_v', 'new_v_pl_w_proj': 'new_v', 'new_v_pl_w_gate': 'new_v', 'new_v_pl_b_gate': 'new_v'}


def _forward(args):
    return _fwd_reference(*[args[k] for k in FWD_PARAMS])


def _output_shape():
    out = _jax.eval_shape(lambda: _forward(_fwd_setup_inputs(0)))
    return out.shape, out.dtype

N_MICROBATCH = 1
ADAM_LR = 0.001
ADAM_B1 = 0.9
ADAM_B2 = 0.999
ADAM_EPS = 1e-08
ADAM_WD = 0.01
ADAM_STEP = 10
PER_EXAMPLE_BATCH_AXIS = {'x': 0, 'p': 1, 'loss_target': 0}
SHARED_INPUTS = []
_WEIGHT_DTYPES = {'pre_norm_g': _jnp.float32, 'w_in': _jnp.float32, 'ssm_lam_re': _jnp.float32, 'ssm_lam_im': _jnp.float32, 'ssm_log_step': _jnp.float32, 'ssm_b_re': _jnp.float32, 'ssm_b_im': _jnp.float32, 'ssm_c_re': _jnp.float32, 'ssm_c_im': _jnp.float32, 'ssm_d': _jnp.float32, 'ssm_w_glu': _jnp.float32, 'ssm_b_glu': _jnp.float32, 'attn_sinks': _jnp.float32, 'w_out': _jnp.float32, 'post_norm_g': _jnp.float32, 'pl_w_proj': _jnp.float32, 'pl_w_gate': _jnp.float32, 'pl_b_gate': _jnp.float32}
MOMENT_SCALE = {'pre_norm_g': 5.123730e-01, 'w_in': 3.338665e-01, 'ssm_lam_re': 1.935828e-02, 'ssm_lam_im': 1.839759e-02, 'ssm_log_step': 7.020334e+00, 'ssm_b_re': 1.150667e-02, 'ssm_b_im': 1.106353e-02, 'ssm_c_re': 2.374008e-02, 'ssm_c_im': 2.267345e-02, 'ssm_d': 4.823042e-01, 'ssm_w_glu': 1.017194e-01, 'ssm_b_glu': 2.215555e-01, 'attn_sinks': 2.034784e-01, 'w_out': 4.031186e-01, 'post_norm_g': 3.276898e+01, 'pl_w_proj': 4.571320e-01, 'pl_w_gate': 2.568819e-01, 'pl_b_gate': 2.907205e+00}


def _to_microbatches(a, axis):
    t = _jnp.moveaxis(a, axis, 0)
    t = t.reshape((N_MICROBATCH, t.shape[0] // N_MICROBATCH) + t.shape[1:])
    return _jnp.moveaxis(t, 1, axis + 1)


def setup_inputs(seed: int = 0) -> dict:
    inp = _fwd_setup_inputs(seed)
    key = _jax.random.fold_in(_jax.random.key(seed), 7919)
    shape, _ = _output_shape()
    out = dict(inp)
    out["loss_target"] = _jax.random.normal(_jax.random.fold_in(key, 0), shape, _jnp.float32)
    for i, name in enumerate(TWIN_WEIGHTS):
        w = inp[name].astype(_jnp.float32)
        if MOMENT_SCALE is None:
            s = _jnp.sqrt(_jnp.mean(_jnp.square(w)) + 1e-30)
        else:
            s = MOMENT_SCALE[name]
        km, kv = _jax.random.split(_jax.random.fold_in(key, i + 1))
        out[name] = w
        out["m_" + name] = s * _jax.random.normal(km, w.shape, _jnp.float32)
        out["v_" + name] = (s * s) * _jax.random.uniform(kv, w.shape, _jnp.float32, 0.5, 1.5)
    if N_MICROBATCH > 1:
        for name, axis in PER_EXAMPLE_BATCH_AXIS.items():
            out[name] = _to_microbatches(out[name], axis)
    return {'x': out['x'], 'p': out['p'], 'pre_norm_g': out['pre_norm_g'], 'w_in': out['w_in'], 'ssm_lam_re': out['ssm_lam_re'], 'ssm_lam_im': out['ssm_lam_im'], 'ssm_log_step': out['ssm_log_step'], 'ssm_b_re': out['ssm_b_re'], 'ssm_b_im': out['ssm_b_im'], 'ssm_c_re': out['ssm_c_re'], 'ssm_c_im': out['ssm_c_im'], 'ssm_d': out['ssm_d'], 'ssm_w_glu': out['ssm_w_glu'], 'ssm_b_glu': out['ssm_b_glu'], 'attn_sinks': out['attn_sinks'], 'w_out': out['w_out'], 'post_norm_g': out['post_norm_g'], 'pl_w_proj': out['pl_w_proj'], 'pl_w_gate': out['pl_w_gate'], 'pl_b_gate': out['pl_b_gate'], 'loss_target': out['loss_target'], 'm_pre_norm_g': out['m_pre_norm_g'], 'm_w_in': out['m_w_in'], 'm_ssm_lam_re': out['m_ssm_lam_re'], 'm_ssm_lam_im': out['m_ssm_lam_im'], 'm_ssm_log_step': out['m_ssm_log_step'], 'm_ssm_b_re': out['m_ssm_b_re'], 'm_ssm_b_im': out['m_ssm_b_im'], 'm_ssm_c_re': out['m_ssm_c_re'], 'm_ssm_c_im': out['m_ssm_c_im'], 'm_ssm_d': out['m_ssm_d'], 'm_ssm_w_glu': out['m_ssm_w_glu'], 'm_ssm_b_glu': out['m_ssm_b_glu'], 'm_attn_sinks': out['m_attn_sinks'], 'm_w_out': out['m_w_out'], 'm_post_norm_g': out['m_post_norm_g'], 'm_pl_w_proj': out['m_pl_w_proj'], 'm_pl_w_gate': out['m_pl_w_gate'], 'm_pl_b_gate': out['m_pl_b_gate'], 'v_pre_norm_g': out['v_pre_norm_g'], 'v_w_in': out['v_w_in'], 'v_ssm_lam_re': out['v_ssm_lam_re'], 'v_ssm_lam_im': out['v_ssm_lam_im'], 'v_ssm_log_step': out['v_ssm_log_step'], 'v_ssm_b_re': out['v_ssm_b_re'], 'v_ssm_b_im': out['v_ssm_b_im'], 'v_ssm_c_re': out['v_ssm_c_re'], 'v_ssm_c_im': out['v_ssm_c_im'], 'v_ssm_d': out['v_ssm_d'], 'v_ssm_w_glu': out['v_ssm_w_glu'], 'v_ssm_b_glu': out['v_ssm_b_glu'], 'v_attn_sinks': out['v_attn_sinks'], 'v_w_out': out['v_w_out'], 'v_post_norm_g': out['v_post_norm_g'], 'v_pl_w_proj': out['v_pl_w_proj'], 'v_pl_w_gate': out['v_pl_w_gate'], 'v_pl_b_gate': out['v_pl_b_gate']}


def _loss(weights, diff, rest, loss_target):
    with _jax.named_scope("forward"):
        args = {**rest, TWIN_DIFF_INPUT: diff, **{k: w.astype(_WEIGHT_DTYPES[k]) for k, w in weights.items()}}
        y = _forward(args)
    with _jax.named_scope("loss_head"):
        err = _jnp.square(y.astype(_jnp.float32) - loss_target)
        return 0.5 * _jnp.sum(_jnp.mean(err, axis=-1)) if err.ndim else 0.5 * err


def _adamw(w, g, m, v):
    m = ADAM_B1 * m + (1.0 - ADAM_B1) * g
    v = ADAM_B2 * v + (1.0 - ADAM_B2) * _jnp.square(g)
    m_hat = m / (1.0 - ADAM_B1 ** ADAM_STEP)
    v_hat = v / (1.0 - ADAM_B2 ** ADAM_STEP)
    delta = -ADAM_LR * (m_hat / (_jnp.sqrt(v_hat) + ADAM_EPS) + ADAM_WD * w)
    return delta, m, v


def reference(x, p, pre_norm_g, w_in, ssm_lam_re, ssm_lam_im, ssm_log_step, ssm_b_re, ssm_b_im, ssm_c_re, ssm_c_im, ssm_d, ssm_w_glu, ssm_b_glu, attn_sinks, w_out, post_norm_g, pl_w_proj, pl_w_gate, pl_b_gate, loss_target, m_pre_norm_g, m_w_in, m_ssm_lam_re, m_ssm_lam_im, m_ssm_log_step, m_ssm_b_re, m_ssm_b_im, m_ssm_c_re, m_ssm_c_im, m_ssm_d, m_ssm_w_glu, m_ssm_b_glu, m_attn_sinks, m_w_out, m_post_norm_g, m_pl_w_proj, m_pl_w_gate, m_pl_b_gate, v_pre_norm_g, v_w_in, v_ssm_lam_re, v_ssm_lam_im, v_ssm_log_step, v_ssm_b_re, v_ssm_b_im, v_ssm_c_re, v_ssm_c_im, v_ssm_d, v_ssm_w_glu, v_ssm_b_glu, v_attn_sinks, v_w_out, v_post_norm_g, v_pl_w_proj, v_pl_w_gate, v_pl_b_gate):
    given = dict(x=x, p=p, pre_norm_g=pre_norm_g, w_in=w_in, ssm_lam_re=ssm_lam_re, ssm_lam_im=ssm_lam_im, ssm_log_step=ssm_log_step, ssm_b_re=ssm_b_re, ssm_b_im=ssm_b_im, ssm_c_re=ssm_c_re, ssm_c_im=ssm_c_im, ssm_d=ssm_d, ssm_w_glu=ssm_w_glu, ssm_b_glu=ssm_b_glu, attn_sinks=attn_sinks, w_out=w_out, post_norm_g=post_norm_g, pl_w_proj=pl_w_proj, pl_w_gate=pl_w_gate, pl_b_gate=pl_b_gate, loss_target=loss_target, m_pre_norm_g=m_pre_norm_g, m_w_in=m_w_in, m_ssm_lam_re=m_ssm_lam_re, m_ssm_lam_im=m_ssm_lam_im, m_ssm_log_step=m_ssm_log_step, m_ssm_b_re=m_ssm_b_re, m_ssm_b_im=m_ssm_b_im, m_ssm_c_re=m_ssm_c_re, m_ssm_c_im=m_ssm_c_im, m_ssm_d=m_ssm_d, m_ssm_w_glu=m_ssm_w_glu, m_ssm_b_glu=m_ssm_b_glu, m_attn_sinks=m_attn_sinks, m_w_out=m_w_out, m_post_norm_g=m_post_norm_g, m_pl_w_proj=m_pl_w_proj, m_pl_w_gate=m_pl_w_gate, m_pl_b_gate=m_pl_b_gate, v_pre_norm_g=v_pre_norm_g, v_w_in=v_w_in, v_ssm_lam_re=v_ssm_lam_re, v_ssm_lam_im=v_ssm_lam_im, v_ssm_log_step=v_ssm_log_step, v_ssm_b_re=v_ssm_b_re, v_ssm_b_im=v_ssm_b_im, v_ssm_c_re=v_ssm_c_re, v_ssm_c_im=v_ssm_c_im, v_ssm_d=v_ssm_d, v_ssm_w_glu=v_ssm_w_glu, v_ssm_b_glu=v_ssm_b_glu, v_attn_sinks=v_attn_sinks, v_w_out=v_w_out, v_post_norm_g=v_post_norm_g, v_pl_w_proj=v_pl_w_proj, v_pl_w_gate=v_pl_w_gate, v_pl_b_gate=v_pl_b_gate)
    weights = {n: given[n] for n in TWIN_WEIGHTS}
    shared = {n: given[n] for n in SHARED_INPUTS}
    per_example = {n: given[n] for n in ['x', 'p']}
    grad_fn = _jax.value_and_grad(_loss, argnums=(0, 1))

    def one_microbatch(ex, loss_target):
        ex = dict(ex)
        diff = ex.pop(TWIN_DIFF_INPUT)
        return grad_fn(weights, diff, {**shared, **ex}, loss_target)

    if N_MICROBATCH == 1:
        loss, (grad_w, grad_x) = one_microbatch(per_example, given["loss_target"])
    else:
        def body(carry, xs):
            loss_sum, grad_sum = carry
            l_k, (gw_k, gx_k) = one_microbatch(xs[0], xs[1])
            with _jax.named_scope("update"):
                return (loss_sum + l_k, _jax.tree.map(_jnp.add, grad_sum, gw_k)), gx_k

        init = (_jnp.zeros((), _jnp.float32), _jax.tree.map(_jnp.zeros_like, weights))
        (loss, grad_w), grad_x = _jax.lax.scan(body, init, (per_example, given["loss_target"]))
    with _jax.named_scope("update"):
        delta_w, new_m, new_v = {}, {}, {}
        for n in TWIN_WEIGHTS:
            delta_w[n], new_m[n], new_v[n] = _adamw(weights[n], grad_w[n], given["m_" + n], given["v_" + n])
    return (loss, grad_x, *[grad_w[n] for n in TWIN_WEIGHTS], *[delta_w[n] for n in TWIN_WEIGHTS],
            *[new_m[n] for n in TWIN_WEIGHTS], *[new_v[n] for n in TWIN_WEIGHTS])
```

```python
import functools
import math

import jax
import jax.numpy as jnp
from jax import lax
from jax.experimental import pallas as pl
from jax.experimental.pallas import tpu as pltpu

F32 = jnp.float32
BF16 = jnp.bfloat16

D_MODEL = 1024
D_SSM = 512
D_ATTN = 512
SSM_GROUPS = 32
SSM_GROUP_CH = 16
SSM_STATE = 64
SSM_LANES = SSM_GROUPS * SSM_STATE
HEAD_DIM = 64
N_HEADS = 8
KV_HEADS = 2
Q_PER_KV = 4
WINDOW = 128
BLOCK = 128
D_PLE = 256
D_IN = 2304
EPS = 1e-6
ATTN_SCALE = 1.0 / math.sqrt(HEAD_DIM)

ADAM_LR = 0.001
ADAM_B1 = 0.9
ADAM_B2 = 0.999
ADAM_EPS = 1e-08
ADAM_WD = 0.01
ADAM_STEP = 10

N_CHIPS = 4
LANES = 128
SCAN_CHUNKS = 8
SCAN_TILE_STEPS = 16
SCAN_LANE_CHUNK = 512
MIB = 2 ** 20
MESH = pl.DeviceIdType.MESH


def _dot(a, b):
    return jnp.dot(a, b, preferred_element_type=F32)


def _dot_nt(a, b):
    return lax.dot_general(a, b, (((1,), (1,)), ((), ())), preferred_element_type=F32)


def _dot_tn(a, b):
    return lax.dot_general(a, b, (((0,), (0,)), ((), ())), preferred_element_type=F32)


def _params(vmem_mib, semantics=None):
    kw = dict(vmem_limit_bytes=vmem_mib * MIB)
    if semantics is not None:
        kw["dimension_semantics"] = semantics
    return pltpu.CompilerParams(**kw)


def _full(shape):
    nd = len(shape)
    return pl.BlockSpec(shape, lambda *_: (0,) * nd)


def _rows(tm, width):
    return pl.BlockSpec((tm, width), lambda i: (i, 0))


def _sds(shape, dtype=F32):
    return jax.ShapeDtypeStruct(shape, dtype)


def _silu(z):
    return z * jax.nn.sigmoid(z)


def _in_proj(x2d, g1, w_in):
    rows = x2d.shape[0]
    tm = 512

    def body(x_ref, g_ref, w_ref, u_ref, zs_ref, q_ref, k_ref, v_ref, za_ref):
        x = x_ref[...]
        r = lax.rsqrt(jnp.mean(x * x, axis=-1, keepdims=True) + EPS)
        hn = (x * r * g_ref[...]).astype(BF16)

        def proj(a, b):
            return _dot(hn, w_ref[:, a:b])

        u_ref[...] = proj(0, 512)
        zs_ref[...] = proj(512, 1024)
        q_ref[...] = proj(1024, 1536).astype(BF16)
        k_ref[...] = proj(1536, 1664).astype(BF16)
        v_ref[...] = proj(1664, 1792).astype(BF16)
        za_ref[...] = proj(1792, 2304)

    return pl.pallas_call(
        body, name="in_proj", grid=(rows // tm,),
        in_specs=[_rows(tm, D_MODEL), _full((1, D_MODEL)), _full((D_MODEL, D_IN))],
        out_specs=[_rows(tm, 512), _rows(tm, 512), _rows(tm, 512), _rows(tm, 128), _rows(tm, 128), _rows(tm, 512)],
        out_shape=[_sds((rows, 512)), _sds((rows, 512)), _sds((rows, 512), BF16), _sds((rows, 128), BF16),
                   _sds((rows, 128), BF16), _sds((rows, 512))],
        compiler_params=_params(40, ("arbitrary",)),
    )(x2d, g1, w_in)


def _in_proj_bwd(x2d, dh1, g1, w_in, du, dzs, dq, dk, dv, dza):
    rows = x2d.shape[0]
    tm = 256
    pieces = ((0, 512), (512, 1024), (1024, 1536), (1536, 1664), (1664, 1792), (1792, 2304))

    def body(x_ref, dh1_ref, g_ref, w_ref, du_ref, dzs_ref, dq_ref, dk_ref, dv_ref, dza_ref, gx_ref, dw_ref, dg_ref):
        @pl.when(pl.program_id(0) == 0)
        def _():
            dw_ref[...] = jnp.zeros_like(dw_ref)
            dg_ref[...] = jnp.zeros_like(dg_ref)

        x = x_ref[...]
        g = g_ref[...]
        r = lax.rsqrt(jnp.mean(x * x, axis=-1, keepdims=True) + EPS)
        xr = x * r
        hn = (xr * g).astype(BF16)
        dhn = jnp.zeros((tm, D_MODEL), F32)
        for (a, b), ref in zip(pieces, (du_ref, dzs_ref, dq_ref, dk_ref, dv_ref, dza_ref)):
            piece = ref[...].astype(BF16)
            dhn = dhn + _dot_nt(piece, w_ref[:, a:b])
            dw_ref[:, a:b] += _dot_tn(hn, piece)
        dg_ref[...] += jnp.sum(dhn * xr, axis=0, keepdims=True)
        a_ = dhn * g
        gx_ref[...] = dh1_ref[...] + r * a_ - xr * (r * jnp.mean(a_ * xr, axis=-1, keepdims=True))

    return pl.pallas_call(
        body, name="in_proj_bwd", grid=(rows // tm,),
        in_specs=[_rows(tm, D_MODEL), _rows(tm, D_MODEL), _full((1, D_MODEL)), _full((D_MODEL, D_IN)),
                  _rows(tm, 512), _rows(tm, 512), _rows(tm, 512), _rows(tm, 128), _rows(tm, 128), _rows(tm, 512)],
        out_specs=[_rows(tm, D_MODEL), _full((D_MODEL, D_IN)), _full((1, D_MODEL))],
        out_shape=[_sds((rows, D_MODEL)), _sds((D_MODEL, D_IN)), _sds((1, D_MODEL))],
        compiler_params=_params(52, ("arbitrary",)),
    )(x2d, dh1, g1, w_in, du, dzs, dq, dk, dv, dza)


def _s5_discretize(lam_re, lam_im, log_step, b_re, b_im):
    step = jnp.exp(log_step)
    a_re = lam_re * step
    a_im = lam_im * step
    mag = jnp.exp(a_re)
    lbar_re = mag * jnp.cos(a_im)
    lbar_im = mag * jnp.sin(a_im)
    n_re = lbar_re - 1.0
    den = lam_re * lam_re + lam_im * lam_im
    f_re = (n_re * lam_re + lbar_im * lam_im) / den
    f_im = (lbar_im * lam_re - n_re * lam_im) / den
    bbar_re = f_re * b_re - f_im * b_im
    bbar_im = f_re * b_im + f_im * b_re
    return lbar_re, lbar_im, bbar_re, bbar_im


_S5_PARAM_SHAPES = ((SSM_GROUPS, 1, SSM_STATE), (SSM_GROUPS, 1, SSM_STATE), (SSM_GROUPS, 1, 1),
                    (SSM_GROUPS, SSM_GROUP_CH, SSM_STATE), (SSM_GROUPS, SSM_GROUP_CH, SSM_STATE))
_S5_DISC_SHAPES = ((SSM_GROUPS, 1, SSM_STATE), (SSM_GROUPS, 1, SSM_STATE),
                   (SSM_GROUPS, SSM_GROUP_CH, SSM_STATE), (SSM_GROUPS, SSM_GROUP_CH, SSM_STATE))


def _s5_params_fwd(lam_re, lam_im, log_step, b_re, b_im):
    def body(*refs):
        ins, outs = refs[:5], refs[5:]
        for ref, val in zip(outs, _s5_discretize(*[r[...] for r in ins])):
            ref[...] = val

    return pl.pallas_call(
        body, name="s5_params_fwd",
        in_specs=[_full(s) for s in _S5_PARAM_SHAPES], out_specs=[_full(s) for s in _S5_DISC_SHAPES],
        out_shape=[_sds(s) for s in _S5_DISC_SHAPES], compiler_params=_params(16),
    )(lam_re, lam_im, log_step, b_re, b_im)


def _s5_params_bwd(lam_re, lam_im, log_step, b_re, b_im, d_lbar_re, d_lbar_im, d_bbar_re, d_bbar_im):
    def body(*refs):
        ins, cts, outs = refs[:5], refs[5:9], refs[9:]
        _, vjp = jax.vjp(_s5_discretize, *[r[...] for r in ins])
        for ref, val in zip(outs, vjp(tuple(r[...] for r in cts))):
            ref[...] = val

    return pl.pallas_call(
        body, name="s5_params_bwd",
        in_specs=[_full(s) for s in _S5_PARAM_SHAPES + _S5_DISC_SHAPES],
        out_specs=[_full(s) for s in _S5_PARAM_SHAPES],
        out_shape=[_sds(s) for s in _S5_PARAM_SHAPES], compiler_params=_params(16),
    )(lam_re, lam_im, log_step, b_re, b_im, d_lbar_re, d_lbar_im, d_bbar_re, d_bbar_im)


def _block_diag(a):
    g, p, n = a.shape
    half = g // 2
    a = a.reshape(2, half, p, n)
    eye = jnp.eye(half, dtype=a.dtype)
    return jnp.einsum("hgpn,gk->hgpkn", a, eye).reshape(2, half * p, half * n)


def _block_diag_extract(m):
    half = SSM_GROUPS // 2
    m = m.reshape(2, half, SSM_GROUP_CH, half, SSM_STATE)
    idx = jnp.arange(half)
    return m[:, idx, :, idx, :].transpose(1, 0, 2, 3).reshape(SSM_GROUPS, SSM_GROUP_CH, SSM_STATE)


def _scan_geometry(n_seq, seq):
    slab = n_seq * SCAN_CHUNKS
    steps = seq // SCAN_CHUNKS
    tile_rows = slab * SCAN_TILE_STEPS
    n_tiles = steps // SCAN_TILE_STEPS
    return slab, steps, tile_rows, n_tiles


def _per_lane_block(body, name, a):
    spec = pl.BlockSpec((a.shape[0], LANES), lambda j: (0, j))
    return pl.pallas_call(body, name=name, grid=(a.shape[1] // LANES,), in_specs=[spec], out_specs=spec,
                          out_shape=_sds(a.shape), compiler_params=_params(32, ("arbitrary",)))(a)


def _to_scan_order(a, n_seq, seq):
    slab, steps, _, _ = _scan_geometry(n_seq, seq)

    def body(a_ref, o_ref):
        def step(t, carry):
            o_ref[pl.ds(pl.multiple_of(t * slab, slab), slab), :] = a_ref[pl.ds(t, slab, stride=steps), :]
            return carry

        lax.fori_loop(0, steps, step, 0)

    return _per_lane_block(body, "to_scan_order", a)


def _from_scan_order(a, n_seq, seq):
    slab, steps, _, _ = _scan_geometry(n_seq, seq)

    def body(a_ref, o_ref):
        def step(t, carry):
            o_ref[pl.ds(t, slab, stride=steps), :] = a_ref[pl.ds(pl.multiple_of(t * slab, slab), slab), :]
            return carry

        lax.fori_loop(0, steps, step, 0)

    return _per_lane_block(body, "from_scan_order", a)


def _complex_power(re, im, n):
    out = None
    while n:
        if n & 1:
            out = (re, im) if out is None else (out[0] * re - out[1] * im, out[0] * im + out[1] * re)
        n >>= 1
        if n:
            re, im = re * re - im * im, 2.0 * re * im
    return out


def _chunk_carry(sum_re, sum_im, carry_re, carry_im, a_re, a_im, n_seq, reverse):
    carry_re[...] = jnp.zeros_like(carry_re)
    carry_im[...] = jnp.zeros_like(carry_im)
    for s in range(n_seq):
        order = range(SCAN_CHUNKS - 2, -1, -1) if reverse else range(1, SCAN_CHUNKS)
        for c in order:
            r = s * SCAN_CHUNKS + c
            p = r + 1 if reverse else r - 1
            p_re, p_im = carry_re[p:p + 1, :], carry_im[p:p + 1, :]
            carry_re[r:r + 1, :] = a_re * p_re - a_im * p_im + sum_re[p:p + 1, :]
            carry_im[r:r + 1, :] = a_re * p_im + a_im * p_re + sum_im[p:p + 1, :]


def _s5_scan_fwd(u_scan, bm_re, bm_im, cm_re, cm_im, lbar_re, lbar_im, d_row, n_seq, seq):
    slab, steps, tile_rows, n_tiles = _scan_geometry(n_seq, seq)
    rows = u_scan.shape[0]

    def body(u_ref, bre_ref, bim_ref, cre_ref, cim_ref, lre_ref, lim_ref, d_ref, y_ref, hre_ref, him_ref,
             st_re, st_im, h0_re, h0_im, buf_re, buf_im):
        second = pl.program_id(0) == 1
        i = pl.program_id(1)

        @pl.when(jnp.logical_and(i == 0, jnp.logical_not(second)))
        def _():
            st_re[...] = jnp.zeros_like(st_re)
            st_im[...] = jnp.zeros_like(st_im)

        u = u_ref[...]
        ub = u.astype(BF16)
        for hf in range(2):
            cols = slice(hf * 1024, (hf + 1) * 1024)
            buf_re[:, cols] = _dot(ub[:, hf * 256:(hf + 1) * 256], bre_ref[hf])
            buf_im[:, cols] = _dot(ub[:, hf * 256:(hf + 1) * 256], bim_ref[hf])

        for lc in range(SSM_LANES // SCAN_LANE_CHUNK):
            cols = slice(lc * SCAN_LANE_CHUNK, (lc + 1) * SCAN_LANE_CHUNK)
            l_re = jnp.broadcast_to(lre_ref[:, cols], (slab, SCAN_LANE_CHUNK))
            l_im = jnp.broadcast_to(lim_ref[:, cols], (slab, SCAN_LANE_CHUNK))

            def step(t, carry):
                s_re, s_im = carry
                r0 = pl.multiple_of(t * slab, slab)
                n_re = l_re * s_re - l_im * s_im + buf_re[pl.ds(r0, slab), cols]
                n_im = l_re * s_im + l_im * s_re + buf_im[pl.ds(r0, slab), cols]
                buf_re[pl.ds(r0, slab), cols] = n_re
                buf_im[pl.ds(r0, slab), cols] = n_im
                return n_re, n_im

            s_re, s_im = lax.fori_loop(0, SCAN_TILE_STEPS, step, (st_re[:, cols], st_im[:, cols]), unroll=True)
            st_re[:, cols] = s_re
            st_im[:, cols] = s_im

        @pl.when(jnp.logical_and(i == n_tiles - 1, jnp.logical_not(second)))
        def _():
            a_re, a_im = _complex_power(lre_ref[...], lim_ref[...], steps)
            _chunk_carry(st_re, st_im, h0_re, h0_im, a_re, a_im, n_seq, reverse=False)
            st_re[...] = h0_re[...]
            st_im[...] = h0_im[...]

        @pl.when(second)
        def _():
            h_re = buf_re[...].astype(BF16)
            h_im = buf_im[...].astype(BF16)
            hre_ref[...] = h_re
            him_ref[...] = h_im
            for hf in range(2):
                cols = slice(hf * 1024, (hf + 1) * 1024)
                ycols = slice(hf * 256, (hf + 1) * 256)
                y_ref[:, ycols] = (_dot_nt(h_re[:, cols], cre_ref[hf]) - _dot_nt(h_im[:, cols], cim_ref[hf])
                                   + d_ref[:, ycols] * u[:, ycols])

    tile = lambda w: pl.BlockSpec((tile_rows, w), lambda p, i: (i, 0))
    out_tile = lambda w: pl.BlockSpec((tile_rows, w), lambda p, i: (i * p, 0))
    mat = _full((2, 256, 1024))
    return pl.pallas_call(
        body, name="s5_scan_fwd", grid=(2, n_tiles),
        in_specs=[tile(512), mat, mat, mat, mat, _full((1, SSM_LANES)), _full((1, SSM_LANES)), _full((1, 512))],
        out_specs=[out_tile(512), out_tile(SSM_LANES), out_tile(SSM_LANES)],
        out_shape=[_sds((rows, 512)), _sds((rows, SSM_LANES), BF16), _sds((rows, SSM_LANES), BF16)],
        scratch_shapes=[pltpu.VMEM((slab, SSM_LANES), F32)] * 4 + [pltpu.VMEM((tile_rows, SSM_LANES), F32)] * 2,
        compiler_params=_params(40, ("arbitrary", "arbitrary")),
    )(u_scan, bm_re, bm_im, cm_re, cm_im, lbar_re, lbar_im, d_row)


def _s5_scan_bwd(dy_scan, u_scan, h_re, h_im, bm_re, bm_im, cm_re, cm_im, lbar_re, lbar_im, d_row, n_seq, seq):
    slab, steps, tile_rows, n_tiles = _scan_geometry(n_seq, seq)
    rows = u_scan.shape[0]

    def body(dy_ref, u_ref, hre_ref, him_ref, bre_ref, bim_ref, cre_ref, cim_ref, lre_ref, lim_ref, d_ref,
             du_ref, dbre_ref, dbim_ref, dcre_ref, dcim_ref, dlre_ref, dlim_ref, dd_ref,
             st_re, st_im, g0_re, g0_im, acc_re, acc_im, buf_re, buf_im):
        second = pl.program_id(0) == 1
        i = pl.program_id(1)

        @pl.when(jnp.logical_and(i == 0, jnp.logical_not(second)))
        def _():
            st_re[...] = jnp.zeros_like(st_re)
            st_im[...] = jnp.zeros_like(st_im)
            acc_re[...] = jnp.zeros_like(acc_re)
            acc_im[...] = jnp.zeros_like(acc_im)
            for ref in (dbre_ref, dbim_ref, dcre_ref, dcim_ref, dd_ref):
                ref[...] = jnp.zeros_like(ref)

        dy = dy_ref[...]
        dyb = dy.astype(BF16)
        for hf in range(2):
            cols = slice(hf * 1024, (hf + 1) * 1024)
            buf_re[:, cols] = _dot(dyb[:, hf * 256:(hf + 1) * 256], cre_ref[hf])
            buf_im[:, cols] = -_dot(dyb[:, hf * 256:(hf + 1) * 256], cim_ref[hf])

        for lc in range(SSM_LANES // SCAN_LANE_CHUNK):
            cols = slice(lc * SCAN_LANE_CHUNK, (lc + 1) * SCAN_LANE_CHUNK)
            l_re = jnp.broadcast_to(lre_ref[:, cols], (slab, SCAN_LANE_CHUNK))
            l_im = jnp.broadcast_to(lim_ref[:, cols], (slab, SCAN_LANE_CHUNK))

            def step(k, carry):
                s_re, s_im, a_re, a_im = carry
                r0 = pl.multiple_of((SCAN_TILE_STEPS - 1 - k) * slab, slab)
                hr = hre_ref[pl.ds(r0, slab), cols].astype(F32)
                hi = him_ref[pl.ds(r0, slab), cols].astype(F32)
                a_re = a_re + s_re * hr + s_im * hi
                a_im = a_im + s_im * hr - s_re * hi
                n_re = l_re * s_re + l_im * s_im + buf_re[pl.ds(r0, slab), cols]
                n_im = l_re * s_im - l_im * s_re + buf_im[pl.ds(r0, slab), cols]
                buf_re[pl.ds(r0, slab), cols] = n_re
                buf_im[pl.ds(r0, slab), cols] = n_im
                return n_re, n_im, a_re, a_im

            zero = jnp.zeros((slab, SCAN_LANE_CHUNK), F32)
            s_re, s_im, a_re, a_im = lax.fori_loop(
                0, SCAN_TILE_STEPS, step, (st_re[:, cols], st_im[:, cols], zero, zero), unroll=True)
            st_re[:, cols] = s_re
            st_im[:, cols] = s_im

            @pl.when(second)
            def _():
                acc_re[:, cols] += a_re
                acc_im[:, cols] += a_im

        @pl.when(jnp.logical_and(i == n_tiles - 1, jnp.logical_not(second)))
        def _():
            p_re, p_im = _complex_power(lre_ref[...], lim_ref[...], steps)
            _chunk_carry(st_re, st_im, g0_re, g0_im, p_re, -p_im, n_seq, reverse=True)
            st_re[...] = g0_re[...]
            st_im[...] = g0_im[...]

        @pl.when(second)
        def _():
            u = u_ref[...]
            ub = u.astype(BF16)
            g_re = buf_re[...].astype(BF16)
            g_im = buf_im[...].astype(BF16)
            dd_ref[...] += jnp.sum(dy * u, axis=0, keepdims=True)
            for hf in range(2):
                cols = slice(hf * 1024, (hf + 1) * 1024)
                ycols = slice(hf * 256, (hf + 1) * 256)
                du_ref[:, ycols] = (_dot_nt(g_re[:, cols], bre_ref[hf]) + _dot_nt(g_im[:, cols], bim_ref[hf])
                                    + d_ref[:, ycols] * dy[:, ycols])
                dbre_ref[hf] += _dot_tn(ub[:, ycols], g_re[:, cols])
                dbim_ref[hf] += _dot_tn(ub[:, ycols], g_im[:, cols])
                dcre_ref[hf] += _dot_tn(dyb[:, ycols], hre_ref[:, cols])
                dcim_ref[hf] -= _dot_tn(dyb[:, ycols], him_ref[:, cols])

        @pl.when(jnp.logical_and(i == n_tiles - 1, second))
        def _():
            dlre_ref[...] = jnp.sum(acc_re[...], axis=0, keepdims=True)
            dlim_ref[...] = jnp.sum(acc_im[...], axis=0, keepdims=True)

    tile = lambda w: pl.BlockSpec((tile_rows, w), lambda p, i: (n_tiles - 1 - i, 0))
    second_tile = lambda w: pl.BlockSpec((tile_rows, w), lambda p, i: (n_tiles - 1 - i * p, 0))
    mat = _full((2, 256, 1024))
    row = _full((1, SSM_LANES))
    return pl.pallas_call(
        body, name="s5_scan_bwd", grid=(2, n_tiles),
        in_specs=[tile(512), second_tile(512), second_tile(SSM_LANES), second_tile(SSM_LANES),
                  mat, mat, mat, mat, row, row, _full((1, 512))],
        out_specs=[second_tile(512), mat, mat, mat, mat, row, row, _full((1, 512))],
        out_shape=[_sds((rows, 512))] + [_sds((2, 256, 1024))] * 4 + [_sds((1, SSM_LANES))] * 2 + [_sds((1, 512))],
        scratch_shapes=[pltpu.VMEM((slab, SSM_LANES), F32)] * 6 + [pltpu.VMEM((tile_rows, SSM_LANES), F32)] * 2,
        compiler_params=_params(48, ("arbitrary", "arbitrary")),
    )(dy_scan, u_scan, h_re, h_im, bm_re, bm_im, cm_re, cm_im, lbar_re, lbar_im, d_row)


def _glu_gate(gl, a, zs):
    return gl * jax.nn.sigmoid(a) * _silu(zs)


def _glu_fwd(y, zs, w_glu, b_glu):
    rows = y.shape[0]
    tm = 512

    def body(y_ref, zs_ref, w_ref, b_ref, o_ref):
        gl = jax.nn.gelu(y_ref[...])
        a = _dot(gl.astype(BF16), w_ref[...]) + b_ref[...]
        o_ref[...] = _glu_gate(gl, a, zs_ref[...]).astype(BF16)

    return pl.pallas_call(
        body, name="glu_fwd", grid=(rows // tm,),
        in_specs=[_rows(tm, 512), _rows(tm, 512), _full((512, 512)), _full((1, 512))],
        out_specs=_rows(tm, 512), out_shape=_sds((rows, 512), BF16),
        compiler_params=_params(32, ("arbitrary",)),
    )(y, zs, w_glu, b_glu)


def _glu_bwd(y, zs, d_out, w_glu, b_glu):
    rows = y.shape[0]
    tm = 512

    def body(y_ref, zs_ref, d_ref, w_ref, b_ref, dy_ref, dzs_ref, dw_ref, db_ref):
        @pl.when(pl.program_id(0) == 0)
        def _():
            dw_ref[...] = jnp.zeros_like(dw_ref)
            db_ref[...] = jnp.zeros_like(db_ref)

        gl, gelu_vjp = jax.vjp(jax.nn.gelu, y_ref[...])
        glb = gl.astype(BF16)
        a = _dot(glb, w_ref[...]) + b_ref[...]
        _, gate_vjp = jax.vjp(_glu_gate, gl, a, zs_ref[...])
        d_gl, d_a, d_zs = gate_vjp(d_ref[...])
        dab = d_a.astype(BF16)
        d_gl = d_gl + _dot_nt(dab, w_ref[...])
        dy_ref[...] = gelu_vjp(d_gl)[0]
        dzs_ref[...] = d_zs.astype(BF16)
        dw_ref[...] += _dot_tn(glb, dab)
        db_ref[...] += jnp.sum(d_a, axis=0, keepdims=True)

    return pl.pallas_call(
        body, name="glu_bwd", grid=(rows // tm,),
        in_specs=[_rows(tm, 512), _rows(tm, 512), _rows(tm, 512), _full((512, 512)), _full((1, 512))],
        out_specs=[_rows(tm, 512), _rows(tm, 512), _full((512, 512)), _full((1, 512))],
        out_shape=[_sds((rows, 512)), _sds((rows, 512), BF16), _sds((512, 512)), _sds((1, 512))],
        compiler_params=_params(32, ("arbitrary",)),
    )(y, zs, d_out, w_glu, b_glu)


def _attn_masks(first_block):
    qi = lax.broadcasted_iota(jnp.int32, (BLOCK, BLOCK), 0)
    si = lax.broadcasted_iota(jnp.int32, (BLOCK, BLOCK), 1)
    dist_cur = qi - si
    dist_prev = dist_cur + BLOCK
    valid_cur = dist_cur >= 0
    valid_prev = jnp.logical_and(dist_prev < WINDOW, jnp.logical_not(first_block))
    return dist_cur.astype(F32), dist_prev.astype(F32), valid_cur, valid_prev


def _scores(qh, kh, slope, dist, valid):
    s = _dot_nt(qh, kh) * ATTN_SCALE - slope * dist
    return jnp.where(valid, s, -jnp.inf)


def _attn_fwd(q, k, v, za, sinks, n_seq, seq):
    nb = seq // BLOCK
    rows = q.shape[0]

    def body(q_ref, kc_ref, kp_ref, vc_ref, vp_ref, za_ref, sk_ref, o_ref, ao_ref, lse_ref):
        dist_c, dist_p, valid_c, valid_p = _attn_masks(pl.program_id(1) == 0)
        for h in range(N_HEADS):
            j = h // Q_PER_KV
            hs = slice(h * HEAD_DIM, (h + 1) * HEAD_DIM)
            js = slice(j * HEAD_DIM, (j + 1) * HEAD_DIM)
            slope = 2.0 ** (-(h + 1))
            qh = q_ref[:, hs]
            sc = _scores(qh, kc_ref[:, js], slope, dist_c, valid_c)
            sp = _scores(qh, kp_ref[:, js], slope, dist_p, valid_p)
            sink = sk_ref[0:1, h:h + 1]
            m = jnp.maximum(jnp.maximum(jnp.max(sc, axis=-1, keepdims=True), jnp.max(sp, axis=-1, keepdims=True)), sink)
            ec = jnp.exp(sc - m)
            ep = jnp.exp(sp - m)
            den = jnp.sum(ec, axis=-1, keepdims=True) + jnp.sum(ep, axis=-1, keepdims=True) + jnp.exp(sink - m)
            inv = 1.0 / den
            o_ref[:, hs] = _dot((ec * inv).astype(BF16), vc_ref[:, js]) + _dot((ep * inv).astype(BF16), vp_ref[:, js])
            lse_ref[:, h:h + 1] = m + jnp.log(den)
        ao_ref[...] = (o_ref[...] * _silu(za_ref[...])).astype(BF16)

    cur = lambda w: pl.BlockSpec((BLOCK, w), lambda b, n: (b * nb + n, 0))
    prev = lambda w: pl.BlockSpec((BLOCK, w), lambda b, n: (b * nb + jnp.maximum(n - 1, 0), 0))
    return pl.pallas_call(
        body, name="attn_fwd", grid=(n_seq, nb),
        in_specs=[cur(512), cur(128), prev(128), cur(128), prev(128), cur(512), _full((1, N_HEADS))],
        out_specs=[cur(512), cur(512), cur(N_HEADS)],
        out_shape=[_sds((rows, 512)), _sds((rows, 512), BF16), _sds((rows, N_HEADS))],
        compiler_params=_params(32, ("arbitrary", "arbitrary")),
    )(q, k, k, v, v, za, sinks)


def _attn_bwd(q, k, v, za, o, lse, d_ao, sinks, n_seq, seq):
    nb = seq // BLOCK
    rows = q.shape[0]

    def body(q_ref, q2_ref, kc_ref, kp_ref, vc_ref, vp_ref, za_ref, za2_ref, o_ref, o2_ref, lse_ref, lse2_ref,
             d_ref, d2_ref, sk_ref, dq_ref, dk_ref, dv_ref, dza_ref, dsk_ref):
        n = pl.program_id(1)

        @pl.when(jnp.logical_and(pl.program_id(0) == 0, n == 0))
        def _():
            dsk_ref[...] = jnp.zeros_like(dsk_ref)

        dist_c, dist_p, valid_c, valid_p = _attn_masks(n == 0)
        valid_next = jnp.logical_and(dist_p < WINDOW, n + 1 < nb)

        o = o_ref[...]
        _, gate_vjp = jax.vjp(lambda o_, z_: o_ * _silu(z_), o, za_ref[...])
        d_o, d_za = gate_vjp(d_ref[...])
        dza_ref[...] = d_za.astype(BF16)
        o2 = o2_ref[...]
        d_o2 = d2_ref[...] * _silu(za2_ref[...])

        for j in range(KV_HEADS):
            js = slice(j * HEAD_DIM, (j + 1) * HEAD_DIM)
            kc, kp, vc, vp = kc_ref[:, js], kp_ref[:, js], vc_ref[:, js], vp_ref[:, js]
            dk = jnp.zeros((BLOCK, HEAD_DIM), F32)
            dv = jnp.zeros((BLOCK, HEAD_DIM), F32)
            for g in range(Q_PER_KV):
                h = j * Q_PER_KV + g
                hs = slice(h * HEAD_DIM, (h + 1) * HEAD_DIM)
                slope = 2.0 ** (-(h + 1))
                qh = q_ref[:, hs]
                lse_h = lse_ref[:, h:h + 1]
                pc = jnp.exp(_scores(qh, kc, slope, dist_c, valid_c) - lse_h)
                pp = jnp.exp(_scores(qh, kp, slope, dist_p, valid_p) - lse_h)
                do_h = d_o[:, hs]
                dob = do_h.astype(BF16)
                delta = jnp.sum(do_h * o[:, hs], axis=-1, keepdims=True)
                dsc = (pc * (_dot_nt(dob, vc) - delta)).astype(BF16)
                dsp = (pp * (_dot_nt(dob, vp) - delta)).astype(BF16)
                dq_ref[:, hs] = ((_dot(dsc, kc) + _dot(dsp, kp)) * ATTN_SCALE).astype(BF16)
                p_sink = jnp.exp(sk_ref[0:1, h:h + 1] - lse_h)
                dsk_ref[0:1, h:h + 1] -= jnp.sum(p_sink * delta, axis=0, keepdims=True)
                dk = dk + _dot_tn(dsc, qh)
                dv = dv + _dot_tn(pc.astype(BF16), dob)
                q2h = q2_ref[:, hs]
                p2 = jnp.exp(_scores(q2h, kc, slope, dist_p, valid_next) - lse2_ref[:, h:h + 1])
                do2_h = d_o2[:, hs]
                do2b = do2_h.astype(BF16)
                delta2 = jnp.sum(do2_h * o2[:, hs], axis=-1, keepdims=True)
                ds2 = (p2 * (_dot_nt(do2b, vc) - delta2)).astype(BF16)
                dk = dk + _dot_tn(ds2, q2h)
                dv = dv + _dot_tn(p2.astype(BF16), do2b)
            dk_ref[:, js] = (dk * ATTN_SCALE).astype(BF16)
            dv_ref[:, js] = dv.astype(BF16)

    cur = lambda w: pl.BlockSpec((BLOCK, w), lambda b, n: (b * nb + n, 0))
    prev = lambda w: pl.BlockSpec((BLOCK, w), lambda b, n: (b * nb + jnp.maximum(n - 1, 0), 0))
    nxt = lambda w: pl.BlockSpec((BLOCK, w), lambda b, n: (b * nb + jnp.minimum(n + 1, nb - 1), 0))
    return pl.pallas_call(
        body, name="attn_bwd", grid=(n_seq, nb),
        in_specs=[cur(512), nxt(512), cur(128), prev(128), cur(128), prev(128), cur(512), nxt(512),
                  cur(512), nxt(512), cur(N_HEADS), nxt(N_HEADS), cur(512), nxt(512), _full((1, N_HEADS))],
        out_specs=[cur(512), cur(128), cur(128), cur(512), _full((1, N_HEADS))],
        out_shape=[_sds((rows, 512), BF16), _sds((rows, 128), BF16), _sds((rows, 128), BF16),
                   _sds((rows, 512), BF16), _sds((1, N_HEADS))],
        compiler_params=_params(32, ("arbitrary", "arbitrary")),
    )(q, q, k, k, v, v, za, za, o, o, lse, lse, d_ao, d_ao, sinks)


def _tail(ssm_out, attn_out, x2d, p2d, target, w_out, g2, w_gate, b_gate, w_proj):
    rows = x2d.shape[0]
    tm = 256

    def body(so_ref, ao_ref, x_ref, p_ref, t_ref, wo_ref, g2_ref, wg_ref, bg_ref, wp_ref,
             dh1_ref, dso_ref, dao_ref, dwo_ref, dwg_ref, dwp_ref, dbg_ref, dg2_ref, loss_ref):
        @pl.when(pl.program_id(0) == 0)
        def _():
            for ref in (dwo_ref, dwg_ref, dwp_ref, dbg_ref, dg2_ref, loss_ref):
                ref[...] = jnp.zeros_like(ref)

        so = so_ref[...]
        ao = ao_ref[...]
        g2 = g2_ref[...]
        mixed = _dot(so, wo_ref[0:512, :]) + _dot(ao, wo_ref[512:1024, :])
        r = lax.rsqrt(jnp.mean(mixed * mixed, axis=-1, keepdims=True) + EPS)
        mr = mixed * r
        h1 = x_ref[...] + mr * g2
        h1b = h1.astype(BF16)
        gate = jax.nn.sigmoid(_dot(h1b, wg_ref[...]) + bg_ref[...])
        pb = p_ref[...].astype(BF16)
        pp = _dot(pb, wp_ref[...])
        err = h1 + gate * pp - t_ref[...]
        loss_ref[...] += 0.5 * jnp.sum(jnp.mean(err * err, axis=-1, keepdims=True), axis=0, keepdims=True)

        dh2 = err * (1.0 / D_MODEL)
        d_glin = dh2 * pp * gate * (1.0 - gate)
        d_glin_b = d_glin.astype(BF16)
        dwg_ref[...] += _dot_tn(h1b, d_glin_b)
        dbg_ref[...] += jnp.sum(d_glin, axis=0, keepdims=True)
        dwp_ref[...] += _dot_tn(pb, (dh2 * gate).astype(BF16))
        dh1 = dh2 + _dot_nt(d_glin_b, wg_ref[...])
        dh1_ref[...] = dh1
        dg2_ref[...] += jnp.sum(dh1 * mr, axis=0, keepdims=True)
        a_ = dh1 * g2
        d_mixed = (r * a_ - mr * (r * jnp.mean(a_ * mr, axis=-1, keepdims=True))).astype(BF16)
        dwo_ref[0:512, :] += _dot_tn(so, d_mixed)
        dwo_ref[512:1024, :] += _dot_tn(ao, d_mixed)
        dso_ref[...] = _dot_nt(d_mixed, wo_ref[0:512, :])
        dao_ref[...] = _dot_nt(d_mixed, wo_ref[512:1024, :])

    return pl.pallas_call(
        body, name="tail_fwd_bwd", grid=(rows // tm,),
        in_specs=[_rows(tm, 512), _rows(tm, 512), _rows(tm, D_MODEL), _rows(tm, D_PLE), _rows(tm, D_MODEL),
                  _full((D_MODEL, D_MODEL)), _full((1, D_MODEL)), _full((D_MODEL, D_MODEL)), _full((1, D_MODEL)),
                  _full((D_PLE, D_MODEL))],
        out_specs=[_rows(tm, D_MODEL), _rows(tm, 512), _rows(tm, 512), _full((D_MODEL, D_MODEL)),
                   _full((D_MODEL, D_MODEL)), _full((D_PLE, D_MODEL)), _full((1, D_MODEL)), _full((1, D_MODEL)),
                   _full((1, 1))],
        out_shape=[_sds((rows, D_MODEL)), _sds((rows, 512)), _sds((rows, 512)), _sds((D_MODEL, D_MODEL)),
                   _sds((D_MODEL, D_MODEL)), _sds((D_PLE, D_MODEL)), _sds((1, D_MODEL)), _sds((1, D_MODEL)),
                   _sds((1, 1))],
        compiler_params=_params(52, ("arbitrary",)),
    )(ssm_out, attn_out, x2d, p2d, target, w_out, g2, w_gate, b_gate, w_proj)


def _local_step(x, p, target, pre_norm_g, w_in, lam_re, lam_im, log_step, b_re, b_im, c_re, c_im, ssm_d, w_glu, b_glu,
                sinks, w_out, post_norm_g, w_proj, w_gate, b_gate):
    n_seq, seq, _ = x.shape
    rows = n_seq * seq
    x2d = x.reshape(rows, D_MODEL)
    p2d = p.reshape(rows, D_PLE)
    t2d = target.reshape(rows, D_MODEL)

    s5_params = (lam_re.reshape(SSM_GROUPS, 1, SSM_STATE), lam_im.reshape(SSM_GROUPS, 1, SSM_STATE),
                 log_step.reshape(SSM_GROUPS, 1, 1), b_re.transpose(0, 2, 1), b_im.transpose(0, 2, 1))
    lbar_re, lbar_im, bbar_re, bbar_im = _s5_params_fwd(*s5_params)
    bm_re, bm_im = _block_diag(bbar_re).astype(BF16), _block_diag(bbar_im).astype(BF16)
    cm_re, cm_im = _block_diag(c_re).astype(BF16), _block_diag(c_im).astype(BF16)
    l_re, l_im = lbar_re.reshape(1, SSM_LANES), lbar_im.reshape(1, SSM_LANES)

    u, zs, q, k, v, za = _in_proj(x2d, pre_norm_g, w_in)
    u_scan = _to_scan_order(u, n_seq, seq)
    y_scan, h_re, h_im = _s5_scan_fwd(u_scan, bm_re, bm_im, cm_re, cm_im, l_re, l_im, ssm_d, n_seq, seq)
    y = _from_scan_order(y_scan, n_seq, seq)
    ssm_out = _glu_fwd(y, zs, w_glu, b_glu)
    o, attn_out, lse = _attn_fwd(q, k, v, za, sinks, n_seq, seq)

    dh1, d_so, d_ao, d_w_out, d_w_gate, d_w_proj, d_b_gate, d_g2, loss = _tail(
        ssm_out, attn_out, x2d, p2d, t2d, w_out, post_norm_g, w_gate, b_gate, w_proj)

    dq, dk, dv, dza, d_sinks = _attn_bwd(q, k, v, za, o, lse, d_ao, sinks, n_seq, seq)
    dy, dzs, d_w_glu, d_b_glu = _glu_bwd(y, zs, d_so, w_glu, b_glu)
    dy_scan = _to_scan_order(dy, n_seq, seq)
    du_scan, d_bm_re, d_bm_im, d_cm_re, d_cm_im, d_l_re, d_l_im, d_d = _s5_scan_bwd(
        dy_scan, u_scan, h_re, h_im, bm_re, bm_im, cm_re, cm_im, l_re, l_im, ssm_d, n_seq, seq)
    du = _from_scan_order(du_scan, n_seq, seq)
    d_lam_re, d_lam_im, d_log_step, d_bt_re, d_bt_im = _s5_params_bwd(
        *s5_params, d_l_re.reshape(SSM_GROUPS, 1, SSM_STATE), d_l_im.reshape(SSM_GROUPS, 1, SSM_STATE),
        _block_diag_extract(d_bm_re), _block_diag_extract(d_bm_im))

    grad_x, d_w_in, d_g1 = _in_proj_bwd(x2d, dh1, pre_norm_g, w_in, du, dzs, dq, dk, dv, dza)
    grads = dict(
        pre_norm_g=d_g1, w_in=d_w_in, ssm_lam_re=d_lam_re.reshape(SSM_GROUPS, SSM_STATE),
        ssm_lam_im=d_lam_im.reshape(SSM_GROUPS, SSM_STATE), ssm_log_step=d_log_step.reshape(1, SSM_GROUPS),
        ssm_b_re=d_bt_re.transpose(0, 2, 1), ssm_b_im=d_bt_im.transpose(0, 2, 1),
        ssm_c_re=_block_diag_extract(d_cm_re), ssm_c_im=_block_diag_extract(d_cm_im), ssm_d=d_d,
        ssm_w_glu=d_w_glu, ssm_b_glu=d_b_glu, attn_sinks=d_sinks, w_out=d_w_out, post_norm_g=d_g2,
        pl_w_proj=d_w_proj, pl_w_gate=d_w_gate, pl_b_gate=d_b_gate)
    return grad_x.reshape(x.shape), loss, grads


_BIG = (("w_in", (D_MODEL, D_IN), 1), ("ssm_w_glu", (D_SSM, D_SSM), 0), ("w_out", (D_MODEL, D_MODEL), 0),
        ("pl_w_proj", (D_PLE, D_MODEL), 1), ("pl_w_gate", (D_MODEL, D_MODEL), 0))
_BIG_ROWS = tuple(s[0] * s[1] // N_CHIPS // LANES for _, s, _ in _BIG)
_BIG_TOTAL = sum(_BIG_ROWS)
_SMALL = (("pre_norm_g", (1, D_MODEL)), ("ssm_lam_re", (SSM_GROUPS, SSM_STATE)), ("ssm_lam_im", (SSM_GROUPS, SSM_STATE)),
          ("ssm_log_step", (1, SSM_GROUPS)), ("ssm_b_re", (SSM_GROUPS, SSM_STATE, SSM_GROUP_CH)),
          ("ssm_b_im", (SSM_GROUPS, SSM_STATE, SSM_GROUP_CH)), ("ssm_c_re", (SSM_GROUPS, SSM_GROUP_CH, SSM_STATE)),
          ("ssm_c_im", (SSM_GROUPS, SSM_GROUP_CH, SSM_STATE)), ("ssm_d", (1, D_SSM)), ("ssm_b_glu", (1, D_SSM)),
          ("attn_sinks", (1, N_HEADS)), ("post_norm_g", (1, D_MODEL)), ("pl_b_gate", (1, D_MODEL)))
_SMALL_SIZES = tuple(math.prod(s) for _, s in _SMALL)
_SMALL_USED = sum(_SMALL_SIZES) + 1
_SMALL_ROWS = -(-_SMALL_USED // (8 * LANES)) * 8
_WEIGHT_ORDER = ("pre_norm_g", "w_in", "ssm_lam_re", "ssm_lam_im", "ssm_log_step", "ssm_b_re", "ssm_b_im", "ssm_c_re",
                 "ssm_c_im", "ssm_d", "ssm_w_glu", "ssm_b_glu", "attn_sinks", "w_out", "post_norm_g", "pl_w_proj",
                 "pl_w_gate", "pl_b_gate")


def _pack_big_shards(shards):
    return jnp.concatenate([shards[name].reshape(-1, LANES) for name, _, _ in _BIG], axis=0)


def _unpack_big_shards(packed):
    out, r0 = {}, 0
    for (name, shape, axis), rows in zip(_BIG, _BIG_ROWS):
        shard_shape = tuple(d // N_CHIPS if a == axis else d for a, d in enumerate(shape))
        out[name] = packed[r0:r0 + rows].reshape(shard_shape)
        r0 += rows
    return out


def _unpack_gathered(gathered):
    out, r0 = {}, 0
    for (name, shape, axis), rows in zip(_BIG, _BIG_ROWS):
        part = gathered[:, r0:r0 + rows]
        if axis == 0:
            out[name] = part.reshape(shape)
        else:
            out[name] = part.reshape(N_CHIPS, shape[0], shape[1] // N_CHIPS).transpose(1, 0, 2).reshape(shape)
        r0 += rows
    return out


def _pack_for_scatter(full):
    parts = []
    for name, shape, axis in _BIG:
        g = full[name]
        if axis == 1:
            g = g.reshape(shape[0], N_CHIPS, shape[1] // N_CHIPS).transpose(1, 0, 2)
        parts.append(g.reshape(N_CHIPS, -1, LANES))
    return jnp.concatenate(parts, axis=1)


def _pack_small(values, last=None):
    flat = [values[name].reshape(-1) for name, _ in _SMALL]
    flat.append(jnp.zeros((1,), F32) if last is None else last.reshape(1))
    flat.append(jnp.zeros((_SMALL_ROWS * LANES - _SMALL_USED,), F32))
    return jnp.concatenate(flat).reshape(_SMALL_ROWS, LANES)


def _unpack_small(packed):
    flat = packed.reshape(-1)
    out, i0 = {}, 0
    for (name, shape), size in zip(_SMALL, _SMALL_SIZES):
        out[name] = flat[i0:i0 + size].reshape(shape)
        i0 += size
    return out, flat[i0]


def _mesh_place():
    x, y, c = lax.axis_index("x"), lax.axis_index("y"), lax.axis_index("c")
    other_chips = ((1 - x, y), (x, 1 - y), (1 - x, 1 - y))
    return x, y, c, other_chips


def _gather_weights(shard):
    rows = shard.shape[0]
    half = rows // 2

    def body(s_ref, g_ref, send_sems, recv_sems, local_sem):
        x, y, c, other_chips = _mesh_place()
        mine = pl.ds(c * half, half)

        def copy(k, chip, to, src=None):
            dst = g_ref.at[2 * chip[0] + chip[1], mine, :]
            return pltpu.make_async_remote_copy(
                src_ref=dst if src is None else src, dst_ref=dst, send_sem=send_sems.at[k], recv_sem=recv_sems.at[k],
                device_id=to, device_id_type=MESH)

        own = pltpu.make_async_copy(s_ref, g_ref.at[2 * x + y], local_sem)
        own.start()
        first = [copy(k, (x, y), (*chip, c), src=s_ref.at[mine, :]) for k, chip in enumerate(other_chips)]
        for cp in first:
            cp.start()
        passed = [copy(3 + k, chip, (x, y, 1 - c)) for k, chip in enumerate(other_chips)]
        for k in range(3):
            first[k].wait_recv()
            passed[k].start()
        for k in range(3):
            passed[k].wait_recv()
        for cp in first + passed:
            cp.wait_send()
        own.wait()

    any_spec = pl.BlockSpec(memory_space=pl.ANY)
    return pl.pallas_call(
        body, name="gather_weights", in_specs=[any_spec], out_specs=any_spec,
        out_shape=_sds((N_CHIPS, rows, LANES), shard.dtype),
        scratch_shapes=[pltpu.SemaphoreType.DMA((6,)), pltpu.SemaphoreType.DMA((6,)), pltpu.SemaphoreType.DMA],
    )(shard)


def _exchange_grads(big, small):
    rows = big.shape[1]
    half = rows // 2
    s_rows = small.shape[0]

    def body(big_ref, small_ref, out_ref, small_out_ref, land_ref,
             ga, gb, pme, send_b, recv_b, s_sib, s_chips, s_pair, send_sems, recv_sems, local_sems):
        x, y, c, other_chips = _mesh_place()
        me = 2 * x + y
        sibling = (x, y, 1 - c)
        mine = pl.ds(c * half, half)
        theirs = pl.ds((1 - c) * half, half)

        def remote(k, src, dst, to):
            return pltpu.make_async_remote_copy(src_ref=src, dst_ref=dst, send_sem=send_sems.at[k],
                                                recv_sem=recv_sems.at[k], device_id=to, device_id_type=MESH)

        def pair_sum(j, dst):
            a = pltpu.make_async_copy(big_ref.at[j, mine, :], ga, local_sems.at[0])
            b = pltpu.make_async_copy(land_ref.at[j], gb, local_sems.at[1])
            a.start()
            b.start()
            a.wait()
            b.wait()
            dst[...] = (ga[...] + gb[...]).astype(dst.dtype)

        small_swap = remote(0, small_ref, s_sib, sibling)
        big_swap = remote(1, big_ref.at[:, theirs, :], land_ref, sibling)
        small_swap.start()
        big_swap.start()

        small_swap.wait_recv()
        s_pair[...] = small_ref[...] + s_sib[...]
        small_sends = [remote(2 + k, s_pair, s_chips.at[k], (*chip, c)) for k, chip in enumerate(other_chips)]
        for cp in small_sends:
            cp.start()

        big_swap.wait_recv()
        big_sends = []
        for k, chip in enumerate(other_chips):
            pair_sum(2 * chip[0] + chip[1], send_b.at[k])
            big_sends.append(remote(5 + k, send_b.at[k], recv_b.at[k], (*chip, c)))
            big_sends[k].start()
        pair_sum(me, pme)
        for k in range(3):
            big_sends[k].wait_recv()
        pme[...] = ((pme[...] + recv_b[0].astype(F32)) + recv_b[1].astype(F32)) + recv_b[2].astype(F32)
        keep = pltpu.make_async_copy(pme, out_ref.at[mine, :], local_sems.at[0])
        keep.start()
        last_swap = remote(8, pme, out_ref.at[mine, :], sibling)
        last_swap.start()

        for k in range(3):
            small_sends[k].wait_recv()
        total = None
        for j in range(N_CHIPS):
            rel = jnp.bitwise_xor(j, me)
            term = jnp.where(rel == 0, s_pair[...],
                             jnp.where(rel == 2, s_chips[0], jnp.where(rel == 1, s_chips[1], s_chips[2])))
            total = term if total is None else total + term
        small_out_ref[...] = total

        last_swap.wait_recv()
        keep.wait()
        for cp in [small_swap, big_swap, last_swap] + small_sends + big_sends:
            cp.wait_send()

    any_spec = pl.BlockSpec(memory_space=pl.ANY)
    vmem_spec = pl.BlockSpec(memory_space=pltpu.VMEM)
    out, small_out, _ = pl.pallas_call(
        body, name="exchange_grads", in_specs=[any_spec, vmem_spec], out_specs=[any_spec, vmem_spec, any_spec],
        out_shape=[_sds((rows, LANES)), _sds((s_rows, LANES)), _sds((N_CHIPS, half, LANES))],
        scratch_shapes=[pltpu.VMEM((half, LANES), F32), pltpu.VMEM((half, LANES), F32), pltpu.VMEM((half, LANES), F32),
                        pltpu.VMEM((3, half, LANES), BF16), pltpu.VMEM((3, half, LANES), BF16),
                        pltpu.VMEM((s_rows, LANES), F32), pltpu.VMEM((3, s_rows, LANES), F32),
                        pltpu.VMEM((s_rows, LANES), F32),
                        pltpu.SemaphoreType.DMA((9,)), pltpu.SemaphoreType.DMA((9,)), pltpu.SemaphoreType.DMA((2,))],
        compiler_params=_params(40),
    )(big, small)
    return out, small_out


def _adamw(w, g, m, v, tile_rows, name):
    rows = w.shape[0]

    def body(w_ref, g_ref, m_ref, v_ref, d_ref, nm_ref, nv_ref):
        g_ = g_ref[...]
        m_ = ADAM_B1 * m_ref[...] + (1.0 - ADAM_B1) * g_
        v_ = ADAM_B2 * v_ref[...] + (1.0 - ADAM_B2) * (g_ * g_)
        m_hat = m_ / (1.0 - ADAM_B1 ** ADAM_STEP)
        v_hat = v_ / (1.0 - ADAM_B2 ** ADAM_STEP)
        d_ref[...] = -ADAM_LR * (m_hat / (jnp.sqrt(v_hat) + ADAM_EPS) + ADAM_WD * w_ref[...])
        nm_ref[...] = m_
        nv_ref[...] = v_

    spec = _rows(tile_rows, LANES)
    return pl.pallas_call(
        body, name=name, grid=(rows // tile_rows,), in_specs=[spec] * 4, out_specs=[spec] * 3,
        out_shape=[_sds((rows, LANES))] * 3, compiler_params=_params(32, ("arbitrary",)),
    )(w, g, m, v)


def kernel(x, p, pre_norm_g, w_in, ssm_lam_re, ssm_lam_im, ssm_log_step, ssm_b_re, ssm_b_im, ssm_c_re, ssm_c_im, ssm_d, ssm_w_glu, ssm_b_glu, attn_sinks, w_out, post_norm_g, pl_w_proj, pl_w_gate, pl_b_gate, loss_target, m_pre_norm_g, m_w_in, m_ssm_lam_re, m_ssm_lam_im, m_ssm_log_step, m_ssm_b_re, m_ssm_b_im, m_ssm_c_re, m_ssm_c_im, m_ssm_d, m_ssm_w_glu, m_ssm_b_glu, m_attn_sinks, m_w_out, m_post_norm_g, m_pl_w_proj, m_pl_w_gate, m_pl_b_gate, v_pre_norm_g, v_w_in, v_ssm_lam_re, v_ssm_lam_im, v_ssm_log_step, v_ssm_b_re, v_ssm_b_im, v_ssm_c_re, v_ssm_c_im, v_ssm_d, v_ssm_w_glu, v_ssm_b_glu, v_attn_sinks, v_w_out, v_post_norm_g, v_pl_w_proj, v_pl_w_gate, v_pl_b_gate):
    weights = dict(pre_norm_g=pre_norm_g, w_in=w_in, ssm_lam_re=ssm_lam_re, ssm_lam_im=ssm_lam_im,
                   ssm_log_step=ssm_log_step, ssm_b_re=ssm_b_re, ssm_b_im=ssm_b_im, ssm_c_re=ssm_c_re,
                   ssm_c_im=ssm_c_im, ssm_d=ssm_d, ssm_w_glu=ssm_w_glu, ssm_b_glu=ssm_b_glu, attn_sinks=attn_sinks,
                   w_out=w_out, post_norm_g=post_norm_g, pl_w_proj=pl_w_proj, pl_w_gate=pl_w_gate, pl_b_gate=pl_b_gate)
    m_in = dict(pre_norm_g=m_pre_norm_g, w_in=m_w_in, ssm_lam_re=m_ssm_lam_re, ssm_lam_im=m_ssm_lam_im,
                ssm_log_step=m_ssm_log_step, ssm_b_re=m_ssm_b_re, ssm_b_im=m_ssm_b_im, ssm_c_re=m_ssm_c_re,
                ssm_c_im=m_ssm_c_im, ssm_d=m_ssm_d, ssm_w_glu=m_ssm_w_glu, ssm_b_glu=m_ssm_b_glu,
                attn_sinks=m_attn_sinks, w_out=m_w_out, post_norm_g=m_post_norm_g, pl_w_proj=m_pl_w_proj,
                pl_w_gate=m_pl_w_gate, pl_b_gate=m_pl_b_gate)
    v_in = dict(pre_norm_g=v_pre_norm_g, w_in=v_w_in, ssm_lam_re=v_ssm_lam_re, ssm_lam_im=v_ssm_lam_im,
                ssm_log_step=v_ssm_log_step, ssm_b_re=v_ssm_b_re, ssm_b_im=v_ssm_b_im, ssm_c_re=v_ssm_c_re,
                ssm_c_im=v_ssm_c_im, ssm_d=v_ssm_d, ssm_w_glu=v_ssm_w_glu, ssm_b_glu=v_ssm_b_glu,
                attn_sinks=v_attn_sinks, w_out=v_w_out, post_norm_g=v_post_norm_g, pl_w_proj=v_pl_w_proj,
                pl_w_gate=v_pl_w_gate, pl_b_gate=v_pl_b_gate)
    w0 = {k: a[0] for k, a in weights.items()}
    m0 = {k: a[0] for k, a in m_in.items()}
    v0 = {k: a[0] for k, a in v_in.items()}

    w_big = _pack_big_shards(w0)
    full = _unpack_gathered(_gather_weights(w_big.astype(BF16)))
    row = lambda a: a.reshape(1, -1)
    grad_x, loss, grads = _local_step(
        x, p, loss_target, row(w0["pre_norm_g"]), full["w_in"], w0["ssm_lam_re"], w0["ssm_lam_im"], w0["ssm_log_step"],
        w0["ssm_b_re"], w0["ssm_b_im"], w0["ssm_c_re"], w0["ssm_c_im"], row(w0["ssm_d"]), full["ssm_w_glu"],
        row(w0["ssm_b_glu"]), row(w0["attn_sinks"]), full["w_out"], row(w0["post_norm_g"]), full["pl_w_proj"],
        full["pl_w_gate"], row(w0["pl_b_gate"]))

    g_big, g_small = _exchange_grads(_pack_for_scatter(grads), _pack_small(grads, last=loss))
    d_big, nm_big, nv_big = _adamw(w_big, g_big, _pack_big_shards(m0), _pack_big_shards(v0), _BIG_TOTAL // 8, "adamw_big")
    d_small, nm_small, nv_small = _adamw(_pack_small(w0), g_small, _pack_small(m0), _pack_small(v0), _SMALL_ROWS,
                                         "adamw_small")

    total_loss = _unpack_small(g_small)[1]
    results = []
    for big, small in ((g_big, g_small), (d_big, d_small), (nm_big, nm_small), (nv_big, nv_small)):
        merged = {**_unpack_big_shards(big), **_unpack_small(small)[0]}
        results.append([merged[name].reshape(weights[name].shape) for name in _WEIGHT_ORDER])
    return (total_loss, grad_x, *results[0], *results[1], *results[2], *results[3])
```

```python
import math

import jax
import jax.numpy as jnp
from jax import lax
from jax.experimental import pallas as pl
from jax.experimental.pallas import tpu as pltpu

F32 = jnp.float32
BF16 = jnp.bfloat16

D_MODEL = 1024
D_SSM = 512
D_ATTN = 512
SSM_GROUPS = 32
SSM_GROUP_CH = 16
SSM_STATE = 64
SSM_LANES = SSM_GROUPS * SSM_STATE
HEAD_DIM = 64
N_HEADS = 8
KV_HEADS = 2
Q_PER_KV = 4
WINDOW = 128
BLOCK = 128
D_PLE = 256
D_IN = 2304
EPS = 1e-6
ATTN_SCALE = 1.0 / math.sqrt(HEAD_DIM)

ADAM_LR = 0.001
ADAM_B1 = 0.9
ADAM_B2 = 0.999
ADAM_EPS = 1e-08
ADAM_WD = 0.01
ADAM_STEP = 10

N_CHIPS = 4
LANES = 128
SCAN_CHUNKS = 8
SCAN_TILE_STEPS = 16
SCAN_LANE_CHUNK = 512
MIB = 2 ** 20
MESH = pl.DeviceIdType.MESH


def _dot(a, b):
    return jnp.dot(a, b, preferred_element_type=F32)


def _dot_nt(a, b):
    return lax.dot_general(a, b, (((1,), (1,)), ((), ())), preferred_element_type=F32)


def _dot_tn(a, b):
    return lax.dot_general(a, b, (((0,), (0,)), ((), ())), preferred_element_type=F32)


def _params(vmem_mib, semantics=None):
    kw = dict(vmem_limit_bytes=vmem_mib * MIB)
    if semantics is not None:
        kw["dimension_semantics"] = semantics
    return pltpu.CompilerParams(**kw)


def _full(shape):
    nd = len(shape)
    return pl.BlockSpec(shape, lambda *_: (0,) * nd)


def _rows(tm, width):
    return pl.BlockSpec((tm, width), lambda i: (i, 0))


def _sds(shape, dtype=F32):
    return jax.ShapeDtypeStruct(shape, dtype)


def _silu(z):
    return z * jax.nn.sigmoid(z)


def _in_proj(x2d, g1, w_in_t):
    rows = x2d.shape[0]
    tm = 512

    def body(x_ref, g_ref, w_ref, u_ref, zs_ref, q_ref, k_ref, v_ref, za_ref):
        x = x_ref[...]
        r = lax.rsqrt(jnp.mean(x * x, axis=-1, keepdims=True) + EPS)
        hn = (x * r * g_ref[...]).astype(BF16)

        def proj(a, b):
            return _dot_nt(hn, w_ref[a:b, :])

        u_ref[...] = proj(0, 512)
        zs_ref[...] = proj(512, 1024)
        q_ref[...] = proj(1024, 1536).astype(BF16)
        k_ref[...] = proj(1536, 1664).astype(BF16)
        v_ref[...] = proj(1664, 1792).astype(BF16)
        za_ref[...] = proj(1792, 2304)

    return pl.pallas_call(
        body, name="in_proj", grid=(rows // tm,),
        in_specs=[_rows(tm, D_MODEL), _full((1, D_MODEL)), _full((D_IN, D_MODEL))],
        out_specs=[_rows(tm, 512), _rows(tm, 512), _rows(tm, 512), _rows(tm, 128), _rows(tm, 128), _rows(tm, 512)],
        out_shape=[_sds((rows, 512)), _sds((rows, 512)), _sds((rows, 512), BF16), _sds((rows, 128), BF16),
                   _sds((rows, 128), BF16), _sds((rows, 512))],
        compiler_params=_params(40, ("arbitrary",)),
    )(x2d, g1, w_in_t)


def _in_proj_bwd(x2d, dh1, g1, w_in_t, du, dzs, dq, dk, dv, dza):
    rows = x2d.shape[0]
    tm = 256
    pieces = ((0, 512), (512, 1024), (1024, 1536), (1536, 1664), (1664, 1792), (1792, 2304))

    def body(x_ref, dh1_ref, g_ref, w_ref, du_ref, dzs_ref, dq_ref, dk_ref, dv_ref, dza_ref, gx_ref, dw_ref, dg_ref):
        @pl.when(pl.program_id(0) == 0)
        def _():
            dw_ref[...] = jnp.zeros_like(dw_ref)
            dg_ref[...] = jnp.zeros_like(dg_ref)

        x = x_ref[...]
        g = g_ref[...]
        r = lax.rsqrt(jnp.mean(x * x, axis=-1, keepdims=True) + EPS)
        xr = x * r
        hn = (xr * g).astype(BF16)
        dhn = jnp.zeros((tm, D_MODEL), F32)
        for (a, b), ref in zip(pieces, (du_ref, dzs_ref, dq_ref, dk_ref, dv_ref, dza_ref)):
            piece = ref[...].astype(BF16)
            dhn = dhn + _dot(piece, w_ref[a:b, :])
            dw_ref[a:b, :] += _dot_tn(piece, hn)
        dg_ref[...] += jnp.sum(dhn * xr, axis=0, keepdims=True)
        a_ = dhn * g
        gx_ref[...] = dh1_ref[...] + r * a_ - xr * (r * jnp.mean(a_ * xr, axis=-1, keepdims=True))

    return pl.pallas_call(
        body, name="in_proj_bwd", grid=(rows // tm,),
        in_specs=[_rows(tm, D_MODEL), _rows(tm, D_MODEL), _full((1, D_MODEL)), _full((D_IN, D_MODEL)),
                  _rows(tm, 512), _rows(tm, 512), _rows(tm, 512), _rows(tm, 128), _rows(tm, 128), _rows(tm, 512)],
        out_specs=[_rows(tm, D_MODEL), _full((D_IN, D_MODEL)), _full((1, D_MODEL))],
        out_shape=[_sds((rows, D_MODEL)), _sds((D_IN, D_MODEL)), _sds((1, D_MODEL))],
        compiler_params=_params(52, ("arbitrary",)),
    )(x2d, dh1, g1, w_in_t, du, dzs, dq, dk, dv, dza)


def _iota(shape, axis):
    return lax.broadcasted_iota(jnp.int32, shape, axis)


def _exact_dot(a, b):
    return jnp.dot(a, b, precision=lax.Precision.HIGHEST, preferred_element_type=F32)


_HALF_GROUPS = SSM_GROUPS // 2
_N_SHIFT = SSM_STATE.bit_length() - 1
_P_SHIFT = SSM_GROUP_CH.bit_length() - 1


def _s5_operands(lam_re, lam_im, log_step, b_re, b_im, c_re, c_im):
    g, n, p = SSM_GROUPS, SSM_STATE, SSM_GROUP_CH
    gn, hn_, hp = g * n, _HALF_GROUPS * n, _HALF_GROUPS * p
    eye_g = _iota((g, g), 0) == _iota((g, g), 1)
    step = jnp.sum(jnp.where(eye_g, jnp.exp(log_step), 0.0), axis=1, keepdims=True)
    a_re = lam_re * step
    a_im = lam_im * step
    mag = jnp.exp(a_re)
    lbar_re = mag * jnp.cos(a_im)
    lbar_im = mag * jnp.sin(a_im)
    n_re = lbar_re - 1.0
    den = lam_re * lam_re + lam_im * lam_im
    f_re = (n_re * lam_re + lbar_im * lam_im) / den
    f_im = (lbar_im * lam_re - n_re * lam_im) / den

    spread_n = (_iota((n, gn), 0) == (_iota((n, gn), 1) & (n - 1))).astype(F32)
    own_g = _iota((g, gn), 0) == (_iota((g, gn), 1) >> _N_SHIFT)

    def to_row(a):
        return jnp.sum(jnp.where(own_g, _exact_dot(a, spread_n), 0.0), axis=0, keepdims=True)

    pick_g = ((_iota((gn, g), 0) >> _N_SHIFT) == _iota((gn, g), 1)).astype(F32)
    own_n = (_iota((gn, n), 0) & (n - 1)) == _iota((gn, n), 1)

    def to_col(a):
        return jnp.sum(jnp.where(own_n, _exact_dot(pick_g, a), 0.0), axis=1, keepdims=True)

    fc_re, fc_im = to_col(f_re), to_col(f_im)
    bbar_re = fc_re * b_re - fc_im * b_im
    bbar_im = fc_re * b_im + fc_im * b_re

    tile_p = (_iota((p, hp), 0) == (_iota((p, hp), 1) & (p - 1))).astype(F32)
    block_b = (_iota((hn_, hp), 0) >> _N_SHIFT) == (_iota((hn_, hp), 1) >> _P_SHIFT)
    tile_n = (_iota((n, hn_), 0) == (_iota((n, hn_), 1) & (n - 1))).astype(F32)
    block_c = (_iota((hp, hn_), 0) >> _P_SHIFT) == (_iota((hp, hn_), 1) >> _N_SHIFT)

    def embed_b(a, hf):
        return jnp.where(block_b, _exact_dot(a[hf * hn_:(hf + 1) * hn_], tile_p), 0.0)

    def embed_c(a, hf):
        return jnp.where(block_c, _exact_dot(a[hf * hp:(hf + 1) * hp], tile_n), 0.0)

    return (to_row(lbar_re), to_row(lbar_im), embed_b(bbar_re, 0), embed_b(bbar_re, 1), embed_b(bbar_im, 0),
            embed_b(bbar_im, 1), embed_c(c_re, 0), embed_c(c_re, 1), embed_c(c_im, 0), embed_c(c_im, 1))


_S5_PARAM_SHAPES = ((SSM_GROUPS, SSM_STATE), (SSM_GROUPS, SSM_STATE), (1, SSM_GROUPS),
                    (SSM_LANES, SSM_GROUP_CH), (SSM_LANES, SSM_GROUP_CH), (D_SSM, SSM_STATE), (D_SSM, SSM_STATE))
_BT_SHAPE = (2, _HALF_GROUPS * SSM_STATE, _HALF_GROUPS * SSM_GROUP_CH)
_CM_SHAPE = (2, _HALF_GROUPS * SSM_GROUP_CH, _HALF_GROUPS * SSM_STATE)
_S5_OPERAND_SHAPES = ((1, SSM_LANES), (1, SSM_LANES), _BT_SHAPE, _BT_SHAPE, _CM_SHAPE, _CM_SHAPE)


def _s5_params_fwd(*params):
    def body(*refs):
        ins, (lre_ref, lim_ref, btre_ref, btim_ref, cmre_ref, cmim_ref) = refs[:7], refs[7:]
        vals = _s5_operands(*[r[...] for r in ins])
        lre_ref[...] = vals[0]
        lim_ref[...] = vals[1]
        for ref, pair in zip((btre_ref, btim_ref, cmre_ref, cmim_ref), (vals[2:4], vals[4:6], vals[6:8], vals[8:10])):
            ref[0] = pair[0].astype(BF16)
            ref[1] = pair[1].astype(BF16)

    dtypes = (F32, F32, BF16, BF16, BF16, BF16)
    return pl.pallas_call(
        body, name="s5_params_fwd",
        in_specs=[_full(s) for s in _S5_PARAM_SHAPES], out_specs=[_full(s) for s in _S5_OPERAND_SHAPES],
        out_shape=[_sds(s, d) for s, d in zip(_S5_OPERAND_SHAPES, dtypes)], compiler_params=_params(32),
    )(*params)


def _s5_params_bwd(params, cotangents):
    def body(*refs):
        ins, (dlre, dlim, dbtre, dbtim, dcmre, dcmim), outs = refs[:7], refs[7:13], refs[13:]
        _, vjp = jax.vjp(_s5_operands, *[r[...] for r in ins])
        cts = (dlre[...], dlim[...], dbtre[0], dbtre[1], dbtim[0], dbtim[1], dcmre[0], dcmre[1], dcmim[0], dcmim[1])
        for ref, val in zip(outs, vjp(cts)):
            ref[...] = val

    return pl.pallas_call(
        body, name="s5_params_bwd",
        in_specs=[_full(s) for s in _S5_PARAM_SHAPES + _S5_OPERAND_SHAPES],
        out_specs=[_full(s) for s in _S5_PARAM_SHAPES],
        out_shape=[_sds(s) for s in _S5_PARAM_SHAPES], compiler_params=_params(48),
    )(*params, *cotangents)


def _scan_geometry(n_seq, seq):
    slab = n_seq * SCAN_CHUNKS
    steps = seq // SCAN_CHUNKS
    tile_rows = slab * SCAN_TILE_STEPS
    n_tiles = steps // SCAN_TILE_STEPS
    return slab, steps, tile_rows, n_tiles


def _per_lane_block(body, name, a):
    spec = pl.BlockSpec((a.shape[0], LANES), lambda j: (0, j))
    return pl.pallas_call(body, name=name, grid=(a.shape[1] // LANES,), in_specs=[spec], out_specs=spec,
                          out_shape=_sds(a.shape), compiler_params=_params(32, ("arbitrary",)))(a)


def _to_scan_order(a, n_seq, seq):
    slab, steps, _, _ = _scan_geometry(n_seq, seq)

    def body(a_ref, o_ref):
        def step(t, carry):
            o_ref[pl.ds(pl.multiple_of(t * slab, slab), slab), :] = a_ref[pl.ds(t, slab, stride=steps), :]
            return carry

        lax.fori_loop(0, steps, step, 0)

    return _per_lane_block(body, "to_scan_order", a)


def _from_scan_order(a, n_seq, seq):
    slab, steps, _, _ = _scan_geometry(n_seq, seq)

    def body(a_ref, o_ref):
        def step(t, carry):
            o_ref[pl.ds(t, slab, stride=steps), :] = a_ref[pl.ds(pl.multiple_of(t * slab, slab), slab), :]
            return carry

        lax.fori_loop(0, steps, step, 0)

    return _per_lane_block(body, "from_scan_order", a)


def _complex_power(re, im, n):
    out = None
    while n:
        if n & 1:
            out = (re, im) if out is None else (out[0] * re - out[1] * im, out[0] * im + out[1] * re)
        n >>= 1
        if n:
            re, im = re * re - im * im, 2.0 * re * im
    return out


def _chunk_carry(sum_re, sum_im, carry_re, carry_im, a_re, a_im, n_seq, reverse):
    carry_re[...] = jnp.zeros_like(carry_re)
    carry_im[...] = jnp.zeros_like(carry_im)
    for s in range(n_seq):
        order = range(SCAN_CHUNKS - 2, -1, -1) if reverse else range(1, SCAN_CHUNKS)
        for c in order:
            r = s * SCAN_CHUNKS + c
            p = r + 1 if reverse else r - 1
            p_re, p_im = carry_re[p:p + 1, :], carry_im[p:p + 1, :]
            carry_re[r:r + 1, :] = a_re * p_re - a_im * p_im + sum_re[p:p + 1, :]
            carry_im[r:r + 1, :] = a_re * p_im + a_im * p_re + sum_im[p:p + 1, :]


def _s5_scan_fwd(u_scan, bt_re, bt_im, cm_re, cm_im, lbar_re, lbar_im, d_row, n_seq, seq):
    slab, steps, tile_rows, n_tiles = _scan_geometry(n_seq, seq)
    rows = u_scan.shape[0]

    def body(u_ref, bre_ref, bim_ref, cre_ref, cim_ref, lre_ref, lim_ref, d_ref, y_ref, hre_ref, him_ref,
             st_re, st_im, h0_re, h0_im, buf_re, buf_im):
        second = pl.program_id(0) == 1
        i = pl.program_id(1)

        @pl.when(jnp.logical_and(i == 0, jnp.logical_not(second)))
        def _():
            st_re[...] = jnp.zeros_like(st_re)
            st_im[...] = jnp.zeros_like(st_im)

        u = u_ref[...]
        ub = u.astype(BF16)
        for hf in range(2):
            cols = slice(hf * 1024, (hf + 1) * 1024)
            buf_re[:, cols] = _dot_nt(ub[:, hf * 256:(hf + 1) * 256], bre_ref[hf])
            buf_im[:, cols] = _dot_nt(ub[:, hf * 256:(hf + 1) * 256], bim_ref[hf])

        for lc in range(SSM_LANES // SCAN_LANE_CHUNK):
            cols = slice(lc * SCAN_LANE_CHUNK, (lc + 1) * SCAN_LANE_CHUNK)
            l_re = jnp.broadcast_to(lre_ref[:, cols], (slab, SCAN_LANE_CHUNK))
            l_im = jnp.broadcast_to(lim_ref[:, cols], (slab, SCAN_LANE_CHUNK))

            def step(t, carry):
                s_re, s_im = carry
                r0 = pl.multiple_of(t * slab, slab)
                n_re = l_re * s_re - l_im * s_im + buf_re[pl.ds(r0, slab), cols]
                n_im = l_re * s_im + l_im * s_re + buf_im[pl.ds(r0, slab), cols]
                buf_re[pl.ds(r0, slab), cols] = n_re
                buf_im[pl.ds(r0, slab), cols] = n_im
                return n_re, n_im

            s_re, s_im = lax.fori_loop(0, SCAN_TILE_STEPS, step, (st_re[:, cols], st_im[:, cols]), unroll=True)
            st_re[:, cols] = s_re
            st_im[:, cols] = s_im

        @pl.when(jnp.logical_and(i == n_tiles - 1, jnp.logical_not(second)))
        def _():
            a_re, a_im = _complex_power(lre_ref[...], lim_ref[...], steps)
            _chunk_carry(st_re, st_im, h0_re, h0_im, a_re, a_im, n_seq, reverse=False)
            st_re[...] = h0_re[...]
            st_im[...] = h0_im[...]

        @pl.when(second)
        def _():
            h_re = buf_re[...].astype(BF16)
            h_im = buf_im[...].astype(BF16)
            hre_ref[...] = h_re
            him_ref[...] = h_im
            for hf in range(2):
                cols = slice(hf * 1024, (hf + 1) * 1024)
                ycols = slice(hf * 256, (hf + 1) * 256)
                y_ref[:, ycols] = (_dot_nt(h_re[:, cols], cre_ref[hf]) - _dot_nt(h_im[:, cols], cim_ref[hf])
                                   + d_ref[:, ycols] * u[:, ycols])

    tile = lambda w: pl.BlockSpec((tile_rows, w), lambda p, i: (i, 0))
    out_tile = lambda w: pl.BlockSpec((tile_rows, w), lambda p, i: (i * p, 0))
    bt, cm = _full(_BT_SHAPE), _full(_CM_SHAPE)
    return pl.pallas_call(
        body, name="s5_scan_fwd", grid=(2, n_tiles),
        in_specs=[tile(512), bt, bt, cm, cm, _full((1, SSM_LANES)), _full((1, SSM_LANES)), _full((1, 512))],
        out_specs=[out_tile(512), out_tile(SSM_LANES), out_tile(SSM_LANES)],
        out_shape=[_sds((rows, 512)), _sds((rows, SSM_LANES), BF16), _sds((rows, SSM_LANES), BF16)],
        scratch_shapes=[pltpu.VMEM((slab, SSM_LANES), F32)] * 4 + [pltpu.VMEM((tile_rows, SSM_LANES), F32)] * 2,
        compiler_params=_params(40, ("arbitrary", "arbitrary")),
    )(u_scan, bt_re, bt_im, cm_re, cm_im, lbar_re, lbar_im, d_row)


def _s5_scan_bwd(dy_scan, u_scan, h_re, h_im, bt_re, bt_im, cm_re, cm_im, lbar_re, lbar_im, d_row, n_seq, seq):
    slab, steps, tile_rows, n_tiles = _scan_geometry(n_seq, seq)
    rows = u_scan.shape[0]

    def body(dy_ref, u_ref, hre_ref, him_ref, bre_ref, bim_ref, cre_ref, cim_ref, lre_ref, lim_ref, d_ref,
             du_ref, dbre_ref, dbim_ref, dcre_ref, dcim_ref, dlre_ref, dlim_ref, dd_ref,
             st_re, st_im, g0_re, g0_im, acc_re, acc_im, buf_re, buf_im):
        second = pl.program_id(0) == 1
        i = pl.program_id(1)

        @pl.when(jnp.logical_and(i == 0, jnp.logical_not(second)))
        def _():
            st_re[...] = jnp.zeros_like(st_re)
            st_im[...] = jnp.zeros_like(st_im)
            acc_re[...] = jnp.zeros_like(acc_re)
            acc_im[...] = jnp.zeros_like(acc_im)
            for ref in (dbre_ref, dbim_ref, dcre_ref, dcim_ref, dd_ref):
                ref[...] = jnp.zeros_like(ref)

        dy = dy_ref[...]
        dyb = dy.astype(BF16)
        for hf in range(2):
            cols = slice(hf * 1024, (hf + 1) * 1024)
            buf_re[:, cols] = _dot(dyb[:, hf * 256:(hf + 1) * 256], cre_ref[hf])
            buf_im[:, cols] = -_dot(dyb[:, hf * 256:(hf + 1) * 256], cim_ref[hf])

        for lc in range(SSM_LANES // SCAN_LANE_CHUNK):
            cols = slice(lc * SCAN_LANE_CHUNK, (lc + 1) * SCAN_LANE_CHUNK)
            l_re = jnp.broadcast_to(lre_ref[:, cols], (slab, SCAN_LANE_CHUNK))
            l_im = jnp.broadcast_to(lim_ref[:, cols], (slab, SCAN_LANE_CHUNK))

            def step(k, carry):
                s_re, s_im, a_re, a_im = carry
                r0 = pl.multiple_of((SCAN_TILE_STEPS - 1 - k) * slab, slab)
                hr = hre_ref[pl.ds(r0, slab), cols].astype(F32)
                hi = him_ref[pl.ds(r0, slab), cols].astype(F32)
                a_re = a_re + s_re * hr + s_im * hi
                a_im = a_im + s_im * hr - s_re * hi
                n_re = l_re * s_re + l_im * s_im + buf_re[pl.ds(r0, slab), cols]
                n_im = l_re * s_im - l_im * s_re + buf_im[pl.ds(r0, slab), cols]
                buf_re[pl.ds(r0, slab), cols] = n_re
                buf_im[pl.ds(r0, slab), cols] = n_im
                return n_re, n_im, a_re, a_im

            zero = jnp.zeros((slab, SCAN_LANE_CHUNK), F32)
            s_re, s_im, a_re, a_im = lax.fori_loop(
                0, SCAN_TILE_STEPS, step, (st_re[:, cols], st_im[:, cols], zero, zero), unroll=True)
            st_re[:, cols] = s_re
            st_im[:, cols] = s_im

            @pl.when(second)
            def _():
                acc_re[:, cols] += a_re
                acc_im[:, cols] += a_im

        @pl.when(jnp.logical_and(i == n_tiles - 1, jnp.logical_not(second)))
        def _():
            p_re, p_im = _complex_power(lre_ref[...], lim_ref[...], steps)
            _chunk_carry(st_re, st_im, g0_re, g0_im, p_re, -p_im, n_seq, reverse=True)
            st_re[...] = g0_re[...]
            st_im[...] = g0_im[...]

        @pl.when(second)
        def _():
            u = u_ref[...]
            ub = u.astype(BF16)
            g_re = buf_re[...].astype(BF16)
            g_im = buf_im[...].astype(BF16)
            dd_ref[...] += jnp.sum(dy * u, axis=0, keepdims=True)
            for hf in range(2):
                cols = slice(hf * 1024, (hf + 1) * 1024)
                ycols = slice(hf * 256, (hf + 1) * 256)
                du_ref[:, ycols] = (_dot(g_re[:, cols], bre_ref[hf]) + _dot(g_im[:, cols], bim_ref[hf])
                                    + d_ref[:, ycols] * dy[:, ycols])
                dbre_ref[hf] += _dot_tn(g_re[:, cols], ub[:, ycols])
                dbim_ref[hf] += _dot_tn(g_im[:, cols], ub[:, ycols])
                dcre_ref[hf] += _dot_tn(dyb[:, ycols], hre_ref[:, cols])
                dcim_ref[hf] -= _dot_tn(dyb[:, ycols], him_ref[:, cols])

        @pl.when(jnp.logical_and(i == n_tiles - 1, second))
        def _():
            dlre_ref[...] = jnp.sum(acc_re[...], axis=0, keepdims=True)
            dlim_ref[...] = jnp.sum(acc_im[...], axis=0, keepdims=True)

    tile = lambda w: pl.BlockSpec((tile_rows, w), lambda p, i: (n_tiles - 1 - i, 0))
    second_tile = lambda w: pl.BlockSpec((tile_rows, w), lambda p, i: (n_tiles - 1 - i * p, 0))
    bt, cm = _full(_BT_SHAPE), _full(_CM_SHAPE)
    row = _full((1, SSM_LANES))
    return pl.pallas_call(
        body, name="s5_scan_bwd", grid=(2, n_tiles),
        in_specs=[tile(512), second_tile(512), second_tile(SSM_LANES), second_tile(SSM_LANES),
                  bt, bt, cm, cm, row, row, _full((1, 512))],
        out_specs=[second_tile(512), bt, bt, cm, cm, row, row, _full((1, 512))],
        out_shape=([_sds((rows, 512))] + [_sds(_BT_SHAPE)] * 2 + [_sds(_CM_SHAPE)] * 2 + [_sds((1, SSM_LANES))] * 2
                   + [_sds((1, 512))]),
        scratch_shapes=[pltpu.VMEM((slab, SSM_LANES), F32)] * 6 + [pltpu.VMEM((tile_rows, SSM_LANES), F32)] * 2,
        compiler_params=_params(48, ("arbitrary", "arbitrary")),
    )(dy_scan, u_scan, h_re, h_im, bt_re, bt_im, cm_re, cm_im, lbar_re, lbar_im, d_row)


def _glu_gate(gl, a, zs):
    return gl * jax.nn.sigmoid(a) * _silu(zs)


def _glu_fwd(y, zs, w_glu, b_glu):
    rows = y.shape[0]
    tm = 512

    def body(y_ref, zs_ref, w_ref, b_ref, o_ref):
        gl = jax.nn.gelu(y_ref[...])
        a = _dot(gl.astype(BF16), w_ref[...]) + b_ref[...]
        o_ref[...] = _glu_gate(gl, a, zs_ref[...]).astype(BF16)

    return pl.pallas_call(
        body, name="glu_fwd", grid=(rows // tm,),
        in_specs=[_rows(tm, 512), _rows(tm, 512), _full((512, 512)), _full((1, 512))],
        out_specs=_rows(tm, 512), out_shape=_sds((rows, 512), BF16),
        compiler_params=_params(32, ("arbitrary",)),
    )(y, zs, w_glu, b_glu)


def _glu_bwd(y, zs, d_out, w_glu, b_glu):
    rows = y.shape[0]
    tm = 512

    def body(y_ref, zs_ref, d_ref, w_ref, b_ref, dy_ref, dzs_ref, dw_ref, db_ref):
        @pl.when(pl.program_id(0) == 0)
        def _():
            dw_ref[...] = jnp.zeros_like(dw_ref)
            db_ref[...] = jnp.zeros_like(db_ref)

        gl, gelu_vjp = jax.vjp(jax.nn.gelu, y_ref[...])
        glb = gl.astype(BF16)
        a = _dot(glb, w_ref[...]) + b_ref[...]
        _, gate_vjp = jax.vjp(_glu_gate, gl, a, zs_ref[...])
        d_gl, d_a, d_zs = gate_vjp(d_ref[...])
        dab = d_a.astype(BF16)
        d_gl = d_gl + _dot_nt(dab, w_ref[...])
        dy_ref[...] = gelu_vjp(d_gl)[0]
        dzs_ref[...] = d_zs.astype(BF16)
        dw_ref[...] += _dot_tn(glb, dab)
        db_ref[...] += jnp.sum(d_a, axis=0, keepdims=True)

    return pl.pallas_call(
        body, name="glu_bwd", grid=(rows // tm,),
        in_specs=[_rows(tm, 512), _rows(tm, 512), _rows(tm, 512), _full((512, 512)), _full((1, 512))],
        out_specs=[_rows(tm, 512), _rows(tm, 512), _full((512, 512)), _full((1, 512))],
        out_shape=[_sds((rows, 512)), _sds((rows, 512), BF16), _sds((512, 512)), _sds((1, 512))],
        compiler_params=_params(32, ("arbitrary",)),
    )(y, zs, d_out, w_glu, b_glu)


def _attn_masks(first_block):
    qi = lax.broadcasted_iota(jnp.int32, (BLOCK, BLOCK), 0)
    si = lax.broadcasted_iota(jnp.int32, (BLOCK, BLOCK), 1)
    dist_cur = qi - si
    dist_prev = dist_cur + BLOCK
    valid_cur = dist_cur >= 0
    valid_prev = jnp.logical_and(dist_prev < WINDOW, jnp.logical_not(first_block))
    return dist_cur.astype(F32), dist_prev.astype(F32), valid_cur, valid_prev


def _scores(qh, kh, slope, dist, valid):
    s = _dot_nt(qh, kh) * ATTN_SCALE - slope * dist
    return jnp.where(valid, s, -jnp.inf)


def _attn_fwd(q, k, v, za, sinks, n_seq, seq):
    nb = seq // BLOCK
    rows = q.shape[0]

    def body(q_ref, kc_ref, kp_ref, vc_ref, vp_ref, za_ref, sk_ref, o_ref, ao_ref, lse_ref):
        dist_c, dist_p, valid_c, valid_p = _attn_masks(pl.program_id(1) == 0)
        for h in range(N_HEADS):
            j = h // Q_PER_KV
            hs = slice(h * HEAD_DIM, (h + 1) * HEAD_DIM)
            js = slice(j * HEAD_DIM, (j + 1) * HEAD_DIM)
            slope = 2.0 ** (-(h + 1))
            qh = q_ref[:, hs]
            sc = _scores(qh, kc_ref[:, js], slope, dist_c, valid_c)
            sp = _scores(qh, kp_ref[:, js], slope, dist_p, valid_p)
            sink = sk_ref[0:1, h:h + 1]
            m = jnp.maximum(jnp.maximum(jnp.max(sc, axis=-1, keepdims=True), jnp.max(sp, axis=-1, keepdims=True)), sink)
            ec = jnp.exp(sc - m)
            ep = jnp.exp(sp - m)
            den = jnp.sum(ec, axis=-1, keepdims=True) + jnp.sum(ep, axis=-1, keepdims=True) + jnp.exp(sink - m)
            inv = 1.0 / den
            o_ref[:, hs] = _dot((ec * inv).astype(BF16), vc_ref[:, js]) + _dot((ep * inv).astype(BF16), vp_ref[:, js])
            lse_ref[:, h:h + 1] = m + jnp.log(den)
        ao_ref[...] = (o_ref[...] * _silu(za_ref[...])).astype(BF16)

    cur = lambda w: pl.BlockSpec((BLOCK, w), lambda b, n: (b * nb + n, 0))
    prev = lambda w: pl.BlockSpec((BLOCK, w), lambda b, n: (b * nb + jnp.maximum(n - 1, 0), 0))
    return pl.pallas_call(
        body, name="attn_fwd", grid=(n_seq, nb),
        in_specs=[cur(512), cur(128), prev(128), cur(128), prev(128), cur(512), _full((1, N_HEADS))],
        out_specs=[cur(512), cur(512), cur(N_HEADS)],
        out_shape=[_sds((rows, 512)), _sds((rows, 512), BF16), _sds((rows, N_HEADS))],
        compiler_params=_params(32, ("arbitrary", "arbitrary")),
    )(q, k, k, v, v, za, sinks)


def _attn_bwd(q, k, v, za, o, lse, d_ao, sinks, n_seq, seq):
    nb = seq // BLOCK
    rows = q.shape[0]

    def body(q_ref, q2_ref, kc_ref, kp_ref, vc_ref, vp_ref, za_ref, za2_ref, o_ref, o2_ref, lse_ref, lse2_ref,
             d_ref, d2_ref, sk_ref, dq_ref, dk_ref, dv_ref, dza_ref, dsk_ref):
        n = pl.program_id(1)

        @pl.when(jnp.logical_and(pl.program_id(0) == 0, n == 0))
        def _():
            dsk_ref[...] = jnp.zeros_like(dsk_ref)

        dist_c, dist_p, valid_c, valid_p = _attn_masks(n == 0)
        valid_next = jnp.logical_and(dist_p < WINDOW, n + 1 < nb)

        o = o_ref[...]
        _, gate_vjp = jax.vjp(lambda o_, z_: o_ * _silu(z_), o, za_ref[...])
        d_o, d_za = gate_vjp(d_ref[...])
        dza_ref[...] = d_za.astype(BF16)
        o2 = o2_ref[...]
        d_o2 = d2_ref[...] * _silu(za2_ref[...])

        for j in range(KV_HEADS):
            js = slice(j * HEAD_DIM, (j + 1) * HEAD_DIM)
            kc, kp, vc, vp = kc_ref[:, js], kp_ref[:, js], vc_ref[:, js], vp_ref[:, js]
            dk = jnp.zeros((BLOCK, HEAD_DIM), F32)
            dv = jnp.zeros((BLOCK, HEAD_DIM), F32)
            for g in range(Q_PER_KV):
                h = j * Q_PER_KV + g
                hs = slice(h * HEAD_DIM, (h + 1) * HEAD_DIM)
                slope = 2.0 ** (-(h + 1))
                qh = q_ref[:, hs]
                lse_h = lse_ref[:, h:h + 1]
                pc = jnp.exp(_scores(qh, kc, slope, dist_c, valid_c) - lse_h)
                pp = jnp.exp(_scores(qh, kp, slope, dist_p, valid_p) - lse_h)
                do_h = d_o[:, hs]
                dob = do_h.astype(BF16)
                delta = jnp.sum(do_h * o[:, hs], axis=-1, keepdims=True)
                dsc = (pc * (_dot_nt(dob, vc) - delta)).astype(BF16)
                dsp = (pp * (_dot_nt(dob, vp) - delta)).astype(BF16)
                dq_ref[:, hs] = ((_dot(dsc, kc) + _dot(dsp, kp)) * ATTN_SCALE).astype(BF16)
                p_sink = jnp.exp(sk_ref[0:1, h:h + 1] - lse_h)
                dsk_ref[0:1, h:h + 1] -= jnp.sum(p_sink * delta, axis=0, keepdims=True)
                dk = dk + _dot_tn(dsc, qh)
                dv = dv + _dot_tn(pc.astype(BF16), dob)
                q2h = q2_ref[:, hs]
                p2 = jnp.exp(_scores(q2h, kc, slope, dist_p, valid_next) - lse2_ref[:, h:h + 1])
                do2_h = d_o2[:, hs]
                do2b = do2_h.astype(BF16)
                delta2 = jnp.sum(do2_h * o2[:, hs], axis=-1, keepdims=True)
                ds2 = (p2 * (_dot_nt(do2b, vc) - delta2)).astype(BF16)
                dk = dk + _dot_tn(ds2, q2h)
                dv = dv + _dot_tn(p2.astype(BF16), do2b)
            dk_ref[:, js] = (dk * ATTN_SCALE).astype(BF16)
            dv_ref[:, js] = dv.astype(BF16)

    cur = lambda w: pl.BlockSpec((BLOCK, w), lambda b, n: (b * nb + n, 0))
    prev = lambda w: pl.BlockSpec((BLOCK, w), lambda b, n: (b * nb + jnp.maximum(n - 1, 0), 0))
    nxt = lambda w: pl.BlockSpec((BLOCK, w), lambda b, n: (b * nb + jnp.minimum(n + 1, nb - 1), 0))
    return pl.pallas_call(
        body, name="attn_bwd", grid=(n_seq, nb),
        in_specs=[cur(512), nxt(512), cur(128), prev(128), cur(128), prev(128), cur(512), nxt(512),
                  cur(512), nxt(512), cur(N_HEADS), nxt(N_HEADS), cur(512), nxt(512), _full((1, N_HEADS))],
        out_specs=[cur(512), cur(128), cur(128), cur(512), _full((1, N_HEADS))],
        out_shape=[_sds((rows, 512), BF16), _sds((rows, 128), BF16), _sds((rows, 128), BF16),
                   _sds((rows, 512), BF16), _sds((1, N_HEADS))],
        compiler_params=_params(32, ("arbitrary", "arbitrary")),
    )(q, q, k, k, v, v, za, za, o, o, lse, lse, d_ao, d_ao, sinks)


def _tail(ssm_out, attn_out, x2d, p2d, target, w_out, g2, w_gate, b_gate, w_proj):
    rows = x2d.shape[0]
    tm = 256

    def body(so_ref, ao_ref, x_ref, p_ref, t_ref, wo_ref, g2_ref, wg_ref, bg_ref, wp_ref,
             dh1_ref, dso_ref, dao_ref, dwo_ref, dwg_ref, dwp_ref, dbg_ref, dg2_ref, loss_ref):
        @pl.when(pl.program_id(0) == 0)
        def _():
            for ref in (dwo_ref, dwg_ref, dwp_ref, dbg_ref, dg2_ref, loss_ref):
                ref[...] = jnp.zeros_like(ref)

        so = so_ref[...]
        ao = ao_ref[...]
        g2 = g2_ref[...]
        mixed = _dot(so, wo_ref[0:512, :]) + _dot(ao, wo_ref[512:1024, :])
        r = lax.rsqrt(jnp.mean(mixed * mixed, axis=-1, keepdims=True) + EPS)
        mr = mixed * r
        h1 = x_ref[...] + mr * g2
        h1b = h1.astype(BF16)
        gate = jax.nn.sigmoid(_dot(h1b, wg_ref[...]) + bg_ref[...])
        pb = p_ref[...].astype(BF16)
        wp_blocks = [slice(j * D_PLE, (j + 1) * D_PLE) for j in range(N_CHIPS)]
        pp = jnp.concatenate([_dot(pb, wp_ref[blk, :]) for blk in wp_blocks], axis=1)
        err = h1 + gate * pp - t_ref[...]
        loss_ref[...] += 0.5 * jnp.sum(jnp.mean(err * err, axis=-1, keepdims=True), axis=0, keepdims=True)

        dh2 = err * (1.0 / D_MODEL)
        d_glin = dh2 * pp * gate * (1.0 - gate)
        d_glin_b = d_glin.astype(BF16)
        dwg_ref[...] += _dot_tn(h1b, d_glin_b)
        dbg_ref[...] += jnp.sum(d_glin, axis=0, keepdims=True)
        d_pp = (dh2 * gate).astype(BF16)
        for blk in wp_blocks:
            dwp_ref[blk, :] += _dot_tn(pb, d_pp[:, blk])
        dh1 = dh2 + _dot_nt(d_glin_b, wg_ref[...])
        dh1_ref[...] = dh1
        dg2_ref[...] += jnp.sum(dh1 * mr, axis=0, keepdims=True)
        a_ = dh1 * g2
        d_mixed = (r * a_ - mr * (r * jnp.mean(a_ * mr, axis=-1, keepdims=True))).astype(BF16)
        dwo_ref[0:512, :] += _dot_tn(so, d_mixed)
        dwo_ref[512:1024, :] += _dot_tn(ao, d_mixed)
        dso_ref[...] = _dot_nt(d_mixed, wo_ref[0:512, :])
        dao_ref[...] = _dot_nt(d_mixed, wo_ref[512:1024, :])

    return pl.pallas_call(
        body, name="tail_fwd_bwd", grid=(rows // tm,),
        in_specs=[_rows(tm, 512), _rows(tm, 512), _rows(tm, D_MODEL), _rows(tm, D_PLE), _rows(tm, D_MODEL),
                  _full((D_MODEL, D_MODEL)), _full((1, D_MODEL)), _full((D_MODEL, D_MODEL)), _full((1, D_MODEL)),
                  _full((N_CHIPS * D_PLE, D_PLE))],
        out_specs=[_rows(tm, D_MODEL), _rows(tm, 512), _rows(tm, 512), _full((D_MODEL, D_MODEL)),
                   _full((D_MODEL, D_MODEL)), _full((N_CHIPS * D_PLE, D_PLE)), _full((1, D_MODEL)), _full((1, D_MODEL)),
                   _full((1, 1))],
        out_shape=[_sds((rows, D_MODEL)), _sds((rows, 512)), _sds((rows, 512)), _sds((D_MODEL, D_MODEL)),
                   _sds((D_MODEL, D_MODEL)), _sds((N_CHIPS * D_PLE, D_PLE)), _sds((1, D_MODEL)), _sds((1, D_MODEL)),
                   _sds((1, 1))],
        compiler_params=_params(52, ("arbitrary",)),
    )(ssm_out, attn_out, x2d, p2d, target, w_out, g2, w_gate, b_gate, w_proj)


def _local_step(x, p, target, pre_norm_g, w_in_t, s5_params, ssm_d, w_glu, b_glu, sinks, w_out, post_norm_g, w_proj,
                w_gate, b_gate):
    n_seq, seq, _ = x.shape
    rows = n_seq * seq
    x2d = x.reshape(rows, D_MODEL)
    p2d = p.reshape(rows, D_PLE)
    t2d = target.reshape(rows, D_MODEL)

    l_re, l_im, bt_re, bt_im, cm_re, cm_im = _s5_params_fwd(*s5_params)

    u, zs, q, k, v, za = _in_proj(x2d, pre_norm_g, w_in_t)
    u_scan = _to_scan_order(u, n_seq, seq)
    y_scan, h_re, h_im = _s5_scan_fwd(u_scan, bt_re, bt_im, cm_re, cm_im, l_re, l_im, ssm_d, n_seq, seq)
    y = _from_scan_order(y_scan, n_seq, seq)
    ssm_out = _glu_fwd(y, zs, w_glu, b_glu)
    o, attn_out, lse = _attn_fwd(q, k, v, za, sinks, n_seq, seq)

    dh1, d_so, d_ao, d_w_out, d_w_gate, d_w_proj, d_b_gate, d_g2, loss = _tail(
        ssm_out, attn_out, x2d, p2d, t2d, w_out, post_norm_g, w_gate, b_gate, w_proj)

    dq, dk, dv, dza, d_sinks = _attn_bwd(q, k, v, za, o, lse, d_ao, sinks, n_seq, seq)
    dy, dzs, d_w_glu, d_b_glu = _glu_bwd(y, zs, d_so, w_glu, b_glu)
    dy_scan = _to_scan_order(dy, n_seq, seq)
    du_scan, d_bt_re, d_bt_im, d_cm_re, d_cm_im, d_l_re, d_l_im, d_d = _s5_scan_bwd(
        dy_scan, u_scan, h_re, h_im, bt_re, bt_im, cm_re, cm_im, l_re, l_im, ssm_d, n_seq, seq)
    du = _from_scan_order(du_scan, n_seq, seq)
    d_lam_re, d_lam_im, d_log_step, d_b_re, d_b_im, d_c_re, d_c_im = _s5_params_bwd(
        s5_params, (d_l_re, d_l_im, d_bt_re, d_bt_im, d_cm_re, d_cm_im))

    grad_x, d_w_in_t, d_g1 = _in_proj_bwd(x2d, dh1, pre_norm_g, w_in_t, du, dzs, dq, dk, dv, dza)
    grads = dict(
        pre_norm_g=d_g1, w_in=d_w_in_t, ssm_lam_re=d_lam_re, ssm_lam_im=d_lam_im, ssm_log_step=d_log_step,
        ssm_b_re=d_b_re, ssm_b_im=d_b_im, ssm_c_re=d_c_re, ssm_c_im=d_c_im, ssm_d=d_d, ssm_w_glu=d_w_glu,
        ssm_b_glu=d_b_glu, attn_sinks=d_sinks, w_out=d_w_out, post_norm_g=d_g2, pl_w_proj=d_w_proj,
        pl_w_gate=d_w_gate, pl_b_gate=d_b_gate)
    return grad_x.reshape(x.shape), loss, grads


_BIG = ("w_in", "ssm_w_glu", "w_out", "pl_w_proj", "pl_w_gate")
_BIG_SHARD = {"w_in": (D_IN // N_CHIPS, D_MODEL), "ssm_w_glu": (D_SSM // N_CHIPS, D_SSM),
              "w_out": (D_MODEL // N_CHIPS, D_MODEL), "pl_w_proj": (D_PLE, D_MODEL // N_CHIPS),
              "pl_w_gate": (D_MODEL // N_CHIPS, D_MODEL)}
_SMALL = {"pre_norm_g": (1, D_MODEL), "ssm_lam_re": (SSM_GROUPS, SSM_STATE), "ssm_lam_im": (SSM_GROUPS, SSM_STATE),
          "ssm_log_step": (1, SSM_GROUPS), "ssm_b_re": (SSM_LANES, SSM_GROUP_CH), "ssm_b_im": (SSM_LANES, SSM_GROUP_CH),
          "ssm_c_re": (D_SSM, SSM_STATE), "ssm_c_im": (D_SSM, SSM_STATE), "ssm_d": (1, D_SSM), "ssm_b_glu": (1, D_SSM),
          "attn_sinks": (1, N_HEADS), "post_norm_g": (1, D_MODEL), "pl_b_gate": (1, D_MODEL)}
_VEC_ROWS = ("pre_norm_g", "post_norm_g", "pl_b_gate", "ssm_d", "ssm_b_glu", "attn_sinks", "ssm_log_step", "loss")
_SMALL_GROUPS = (
    ("vec", (8, D_MODEL), tuple((name, r) for r, name in enumerate(_VEC_ROWS))),
    ("lam", (2 * SSM_GROUPS, SSM_STATE), (("ssm_lam_re", 0), ("ssm_lam_im", SSM_GROUPS))),
    ("b", (2 * SSM_LANES, SSM_GROUP_CH), (("ssm_b_re", 0), ("ssm_b_im", SSM_LANES))),
    ("c", (2 * D_SSM, SSM_STATE), (("ssm_c_re", 0), ("ssm_c_im", D_SSM))),
)
_SMALL_ORDER = tuple(name for _, _, members in _SMALL_GROUPS for name, _ in members)
_WEIGHT_ORDER = ("pre_norm_g", "w_in", "ssm_lam_re", "ssm_lam_im", "ssm_log_step", "ssm_b_re", "ssm_b_im", "ssm_c_re",
                 "ssm_c_im", "ssm_d", "ssm_w_glu", "ssm_b_glu", "attn_sinks", "w_out", "post_norm_g", "pl_w_proj",
                 "pl_w_gate", "pl_b_gate")


def _small_shape(name):
    return (1, 1) if name == "loss" else _SMALL[name]


def _mesh_place():
    x, y, c = lax.axis_index("x"), lax.axis_index("y"), lax.axis_index("c")
    other_chips = ((1 - x, y), (x, 1 - y), (1 - x, 1 - y))
    return x, y, c, other_chips


def _gather_weights(shards):
    n_t = len(shards)

    def body(*refs):
        s_refs, g_refs = refs[:n_t], refs[n_t:2 * n_t]
        send_sems, recv_sems, local_sems = refs[2 * n_t:]
        x, y, c, other_chips = _mesh_place()
        started = []
        for i, (s_ref, g_ref) in enumerate(zip(s_refs, g_refs)):
            rows = s_ref.shape[0]
            half = rows // 2

            def block(chip, g_ref=g_ref, rows=rows, half=half):
                return g_ref.at[pl.ds((2 * chip[0] + chip[1]) * rows + c * half, half), :]

            def copy(k, chip, to, src=None, i=i, block=block):
                return pltpu.make_async_remote_copy(
                    src_ref=block(chip) if src is None else src, dst_ref=block(chip), send_sem=send_sems.at[6 * i + k],
                    recv_sem=recv_sems.at[6 * i + k], device_id=to, device_id_type=MESH)

            own = pltpu.make_async_copy(s_ref, g_ref.at[pl.ds((2 * x + y) * rows, rows), :], local_sems.at[i])
            own.start()
            first = [copy(k, (x, y), (*chip, c), src=s_ref.at[pl.ds(c * half, half), :])
                     for k, chip in enumerate(other_chips)]
            for cp in first:
                cp.start()
            passed = [copy(3 + k, chip, (x, y, 1 - c)) for k, chip in enumerate(other_chips)]
            started.append((own, first, passed))
        for own, first, passed in started:
            for k in range(3):
                first[k].wait_recv()
                passed[k].start()
        for own, first, passed in started:
            for k in range(3):
                passed[k].wait_recv()
            for cp in first + passed:
                cp.wait_send()
            own.wait()

    any_spec = pl.BlockSpec(memory_space=pl.ANY)
    return pl.pallas_call(
        body, name="gather_weights", in_specs=[any_spec] * n_t, out_specs=[any_spec] * n_t,
        out_shape=[_sds((N_CHIPS * s.shape[0], s.shape[1]), s.dtype) for s in shards],
        scratch_shapes=[pltpu.SemaphoreType.DMA((6 * n_t,)), pltpu.SemaphoreType.DMA((6 * n_t,)),
                        pltpu.SemaphoreType.DMA((n_t,))],
    )(*shards)


def _exchange_grads(big, small):
    n_t = len(big)
    n_g = len(_SMALL_GROUPS)
    names = _SMALL_ORDER
    halves = [(b.shape[0] // N_CHIPS // 2, b.shape[1]) for b in big]
    n_sems = 4 * n_g + 8 * n_t

    def body(*refs):
        pos = 0

        def take(n):
            nonlocal pos
            pos += n
            return refs[pos - n:pos]

        big_refs, small_refs = take(n_t), dict(zip(names, take(len(names))))
        out_refs, small_out_refs, land_refs = take(n_t), dict(zip(names, take(len(names)))), take(n_t)
        ga, gb, pme, send_b, recv_b = take(n_t), take(n_t), take(n_t), take(n_t), take(n_t)
        s_own, s_sib, s_chips, s_pair = take(n_g), take(n_g), take(n_g), take(n_g)
        send_sems, recv_sems, local_sems = take(3)
        x, y, c, other_chips = _mesh_place()
        me = 2 * x + y
        sibling = (x, y, 1 - c)
        sem_at = iter(range(n_sems))

        def remote(src, dst, to):
            k = next(sem_at)
            return pltpu.make_async_remote_copy(src_ref=src, dst_ref=dst, send_sem=send_sems.at[k],
                                                recv_sem=recv_sems.at[k], device_id=to, device_id_type=MESH)

        small_swaps = []
        for gi, (_, _, members) in enumerate(_SMALL_GROUPS):
            s_own[gi][...] = jnp.zeros_like(s_own[gi])
            for name, r0 in members:
                r, n = _small_shape(name)
                s_own[gi][r0:r0 + r, 0:n] = small_refs[name][...]
            small_swaps.append(remote(s_own[gi], s_sib[gi], sibling))
            small_swaps[gi].start()
        big_swaps = []
        for i in range(n_t):
            hr = halves[i][0]
            big_swaps.append([])
            for j in range(N_CHIPS):
                src = big_refs[i].at[pl.ds(j * 2 * hr + (1 - c) * hr, hr), :]
                big_swaps[i].append(remote(src, land_refs[i].at[j], sibling))
                big_swaps[i][j].start()
        small_sends = []
        for gi in range(n_g):
            small_swaps[gi].wait_recv()
            s_pair[gi][...] = s_own[gi][...] + s_sib[gi][...]
            small_sends.append([remote(s_pair[gi], s_chips[gi].at[k], (*chip, c)) for k, chip in enumerate(other_chips)])
            for cp in small_sends[gi]:
                cp.start()

        def pair_sum(i, j, dst):
            hr = halves[i][0]
            a = pltpu.make_async_copy(big_refs[i].at[pl.ds(j * 2 * hr + c * hr, hr), :], ga[i], local_sems.at[0])
            b = pltpu.make_async_copy(land_refs[i].at[j], gb[i], local_sems.at[1])
            a.start()
            b.start()
            a.wait()
            b.wait()
            dst[...] = (ga[i][...] + gb[i][...]).astype(dst.dtype)

        big_sends = []
        for i in range(n_t):
            for j in range(N_CHIPS):
                big_swaps[i][j].wait_recv()
            big_sends.append([])
            for k, chip in enumerate(other_chips):
                pair_sum(i, 2 * chip[0] + chip[1], send_b[i].at[k])
                big_sends[i].append(remote(send_b[i].at[k], recv_b[i].at[k], (*chip, c)))
                big_sends[i][k].start()
        last_swaps, keeps = [], []
        for i in range(n_t):
            hr = halves[i][0]
            pair_sum(i, me, pme[i])
            for k in range(3):
                big_sends[i][k].wait_recv()
            pme[i][...] = ((pme[i][...] + recv_b[i][0].astype(F32)) + recv_b[i][1].astype(F32)) + recv_b[i][2].astype(F32)
            mine = out_refs[i].at[pl.ds(c * hr, hr), :]
            keeps.append(pltpu.make_async_copy(pme[i], mine, local_sems.at[2 + i]))
            keeps[i].start()
            last_swaps.append(remote(pme[i], mine, sibling))
            last_swaps[i].start()

        for gi, (_, _, members) in enumerate(_SMALL_GROUPS):
            for k in range(3):
                small_sends[gi][k].wait_recv()
            total = None
            for j in range(N_CHIPS):
                rel = jnp.bitwise_xor(j, me)
                term = jnp.where(rel == 0, s_pair[gi][...], jnp.where(
                    rel == 2, s_chips[gi][0], jnp.where(rel == 1, s_chips[gi][1], s_chips[gi][2])))
                total = term if total is None else total + term
            s_sib[gi][...] = total
            for name, r0 in members:
                r, n = _small_shape(name)
                small_out_refs[name][...] = s_sib[gi][r0:r0 + r, 0:n]

        for i in range(n_t):
            last_swaps[i].wait_recv()
            keeps[i].wait()
        for cp in (small_swaps + [cp for group in big_swaps + small_sends + big_sends for cp in group] + last_swaps):
            cp.wait_send()

    any_spec = pl.BlockSpec(memory_space=pl.ANY)
    vmem_spec = pl.BlockSpec(memory_space=pltpu.VMEM)
    small_shapes = [_sds(_small_shape(n)) for n in names]
    group_shapes = [shape for _, shape, _ in _SMALL_GROUPS]
    per_matrix = lambda dtype, lead=(): [pltpu.VMEM(lead + h, dtype) for h in halves]
    outs = pl.pallas_call(
        body, name="exchange_grads",
        in_specs=[any_spec] * n_t + [vmem_spec] * len(names),
        out_specs=[any_spec] * n_t + [vmem_spec] * len(names) + [any_spec] * n_t,
        out_shape=([_sds((b.shape[0] // N_CHIPS, b.shape[1])) for b in big] + small_shapes
                   + [_sds((N_CHIPS,) + h) for h in halves]),
        scratch_shapes=(per_matrix(F32) + per_matrix(F32) + per_matrix(F32) + per_matrix(BF16, (3,)) + per_matrix(BF16, (3,))
                        + [pltpu.VMEM(s, F32) for s in group_shapes] * 2 + [pltpu.VMEM((3,) + s, F32) for s in group_shapes]
                        + [pltpu.VMEM(s, F32) for s in group_shapes]
                        + [pltpu.SemaphoreType.DMA((n_sems,)), pltpu.SemaphoreType.DMA((n_sems,)),
                           pltpu.SemaphoreType.DMA((2 + n_t,))]),
        compiler_params=_params(48),
    )(*big, *[small[n] for n in names])
    return list(outs[:n_t]), dict(zip(names, outs[n_t:n_t + len(names)]))


def _adamw_update(w, g, m, v):
    m = ADAM_B1 * m + (1.0 - ADAM_B1) * g
    v = ADAM_B2 * v + (1.0 - ADAM_B2) * (g * g)
    m_hat = m / (1.0 - ADAM_B1 ** ADAM_STEP)
    v_hat = v / (1.0 - ADAM_B2 ** ADAM_STEP)
    return -ADAM_LR * (m_hat / (jnp.sqrt(v_hat) + ADAM_EPS) + ADAM_WD * w), m, v


def _adamw(w, g, m, v, grid, name):
    n_t = len(w)

    def body(*refs):
        ins, outs = refs[:4 * n_t], refs[4 * n_t:]
        for i in range(n_t):
            vals = _adamw_update(*[ins[a * n_t + i][...] for a in range(4)])
            for a in range(3):
                outs[a * n_t + i][...] = vals[a]

    specs = [pl.BlockSpec((a.shape[0] // grid, a.shape[1]), lambda i: (i, 0)) for a in w]
    shapes = [_sds(a.shape) for a in w]
    outs = pl.pallas_call(
        body, name=name, grid=(grid,), in_specs=specs * 4, out_specs=specs * 3, out_shape=shapes * 3,
        compiler_params=_params(40, ("arbitrary",)),
    )(*w, *g, *m, *v)
    return outs[:n_t], outs[n_t:2 * n_t], outs[2 * n_t:]


def kernel(x, p, pre_norm_g, w_in, ssm_lam_re, ssm_lam_im, ssm_log_step, ssm_b_re, ssm_b_im, ssm_c_re, ssm_c_im, ssm_d, ssm_w_glu, ssm_b_glu, attn_sinks, w_out, post_norm_g, pl_w_proj, pl_w_gate, pl_b_gate, loss_target, m_pre_norm_g, m_w_in, m_ssm_lam_re, m_ssm_lam_im, m_ssm_log_step, m_ssm_b_re, m_ssm_b_im, m_ssm_c_re, m_ssm_c_im, m_ssm_d, m_ssm_w_glu, m_ssm_b_glu, m_attn_sinks, m_w_out, m_post_norm_g, m_pl_w_proj, m_pl_w_gate, m_pl_b_gate, v_pre_norm_g, v_w_in, v_ssm_lam_re, v_ssm_lam_im, v_ssm_log_step, v_ssm_b_re, v_ssm_b_im, v_ssm_c_re, v_ssm_c_im, v_ssm_d, v_ssm_w_glu, v_ssm_b_glu, v_attn_sinks, v_w_out, v_post_norm_g, v_pl_w_proj, v_pl_w_gate, v_pl_b_gate):
    weights = dict(pre_norm_g=pre_norm_g, w_in=w_in, ssm_lam_re=ssm_lam_re, ssm_lam_im=ssm_lam_im,
                   ssm_log_step=ssm_log_step, ssm_b_re=ssm_b_re, ssm_b_im=ssm_b_im, ssm_c_re=ssm_c_re,
                   ssm_c_im=ssm_c_im, ssm_d=ssm_d, ssm_w_glu=ssm_w_glu, ssm_b_glu=ssm_b_glu, attn_sinks=attn_sinks,
                   w_out=w_out, post_norm_g=post_norm_g, pl_w_proj=pl_w_proj, pl_w_gate=pl_w_gate, pl_b_gate=pl_b_gate)
    m_in = dict(pre_norm_g=m_pre_norm_g, w_in=m_w_in, ssm_lam_re=m_ssm_lam_re, ssm_lam_im=m_ssm_lam_im,
                ssm_log_step=m_ssm_log_step, ssm_b_re=m_ssm_b_re, ssm_b_im=m_ssm_b_im, ssm_c_re=m_ssm_c_re,
                ssm_c_im=m_ssm_c_im, ssm_d=m_ssm_d, ssm_w_glu=m_ssm_w_glu, ssm_b_glu=m_ssm_b_glu,
                attn_sinks=m_attn_sinks, w_out=m_w_out, post_norm_g=m_post_norm_g, pl_w_proj=m_pl_w_proj,
                pl_w_gate=m_pl_w_gate, pl_b_gate=m_pl_b_gate)
    v_in = dict(pre_norm_g=v_pre_norm_g, w_in=v_w_in, ssm_lam_re=v_ssm_lam_re, ssm_lam_im=v_ssm_lam_im,
                ssm_log_step=v_ssm_log_step, ssm_b_re=v_ssm_b_re, ssm_b_im=v_ssm_b_im, ssm_c_re=v_ssm_c_re,
                ssm_c_im=v_ssm_c_im, ssm_d=v_ssm_d, ssm_w_glu=v_ssm_w_glu, ssm_b_glu=v_ssm_b_glu,
                attn_sinks=v_attn_sinks, w_out=v_w_out, post_norm_g=v_post_norm_g, pl_w_proj=v_pl_w_proj,
                pl_w_gate=v_pl_w_gate, pl_b_gate=v_pl_b_gate)

    def two_d(tree):
        return {k: (a.reshape(_SMALL[k]) if k in _SMALL else a[0]) for k, a in tree.items()}

    w2, m2, v2 = two_d(weights), two_d(m_in), two_d(v_in)

    shards = [w2[n].T if n == "w_in" else w2[n] for n in _BIG]
    full = dict(zip(_BIG, _gather_weights([s.astype(BF16) for s in shards])))
    s5_params = tuple(w2[n] for n in ("ssm_lam_re", "ssm_lam_im", "ssm_log_step", "ssm_b_re", "ssm_b_im", "ssm_c_re",
                                      "ssm_c_im"))
    grad_x, loss, grads = _local_step(
        x, p, loss_target, w2["pre_norm_g"], full["w_in"], s5_params, w2["ssm_d"], full["ssm_w_glu"], w2["ssm_b_glu"],
        w2["attn_sinks"], full["w_out"], w2["post_norm_g"], full["pl_w_proj"], full["pl_w_gate"], w2["pl_b_gate"])

    g_big, g_small = _exchange_grads([grads[n] for n in _BIG], {**{n: grads[n] for n in _SMALL}, "loss": loss})
    g_big = dict(zip(_BIG, g_big))
    g_big["w_in"] = g_big["w_in"].T
    total_loss = g_small.pop("loss")

    big_out = _adamw([w2[n] for n in _BIG], [g_big[n] for n in _BIG], [m2[n] for n in _BIG], [v2[n] for n in _BIG],
                     8, "adamw_matrices")
    small_names = tuple(_SMALL)
    small_out = _adamw([w2[n] for n in small_names], [g_small[n] for n in small_names], [m2[n] for n in small_names],
                       [v2[n] for n in small_names], 1, "adamw_small")

    results = [{**g_big, **g_small}]
    for big_part, small_part in zip(big_out, small_out):
        results.append({**dict(zip(_BIG, big_part)), **dict(zip(small_names, small_part))})
    flat = [r[name].reshape(weights[name].shape) for r in results for name in _WEIGHT_ORDER]
    return (total_loss.reshape(()), grad_x, *flat)
```

```python
import math

import jax
import jax.numpy as jnp
from jax import lax
from jax.experimental import pallas as pl
from jax.experimental.pallas import tpu as pltpu

F32 = jnp.float32
BF16 = jnp.bfloat16

D_MODEL = 1024
D_SSM = 512
D_ATTN = 512
SSM_GROUPS = 32
SSM_GROUP_CH = 16
SSM_STATE = 64
SSM_LANES = SSM_GROUPS * SSM_STATE
HEAD_DIM = 64
N_HEADS = 8
KV_HEADS = 2
Q_PER_KV = 4
WINDOW = 128
BLOCK = 128
D_PLE = 256
D_IN = 2304
EPS = 1e-6
ATTN_SCALE = 1.0 / math.sqrt(HEAD_DIM)

ADAM_LR = 0.001
ADAM_B1 = 0.9
ADAM_B2 = 0.999
ADAM_EPS = 1e-08
ADAM_WD = 0.01
ADAM_STEP = 10

N_CHIPS = 4
LANES = 128
SCAN_CHUNKS = 8
SCAN_TILE_STEPS = 16
SCAN_LANE_CHUNK = 512
MIB = 2 ** 20
MESH = pl.DeviceIdType.MESH


def _dot(a, b):
    return jnp.dot(a, b, preferred_element_type=F32)


def _dot_nt(a, b):
    return lax.dot_general(a, b, (((1,), (1,)), ((), ())), preferred_element_type=F32)


def _dot_tn(a, b):
    return lax.dot_general(a, b, (((0,), (0,)), ((), ())), preferred_element_type=F32)


def _params(vmem_mib, semantics=None):
    kw = dict(vmem_limit_bytes=vmem_mib * MIB)
    if semantics is not None:
        kw["dimension_semantics"] = semantics
    return pltpu.CompilerParams(**kw)


def _full(shape):
    nd = len(shape)
    return pl.BlockSpec(shape, lambda *_: (0,) * nd)


def _rows(tm, width):
    return pl.BlockSpec((tm, width), lambda i: (i, 0))


def _sds(shape, dtype=F32):
    return pltpu.HBM(shape, dtype)


def _call(body, **kw):
    fn = pl.pallas_call(body, **kw)
    return lambda *args: fn(*[pltpu.with_memory_space_constraint(a, pltpu.HBM) for a in args])


def _silu(z):
    return z * jax.nn.sigmoid(z)


def _in_proj(x2d, g1, w_in_t):
    rows = x2d.shape[0]
    tm = 512

    def body(x_ref, g_ref, w_ref, u_ref, zs_ref, q_ref, k_ref, v_ref, za_ref):
        x = x_ref[...]
        r = lax.rsqrt(jnp.mean(x * x, axis=-1, keepdims=True) + EPS)
        hn = (x * r * g_ref[...]).astype(BF16)

        def proj(a, b):
            return _dot_nt(hn, w_ref[a:b, :])

        u_ref[...] = proj(0, 512)
        zs_ref[...] = proj(512, 1024)
        q_ref[...] = proj(1024, 1536).astype(BF16)
        k_ref[...] = proj(1536, 1664).astype(BF16)
        v_ref[...] = proj(1664, 1792).astype(BF16)
        za_ref[...] = proj(1792, 2304)

    return _call(
        body, name="in_proj", grid=(rows // tm,),
        in_specs=[_rows(tm, D_MODEL), _full((1, D_MODEL)), _full((D_IN, D_MODEL))],
        out_specs=[_rows(tm, 512), _rows(tm, 512), _rows(tm, 512), _rows(tm, 128), _rows(tm, 128), _rows(tm, 512)],
        out_shape=[_sds((rows, 512)), _sds((rows, 512)), _sds((rows, 512), BF16), _sds((rows, 128), BF16),
                   _sds((rows, 128), BF16), _sds((rows, 512))],
        compiler_params=_params(40, ("arbitrary",)),
    )(x2d, g1, w_in_t)


def _in_proj_bwd(x2d, dh1, g1, w_in_t, du, dzs, dq, dk, dv, dza):
    rows = x2d.shape[0]
    tm = 256
    pieces = ((0, 512), (512, 1024), (1024, 1536), (1536, 1664), (1664, 1792), (1792, 2304))

    def body(x_ref, dh1_ref, g_ref, w_ref, du_ref, dzs_ref, dq_ref, dk_ref, dv_ref, dza_ref, gx_ref, dw_ref, dg_ref):
        @pl.when(pl.program_id(0) == 0)
        def _():
            dw_ref[...] = jnp.zeros_like(dw_ref)
            dg_ref[...] = jnp.zeros_like(dg_ref)

        x = x_ref[...]
        g = g_ref[...]
        r = lax.rsqrt(jnp.mean(x * x, axis=-1, keepdims=True) + EPS)
        xr = x * r
        hn = (xr * g).astype(BF16)
        dhn = jnp.zeros((tm, D_MODEL), F32)
        for (a, b), ref in zip(pieces, (du_ref, dzs_ref, dq_ref, dk_ref, dv_ref, dza_ref)):
            piece = ref[...].astype(BF16)
            dhn = dhn + _dot(piece, w_ref[a:b, :])
            dw_ref[a:b, :] += _dot_tn(piece, hn)
        dg_ref[...] += jnp.sum(dhn * xr, axis=0, keepdims=True)
        a_ = dhn * g
        gx_ref[...] = dh1_ref[...] + r * a_ - xr * (r * jnp.mean(a_ * xr, axis=-1, keepdims=True))

    return _call(
        body, name="in_proj_bwd", grid=(rows // tm,),
        in_specs=[_rows(tm, D_MODEL), _rows(tm, D_MODEL), _full((1, D_MODEL)), _full((D_IN, D_MODEL)),
                  _rows(tm, 512), _rows(tm, 512), _rows(tm, 512), _rows(tm, 128), _rows(tm, 128), _rows(tm, 512)],
        out_specs=[_rows(tm, D_MODEL), _full((D_IN, D_MODEL)), _full((1, D_MODEL))],
        out_shape=[_sds((rows, D_MODEL)), _sds((D_IN, D_MODEL)), _sds((1, D_MODEL))],
        compiler_params=_params(52, ("arbitrary",)),
    )(x2d, dh1, g1, w_in_t, du, dzs, dq, dk, dv, dza)


def _iota(shape, axis):
    return lax.broadcasted_iota(jnp.int32, shape, axis)


def _exact_dot(a, b):
    return jnp.dot(a, b, precision=lax.Precision.HIGHEST, preferred_element_type=F32)


_HALF_GROUPS = SSM_GROUPS // 2
_N_SHIFT = SSM_STATE.bit_length() - 1
_P_SHIFT = SSM_GROUP_CH.bit_length() - 1


def _s5_operands(lam_re, lam_im, log_step, b_re, b_im, c_re, c_im):
    g, n, p = SSM_GROUPS, SSM_STATE, SSM_GROUP_CH
    gn, hn_, hp = g * n, _HALF_GROUPS * n, _HALF_GROUPS * p
    eye_g = _iota((g, g), 0) == _iota((g, g), 1)
    step = jnp.sum(jnp.where(eye_g, jnp.exp(log_step), 0.0), axis=1, keepdims=True)
    a_re = lam_re * step
    a_im = lam_im * step
    mag = jnp.exp(a_re)
    lbar_re = mag * jnp.cos(a_im)
    lbar_im = mag * jnp.sin(a_im)
    n_re = lbar_re - 1.0
    den = lam_re * lam_re + lam_im * lam_im
    f_re = (n_re * lam_re + lbar_im * lam_im) / den
    f_im = (lbar_im * lam_re - n_re * lam_im) / den

    spread_n = (_iota((n, gn), 0) == (_iota((n, gn), 1) & (n - 1))).astype(F32)
    own_g = _iota((g, gn), 0) == (_iota((g, gn), 1) >> _N_SHIFT)

    def to_row(a):
        return jnp.sum(jnp.where(own_g, _exact_dot(a, spread_n), 0.0), axis=0, keepdims=True)

    pick_g = ((_iota((gn, g), 0) >> _N_SHIFT) == _iota((gn, g), 1)).astype(F32)
    own_n = (_iota((gn, n), 0) & (n - 1)) == _iota((gn, n), 1)

    def to_col(a):
        return jnp.sum(jnp.where(own_n, _exact_dot(pick_g, a), 0.0), axis=1, keepdims=True)

    fc_re, fc_im = to_col(f_re), to_col(f_im)
    bbar_re = fc_re * b_re - fc_im * b_im
    bbar_im = fc_re * b_im + fc_im * b_re

    tile_p = (_iota((p, hp), 0) == (_iota((p, hp), 1) & (p - 1))).astype(F32)
    block_b = (_iota((hn_, hp), 0) >> _N_SHIFT) == (_iota((hn_, hp), 1) >> _P_SHIFT)
    tile_n = (_iota((n, hn_), 0) == (_iota((n, hn_), 1) & (n - 1))).astype(F32)
    block_c = (_iota((hp, hn_), 0) >> _P_SHIFT) == (_iota((hp, hn_), 1) >> _N_SHIFT)

    def embed_b(a, hf):
        return jnp.where(block_b, _exact_dot(a[hf * hn_:(hf + 1) * hn_], tile_p), 0.0)

    def embed_c(a, hf):
        return jnp.where(block_c, _exact_dot(a[hf * hp:(hf + 1) * hp], tile_n), 0.0)

    return (to_row(lbar_re), to_row(lbar_im), embed_b(bbar_re, 0), embed_b(bbar_re, 1), embed_b(bbar_im, 0),
            embed_b(bbar_im, 1), embed_c(c_re, 0), embed_c(c_re, 1), embed_c(c_im, 0), embed_c(c_im, 1))


_S5_PARAM_SHAPES = ((SSM_GROUPS, SSM_STATE), (SSM_GROUPS, SSM_STATE), (1, SSM_GROUPS),
                    (SSM_LANES, SSM_GROUP_CH), (SSM_LANES, SSM_GROUP_CH), (D_SSM, SSM_STATE), (D_SSM, SSM_STATE))
_BT_SHAPE = (2, _HALF_GROUPS * SSM_STATE, _HALF_GROUPS * SSM_GROUP_CH)
_CM_SHAPE = (2, _HALF_GROUPS * SSM_GROUP_CH, _HALF_GROUPS * SSM_STATE)
_S5_OPERAND_SHAPES = ((1, SSM_LANES), (1, SSM_LANES), _BT_SHAPE, _BT_SHAPE, _CM_SHAPE, _CM_SHAPE)


def _s5_params_fwd(*params):
    def body(*refs):
        ins, (lre_ref, lim_ref, btre_ref, btim_ref, cmre_ref, cmim_ref) = refs[:7], refs[7:]
        vals = _s5_operands(*[r[...] for r in ins])
        lre_ref[...] = vals[0]
        lim_ref[...] = vals[1]
        for ref, pair in zip((btre_ref, btim_ref, cmre_ref, cmim_ref), (vals[2:4], vals[4:6], vals[6:8], vals[8:10])):
            ref[0] = pair[0].astype(BF16)
            ref[1] = pair[1].astype(BF16)

    dtypes = (F32, F32, BF16, BF16, BF16, BF16)
    return _call(
        body, name="s5_params_fwd",
        in_specs=[_full(s) for s in _S5_PARAM_SHAPES], out_specs=[_full(s) for s in _S5_OPERAND_SHAPES],
        out_shape=[_sds(s, d) for s, d in zip(_S5_OPERAND_SHAPES, dtypes)], compiler_params=_params(32),
    )(*params)


def _s5_params_bwd(params, cotangents):
    def body(*refs):
        ins, (dlre, dlim, dbtre, dbtim, dcmre, dcmim), outs = refs[:7], refs[7:13], refs[13:]
        _, vjp = jax.vjp(_s5_operands, *[r[...] for r in ins])
        cts = (dlre[...], dlim[...], dbtre[0], dbtre[1], dbtim[0], dbtim[1], dcmre[0], dcmre[1], dcmim[0], dcmim[1])
        for ref, val in zip(outs, vjp(cts)):
            ref[...] = val

    return _call(
        body, name="s5_params_bwd",
        in_specs=[_full(s) for s in _S5_PARAM_SHAPES + _S5_OPERAND_SHAPES],
        out_specs=[_full(s) for s in _S5_PARAM_SHAPES],
        out_shape=[_sds(s) for s in _S5_PARAM_SHAPES], compiler_params=_params(48),
    )(*params, *cotangents)


def _scan_geometry(n_seq, seq):
    slab = n_seq * SCAN_CHUNKS
    steps = seq // SCAN_CHUNKS
    tile_rows = slab * SCAN_TILE_STEPS
    n_tiles = steps // SCAN_TILE_STEPS
    return slab, steps, tile_rows, n_tiles


def _per_lane_block(body, name, a):
    spec = pl.BlockSpec((a.shape[0], LANES), lambda j: (0, j))
    return _call(body, name=name, grid=(a.shape[1] // LANES,), in_specs=[spec], out_specs=spec,
                          out_shape=_sds(a.shape), compiler_params=_params(32, ("arbitrary",)))(a)


def _to_scan_order(a, n_seq, seq):
    slab, steps, _, _ = _scan_geometry(n_seq, seq)

    def body(a_ref, o_ref):
        def step(t, carry):
            o_ref[pl.ds(pl.multiple_of(t * slab, slab), slab), :] = a_ref[pl.ds(t, slab, stride=steps), :]
            return carry

        lax.fori_loop(0, steps, step, 0)

    return _per_lane_block(body, "to_scan_order", a)


def _from_scan_order(a, n_seq, seq):
    slab, steps, _, _ = _scan_geometry(n_seq, seq)

    def body(a_ref, o_ref):
        def step(t, carry):
            o_ref[pl.ds(t, slab, stride=steps), :] = a_ref[pl.ds(pl.multiple_of(t * slab, slab), slab), :]
            return carry

        lax.fori_loop(0, steps, step, 0)

    return _per_lane_block(body, "from_scan_order", a)


def _complex_power(re, im, n):
    out = None
    while n:
        if n & 1:
            out = (re, im) if out is None else (out[0] * re - out[1] * im, out[0] * im + out[1] * re)
        n >>= 1
        if n:
            re, im = re * re - im * im, 2.0 * re * im
    return out


def _chunk_carry(sum_re, sum_im, carry_re, carry_im, a_re, a_im, n_seq, reverse):
    carry_re[...] = jnp.zeros_like(carry_re)
    carry_im[...] = jnp.zeros_like(carry_im)
    for s in range(n_seq):
        order = range(SCAN_CHUNKS - 2, -1, -1) if reverse else range(1, SCAN_CHUNKS)
        for c in order:
            r = s * SCAN_CHUNKS + c
            p = r + 1 if reverse else r - 1
            p_re, p_im = carry_re[p:p + 1, :], carry_im[p:p + 1, :]
            carry_re[r:r + 1, :] = a_re * p_re - a_im * p_im + sum_re[p:p + 1, :]
            carry_im[r:r + 1, :] = a_re * p_im + a_im * p_re + sum_im[p:p + 1, :]


def _s5_scan_fwd(u_scan, bt_re, bt_im, cm_re, cm_im, lbar_re, lbar_im, d_row, n_seq, seq):
    slab, steps, tile_rows, n_tiles = _scan_geometry(n_seq, seq)
    rows = u_scan.shape[0]

    def body(u_ref, bre_ref, bim_ref, cre_ref, cim_ref, lre_ref, lim_ref, d_ref, y_ref, hre_ref, him_ref,
             st_re, st_im, h0_re, h0_im, buf_re, buf_im):
        second = pl.program_id(0) == 1
        i = pl.program_id(1)

        @pl.when(jnp.logical_and(i == 0, jnp.logical_not(second)))
        def _():
            st_re[...] = jnp.zeros_like(st_re)
            st_im[...] = jnp.zeros_like(st_im)

        u = u_ref[...]
        ub = u.astype(BF16)
        for hf in range(2):
            cols = slice(hf * 1024, (hf + 1) * 1024)
            buf_re[:, cols] = _dot_nt(ub[:, hf * 256:(hf + 1) * 256], bre_ref[hf])
            buf_im[:, cols] = _dot_nt(ub[:, hf * 256:(hf + 1) * 256], bim_ref[hf])

        for lc in range(SSM_LANES // SCAN_LANE_CHUNK):
            cols = slice(lc * SCAN_LANE_CHUNK, (lc + 1) * SCAN_LANE_CHUNK)
            l_re = jnp.broadcast_to(lre_ref[:, cols], (slab, SCAN_LANE_CHUNK))
            l_im = jnp.broadcast_to(lim_ref[:, cols], (slab, SCAN_LANE_CHUNK))

            def step(t, carry):
                s_re, s_im = carry
                r0 = pl.multiple_of(t * slab, slab)
                n_re = l_re * s_re - l_im * s_im + buf_re[pl.ds(r0, slab), cols]
                n_im = l_re * s_im + l_im * s_re + buf_im[pl.ds(r0, slab), cols]
                buf_re[pl.ds(r0, slab), cols] = n_re
                buf_im[pl.ds(r0, slab), cols] = n_im
                return n_re, n_im

            s_re, s_im = lax.fori_loop(0, SCAN_TILE_STEPS, step, (st_re[:, cols], st_im[:, cols]), unroll=True)
            st_re[:, cols] = s_re
            st_im[:, cols] = s_im

        @pl.when(jnp.logical_and(i == n_tiles - 1, jnp.logical_not(second)))
        def _():
            a_re, a_im = _complex_power(lre_ref[...], lim_ref[...], steps)
            _chunk_carry(st_re, st_im, h0_re, h0_im, a_re, a_im, n_seq, reverse=False)
            st_re[...] = h0_re[...]
            st_im[...] = h0_im[...]

        @pl.when(second)
        def _():
            h_re = buf_re[...].astype(BF16)
            h_im = buf_im[...].astype(BF16)
            hre_ref[...] = h_re
            him_ref[...] = h_im
            for hf in range(2):
                cols = slice(hf * 1024, (hf + 1) * 1024)
                ycols = slice(hf * 256, (hf + 1) * 256)
                y_ref[:, ycols] = (_dot_nt(h_re[:, cols], cre_ref[hf]) - _dot_nt(h_im[:, cols], cim_ref[hf])
                                   + d_ref[:, ycols] * u[:, ycols])

    tile = lambda w: pl.BlockSpec((tile_rows, w), lambda p, i: (i, 0))
    out_tile = lambda w: pl.BlockSpec((tile_rows, w), lambda p, i: (i * p, 0))
    bt, cm = _full(_BT_SHAPE), _full(_CM_SHAPE)
    return _call(
        body, name="s5_scan_fwd", grid=(2, n_tiles),
        in_specs=[tile(512), bt, bt, cm, cm, _full((1, SSM_LANES)), _full((1, SSM_LANES)), _full((1, 512))],
        out_specs=[out_tile(512), out_tile(SSM_LANES), out_tile(SSM_LANES)],
        out_shape=[_sds((rows, 512)), _sds((rows, SSM_LANES), BF16), _sds((rows, SSM_LANES), BF16)],
        scratch_shapes=[pltpu.VMEM((slab, SSM_LANES), F32)] * 4 + [pltpu.VMEM((tile_rows, SSM_LANES), F32)] * 2,
        compiler_params=_params(40, ("arbitrary", "arbitrary")),
    )(u_scan, bt_re, bt_im, cm_re, cm_im, lbar_re, lbar_im, d_row)


def _s5_scan_bwd(dy_scan, u_scan, h_re, h_im, bt_re, bt_im, cm_re, cm_im, lbar_re, lbar_im, d_row, n_seq, seq):
    slab, steps, tile_rows, n_tiles = _scan_geometry(n_seq, seq)
    rows = u_scan.shape[0]

    def body(dy_ref, u_ref, hre_ref, him_ref, bre_ref, bim_ref, cre_ref, cim_ref, lre_ref, lim_ref, d_ref,
             du_ref, dbre_ref, dbim_ref, dcre_ref, dcim_ref, dlre_ref, dlim_ref, dd_ref,
             st_re, st_im, g0_re, g0_im, acc_re, acc_im, buf_re, buf_im):
        second = pl.program_id(0) == 1
        i = pl.program_id(1)

        @pl.when(jnp.logical_and(i == 0, jnp.logical_not(second)))
        def _():
            st_re[...] = jnp.zeros_like(st_re)
            st_im[...] = jnp.zeros_like(st_im)
            acc_re[...] = jnp.zeros_like(acc_re)
            acc_im[...] = jnp.zeros_like(acc_im)
            for ref in (dbre_ref, dbim_ref, dcre_ref, dcim_ref, dd_ref):
                ref[...] = jnp.zeros_like(ref)

        dy = dy_ref[...]
        dyb = dy.astype(BF16)
        for hf in range(2):
            cols = slice(hf * 1024, (hf + 1) * 1024)
            buf_re[:, cols] = _dot(dyb[:, hf * 256:(hf + 1) * 256], cre_ref[hf])
            buf_im[:, cols] = -_dot(dyb[:, hf * 256:(hf + 1) * 256], cim_ref[hf])

        for lc in range(SSM_LANES // SCAN_LANE_CHUNK):
            cols = slice(lc * SCAN_LANE_CHUNK, (lc + 1) * SCAN_LANE_CHUNK)
            l_re = jnp.broadcast_to(lre_ref[:, cols], (slab, SCAN_LANE_CHUNK))
            l_im = jnp.broadcast_to(lim_ref[:, cols], (slab, SCAN_LANE_CHUNK))

            def step(k, carry):
                s_re, s_im, a_re, a_im = carry
                r0 = pl.multiple_of((SCAN_TILE_STEPS - 1 - k) * slab, slab)
                hr = hre_ref[pl.ds(r0, slab), cols].astype(F32)
                hi = him_ref[pl.ds(r0, slab), cols].astype(F32)
                a_re = a_re + s_re * hr + s_im * hi
                a_im = a_im + s_im * hr - s_re * hi
                n_re = l_re * s_re + l_im * s_im + buf_re[pl.ds(r0, slab), cols]
                n_im = l_re * s_im - l_im * s_re + buf_im[pl.ds(r0, slab), cols]
                buf_re[pl.ds(r0, slab), cols] = n_re
                buf_im[pl.ds(r0, slab), cols] = n_im
                return n_re, n_im, a_re, a_im

            zero = jnp.zeros((slab, SCAN_LANE_CHUNK), F32)
            s_re, s_im, a_re, a_im = lax.fori_loop(
                0, SCAN_TILE_STEPS, step, (st_re[:, cols], st_im[:, cols], zero, zero), unroll=True)
            st_re[:, cols] = s_re
            st_im[:, cols] = s_im

            @pl.when(second)
            def _():
                acc_re[:, cols] += a_re
                acc_im[:, cols] += a_im

        @pl.when(jnp.logical_and(i == n_tiles - 1, jnp.logical_not(second)))
        def _():
            p_re, p_im = _complex_power(lre_ref[...], lim_ref[...], steps)
            _chunk_carry(st_re, st_im, g0_re, g0_im, p_re, -p_im, n_seq, reverse=True)
            st_re[...] = g0_re[...]
            st_im[...] = g0_im[...]

        @pl.when(second)
        def _():
            u = u_ref[...]
            ub = u.astype(BF16)
            g_re = buf_re[...].astype(BF16)
            g_im = buf_im[...].astype(BF16)
            dd_ref[...] += jnp.sum(dy * u, axis=0, keepdims=True)
            for hf in range(2):
                cols = slice(hf * 1024, (hf + 1) * 1024)
                ycols = slice(hf * 256, (hf + 1) * 256)
                du_ref[:, ycols] = (_dot(g_re[:, cols], bre_ref[hf]) + _dot(g_im[:, cols], bim_ref[hf])
                                    + d_ref[:, ycols] * dy[:, ycols])
                dbre_ref[hf] += _dot_tn(g_re[:, cols], ub[:, ycols])
                dbim_ref[hf] += _dot_tn(g_im[:, cols], ub[:, ycols])
                dcre_ref[hf] += _dot_tn(dyb[:, ycols], hre_ref[:, cols])
                dcim_ref[hf] -= _dot_tn(dyb[:, ycols], him_ref[:, cols])

        @pl.when(jnp.logical_and(i == n_tiles - 1, second))
        def _():
            dlre_ref[...] = jnp.sum(acc_re[...], axis=0, keepdims=True)
            dlim_ref[...] = jnp.sum(acc_im[...], axis=0, keepdims=True)

    tile = lambda w: pl.BlockSpec((tile_rows, w), lambda p, i: (n_tiles - 1 - i, 0))
    second_tile = lambda w: pl.BlockSpec((tile_rows, w), lambda p, i: (n_tiles - 1 - i * p, 0))
    bt, cm = _full(_BT_SHAPE), _full(_CM_SHAPE)
    row = _full((1, SSM_LANES))
    return _call(
        body, name="s5_scan_bwd", grid=(2, n_tiles),
        in_specs=[tile(512), second_tile(512), second_tile(SSM_LANES), second_tile(SSM_LANES),
                  bt, bt, cm, cm, row, row, _full((1, 512))],
        out_specs=[second_tile(512), bt, bt, cm, cm, row, row, _full((1, 512))],
        out_shape=([_sds((rows, 512))] + [_sds(_BT_SHAPE)] * 2 + [_sds(_CM_SHAPE)] * 2 + [_sds((1, SSM_LANES))] * 2
                   + [_sds((1, 512))]),
        scratch_shapes=[pltpu.VMEM((slab, SSM_LANES), F32)] * 6 + [pltpu.VMEM((tile_rows, SSM_LANES), F32)] * 2,
        compiler_params=_params(48, ("arbitrary", "arbitrary")),
    )(dy_scan, u_scan, h_re, h_im, bt_re, bt_im, cm_re, cm_im, lbar_re, lbar_im, d_row)


def _glu_gate(gl, a, zs):
    return gl * jax.nn.sigmoid(a) * _silu(zs)


def _glu_fwd(y, zs, w_glu, b_glu):
    rows = y.shape[0]
    tm = 512

    def body(y_ref, zs_ref, w_ref, b_ref, o_ref):
        gl = jax.nn.gelu(y_ref[...])
        a = _dot(gl.astype(BF16), w_ref[...]) + b_ref[...]
        o_ref[...] = _glu_gate(gl, a, zs_ref[...]).astype(BF16)

    return _call(
        body, name="glu_fwd", grid=(rows // tm,),
        in_specs=[_rows(tm, 512), _rows(tm, 512), _full((512, 512)), _full((1, 512))],
        out_specs=_rows(tm, 512), out_shape=_sds((rows, 512), BF16),
        compiler_params=_params(32, ("arbitrary",)),
    )(y, zs, w_glu, b_glu)


def _glu_bwd(y, zs, d_out, w_glu, b_glu):
    rows = y.shape[0]
    tm = 512

    def body(y_ref, zs_ref, d_ref, w_ref, b_ref, dy_ref, dzs_ref, dw_ref, db_ref):
        @pl.when(pl.program_id(0) == 0)
        def _():
            dw_ref[...] = jnp.zeros_like(dw_ref)
            db_ref[...] = jnp.zeros_like(db_ref)

        gl, gelu_vjp = jax.vjp(jax.nn.gelu, y_ref[...])
        glb = gl.astype(BF16)
        a = _dot(glb, w_ref[...]) + b_ref[...]
        _, gate_vjp = jax.vjp(_glu_gate, gl, a, zs_ref[...])
        d_gl, d_a, d_zs = gate_vjp(d_ref[...])
        dab = d_a.astype(BF16)
        d_gl = d_gl + _dot_nt(dab, w_ref[...])
        dy_ref[...] = gelu_vjp(d_gl)[0]
        dzs_ref[...] = d_zs.astype(BF16)
        dw_ref[...] += _dot_tn(glb, dab)
        db_ref[...] += jnp.sum(d_a, axis=0, keepdims=True)

    return _call(
        body, name="glu_bwd", grid=(rows // tm,),
        in_specs=[_rows(tm, 512), _rows(tm, 512), _rows(tm, 512), _full((512, 512)), _full((1, 512))],
        out_specs=[_rows(tm, 512), _rows(tm, 512), _full((512, 512)), _full((1, 512))],
        out_shape=[_sds((rows, 512)), _sds((rows, 512), BF16), _sds((512, 512)), _sds((1, 512))],
        compiler_params=_params(32, ("arbitrary",)),
    )(y, zs, d_out, w_glu, b_glu)


_GROUP_ROWS = Q_PER_KV * BLOCK
_BLOCK_SHIFT = BLOCK.bit_length() - 1


def _attn_bias(j):
    row = _iota((_GROUP_ROWS, BLOCK), 0)
    dist_cur = (row & (BLOCK - 1)) - _iota((_GROUP_ROWS, BLOCK), 1)
    dist_prev = dist_cur + BLOCK
    head = row >> _BLOCK_SHIFT
    slope = jnp.zeros((_GROUP_ROWS, BLOCK), F32)
    for g in range(Q_PER_KV):
        slope = jnp.where(head == g, 2.0 ** (-(j * Q_PER_KV + g + 1)), slope)
    bias_cur = jnp.where(dist_cur >= 0, -slope * dist_cur.astype(F32), -jnp.inf)
    bias_prev = jnp.where(dist_prev < WINDOW, -slope * dist_prev.astype(F32), -jnp.inf)
    return bias_cur, bias_prev


def _stack_heads(x, j):
    heads = range(j * Q_PER_KV, (j + 1) * Q_PER_KV)
    return jnp.concatenate([x[:, h * HEAD_DIM:(h + 1) * HEAD_DIM] for h in heads], axis=0)


def _stack_columns(x, j):
    heads = range(j * Q_PER_KV, (j + 1) * Q_PER_KV)
    return jnp.concatenate([jnp.broadcast_to(x[:, h:h + 1], (BLOCK, 1)) for h in heads], axis=0)


def _attn_fwd(q, k, v, za, sinks, n_seq, seq):
    nb = seq // BLOCK
    rows = q.shape[0]

    def body(q_ref, kc_ref, kp_ref, vc_ref, vp_ref, za_ref, sk_ref, o_ref, ao_ref, lse_ref):
        has_prev = pl.program_id(1) > 0
        q_all = q_ref[...]
        for j in range(KV_HEADS):
            js = slice(j * HEAD_DIM, (j + 1) * HEAD_DIM)
            bias_c, bias_p = _attn_bias(j)
            q4 = _stack_heads(q_all, j)
            sc = _dot_nt(q4, kc_ref[:, js]) * ATTN_SCALE + bias_c
            sp = _dot_nt(q4, kp_ref[:, js]) * ATTN_SCALE + jnp.where(has_prev, bias_p, -jnp.inf)
            sink = _stack_columns(sk_ref[...], j)
            m = jnp.maximum(jnp.maximum(jnp.max(sc, axis=-1, keepdims=True), jnp.max(sp, axis=-1, keepdims=True)), sink)
            ec = jnp.exp(sc - m)
            ep = jnp.exp(sp - m)
            den = jnp.sum(ec, axis=-1, keepdims=True) + jnp.sum(ep, axis=-1, keepdims=True) + jnp.exp(sink - m)
            o4 = (_dot(ec.astype(BF16), vc_ref[:, js]) + _dot(ep.astype(BF16), vp_ref[:, js])) * (1.0 / den)
            lse4 = m + jnp.log(den)
            for g in range(Q_PER_KV):
                h = j * Q_PER_KV + g
                o_ref[:, h * HEAD_DIM:(h + 1) * HEAD_DIM] = o4[g * BLOCK:(g + 1) * BLOCK]
                lse_ref[:, h:h + 1] = lse4[g * BLOCK:(g + 1) * BLOCK]
        ao_ref[...] = (o_ref[...] * _silu(za_ref[...])).astype(BF16)

    cur = lambda w: pl.BlockSpec((BLOCK, w), lambda b, n: (b * nb + n, 0))
    prev = lambda w: pl.BlockSpec((BLOCK, w), lambda b, n: (b * nb + jnp.maximum(n - 1, 0), 0))
    return _call(
        body, name="attn_fwd", grid=(n_seq, nb),
        in_specs=[cur(512), cur(128), prev(128), cur(128), prev(128), cur(512), _full((1, N_HEADS))],
        out_specs=[cur(512), cur(512), cur(N_HEADS)],
        out_shape=[_sds((rows, 512)), _sds((rows, 512), BF16), _sds((rows, N_HEADS))],
        compiler_params=_params(32, ("arbitrary", "arbitrary")),
    )(q, k, k, v, v, za, sinks)


def _attn_bwd(q, k, v, za, o, lse, d_ao, sinks, n_seq, seq):
    nb = seq // BLOCK
    rows = q.shape[0]

    def body(q_ref, q2_ref, kc_ref, kp_ref, vc_ref, vp_ref, za_ref, za2_ref, o_ref, o2_ref, lse_ref, lse2_ref,
             d_ref, d2_ref, sk_ref, dq_ref, dk_ref, dv_ref, dza_ref, dsk_ref):
        n = pl.program_id(1)

        @pl.when(jnp.logical_and(pl.program_id(0) == 0, n == 0))
        def _():
            dsk_ref[...] = jnp.zeros_like(dsk_ref)

        has_prev = n > 0
        has_next = n + 1 < nb

        o = o_ref[...]
        _, gate_vjp = jax.vjp(lambda o_, z_: o_ * _silu(z_), o, za_ref[...])
        d_o, d_za = gate_vjp(d_ref[...])
        dza_ref[...] = d_za.astype(BF16)
        o2 = o2_ref[...]
        d_o2 = d2_ref[...] * _silu(za2_ref[...])
        q_all, q2_all = q_ref[...], q2_ref[...]
        lse_all, lse2_all = lse_ref[...], lse2_ref[...]

        for j in range(KV_HEADS):
            js = slice(j * HEAD_DIM, (j + 1) * HEAD_DIM)
            kc, kp, vc, vp = kc_ref[:, js], kp_ref[:, js], vc_ref[:, js], vp_ref[:, js]
            bias_c, bias_p = _attn_bias(j)
            q4 = _stack_heads(q_all, j)
            do4 = _stack_heads(d_o, j)
            do4b = do4.astype(BF16)
            delta = jnp.sum(do4 * _stack_heads(o, j), axis=-1, keepdims=True)
            lse4 = _stack_columns(lse_all, j)
            pc = jnp.exp(_dot_nt(q4, kc) * ATTN_SCALE + bias_c - lse4)
            pp = jnp.exp(_dot_nt(q4, kp) * ATTN_SCALE + jnp.where(has_prev, bias_p, -jnp.inf) - lse4)
            dsc = (pc * (_dot_nt(do4b, vc) - delta)).astype(BF16)
            dsp = (pp * (_dot_nt(do4b, vp) - delta)).astype(BF16)
            dq4 = ((_dot(dsc, kc) + _dot(dsp, kp)) * ATTN_SCALE).astype(BF16)
            sink_loss = jnp.exp(_stack_columns(sk_ref[...], j) - lse4) * delta
            for g in range(Q_PER_KV):
                h = j * Q_PER_KV + g
                dq_ref[:, h * HEAD_DIM:(h + 1) * HEAD_DIM] = dq4[g * BLOCK:(g + 1) * BLOCK]
                dsk_ref[0:1, h:h + 1] -= jnp.sum(sink_loss[g * BLOCK:(g + 1) * BLOCK], axis=0, keepdims=True)
            dk = _dot_tn(dsc, q4)
            dv = _dot_tn(pc.astype(BF16), do4b)
            q4n = _stack_heads(q2_all, j)
            do4n = _stack_heads(d_o2, j)
            do4nb = do4n.astype(BF16)
            delta2 = jnp.sum(do4n * _stack_heads(o2, j), axis=-1, keepdims=True)
            p2 = jnp.exp(_dot_nt(q4n, kc) * ATTN_SCALE + jnp.where(has_next, bias_p, -jnp.inf)
                         - _stack_columns(lse2_all, j))
            ds2 = (p2 * (_dot_nt(do4nb, vc) - delta2)).astype(BF16)
            dk = dk + _dot_tn(ds2, q4n)
            dv = dv + _dot_tn(p2.astype(BF16), do4nb)
            dk_ref[:, js] = (dk * ATTN_SCALE).astype(BF16)
            dv_ref[:, js] = dv.astype(BF16)

    cur = lambda w: pl.BlockSpec((BLOCK, w), lambda b, n: (b * nb + n, 0))
    prev = lambda w: pl.BlockSpec((BLOCK, w), lambda b, n: (b * nb + jnp.maximum(n - 1, 0), 0))
    nxt = lambda w: pl.BlockSpec((BLOCK, w), lambda b, n: (b * nb + jnp.minimum(n + 1, nb - 1), 0))
    return _call(
        body, name="attn_bwd", grid=(n_seq, nb),
        in_specs=[cur(512), nxt(512), cur(128), prev(128), cur(128), prev(128), cur(512), nxt(512),
                  cur(512), nxt(512), cur(N_HEADS), nxt(N_HEADS), cur(512), nxt(512), _full((1, N_HEADS))],
        out_specs=[cur(512), cur(128), cur(128), cur(512), _full((1, N_HEADS))],
        out_shape=[_sds((rows, 512), BF16), _sds((rows, 128), BF16), _sds((rows, 128), BF16),
                   _sds((rows, 512), BF16), _sds((1, N_HEADS))],
        compiler_params=_params(32, ("arbitrary", "arbitrary")),
    )(q, q, k, k, v, v, za, za, o, o, lse, lse, d_ao, d_ao, sinks)


def _tail(ssm_out, attn_out, x2d, p2d, target, w_out, g2, w_gate, b_gate, w_proj):
    rows = x2d.shape[0]
    tm = 256

    def body(so_ref, ao_ref, x_ref, p_ref, t_ref, wo_ref, g2_ref, wg_ref, bg_ref, wp_ref,
             dh1_ref, dso_ref, dao_ref, dwo_ref, dwg_ref, dwp_ref, dbg_ref, dg2_ref, loss_ref):
        @pl.when(pl.program_id(0) == 0)
        def _():
            for ref in (dwo_ref, dwg_ref, dwp_ref, dbg_ref, dg2_ref, loss_ref):
                ref[...] = jnp.zeros_like(ref)

        so = so_ref[...]
        ao = ao_ref[...]
        g2 = g2_ref[...]
        mixed = _dot(so, wo_ref[0:512, :]) + _dot(ao, wo_ref[512:1024, :])
        r = lax.rsqrt(jnp.mean(mixed * mixed, axis=-1, keepdims=True) + EPS)
        mr = mixed * r
        h1 = x_ref[...] + mr * g2
        h1b = h1.astype(BF16)
        gate = jax.nn.sigmoid(_dot(h1b, wg_ref[...]) + bg_ref[...])
        pb = p_ref[...].astype(BF16)
        wp_blocks = [slice(j * D_PLE, (j + 1) * D_PLE) for j in range(N_CHIPS)]
        pp = jnp.concatenate([_dot(pb, wp_ref[blk, :]) for blk in wp_blocks], axis=1)
        err = h1 + gate * pp - t_ref[...]
        loss_ref[...] += 0.5 * jnp.sum(jnp.mean(err * err, axis=-1, keepdims=True), axis=0, keepdims=True)

        dh2 = err * (1.0 / D_MODEL)
        d_glin = dh2 * pp * gate * (1.0 - gate)
        d_glin_b = d_glin.astype(BF16)
        dwg_ref[...] += _dot_tn(h1b, d_glin_b)
        dbg_ref[...] += jnp.sum(d_glin, axis=0, keepdims=True)
        d_pp = (dh2 * gate).astype(BF16)
        for blk in wp_blocks:
            dwp_ref[blk, :] += _dot_tn(pb, d_pp[:, blk])
        dh1 = dh2 + _dot_nt(d_glin_b, wg_ref[...])
        dh1_ref[...] = dh1
        dg2_ref[...] += jnp.sum(dh1 * mr, axis=0, keepdims=True)
        a_ = dh1 * g2
        d_mixed = (r * a_ - mr * (r * jnp.mean(a_ * mr, axis=-1, keepdims=True))).astype(BF16)
        dwo_ref[0:512, :] += _dot_tn(so, d_mixed)
        dwo_ref[512:1024, :] += _dot_tn(ao, d_mixed)
        dso_ref[...] = _dot_nt(d_mixed, wo_ref[0:512, :])
        dao_ref[...] = _dot_nt(d_mixed, wo_ref[512:1024, :])

    return _call(
        body, name="tail_fwd_bwd", grid=(rows // tm,),
        in_specs=[_rows(tm, 512), _rows(tm, 512), _rows(tm, D_MODEL), _rows(tm, D_PLE), _rows(tm, D_MODEL),
                  _full((D_MODEL, D_MODEL)), _full((1, D_MODEL)), _full((D_MODEL, D_MODEL)), _full((1, D_MODEL)),
                  _full((N_CHIPS * D_PLE, D_PLE))],
        out_specs=[_rows(tm, D_MODEL), _rows(tm, 512), _rows(tm, 512), _full((D_MODEL, D_MODEL)),
                   _full((D_MODEL, D_MODEL)), _full((N_CHIPS * D_PLE, D_PLE)), _full((1, D_MODEL)), _full((1, D_MODEL)),
                   _full((1, 1))],
        out_shape=[_sds((rows, D_MODEL)), _sds((rows, 512)), _sds((rows, 512)), _sds((D_MODEL, D_MODEL)),
                   _sds((D_MODEL, D_MODEL)), _sds((N_CHIPS * D_PLE, D_PLE)), _sds((1, D_MODEL)), _sds((1, D_MODEL)),
                   _sds((1, 1))],
        compiler_params=_params(52, ("arbitrary",)),
    )(ssm_out, attn_out, x2d, p2d, target, w_out, g2, w_gate, b_gate, w_proj)


def _local_step(x, p, target, pre_norm_g, w_in_t, s5_params, ssm_d, w_glu, b_glu, sinks, w_out, post_norm_g, w_proj,
                w_gate, b_gate):
    n_seq, seq, _ = x.shape
    rows = n_seq * seq
    x2d = x.reshape(rows, D_MODEL)
    p2d = p.reshape(rows, D_PLE)
    t2d = target.reshape(rows, D_MODEL)

    l_re, l_im, bt_re, bt_im, cm_re, cm_im = _s5_params_fwd(*s5_params)

    u, zs, q, k, v, za = _in_proj(x2d, pre_norm_g, w_in_t)
    u_scan = _to_scan_order(u, n_seq, seq)
    y_scan, h_re, h_im = _s5_scan_fwd(u_scan, bt_re, bt_im, cm_re, cm_im, l_re, l_im, ssm_d, n_seq, seq)
    y = _from_scan_order(y_scan, n_seq, seq)
    ssm_out = _glu_fwd(y, zs, w_glu, b_glu)
    o, attn_out, lse = _attn_fwd(q, k, v, za, sinks, n_seq, seq)

    dh1, d_so, d_ao, d_w_out, d_w_gate, d_w_proj, d_b_gate, d_g2, loss = _tail(
        ssm_out, attn_out, x2d, p2d, t2d, w_out, post_norm_g, w_gate, b_gate, w_proj)

    dq, dk, dv, dza, d_sinks = _attn_bwd(q, k, v, za, o, lse, d_ao, sinks, n_seq, seq)
    dy, dzs, d_w_glu, d_b_glu = _glu_bwd(y, zs, d_so, w_glu, b_glu)
    dy_scan = _to_scan_order(dy, n_seq, seq)
    du_scan, d_bt_re, d_bt_im, d_cm_re, d_cm_im, d_l_re, d_l_im, d_d = _s5_scan_bwd(
        dy_scan, u_scan, h_re, h_im, bt_re, bt_im, cm_re, cm_im, l_re, l_im, ssm_d, n_seq, seq)
    du = _from_scan_order(du_scan, n_seq, seq)
    d_lam_re, d_lam_im, d_log_step, d_b_re, d_b_im, d_c_re, d_c_im = _s5_params_bwd(
        s5_params, (d_l_re, d_l_im, d_bt_re, d_bt_im, d_cm_re, d_cm_im))

    grad_x, d_w_in_t, d_g1 = _in_proj_bwd(x2d, dh1, pre_norm_g, w_in_t, du, dzs, dq, dk, dv, dza)
    grads = dict(
        pre_norm_g=d_g1, w_in=d_w_in_t, ssm_lam_re=d_lam_re, ssm_lam_im=d_lam_im, ssm_log_step=d_log_step,
        ssm_b_re=d_b_re, ssm_b_im=d_b_im, ssm_c_re=d_c_re, ssm_c_im=d_c_im, ssm_d=d_d, ssm_w_glu=d_w_glu,
        ssm_b_glu=d_b_glu, attn_sinks=d_sinks, w_out=d_w_out, post_norm_g=d_g2, pl_w_proj=d_w_proj,
        pl_w_gate=d_w_gate, pl_b_gate=d_b_gate)
    return grad_x.reshape(x.shape), loss, grads


_BIG = ("w_in", "ssm_w_glu", "w_out", "pl_w_proj", "pl_w_gate")
_BIG_SHARD = {"w_in": (D_IN // N_CHIPS, D_MODEL), "ssm_w_glu": (D_SSM // N_CHIPS, D_SSM),
              "w_out": (D_MODEL // N_CHIPS, D_MODEL), "pl_w_proj": (D_PLE, D_MODEL // N_CHIPS),
              "pl_w_gate": (D_MODEL // N_CHIPS, D_MODEL)}
_SMALL = {"pre_norm_g": (1, D_MODEL), "ssm_lam_re": (SSM_GROUPS, SSM_STATE), "ssm_lam_im": (SSM_GROUPS, SSM_STATE),
          "ssm_log_step": (1, SSM_GROUPS), "ssm_b_re": (SSM_LANES, SSM_GROUP_CH), "ssm_b_im": (SSM_LANES, SSM_GROUP_CH),
          "ssm_c_re": (D_SSM, SSM_STATE), "ssm_c_im": (D_SSM, SSM_STATE), "ssm_d": (1, D_SSM), "ssm_b_glu": (1, D_SSM),
          "attn_sinks": (1, N_HEADS), "post_norm_g": (1, D_MODEL), "pl_b_gate": (1, D_MODEL)}
_B_TRAVEL = (SSM_LANES * SSM_GROUP_CH // LANES, LANES)
_VEC_ROWS = ("pre_norm_g", "post_norm_g", "pl_b_gate", "ssm_d", "ssm_b_glu", "attn_sinks", "ssm_log_step", "loss")
_SMALL_GROUPS = (
    ("vec", (8, D_MODEL), tuple((name, r) for r, name in enumerate(_VEC_ROWS))),
    ("lam", (2 * SSM_GROUPS, SSM_STATE), (("ssm_lam_re", 0), ("ssm_lam_im", SSM_GROUPS))),
    ("b", (2 * _B_TRAVEL[0], LANES), (("ssm_b_re", 0), ("ssm_b_im", _B_TRAVEL[0]))),
    ("c", (2 * D_SSM, SSM_STATE), (("ssm_c_re", 0), ("ssm_c_im", D_SSM))),
)
_SMALL_ORDER = tuple(name for _, _, members in _SMALL_GROUPS for name, _ in members)
_WEIGHT_ORDER = ("pre_norm_g", "w_in", "ssm_lam_re", "ssm_lam_im", "ssm_log_step", "ssm_b_re", "ssm_b_im", "ssm_c_re",
                 "ssm_c_im", "ssm_d", "ssm_w_glu", "ssm_b_glu", "attn_sinks", "w_out", "post_norm_g", "pl_w_proj",
                 "pl_w_gate", "pl_b_gate")


def _small_shape(name):
    if name == "loss":
        return (1, 1)
    return _B_TRAVEL if name in ("ssm_b_re", "ssm_b_im") else _SMALL[name]


def _mesh_place():
    x, y, c = lax.axis_index("x"), lax.axis_index("y"), lax.axis_index("c")
    other_chips = ((1 - x, y), (x, 1 - y), (1 - x, 1 - y))
    return x, y, c, other_chips


def _gather_weights(shards):
    n_t = len(shards)

    def body(*refs):
        s_refs, g_refs = refs[:n_t], refs[n_t:2 * n_t]
        send_sems, recv_sems, local_sems = refs[2 * n_t:]
        x, y, c, other_chips = _mesh_place()
        started = []
        for i, (s_ref, g_ref) in enumerate(zip(s_refs, g_refs)):
            rows = s_ref.shape[0]
            half = rows // 2

            def block(chip, g_ref=g_ref, rows=rows, half=half):
                return g_ref.at[pl.ds((2 * chip[0] + chip[1]) * rows + c * half, half), :]

            def copy(k, chip, to, src=None, i=i, block=block):
                return pltpu.make_async_remote_copy(
                    src_ref=block(chip) if src is None else src, dst_ref=block(chip), send_sem=send_sems.at[6 * i + k],
                    recv_sem=recv_sems.at[6 * i + k], device_id=to, device_id_type=MESH)

            own = pltpu.make_async_copy(s_ref, g_ref.at[pl.ds((2 * x + y) * rows, rows), :], local_sems.at[i])
            own.start()
            first = [copy(k, (x, y), (*chip, c), src=s_ref.at[pl.ds(c * half, half), :])
                     for k, chip in enumerate(other_chips)]
            for cp in first:
                cp.start()
            passed = [copy(3 + k, chip, (x, y, 1 - c)) for k, chip in enumerate(other_chips)]
            started.append((own, first, passed))
        for own, first, passed in started:
            for k in range(3):
                first[k].wait_recv()
                passed[k].start()
        for own, first, passed in started:
            for k in range(3):
                passed[k].wait_recv()
            for cp in first + passed:
                cp.wait_send()
            own.wait()

    any_spec = pl.BlockSpec(memory_space=pl.ANY)
    return _call(
        body, name="gather_weights", in_specs=[any_spec] * n_t, out_specs=[any_spec] * n_t,
        out_shape=[_sds((N_CHIPS * s.shape[0], s.shape[1]), s.dtype) for s in shards],
        scratch_shapes=[pltpu.SemaphoreType.DMA((6 * n_t,)), pltpu.SemaphoreType.DMA((6 * n_t,)),
                        pltpu.SemaphoreType.DMA((n_t,))],
    )(*shards)


def _exchange_grads(big, small):
    n_t = len(big)
    n_g = len(_SMALL_GROUPS)
    names = _SMALL_ORDER
    halves = [(b.shape[0] // N_CHIPS // 2, b.shape[1]) for b in big]
    n_sems = 4 * n_g + 8 * n_t

    def body(*refs):
        pos = 0

        def take(n):
            nonlocal pos
            pos += n
            return refs[pos - n:pos]

        big_refs, small_refs = take(n_t), dict(zip(names, take(len(names))))
        out_refs, small_out_refs, land_refs = take(n_t), dict(zip(names, take(len(names)))), take(n_t)
        ga, gb, pme, send_b, recv_b = take(n_t), take(n_t), take(n_t), take(n_t), take(n_t)
        s_own, s_sib, s_chips, s_pair = take(n_g), take(n_g), take(n_g), take(n_g)
        stage = dict(zip(names, take(len(names))))
        send_sems, recv_sems, local_sems = take(3)
        x, y, c, other_chips = _mesh_place()
        me = 2 * x + y
        sibling = (x, y, 1 - c)
        sem_at = iter(range(n_sems))

        def remote(src, dst, to):
            k = next(sem_at)
            return pltpu.make_async_remote_copy(src_ref=src, dst_ref=dst, send_sem=send_sems.at[k],
                                                recv_sem=recv_sems.at[k], device_id=to, device_id_type=MESH)

        loads = [pltpu.make_async_copy(small_refs[name], stage[name], local_sems.at[2 + n_t + a])
                 for a, name in enumerate(names)]
        for cp in loads:
            cp.start()
        for cp in loads:
            cp.wait()
        small_swaps = []
        for gi, (_, _, members) in enumerate(_SMALL_GROUPS):
            s_own[gi][...] = jnp.zeros_like(s_own[gi])
            for name, r0 in members:
                r, n = _small_shape(name)
                s_own[gi][r0:r0 + r, 0:n] = stage[name][...]
            small_swaps.append(remote(s_own[gi], s_sib[gi], sibling))
            small_swaps[gi].start()
        big_swaps = []
        for i in range(n_t):
            hr = halves[i][0]
            big_swaps.append([])
            for j in range(N_CHIPS):
                src = big_refs[i].at[pl.ds(j * 2 * hr + (1 - c) * hr, hr), :]
                big_swaps[i].append(remote(src, land_refs[i].at[j], sibling))
                big_swaps[i][j].start()
        small_sends = []
        for gi in range(n_g):
            small_swaps[gi].wait_recv()
            s_pair[gi][...] = s_own[gi][...] + s_sib[gi][...]
            small_sends.append([remote(s_pair[gi], s_chips[gi].at[k], (*chip, c)) for k, chip in enumerate(other_chips)])
            for cp in small_sends[gi]:
                cp.start()

        def pair_sum(i, j, dst):
            hr = halves[i][0]
            a = pltpu.make_async_copy(big_refs[i].at[pl.ds(j * 2 * hr + c * hr, hr), :], ga[i], local_sems.at[0])
            b = pltpu.make_async_copy(land_refs[i].at[j], gb[i], local_sems.at[1])
            a.start()
            b.start()
            a.wait()
            b.wait()
            dst[...] = (ga[i][...] + gb[i][...]).astype(dst.dtype)

        big_sends = []
        for i in range(n_t):
            for j in range(N_CHIPS):
                big_swaps[i][j].wait_recv()
            big_sends.append([])
            for k, chip in enumerate(other_chips):
                pair_sum(i, 2 * chip[0] + chip[1], send_b[i].at[k])
                big_sends[i].append(remote(send_b[i].at[k], recv_b[i].at[k], (*chip, c)))
                big_sends[i][k].start()
        last_swaps, keeps = [], []
        for i in range(n_t):
            hr = halves[i][0]
            pair_sum(i, me, pme[i])
            for k in range(3):
                big_sends[i][k].wait_recv()
            pme[i][...] = ((pme[i][...] + recv_b[i][0].astype(F32)) + recv_b[i][1].astype(F32)) + recv_b[i][2].astype(F32)
            mine = out_refs[i].at[pl.ds(c * hr, hr), :]
            keeps.append(pltpu.make_async_copy(pme[i], mine, local_sems.at[2 + i]))
            keeps[i].start()
            last_swaps.append(remote(pme[i], mine, sibling))
            last_swaps[i].start()

        for gi, (_, _, members) in enumerate(_SMALL_GROUPS):
            for k in range(3):
                small_sends[gi][k].wait_recv()
            total = None
            for j in range(N_CHIPS):
                rel = jnp.bitwise_xor(j, me)
                term = jnp.where(rel == 0, s_pair[gi][...], jnp.where(
                    rel == 2, s_chips[gi][0], jnp.where(rel == 1, s_chips[gi][1], s_chips[gi][2])))
                total = term if total is None else total + term
            s_sib[gi][...] = total
            for name, r0 in members:
                r, n = _small_shape(name)
                stage[name][...] = s_sib[gi][r0:r0 + r, 0:n]
        stores = [pltpu.make_async_copy(stage[name], small_out_refs[name], local_sems.at[2 + n_t + a])
                  for a, name in enumerate(names)]
        for cp in stores:
            cp.start()

        for i in range(n_t):
            last_swaps[i].wait_recv()
            keeps[i].wait()
        for cp in stores:
            cp.wait()
        for cp in (small_swaps + [cp for group in big_swaps + small_sends + big_sends for cp in group] + last_swaps):
            cp.wait_send()

    any_spec = pl.BlockSpec(memory_space=pl.ANY)
    small_shapes = [_sds(_small_shape(n)) for n in names]
    group_shapes = [shape for _, shape, _ in _SMALL_GROUPS]
    per_matrix = lambda dtype, lead=(): [pltpu.VMEM(lead + h, dtype) for h in halves]
    outs = _call(
        body, name="exchange_grads",
        in_specs=[any_spec] * (n_t + len(names)),
        out_specs=[any_spec] * (2 * n_t + len(names)),
        out_shape=([_sds((b.shape[0] // N_CHIPS, b.shape[1])) for b in big] + small_shapes
                   + [_sds((N_CHIPS,) + h) for h in halves]),
        scratch_shapes=(per_matrix(F32) + per_matrix(F32) + per_matrix(F32) + per_matrix(BF16, (3,)) + per_matrix(BF16, (3,))
                        + [pltpu.VMEM(s, F32) for s in group_shapes] * 2 + [pltpu.VMEM((3,) + s, F32) for s in group_shapes]
                        + [pltpu.VMEM(s, F32) for s in group_shapes]
                        + [pltpu.VMEM(_small_shape(n), F32) for n in names]
                        + [pltpu.SemaphoreType.DMA((n_sems,)), pltpu.SemaphoreType.DMA((n_sems,)),
                           pltpu.SemaphoreType.DMA((2 + n_t + len(names),))]),
        compiler_params=_params(48),
    )(*big, *[small[n] for n in names])
    return list(outs[:n_t]), dict(zip(names, outs[n_t:n_t + len(names)]))


def _adamw_update(w, g, m, v):
    m = ADAM_B1 * m + (1.0 - ADAM_B1) * g
    v = ADAM_B2 * v + (1.0 - ADAM_B2) * (g * g)
    m_hat = m / (1.0 - ADAM_B1 ** ADAM_STEP)
    v_hat = v / (1.0 - ADAM_B2 ** ADAM_STEP)
    return -ADAM_LR * (m_hat / (jnp.sqrt(v_hat) + ADAM_EPS) + ADAM_WD * w), m, v


def _adamw(w, g, m, v, grid, name):
    n_t = len(w)

    def body(*refs):
        ins, outs = refs[:4 * n_t], refs[4 * n_t:]
        for i in range(n_t):
            w_, g_, m_, v_ = [ins[a * n_t + i][...] for a in range(4)]
            vals = (g_,) + _adamw_update(w_, g_, m_, v_)
            for a in range(4):
                outs[a * n_t + i][...] = vals[a]

    specs = [pl.BlockSpec((a.shape[0] // grid, a.shape[1]), lambda i: (i, 0)) for a in w]
    shapes = [_sds(a.shape) for a in w]
    outs = _call(
        body, name=name, grid=(grid,), in_specs=specs * 4, out_specs=specs * 4, out_shape=shapes * 4,
        compiler_params=_params(40, ("arbitrary",)),
    )(*w, *g, *m, *v)
    return [outs[a * n_t:(a + 1) * n_t] for a in range(4)]


def kernel(x, p, pre_norm_g, w_in, ssm_lam_re, ssm_lam_im, ssm_log_step, ssm_b_re, ssm_b_im, ssm_c_re, ssm_c_im, ssm_d, ssm_w_glu, ssm_b_glu, attn_sinks, w_out, post_norm_g, pl_w_proj, pl_w_gate, pl_b_gate, loss_target, m_pre_norm_g, m_w_in, m_ssm_lam_re, m_ssm_lam_im, m_ssm_log_step, m_ssm_b_re, m_ssm_b_im, m_ssm_c_re, m_ssm_c_im, m_ssm_d, m_ssm_w_glu, m_ssm_b_glu, m_attn_sinks, m_w_out, m_post_norm_g, m_pl_w_proj, m_pl_w_gate, m_pl_b_gate, v_pre_norm_g, v_w_in, v_ssm_lam_re, v_ssm_lam_im, v_ssm_log_step, v_ssm_b_re, v_ssm_b_im, v_ssm_c_re, v_ssm_c_im, v_ssm_d, v_ssm_w_glu, v_ssm_b_glu, v_attn_sinks, v_w_out, v_post_norm_g, v_pl_w_proj, v_pl_w_gate, v_pl_b_gate):
    weights = dict(pre_norm_g=pre_norm_g, w_in=w_in, ssm_lam_re=ssm_lam_re, ssm_lam_im=ssm_lam_im,
                   ssm_log_step=ssm_log_step, ssm_b_re=ssm_b_re, ssm_b_im=ssm_b_im, ssm_c_re=ssm_c_re,
                   ssm_c_im=ssm_c_im, ssm_d=ssm_d, ssm_w_glu=ssm_w_glu, ssm_b_glu=ssm_b_glu, attn_sinks=attn_sinks,
                   w_out=w_out, post_norm_g=post_norm_g, pl_w_proj=pl_w_proj, pl_w_gate=pl_w_gate, pl_b_gate=pl_b_gate)
    m_in = dict(pre_norm_g=m_pre_norm_g, w_in=m_w_in, ssm_lam_re=m_ssm_lam_re, ssm_lam_im=m_ssm_lam_im,
                ssm_log_step=m_ssm_log_step, ssm_b_re=m_ssm_b_re, ssm_b_im=m_ssm_b_im, ssm_c_re=m_ssm_c_re,
                ssm_c_im=m_ssm_c_im, ssm_d=m_ssm_d, ssm_w_glu=m_ssm_w_glu, ssm_b_glu=m_ssm_b_glu,
                attn_sinks=m_attn_sinks, w_out=m_w_out, post_norm_g=m_post_norm_g, pl_w_proj=m_pl_w_proj,
                pl_w_gate=m_pl_w_gate, pl_b_gate=m_pl_b_gate)
    v_in = dict(pre_norm_g=v_pre_norm_g, w_in=v_w_in, ssm_lam_re=v_ssm_lam_re, ssm_lam_im=v_ssm_lam_im,
                ssm_log_step=v_ssm_log_step, ssm_b_re=v_ssm_b_re, ssm_b_im=v_ssm_b_im, ssm_c_re=v_ssm_c_re,
                ssm_c_im=v_ssm_c_im, ssm_d=v_ssm_d, ssm_w_glu=v_ssm_w_glu, ssm_b_glu=v_ssm_b_glu,
                attn_sinks=v_attn_sinks, w_out=v_w_out, post_norm_g=v_post_norm_g, pl_w_proj=v_pl_w_proj,
                pl_w_gate=v_pl_w_gate, pl_b_gate=v_pl_b_gate)

    def two_d(tree):
        return {k: (a.reshape(_SMALL[k]) if k in _SMALL else a[0]) for k, a in tree.items()}

    w2, m2, v2 = two_d(weights), two_d(m_in), two_d(v_in)

    shards = [w2[n].T if n == "w_in" else w2[n] for n in _BIG]
    full = dict(zip(_BIG, _gather_weights([s.astype(BF16) for s in shards])))
    s5_params = tuple(w2[n] for n in ("ssm_lam_re", "ssm_lam_im", "ssm_log_step", "ssm_b_re", "ssm_b_im", "ssm_c_re",
                                      "ssm_c_im"))
    grad_x, loss, grads = _local_step(
        x, p, loss_target, w2["pre_norm_g"], full["w_in"], s5_params, w2["ssm_d"], full["ssm_w_glu"], w2["ssm_b_glu"],
        w2["attn_sinks"], full["w_out"], w2["post_norm_g"], full["pl_w_proj"], full["pl_w_gate"], w2["pl_b_gate"])

    travel = {n: grads[n].reshape(_small_shape(n)) for n in _SMALL}
    g_big, g_small = _exchange_grads([grads[n] for n in _BIG], {**travel, "loss": loss})
    g_big = dict(zip(_BIG, g_big))
    g_big["w_in"] = g_big["w_in"].T
    total_loss = g_small.pop("loss")
    g_small = {n: a.reshape(_SMALL[n]) for n, a in g_small.items()}

    big_out = _adamw([w2[n] for n in _BIG], [g_big[n] for n in _BIG], [m2[n] for n in _BIG], [v2[n] for n in _BIG],
                     8, "adamw_matrices")
    small_names = tuple(_SMALL)
    small_out = _adamw([w2[n] for n in small_names], [g_small[n] for n in small_names], [m2[n] for n in small_names],
                       [v2[n] for n in small_names], 1, "adamw_small")

    results = [{**dict(zip(_BIG, big_part)), **dict(zip(small_names, small_part))}
               for big_part, small_part in zip(big_out, small_out)]
    flat = [r[name].reshape(weights[name].shape) for r in results for name in _WEIGHT_ORDER]
    return (total_loss.reshape(()), grad_x, *flat)
```

```python
import math

import jax
import jax.numpy as jnp
from jax import lax
from jax.experimental import pallas as pl
from jax.experimental.pallas import tpu as pltpu
from jax.experimental.pallas import tpu_sc as plsc

F32 = jnp.float32
BF16 = jnp.bfloat16

D_MODEL = 1024
D_SSM = 512
D_ATTN = 512
SSM_GROUPS = 32
SSM_GROUP_CH = 16
SSM_STATE = 64
SSM_LANES = SSM_GROUPS * SSM_STATE
HEAD_DIM = 64
N_HEADS = 8
KV_HEADS = 2
Q_PER_KV = 4
WINDOW = 128
BLOCK = 128
D_PLE = 256
D_IN = 2304
EPS = 1e-6
ATTN_SCALE = 1.0 / math.sqrt(HEAD_DIM)

ADAM_LR = 0.001
ADAM_B1 = 0.9
ADAM_B2 = 0.999
ADAM_EPS = 1e-08
ADAM_WD = 0.01
ADAM_STEP = 10

N_CHIPS = 4
LANES = 128
SCAN_CHUNKS = 8
SCAN_TILE_STEPS = 16
SCAN_LANE_CHUNK = 512
MIB = 2 ** 20
MESH = pl.DeviceIdType.MESH


def _dot(a, b):
    return jnp.dot(a, b, preferred_element_type=F32)


def _dot_nt(a, b):
    return lax.dot_general(a, b, (((1,), (1,)), ((), ())), preferred_element_type=F32)


def _dot_tn(a, b):
    return lax.dot_general(a, b, (((0,), (0,)), ((), ())), preferred_element_type=F32)


def _params(vmem_mib, semantics=None):
    kw = dict(vmem_limit_bytes=vmem_mib * MIB)
    if semantics is not None:
        kw["dimension_semantics"] = semantics
    return pltpu.CompilerParams(**kw)


def _full(shape):
    nd = len(shape)
    return pl.BlockSpec(shape, lambda *_: (0,) * nd)


def _rows(tm, width):
    return pl.BlockSpec((tm, width), lambda i: (i, 0))


def _sds(shape, dtype=F32):
    return pltpu.HBM(shape, dtype)


def _call(body, **kw):
    fn = pl.pallas_call(body, **kw)
    return lambda *args: fn(*[pltpu.with_memory_space_constraint(a, pltpu.HBM) for a in args])


def _silu(z):
    return z * jax.nn.sigmoid(z)


def _in_proj(x2d, g1, w_in_t):
    rows = x2d.shape[0]
    tm = 512

    def body(x_ref, g_ref, w_ref, u_ref, zs_ref, q_ref, k_ref, v_ref, za_ref):
        x = x_ref[...]
        r = lax.rsqrt(jnp.mean(x * x, axis=-1, keepdims=True) + EPS)
        hn = (x * r * g_ref[...]).astype(BF16)

        def proj(a, b):
            return _dot_nt(hn, w_ref[a:b, :])

        u_ref[...] = proj(0, 512)
        zs_ref[...] = proj(512, 1024)
        q_ref[...] = proj(1024, 1536).astype(BF16)
        k_ref[...] = proj(1536, 1664).astype(BF16)
        v_ref[...] = proj(1664, 1792).astype(BF16)
        za_ref[...] = proj(1792, 2304)

    return _call(
        body, name="in_proj", grid=(rows // tm,),
        in_specs=[_rows(tm, D_MODEL), _full((1, D_MODEL)), _full((D_IN, D_MODEL))],
        out_specs=[_rows(tm, 512), _rows(tm, 512), _rows(tm, 512), _rows(tm, 128), _rows(tm, 128), _rows(tm, 512)],
        out_shape=[_sds((rows, 512)), _sds((rows, 512)), _sds((rows, 512), BF16), _sds((rows, 128), BF16),
                   _sds((rows, 128), BF16), _sds((rows, 512))],
        compiler_params=_params(40, ("arbitrary",)),
    )(x2d, g1, w_in_t)


def _in_proj_bwd(x2d, dh1, g1, w_in_t, du, dzs, dq, dk, dv, dza):
    rows = x2d.shape[0]
    tm = 256
    pieces = ((0, 512), (512, 1024), (1024, 1536), (1536, 1664), (1664, 1792), (1792, 2304))

    def body(x_ref, dh1_ref, g_ref, w_ref, du_ref, dzs_ref, dq_ref, dk_ref, dv_ref, dza_ref, gx_ref, dw_ref, dg_ref):
        @pl.when(pl.program_id(0) == 0)
        def _():
            dw_ref[...] = jnp.zeros_like(dw_ref)
            dg_ref[...] = jnp.zeros_like(dg_ref)

        x = x_ref[...]
        g = g_ref[...]
        r = lax.rsqrt(jnp.mean(x * x, axis=-1, keepdims=True) + EPS)
        xr = x * r
        hn = (xr * g).astype(BF16)
        dhn = jnp.zeros((tm, D_MODEL), F32)
        for (a, b), ref in zip(pieces, (du_ref, dzs_ref, dq_ref, dk_ref, dv_ref, dza_ref)):
            piece = ref[...].astype(BF16)
            dhn = dhn + _dot(piece, w_ref[a:b, :])
            dw_ref[a:b, :] += _dot_tn(piece, hn)
        dg_ref[...] += jnp.sum(dhn * xr, axis=0, keepdims=True)
        a_ = dhn * g
        gx_ref[...] = dh1_ref[...] + r * a_ - xr * (r * jnp.mean(a_ * xr, axis=-1, keepdims=True))

    return _call(
        body, name="in_proj_bwd", grid=(rows // tm,),
        in_specs=[_rows(tm, D_MODEL), _rows(tm, D_MODEL), _full((1, D_MODEL)), _full((D_IN, D_MODEL)),
                  _rows(tm, 512), _rows(tm, 512), _rows(tm, 512), _rows(tm, 128), _rows(tm, 128), _rows(tm, 512)],
        out_specs=[_rows(tm, D_MODEL), _full((D_IN, D_MODEL)), _full((1, D_MODEL))],
        out_shape=[_sds((rows, D_MODEL)), _sds((D_IN, D_MODEL)), _sds((1, D_MODEL))],
        compiler_params=_params(52, ("arbitrary",)),
    )(x2d, dh1, g1, w_in_t, du, dzs, dq, dk, dv, dza)


def _iota(shape, axis):
    return lax.broadcasted_iota(jnp.int32, shape, axis)


def _exact_dot(a, b):
    return jnp.dot(a, b, precision=lax.Precision.HIGHEST, preferred_element_type=F32)


_HALF_GROUPS = SSM_GROUPS // 2
_N_SHIFT = SSM_STATE.bit_length() - 1
_P_SHIFT = SSM_GROUP_CH.bit_length() - 1


def _s5_operands(lam_re, lam_im, log_step, b_re, b_im, c_re, c_im):
    g, n, p = SSM_GROUPS, SSM_STATE, SSM_GROUP_CH
    gn, hn_, hp = g * n, _HALF_GROUPS * n, _HALF_GROUPS * p
    eye_g = _iota((g, g), 0) == _iota((g, g), 1)
    step = jnp.sum(jnp.where(eye_g, jnp.exp(log_step), 0.0), axis=1, keepdims=True)
    a_re = lam_re * step
    a_im = lam_im * step
    mag = jnp.exp(a_re)
    lbar_re = mag * jnp.cos(a_im)
    lbar_im = mag * jnp.sin(a_im)
    n_re = lbar_re - 1.0
    den = lam_re * lam_re + lam_im * lam_im
    f_re = (n_re * lam_re + lbar_im * lam_im) / den
    f_im = (lbar_im * lam_re - n_re * lam_im) / den

    spread_n = (_iota((n, gn), 0) == (_iota((n, gn), 1) & (n - 1))).astype(F32)
    own_g = _iota((g, gn), 0) == (_iota((g, gn), 1) >> _N_SHIFT)

    def to_row(a):
        return jnp.sum(jnp.where(own_g, _exact_dot(a, spread_n), 0.0), axis=0, keepdims=True)

    pick_g = ((_iota((gn, g), 0) >> _N_SHIFT) == _iota((gn, g), 1)).astype(F32)
    own_n = (_iota((gn, n), 0) & (n - 1)) == _iota((gn, n), 1)

    def to_col(a):
        return jnp.sum(jnp.where(own_n, _exact_dot(pick_g, a), 0.0), axis=1, keepdims=True)

    fc_re, fc_im = to_col(f_re), to_col(f_im)
    bbar_re = fc_re * b_re - fc_im * b_im
    bbar_im = fc_re * b_im + fc_im * b_re

    tile_p = (_iota((p, hp), 0) == (_iota((p, hp), 1) & (p - 1))).astype(F32)
    block_b = (_iota((hn_, hp), 0) >> _N_SHIFT) == (_iota((hn_, hp), 1) >> _P_SHIFT)
    tile_n = (_iota((n, hn_), 0) == (_iota((n, hn_), 1) & (n - 1))).astype(F32)
    block_c = (_iota((hp, hn_), 0) >> _P_SHIFT) == (_iota((hp, hn_), 1) >> _N_SHIFT)

    def embed_b(a, hf):
        return jnp.where(block_b, _exact_dot(a[hf * hn_:(hf + 1) * hn_], tile_p), 0.0)

    def embed_c(a, hf):
        return jnp.where(block_c, _exact_dot(a[hf * hp:(hf + 1) * hp], tile_n), 0.0)

    return (to_row(lbar_re), to_row(lbar_im), embed_b(bbar_re, 0), embed_b(bbar_re, 1), embed_b(bbar_im, 0),
            embed_b(bbar_im, 1), embed_c(c_re, 0), embed_c(c_re, 1), embed_c(c_im, 0), embed_c(c_im, 1))


_S5_PARAM_SHAPES = ((SSM_GROUPS, SSM_STATE), (SSM_GROUPS, SSM_STATE), (1, SSM_GROUPS),
                    (SSM_LANES, SSM_GROUP_CH), (SSM_LANES, SSM_GROUP_CH), (D_SSM, SSM_STATE), (D_SSM, SSM_STATE))
_BT_SHAPE = (2, _HALF_GROUPS * SSM_STATE, _HALF_GROUPS * SSM_GROUP_CH)
_CM_SHAPE = (2, _HALF_GROUPS * SSM_GROUP_CH, _HALF_GROUPS * SSM_STATE)
_S5_OPERAND_SHAPES = ((1, SSM_LANES), (1, SSM_LANES), _BT_SHAPE, _BT_SHAPE, _CM_SHAPE, _CM_SHAPE)


def _s5_params_fwd(*params):
    def body(*refs):
        ins, (lre_ref, lim_ref, btre_ref, btim_ref, cmre_ref, cmim_ref) = refs[:7], refs[7:]
        vals = _s5_operands(*[r[...] for r in ins])
        lre_ref[...] = vals[0]
        lim_ref[...] = vals[1]
        for ref, pair in zip((btre_ref, btim_ref, cmre_ref, cmim_ref), (vals[2:4], vals[4:6], vals[6:8], vals[8:10])):
            ref[0] = pair[0].astype(BF16)
            ref[1] = pair[1].astype(BF16)

    dtypes = (F32, F32, BF16, BF16, BF16, BF16)
    return _call(
        body, name="s5_params_fwd",
        in_specs=[_full(s) for s in _S5_PARAM_SHAPES], out_specs=[_full(s) for s in _S5_OPERAND_SHAPES],
        out_shape=[_sds(s, d) for s, d in zip(_S5_OPERAND_SHAPES, dtypes)], compiler_params=_params(32),
    )(*params)


def _s5_params_bwd(params, cotangents):
    def body(*refs):
        ins, (dlre, dlim, dbtre, dbtim, dcmre, dcmim), outs = refs[:7], refs[7:13], refs[13:]
        _, vjp = jax.vjp(_s5_operands, *[r[...] for r in ins])
        cts = (dlre[...], dlim[...], dbtre[0], dbtre[1], dbtim[0], dbtim[1], dcmre[0], dcmre[1], dcmim[0], dcmim[1])
        for ref, val in zip(outs, vjp(cts)):
            ref[...] = val

    return _call(
        body, name="s5_params_bwd",
        in_specs=[_full(s) for s in _S5_PARAM_SHAPES + _S5_OPERAND_SHAPES],
        out_specs=[_full(s) for s in _S5_PARAM_SHAPES],
        out_shape=[_sds(s) for s in _S5_PARAM_SHAPES], compiler_params=_params(48),
    )(*params, *cotangents)


def _scan_geometry(n_seq, seq):
    slab = n_seq * SCAN_CHUNKS
    steps = seq // SCAN_CHUNKS
    tile_rows = slab * SCAN_TILE_STEPS
    n_tiles = steps // SCAN_TILE_STEPS
    return slab, steps, tile_rows, n_tiles


def _per_lane_block(body, name, a):
    spec = pl.BlockSpec((a.shape[0], LANES), lambda j: (0, j))
    return _call(body, name=name, grid=(a.shape[1] // LANES,), in_specs=[spec], out_specs=spec,
                          out_shape=_sds(a.shape), compiler_params=_params(32, ("arbitrary",)))(a)


def _to_scan_order(a, n_seq, seq):
    slab, steps, _, _ = _scan_geometry(n_seq, seq)

    def body(a_ref, o_ref):
        def step(t, carry):
            o_ref[pl.ds(pl.multiple_of(t * slab, slab), slab), :] = a_ref[pl.ds(t, slab, stride=steps), :]
            return carry

        lax.fori_loop(0, steps, step, 0, unroll=8)

    return _per_lane_block(body, "to_scan_order", a)


def _from_scan_order(a, n_seq, seq):
    slab, steps, _, _ = _scan_geometry(n_seq, seq)

    def body(a_ref, o_ref):
        def step(t, carry):
            o_ref[pl.ds(t, slab, stride=steps), :] = a_ref[pl.ds(pl.multiple_of(t * slab, slab), slab), :]
            return carry

        lax.fori_loop(0, steps, step, 0, unroll=8)

    return _per_lane_block(body, "from_scan_order", a)


def _complex_power(re, im, n):
    out = None
    while n:
        if n & 1:
            out = (re, im) if out is None else (out[0] * re - out[1] * im, out[0] * im + out[1] * re)
        n >>= 1
        if n:
            re, im = re * re - im * im, 2.0 * re * im
    return out


def _chunk_carry(sum_re, sum_im, carry_re, carry_im, a_re, a_im, n_seq, reverse):
    carry_re[...] = jnp.zeros_like(carry_re)
    carry_im[...] = jnp.zeros_like(carry_im)
    for s in range(n_seq):
        order = range(SCAN_CHUNKS - 2, -1, -1) if reverse else range(1, SCAN_CHUNKS)
        for c in order:
            r = s * SCAN_CHUNKS + c
            p = r + 1 if reverse else r - 1
            p_re, p_im = carry_re[p:p + 1, :], carry_im[p:p + 1, :]
            carry_re[r:r + 1, :] = a_re * p_re - a_im * p_im + sum_re[p:p + 1, :]
            carry_im[r:r + 1, :] = a_re * p_im + a_im * p_re + sum_im[p:p + 1, :]


def _s5_scan_fwd(u_scan, bt_re, bt_im, cm_re, cm_im, lbar_re, lbar_im, d_row, n_seq, seq):
    slab, steps, tile_rows, n_tiles = _scan_geometry(n_seq, seq)
    rows = u_scan.shape[0]

    def body(u_ref, bre_ref, bim_ref, cre_ref, cim_ref, lre_ref, lim_ref, d_ref, y_ref, hre_ref, him_ref,
             st_re, st_im, h0_re, h0_im, buf_re, buf_im):
        second = pl.program_id(0) == 1
        i = pl.program_id(1)

        @pl.when(jnp.logical_and(i == 0, jnp.logical_not(second)))
        def _():
            st_re[...] = jnp.zeros_like(st_re)
            st_im[...] = jnp.zeros_like(st_im)

        u = u_ref[...]
        ub = u.astype(BF16)
        for hf in range(2):
            cols = slice(hf * 1024, (hf + 1) * 1024)
            buf_re[:, cols] = _dot_nt(ub[:, hf * 256:(hf + 1) * 256], bre_ref[hf])
            buf_im[:, cols] = _dot_nt(ub[:, hf * 256:(hf + 1) * 256], bim_ref[hf])

        for lc in range(SSM_LANES // SCAN_LANE_CHUNK):
            cols = slice(lc * SCAN_LANE_CHUNK, (lc + 1) * SCAN_LANE_CHUNK)
            l_re = jnp.broadcast_to(lre_ref[:, cols], (slab, SCAN_LANE_CHUNK))
            l_im = jnp.broadcast_to(lim_ref[:, cols], (slab, SCAN_LANE_CHUNK))

            def scan_tile(keep_states):
                def step(t, carry):
                    s_re, s_im = carry
                    r0 = pl.multiple_of(t * slab, slab)
                    n_re = l_re * s_re - l_im * s_im + buf_re[pl.ds(r0, slab), cols]
                    n_im = l_re * s_im + l_im * s_re + buf_im[pl.ds(r0, slab), cols]
                    if keep_states:
                        buf_re[pl.ds(r0, slab), cols] = n_re
                        buf_im[pl.ds(r0, slab), cols] = n_im
                    return n_re, n_im

                s_re, s_im = lax.fori_loop(0, SCAN_TILE_STEPS, step, (st_re[:, cols], st_im[:, cols]), unroll=True)
                st_re[:, cols] = s_re
                st_im[:, cols] = s_im

            pl.when(jnp.logical_not(second))(lambda: scan_tile(False))
            pl.when(second)(lambda: scan_tile(True))

        @pl.when(jnp.logical_and(i == n_tiles - 1, jnp.logical_not(second)))
        def _():
            a_re, a_im = _complex_power(lre_ref[...], lim_ref[...], steps)
            _chunk_carry(st_re, st_im, h0_re, h0_im, a_re, a_im, n_seq, reverse=False)
            st_re[...] = h0_re[...]
            st_im[...] = h0_im[...]

        @pl.when(second)
        def _():
            h_re = buf_re[...].astype(BF16)
            h_im = buf_im[...].astype(BF16)
            hre_ref[...] = h_re
            him_ref[...] = h_im
            for hf in range(2):
                cols = slice(hf * 1024, (hf + 1) * 1024)
                ycols = slice(hf * 256, (hf + 1) * 256)
                y_ref[:, ycols] = (_dot_nt(h_re[:, cols], cre_ref[hf]) - _dot_nt(h_im[:, cols], cim_ref[hf])
                                   + d_ref[:, ycols] * u[:, ycols])

    tile = lambda w: pl.BlockSpec((tile_rows, w), lambda p, i: (i, 0))
    out_tile = lambda w: pl.BlockSpec((tile_rows, w), lambda p, i: (i * p, 0))
    bt, cm = _full(_BT_SHAPE), _full(_CM_SHAPE)
    return _call(
        body, name="s5_scan_fwd", grid=(2, n_tiles),
        in_specs=[tile(512), bt, bt, cm, cm, _full((1, SSM_LANES)), _full((1, SSM_LANES)), _full((1, 512))],
        out_specs=[out_tile(512), out_tile(SSM_LANES), out_tile(SSM_LANES)],
        out_shape=[_sds((rows, 512)), _sds((rows, SSM_LANES), BF16), _sds((rows, SSM_LANES), BF16)],
        scratch_shapes=[pltpu.VMEM((slab, SSM_LANES), F32)] * 4 + [pltpu.VMEM((tile_rows, SSM_LANES), F32)] * 2,
        compiler_params=_params(40, ("arbitrary", "arbitrary")),
    )(u_scan, bt_re, bt_im, cm_re, cm_im, lbar_re, lbar_im, d_row)


def _s5_scan_bwd(dy_scan, u_scan, h_re, h_im, bt_re, bt_im, cm_re, cm_im, lbar_re, lbar_im, d_row, n_seq, seq):
    slab, steps, tile_rows, n_tiles = _scan_geometry(n_seq, seq)
    rows = u_scan.shape[0]

    def body(dy_ref, u_ref, hre_ref, him_ref, bre_ref, bim_ref, cre_ref, cim_ref, lre_ref, lim_ref, d_ref,
             du_ref, dbre_ref, dbim_ref, dcre_ref, dcim_ref, dlre_ref, dlim_ref, dd_ref,
             st_re, st_im, g0_re, g0_im, acc_re, acc_im, buf_re, buf_im):
        second = pl.program_id(0) == 1
        i = pl.program_id(1)

        @pl.when(jnp.logical_and(i == 0, jnp.logical_not(second)))
        def _():
            st_re[...] = jnp.zeros_like(st_re)
            st_im[...] = jnp.zeros_like(st_im)
            acc_re[...] = jnp.zeros_like(acc_re)
            acc_im[...] = jnp.zeros_like(acc_im)
            for ref in (dbre_ref, dbim_ref, dcre_ref, dcim_ref, dd_ref):
                ref[...] = jnp.zeros_like(ref)

        dy = dy_ref[...]
        dyb = dy.astype(BF16)
        for hf in range(2):
            cols = slice(hf * 1024, (hf + 1) * 1024)
            buf_re[:, cols] = _dot(dyb[:, hf * 256:(hf + 1) * 256], cre_ref[hf])
            buf_im[:, cols] = -_dot(dyb[:, hf * 256:(hf + 1) * 256], cim_ref[hf])

        for lc in range(SSM_LANES // SCAN_LANE_CHUNK):
            cols = slice(lc * SCAN_LANE_CHUNK, (lc + 1) * SCAN_LANE_CHUNK)
            l_re = jnp.broadcast_to(lre_ref[:, cols], (slab, SCAN_LANE_CHUNK))
            l_im = jnp.broadcast_to(lim_ref[:, cols], (slab, SCAN_LANE_CHUNK))

            def advance(r0, s_re, s_im):
                n_re = l_re * s_re + l_im * s_im + buf_re[pl.ds(r0, slab), cols]
                n_im = l_re * s_im - l_im * s_re + buf_im[pl.ds(r0, slab), cols]
                buf_re[pl.ds(r0, slab), cols] = n_re
                buf_im[pl.ds(r0, slab), cols] = n_im
                return n_re, n_im

            def row0(k):
                return pl.multiple_of((SCAN_TILE_STEPS - 1 - k) * slab, slab)

            @pl.when(jnp.logical_not(second))
            def _():
                s_re, s_im = lax.fori_loop(0, SCAN_TILE_STEPS, lambda k, s: advance(row0(k), *s),
                                           (st_re[:, cols], st_im[:, cols]), unroll=True)
                st_re[:, cols] = s_re
                st_im[:, cols] = s_im

            @pl.when(second)
            def _():
                def step(k, carry):
                    s_re, s_im, a_re, a_im = carry
                    r0 = row0(k)
                    hr = hre_ref[pl.ds(r0, slab), cols].astype(F32)
                    hi = him_ref[pl.ds(r0, slab), cols].astype(F32)
                    a_re = a_re + s_re * hr + s_im * hi
                    a_im = a_im + s_im * hr - s_re * hi
                    return advance(r0, s_re, s_im) + (a_re, a_im)

                zero = jnp.zeros((slab, SCAN_LANE_CHUNK), F32)
                s_re, s_im, a_re, a_im = lax.fori_loop(
                    0, SCAN_TILE_STEPS, step, (st_re[:, cols], st_im[:, cols], zero, zero), unroll=True)
                st_re[:, cols] = s_re
                st_im[:, cols] = s_im
                acc_re[:, cols] += a_re
                acc_im[:, cols] += a_im

        @pl.when(jnp.logical_and(i == n_tiles - 1, jnp.logical_not(second)))
        def _():
            p_re, p_im = _complex_power(lre_ref[...], lim_ref[...], steps)
            _chunk_carry(st_re, st_im, g0_re, g0_im, p_re, -p_im, n_seq, reverse=True)
            st_re[...] = g0_re[...]
            st_im[...] = g0_im[...]

        @pl.when(second)
        def _():
            u = u_ref[...]
            ub = u.astype(BF16)
            g_re = buf_re[...].astype(BF16)
            g_im = buf_im[...].astype(BF16)
            dd_ref[...] += jnp.sum(dy * u, axis=0, keepdims=True)
            for hf in range(2):
                cols = slice(hf * 1024, (hf + 1) * 1024)
                ycols = slice(hf * 256, (hf + 1) * 256)
                du_ref[:, ycols] = (_dot(g_re[:, cols], bre_ref[hf]) + _dot(g_im[:, cols], bim_ref[hf])
                                    + d_ref[:, ycols] * dy[:, ycols])
                dbre_ref[hf] += _dot_tn(g_re[:, cols], ub[:, ycols])
                dbim_ref[hf] += _dot_tn(g_im[:, cols], ub[:, ycols])
                dcre_ref[hf] += _dot_tn(dyb[:, ycols], hre_ref[:, cols])
                dcim_ref[hf] -= _dot_tn(dyb[:, ycols], him_ref[:, cols])

        @pl.when(jnp.logical_and(i == n_tiles - 1, second))
        def _():
            dlre_ref[...] = jnp.sum(acc_re[...], axis=0, keepdims=True)
            dlim_ref[...] = jnp.sum(acc_im[...], axis=0, keepdims=True)

    tile = lambda w: pl.BlockSpec((tile_rows, w), lambda p, i: (n_tiles - 1 - i, 0))
    second_tile = lambda w: pl.BlockSpec((tile_rows, w), lambda p, i: (n_tiles - 1 - i * p, 0))
    bt, cm = _full(_BT_SHAPE), _full(_CM_SHAPE)
    row = _full((1, SSM_LANES))
    return _call(
        body, name="s5_scan_bwd", grid=(2, n_tiles),
        in_specs=[tile(512), second_tile(512), second_tile(SSM_LANES), second_tile(SSM_LANES),
                  bt, bt, cm, cm, row, row, _full((1, 512))],
        out_specs=[second_tile(512), bt, bt, cm, cm, row, row, _full((1, 512))],
        out_shape=([_sds((rows, 512))] + [_sds(_BT_SHAPE)] * 2 + [_sds(_CM_SHAPE)] * 2 + [_sds((1, SSM_LANES))] * 2
                   + [_sds((1, 512))]),
        scratch_shapes=[pltpu.VMEM((slab, SSM_LANES), F32)] * 6 + [pltpu.VMEM((tile_rows, SSM_LANES), F32)] * 2,
        compiler_params=_params(48, ("arbitrary", "arbitrary")),
    )(dy_scan, u_scan, h_re, h_im, bt_re, bt_im, cm_re, cm_im, lbar_re, lbar_im, d_row)


def _glu_gate(gl, a, zs):
    return gl * jax.nn.sigmoid(a) * _silu(zs)


def _glu_fwd(y, zs, w_glu, b_glu):
    rows = y.shape[0]
    tm = 512

    def body(y_ref, zs_ref, w_ref, b_ref, o_ref):
        gl = jax.nn.gelu(y_ref[...])
        a = _dot(gl.astype(BF16), w_ref[...]) + b_ref[...]
        o_ref[...] = _glu_gate(gl, a, zs_ref[...]).astype(BF16)

    return _call(
        body, name="glu_fwd", grid=(rows // tm,),
        in_specs=[_rows(tm, 512), _rows(tm, 512), _full((512, 512)), _full((1, 512))],
        out_specs=_rows(tm, 512), out_shape=_sds((rows, 512), BF16),
        compiler_params=_params(32, ("arbitrary",)),
    )(y, zs, w_glu, b_glu)


def _glu_bwd(y, zs, d_out, w_glu, b_glu):
    rows = y.shape[0]
    tm = 512

    def body(y_ref, zs_ref, d_ref, w_ref, b_ref, dy_ref, dzs_ref, dw_ref, db_ref):
        @pl.when(pl.program_id(0) == 0)
        def _():
            dw_ref[...] = jnp.zeros_like(dw_ref)
            db_ref[...] = jnp.zeros_like(db_ref)

        gl, gelu_vjp = jax.vjp(jax.nn.gelu, y_ref[...])
        glb = gl.astype(BF16)
        a = _dot(glb, w_ref[...]) + b_ref[...]
        _, gate_vjp = jax.vjp(_glu_gate, gl, a, zs_ref[...])
        d_gl, d_a, d_zs = gate_vjp(d_ref[...])
        dab = d_a.astype(BF16)
        d_gl = d_gl + _dot_nt(dab, w_ref[...])
        dy_ref[...] = gelu_vjp(d_gl)[0]
        dzs_ref[...] = d_zs.astype(BF16)
        dw_ref[...] += _dot_tn(glb, dab)
        db_ref[...] += jnp.sum(d_a, axis=0, keepdims=True)

    return _call(
        body, name="glu_bwd", grid=(rows // tm,),
        in_specs=[_rows(tm, 512), _rows(tm, 512), _rows(tm, 512), _full((512, 512)), _full((1, 512))],
        out_specs=[_rows(tm, 512), _rows(tm, 512), _full((512, 512)), _full((1, 512))],
        out_shape=[_sds((rows, 512)), _sds((rows, 512), BF16), _sds((512, 512)), _sds((1, 512))],
        compiler_params=_params(32, ("arbitrary",)),
    )(y, zs, d_out, w_glu, b_glu)


_GROUP_ROWS = Q_PER_KV * BLOCK
_BLOCK_SHIFT = BLOCK.bit_length() - 1


def _attn_bias(j):
    row = _iota((_GROUP_ROWS, BLOCK), 0)
    dist_cur = (row & (BLOCK - 1)) - _iota((_GROUP_ROWS, BLOCK), 1)
    dist_prev = dist_cur + BLOCK
    head = row >> _BLOCK_SHIFT
    slope = jnp.zeros((_GROUP_ROWS, BLOCK), F32)
    for g in range(Q_PER_KV):
        slope = jnp.where(head == g, 2.0 ** (-(j * Q_PER_KV + g + 1)), slope)
    bias_cur = jnp.where(dist_cur >= 0, -slope * dist_cur.astype(F32), -jnp.inf)
    bias_prev = jnp.where(dist_prev < WINDOW, -slope * dist_prev.astype(F32), -jnp.inf)
    return bias_cur, bias_prev


_ATTN_BIAS_SCRATCH = pltpu.VMEM((KV_HEADS, 2, _GROUP_ROWS, BLOCK), F32)


def _fill_attn_bias(bias_ref):
    @pl.when(jnp.logical_and(pl.program_id(0) == 0, pl.program_id(1) == 0))
    def _():
        for j in range(KV_HEADS):
            bias_ref[j, 0], bias_ref[j, 1] = _attn_bias(j)


def _stack_heads(x, j):
    heads = range(j * Q_PER_KV, (j + 1) * Q_PER_KV)
    return jnp.concatenate([x[:, h * HEAD_DIM:(h + 1) * HEAD_DIM] for h in heads], axis=0)


def _stack_columns(x, j):
    heads = range(j * Q_PER_KV, (j + 1) * Q_PER_KV)
    return jnp.concatenate([jnp.broadcast_to(x[:, h:h + 1], (BLOCK, 1)) for h in heads], axis=0)


def _attn_fwd(q, k, v, za, sinks, n_seq, seq):
    nb = seq // BLOCK
    rows = q.shape[0]

    def body(q_ref, kc_ref, kp_ref, vc_ref, vp_ref, za_ref, sk_ref, o_ref, ao_ref, lse_ref, bias_ref):
        _fill_attn_bias(bias_ref)
        has_prev = pl.program_id(1) > 0
        q_all = q_ref[...]
        for j in range(KV_HEADS):
            js = slice(j * HEAD_DIM, (j + 1) * HEAD_DIM)
            bias_c, bias_p = bias_ref[j, 0], bias_ref[j, 1]
            q4 = _stack_heads(q_all, j)
            sc = _dot_nt(q4, kc_ref[:, js]) * ATTN_SCALE + bias_c
            sp = _dot_nt(q4, kp_ref[:, js]) * ATTN_SCALE + jnp.where(has_prev, bias_p, -jnp.inf)
            sink = _stack_columns(sk_ref[...], j)
            m = jnp.maximum(jnp.maximum(jnp.max(sc, axis=-1, keepdims=True), jnp.max(sp, axis=-1, keepdims=True)), sink)
            ec = jnp.exp(sc - m)
            ep = jnp.exp(sp - m)
            den = jnp.sum(ec, axis=-1, keepdims=True) + jnp.sum(ep, axis=-1, keepdims=True) + jnp.exp(sink - m)
            o4 = (_dot(ec.astype(BF16), vc_ref[:, js]) + _dot(ep.astype(BF16), vp_ref[:, js])) * (1.0 / den)
            lse4 = m + jnp.log(den)
            for g in range(Q_PER_KV):
                h = j * Q_PER_KV + g
                o_ref[:, h * HEAD_DIM:(h + 1) * HEAD_DIM] = o4[g * BLOCK:(g + 1) * BLOCK]
                lse_ref[:, h:h + 1] = lse4[g * BLOCK:(g + 1) * BLOCK]
        ao_ref[...] = (o_ref[...] * _silu(za_ref[...])).astype(BF16)

    cur = lambda w: pl.BlockSpec((BLOCK, w), lambda b, n: (b * nb + n, 0))
    prev = lambda w: pl.BlockSpec((BLOCK, w), lambda b, n: (b * nb + jnp.maximum(n - 1, 0), 0))
    return _call(
        body, name="attn_fwd", grid=(n_seq, nb),
        in_specs=[cur(512), cur(128), prev(128), cur(128), prev(128), cur(512), _full((1, N_HEADS))],
        out_specs=[cur(512), cur(512), cur(N_HEADS)],
        out_shape=[_sds((rows, 512)), _sds((rows, 512), BF16), _sds((rows, N_HEADS))],
        scratch_shapes=[_ATTN_BIAS_SCRATCH], compiler_params=_params(32, ("arbitrary", "arbitrary")),
    )(q, k, k, v, v, za, sinks)


def _attn_bwd(q, k, v, za, o, lse, d_ao, sinks, n_seq, seq):
    nb = seq // BLOCK
    rows = q.shape[0]

    def body(q_ref, q2_ref, kc_ref, kp_ref, vc_ref, vp_ref, za_ref, za2_ref, o_ref, o2_ref, lse_ref, lse2_ref,
             d_ref, d2_ref, sk_ref, dq_ref, dk_ref, dv_ref, dza_ref, dsk_ref, bias_ref):
        n = pl.program_id(1)
        _fill_attn_bias(bias_ref)

        @pl.when(jnp.logical_and(pl.program_id(0) == 0, n == 0))
        def _():
            dsk_ref[...] = jnp.zeros_like(dsk_ref)

        has_prev = n > 0
        has_next = n + 1 < nb

        o = o_ref[...]
        _, gate_vjp = jax.vjp(lambda o_, z_: o_ * _silu(z_), o, za_ref[...])
        d_o, d_za = gate_vjp(d_ref[...])
        dza_ref[...] = d_za.astype(BF16)
        o2 = o2_ref[...]
        d_o2 = d2_ref[...] * _silu(za2_ref[...])
        q_all, q2_all = q_ref[...], q2_ref[...]
        lse_all, lse2_all = lse_ref[...], lse2_ref[...]

        for j in range(KV_HEADS):
            js = slice(j * HEAD_DIM, (j + 1) * HEAD_DIM)
            kc, kp, vc, vp = kc_ref[:, js], kp_ref[:, js], vc_ref[:, js], vp_ref[:, js]
            bias_c, bias_p = bias_ref[j, 0], bias_ref[j, 1]
            q4 = _stack_heads(q_all, j)
            do4 = _stack_heads(d_o, j)
            do4b = do4.astype(BF16)
            delta = jnp.sum(do4 * _stack_heads(o, j), axis=-1, keepdims=True)
            lse4 = _stack_columns(lse_all, j)
            pc = jnp.exp(_dot_nt(q4, kc) * ATTN_SCALE + bias_c - lse4)
            pp = jnp.exp(_dot_nt(q4, kp) * ATTN_SCALE + jnp.where(has_prev, bias_p, -jnp.inf) - lse4)
            dsc = (pc * (_dot_nt(do4b, vc) - delta)).astype(BF16)
            dsp = (pp * (_dot_nt(do4b, vp) - delta)).astype(BF16)
            dq4 = ((_dot(dsc, kc) + _dot(dsp, kp)) * ATTN_SCALE).astype(BF16)
            sink_loss = jnp.exp(_stack_columns(sk_ref[...], j) - lse4) * delta
            for g in range(Q_PER_KV):
                h = j * Q_PER_KV + g
                dq_ref[:, h * HEAD_DIM:(h + 1) * HEAD_DIM] = dq4[g * BLOCK:(g + 1) * BLOCK]
                dsk_ref[0:1, h:h + 1] -= jnp.sum(sink_loss[g * BLOCK:(g + 1) * BLOCK], axis=0, keepdims=True)
            dk = _dot_tn(dsc, q4)
            dv = _dot_tn(pc.astype(BF16), do4b)
            q4n = _stack_heads(q2_all, j)
            do4n = _stack_heads(d_o2, j)
            do4nb = do4n.astype(BF16)
            delta2 = jnp.sum(do4n * _stack_heads(o2, j), axis=-1, keepdims=True)
            p2 = jnp.exp(_dot_nt(q4n, kc) * ATTN_SCALE + jnp.where(has_next, bias_p, -jnp.inf)
                         - _stack_columns(lse2_all, j))
            ds2 = (p2 * (_dot_nt(do4nb, vc) - delta2)).astype(BF16)
            dk = dk + _dot_tn(ds2, q4n)
            dv = dv + _dot_tn(p2.astype(BF16), do4nb)
            dk_ref[:, js] = (dk * ATTN_SCALE).astype(BF16)
            dv_ref[:, js] = dv.astype(BF16)

    cur = lambda w: pl.BlockSpec((BLOCK, w), lambda b, n: (b * nb + n, 0))
    prev = lambda w: pl.BlockSpec((BLOCK, w), lambda b, n: (b * nb + jnp.maximum(n - 1, 0), 0))
    nxt = lambda w: pl.BlockSpec((BLOCK, w), lambda b, n: (b * nb + jnp.minimum(n + 1, nb - 1), 0))
    return _call(
        body, name="attn_bwd", grid=(n_seq, nb),
        in_specs=[cur(512), nxt(512), cur(128), prev(128), cur(128), prev(128), cur(512), nxt(512),
                  cur(512), nxt(512), cur(N_HEADS), nxt(N_HEADS), cur(512), nxt(512), _full((1, N_HEADS))],
        out_specs=[cur(512), cur(128), cur(128), cur(512), _full((1, N_HEADS))],
        out_shape=[_sds((rows, 512), BF16), _sds((rows, 128), BF16), _sds((rows, 128), BF16),
                   _sds((rows, 512), BF16), _sds((1, N_HEADS))],
        scratch_shapes=[_ATTN_BIAS_SCRATCH], compiler_params=_params(32, ("arbitrary", "arbitrary")),
    )(q, q, k, k, v, v, za, za, o, o, lse, lse, d_ao, d_ao, sinks)


def _tail(ssm_out, attn_out, x2d, p2d, target, w_out, g2, w_gate, b_gate, w_proj):
    rows = x2d.shape[0]
    tm = 256

    def body(so_ref, ao_ref, x_ref, p_ref, t_ref, wo_ref, g2_ref, wg_ref, bg_ref, wp_ref,
             dh1_ref, dso_ref, dao_ref, dwo_ref, dwg_ref, dwp_ref, dbg_ref, dg2_ref, loss_ref):
        @pl.when(pl.program_id(0) == 0)
        def _():
            for ref in (dwo_ref, dwg_ref, dwp_ref, dbg_ref, dg2_ref, loss_ref):
                ref[...] = jnp.zeros_like(ref)

        so = so_ref[...]
        ao = ao_ref[...]
        g2 = g2_ref[...]
        mixed = _dot(so, wo_ref[0:512, :]) + _dot(ao, wo_ref[512:1024, :])
        r = lax.rsqrt(jnp.mean(mixed * mixed, axis=-1, keepdims=True) + EPS)
        mr = mixed * r
        h1 = x_ref[...] + mr * g2
        h1b = h1.astype(BF16)
        gate = jax.nn.sigmoid(_dot(h1b, wg_ref[...]) + bg_ref[...])
        pb = p_ref[...].astype(BF16)
        wp_blocks = [slice(j * D_PLE, (j + 1) * D_PLE) for j in range(N_CHIPS)]
        pp = jnp.concatenate([_dot(pb, wp_ref[blk, :]) for blk in wp_blocks], axis=1)
        err = h1 + gate * pp - t_ref[...]
        loss_ref[...] += 0.5 * jnp.sum(jnp.mean(err * err, axis=-1, keepdims=True), axis=0, keepdims=True)

        dh2 = err * (1.0 / D_MODEL)
        d_glin = dh2 * pp * gate * (1.0 - gate)
        d_glin_b = d_glin.astype(BF16)
        dwg_ref[...] += _dot_tn(h1b, d_glin_b)
        dbg_ref[...] += jnp.sum(d_glin, axis=0, keepdims=True)
        d_pp = (dh2 * gate).astype(BF16)
        for blk in wp_blocks:
            dwp_ref[blk, :] += _dot_tn(pb, d_pp[:, blk])
        dh1 = dh2 + _dot_nt(d_glin_b, wg_ref[...])
        dh1_ref[...] = dh1
        dg2_ref[...] += jnp.sum(dh1 * mr, axis=0, keepdims=True)
        a_ = dh1 * g2
        d_mixed = (r * a_ - mr * (r * jnp.mean(a_ * mr, axis=-1, keepdims=True))).astype(BF16)
        dwo_ref[0:512, :] += _dot_tn(so, d_mixed)
        dwo_ref[512:1024, :] += _dot_tn(ao, d_mixed)
        dso_ref[...] = _dot_nt(d_mixed, wo_ref[0:512, :])
        dao_ref[...] = _dot_nt(d_mixed, wo_ref[512:1024, :])

    return _call(
        body, name="tail_fwd_bwd", grid=(rows // tm,),
        in_specs=[_rows(tm, 512), _rows(tm, 512), _rows(tm, D_MODEL), _rows(tm, D_PLE), _rows(tm, D_MODEL),
                  _full((D_MODEL, D_MODEL)), _full((1, D_MODEL)), _full((D_MODEL, D_MODEL)), _full((1, D_MODEL)),
                  _full((N_CHIPS * D_PLE, D_PLE))],
        out_specs=[_rows(tm, D_MODEL), _rows(tm, 512), _rows(tm, 512), _full((D_MODEL, D_MODEL)),
                   _full((D_MODEL, D_MODEL)), _full((N_CHIPS * D_PLE, D_PLE)), _full((1, D_MODEL)), _full((1, D_MODEL)),
                   _full((1, 1))],
        out_shape=[_sds((rows, D_MODEL)), _sds((rows, 512)), _sds((rows, 512)), _sds((D_MODEL, D_MODEL)),
                   _sds((D_MODEL, D_MODEL)), _sds((N_CHIPS * D_PLE, D_PLE)), _sds((1, D_MODEL)), _sds((1, D_MODEL)),
                   _sds((1, 1))],
        compiler_params=_params(52, ("arbitrary",)),
    )(ssm_out, attn_out, x2d, p2d, target, w_out, g2, w_gate, b_gate, w_proj)


def _local_step(x, p, target, pre_norm_g, w_in_t, s5_params, ssm_d, w_glu, b_glu, sinks, w_out, post_norm_g, w_proj,
                w_gate, b_gate):
    n_seq, seq, _ = x.shape
    rows = n_seq * seq
    x2d = x.reshape(rows, D_MODEL)
    p2d = p.reshape(rows, D_PLE)
    t2d = target.reshape(rows, D_MODEL)

    l_re, l_im, bt_re, bt_im, cm_re, cm_im = _s5_params_fwd(*s5_params)

    u, zs, q, k, v, za = _in_proj(x2d, pre_norm_g, w_in_t)
    u_scan = _to_scan_order(u, n_seq, seq)
    y_scan, h_re, h_im = _s5_scan_fwd(u_scan, bt_re, bt_im, cm_re, cm_im, l_re, l_im, ssm_d, n_seq, seq)
    y = _from_scan_order(y_scan, n_seq, seq)
    ssm_out = _glu_fwd(y, zs, w_glu, b_glu)
    o, attn_out, lse = _attn_fwd(q, k, v, za, sinks, n_seq, seq)

    dh1, d_so, d_ao, d_w_out, d_w_gate, d_w_proj, d_b_gate, d_g2, loss = _tail(
        ssm_out, attn_out, x2d, p2d, t2d, w_out, post_norm_g, w_gate, b_gate, w_proj)

    dq, dk, dv, dza, d_sinks = _attn_bwd(q, k, v, za, o, lse, d_ao, sinks, n_seq, seq)
    dy, dzs, d_w_glu, d_b_glu = _glu_bwd(y, zs, d_so, w_glu, b_glu)
    dy_scan = _to_scan_order(dy, n_seq, seq)
    du_scan, d_bt_re, d_bt_im, d_cm_re, d_cm_im, d_l_re, d_l_im, d_d = _s5_scan_bwd(
        dy_scan, u_scan, h_re, h_im, bt_re, bt_im, cm_re, cm_im, l_re, l_im, ssm_d, n_seq, seq)
    du = _from_scan_order(du_scan, n_seq, seq)
    d_lam_re, d_lam_im, d_log_step, d_b_re, d_b_im, d_c_re, d_c_im = _s5_params_bwd(
        s5_params, (d_l_re, d_l_im, d_bt_re, d_bt_im, d_cm_re, d_cm_im))

    grad_x, d_w_in_t, d_g1 = _in_proj_bwd(x2d, dh1, pre_norm_g, w_in_t, du, dzs, dq, dk, dv, dza)
    grads = dict(
        pre_norm_g=d_g1, w_in=d_w_in_t, ssm_lam_re=d_lam_re, ssm_lam_im=d_lam_im, ssm_log_step=d_log_step,
        ssm_b_re=d_b_re, ssm_b_im=d_b_im, ssm_c_re=d_c_re, ssm_c_im=d_c_im, ssm_d=d_d, ssm_w_glu=d_w_glu,
        ssm_b_glu=d_b_glu, attn_sinks=d_sinks, w_out=d_w_out, post_norm_g=d_g2, pl_w_proj=d_w_proj,
        pl_w_gate=d_w_gate, pl_b_gate=d_b_gate)
    return grad_x.reshape(x.shape), loss, grads


_BIG = ("w_in", "ssm_w_glu", "w_out", "pl_w_proj", "pl_w_gate")
_BIG_SHARD = {"w_in": (D_IN // N_CHIPS, D_MODEL), "ssm_w_glu": (D_SSM // N_CHIPS, D_SSM),
              "w_out": (D_MODEL // N_CHIPS, D_MODEL), "pl_w_proj": (D_PLE, D_MODEL // N_CHIPS),
              "pl_w_gate": (D_MODEL // N_CHIPS, D_MODEL)}
_SMALL = {"pre_norm_g": (1, D_MODEL), "ssm_lam_re": (SSM_GROUPS, SSM_STATE), "ssm_lam_im": (SSM_GROUPS, SSM_STATE),
          "ssm_log_step": (1, SSM_GROUPS), "ssm_b_re": (SSM_LANES, SSM_GROUP_CH), "ssm_b_im": (SSM_LANES, SSM_GROUP_CH),
          "ssm_c_re": (D_SSM, SSM_STATE), "ssm_c_im": (D_SSM, SSM_STATE), "ssm_d": (1, D_SSM), "ssm_b_glu": (1, D_SSM),
          "attn_sinks": (1, N_HEADS), "post_norm_g": (1, D_MODEL), "pl_b_gate": (1, D_MODEL)}
_B_TRAVEL = (SSM_LANES * SSM_GROUP_CH // LANES, LANES)
_VEC_ROWS = ("pre_norm_g", "post_norm_g", "pl_b_gate", "ssm_d", "ssm_b_glu", "attn_sinks", "ssm_log_step", "loss")
_SMALL_GROUPS = (
    ("vec", (8, D_MODEL), tuple((name, r) for r, name in enumerate(_VEC_ROWS))),
    ("lam", (2 * SSM_GROUPS, SSM_STATE), (("ssm_lam_re", 0), ("ssm_lam_im", SSM_GROUPS))),
    ("b", (2 * _B_TRAVEL[0], LANES), (("ssm_b_re", 0), ("ssm_b_im", _B_TRAVEL[0]))),
    ("c", (2 * D_SSM, SSM_STATE), (("ssm_c_re", 0), ("ssm_c_im", D_SSM))),
)
_SMALL_ORDER = tuple(name for _, _, members in _SMALL_GROUPS for name, _ in members)
_WEIGHT_ORDER = ("pre_norm_g", "w_in", "ssm_lam_re", "ssm_lam_im", "ssm_log_step", "ssm_b_re", "ssm_b_im", "ssm_c_re",
                 "ssm_c_im", "ssm_d", "ssm_w_glu", "ssm_b_glu", "attn_sinks", "w_out", "post_norm_g", "pl_w_proj",
                 "pl_w_gate", "pl_b_gate")


def _small_shape(name):
    if name == "loss":
        return (1, 1)
    return _B_TRAVEL if name in ("ssm_b_re", "ssm_b_im") else _SMALL[name]


def _mesh_place():
    x, y, c = lax.axis_index("x"), lax.axis_index("y"), lax.axis_index("c")
    other_chips = ((1 - x, y), (x, 1 - y), (1 - x, 1 - y))
    return x, y, c, other_chips


def _gather_copies(s_refs, g_refs, send_sems, recv_sems, local_sems):
    x, y, c, other_chips = _mesh_place()
    started = []
    for i, (s_ref, g_ref) in enumerate(zip(s_refs, g_refs)):
        rows = s_ref.shape[0]
        half = rows // 2

        def block(chip, g_ref=g_ref, rows=rows, half=half):
            return g_ref.at[pl.ds((2 * chip[0] + chip[1]) * rows + c * half, half), :]

        def copy(k, chip, to, src=None, i=i, block=block):
            return pltpu.make_async_remote_copy(
                src_ref=block(chip) if src is None else src, dst_ref=block(chip), send_sem=send_sems.at[6 * i + k],
                recv_sem=recv_sems.at[6 * i + k], device_id=to, device_id_type=MESH)

        own = pltpu.make_async_copy(s_ref, g_ref.at[pl.ds((2 * x + y) * rows, rows), :], local_sems.at[i])
        own.start()
        first = [copy(k, (x, y), (*chip, c), src=s_ref.at[pl.ds(c * half, half), :])
                 for k, chip in enumerate(other_chips)]
        for cp in first:
            cp.start()
        passed = [copy(3 + k, chip, (x, y, 1 - c)) for k, chip in enumerate(other_chips)]
        started.append((own, first, passed))
    for own, first, passed in started:
        for k in range(3):
            first[k].wait_recv()
            passed[k].start()
    for own, first, passed in started:
        for k in range(3):
            passed[k].wait_recv()
        for cp in first + passed:
            cp.wait_send()
        own.wait()


def _gather_semaphores(n_t):
    return [pltpu.SemaphoreType.DMA((6 * n_t,)), pltpu.SemaphoreType.DMA((6 * n_t,)), pltpu.SemaphoreType.DMA((n_t,))]


def _gather_weights(shards):
    n_t = len(shards)

    def body(*refs):
        _gather_copies(refs[:n_t], refs[n_t:2 * n_t], *refs[2 * n_t + 1:])
        refs[2 * n_t][...] = jnp.zeros_like(refs[2 * n_t])

    any_spec = pl.BlockSpec(memory_space=pl.ANY)
    *full, done = _call(
        body, name="gather_weights", in_specs=[any_spec] * n_t,
        out_specs=[any_spec] * n_t + [pl.BlockSpec(memory_space=pltpu.VMEM)],
        out_shape=[_sds((N_CHIPS * s.shape[0], s.shape[1]), s.dtype) for s in shards]
        + [jax.ShapeDtypeStruct((8, LANES), F32)],
        scratch_shapes=_gather_semaphores(n_t),
    )(*shards)
    return full, done[0, 0]


def _gather_weights_beside(shards):
    n_t = len(shards)
    hbm = pltpu.MemorySpace.HBM
    s_refs = [jax.new_ref(s, memory_space=hbm) for s in shards]
    g_refs = [jax.empty_ref(jax.ShapeDtypeStruct((N_CHIPS * s.shape[0], s.shape[1]), s.dtype), memory_space=hbm)
              for s in shards]

    def launch(send_sems, recv_sems, local_sems):
        x, y, c, other_chips = _mesh_place()
        peers = [(*chip, c) for chip in other_chips] + [(x, y, 1 - c)]
        barrier = pltpu.get_barrier_semaphore()
        for peer in peers:
            pl.semaphore_signal(barrier, inc=1, device_id=peer, device_id_type=MESH)
        pl.semaphore_wait(barrier, len(peers))
        _gather_copies(s_refs, g_refs, send_sems, recv_sems, local_sems)

    pl.kernel(launch, mesh=plsc.ScalarSubcoreMesh(axis_name="sequencer", num_cores=1), name="gather_weights_beside",
              scratch_types=_gather_semaphores(n_t), compiler_params=pltpu.CompilerParams(collective_id=1))()
    return [g[...] for g in g_refs]


def _exchange_grads(big, small):
    n_t = len(big)
    n_g = len(_SMALL_GROUPS)
    names = _SMALL_ORDER
    halves = [(b.shape[0] // N_CHIPS // 2, b.shape[1]) for b in big]
    n_sems = 4 * n_g + 8 * n_t

    def body(*refs):
        pos = 0

        def take(n):
            nonlocal pos
            pos += n
            return refs[pos - n:pos]

        big_refs, small_refs = take(n_t), dict(zip(names, take(len(names))))
        out_refs, small_out_refs, land_refs = take(n_t), dict(zip(names, take(len(names)))), take(n_t)
        ga, gb, pme, send_b, recv_b = take(n_t), take(n_t), take(n_t), take(n_t), take(n_t)
        s_own, s_sib, s_chips, s_pair = take(n_g), take(n_g), take(n_g), take(n_g)
        stage = dict(zip(names, take(len(names))))
        send_sems, recv_sems, local_sems = take(3)
        x, y, c, other_chips = _mesh_place()
        me = 2 * x + y
        sibling = (x, y, 1 - c)
        sem_at = iter(range(n_sems))

        def remote(src, dst, to):
            k = next(sem_at)
            return pltpu.make_async_remote_copy(src_ref=src, dst_ref=dst, send_sem=send_sems.at[k],
                                                recv_sem=recv_sems.at[k], device_id=to, device_id_type=MESH)

        loads = [pltpu.make_async_copy(small_refs[name], stage[name], local_sems.at[2 + n_t + a])
                 for a, name in enumerate(names)]
        for cp in loads:
            cp.start()
        for cp in loads:
            cp.wait()
        small_swaps = []
        for gi, (_, _, members) in enumerate(_SMALL_GROUPS):
            s_own[gi][...] = jnp.zeros_like(s_own[gi])
            for name, r0 in members:
                r, n = _small_shape(name)
                s_own[gi][r0:r0 + r, 0:n] = stage[name][...]
            small_swaps.append(remote(s_own[gi], s_sib[gi], sibling))
            small_swaps[gi].start()
        big_swaps = []
        for i in range(n_t):
            hr = halves[i][0]
            big_swaps.append([])
            for j in range(N_CHIPS):
                src = big_refs[i].at[pl.ds(j * 2 * hr + (1 - c) * hr, hr), :]
                big_swaps[i].append(remote(src, land_refs[i].at[j], sibling))
                big_swaps[i][j].start()
        small_sends = []
        for gi in range(n_g):
            small_swaps[gi].wait_recv()
            s_pair[gi][...] = s_own[gi][...] + s_sib[gi][...]
            small_sends.append([remote(s_pair[gi], s_chips[gi].at[k], (*chip, c)) for k, chip in enumerate(other_chips)])
            for cp in small_sends[gi]:
                cp.start()

        def pair_sum(i, j, dst):
            hr = halves[i][0]
            a = pltpu.make_async_copy(big_refs[i].at[pl.ds(j * 2 * hr + c * hr, hr), :], ga[i], local_sems.at[0])
            b = pltpu.make_async_copy(land_refs[i].at[j], gb[i], local_sems.at[1])
            a.start()
            b.start()
            a.wait()
            b.wait()
            dst[...] = (ga[i][...] + gb[i][...]).astype(dst.dtype)

        big_sends = []
        for i in range(n_t):
            for j in range(N_CHIPS):
                big_swaps[i][j].wait_recv()
            big_sends.append([])
            for k, chip in enumerate(other_chips):
                pair_sum(i, 2 * chip[0] + chip[1], send_b[i].at[k])
                big_sends[i].append(remote(send_b[i].at[k], recv_b[i].at[k], (*chip, c)))
                big_sends[i][k].start()
        last_swaps, keeps = [], []
        for i in range(n_t):
            hr = halves[i][0]
            pair_sum(i, me, pme[i])
            for k in range(3):
                big_sends[i][k].wait_recv()
            pme[i][...] = ((pme[i][...] + recv_b[i][0].astype(F32)) + recv_b[i][1].astype(F32)) + recv_b[i][2].astype(F32)
            mine = out_refs[i].at[pl.ds(c * hr, hr), :]
            keeps.append(pltpu.make_async_copy(pme[i], mine, local_sems.at[2 + i]))
            keeps[i].start()
            last_swaps.append(remote(pme[i], mine, sibling))
            last_swaps[i].start()

        for gi, (_, _, members) in enumerate(_SMALL_GROUPS):
            for k in range(3):
                small_sends[gi][k].wait_recv()
            total = None
            for j in range(N_CHIPS):
                rel = jnp.bitwise_xor(j, me)
                term = jnp.where(rel == 0, s_pair[gi][...], jnp.where(
                    rel == 2, s_chips[gi][0], jnp.where(rel == 1, s_chips[gi][1], s_chips[gi][2])))
                total = term if total is None else total + term
            s_sib[gi][...] = total
            for name, r0 in members:
                r, n = _small_shape(name)
                stage[name][...] = s_sib[gi][r0:r0 + r, 0:n]
        stores = [pltpu.make_async_copy(stage[name], small_out_refs[name], local_sems.at[2 + n_t + a])
                  for a, name in enumerate(names)]
        for cp in stores:
            cp.start()

        for i in range(n_t):
            last_swaps[i].wait_recv()
            keeps[i].wait()
        for cp in stores:
            cp.wait()
        for cp in (small_swaps + [cp for group in big_swaps + small_sends + big_sends for cp in group] + last_swaps):
            cp.wait_send()

    any_spec = pl.BlockSpec(memory_space=pl.ANY)
    small_shapes = [_sds(_small_shape(n)) for n in names]
    group_shapes = [shape for _, shape, _ in _SMALL_GROUPS]
    per_matrix = lambda dtype, lead=(): [pltpu.VMEM(lead + h, dtype) for h in halves]
    outs = _call(
        body, name="exchange_grads",
        in_specs=[any_spec] * (n_t + len(names)),
        out_specs=[any_spec] * (2 * n_t + len(names)),
        out_shape=([_sds((b.shape[0] // N_CHIPS, b.shape[1])) for b in big] + small_shapes
                   + [_sds((N_CHIPS,) + h) for h in halves]),
        scratch_shapes=(per_matrix(F32) + per_matrix(F32) + per_matrix(F32) + per_matrix(BF16, (3,)) + per_matrix(BF16, (3,))
                        + [pltpu.VMEM(s, F32) for s in group_shapes] * 2 + [pltpu.VMEM((3,) + s, F32) for s in group_shapes]
                        + [pltpu.VMEM(s, F32) for s in group_shapes]
                        + [pltpu.VMEM(_small_shape(n), F32) for n in names]
                        + [pltpu.SemaphoreType.DMA((n_sems,)), pltpu.SemaphoreType.DMA((n_sems,)),
                           pltpu.SemaphoreType.DMA((2 + n_t + len(names),))]),
        compiler_params=_params(48),
    )(*big, *[small[n] for n in names])
    return list(outs[:n_t]), dict(zip(names, outs[n_t:n_t + len(names)]))


def _adamw_update(w, g, m, v):
    m = ADAM_B1 * m + (1.0 - ADAM_B1) * g
    v = ADAM_B2 * v + (1.0 - ADAM_B2) * (g * g)
    m_hat = m / (1.0 - ADAM_B1 ** ADAM_STEP)
    v_hat = v / (1.0 - ADAM_B2 ** ADAM_STEP)
    return -ADAM_LR * (m_hat / (jnp.sqrt(v_hat) + ADAM_EPS) + ADAM_WD * w), m, v


def _adamw(w, g, m, v, grid, name):
    n_t = len(w)

    def body(*refs):
        ins, outs = refs[:4 * n_t], refs[4 * n_t:]
        for i in range(n_t):
            w_, g_, m_, v_ = [ins[a * n_t + i][...] for a in range(4)]
            vals = (g_,) + _adamw_update(w_, g_, m_, v_)
            for a in range(4):
                outs[a * n_t + i][...] = vals[a]

    specs = [pl.BlockSpec((a.shape[0] // grid, a.shape[1]), lambda i: (i, 0)) for a in w]
    shapes = [_sds(a.shape) for a in w]
    outs = _call(
        body, name=name, grid=(grid,), in_specs=specs * 4, out_specs=specs * 4, out_shape=shapes * 4,
        compiler_params=_params(40, ("arbitrary",)),
    )(*w, *g, *m, *v)
    return [outs[a * n_t:(a + 1) * n_t] for a in range(4)]


def kernel(x, p, pre_norm_g, w_in, ssm_lam_re, ssm_lam_im, ssm_log_step, ssm_b_re, ssm_b_im, ssm_c_re, ssm_c_im, ssm_d, ssm_w_glu, ssm_b_glu, attn_sinks, w_out, post_norm_g, pl_w_proj, pl_w_gate, pl_b_gate, loss_target, m_pre_norm_g, m_w_in, m_ssm_lam_re, m_ssm_lam_im, m_ssm_log_step, m_ssm_b_re, m_ssm_b_im, m_ssm_c_re, m_ssm_c_im, m_ssm_d, m_ssm_w_glu, m_ssm_b_glu, m_attn_sinks, m_w_out, m_post_norm_g, m_pl_w_proj, m_pl_w_gate, m_pl_b_gate, v_pre_norm_g, v_w_in, v_ssm_lam_re, v_ssm_lam_im, v_ssm_log_step, v_ssm_b_re, v_ssm_b_im, v_ssm_c_re, v_ssm_c_im, v_ssm_d, v_ssm_w_glu, v_ssm_b_glu, v_attn_sinks, v_w_out, v_post_norm_g, v_pl_w_proj, v_pl_w_gate, v_pl_b_gate):
    weights = dict(pre_norm_g=pre_norm_g, w_in=w_in, ssm_lam_re=ssm_lam_re, ssm_lam_im=ssm_lam_im,
                   ssm_log_step=ssm_log_step, ssm_b_re=ssm_b_re, ssm_b_im=ssm_b_im, ssm_c_re=ssm_c_re,
                   ssm_c_im=ssm_c_im, ssm_d=ssm_d, ssm_w_glu=ssm_w_glu, ssm_b_glu=ssm_b_glu, attn_sinks=attn_sinks,
                   w_out=w_out, post_norm_g=post_norm_g, pl_w_proj=pl_w_proj, pl_w_gate=pl_w_gate, pl_b_gate=pl_b_gate)
    m_in = dict(pre_norm_g=m_pre_norm_g, w_in=m_w_in, ssm_lam_re=m_ssm_lam_re, ssm_lam_im=m_ssm_lam_im,
                ssm_log_step=m_ssm_log_step, ssm_b_re=m_ssm_b_re, ssm_b_im=m_ssm_b_im, ssm_c_re=m_ssm_c_re,
                ssm_c_im=m_ssm_c_im, ssm_d=m_ssm_d, ssm_w_glu=m_ssm_w_glu, ssm_b_glu=m_ssm_b_glu,
                attn_sinks=m_attn_sinks, w_out=m_w_out, post_norm_g=m_post_norm_g, pl_w_proj=m_pl_w_proj,
                pl_w_gate=m_pl_w_gate, pl_b_gate=m_pl_b_gate)
    v_in = dict(pre_norm_g=v_pre_norm_g, w_in=v_w_in, ssm_lam_re=v_ssm_lam_re, ssm_lam_im=v_ssm_lam_im,
                ssm_log_step=v_ssm_log_step, ssm_b_re=v_ssm_b_re, ssm_b_im=v_ssm_b_im, ssm_c_re=v_ssm_c_re,
                ssm_c_im=v_ssm_c_im, ssm_d=v_ssm_d, ssm_w_glu=v_ssm_w_glu, ssm_b_glu=v_ssm_b_glu,
                attn_sinks=v_attn_sinks, w_out=v_w_out, post_norm_g=v_post_norm_g, pl_w_proj=v_pl_w_proj,
                pl_w_gate=v_pl_w_gate, pl_b_gate=v_pl_b_gate)

    def two_d(tree):
        return {k: (a.reshape(_SMALL[k]) if k in _SMALL else a[0]) for k, a in tree.items()}

    w2, m2, v2 = two_d(weights), two_d(m_in), two_d(v_in)

    (w_in_full,), gathered = _gather_weights([w2["w_in"].T.astype(BF16)])
    rest = _gather_weights_beside([(w2[n] + gathered).astype(BF16) for n in _BIG[1:]])
    full = dict(zip(_BIG, [w_in_full] + rest))
    s5_params = tuple(w2[n] for n in ("ssm_lam_re", "ssm_lam_im", "ssm_log_step", "ssm_b_re", "ssm_b_im", "ssm_c_re",
                                      "ssm_c_im"))
    grad_x, loss, grads = _local_step(
        x, p, loss_target, w2["pre_norm_g"], full["w_in"], s5_params, w2["ssm_d"], full["ssm_w_glu"], w2["ssm_b_glu"],
        w2["attn_sinks"], full["w_out"], w2["post_norm_g"], full["pl_w_proj"], full["pl_w_gate"], w2["pl_b_gate"])

    travel = {n: grads[n].reshape(_small_shape(n)) for n in _SMALL}
    g_big, g_small = _exchange_grads([grads[n] for n in _BIG], {**travel, "loss": loss})
    g_big = dict(zip(_BIG, g_big))
    g_big["w_in"] = g_big["w_in"].T
    total_loss = g_small.pop("loss")
    g_small = {n: a.reshape(_SMALL[n]) for n, a in g_small.items()}

    big_out = _adamw([w2[n] for n in _BIG], [g_big[n] for n in _BIG], [m2[n] for n in _BIG], [v2[n] for n in _BIG],
                     8, "adamw_matrices")
    small_names = tuple(_SMALL)
    small_out = _adamw([w2[n] for n in small_names], [g_small[n] for n in small_names], [m2[n] for n in small_names],
                       [v2[n] for n in small_names], 1, "adamw_small")

    results = [{**dict(zip(_BIG, big_part)), **dict(zip(small_names, small_part))}
               for big_part, small_part in zip(big_out, small_out)]
    flat = [r[name].reshape(weights[name].shape) for r in results for name in _WEIGHT_ORDER]
    return (total_loss.reshape(()), grad_x, *flat)
```

```python
import math

import jax
import jax.numpy as jnp
from jax import lax
from jax.experimental import pallas as pl
from jax.experimental.pallas import tpu as pltpu
from jax.experimental.pallas import tpu_sc as plsc

F32 = jnp.float32
BF16 = jnp.bfloat16

D_MODEL = 1024
D_SSM = 512
D_ATTN = 512
SSM_GROUPS = 32
SSM_GROUP_CH = 16
SSM_STATE = 64
SSM_LANES = SSM_GROUPS * SSM_STATE
HEAD_DIM = 64
N_HEADS = 8
KV_HEADS = 2
Q_PER_KV = 4
WINDOW = 128
BLOCK = 128
D_PLE = 256
D_IN = 2304
EPS = 1e-6
ATTN_SCALE = 1.0 / math.sqrt(HEAD_DIM)

ADAM_LR = 0.001
ADAM_B1 = 0.9
ADAM_B2 = 0.999
ADAM_EPS = 1e-08
ADAM_WD = 0.01
ADAM_STEP = 10

N_CHIPS = 4
LANES = 128
SCAN_CHUNKS = 8
SCAN_TILE_STEPS = 16
SCAN_LANE_CHUNK = 512
MIB = 2 ** 20
MESH = pl.DeviceIdType.MESH


def _dot(a, b):
    return jnp.dot(a, b, preferred_element_type=F32)


def _dot_nt(a, b):
    return lax.dot_general(a, b, (((1,), (1,)), ((), ())), preferred_element_type=F32)


def _dot_tn(a, b):
    return lax.dot_general(a, b, (((0,), (0,)), ((), ())), preferred_element_type=F32)


def _params(vmem_mib, semantics=None):
    kw = dict(vmem_limit_bytes=vmem_mib * MIB)
    if semantics is not None:
        kw["dimension_semantics"] = semantics
    return pltpu.CompilerParams(**kw)


def _full(shape):
    nd = len(shape)
    return pl.BlockSpec(shape, lambda *_: (0,) * nd, pipeline_mode=pl.Buffered(1))


def _rows(tm, width):
    return pl.BlockSpec((tm, width), lambda i: (i, 0))


def _sds(shape, dtype=F32):
    return pltpu.HBM(shape, dtype)


def _call(body, **kw):
    fn = pl.pallas_call(body, **kw)
    return lambda *args: fn(*[pltpu.with_memory_space_constraint(a, pltpu.HBM) for a in args])


def _silu(z):
    return z * jax.nn.sigmoid(z)


def _in_proj(x2d, g1, w_in_t):
    rows = x2d.shape[0]
    tm = 512

    def body(x_ref, g_ref, w_ref, u_ref, zs_ref, q_ref, k_ref, v_ref, za_ref):
        x = x_ref[...]
        r = lax.rsqrt(jnp.mean(x * x, axis=-1, keepdims=True) + EPS)
        hn = (x * r * g_ref[...]).astype(BF16)

        def proj(a, b):
            return _dot_nt(hn, w_ref[a:b, :])

        u_ref[...] = proj(0, 512)
        zs_ref[...] = proj(512, 1024)
        q_ref[...] = proj(1024, 1536).astype(BF16)
        k_ref[...] = proj(1536, 1664).astype(BF16)
        v_ref[...] = proj(1664, 1792).astype(BF16)
        za_ref[...] = proj(1792, 2304)

    return _call(
        body, name="in_proj", grid=(rows // tm,),
        in_specs=[_rows(tm, D_MODEL), _full((1, D_MODEL)), _full((D_IN, D_MODEL))],
        out_specs=[_rows(tm, 512), _rows(tm, 512), _rows(tm, 512), _rows(tm, 128), _rows(tm, 128), _rows(tm, 512)],
        out_shape=[_sds((rows, 512)), _sds((rows, 512)), _sds((rows, 512), BF16), _sds((rows, 128), BF16),
                   _sds((rows, 128), BF16), _sds((rows, 512))],
        compiler_params=_params(40, ("arbitrary",)),
    )(x2d, g1, w_in_t)


def _in_proj_bwd(x2d, dh1, g1, w_in_t, du, dzs, dq, dk, dv, dza):
    rows = x2d.shape[0]
    tm = 512
    pieces = ((0, 512), (512, 1024), (1024, 1536), (1536, 1664), (1664, 1792), (1792, 2304))

    def body(x_ref, dh1_ref, g_ref, w_ref, du_ref, dzs_ref, dq_ref, dk_ref, dv_ref, dza_ref, gx_ref, dw_ref, dg_ref):
        @pl.when(pl.program_id(0) == 0)
        def _():
            dw_ref[...] = jnp.zeros_like(dw_ref)
            dg_ref[...] = jnp.zeros_like(dg_ref)

        x = x_ref[...]
        g = g_ref[...]
        r = lax.rsqrt(jnp.mean(x * x, axis=-1, keepdims=True) + EPS)
        xr = x * r
        hn = (xr * g).astype(BF16)
        dhn = jnp.zeros((tm, D_MODEL), F32)
        for (a, b), ref in zip(pieces, (du_ref, dzs_ref, dq_ref, dk_ref, dv_ref, dza_ref)):
            piece = ref[...].astype(BF16)
            dhn = dhn + _dot(piece, w_ref[a:b, :])
            dw_ref[a:b, :] += _dot_tn(piece, hn)
        dg_ref[...] += jnp.sum(dhn * xr, axis=0, keepdims=True)
        a_ = dhn * g
        gx_ref[...] = dh1_ref[...] + r * a_ - xr * (r * jnp.mean(a_ * xr, axis=-1, keepdims=True))

    return _call(
        body, name="in_proj_bwd", grid=(rows // tm,),
        in_specs=[_rows(tm, D_MODEL), _rows(tm, D_MODEL), _full((1, D_MODEL)), _full((D_IN, D_MODEL)),
                  _rows(tm, 512), _rows(tm, 512), _rows(tm, 512), _rows(tm, 128), _rows(tm, 128), _rows(tm, 512)],
        out_specs=[_rows(tm, D_MODEL), _full((D_IN, D_MODEL)), _full((1, D_MODEL))],
        out_shape=[_sds((rows, D_MODEL)), _sds((D_IN, D_MODEL)), _sds((1, D_MODEL))],
        compiler_params=_params(52, ("arbitrary",)),
    )(x2d, dh1, g1, w_in_t, du, dzs, dq, dk, dv, dza)


def _iota(shape, axis):
    return lax.broadcasted_iota(jnp.int32, shape, axis)


def _exact_dot(a, b):
    return jnp.dot(a, b, precision=lax.Precision.HIGHEST, preferred_element_type=F32)


_HALF_GROUPS = SSM_GROUPS // 2
_N_SHIFT = SSM_STATE.bit_length() - 1
_P_SHIFT = SSM_GROUP_CH.bit_length() - 1


def _s5_operands(lam_re, lam_im, log_step, b_re, b_im, c_re, c_im):
    g, n, p = SSM_GROUPS, SSM_STATE, SSM_GROUP_CH
    gn, gp, hn_, hp = g * n, g * p, _HALF_GROUPS * n, _HALF_GROUPS * p
    eye_g = _iota((g, g), 0) == _iota((g, g), 1)
    step = jnp.sum(jnp.where(eye_g, jnp.exp(log_step), 0.0), axis=1, keepdims=True)
    a_re = lam_re * step
    a_im = lam_im * step
    mag = jnp.exp(a_re)
    lbar_re = mag * jnp.cos(a_im)
    lbar_im = mag * jnp.sin(a_im)
    n_re = lbar_re - 1.0
    den = lam_re * lam_re + lam_im * lam_im
    f_re = (n_re * lam_re + lbar_im * lam_im) / den
    f_im = (lbar_im * lam_re - n_re * lam_im) / den

    spread_n = (_iota((n, gn), 0) == (_iota((n, gn), 1) & (n - 1))).astype(F32)
    own_g = _iota((g, gn), 0) == (_iota((g, gn), 1) >> _N_SHIFT)

    def to_row(a):
        return jnp.sum(jnp.where(own_g, _exact_dot(a, spread_n), 0.0), axis=0, keepdims=True)

    per_group = ((_iota((gp, g), 0) >> _P_SHIFT) == _iota((gp, g), 1)).astype(F32)
    fx_re, fx_im = _exact_dot(per_group, f_re), _exact_dot(per_group, f_im)
    bbar_re = fx_re * b_re - fx_im * b_im
    bbar_im = fx_re * b_im + fx_im * b_re

    tile_n = (_iota((n, hn_), 0) == (_iota((n, hn_), 1) & (n - 1))).astype(F32)
    same_group = (_iota((hp, hn_), 0) >> _P_SHIFT) == (_iota((hp, hn_), 1) >> _N_SHIFT)

    def embed(a, hf):
        return jnp.where(same_group, _exact_dot(a[hf * hp:(hf + 1) * hp], tile_n), 0.0)

    return (to_row(lbar_re), to_row(lbar_im), embed(bbar_re, 0), embed(bbar_re, 1), embed(bbar_im, 0),
            embed(bbar_im, 1), embed(c_re, 0), embed(c_re, 1), embed(c_im, 0), embed(c_im, 1))


_S5_PARAM_SHAPES = ((SSM_GROUPS, SSM_STATE), (SSM_GROUPS, SSM_STATE), (1, SSM_GROUPS),
                    (D_SSM, SSM_STATE), (D_SSM, SSM_STATE), (D_SSM, SSM_STATE), (D_SSM, SSM_STATE))
_CM_SHAPE = (2, _HALF_GROUPS * SSM_GROUP_CH, _HALF_GROUPS * SSM_STATE)
_S5_OPERAND_SHAPES = ((1, SSM_LANES), (1, SSM_LANES), _CM_SHAPE, _CM_SHAPE, _CM_SHAPE, _CM_SHAPE)


def _s5_params_fwd(*params):
    def body(*refs):
        ins, (lre_ref, lim_ref, btre_ref, btim_ref, cmre_ref, cmim_ref) = refs[:7], refs[7:]
        vals = _s5_operands(*[r[...] for r in ins])
        lre_ref[...] = vals[0]
        lim_ref[...] = vals[1]
        for ref, pair in zip((btre_ref, btim_ref, cmre_ref, cmim_ref), (vals[2:4], vals[4:6], vals[6:8], vals[8:10])):
            ref[0] = pair[0].astype(BF16)
            ref[1] = pair[1].astype(BF16)

    dtypes = (F32, F32, BF16, BF16, BF16, BF16)
    return _call(
        body, name="s5_params_fwd",
        in_specs=[_full(s) for s in _S5_PARAM_SHAPES], out_specs=[_full(s) for s in _S5_OPERAND_SHAPES],
        out_shape=[_sds(s, d) for s, d in zip(_S5_OPERAND_SHAPES, dtypes)], compiler_params=_params(32),
    )(*params)


def _s5_params_bwd(params, cotangents):
    def body(*refs):
        ins, (dlre, dlim, dbtre, dbtim, dcmre, dcmim), outs = refs[:7], refs[7:13], refs[13:]
        _, vjp = jax.vjp(_s5_operands, *[r[...] for r in ins])
        cts = (dlre[...], dlim[...], dbtre[0], dbtre[1], dbtim[0], dbtim[1], dcmre[0], dcmre[1], dcmim[0], dcmim[1])
        for ref, val in zip(outs, vjp(cts)):
            ref[...] = val

    return _call(
        body, name="s5_params_bwd",
        in_specs=[_full(s) for s in _S5_PARAM_SHAPES + _S5_OPERAND_SHAPES],
        out_specs=[_full(s) for s in _S5_PARAM_SHAPES],
        out_shape=[_sds(s) for s in _S5_PARAM_SHAPES], compiler_params=_params(48),
    )(*params, *cotangents)


def _scan_geometry(n_seq, seq):
    slab = n_seq * SCAN_CHUNKS
    steps = seq // SCAN_CHUNKS
    tile_rows = slab * SCAN_TILE_STEPS
    n_tiles = steps // SCAN_TILE_STEPS
    return slab, steps, tile_rows, n_tiles


def _per_lane_block(body, name, a):
    spec = pl.BlockSpec((a.shape[0], LANES), lambda j: (0, j))
    return _call(body, name=name, grid=(a.shape[1] // LANES,), in_specs=[spec], out_specs=spec,
                          out_shape=_sds(a.shape), compiler_params=_params(32, ("arbitrary",)))(a)


def _to_scan_order(a, n_seq, seq):
    slab, steps, _, _ = _scan_geometry(n_seq, seq)

    def body(a_ref, o_ref):
        def step(t, carry):
            o_ref[pl.ds(pl.multiple_of(t * slab, slab), slab), :] = a_ref[pl.ds(t, slab, stride=steps), :]
            return carry

        lax.fori_loop(0, steps, step, 0, unroll=8)

    return _per_lane_block(body, "to_scan_order", a)


def _from_scan_order(a, n_seq, seq):
    slab, steps, _, _ = _scan_geometry(n_seq, seq)

    def body(a_ref, o_ref):
        def step(t, carry):
            o_ref[pl.ds(t, slab, stride=steps), :] = a_ref[pl.ds(pl.multiple_of(t * slab, slab), slab), :]
            return carry

        lax.fori_loop(0, steps, step, 0, unroll=8)

    return _per_lane_block(body, "from_scan_order", a)


def _complex_power(re, im, n):
    out = None
    while n:
        if n & 1:
            out = (re, im) if out is None else (out[0] * re - out[1] * im, out[0] * im + out[1] * re)
        n >>= 1
        if n:
            re, im = re * re - im * im, 2.0 * re * im
    return out


def _chunk_carry(sum_re, sum_im, carry_re, carry_im, a_re, a_im, n_seq, reverse):
    carry_re[...] = jnp.zeros_like(carry_re)
    carry_im[...] = jnp.zeros_like(carry_im)
    for s in range(n_seq):
        order = range(SCAN_CHUNKS - 2, -1, -1) if reverse else range(1, SCAN_CHUNKS)
        for c in order:
            r = s * SCAN_CHUNKS + c
            p = r + 1 if reverse else r - 1
            p_re, p_im = carry_re[p:p + 1, :], carry_im[p:p + 1, :]
            carry_re[r:r + 1, :] = a_re * p_re - a_im * p_im + sum_re[p:p + 1, :]
            carry_im[r:r + 1, :] = a_re * p_im + a_im * p_re + sum_im[p:p + 1, :]


def _s5_scan_fwd(u_scan, bt_re, bt_im, cm_re, cm_im, lbar_re, lbar_im, d_row, n_seq, seq):
    slab, steps, tile_rows, n_tiles = _scan_geometry(n_seq, seq)
    rows = u_scan.shape[0]

    def body(u_ref, bre_ref, bim_ref, cre_ref, cim_ref, lre_ref, lim_ref, d_ref, y_ref, hre_ref, him_ref,
             st_re, st_im, h0_re, h0_im, buf_re, buf_im):
        second = pl.program_id(0) == 1
        i = pl.program_id(1)

        @pl.when(jnp.logical_and(i == 0, jnp.logical_not(second)))
        def _():
            st_re[...] = jnp.zeros_like(st_re)
            st_im[...] = jnp.zeros_like(st_im)

        u = u_ref[...]
        ub = u.astype(BF16)
        for hf in range(2):
            cols = slice(hf * 1024, (hf + 1) * 1024)
            buf_re[:, cols] = _dot(ub[:, hf * 256:(hf + 1) * 256], bre_ref[hf])
            buf_im[:, cols] = _dot(ub[:, hf * 256:(hf + 1) * 256], bim_ref[hf])

        for lc in range(SSM_LANES // SCAN_LANE_CHUNK):
            cols = slice(lc * SCAN_LANE_CHUNK, (lc + 1) * SCAN_LANE_CHUNK)
            l_re = jnp.broadcast_to(lre_ref[:, cols], (slab, SCAN_LANE_CHUNK))
            l_im = jnp.broadcast_to(lim_ref[:, cols], (slab, SCAN_LANE_CHUNK))

            def scan_tile(keep_states):
                def step(t, carry):
                    s_re, s_im = carry
                    r0 = pl.multiple_of(t * slab, slab)
                    n_re = l_re * s_re - l_im * s_im + buf_re[pl.ds(r0, slab), cols]
                    n_im = l_re * s_im + l_im * s_re + buf_im[pl.ds(r0, slab), cols]
                    if keep_states:
                        buf_re[pl.ds(r0, slab), cols] = n_re
                        buf_im[pl.ds(r0, slab), cols] = n_im
                    return n_re, n_im

                s_re, s_im = lax.fori_loop(0, SCAN_TILE_STEPS, step, (st_re[:, cols], st_im[:, cols]), unroll=True)
                st_re[:, cols] = s_re
                st_im[:, cols] = s_im

            pl.when(jnp.logical_not(second))(lambda: scan_tile(False))
            pl.when(second)(lambda: scan_tile(True))

        @pl.when(jnp.logical_and(i == n_tiles - 1, jnp.logical_not(second)))
        def _():
            a_re, a_im = _complex_power(lre_ref[...], lim_ref[...], steps)
            _chunk_carry(st_re, st_im, h0_re, h0_im, a_re, a_im, n_seq, reverse=False)
            st_re[...] = h0_re[...]
            st_im[...] = h0_im[...]

        @pl.when(second)
        def _():
            h_re = buf_re[...].astype(BF16)
            h_im = buf_im[...].astype(BF16)
            hre_ref[...] = h_re
            him_ref[...] = h_im
            for hf in range(2):
                cols = slice(hf * 1024, (hf + 1) * 1024)
                ycols = slice(hf * 256, (hf + 1) * 256)
                y_ref[:, ycols] = (_dot_nt(h_re[:, cols], cre_ref[hf]) - _dot_nt(h_im[:, cols], cim_ref[hf])
                                   + d_ref[:, ycols] * u[:, ycols])

    tile = lambda w: pl.BlockSpec((tile_rows, w), lambda p, i: (i, 0))
    out_tile = lambda w: pl.BlockSpec((tile_rows, w), lambda p, i: (i * p, 0))
    cm = _full(_CM_SHAPE)
    return _call(
        body, name="s5_scan_fwd", grid=(2, n_tiles),
        in_specs=[tile(512), cm, cm, cm, cm, _full((1, SSM_LANES)), _full((1, SSM_LANES)), _full((1, 512))],
        out_specs=[out_tile(512), out_tile(SSM_LANES), out_tile(SSM_LANES)],
        out_shape=[_sds((rows, 512)), _sds((rows, SSM_LANES), BF16), _sds((rows, SSM_LANES), BF16)],
        scratch_shapes=[pltpu.VMEM((slab, SSM_LANES), F32)] * 4 + [pltpu.VMEM((tile_rows, SSM_LANES), F32)] * 2,
        compiler_params=_params(40, ("arbitrary", "arbitrary")),
    )(u_scan, bt_re, bt_im, cm_re, cm_im, lbar_re, lbar_im, d_row)


def _s5_scan_bwd(dy_scan, u_scan, h_re, h_im, bt_re, bt_im, cm_re, cm_im, lbar_re, lbar_im, d_row, n_seq, seq):
    slab, steps, tile_rows, n_tiles = _scan_geometry(n_seq, seq)
    rows = u_scan.shape[0]

    def body(dy_ref, u_ref, hre_ref, him_ref, bre_ref, bim_ref, cre_ref, cim_ref, lre_ref, lim_ref, d_ref,
             du_ref, dbre_ref, dbim_ref, dcre_ref, dcim_ref, dlre_ref, dlim_ref, dd_ref,
             st_re, st_im, g0_re, g0_im, acc_re, acc_im, buf_re, buf_im):
        second = pl.program_id(0) == 1
        i = pl.program_id(1)

        @pl.when(jnp.logical_and(i == 0, jnp.logical_not(second)))
        def _():
            st_re[...] = jnp.zeros_like(st_re)
            st_im[...] = jnp.zeros_like(st_im)
            acc_re[...] = jnp.zeros_like(acc_re)
            acc_im[...] = jnp.zeros_like(acc_im)
            for ref in (dbre_ref, dbim_ref, dcre_ref, dcim_ref, dd_ref):
                ref[...] = jnp.zeros_like(ref)

        dy = dy_ref[...]
        dyb = dy.astype(BF16)
        for hf in range(2):
            cols = slice(hf * 1024, (hf + 1) * 1024)
            buf_re[:, cols] = _dot(dyb[:, hf * 256:(hf + 1) * 256], cre_ref[hf])
            buf_im[:, cols] = -_dot(dyb[:, hf * 256:(hf + 1) * 256], cim_ref[hf])

        for lc in range(SSM_LANES // SCAN_LANE_CHUNK):
            cols = slice(lc * SCAN_LANE_CHUNK, (lc + 1) * SCAN_LANE_CHUNK)
            l_re = jnp.broadcast_to(lre_ref[:, cols], (slab, SCAN_LANE_CHUNK))
            l_im = jnp.broadcast_to(lim_ref[:, cols], (slab, SCAN_LANE_CHUNK))

            def advance(r0, s_re, s_im):
                n_re = l_re * s_re + l_im * s_im + buf_re[pl.ds(r0, slab), cols]
                n_im = l_re * s_im - l_im * s_re + buf_im[pl.ds(r0, slab), cols]
                buf_re[pl.ds(r0, slab), cols] = n_re
                buf_im[pl.ds(r0, slab), cols] = n_im
                return n_re, n_im

            def row0(k):
                return pl.multiple_of((SCAN_TILE_STEPS - 1 - k) * slab, slab)

            @pl.when(jnp.logical_not(second))
            def _():
                s_re, s_im = lax.fori_loop(0, SCAN_TILE_STEPS, lambda k, s: advance(row0(k), *s),
                                           (st_re[:, cols], st_im[:, cols]), unroll=True)
                st_re[:, cols] = s_re
                st_im[:, cols] = s_im

            @pl.when(second)
            def _():
                def step(k, carry):
                    s_re, s_im, a_re, a_im = carry
                    r0 = row0(k)
                    hr = hre_ref[pl.ds(r0, slab), cols].astype(F32)
                    hi = him_ref[pl.ds(r0, slab), cols].astype(F32)
                    a_re = a_re + s_re * hr + s_im * hi
                    a_im = a_im + s_im * hr - s_re * hi
                    return advance(r0, s_re, s_im) + (a_re, a_im)

                zero = jnp.zeros((slab, SCAN_LANE_CHUNK), F32)
                s_re, s_im, a_re, a_im = lax.fori_loop(
                    0, SCAN_TILE_STEPS, step, (st_re[:, cols], st_im[:, cols], zero, zero), unroll=True)
                st_re[:, cols] = s_re
                st_im[:, cols] = s_im
                acc_re[:, cols] += a_re
                acc_im[:, cols] += a_im

        @pl.when(jnp.logical_and(i == n_tiles - 1, jnp.logical_not(second)))
        def _():
            p_re, p_im = _complex_power(lre_ref[...], lim_ref[...], steps)
            _chunk_carry(st_re, st_im, g0_re, g0_im, p_re, -p_im, n_seq, reverse=True)
            st_re[...] = g0_re[...]
            st_im[...] = g0_im[...]

        @pl.when(second)
        def _():
            u = u_ref[...]
            ub = u.astype(BF16)
            g_re = buf_re[...].astype(BF16)
            g_im = buf_im[...].astype(BF16)
            dd_ref[...] += jnp.sum(dy * u, axis=0, keepdims=True)
            for hf in range(2):
                cols = slice(hf * 1024, (hf + 1) * 1024)
                ycols = slice(hf * 256, (hf + 1) * 256)
                du_ref[:, ycols] = (_dot_nt(g_re[:, cols], bre_ref[hf]) + _dot_nt(g_im[:, cols], bim_ref[hf])
                                    + d_ref[:, ycols] * dy[:, ycols])
                dbre_ref[hf] += _dot_tn(ub[:, ycols], g_re[:, cols])
                dbim_ref[hf] += _dot_tn(ub[:, ycols], g_im[:, cols])
                dcre_ref[hf] += _dot_tn(dyb[:, ycols], hre_ref[:, cols])
                dcim_ref[hf] -= _dot_tn(dyb[:, ycols], him_ref[:, cols])

        @pl.when(jnp.logical_and(i == n_tiles - 1, second))
        def _():
            dlre_ref[...] = jnp.sum(acc_re[...], axis=0, keepdims=True)
            dlim_ref[...] = jnp.sum(acc_im[...], axis=0, keepdims=True)

    tile = lambda w: pl.BlockSpec((tile_rows, w), lambda p, i: (n_tiles - 1 - i, 0))
    second_tile = lambda w: pl.BlockSpec((tile_rows, w), lambda p, i: (n_tiles - 1 - i * p, 0))
    cm = _full(_CM_SHAPE)
    row = _full((1, SSM_LANES))
    return _call(
        body, name="s5_scan_bwd", grid=(2, n_tiles),
        in_specs=[tile(512), second_tile(512), second_tile(SSM_LANES), second_tile(SSM_LANES),
                  cm, cm, cm, cm, row, row, _full((1, 512))],
        out_specs=[second_tile(512), cm, cm, cm, cm, row, row, _full((1, 512))],
        out_shape=([_sds((rows, 512))] + [_sds(_CM_SHAPE)] * 4 + [_sds((1, SSM_LANES))] * 2
                   + [_sds((1, 512))]),
        scratch_shapes=[pltpu.VMEM((slab, SSM_LANES), F32)] * 6 + [pltpu.VMEM((tile_rows, SSM_LANES), F32)] * 2,
        compiler_params=_params(48, ("arbitrary", "arbitrary")),
    )(dy_scan, u_scan, h_re, h_im, bt_re, bt_im, cm_re, cm_im, lbar_re, lbar_im, d_row)


def _glu_gate(gl, a, zs):
    return gl * jax.nn.sigmoid(a) * _silu(zs)


def _glu_fwd(y, zs, w_glu, b_glu):
    rows = y.shape[0]
    tm = 512

    def body(y_ref, zs_ref, w_ref, b_ref, o_ref):
        gl = jax.nn.gelu(y_ref[...])
        a = _dot(gl.astype(BF16), w_ref[...]) + b_ref[...]
        o_ref[...] = _glu_gate(gl, a, zs_ref[...]).astype(BF16)

    return _call(
        body, name="glu_fwd", grid=(rows // tm,),
        in_specs=[_rows(tm, 512), _rows(tm, 512), _full((512, 512)), _full((1, 512))],
        out_specs=_rows(tm, 512), out_shape=_sds((rows, 512), BF16),
        compiler_params=_params(32, ("arbitrary",)),
    )(y, zs, w_glu, b_glu)


def _glu_bwd(y, zs, d_out, w_glu, b_glu):
    rows = y.shape[0]
    tm = 512

    def body(y_ref, zs_ref, d_ref, w_ref, b_ref, dy_ref, dzs_ref, dw_ref, db_ref):
        @pl.when(pl.program_id(0) == 0)
        def _():
            dw_ref[...] = jnp.zeros_like(dw_ref)
            db_ref[...] = jnp.zeros_like(db_ref)

        gl, gelu_vjp = jax.vjp(jax.nn.gelu, y_ref[...])
        glb = gl.astype(BF16)
        a = _dot(glb, w_ref[...]) + b_ref[...]
        _, gate_vjp = jax.vjp(_glu_gate, gl, a, zs_ref[...])
        d_gl, d_a, d_zs = gate_vjp(d_ref[...])
        dab = d_a.astype(BF16)
        d_gl = d_gl + _dot_nt(dab, w_ref[...])
        dy_ref[...] = gelu_vjp(d_gl)[0]
        dzs_ref[...] = d_zs.astype(BF16)
        dw_ref[...] += _dot_tn(glb, dab)
        db_ref[...] += jnp.sum(d_a, axis=0, keepdims=True)

    return _call(
        body, name="glu_bwd", grid=(rows // tm,),
        in_specs=[_rows(tm, 512), _rows(tm, 512), _rows(tm, 512), _full((512, 512)), _full((1, 512))],
        out_specs=[_rows(tm, 512), _rows(tm, 512), _full((512, 512)), _full((1, 512))],
        out_shape=[_sds((rows, 512)), _sds((rows, 512), BF16), _sds((512, 512)), _sds((1, 512))],
        compiler_params=_params(32, ("arbitrary",)),
    )(y, zs, d_out, w_glu, b_glu)


_GROUP_ROWS = Q_PER_KV * BLOCK
_BLOCK_SHIFT = BLOCK.bit_length() - 1


def _attn_bias(j):
    row = _iota((_GROUP_ROWS, BLOCK), 0)
    dist_cur = (row & (BLOCK - 1)) - _iota((_GROUP_ROWS, BLOCK), 1)
    dist_prev = dist_cur + BLOCK
    head = row >> _BLOCK_SHIFT
    slope = jnp.zeros((_GROUP_ROWS, BLOCK), F32)
    for g in range(Q_PER_KV):
        slope = jnp.where(head == g, 2.0 ** (-(j * Q_PER_KV + g + 1)), slope)
    bias_cur = jnp.where(dist_cur >= 0, -slope * dist_cur.astype(F32), -jnp.inf)
    bias_prev = jnp.where(dist_prev < WINDOW, -slope * dist_prev.astype(F32), -jnp.inf)
    return bias_cur, bias_prev


_ATTN_BIAS_SCRATCH = pltpu.VMEM((KV_HEADS, 2, _GROUP_ROWS, BLOCK), F32)


def _fill_attn_bias(bias_ref):
    @pl.when(jnp.logical_and(pl.program_id(0) == 0, pl.program_id(1) == 0))
    def _():
        for j in range(KV_HEADS):
            bias_ref[j, 0], bias_ref[j, 1] = _attn_bias(j)


def _stack_heads(x, j):
    heads = range(j * Q_PER_KV, (j + 1) * Q_PER_KV)
    return jnp.concatenate([x[:, h * HEAD_DIM:(h + 1) * HEAD_DIM] for h in heads], axis=0)


def _stack_columns(x, j):
    heads = range(j * Q_PER_KV, (j + 1) * Q_PER_KV)
    return jnp.concatenate([jnp.broadcast_to(x[:, h:h + 1], (BLOCK, 1)) for h in heads], axis=0)


def _attn_fwd(q, k, v, za, sinks, n_seq, seq):
    nb = seq // BLOCK
    rows = q.shape[0]

    def body(q_ref, kc_ref, kp_ref, vc_ref, vp_ref, za_ref, sk_ref, o_ref, ao_ref, lse_ref, bias_ref):
        _fill_attn_bias(bias_ref)
        has_prev = pl.program_id(1) > 0
        q_all = q_ref[...]
        for j in range(KV_HEADS):
            js = slice(j * HEAD_DIM, (j + 1) * HEAD_DIM)
            bias_c, bias_p = bias_ref[j, 0], bias_ref[j, 1]
            q4 = _stack_heads(q_all, j)
            sc = _dot_nt(q4, kc_ref[:, js]) * ATTN_SCALE + bias_c
            sp = _dot_nt(q4, kp_ref[:, js]) * ATTN_SCALE + jnp.where(has_prev, bias_p, -jnp.inf)
            sink = _stack_columns(sk_ref[...], j)
            m = jnp.maximum(jnp.maximum(jnp.max(sc, axis=-1, keepdims=True), jnp.max(sp, axis=-1, keepdims=True)), sink)
            ec = jnp.exp(sc - m)
            ep = jnp.exp(sp - m)
            den = jnp.sum(ec, axis=-1, keepdims=True) + jnp.sum(ep, axis=-1, keepdims=True) + jnp.exp(sink - m)
            o4 = (_dot(ec.astype(BF16), vc_ref[:, js]) + _dot(ep.astype(BF16), vp_ref[:, js])) * (1.0 / den)
            lse4 = m + jnp.log(den)
            for g in range(Q_PER_KV):
                h = j * Q_PER_KV + g
                o_ref[:, h * HEAD_DIM:(h + 1) * HEAD_DIM] = o4[g * BLOCK:(g + 1) * BLOCK]
                lse_ref[:, h:h + 1] = lse4[g * BLOCK:(g + 1) * BLOCK]
        ao_ref[...] = (o_ref[...] * _silu(za_ref[...])).astype(BF16)

    cur = lambda w: pl.BlockSpec((BLOCK, w), lambda b, n: (b * nb + n, 0))
    prev = lambda w: pl.BlockSpec((BLOCK, w), lambda b, n: (b * nb + jnp.maximum(n - 1, 0), 0))
    return _call(
        body, name="attn_fwd", grid=(n_seq, nb),
        in_specs=[cur(512), cur(128), prev(128), cur(128), prev(128), cur(512), _full((1, N_HEADS))],
        out_specs=[cur(512), cur(512), cur(N_HEADS)],
        out_shape=[_sds((rows, 512)), _sds((rows, 512), BF16), _sds((rows, N_HEADS))],
        scratch_shapes=[_ATTN_BIAS_SCRATCH], compiler_params=_params(32, ("arbitrary", "arbitrary")),
    )(q, k, k, v, v, za, sinks)


def _attn_bwd(q, k, v, za, o, lse, d_ao, sinks, n_seq, seq):
    nb = seq // BLOCK
    rows = q.shape[0]

    def body(q_ref, q2_ref, kc_ref, kp_ref, vc_ref, vp_ref, za_ref, za2_ref, o_ref, o2_ref, lse_ref, lse2_ref,
             d_ref, d2_ref, sk_ref, dq_ref, dk_ref, dv_ref, dza_ref, dsk_ref, bias_ref):
        n = pl.program_id(1)
        _fill_attn_bias(bias_ref)

        @pl.when(jnp.logical_and(pl.program_id(0) == 0, n == 0))
        def _():
            dsk_ref[...] = jnp.zeros_like(dsk_ref)

        has_prev = n > 0
        has_next = n + 1 < nb

        o = o_ref[...]
        _, gate_vjp = jax.vjp(lambda o_, z_: o_ * _silu(z_), o, za_ref[...])
        d_o, d_za = gate_vjp(d_ref[...])
        dza_ref[...] = d_za.astype(BF16)
        o2 = o2_ref[...]
        d_o2 = d2_ref[...] * _silu(za2_ref[...])
        q_all, q2_all = q_ref[...], q2_ref[...]
        lse_all, lse2_all = lse_ref[...], lse2_ref[...]

        for j in range(KV_HEADS):
            js = slice(j * HEAD_DIM, (j + 1) * HEAD_DIM)
            kc, kp, vc, vp = kc_ref[:, js], kp_ref[:, js], vc_ref[:, js], vp_ref[:, js]
            bias_c, bias_p = bias_ref[j, 0], bias_ref[j, 1]
            q4 = _stack_heads(q_all, j)
            do4 = _stack_heads(d_o, j)
            do4b = do4.astype(BF16)
            delta = jnp.sum(do4 * _stack_heads(o, j), axis=-1, keepdims=True)
            lse4 = _stack_columns(lse_all, j)
            pc = jnp.exp(_dot_nt(q4, kc) * ATTN_SCALE + bias_c - lse4)
            pp = jnp.exp(_dot_nt(q4, kp) * ATTN_SCALE + jnp.where(has_prev, bias_p, -jnp.inf) - lse4)
            dsc = (pc * (_dot_nt(do4b, vc) - delta)).astype(BF16)
            dsp = (pp * (_dot_nt(do4b, vp) - delta)).astype(BF16)
            dq4 = ((_dot(dsc, kc) + _dot(dsp, kp)) * ATTN_SCALE).astype(BF16)
            sink_loss = jnp.exp(_stack_columns(sk_ref[...], j) - lse4) * delta
            for g in range(Q_PER_KV):
                h = j * Q_PER_KV + g
                dq_ref[:, h * HEAD_DIM:(h + 1) * HEAD_DIM] = dq4[g * BLOCK:(g + 1) * BLOCK]
                dsk_ref[0:1, h:h + 1] -= jnp.sum(sink_loss[g * BLOCK:(g + 1) * BLOCK], axis=0, keepdims=True)
            dk = _dot_tn(dsc, q4)
            dv = _dot_tn(pc.astype(BF16), do4b)
            q4n = _stack_heads(q2_all, j)
            do4n = _stack_heads(d_o2, j)
            do4nb = do4n.astype(BF16)
            delta2 = jnp.sum(do4n * _stack_heads(o2, j), axis=-1, keepdims=True)
            p2 = jnp.exp(_dot_nt(q4n, kc) * ATTN_SCALE + jnp.where(has_next, bias_p, -jnp.inf)
                         - _stack_columns(lse2_all, j))
            ds2 = (p2 * (_dot_nt(do4nb, vc) - delta2)).astype(BF16)
            dk = dk + _dot_tn(ds2, q4n)
            dv = dv + _dot_tn(p2.astype(BF16), do4nb)
            dk_ref[:, js] = (dk * ATTN_SCALE).astype(BF16)
            dv_ref[:, js] = dv.astype(BF16)

    cur = lambda w: pl.BlockSpec((BLOCK, w), lambda b, n: (b * nb + n, 0))
    prev = lambda w: pl.BlockSpec((BLOCK, w), lambda b, n: (b * nb + jnp.maximum(n - 1, 0), 0))
    nxt = lambda w: pl.BlockSpec((BLOCK, w), lambda b, n: (b * nb + jnp.minimum(n + 1, nb - 1), 0))
    return _call(
        body, name="attn_bwd", grid=(n_seq, nb),
        in_specs=[cur(512), nxt(512), cur(128), prev(128), cur(128), prev(128), cur(512), nxt(512),
                  cur(512), nxt(512), cur(N_HEADS), nxt(N_HEADS), cur(512), nxt(512), _full((1, N_HEADS))],
        out_specs=[cur(512), cur(128), cur(128), cur(512), _full((1, N_HEADS))],
        out_shape=[_sds((rows, 512), BF16), _sds((rows, 128), BF16), _sds((rows, 128), BF16),
                   _sds((rows, 512), BF16), _sds((1, N_HEADS))],
        scratch_shapes=[_ATTN_BIAS_SCRATCH], compiler_params=_params(32, ("arbitrary", "arbitrary")),
    )(q, q, k, k, v, v, za, za, o, o, lse, lse, d_ao, d_ao, sinks)


def _tail(ssm_out, attn_out, x2d, p2d, target, w_out, g2, w_gate, b_gate, w_proj):
    rows = x2d.shape[0]
    tm = 512

    def body(so_ref, ao_ref, x_ref, p_ref, t_ref, wo_ref, g2_ref, wg_ref, bg_ref, wp_ref,
             dh1_ref, dso_ref, dao_ref, dwo_ref, dwg_ref, dwp_ref, dbg_ref, dg2_ref, loss_ref):
        @pl.when(pl.program_id(0) == 0)
        def _():
            for ref in (dwo_ref, dwg_ref, dwp_ref, dbg_ref, dg2_ref, loss_ref):
                ref[...] = jnp.zeros_like(ref)

        so = so_ref[...]
        ao = ao_ref[...]
        g2 = g2_ref[...]
        mixed = _dot(so, wo_ref[0:512, :]) + _dot(ao, wo_ref[512:1024, :])
        r = lax.rsqrt(jnp.mean(mixed * mixed, axis=-1, keepdims=True) + EPS)
        mr = mixed * r
        h1 = x_ref[...] + mr * g2
        h1b = h1.astype(BF16)
        gate = jax.nn.sigmoid(_dot(h1b, wg_ref[...]) + bg_ref[...])
        pb = p_ref[...].astype(BF16)
        wp_blocks = [slice(j * D_PLE, (j + 1) * D_PLE) for j in range(N_CHIPS)]
        pp = jnp.concatenate([_dot(pb, wp_ref[blk, :]) for blk in wp_blocks], axis=1)
        err = h1 + gate * pp - t_ref[...]
        loss_ref[...] += 0.5 * jnp.sum(jnp.mean(err * err, axis=-1, keepdims=True), axis=0, keepdims=True)

        dh2 = err * (1.0 / D_MODEL)
        d_glin = dh2 * pp * gate * (1.0 - gate)
        d_glin_b = d_glin.astype(BF16)
        dwg_ref[...] += _dot_tn(h1b, d_glin_b)
        dbg_ref[...] += jnp.sum(d_glin, axis=0, keepdims=True)
        d_pp = (dh2 * gate).astype(BF16)
        for blk in wp_blocks:
            dwp_ref[blk, :] += _dot_tn(pb, d_pp[:, blk])
        dh1 = dh2 + _dot_nt(d_glin_b, wg_ref[...])
        dh1_ref[...] = dh1
        dg2_ref[...] += jnp.sum(dh1 * mr, axis=0, keepdims=True)
        a_ = dh1 * g2
        d_mixed = (r * a_ - mr * (r * jnp.mean(a_ * mr, axis=-1, keepdims=True))).astype(BF16)
        dwo_ref[0:512, :] += _dot_tn(so, d_mixed)
        dwo_ref[512:1024, :] += _dot_tn(ao, d_mixed)
        dso_ref[...] = _dot_nt(d_mixed, wo_ref[0:512, :])
        dao_ref[...] = _dot_nt(d_mixed, wo_ref[512:1024, :])

    return _call(
        body, name="tail_fwd_bwd", grid=(rows // tm,),
        in_specs=[_rows(tm, 512), _rows(tm, 512), _rows(tm, D_MODEL), _rows(tm, D_PLE), _rows(tm, D_MODEL),
                  _full((D_MODEL, D_MODEL)), _full((1, D_MODEL)), _full((D_MODEL, D_MODEL)), _full((1, D_MODEL)),
                  _full((N_CHIPS * D_PLE, D_PLE))],
        out_specs=[_rows(tm, D_MODEL), _rows(tm, 512), _rows(tm, 512), _full((D_MODEL, D_MODEL)),
                   _full((D_MODEL, D_MODEL)), _full((N_CHIPS * D_PLE, D_PLE)), _full((1, D_MODEL)), _full((1, D_MODEL)),
                   _full((1, 1))],
        out_shape=[_sds((rows, D_MODEL)), _sds((rows, 512)), _sds((rows, 512)), _sds((D_MODEL, D_MODEL)),
                   _sds((D_MODEL, D_MODEL)), _sds((N_CHIPS * D_PLE, D_PLE)), _sds((1, D_MODEL)), _sds((1, D_MODEL)),
                   _sds((1, 1))],
        compiler_params=_params(52, ("arbitrary",)),
    )(ssm_out, attn_out, x2d, p2d, target, w_out, g2, w_gate, b_gate, w_proj)


def _local_step(x, p, target, pre_norm_g, w_in_t, s5_params, ssm_d, w_glu, b_glu, sinks, w_out, post_norm_g, w_proj,
                w_gate, b_gate):
    n_seq, seq, _ = x.shape
    rows = n_seq * seq
    x2d = x.reshape(rows, D_MODEL)
    p2d = p.reshape(rows, D_PLE)
    t2d = target.reshape(rows, D_MODEL)

    l_re, l_im, bt_re, bt_im, cm_re, cm_im = _s5_params_fwd(*s5_params)

    u, zs, q, k, v, za = _in_proj(x2d, pre_norm_g, w_in_t)
    u_scan = _to_scan_order(u, n_seq, seq)
    y_scan, h_re, h_im = _s5_scan_fwd(u_scan, bt_re, bt_im, cm_re, cm_im, l_re, l_im, ssm_d, n_seq, seq)
    y = _from_scan_order(y_scan, n_seq, seq)
    ssm_out = _glu_fwd(y, zs, w_glu, b_glu)
    o, attn_out, lse = _attn_fwd(q, k, v, za, sinks, n_seq, seq)

    dh1, d_so, d_ao, d_w_out, d_w_gate, d_w_proj, d_b_gate, d_g2, loss = _tail(
        ssm_out, attn_out, x2d, p2d, t2d, w_out, post_norm_g, w_gate, b_gate, w_proj)

    dq, dk, dv, dza, d_sinks = _attn_bwd(q, k, v, za, o, lse, d_ao, sinks, n_seq, seq)
    dy, dzs, d_w_glu, d_b_glu = _glu_bwd(y, zs, d_so, w_glu, b_glu)
    dy_scan = _to_scan_order(dy, n_seq, seq)
    du_scan, d_bt_re, d_bt_im, d_cm_re, d_cm_im, d_l_re, d_l_im, d_d = _s5_scan_bwd(
        dy_scan, u_scan, h_re, h_im, bt_re, bt_im, cm_re, cm_im, l_re, l_im, ssm_d, n_seq, seq)
    du = _from_scan_order(du_scan, n_seq, seq)
    d_lam_re, d_lam_im, d_log_step, d_b_re, d_b_im, d_c_re, d_c_im = _s5_params_bwd(
        s5_params, (d_l_re, d_l_im, d_bt_re, d_bt_im, d_cm_re, d_cm_im))

    grad_x, d_w_in_t, d_g1 = _in_proj_bwd(x2d, dh1, pre_norm_g, w_in_t, du, dzs, dq, dk, dv, dza)
    grads = dict(
        pre_norm_g=d_g1, w_in=d_w_in_t, ssm_lam_re=d_lam_re, ssm_lam_im=d_lam_im, ssm_log_step=d_log_step,
        ssm_b_re=d_b_re, ssm_b_im=d_b_im, ssm_c_re=d_c_re, ssm_c_im=d_c_im, ssm_d=d_d, ssm_w_glu=d_w_glu,
        ssm_b_glu=d_b_glu, attn_sinks=d_sinks, w_out=d_w_out, post_norm_g=d_g2, pl_w_proj=d_w_proj,
        pl_w_gate=d_w_gate, pl_b_gate=d_b_gate)
    return grad_x.reshape(x.shape), loss, grads


_BIG = ("w_in", "ssm_w_glu", "w_out", "pl_w_proj", "pl_w_gate")
_BIG_SHARD = {"w_in": (D_IN // N_CHIPS, D_MODEL), "ssm_w_glu": (D_SSM // N_CHIPS, D_SSM),
              "w_out": (D_MODEL // N_CHIPS, D_MODEL), "pl_w_proj": (D_PLE, D_MODEL // N_CHIPS),
              "pl_w_gate": (D_MODEL // N_CHIPS, D_MODEL)}
_SMALL = {"pre_norm_g": (1, D_MODEL), "ssm_lam_re": (SSM_GROUPS, SSM_STATE), "ssm_lam_im": (SSM_GROUPS, SSM_STATE),
          "ssm_log_step": (1, SSM_GROUPS), "ssm_b_re": (D_SSM, SSM_STATE), "ssm_b_im": (D_SSM, SSM_STATE),
          "ssm_c_re": (D_SSM, SSM_STATE), "ssm_c_im": (D_SSM, SSM_STATE), "ssm_d": (1, D_SSM), "ssm_b_glu": (1, D_SSM),
          "attn_sinks": (1, N_HEADS), "post_norm_g": (1, D_MODEL), "pl_b_gate": (1, D_MODEL)}
_VEC_ROWS = ("pre_norm_g", "post_norm_g", "pl_b_gate", "ssm_d", "ssm_b_glu", "attn_sinks", "ssm_log_step", "loss")
_SMALL_GROUPS = (
    ("vec", (8, D_MODEL), tuple((name, r) for r, name in enumerate(_VEC_ROWS))),
    ("lam", (2 * SSM_GROUPS, SSM_STATE), (("ssm_lam_re", 0), ("ssm_lam_im", SSM_GROUPS))),
    ("bc", (4 * D_SSM, SSM_STATE), (("ssm_b_re", 0), ("ssm_b_im", D_SSM), ("ssm_c_re", 2 * D_SSM),
                                    ("ssm_c_im", 3 * D_SSM))),
)
_SMALL_ORDER = tuple(name for _, _, members in _SMALL_GROUPS for name, _ in members)
_WEIGHT_ORDER = ("pre_norm_g", "w_in", "ssm_lam_re", "ssm_lam_im", "ssm_log_step", "ssm_b_re", "ssm_b_im", "ssm_c_re",
                 "ssm_c_im", "ssm_d", "ssm_w_glu", "ssm_b_glu", "attn_sinks", "w_out", "post_norm_g", "pl_w_proj",
                 "pl_w_gate", "pl_b_gate")


def _small_shape(name):
    return (1, 1) if name == "loss" else _SMALL[name]


def _to_kernel_form(name, a):
    a = a[0]
    if name == "w_in":
        return a.T
    if name in ("ssm_b_re", "ssm_b_im"):
        a = a.transpose(0, 2, 1)
    return a.reshape(_SMALL[name]) if name in _SMALL else a


def _from_kernel_form(name, a, shape):
    if name == "w_in":
        a = a.T
    if name in ("ssm_b_re", "ssm_b_im"):
        a = a.reshape(SSM_GROUPS, SSM_GROUP_CH, SSM_STATE).transpose(0, 2, 1)
    return a.reshape(shape)


def _mesh_place():
    x, y, c = lax.axis_index("x"), lax.axis_index("y"), lax.axis_index("c")
    other_chips = ((1 - x, y), (x, 1 - y), (1 - x, 1 - y))
    return x, y, c, other_chips


def _gather_copies(s_refs, g_refs, send_sems, recv_sems, local_sems):
    x, y, c, other_chips = _mesh_place()
    started = []
    for i, (s_ref, g_ref) in enumerate(zip(s_refs, g_refs)):
        rows = s_ref.shape[0]
        half = rows // 2

        def block(chip, g_ref=g_ref, rows=rows, half=half):
            return g_ref.at[pl.ds((2 * chip[0] + chip[1]) * rows + c * half, half), :]

        def copy(k, chip, to, src=None, i=i, block=block):
            return pltpu.make_async_remote_copy(
                src_ref=block(chip) if src is None else src, dst_ref=block(chip), send_sem=send_sems.at[6 * i + k],
                recv_sem=recv_sems.at[6 * i + k], device_id=to, device_id_type=MESH)

        own = pltpu.make_async_copy(s_ref, g_ref.at[pl.ds((2 * x + y) * rows, rows), :], local_sems.at[i])
        own.start()
        first = [copy(k, (x, y), (*chip, c), src=s_ref.at[pl.ds(c * half, half), :])
                 for k, chip in enumerate(other_chips)]
        for cp in first:
            cp.start()
        passed = [copy(3 + k, chip, (x, y, 1 - c)) for k, chip in enumerate(other_chips)]
        started.append((own, first, passed))
    for own, first, passed in started:
        for k in range(3):
            first[k].wait_recv()
            passed[k].start()
    for own, first, passed in started:
        for k in range(3):
            passed[k].wait_recv()
        for cp in first + passed:
            cp.wait_send()
        own.wait()


def _gather_semaphores(n_t):
    return [pltpu.SemaphoreType.DMA((6 * n_t,)), pltpu.SemaphoreType.DMA((6 * n_t,)), pltpu.SemaphoreType.DMA((n_t,))]


def _gather_weights(shards):
    n_t = len(shards)

    def body(*refs):
        _gather_copies(refs[:n_t], refs[n_t:2 * n_t], *refs[2 * n_t + 1:])
        refs[2 * n_t][...] = jnp.zeros_like(refs[2 * n_t])

    any_spec = pl.BlockSpec(memory_space=pl.ANY)
    *full, done = _call(
        body, name="gather_weights", in_specs=[any_spec] * n_t,
        out_specs=[any_spec] * n_t + [pl.BlockSpec(memory_space=pltpu.VMEM)],
        out_shape=[_sds((N_CHIPS * s.shape[0], s.shape[1]), s.dtype) for s in shards]
        + [jax.ShapeDtypeStruct((8, LANES), F32)],
        scratch_shapes=_gather_semaphores(n_t),
    )(*shards)
    return full, done[0, 0]


def _gather_weights_beside(shards):
    n_t = len(shards)
    hbm = pltpu.MemorySpace.HBM
    s_refs = [jax.new_ref(s, memory_space=hbm) for s in shards]
    g_refs = [jax.empty_ref(jax.ShapeDtypeStruct((N_CHIPS * s.shape[0], s.shape[1]), s.dtype), memory_space=hbm)
              for s in shards]

    def launch(send_sems, recv_sems, local_sems):
        x, y, c, other_chips = _mesh_place()
        peers = [(*chip, c) for chip in other_chips] + [(x, y, 1 - c)]
        barrier = pltpu.get_barrier_semaphore()
        for peer in peers:
            pl.semaphore_signal(barrier, inc=1, device_id=peer, device_id_type=MESH)
        pl.semaphore_wait(barrier, len(peers))
        _gather_copies(s_refs, g_refs, send_sems, recv_sems, local_sems)

    pl.kernel(launch, mesh=plsc.ScalarSubcoreMesh(axis_name="sequencer", num_cores=1), name="gather_weights_beside",
              scratch_types=_gather_semaphores(n_t), compiler_params=pltpu.CompilerParams(collective_id=1))()
    return [g[...] for g in g_refs]


def _exchange_grads(big, small):
    n_t = len(big)
    n_g = len(_SMALL_GROUPS)
    names = _SMALL_ORDER
    halves = [(b.shape[0] // N_CHIPS // 2, b.shape[1]) for b in big]
    n_sems = 4 * n_g + 8 * n_t

    def body(*refs):
        pos = 0

        def take(n):
            nonlocal pos
            pos += n
            return refs[pos - n:pos]

        big_refs, small_refs = take(n_t), dict(zip(names, take(len(names))))
        out_refs, small_out_refs, land_refs = take(n_t), dict(zip(names, take(len(names)))), take(n_t)
        ga, gb, pme, send_b, recv_b = take(n_t), take(n_t), take(n_t), take(n_t), take(n_t)
        s_own, s_sib, s_chips, s_pair = take(n_g), take(n_g), take(n_g), take(n_g)
        stage = dict(zip(names, take(len(names))))
        send_sems, recv_sems, local_sems = take(3)
        x, y, c, other_chips = _mesh_place()
        me = 2 * x + y
        sibling = (x, y, 1 - c)
        sem_at = iter(range(n_sems))

        def remote(src, dst, to):
            k = next(sem_at)
            return pltpu.make_async_remote_copy(src_ref=src, dst_ref=dst, send_sem=send_sems.at[k],
                                                recv_sem=recv_sems.at[k], device_id=to, device_id_type=MESH)

        loads = [pltpu.make_async_copy(small_refs[name], stage[name], local_sems.at[2 + n_t + a])
                 for a, name in enumerate(names)]
        for cp in loads:
            cp.start()
        for cp in loads:
            cp.wait()
        small_swaps = []
        for gi, (_, _, members) in enumerate(_SMALL_GROUPS):
            s_own[gi][...] = jnp.zeros_like(s_own[gi])
            for name, r0 in members:
                r, n = _small_shape(name)
                s_own[gi][r0:r0 + r, 0:n] = stage[name][...]
            small_swaps.append(remote(s_own[gi], s_sib[gi], sibling))
            small_swaps[gi].start()
        big_swaps = []
        for i in range(n_t):
            hr = halves[i][0]
            big_swaps.append([])
            for j in range(N_CHIPS):
                src = big_refs[i].at[pl.ds(j * 2 * hr + (1 - c) * hr, hr), :]
                big_swaps[i].append(remote(src, land_refs[i].at[j], sibling))
                big_swaps[i][j].start()
        small_sends = []
        for gi in range(n_g):
            small_swaps[gi].wait_recv()
            s_pair[gi][...] = s_own[gi][...] + s_sib[gi][...]
            small_sends.append([remote(s_pair[gi], s_chips[gi].at[k], (*chip, c)) for k, chip in enumerate(other_chips)])
            for cp in small_sends[gi]:
                cp.start()

        def pair_sum(i, j, dst):
            hr = halves[i][0]
            a = pltpu.make_async_copy(big_refs[i].at[pl.ds(j * 2 * hr + c * hr, hr), :], ga[i], local_sems.at[0])
            b = pltpu.make_async_copy(land_refs[i].at[j], gb[i], local_sems.at[1])
            a.start()
            b.start()
            a.wait()
            b.wait()
            dst[...] = (ga[i][...] + gb[i][...]).astype(dst.dtype)

        big_sends = []
        for i in range(n_t):
            for j in range(N_CHIPS):
                big_swaps[i][j].wait_recv()
            big_sends.append([])
            for k, chip in enumerate(other_chips):
                pair_sum(i, 2 * chip[0] + chip[1], send_b[i].at[k])
                big_sends[i].append(remote(send_b[i].at[k], recv_b[i].at[k], (*chip, c)))
                big_sends[i][k].start()
        last_swaps, keeps = [], []
        for i in range(n_t):
            hr = halves[i][0]
            pair_sum(i, me, pme[i])
            for k in range(3):
                big_sends[i][k].wait_recv()
            pme[i][...] = ((pme[i][...] + recv_b[i][0].astype(F32)) + recv_b[i][1].astype(F32)) + recv_b[i][2].astype(F32)
            mine = out_refs[i].at[pl.ds(c * hr, hr), :]
            keeps.append(pltpu.make_async_copy(pme[i], mine, local_sems.at[2 + i]))
            keeps[i].start()
            last_swaps.append(remote(pme[i], mine, sibling))
            last_swaps[i].start()

        for gi, (_, _, members) in enumerate(_SMALL_GROUPS):
            for k in range(3):
                small_sends[gi][k].wait_recv()
            total = None
            for j in range(N_CHIPS):
                rel = jnp.bitwise_xor(j, me)
                term = jnp.where(rel == 0, s_pair[gi][...], jnp.where(
                    rel == 2, s_chips[gi][0], jnp.where(rel == 1, s_chips[gi][1], s_chips[gi][2])))
                total = term if total is None else total + term
            s_sib[gi][...] = total
            for name, r0 in members:
                r, n = _small_shape(name)
                stage[name][...] = s_sib[gi][r0:r0 + r, 0:n]
        stores = [pltpu.make_async_copy(stage[name], small_out_refs[name], local_sems.at[2 + n_t + a])
                  for a, name in enumerate(names)]
        for cp in stores:
            cp.start()

        for i in range(n_t):
            last_swaps[i].wait_recv()
            keeps[i].wait()
        for cp in stores:
            cp.wait()
        for cp in (small_swaps + [cp for group in big_swaps + small_sends + big_sends for cp in group] + last_swaps):
            cp.wait_send()

    any_spec = pl.BlockSpec(memory_space=pl.ANY)
    small_shapes = [_sds(_small_shape(n)) for n in names]
    group_shapes = [shape for _, shape, _ in _SMALL_GROUPS]
    per_matrix = lambda dtype, lead=(): [pltpu.VMEM(lead + h, dtype) for h in halves]
    outs = _call(
        body, name="exchange_grads",
        in_specs=[any_spec] * (n_t + len(names)),
        out_specs=[any_spec] * (2 * n_t + len(names)),
        out_shape=([_sds((b.shape[0] // N_CHIPS, b.shape[1])) for b in big] + small_shapes
                   + [_sds((N_CHIPS,) + h) for h in halves]),
        scratch_shapes=(per_matrix(F32) + per_matrix(F32) + per_matrix(F32) + per_matrix(BF16, (3,)) + per_matrix(BF16, (3,))
                        + [pltpu.VMEM(s, F32) for s in group_shapes] * 2 + [pltpu.VMEM((3,) + s, F32) for s in group_shapes]
                        + [pltpu.VMEM(s, F32) for s in group_shapes]
                        + [pltpu.VMEM(_small_shape(n), F32) for n in names]
                        + [pltpu.SemaphoreType.DMA((n_sems,)), pltpu.SemaphoreType.DMA((n_sems,)),
                           pltpu.SemaphoreType.DMA((2 + n_t + len(names),))]),
        compiler_params=_params(48),
    )(*big, *[small[n] for n in names])
    return list(outs[:n_t]), dict(zip(names, outs[n_t:n_t + len(names)]))


def _adamw_update(w, g, m, v):
    m = ADAM_B1 * m + (1.0 - ADAM_B1) * g
    v = ADAM_B2 * v + (1.0 - ADAM_B2) * (g * g)
    m_hat = m / (1.0 - ADAM_B1 ** ADAM_STEP)
    v_hat = v / (1.0 - ADAM_B2 ** ADAM_STEP)
    return -ADAM_LR * (m_hat / (jnp.sqrt(v_hat) + ADAM_EPS) + ADAM_WD * w), m, v


def _adamw(w, g, m, v, grid, name):
    n_t = len(w)

    def body(*refs):
        ins, outs = refs[:4 * n_t], refs[4 * n_t:]
        for i in range(n_t):
            w_, g_, m_, v_ = [ins[a * n_t + i][...] for a in range(4)]
            vals = (g_,) + _adamw_update(w_, g_, m_, v_)
            for a in range(4):
                outs[a * n_t + i][...] = vals[a]

    specs = [pl.BlockSpec((a.shape[0] // grid, a.shape[1]), lambda i: (i, 0)) for a in w]
    shapes = [_sds(a.shape) for a in w]
    outs = _call(
        body, name=name, grid=(grid,), in_specs=specs * 4, out_specs=specs * 4, out_shape=shapes * 4,
        compiler_params=_params(40, ("arbitrary",)),
    )(*w, *g, *m, *v)
    return [outs[a * n_t:(a + 1) * n_t] for a in range(4)]


def kernel(x, p, pre_norm_g, w_in, ssm_lam_re, ssm_lam_im, ssm_log_step, ssm_b_re, ssm_b_im, ssm_c_re, ssm_c_im, ssm_d, ssm_w_glu, ssm_b_glu, attn_sinks, w_out, post_norm_g, pl_w_proj, pl_w_gate, pl_b_gate, loss_target, m_pre_norm_g, m_w_in, m_ssm_lam_re, m_ssm_lam_im, m_ssm_log_step, m_ssm_b_re, m_ssm_b_im, m_ssm_c_re, m_ssm_c_im, m_ssm_d, m_ssm_w_glu, m_ssm_b_glu, m_attn_sinks, m_w_out, m_post_norm_g, m_pl_w_proj, m_pl_w_gate, m_pl_b_gate, v_pre_norm_g, v_w_in, v_ssm_lam_re, v_ssm_lam_im, v_ssm_log_step, v_ssm_b_re, v_ssm_b_im, v_ssm_c_re, v_ssm_c_im, v_ssm_d, v_ssm_w_glu, v_ssm_b_glu, v_attn_sinks, v_w_out, v_post_norm_g, v_pl_w_proj, v_pl_w_gate, v_pl_b_gate):
    weights = dict(pre_norm_g=pre_norm_g, w_in=w_in, ssm_lam_re=ssm_lam_re, ssm_lam_im=ssm_lam_im,
                   ssm_log_step=ssm_log_step, ssm_b_re=ssm_b_re, ssm_b_im=ssm_b_im, ssm_c_re=ssm_c_re,
                   ssm_c_im=ssm_c_im, ssm_d=ssm_d, ssm_w_glu=ssm_w_glu, ssm_b_glu=ssm_b_glu, attn_sinks=attn_sinks,
                   w_out=w_out, post_norm_g=post_norm_g, pl_w_proj=pl_w_proj, pl_w_gate=pl_w_gate, pl_b_gate=pl_b_gate)
    m_in = dict(pre_norm_g=m_pre_norm_g, w_in=m_w_in, ssm_lam_re=m_ssm_lam_re, ssm_lam_im=m_ssm_lam_im,
                ssm_log_step=m_ssm_log_step, ssm_b_re=m_ssm_b_re, ssm_b_im=m_ssm_b_im, ssm_c_re=m_ssm_c_re,
                ssm_c_im=m_ssm_c_im, ssm_d=m_ssm_d, ssm_w_glu=m_ssm_w_glu, ssm_b_glu=m_ssm_b_glu,
                attn_sinks=m_attn_sinks, w_out=m_w_out, post_norm_g=m_post_norm_g, pl_w_proj=m_pl_w_proj,
                pl_w_gate=m_pl_w_gate, pl_b_gate=m_pl_b_gate)
    v_in = dict(pre_norm_g=v_pre_norm_g, w_in=v_w_in, ssm_lam_re=v_ssm_lam_re, ssm_lam_im=v_ssm_lam_im,
                ssm_log_step=v_ssm_log_step, ssm_b_re=v_ssm_b_re, ssm_b_im=v_ssm_b_im, ssm_c_re=v_ssm_c_re,
                ssm_c_im=v_ssm_c_im, ssm_d=v_ssm_d, ssm_w_glu=v_ssm_w_glu, ssm_b_glu=v_ssm_b_glu,
                attn_sinks=v_attn_sinks, w_out=v_w_out, post_norm_g=v_post_norm_g, pl_w_proj=v_pl_w_proj,
                pl_w_gate=v_pl_w_gate, pl_b_gate=v_pl_b_gate)

    def two_d(tree):
        return {k: _to_kernel_form(k, a) for k, a in tree.items()}

    w2, m2, v2 = two_d(weights), two_d(m_in), two_d(v_in)

    (w_in_full,), gathered = _gather_weights([w2["w_in"].astype(BF16)])
    rest = _gather_weights_beside([(w2[n] + gathered).astype(BF16) for n in _BIG[1:]])
    full = dict(zip(_BIG, [w_in_full] + rest))
    s5_params = tuple(w2[n] for n in ("ssm_lam_re", "ssm_lam_im", "ssm_log_step", "ssm_b_re", "ssm_b_im", "ssm_c_re",
                                      "ssm_c_im"))
    grad_x, loss, grads = _local_step(
        x, p, loss_target, w2["pre_norm_g"], full["w_in"], s5_params, w2["ssm_d"], full["ssm_w_glu"], w2["ssm_b_glu"],
        w2["attn_sinks"], full["w_out"], w2["post_norm_g"], full["pl_w_proj"], full["pl_w_gate"], w2["pl_b_gate"])

    g_big, g_small = _exchange_grads([grads[n] for n in _BIG], {**{n: grads[n] for n in _SMALL}, "loss": loss})
    g_big = dict(zip(_BIG, g_big))
    total_loss = g_small.pop("loss")

    big_out = _adamw([w2[n] for n in _BIG], [g_big[n] for n in _BIG], [m2[n] for n in _BIG], [v2[n] for n in _BIG],
                     8, "adamw_matrices")
    small_names = tuple(_SMALL)
    small_out = _adamw([w2[n] for n in small_names], [g_small[n] for n in small_names], [m2[n] for n in small_names],
                       [v2[n] for n in small_names], 1, "adamw_small")

    results = [{**dict(zip(_BIG, big_part)), **dict(zip(small_names, small_part))}
               for big_part, small_part in zip(big_out, small_out)]
    flat = [_from_kernel_form(name, r[name], weights[name].shape) for r in results for name in _WEIGHT_ORDER]
    return (total_loss.reshape(()), grad_x, *flat)
```

```python
import math

import jax
import jax.numpy as jnp
from jax import lax
from jax.experimental import pallas as pl
from jax.experimental.pallas import tpu as pltpu
from jax.experimental.pallas import tpu_sc as plsc

F32 = jnp.float32
BF16 = jnp.bfloat16

D_MODEL = 1024
D_SSM = 512
D_ATTN = 512
SSM_GROUPS = 32
SSM_GROUP_CH = 16
SSM_STATE = 64
SSM_LANES = SSM_GROUPS * SSM_STATE
HEAD_DIM = 64
N_HEADS = 8
KV_HEADS = 2
Q_PER_KV = 4
WINDOW = 128
BLOCK = 128
D_PLE = 256
D_IN = 2304
EPS = 1e-6
ATTN_SCALE = 1.0 / math.sqrt(HEAD_DIM)

ADAM_LR = 0.001
ADAM_B1 = 0.9
ADAM_B2 = 0.999
ADAM_EPS = 1e-08
ADAM_WD = 0.01
ADAM_STEP = 10

N_CHIPS = 4
LANES = 128
SCAN_CHUNKS = 8
SCAN_TILE_STEPS = 16
SCAN_LANE_CHUNK = 512
MIB = 2 ** 20
MESH = pl.DeviceIdType.MESH


def _dot(a, b):
    return jnp.dot(a, b, preferred_element_type=F32)


def _dot_nt(a, b):
    return lax.dot_general(a, b, (((1,), (1,)), ((), ())), preferred_element_type=F32)


def _dot_tn(a, b):
    return lax.dot_general(a, b, (((0,), (0,)), ((), ())), preferred_element_type=F32)


def _params(vmem_mib, semantics=None):
    kw = dict(vmem_limit_bytes=vmem_mib * MIB)
    if semantics is not None:
        kw["dimension_semantics"] = semantics
    return pltpu.CompilerParams(**kw)


def _full(shape):
    nd = len(shape)
    return pl.BlockSpec(shape, lambda *_: (0,) * nd, pipeline_mode=pl.Buffered(1))


def _rows(tm, width):
    return pl.BlockSpec((tm, width), lambda i: (i, 0))


def _sds(shape, dtype=F32):
    return pltpu.HBM(shape, dtype)


def _call(body, **kw):
    fn = pl.pallas_call(body, **kw)
    return lambda *args: fn(*[pltpu.with_memory_space_constraint(a, pltpu.HBM) for a in args])


def _silu(z):
    return z * jax.nn.sigmoid(z)


def _in_proj(x2d, g1, w_in_t, n_seq, seq):
    rows = x2d.shape[0]
    tm = 512
    slab, steps, _, _ = _scan_geometry(n_seq, seq)

    def body(x_ref, g_ref, w_ref, *out_refs):
        u_parts, (zs_ref, q_ref, k_ref, v_ref, za_ref) = out_refs[:_SCAN_PARTS], out_refs[_SCAN_PARTS:]
        x = x_ref[...]
        r = lax.rsqrt(jnp.mean(x * x, axis=-1, keepdims=True) + EPS)
        hn = (x * r * g_ref[...]).astype(BF16)

        def proj(a, b):
            return _dot_nt(hn, w_ref[a:b, :])

        _store_chunks(u_parts, pl.program_id(0) * (tm // steps), proj(0, 512), steps, slab)
        zs_ref[...] = proj(512, 1024)
        q_ref[...] = proj(1024, 1536).astype(BF16)
        k_ref[...] = proj(1536, 1664).astype(BF16)
        v_ref[...] = proj(1664, 1792).astype(BF16)
        za_ref[...] = proj(1792, 2304)

    *u_parts, zs, q, k, v, za = _call(
        body, name="in_proj", grid=(rows // tm,),
        in_specs=[_rows(tm, D_MODEL), _full((1, D_MODEL)), _full((D_IN, D_MODEL))],
        out_specs=_whole_parts(rows) + [_rows(tm, 512), _rows(tm, 512), _rows(tm, 128), _rows(tm, 128), _rows(tm, 512)],
        out_shape=_part_shapes(rows) + [_sds((rows, 512)), _sds((rows, 512), BF16), _sds((rows, 128), BF16),
                                        _sds((rows, 128), BF16), _sds((rows, 512))],
        compiler_params=_params(48, ("arbitrary",)),
    )(x2d, g1, w_in_t)
    return u_parts, zs, q, k, v, za


def _in_proj_bwd(x2d, dh1, g1, w_in_t, du_parts, dzs, dq, dk, dv, dza, n_seq, seq):
    rows = x2d.shape[0]
    tm = 256
    slab, steps, _, _ = _scan_geometry(n_seq, seq)
    pieces = ((0, 512), (512, 1024), (1024, 1536), (1536, 1664), (1664, 1792), (1792, 2304))

    def body(x_ref, dh1_ref, g_ref, w_ref, *refs):
        du_parts, (dzs_ref, dq_ref, dk_ref, dv_ref, dza_ref, gx_ref, dw_ref, dg_ref) = refs[:_SCAN_PARTS], refs[_SCAN_PARTS:]

        @pl.when(pl.program_id(0) == 0)
        def _():
            dw_ref[...] = jnp.zeros_like(dw_ref)
            dg_ref[...] = jnp.zeros_like(dg_ref)

        x = x_ref[...]
        g = g_ref[...]
        r = lax.rsqrt(jnp.mean(x * x, axis=-1, keepdims=True) + EPS)
        xr = x * r
        hn = (xr * g).astype(BF16)
        dhn = jnp.zeros((tm, D_MODEL), F32)
        du = _load_chunks(du_parts, pl.program_id(0) * (tm // steps), tm // steps, steps, slab)
        for (a, b), piece in zip(pieces, (du, dzs_ref[...], dq_ref[...], dk_ref[...], dv_ref[...], dza_ref[...])):
            piece = piece.astype(BF16)
            dhn = dhn + _dot(piece, w_ref[a:b, :])
            dw_ref[a:b, :] += _dot_tn(piece, hn)
        dg_ref[...] += jnp.sum(dhn * xr, axis=0, keepdims=True)
        a_ = dhn * g
        gx_ref[...] = dh1_ref[...] + r * a_ - xr * (r * jnp.mean(a_ * xr, axis=-1, keepdims=True))

    return _call(
        body, name="in_proj_bwd", grid=(rows // tm,),
        in_specs=[_rows(tm, D_MODEL), _rows(tm, D_MODEL), _full((1, D_MODEL)), _full((D_IN, D_MODEL))]
        + _whole_parts(rows) + [_rows(tm, 512), _rows(tm, 512), _rows(tm, 128), _rows(tm, 128), _rows(tm, 512)],
        out_specs=[_rows(tm, D_MODEL), _full((D_IN, D_MODEL)), _full((1, D_MODEL))],
        out_shape=[_sds((rows, D_MODEL)), _sds((D_IN, D_MODEL)), _sds((1, D_MODEL))],
        compiler_params=_params(52, ("arbitrary",)),
    )(x2d, dh1, g1, w_in_t, *du_parts, dzs, dq, dk, dv, dza)


def _iota(shape, axis):
    return lax.broadcasted_iota(jnp.int32, shape, axis)


def _exact_dot(a, b):
    return jnp.dot(a, b, precision=lax.Precision.HIGHEST, preferred_element_type=F32)


_HALF_GROUPS = SSM_GROUPS // 2
_N_SHIFT = SSM_STATE.bit_length() - 1
_P_SHIFT = SSM_GROUP_CH.bit_length() - 1


def _s5_operands(lam_re, lam_im, log_step, b_re, b_im, c_re, c_im):
    g, n, p = SSM_GROUPS, SSM_STATE, SSM_GROUP_CH
    gn, gp, hn_, hp = g * n, g * p, _HALF_GROUPS * n, _HALF_GROUPS * p
    eye_g = _iota((g, g), 0) == _iota((g, g), 1)
    step = jnp.sum(jnp.where(eye_g, jnp.exp(log_step), 0.0), axis=1, keepdims=True)
    a_re = lam_re * step
    a_im = lam_im * step
    mag = jnp.exp(a_re)
    lbar_re = mag * jnp.cos(a_im)
    lbar_im = mag * jnp.sin(a_im)
    n_re = lbar_re - 1.0
    den = lam_re * lam_re + lam_im * lam_im
    f_re = (n_re * lam_re + lbar_im * lam_im) / den
    f_im = (lbar_im * lam_re - n_re * lam_im) / den

    spread_n = (_iota((n, gn), 0) == (_iota((n, gn), 1) & (n - 1))).astype(F32)
    own_g = _iota((g, gn), 0) == (_iota((g, gn), 1) >> _N_SHIFT)

    def to_row(a):
        return jnp.sum(jnp.where(own_g, _exact_dot(a, spread_n), 0.0), axis=0, keepdims=True)

    per_group = ((_iota((gp, g), 0) >> _P_SHIFT) == _iota((gp, g), 1)).astype(F32)
    fx_re, fx_im = _exact_dot(per_group, f_re), _exact_dot(per_group, f_im)
    bbar_re = fx_re * b_re - fx_im * b_im
    bbar_im = fx_re * b_im + fx_im * b_re

    tile_n = (_iota((n, hn_), 0) == (_iota((n, hn_), 1) & (n - 1))).astype(F32)
    same_group = (_iota((hp, hn_), 0) >> _P_SHIFT) == (_iota((hp, hn_), 1) >> _N_SHIFT)

    def embed(a, hf):
        return jnp.where(same_group, _exact_dot(a[hf * hp:(hf + 1) * hp], tile_n), 0.0)

    return (to_row(lbar_re), to_row(lbar_im), embed(bbar_re, 0), embed(bbar_re, 1), embed(bbar_im, 0),
            embed(bbar_im, 1), embed(c_re, 0), embed(c_re, 1), embed(c_im, 0), embed(c_im, 1))


_S5_PARAM_SHAPES = ((SSM_GROUPS, SSM_STATE), (SSM_GROUPS, SSM_STATE), (1, SSM_GROUPS),
                    (D_SSM, SSM_STATE), (D_SSM, SSM_STATE), (D_SSM, SSM_STATE), (D_SSM, SSM_STATE))
_CM_SHAPE = (2, _HALF_GROUPS * SSM_GROUP_CH, _HALF_GROUPS * SSM_STATE)
_S5_OPERAND_SHAPES = ((1, SSM_LANES), (1, SSM_LANES), _CM_SHAPE, _CM_SHAPE, _CM_SHAPE, _CM_SHAPE)


def _s5_params_fwd(*params):
    def body(*refs):
        ins, (lre_ref, lim_ref, btre_ref, btim_ref, cmre_ref, cmim_ref) = refs[:7], refs[7:]
        vals = _s5_operands(*[r[...] for r in ins])
        lre_ref[...] = vals[0]
        lim_ref[...] = vals[1]
        for ref, pair in zip((btre_ref, btim_ref, cmre_ref, cmim_ref), (vals[2:4], vals[4:6], vals[6:8], vals[8:10])):
            ref[0] = pair[0].astype(BF16)
            ref[1] = pair[1].astype(BF16)

    dtypes = (F32, F32, BF16, BF16, BF16, BF16)
    return _call(
        body, name="s5_params_fwd",
        in_specs=[_full(s) for s in _S5_PARAM_SHAPES], out_specs=[_full(s) for s in _S5_OPERAND_SHAPES],
        out_shape=[_sds(s, d) for s, d in zip(_S5_OPERAND_SHAPES, dtypes)], compiler_params=_params(32),
    )(*params)


def _s5_params_bwd(params, cotangents):
    def body(*refs):
        ins, (dlre, dlim, dbtre, dbtim, dcmre, dcmim), outs = refs[:7], refs[7:13], refs[13:]
        _, vjp = jax.vjp(_s5_operands, *[r[...] for r in ins])
        cts = (dlre[...], dlim[...], dbtre[0], dbtre[1], dbtim[0], dbtim[1], dcmre[0], dcmre[1], dcmim[0], dcmim[1])
        for ref, val in zip(outs, vjp(cts)):
            ref[...] = val

    return _call(
        body, name="s5_params_bwd",
        in_specs=[_full(s) for s in _S5_PARAM_SHAPES + _S5_OPERAND_SHAPES],
        out_specs=[_full(s) for s in _S5_PARAM_SHAPES],
        out_shape=[_sds(s) for s in _S5_PARAM_SHAPES], compiler_params=_params(48),
    )(*params, *cotangents)


def _scan_geometry(n_seq, seq):
    slab = n_seq * SCAN_CHUNKS
    steps = seq // SCAN_CHUNKS
    tile_rows = slab * SCAN_TILE_STEPS
    n_tiles = steps // SCAN_TILE_STEPS
    return slab, steps, tile_rows, n_tiles


_SCAN_PARTS = D_SSM // LANES


def _whole_parts(rows):
    return [_full((rows, LANES))] * _SCAN_PARTS


def _part_shapes(rows):
    return [_sds((rows, LANES))] * _SCAN_PARTS


def _load_chunks(parts, first_chunk, n_chunks, steps, slab):
    return jnp.concatenate([
        jnp.concatenate([ref[pl.ds(first_chunk + q, steps, stride=slab), :] for ref in parts], axis=1)
        for q in range(n_chunks)], axis=0)


def _store_chunks(parts, first_chunk, value, steps, slab):
    for q in range(value.shape[0] // steps):
        for j, ref in enumerate(parts):
            ref[pl.ds(first_chunk + q, steps, stride=slab), :] = value[q * steps:(q + 1) * steps,
                                                                     j * LANES:(j + 1) * LANES]


def _join_parts(parts):
    return jnp.concatenate([ref[...] for ref in parts], axis=1)


def _split_parts(parts, value):
    for j, ref in enumerate(parts):
        ref[...] = value[:, j * LANES:(j + 1) * LANES]


def _complex_power(re, im, n):
    out = None
    while n:
        if n & 1:
            out = (re, im) if out is None else (out[0] * re - out[1] * im, out[0] * im + out[1] * re)
        n >>= 1
        if n:
            re, im = re * re - im * im, 2.0 * re * im
    return out


def _chunk_carry(sum_re, sum_im, carry_re, carry_im, a_re, a_im, n_seq, reverse):
    carry_re[...] = jnp.zeros_like(carry_re)
    carry_im[...] = jnp.zeros_like(carry_im)
    for s in range(n_seq):
        order = range(SCAN_CHUNKS - 2, -1, -1) if reverse else range(1, SCAN_CHUNKS)
        for c in order:
            r = s * SCAN_CHUNKS + c
            p = r + 1 if reverse else r - 1
            p_re, p_im = carry_re[p:p + 1, :], carry_im[p:p + 1, :]
            carry_re[r:r + 1, :] = a_re * p_re - a_im * p_im + sum_re[p:p + 1, :]
            carry_im[r:r + 1, :] = a_re * p_im + a_im * p_re + sum_im[p:p + 1, :]


def _s5_scan_fwd(u_parts, bt_re, bt_im, cm_re, cm_im, lbar_re, lbar_im, d_row, n_seq, seq):
    slab, steps, tile_rows, n_tiles = _scan_geometry(n_seq, seq)
    rows = u_parts[0].shape[0]

    def body(*refs):
        u_refs, refs = refs[:_SCAN_PARTS], refs[_SCAN_PARTS:]
        (bre_ref, bim_ref, cre_ref, cim_ref, lre_ref, lim_ref, d_ref), refs = refs[:7], refs[7:]
        y_refs, (hre_ref, him_ref, st_re, st_im, h0_re, h0_im, buf_re, buf_im) = refs[:_SCAN_PARTS], refs[_SCAN_PARTS:]
        second = pl.program_id(0) == 1
        i = pl.program_id(1)

        @pl.when(jnp.logical_and(i == 0, jnp.logical_not(second)))
        def _():
            st_re[...] = jnp.zeros_like(st_re)
            st_im[...] = jnp.zeros_like(st_im)

        u = _join_parts(u_refs)
        ub = u.astype(BF16)
        for hf in range(2):
            cols = slice(hf * 1024, (hf + 1) * 1024)
            buf_re[:, cols] = _dot(ub[:, hf * 256:(hf + 1) * 256], bre_ref[hf])
            buf_im[:, cols] = _dot(ub[:, hf * 256:(hf + 1) * 256], bim_ref[hf])

        for lc in range(SSM_LANES // SCAN_LANE_CHUNK):
            cols = slice(lc * SCAN_LANE_CHUNK, (lc + 1) * SCAN_LANE_CHUNK)
            l_re = jnp.broadcast_to(lre_ref[:, cols], (slab, SCAN_LANE_CHUNK))
            l_im = jnp.broadcast_to(lim_ref[:, cols], (slab, SCAN_LANE_CHUNK))

            def scan_tile(keep_states):
                def step(t, carry):
                    s_re, s_im = carry
                    r0 = pl.multiple_of(t * slab, slab)
                    n_re = l_re * s_re - l_im * s_im + buf_re[pl.ds(r0, slab), cols]
                    n_im = l_re * s_im + l_im * s_re + buf_im[pl.ds(r0, slab), cols]
                    if keep_states:
                        buf_re[pl.ds(r0, slab), cols] = n_re
                        buf_im[pl.ds(r0, slab), cols] = n_im
                    return n_re, n_im

                s_re, s_im = lax.fori_loop(0, SCAN_TILE_STEPS, step, (st_re[:, cols], st_im[:, cols]), unroll=True)
                st_re[:, cols] = s_re
                st_im[:, cols] = s_im

            pl.when(jnp.logical_not(second))(lambda: scan_tile(False))
            pl.when(second)(lambda: scan_tile(True))

        @pl.when(jnp.logical_and(i == n_tiles - 1, jnp.logical_not(second)))
        def _():
            a_re, a_im = _complex_power(lre_ref[...], lim_ref[...], steps)
            _chunk_carry(st_re, st_im, h0_re, h0_im, a_re, a_im, n_seq, reverse=False)
            st_re[...] = h0_re[...]
            st_im[...] = h0_im[...]

        @pl.when(second)
        def _():
            h_re = buf_re[...].astype(BF16)
            h_im = buf_im[...].astype(BF16)
            hre_ref[...] = h_re
            him_ref[...] = h_im
            for hf in range(2):
                cols = slice(hf * 1024, (hf + 1) * 1024)
                ycols = slice(hf * 256, (hf + 1) * 256)
                y_half = (_dot_nt(h_re[:, cols], cre_ref[hf]) - _dot_nt(h_im[:, cols], cim_ref[hf])
                          + d_ref[:, ycols] * u[:, ycols])
                _split_parts(y_refs[2 * hf:2 * hf + 2], y_half)

    tile = lambda w: pl.BlockSpec((tile_rows, w), lambda p, i: (i, 0))
    out_tile = lambda w: pl.BlockSpec((tile_rows, w), lambda p, i: (i * p, 0))
    cm = _full(_CM_SHAPE)
    outs = _call(
        body, name="s5_scan_fwd", grid=(2, n_tiles),
        in_specs=[tile(LANES)] * _SCAN_PARTS + [cm, cm, cm, cm, _full((1, SSM_LANES)), _full((1, SSM_LANES)),
                                                _full((1, 512))],
        out_specs=[out_tile(LANES)] * _SCAN_PARTS + [out_tile(SSM_LANES), out_tile(SSM_LANES)],
        out_shape=_part_shapes(rows) + [_sds((rows, SSM_LANES), BF16), _sds((rows, SSM_LANES), BF16)],
        scratch_shapes=[pltpu.VMEM((slab, SSM_LANES), F32)] * 4 + [pltpu.VMEM((tile_rows, SSM_LANES), F32)] * 2,
        compiler_params=_params(40, ("arbitrary", "arbitrary")),
    )(*u_parts, bt_re, bt_im, cm_re, cm_im, lbar_re, lbar_im, d_row)
    return outs[:_SCAN_PARTS], outs[_SCAN_PARTS], outs[_SCAN_PARTS + 1]


def _s5_scan_bwd(dy_parts, u_parts, h_re, h_im, bt_re, bt_im, cm_re, cm_im, lbar_re, lbar_im, d_row, n_seq, seq):
    slab, steps, tile_rows, n_tiles = _scan_geometry(n_seq, seq)
    rows = u_parts[0].shape[0]

    def body(*refs):
        dy_refs, u_refs, refs = refs[:_SCAN_PARTS], refs[_SCAN_PARTS:2 * _SCAN_PARTS], refs[2 * _SCAN_PARTS:]
        (hre_ref, him_ref, bre_ref, bim_ref, cre_ref, cim_ref, lre_ref, lim_ref, d_ref), refs = refs[:9], refs[9:]
        du_refs, refs = refs[:_SCAN_PARTS], refs[_SCAN_PARTS:]
        (dbre_ref, dbim_ref, dcre_ref, dcim_ref, dlre_ref, dlim_ref, dd_ref,
         st_re, st_im, g0_re, g0_im, acc_re, acc_im, buf_re, buf_im) = refs
        second = pl.program_id(0) == 1
        i = pl.program_id(1)

        @pl.when(jnp.logical_and(i == 0, jnp.logical_not(second)))
        def _():
            st_re[...] = jnp.zeros_like(st_re)
            st_im[...] = jnp.zeros_like(st_im)
            acc_re[...] = jnp.zeros_like(acc_re)
            acc_im[...] = jnp.zeros_like(acc_im)
            for ref in (dbre_ref, dbim_ref, dcre_ref, dcim_ref, dd_ref):
                ref[...] = jnp.zeros_like(ref)

        dy = _join_parts(dy_refs)
        dyb = dy.astype(BF16)
        for hf in range(2):
            cols = slice(hf * 1024, (hf + 1) * 1024)
            buf_re[:, cols] = _dot(dyb[:, hf * 256:(hf + 1) * 256], cre_ref[hf])
            buf_im[:, cols] = -_dot(dyb[:, hf * 256:(hf + 1) * 256], cim_ref[hf])

        for lc in range(SSM_LANES // SCAN_LANE_CHUNK):
            cols = slice(lc * SCAN_LANE_CHUNK, (lc + 1) * SCAN_LANE_CHUNK)
            l_re = jnp.broadcast_to(lre_ref[:, cols], (slab, SCAN_LANE_CHUNK))
            l_im = jnp.broadcast_to(lim_ref[:, cols], (slab, SCAN_LANE_CHUNK))

            def advance(r0, s_re, s_im):
                n_re = l_re * s_re + l_im * s_im + buf_re[pl.ds(r0, slab), cols]
                n_im = l_re * s_im - l_im * s_re + buf_im[pl.ds(r0, slab), cols]
                buf_re[pl.ds(r0, slab), cols] = n_re
                buf_im[pl.ds(r0, slab), cols] = n_im
                return n_re, n_im

            def row0(k):
                return pl.multiple_of((SCAN_TILE_STEPS - 1 - k) * slab, slab)

            @pl.when(jnp.logical_not(second))
            def _():
                s_re, s_im = lax.fori_loop(0, SCAN_TILE_STEPS, lambda k, s: advance(row0(k), *s),
                                           (st_re[:, cols], st_im[:, cols]), unroll=True)
                st_re[:, cols] = s_re
                st_im[:, cols] = s_im

            @pl.when(second)
            def _():
                def step(k, carry):
                    s_re, s_im, a_re, a_im = carry
                    r0 = row0(k)
                    hr = hre_ref[pl.ds(r0, slab), cols].astype(F32)
                    hi = him_ref[pl.ds(r0, slab), cols].astype(F32)
                    a_re = a_re + s_re * hr + s_im * hi
                    a_im = a_im + s_im * hr - s_re * hi
                    return advance(r0, s_re, s_im) + (a_re, a_im)

                zero = jnp.zeros((slab, SCAN_LANE_CHUNK), F32)
                s_re, s_im, a_re, a_im = lax.fori_loop(
                    0, SCAN_TILE_STEPS, step, (st_re[:, cols], st_im[:, cols], zero, zero), unroll=True)
                st_re[:, cols] = s_re
                st_im[:, cols] = s_im
                acc_re[:, cols] += a_re
                acc_im[:, cols] += a_im

        @pl.when(jnp.logical_and(i == n_tiles - 1, jnp.logical_not(second)))
        def _():
            p_re, p_im = _complex_power(lre_ref[...], lim_ref[...], steps)
            _chunk_carry(st_re, st_im, g0_re, g0_im, p_re, -p_im, n_seq, reverse=True)
            st_re[...] = g0_re[...]
            st_im[...] = g0_im[...]

        @pl.when(second)
        def _():
            u = _join_parts(u_refs)
            ub = u.astype(BF16)
            g_re = buf_re[...].astype(BF16)
            g_im = buf_im[...].astype(BF16)
            dd_ref[...] += jnp.sum(dy * u, axis=0, keepdims=True)
            for hf in range(2):
                cols = slice(hf * 1024, (hf + 1) * 1024)
                ycols = slice(hf * 256, (hf + 1) * 256)
                du_half = (_dot_nt(g_re[:, cols], bre_ref[hf]) + _dot_nt(g_im[:, cols], bim_ref[hf])
                           + d_ref[:, ycols] * dy[:, ycols])
                _split_parts(du_refs[2 * hf:2 * hf + 2], du_half)
                dbre_ref[hf] += _dot_tn(ub[:, ycols], g_re[:, cols])
                dbim_ref[hf] += _dot_tn(ub[:, ycols], g_im[:, cols])
                dcre_ref[hf] += _dot_tn(dyb[:, ycols], hre_ref[:, cols])
                dcim_ref[hf] -= _dot_tn(dyb[:, ycols], him_ref[:, cols])

        @pl.when(jnp.logical_and(i == n_tiles - 1, second))
        def _():
            dlre_ref[...] = jnp.sum(acc_re[...], axis=0, keepdims=True)
            dlim_ref[...] = jnp.sum(acc_im[...], axis=0, keepdims=True)

    tile = lambda w: pl.BlockSpec((tile_rows, w), lambda p, i: (n_tiles - 1 - i, 0))
    second_tile = lambda w: pl.BlockSpec((tile_rows, w), lambda p, i: (n_tiles - 1 - i * p, 0))
    cm = _full(_CM_SHAPE)
    row = _full((1, SSM_LANES))
    outs = _call(
        body, name="s5_scan_bwd", grid=(2, n_tiles),
        in_specs=[tile(LANES)] * _SCAN_PARTS + [second_tile(LANES)] * _SCAN_PARTS
        + [second_tile(SSM_LANES), second_tile(SSM_LANES), cm, cm, cm, cm, row, row, _full((1, 512))],
        out_specs=[second_tile(LANES)] * _SCAN_PARTS + [cm, cm, cm, cm, row, row, _full((1, 512))],
        out_shape=(_part_shapes(rows) + [_sds(_CM_SHAPE)] * 4 + [_sds((1, SSM_LANES))] * 2 + [_sds((1, 512))]),
        scratch_shapes=[pltpu.VMEM((slab, SSM_LANES), F32)] * 6 + [pltpu.VMEM((tile_rows, SSM_LANES), F32)] * 2,
        compiler_params=_params(48, ("arbitrary", "arbitrary")),
    )(*dy_parts, *u_parts, h_re, h_im, bt_re, bt_im, cm_re, cm_im, lbar_re, lbar_im, d_row)
    return (outs[:_SCAN_PARTS],) + tuple(outs[_SCAN_PARTS:])


def _glu_gate(gl, a, zs):
    return gl * jax.nn.sigmoid(a) * _silu(zs)


def _glu_fwd(y_parts, zs, w_glu, b_glu, n_seq, seq):
    rows = zs.shape[0]
    tm = 512
    slab, steps, _, _ = _scan_geometry(n_seq, seq)

    def body(*refs):
        y_refs, (zs_ref, w_ref, b_ref, o_ref) = refs[:_SCAN_PARTS], refs[_SCAN_PARTS:]
        y = _load_chunks(y_refs, pl.program_id(0) * (tm // steps), tm // steps, steps, slab)
        gl = jax.nn.gelu(y)
        a = _dot(gl.astype(BF16), w_ref[...]) + b_ref[...]
        o_ref[...] = _glu_gate(gl, a, zs_ref[...]).astype(BF16)

    return _call(
        body, name="glu_fwd", grid=(rows // tm,),
        in_specs=_whole_parts(rows) + [_rows(tm, 512), _full((512, 512)), _full((1, 512))],
        out_specs=_rows(tm, 512), out_shape=_sds((rows, 512), BF16),
        compiler_params=_params(32, ("arbitrary",)),
    )(*y_parts, zs, w_glu, b_glu)


def _glu_bwd(y_parts, zs, d_out, w_glu, b_glu, n_seq, seq):
    rows = zs.shape[0]
    tm = 512
    slab, steps, _, _ = _scan_geometry(n_seq, seq)

    def body(*refs):
        y_refs, (zs_ref, d_ref, w_ref, b_ref), refs = refs[:_SCAN_PARTS], refs[_SCAN_PARTS:_SCAN_PARTS + 4], refs[_SCAN_PARTS + 4:]
        dy_refs, (dzs_ref, dw_ref, db_ref) = refs[:_SCAN_PARTS], refs[_SCAN_PARTS:]
        first_chunk = pl.program_id(0) * (tm // steps)

        @pl.when(pl.program_id(0) == 0)
        def _():
            dw_ref[...] = jnp.zeros_like(dw_ref)
            db_ref[...] = jnp.zeros_like(db_ref)

        gl, gelu_vjp = jax.vjp(jax.nn.gelu, _load_chunks(y_refs, first_chunk, tm // steps, steps, slab))
        glb = gl.astype(BF16)
        a = _dot(glb, w_ref[...]) + b_ref[...]
        _, gate_vjp = jax.vjp(_glu_gate, gl, a, zs_ref[...])
        d_gl, d_a, d_zs = gate_vjp(d_ref[...])
        dab = d_a.astype(BF16)
        d_gl = d_gl + _dot_nt(dab, w_ref[...])
        _store_chunks(dy_refs, first_chunk, gelu_vjp(d_gl)[0], steps, slab)
        dzs_ref[...] = d_zs.astype(BF16)
        dw_ref[...] += _dot_tn(glb, dab)
        db_ref[...] += jnp.sum(d_a, axis=0, keepdims=True)

    *dy_parts, dzs, dw, db = _call(
        body, name="glu_bwd", grid=(rows // tm,),
        in_specs=_whole_parts(rows) + [_rows(tm, 512), _rows(tm, 512), _full((512, 512)), _full((1, 512))],
        out_specs=_whole_parts(rows) + [_rows(tm, 512), _full((512, 512)), _full((1, 512))],
        out_shape=_part_shapes(rows) + [_sds((rows, 512), BF16), _sds((512, 512)), _sds((1, 512))],
        compiler_params=_params(40, ("arbitrary",)),
    )(*y_parts, zs, d_out, w_glu, b_glu)
    return dy_parts, dzs, dw, db


_GROUP_ROWS = Q_PER_KV * BLOCK
_BLOCK_SHIFT = BLOCK.bit_length() - 1


def _attn_bias(j):
    row = _iota((_GROUP_ROWS, BLOCK), 0)
    dist_cur = (row & (BLOCK - 1)) - _iota((_GROUP_ROWS, BLOCK), 1)
    dist_prev = dist_cur + BLOCK
    head = row >> _BLOCK_SHIFT
    slope = jnp.zeros((_GROUP_ROWS, BLOCK), F32)
    for g in range(Q_PER_KV):
        slope = jnp.where(head == g, 2.0 ** (-(j * Q_PER_KV + g + 1)), slope)
    bias_cur = jnp.where(dist_cur >= 0, -slope * dist_cur.astype(F32), -jnp.inf)
    bias_prev = jnp.where(dist_prev < WINDOW, -slope * dist_prev.astype(F32), -jnp.inf)
    return bias_cur, bias_prev


_ATTN_BIAS_SCRATCH = pltpu.VMEM((KV_HEADS, 2, _GROUP_ROWS, BLOCK), F32)


def _fill_attn_bias(bias_ref):
    @pl.when(jnp.logical_and(pl.program_id(0) == 0, pl.program_id(1) == 0))
    def _():
        for j in range(KV_HEADS):
            bias_ref[j, 0], bias_ref[j, 1] = _attn_bias(j)


def _stack_heads(x, j):
    heads = range(j * Q_PER_KV, (j + 1) * Q_PER_KV)
    return jnp.concatenate([x[:, h * HEAD_DIM:(h + 1) * HEAD_DIM] for h in heads], axis=0)


def _stack_columns(x, j):
    heads = range(j * Q_PER_KV, (j + 1) * Q_PER_KV)
    return jnp.concatenate([jnp.broadcast_to(x[:, h:h + 1], (BLOCK, 1)) for h in heads], axis=0)


def _attn_fwd(q, k, v, za, sinks, n_seq, seq):
    nb = seq // BLOCK
    rows = q.shape[0]

    def body(q_ref, kc_ref, kp_ref, vc_ref, vp_ref, za_ref, sk_ref, o_ref, ao_ref, lse_ref, bias_ref):
        _fill_attn_bias(bias_ref)
        has_prev = pl.program_id(1) > 0
        q_all = q_ref[...]
        for j in range(KV_HEADS):
            js = slice(j * HEAD_DIM, (j + 1) * HEAD_DIM)
            bias_c, bias_p = bias_ref[j, 0], bias_ref[j, 1]
            q4 = _stack_heads(q_all, j)
            sc = _dot_nt(q4, kc_ref[:, js]) * ATTN_SCALE + bias_c
            sp = _dot_nt(q4, kp_ref[:, js]) * ATTN_SCALE + jnp.where(has_prev, bias_p, -jnp.inf)
            sink = _stack_columns(sk_ref[...], j)
            m = jnp.maximum(jnp.maximum(jnp.max(sc, axis=-1, keepdims=True), jnp.max(sp, axis=-1, keepdims=True)), sink)
            ec = jnp.exp(sc - m)
            ep = jnp.exp(sp - m)
            den = jnp.sum(ec, axis=-1, keepdims=True) + jnp.sum(ep, axis=-1, keepdims=True) + jnp.exp(sink - m)
            o4 = (_dot(ec.astype(BF16), vc_ref[:, js]) + _dot(ep.astype(BF16), vp_ref[:, js])) * (1.0 / den)
            lse4 = m + jnp.log(den)
            for g in range(Q_PER_KV):
                h = j * Q_PER_KV + g
                o_ref[:, h * HEAD_DIM:(h + 1) * HEAD_DIM] = o4[g * BLOCK:(g + 1) * BLOCK]
                lse_ref[:, h:h + 1] = lse4[g * BLOCK:(g + 1) * BLOCK]
        ao_ref[...] = (o_ref[...] * _silu(za_ref[...])).astype(BF16)

    cur = lambda w: pl.BlockSpec((BLOCK, w), lambda b, n: (b * nb + n, 0))
    prev = lambda w: pl.BlockSpec((BLOCK, w), lambda b, n: (b * nb + jnp.maximum(n - 1, 0), 0))
    return _call(
        body, name="attn_fwd", grid=(n_seq, nb),
        in_specs=[cur(512), cur(128), prev(128), cur(128), prev(128), cur(512), _full((1, N_HEADS))],
        out_specs=[cur(512), cur(512), cur(N_HEADS)],
        out_shape=[_sds((rows, 512)), _sds((rows, 512), BF16), _sds((rows, N_HEADS))],
        scratch_shapes=[_ATTN_BIAS_SCRATCH], compiler_params=_params(32, ("arbitrary", "arbitrary")),
    )(q, k, k, v, v, za, sinks)


def _attn_bwd(q, k, v, za, o, lse, d_ao, sinks, n_seq, seq):
    nb = seq // BLOCK
    rows = q.shape[0]

    def body(q_ref, q2_ref, kc_ref, kp_ref, vc_ref, vp_ref, za_ref, za2_ref, o_ref, o2_ref, lse_ref, lse2_ref,
             d_ref, d2_ref, sk_ref, dq_ref, dk_ref, dv_ref, dza_ref, dsk_ref, bias_ref):
        n = pl.program_id(1)
        _fill_attn_bias(bias_ref)

        @pl.when(jnp.logical_and(pl.program_id(0) == 0, n == 0))
        def _():
            dsk_ref[...] = jnp.zeros_like(dsk_ref)

        has_prev = n > 0
        has_next = n + 1 < nb

        o = o_ref[...]
        _, gate_vjp = jax.vjp(lambda o_, z_: o_ * _silu(z_), o, za_ref[...])
        d_o, d_za = gate_vjp(d_ref[...])
        dza_ref[...] = d_za.astype(BF16)
        o2 = o2_ref[...]
        d_o2 = d2_ref[...] * _silu(za2_ref[...])
        q_all, q2_all = q_ref[...], q2_ref[...]
        lse_all, lse2_all = lse_ref[...], lse2_ref[...]

        for j in range(KV_HEADS):
            js = slice(j * HEAD_DIM, (j + 1) * HEAD_DIM)
            kc, kp, vc, vp = kc_ref[:, js], kp_ref[:, js], vc_ref[:, js], vp_ref[:, js]
            bias_c, bias_p = bias_ref[j, 0], bias_ref[j, 1]
            q4 = _stack_heads(q_all, j)
            do4 = _stack_heads(d_o, j)
            do4b = do4.astype(BF16)
            delta = jnp.sum(do4 * _stack_heads(o, j), axis=-1, keepdims=True)
            lse4 = _stack_columns(lse_all, j)
            pc = jnp.exp(_dot_nt(q4, kc) * ATTN_SCALE + bias_c - lse4)
            pp = jnp.exp(_dot_nt(q4, kp) * ATTN_SCALE + jnp.where(has_prev, bias_p, -jnp.inf) - lse4)
            dsc = (pc * (_dot_nt(do4b, vc) - delta)).astype(BF16)
            dsp = (pp * (_dot_nt(do4b, vp) - delta)).astype(BF16)
            dq4 = ((_dot(dsc, kc) + _dot(dsp, kp)) * ATTN_SCALE).astype(BF16)
            sink_loss = jnp.exp(_stack_columns(sk_ref[...], j) - lse4) * delta
            for g in range(Q_PER_KV):
                h = j * Q_PER_KV + g
                dq_ref[:, h * HEAD_DIM:(h + 1) * HEAD_DIM] = dq4[g * BLOCK:(g + 1) * BLOCK]
                dsk_ref[0:1, h:h + 1] -= jnp.sum(sink_loss[g * BLOCK:(g + 1) * BLOCK], axis=0, keepdims=True)
            dk = _dot_tn(dsc, q4)
            dv = _dot_tn(pc.astype(BF16), do4b)
            q4n = _stack_heads(q2_all, j)
            do4n = _stack_heads(d_o2, j)
            do4nb = do4n.astype(BF16)
            delta2 = jnp.sum(do4n * _stack_heads(o2, j), axis=-1, keepdims=True)
            p2 = jnp.exp(_dot_nt(q4n, kc) * ATTN_SCALE + jnp.where(has_next, bias_p, -jnp.inf)
                         - _stack_columns(lse2_all, j))
            ds2 = (p2 * (_dot_nt(do4nb, vc) - delta2)).astype(BF16)
            dk = dk + _dot_tn(ds2, q4n)
            dv = dv + _dot_tn(p2.astype(BF16), do4nb)
            dk_ref[:, js] = (dk * ATTN_SCALE).astype(BF16)
            dv_ref[:, js] = dv.astype(BF16)

    cur = lambda w: pl.BlockSpec((BLOCK, w), lambda b, n: (b * nb + n, 0))
    prev = lambda w: pl.BlockSpec((BLOCK, w), lambda b, n: (b * nb + jnp.maximum(n - 1, 0), 0))
    nxt = lambda w: pl.BlockSpec((BLOCK, w), lambda b, n: (b * nb + jnp.minimum(n + 1, nb - 1), 0))
    return _call(
        body, name="attn_bwd", grid=(n_seq, nb),
        in_specs=[cur(512), nxt(512), cur(128), prev(128), cur(128), prev(128), cur(512), nxt(512),
                  cur(512), nxt(512), cur(N_HEADS), nxt(N_HEADS), cur(512), nxt(512), _full((1, N_HEADS))],
        out_specs=[cur(512), cur(128), cur(128), cur(512), _full((1, N_HEADS))],
        out_shape=[_sds((rows, 512), BF16), _sds((rows, 128), BF16), _sds((rows, 128), BF16),
                   _sds((rows, 512), BF16), _sds((1, N_HEADS))],
        scratch_shapes=[_ATTN_BIAS_SCRATCH], compiler_params=_params(32, ("arbitrary", "arbitrary")),
    )(q, q, k, k, v, v, za, za, o, o, lse, lse, d_ao, d_ao, sinks)


def _tail(ssm_out, attn_out, x2d, p2d, target, w_out, g2, w_gate, b_gate, w_proj):
    rows = x2d.shape[0]
    tm = 512

    def body(so_ref, ao_ref, x_ref, p_ref, t_ref, wo_ref, g2_ref, wg_ref, bg_ref, wp_ref,
             dh1_ref, dso_ref, dao_ref, dwo_ref, dwg_ref, dwp_ref, dbg_ref, dg2_ref, loss_ref):
        @pl.when(pl.program_id(0) == 0)
        def _():
            for ref in (dwo_ref, dwg_ref, dwp_ref, dbg_ref, dg2_ref, loss_ref):
                ref[...] = jnp.zeros_like(ref)

        so = so_ref[...]
        ao = ao_ref[...]
        g2 = g2_ref[...]
        mixed = _dot(so, wo_ref[0:512, :]) + _dot(ao, wo_ref[512:1024, :])
        r = lax.rsqrt(jnp.mean(mixed * mixed, axis=-1, keepdims=True) + EPS)
        mr = mixed * r
        h1 = x_ref[...] + mr * g2
        h1b = h1.astype(BF16)
        gate = jax.nn.sigmoid(_dot(h1b, wg_ref[...]) + bg_ref[...])
        pb = p_ref[...].astype(BF16)
        wp_blocks = [slice(j * D_PLE, (j + 1) * D_PLE) for j in range(N_CHIPS)]
        pp = jnp.concatenate([_dot(pb, wp_ref[blk, :]) for blk in wp_blocks], axis=1)
        err = h1 + gate * pp - t_ref[...]
        loss_ref[...] += 0.5 * jnp.sum(jnp.mean(err * err, axis=-1, keepdims=True), axis=0, keepdims=True)

        dh2 = err * (1.0 / D_MODEL)
        d_glin = dh2 * pp * gate * (1.0 - gate)
        d_glin_b = d_glin.astype(BF16)
        dwg_ref[...] += _dot_tn(h1b, d_glin_b)
        dbg_ref[...] += jnp.sum(d_glin, axis=0, keepdims=True)
        d_pp = (dh2 * gate).astype(BF16)
        for blk in wp_blocks:
            dwp_ref[blk, :] += _dot_tn(pb, d_pp[:, blk])
        dh1 = dh2 + _dot_nt(d_glin_b, wg_ref[...])
        dh1_ref[...] = dh1
        dg2_ref[...] += jnp.sum(dh1 * mr, axis=0, keepdims=True)
        a_ = dh1 * g2
        d_mixed = (r * a_ - mr * (r * jnp.mean(a_ * mr, axis=-1, keepdims=True))).astype(BF16)
        dwo_ref[0:512, :] += _dot_tn(so, d_mixed)
        dwo_ref[512:1024, :] += _dot_tn(ao, d_mixed)
        dso_ref[...] = _dot_nt(d_mixed, wo_ref[0:512, :])
        dao_ref[...] = _dot_nt(d_mixed, wo_ref[512:1024, :])

    return _call(
        body, name="tail_fwd_bwd", grid=(rows // tm,),
        in_specs=[_rows(tm, 512), _rows(tm, 512), _rows(tm, D_MODEL), _rows(tm, D_PLE), _rows(tm, D_MODEL),
                  _full((D_MODEL, D_MODEL)), _full((1, D_MODEL)), _full((D_MODEL, D_MODEL)), _full((1, D_MODEL)),
                  _full((N_CHIPS * D_PLE, D_PLE))],
        out_specs=[_rows(tm, D_MODEL), _rows(tm, 512), _rows(tm, 512), _full((D_MODEL, D_MODEL)),
                   _full((D_MODEL, D_MODEL)), _full((N_CHIPS * D_PLE, D_PLE)), _full((1, D_MODEL)), _full((1, D_MODEL)),
                   _full((1, 1))],
        out_shape=[_sds((rows, D_MODEL)), _sds((rows, 512)), _sds((rows, 512)), _sds((D_MODEL, D_MODEL)),
                   _sds((D_MODEL, D_MODEL)), _sds((N_CHIPS * D_PLE, D_PLE)), _sds((1, D_MODEL)), _sds((1, D_MODEL)),
                   _sds((1, 1))],
        compiler_params=_params(52, ("arbitrary",)),
    )(ssm_out, attn_out, x2d, p2d, target, w_out, g2, w_gate, b_gate, w_proj)


def _local_step(x, p, target, pre_norm_g, w_in_t, s5_params, ssm_d, w_glu, b_glu, sinks, w_out, post_norm_g, w_proj,
                w_gate, b_gate):
    n_seq, seq, _ = x.shape
    rows = n_seq * seq
    x2d = x.reshape(rows, D_MODEL)
    p2d = p.reshape(rows, D_PLE)
    t2d = target.reshape(rows, D_MODEL)

    l_re, l_im, bt_re, bt_im, cm_re, cm_im = _s5_params_fwd(*s5_params)

    u_scan, zs, q, k, v, za = _in_proj(x2d, pre_norm_g, w_in_t, n_seq, seq)
    y_scan, h_re, h_im = _s5_scan_fwd(u_scan, bt_re, bt_im, cm_re, cm_im, l_re, l_im, ssm_d, n_seq, seq)
    ssm_out = _glu_fwd(y_scan, zs, w_glu, b_glu, n_seq, seq)
    o, attn_out, lse = _attn_fwd(q, k, v, za, sinks, n_seq, seq)

    dh1, d_so, d_ao, d_w_out, d_w_gate, d_w_proj, d_b_gate, d_g2, loss = _tail(
        ssm_out, attn_out, x2d, p2d, t2d, w_out, post_norm_g, w_gate, b_gate, w_proj)

    dq, dk, dv, dza, d_sinks = _attn_bwd(q, k, v, za, o, lse, d_ao, sinks, n_seq, seq)
    dy_scan, dzs, d_w_glu, d_b_glu = _glu_bwd(y_scan, zs, d_so, w_glu, b_glu, n_seq, seq)
    du_scan, d_bt_re, d_bt_im, d_cm_re, d_cm_im, d_l_re, d_l_im, d_d = _s5_scan_bwd(
        dy_scan, u_scan, h_re, h_im, bt_re, bt_im, cm_re, cm_im, l_re, l_im, ssm_d, n_seq, seq)
    d_lam_re, d_lam_im, d_log_step, d_b_re, d_b_im, d_c_re, d_c_im = _s5_params_bwd(
        s5_params, (d_l_re, d_l_im, d_bt_re, d_bt_im, d_cm_re, d_cm_im))

    grad_x, d_w_in_t, d_g1 = _in_proj_bwd(x2d, dh1, pre_norm_g, w_in_t, du_scan, dzs, dq, dk, dv, dza, n_seq, seq)
    grads = dict(
        pre_norm_g=d_g1, w_in=d_w_in_t, ssm_lam_re=d_lam_re, ssm_lam_im=d_lam_im, ssm_log_step=d_log_step,
        ssm_b_re=d_b_re, ssm_b_im=d_b_im, ssm_c_re=d_c_re, ssm_c_im=d_c_im, ssm_d=d_d, ssm_w_glu=d_w_glu,
        ssm_b_glu=d_b_glu, attn_sinks=d_sinks, w_out=d_w_out, post_norm_g=d_g2, pl_w_proj=d_w_proj,
        pl_w_gate=d_w_gate, pl_b_gate=d_b_gate)
    return grad_x.reshape(x.shape), loss, grads


_BIG = ("w_in", "ssm_w_glu", "w_out", "pl_w_proj", "pl_w_gate")
_BIG_SHARD = {"w_in": (D_IN // N_CHIPS, D_MODEL), "ssm_w_glu": (D_SSM // N_CHIPS, D_SSM),
              "w_out": (D_MODEL // N_CHIPS, D_MODEL), "pl_w_proj": (D_PLE, D_MODEL // N_CHIPS),
              "pl_w_gate": (D_MODEL // N_CHIPS, D_MODEL)}
_SMALL = {"pre_norm_g": (1, D_MODEL), "ssm_lam_re": (SSM_GROUPS, SSM_STATE), "ssm_lam_im": (SSM_GROUPS, SSM_STATE),
          "ssm_log_step": (1, SSM_GROUPS), "ssm_b_re": (D_SSM, SSM_STATE), "ssm_b_im": (D_SSM, SSM_STATE),
          "ssm_c_re": (D_SSM, SSM_STATE), "ssm_c_im": (D_SSM, SSM_STATE), "ssm_d": (1, D_SSM), "ssm_b_glu": (1, D_SSM),
          "attn_sinks": (1, N_HEADS), "post_norm_g": (1, D_MODEL), "pl_b_gate": (1, D_MODEL)}
_VEC_ROWS = ("pre_norm_g", "post_norm_g", "pl_b_gate", "ssm_d", "ssm_b_glu", "attn_sinks", "ssm_log_step", "loss")
_SMALL_GROUPS = (
    ("vec", (8, D_MODEL), tuple((name, r) for r, name in enumerate(_VEC_ROWS))),
    ("lam", (2 * SSM_GROUPS, SSM_STATE), (("ssm_lam_re", 0), ("ssm_lam_im", SSM_GROUPS))),
    ("bc", (4 * D_SSM, SSM_STATE), (("ssm_b_re", 0), ("ssm_b_im", D_SSM), ("ssm_c_re", 2 * D_SSM),
                                    ("ssm_c_im", 3 * D_SSM))),
)
_SMALL_ORDER = tuple(name for _, _, members in _SMALL_GROUPS for name, _ in members)
_WEIGHT_ORDER = ("pre_norm_g", "w_in", "ssm_lam_re", "ssm_lam_im", "ssm_log_step", "ssm_b_re", "ssm_b_im", "ssm_c_re",
                 "ssm_c_im", "ssm_d", "ssm_w_glu", "ssm_b_glu", "attn_sinks", "w_out", "post_norm_g", "pl_w_proj",
                 "pl_w_gate", "pl_b_gate")


def _small_shape(name):
    return (1, 1) if name == "loss" else _SMALL[name]


def _to_kernel_form(name, a):
    a = a[0]
    if name == "w_in":
        return a.T
    if name in ("ssm_b_re", "ssm_b_im"):
        a = a.transpose(0, 2, 1)
    return a.reshape(_SMALL[name]) if name in _SMALL else a


def _from_kernel_form(name, a, shape):
    if name == "w_in":
        a = a.T
    if name in ("ssm_b_re", "ssm_b_im"):
        a = a.reshape(SSM_GROUPS, SSM_GROUP_CH, SSM_STATE).transpose(0, 2, 1)
    return a.reshape(shape)


def _mesh_place():
    x, y, c = lax.axis_index("x"), lax.axis_index("y"), lax.axis_index("c")
    other_chips = ((1 - x, y), (x, 1 - y), (1 - x, 1 - y))
    return x, y, c, other_chips


def _gather_copies(s_refs, g_refs, send_sems, recv_sems, local_sems):
    x, y, c, other_chips = _mesh_place()
    started = []
    for i, (s_ref, g_ref) in enumerate(zip(s_refs, g_refs)):
        rows = s_ref.shape[0]
        half = rows // 2

        def block(chip, g_ref=g_ref, rows=rows, half=half):
            return g_ref.at[pl.ds((2 * chip[0] + chip[1]) * rows + c * half, half), :]

        def copy(k, chip, to, src=None, i=i, block=block):
            return pltpu.make_async_remote_copy(
                src_ref=block(chip) if src is None else src, dst_ref=block(chip), send_sem=send_sems.at[6 * i + k],
                recv_sem=recv_sems.at[6 * i + k], device_id=to, device_id_type=MESH)

        own = pltpu.make_async_copy(s_ref, g_ref.at[pl.ds((2 * x + y) * rows, rows), :], local_sems.at[i])
        own.start()
        first = [copy(k, (x, y), (*chip, c), src=s_ref.at[pl.ds(c * half, half), :])
                 for k, chip in enumerate(other_chips)]
        for cp in first:
            cp.start()
        passed = [copy(3 + k, chip, (x, y, 1 - c)) for k, chip in enumerate(other_chips)]
        started.append((own, first, passed))
    for own, first, passed in started:
        for k in range(3):
            first[k].wait_recv()
            passed[k].start()
    for own, first, passed in started:
        for k in range(3):
            passed[k].wait_recv()
        for cp in first + passed:
            cp.wait_send()
        own.wait()


def _gather_semaphores(n_t):
    return [pltpu.SemaphoreType.DMA((6 * n_t,)), pltpu.SemaphoreType.DMA((6 * n_t,)), pltpu.SemaphoreType.DMA((n_t,))]


def _gather_weights(shards):
    n_t = len(shards)

    def body(*refs):
        _gather_copies(refs[:n_t], refs[n_t:2 * n_t], *refs[2 * n_t + 1:])
        refs[2 * n_t][...] = jnp.zeros_like(refs[2 * n_t])

    any_spec = pl.BlockSpec(memory_space=pl.ANY)
    *full, done = _call(
        body, name="gather_weights", in_specs=[any_spec] * n_t,
        out_specs=[any_spec] * n_t + [pl.BlockSpec(memory_space=pltpu.VMEM)],
        out_shape=[_sds((N_CHIPS * s.shape[0], s.shape[1]), s.dtype) for s in shards]
        + [jax.ShapeDtypeStruct((8, LANES), F32)],
        scratch_shapes=_gather_semaphores(n_t),
    )(*shards)
    return full, done[0, 0]


def _gather_weights_beside(shards):
    n_t = len(shards)
    hbm = pltpu.MemorySpace.HBM
    s_refs = [jax.new_ref(s, memory_space=hbm) for s in shards]
    g_refs = [jax.empty_ref(jax.ShapeDtypeStruct((N_CHIPS * s.shape[0], s.shape[1]), s.dtype), memory_space=hbm)
              for s in shards]

    def launch(send_sems, recv_sems, local_sems):
        x, y, c, other_chips = _mesh_place()
        peers = [(*chip, c) for chip in other_chips] + [(x, y, 1 - c)]
        barrier = pltpu.get_barrier_semaphore()
        for peer in peers:
            pl.semaphore_signal(barrier, inc=1, device_id=peer, device_id_type=MESH)
        pl.semaphore_wait(barrier, len(peers))
        _gather_copies(s_refs, g_refs, send_sems, recv_sems, local_sems)

    pl.kernel(launch, mesh=plsc.ScalarSubcoreMesh(axis_name="sequencer", num_cores=1), name="gather_weights_beside",
              scratch_types=_gather_semaphores(n_t), compiler_params=pltpu.CompilerParams(collective_id=1))()
    return [g[...] for g in g_refs]


def _exchange_grads(big, small):
    n_t = len(big)
    n_g = len(_SMALL_GROUPS)
    names = _SMALL_ORDER
    halves = [(b.shape[0] // N_CHIPS // 2, b.shape[1]) for b in big]
    n_sems = 4 * n_g + 8 * n_t

    def body(*refs):
        pos = 0

        def take(n):
            nonlocal pos
            pos += n
            return refs[pos - n:pos]

        big_refs, small_refs = take(n_t), dict(zip(names, take(len(names))))
        out_refs, small_out_refs, land_refs = take(n_t), dict(zip(names, take(len(names)))), take(n_t)
        ga, gb, pme, send_b, recv_b = take(n_t), take(n_t), take(n_t), take(n_t), take(n_t)
        s_own, s_sib, s_chips, s_pair = take(n_g), take(n_g), take(n_g), take(n_g)
        stage = dict(zip(names, take(len(names))))
        send_sems, recv_sems, local_sems = take(3)
        x, y, c, other_chips = _mesh_place()
        me = 2 * x + y
        sibling = (x, y, 1 - c)
        sem_at = iter(range(n_sems))

        def remote(src, dst, to):
            k = next(sem_at)
            return pltpu.make_async_remote_copy(src_ref=src, dst_ref=dst, send_sem=send_sems.at[k],
                                                recv_sem=recv_sems.at[k], device_id=to, device_id_type=MESH)

        loads = [pltpu.make_async_copy(small_refs[name], stage[name], local_sems.at[2 + n_t + a])
                 for a, name in enumerate(names)]
        for cp in loads:
            cp.start()
        for cp in loads:
            cp.wait()
        small_swaps = []
        for gi, (_, _, members) in enumerate(_SMALL_GROUPS):
            s_own[gi][...] = jnp.zeros_like(s_own[gi])
            for name, r0 in members:
                r, n = _small_shape(name)
                s_own[gi][r0:r0 + r, 0:n] = stage[name][...]
            small_swaps.append(remote(s_own[gi], s_sib[gi], sibling))
            small_swaps[gi].start()
        big_swaps = []
        for i in range(n_t):
            hr = halves[i][0]
            big_swaps.append([])
            for j in range(N_CHIPS):
                src = big_refs[i].at[pl.ds(j * 2 * hr + (1 - c) * hr, hr), :]
                big_swaps[i].append(remote(src, land_refs[i].at[j], sibling))
                big_swaps[i][j].start()
        small_sends = []
        for gi in range(n_g):
            small_swaps[gi].wait_recv()
            s_pair[gi][...] = s_own[gi][...] + s_sib[gi][...]
            small_sends.append([remote(s_pair[gi], s_chips[gi].at[k], (*chip, c)) for k, chip in enumerate(other_chips)])
            for cp in small_sends[gi]:
                cp.start()

        def pair_sum(i, j, dst):
            hr = halves[i][0]
            a = pltpu.make_async_copy(big_refs[i].at[pl.ds(j * 2 * hr + c * hr, hr), :], ga[i], local_sems.at[0])
            b = pltpu.make_async_copy(land_refs[i].at[j], gb[i], local_sems.at[1])
            a.start()
            b.start()
            a.wait()
            b.wait()
            dst[...] = (ga[i][...] + gb[i][...]).astype(dst.dtype)

        big_sends = []
        for i in range(n_t):
            for j in range(N_CHIPS):
                big_swaps[i][j].wait_recv()
            big_sends.append([])
            for k, chip in enumerate(other_chips):
                pair_sum(i, 2 * chip[0] + chip[1], send_b[i].at[k])
                big_sends[i].append(remote(send_b[i].at[k], recv_b[i].at[k], (*chip, c)))
                big_sends[i][k].start()
        last_swaps, keeps = [], []
        for i in range(n_t):
            hr = halves[i][0]
            pair_sum(i, me, pme[i])
            for k in range(3):
                big_sends[i][k].wait_recv()
            pme[i][...] = ((pme[i][...] + recv_b[i][0].astype(F32)) + recv_b[i][1].astype(F32)) + recv_b[i][2].astype(F32)
            mine = out_refs[i].at[pl.ds(c * hr, hr), :]
            keeps.append(pltpu.make_async_copy(pme[i], mine, local_sems.at[2 + i]))
            keeps[i].start()
            last_swaps.append(remote(pme[i], mine, sibling))
            last_swaps[i].start()

        for gi, (_, _, members) in enumerate(_SMALL_GROUPS):
            for k in range(3):
                small_sends[gi][k].wait_recv()
            total = None
            for j in range(N_CHIPS):
                rel = jnp.bitwise_xor(j, me)
                term = jnp.where(rel == 0, s_pair[gi][...], jnp.where(
                    rel == 2, s_chips[gi][0], jnp.where(rel == 1, s_chips[gi][1], s_chips[gi][2])))
                total = term if total is None else total + term
            s_sib[gi][...] = total
            for name, r0 in members:
                r, n = _small_shape(name)
                stage[name][...] = s_sib[gi][r0:r0 + r, 0:n]
        stores = [pltpu.make_async_copy(stage[name], small_out_refs[name], local_sems.at[2 + n_t + a])
                  for a, name in enumerate(names)]
        for cp in stores:
            cp.start()

        for i in range(n_t):
            last_swaps[i].wait_recv()
            keeps[i].wait()
        for cp in stores:
            cp.wait()
        for cp in (small_swaps + [cp for group in big_swaps + small_sends + big_sends for cp in group] + last_swaps):
            cp.wait_send()

    any_spec = pl.BlockSpec(memory_space=pl.ANY)
    small_shapes = [_sds(_small_shape(n)) for n in names]
    group_shapes = [shape for _, shape, _ in _SMALL_GROUPS]
    per_matrix = lambda dtype, lead=(): [pltpu.VMEM(lead + h, dtype) for h in halves]
    outs = _call(
        body, name="exchange_grads",
        in_specs=[any_spec] * (n_t + len(names)),
        out_specs=[any_spec] * (2 * n_t + len(names)),
        out_shape=([_sds((b.shape[0] // N_CHIPS, b.shape[1])) for b in big] + small_shapes
                   + [_sds((N_CHIPS,) + h) for h in halves]),
        scratch_shapes=(per_matrix(F32) + per_matrix(F32) + per_matrix(F32) + per_matrix(BF16, (3,)) + per_matrix(BF16, (3,))
                        + [pltpu.VMEM(s, F32) for s in group_shapes] * 2 + [pltpu.VMEM((3,) + s, F32) for s in group_shapes]
                        + [pltpu.VMEM(s, F32) for s in group_shapes]
                        + [pltpu.VMEM(_small_shape(n), F32) for n in names]
                        + [pltpu.SemaphoreType.DMA((n_sems,)), pltpu.SemaphoreType.DMA((n_sems,)),
                           pltpu.SemaphoreType.DMA((2 + n_t + len(names),))]),
        compiler_params=_params(48),
    )(*big, *[small[n] for n in names])
    return list(outs[:n_t]), dict(zip(names, outs[n_t:n_t + len(names)]))


def _adamw_update(w, g, m, v):
    m = ADAM_B1 * m + (1.0 - ADAM_B1) * g
    v = ADAM_B2 * v + (1.0 - ADAM_B2) * (g * g)
    m_hat = m / (1.0 - ADAM_B1 ** ADAM_STEP)
    v_hat = v / (1.0 - ADAM_B2 ** ADAM_STEP)
    return -ADAM_LR * (m_hat / (jnp.sqrt(v_hat) + ADAM_EPS) + ADAM_WD * w), m, v


def _adamw(w, g, m, v, grid, name):
    n_t = len(w)

    def body(*refs):
        ins, outs = refs[:4 * n_t], refs[4 * n_t:]
        for i in range(n_t):
            w_, g_, m_, v_ = [ins[a * n_t + i][...] for a in range(4)]
            vals = (g_,) + _adamw_update(w_, g_, m_, v_)
            for a in range(4):
                outs[a * n_t + i][...] = vals[a]

    specs = [pl.BlockSpec((a.shape[0] // grid, a.shape[1]), lambda i: (i, 0)) for a in w]
    shapes = [_sds(a.shape) for a in w]
    outs = _call(
        body, name=name, grid=(grid,), in_specs=specs * 4, out_specs=specs * 4, out_shape=shapes * 4,
        compiler_params=_params(40, ("arbitrary",)),
    )(*w, *g, *m, *v)
    return [outs[a * n_t:(a + 1) * n_t] for a in range(4)]


def kernel(x, p, pre_norm_g, w_in, ssm_lam_re, ssm_lam_im, ssm_log_step, ssm_b_re, ssm_b_im, ssm_c_re, ssm_c_im, ssm_d, ssm_w_glu, ssm_b_glu, attn_sinks, w_out, post_norm_g, pl_w_proj, pl_w_gate, pl_b_gate, loss_target, m_pre_norm_g, m_w_in, m_ssm_lam_re, m_ssm_lam_im, m_ssm_log_step, m_ssm_b_re, m_ssm_b_im, m_ssm_c_re, m_ssm_c_im, m_ssm_d, m_ssm_w_glu, m_ssm_b_glu, m_attn_sinks, m_w_out, m_post_norm_g, m_pl_w_proj, m_pl_w_gate, m_pl_b_gate, v_pre_norm_g, v_w_in, v_ssm_lam_re, v_ssm_lam_im, v_ssm_log_step, v_ssm_b_re, v_ssm_b_im, v_ssm_c_re, v_ssm_c_im, v_ssm_d, v_ssm_w_glu, v_ssm_b_glu, v_attn_sinks, v_w_out, v_post_norm_g, v_pl_w_proj, v_pl_w_gate, v_pl_b_gate):
    weights = dict(pre_norm_g=pre_norm_g, w_in=w_in, ssm_lam_re=ssm_lam_re, ssm_lam_im=ssm_lam_im,
                   ssm_log_step=ssm_log_step, ssm_b_re=ssm_b_re, ssm_b_im=ssm_b_im, ssm_c_re=ssm_c_re,
                   ssm_c_im=ssm_c_im, ssm_d=ssm_d, ssm_w_glu=ssm_w_glu, ssm_b_glu=ssm_b_glu, attn_sinks=attn_sinks,
                   w_out=w_out, post_norm_g=post_norm_g, pl_w_proj=pl_w_proj, pl_w_gate=pl_w_gate, pl_b_gate=pl_b_gate)
    m_in = dict(pre_norm_g=m_pre_norm_g, w_in=m_w_in, ssm_lam_re=m_ssm_lam_re, ssm_lam_im=m_ssm_lam_im,
                ssm_log_step=m_ssm_log_step, ssm_b_re=m_ssm_b_re, ssm_b_im=m_ssm_b_im, ssm_c_re=m_ssm_c_re,
                ssm_c_im=m_ssm_c_im, ssm_d=m_ssm_d, ssm_w_glu=m_ssm_w_glu, ssm_b_glu=m_ssm_b_glu,
                attn_sinks=m_attn_sinks, w_out=m_w_out, post_norm_g=m_post_norm_g, pl_w_proj=m_pl_w_proj,
                pl_w_gate=m_pl_w_gate, pl_b_gate=m_pl_b_gate)
    v_in = dict(pre_norm_g=v_pre_norm_g, w_in=v_w_in, ssm_lam_re=v_ssm_lam_re, ssm_lam_im=v_ssm_lam_im,
                ssm_log_step=v_ssm_log_step, ssm_b_re=v_ssm_b_re, ssm_b_im=v_ssm_b_im, ssm_c_re=v_ssm_c_re,
                ssm_c_im=v_ssm_c_im, ssm_d=v_ssm_d, ssm_w_glu=v_ssm_w_glu, ssm_b_glu=v_ssm_b_glu,
                attn_sinks=v_attn_sinks, w_out=v_w_out, post_norm_g=v_post_norm_g, pl_w_proj=v_pl_w_proj,
                pl_w_gate=v_pl_w_gate, pl_b_gate=v_pl_b_gate)

    def two_d(tree):
        return {k: _to_kernel_form(k, a) for k, a in tree.items()}

    w2, m2, v2 = two_d(weights), two_d(m_in), two_d(v_in)

    (w_in_full,), gathered = _gather_weights([w2["w_in"].astype(BF16)])
    rest = _gather_weights_beside([(w2[n] + gathered).astype(BF16) for n in _BIG[1:]])
    full = dict(zip(_BIG, [w_in_full] + rest))
    s5_params = tuple(w2[n] for n in ("ssm_lam_re", "ssm_lam_im", "ssm_log_step", "ssm_b_re", "ssm_b_im", "ssm_c_re",
                                      "ssm_c_im"))
    grad_x, loss, grads = _local_step(
        x, p, loss_target, w2["pre_norm_g"], full["w_in"], s5_params, w2["ssm_d"], full["ssm_w_glu"], w2["ssm_b_glu"],
        w2["attn_sinks"], full["w_out"], w2["post_norm_g"], full["pl_w_proj"], full["pl_w_gate"], w2["pl_b_gate"])

    g_big, g_small = _exchange_grads([grads[n] for n in _BIG], {**{n: grads[n] for n in _SMALL}, "loss": loss})
    g_big = dict(zip(_BIG, g_big))
    total_loss = g_small.pop("loss")

    big_out = _adamw([w2[n] for n in _BIG], [g_big[n] for n in _BIG], [m2[n] for n in _BIG], [v2[n] for n in _BIG],
                     8, "adamw_matrices")
    small_names = tuple(_SMALL)
    small_out = _adamw([w2[n] for n in small_names], [g_small[n] for n in small_names], [m2[n] for n in small_names],
                       [v2[n] for n in small_names], 1, "adamw_small")

    results = [{**dict(zip(_BIG, big_part)), **dict(zip(small_names, small_part))}
               for big_part, small_part in zip(big_out, small_out)]
    flat = [_from_kernel_form(name, r[name], weights[name].shape) for r in results for name in _WEIGHT_ORDER]
    return (total_loss.reshape(()), grad_x, *flat)
```

```python
import math

import jax
import jax.numpy as jnp
from jax import lax
from jax.experimental import pallas as pl
from jax.experimental.pallas import tpu as pltpu
from jax.experimental.pallas import tpu_sc as plsc

F32 = jnp.float32
BF16 = jnp.bfloat16

D_MODEL = 1024
D_SSM = 512
D_ATTN = 512
SSM_GROUPS = 32
SSM_GROUP_CH = 16
SSM_STATE = 64
SSM_LANES = SSM_GROUPS * SSM_STATE
HEAD_DIM = 64
N_HEADS = 8
KV_HEADS = 2
Q_PER_KV = 4
WINDOW = 128
BLOCK = 128
D_PLE = 256
D_IN = 2304
EPS = 1e-6
ATTN_SCALE = 1.0 / math.sqrt(HEAD_DIM)

ADAM_LR = 0.001
ADAM_B1 = 0.9
ADAM_B2 = 0.999
ADAM_EPS = 1e-08
ADAM_WD = 0.01
ADAM_STEP = 10

N_CHIPS = 4
LANES = 128
SCAN_CHUNKS = 8
SCAN_TILE_STEPS = 16
SCAN_LANE_CHUNK = 512
MIB = 2 ** 20
MESH = pl.DeviceIdType.MESH


def _dot(a, b):
    return jnp.dot(a, b, preferred_element_type=F32)


def _dot_nt(a, b):
    return lax.dot_general(a, b, (((1,), (1,)), ((), ())), preferred_element_type=F32)


def _dot_tn(a, b):
    return lax.dot_general(a, b, (((0,), (0,)), ((), ())), preferred_element_type=F32)


def _params(vmem_mib, semantics=None):
    kw = dict(vmem_limit_bytes=vmem_mib * MIB)
    if semantics is not None:
        kw["dimension_semantics"] = semantics
    return pltpu.CompilerParams(**kw)


def _full(shape):
    nd = len(shape)
    return pl.BlockSpec(shape, lambda *_: (0,) * nd, pipeline_mode=pl.Buffered(1))


def _rows(tm, width):
    return pl.BlockSpec((tm, width), lambda i: (i, 0))


def _sds(shape, dtype=F32):
    return pltpu.HBM(shape, dtype)


def _call(body, **kw):
    fn = pl.pallas_call(body, **kw)
    return lambda *args: fn(*[pltpu.with_memory_space_constraint(a, pltpu.HBM) for a in args])


def _silu(z):
    return z * jax.nn.sigmoid(z)


def _in_proj(x2d, g1, w_in_t, n_seq, seq):
    rows = x2d.shape[0]
    tm = 512
    slab, steps, _, _ = _scan_geometry(n_seq, seq)

    def body(x_ref, g_ref, w_ref, *out_refs):
        u_parts, (zs_ref, q_ref, k_ref, v_ref, za_ref) = out_refs[:_SCAN_PARTS], out_refs[_SCAN_PARTS:]
        x = x_ref[...]
        r = lax.rsqrt(jnp.mean(x * x, axis=-1, keepdims=True) + EPS)
        hn = (x * r * g_ref[...]).astype(BF16)

        def proj(a, b):
            return _dot_nt(hn, w_ref[a:b, :])

        _store_chunks(u_parts, pl.program_id(0) * (tm // steps), proj(0, 512), steps, slab)
        zs_ref[...] = proj(512, 1024)
        q_ref[...] = proj(1024, 1536).astype(BF16)
        k_ref[...] = proj(1536, 1664).astype(BF16)
        v_ref[...] = proj(1664, 1792).astype(BF16)
        za_ref[...] = proj(1792, 2304)

    *u_parts, zs, q, k, v, za = _call(
        body, name="in_proj", grid=(rows // tm,),
        in_specs=[_rows(tm, D_MODEL), _full((1, D_MODEL)), _full((D_IN, D_MODEL))],
        out_specs=_whole_parts(rows) + [_rows(tm, 512), _rows(tm, 512), _rows(tm, 128), _rows(tm, 128), _rows(tm, 512)],
        out_shape=_part_shapes(rows) + [_sds((rows, 512)), _sds((rows, 512), BF16), _sds((rows, 128), BF16),
                                        _sds((rows, 128), BF16), _sds((rows, 512))],
        compiler_params=_params(48, ("arbitrary",)),
    )(x2d, g1, w_in_t)
    return u_parts, zs, q, k, v, za


def _in_proj_bwd(x2d, dh1, g1, w_in_t, du_parts, dzs, dq, dk, dv, dza, n_seq, seq):
    rows = x2d.shape[0]
    tm = 256
    slab, steps, _, _ = _scan_geometry(n_seq, seq)
    pieces = ((0, 512), (512, 1024), (1024, 1536), (1536, 1664), (1664, 1792), (1792, 2304))

    def body(x_ref, dh1_ref, g_ref, w_ref, *refs):
        du_parts, (dzs_ref, dq_ref, dk_ref, dv_ref, dza_ref, gx_ref, dw_ref, dg_ref) = refs[:_SCAN_PARTS], refs[_SCAN_PARTS:]

        @pl.when(pl.program_id(0) == 0)
        def _():
            dw_ref[...] = jnp.zeros_like(dw_ref)
            dg_ref[...] = jnp.zeros_like(dg_ref)

        x = x_ref[...]
        g = g_ref[...]
        r = lax.rsqrt(jnp.mean(x * x, axis=-1, keepdims=True) + EPS)
        xr = x * r
        hn = (xr * g).astype(BF16)
        dhn = jnp.zeros((tm, D_MODEL), F32)
        du = _load_chunks(du_parts, pl.program_id(0) * (tm // steps), tm // steps, steps, slab)
        for (a, b), piece in zip(pieces, (du, dzs_ref[...], dq_ref[...], dk_ref[...], dv_ref[...], dza_ref[...])):
            piece = piece.astype(BF16)
            dhn = dhn + _dot(piece, w_ref[a:b, :])
            dw_ref[a:b, :] += _dot_tn(piece, hn)
        dg_ref[...] += jnp.sum(dhn * xr, axis=0, keepdims=True)
        a_ = dhn * g
        gx_ref[...] = dh1_ref[...] + r * a_ - xr * (r * jnp.mean(a_ * xr, axis=-1, keepdims=True))

    return _call(
        body, name="in_proj_bwd", grid=(rows // tm,),
        in_specs=[_rows(tm, D_MODEL), _rows(tm, D_MODEL), _full((1, D_MODEL)), _full((D_IN, D_MODEL))]
        + _whole_parts(rows) + [_rows(tm, 512), _rows(tm, 512), _rows(tm, 128), _rows(tm, 128), _rows(tm, 512)],
        out_specs=[_rows(tm, D_MODEL), _full((D_IN, D_MODEL)), _full((1, D_MODEL))],
        out_shape=[_sds((rows, D_MODEL)), _sds((D_IN, D_MODEL)), _sds((1, D_MODEL))],
        compiler_params=_params(52, ("arbitrary",)),
    )(x2d, dh1, g1, w_in_t, *du_parts, dzs, dq, dk, dv, dza)


def _iota(shape, axis):
    return lax.broadcasted_iota(jnp.int32, shape, axis)


def _exact_dot(a, b):
    return jnp.dot(a, b, precision=lax.Precision.HIGHEST, preferred_element_type=F32)


_HALF_GROUPS = SSM_GROUPS // 2
_N_SHIFT = SSM_STATE.bit_length() - 1
_P_SHIFT = SSM_GROUP_CH.bit_length() - 1


def _s5_operands(lam_re, lam_im, log_step, b_re, b_im, c_re, c_im):
    g, n, p = SSM_GROUPS, SSM_STATE, SSM_GROUP_CH
    gn, gp, hn_, hp = g * n, g * p, _HALF_GROUPS * n, _HALF_GROUPS * p
    eye_g = _iota((g, g), 0) == _iota((g, g), 1)
    step = jnp.sum(jnp.where(eye_g, jnp.exp(log_step), 0.0), axis=1, keepdims=True)
    a_re = lam_re * step
    a_im = lam_im * step
    mag = jnp.exp(a_re)
    lbar_re = mag * jnp.cos(a_im)
    lbar_im = mag * jnp.sin(a_im)
    n_re = lbar_re - 1.0
    den = lam_re * lam_re + lam_im * lam_im
    f_re = (n_re * lam_re + lbar_im * lam_im) / den
    f_im = (lbar_im * lam_re - n_re * lam_im) / den

    spread_n = (_iota((n, gn), 0) == (_iota((n, gn), 1) & (n - 1))).astype(F32)
    own_g = _iota((g, gn), 0) == (_iota((g, gn), 1) >> _N_SHIFT)

    def to_row(a):
        return jnp.sum(jnp.where(own_g, _exact_dot(a, spread_n), 0.0), axis=0, keepdims=True)

    per_group = ((_iota((gp, g), 0) >> _P_SHIFT) == _iota((gp, g), 1)).astype(F32)
    fx_re, fx_im = _exact_dot(per_group, f_re), _exact_dot(per_group, f_im)
    bbar_re = fx_re * b_re - fx_im * b_im
    bbar_im = fx_re * b_im + fx_im * b_re

    tile_n = (_iota((n, hn_), 0) == (_iota((n, hn_), 1) & (n - 1))).astype(F32)
    same_group = (_iota((hp, hn_), 0) >> _P_SHIFT) == (_iota((hp, hn_), 1) >> _N_SHIFT)

    def embed(a, hf):
        return jnp.where(same_group, _exact_dot(a[hf * hp:(hf + 1) * hp], tile_n), 0.0)

    return (to_row(lbar_re), to_row(lbar_im), embed(bbar_re, 0), embed(bbar_re, 1), embed(bbar_im, 0),
            embed(bbar_im, 1), embed(c_re, 0), embed(c_re, 1), embed(c_im, 0), embed(c_im, 1))


_S5_PARAM_SHAPES = ((SSM_GROUPS, SSM_STATE), (SSM_GROUPS, SSM_STATE), (1, SSM_GROUPS),
                    (D_SSM, SSM_STATE), (D_SSM, SSM_STATE), (D_SSM, SSM_STATE), (D_SSM, SSM_STATE))
_CM_SHAPE = (2, _HALF_GROUPS * SSM_GROUP_CH, _HALF_GROUPS * SSM_STATE)
_S5_OPERAND_SHAPES = ((1, SSM_LANES), (1, SSM_LANES), _CM_SHAPE, _CM_SHAPE, _CM_SHAPE, _CM_SHAPE)


def _s5_params_fwd(*params):
    def body(*refs):
        ins, (lre_ref, lim_ref, btre_ref, btim_ref, cmre_ref, cmim_ref) = refs[:7], refs[7:]
        vals = _s5_operands(*[r[...] for r in ins])
        lre_ref[...] = vals[0]
        lim_ref[...] = vals[1]
        for ref, pair in zip((btre_ref, btim_ref, cmre_ref, cmim_ref), (vals[2:4], vals[4:6], vals[6:8], vals[8:10])):
            ref[0] = pair[0].astype(BF16)
            ref[1] = pair[1].astype(BF16)

    dtypes = (F32, F32, BF16, BF16, BF16, BF16)
    return _call(
        body, name="s5_params_fwd",
        in_specs=[_full(s) for s in _S5_PARAM_SHAPES], out_specs=[_full(s) for s in _S5_OPERAND_SHAPES],
        out_shape=[_sds(s, d) for s, d in zip(_S5_OPERAND_SHAPES, dtypes)], compiler_params=_params(32),
    )(*params)


def _s5_params_bwd(params, cotangents):
    def body(*refs):
        ins, (dlre, dlim, dbtre, dbtim, dcmre, dcmim), outs = refs[:7], refs[7:13], refs[13:]
        _, vjp = jax.vjp(_s5_operands, *[r[...] for r in ins])
        cts = (dlre[...], dlim[...], dbtre[0], dbtre[1], dbtim[0], dbtim[1], dcmre[0], dcmre[1], dcmim[0], dcmim[1])
        for ref, val in zip(outs, vjp(cts)):
            ref[...] = val

    return _call(
        body, name="s5_params_bwd",
        in_specs=[_full(s) for s in _S5_PARAM_SHAPES + _S5_OPERAND_SHAPES],
        out_specs=[_full(s) for s in _S5_PARAM_SHAPES],
        out_shape=[_sds(s) for s in _S5_PARAM_SHAPES], compiler_params=_params(48),
    )(*params, *cotangents)


def _scan_geometry(n_seq, seq):
    slab = n_seq * SCAN_CHUNKS
    steps = seq // SCAN_CHUNKS
    tile_rows = slab * SCAN_TILE_STEPS
    n_tiles = steps // SCAN_TILE_STEPS
    return slab, steps, tile_rows, n_tiles


_SCAN_PARTS = D_SSM // LANES


def _whole_parts(rows):
    return [_full((rows, LANES))] * _SCAN_PARTS


def _part_shapes(rows):
    return [_sds((rows, LANES))] * _SCAN_PARTS


def _load_chunks(parts, first_chunk, n_chunks, steps, slab):
    return jnp.concatenate([
        jnp.concatenate([ref[pl.ds(first_chunk + q, steps, stride=slab), :] for ref in parts], axis=1)
        for q in range(n_chunks)], axis=0)


def _store_chunks(parts, first_chunk, value, steps, slab):
    for q in range(value.shape[0] // steps):
        for j, ref in enumerate(parts):
            ref[pl.ds(first_chunk + q, steps, stride=slab), :] = value[q * steps:(q + 1) * steps,
                                                                     j * LANES:(j + 1) * LANES]


def _join_parts(parts):
    return jnp.concatenate([ref[...] for ref in parts], axis=1)


def _split_parts(parts, value):
    for j, ref in enumerate(parts):
        ref[...] = value[:, j * LANES:(j + 1) * LANES]


def _complex_power(re, im, n):
    out = None
    while n:
        if n & 1:
            out = (re, im) if out is None else (out[0] * re - out[1] * im, out[0] * im + out[1] * re)
        n >>= 1
        if n:
            re, im = re * re - im * im, 2.0 * re * im
    return out


def _chunk_carry(sum_re, sum_im, carry_re, carry_im, a_re, a_im, n_seq, reverse):
    carry_re[...] = jnp.zeros_like(carry_re)
    carry_im[...] = jnp.zeros_like(carry_im)
    for s in range(n_seq):
        order = range(SCAN_CHUNKS - 2, -1, -1) if reverse else range(1, SCAN_CHUNKS)
        for c in order:
            r = s * SCAN_CHUNKS + c
            p = r + 1 if reverse else r - 1
            p_re, p_im = carry_re[p:p + 1, :], carry_im[p:p + 1, :]
            carry_re[r:r + 1, :] = a_re * p_re - a_im * p_im + sum_re[p:p + 1, :]
            carry_im[r:r + 1, :] = a_re * p_im + a_im * p_re + sum_im[p:p + 1, :]


def _s5_scan_fwd(u_parts, bt_re, bt_im, cm_re, cm_im, lbar_re, lbar_im, d_row, n_seq, seq):
    slab, steps, tile_rows, n_tiles = _scan_geometry(n_seq, seq)
    rows = u_parts[0].shape[0]

    def body(*refs):
        u_refs, refs = refs[:_SCAN_PARTS], refs[_SCAN_PARTS:]
        (bre_ref, bim_ref, cre_ref, cim_ref, lre_ref, lim_ref, d_ref), refs = refs[:7], refs[7:]
        y_refs, (hre_ref, him_ref, st_re, st_im, h0_re, h0_im, buf_re, buf_im) = refs[:_SCAN_PARTS], refs[_SCAN_PARTS:]
        second = pl.program_id(0) == 1
        i = pl.program_id(1)

        @pl.when(jnp.logical_and(i == 0, jnp.logical_not(second)))
        def _():
            st_re[...] = jnp.zeros_like(st_re)
            st_im[...] = jnp.zeros_like(st_im)

        u = _join_parts(u_refs)
        ub = u.astype(BF16)
        for hf in range(2):
            cols = slice(hf * 1024, (hf + 1) * 1024)
            buf_re[:, cols] = _dot(ub[:, hf * 256:(hf + 1) * 256], bre_ref[hf])
            buf_im[:, cols] = _dot(ub[:, hf * 256:(hf + 1) * 256], bim_ref[hf])

        for lc in range(SSM_LANES // SCAN_LANE_CHUNK):
            cols = slice(lc * SCAN_LANE_CHUNK, (lc + 1) * SCAN_LANE_CHUNK)
            l_re = jnp.broadcast_to(lre_ref[:, cols], (slab, SCAN_LANE_CHUNK))
            l_im = jnp.broadcast_to(lim_ref[:, cols], (slab, SCAN_LANE_CHUNK))

            def scan_tile(keep_states):
                def step(t, carry):
                    s_re, s_im = carry
                    r0 = pl.multiple_of(t * slab, slab)
                    n_re = l_re * s_re - l_im * s_im + buf_re[pl.ds(r0, slab), cols]
                    n_im = l_re * s_im + l_im * s_re + buf_im[pl.ds(r0, slab), cols]
                    if keep_states:
                        buf_re[pl.ds(r0, slab), cols] = n_re
                        buf_im[pl.ds(r0, slab), cols] = n_im
                    return n_re, n_im

                s_re, s_im = lax.fori_loop(0, SCAN_TILE_STEPS, step, (st_re[:, cols], st_im[:, cols]), unroll=True)
                st_re[:, cols] = s_re
                st_im[:, cols] = s_im

            pl.when(jnp.logical_not(second))(lambda: scan_tile(False))
            pl.when(second)(lambda: scan_tile(True))

        @pl.when(jnp.logical_and(i == n_tiles - 1, jnp.logical_not(second)))
        def _():
            a_re, a_im = _complex_power(lre_ref[...], lim_ref[...], steps)
            _chunk_carry(st_re, st_im, h0_re, h0_im, a_re, a_im, n_seq, reverse=False)
            st_re[...] = h0_re[...]
            st_im[...] = h0_im[...]

        @pl.when(second)
        def _():
            h_re = buf_re[...].astype(BF16)
            h_im = buf_im[...].astype(BF16)
            hre_ref[...] = h_re
            him_ref[...] = h_im
            for hf in range(2):
                cols = slice(hf * 1024, (hf + 1) * 1024)
                ycols = slice(hf * 256, (hf + 1) * 256)
                y_half = (_dot_nt(h_re[:, cols], cre_ref[hf]) - _dot_nt(h_im[:, cols], cim_ref[hf])
                          + d_ref[:, ycols] * u[:, ycols])
                _split_parts(y_refs[2 * hf:2 * hf + 2], y_half)

    tile = lambda w: pl.BlockSpec((tile_rows, w), lambda p, i: (i, 0))
    out_tile = lambda w: pl.BlockSpec((tile_rows, w), lambda p, i: (i * p, 0))
    cm = _full(_CM_SHAPE)
    outs = _call(
        body, name="s5_scan_fwd", grid=(2, n_tiles),
        in_specs=[tile(LANES)] * _SCAN_PARTS + [cm, cm, cm, cm, _full((1, SSM_LANES)), _full((1, SSM_LANES)),
                                                _full((1, 512))],
        out_specs=[out_tile(LANES)] * _SCAN_PARTS + [out_tile(SSM_LANES), out_tile(SSM_LANES)],
        out_shape=_part_shapes(rows) + [_sds((rows, SSM_LANES), BF16), _sds((rows, SSM_LANES), BF16)],
        scratch_shapes=[pltpu.VMEM((slab, SSM_LANES), F32)] * 4 + [pltpu.VMEM((tile_rows, SSM_LANES), F32)] * 2,
        compiler_params=_params(40, ("arbitrary", "arbitrary")),
    )(*u_parts, bt_re, bt_im, cm_re, cm_im, lbar_re, lbar_im, d_row)
    return outs[:_SCAN_PARTS], outs[_SCAN_PARTS], outs[_SCAN_PARTS + 1]


def _s5_scan_bwd(dy_parts, u_parts, h_re, h_im, bt_re, bt_im, cm_re, cm_im, lbar_re, lbar_im, d_row, n_seq, seq):
    slab, steps, tile_rows, n_tiles = _scan_geometry(n_seq, seq)
    rows = u_parts[0].shape[0]

    def body(*refs):
        dy_refs, u_refs, refs = refs[:_SCAN_PARTS], refs[_SCAN_PARTS:2 * _SCAN_PARTS], refs[2 * _SCAN_PARTS:]
        (hre_ref, him_ref, bre_ref, bim_ref, cre_ref, cim_ref, lre_ref, lim_ref, d_ref), refs = refs[:9], refs[9:]
        du_refs, refs = refs[:_SCAN_PARTS], refs[_SCAN_PARTS:]
        (dbre_ref, dbim_ref, dcre_ref, dcim_ref, dlre_ref, dlim_ref, dd_ref,
         st_re, st_im, g0_re, g0_im, acc_re, acc_im, buf_re, buf_im) = refs
        second = pl.program_id(0) == 1
        i = pl.program_id(1)

        @pl.when(jnp.logical_and(i == 0, jnp.logical_not(second)))
        def _():
            st_re[...] = jnp.zeros_like(st_re)
            st_im[...] = jnp.zeros_like(st_im)
            acc_re[...] = jnp.zeros_like(acc_re)
            acc_im[...] = jnp.zeros_like(acc_im)
            for ref in (dbre_ref, dbim_ref, dcre_ref, dcim_ref, dd_ref):
                ref[...] = jnp.zeros_like(ref)

        dy = _join_parts(dy_refs)
        dyb = dy.astype(BF16)
        for hf in range(2):
            cols = slice(hf * 1024, (hf + 1) * 1024)
            buf_re[:, cols] = _dot(dyb[:, hf * 256:(hf + 1) * 256], cre_ref[hf])
            buf_im[:, cols] = -_dot(dyb[:, hf * 256:(hf + 1) * 256], cim_ref[hf])

        for lc in range(SSM_LANES // SCAN_LANE_CHUNK):
            cols = slice(lc * SCAN_LANE_CHUNK, (lc + 1) * SCAN_LANE_CHUNK)
            l_re = jnp.broadcast_to(lre_ref[:, cols], (slab, SCAN_LANE_CHUNK))
            l_im = jnp.broadcast_to(lim_ref[:, cols], (slab, SCAN_LANE_CHUNK))

            def advance(r0, s_re, s_im):
                n_re = l_re * s_re + l_im * s_im + buf_re[pl.ds(r0, slab), cols]
                n_im = l_re * s_im - l_im * s_re + buf_im[pl.ds(r0, slab), cols]
                buf_re[pl.ds(r0, slab), cols] = n_re
                buf_im[pl.ds(r0, slab), cols] = n_im
                return n_re, n_im

            def row0(k):
                return pl.multiple_of((SCAN_TILE_STEPS - 1 - k) * slab, slab)

            @pl.when(jnp.logical_not(second))
            def _():
                s_re, s_im = lax.fori_loop(0, SCAN_TILE_STEPS, lambda k, s: advance(row0(k), *s),
                                           (st_re[:, cols], st_im[:, cols]), unroll=True)
                st_re[:, cols] = s_re
                st_im[:, cols] = s_im

            @pl.when(second)
            def _():
                def step(k, carry):
                    s_re, s_im, a_re, a_im = carry
                    r0 = row0(k)
                    hr = hre_ref[pl.ds(r0, slab), cols].astype(F32)
                    hi = him_ref[pl.ds(r0, slab), cols].astype(F32)
                    a_re = a_re + s_re * hr + s_im * hi
                    a_im = a_im + s_im * hr - s_re * hi
                    return advance(r0, s_re, s_im) + (a_re, a_im)

                zero = jnp.zeros((slab, SCAN_LANE_CHUNK), F32)
                s_re, s_im, a_re, a_im = lax.fori_loop(
                    0, SCAN_TILE_STEPS, step, (st_re[:, cols], st_im[:, cols], zero, zero), unroll=True)
                st_re[:, cols] = s_re
                st_im[:, cols] = s_im
                acc_re[:, cols] += a_re
                acc_im[:, cols] += a_im

        @pl.when(jnp.logical_and(i == n_tiles - 1, jnp.logical_not(second)))
        def _():
            p_re, p_im = _complex_power(lre_ref[...], lim_ref[...], steps)
            _chunk_carry(st_re, st_im, g0_re, g0_im, p_re, -p_im, n_seq, reverse=True)
            st_re[...] = g0_re[...]
            st_im[...] = g0_im[...]

        @pl.when(second)
        def _():
            u = _join_parts(u_refs)
            ub = u.astype(BF16)
            g_re = buf_re[...].astype(BF16)
            g_im = buf_im[...].astype(BF16)
            dd_ref[...] += jnp.sum(dy * u, axis=0, keepdims=True)
            for hf in range(2):
                cols = slice(hf * 1024, (hf + 1) * 1024)
                ycols = slice(hf * 256, (hf + 1) * 256)
                du_half = (_dot_nt(g_re[:, cols], bre_ref[hf]) + _dot_nt(g_im[:, cols], bim_ref[hf])
                           + d_ref[:, ycols] * dy[:, ycols])
                _split_parts(du_refs[2 * hf:2 * hf + 2], du_half)
                dbre_ref[hf] += _dot_tn(ub[:, ycols], g_re[:, cols])
                dbim_ref[hf] += _dot_tn(ub[:, ycols], g_im[:, cols])
                dcre_ref[hf] += _dot_tn(dyb[:, ycols], hre_ref[:, cols])
                dcim_ref[hf] -= _dot_tn(dyb[:, ycols], him_ref[:, cols])

        @pl.when(jnp.logical_and(i == n_tiles - 1, second))
        def _():
            dlre_ref[...] = jnp.sum(acc_re[...], axis=0, keepdims=True)
            dlim_ref[...] = jnp.sum(acc_im[...], axis=0, keepdims=True)

    tile = lambda w: pl.BlockSpec((tile_rows, w), lambda p, i: (n_tiles - 1 - i, 0))
    second_tile = lambda w: pl.BlockSpec((tile_rows, w), lambda p, i: (n_tiles - 1 - i * p, 0))
    cm = _full(_CM_SHAPE)
    row = _full((1, SSM_LANES))
    outs = _call(
        body, name="s5_scan_bwd", grid=(2, n_tiles),
        in_specs=[tile(LANES)] * _SCAN_PARTS + [second_tile(LANES)] * _SCAN_PARTS
        + [second_tile(SSM_LANES), second_tile(SSM_LANES), cm, cm, cm, cm, row, row, _full((1, 512))],
        out_specs=[second_tile(LANES)] * _SCAN_PARTS + [cm, cm, cm, cm, row, row, _full((1, 512))],
        out_shape=(_part_shapes(rows) + [_sds(_CM_SHAPE)] * 4 + [_sds((1, SSM_LANES))] * 2 + [_sds((1, 512))]),
        scratch_shapes=[pltpu.VMEM((slab, SSM_LANES), F32)] * 6 + [pltpu.VMEM((tile_rows, SSM_LANES), F32)] * 2,
        compiler_params=_params(48, ("arbitrary", "arbitrary")),
    )(*dy_parts, *u_parts, h_re, h_im, bt_re, bt_im, cm_re, cm_im, lbar_re, lbar_im, d_row)
    return (outs[:_SCAN_PARTS],) + tuple(outs[_SCAN_PARTS:])


def _glu_gate(gl, a, zs):
    return gl * jax.nn.sigmoid(a) * _silu(zs)


def _glu_fwd(y_parts, zs, w_glu, b_glu, n_seq, seq):
    rows = zs.shape[0]
    tm = 512
    slab, steps, _, _ = _scan_geometry(n_seq, seq)

    def body(*refs):
        y_refs, (zs_ref, w_ref, b_ref, o_ref) = refs[:_SCAN_PARTS], refs[_SCAN_PARTS:]
        y = _load_chunks(y_refs, pl.program_id(0) * (tm // steps), tm // steps, steps, slab)
        gl = jax.nn.gelu(y)
        a = _dot(gl.astype(BF16), w_ref[...]) + b_ref[...]
        o_ref[...] = _glu_gate(gl, a, zs_ref[...]).astype(BF16)

    return _call(
        body, name="glu_fwd", grid=(rows // tm,),
        in_specs=_whole_parts(rows) + [_rows(tm, 512), _full((512, 512)), _full((1, 512))],
        out_specs=_rows(tm, 512), out_shape=_sds((rows, 512), BF16),
        compiler_params=_params(32, ("arbitrary",)),
    )(*y_parts, zs, w_glu, b_glu)


def _glu_bwd(y_parts, zs, d_out, w_glu, b_glu, n_seq, seq):
    rows = zs.shape[0]
    tm = 512
    slab, steps, _, _ = _scan_geometry(n_seq, seq)

    def body(*refs):
        y_refs, (zs_ref, d_ref, w_ref, b_ref), refs = refs[:_SCAN_PARTS], refs[_SCAN_PARTS:_SCAN_PARTS + 4], refs[_SCAN_PARTS + 4:]
        dy_refs, (dzs_ref, dw_ref, db_ref) = refs[:_SCAN_PARTS], refs[_SCAN_PARTS:]
        first_chunk = pl.program_id(0) * (tm // steps)

        @pl.when(pl.program_id(0) == 0)
        def _():
            dw_ref[...] = jnp.zeros_like(dw_ref)
            db_ref[...] = jnp.zeros_like(db_ref)

        gl, gelu_vjp = jax.vjp(jax.nn.gelu, _load_chunks(y_refs, first_chunk, tm // steps, steps, slab))
        glb = gl.astype(BF16)
        a = _dot(glb, w_ref[...]) + b_ref[...]
        _, gate_vjp = jax.vjp(_glu_gate, gl, a, zs_ref[...])
        d_gl, d_a, d_zs = gate_vjp(d_ref[...])
        dab = d_a.astype(BF16)
        d_gl = d_gl + _dot_nt(dab, w_ref[...])
        _store_chunks(dy_refs, first_chunk, gelu_vjp(d_gl)[0], steps, slab)
        dzs_ref[...] = d_zs.astype(BF16)
        dw_ref[...] += _dot_tn(glb, dab)
        db_ref[...] += jnp.sum(d_a, axis=0, keepdims=True)

    *dy_parts, dzs, dw, db = _call(
        body, name="glu_bwd", grid=(rows // tm,),
        in_specs=_whole_parts(rows) + [_rows(tm, 512), _rows(tm, 512), _full((512, 512)), _full((1, 512))],
        out_specs=_whole_parts(rows) + [_rows(tm, 512), _full((512, 512)), _full((1, 512))],
        out_shape=_part_shapes(rows) + [_sds((rows, 512), BF16), _sds((512, 512)), _sds((1, 512))],
        compiler_params=_params(40, ("arbitrary",)),
    )(*y_parts, zs, d_out, w_glu, b_glu)
    return dy_parts, dzs, dw, db


_GROUP_ROWS = Q_PER_KV * BLOCK
_BLOCK_SHIFT = BLOCK.bit_length() - 1


def _attn_bias(j):
    row = _iota((_GROUP_ROWS, BLOCK), 0)
    dist_cur = (row & (BLOCK - 1)) - _iota((_GROUP_ROWS, BLOCK), 1)
    dist_prev = dist_cur + BLOCK
    head = row >> _BLOCK_SHIFT
    slope = jnp.zeros((_GROUP_ROWS, BLOCK), F32)
    for g in range(Q_PER_KV):
        slope = jnp.where(head == g, 2.0 ** (-(j * Q_PER_KV + g + 1)), slope)
    bias_cur = jnp.where(dist_cur >= 0, -slope * dist_cur.astype(F32), -jnp.inf)
    bias_prev = jnp.where(dist_prev < WINDOW, -slope * dist_prev.astype(F32), -jnp.inf)
    return bias_cur, bias_prev


_ATTN_BIAS_SCRATCH = pltpu.VMEM((KV_HEADS, 2, _GROUP_ROWS, BLOCK), F32)


def _fill_attn_bias(bias_ref):
    @pl.when(jnp.logical_and(pl.program_id(0) == 0, pl.program_id(1) == 0))
    def _():
        for j in range(KV_HEADS):
            bias_ref[j, 0], bias_ref[j, 1] = _attn_bias(j)


def _stack_heads(x, j):
    heads = range(j * Q_PER_KV, (j + 1) * Q_PER_KV)
    return jnp.concatenate([x[:, h * HEAD_DIM:(h + 1) * HEAD_DIM] for h in heads], axis=0)


def _stack_columns(x, j):
    heads = range(j * Q_PER_KV, (j + 1) * Q_PER_KV)
    return jnp.concatenate([jnp.broadcast_to(x[:, h:h + 1], (BLOCK, 1)) for h in heads], axis=0)


def _attn_fwd(q, k, v, za, sinks, n_seq, seq):
    nb = seq // BLOCK
    rows = q.shape[0]

    def body(q_ref, kc_ref, kp_ref, vc_ref, vp_ref, za_ref, sk_ref, o_ref, ao_ref, lse_ref, bias_ref):
        _fill_attn_bias(bias_ref)
        has_prev = pl.program_id(1) > 0
        q_all = q_ref[...]
        for j in range(KV_HEADS):
            js = slice(j * HEAD_DIM, (j + 1) * HEAD_DIM)
            bias_c, bias_p = bias_ref[j, 0], bias_ref[j, 1]
            q4 = _stack_heads(q_all, j)
            sc = _dot_nt(q4, kc_ref[:, js]) * ATTN_SCALE + bias_c
            sp = _dot_nt(q4, kp_ref[:, js]) * ATTN_SCALE + jnp.where(has_prev, bias_p, -jnp.inf)
            sink = _stack_columns(sk_ref[...], j)
            m = jnp.maximum(jnp.maximum(jnp.max(sc, axis=-1, keepdims=True), jnp.max(sp, axis=-1, keepdims=True)), sink)
            ec = jnp.exp(sc - m)
            ep = jnp.exp(sp - m)
            den = jnp.sum(ec, axis=-1, keepdims=True) + jnp.sum(ep, axis=-1, keepdims=True) + jnp.exp(sink - m)
            inv = 1.0 / den
            o4 = _dot((ec * inv).astype(BF16), vc_ref[:, js]) + _dot((ep * inv).astype(BF16), vp_ref[:, js])
            lse4 = m + jnp.log(den)
            for g in range(Q_PER_KV):
                h = j * Q_PER_KV + g
                o_ref[:, h * HEAD_DIM:(h + 1) * HEAD_DIM] = o4[g * BLOCK:(g + 1) * BLOCK]
                lse_ref[:, h:h + 1] = lse4[g * BLOCK:(g + 1) * BLOCK]
        ao_ref[...] = (o_ref[...] * _silu(za_ref[...])).astype(BF16)

    cur = lambda w: pl.BlockSpec((BLOCK, w), lambda b, n: (b * nb + n, 0))
    prev = lambda w: pl.BlockSpec((BLOCK, w), lambda b, n: (b * nb + jnp.maximum(n - 1, 0), 0))
    return _call(
        body, name="attn_fwd", grid=(n_seq, nb),
        in_specs=[cur(512), cur(128), prev(128), cur(128), prev(128), cur(512), _full((1, N_HEADS))],
        out_specs=[cur(512), cur(512), cur(N_HEADS)],
        out_shape=[_sds((rows, 512)), _sds((rows, 512), BF16), _sds((rows, N_HEADS))],
        scratch_shapes=[_ATTN_BIAS_SCRATCH], compiler_params=_params(32, ("arbitrary", "arbitrary")),
    )(q, k, k, v, v, za, sinks)


def _attn_bwd(q, k, v, za, o, lse, d_ao, sinks, n_seq, seq):
    nb = seq // BLOCK
    rows = q.shape[0]

    def body(q_ref, q2_ref, kc_ref, kp_ref, vc_ref, vp_ref, za_ref, za2_ref, o_ref, lse_ref, lse2_ref,
             d_ref, d2_ref, sk_ref, dq_ref, dk_ref, dv_ref, dza_ref, dsk_ref, bias_ref, delta_ref):
        n = nb - 1 - pl.program_id(1)
        _fill_attn_bias(bias_ref)

        @pl.when(jnp.logical_and(pl.program_id(0) == 0, pl.program_id(1) == 0))
        def _():
            dsk_ref[...] = jnp.zeros_like(dsk_ref)
            delta_ref[...] = jnp.zeros_like(delta_ref)

        has_prev = n > 0
        has_next = n + 1 < nb

        _, gate_vjp = jax.vjp(lambda o_, z_: o_ * _silu(z_), o_ref[...], za_ref[...])
        d_o, d_za = gate_vjp(d_ref[...])
        dza_ref[...] = d_za.astype(BF16)
        d_o2 = d2_ref[...] * _silu(za2_ref[...])
        q_all, q2_all = q_ref[...], q2_ref[...]
        lse_all, lse2_all = lse_ref[...], lse2_ref[...]

        for j in range(KV_HEADS):
            js = slice(j * HEAD_DIM, (j + 1) * HEAD_DIM)
            kc, kp, vc, vp = kc_ref[:, js], kp_ref[:, js], vc_ref[:, js], vp_ref[:, js]
            bias_c, bias_p = bias_ref[j, 0], bias_ref[j, 1]
            q4 = _stack_heads(q_all, j)
            do4b = _stack_heads(d_o, j).astype(BF16)
            lse4 = _stack_columns(lse_all, j)
            pc = jnp.exp(_dot_nt(q4, kc) * ATTN_SCALE + bias_c - lse4)
            pp = jnp.exp(_dot_nt(q4, kp) * ATTN_SCALE + jnp.where(has_prev, bias_p, -jnp.inf) - lse4)
            dpc = _dot_nt(do4b, vc)
            dpp = _dot_nt(do4b, vp)
            delta = jnp.sum(pc * dpc, axis=-1, keepdims=True) + jnp.sum(pp * dpp, axis=-1, keepdims=True)
            delta2 = jnp.where(has_next, delta_ref[j], 0.0)
            delta_ref[j] = delta
            dsc = (pc * (dpc - delta)).astype(BF16)
            dsp = (pp * (dpp - delta)).astype(BF16)
            dq4 = ((_dot(dsc, kc) + _dot(dsp, kp)) * ATTN_SCALE).astype(BF16)
            sink_loss = jnp.exp(_stack_columns(sk_ref[...], j) - lse4) * delta
            for g in range(Q_PER_KV):
                h = j * Q_PER_KV + g
                dq_ref[:, h * HEAD_DIM:(h + 1) * HEAD_DIM] = dq4[g * BLOCK:(g + 1) * BLOCK]
                dsk_ref[0:1, h:h + 1] -= jnp.sum(sink_loss[g * BLOCK:(g + 1) * BLOCK], axis=0, keepdims=True)
            dk = _dot_tn(dsc, q4)
            dv = _dot_tn(pc.astype(BF16), do4b)
            q4n = _stack_heads(q2_all, j)
            do4nb = _stack_heads(d_o2, j).astype(BF16)
            p2 = jnp.exp(_dot_nt(q4n, kc) * ATTN_SCALE + jnp.where(has_next, bias_p, -jnp.inf)
                         - _stack_columns(lse2_all, j))
            ds2 = (p2 * (_dot_nt(do4nb, vc) - delta2)).astype(BF16)
            dk = dk + _dot_tn(ds2, q4n)
            dv = dv + _dot_tn(p2.astype(BF16), do4nb)
            dk_ref[:, js] = (dk * ATTN_SCALE).astype(BF16)
            dv_ref[:, js] = dv.astype(BF16)

    cur = lambda w: pl.BlockSpec((BLOCK, w), lambda b, s: (b * nb + nb - 1 - s, 0))
    prev = lambda w: pl.BlockSpec((BLOCK, w), lambda b, s: (b * nb + jnp.maximum(nb - 2 - s, 0), 0))
    nxt = lambda w: pl.BlockSpec((BLOCK, w), lambda b, s: (b * nb + jnp.minimum(nb - s, nb - 1), 0))
    return _call(
        body, name="attn_bwd", grid=(n_seq, nb),
        in_specs=[cur(512), nxt(512), cur(128), prev(128), cur(128), prev(128), cur(512), nxt(512),
                  cur(512), cur(N_HEADS), nxt(N_HEADS), cur(512), nxt(512), _full((1, N_HEADS))],
        out_specs=[cur(512), cur(128), cur(128), cur(512), _full((1, N_HEADS))],
        out_shape=[_sds((rows, 512), BF16), _sds((rows, 128), BF16), _sds((rows, 128), BF16),
                   _sds((rows, 512), BF16), _sds((1, N_HEADS))],
        scratch_shapes=[_ATTN_BIAS_SCRATCH, pltpu.VMEM((KV_HEADS, _GROUP_ROWS, 1), F32)],
        compiler_params=_params(32, ("arbitrary", "arbitrary")),
    )(q, q, k, k, v, v, za, za, o, lse, lse, d_ao, d_ao, sinks)


def _tail(ssm_out, attn_out, x2d, p2d, target, w_out, g2, w_gate, b_gate, w_proj):
    rows = x2d.shape[0]
    tm = 512

    def body(so_ref, ao_ref, x_ref, p_ref, t_ref, wo_ref, g2_ref, wg_ref, bg_ref, wp_ref,
             dh1_ref, dso_ref, dao_ref, dwo_ref, dwg_ref, dwp_ref, dbg_ref, dg2_ref, loss_ref):
        @pl.when(pl.program_id(0) == 0)
        def _():
            for ref in (dwo_ref, dwg_ref, dwp_ref, dbg_ref, dg2_ref, loss_ref):
                ref[...] = jnp.zeros_like(ref)

        so = so_ref[...]
        ao = ao_ref[...]
        g2 = g2_ref[...]
        mixed = _dot(so, wo_ref[0:512, :]) + _dot(ao, wo_ref[512:1024, :])
        r = lax.rsqrt(jnp.mean(mixed * mixed, axis=-1, keepdims=True) + EPS)
        mr = mixed * r
        h1 = x_ref[...] + mr * g2
        h1b = h1.astype(BF16)
        gate = jax.nn.sigmoid(_dot(h1b, wg_ref[...]) + bg_ref[...])
        pb = p_ref[...].astype(BF16)
        wp_blocks = [slice(j * D_PLE, (j + 1) * D_PLE) for j in range(N_CHIPS)]
        pp = jnp.concatenate([_dot(pb, wp_ref[blk, :]) for blk in wp_blocks], axis=1)
        err = h1 + gate * pp - t_ref[...]
        loss_ref[...] += 0.5 * jnp.sum(jnp.mean(err * err, axis=-1, keepdims=True), axis=0, keepdims=True)

        dh2 = err * (1.0 / D_MODEL)
        d_glin = dh2 * pp * gate * (1.0 - gate)
        d_glin_b = d_glin.astype(BF16)
        dwg_ref[...] += _dot_tn(h1b, d_glin_b)
        dbg_ref[...] += jnp.sum(d_glin, axis=0, keepdims=True)
        d_pp = (dh2 * gate).astype(BF16)
        for blk in wp_blocks:
            dwp_ref[blk, :] += _dot_tn(pb, d_pp[:, blk])
        dh1 = dh2 + _dot_nt(d_glin_b, wg_ref[...])
        dh1_ref[...] = dh1
        dg2_ref[...] += jnp.sum(dh1 * mr, axis=0, keepdims=True)
        a_ = dh1 * g2
        d_mixed = (r * a_ - mr * (r * jnp.mean(a_ * mr, axis=-1, keepdims=True))).astype(BF16)
        dwo_ref[0:512, :] += _dot_tn(so, d_mixed)
        dwo_ref[512:1024, :] += _dot_tn(ao, d_mixed)
        dso_ref[...] = _dot_nt(d_mixed, wo_ref[0:512, :])
        dao_ref[...] = _dot_nt(d_mixed, wo_ref[512:1024, :])

    return _call(
        body, name="tail_fwd_bwd", grid=(rows // tm,),
        in_specs=[_rows(tm, 512), _rows(tm, 512), _rows(tm, D_MODEL), _rows(tm, D_PLE), _rows(tm, D_MODEL),
                  _full((D_MODEL, D_MODEL)), _full((1, D_MODEL)), _full((D_MODEL, D_MODEL)), _full((1, D_MODEL)),
                  _full((N_CHIPS * D_PLE, D_PLE))],
        out_specs=[_rows(tm, D_MODEL), _rows(tm, 512), _rows(tm, 512), _full((D_MODEL, D_MODEL)),
                   _full((D_MODEL, D_MODEL)), _full((N_CHIPS * D_PLE, D_PLE)), _full((1, D_MODEL)), _full((1, D_MODEL)),
                   _full((1, 1))],
        out_shape=[_sds((rows, D_MODEL)), _sds((rows, 512)), _sds((rows, 512)), _sds((D_MODEL, D_MODEL)),
                   _sds((D_MODEL, D_MODEL)), _sds((N_CHIPS * D_PLE, D_PLE)), _sds((1, D_MODEL)), _sds((1, D_MODEL)),
                   _sds((1, 1))],
        compiler_params=_params(52, ("arbitrary",)),
    )(ssm_out, attn_out, x2d, p2d, target, w_out, g2, w_gate, b_gate, w_proj)


def _local_step(x, p, target, pre_norm_g, w_in_t, s5_params, ssm_d, w_glu, b_glu, sinks, w_out, post_norm_g, w_proj,
                w_gate, b_gate):
    n_seq, seq, _ = x.shape
    rows = n_seq * seq
    x2d = x.reshape(rows, D_MODEL)
    p2d = p.reshape(rows, D_PLE)
    t2d = target.reshape(rows, D_MODEL)

    l_re, l_im, bt_re, bt_im, cm_re, cm_im = _s5_params_fwd(*s5_params)

    u_scan, zs, q, k, v, za = _in_proj(x2d, pre_norm_g, w_in_t, n_seq, seq)
    y_scan, h_re, h_im = _s5_scan_fwd(u_scan, bt_re, bt_im, cm_re, cm_im, l_re, l_im, ssm_d, n_seq, seq)
    ssm_out = _glu_fwd(y_scan, zs, w_glu, b_glu, n_seq, seq)
    o, attn_out, lse = _attn_fwd(q, k, v, za, sinks, n_seq, seq)

    dh1, d_so, d_ao, d_w_out, d_w_gate, d_w_proj, d_b_gate, d_g2, loss = _tail(
        ssm_out, attn_out, x2d, p2d, t2d, w_out, post_norm_g, w_gate, b_gate, w_proj)

    dq, dk, dv, dza, d_sinks = _attn_bwd(q, k, v, za, o, lse, d_ao, sinks, n_seq, seq)
    dy_scan, dzs, d_w_glu, d_b_glu = _glu_bwd(y_scan, zs, d_so, w_glu, b_glu, n_seq, seq)
    du_scan, d_bt_re, d_bt_im, d_cm_re, d_cm_im, d_l_re, d_l_im, d_d = _s5_scan_bwd(
        dy_scan, u_scan, h_re, h_im, bt_re, bt_im, cm_re, cm_im, l_re, l_im, ssm_d, n_seq, seq)
    d_lam_re, d_lam_im, d_log_step, d_b_re, d_b_im, d_c_re, d_c_im = _s5_params_bwd(
        s5_params, (d_l_re, d_l_im, d_bt_re, d_bt_im, d_cm_re, d_cm_im))

    grad_x, d_w_in_t, d_g1 = _in_proj_bwd(x2d, dh1, pre_norm_g, w_in_t, du_scan, dzs, dq, dk, dv, dza, n_seq, seq)
    grads = dict(
        pre_norm_g=d_g1, w_in=d_w_in_t, ssm_lam_re=d_lam_re, ssm_lam_im=d_lam_im, ssm_log_step=d_log_step,
        ssm_b_re=d_b_re, ssm_b_im=d_b_im, ssm_c_re=d_c_re, ssm_c_im=d_c_im, ssm_d=d_d, ssm_w_glu=d_w_glu,
        ssm_b_glu=d_b_glu, attn_sinks=d_sinks, w_out=d_w_out, post_norm_g=d_g2, pl_w_proj=d_w_proj,
        pl_w_gate=d_w_gate, pl_b_gate=d_b_gate)
    return grad_x.reshape(x.shape), loss, grads


_BIG = ("w_in", "ssm_w_glu", "w_out", "pl_w_proj", "pl_w_gate")
_BIG_SHARD = {"w_in": (D_IN // N_CHIPS, D_MODEL), "ssm_w_glu": (D_SSM // N_CHIPS, D_SSM),
              "w_out": (D_MODEL // N_CHIPS, D_MODEL), "pl_w_proj": (D_PLE, D_MODEL // N_CHIPS),
              "pl_w_gate": (D_MODEL // N_CHIPS, D_MODEL)}
_SMALL = {"pre_norm_g": (1, D_MODEL), "ssm_lam_re": (SSM_GROUPS, SSM_STATE), "ssm_lam_im": (SSM_GROUPS, SSM_STATE),
          "ssm_log_step": (1, SSM_GROUPS), "ssm_b_re": (D_SSM, SSM_STATE), "ssm_b_im": (D_SSM, SSM_STATE),
          "ssm_c_re": (D_SSM, SSM_STATE), "ssm_c_im": (D_SSM, SSM_STATE), "ssm_d": (1, D_SSM), "ssm_b_glu": (1, D_SSM),
          "attn_sinks": (1, N_HEADS), "post_norm_g": (1, D_MODEL), "pl_b_gate": (1, D_MODEL)}
_VEC_ROWS = ("pre_norm_g", "post_norm_g", "pl_b_gate", "ssm_d", "ssm_b_glu", "attn_sinks", "ssm_log_step", "loss")
_SMALL_GROUPS = (
    ("vec", (8, D_MODEL), tuple((name, r) for r, name in enumerate(_VEC_ROWS))),
    ("lam", (2 * SSM_GROUPS, SSM_STATE), (("ssm_lam_re", 0), ("ssm_lam_im", SSM_GROUPS))),
    ("bc", (4 * D_SSM, SSM_STATE), (("ssm_b_re", 0), ("ssm_b_im", D_SSM), ("ssm_c_re", 2 * D_SSM),
                                    ("ssm_c_im", 3 * D_SSM))),
)
_SMALL_ORDER = tuple(name for _, _, members in _SMALL_GROUPS for name, _ in members)
_WEIGHT_ORDER = ("pre_norm_g", "w_in", "ssm_lam_re", "ssm_lam_im", "ssm_log_step", "ssm_b_re", "ssm_b_im", "ssm_c_re",
                 "ssm_c_im", "ssm_d", "ssm_w_glu", "ssm_b_glu", "attn_sinks", "w_out", "post_norm_g", "pl_w_proj",
                 "pl_w_gate", "pl_b_gate")


def _small_shape(name):
    return (1, 1) if name == "loss" else _SMALL[name]


def _to_kernel_form(name, a):
    a = a[0]
    if name == "w_in":
        return a.T
    if name in ("ssm_b_re", "ssm_b_im"):
        a = a.transpose(0, 2, 1)
    return a.reshape(_SMALL[name]) if name in _SMALL else a


def _from_kernel_form(name, a, shape):
    if name == "w_in":
        a = a.T
    if name in ("ssm_b_re", "ssm_b_im"):
        a = a.reshape(SSM_GROUPS, SSM_GROUP_CH, SSM_STATE).transpose(0, 2, 1)
    return a.reshape(shape)


def _mesh_place():
    x, y, c = lax.axis_index("x"), lax.axis_index("y"), lax.axis_index("c")
    other_chips = ((1 - x, y), (x, 1 - y), (1 - x, 1 - y))
    return x, y, c, other_chips


def _gather_copies(s_refs, g_refs, send_sems, recv_sems, local_sems):
    x, y, c, other_chips = _mesh_place()
    started = []
    for i, (s_ref, g_ref) in enumerate(zip(s_refs, g_refs)):
        rows = s_ref.shape[0]
        half = rows // 2

        def block(chip, g_ref=g_ref, rows=rows, half=half):
            return g_ref.at[pl.ds((2 * chip[0] + chip[1]) * rows + c * half, half), :]

        def copy(k, chip, to, src=None, i=i, block=block):
            return pltpu.make_async_remote_copy(
                src_ref=block(chip) if src is None else src, dst_ref=block(chip), send_sem=send_sems.at[6 * i + k],
                recv_sem=recv_sems.at[6 * i + k], device_id=to, device_id_type=MESH)

        own = pltpu.make_async_copy(s_ref, g_ref.at[pl.ds((2 * x + y) * rows, rows), :], local_sems.at[i])
        own.start()
        first = [copy(k, (x, y), (*chip, c), src=s_ref.at[pl.ds(c * half, half), :])
                 for k, chip in enumerate(other_chips)]
        for cp in first:
            cp.start()
        passed = [copy(3 + k, chip, (x, y, 1 - c)) for k, chip in enumerate(other_chips)]
        started.append((own, first, passed))
    for own, first, passed in started:
        for k in range(3):
            first[k].wait_recv()
            passed[k].start()
    for own, first, passed in started:
        for k in range(3):
            passed[k].wait_recv()
        for cp in first + passed:
            cp.wait_send()
        own.wait()


def _gather_semaphores(n_t):
    return [pltpu.SemaphoreType.DMA((6 * n_t,)), pltpu.SemaphoreType.DMA((6 * n_t,)), pltpu.SemaphoreType.DMA((n_t,))]


def _gather_weights(shards):
    n_t = len(shards)

    def body(*refs):
        _gather_copies(refs[:n_t], refs[n_t:2 * n_t], *refs[2 * n_t + 1:])
        refs[2 * n_t][...] = jnp.zeros_like(refs[2 * n_t])

    any_spec = pl.BlockSpec(memory_space=pl.ANY)
    *full, done = _call(
        body, name="gather_weights", in_specs=[any_spec] * n_t,
        out_specs=[any_spec] * n_t + [pl.BlockSpec(memory_space=pltpu.VMEM)],
        out_shape=[_sds((N_CHIPS * s.shape[0], s.shape[1]), s.dtype) for s in shards]
        + [jax.ShapeDtypeStruct((8, LANES), F32)],
        scratch_shapes=_gather_semaphores(n_t),
    )(*shards)
    return full, done[0, 0]


def _gather_weights_beside(shards):
    n_t = len(shards)
    hbm = pltpu.MemorySpace.HBM
    s_refs = [jax.new_ref(s, memory_space=hbm) for s in shards]
    g_refs = [jax.empty_ref(jax.ShapeDtypeStruct((N_CHIPS * s.shape[0], s.shape[1]), s.dtype), memory_space=hbm)
              for s in shards]

    def launch(send_sems, recv_sems, local_sems):
        x, y, c, other_chips = _mesh_place()
        peers = [(*chip, c) for chip in other_chips] + [(x, y, 1 - c)]
        barrier = pltpu.get_barrier_semaphore()
        for peer in peers:
            pl.semaphore_signal(barrier, inc=1, device_id=peer, device_id_type=MESH)
        pl.semaphore_wait(barrier, len(peers))
        _gather_copies(s_refs, g_refs, send_sems, recv_sems, local_sems)

    pl.kernel(launch, mesh=plsc.ScalarSubcoreMesh(axis_name="sequencer", num_cores=1), name="gather_weights_beside",
              scratch_types=_gather_semaphores(n_t), compiler_params=pltpu.CompilerParams(collective_id=1))()
    return [g[...] for g in g_refs]


def _exchange_grads(big, small):
    n_t = len(big)
    n_g = len(_SMALL_GROUPS)
    names = _SMALL_ORDER
    halves = [(b.shape[0] // N_CHIPS // 2, b.shape[1]) for b in big]
    n_sems = 4 * n_g + 8 * n_t

    def body(*refs):
        pos = 0

        def take(n):
            nonlocal pos
            pos += n
            return refs[pos - n:pos]

        big_refs, small_refs = take(n_t), dict(zip(names, take(len(names))))
        out_refs, small_out_refs, land_refs = take(n_t), dict(zip(names, take(len(names)))), take(n_t)
        ga, gb, pme, send_b, recv_b = take(n_t), take(n_t), take(n_t), take(n_t), take(n_t)
        s_own, s_sib, s_chips, s_pair = take(n_g), take(n_g), take(n_g), take(n_g)
        stage = dict(zip(names, take(len(names))))
        send_sems, recv_sems, local_sems = take(3)
        x, y, c, other_chips = _mesh_place()
        me = 2 * x + y
        sibling = (x, y, 1 - c)
        sem_at = iter(range(n_sems))

        def remote(src, dst, to):
            k = next(sem_at)
            return pltpu.make_async_remote_copy(src_ref=src, dst_ref=dst, send_sem=send_sems.at[k],
                                                recv_sem=recv_sems.at[k], device_id=to, device_id_type=MESH)

        loads = [pltpu.make_async_copy(small_refs[name], stage[name], local_sems.at[2 + n_t + a])
                 for a, name in enumerate(names)]
        for cp in loads:
            cp.start()
        for cp in loads:
            cp.wait()
        small_swaps = []
        for gi, (_, _, members) in enumerate(_SMALL_GROUPS):
            s_own[gi][...] = jnp.zeros_like(s_own[gi])
            for name, r0 in members:
                r, n = _small_shape(name)
                s_own[gi][r0:r0 + r, 0:n] = stage[name][...]
            small_swaps.append(remote(s_own[gi], s_sib[gi], sibling))
            small_swaps[gi].start()
        big_swaps = []
        for i in range(n_t):
            hr = halves[i][0]
            big_swaps.append([])
            for j in range(N_CHIPS):
                src = big_refs[i].at[pl.ds(j * 2 * hr + (1 - c) * hr, hr), :]
                big_swaps[i].append(remote(src, land_refs[i].at[j], sibling))
                big_swaps[i][j].start()
        small_sends = []
        for gi in range(n_g):
            small_swaps[gi].wait_recv()
            s_pair[gi][...] = s_own[gi][...] + s_sib[gi][...]
            small_sends.append([remote(s_pair[gi], s_chips[gi].at[k], (*chip, c)) for k, chip in enumerate(other_chips)])
            for cp in small_sends[gi]:
                cp.start()

        def pair_sum(i, j, dst):
            hr = halves[i][0]
            a = pltpu.make_async_copy(big_refs[i].at[pl.ds(j * 2 * hr + c * hr, hr), :], ga[i], local_sems.at[0])
            b = pltpu.make_async_copy(land_refs[i].at[j], gb[i], local_sems.at[1])
            a.start()
            b.start()
            a.wait()
            b.wait()
            dst[...] = (ga[i][...] + gb[i][...]).astype(dst.dtype)

        big_sends = []
        for i in range(n_t):
            for j in range(N_CHIPS):
                big_swaps[i][j].wait_recv()
            big_sends.append([])
            for k, chip in enumerate(other_chips):
                pair_sum(i, 2 * chip[0] + chip[1], send_b[i].at[k])
                big_sends[i].append(remote(send_b[i].at[k], recv_b[i].at[k], (*chip, c)))
                big_sends[i][k].start()
        last_swaps, keeps = [], []
        for i in range(n_t):
            hr = halves[i][0]
            pair_sum(i, me, pme[i])
            for k in range(3):
                big_sends[i][k].wait_recv()
            pme[i][...] = ((pme[i][...] + recv_b[i][0].astype(F32)) + recv_b[i][1].astype(F32)) + recv_b[i][2].astype(F32)
            mine = out_refs[i].at[pl.ds(c * hr, hr), :]
            keeps.append(pltpu.make_async_copy(pme[i], mine, local_sems.at[2 + i]))
            keeps[i].start()
            last_swaps.append(remote(pme[i], mine, sibling))
            last_swaps[i].start()

        for gi, (_, _, members) in enumerate(_SMALL_GROUPS):
            for k in range(3):
                small_sends[gi][k].wait_recv()
            total = None
            for j in range(N_CHIPS):
                rel = jnp.bitwise_xor(j, me)
                term = jnp.where(rel == 0, s_pair[gi][...], jnp.where(
                    rel == 2, s_chips[gi][0], jnp.where(rel == 1, s_chips[gi][1], s_chips[gi][2])))
                total = term if total is None else total + term
            s_sib[gi][...] = total
            for name, r0 in members:
                r, n = _small_shape(name)
                stage[name][...] = s_sib[gi][r0:r0 + r, 0:n]
        stores = [pltpu.make_async_copy(stage[name], small_out_refs[name], local_sems.at[2 + n_t + a])
                  for a, name in enumerate(names)]
        for cp in stores:
            cp.start()

        for i in range(n_t):
            last_swaps[i].wait_recv()
            keeps[i].wait()
        for cp in stores:
            cp.wait()
        for cp in (small_swaps + [cp for group in big_swaps + small_sends + big_sends for cp in group] + last_swaps):
            cp.wait_send()

    any_spec = pl.BlockSpec(memory_space=pl.ANY)
    small_shapes = [_sds(_small_shape(n)) for n in names]
    group_shapes = [shape for _, shape, _ in _SMALL_GROUPS]
    per_matrix = lambda dtype, lead=(): [pltpu.VMEM(lead + h, dtype) for h in halves]
    outs = _call(
        body, name="exchange_grads",
        in_specs=[any_spec] * (n_t + len(names)),
        out_specs=[any_spec] * (2 * n_t + len(names)),
        out_shape=([_sds((b.shape[0] // N_CHIPS, b.shape[1])) for b in big] + small_shapes
                   + [_sds((N_CHIPS,) + h) for h in halves]),
        scratch_shapes=(per_matrix(F32) + per_matrix(F32) + per_matrix(F32) + per_matrix(BF16, (3,)) + per_matrix(BF16, (3,))
                        + [pltpu.VMEM(s, F32) for s in group_shapes] * 2 + [pltpu.VMEM((3,) + s, F32) for s in group_shapes]
                        + [pltpu.VMEM(s, F32) for s in group_shapes]
                        + [pltpu.VMEM(_small_shape(n), F32) for n in names]
                        + [pltpu.SemaphoreType.DMA((n_sems,)), pltpu.SemaphoreType.DMA((n_sems,)),
                           pltpu.SemaphoreType.DMA((2 + n_t + len(names),))]),
        compiler_params=_params(48),
    )(*big, *[small[n] for n in names])
    return list(outs[:n_t]), dict(zip(names, outs[n_t:n_t + len(names)]))


def _adamw_update(w, g, m, v):
    m = ADAM_B1 * m + (1.0 - ADAM_B1) * g
    v = ADAM_B2 * v + (1.0 - ADAM_B2) * (g * g)
    m_hat = m / (1.0 - ADAM_B1 ** ADAM_STEP)
    v_hat = v / (1.0 - ADAM_B2 ** ADAM_STEP)
    return -ADAM_LR * (m_hat / (jnp.sqrt(v_hat) + ADAM_EPS) + ADAM_WD * w), m, v


def _adamw(w, g, m, v, grid, name):
    n_t = len(w)

    def body(*refs):
        ins, outs = refs[:4 * n_t], refs[4 * n_t:]
        for i in range(n_t):
            w_, g_, m_, v_ = [ins[a * n_t + i][...] for a in range(4)]
            vals = (g_,) + _adamw_update(w_, g_, m_, v_)
            for a in range(4):
                outs[a * n_t + i][...] = vals[a]

    specs = [pl.BlockSpec((a.shape[0] // grid, a.shape[1]), lambda i: (i, 0)) for a in w]
    shapes = [_sds(a.shape) for a in w]
    outs = _call(
        body, name=name, grid=(grid,), in_specs=specs * 4, out_specs=specs * 4, out_shape=shapes * 4,
        compiler_params=_params(40, ("arbitrary",)),
    )(*w, *g, *m, *v)
    return [outs[a * n_t:(a + 1) * n_t] for a in range(4)]


def kernel(x, p, pre_norm_g, w_in, ssm_lam_re, ssm_lam_im, ssm_log_step, ssm_b_re, ssm_b_im, ssm_c_re, ssm_c_im, ssm_d, ssm_w_glu, ssm_b_glu, attn_sinks, w_out, post_norm_g, pl_w_proj, pl_w_gate, pl_b_gate, loss_target, m_pre_norm_g, m_w_in, m_ssm_lam_re, m_ssm_lam_im, m_ssm_log_step, m_ssm_b_re, m_ssm_b_im, m_ssm_c_re, m_ssm_c_im, m_ssm_d, m_ssm_w_glu, m_ssm_b_glu, m_attn_sinks, m_w_out, m_post_norm_g, m_pl_w_proj, m_pl_w_gate, m_pl_b_gate, v_pre_norm_g, v_w_in, v_ssm_lam_re, v_ssm_lam_im, v_ssm_log_step, v_ssm_b_re, v_ssm_b_im, v_ssm_c_re, v_ssm_c_im, v_ssm_d, v_ssm_w_glu, v_ssm_b_glu, v_attn_sinks, v_w_out, v_post_norm_g, v_pl_w_proj, v_pl_w_gate, v_pl_b_gate):
    weights = dict(pre_norm_g=pre_norm_g, w_in=w_in, ssm_lam_re=ssm_lam_re, ssm_lam_im=ssm_lam_im,
                   ssm_log_step=ssm_log_step, ssm_b_re=ssm_b_re, ssm_b_im=ssm_b_im, ssm_c_re=ssm_c_re,
                   ssm_c_im=ssm_c_im, ssm_d=ssm_d, ssm_w_glu=ssm_w_glu, ssm_b_glu=ssm_b_glu, attn_sinks=attn_sinks,
                   w_out=w_out, post_norm_g=post_norm_g, pl_w_proj=pl_w_proj, pl_w_gate=pl_w_gate, pl_b_gate=pl_b_gate)
    m_in = dict(pre_norm_g=m_pre_norm_g, w_in=m_w_in, ssm_lam_re=m_ssm_lam_re, ssm_lam_im=m_ssm_lam_im,
                ssm_log_step=m_ssm_log_step, ssm_b_re=m_ssm_b_re, ssm_b_im=m_ssm_b_im, ssm_c_re=m_ssm_c_re,
                ssm_c_im=m_ssm_c_im, ssm_d=m_ssm_d, ssm_w_glu=m_ssm_w_glu, ssm_b_glu=m_ssm_b_glu,
                attn_sinks=m_attn_sinks, w_out=m_w_out, post_norm_g=m_post_norm_g, pl_w_proj=m_pl_w_proj,
                pl_w_gate=m_pl_w_gate, pl_b_gate=m_pl_b_gate)
    v_in = dict(pre_norm_g=v_pre_norm_g, w_in=v_w_in, ssm_lam_re=v_ssm_lam_re, ssm_lam_im=v_ssm_lam_im,
                ssm_log_step=v_ssm_log_step, ssm_b_re=v_ssm_b_re, ssm_b_im=v_ssm_b_im, ssm_c_re=v_ssm_c_re,
                ssm_c_im=v_ssm_c_im, ssm_d=v_ssm_d, ssm_w_glu=v_ssm_w_glu, ssm_b_glu=v_ssm_b_glu,
                attn_sinks=v_attn_sinks, w_out=v_w_out, post_norm_g=v_post_norm_g, pl_w_proj=v_pl_w_proj,
                pl_w_gate=v_pl_w_gate, pl_b_gate=v_pl_b_gate)

    def two_d(tree):
        return {k: _to_kernel_form(k, a) for k, a in tree.items()}

    w2, m2, v2 = two_d(weights), two_d(m_in), two_d(v_in)

    (w_in_full,), gathered = _gather_weights([w2["w_in"].astype(BF16)])
    rest = _gather_weights_beside([(w2[n] + gathered).astype(BF16) for n in _BIG[1:]])
    full = dict(zip(_BIG, [w_in_full] + rest))
    s5_params = tuple(w2[n] for n in ("ssm_lam_re", "ssm_lam_im", "ssm_log_step", "ssm_b_re", "ssm_b_im", "ssm_c_re",
                                      "ssm_c_im"))
    grad_x, loss, grads = _local_step(
        x, p, loss_target, w2["pre_norm_g"], full["w_in"], s5_params, w2["ssm_d"], full["ssm_w_glu"], w2["ssm_b_glu"],
        w2["attn_sinks"], full["w_out"], w2["post_norm_g"], full["pl_w_proj"], full["pl_w_gate"], w2["pl_b_gate"])

    g_big, g_small = _exchange_grads([grads[n] for n in _BIG], {**{n: grads[n] for n in _SMALL}, "loss": loss})
    g_big = dict(zip(_BIG, g_big))
    total_loss = g_small.pop("loss")

    big_out = _adamw([w2[n] for n in _BIG], [g_big[n] for n in _BIG], [m2[n] for n in _BIG], [v2[n] for n in _BIG],
                     8, "adamw_matrices")
    small_names = tuple(_SMALL)
    small_out = _adamw([w2[n] for n in small_names], [g_small[n] for n in small_names], [m2[n] for n in small_names],
                       [v2[n] for n in small_names], 1, "adamw_small")

    results = [{**dict(zip(_BIG, big_part)), **dict(zip(small_names, small_part))}
               for big_part, small_part in zip(big_out, small_out)]
    flat = [_from_kernel_form(name, r[name], weights[name].shape) for r in results for name in _WEIGHT_ORDER]
    return (total_loss.reshape(()), grad_x, *flat)
```

```python
import math

import jax
import jax.numpy as jnp
from jax import lax
from jax.experimental import pallas as pl
from jax.experimental.pallas import tpu as pltpu
from jax.experimental.pallas import tpu_sc as plsc

F32 = jnp.float32
BF16 = jnp.bfloat16

D_MODEL = 1024
D_SSM = 512
D_ATTN = 512
SSM_GROUPS = 32
SSM_GROUP_CH = 16
SSM_STATE = 64
SSM_LANES = SSM_GROUPS * SSM_STATE
HEAD_DIM = 64
N_HEADS = 8
KV_HEADS = 2
Q_PER_KV = 4
WINDOW = 128
BLOCK = 128
D_PLE = 256
D_IN = 2304
EPS = 1e-6
ATTN_SCALE = 1.0 / math.sqrt(HEAD_DIM)

ADAM_LR = 0.001
ADAM_B1 = 0.9
ADAM_B2 = 0.999
ADAM_EPS = 1e-08
ADAM_WD = 0.01
ADAM_STEP = 10

N_CHIPS = 4
LANES = 128
SCAN_CHUNKS = 8
SCAN_TILE_STEPS = 16
SCAN_LANE_CHUNK = 512
MIB = 2 ** 20
MESH = pl.DeviceIdType.MESH


def _dot(a, b):
    return jnp.dot(a, b, preferred_element_type=F32)


def _dot_nt(a, b):
    return lax.dot_general(a, b, (((1,), (1,)), ((), ())), preferred_element_type=F32)


def _dot_tn(a, b):
    return lax.dot_general(a, b, (((0,), (0,)), ((), ())), preferred_element_type=F32)


def _params(vmem_mib, semantics=None):
    kw = dict(vmem_limit_bytes=vmem_mib * MIB)
    if semantics is not None:
        kw["dimension_semantics"] = semantics
    return pltpu.CompilerParams(**kw)


def _full(shape):
    nd = len(shape)
    return pl.BlockSpec(shape, lambda *_: (0,) * nd, pipeline_mode=pl.Buffered(1))


def _rows(tm, width):
    return pl.BlockSpec((tm, width), lambda i: (i, 0))


def _sds(shape, dtype=F32):
    return pltpu.HBM(shape, dtype)


def _call(body, **kw):
    fn = pl.pallas_call(body, **kw)
    return lambda *args: fn(*[pltpu.with_memory_space_constraint(a, pltpu.HBM) for a in args])


def _silu(z):
    return z * jax.nn.sigmoid(z)


def _in_proj(x2d, g1, w_in_t, n_seq, seq):
    rows = x2d.shape[0]
    tm = 512
    slab, steps, _, _ = _scan_geometry(n_seq, seq)

    def body(x_ref, g_ref, w_ref, *out_refs):
        u_parts, (zs_ref, q_ref, k_ref, v_ref, za_ref) = out_refs[:_SCAN_PARTS], out_refs[_SCAN_PARTS:]
        x = x_ref[...]
        r = lax.rsqrt(jnp.mean(x * x, axis=-1, keepdims=True) + EPS)
        hn = (x * r * g_ref[...]).astype(BF16)

        def proj(a, b):
            return _dot_nt(hn, w_ref[a:b, :])

        _store_chunks(u_parts, pl.program_id(0) * (tm // steps), proj(0, 512), steps, slab)
        zs_ref[...] = proj(512, 1024)
        q_ref[...] = proj(1024, 1536).astype(BF16)
        k_ref[...] = proj(1536, 1664).astype(BF16)
        v_ref[...] = proj(1664, 1792).astype(BF16)
        za_ref[...] = proj(1792, 2304)

    *u_parts, zs, q, k, v, za = _call(
        body, name="in_proj", grid=(rows // tm,),
        in_specs=[_rows(tm, D_MODEL), _full((1, D_MODEL)), _full((D_IN, D_MODEL))],
        out_specs=_whole_parts(rows) + [_rows(tm, 512), _rows(tm, 512), _rows(tm, 128), _rows(tm, 128), _rows(tm, 512)],
        out_shape=_part_shapes(rows) + [_sds((rows, 512)), _sds((rows, 512), BF16), _sds((rows, 128), BF16),
                                        _sds((rows, 128), BF16), _sds((rows, 512))],
        compiler_params=_params(48, ("arbitrary",)),
    )(x2d, g1, w_in_t)
    return u_parts, zs, q, k, v, za


def _in_proj_bwd(x2d, dh1, g1, w_in_t, du_parts, dzs, dq, dk, dv, dza, n_seq, seq):
    rows = x2d.shape[0]
    tm = 256
    slab, steps, _, _ = _scan_geometry(n_seq, seq)
    pieces = ((0, 512), (512, 1024), (1024, 1536), (1536, 1664), (1664, 1792), (1792, 2304))

    def body(x_ref, dh1_ref, g_ref, w_ref, *refs):
        du_parts, (dzs_ref, dq_ref, dk_ref, dv_ref, dza_ref, gx_ref, dw_ref, dg_ref) = refs[:_SCAN_PARTS], refs[_SCAN_PARTS:]

        @pl.when(pl.program_id(0) == 0)
        def _():
            dw_ref[...] = jnp.zeros_like(dw_ref)
            dg_ref[...] = jnp.zeros_like(dg_ref)

        x = x_ref[...]
        g = g_ref[...]
        r = lax.rsqrt(jnp.mean(x * x, axis=-1, keepdims=True) + EPS)
        xr = x * r
        hn = (xr * g).astype(BF16)
        dhn = jnp.zeros((tm, D_MODEL), F32)
        du = _load_chunks(du_parts, pl.program_id(0) * (tm // steps), tm // steps, steps, slab)
        for (a, b), piece in zip(pieces, (du, dzs_ref[...], dq_ref[...], dk_ref[...], dv_ref[...], dza_ref[...])):
            piece = piece.astype(BF16)
            dhn = dhn + _dot(piece, w_ref[a:b, :])
            dw_ref[a:b, :] += _dot_tn(piece, hn)
        dg_ref[...] += jnp.sum(dhn * xr, axis=0, keepdims=True)
        a_ = dhn * g
        gx_ref[...] = dh1_ref[...] + r * a_ - xr * (r * jnp.mean(a_ * xr, axis=-1, keepdims=True))

    return _call(
        body, name="in_proj_bwd", grid=(rows // tm,),
        in_specs=[_rows(tm, D_MODEL), _rows(tm, D_MODEL), _full((1, D_MODEL)), _full((D_IN, D_MODEL))]
        + _whole_parts(rows) + [_rows(tm, 512), _rows(tm, 512), _rows(tm, 128), _rows(tm, 128), _rows(tm, 512)],
        out_specs=[_rows(tm, D_MODEL), _full((D_IN, D_MODEL)), _full((1, D_MODEL))],
        out_shape=[_sds((rows, D_MODEL)), _sds((D_IN, D_MODEL)), _sds((1, D_MODEL))],
        compiler_params=_params(52, ("arbitrary",)),
    )(x2d, dh1, g1, w_in_t, *du_parts, dzs, dq, dk, dv, dza)


def _iota(shape, axis):
    return lax.broadcasted_iota(jnp.int32, shape, axis)


def _exact_dot(a, b):
    return jnp.dot(a, b, precision=lax.Precision.HIGHEST, preferred_element_type=F32)


_HALF_GROUPS = SSM_GROUPS // 2
_N_SHIFT = SSM_STATE.bit_length() - 1
_P_SHIFT = SSM_GROUP_CH.bit_length() - 1


def _s5_operands(lam_re, lam_im, log_step, b_re, b_im, c_re, c_im):
    g, n, p = SSM_GROUPS, SSM_STATE, SSM_GROUP_CH
    gn, gp, hn_, hp = g * n, g * p, _HALF_GROUPS * n, _HALF_GROUPS * p
    eye_g = _iota((g, g), 0) == _iota((g, g), 1)
    step = jnp.sum(jnp.where(eye_g, jnp.exp(log_step), 0.0), axis=1, keepdims=True)
    a_re = lam_re * step
    a_im = lam_im * step
    mag = jnp.exp(a_re)
    lbar_re = mag * jnp.cos(a_im)
    lbar_im = mag * jnp.sin(a_im)
    n_re = lbar_re - 1.0
    den = lam_re * lam_re + lam_im * lam_im
    f_re = (n_re * lam_re + lbar_im * lam_im) / den
    f_im = (lbar_im * lam_re - n_re * lam_im) / den

    spread_n = (_iota((n, gn), 0) == (_iota((n, gn), 1) & (n - 1))).astype(F32)
    own_g = _iota((g, gn), 0) == (_iota((g, gn), 1) >> _N_SHIFT)

    def to_row(a):
        return jnp.sum(jnp.where(own_g, _exact_dot(a, spread_n), 0.0), axis=0, keepdims=True)

    per_group = ((_iota((gp, g), 0) >> _P_SHIFT) == _iota((gp, g), 1)).astype(F32)
    fx_re, fx_im = _exact_dot(per_group, f_re), _exact_dot(per_group, f_im)
    bbar_re = fx_re * b_re - fx_im * b_im
    bbar_im = fx_re * b_im + fx_im * b_re

    tile_n = (_iota((n, hn_), 0) == (_iota((n, hn_), 1) & (n - 1))).astype(F32)
    same_group = (_iota((hp, hn_), 0) >> _P_SHIFT) == (_iota((hp, hn_), 1) >> _N_SHIFT)

    def embed(a, hf):
        return jnp.where(same_group, _exact_dot(a[hf * hp:(hf + 1) * hp], tile_n), 0.0)

    return (to_row(lbar_re), to_row(lbar_im), embed(bbar_re, 0), embed(bbar_re, 1), embed(bbar_im, 0),
            embed(bbar_im, 1), embed(c_re, 0), embed(c_re, 1), embed(c_im, 0), embed(c_im, 1))


_S5_PARAM_SHAPES = ((SSM_GROUPS, SSM_STATE), (SSM_GROUPS, SSM_STATE), (1, SSM_GROUPS),
                    (D_SSM, SSM_STATE), (D_SSM, SSM_STATE), (D_SSM, SSM_STATE), (D_SSM, SSM_STATE))
_CM_SHAPE = (2, _HALF_GROUPS * SSM_GROUP_CH, _HALF_GROUPS * SSM_STATE)
_S5_OPERAND_SHAPES = ((1, SSM_LANES), (1, SSM_LANES), _CM_SHAPE, _CM_SHAPE, _CM_SHAPE, _CM_SHAPE)


def _s5_params_fwd(*params):
    def body(*refs):
        ins, (lre_ref, lim_ref, btre_ref, btim_ref, cmre_ref, cmim_ref) = refs[:7], refs[7:]
        vals = _s5_operands(*[r[...] for r in ins])
        lre_ref[...] = vals[0]
        lim_ref[...] = vals[1]
        for ref, pair in zip((btre_ref, btim_ref, cmre_ref, cmim_ref), (vals[2:4], vals[4:6], vals[6:8], vals[8:10])):
            ref[0] = pair[0].astype(BF16)
            ref[1] = pair[1].astype(BF16)

    dtypes = (F32, F32, BF16, BF16, BF16, BF16)
    return _call(
        body, name="s5_params_fwd",
        in_specs=[_full(s) for s in _S5_PARAM_SHAPES], out_specs=[_full(s) for s in _S5_OPERAND_SHAPES],
        out_shape=[_sds(s, d) for s, d in zip(_S5_OPERAND_SHAPES, dtypes)], compiler_params=_params(32),
    )(*params)


def _s5_params_bwd(params, cotangents):
    def body(*refs):
        ins, (dlre, dlim, dbtre, dbtim, dcmre, dcmim), outs = refs[:7], refs[7:13], refs[13:]
        _, vjp = jax.vjp(_s5_operands, *[r[...] for r in ins])
        cts = (dlre[...], dlim[...], dbtre[0], dbtre[1], dbtim[0], dbtim[1], dcmre[0], dcmre[1], dcmim[0], dcmim[1])
        for ref, val in zip(outs, vjp(cts)):
            ref[...] = val

    return _call(
        body, name="s5_params_bwd",
        in_specs=[_full(s) for s in _S5_PARAM_SHAPES + _S5_OPERAND_SHAPES],
        out_specs=[_full(s) for s in _S5_PARAM_SHAPES],
        out_shape=[_sds(s) for s in _S5_PARAM_SHAPES], compiler_params=_params(48),
    )(*params, *cotangents)


def _scan_geometry(n_seq, seq):
    slab = n_seq * SCAN_CHUNKS
    steps = seq // SCAN_CHUNKS
    tile_rows = slab * SCAN_TILE_STEPS
    n_tiles = steps // SCAN_TILE_STEPS
    return slab, steps, tile_rows, n_tiles


_SCAN_PARTS = D_SSM // LANES


def _whole_parts(rows):
    return [_full((rows, LANES))] * _SCAN_PARTS


def _part_shapes(rows):
    return [_sds((rows, LANES))] * _SCAN_PARTS


def _load_chunks(parts, first_chunk, n_chunks, steps, slab):
    return jnp.concatenate([
        jnp.concatenate([ref[pl.ds(first_chunk + q, steps, stride=slab), :] for ref in parts], axis=1)
        for q in range(n_chunks)], axis=0)


def _store_chunks(parts, first_chunk, value, steps, slab):
    for q in range(value.shape[0] // steps):
        for j, ref in enumerate(parts):
            ref[pl.ds(first_chunk + q, steps, stride=slab), :] = value[q * steps:(q + 1) * steps,
                                                                     j * LANES:(j + 1) * LANES]


def _join_parts(parts):
    return jnp.concatenate([ref[...] for ref in parts], axis=1)


def _split_parts(parts, value):
    for j, ref in enumerate(parts):
        ref[...] = value[:, j * LANES:(j + 1) * LANES]


def _complex_power(re, im, n):
    out = None
    while n:
        if n & 1:
            out = (re, im) if out is None else (out[0] * re - out[1] * im, out[0] * im + out[1] * re)
        n >>= 1
        if n:
            re, im = re * re - im * im, 2.0 * re * im
    return out


def _chunk_carry(sum_re, sum_im, carry_re, carry_im, a_re, a_im, n_seq, reverse):
    carry_re[...] = jnp.zeros_like(carry_re)
    carry_im[...] = jnp.zeros_like(carry_im)
    for s in range(n_seq):
        order = range(SCAN_CHUNKS - 2, -1, -1) if reverse else range(1, SCAN_CHUNKS)
        for c in order:
            r = s * SCAN_CHUNKS + c
            p = r + 1 if reverse else r - 1
            p_re, p_im = carry_re[p:p + 1, :], carry_im[p:p + 1, :]
            carry_re[r:r + 1, :] = a_re * p_re - a_im * p_im + sum_re[p:p + 1, :]
            carry_im[r:r + 1, :] = a_re * p_im + a_im * p_re + sum_im[p:p + 1, :]


def _s5_scan_fwd(u_parts, bt_re, bt_im, cm_re, cm_im, lbar_re, lbar_im, d_row, n_seq, seq):
    slab, steps, tile_rows, n_tiles = _scan_geometry(n_seq, seq)
    rows = u_parts[0].shape[0]

    def body(*refs):
        u_refs, refs = refs[:_SCAN_PARTS], refs[_SCAN_PARTS:]
        (bre_ref, bim_ref, cre_ref, cim_ref, lre_ref, lim_ref, d_ref), refs = refs[:7], refs[7:]
        y_refs, (hre_ref, him_ref, st_re, st_im, h0_re, h0_im, buf_re, buf_im) = refs[:_SCAN_PARTS], refs[_SCAN_PARTS:]
        second = pl.program_id(0) == 1
        i = pl.program_id(1)

        @pl.when(jnp.logical_and(i == 0, jnp.logical_not(second)))
        def _():
            st_re[...] = jnp.zeros_like(st_re)
            st_im[...] = jnp.zeros_like(st_im)

        u = _join_parts(u_refs)
        ub = u.astype(BF16)
        for hf in range(2):
            cols = slice(hf * 1024, (hf + 1) * 1024)
            buf_re[:, cols] = _dot(ub[:, hf * 256:(hf + 1) * 256], bre_ref[hf])
            buf_im[:, cols] = _dot(ub[:, hf * 256:(hf + 1) * 256], bim_ref[hf])

        for lc in range(SSM_LANES // SCAN_LANE_CHUNK):
            cols = slice(lc * SCAN_LANE_CHUNK, (lc + 1) * SCAN_LANE_CHUNK)
            l_re = jnp.broadcast_to(lre_ref[:, cols], (slab, SCAN_LANE_CHUNK))
            l_im = jnp.broadcast_to(lim_ref[:, cols], (slab, SCAN_LANE_CHUNK))

            def scan_tile(keep_states):
                def step(t, carry):
                    s_re, s_im = carry
                    r0 = pl.multiple_of(t * slab, slab)
                    n_re = l_re * s_re - l_im * s_im + buf_re[pl.ds(r0, slab), cols]
                    n_im = l_re * s_im + l_im * s_re + buf_im[pl.ds(r0, slab), cols]
                    if keep_states:
                        buf_re[pl.ds(r0, slab), cols] = n_re
                        buf_im[pl.ds(r0, slab), cols] = n_im
                    return n_re, n_im

                s_re, s_im = lax.fori_loop(0, SCAN_TILE_STEPS, step, (st_re[:, cols], st_im[:, cols]), unroll=True)
                st_re[:, cols] = s_re
                st_im[:, cols] = s_im

            pl.when(jnp.logical_not(second))(lambda: scan_tile(False))
            pl.when(second)(lambda: scan_tile(True))

        @pl.when(jnp.logical_and(i == n_tiles - 1, jnp.logical_not(second)))
        def _():
            a_re, a_im = _complex_power(lre_ref[...], lim_ref[...], steps)
            _chunk_carry(st_re, st_im, h0_re, h0_im, a_re, a_im, n_seq, reverse=False)
            st_re[...] = h0_re[...]
            st_im[...] = h0_im[...]

        @pl.when(second)
        def _():
            h_re = buf_re[...].astype(BF16)
            h_im = buf_im[...].astype(BF16)
            hre_ref[...] = h_re
            him_ref[...] = h_im
            for hf in range(2):
                cols = slice(hf * 1024, (hf + 1) * 1024)
                ycols = slice(hf * 256, (hf + 1) * 256)
                y_half = (_dot_nt(h_re[:, cols], cre_ref[hf]) - _dot_nt(h_im[:, cols], cim_ref[hf])
                          + d_ref[:, ycols] * u[:, ycols])
                _split_parts(y_refs[2 * hf:2 * hf + 2], y_half)

    tile = lambda w: pl.BlockSpec((tile_rows, w), lambda p, i: (i, 0))
    out_tile = lambda w: pl.BlockSpec((tile_rows, w), lambda p, i: (i * p, 0))
    cm = _full(_CM_SHAPE)
    outs = _call(
        body, name="s5_scan_fwd", grid=(2, n_tiles),
        in_specs=[tile(LANES)] * _SCAN_PARTS + [cm, cm, cm, cm, _full((1, SSM_LANES)), _full((1, SSM_LANES)),
                                                _full((1, 512))],
        out_specs=[out_tile(LANES)] * _SCAN_PARTS + [out_tile(SSM_LANES), out_tile(SSM_LANES)],
        out_shape=_part_shapes(rows) + [_sds((rows, SSM_LANES), BF16), _sds((rows, SSM_LANES), BF16)],
        scratch_shapes=[pltpu.VMEM((slab, SSM_LANES), F32)] * 4 + [pltpu.VMEM((tile_rows, SSM_LANES), F32)] * 2,
        compiler_params=_params(40, ("arbitrary", "arbitrary")),
    )(*u_parts, bt_re, bt_im, cm_re, cm_im, lbar_re, lbar_im, d_row)
    return outs[:_SCAN_PARTS], outs[_SCAN_PARTS], outs[_SCAN_PARTS + 1]


def _s5_scan_bwd(dy_parts, u_parts, h_re, h_im, bt_re, bt_im, cm_re, cm_im, lbar_re, lbar_im, d_row, n_seq, seq):
    slab, steps, tile_rows, n_tiles = _scan_geometry(n_seq, seq)
    rows = u_parts[0].shape[0]

    def body(*refs):
        dy_refs, u_refs, refs = refs[:_SCAN_PARTS], refs[_SCAN_PARTS:2 * _SCAN_PARTS], refs[2 * _SCAN_PARTS:]
        (hre_ref, him_ref, bre_ref, bim_ref, cre_ref, cim_ref, lre_ref, lim_ref, d_ref), refs = refs[:9], refs[9:]
        du_refs, refs = refs[:_SCAN_PARTS], refs[_SCAN_PARTS:]
        (dbre_ref, dbim_ref, dcre_ref, dcim_ref, dlre_ref, dlim_ref, dd_ref,
         st_re, st_im, g0_re, g0_im, acc_re, acc_im, buf_re, buf_im) = refs
        second = pl.program_id(0) == 1
        i = pl.program_id(1)

        @pl.when(jnp.logical_and(i == 0, jnp.logical_not(second)))
        def _():
            st_re[...] = jnp.zeros_like(st_re)
            st_im[...] = jnp.zeros_like(st_im)
            acc_re[...] = jnp.zeros_like(acc_re)
            acc_im[...] = jnp.zeros_like(acc_im)
            for ref in (dbre_ref, dbim_ref, dcre_ref, dcim_ref, dd_ref):
                ref[...] = jnp.zeros_like(ref)

        dy = _join_parts(dy_refs)
        dyb = dy.astype(BF16)
        for hf in range(2):
            cols = slice(hf * 1024, (hf + 1) * 1024)
            buf_re[:, cols] = _dot(dyb[:, hf * 256:(hf + 1) * 256], cre_ref[hf])
            buf_im[:, cols] = -_dot(dyb[:, hf * 256:(hf + 1) * 256], cim_ref[hf])

        for lc in range(SSM_LANES // SCAN_LANE_CHUNK):
            cols = slice(lc * SCAN_LANE_CHUNK, (lc + 1) * SCAN_LANE_CHUNK)
            l_re = jnp.broadcast_to(lre_ref[:, cols], (slab, SCAN_LANE_CHUNK))
            l_im = jnp.broadcast_to(lim_ref[:, cols], (slab, SCAN_LANE_CHUNK))

            def advance(r0, s_re, s_im):
                n_re = l_re * s_re + l_im * s_im + buf_re[pl.ds(r0, slab), cols]
                n_im = l_re * s_im - l_im * s_re + buf_im[pl.ds(r0, slab), cols]
                buf_re[pl.ds(r0, slab), cols] = n_re
                buf_im[pl.ds(r0, slab), cols] = n_im
                return n_re, n_im

            def row0(k):
                return pl.multiple_of((SCAN_TILE_STEPS - 1 - k) * slab, slab)

            @pl.when(jnp.logical_not(second))
            def _():
                s_re, s_im = lax.fori_loop(0, SCAN_TILE_STEPS, lambda k, s: advance(row0(k), *s),
                                           (st_re[:, cols], st_im[:, cols]), unroll=True)
                st_re[:, cols] = s_re
                st_im[:, cols] = s_im

            @pl.when(second)
            def _():
                def step(k, carry):
                    s_re, s_im, a_re, a_im = carry
                    r0 = row0(k)
                    hr = hre_ref[pl.ds(r0, slab), cols].astype(F32)
                    hi = him_ref[pl.ds(r0, slab), cols].astype(F32)
                    a_re = a_re + s_re * hr + s_im * hi
                    a_im = a_im + s_im * hr - s_re * hi
                    return advance(r0, s_re, s_im) + (a_re, a_im)

                zero = jnp.zeros((slab, SCAN_LANE_CHUNK), F32)
                s_re, s_im, a_re, a_im = lax.fori_loop(
                    0, SCAN_TILE_STEPS, step, (st_re[:, cols], st_im[:, cols], zero, zero), unroll=True)
                st_re[:, cols] = s_re
                st_im[:, cols] = s_im
                acc_re[:, cols] += a_re
                acc_im[:, cols] += a_im

        @pl.when(jnp.logical_and(i == n_tiles - 1, jnp.logical_not(second)))
        def _():
            p_re, p_im = _complex_power(lre_ref[...], lim_ref[...], steps)
            _chunk_carry(st_re, st_im, g0_re, g0_im, p_re, -p_im, n_seq, reverse=True)
            st_re[...] = g0_re[...]
            st_im[...] = g0_im[...]

        @pl.when(second)
        def _():
            u = _join_parts(u_refs)
            ub = u.astype(BF16)
            g_re = buf_re[...].astype(BF16)
            g_im = buf_im[...].astype(BF16)
            dd_ref[...] += jnp.sum(dy * u, axis=0, keepdims=True)
            for hf in range(2):
                cols = slice(hf * 1024, (hf + 1) * 1024)
                ycols = slice(hf * 256, (hf + 1) * 256)
                du_half = (_dot_nt(g_re[:, cols], bre_ref[hf]) + _dot_nt(g_im[:, cols], bim_ref[hf])
                           + d_ref[:, ycols] * dy[:, ycols])
                _split_parts(du_refs[2 * hf:2 * hf + 2], du_half)
                dbre_ref[hf] += _dot_tn(ub[:, ycols], g_re[:, cols])
                dbim_ref[hf] += _dot_tn(ub[:, ycols], g_im[:, cols])
                dcre_ref[hf] += _dot_tn(dyb[:, ycols], hre_ref[:, cols])
                dcim_ref[hf] -= _dot_tn(dyb[:, ycols], him_ref[:, cols])

        @pl.when(jnp.logical_and(i == n_tiles - 1, second))
        def _():
            dlre_ref[...] = jnp.sum(acc_re[...], axis=0, keepdims=True)
            dlim_ref[...] = jnp.sum(acc_im[...], axis=0, keepdims=True)

    tile = lambda w: pl.BlockSpec((tile_rows, w), lambda p, i: (n_tiles - 1 - i, 0))
    second_tile = lambda w: pl.BlockSpec((tile_rows, w), lambda p, i: (n_tiles - 1 - i * p, 0))
    cm = _full(_CM_SHAPE)
    row = _full((1, SSM_LANES))
    outs = _call(
        body, name="s5_scan_bwd", grid=(2, n_tiles),
        in_specs=[tile(LANES)] * _SCAN_PARTS + [second_tile(LANES)] * _SCAN_PARTS
        + [second_tile(SSM_LANES), second_tile(SSM_LANES), cm, cm, cm, cm, row, row, _full((1, 512))],
        out_specs=[second_tile(LANES)] * _SCAN_PARTS + [cm, cm, cm, cm, row, row, _full((1, 512))],
        out_shape=(_part_shapes(rows) + [_sds(_CM_SHAPE)] * 4 + [_sds((1, SSM_LANES))] * 2 + [_sds((1, 512))]),
        scratch_shapes=[pltpu.VMEM((slab, SSM_LANES), F32)] * 6 + [pltpu.VMEM((tile_rows, SSM_LANES), F32)] * 2,
        compiler_params=_params(48, ("arbitrary", "arbitrary")),
    )(*dy_parts, *u_parts, h_re, h_im, bt_re, bt_im, cm_re, cm_im, lbar_re, lbar_im, d_row)
    return (outs[:_SCAN_PARTS],) + tuple(outs[_SCAN_PARTS:])


def _glu_gate(gl, a, zs):
    return gl * jax.nn.sigmoid(a) * _silu(zs)


def _glu_fwd(y_parts, zs, w_glu, b_glu, n_seq, seq):
    rows = zs.shape[0]
    tm = 512
    slab, steps, _, _ = _scan_geometry(n_seq, seq)

    def body(*refs):
        y_refs, (zs_ref, w_ref, b_ref, o_ref) = refs[:_SCAN_PARTS], refs[_SCAN_PARTS:]
        y = _load_chunks(y_refs, pl.program_id(0) * (tm // steps), tm // steps, steps, slab)
        gl = jax.nn.gelu(y)
        a = _dot(gl.astype(BF16), w_ref[...]) + b_ref[...]
        o_ref[...] = _glu_gate(gl, a, zs_ref[...]).astype(BF16)

    return _call(
        body, name="glu_fwd", grid=(rows // tm,),
        in_specs=_whole_parts(rows) + [_rows(tm, 512), _full((512, 512)), _full((1, 512))],
        out_specs=_rows(tm, 512), out_shape=_sds((rows, 512), BF16),
        compiler_params=_params(32, ("arbitrary",)),
    )(*y_parts, zs, w_glu, b_glu)


def _glu_bwd(y_parts, zs, d_out, w_glu, b_glu, n_seq, seq):
    rows = zs.shape[0]
    tm = 512
    slab, steps, _, _ = _scan_geometry(n_seq, seq)

    def body(*refs):
        y_refs, (zs_ref, d_ref, w_ref, b_ref), refs = refs[:_SCAN_PARTS], refs[_SCAN_PARTS:_SCAN_PARTS + 4], refs[_SCAN_PARTS + 4:]
        dy_refs, (dzs_ref, dw_ref, db_ref) = refs[:_SCAN_PARTS], refs[_SCAN_PARTS:]
        first_chunk = pl.program_id(0) * (tm // steps)

        @pl.when(pl.program_id(0) == 0)
        def _():
            dw_ref[...] = jnp.zeros_like(dw_ref)
            db_ref[...] = jnp.zeros_like(db_ref)

        gl, gelu_vjp = jax.vjp(jax.nn.gelu, _load_chunks(y_refs, first_chunk, tm // steps, steps, slab))
        glb = gl.astype(BF16)
        a = _dot(glb, w_ref[...]) + b_ref[...]
        _, gate_vjp = jax.vjp(_glu_gate, gl, a, zs_ref[...])
        d_gl, d_a, d_zs = gate_vjp(d_ref[...])
        dab = d_a.astype(BF16)
        d_gl = d_gl + _dot_nt(dab, w_ref[...])
        _store_chunks(dy_refs, first_chunk, gelu_vjp(d_gl)[0], steps, slab)
        dzs_ref[...] = d_zs.astype(BF16)
        dw_ref[...] += _dot_tn(glb, dab)
        db_ref[...] += jnp.sum(d_a, axis=0, keepdims=True)

    *dy_parts, dzs, dw, db = _call(
        body, name="glu_bwd", grid=(rows // tm,),
        in_specs=_whole_parts(rows) + [_rows(tm, 512), _rows(tm, 512), _full((512, 512)), _full((1, 512))],
        out_specs=_whole_parts(rows) + [_rows(tm, 512), _full((512, 512)), _full((1, 512))],
        out_shape=_part_shapes(rows) + [_sds((rows, 512), BF16), _sds((512, 512)), _sds((1, 512))],
        compiler_params=_params(40, ("arbitrary",)),
    )(*y_parts, zs, d_out, w_glu, b_glu)
    return dy_parts, dzs, dw, db


_GROUP_ROWS = Q_PER_KV * BLOCK
_BLOCK_SHIFT = BLOCK.bit_length() - 1


def _attn_bias(j):
    row = _iota((_GROUP_ROWS, BLOCK), 0)
    dist_cur = (row & (BLOCK - 1)) - _iota((_GROUP_ROWS, BLOCK), 1)
    dist_prev = dist_cur + BLOCK
    head = row >> _BLOCK_SHIFT
    slope = jnp.zeros((_GROUP_ROWS, BLOCK), F32)
    for g in range(Q_PER_KV):
        slope = jnp.where(head == g, 2.0 ** (-(j * Q_PER_KV + g + 1)), slope)
    bias_cur = jnp.where(dist_cur >= 0, -slope * dist_cur.astype(F32), -jnp.inf)
    bias_prev = jnp.where(dist_prev < WINDOW, -slope * dist_prev.astype(F32), -jnp.inf)
    return bias_cur, bias_prev


_ATTN_BIAS_SCRATCH = pltpu.VMEM((KV_HEADS, 2, _GROUP_ROWS, BLOCK), F32)


def _fill_attn_bias(bias_ref):
    @pl.when(jnp.logical_and(pl.program_id(0) == 0, pl.program_id(1) == 0))
    def _():
        for j in range(KV_HEADS):
            bias_ref[j, 0], bias_ref[j, 1] = _attn_bias(j)


def _stack_heads(x, j):
    heads = range(j * Q_PER_KV, (j + 1) * Q_PER_KV)
    return jnp.concatenate([x[:, h * HEAD_DIM:(h + 1) * HEAD_DIM] for h in heads], axis=0)


def _stack_columns(x, j):
    heads = range(j * Q_PER_KV, (j + 1) * Q_PER_KV)
    return jnp.concatenate([jnp.broadcast_to(x[:, h:h + 1], (BLOCK, 1)) for h in heads], axis=0)


def _attn_fwd(q, k, v, za, sinks, n_seq, seq):
    nb = seq // BLOCK
    rows = q.shape[0]

    def body(q_ref, kc_ref, kp_ref, vc_ref, vp_ref, za_ref, sk_ref, o_ref, ao_ref, lse_ref, bias_ref):
        _fill_attn_bias(bias_ref)
        has_prev = pl.program_id(1) > 0
        q_all = q_ref[...]
        for j in range(KV_HEADS):
            js = slice(j * HEAD_DIM, (j + 1) * HEAD_DIM)
            bias_c, bias_p = bias_ref[j, 0], bias_ref[j, 1]
            q4 = _stack_heads(q_all, j)
            sc = _dot_nt(q4, kc_ref[:, js]) * ATTN_SCALE + bias_c
            sp = _dot_nt(q4, kp_ref[:, js]) * ATTN_SCALE + jnp.where(has_prev, bias_p, -jnp.inf)
            sink = _stack_columns(sk_ref[...], j)
            m = jnp.maximum(jnp.maximum(jnp.max(sc, axis=-1, keepdims=True), jnp.max(sp, axis=-1, keepdims=True)), sink)
            ec = jnp.exp(sc - m)
            ep = jnp.exp(sp - m)
            den = jnp.sum(ec, axis=-1, keepdims=True) + jnp.sum(ep, axis=-1, keepdims=True) + jnp.exp(sink - m)
            inv = 1.0 / den
            o4 = _dot((ec * inv).astype(BF16), vc_ref[:, js]) + _dot((ep * inv).astype(BF16), vp_ref[:, js])
            lse4 = m + jnp.log(den)
            for g in range(Q_PER_KV):
                h = j * Q_PER_KV + g
                o_ref[:, h * HEAD_DIM:(h + 1) * HEAD_DIM] = o4[g * BLOCK:(g + 1) * BLOCK]
                lse_ref[:, h:h + 1] = lse4[g * BLOCK:(g + 1) * BLOCK]
        ao_ref[...] = (o_ref[...] * _silu(za_ref[...])).astype(BF16)

    cur = lambda w: pl.BlockSpec((BLOCK, w), lambda b, n: (b * nb + n, 0))
    prev = lambda w: pl.BlockSpec((BLOCK, w), lambda b, n: (b * nb + jnp.maximum(n - 1, 0), 0))
    return _call(
        body, name="attn_fwd", grid=(n_seq, nb),
        in_specs=[cur(512), cur(128), prev(128), cur(128), prev(128), cur(512), _full((1, N_HEADS))],
        out_specs=[cur(512), cur(512), cur(N_HEADS)],
        out_shape=[_sds((rows, 512)), _sds((rows, 512), BF16), _sds((rows, N_HEADS))],
        scratch_shapes=[_ATTN_BIAS_SCRATCH], compiler_params=_params(32, ("arbitrary", "arbitrary")),
    )(q, k, k, v, v, za, sinks)


def _attn_bwd(q, k, v, za, o, lse, d_ao, sinks, n_seq, seq):
    nb = seq // BLOCK
    rows = q.shape[0]

    def body(q_ref, q2_ref, kc_ref, kp_ref, vc_ref, vp_ref, za_ref, za2_ref, o_ref, lse_ref, lse2_ref,
             d_ref, d2_ref, sk_ref, dq_ref, dk_ref, dv_ref, dza_ref, dsk_ref, bias_ref, delta_ref):
        n = nb - 1 - pl.program_id(1)
        _fill_attn_bias(bias_ref)

        @pl.when(jnp.logical_and(pl.program_id(0) == 0, pl.program_id(1) == 0))
        def _():
            dsk_ref[...] = jnp.zeros_like(dsk_ref)
            delta_ref[...] = jnp.zeros_like(delta_ref)

        has_prev = n > 0
        has_next = n + 1 < nb

        _, gate_vjp = jax.vjp(lambda o_, z_: o_ * _silu(z_), o_ref[...], za_ref[...])
        d_o, d_za = gate_vjp(d_ref[...])
        dza_ref[...] = d_za.astype(BF16)
        d_o2 = d2_ref[...] * _silu(za2_ref[...])
        q_all, q2_all = q_ref[...], q2_ref[...]
        lse_all, lse2_all = lse_ref[...], lse2_ref[...]

        for j in range(KV_HEADS):
            js = slice(j * HEAD_DIM, (j + 1) * HEAD_DIM)
            kc, kp, vc, vp = kc_ref[:, js], kp_ref[:, js], vc_ref[:, js], vp_ref[:, js]
            bias_c, bias_p = bias_ref[j, 0], bias_ref[j, 1]
            q4 = _stack_heads(q_all, j)
            do4b = _stack_heads(d_o, j).astype(BF16)
            lse4 = _stack_columns(lse_all, j)
            pc = jnp.exp(_dot_nt(q4, kc) * ATTN_SCALE + bias_c - lse4)
            pp = jnp.exp(_dot_nt(q4, kp) * ATTN_SCALE + jnp.where(has_prev, bias_p, -jnp.inf) - lse4)
            dpc = _dot_nt(do4b, vc)
            dpp = _dot_nt(do4b, vp)
            delta = jnp.sum(pc * dpc, axis=-1, keepdims=True) + jnp.sum(pp * dpp, axis=-1, keepdims=True)
            delta2 = jnp.where(has_next, delta_ref[j], 0.0)
            delta_ref[j] = delta
            dsc = (pc * (dpc - delta)).astype(BF16)
            dsp = (pp * (dpp - delta)).astype(BF16)
            dq4 = ((_dot(dsc, kc) + _dot(dsp, kp)) * ATTN_SCALE).astype(BF16)
            sink_loss = jnp.exp(_stack_columns(sk_ref[...], j) - lse4) * delta
            for g in range(Q_PER_KV):
                h = j * Q_PER_KV + g
                dq_ref[:, h * HEAD_DIM:(h + 1) * HEAD_DIM] = dq4[g * BLOCK:(g + 1) * BLOCK]
                dsk_ref[0:1, h:h + 1] -= jnp.sum(sink_loss[g * BLOCK:(g + 1) * BLOCK], axis=0, keepdims=True)
            dk = _dot_tn(dsc, q4)
            dv = _dot_tn(pc.astype(BF16), do4b)
            q4n = _stack_heads(q2_all, j)
            do4nb = _stack_heads(d_o2, j).astype(BF16)
            p2 = jnp.exp(_dot_nt(q4n, kc) * ATTN_SCALE + jnp.where(has_next, bias_p, -jnp.inf)
                         - _stack_columns(lse2_all, j))
            ds2 = (p2 * (_dot_nt(do4nb, vc) - delta2)).astype(BF16)
            dk = dk + _dot_tn(ds2, q4n)
            dv = dv + _dot_tn(p2.astype(BF16), do4nb)
            dk_ref[:, js] = (dk * ATTN_SCALE).astype(BF16)
            dv_ref[:, js] = dv.astype(BF16)

    cur = lambda w: pl.BlockSpec((BLOCK, w), lambda b, s: (b * nb + nb - 1 - s, 0))
    prev = lambda w: pl.BlockSpec((BLOCK, w), lambda b, s: (b * nb + jnp.maximum(nb - 2 - s, 0), 0))
    nxt = lambda w: pl.BlockSpec((BLOCK, w), lambda b, s: (b * nb + jnp.minimum(nb - s, nb - 1), 0))
    return _call(
        body, name="attn_bwd", grid=(n_seq, nb),
        in_specs=[cur(512), nxt(512), cur(128), prev(128), cur(128), prev(128), cur(512), nxt(512),
                  cur(512), cur(N_HEADS), nxt(N_HEADS), cur(512), nxt(512), _full((1, N_HEADS))],
        out_specs=[cur(512), cur(128), cur(128), cur(512), _full((1, N_HEADS))],
        out_shape=[_sds((rows, 512), BF16), _sds((rows, 128), BF16), _sds((rows, 128), BF16),
                   _sds((rows, 512), BF16), _sds((1, N_HEADS))],
        scratch_shapes=[_ATTN_BIAS_SCRATCH, pltpu.VMEM((KV_HEADS, _GROUP_ROWS, 1), F32)],
        compiler_params=_params(32, ("arbitrary", "arbitrary")),
    )(q, q, k, k, v, v, za, za, o, lse, lse, d_ao, d_ao, sinks)


def _tail(ssm_out, attn_out, x2d, p2d, target, w_out, g2, w_gate, b_gate, w_proj):
    rows = x2d.shape[0]
    tm = 512

    def body(so_ref, ao_ref, x_ref, p_ref, t_ref, wo_ref, g2_ref, wg_ref, bg_ref, wp_ref,
             dh1_ref, dso_ref, dao_ref, dwo_ref, dwg_ref, dwp_ref, dbg_ref, dg2_ref, loss_ref):
        @pl.when(pl.program_id(0) == 0)
        def _():
            for ref in (dwo_ref, dwg_ref, dwp_ref, dbg_ref, dg2_ref, loss_ref):
                ref[...] = jnp.zeros_like(ref)

        so = so_ref[...]
        ao = ao_ref[...]
        g2 = g2_ref[...]
        mixed = _dot(so, wo_ref[0:512, :]) + _dot(ao, wo_ref[512:1024, :])
        r = lax.rsqrt(jnp.mean(mixed * mixed, axis=-1, keepdims=True) + EPS)
        mr = mixed * r
        h1 = x_ref[...] + mr * g2
        h1b = h1.astype(BF16)
        gate = jax.nn.sigmoid(_dot(h1b, wg_ref[...]) + bg_ref[...])
        pb = p_ref[...].astype(BF16)
        wp_blocks = [slice(j * D_PLE, (j + 1) * D_PLE) for j in range(N_CHIPS)]
        pp = jnp.concatenate([_dot(pb, wp_ref[blk, :]) for blk in wp_blocks], axis=1)
        err = h1 + gate * pp - t_ref[...]
        loss_ref[...] += 0.5 * jnp.sum(jnp.mean(err * err, axis=-1, keepdims=True), axis=0, keepdims=True)

        dh2 = err * (1.0 / D_MODEL)
        d_glin = dh2 * pp * gate * (1.0 - gate)
        d_glin_b = d_glin.astype(BF16)
        dwg_ref[...] += _dot_tn(h1b, d_glin_b)
        dbg_ref[...] += jnp.sum(d_glin, axis=0, keepdims=True)
        d_pp = (dh2 * gate).astype(BF16)
        for blk in wp_blocks:
            dwp_ref[blk, :] += _dot_tn(pb, d_pp[:, blk])
        dh1 = dh2 + _dot_nt(d_glin_b, wg_ref[...])
        dh1_ref[...] = dh1
        dg2_ref[...] += jnp.sum(dh1 * mr, axis=0, keepdims=True)
        a_ = dh1 * g2
        d_mixed = (r * a_ - mr * (r * jnp.mean(a_ * mr, axis=-1, keepdims=True))).astype(BF16)
        dwo_ref[0:512, :] += _dot_tn(so, d_mixed)
        dwo_ref[512:1024, :] += _dot_tn(ao, d_mixed)
        dso_ref[...] = _dot_nt(d_mixed, wo_ref[0:512, :])
        dao_ref[...] = _dot_nt(d_mixed, wo_ref[512:1024, :])

    return _call(
        body, name="tail_fwd_bwd", grid=(rows // tm,),
        in_specs=[_rows(tm, 512), _rows(tm, 512), _rows(tm, D_MODEL), _rows(tm, D_PLE), _rows(tm, D_MODEL),
                  _full((D_MODEL, D_MODEL)), _full((1, D_MODEL)), _full((D_MODEL, D_MODEL)), _full((1, D_MODEL)),
                  _full((N_CHIPS * D_PLE, D_PLE))],
        out_specs=[_rows(tm, D_MODEL), _rows(tm, 512), _rows(tm, 512), _full((D_MODEL, D_MODEL)),
                   _full((D_MODEL, D_MODEL)), _full((N_CHIPS * D_PLE, D_PLE)), _full((1, D_MODEL)), _full((1, D_MODEL)),
                   _full((1, 1))],
        out_shape=[_sds((rows, D_MODEL)), _sds((rows, 512)), _sds((rows, 512)), _sds((D_MODEL, D_MODEL)),
                   _sds((D_MODEL, D_MODEL)), _sds((N_CHIPS * D_PLE, D_PLE)), _sds((1, D_MODEL)), _sds((1, D_MODEL)),
                   _sds((1, 1))],
        compiler_params=_params(52, ("arbitrary",)),
    )(ssm_out, attn_out, x2d, p2d, target, w_out, g2, w_gate, b_gate, w_proj)


def _local_step(x, p, target, pre_norm_g, w_in_t, s5_params, ssm_d, w_glu, b_glu, sinks, w_out, post_norm_g, w_proj,
                w_gate, b_gate):
    n_seq, seq, _ = x.shape
    rows = n_seq * seq
    x2d = x.reshape(rows, D_MODEL)
    p2d = p.reshape(rows, D_PLE)
    t2d = target.reshape(rows, D_MODEL)

    l_re, l_im, bt_re, bt_im, cm_re, cm_im = _s5_params_fwd(*s5_params)

    u_scan, zs, q, k, v, za = _in_proj(x2d, pre_norm_g, w_in_t, n_seq, seq)
    y_scan, h_re, h_im = _s5_scan_fwd(u_scan, bt_re, bt_im, cm_re, cm_im, l_re, l_im, ssm_d, n_seq, seq)
    ssm_out = _glu_fwd(y_scan, zs, w_glu, b_glu, n_seq, seq)
    o, attn_out, lse = _attn_fwd(q, k, v, za, sinks, n_seq, seq)

    dh1, d_so, d_ao, d_w_out, d_w_gate, d_w_proj, d_b_gate, d_g2, loss = _tail(
        ssm_out, attn_out, x2d, p2d, t2d, w_out, post_norm_g, w_gate, b_gate, w_proj)

    dq, dk, dv, dza, d_sinks = _attn_bwd(q, k, v, za, o, lse, d_ao, sinks, n_seq, seq)
    dy_scan, dzs, d_w_glu, d_b_glu = _glu_bwd(y_scan, zs, d_so, w_glu, b_glu, n_seq, seq)
    du_scan, d_bt_re, d_bt_im, d_cm_re, d_cm_im, d_l_re, d_l_im, d_d = _s5_scan_bwd(
        dy_scan, u_scan, h_re, h_im, bt_re, bt_im, cm_re, cm_im, l_re, l_im, ssm_d, n_seq, seq)
    d_lam_re, d_lam_im, d_log_step, d_b_re, d_b_im, d_c_re, d_c_im = _s5_params_bwd(
        s5_params, (d_l_re, d_l_im, d_bt_re, d_bt_im, d_cm_re, d_cm_im))

    grad_x, d_w_in_t, d_g1 = _in_proj_bwd(x2d, dh1, pre_norm_g, w_in_t, du_scan, dzs, dq, dk, dv, dza, n_seq, seq)
    grads = dict(
        pre_norm_g=d_g1, w_in=d_w_in_t, ssm_lam_re=d_lam_re, ssm_lam_im=d_lam_im, ssm_log_step=d_log_step,
        ssm_b_re=d_b_re, ssm_b_im=d_b_im, ssm_c_re=d_c_re, ssm_c_im=d_c_im, ssm_d=d_d, ssm_w_glu=d_w_glu,
        ssm_b_glu=d_b_glu, attn_sinks=d_sinks, w_out=d_w_out, post_norm_g=d_g2, pl_w_proj=d_w_proj,
        pl_w_gate=d_w_gate, pl_b_gate=d_b_gate)
    return grad_x.reshape(x.shape), loss, grads


_BIG = ("w_in", "ssm_w_glu", "w_out", "pl_w_proj", "pl_w_gate")
_BIG_SHARD = {"w_in": (D_IN // N_CHIPS, D_MODEL), "ssm_w_glu": (D_SSM // N_CHIPS, D_SSM),
              "w_out": (D_MODEL // N_CHIPS, D_MODEL), "pl_w_proj": (D_PLE, D_MODEL // N_CHIPS),
              "pl_w_gate": (D_MODEL // N_CHIPS, D_MODEL)}
_SMALL = {"pre_norm_g": (1, D_MODEL), "ssm_lam_re": (SSM_GROUPS, SSM_STATE), "ssm_lam_im": (SSM_GROUPS, SSM_STATE),
          "ssm_log_step": (1, SSM_GROUPS), "ssm_b_re": (D_SSM, SSM_STATE), "ssm_b_im": (D_SSM, SSM_STATE),
          "ssm_c_re": (D_SSM, SSM_STATE), "ssm_c_im": (D_SSM, SSM_STATE), "ssm_d": (1, D_SSM), "ssm_b_glu": (1, D_SSM),
          "attn_sinks": (1, N_HEADS), "post_norm_g": (1, D_MODEL), "pl_b_gate": (1, D_MODEL)}
_VEC_ROWS = ("pre_norm_g", "post_norm_g", "pl_b_gate", "ssm_d", "ssm_b_glu", "attn_sinks", "ssm_log_step", "loss")
_SMALL_GROUPS = (
    ("vec", (8, D_MODEL), tuple((name, r) for r, name in enumerate(_VEC_ROWS))),
    ("lam", (2 * SSM_GROUPS, SSM_STATE), (("ssm_lam_re", 0), ("ssm_lam_im", SSM_GROUPS))),
    ("bc", (4 * D_SSM, SSM_STATE), (("ssm_b_re", 0), ("ssm_b_im", D_SSM), ("ssm_c_re", 2 * D_SSM),
                                    ("ssm_c_im", 3 * D_SSM))),
)
_SMALL_ORDER = tuple(name for _, _, members in _SMALL_GROUPS for name, _ in members)
_WEIGHT_ORDER = ("pre_norm_g", "w_in", "ssm_lam_re", "ssm_lam_im", "ssm_log_step", "ssm_b_re", "ssm_b_im", "ssm_c_re",
                 "ssm_c_im", "ssm_d", "ssm_w_glu", "ssm_b_glu", "attn_sinks", "w_out", "post_norm_g", "pl_w_proj",
                 "pl_w_gate", "pl_b_gate")


def _small_shape(name):
    return (1, 1) if name == "loss" else _SMALL[name]


def _to_kernel_form(name, a):
    a = a[0]
    if name == "w_in":
        return a.T
    if name in ("ssm_b_re", "ssm_b_im"):
        a = a.transpose(0, 2, 1)
    return a.reshape(_SMALL[name]) if name in _SMALL else a


def _from_kernel_form(name, a, shape):
    if name == "w_in":
        a = a.T
    if name in ("ssm_b_re", "ssm_b_im"):
        a = a.reshape(SSM_GROUPS, SSM_GROUP_CH, SSM_STATE).transpose(0, 2, 1)
    return a.reshape(shape)


def _mesh_place():
    x, y, c = lax.axis_index("x"), lax.axis_index("y"), lax.axis_index("c")
    other_chips = ((1 - x, y), (x, 1 - y), (1 - x, 1 - y))
    return x, y, c, other_chips


def _gather_copies(s_refs, g_refs, send_sems, recv_sems, local_sems):
    x, y, c, other_chips = _mesh_place()
    started = []
    for i, (s_ref, g_ref) in enumerate(zip(s_refs, g_refs)):
        rows = s_ref.shape[0]
        half = rows // 2

        def block(chip, g_ref=g_ref, rows=rows, half=half):
            return g_ref.at[pl.ds((2 * chip[0] + chip[1]) * rows + c * half, half), :]

        def copy(k, chip, to, src=None, i=i, block=block):
            return pltpu.make_async_remote_copy(
                src_ref=block(chip) if src is None else src, dst_ref=block(chip), send_sem=send_sems.at[6 * i + k],
                recv_sem=recv_sems.at[6 * i + k], device_id=to, device_id_type=MESH)

        own = pltpu.make_async_copy(s_ref, g_ref.at[pl.ds((2 * x + y) * rows, rows), :], local_sems.at[i])
        own.start()
        first = [copy(k, (x, y), (*chip, c), src=s_ref.at[pl.ds(c * half, half), :])
                 for k, chip in enumerate(other_chips)]
        for cp in first:
            cp.start()
        passed = [copy(3 + k, chip, (x, y, 1 - c)) for k, chip in enumerate(other_chips)]
        started.append((own, first, passed))
    for own, first, passed in started:
        for k in range(3):
            first[k].wait_recv()
            passed[k].start()
    for own, first, passed in started:
        for k in range(3):
            passed[k].wait_recv()
        for cp in first + passed:
            cp.wait_send()
        own.wait()


def _gather_semaphores(n_t):
    return [pltpu.SemaphoreType.DMA((6 * n_t,)), pltpu.SemaphoreType.DMA((6 * n_t,)), pltpu.SemaphoreType.DMA((n_t,))]


def _gather_weights(shards):
    n_t = len(shards)

    def body(*refs):
        _gather_copies(refs[:n_t], refs[n_t:2 * n_t], *refs[2 * n_t + 1:])
        refs[2 * n_t][...] = jnp.zeros_like(refs[2 * n_t])

    any_spec = pl.BlockSpec(memory_space=pl.ANY)
    *full, done = _call(
        body, name="gather_weights", in_specs=[any_spec] * n_t,
        out_specs=[any_spec] * n_t + [pl.BlockSpec(memory_space=pltpu.VMEM)],
        out_shape=[_sds((N_CHIPS * s.shape[0], s.shape[1]), s.dtype) for s in shards]
        + [jax.ShapeDtypeStruct((8, LANES), F32)],
        scratch_shapes=_gather_semaphores(n_t),
    )(*shards)
    return full, done[0, 0]


def _gather_weights_beside(shards):
    n_t = len(shards)
    hbm = pltpu.MemorySpace.HBM
    s_refs = [jax.new_ref(s, memory_space=hbm) for s in shards]
    g_refs = [jax.empty_ref(jax.ShapeDtypeStruct((N_CHIPS * s.shape[0], s.shape[1]), s.dtype), memory_space=hbm)
              for s in shards]

    def launch(send_sems, recv_sems, local_sems):
        x, y, c, other_chips = _mesh_place()
        peers = [(*chip, c) for chip in other_chips] + [(x, y, 1 - c)]
        barrier = pltpu.get_barrier_semaphore()
        for peer in peers:
            pl.semaphore_signal(barrier, inc=1, device_id=peer, device_id_type=MESH)
        pl.semaphore_wait(barrier, len(peers))
        _gather_copies(s_refs, g_refs, send_sems, recv_sems, local_sems)

    pl.kernel(launch, mesh=plsc.ScalarSubcoreMesh(axis_name="sequencer", num_cores=1), name="gather_weights_beside",
              scratch_types=_gather_semaphores(n_t), compiler_params=pltpu.CompilerParams(collective_id=1))()
    return [g[...] for g in g_refs]


def _exchange_grads(big, small):
    n_t = len(big)
    n_g = len(_SMALL_GROUPS)
    names = _SMALL_ORDER
    halves = [(b.shape[0] // N_CHIPS // 2, b.shape[1]) for b in big]
    n_sems = 4 * n_g + 8 * n_t
    small_sem0, block_sem0 = n_t, n_t + len(names)

    def body(*refs):
        pos = 0

        def take(n):
            nonlocal pos
            pos += n
            return refs[pos - n:pos]

        big_refs, small_refs = take(n_t), dict(zip(names, take(len(names))))
        out_refs, small_out_refs = take(n_t), dict(zip(names, take(len(names))))
        ga, gb, pme, send_b, recv_b = take(n_t), take(n_t), take(n_t), take(n_t), take(n_t)
        s_own, s_sib, s_chips, s_pair = take(n_g), take(n_g), take(n_g), take(n_g)
        stage = dict(zip(names, take(len(names))))
        send_sems, recv_sems, local_sems = take(3)
        x, y, c, other_chips = _mesh_place()
        me = 2 * x + y
        sibling = (x, y, 1 - c)
        sem_at = iter(range(n_sems))

        def remote(src, dst, to):
            k = next(sem_at)
            return pltpu.make_async_remote_copy(src_ref=src, dst_ref=dst, send_sem=send_sems.at[k],
                                                recv_sem=recv_sems.at[k], device_id=to, device_id_type=MESH)

        loads = [pltpu.make_async_copy(small_refs[name], stage[name], local_sems.at[small_sem0 + a])
                 for a, name in enumerate(names)]
        for cp in loads:
            cp.start()
        for cp in loads:
            cp.wait()
        small_swaps = []
        for gi, (_, _, members) in enumerate(_SMALL_GROUPS):
            s_own[gi][...] = jnp.zeros_like(s_own[gi])
            for name, r0 in members:
                r, n = _small_shape(name)
                s_own[gi][r0:r0 + r, 0:n] = stage[name][...]
            small_swaps.append(remote(s_own[gi], s_sib[gi], sibling))
            small_swaps[gi].start()
        order = sorted(range(n_t), key=lambda i: halves[i][0] * halves[i][1])
        own_loads, big_swaps = {}, {}
        for i in order:
            hr = halves[i][0]
            own_loads[i], big_swaps[i] = [], []
            for j in range(N_CHIPS):
                mine = big_refs[i].at[pl.ds(j * 2 * hr + c * hr, hr), :]
                theirs = big_refs[i].at[pl.ds(j * 2 * hr + (1 - c) * hr, hr), :]
                own_loads[i].append(pltpu.make_async_copy(mine, ga[i].at[j], local_sems.at[block_sem0 + 4 * i + j]))
                own_loads[i][j].start()
                big_swaps[i].append(remote(theirs, gb[i].at[j], sibling))
                big_swaps[i][j].start()
        small_sends = []
        for gi in range(n_g):
            small_swaps[gi].wait_recv()
            s_pair[gi][...] = s_own[gi][...] + s_sib[gi][...]
            small_sends.append([remote(s_pair[gi], s_chips[gi].at[k], (*chip, c)) for k, chip in enumerate(other_chips)])
            for cp in small_sends[gi]:
                cp.start()

        def pair_sum(i, j):
            return ga[i][j] + gb[i][j]

        big_sends = {}
        for i in order:
            for j in range(N_CHIPS):
                own_loads[i][j].wait()
                big_swaps[i][j].wait_recv()
            big_sends[i] = []
            for k, chip in enumerate(other_chips):
                send_b[i][k] = pair_sum(i, 2 * chip[0] + chip[1]).astype(BF16)
                big_sends[i].append(remote(send_b[i].at[k], recv_b[i].at[k], (*chip, c)))
                big_sends[i][k].start()
        last_swaps, keeps = {}, {}
        for i in order:
            hr = halves[i][0]
            for k in range(3):
                big_sends[i][k].wait_recv()
            pme[i][...] = ((pair_sum(i, me) + recv_b[i][0].astype(F32)) + recv_b[i][1].astype(F32)) + recv_b[i][2].astype(F32)
            mine = out_refs[i].at[pl.ds(c * hr, hr), :]
            keeps[i] = pltpu.make_async_copy(pme[i], mine, local_sems.at[i])
            keeps[i].start()
            last_swaps[i] = remote(pme[i], mine, sibling)
            last_swaps[i].start()

        for gi, (_, _, members) in enumerate(_SMALL_GROUPS):
            for k in range(3):
                small_sends[gi][k].wait_recv()
            total = None
            for j in range(N_CHIPS):
                rel = jnp.bitwise_xor(j, me)
                term = jnp.where(rel == 0, s_pair[gi][...], jnp.where(
                    rel == 2, s_chips[gi][0], jnp.where(rel == 1, s_chips[gi][1], s_chips[gi][2])))
                total = term if total is None else total + term
            s_sib[gi][...] = total
            for name, r0 in members:
                r, n = _small_shape(name)
                stage[name][...] = s_sib[gi][r0:r0 + r, 0:n]
        stores = [pltpu.make_async_copy(stage[name], small_out_refs[name], local_sems.at[small_sem0 + a])
                  for a, name in enumerate(names)]
        for cp in stores:
            cp.start()

        for i in range(n_t):
            last_swaps[i].wait_recv()
            keeps[i].wait()
        for cp in stores:
            cp.wait()
        groups = list(big_swaps.values()) + small_sends + list(big_sends.values())
        for cp in small_swaps + [cp for group in groups for cp in group] + list(last_swaps.values()):
            cp.wait_send()

    any_spec = pl.BlockSpec(memory_space=pl.ANY)
    small_shapes = [_sds(_small_shape(n)) for n in names]
    group_shapes = [shape for _, shape, _ in _SMALL_GROUPS]
    per_matrix = lambda dtype, lead=(): [pltpu.VMEM(lead + h, dtype) for h in halves]
    outs = _call(
        body, name="exchange_grads",
        in_specs=[any_spec] * (n_t + len(names)),
        out_specs=[any_spec] * (n_t + len(names)),
        out_shape=[_sds((b.shape[0] // N_CHIPS, b.shape[1])) for b in big] + small_shapes,
        scratch_shapes=(per_matrix(F32, (N_CHIPS,)) + per_matrix(F32, (N_CHIPS,)) + per_matrix(F32)
                        + per_matrix(BF16, (3,)) + per_matrix(BF16, (3,))
                        + [pltpu.VMEM(s, F32) for s in group_shapes] * 2 + [pltpu.VMEM((3,) + s, F32) for s in group_shapes]
                        + [pltpu.VMEM(s, F32) for s in group_shapes]
                        + [pltpu.VMEM(_small_shape(n), F32) for n in names]
                        + [pltpu.SemaphoreType.DMA((n_sems,)), pltpu.SemaphoreType.DMA((n_sems,)),
                           pltpu.SemaphoreType.DMA((block_sem0 + N_CHIPS * n_t,))]),
        compiler_params=_params(48),
    )(*big, *[small[n] for n in names])
    return list(outs[:n_t]), dict(zip(names, outs[n_t:n_t + len(names)]))


def _adamw_update(w, g, m, v):
    m = ADAM_B1 * m + (1.0 - ADAM_B1) * g
    v = ADAM_B2 * v + (1.0 - ADAM_B2) * (g * g)
    m_hat = m / (1.0 - ADAM_B1 ** ADAM_STEP)
    v_hat = v / (1.0 - ADAM_B2 ** ADAM_STEP)
    return -ADAM_LR * (m_hat / (jnp.sqrt(v_hat) + ADAM_EPS) + ADAM_WD * w), m, v


def _adamw(w, g, m, v, grid, name):
    n_t = len(w)

    def body(*refs):
        ins, outs = refs[:4 * n_t], refs[4 * n_t:]
        for i in range(n_t):
            w_, g_, m_, v_ = [ins[a * n_t + i][...] for a in range(4)]
            vals = (g_,) + _adamw_update(w_, g_, m_, v_)
            for a in range(4):
                outs[a * n_t + i][...] = vals[a]

    specs = [pl.BlockSpec((a.shape[0] // grid, a.shape[1]), lambda i: (i, 0)) for a in w]
    shapes = [_sds(a.shape) for a in w]
    outs = _call(
        body, name=name, grid=(grid,), in_specs=specs * 4, out_specs=specs * 4, out_shape=shapes * 4,
        compiler_params=_params(40, ("arbitrary",)),
    )(*w, *g, *m, *v)
    return [outs[a * n_t:(a + 1) * n_t] for a in range(4)]


def kernel(x, p, pre_norm_g, w_in, ssm_lam_re, ssm_lam_im, ssm_log_step, ssm_b_re, ssm_b_im, ssm_c_re, ssm_c_im, ssm_d, ssm_w_glu, ssm_b_glu, attn_sinks, w_out, post_norm_g, pl_w_proj, pl_w_gate, pl_b_gate, loss_target, m_pre_norm_g, m_w_in, m_ssm_lam_re, m_ssm_lam_im, m_ssm_log_step, m_ssm_b_re, m_ssm_b_im, m_ssm_c_re, m_ssm_c_im, m_ssm_d, m_ssm_w_glu, m_ssm_b_glu, m_attn_sinks, m_w_out, m_post_norm_g, m_pl_w_proj, m_pl_w_gate, m_pl_b_gate, v_pre_norm_g, v_w_in, v_ssm_lam_re, v_ssm_lam_im, v_ssm_log_step, v_ssm_b_re, v_ssm_b_im, v_ssm_c_re, v_ssm_c_im, v_ssm_d, v_ssm_w_glu, v_ssm_b_glu, v_attn_sinks, v_w_out, v_post_norm_g, v_pl_w_proj, v_pl_w_gate, v_pl_b_gate):
    weights = dict(pre_norm_g=pre_norm_g, w_in=w_in, ssm_lam_re=ssm_lam_re, ssm_lam_im=ssm_lam_im,
                   ssm_log_step=ssm_log_step, ssm_b_re=ssm_b_re, ssm_b_im=ssm_b_im, ssm_c_re=ssm_c_re,
                   ssm_c_im=ssm_c_im, ssm_d=ssm_d, ssm_w_glu=ssm_w_glu, ssm_b_glu=ssm_b_glu, attn_sinks=attn_sinks,
                   w_out=w_out, post_norm_g=post_norm_g, pl_w_proj=pl_w_proj, pl_w_gate=pl_w_gate, pl_b_gate=pl_b_gate)
    m_in = dict(pre_norm_g=m_pre_norm_g, w_in=m_w_in, ssm_lam_re=m_ssm_lam_re, ssm_lam_im=m_ssm_lam_im,
                ssm_log_step=m_ssm_log_step, ssm_b_re=m_ssm_b_re, ssm_b_im=m_ssm_b_im, ssm_c_re=m_ssm_c_re,
                ssm_c_im=m_ssm_c_im, ssm_d=m_ssm_d, ssm_w_glu=m_ssm_w_glu, ssm_b_glu=m_ssm_b_glu,
                attn_sinks=m_attn_sinks, w_out=m_w_out, post_norm_g=m_post_norm_g, pl_w_proj=m_pl_w_proj,
                pl_w_gate=m_pl_w_gate, pl_b_gate=m_pl_b_gate)
    v_in = dict(pre_norm_g=v_pre_norm_g, w_in=v_w_in, ssm_lam_re=v_ssm_lam_re, ssm_lam_im=v_ssm_lam_im,
                ssm_log_step=v_ssm_log_step, ssm_b_re=v_ssm_b_re, ssm_b_im=v_ssm_b_im, ssm_c_re=v_ssm_c_re,
                ssm_c_im=v_ssm_c_im, ssm_d=v_ssm_d, ssm_w_glu=v_ssm_w_glu, ssm_b_glu=v_ssm_b_glu,
                attn_sinks=v_attn_sinks, w_out=v_w_out, post_norm_g=v_post_norm_g, pl_w_proj=v_pl_w_proj,
                pl_w_gate=v_pl_w_gate, pl_b_gate=v_pl_b_gate)

    def two_d(tree):
        return {k: _to_kernel_form(k, a) for k, a in tree.items()}

    w2, m2, v2 = two_d(weights), two_d(m_in), two_d(v_in)

    (w_in_full,), gathered = _gather_weights([w2["w_in"].astype(BF16)])
    rest = _gather_weights_beside([(w2[n] + gathered).astype(BF16) for n in _BIG[1:]])
    full = dict(zip(_BIG, [w_in_full] + rest))
    s5_params = tuple(w2[n] for n in ("ssm_lam_re", "ssm_lam_im", "ssm_log_step", "ssm_b_re", "ssm_b_im", "ssm_c_re",
                                      "ssm_c_im"))
    grad_x, loss, grads = _local_step(
        x, p, loss_target, w2["pre_norm_g"], full["w_in"], s5_params, w2["ssm_d"], full["ssm_w_glu"], w2["ssm_b_glu"],
        w2["attn_sinks"], full["w_out"], w2["post_norm_g"], full["pl_w_proj"], full["pl_w_gate"], w2["pl_b_gate"])

    g_big, g_small = _exchange_grads([grads[n] for n in _BIG], {**{n: grads[n] for n in _SMALL}, "loss": loss})
    g_big = dict(zip(_BIG, g_big))
    total_loss = g_small.pop("loss")

    big_out = _adamw([w2[n] for n in _BIG], [g_big[n] for n in _BIG], [m2[n] for n in _BIG], [v2[n] for n in _BIG],
                     8, "adamw_matrices")
    small_names = tuple(_SMALL)
    small_out = _adamw([w2[n] for n in small_names], [g_small[n] for n in small_names], [m2[n] for n in small_names],
                       [v2[n] for n in small_names], 1, "adamw_small")

    results = [{**dict(zip(_BIG, big_part)), **dict(zip(small_names, small_part))}
               for big_part, small_part in zip(big_out, small_out)]
    flat = [_from_kernel_form(name, r[name], weights[name].shape) for r in results for name in _WEIGHT_ORDER]
    return (total_loss.reshape(()), grad_x, *flat)
```

```python
import math

import jax
import jax.numpy as jnp
from jax import lax
from jax.experimental import pallas as pl
from jax.experimental.pallas import tpu as pltpu
from jax.experimental.pallas import tpu_sc as plsc

F32 = jnp.float32
BF16 = jnp.bfloat16

D_MODEL = 1024
D_SSM = 512
D_ATTN = 512
SSM_GROUPS = 32
SSM_GROUP_CH = 16
SSM_STATE = 64
SSM_LANES = SSM_GROUPS * SSM_STATE
HEAD_DIM = 64
N_HEADS = 8
KV_HEADS = 2
Q_PER_KV = 4
WINDOW = 128
BLOCK = 128
D_PLE = 256
D_IN = 2304
EPS = 1e-6
ATTN_SCALE = 1.0 / math.sqrt(HEAD_DIM)

ADAM_LR = 0.001
ADAM_B1 = 0.9
ADAM_B2 = 0.999
ADAM_EPS = 1e-08
ADAM_WD = 0.01
ADAM_STEP = 10

N_CHIPS = 4
LANES = 128
SCAN_CHUNKS = 8
SCAN_TILE_STEPS = 16
SCAN_LANE_CHUNK = 512
MIB = 2 ** 20
MESH = pl.DeviceIdType.MESH


def _dot(a, b):
    return jnp.dot(a, b, preferred_element_type=F32)


def _dot_nt(a, b):
    return lax.dot_general(a, b, (((1,), (1,)), ((), ())), preferred_element_type=F32)


def _dot_tn(a, b):
    return lax.dot_general(a, b, (((0,), (0,)), ((), ())), preferred_element_type=F32)


def _params(vmem_mib, semantics=None):
    kw = dict(vmem_limit_bytes=vmem_mib * MIB)
    if semantics is not None:
        kw["dimension_semantics"] = semantics
    return pltpu.CompilerParams(**kw)


def _full(shape):
    nd = len(shape)
    return pl.BlockSpec(shape, lambda *_: (0,) * nd, pipeline_mode=pl.Buffered(1))


def _rows(tm, width):
    return pl.BlockSpec((tm, width), lambda i: (i, 0))


def _sds(shape, dtype=F32):
    return pltpu.HBM(shape, dtype)


def _call(body, **kw):
    fn = pl.pallas_call(body, **kw)
    return lambda *args: fn(*[pltpu.with_memory_space_constraint(a, pltpu.HBM) for a in args])


def _silu(z):
    return z * jax.nn.sigmoid(z)


def _in_proj(x2d, g1, w_in_t, n_seq, seq):
    rows = x2d.shape[0]
    tm = 512
    slab, steps, _, _ = _scan_geometry(n_seq, seq)

    def body(x_ref, g_ref, w_ref, *out_refs):
        u_parts, (zs_ref, q_ref, k_ref, v_ref, za_ref) = out_refs[:_SCAN_PARTS], out_refs[_SCAN_PARTS:]
        x = x_ref[...]
        r = lax.rsqrt(jnp.mean(x * x, axis=-1, keepdims=True) + EPS)
        hn = (x * r * g_ref[...]).astype(BF16)

        def proj(a, b):
            return _dot_nt(hn, w_ref[a:b, :])

        _store_chunks(u_parts, pl.program_id(0) * (tm // steps), proj(0, 512), steps, slab)
        zs_ref[...] = proj(512, 1024)
        q_ref[...] = proj(1024, 1536).astype(BF16)
        k_ref[...] = proj(1536, 1664).astype(BF16)
        v_ref[...] = proj(1664, 1792).astype(BF16)
        za_ref[...] = proj(1792, 2304)

    *u_parts, zs, q, k, v, za = _call(
        body, name="in_proj", grid=(rows // tm,),
        in_specs=[_rows(tm, D_MODEL), _full((1, D_MODEL)), _full((D_IN, D_MODEL))],
        out_specs=_whole_parts(rows) + [_rows(tm, 512), _rows(tm, 512), _rows(tm, 128), _rows(tm, 128), _rows(tm, 512)],
        out_shape=_part_shapes(rows) + [_sds((rows, 512)), _sds((rows, 512), BF16), _sds((rows, 128), BF16),
                                        _sds((rows, 128), BF16), _sds((rows, 512))],
        compiler_params=_params(48, ("arbitrary",)),
    )(x2d, g1, w_in_t)
    return u_parts, zs, q, k, v, za


def _in_proj_bwd(x2d, dh1, g1, w_in_t, du_parts, dzs, dq, dk, dv, dza, n_seq, seq):
    rows = x2d.shape[0]
    tm = 256
    slab, steps, _, _ = _scan_geometry(n_seq, seq)
    pieces = ((0, 512), (512, 1024), (1024, 1536), (1536, 1664), (1664, 1792), (1792, 2304))

    def body(x_ref, dh1_ref, g_ref, w_ref, *refs):
        du_parts, (dzs_ref, dq_ref, dk_ref, dv_ref, dza_ref, gx_ref, dw_ref, dg_ref) = refs[:_SCAN_PARTS], refs[_SCAN_PARTS:]

        @pl.when(pl.program_id(0) == 0)
        def _():
            dw_ref[...] = jnp.zeros_like(dw_ref)
            dg_ref[...] = jnp.zeros_like(dg_ref)

        x = x_ref[...]
        g = g_ref[...]
        r = lax.rsqrt(jnp.mean(x * x, axis=-1, keepdims=True) + EPS)
        xr = x * r
        hn = (xr * g).astype(BF16)
        dhn = jnp.zeros((tm, D_MODEL), F32)
        du = _load_chunks(du_parts, pl.program_id(0) * (tm // steps), tm // steps, steps, slab)
        for (a, b), piece in zip(pieces, (du, dzs_ref[...], dq_ref[...], dk_ref[...], dv_ref[...], dza_ref[...])):
            piece = piece.astype(BF16)
            dhn = dhn + _dot(piece, w_ref[a:b, :])
            dw_ref[a:b, :] += _dot_tn(piece, hn)
        dg_ref[...] += jnp.sum(dhn * xr, axis=0, keepdims=True)
        a_ = dhn * g
        gx_ref[...] = dh1_ref[...] + r * a_ - xr * (r * jnp.mean(a_ * xr, axis=-1, keepdims=True))

    return _call(
        body, name="in_proj_bwd", grid=(rows // tm,),
        in_specs=[_rows(tm, D_MODEL), _rows(tm, D_MODEL), _full((1, D_MODEL)), _full((D_IN, D_MODEL))]
        + _whole_parts(rows) + [_rows(tm, 512), _rows(tm, 512), _rows(tm, 128), _rows(tm, 128), _rows(tm, 512)],
        out_specs=[_rows(tm, D_MODEL), _full((D_IN, D_MODEL)), _full((1, D_MODEL))],
        out_shape=[_sds((rows, D_MODEL)), _sds((D_IN, D_MODEL)), _sds((1, D_MODEL))],
        compiler_params=_params(52, ("arbitrary",)),
    )(x2d, dh1, g1, w_in_t, *du_parts, dzs, dq, dk, dv, dza)


def _iota(shape, axis):
    return lax.broadcasted_iota(jnp.int32, shape, axis)


def _exact_dot(a, b):
    return jnp.dot(a, b, precision=lax.Precision.HIGHEST, preferred_element_type=F32)


_HALF_GROUPS = SSM_GROUPS // 2
_N_SHIFT = SSM_STATE.bit_length() - 1
_P_SHIFT = SSM_GROUP_CH.bit_length() - 1


def _s5_operands(lam_re, lam_im, log_step, b_re, b_im, c_re, c_im):
    g, n, p = SSM_GROUPS, SSM_STATE, SSM_GROUP_CH
    gn, gp, hn_, hp = g * n, g * p, _HALF_GROUPS * n, _HALF_GROUPS * p
    eye_g = _iota((g, g), 0) == _iota((g, g), 1)
    step = jnp.sum(jnp.where(eye_g, jnp.exp(log_step), 0.0), axis=1, keepdims=True)
    a_re = lam_re * step
    a_im = lam_im * step
    mag = jnp.exp(a_re)
    lbar_re = mag * jnp.cos(a_im)
    lbar_im = mag * jnp.sin(a_im)
    n_re = lbar_re - 1.0
    den = lam_re * lam_re + lam_im * lam_im
    f_re = (n_re * lam_re + lbar_im * lam_im) / den
    f_im = (lbar_im * lam_re - n_re * lam_im) / den

    spread_n = (_iota((n, gn), 0) == (_iota((n, gn), 1) & (n - 1))).astype(F32)
    own_g = _iota((g, gn), 0) == (_iota((g, gn), 1) >> _N_SHIFT)

    def to_row(a):
        return jnp.sum(jnp.where(own_g, _exact_dot(a, spread_n), 0.0), axis=0, keepdims=True)

    per_group = ((_iota((gp, g), 0) >> _P_SHIFT) == _iota((gp, g), 1)).astype(F32)
    fx_re, fx_im = _exact_dot(per_group, f_re), _exact_dot(per_group, f_im)
    bbar_re = fx_re * b_re - fx_im * b_im
    bbar_im = fx_re * b_im + fx_im * b_re

    tile_n = (_iota((n, hn_), 0) == (_iota((n, hn_), 1) & (n - 1))).astype(F32)
    same_group = (_iota((hp, hn_), 0) >> _P_SHIFT) == (_iota((hp, hn_), 1) >> _N_SHIFT)

    def embed(a, hf):
        return jnp.where(same_group, _exact_dot(a[hf * hp:(hf + 1) * hp], tile_n), 0.0)

    return (to_row(lbar_re), to_row(lbar_im), embed(bbar_re, 0), embed(bbar_re, 1), embed(bbar_im, 0),
            embed(bbar_im, 1), embed(c_re, 0), embed(c_re, 1), embed(c_im, 0), embed(c_im, 1))


_S5_PARAM_SHAPES = ((SSM_GROUPS, SSM_STATE), (SSM_GROUPS, SSM_STATE), (1, SSM_GROUPS),
                    (D_SSM, SSM_STATE), (D_SSM, SSM_STATE), (D_SSM, SSM_STATE), (D_SSM, SSM_STATE))
_CM_SHAPE = (2, _HALF_GROUPS * SSM_GROUP_CH, _HALF_GROUPS * SSM_STATE)
_S5_OPERAND_SHAPES = ((1, SSM_LANES), (1, SSM_LANES), _CM_SHAPE, _CM_SHAPE, _CM_SHAPE, _CM_SHAPE)


def _s5_params_fwd(*params):
    def body(*refs):
        ins, (lre_ref, lim_ref, btre_ref, btim_ref, cmre_ref, cmim_ref) = refs[:7], refs[7:]
        vals = _s5_operands(*[r[...] for r in ins])
        lre_ref[...] = vals[0]
        lim_ref[...] = vals[1]
        for ref, pair in zip((btre_ref, btim_ref, cmre_ref, cmim_ref), (vals[2:4], vals[4:6], vals[6:8], vals[8:10])):
            ref[0] = pair[0].astype(BF16)
            ref[1] = pair[1].astype(BF16)

    dtypes = (F32, F32, BF16, BF16, BF16, BF16)
    return _call(
        body, name="s5_params_fwd",
        in_specs=[_full(s) for s in _S5_PARAM_SHAPES], out_specs=[_full(s) for s in _S5_OPERAND_SHAPES],
        out_shape=[_sds(s, d) for s, d in zip(_S5_OPERAND_SHAPES, dtypes)], compiler_params=_params(32),
    )(*params)


def _s5_params_bwd(params, cotangents):
    def body(*refs):
        ins, (dlre, dlim, dbtre, dbtim, dcmre, dcmim), outs = refs[:7], refs[7:13], refs[13:]
        _, vjp = jax.vjp(_s5_operands, *[r[...] for r in ins])
        cts = (dlre[...], dlim[...], dbtre[0], dbtre[1], dbtim[0], dbtim[1], dcmre[0], dcmre[1], dcmim[0], dcmim[1])
        for ref, val in zip(outs, vjp(cts)):
            ref[...] = val

    return _call(
        body, name="s5_params_bwd",
        in_specs=[_full(s) for s in _S5_PARAM_SHAPES + _S5_OPERAND_SHAPES],
        out_specs=[_full(s) for s in _S5_PARAM_SHAPES],
        out_shape=[_sds(s) for s in _S5_PARAM_SHAPES], compiler_params=_params(48),
    )(*params, *cotangents)


def _scan_geometry(n_seq, seq):
    slab = n_seq * SCAN_CHUNKS
    steps = seq // SCAN_CHUNKS
    tile_rows = slab * SCAN_TILE_STEPS
    n_tiles = steps // SCAN_TILE_STEPS
    return slab, steps, tile_rows, n_tiles


_SCAN_PARTS = D_SSM // LANES


def _whole_parts(rows):
    return [_full((rows, LANES))] * _SCAN_PARTS


def _part_shapes(rows):
    return [_sds((rows, LANES))] * _SCAN_PARTS


def _load_chunks(parts, first_chunk, n_chunks, steps, slab):
    return jnp.concatenate([
        jnp.concatenate([ref[pl.ds(first_chunk + q, steps, stride=slab), :] for ref in parts], axis=1)
        for q in range(n_chunks)], axis=0)


def _store_chunks(parts, first_chunk, value, steps, slab):
    for q in range(value.shape[0] // steps):
        for j, ref in enumerate(parts):
            ref[pl.ds(first_chunk + q, steps, stride=slab), :] = value[q * steps:(q + 1) * steps,
                                                                     j * LANES:(j + 1) * LANES]


def _join_parts(parts):
    return jnp.concatenate([ref[...] for ref in parts], axis=1)


def _split_parts(parts, value):
    for j, ref in enumerate(parts):
        ref[...] = value[:, j * LANES:(j + 1) * LANES]


def _complex_power(re, im, n):
    out = None
    while n:
        if n & 1:
            out = (re, im) if out is None else (out[0] * re - out[1] * im, out[0] * im + out[1] * re)
        n >>= 1
        if n:
            re, im = re * re - im * im, 2.0 * re * im
    return out


def _chunk_carry(sum_re, sum_im, carry_re, carry_im, a_re, a_im, n_seq, reverse):
    carry_re[...] = jnp.zeros_like(carry_re)
    carry_im[...] = jnp.zeros_like(carry_im)
    for s in range(n_seq):
        order = range(SCAN_CHUNKS - 2, -1, -1) if reverse else range(1, SCAN_CHUNKS)
        for c in order:
            r = s * SCAN_CHUNKS + c
            p = r + 1 if reverse else r - 1
            p_re, p_im = carry_re[p:p + 1, :], carry_im[p:p + 1, :]
            carry_re[r:r + 1, :] = a_re * p_re - a_im * p_im + sum_re[p:p + 1, :]
            carry_im[r:r + 1, :] = a_re * p_im + a_im * p_re + sum_im[p:p + 1, :]


def _s5_scan_fwd(u_parts, bt_re, bt_im, cm_re, cm_im, lbar_re, lbar_im, d_row, n_seq, seq):
    slab, steps, tile_rows, n_tiles = _scan_geometry(n_seq, seq)
    rows = u_parts[0].shape[0]

    def body(*refs):
        u_refs, refs = refs[:_SCAN_PARTS], refs[_SCAN_PARTS:]
        (bre_ref, bim_ref, cre_ref, cim_ref, lre_ref, lim_ref, d_ref), refs = refs[:7], refs[7:]
        y_refs, (hre_ref, him_ref, st_re, st_im, h0_re, h0_im, buf_re, buf_im) = refs[:_SCAN_PARTS], refs[_SCAN_PARTS:]
        second = pl.program_id(0) == 1
        i = pl.program_id(1)

        @pl.when(jnp.logical_and(i == 0, jnp.logical_not(second)))
        def _():
            st_re[...] = jnp.zeros_like(st_re)
            st_im[...] = jnp.zeros_like(st_im)

        u = _join_parts(u_refs)
        ub = u.astype(BF16)
        for hf in range(2):
            cols = slice(hf * 1024, (hf + 1) * 1024)
            buf_re[:, cols] = _dot(ub[:, hf * 256:(hf + 1) * 256], bre_ref[hf])
            buf_im[:, cols] = _dot(ub[:, hf * 256:(hf + 1) * 256], bim_ref[hf])

        for lc in range(SSM_LANES // SCAN_LANE_CHUNK):
            cols = slice(lc * SCAN_LANE_CHUNK, (lc + 1) * SCAN_LANE_CHUNK)
            l_re = jnp.broadcast_to(lre_ref[:, cols], (slab, SCAN_LANE_CHUNK))
            l_im = jnp.broadcast_to(lim_ref[:, cols], (slab, SCAN_LANE_CHUNK))

            def scan_tile(keep_states):
                def step(t, carry):
                    s_re, s_im = carry
                    r0 = pl.multiple_of(t * slab, slab)
                    n_re = l_re * s_re - l_im * s_im + buf_re[pl.ds(r0, slab), cols]
                    n_im = l_re * s_im + l_im * s_re + buf_im[pl.ds(r0, slab), cols]
                    if keep_states:
                        buf_re[pl.ds(r0, slab), cols] = n_re
                        buf_im[pl.ds(r0, slab), cols] = n_im
                    return n_re, n_im

                s_re, s_im = lax.fori_loop(0, SCAN_TILE_STEPS, step, (st_re[:, cols], st_im[:, cols]), unroll=True)
                st_re[:, cols] = s_re
                st_im[:, cols] = s_im

            pl.when(jnp.logical_not(second))(lambda: scan_tile(False))
            pl.when(second)(lambda: scan_tile(True))

        @pl.when(jnp.logical_and(i == n_tiles - 1, jnp.logical_not(second)))
        def _():
            a_re, a_im = _complex_power(lre_ref[...], lim_ref[...], steps)
            _chunk_carry(st_re, st_im, h0_re, h0_im, a_re, a_im, n_seq, reverse=False)
            st_re[...] = h0_re[...]
            st_im[...] = h0_im[...]

        @pl.when(second)
        def _():
            h_re = buf_re[...].astype(BF16)
            h_im = buf_im[...].astype(BF16)
            hre_ref[...] = h_re
            him_ref[...] = h_im
            for hf in range(2):
                cols = slice(hf * 1024, (hf + 1) * 1024)
                ycols = slice(hf * 256, (hf + 1) * 256)
                y_half = (_dot_nt(h_re[:, cols], cre_ref[hf]) - _dot_nt(h_im[:, cols], cim_ref[hf])
                          + d_ref[:, ycols] * u[:, ycols])
                _split_parts(y_refs[2 * hf:2 * hf + 2], y_half)

    tile = lambda w: pl.BlockSpec((tile_rows, w), lambda p, i: (i, 0))
    out_tile = lambda w: pl.BlockSpec((tile_rows, w), lambda p, i: (i * p, 0))
    cm = _full(_CM_SHAPE)
    outs = _call(
        body, name="s5_scan_fwd", grid=(2, n_tiles),
        in_specs=[tile(LANES)] * _SCAN_PARTS + [cm, cm, cm, cm, _full((1, SSM_LANES)), _full((1, SSM_LANES)),
                                                _full((1, 512))],
        out_specs=[out_tile(LANES)] * _SCAN_PARTS + [out_tile(SSM_LANES), out_tile(SSM_LANES)],
        out_shape=_part_shapes(rows) + [_sds((rows, SSM_LANES), BF16), _sds((rows, SSM_LANES), BF16)],
        scratch_shapes=[pltpu.VMEM((slab, SSM_LANES), F32)] * 4 + [pltpu.VMEM((tile_rows, SSM_LANES), F32)] * 2,
        compiler_params=_params(40, ("arbitrary", "arbitrary")),
    )(*u_parts, bt_re, bt_im, cm_re, cm_im, lbar_re, lbar_im, d_row)
    return outs[:_SCAN_PARTS], outs[_SCAN_PARTS], outs[_SCAN_PARTS + 1]


def _s5_scan_bwd(dy_parts, u_parts, h_re, h_im, bt_re, bt_im, cm_re, cm_im, lbar_re, lbar_im, d_row, n_seq, seq):
    slab, steps, tile_rows, n_tiles = _scan_geometry(n_seq, seq)
    rows = u_parts[0].shape[0]

    def body(*refs):
        dy_refs, u_refs, refs = refs[:_SCAN_PARTS], refs[_SCAN_PARTS:2 * _SCAN_PARTS], refs[2 * _SCAN_PARTS:]
        (hre_ref, him_ref, bre_ref, bim_ref, cre_ref, cim_ref, lre_ref, lim_ref, d_ref), refs = refs[:9], refs[9:]
        du_refs, refs = refs[:_SCAN_PARTS], refs[_SCAN_PARTS:]
        (dbre_ref, dbim_ref, dcre_ref, dcim_ref, dlre_ref, dlim_ref, dd_ref,
         st_re, st_im, g0_re, g0_im, acc_re, acc_im, buf_re, buf_im) = refs
        second = pl.program_id(0) == 1
        i = pl.program_id(1)

        @pl.when(jnp.logical_and(i == 0, jnp.logical_not(second)))
        def _():
            st_re[...] = jnp.zeros_like(st_re)
            st_im[...] = jnp.zeros_like(st_im)
            acc_re[...] = jnp.zeros_like(acc_re)
            acc_im[...] = jnp.zeros_like(acc_im)
            for ref in (dbre_ref, dbim_ref, dcre_ref, dcim_ref, dd_ref):
                ref[...] = jnp.zeros_like(ref)

        dy = _join_parts(dy_refs)
        dyb = dy.astype(BF16)
        for hf in range(2):
            cols = slice(hf * 1024, (hf + 1) * 1024)
            buf_re[:, cols] = _dot(dyb[:, hf * 256:(hf + 1) * 256], cre_ref[hf])
            buf_im[:, cols] = -_dot(dyb[:, hf * 256:(hf + 1) * 256], cim_ref[hf])

        for lc in range(SSM_LANES // SCAN_LANE_CHUNK):
            cols = slice(lc * SCAN_LANE_CHUNK, (lc + 1) * SCAN_LANE_CHUNK)
            l_re = jnp.broadcast_to(lre_ref[:, cols], (slab, SCAN_LANE_CHUNK))
            l_im = jnp.broadcast_to(lim_ref[:, cols], (slab, SCAN_LANE_CHUNK))

            def advance(r0, s_re, s_im):
                n_re = l_re * s_re + l_im * s_im + buf_re[pl.ds(r0, slab), cols]
                n_im = l_re * s_im - l_im * s_re + buf_im[pl.ds(r0, slab), cols]
                buf_re[pl.ds(r0, slab), cols] = n_re
                buf_im[pl.ds(r0, slab), cols] = n_im
                return n_re, n_im

            def row0(k):
                return pl.multiple_of((SCAN_TILE_STEPS - 1 - k) * slab, slab)

            @pl.when(jnp.logical_not(second))
            def _():
                s_re, s_im = lax.fori_loop(0, SCAN_TILE_STEPS, lambda k, s: advance(row0(k), *s),
                                           (st_re[:, cols], st_im[:, cols]), unroll=True)
                st_re[:, cols] = s_re
                st_im[:, cols] = s_im

            @pl.when(second)
            def _():
                def step(k, carry):
                    s_re, s_im, a_re, a_im = carry
                    r0 = row0(k)
                    hr = hre_ref[pl.ds(r0, slab), cols].astype(F32)
                    hi = him_ref[pl.ds(r0, slab), cols].astype(F32)
                    a_re = a_re + s_re * hr + s_im * hi
                    a_im = a_im + s_im * hr - s_re * hi
                    return advance(r0, s_re, s_im) + (a_re, a_im)

                zero = jnp.zeros((slab, SCAN_LANE_CHUNK), F32)
                s_re, s_im, a_re, a_im = lax.fori_loop(
                    0, SCAN_TILE_STEPS, step, (st_re[:, cols], st_im[:, cols], zero, zero), unroll=True)
                st_re[:, cols] = s_re
                st_im[:, cols] = s_im
                acc_re[:, cols] += a_re
                acc_im[:, cols] += a_im

        @pl.when(jnp.logical_and(i == n_tiles - 1, jnp.logical_not(second)))
        def _():
            p_re, p_im = _complex_power(lre_ref[...], lim_ref[...], steps)
            _chunk_carry(st_re, st_im, g0_re, g0_im, p_re, -p_im, n_seq, reverse=True)
            st_re[...] = g0_re[...]
            st_im[...] = g0_im[...]

        @pl.when(second)
        def _():
            u = _join_parts(u_refs)
            ub = u.astype(BF16)
            g_re = buf_re[...].astype(BF16)
            g_im = buf_im[...].astype(BF16)
            dd_ref[...] += jnp.sum(dy * u, axis=0, keepdims=True)
            for hf in range(2):
                cols = slice(hf * 1024, (hf + 1) * 1024)
                ycols = slice(hf * 256, (hf + 1) * 256)
                du_half = (_dot_nt(g_re[:, cols], bre_ref[hf]) + _dot_nt(g_im[:, cols], bim_ref[hf])
                           + d_ref[:, ycols] * dy[:, ycols])
                _split_parts(du_refs[2 * hf:2 * hf + 2], du_half)
                dbre_ref[hf] += _dot_tn(ub[:, ycols], g_re[:, cols])
                dbim_ref[hf] += _dot_tn(ub[:, ycols], g_im[:, cols])
                dcre_ref[hf] += _dot_tn(dyb[:, ycols], hre_ref[:, cols])
                dcim_ref[hf] -= _dot_tn(dyb[:, ycols], him_ref[:, cols])

        @pl.when(jnp.logical_and(i == n_tiles - 1, second))
        def _():
            dlre_ref[...] = jnp.sum(acc_re[...], axis=0, keepdims=True)
            dlim_ref[...] = jnp.sum(acc_im[...], axis=0, keepdims=True)

    tile = lambda w: pl.BlockSpec((tile_rows, w), lambda p, i: (n_tiles - 1 - i, 0))
    second_tile = lambda w: pl.BlockSpec((tile_rows, w), lambda p, i: (n_tiles - 1 - i * p, 0))
    cm = _full(_CM_SHAPE)
    row = _full((1, SSM_LANES))
    outs = _call(
        body, name="s5_scan_bwd", grid=(2, n_tiles),
        in_specs=[tile(LANES)] * _SCAN_PARTS + [second_tile(LANES)] * _SCAN_PARTS
        + [second_tile(SSM_LANES), second_tile(SSM_LANES), cm, cm, cm, cm, row, row, _full((1, 512))],
        out_specs=[second_tile(LANES)] * _SCAN_PARTS + [cm, cm, cm, cm, row, row, _full((1, 512))],
        out_shape=(_part_shapes(rows) + [_sds(_CM_SHAPE)] * 4 + [_sds((1, SSM_LANES))] * 2 + [_sds((1, 512))]),
        scratch_shapes=[pltpu.VMEM((slab, SSM_LANES), F32)] * 6 + [pltpu.VMEM((tile_rows, SSM_LANES), F32)] * 2,
        compiler_params=_params(48, ("arbitrary", "arbitrary")),
    )(*dy_parts, *u_parts, h_re, h_im, bt_re, bt_im, cm_re, cm_im, lbar_re, lbar_im, d_row)
    return (outs[:_SCAN_PARTS],) + tuple(outs[_SCAN_PARTS:])


def _glu_gate(gl, a, zs):
    return gl * jax.nn.sigmoid(a) * _silu(zs)


def _glu_fwd(y_parts, zs, w_glu, b_glu, n_seq, seq):
    rows = zs.shape[0]
    tm = 512
    slab, steps, _, _ = _scan_geometry(n_seq, seq)

    def body(*refs):
        y_refs, (zs_ref, w_ref, b_ref, o_ref) = refs[:_SCAN_PARTS], refs[_SCAN_PARTS:]
        y = _load_chunks(y_refs, pl.program_id(0) * (tm // steps), tm // steps, steps, slab)
        gl = jax.nn.gelu(y)
        a = _dot(gl.astype(BF16), w_ref[...]) + b_ref[...]
        o_ref[...] = _glu_gate(gl, a, zs_ref[...]).astype(BF16)

    return _call(
        body, name="glu_fwd", grid=(rows // tm,),
        in_specs=_whole_parts(rows) + [_rows(tm, 512), _full((512, 512)), _full((1, 512))],
        out_specs=_rows(tm, 512), out_shape=_sds((rows, 512), BF16),
        compiler_params=_params(32, ("arbitrary",)),
    )(*y_parts, zs, w_glu, b_glu)


def _glu_bwd(y_parts, zs, d_out, w_glu, b_glu, n_seq, seq):
    rows = zs.shape[0]
    tm = 512
    slab, steps, _, _ = _scan_geometry(n_seq, seq)

    def body(*refs):
        y_refs, (zs_ref, d_ref, w_ref, b_ref), refs = refs[:_SCAN_PARTS], refs[_SCAN_PARTS:_SCAN_PARTS + 4], refs[_SCAN_PARTS + 4:]
        dy_refs, (dzs_ref, dw_ref, db_ref) = refs[:_SCAN_PARTS], refs[_SCAN_PARTS:]
        first_chunk = pl.program_id(0) * (tm // steps)

        @pl.when(pl.program_id(0) == 0)
        def _():
            dw_ref[...] = jnp.zeros_like(dw_ref)
            db_ref[...] = jnp.zeros_like(db_ref)

        gl, gelu_vjp = jax.vjp(jax.nn.gelu, _load_chunks(y_refs, first_chunk, tm // steps, steps, slab))
        glb = gl.astype(BF16)
        a = _dot(glb, w_ref[...]) + b_ref[...]
        _, gate_vjp = jax.vjp(_glu_gate, gl, a, zs_ref[...])
        d_gl, d_a, d_zs = gate_vjp(d_ref[...])
        dab = d_a.astype(BF16)
        d_gl = d_gl + _dot_nt(dab, w_ref[...])
        _store_chunks(dy_refs, first_chunk, gelu_vjp(d_gl)[0], steps, slab)
        dzs_ref[...] = d_zs.astype(BF16)
        dw_ref[...] += _dot_tn(glb, dab)
        db_ref[...] += jnp.sum(d_a, axis=0, keepdims=True)

    *dy_parts, dzs, dw, db = _call(
        body, name="glu_bwd", grid=(rows // tm,),
        in_specs=_whole_parts(rows) + [_rows(tm, 512), _rows(tm, 512), _full((512, 512)), _full((1, 512))],
        out_specs=_whole_parts(rows) + [_rows(tm, 512), _full((512, 512)), _full((1, 512))],
        out_shape=_part_shapes(rows) + [_sds((rows, 512), BF16), _sds((512, 512)), _sds((1, 512))],
        compiler_params=_params(40, ("arbitrary",)),
    )(*y_parts, zs, d_out, w_glu, b_glu)
    return dy_parts, dzs, dw, db


_GROUP_ROWS = Q_PER_KV * BLOCK
_BLOCK_SHIFT = BLOCK.bit_length() - 1


def _attn_bias(j):
    row = _iota((_GROUP_ROWS, BLOCK), 0)
    dist_cur = (row & (BLOCK - 1)) - _iota((_GROUP_ROWS, BLOCK), 1)
    dist_prev = dist_cur + BLOCK
    head = row >> _BLOCK_SHIFT
    slope = jnp.zeros((_GROUP_ROWS, BLOCK), F32)
    for g in range(Q_PER_KV):
        slope = jnp.where(head == g, 2.0 ** (-(j * Q_PER_KV + g + 1)), slope)
    bias_cur = jnp.where(dist_cur >= 0, -slope * dist_cur.astype(F32), -jnp.inf)
    bias_prev = jnp.where(dist_prev < WINDOW, -slope * dist_prev.astype(F32), -jnp.inf)
    return bias_cur, bias_prev


_ATTN_BIAS_SCRATCH = pltpu.VMEM((KV_HEADS, 2, _GROUP_ROWS, BLOCK), F32)


def _fill_attn_bias(bias_ref):
    @pl.when(jnp.logical_and(pl.program_id(0) == 0, pl.program_id(1) == 0))
    def _():
        for j in range(KV_HEADS):
            bias_ref[j, 0], bias_ref[j, 1] = _attn_bias(j)


def _stack_heads(x, j):
    heads = range(j * Q_PER_KV, (j + 1) * Q_PER_KV)
    return jnp.concatenate([x[:, h * HEAD_DIM:(h + 1) * HEAD_DIM] for h in heads], axis=0)


def _stack_columns(x, j):
    heads = range(j * Q_PER_KV, (j + 1) * Q_PER_KV)
    return jnp.concatenate([jnp.broadcast_to(x[:, h:h + 1], (BLOCK, 1)) for h in heads], axis=0)


def _attn_fwd(q, k, v, za, sinks, n_seq, seq):
    nb = seq // BLOCK
    rows = q.shape[0]

    def body(q_ref, kc_ref, kp_ref, vc_ref, vp_ref, za_ref, sk_ref, o_ref, ao_ref, lse_ref, bias_ref):
        _fill_attn_bias(bias_ref)
        has_prev = pl.program_id(1) > 0
        q_all = q_ref[...]
        for j in range(KV_HEADS):
            js = slice(j * HEAD_DIM, (j + 1) * HEAD_DIM)
            bias_c, bias_p = bias_ref[j, 0], bias_ref[j, 1]
            q4 = _stack_heads(q_all, j)
            sc = _dot_nt(q4, kc_ref[:, js]) * ATTN_SCALE + bias_c
            sp = _dot_nt(q4, kp_ref[:, js]) * ATTN_SCALE + jnp.where(has_prev, bias_p, -jnp.inf)
            sink = _stack_columns(sk_ref[...], j)
            m = jnp.maximum(jnp.maximum(jnp.max(sc, axis=-1, keepdims=True), jnp.max(sp, axis=-1, keepdims=True)), sink)
            ec = jnp.exp(sc - m)
            ep = jnp.exp(sp - m)
            den = jnp.sum(ec, axis=-1, keepdims=True) + jnp.sum(ep, axis=-1, keepdims=True) + jnp.exp(sink - m)
            inv = 1.0 / den
            o4 = _dot((ec * inv).astype(BF16), vc_ref[:, js]) + _dot((ep * inv).astype(BF16), vp_ref[:, js])
            lse4 = m + jnp.log(den)
            for g in range(Q_PER_KV):
                h = j * Q_PER_KV + g
                o_ref[:, h * HEAD_DIM:(h + 1) * HEAD_DIM] = o4[g * BLOCK:(g + 1) * BLOCK]
                lse_ref[:, h:h + 1] = lse4[g * BLOCK:(g + 1) * BLOCK]
        ao_ref[...] = (o_ref[...] * _silu(za_ref[...])).astype(BF16)

    cur = lambda w: pl.BlockSpec((BLOCK, w), lambda b, n: (b * nb + n, 0))
    prev = lambda w: pl.BlockSpec((BLOCK, w), lambda b, n: (b * nb + jnp.maximum(n - 1, 0), 0))
    return _call(
        body, name="attn_fwd", grid=(n_seq, nb),
        in_specs=[cur(512), cur(128), prev(128), cur(128), prev(128), cur(512), _full((1, N_HEADS))],
        out_specs=[cur(512), cur(512), cur(N_HEADS)],
        out_shape=[_sds((rows, 512)), _sds((rows, 512), BF16), _sds((rows, N_HEADS))],
        scratch_shapes=[_ATTN_BIAS_SCRATCH], compiler_params=_params(32, ("arbitrary", "arbitrary")),
    )(q, k, k, v, v, za, sinks)


def _attn_bwd(q, k, v, za, o, lse, d_ao, sinks, n_seq, seq):
    nb = seq // BLOCK
    rows = q.shape[0]

    def body(q_ref, q2_ref, kc_ref, kp_ref, vc_ref, vp_ref, za_ref, za2_ref, o_ref, lse_ref, lse2_ref,
             d_ref, d2_ref, sk_ref, dq_ref, dk_ref, dv_ref, dza_ref, dsk_ref, bias_ref, delta_ref):
        n = nb - 1 - pl.program_id(1)
        _fill_attn_bias(bias_ref)

        @pl.when(jnp.logical_and(pl.program_id(0) == 0, pl.program_id(1) == 0))
        def _():
            dsk_ref[...] = jnp.zeros_like(dsk_ref)
            delta_ref[...] = jnp.zeros_like(delta_ref)

        has_prev = n > 0
        has_next = n + 1 < nb

        _, gate_vjp = jax.vjp(lambda o_, z_: o_ * _silu(z_), o_ref[...], za_ref[...])
        d_o, d_za = gate_vjp(d_ref[...])
        dza_ref[...] = d_za.astype(BF16)
        d_o2 = d2_ref[...] * _silu(za2_ref[...])
        q_all, q2_all = q_ref[...], q2_ref[...]
        lse_all, lse2_all = lse_ref[...], lse2_ref[...]

        for j in range(KV_HEADS):
            js = slice(j * HEAD_DIM, (j + 1) * HEAD_DIM)
            kc, kp, vc, vp = kc_ref[:, js], kp_ref[:, js], vc_ref[:, js], vp_ref[:, js]
            bias_c, bias_p = bias_ref[j, 0], bias_ref[j, 1]
            q4 = _stack_heads(q_all, j)
            do4b = _stack_heads(d_o, j).astype(BF16)
            lse4 = _stack_columns(lse_all, j)
            pc = jnp.exp(_dot_nt(q4, kc) * ATTN_SCALE + bias_c - lse4)
            pp = jnp.exp(_dot_nt(q4, kp) * ATTN_SCALE + jnp.where(has_prev, bias_p, -jnp.inf) - lse4)
            dpc = _dot_nt(do4b, vc)
            dpp = _dot_nt(do4b, vp)
            delta = jnp.sum(pc * dpc, axis=-1, keepdims=True) + jnp.sum(pp * dpp, axis=-1, keepdims=True)
            delta2 = jnp.where(has_next, delta_ref[j], 0.0)
            delta_ref[j] = delta
            dsc = (pc * (dpc - delta)).astype(BF16)
            dsp = (pp * (dpp - delta)).astype(BF16)
            dq4 = ((_dot(dsc, kc) + _dot(dsp, kp)) * ATTN_SCALE).astype(BF16)
            sink_loss = jnp.exp(_stack_columns(sk_ref[...], j) - lse4) * delta
            for g in range(Q_PER_KV):
                h = j * Q_PER_KV + g
                dq_ref[:, h * HEAD_DIM:(h + 1) * HEAD_DIM] = dq4[g * BLOCK:(g + 1) * BLOCK]
                dsk_ref[0:1, h:h + 1] -= jnp.sum(sink_loss[g * BLOCK:(g + 1) * BLOCK], axis=0, keepdims=True)
            dk = _dot_tn(dsc, q4)
            dv = _dot_tn(pc.astype(BF16), do4b)
            q4n = _stack_heads(q2_all, j)
            do4nb = _stack_heads(d_o2, j).astype(BF16)
            p2 = jnp.exp(_dot_nt(q4n, kc) * ATTN_SCALE + jnp.where(has_next, bias_p, -jnp.inf)
                         - _stack_columns(lse2_all, j))
            ds2 = (p2 * (_dot_nt(do4nb, vc) - delta2)).astype(BF16)
            dk = dk + _dot_tn(ds2, q4n)
            dv = dv + _dot_tn(p2.astype(BF16), do4nb)
            dk_ref[:, js] = (dk * ATTN_SCALE).astype(BF16)
            dv_ref[:, js] = dv.astype(BF16)

    cur = lambda w: pl.BlockSpec((BLOCK, w), lambda b, s: (b * nb + nb - 1 - s, 0))
    prev = lambda w: pl.BlockSpec((BLOCK, w), lambda b, s: (b * nb + jnp.maximum(nb - 2 - s, 0), 0))
    nxt = lambda w: pl.BlockSpec((BLOCK, w), lambda b, s: (b * nb + jnp.minimum(nb - s, nb - 1), 0))
    return _call(
        body, name="attn_bwd", grid=(n_seq, nb),
        in_specs=[cur(512), nxt(512), cur(128), prev(128), cur(128), prev(128), cur(512), nxt(512),
                  cur(512), cur(N_HEADS), nxt(N_HEADS), cur(512), nxt(512), _full((1, N_HEADS))],
        out_specs=[cur(512), cur(128), cur(128), cur(512), _full((1, N_HEADS))],
        out_shape=[_sds((rows, 512), BF16), _sds((rows, 128), BF16), _sds((rows, 128), BF16),
                   _sds((rows, 512), BF16), _sds((1, N_HEADS))],
        scratch_shapes=[_ATTN_BIAS_SCRATCH, pltpu.VMEM((KV_HEADS, _GROUP_ROWS, 1), F32)],
        compiler_params=_params(32, ("arbitrary", "arbitrary")),
    )(q, q, k, k, v, v, za, za, o, lse, lse, d_ao, d_ao, sinks)


def _tail(ssm_out, attn_out, x2d, p2d, target, w_out, g2, w_gate, b_gate, w_proj):
    rows = x2d.shape[0]
    tm = 512

    def body(so_ref, ao_ref, x_ref, p_ref, t_ref, wo_ref, g2_ref, wg_ref, bg_ref, wp_ref,
             dh1_ref, dso_ref, dao_ref, dwo_ref, dwg_ref, dwp_ref, dbg_ref, dg2_ref, loss_ref):
        @pl.when(pl.program_id(0) == 0)
        def _():
            for ref in (dwo_ref, dwg_ref, dwp_ref, dbg_ref, dg2_ref, loss_ref):
                ref[...] = jnp.zeros_like(ref)

        so = so_ref[...]
        ao = ao_ref[...]
        g2 = g2_ref[...]
        mixed = _dot(so, wo_ref[0:512, :]) + _dot(ao, wo_ref[512:1024, :])
        r = lax.rsqrt(jnp.mean(mixed * mixed, axis=-1, keepdims=True) + EPS)
        mr = mixed * r
        h1 = x_ref[...] + mr * g2
        h1b = h1.astype(BF16)
        gate = jax.nn.sigmoid(_dot(h1b, wg_ref[...]) + bg_ref[...])
        pb = p_ref[...].astype(BF16)
        wp_blocks = [slice(j * D_PLE, (j + 1) * D_PLE) for j in range(N_CHIPS)]
        pp = jnp.concatenate([_dot(pb, wp_ref[blk, :]) for blk in wp_blocks], axis=1)
        err = h1 + gate * pp - t_ref[...]
        loss_ref[...] += 0.5 * jnp.sum(jnp.mean(err * err, axis=-1, keepdims=True), axis=0, keepdims=True)

        dh2 = err * (1.0 / D_MODEL)
        d_glin = dh2 * pp * gate * (1.0 - gate)
        d_glin_b = d_glin.astype(BF16)
        dwg_ref[...] += _dot_tn(h1b, d_glin_b)
        dbg_ref[...] += jnp.sum(d_glin, axis=0, keepdims=True)
        d_pp = (dh2 * gate).astype(BF16)
        for blk in wp_blocks:
            dwp_ref[blk, :] += _dot_tn(pb, d_pp[:, blk])
        dh1 = dh2 + _dot_nt(d_glin_b, wg_ref[...])
        dh1_ref[...] = dh1
        dg2_ref[...] += jnp.sum(dh1 * mr, axis=0, keepdims=True)
        a_ = dh1 * g2
        d_mixed = (r * a_ - mr * (r * jnp.mean(a_ * mr, axis=-1, keepdims=True))).astype(BF16)
        dwo_ref[0:512, :] += _dot_tn(so, d_mixed)
        dwo_ref[512:1024, :] += _dot_tn(ao, d_mixed)
        dso_ref[...] = _dot_nt(d_mixed, wo_ref[0:512, :])
        dao_ref[...] = _dot_nt(d_mixed, wo_ref[512:1024, :])

    return _call(
        body, name="tail_fwd_bwd", grid=(rows // tm,),
        in_specs=[_rows(tm, 512), _rows(tm, 512), _rows(tm, D_MODEL), _rows(tm, D_PLE), _rows(tm, D_MODEL),
                  _full((D_MODEL, D_MODEL)), _full((1, D_MODEL)), _full((D_MODEL, D_MODEL)), _full((1, D_MODEL)),
                  _full((N_CHIPS * D_PLE, D_PLE))],
        out_specs=[_rows(tm, D_MODEL), _rows(tm, 512), _rows(tm, 512), _full((D_MODEL, D_MODEL)),
                   _full((D_MODEL, D_MODEL)), _full((N_CHIPS * D_PLE, D_PLE)), _full((1, D_MODEL)), _full((1, D_MODEL)),
                   _full((1, 1))],
        out_shape=[_sds((rows, D_MODEL)), _sds((rows, 512)), _sds((rows, 512)), _sds((D_MODEL, D_MODEL)),
                   _sds((D_MODEL, D_MODEL)), _sds((N_CHIPS * D_PLE, D_PLE)), _sds((1, D_MODEL)), _sds((1, D_MODEL)),
                   _sds((1, 1))],
        compiler_params=_params(52, ("arbitrary",)),
    )(ssm_out, attn_out, x2d, p2d, target, w_out, g2, w_gate, b_gate, w_proj)


def _local_step(x, p, target, pre_norm_g, w_in_t, s5_params, ssm_d, w_glu, b_glu, sinks, w_out, post_norm_g, w_proj,
                w_gate, b_gate):
    n_seq, seq, _ = x.shape
    rows = n_seq * seq
    x2d = x.reshape(rows, D_MODEL)
    p2d = p.reshape(rows, D_PLE)
    t2d = target.reshape(rows, D_MODEL)

    l_re, l_im, bt_re, bt_im, cm_re, cm_im = _s5_params_fwd(*s5_params)

    u_scan, zs, q, k, v, za = _in_proj(x2d, pre_norm_g, w_in_t, n_seq, seq)
    y_scan, h_re, h_im = _s5_scan_fwd(u_scan, bt_re, bt_im, cm_re, cm_im, l_re, l_im, ssm_d, n_seq, seq)
    ssm_out = _glu_fwd(y_scan, zs, w_glu, b_glu, n_seq, seq)
    o, attn_out, lse = _attn_fwd(q, k, v, za, sinks, n_seq, seq)

    dh1, d_so, d_ao, d_w_out, d_w_gate, d_w_proj, d_b_gate, d_g2, loss = _tail(
        ssm_out, attn_out, x2d, p2d, t2d, w_out, post_norm_g, w_gate, b_gate, w_proj)

    dq, dk, dv, dza, d_sinks = _attn_bwd(q, k, v, za, o, lse, d_ao, sinks, n_seq, seq)
    dy_scan, dzs, d_w_glu, d_b_glu = _glu_bwd(y_scan, zs, d_so, w_glu, b_glu, n_seq, seq)
    du_scan, d_bt_re, d_bt_im, d_cm_re, d_cm_im, d_l_re, d_l_im, d_d = _s5_scan_bwd(
        dy_scan, u_scan, h_re, h_im, bt_re, bt_im, cm_re, cm_im, l_re, l_im, ssm_d, n_seq, seq)
    d_lam_re, d_lam_im, d_log_step, d_b_re, d_b_im, d_c_re, d_c_im = _s5_params_bwd(
        s5_params, (d_l_re, d_l_im, d_bt_re, d_bt_im, d_cm_re, d_cm_im))

    grad_x, d_w_in_t, d_g1 = _in_proj_bwd(x2d, dh1, pre_norm_g, w_in_t, du_scan, dzs, dq, dk, dv, dza, n_seq, seq)
    grads = dict(
        pre_norm_g=d_g1, w_in=d_w_in_t, ssm_lam_re=d_lam_re, ssm_lam_im=d_lam_im, ssm_log_step=d_log_step,
        ssm_b_re=d_b_re, ssm_b_im=d_b_im, ssm_c_re=d_c_re, ssm_c_im=d_c_im, ssm_d=d_d, ssm_w_glu=d_w_glu,
        ssm_b_glu=d_b_glu, attn_sinks=d_sinks, w_out=d_w_out, post_norm_g=d_g2, pl_w_proj=d_w_proj,
        pl_w_gate=d_w_gate, pl_b_gate=d_b_gate)
    return grad_x.reshape(x.shape), loss, grads


_BIG = ("w_in", "ssm_w_glu", "w_out", "pl_w_proj", "pl_w_gate")
_BIG_SHARD = {"w_in": (D_IN // N_CHIPS, D_MODEL), "ssm_w_glu": (D_SSM // N_CHIPS, D_SSM),
              "w_out": (D_MODEL // N_CHIPS, D_MODEL), "pl_w_proj": (D_PLE, D_MODEL // N_CHIPS),
              "pl_w_gate": (D_MODEL // N_CHIPS, D_MODEL)}
_SMALL = {"pre_norm_g": (1, D_MODEL), "ssm_lam_re": (SSM_GROUPS, SSM_STATE), "ssm_lam_im": (SSM_GROUPS, SSM_STATE),
          "ssm_log_step": (1, SSM_GROUPS), "ssm_b_re": (D_SSM, SSM_STATE), "ssm_b_im": (D_SSM, SSM_STATE),
          "ssm_c_re": (D_SSM, SSM_STATE), "ssm_c_im": (D_SSM, SSM_STATE), "ssm_d": (1, D_SSM), "ssm_b_glu": (1, D_SSM),
          "attn_sinks": (1, N_HEADS), "post_norm_g": (1, D_MODEL), "pl_b_gate": (1, D_MODEL)}
_VEC_ROWS = ("pre_norm_g", "post_norm_g", "pl_b_gate", "ssm_d", "ssm_b_glu", "attn_sinks", "ssm_log_step", "loss")
_SMALL_GROUPS = (
    ("vec", (8, D_MODEL), tuple((name, r) for r, name in enumerate(_VEC_ROWS))),
    ("lam", (2 * SSM_GROUPS, SSM_STATE), (("ssm_lam_re", 0), ("ssm_lam_im", SSM_GROUPS))),
    ("bc", (4 * D_SSM, SSM_STATE), (("ssm_b_re", 0), ("ssm_b_im", D_SSM), ("ssm_c_re", 2 * D_SSM),
                                    ("ssm_c_im", 3 * D_SSM))),
)
_SMALL_ORDER = tuple(name for _, _, members in _SMALL_GROUPS for name, _ in members)
_WEIGHT_ORDER = ("pre_norm_g", "w_in", "ssm_lam_re", "ssm_lam_im", "ssm_log_step", "ssm_b_re", "ssm_b_im", "ssm_c_re",
                 "ssm_c_im", "ssm_d", "ssm_w_glu", "ssm_b_glu", "attn_sinks", "w_out", "post_norm_g", "pl_w_proj",
                 "pl_w_gate", "pl_b_gate")


def _small_shape(name):
    return (1, 1) if name == "loss" else _SMALL[name]


def _to_kernel_form(name, a):
    a = a[0]
    if name == "w_in":
        return a.T
    if name in ("ssm_b_re", "ssm_b_im"):
        a = a.transpose(0, 2, 1)
    return a.reshape(_SMALL[name]) if name in _SMALL else a


def _from_kernel_form(name, a, shape):
    if name == "w_in":
        a = a.T
    if name in ("ssm_b_re", "ssm_b_im"):
        a = a.reshape(SSM_GROUPS, SSM_GROUP_CH, SSM_STATE).transpose(0, 2, 1)
    return a.reshape(shape)


def _mesh_place():
    x, y, c = lax.axis_index("x"), lax.axis_index("y"), lax.axis_index("c")
    other_chips = ((1 - x, y), (x, 1 - y), (1 - x, 1 - y))
    return x, y, c, other_chips


def _gather_copies(s_refs, g_refs, send_sems, recv_sems, local_sems):
    x, y, c, other_chips = _mesh_place()
    started = []
    for i, (s_ref, g_ref) in enumerate(zip(s_refs, g_refs)):
        rows = s_ref.shape[0]
        half = rows // 2

        def block(chip, g_ref=g_ref, rows=rows, half=half):
            return g_ref.at[pl.ds((2 * chip[0] + chip[1]) * rows + c * half, half), :]

        def copy(k, chip, to, src=None, i=i, block=block):
            return pltpu.make_async_remote_copy(
                src_ref=block(chip) if src is None else src, dst_ref=block(chip), send_sem=send_sems.at[6 * i + k],
                recv_sem=recv_sems.at[6 * i + k], device_id=to, device_id_type=MESH)

        own = pltpu.make_async_copy(s_ref, g_ref.at[pl.ds((2 * x + y) * rows, rows), :], local_sems.at[i])
        own.start()
        first = [copy(k, (x, y), (*chip, c), src=s_ref.at[pl.ds(c * half, half), :])
                 for k, chip in enumerate(other_chips)]
        for cp in first:
            cp.start()
        passed = [copy(3 + k, chip, (x, y, 1 - c)) for k, chip in enumerate(other_chips)]
        started.append((own, first, passed))
    for own, first, passed in started:
        for k in range(3):
            first[k].wait_recv()
            passed[k].start()
    for own, first, passed in started:
        for k in range(3):
            passed[k].wait_recv()
        for cp in first + passed:
            cp.wait_send()
        own.wait()


def _gather_semaphores(n_t):
    return [pltpu.SemaphoreType.DMA((6 * n_t,)), pltpu.SemaphoreType.DMA((6 * n_t,)), pltpu.SemaphoreType.DMA((n_t,))]


def _gather_weights(shards):
    n_t = len(shards)

    def body(*refs):
        _gather_copies(refs[:n_t], refs[n_t:2 * n_t], *refs[2 * n_t + 1:])
        refs[2 * n_t][...] = jnp.zeros_like(refs[2 * n_t])

    any_spec = pl.BlockSpec(memory_space=pl.ANY)
    *full, done = _call(
        body, name="gather_weights", in_specs=[any_spec] * n_t,
        out_specs=[any_spec] * n_t + [pl.BlockSpec(memory_space=pltpu.VMEM)],
        out_shape=[_sds((N_CHIPS * s.shape[0], s.shape[1]), s.dtype) for s in shards]
        + [jax.ShapeDtypeStruct((8, LANES), F32)],
        scratch_shapes=_gather_semaphores(n_t),
    )(*shards)
    return full, done[0, 0]


def _gather_weights_beside(shards):
    n_t = len(shards)
    hbm = pltpu.MemorySpace.HBM
    s_refs = [jax.new_ref(s, memory_space=hbm) for s in shards]
    g_refs = [jax.empty_ref(jax.ShapeDtypeStruct((N_CHIPS * s.shape[0], s.shape[1]), s.dtype), memory_space=hbm)
              for s in shards]

    def launch(send_sems, recv_sems, local_sems):
        x, y, c, other_chips = _mesh_place()
        peers = [(*chip, c) for chip in other_chips] + [(x, y, 1 - c)]
        barrier = pltpu.get_barrier_semaphore()
        for peer in peers:
            pl.semaphore_signal(barrier, inc=1, device_id=peer, device_id_type=MESH)
        pl.semaphore_wait(barrier, len(peers))
        _gather_copies(s_refs, g_refs, send_sems, recv_sems, local_sems)

    pl.kernel(launch, mesh=plsc.ScalarSubcoreMesh(axis_name="sequencer", num_cores=1), name="gather_weights_beside",
              scratch_types=_gather_semaphores(n_t), compiler_params=pltpu.CompilerParams(collective_id=1))()
    return [g[...] for g in g_refs]


_RELATIONS = tuple(((r >> 2) & 1, (r >> 1) & 1, r & 1) for r in range(1, 8))


def _related(place, relation):
    return tuple(1 - a if flip else a for a, flip in zip(place, relation))


def _scatter_beside(mats):
    hbm = pltpu.MemorySpace.HBM
    src_refs = [jax.new_ref(a, memory_space=hbm) for a in mats]
    land_refs = [jax.empty_ref(jax.ShapeDtypeStruct((7, a.shape[0] // 8, a.shape[1]), a.dtype), memory_space=hbm)
                 for a in mats]

    def launch(send_sems, recv_sems):
        me = (lax.axis_index("x"), lax.axis_index("y"), lax.axis_index("c"))
        peers = [_related(me, rel) for rel in _RELATIONS]
        barrier = pltpu.get_barrier_semaphore()
        for peer in peers:
            pl.semaphore_signal(barrier, inc=1, device_id=peer, device_id_type=MESH)
        pl.semaphore_wait(barrier, len(peers))
        copies = []
        for i, (src, land) in enumerate(zip(src_refs, land_refs)):
            hr = land.shape[1]
            for k, (tx, ty, tc) in enumerate(peers):
                rows = pl.ds((2 * tx + ty) * 2 * hr + tc * hr, hr)
                copies.append(pltpu.make_async_remote_copy(
                    src_ref=src.at[rows, :], dst_ref=land.at[k], send_sem=send_sems.at[7 * i + k],
                    recv_sem=recv_sems.at[7 * i + k], device_id=(tx, ty, tc), device_id_type=MESH))
                copies[-1].start()
        for cp in copies:
            cp.wait()

    n_sems = 7 * len(mats)
    pl.kernel(launch, mesh=plsc.ScalarSubcoreMesh(axis_name="sequencer", num_cores=1), name="scatter_beside",
              scratch_types=[pltpu.SemaphoreType.DMA((n_sems,)), pltpu.SemaphoreType.DMA((n_sems,))],
              compiler_params=pltpu.CompilerParams(collective_id=2))()
    return [ref[...] for ref in land_refs]


def _exchange_grads(big, small, landed):
    n_t = len(big)
    n_g = len(_SMALL_GROUPS)
    names = _SMALL_ORDER
    halves = [(b.shape[0] // N_CHIPS // 2, b.shape[1]) for b in big]
    early = sorted(landed)
    late = [i for i in range(n_t) if i not in landed]
    n_sems = 4 * n_g + 7 * len(late) + n_t
    small_sem0, block_sem0 = n_t, n_t + len(names)
    early_sem0 = block_sem0 + N_CHIPS * len(late)

    def body(*refs):
        pos = 0

        def take(n):
            nonlocal pos
            pos += n
            return refs[pos - n:pos]

        big_refs, small_refs = take(n_t), dict(zip(names, take(len(names))))
        land_refs = dict(zip(early, take(len(early))))
        out_refs, small_out_refs = take(n_t), dict(zip(names, take(len(names))))
        per_late = lambda: dict(zip(late, take(len(late))))
        ga, gb, pme, send_b, recv_b = per_late(), per_late(), take(n_t), per_late(), per_late()
        own_e, land_e = dict(zip(early, take(len(early)))), dict(zip(early, take(len(early))))
        s_own, s_sib, s_chips, s_pair = take(n_g), take(n_g), take(n_g), take(n_g)
        stage = dict(zip(names, take(len(names))))
        send_sems, recv_sems, local_sems = take(3)
        x, y, c, other_chips = _mesh_place()
        me = 2 * x + y
        sibling = (x, y, 1 - c)
        sem_at = iter(range(n_sems))

        def remote(src, dst, to):
            k = next(sem_at)
            return pltpu.make_async_remote_copy(src_ref=src, dst_ref=dst, send_sem=send_sems.at[k],
                                                recv_sem=recv_sems.at[k], device_id=to, device_id_type=MESH)

        loads = [pltpu.make_async_copy(small_refs[name], stage[name], local_sems.at[small_sem0 + a])
                 for a, name in enumerate(names)]
        for cp in loads:
            cp.start()
        for cp in loads:
            cp.wait()
        small_swaps = []
        for gi, (_, _, members) in enumerate(_SMALL_GROUPS):
            s_own[gi][...] = jnp.zeros_like(s_own[gi])
            for name, r0 in members:
                r, n = _small_shape(name)
                s_own[gi][r0:r0 + r, 0:n] = stage[name][...]
            small_swaps.append(remote(s_own[gi], s_sib[gi], sibling))
            small_swaps[gi].start()
        order = sorted(late, key=lambda i: halves[i][0] * halves[i][1])
        own_loads, big_swaps = {}, {}
        for i in order:
            hr = halves[i][0]
            own_loads[i], big_swaps[i] = [], []
            for j in range(N_CHIPS):
                mine = big_refs[i].at[pl.ds(j * 2 * hr + c * hr, hr), :]
                theirs = big_refs[i].at[pl.ds(j * 2 * hr + (1 - c) * hr, hr), :]
                sem = local_sems.at[block_sem0 + N_CHIPS * late.index(i) + j]
                own_loads[i].append(pltpu.make_async_copy(mine, ga[i].at[j], sem))
                own_loads[i][j].start()
                big_swaps[i].append(remote(theirs, gb[i].at[j], sibling))
                big_swaps[i][j].start()
        early_loads = {}
        for e, i in enumerate(early):
            hr = halves[i][0]
            mine = big_refs[i].at[pl.ds(me * 2 * hr + c * hr, hr), :]
            early_loads[i] = [pltpu.make_async_copy(mine, own_e[i], local_sems.at[early_sem0 + 2 * e]),
                              pltpu.make_async_copy(land_refs[i], land_e[i], local_sems.at[early_sem0 + 2 * e + 1])]
            for cp in early_loads[i]:
                cp.start()
        small_sends = []
        for gi in range(n_g):
            small_swaps[gi].wait_recv()
            s_pair[gi][...] = s_own[gi][...] + s_sib[gi][...]
            small_sends.append([remote(s_pair[gi], s_chips[gi].at[k], (*chip, c)) for k, chip in enumerate(other_chips)])
            for cp in small_sends[gi]:
                cp.start()

        def pair_sum(i, j):
            return ga[i][j] + gb[i][j]

        big_sends = {}
        for i in order:
            for j in range(N_CHIPS):
                own_loads[i][j].wait()
                big_swaps[i][j].wait_recv()
            big_sends[i] = []
            for k, chip in enumerate(other_chips):
                send_b[i][k] = pair_sum(i, 2 * chip[0] + chip[1]).astype(BF16)
                big_sends[i].append(remote(send_b[i].at[k], recv_b[i].at[k], (*chip, c)))
                big_sends[i][k].start()
        last_swaps, keeps = {}, {}
        for i in early + order:
            hr = halves[i][0]
            if i in landed:
                for cp in early_loads[i]:
                    cp.wait()
                total = own_e[i][...]
                for k in range(len(_RELATIONS)):
                    total = total + land_e[i][k]
                pme[i][...] = total
            else:
                for k in range(3):
                    big_sends[i][k].wait_recv()
                pme[i][...] = ((pair_sum(i, me) + recv_b[i][0].astype(F32)) + recv_b[i][1].astype(F32)) + recv_b[i][2].astype(F32)
            mine = out_refs[i].at[pl.ds(c * hr, hr), :]
            keeps[i] = pltpu.make_async_copy(pme[i], mine, local_sems.at[i])
            keeps[i].start()
            last_swaps[i] = remote(pme[i], mine, sibling)
            last_swaps[i].start()

        for gi, (_, _, members) in enumerate(_SMALL_GROUPS):
            for k in range(3):
                small_sends[gi][k].wait_recv()
            total = None
            for j in range(N_CHIPS):
                rel = jnp.bitwise_xor(j, me)
                term = jnp.where(rel == 0, s_pair[gi][...], jnp.where(
                    rel == 2, s_chips[gi][0], jnp.where(rel == 1, s_chips[gi][1], s_chips[gi][2])))
                total = term if total is None else total + term
            s_sib[gi][...] = total
            for name, r0 in members:
                r, n = _small_shape(name)
                stage[name][...] = s_sib[gi][r0:r0 + r, 0:n]
        stores = [pltpu.make_async_copy(stage[name], small_out_refs[name], local_sems.at[small_sem0 + a])
                  for a, name in enumerate(names)]
        for cp in stores:
            cp.start()

        for i in range(n_t):
            last_swaps[i].wait_recv()
            keeps[i].wait()
        for cp in stores:
            cp.wait()
        groups = list(big_swaps.values()) + small_sends + list(big_sends.values())
        for cp in small_swaps + [cp for group in groups for cp in group] + list(last_swaps.values()):
            cp.wait_send()

    any_spec = pl.BlockSpec(memory_space=pl.ANY)
    small_shapes = [_sds(_small_shape(n)) for n in names]
    group_shapes = [shape for _, shape, _ in _SMALL_GROUPS]
    vmem = lambda which, dtype, lead=(): [pltpu.VMEM(lead + halves[i], dtype) for i in which]
    outs = _call(
        body, name="exchange_grads",
        in_specs=[any_spec] * (n_t + len(names) + len(early)),
        out_specs=[any_spec] * (n_t + len(names)),
        out_shape=[_sds((b.shape[0] // N_CHIPS, b.shape[1])) for b in big] + small_shapes,
        scratch_shapes=(vmem(late, F32, (N_CHIPS,)) + vmem(late, F32, (N_CHIPS,)) + vmem(range(n_t), F32)
                        + vmem(late, BF16, (3,)) + vmem(late, BF16, (3,))
                        + vmem(early, F32) + vmem(early, F32, (len(_RELATIONS),))
                        + [pltpu.VMEM(s, F32) for s in group_shapes] * 2 + [pltpu.VMEM((3,) + s, F32) for s in group_shapes]
                        + [pltpu.VMEM(s, F32) for s in group_shapes]
                        + [pltpu.VMEM(_small_shape(n), F32) for n in names]
                        + [pltpu.SemaphoreType.DMA((n_sems,)), pltpu.SemaphoreType.DMA((n_sems,)),
                           pltpu.SemaphoreType.DMA((early_sem0 + 2 * len(early),))]),
        compiler_params=_params(48),
    )(*big, *[small[n] for n in names], *[landed[i] for i in early])
    return list(outs[:n_t]), dict(zip(names, outs[n_t:n_t + len(names)]))


def _adamw_update(w, g, m, v):
    m = ADAM_B1 * m + (1.0 - ADAM_B1) * g
    v = ADAM_B2 * v + (1.0 - ADAM_B2) * (g * g)
    m_hat = m / (1.0 - ADAM_B1 ** ADAM_STEP)
    v_hat = v / (1.0 - ADAM_B2 ** ADAM_STEP)
    return -ADAM_LR * (m_hat / (jnp.sqrt(v_hat) + ADAM_EPS) + ADAM_WD * w), m, v


def _adamw(w, g, m, v, grid, name):
    n_t = len(w)

    def body(*refs):
        ins, outs = refs[:4 * n_t], refs[4 * n_t:]
        for i in range(n_t):
            w_, g_, m_, v_ = [ins[a * n_t + i][...] for a in range(4)]
            vals = (g_,) + _adamw_update(w_, g_, m_, v_)
            for a in range(4):
                outs[a * n_t + i][...] = vals[a]

    specs = [pl.BlockSpec((a.shape[0] // grid, a.shape[1]), lambda i: (i, 0)) for a in w]
    shapes = [_sds(a.shape) for a in w]
    outs = _call(
        body, name=name, grid=(grid,), in_specs=specs * 4, out_specs=specs * 4, out_shape=shapes * 4,
        compiler_params=_params(40, ("arbitrary",)),
    )(*w, *g, *m, *v)
    return [outs[a * n_t:(a + 1) * n_t] for a in range(4)]


def kernel(x, p, pre_norm_g, w_in, ssm_lam_re, ssm_lam_im, ssm_log_step, ssm_b_re, ssm_b_im, ssm_c_re, ssm_c_im, ssm_d, ssm_w_glu, ssm_b_glu, attn_sinks, w_out, post_norm_g, pl_w_proj, pl_w_gate, pl_b_gate, loss_target, m_pre_norm_g, m_w_in, m_ssm_lam_re, m_ssm_lam_im, m_ssm_log_step, m_ssm_b_re, m_ssm_b_im, m_ssm_c_re, m_ssm_c_im, m_ssm_d, m_ssm_w_glu, m_ssm_b_glu, m_attn_sinks, m_w_out, m_post_norm_g, m_pl_w_proj, m_pl_w_gate, m_pl_b_gate, v_pre_norm_g, v_w_in, v_ssm_lam_re, v_ssm_lam_im, v_ssm_log_step, v_ssm_b_re, v_ssm_b_im, v_ssm_c_re, v_ssm_c_im, v_ssm_d, v_ssm_w_glu, v_ssm_b_glu, v_attn_sinks, v_w_out, v_post_norm_g, v_pl_w_proj, v_pl_w_gate, v_pl_b_gate):
    weights = dict(pre_norm_g=pre_norm_g, w_in=w_in, ssm_lam_re=ssm_lam_re, ssm_lam_im=ssm_lam_im,
                   ssm_log_step=ssm_log_step, ssm_b_re=ssm_b_re, ssm_b_im=ssm_b_im, ssm_c_re=ssm_c_re,
                   ssm_c_im=ssm_c_im, ssm_d=ssm_d, ssm_w_glu=ssm_w_glu, ssm_b_glu=ssm_b_glu, attn_sinks=attn_sinks,
                   w_out=w_out, post_norm_g=post_norm_g, pl_w_proj=pl_w_proj, pl_w_gate=pl_w_gate, pl_b_gate=pl_b_gate)
    m_in = dict(pre_norm_g=m_pre_norm_g, w_in=m_w_in, ssm_lam_re=m_ssm_lam_re, ssm_lam_im=m_ssm_lam_im,
                ssm_log_step=m_ssm_log_step, ssm_b_re=m_ssm_b_re, ssm_b_im=m_ssm_b_im, ssm_c_re=m_ssm_c_re,
                ssm_c_im=m_ssm_c_im, ssm_d=m_ssm_d, ssm_w_glu=m_ssm_w_glu, ssm_b_glu=m_ssm_b_glu,
                attn_sinks=m_attn_sinks, w_out=m_w_out, post_norm_g=m_post_norm_g, pl_w_proj=m_pl_w_proj,
                pl_w_gate=m_pl_w_gate, pl_b_gate=m_pl_b_gate)
    v_in = dict(pre_norm_g=v_pre_norm_g, w_in=v_w_in, ssm_lam_re=v_ssm_lam_re, ssm_lam_im=v_ssm_lam_im,
                ssm_log_step=v_ssm_log_step, ssm_b_re=v_ssm_b_re, ssm_b_im=v_ssm_b_im, ssm_c_re=v_ssm_c_re,
                ssm_c_im=v_ssm_c_im, ssm_d=v_ssm_d, ssm_w_glu=v_ssm_w_glu, ssm_b_glu=v_ssm_b_glu,
                attn_sinks=v_attn_sinks, w_out=v_w_out, post_norm_g=v_post_norm_g, pl_w_proj=v_pl_w_proj,
                pl_w_gate=v_pl_w_gate, pl_b_gate=v_pl_b_gate)

    def two_d(tree):
        return {k: _to_kernel_form(k, a) for k, a in tree.items()}

    w2, m2, v2 = two_d(weights), two_d(m_in), two_d(v_in)

    (w_in_full,), gathered = _gather_weights([w2["w_in"].astype(BF16)])
    rest = _gather_weights_beside([(w2[n] + gathered).astype(BF16) for n in _BIG[1:]])
    full = dict(zip(_BIG, [w_in_full] + rest))
    s5_params = tuple(w2[n] for n in ("ssm_lam_re", "ssm_lam_im", "ssm_log_step", "ssm_b_re", "ssm_b_im", "ssm_c_re",
                                      "ssm_c_im"))
    grad_x, loss, grads = _local_step(
        x, p, loss_target, w2["pre_norm_g"], full["w_in"], s5_params, w2["ssm_d"], full["ssm_w_glu"], w2["ssm_b_glu"],
        w2["attn_sinks"], full["w_out"], w2["post_norm_g"], full["pl_w_proj"], full["pl_w_gate"], w2["pl_b_gate"])

    sent_early = ("w_out", "pl_w_gate", "pl_w_proj")
    landed = dict(zip([_BIG.index(n) for n in sent_early], _scatter_beside([grads[n] for n in sent_early])))
    g_big, g_small = _exchange_grads([grads[n] for n in _BIG], {**{n: grads[n] for n in _SMALL}, "loss": loss}, landed)
    g_big = dict(zip(_BIG, g_big))
    total_loss = g_small.pop("loss")

    big_out = _adamw([w2[n] for n in _BIG], [g_big[n] for n in _BIG], [m2[n] for n in _BIG], [v2[n] for n in _BIG],
                     8, "adamw_matrices")
    small_names = tuple(_SMALL)
    small_out = _adamw([w2[n] for n in small_names], [g_small[n] for n in small_names], [m2[n] for n in small_names],
                       [v2[n] for n in small_names], 1, "adamw_small")

    results = [{**dict(zip(_BIG, big_part)), **dict(zip(small_names, small_part))}
               for big_part, small_part in zip(big_out, small_out)]
    flat = [_from_kernel_form(name, r[name], weights[name].shape) for r in results for name in _WEIGHT_ORDER]
    return (total_loss.reshape(()), grad_x, *flat)
```

```python
import math

import jax
import jax.numpy as jnp
from jax import lax
from jax.experimental import pallas as pl
from jax.experimental.pallas import tpu as pltpu
from jax.experimental.pallas import tpu_sc as plsc

F32 = jnp.float32
BF16 = jnp.bfloat16

D_MODEL = 1024
D_SSM = 512
D_ATTN = 512
SSM_GROUPS = 32
SSM_GROUP_CH = 16
SSM_STATE = 64
SSM_LANES = SSM_GROUPS * SSM_STATE
HEAD_DIM = 64
N_HEADS = 8
KV_HEADS = 2
Q_PER_KV = 4
WINDOW = 128
BLOCK = 128
D_PLE = 256
D_IN = 2304
EPS = 1e-6
ATTN_SCALE = 1.0 / math.sqrt(HEAD_DIM)

ADAM_LR = 0.001
ADAM_B1 = 0.9
ADAM_B2 = 0.999
ADAM_EPS = 1e-08
ADAM_WD = 0.01
ADAM_STEP = 10

N_CHIPS = 4
LANES = 128
SCAN_CHUNKS = 8
SCAN_TILE_STEPS = 32
SCAN_LANE_CHUNK = 512
MIB = 2 ** 20
MESH = pl.DeviceIdType.MESH


def _dot(a, b):
    return jnp.dot(a, b, preferred_element_type=F32)


def _dot_nt(a, b):
    return lax.dot_general(a, b, (((1,), (1,)), ((), ())), preferred_element_type=F32)


def _dot_tn(a, b):
    return lax.dot_general(a, b, (((0,), (0,)), ((), ())), preferred_element_type=F32)


def _params(vmem_mib, semantics=None):
    kw = dict(vmem_limit_bytes=vmem_mib * MIB)
    if semantics is not None:
        kw["dimension_semantics"] = semantics
    return pltpu.CompilerParams(**kw)


def _full(shape):
    nd = len(shape)
    return pl.BlockSpec(shape, lambda *_: (0,) * nd, pipeline_mode=pl.Buffered(1))


def _rows(tm, width):
    return pl.BlockSpec((tm, width), lambda i: (i, 0))


def _sds(shape, dtype=F32):
    return pltpu.HBM(shape, dtype)


def _call(body, **kw):
    fn = pl.pallas_call(body, **kw)
    return lambda *args: fn(*[pltpu.with_memory_space_constraint(a, pltpu.HBM) for a in args])


def _silu(z):
    return z * jax.nn.sigmoid(z)


def _in_proj(x2d, g1, w_in_t, n_seq, seq):
    rows = x2d.shape[0]
    tm = 512
    slab, steps, _, _ = _scan_geometry(n_seq, seq)

    def body(x_ref, g_ref, w_ref, *out_refs):
        u_parts, (zs_ref, q_ref, k_ref, v_ref, za_ref) = out_refs[:_SCAN_PARTS], out_refs[_SCAN_PARTS:]
        x = x_ref[...]
        r = lax.rsqrt(jnp.mean(x * x, axis=-1, keepdims=True) + EPS)
        hn = (x * r * g_ref[...]).astype(BF16)

        def proj(a, b):
            return _dot_nt(hn, w_ref[a:b, :])

        _store_chunks(u_parts, pl.program_id(0) * (tm // steps), proj(0, 512), steps, slab)
        zs_ref[...] = proj(512, 1024)
        q_ref[...] = proj(1024, 1536).astype(BF16)
        k_ref[...] = proj(1536, 1664).astype(BF16)
        v_ref[...] = proj(1664, 1792).astype(BF16)
        za_ref[...] = proj(1792, 2304)

    *u_parts, zs, q, k, v, za = _call(
        body, name="in_proj", grid=(rows // tm,),
        in_specs=[_rows(tm, D_MODEL), _full((1, D_MODEL)), _full((D_IN, D_MODEL))],
        out_specs=_whole_parts(rows) + [_rows(tm, 512), _rows(tm, 512), _rows(tm, 128), _rows(tm, 128), _rows(tm, 512)],
        out_shape=_part_shapes(rows) + [_sds((rows, 512)), _sds((rows, 512), BF16), _sds((rows, 128), BF16),
                                        _sds((rows, 128), BF16), _sds((rows, 512))],
        compiler_params=_params(48, ("arbitrary",)),
    )(x2d, g1, w_in_t)
    return u_parts, zs, q, k, v, za


def _in_proj_bwd(x2d, dh1, g1, w_in_t, du_parts, dzs, dq, dk, dv, dza, n_seq, seq):
    rows = x2d.shape[0]
    tm = 256
    slab, steps, _, _ = _scan_geometry(n_seq, seq)
    pieces = ((0, 512), (512, 1024), (1024, 1536), (1536, 1664), (1664, 1792), (1792, 2304))

    def body(x_ref, dh1_ref, g_ref, w_ref, *refs):
        du_parts, (dzs_ref, dq_ref, dk_ref, dv_ref, dza_ref, gx_ref, dw_ref, dg_ref) = refs[:_SCAN_PARTS], refs[_SCAN_PARTS:]

        @pl.when(pl.program_id(0) == 0)
        def _():
            dw_ref[...] = jnp.zeros_like(dw_ref)
            dg_ref[...] = jnp.zeros_like(dg_ref)

        x = x_ref[...]
        g = g_ref[...]
        r = lax.rsqrt(jnp.mean(x * x, axis=-1, keepdims=True) + EPS)
        xr = x * r
        hn = (xr * g).astype(BF16)
        dhn = jnp.zeros((tm, D_MODEL), F32)
        du = _load_chunks(du_parts, pl.program_id(0) * (tm // steps), tm // steps, steps, slab)
        for (a, b), piece in zip(pieces, (du, dzs_ref[...], dq_ref[...], dk_ref[...], dv_ref[...], dza_ref[...])):
            piece = piece.astype(BF16)
            dhn = dhn + _dot(piece, w_ref[a:b, :])
            dw_ref[a:b, :] += _dot_tn(piece, hn)
        dg_ref[...] += jnp.sum(dhn * xr, axis=0, keepdims=True)
        a_ = dhn * g
        gx_ref[...] = dh1_ref[...] + r * a_ - xr * (r * jnp.mean(a_ * xr, axis=-1, keepdims=True))

    return _call(
        body, name="in_proj_bwd", grid=(rows // tm,),
        in_specs=[_rows(tm, D_MODEL), _rows(tm, D_MODEL), _full((1, D_MODEL)), _full((D_IN, D_MODEL))]
        + _whole_parts(rows) + [_rows(tm, 512), _rows(tm, 512), _rows(tm, 128), _rows(tm, 128), _rows(tm, 512)],
        out_specs=[_rows(tm, D_MODEL), _full((D_IN, D_MODEL)), _full((1, D_MODEL))],
        out_shape=[_sds((rows, D_MODEL)), _sds((D_IN, D_MODEL)), _sds((1, D_MODEL))],
        compiler_params=_params(52, ("arbitrary",)),
    )(x2d, dh1, g1, w_in_t, *du_parts, dzs, dq, dk, dv, dza)


def _iota(shape, axis):
    return lax.broadcasted_iota(jnp.int32, shape, axis)


def _exact_dot(a, b):
    return jnp.dot(a, b, precision=lax.Precision.HIGHEST, preferred_element_type=F32)


_HALF_GROUPS = SSM_GROUPS // 2
_N_SHIFT = SSM_STATE.bit_length() - 1
_P_SHIFT = SSM_GROUP_CH.bit_length() - 1


def _s5_operands(lam_re, lam_im, log_step, b_re, b_im, c_re, c_im):
    g, n, p = SSM_GROUPS, SSM_STATE, SSM_GROUP_CH
    gn, gp, hn_, hp = g * n, g * p, _HALF_GROUPS * n, _HALF_GROUPS * p
    eye_g = _iota((g, g), 0) == _iota((g, g), 1)
    step = jnp.sum(jnp.where(eye_g, jnp.exp(log_step), 0.0), axis=1, keepdims=True)
    a_re = lam_re * step
    a_im = lam_im * step
    mag = jnp.exp(a_re)
    lbar_re = mag * jnp.cos(a_im)
    lbar_im = mag * jnp.sin(a_im)
    n_re = lbar_re - 1.0
    den = lam_re * lam_re + lam_im * lam_im
    f_re = (n_re * lam_re + lbar_im * lam_im) / den
    f_im = (lbar_im * lam_re - n_re * lam_im) / den

    spread_n = (_iota((n, gn), 0) == (_iota((n, gn), 1) & (n - 1))).astype(F32)
    own_g = _iota((g, gn), 0) == (_iota((g, gn), 1) >> _N_SHIFT)

    def to_row(a):
        return jnp.sum(jnp.where(own_g, _exact_dot(a, spread_n), 0.0), axis=0, keepdims=True)

    per_group = ((_iota((gp, g), 0) >> _P_SHIFT) == _iota((gp, g), 1)).astype(F32)
    fx_re, fx_im = _exact_dot(per_group, f_re), _exact_dot(per_group, f_im)
    bbar_re = fx_re * b_re - fx_im * b_im
    bbar_im = fx_re * b_im + fx_im * b_re

    tile_n = (_iota((n, hn_), 0) == (_iota((n, hn_), 1) & (n - 1))).astype(F32)
    same_group = (_iota((hp, hn_), 0) >> _P_SHIFT) == (_iota((hp, hn_), 1) >> _N_SHIFT)

    def embed(a, hf):
        return jnp.where(same_group, _exact_dot(a[hf * hp:(hf + 1) * hp], tile_n), 0.0)

    return (to_row(lbar_re), to_row(lbar_im), embed(bbar_re, 0), embed(bbar_re, 1), embed(bbar_im, 0),
            embed(bbar_im, 1), embed(c_re, 0), embed(c_re, 1), embed(c_im, 0), embed(c_im, 1))


_S5_PARAM_SHAPES = ((SSM_GROUPS, SSM_STATE), (SSM_GROUPS, SSM_STATE), (1, SSM_GROUPS),
                    (D_SSM, SSM_STATE), (D_SSM, SSM_STATE), (D_SSM, SSM_STATE), (D_SSM, SSM_STATE))
_CM_SHAPE = (2, _HALF_GROUPS * SSM_GROUP_CH, _HALF_GROUPS * SSM_STATE)
_S5_OPERAND_SHAPES = ((1, SSM_LANES), (1, SSM_LANES), _CM_SHAPE, _CM_SHAPE, _CM_SHAPE, _CM_SHAPE)


def _s5_params_fwd(*params):
    def body(*refs):
        ins, (lre_ref, lim_ref, btre_ref, btim_ref, cmre_ref, cmim_ref) = refs[:7], refs[7:]
        vals = _s5_operands(*[r[...] for r in ins])
        lre_ref[...] = vals[0]
        lim_ref[...] = vals[1]
        for ref, pair in zip((btre_ref, btim_ref, cmre_ref, cmim_ref), (vals[2:4], vals[4:6], vals[6:8], vals[8:10])):
            ref[0] = pair[0].astype(BF16)
            ref[1] = pair[1].astype(BF16)

    dtypes = (F32, F32, BF16, BF16, BF16, BF16)
    return _call(
        body, name="s5_params_fwd",
        in_specs=[_full(s) for s in _S5_PARAM_SHAPES], out_specs=[_full(s) for s in _S5_OPERAND_SHAPES],
        out_shape=[_sds(s, d) for s, d in zip(_S5_OPERAND_SHAPES, dtypes)], compiler_params=_params(32),
    )(*params)


def _s5_params_bwd(params, cotangents):
    def body(*refs):
        ins, (dlre, dlim, dbtre, dbtim, dcmre, dcmim), outs = refs[:7], refs[7:13], refs[13:]
        _, vjp = jax.vjp(_s5_operands, *[r[...] for r in ins])
        cts = (dlre[...], dlim[...], dbtre[0], dbtre[1], dbtim[0], dbtim[1], dcmre[0], dcmre[1], dcmim[0], dcmim[1])
        for ref, val in zip(outs, vjp(cts)):
            ref[...] = val

    return _call(
        body, name="s5_params_bwd",
        in_specs=[_full(s) for s in _S5_PARAM_SHAPES + _S5_OPERAND_SHAPES],
        out_specs=[_full(s) for s in _S5_PARAM_SHAPES],
        out_shape=[_sds(s) for s in _S5_PARAM_SHAPES], compiler_params=_params(48),
    )(*params, *cotangents)


def _scan_geometry(n_seq, seq):
    slab = n_seq * SCAN_CHUNKS
    steps = seq // SCAN_CHUNKS
    tile_rows = slab * SCAN_TILE_STEPS
    n_tiles = steps // SCAN_TILE_STEPS
    return slab, steps, tile_rows, n_tiles


_SCAN_PARTS = D_SSM // LANES


def _whole_parts(rows):
    return [_full((rows, LANES))] * _SCAN_PARTS


def _part_shapes(rows):
    return [_sds((rows, LANES))] * _SCAN_PARTS


def _load_chunks(parts, first_chunk, n_chunks, steps, slab):
    return jnp.concatenate([
        jnp.concatenate([ref[pl.ds(first_chunk + q, steps, stride=slab), :] for ref in parts], axis=1)
        for q in range(n_chunks)], axis=0)


def _store_chunks(parts, first_chunk, value, steps, slab):
    for q in range(value.shape[0] // steps):
        for j, ref in enumerate(parts):
            ref[pl.ds(first_chunk + q, steps, stride=slab), :] = value[q * steps:(q + 1) * steps,
                                                                     j * LANES:(j + 1) * LANES]


def _join_parts(parts):
    return jnp.concatenate([ref[...] for ref in parts], axis=1)


def _split_parts(parts, value):
    for j, ref in enumerate(parts):
        ref[...] = value[:, j * LANES:(j + 1) * LANES]


def _complex_power(re, im, n):
    out = None
    while n:
        if n & 1:
            out = (re, im) if out is None else (out[0] * re - out[1] * im, out[0] * im + out[1] * re)
        n >>= 1
        if n:
            re, im = re * re - im * im, 2.0 * re * im
    return out


def _chunk_carry(sum_re, sum_im, carry_re, carry_im, a_re, a_im, n_seq, reverse):
    carry_re[...] = jnp.zeros_like(carry_re)
    carry_im[...] = jnp.zeros_like(carry_im)
    for s in range(n_seq):
        order = range(SCAN_CHUNKS - 2, -1, -1) if reverse else range(1, SCAN_CHUNKS)
        for c in order:
            r = s * SCAN_CHUNKS + c
            p = r + 1 if reverse else r - 1
            p_re, p_im = carry_re[p:p + 1, :], carry_im[p:p + 1, :]
            carry_re[r:r + 1, :] = a_re * p_re - a_im * p_im + sum_re[p:p + 1, :]
            carry_im[r:r + 1, :] = a_re * p_im + a_im * p_re + sum_im[p:p + 1, :]


def _s5_scan_fwd(u_parts, bt_re, bt_im, cm_re, cm_im, lbar_re, lbar_im, d_row, n_seq, seq):
    slab, steps, tile_rows, n_tiles = _scan_geometry(n_seq, seq)
    rows = u_parts[0].shape[0]

    def body(*refs):
        u_refs, refs = refs[:_SCAN_PARTS], refs[_SCAN_PARTS:]
        (bre_ref, bim_ref, cre_ref, cim_ref, lre_ref, lim_ref, d_ref), refs = refs[:7], refs[7:]
        y_refs, (hre_ref, him_ref, st_re, st_im, h0_re, h0_im, buf_re, buf_im) = refs[:_SCAN_PARTS], refs[_SCAN_PARTS:]
        second = pl.program_id(0) == 1
        i = pl.program_id(1)

        @pl.when(jnp.logical_and(i == 0, jnp.logical_not(second)))
        def _():
            st_re[...] = jnp.zeros_like(st_re)
            st_im[...] = jnp.zeros_like(st_im)

        u = _join_parts(u_refs)
        ub = u.astype(BF16)
        for hf in range(2):
            cols = slice(hf * 1024, (hf + 1) * 1024)
            buf_re[:, cols] = _dot(ub[:, hf * 256:(hf + 1) * 256], bre_ref[hf])
            buf_im[:, cols] = _dot(ub[:, hf * 256:(hf + 1) * 256], bim_ref[hf])

        for lc in range(SSM_LANES // SCAN_LANE_CHUNK):
            cols = slice(lc * SCAN_LANE_CHUNK, (lc + 1) * SCAN_LANE_CHUNK)
            l_re = jnp.broadcast_to(lre_ref[:, cols], (slab, SCAN_LANE_CHUNK))
            l_im = jnp.broadcast_to(lim_ref[:, cols], (slab, SCAN_LANE_CHUNK))

            def scan_tile(keep_states):
                def step(t, carry):
                    s_re, s_im = carry
                    r0 = pl.multiple_of(t * slab, slab)
                    n_re = l_re * s_re - l_im * s_im + buf_re[pl.ds(r0, slab), cols]
                    n_im = l_re * s_im + l_im * s_re + buf_im[pl.ds(r0, slab), cols]
                    if keep_states:
                        buf_re[pl.ds(r0, slab), cols] = n_re
                        buf_im[pl.ds(r0, slab), cols] = n_im
                    return n_re, n_im

                s_re, s_im = lax.fori_loop(0, SCAN_TILE_STEPS, step, (st_re[:, cols], st_im[:, cols]), unroll=True)
                st_re[:, cols] = s_re
                st_im[:, cols] = s_im

            pl.when(jnp.logical_not(second))(lambda: scan_tile(False))
            pl.when(second)(lambda: scan_tile(True))

        @pl.when(jnp.logical_and(i == n_tiles - 1, jnp.logical_not(second)))
        def _():
            a_re, a_im = _complex_power(lre_ref[...], lim_ref[...], steps)
            _chunk_carry(st_re, st_im, h0_re, h0_im, a_re, a_im, n_seq, reverse=False)
            st_re[...] = h0_re[...]
            st_im[...] = h0_im[...]

        @pl.when(second)
        def _():
            h_re = buf_re[...].astype(BF16)
            h_im = buf_im[...].astype(BF16)
            hre_ref[...] = h_re
            him_ref[...] = h_im
            for hf in range(2):
                cols = slice(hf * 1024, (hf + 1) * 1024)
                ycols = slice(hf * 256, (hf + 1) * 256)
                y_half = (_dot_nt(h_re[:, cols], cre_ref[hf]) - _dot_nt(h_im[:, cols], cim_ref[hf])
                          + d_ref[:, ycols] * u[:, ycols])
                _split_parts(y_refs[2 * hf:2 * hf + 2], y_half)

    tile = lambda w: pl.BlockSpec((tile_rows, w), lambda p, i: (i, 0))
    out_tile = lambda w: pl.BlockSpec((tile_rows, w), lambda p, i: (i * p, 0))
    cm = _full(_CM_SHAPE)
    outs = _call(
        body, name="s5_scan_fwd", grid=(2, n_tiles),
        in_specs=[tile(LANES)] * _SCAN_PARTS + [cm, cm, cm, cm, _full((1, SSM_LANES)), _full((1, SSM_LANES)),
                                                _full((1, 512))],
        out_specs=[out_tile(LANES)] * _SCAN_PARTS + [out_tile(SSM_LANES), out_tile(SSM_LANES)],
        out_shape=_part_shapes(rows) + [_sds((rows, SSM_LANES), BF16), _sds((rows, SSM_LANES), BF16)],
        scratch_shapes=[pltpu.VMEM((slab, SSM_LANES), F32)] * 4 + [pltpu.VMEM((tile_rows, SSM_LANES), F32)] * 2,
        compiler_params=_params(40, ("arbitrary", "arbitrary")),
    )(*u_parts, bt_re, bt_im, cm_re, cm_im, lbar_re, lbar_im, d_row)
    return outs[:_SCAN_PARTS], outs[_SCAN_PARTS], outs[_SCAN_PARTS + 1]


def _s5_scan_bwd(dy_parts, u_parts, h_re, h_im, bt_re, bt_im, cm_re, cm_im, lbar_re, lbar_im, d_row, n_seq, seq):
    slab, steps, tile_rows, n_tiles = _scan_geometry(n_seq, seq)
    rows = u_parts[0].shape[0]

    def body(*refs):
        dy_refs, u_refs, refs = refs[:_SCAN_PARTS], refs[_SCAN_PARTS:2 * _SCAN_PARTS], refs[2 * _SCAN_PARTS:]
        (hre_ref, him_ref, bre_ref, bim_ref, cre_ref, cim_ref, lre_ref, lim_ref, d_ref), refs = refs[:9], refs[9:]
        du_refs, refs = refs[:_SCAN_PARTS], refs[_SCAN_PARTS:]
        (dbre_ref, dbim_ref, dcre_ref, dcim_ref, dlre_ref, dlim_ref, dd_ref,
         st_re, st_im, g0_re, g0_im, acc_re, acc_im, buf_re, buf_im) = refs
        second = pl.program_id(0) == 1
        i = pl.program_id(1)

        @pl.when(jnp.logical_and(i == 0, jnp.logical_not(second)))
        def _():
            st_re[...] = jnp.zeros_like(st_re)
            st_im[...] = jnp.zeros_like(st_im)
            acc_re[...] = jnp.zeros_like(acc_re)
            acc_im[...] = jnp.zeros_like(acc_im)
            for ref in (dbre_ref, dbim_ref, dcre_ref, dcim_ref, dd_ref):
                ref[...] = jnp.zeros_like(ref)

        dy = _join_parts(dy_refs)
        dyb = dy.astype(BF16)
        for hf in range(2):
            cols = slice(hf * 1024, (hf + 1) * 1024)
            buf_re[:, cols] = _dot(dyb[:, hf * 256:(hf + 1) * 256], cre_ref[hf])
            buf_im[:, cols] = -_dot(dyb[:, hf * 256:(hf + 1) * 256], cim_ref[hf])

        for lc in range(SSM_LANES // SCAN_LANE_CHUNK):
            cols = slice(lc * SCAN_LANE_CHUNK, (lc + 1) * SCAN_LANE_CHUNK)
            l_re = jnp.broadcast_to(lre_ref[:, cols], (slab, SCAN_LANE_CHUNK))
            l_im = jnp.broadcast_to(lim_ref[:, cols], (slab, SCAN_LANE_CHUNK))

            def advance(r0, s_re, s_im):
                n_re = l_re * s_re + l_im * s_im + buf_re[pl.ds(r0, slab), cols]
                n_im = l_re * s_im - l_im * s_re + buf_im[pl.ds(r0, slab), cols]
                buf_re[pl.ds(r0, slab), cols] = n_re
                buf_im[pl.ds(r0, slab), cols] = n_im
                return n_re, n_im

            def row0(k):
                return pl.multiple_of((SCAN_TILE_STEPS - 1 - k) * slab, slab)

            @pl.when(jnp.logical_not(second))
            def _():
                s_re, s_im = lax.fori_loop(0, SCAN_TILE_STEPS, lambda k, s: advance(row0(k), *s),
                                           (st_re[:, cols], st_im[:, cols]), unroll=True)
                st_re[:, cols] = s_re
                st_im[:, cols] = s_im

            @pl.when(second)
            def _():
                def step(k, carry):
                    s_re, s_im, a_re, a_im = carry
                    r0 = row0(k)
                    hr = hre_ref[pl.ds(r0, slab), cols].astype(F32)
                    hi = him_ref[pl.ds(r0, slab), cols].astype(F32)
                    a_re = a_re + s_re * hr + s_im * hi
                    a_im = a_im + s_im * hr - s_re * hi
                    return advance(r0, s_re, s_im) + (a_re, a_im)

                zero = jnp.zeros((slab, SCAN_LANE_CHUNK), F32)
                s_re, s_im, a_re, a_im = lax.fori_loop(
                    0, SCAN_TILE_STEPS, step, (st_re[:, cols], st_im[:, cols], zero, zero), unroll=True)
                st_re[:, cols] = s_re
                st_im[:, cols] = s_im
                acc_re[:, cols] += a_re
                acc_im[:, cols] += a_im

        @pl.when(jnp.logical_and(i == n_tiles - 1, jnp.logical_not(second)))
        def _():
            p_re, p_im = _complex_power(lre_ref[...], lim_ref[...], steps)
            _chunk_carry(st_re, st_im, g0_re, g0_im, p_re, -p_im, n_seq, reverse=True)
            st_re[...] = g0_re[...]
            st_im[...] = g0_im[...]

        @pl.when(second)
        def _():
            u = _join_parts(u_refs)
            ub = u.astype(BF16)
            g_re = buf_re[...].astype(BF16)
            g_im = buf_im[...].astype(BF16)
            dd_ref[...] += jnp.sum(dy * u, axis=0, keepdims=True)
            for hf in range(2):
                cols = slice(hf * 1024, (hf + 1) * 1024)
                ycols = slice(hf * 256, (hf + 1) * 256)
                du_half = (_dot_nt(g_re[:, cols], bre_ref[hf]) + _dot_nt(g_im[:, cols], bim_ref[hf])
                           + d_ref[:, ycols] * dy[:, ycols])
                _split_parts(du_refs[2 * hf:2 * hf + 2], du_half)
                dbre_ref[hf] += _dot_tn(ub[:, ycols], g_re[:, cols])
                dbim_ref[hf] += _dot_tn(ub[:, ycols], g_im[:, cols])
                dcre_ref[hf] += _dot_tn(dyb[:, ycols], hre_ref[:, cols])
                dcim_ref[hf] -= _dot_tn(dyb[:, ycols], him_ref[:, cols])

        @pl.when(jnp.logical_and(i == n_tiles - 1, second))
        def _():
            dlre_ref[...] = jnp.sum(acc_re[...], axis=0, keepdims=True)
            dlim_ref[...] = jnp.sum(acc_im[...], axis=0, keepdims=True)

    tile = lambda w: pl.BlockSpec((tile_rows, w), lambda p, i: (n_tiles - 1 - i, 0))
    second_tile = lambda w: pl.BlockSpec((tile_rows, w), lambda p, i: (n_tiles - 1 - i * p, 0))
    cm = _full(_CM_SHAPE)
    row = _full((1, SSM_LANES))
    outs = _call(
        body, name="s5_scan_bwd", grid=(2, n_tiles),
        in_specs=[tile(LANES)] * _SCAN_PARTS + [second_tile(LANES)] * _SCAN_PARTS
        + [second_tile(SSM_LANES), second_tile(SSM_LANES), cm, cm, cm, cm, row, row, _full((1, 512))],
        out_specs=[second_tile(LANES)] * _SCAN_PARTS + [cm, cm, cm, cm, row, row, _full((1, 512))],
        out_shape=(_part_shapes(rows) + [_sds(_CM_SHAPE)] * 4 + [_sds((1, SSM_LANES))] * 2 + [_sds((1, 512))]),
        scratch_shapes=[pltpu.VMEM((slab, SSM_LANES), F32)] * 6 + [pltpu.VMEM((tile_rows, SSM_LANES), F32)] * 2,
        compiler_params=_params(48, ("arbitrary", "arbitrary")),
    )(*dy_parts, *u_parts, h_re, h_im, bt_re, bt_im, cm_re, cm_im, lbar_re, lbar_im, d_row)
    return (outs[:_SCAN_PARTS],) + tuple(outs[_SCAN_PARTS:])


def _glu_gate(gl, a, zs):
    return gl * jax.nn.sigmoid(a) * _silu(zs)


def _glu_fwd(y_parts, zs, w_glu, b_glu, n_seq, seq):
    rows = zs.shape[0]
    tm = 512
    slab, steps, _, _ = _scan_geometry(n_seq, seq)

    def body(*refs):
        y_refs, (zs_ref, w_ref, b_ref, o_ref) = refs[:_SCAN_PARTS], refs[_SCAN_PARTS:]
        y = _load_chunks(y_refs, pl.program_id(0) * (tm // steps), tm // steps, steps, slab)
        gl = jax.nn.gelu(y)
        a = _dot(gl.astype(BF16), w_ref[...]) + b_ref[...]
        o_ref[...] = _glu_gate(gl, a, zs_ref[...]).astype(BF16)

    return _call(
        body, name="glu_fwd", grid=(rows // tm,),
        in_specs=_whole_parts(rows) + [_rows(tm, 512), _full((512, 512)), _full((1, 512))],
        out_specs=_rows(tm, 512), out_shape=_sds((rows, 512), BF16),
        compiler_params=_params(32, ("arbitrary",)),
    )(*y_parts, zs, w_glu, b_glu)


def _glu_bwd(y_parts, zs, d_out, w_glu, b_glu, n_seq, seq):
    rows = zs.shape[0]
    tm = 512
    slab, steps, _, _ = _scan_geometry(n_seq, seq)

    def body(*refs):
        y_refs, (zs_ref, d_ref, w_ref, b_ref), refs = refs[:_SCAN_PARTS], refs[_SCAN_PARTS:_SCAN_PARTS + 4], refs[_SCAN_PARTS + 4:]
        dy_refs, (dzs_ref, dw_ref, db_ref) = refs[:_SCAN_PARTS], refs[_SCAN_PARTS:]
        first_chunk = pl.program_id(0) * (tm // steps)

        @pl.when(pl.program_id(0) == 0)
        def _():
            dw_ref[...] = jnp.zeros_like(dw_ref)
            db_ref[...] = jnp.zeros_like(db_ref)

        gl, gelu_vjp = jax.vjp(jax.nn.gelu, _load_chunks(y_refs, first_chunk, tm // steps, steps, slab))
        glb = gl.astype(BF16)
        a = _dot(glb, w_ref[...]) + b_ref[...]
        _, gate_vjp = jax.vjp(_glu_gate, gl, a, zs_ref[...])
        d_gl, d_a, d_zs = gate_vjp(d_ref[...])
        dab = d_a.astype(BF16)
        d_gl = d_gl + _dot_nt(dab, w_ref[...])
        _store_chunks(dy_refs, first_chunk, gelu_vjp(d_gl)[0], steps, slab)
        dzs_ref[...] = d_zs.astype(BF16)
        dw_ref[...] += _dot_tn(glb, dab)
        db_ref[...] += jnp.sum(d_a, axis=0, keepdims=True)

    *dy_parts, dzs, dw, db = _call(
        body, name="glu_bwd", grid=(rows // tm,),
        in_specs=_whole_parts(rows) + [_rows(tm, 512), _rows(tm, 512), _full((512, 512)), _full((1, 512))],
        out_specs=_whole_parts(rows) + [_rows(tm, 512), _full((512, 512)), _full((1, 512))],
        out_shape=_part_shapes(rows) + [_sds((rows, 512), BF16), _sds((512, 512)), _sds((1, 512))],
        compiler_params=_params(40, ("arbitrary",)),
    )(*y_parts, zs, d_out, w_glu, b_glu)
    return dy_parts, dzs, dw, db


_GROUP_ROWS = Q_PER_KV * BLOCK
_BLOCK_SHIFT = BLOCK.bit_length() - 1


def _attn_bias(j):
    row = _iota((_GROUP_ROWS, BLOCK), 0)
    dist_cur = (row & (BLOCK - 1)) - _iota((_GROUP_ROWS, BLOCK), 1)
    dist_prev = dist_cur + BLOCK
    head = row >> _BLOCK_SHIFT
    slope = jnp.zeros((_GROUP_ROWS, BLOCK), F32)
    for g in range(Q_PER_KV):
        slope = jnp.where(head == g, 2.0 ** (-(j * Q_PER_KV + g + 1)), slope)
    bias_cur = jnp.where(dist_cur >= 0, -slope * dist_cur.astype(F32), -jnp.inf)
    bias_prev = jnp.where(dist_prev < WINDOW, -slope * dist_prev.astype(F32), -jnp.inf)
    return bias_cur, bias_prev


_ATTN_BIAS_SCRATCH = pltpu.VMEM((KV_HEADS, 2, _GROUP_ROWS, BLOCK), F32)


def _fill_attn_bias(bias_ref):
    @pl.when(jnp.logical_and(pl.program_id(0) == 0, pl.program_id(1) == 0))
    def _():
        for j in range(KV_HEADS):
            bias_ref[j, 0], bias_ref[j, 1] = _attn_bias(j)


def _stack_heads(x, j):
    heads = range(j * Q_PER_KV, (j + 1) * Q_PER_KV)
    return jnp.concatenate([x[:, h * HEAD_DIM:(h + 1) * HEAD_DIM] for h in heads], axis=0)


def _stack_columns(x, j):
    heads = range(j * Q_PER_KV, (j + 1) * Q_PER_KV)
    return jnp.concatenate([jnp.broadcast_to(x[:, h:h + 1], (BLOCK, 1)) for h in heads], axis=0)


def _attn_fwd(q, k, v, za, sinks, n_seq, seq):
    nb = seq // BLOCK
    rows = q.shape[0]

    def body(q_ref, kc_ref, kp_ref, vc_ref, vp_ref, za_ref, sk_ref, o_ref, ao_ref, lse_ref, bias_ref):
        _fill_attn_bias(bias_ref)
        has_prev = pl.program_id(1) > 0
        q_all = q_ref[...]
        for j in range(KV_HEADS):
            js = slice(j * HEAD_DIM, (j + 1) * HEAD_DIM)
            bias_c, bias_p = bias_ref[j, 0], bias_ref[j, 1]
            q4 = _stack_heads(q_all, j)
            sc = _dot_nt(q4, kc_ref[:, js]) * ATTN_SCALE + bias_c
            sp = _dot_nt(q4, kp_ref[:, js]) * ATTN_SCALE + jnp.where(has_prev, bias_p, -jnp.inf)
            sink = _stack_columns(sk_ref[...], j)
            m = jnp.maximum(jnp.maximum(jnp.max(sc, axis=-1, keepdims=True), jnp.max(sp, axis=-1, keepdims=True)), sink)
            ec = jnp.exp(sc - m)
            ep = jnp.exp(sp - m)
            den = jnp.sum(ec, axis=-1, keepdims=True) + jnp.sum(ep, axis=-1, keepdims=True) + jnp.exp(sink - m)
            inv = 1.0 / den
            o4 = _dot((ec * inv).astype(BF16), vc_ref[:, js]) + _dot((ep * inv).astype(BF16), vp_ref[:, js])
            lse4 = m + jnp.log(den)
            for g in range(Q_PER_KV):
                h = j * Q_PER_KV + g
                o_ref[:, h * HEAD_DIM:(h + 1) * HEAD_DIM] = o4[g * BLOCK:(g + 1) * BLOCK]
                lse_ref[:, h:h + 1] = lse4[g * BLOCK:(g + 1) * BLOCK]
        ao_ref[...] = (o_ref[...] * _silu(za_ref[...])).astype(BF16)

    cur = lambda w: pl.BlockSpec((BLOCK, w), lambda b, n: (b * nb + n, 0))
    prev = lambda w: pl.BlockSpec((BLOCK, w), lambda b, n: (b * nb + jnp.maximum(n - 1, 0), 0))
    return _call(
        body, name="attn_fwd", grid=(n_seq, nb),
        in_specs=[cur(512), cur(128), prev(128), cur(128), prev(128), cur(512), _full((1, N_HEADS))],
        out_specs=[cur(512), cur(512), cur(N_HEADS)],
        out_shape=[_sds((rows, 512)), _sds((rows, 512), BF16), _sds((rows, N_HEADS))],
        scratch_shapes=[_ATTN_BIAS_SCRATCH], compiler_params=_params(32, ("arbitrary", "arbitrary")),
    )(q, k, k, v, v, za, sinks)


def _attn_bwd(q, k, v, za, o, lse, d_ao, sinks, n_seq, seq):
    nb = seq // BLOCK
    rows = q.shape[0]

    def body(q_ref, q2_ref, kc_ref, kp_ref, vc_ref, vp_ref, za_ref, za2_ref, o_ref, lse_ref, lse2_ref,
             d_ref, d2_ref, sk_ref, dq_ref, dk_ref, dv_ref, dza_ref, dsk_ref, bias_ref, delta_ref):
        n = nb - 1 - pl.program_id(1)
        _fill_attn_bias(bias_ref)

        @pl.when(jnp.logical_and(pl.program_id(0) == 0, pl.program_id(1) == 0))
        def _():
            dsk_ref[...] = jnp.zeros_like(dsk_ref)
            delta_ref[...] = jnp.zeros_like(delta_ref)

        has_prev = n > 0
        has_next = n + 1 < nb

        _, gate_vjp = jax.vjp(lambda o_, z_: o_ * _silu(z_), o_ref[...], za_ref[...])
        d_o, d_za = gate_vjp(d_ref[...])
        dza_ref[...] = d_za.astype(BF16)
        d_o2 = d2_ref[...] * _silu(za2_ref[...])
        q_all, q2_all = q_ref[...], q2_ref[...]
        lse_all, lse2_all = lse_ref[...], lse2_ref[...]

        for j in range(KV_HEADS):
            js = slice(j * HEAD_DIM, (j + 1) * HEAD_DIM)
            kc, kp, vc, vp = kc_ref[:, js], kp_ref[:, js], vc_ref[:, js], vp_ref[:, js]
            bias_c, bias_p = bias_ref[j, 0], bias_ref[j, 1]
            q4 = _stack_heads(q_all, j)
            do4b = _stack_heads(d_o, j).astype(BF16)
            lse4 = _stack_columns(lse_all, j)
            pc = jnp.exp(_dot_nt(q4, kc) * ATTN_SCALE + bias_c - lse4)
            pp = jnp.exp(_dot_nt(q4, kp) * ATTN_SCALE + jnp.where(has_prev, bias_p, -jnp.inf) - lse4)
            dpc = _dot_nt(do4b, vc)
            dpp = _dot_nt(do4b, vp)
            delta = jnp.sum(pc * dpc, axis=-1, keepdims=True) + jnp.sum(pp * dpp, axis=-1, keepdims=True)
            delta2 = jnp.where(has_next, delta_ref[j], 0.0)
            delta_ref[j] = delta
            dsc = (pc * (dpc - delta)).astype(BF16)
            dsp = (pp * (dpp - delta)).astype(BF16)
            dq4 = ((_dot(dsc, kc) + _dot(dsp, kp)) * ATTN_SCALE).astype(BF16)
            sink_loss = jnp.exp(_stack_columns(sk_ref[...], j) - lse4) * delta
            for g in range(Q_PER_KV):
                h = j * Q_PER_KV + g
                dq_ref[:, h * HEAD_DIM:(h + 1) * HEAD_DIM] = dq4[g * BLOCK:(g + 1) * BLOCK]
                dsk_ref[0:1, h:h + 1] -= jnp.sum(sink_loss[g * BLOCK:(g + 1) * BLOCK], axis=0, keepdims=True)
            dk = _dot_tn(dsc, q4)
            dv = _dot_tn(pc.astype(BF16), do4b)
            q4n = _stack_heads(q2_all, j)
            do4nb = _stack_heads(d_o2, j).astype(BF16)
            p2 = jnp.exp(_dot_nt(q4n, kc) * ATTN_SCALE + jnp.where(has_next, bias_p, -jnp.inf)
                         - _stack_columns(lse2_all, j))
            ds2 = (p2 * (_dot_nt(do4nb, vc) - delta2)).astype(BF16)
            dk = dk + _dot_tn(ds2, q4n)
            dv = dv + _dot_tn(p2.astype(BF16), do4nb)
            dk_ref[:, js] = (dk * ATTN_SCALE).astype(BF16)
            dv_ref[:, js] = dv.astype(BF16)

    cur = lambda w: pl.BlockSpec((BLOCK, w), lambda b, s: (b * nb + nb - 1 - s, 0))
    prev = lambda w: pl.BlockSpec((BLOCK, w), lambda b, s: (b * nb + jnp.maximum(nb - 2 - s, 0), 0))
    nxt = lambda w: pl.BlockSpec((BLOCK, w), lambda b, s: (b * nb + jnp.minimum(nb - s, nb - 1), 0))
    return _call(
        body, name="attn_bwd", grid=(n_seq, nb),
        in_specs=[cur(512), nxt(512), cur(128), prev(128), cur(128), prev(128), cur(512), nxt(512),
                  cur(512), cur(N_HEADS), nxt(N_HEADS), cur(512), nxt(512), _full((1, N_HEADS))],
        out_specs=[cur(512), cur(128), cur(128), cur(512), _full((1, N_HEADS))],
        out_shape=[_sds((rows, 512), BF16), _sds((rows, 128), BF16), _sds((rows, 128), BF16),
                   _sds((rows, 512), BF16), _sds((1, N_HEADS))],
        scratch_shapes=[_ATTN_BIAS_SCRATCH, pltpu.VMEM((KV_HEADS, _GROUP_ROWS, 1), F32)],
        compiler_params=_params(32, ("arbitrary", "arbitrary")),
    )(q, q, k, k, v, v, za, za, o, lse, lse, d_ao, d_ao, sinks)


def _tail(ssm_out, attn_out, x2d, p2d, target, w_out, g2, w_gate, b_gate, w_proj):
    rows = x2d.shape[0]
    tm = 512

    def body(so_ref, ao_ref, x_ref, p_ref, t_ref, wo_ref, g2_ref, wg_ref, bg_ref, wp_ref,
             dh1_ref, dso_ref, dao_ref, dwo_ref, dwg_ref, dwp_ref, dbg_ref, dg2_ref, loss_ref):
        @pl.when(pl.program_id(0) == 0)
        def _():
            for ref in (dwo_ref, dwg_ref, dwp_ref, dbg_ref, dg2_ref, loss_ref):
                ref[...] = jnp.zeros_like(ref)

        so = so_ref[...]
        ao = ao_ref[...]
        g2 = g2_ref[...]
        mixed = _dot(so, wo_ref[0:512, :]) + _dot(ao, wo_ref[512:1024, :])
        r = lax.rsqrt(jnp.mean(mixed * mixed, axis=-1, keepdims=True) + EPS)
        mr = mixed * r
        h1 = x_ref[...] + mr * g2
        h1b = h1.astype(BF16)
        gate = jax.nn.sigmoid(_dot(h1b, wg_ref[...]) + bg_ref[...])
        pb = p_ref[...].astype(BF16)
        wp_blocks = [slice(j * D_PLE, (j + 1) * D_PLE) for j in range(N_CHIPS)]
        pp = jnp.concatenate([_dot(pb, wp_ref[blk, :]) for blk in wp_blocks], axis=1)
        err = h1 + gate * pp - t_ref[...]
        loss_ref[...] += 0.5 * jnp.sum(jnp.mean(err * err, axis=-1, keepdims=True), axis=0, keepdims=True)

        dh2 = err * (1.0 / D_MODEL)
        d_glin = dh2 * pp * gate * (1.0 - gate)
        d_glin_b = d_glin.astype(BF16)
        dwg_ref[...] += _dot_tn(h1b, d_glin_b)
        dbg_ref[...] += jnp.sum(d_glin, axis=0, keepdims=True)
        d_pp = (dh2 * gate).astype(BF16)
        for blk in wp_blocks:
            dwp_ref[blk, :] += _dot_tn(pb, d_pp[:, blk])
        dh1 = dh2 + _dot_nt(d_glin_b, wg_ref[...])
        dh1_ref[...] = dh1
        dg2_ref[...] += jnp.sum(dh1 * mr, axis=0, keepdims=True)
        a_ = dh1 * g2
        d_mixed = (r * a_ - mr * (r * jnp.mean(a_ * mr, axis=-1, keepdims=True))).astype(BF16)
        dwo_ref[0:512, :] += _dot_tn(so, d_mixed)
        dwo_ref[512:1024, :] += _dot_tn(ao, d_mixed)
        dso_ref[...] = _dot_nt(d_mixed, wo_ref[0:512, :])
        dao_ref[...] = _dot_nt(d_mixed, wo_ref[512:1024, :])

    return _call(
        body, name="tail_fwd_bwd", grid=(rows // tm,),
        in_specs=[_rows(tm, 512), _rows(tm, 512), _rows(tm, D_MODEL), _rows(tm, D_PLE), _rows(tm, D_MODEL),
                  _full((D_MODEL, D_MODEL)), _full((1, D_MODEL)), _full((D_MODEL, D_MODEL)), _full((1, D_MODEL)),
                  _full((N_CHIPS * D_PLE, D_PLE))],
        out_specs=[_rows(tm, D_MODEL), _rows(tm, 512), _rows(tm, 512), _full((D_MODEL, D_MODEL)),
                   _full((D_MODEL, D_MODEL)), _full((N_CHIPS * D_PLE, D_PLE)), _full((1, D_MODEL)), _full((1, D_MODEL)),
                   _full((1, 1))],
        out_shape=[_sds((rows, D_MODEL)), _sds((rows, 512)), _sds((rows, 512)), _sds((D_MODEL, D_MODEL)),
                   _sds((D_MODEL, D_MODEL)), _sds((N_CHIPS * D_PLE, D_PLE)), _sds((1, D_MODEL)), _sds((1, D_MODEL)),
                   _sds((1, 1))],
        compiler_params=_params(52, ("arbitrary",)),
    )(ssm_out, attn_out, x2d, p2d, target, w_out, g2, w_gate, b_gate, w_proj)


def _local_step(x, p, target, pre_norm_g, w_in_t, s5_params, ssm_d, w_glu, b_glu, sinks, w_out, post_norm_g, w_proj,
                w_gate, b_gate):
    n_seq, seq, _ = x.shape
    rows = n_seq * seq
    x2d = x.reshape(rows, D_MODEL)
    p2d = p.reshape(rows, D_PLE)
    t2d = target.reshape(rows, D_MODEL)

    l_re, l_im, bt_re, bt_im, cm_re, cm_im = _s5_params_fwd(*s5_params)

    u_scan, zs, q, k, v, za = _in_proj(x2d, pre_norm_g, w_in_t, n_seq, seq)
    y_scan, h_re, h_im = _s5_scan_fwd(u_scan, bt_re, bt_im, cm_re, cm_im, l_re, l_im, ssm_d, n_seq, seq)
    ssm_out = _glu_fwd(y_scan, zs, w_glu, b_glu, n_seq, seq)
    o, attn_out, lse = _attn_fwd(q, k, v, za, sinks, n_seq, seq)

    dh1, d_so, d_ao, d_w_out, d_w_gate, d_w_proj, d_b_gate, d_g2, loss = _tail(
        ssm_out, attn_out, x2d, p2d, t2d, w_out, post_norm_g, w_gate, b_gate, w_proj)

    dq, dk, dv, dza, d_sinks = _attn_bwd(q, k, v, za, o, lse, d_ao, sinks, n_seq, seq)
    dy_scan, dzs, d_w_glu, d_b_glu = _glu_bwd(y_scan, zs, d_so, w_glu, b_glu, n_seq, seq)
    du_scan, d_bt_re, d_bt_im, d_cm_re, d_cm_im, d_l_re, d_l_im, d_d = _s5_scan_bwd(
        dy_scan, u_scan, h_re, h_im, bt_re, bt_im, cm_re, cm_im, l_re, l_im, ssm_d, n_seq, seq)
    d_lam_re, d_lam_im, d_log_step, d_b_re, d_b_im, d_c_re, d_c_im = _s5_params_bwd(
        s5_params, (d_l_re, d_l_im, d_bt_re, d_bt_im, d_cm_re, d_cm_im))

    grad_x, d_w_in_t, d_g1 = _in_proj_bwd(x2d, dh1, pre_norm_g, w_in_t, du_scan, dzs, dq, dk, dv, dza, n_seq, seq)
    grads = dict(
        pre_norm_g=d_g1, w_in=d_w_in_t, ssm_lam_re=d_lam_re, ssm_lam_im=d_lam_im, ssm_log_step=d_log_step,
        ssm_b_re=d_b_re, ssm_b_im=d_b_im, ssm_c_re=d_c_re, ssm_c_im=d_c_im, ssm_d=d_d, ssm_w_glu=d_w_glu,
        ssm_b_glu=d_b_glu, attn_sinks=d_sinks, w_out=d_w_out, post_norm_g=d_g2, pl_w_proj=d_w_proj,
        pl_w_gate=d_w_gate, pl_b_gate=d_b_gate)
    return grad_x.reshape(x.shape), loss, grads


_BIG = ("w_in", "ssm_w_glu", "w_out", "pl_w_proj", "pl_w_gate")
_BIG_SHARD = {"w_in": (D_IN // N_CHIPS, D_MODEL), "ssm_w_glu": (D_SSM // N_CHIPS, D_SSM),
              "w_out": (D_MODEL // N_CHIPS, D_MODEL), "pl_w_proj": (D_PLE, D_MODEL // N_CHIPS),
              "pl_w_gate": (D_MODEL // N_CHIPS, D_MODEL)}
_SMALL = {"pre_norm_g": (1, D_MODEL), "ssm_lam_re": (SSM_GROUPS, SSM_STATE), "ssm_lam_im": (SSM_GROUPS, SSM_STATE),
          "ssm_log_step": (1, SSM_GROUPS), "ssm_b_re": (D_SSM, SSM_STATE), "ssm_b_im": (D_SSM, SSM_STATE),
          "ssm_c_re": (D_SSM, SSM_STATE), "ssm_c_im": (D_SSM, SSM_STATE), "ssm_d": (1, D_SSM), "ssm_b_glu": (1, D_SSM),
          "attn_sinks": (1, N_HEADS), "post_norm_g": (1, D_MODEL), "pl_b_gate": (1, D_MODEL)}
_VEC_ROWS = ("pre_norm_g", "post_norm_g", "pl_b_gate", "ssm_d", "ssm_b_glu", "attn_sinks", "ssm_log_step", "loss")
_SMALL_GROUPS = (
    ("vec", (8, D_MODEL), tuple((name, r) for r, name in enumerate(_VEC_ROWS))),
    ("lam", (2 * SSM_GROUPS, SSM_STATE), (("ssm_lam_re", 0), ("ssm_lam_im", SSM_GROUPS))),
    ("bc", (4 * D_SSM, SSM_STATE), (("ssm_b_re", 0), ("ssm_b_im", D_SSM), ("ssm_c_re", 2 * D_SSM),
                                    ("ssm_c_im", 3 * D_SSM))),
)
_SMALL_ORDER = tuple(name for _, _, members in _SMALL_GROUPS for name, _ in members)
_WEIGHT_ORDER = ("pre_norm_g", "w_in", "ssm_lam_re", "ssm_lam_im", "ssm_log_step", "ssm_b_re", "ssm_b_im", "ssm_c_re",
                 "ssm_c_im", "ssm_d", "ssm_w_glu", "ssm_b_glu", "attn_sinks", "w_out", "post_norm_g", "pl_w_proj",
                 "pl_w_gate", "pl_b_gate")


def _small_shape(name):
    return (1, 1) if name == "loss" else _SMALL[name]


def _to_kernel_form(name, a):
    a = a[0]
    if name == "w_in":
        return a.T
    if name in ("ssm_b_re", "ssm_b_im"):
        a = a.transpose(0, 2, 1)
    return a.reshape(_SMALL[name]) if name in _SMALL else a


def _from_kernel_form(name, a, shape):
    if name == "w_in":
        a = a.T
    if name in ("ssm_b_re", "ssm_b_im"):
        a = a.reshape(SSM_GROUPS, SSM_GROUP_CH, SSM_STATE).transpose(0, 2, 1)
    return a.reshape(shape)


def _mesh_place():
    x, y, c = lax.axis_index("x"), lax.axis_index("y"), lax.axis_index("c")
    other_chips = ((1 - x, y), (x, 1 - y), (1 - x, 1 - y))
    return x, y, c, other_chips


def _gather_copies(s_refs, g_refs, send_sems, recv_sems, local_sems):
    x, y, c, other_chips = _mesh_place()
    started = []
    for i, (s_ref, g_ref) in enumerate(zip(s_refs, g_refs)):
        rows = s_ref.shape[0]
        half = rows // 2

        def block(chip, g_ref=g_ref, rows=rows, half=half):
            return g_ref.at[pl.ds((2 * chip[0] + chip[1]) * rows + c * half, half), :]

        def copy(k, chip, to, src=None, i=i, block=block):
            return pltpu.make_async_remote_copy(
                src_ref=block(chip) if src is None else src, dst_ref=block(chip), send_sem=send_sems.at[6 * i + k],
                recv_sem=recv_sems.at[6 * i + k], device_id=to, device_id_type=MESH)

        own = pltpu.make_async_copy(s_ref, g_ref.at[pl.ds((2 * x + y) * rows, rows), :], local_sems.at[i])
        own.start()
        first = [copy(k, (x, y), (*chip, c), src=s_ref.at[pl.ds(c * half, half), :])
                 for k, chip in enumerate(other_chips)]
        for cp in first:
            cp.start()
        passed = [copy(3 + k, chip, (x, y, 1 - c)) for k, chip in enumerate(other_chips)]
        started.append((own, first, passed))
    for own, first, passed in started:
        for k in range(3):
            first[k].wait_recv()
            passed[k].start()
    for own, first, passed in started:
        for k in range(3):
            passed[k].wait_recv()
        for cp in first + passed:
            cp.wait_send()
        own.wait()


def _gather_semaphores(n_t):
    return [pltpu.SemaphoreType.DMA((6 * n_t,)), pltpu.SemaphoreType.DMA((6 * n_t,)), pltpu.SemaphoreType.DMA((n_t,))]


def _gather_weights(shards):
    n_t = len(shards)

    def body(*refs):
        _gather_copies(refs[:n_t], refs[n_t:2 * n_t], *refs[2 * n_t + 1:])
        refs[2 * n_t][...] = jnp.zeros_like(refs[2 * n_t])

    any_spec = pl.BlockSpec(memory_space=pl.ANY)
    *full, done = _call(
        body, name="gather_weights", in_specs=[any_spec] * n_t,
        out_specs=[any_spec] * n_t + [pl.BlockSpec(memory_space=pltpu.VMEM)],
        out_shape=[_sds((N_CHIPS * s.shape[0], s.shape[1]), s.dtype) for s in shards]
        + [jax.ShapeDtypeStruct((8, LANES), F32)],
        scratch_shapes=_gather_semaphores(n_t),
    )(*shards)
    return full, done[0, 0]


def _gather_weights_beside(shards):
    n_t = len(shards)
    hbm = pltpu.MemorySpace.HBM
    s_refs = [jax.new_ref(s, memory_space=hbm) for s in shards]
    g_refs = [jax.empty_ref(jax.ShapeDtypeStruct((N_CHIPS * s.shape[0], s.shape[1]), s.dtype), memory_space=hbm)
              for s in shards]

    def launch(send_sems, recv_sems, local_sems):
        x, y, c, other_chips = _mesh_place()
        peers = [(*chip, c) for chip in other_chips] + [(x, y, 1 - c)]
        barrier = pltpu.get_barrier_semaphore()
        for peer in peers:
            pl.semaphore_signal(barrier, inc=1, device_id=peer, device_id_type=MESH)
        pl.semaphore_wait(barrier, len(peers))
        _gather_copies(s_refs, g_refs, send_sems, recv_sems, local_sems)

    pl.kernel(launch, mesh=plsc.ScalarSubcoreMesh(axis_name="sequencer", num_cores=1), name="gather_weights_beside",
              scratch_types=_gather_semaphores(n_t), compiler_params=pltpu.CompilerParams(collective_id=1))()
    return [g[...] for g in g_refs]


_RELATIONS = tuple(((r >> 2) & 1, (r >> 1) & 1, r & 1) for r in range(1, 8))


def _related(place, relation):
    return tuple(1 - a if flip else a for a, flip in zip(place, relation))


def _scatter_beside(mats):
    hbm = pltpu.MemorySpace.HBM
    src_refs = [jax.new_ref(a, memory_space=hbm) for a in mats]
    land_refs = [jax.empty_ref(jax.ShapeDtypeStruct((7, a.shape[0] // 8, a.shape[1]), a.dtype), memory_space=hbm)
                 for a in mats]

    def launch(send_sems, recv_sems):
        me = (lax.axis_index("x"), lax.axis_index("y"), lax.axis_index("c"))
        peers = [_related(me, rel) for rel in _RELATIONS]
        barrier = pltpu.get_barrier_semaphore()
        for peer in peers:
            pl.semaphore_signal(barrier, inc=1, device_id=peer, device_id_type=MESH)
        pl.semaphore_wait(barrier, len(peers))
        copies = []
        for i, (src, land) in enumerate(zip(src_refs, land_refs)):
            hr = land.shape[1]
            for k, (tx, ty, tc) in enumerate(peers):
                rows = pl.ds((2 * tx + ty) * 2 * hr + tc * hr, hr)
                copies.append(pltpu.make_async_remote_copy(
                    src_ref=src.at[rows, :], dst_ref=land.at[k], send_sem=send_sems.at[7 * i + k],
                    recv_sem=recv_sems.at[7 * i + k], device_id=(tx, ty, tc), device_id_type=MESH))
                copies[-1].start()
        for cp in copies:
            cp.wait()

    n_sems = 7 * len(mats)
    pl.kernel(launch, mesh=plsc.ScalarSubcoreMesh(axis_name="sequencer", num_cores=1), name="scatter_beside",
              scratch_types=[pltpu.SemaphoreType.DMA((n_sems,)), pltpu.SemaphoreType.DMA((n_sems,))],
              compiler_params=pltpu.CompilerParams(collective_id=2))()
    return [ref[...] for ref in land_refs]


def _exchange_grads(big, small, landed):
    n_t = len(big)
    n_g = len(_SMALL_GROUPS)
    names = _SMALL_ORDER
    halves = [(b.shape[0] // N_CHIPS // 2, b.shape[1]) for b in big]
    early = sorted(landed)
    late = [i for i in range(n_t) if i not in landed]
    n_sems = 4 * n_g + 7 * len(late) + n_t
    small_sem0, block_sem0 = n_t, n_t + len(names)
    early_sem0 = block_sem0 + N_CHIPS * len(late)

    def body(*refs):
        pos = 0

        def take(n):
            nonlocal pos
            pos += n
            return refs[pos - n:pos]

        big_refs, small_refs = take(n_t), dict(zip(names, take(len(names))))
        land_refs = dict(zip(early, take(len(early))))
        out_refs, small_out_refs = take(n_t), dict(zip(names, take(len(names))))
        per_late = lambda: dict(zip(late, take(len(late))))
        ga, gb, pme, send_b, recv_b = per_late(), per_late(), take(n_t), per_late(), per_late()
        own_e, land_e = dict(zip(early, take(len(early)))), dict(zip(early, take(len(early))))
        s_own, s_sib, s_chips, s_pair = take(n_g), take(n_g), take(n_g), take(n_g)
        stage = dict(zip(names, take(len(names))))
        send_sems, recv_sems, local_sems = take(3)
        x, y, c, other_chips = _mesh_place()
        me = 2 * x + y
        sibling = (x, y, 1 - c)
        sem_at = iter(range(n_sems))

        def remote(src, dst, to):
            k = next(sem_at)
            return pltpu.make_async_remote_copy(src_ref=src, dst_ref=dst, send_sem=send_sems.at[k],
                                                recv_sem=recv_sems.at[k], device_id=to, device_id_type=MESH)

        loads = [pltpu.make_async_copy(small_refs[name], stage[name], local_sems.at[small_sem0 + a])
                 for a, name in enumerate(names)]
        for cp in loads:
            cp.start()
        for cp in loads:
            cp.wait()
        small_swaps = []
        for gi, (_, _, members) in enumerate(_SMALL_GROUPS):
            s_own[gi][...] = jnp.zeros_like(s_own[gi])
            for name, r0 in members:
                r, n = _small_shape(name)
                s_own[gi][r0:r0 + r, 0:n] = stage[name][...]
            small_swaps.append(remote(s_own[gi], s_sib[gi], sibling))
            small_swaps[gi].start()
        order = sorted(late, key=lambda i: halves[i][0] * halves[i][1])
        own_loads, big_swaps = {}, {}
        for i in order:
            hr = halves[i][0]
            own_loads[i], big_swaps[i] = [], []
            for j in range(N_CHIPS):
                mine = big_refs[i].at[pl.ds(j * 2 * hr + c * hr, hr), :]
                theirs = big_refs[i].at[pl.ds(j * 2 * hr + (1 - c) * hr, hr), :]
                sem = local_sems.at[block_sem0 + N_CHIPS * late.index(i) + j]
                own_loads[i].append(pltpu.make_async_copy(mine, ga[i].at[j], sem))
                own_loads[i][j].start()
                big_swaps[i].append(remote(theirs, gb[i].at[j], sibling))
                big_swaps[i][j].start()
        early_loads = {}
        for e, i in enumerate(early):
            hr = halves[i][0]
            mine = big_refs[i].at[pl.ds(me * 2 * hr + c * hr, hr), :]
            early_loads[i] = [pltpu.make_async_copy(mine, own_e[i], local_sems.at[early_sem0 + 2 * e]),
                              pltpu.make_async_copy(land_refs[i], land_e[i], local_sems.at[early_sem0 + 2 * e + 1])]
            for cp in early_loads[i]:
                cp.start()
        small_sends = []
        for gi in range(n_g):
            small_swaps[gi].wait_recv()
            s_pair[gi][...] = s_own[gi][...] + s_sib[gi][...]
            small_sends.append([remote(s_pair[gi], s_chips[gi].at[k], (*chip, c)) for k, chip in enumerate(other_chips)])
            for cp in small_sends[gi]:
                cp.start()

        def pair_sum(i, j):
            return ga[i][j] + gb[i][j]

        big_sends = {}
        for i in order:
            for j in range(N_CHIPS):
                own_loads[i][j].wait()
                big_swaps[i][j].wait_recv()
            big_sends[i] = []
            for k, chip in enumerate(other_chips):
                send_b[i][k] = pair_sum(i, 2 * chip[0] + chip[1]).astype(BF16)
                big_sends[i].append(remote(send_b[i].at[k], recv_b[i].at[k], (*chip, c)))
                big_sends[i][k].start()
        last_swaps, keeps = {}, {}
        for i in early + order:
            hr = halves[i][0]
            if i in landed:
                for cp in early_loads[i]:
                    cp.wait()
                total = own_e[i][...]
                for k in range(len(_RELATIONS)):
                    total = total + land_e[i][k]
                pme[i][...] = total
            else:
                for k in range(3):
                    big_sends[i][k].wait_recv()
                pme[i][...] = ((pair_sum(i, me) + recv_b[i][0].astype(F32)) + recv_b[i][1].astype(F32)) + recv_b[i][2].astype(F32)
            mine = out_refs[i].at[pl.ds(c * hr, hr), :]
            keeps[i] = pltpu.make_async_copy(pme[i], mine, local_sems.at[i])
            keeps[i].start()
            last_swaps[i] = remote(pme[i], mine, sibling)
            last_swaps[i].start()

        for gi, (_, _, members) in enumerate(_SMALL_GROUPS):
            for k in range(3):
                small_sends[gi][k].wait_recv()
            total = None
            for j in range(N_CHIPS):
                rel = jnp.bitwise_xor(j, me)
                term = jnp.where(rel == 0, s_pair[gi][...], jnp.where(
                    rel == 2, s_chips[gi][0], jnp.where(rel == 1, s_chips[gi][1], s_chips[gi][2])))
                total = term if total is None else total + term
            s_sib[gi][...] = total
            for name, r0 in members:
                r, n = _small_shape(name)
                stage[name][...] = s_sib[gi][r0:r0 + r, 0:n]
        stores = [pltpu.make_async_copy(stage[name], small_out_refs[name], local_sems.at[small_sem0 + a])
                  for a, name in enumerate(names)]
        for cp in stores:
            cp.start()

        for i in range(n_t):
            last_swaps[i].wait_recv()
            keeps[i].wait()
        for cp in stores:
            cp.wait()
        groups = list(big_swaps.values()) + small_sends + list(big_sends.values())
        for cp in small_swaps + [cp for group in groups for cp in group] + list(last_swaps.values()):
            cp.wait_send()

    any_spec = pl.BlockSpec(memory_space=pl.ANY)
    small_shapes = [_sds(_small_shape(n)) for n in names]
    group_shapes = [shape for _, shape, _ in _SMALL_GROUPS]
    vmem = lambda which, dtype, lead=(): [pltpu.VMEM(lead + halves[i], dtype) for i in which]
    outs = _call(
        body, name="exchange_grads",
        in_specs=[any_spec] * (n_t + len(names) + len(early)),
        out_specs=[any_spec] * (n_t + len(names)),
        out_shape=[_sds((b.shape[0] // N_CHIPS, b.shape[1])) for b in big] + small_shapes,
        scratch_shapes=(vmem(late, F32, (N_CHIPS,)) + vmem(late, F32, (N_CHIPS,)) + vmem(range(n_t), F32)
                        + vmem(late, BF16, (3,)) + vmem(late, BF16, (3,))
                        + vmem(early, F32) + vmem(early, F32, (len(_RELATIONS),))
                        + [pltpu.VMEM(s, F32) for s in group_shapes] * 2 + [pltpu.VMEM((3,) + s, F32) for s in group_shapes]
                        + [pltpu.VMEM(s, F32) for s in group_shapes]
                        + [pltpu.VMEM(_small_shape(n), F32) for n in names]
                        + [pltpu.SemaphoreType.DMA((n_sems,)), pltpu.SemaphoreType.DMA((n_sems,)),
                           pltpu.SemaphoreType.DMA((early_sem0 + 2 * len(early),))]),
        compiler_params=_params(48),
    )(*big, *[small[n] for n in names], *[landed[i] for i in early])
    return list(outs[:n_t]), dict(zip(names, outs[n_t:n_t + len(names)]))


def _adamw_update(w, g, m, v):
    m = ADAM_B1 * m + (1.0 - ADAM_B1) * g
    v = ADAM_B2 * v + (1.0 - ADAM_B2) * (g * g)
    m_hat = m / (1.0 - ADAM_B1 ** ADAM_STEP)
    v_hat = v / (1.0 - ADAM_B2 ** ADAM_STEP)
    return -ADAM_LR * (m_hat / (jnp.sqrt(v_hat) + ADAM_EPS) + ADAM_WD * w), m, v


def _adamw(w, g, m, v, grid, name):
    n_t = len(w)

    def body(*refs):
        ins, outs = refs[:4 * n_t], refs[4 * n_t:]
        for i in range(n_t):
            w_, g_, m_, v_ = [ins[a * n_t + i][...] for a in range(4)]
            vals = (g_,) + _adamw_update(w_, g_, m_, v_)
            for a in range(4):
                outs[a * n_t + i][...] = vals[a]

    specs = [pl.BlockSpec((a.shape[0] // grid, a.shape[1]), lambda i: (i, 0)) for a in w]
    shapes = [_sds(a.shape) for a in w]
    outs = _call(
        body, name=name, grid=(grid,), in_specs=specs * 4, out_specs=specs * 4, out_shape=shapes * 4,
        compiler_params=_params(40, ("arbitrary",)),
    )(*w, *g, *m, *v)
    return [outs[a * n_t:(a + 1) * n_t] for a in range(4)]


def kernel(x, p, pre_norm_g, w_in, ssm_lam_re, ssm_lam_im, ssm_log_step, ssm_b_re, ssm_b_im, ssm_c_re, ssm_c_im, ssm_d, ssm_w_glu, ssm_b_glu, attn_sinks, w_out, post_norm_g, pl_w_proj, pl_w_gate, pl_b_gate, loss_target, m_pre_norm_g, m_w_in, m_ssm_lam_re, m_ssm_lam_im, m_ssm_log_step, m_ssm_b_re, m_ssm_b_im, m_ssm_c_re, m_ssm_c_im, m_ssm_d, m_ssm_w_glu, m_ssm_b_glu, m_attn_sinks, m_w_out, m_post_norm_g, m_pl_w_proj, m_pl_w_gate, m_pl_b_gate, v_pre_norm_g, v_w_in, v_ssm_lam_re, v_ssm_lam_im, v_ssm_log_step, v_ssm_b_re, v_ssm_b_im, v_ssm_c_re, v_ssm_c_im, v_ssm_d, v_ssm_w_glu, v_ssm_b_glu, v_attn_sinks, v_w_out, v_post_norm_g, v_pl_w_proj, v_pl_w_gate, v_pl_b_gate):
    weights = dict(pre_norm_g=pre_norm_g, w_in=w_in, ssm_lam_re=ssm_lam_re, ssm_lam_im=ssm_lam_im,
                   ssm_log_step=ssm_log_step, ssm_b_re=ssm_b_re, ssm_b_im=ssm_b_im, ssm_c_re=ssm_c_re,
                   ssm_c_im=ssm_c_im, ssm_d=ssm_d, ssm_w_glu=ssm_w_glu, ssm_b_glu=ssm_b_glu, attn_sinks=attn_sinks,
                   w_out=w_out, post_norm_g=post_norm_g, pl_w_proj=pl_w_proj, pl_w_gate=pl_w_gate, pl_b_gate=pl_b_gate)
    m_in = dict(pre_norm_g=m_pre_norm_g, w_in=m_w_in, ssm_lam_re=m_ssm_lam_re, ssm_lam_im=m_ssm_lam_im,
                ssm_log_step=m_ssm_log_step, ssm_b_re=m_ssm_b_re, ssm_b_im=m_ssm_b_im, ssm_c_re=m_ssm_c_re,
                ssm_c_im=m_ssm_c_im, ssm_d=m_ssm_d, ssm_w_glu=m_ssm_w_glu, ssm_b_glu=m_ssm_b_glu,
                attn_sinks=m_attn_sinks, w_out=m_w_out, post_norm_g=m_post_norm_g, pl_w_proj=m_pl_w_proj,
                pl_w_gate=m_pl_w_gate, pl_b_gate=m_pl_b_gate)
    v_in = dict(pre_norm_g=v_pre_norm_g, w_in=v_w_in, ssm_lam_re=v_ssm_lam_re, ssm_lam_im=v_ssm_lam_im,
                ssm_log_step=v_ssm_log_step, ssm_b_re=v_ssm_b_re, ssm_b_im=v_ssm_b_im, ssm_c_re=v_ssm_c_re,
                ssm_c_im=v_ssm_c_im, ssm_d=v_ssm_d, ssm_w_glu=v_ssm_w_glu, ssm_b_glu=v_ssm_b_glu,
                attn_sinks=v_attn_sinks, w_out=v_w_out, post_norm_g=v_post_norm_g, pl_w_proj=v_pl_w_proj,
                pl_w_gate=v_pl_w_gate, pl_b_gate=v_pl_b_gate)

    def two_d(tree):
        return {k: _to_kernel_form(k, a) for k, a in tree.items()}

    w2, m2, v2 = two_d(weights), two_d(m_in), two_d(v_in)

    (w_in_full,), gathered = _gather_weights([w2["w_in"].astype(BF16)])
    rest = _gather_weights_beside([(w2[n] + gathered).astype(BF16) for n in _BIG[1:]])
    full = dict(zip(_BIG, [w_in_full] + rest))
    s5_params = tuple(w2[n] for n in ("ssm_lam_re", "ssm_lam_im", "ssm_log_step", "ssm_b_re", "ssm_b_im", "ssm_c_re",
                                      "ssm_c_im"))
    grad_x, loss, grads = _local_step(
        x, p, loss_target, w2["pre_norm_g"], full["w_in"], s5_params, w2["ssm_d"], full["ssm_w_glu"], w2["ssm_b_glu"],
        w2["attn_sinks"], full["w_out"], w2["post_norm_g"], full["pl_w_proj"], full["pl_w_gate"], w2["pl_b_gate"])

    sent_early = ("w_out", "pl_w_gate", "pl_w_proj")
    landed = dict(zip([_BIG.index(n) for n in sent_early], _scatter_beside([grads[n] for n in sent_early])))
    g_big, g_small = _exchange_grads([grads[n] for n in _BIG], {**{n: grads[n] for n in _SMALL}, "loss": loss}, landed)
    g_big = dict(zip(_BIG, g_big))
    total_loss = g_small.pop("loss")

    big_out = _adamw([w2[n] for n in _BIG], [g_big[n] for n in _BIG], [m2[n] for n in _BIG], [v2[n] for n in _BIG],
                     8, "adamw_matrices")
    small_names = tuple(_SMALL)
    small_out = _adamw([w2[n] for n in small_names], [g_small[n] for n in small_names], [m2[n] for n in small_names],
                       [v2[n] for n in small_names], 1, "adamw_small")

    results = [{**dict(zip(_BIG, big_part)), **dict(zip(small_names, small_part))}
               for big_part, small_part in zip(big_out, small_out)]
    flat = [_from_kernel_form(name, r[name], weights[name].shape) for r in results for name in _WEIGHT_ORDER]
    return (total_loss.reshape(()), grad_x, *flat)
```

```python
import math

import jax
import jax.numpy as jnp
from jax import lax
from jax.experimental import pallas as pl
from jax.experimental.pallas import tpu as pltpu
from jax.experimental.pallas import tpu_sc as plsc

F32 = jnp.float32
BF16 = jnp.bfloat16

D_MODEL = 1024
D_SSM = 512
D_ATTN = 512
SSM_GROUPS = 32
SSM_GROUP_CH = 16
SSM_STATE = 64
SSM_LANES = SSM_GROUPS * SSM_STATE
HEAD_DIM = 64
N_HEADS = 8
KV_HEADS = 2
Q_PER_KV = 4
WINDOW = 128
BLOCK = 128
D_PLE = 256
D_IN = 2304
EPS = 1e-6
ATTN_SCALE = 1.0 / math.sqrt(HEAD_DIM)

ADAM_LR = 0.001
ADAM_B1 = 0.9
ADAM_B2 = 0.999
ADAM_EPS = 1e-08
ADAM_WD = 0.01
ADAM_STEP = 10

N_CHIPS = 4
LANES = 128
SCAN_CHUNKS = 8
SCAN_TILE_STEPS = 32
SCAN_LANE_CHUNK = 512
MIB = 2 ** 20
MESH = pl.DeviceIdType.MESH


def _dot(a, b):
    return jnp.dot(a, b, preferred_element_type=F32)


def _dot_nt(a, b):
    return lax.dot_general(a, b, (((1,), (1,)), ((), ())), preferred_element_type=F32)


def _dot_tn(a, b):
    return lax.dot_general(a, b, (((0,), (0,)), ((), ())), preferred_element_type=F32)


def _params(vmem_mib, semantics=None):
    kw = dict(vmem_limit_bytes=vmem_mib * MIB)
    if semantics is not None:
        kw["dimension_semantics"] = semantics
    return pltpu.CompilerParams(**kw)


def _full(shape):
    nd = len(shape)
    return pl.BlockSpec(shape, lambda *_: (0,) * nd, pipeline_mode=pl.Buffered(1))


def _rows(tm, width):
    return pl.BlockSpec((tm, width), lambda i: (i, 0))


def _sds(shape, dtype=F32):
    return pltpu.HBM(shape, dtype)


def _call(body, **kw):
    fn = pl.pallas_call(body, **kw)
    return lambda *args: fn(*[pltpu.with_memory_space_constraint(a, pltpu.HBM) for a in args])


def _silu(z):
    return z * jax.nn.sigmoid(z)


def _in_proj(x2d, g1, w_in_t, n_seq, seq):
    rows = x2d.shape[0]
    tm = 512
    slab, steps, _, _ = _scan_geometry(n_seq, seq)

    def body(x_ref, g_ref, w_ref, *out_refs):
        u_parts, (zs_ref, q_ref, k_ref, v_ref, za_ref) = out_refs[:_SCAN_PARTS], out_refs[_SCAN_PARTS:]
        x = x_ref[...]
        r = lax.rsqrt(jnp.mean(x * x, axis=-1, keepdims=True) + EPS)
        hn = (x * r * g_ref[...]).astype(BF16)

        def proj(a, b):
            return _dot_nt(hn, w_ref[a:b, :])

        _store_chunks(u_parts, pl.program_id(0) * (tm // steps), proj(0, 512), steps, slab)
        zs_ref[...] = proj(512, 1024)
        q_ref[...] = proj(1024, 1536).astype(BF16)
        k_ref[...] = proj(1536, 1664).astype(BF16)
        v_ref[...] = proj(1664, 1792).astype(BF16)
        za_ref[...] = proj(1792, 2304)

    *u_parts, zs, q, k, v, za = _call(
        body, name="in_proj", grid=(rows // tm,),
        in_specs=[_rows(tm, D_MODEL), _full((1, D_MODEL)), _full((D_IN, D_MODEL))],
        out_specs=_whole_parts(rows) + [_rows(tm, 512), _rows(tm, 512), _rows(tm, 128), _rows(tm, 128), _rows(tm, 512)],
        out_shape=_part_shapes(rows) + [_sds((rows, 512)), _sds((rows, 512), BF16), _sds((rows, 128), BF16),
                                        _sds((rows, 128), BF16), _sds((rows, 512))],
        compiler_params=_params(48, ("arbitrary",)),
    )(x2d, g1, w_in_t)
    return u_parts, zs, q, k, v, za


def _in_proj_bwd(x2d, dh1, g1, w_in_t, du_parts, dzs, dq, dk, dv, dza, n_seq, seq):
    rows = x2d.shape[0]
    tm = 256
    slab, steps, _, _ = _scan_geometry(n_seq, seq)
    pieces = ((0, 512), (512, 1024), (1024, 1536), (1536, 1664), (1664, 1792), (1792, 2304))

    def body(x_ref, dh1_ref, g_ref, w_ref, *refs):
        du_parts, (dzs_ref, dq_ref, dk_ref, dv_ref, dza_ref, gx_ref, dw_ref, dg_ref) = refs[:_SCAN_PARTS], refs[_SCAN_PARTS:]

        @pl.when(pl.program_id(0) == 0)
        def _():
            dw_ref[...] = jnp.zeros_like(dw_ref)
            dg_ref[...] = jnp.zeros_like(dg_ref)

        x = x_ref[...]
        g = g_ref[...]
        r = lax.rsqrt(jnp.mean(x * x, axis=-1, keepdims=True) + EPS)
        xr = x * r
        hn = (xr * g).astype(BF16)
        dhn = jnp.zeros((tm, D_MODEL), F32)
        du = _load_chunks(du_parts, pl.program_id(0) * (tm // steps), tm // steps, steps, slab)
        for (a, b), piece in zip(pieces, (du, dzs_ref[...], dq_ref[...], dk_ref[...], dv_ref[...], dza_ref[...])):
            piece = piece.astype(BF16)
            dhn = dhn + _dot(piece, w_ref[a:b, :])
            dw_ref[a:b, :] += _dot_tn(piece, hn)
        dg_ref[...] += jnp.sum(dhn * xr, axis=0, keepdims=True)
        a_ = dhn * g
        gx_ref[...] = dh1_ref[...] + r * a_ - xr * (r * jnp.mean(a_ * xr, axis=-1, keepdims=True))

    return _call(
        body, name="in_proj_bwd", grid=(rows // tm,),
        in_specs=[_rows(tm, D_MODEL), _rows(tm, D_MODEL), _full((1, D_MODEL)), _full((D_IN, D_MODEL))]
        + _whole_parts(rows) + [_rows(tm, 512), _rows(tm, 512), _rows(tm, 128), _rows(tm, 128), _rows(tm, 512)],
        out_specs=[_rows(tm, D_MODEL), _full((D_IN, D_MODEL)), _full((1, D_MODEL))],
        out_shape=[_sds((rows, D_MODEL)), _sds((D_IN, D_MODEL)), _sds((1, D_MODEL))],
        compiler_params=_params(52, ("arbitrary",)),
    )(x2d, dh1, g1, w_in_t, *du_parts, dzs, dq, dk, dv, dza)


def _iota(shape, axis):
    return lax.broadcasted_iota(jnp.int32, shape, axis)


def _exact_dot(a, b):
    return jnp.dot(a, b, precision=lax.Precision.HIGHEST, preferred_element_type=F32)


_HALF_GROUPS = SSM_GROUPS // 2
_N_SHIFT = SSM_STATE.bit_length() - 1
_P_SHIFT = SSM_GROUP_CH.bit_length() - 1


def _s5_operands(lam_re, lam_im, log_step, b_re, b_im, c_re, c_im):
    g, n, p = SSM_GROUPS, SSM_STATE, SSM_GROUP_CH
    gn, gp, hn_, hp = g * n, g * p, _HALF_GROUPS * n, _HALF_GROUPS * p
    eye_g = _iota((g, g), 0) == _iota((g, g), 1)
    step = jnp.sum(jnp.where(eye_g, jnp.exp(log_step), 0.0), axis=1, keepdims=True)
    a_re = lam_re * step
    a_im = lam_im * step
    mag = jnp.exp(a_re)
    lbar_re = mag * jnp.cos(a_im)
    lbar_im = mag * jnp.sin(a_im)
    n_re = lbar_re - 1.0
    den = lam_re * lam_re + lam_im * lam_im
    f_re = (n_re * lam_re + lbar_im * lam_im) / den
    f_im = (lbar_im * lam_re - n_re * lam_im) / den

    spread_n = (_iota((n, gn), 0) == (_iota((n, gn), 1) & (n - 1))).astype(F32)
    own_g = _iota((g, gn), 0) == (_iota((g, gn), 1) >> _N_SHIFT)

    def to_row(a):
        return jnp.sum(jnp.where(own_g, _exact_dot(a, spread_n), 0.0), axis=0, keepdims=True)

    per_group = ((_iota((gp, g), 0) >> _P_SHIFT) == _iota((gp, g), 1)).astype(F32)
    fx_re, fx_im = _exact_dot(per_group, f_re), _exact_dot(per_group, f_im)
    bbar_re = fx_re * b_re - fx_im * b_im
    bbar_im = fx_re * b_im + fx_im * b_re

    tile_n = (_iota((n, hn_), 0) == (_iota((n, hn_), 1) & (n - 1))).astype(F32)
    same_group = (_iota((hp, hn_), 0) >> _P_SHIFT) == (_iota((hp, hn_), 1) >> _N_SHIFT)

    def embed(a, hf):
        return jnp.where(same_group, _exact_dot(a[hf * hp:(hf + 1) * hp], tile_n), 0.0)

    return (to_row(lbar_re), to_row(lbar_im), embed(bbar_re, 0), embed(bbar_re, 1), embed(bbar_im, 0),
            embed(bbar_im, 1), embed(c_re, 0), embed(c_re, 1), embed(c_im, 0), embed(c_im, 1))


_S5_PARAM_SHAPES = ((SSM_GROUPS, SSM_STATE), (SSM_GROUPS, SSM_STATE), (1, SSM_GROUPS),
                    (D_SSM, SSM_STATE), (D_SSM, SSM_STATE), (D_SSM, SSM_STATE), (D_SSM, SSM_STATE))
_CM_SHAPE = (2, _HALF_GROUPS * SSM_GROUP_CH, _HALF_GROUPS * SSM_STATE)
_S5_OPERAND_SHAPES = ((1, SSM_LANES), (1, SSM_LANES), _CM_SHAPE, _CM_SHAPE, _CM_SHAPE, _CM_SHAPE)


def _s5_params_fwd(*params):
    def body(*refs):
        ins, (lre_ref, lim_ref, btre_ref, btim_ref, cmre_ref, cmim_ref) = refs[:7], refs[7:]
        vals = _s5_operands(*[r[...] for r in ins])
        lre_ref[...] = vals[0]
        lim_ref[...] = vals[1]
        for ref, pair in zip((btre_ref, btim_ref, cmre_ref, cmim_ref), (vals[2:4], vals[4:6], vals[6:8], vals[8:10])):
            ref[0] = pair[0].astype(BF16)
            ref[1] = pair[1].astype(BF16)

    dtypes = (F32, F32, BF16, BF16, BF16, BF16)
    return _call(
        body, name="s5_params_fwd",
        in_specs=[_full(s) for s in _S5_PARAM_SHAPES], out_specs=[_full(s) for s in _S5_OPERAND_SHAPES],
        out_shape=[_sds(s, d) for s, d in zip(_S5_OPERAND_SHAPES, dtypes)], compiler_params=_params(32),
    )(*params)


def _s5_params_bwd(params, cotangents):
    def body(*refs):
        ins, (dlre, dlim, dbtre, dbtim, dcmre, dcmim), outs = refs[:7], refs[7:13], refs[13:]
        _, vjp = jax.vjp(_s5_operands, *[r[...] for r in ins])
        cts = (dlre[...], dlim[...], dbtre[0], dbtre[1], dbtim[0], dbtim[1], dcmre[0], dcmre[1], dcmim[0], dcmim[1])
        for ref, val in zip(outs, vjp(cts)):
            ref[...] = val

    return _call(
        body, name="s5_params_bwd",
        in_specs=[_full(s) for s in _S5_PARAM_SHAPES + _S5_OPERAND_SHAPES],
        out_specs=[_full(s) for s in _S5_PARAM_SHAPES],
        out_shape=[_sds(s) for s in _S5_PARAM_SHAPES], compiler_params=_params(48),
    )(*params, *cotangents)


def _scan_geometry(n_seq, seq):
    slab = n_seq * SCAN_CHUNKS
    steps = seq // SCAN_CHUNKS
    tile_rows = slab * SCAN_TILE_STEPS
    n_tiles = steps // SCAN_TILE_STEPS
    return slab, steps, tile_rows, n_tiles


_SCAN_PARTS = D_SSM // LANES


def _whole_parts(rows):
    return [_full((rows, LANES))] * _SCAN_PARTS


def _part_shapes(rows):
    return [_sds((rows, LANES))] * _SCAN_PARTS


def _load_chunks(parts, first_chunk, n_chunks, steps, slab):
    return jnp.concatenate([
        jnp.concatenate([ref[pl.ds(first_chunk + q, steps, stride=slab), :] for ref in parts], axis=1)
        for q in range(n_chunks)], axis=0)


def _store_chunks(parts, first_chunk, value, steps, slab):
    for q in range(value.shape[0] // steps):
        for j, ref in enumerate(parts):
            ref[pl.ds(first_chunk + q, steps, stride=slab), :] = value[q * steps:(q + 1) * steps,
                                                                     j * LANES:(j + 1) * LANES]


def _join_parts(parts):
    return jnp.concatenate([ref[...] for ref in parts], axis=1)


def _split_parts(parts, value):
    for j, ref in enumerate(parts):
        ref[...] = value[:, j * LANES:(j + 1) * LANES]


def _complex_power(re, im, n):
    out = None
    while n:
        if n & 1:
            out = (re, im) if out is None else (out[0] * re - out[1] * im, out[0] * im + out[1] * re)
        n >>= 1
        if n:
            re, im = re * re - im * im, 2.0 * re * im
    return out


def _chunk_carry(sum_re, sum_im, carry_re, carry_im, a_re, a_im, n_seq, reverse):
    carry_re[...] = jnp.zeros_like(carry_re)
    carry_im[...] = jnp.zeros_like(carry_im)
    for s in range(n_seq):
        order = range(SCAN_CHUNKS - 2, -1, -1) if reverse else range(1, SCAN_CHUNKS)
        for c in order:
            r = s * SCAN_CHUNKS + c
            p = r + 1 if reverse else r - 1
            p_re, p_im = carry_re[p:p + 1, :], carry_im[p:p + 1, :]
            carry_re[r:r + 1, :] = a_re * p_re - a_im * p_im + sum_re[p:p + 1, :]
            carry_im[r:r + 1, :] = a_re * p_im + a_im * p_re + sum_im[p:p + 1, :]


def _s5_scan_fwd(u_parts, bt_re, bt_im, cm_re, cm_im, lbar_re, lbar_im, d_row, n_seq, seq):
    slab, steps, tile_rows, n_tiles = _scan_geometry(n_seq, seq)
    rows = u_parts[0].shape[0]

    def body(*refs):
        u_refs, refs = refs[:_SCAN_PARTS], refs[_SCAN_PARTS:]
        (bre_ref, bim_ref, cre_ref, cim_ref, lre_ref, lim_ref, d_ref), refs = refs[:7], refs[7:]
        y_refs, (hre_ref, him_ref, st_re, st_im, h0_re, h0_im, buf_re, buf_im) = refs[:_SCAN_PARTS], refs[_SCAN_PARTS:]
        second = pl.program_id(0) == 1
        i = pl.program_id(1)

        @pl.when(jnp.logical_and(i == 0, jnp.logical_not(second)))
        def _():
            st_re[...] = jnp.zeros_like(st_re)
            st_im[...] = jnp.zeros_like(st_im)

        u = _join_parts(u_refs)
        ub = u.astype(BF16)
        for hf in range(2):
            cols = slice(hf * 1024, (hf + 1) * 1024)
            buf_re[:, cols] = _dot(ub[:, hf * 256:(hf + 1) * 256], bre_ref[hf])
            buf_im[:, cols] = _dot(ub[:, hf * 256:(hf + 1) * 256], bim_ref[hf])

        for lc in range(SSM_LANES // SCAN_LANE_CHUNK):
            cols = slice(lc * SCAN_LANE_CHUNK, (lc + 1) * SCAN_LANE_CHUNK)
            l_re = jnp.broadcast_to(lre_ref[:, cols], (slab, SCAN_LANE_CHUNK))
            l_im = jnp.broadcast_to(lim_ref[:, cols], (slab, SCAN_LANE_CHUNK))

            def scan_tile(keep_states):
                def step(t, carry):
                    s_re, s_im = carry
                    r0 = pl.multiple_of(t * slab, slab)
                    n_re = l_re * s_re - l_im * s_im + buf_re[pl.ds(r0, slab), cols]
                    n_im = l_re * s_im + l_im * s_re + buf_im[pl.ds(r0, slab), cols]
                    if keep_states:
                        buf_re[pl.ds(r0, slab), cols] = n_re
                        buf_im[pl.ds(r0, slab), cols] = n_im
                    return n_re, n_im

                s_re, s_im = lax.fori_loop(0, SCAN_TILE_STEPS, step, (st_re[:, cols], st_im[:, cols]), unroll=True)
                st_re[:, cols] = s_re
                st_im[:, cols] = s_im

            pl.when(jnp.logical_not(second))(lambda: scan_tile(False))
            pl.when(second)(lambda: scan_tile(True))

        @pl.when(jnp.logical_and(i == n_tiles - 1, jnp.logical_not(second)))
        def _():
            a_re, a_im = _complex_power(lre_ref[...], lim_ref[...], steps)
            _chunk_carry(st_re, st_im, h0_re, h0_im, a_re, a_im, n_seq, reverse=False)
            st_re[...] = h0_re[...]
            st_im[...] = h0_im[...]

        @pl.when(second)
        def _():
            h_re = buf_re[...].astype(BF16)
            h_im = buf_im[...].astype(BF16)
            hre_ref[...] = h_re
            him_ref[...] = h_im
            for hf in range(2):
                cols = slice(hf * 1024, (hf + 1) * 1024)
                ycols = slice(hf * 256, (hf + 1) * 256)
                y_half = (_dot_nt(h_re[:, cols], cre_ref[hf]) - _dot_nt(h_im[:, cols], cim_ref[hf])
                          + d_ref[:, ycols] * u[:, ycols])
                _split_parts(y_refs[2 * hf:2 * hf + 2], y_half)

    tile = lambda w: pl.BlockSpec((tile_rows, w), lambda p, i: (i, 0))
    out_tile = lambda w: pl.BlockSpec((tile_rows, w), lambda p, i: (i * p, 0))
    cm = _full(_CM_SHAPE)
    outs = _call(
        body, name="s5_scan_fwd", grid=(2, n_tiles),
        in_specs=[tile(LANES)] * _SCAN_PARTS + [cm, cm, cm, cm, _full((1, SSM_LANES)), _full((1, SSM_LANES)),
                                                _full((1, 512))],
        out_specs=[out_tile(LANES)] * _SCAN_PARTS + [out_tile(SSM_LANES), out_tile(SSM_LANES)],
        out_shape=_part_shapes(rows) + [_sds((rows, SSM_LANES), BF16), _sds((rows, SSM_LANES), BF16)],
        scratch_shapes=[pltpu.VMEM((slab, SSM_LANES), F32)] * 4 + [pltpu.VMEM((tile_rows, SSM_LANES), F32)] * 2,
        compiler_params=_params(40, ("arbitrary", "arbitrary")),
    )(*u_parts, bt_re, bt_im, cm_re, cm_im, lbar_re, lbar_im, d_row)
    return outs[:_SCAN_PARTS], outs[_SCAN_PARTS], outs[_SCAN_PARTS + 1]


def _s5_scan_bwd(dy_parts, u_parts, h_re, h_im, bt_re, bt_im, cm_re, cm_im, lbar_re, lbar_im, d_row, n_seq, seq):
    slab, steps, tile_rows, n_tiles = _scan_geometry(n_seq, seq)
    rows = u_parts[0].shape[0]

    def body(*refs):
        dy_refs, u_refs, refs = refs[:_SCAN_PARTS], refs[_SCAN_PARTS:2 * _SCAN_PARTS], refs[2 * _SCAN_PARTS:]
        (hre_ref, him_ref, bre_ref, bim_ref, cre_ref, cim_ref, lre_ref, lim_ref, d_ref), refs = refs[:9], refs[9:]
        du_refs, refs = refs[:_SCAN_PARTS], refs[_SCAN_PARTS:]
        (dbre_ref, dbim_ref, dcre_ref, dcim_ref, dlre_ref, dlim_ref, dd_ref,
         st_re, st_im, g0_re, g0_im, acc_re, acc_im, buf_re, buf_im) = refs
        second = pl.program_id(0) == 1
        i = pl.program_id(1)

        @pl.when(jnp.logical_and(i == 0, jnp.logical_not(second)))
        def _():
            st_re[...] = jnp.zeros_like(st_re)
            st_im[...] = jnp.zeros_like(st_im)
            acc_re[...] = jnp.zeros_like(acc_re)
            acc_im[...] = jnp.zeros_like(acc_im)
            for ref in (dbre_ref, dbim_ref, dcre_ref, dcim_ref, dd_ref):
                ref[...] = jnp.zeros_like(ref)

        dy = _join_parts(dy_refs)
        dyb = dy.astype(BF16)
        for hf in range(2):
            cols = slice(hf * 1024, (hf + 1) * 1024)
            buf_re[:, cols] = _dot(dyb[:, hf * 256:(hf + 1) * 256], cre_ref[hf])
            buf_im[:, cols] = -_dot(dyb[:, hf * 256:(hf + 1) * 256], cim_ref[hf])

        for lc in range(SSM_LANES // SCAN_LANE_CHUNK):
            cols = slice(lc * SCAN_LANE_CHUNK, (lc + 1) * SCAN_LANE_CHUNK)
            l_re = jnp.broadcast_to(lre_ref[:, cols], (slab, SCAN_LANE_CHUNK))
            l_im = jnp.broadcast_to(lim_ref[:, cols], (slab, SCAN_LANE_CHUNK))

            def advance(r0, s_re, s_im):
                n_re = l_re * s_re + l_im * s_im + buf_re[pl.ds(r0, slab), cols]
                n_im = l_re * s_im - l_im * s_re + buf_im[pl.ds(r0, slab), cols]
                buf_re[pl.ds(r0, slab), cols] = n_re
                buf_im[pl.ds(r0, slab), cols] = n_im
                return n_re, n_im

            def row0(k):
                return pl.multiple_of((SCAN_TILE_STEPS - 1 - k) * slab, slab)

            @pl.when(jnp.logical_not(second))
            def _():
                s_re, s_im = lax.fori_loop(0, SCAN_TILE_STEPS, lambda k, s: advance(row0(k), *s),
                                           (st_re[:, cols], st_im[:, cols]), unroll=True)
                st_re[:, cols] = s_re
                st_im[:, cols] = s_im

            @pl.when(second)
            def _():
                def step(k, carry):
                    s_re, s_im, a_re, a_im = carry
                    r0 = row0(k)
                    hr = hre_ref[pl.ds(r0, slab), cols].astype(F32)
                    hi = him_ref[pl.ds(r0, slab), cols].astype(F32)
                    a_re = a_re + s_re * hr + s_im * hi
                    a_im = a_im + s_im * hr - s_re * hi
                    return advance(r0, s_re, s_im) + (a_re, a_im)

                zero = jnp.zeros((slab, SCAN_LANE_CHUNK), F32)
                s_re, s_im, a_re, a_im = lax.fori_loop(
                    0, SCAN_TILE_STEPS, step, (st_re[:, cols], st_im[:, cols], zero, zero), unroll=True)
                st_re[:, cols] = s_re
                st_im[:, cols] = s_im
                acc_re[:, cols] += a_re
                acc_im[:, cols] += a_im

        @pl.when(jnp.logical_and(i == n_tiles - 1, jnp.logical_not(second)))
        def _():
            p_re, p_im = _complex_power(lre_ref[...], lim_ref[...], steps)
            _chunk_carry(st_re, st_im, g0_re, g0_im, p_re, -p_im, n_seq, reverse=True)
            st_re[...] = g0_re[...]
            st_im[...] = g0_im[...]

        @pl.when(second)
        def _():
            u = _join_parts(u_refs)
            ub = u.astype(BF16)
            g_re = buf_re[...].astype(BF16)
            g_im = buf_im[...].astype(BF16)
            dd_ref[...] += jnp.sum(dy * u, axis=0, keepdims=True)
            for hf in range(2):
                cols = slice(hf * 1024, (hf + 1) * 1024)
                ycols = slice(hf * 256, (hf + 1) * 256)
                du_half = (_dot_nt(g_re[:, cols], bre_ref[hf]) + _dot_nt(g_im[:, cols], bim_ref[hf])
                           + d_ref[:, ycols] * dy[:, ycols])
                _split_parts(du_refs[2 * hf:2 * hf + 2], du_half)
                dbre_ref[hf] += _dot_tn(ub[:, ycols], g_re[:, cols])
                dbim_ref[hf] += _dot_tn(ub[:, ycols], g_im[:, cols])
                dcre_ref[hf] += _dot_tn(dyb[:, ycols], hre_ref[:, cols])
                dcim_ref[hf] -= _dot_tn(dyb[:, ycols], him_ref[:, cols])

        @pl.when(jnp.logical_and(i == n_tiles - 1, second))
        def _():
            dlre_ref[...] = jnp.sum(acc_re[...], axis=0, keepdims=True)
            dlim_ref[...] = jnp.sum(acc_im[...], axis=0, keepdims=True)

    tile = lambda w: pl.BlockSpec((tile_rows, w), lambda p, i: (n_tiles - 1 - i, 0))
    second_tile = lambda w: pl.BlockSpec((tile_rows, w), lambda p, i: (n_tiles - 1 - i * p, 0))
    cm = _full(_CM_SHAPE)
    row = _full((1, SSM_LANES))
    outs = _call(
        body, name="s5_scan_bwd", grid=(2, n_tiles),
        in_specs=[tile(LANES)] * _SCAN_PARTS + [second_tile(LANES)] * _SCAN_PARTS
        + [second_tile(SSM_LANES), second_tile(SSM_LANES), cm, cm, cm, cm, row, row, _full((1, 512))],
        out_specs=[second_tile(LANES)] * _SCAN_PARTS + [cm, cm, cm, cm, row, row, _full((1, 512))],
        out_shape=(_part_shapes(rows) + [_sds(_CM_SHAPE)] * 4 + [_sds((1, SSM_LANES))] * 2 + [_sds((1, 512))]),
        scratch_shapes=[pltpu.VMEM((slab, SSM_LANES), F32)] * 6 + [pltpu.VMEM((tile_rows, SSM_LANES), F32)] * 2,
        compiler_params=_params(48, ("arbitrary", "arbitrary")),
    )(*dy_parts, *u_parts, h_re, h_im, bt_re, bt_im, cm_re, cm_im, lbar_re, lbar_im, d_row)
    return (outs[:_SCAN_PARTS],) + tuple(outs[_SCAN_PARTS:])


def _glu_gate(gl, a, zs):
    return gl * jax.nn.sigmoid(a) * _silu(zs)


def _glu_fwd(y_parts, zs, w_glu, b_glu, n_seq, seq):
    rows = zs.shape[0]
    tm = 512
    slab, steps, _, _ = _scan_geometry(n_seq, seq)

    def body(*refs):
        y_refs, (zs_ref, w_ref, b_ref, o_ref) = refs[:_SCAN_PARTS], refs[_SCAN_PARTS:]
        y = _load_chunks(y_refs, pl.program_id(0) * (tm // steps), tm // steps, steps, slab)
        gl = jax.nn.gelu(y)
        a = _dot(gl.astype(BF16), w_ref[...]) + b_ref[...]
        o_ref[...] = _glu_gate(gl, a, zs_ref[...]).astype(BF16)

    return _call(
        body, name="glu_fwd", grid=(rows // tm,),
        in_specs=_whole_parts(rows) + [_rows(tm, 512), _full((512, 512)), _full((1, 512))],
        out_specs=_rows(tm, 512), out_shape=_sds((rows, 512), BF16),
        compiler_params=_params(32, ("arbitrary",)),
    )(*y_parts, zs, w_glu, b_glu)


def _glu_bwd(y_parts, zs, d_out, w_glu, b_glu, n_seq, seq):
    rows = zs.shape[0]
    tm = 512
    slab, steps, _, _ = _scan_geometry(n_seq, seq)

    def body(*refs):
        y_refs, (zs_ref, d_ref, w_ref, b_ref), refs = refs[:_SCAN_PARTS], refs[_SCAN_PARTS:_SCAN_PARTS + 4], refs[_SCAN_PARTS + 4:]
        dy_refs, (dzs_ref, dw_ref, db_ref) = refs[:_SCAN_PARTS], refs[_SCAN_PARTS:]
        first_chunk = pl.program_id(0) * (tm // steps)

        @pl.when(pl.program_id(0) == 0)
        def _():
            dw_ref[...] = jnp.zeros_like(dw_ref)
            db_ref[...] = jnp.zeros_like(db_ref)

        gl, gelu_vjp = jax.vjp(jax.nn.gelu, _load_chunks(y_refs, first_chunk, tm // steps, steps, slab))
        glb = gl.astype(BF16)
        a = _dot(glb, w_ref[...]) + b_ref[...]
        _, gate_vjp = jax.vjp(_glu_gate, gl, a, zs_ref[...])
        d_gl, d_a, d_zs = gate_vjp(d_ref[...])
        dab = d_a.astype(BF16)
        d_gl = d_gl + _dot_nt(dab, w_ref[...])
        _store_chunks(dy_refs, first_chunk, gelu_vjp(d_gl)[0], steps, slab)
        dzs_ref[...] = d_zs.astype(BF16)
        dw_ref[...] += _dot_tn(glb, dab)
        db_ref[...] += jnp.sum(d_a, axis=0, keepdims=True)

    *dy_parts, dzs, dw, db = _call(
        body, name="glu_bwd", grid=(rows // tm,),
        in_specs=_whole_parts(rows) + [_rows(tm, 512), _rows(tm, 512), _full((512, 512)), _full((1, 512))],
        out_specs=_whole_parts(rows) + [_rows(tm, 512), _full((512, 512)), _full((1, 512))],
        out_shape=_part_shapes(rows) + [_sds((rows, 512), BF16), _sds((512, 512)), _sds((1, 512))],
        compiler_params=_params(40, ("arbitrary",)),
    )(*y_parts, zs, d_out, w_glu, b_glu)
    return dy_parts, dzs, dw, db


_GROUP_ROWS = Q_PER_KV * BLOCK
_BLOCK_SHIFT = BLOCK.bit_length() - 1


def _attn_bias(j):
    row = _iota((_GROUP_ROWS, BLOCK), 0)
    dist_cur = (row & (BLOCK - 1)) - _iota((_GROUP_ROWS, BLOCK), 1)
    dist_prev = dist_cur + BLOCK
    head = row >> _BLOCK_SHIFT
    slope = jnp.zeros((_GROUP_ROWS, BLOCK), F32)
    for g in range(Q_PER_KV):
        slope = jnp.where(head == g, 2.0 ** (-(j * Q_PER_KV + g + 1)), slope)
    bias_cur = jnp.where(dist_cur >= 0, -slope * dist_cur.astype(F32), -jnp.inf)
    bias_prev = jnp.where(dist_prev < WINDOW, -slope * dist_prev.astype(F32), -jnp.inf)
    return bias_cur, bias_prev


_ATTN_BIAS_SCRATCH = pltpu.VMEM((KV_HEADS, 2, _GROUP_ROWS, BLOCK), F32)


def _fill_attn_bias(bias_ref):
    @pl.when(jnp.logical_and(pl.program_id(0) == 0, pl.program_id(1) == 0))
    def _():
        for j in range(KV_HEADS):
            bias_ref[j, 0], bias_ref[j, 1] = _attn_bias(j)


def _stack_heads(x, j):
    heads = range(j * Q_PER_KV, (j + 1) * Q_PER_KV)
    return jnp.concatenate([x[:, h * HEAD_DIM:(h + 1) * HEAD_DIM] for h in heads], axis=0)


def _stack_columns(x, j):
    heads = range(j * Q_PER_KV, (j + 1) * Q_PER_KV)
    return jnp.concatenate([jnp.broadcast_to(x[:, h:h + 1], (BLOCK, 1)) for h in heads], axis=0)


def _attn_fwd(q, k, v, za, sinks, n_seq, seq):
    nb = seq // BLOCK
    rows = q.shape[0]

    def body(q_ref, kc_ref, kp_ref, vc_ref, vp_ref, za_ref, sk_ref, o_ref, ao_ref, lse_ref, bias_ref):
        _fill_attn_bias(bias_ref)
        has_prev = pl.program_id(1) > 0
        q_all = q_ref[...]
        for j in range(KV_HEADS):
            js = slice(j * HEAD_DIM, (j + 1) * HEAD_DIM)
            bias_c, bias_p = bias_ref[j, 0], bias_ref[j, 1]
            q4 = _stack_heads(q_all, j)
            sc = _dot_nt(q4, kc_ref[:, js]) * ATTN_SCALE + bias_c
            sp = _dot_nt(q4, kp_ref[:, js]) * ATTN_SCALE + jnp.where(has_prev, bias_p, -jnp.inf)
            sink = _stack_columns(sk_ref[...], j)
            m = jnp.maximum(jnp.maximum(jnp.max(sc, axis=-1, keepdims=True), jnp.max(sp, axis=-1, keepdims=True)), sink)
            ec = jnp.exp(sc - m)
            ep = jnp.exp(sp - m)
            den = jnp.sum(ec, axis=-1, keepdims=True) + jnp.sum(ep, axis=-1, keepdims=True) + jnp.exp(sink - m)
            inv = 1.0 / den
            o4 = _dot((ec * inv).astype(BF16), vc_ref[:, js]) + _dot((ep * inv).astype(BF16), vp_ref[:, js])
            lse4 = m + jnp.log(den)
            for g in range(Q_PER_KV):
                h = j * Q_PER_KV + g
                o_ref[:, h * HEAD_DIM:(h + 1) * HEAD_DIM] = o4[g * BLOCK:(g + 1) * BLOCK]
                lse_ref[:, h:h + 1] = lse4[g * BLOCK:(g + 1) * BLOCK]
        ao_ref[...] = (o_ref[...] * _silu(za_ref[...])).astype(BF16)

    cur = lambda w: pl.BlockSpec((BLOCK, w), lambda b, n: (b * nb + n, 0))
    prev = lambda w: pl.BlockSpec((BLOCK, w), lambda b, n: (b * nb + jnp.maximum(n - 1, 0), 0))
    return _call(
        body, name="attn_fwd", grid=(n_seq, nb),
        in_specs=[cur(512), cur(128), prev(128), cur(128), prev(128), cur(512), _full((1, N_HEADS))],
        out_specs=[cur(512), cur(512), cur(N_HEADS)],
        out_shape=[_sds((rows, 512)), _sds((rows, 512), BF16), _sds((rows, N_HEADS))],
        scratch_shapes=[_ATTN_BIAS_SCRATCH], compiler_params=_params(32, ("arbitrary", "arbitrary")),
    )(q, k, k, v, v, za, sinks)


def _attn_bwd(q, k, v, za, o, lse, d_ao, sinks, n_seq, seq):
    nb = seq // BLOCK
    rows = q.shape[0]

    def body(q_ref, q2_ref, kc_ref, kp_ref, vc_ref, vp_ref, za_ref, za2_ref, o_ref, lse_ref, lse2_ref,
             d_ref, d2_ref, sk_ref, dq_ref, dk_ref, dv_ref, dza_ref, dsk_ref, bias_ref, delta_ref):
        n = nb - 1 - pl.program_id(1)
        _fill_attn_bias(bias_ref)

        @pl.when(jnp.logical_and(pl.program_id(0) == 0, pl.program_id(1) == 0))
        def _():
            dsk_ref[...] = jnp.zeros_like(dsk_ref)
            delta_ref[...] = jnp.zeros_like(delta_ref)

        has_prev = n > 0
        has_next = n + 1 < nb

        _, gate_vjp = jax.vjp(lambda o_, z_: o_ * _silu(z_), o_ref[...], za_ref[...])
        d_o, d_za = gate_vjp(d_ref[...])
        dza_ref[...] = d_za.astype(BF16)
        d_o2 = d2_ref[...] * _silu(za2_ref[...])
        q_all, q2_all = q_ref[...], q2_ref[...]
        lse_all, lse2_all = lse_ref[...], lse2_ref[...]

        for j in range(KV_HEADS):
            js = slice(j * HEAD_DIM, (j + 1) * HEAD_DIM)
            kc, kp, vc, vp = kc_ref[:, js], kp_ref[:, js], vc_ref[:, js], vp_ref[:, js]
            bias_c, bias_p = bias_ref[j, 0], bias_ref[j, 1]
            q4 = _stack_heads(q_all, j)
            do4b = _stack_heads(d_o, j).astype(BF16)
            lse4 = _stack_columns(lse_all, j)
            pc = jnp.exp(_dot_nt(q4, kc) * ATTN_SCALE + bias_c - lse4)
            pp = jnp.exp(_dot_nt(q4, kp) * ATTN_SCALE + jnp.where(has_prev, bias_p, -jnp.inf) - lse4)
            dpc = _dot_nt(do4b, vc)
            dpp = _dot_nt(do4b, vp)
            delta = jnp.sum(pc * dpc, axis=-1, keepdims=True) + jnp.sum(pp * dpp, axis=-1, keepdims=True)
            delta2 = jnp.where(has_next, delta_ref[j], 0.0)
            delta_ref[j] = delta
            dsc = (pc * (dpc - delta)).astype(BF16)
            dsp = (pp * (dpp - delta)).astype(BF16)
            dq4 = ((_dot(dsc, kc) + _dot(dsp, kp)) * ATTN_SCALE).astype(BF16)
            sink_loss = jnp.exp(_stack_columns(sk_ref[...], j) - lse4) * delta
            for g in range(Q_PER_KV):
                h = j * Q_PER_KV + g
                dq_ref[:, h * HEAD_DIM:(h + 1) * HEAD_DIM] = dq4[g * BLOCK:(g + 1) * BLOCK]
                dsk_ref[0:1, h:h + 1] -= jnp.sum(sink_loss[g * BLOCK:(g + 1) * BLOCK], axis=0, keepdims=True)
            dk = _dot_tn(dsc, q4)
            dv = _dot_tn(pc.astype(BF16), do4b)
            q4n = _stack_heads(q2_all, j)
            do4nb = _stack_heads(d_o2, j).astype(BF16)
            p2 = jnp.exp(_dot_nt(q4n, kc) * ATTN_SCALE + jnp.where(has_next, bias_p, -jnp.inf)
                         - _stack_columns(lse2_all, j))
            ds2 = (p2 * (_dot_nt(do4nb, vc) - delta2)).astype(BF16)
            dk = dk + _dot_tn(ds2, q4n)
            dv = dv + _dot_tn(p2.astype(BF16), do4nb)
            dk_ref[:, js] = (dk * ATTN_SCALE).astype(BF16)
            dv_ref[:, js] = dv.astype(BF16)

    cur = lambda w: pl.BlockSpec((BLOCK, w), lambda b, s: (b * nb + nb - 1 - s, 0))
    prev = lambda w: pl.BlockSpec((BLOCK, w), lambda b, s: (b * nb + jnp.maximum(nb - 2 - s, 0), 0))
    nxt = lambda w: pl.BlockSpec((BLOCK, w), lambda b, s: (b * nb + jnp.minimum(nb - s, nb - 1), 0))
    return _call(
        body, name="attn_bwd", grid=(n_seq, nb),
        in_specs=[cur(512), nxt(512), cur(128), prev(128), cur(128), prev(128), cur(512), nxt(512),
                  cur(512), cur(N_HEADS), nxt(N_HEADS), cur(512), nxt(512), _full((1, N_HEADS))],
        out_specs=[cur(512), cur(128), cur(128), cur(512), _full((1, N_HEADS))],
        out_shape=[_sds((rows, 512), BF16), _sds((rows, 128), BF16), _sds((rows, 128), BF16),
                   _sds((rows, 512), BF16), _sds((1, N_HEADS))],
        scratch_shapes=[_ATTN_BIAS_SCRATCH, pltpu.VMEM((KV_HEADS, _GROUP_ROWS, 1), F32)],
        compiler_params=_params(32, ("arbitrary", "arbitrary")),
    )(q, q, k, k, v, v, za, za, o, lse, lse, d_ao, d_ao, sinks)


def _tail(ssm_out, attn_out, x2d, p2d, target, w_out, g2, w_gate, b_gate, w_proj):
    rows = x2d.shape[0]
    tm = 512

    def body(so_ref, ao_ref, x_ref, p_ref, t_ref, wo_ref, g2_ref, wg_ref, bg_ref, wp_ref,
             dh1_ref, dso_ref, dao_ref, dwo_ref, dwg_ref, dwp_ref, dbg_ref, dg2_ref, loss_ref):
        @pl.when(pl.program_id(0) == 0)
        def _():
            for ref in (dwo_ref, dwg_ref, dwp_ref, dbg_ref, dg2_ref, loss_ref):
                ref[...] = jnp.zeros_like(ref)

        so = so_ref[...]
        ao = ao_ref[...]
        g2 = g2_ref[...]
        mixed = _dot(so, wo_ref[0:512, :]) + _dot(ao, wo_ref[512:1024, :])
        r = lax.rsqrt(jnp.mean(mixed * mixed, axis=-1, keepdims=True) + EPS)
        mr = mixed * r
        h1 = x_ref[...] + mr * g2
        h1b = h1.astype(BF16)
        gate = jax.nn.sigmoid(_dot(h1b, wg_ref[...]) + bg_ref[...])
        pb = p_ref[...].astype(BF16)
        wp_blocks = [slice(j * D_PLE, (j + 1) * D_PLE) for j in range(N_CHIPS)]
        pp = jnp.concatenate([_dot(pb, wp_ref[blk, :]) for blk in wp_blocks], axis=1)
        err = h1 + gate * pp - t_ref[...]
        loss_ref[...] += 0.5 * jnp.sum(jnp.mean(err * err, axis=-1, keepdims=True), axis=0, keepdims=True)

        dh2 = err * (1.0 / D_MODEL)
        d_glin = dh2 * pp * gate * (1.0 - gate)
        d_glin_b = d_glin.astype(BF16)
        dwg_ref[...] += _dot_tn(h1b, d_glin_b)
        dbg_ref[...] += jnp.sum(d_glin, axis=0, keepdims=True)
        d_pp = (dh2 * gate).astype(BF16)
        for blk in wp_blocks:
            dwp_ref[blk, :] += _dot_tn(pb, d_pp[:, blk])
        dh1 = dh2 + _dot_nt(d_glin_b, wg_ref[...])
        dh1_ref[...] = dh1
        dg2_ref[...] += jnp.sum(dh1 * mr, axis=0, keepdims=True)
        a_ = dh1 * g2
        d_mixed = (r * a_ - mr * (r * jnp.mean(a_ * mr, axis=-1, keepdims=True))).astype(BF16)
        dwo_ref[0:512, :] += _dot_tn(so, d_mixed)
        dwo_ref[512:1024, :] += _dot_tn(ao, d_mixed)
        dso_ref[...] = _dot_nt(d_mixed, wo_ref[0:512, :])
        dao_ref[...] = _dot_nt(d_mixed, wo_ref[512:1024, :])

    return _call(
        body, name="tail_fwd_bwd", grid=(rows // tm,),
        in_specs=[_rows(tm, 512), _rows(tm, 512), _rows(tm, D_MODEL), _rows(tm, D_PLE), _rows(tm, D_MODEL),
                  _full((D_MODEL, D_MODEL)), _full((1, D_MODEL)), _full((D_MODEL, D_MODEL)), _full((1, D_MODEL)),
                  _full((N_CHIPS * D_PLE, D_PLE))],
        out_specs=[_rows(tm, D_MODEL), _rows(tm, 512), _rows(tm, 512), _full((D_MODEL, D_MODEL)),
                   _full((D_MODEL, D_MODEL)), _full((N_CHIPS * D_PLE, D_PLE)), _full((1, D_MODEL)), _full((1, D_MODEL)),
                   _full((1, 1))],
        out_shape=[_sds((rows, D_MODEL)), _sds((rows, 512)), _sds((rows, 512)), _sds((D_MODEL, D_MODEL)),
                   _sds((D_MODEL, D_MODEL)), _sds((N_CHIPS * D_PLE, D_PLE)), _sds((1, D_MODEL)), _sds((1, D_MODEL)),
                   _sds((1, 1))],
        compiler_params=_params(52, ("arbitrary",)),
    )(ssm_out, attn_out, x2d, p2d, target, w_out, g2, w_gate, b_gate, w_proj)


def _local_step(x, p, target, pre_norm_g, w_in_t, s5_params, ssm_d, w_glu, b_glu, sinks, w_out, post_norm_g, w_proj,
                w_gate, b_gate):
    n_seq, seq, _ = x.shape
    rows = n_seq * seq
    x2d = x.reshape(rows, D_MODEL)
    p2d = p.reshape(rows, D_PLE)
    t2d = target.reshape(rows, D_MODEL)

    l_re, l_im, bt_re, bt_im, cm_re, cm_im = _s5_params_fwd(*s5_params)

    u_scan, zs, q, k, v, za = _in_proj(x2d, pre_norm_g, w_in_t, n_seq, seq)
    y_scan, h_re, h_im = _s5_scan_fwd(u_scan, bt_re, bt_im, cm_re, cm_im, l_re, l_im, ssm_d, n_seq, seq)
    ssm_out = _glu_fwd(y_scan, zs, w_glu, b_glu, n_seq, seq)
    o, attn_out, lse = _attn_fwd(q, k, v, za, sinks, n_seq, seq)

    dh1, d_so, d_ao, d_w_out, d_w_gate, d_w_proj, d_b_gate, d_g2, loss = _tail(
        ssm_out, attn_out, x2d, p2d, t2d, w_out, post_norm_g, w_gate, b_gate, w_proj)

    dq, dk, dv, dza, d_sinks = _attn_bwd(q, k, v, za, o, lse, d_ao, sinks, n_seq, seq)
    dy_scan, dzs, d_w_glu, d_b_glu = _glu_bwd(y_scan, zs, d_so, w_glu, b_glu, n_seq, seq)
    du_scan, d_bt_re, d_bt_im, d_cm_re, d_cm_im, d_l_re, d_l_im, d_d = _s5_scan_bwd(
        dy_scan, u_scan, h_re, h_im, bt_re, bt_im, cm_re, cm_im, l_re, l_im, ssm_d, n_seq, seq)
    d_lam_re, d_lam_im, d_log_step, d_b_re, d_b_im, d_c_re, d_c_im = _s5_params_bwd(
        s5_params, (d_l_re, d_l_im, d_bt_re, d_bt_im, d_cm_re, d_cm_im))

    grad_x, d_w_in_t, d_g1 = _in_proj_bwd(x2d, dh1, pre_norm_g, w_in_t, du_scan, dzs, dq, dk, dv, dza, n_seq, seq)
    grads = dict(
        pre_norm_g=d_g1, w_in=d_w_in_t, ssm_lam_re=d_lam_re, ssm_lam_im=d_lam_im, ssm_log_step=d_log_step,
        ssm_b_re=d_b_re, ssm_b_im=d_b_im, ssm_c_re=d_c_re, ssm_c_im=d_c_im, ssm_d=d_d, ssm_w_glu=d_w_glu,
        ssm_b_glu=d_b_glu, attn_sinks=d_sinks, w_out=d_w_out, post_norm_g=d_g2, pl_w_proj=d_w_proj,
        pl_w_gate=d_w_gate, pl_b_gate=d_b_gate)
    return grad_x.reshape(x.shape), loss, grads


_BIG = ("w_in", "ssm_w_glu", "w_out", "pl_w_proj", "pl_w_gate")
_BIG_SHARD = {"w_in": (D_IN // N_CHIPS, D_MODEL), "ssm_w_glu": (D_SSM // N_CHIPS, D_SSM),
              "w_out": (D_MODEL // N_CHIPS, D_MODEL), "pl_w_proj": (D_PLE, D_MODEL // N_CHIPS),
              "pl_w_gate": (D_MODEL // N_CHIPS, D_MODEL)}
_SMALL = {"pre_norm_g": (1, D_MODEL), "ssm_lam_re": (SSM_GROUPS, SSM_STATE), "ssm_lam_im": (SSM_GROUPS, SSM_STATE),
          "ssm_log_step": (1, SSM_GROUPS), "ssm_b_re": (D_SSM, SSM_STATE), "ssm_b_im": (D_SSM, SSM_STATE),
          "ssm_c_re": (D_SSM, SSM_STATE), "ssm_c_im": (D_SSM, SSM_STATE), "ssm_d": (1, D_SSM), "ssm_b_glu": (1, D_SSM),
          "attn_sinks": (1, N_HEADS), "post_norm_g": (1, D_MODEL), "pl_b_gate": (1, D_MODEL)}
_VEC_ROWS = ("pre_norm_g", "post_norm_g", "pl_b_gate", "ssm_d", "ssm_b_glu", "attn_sinks", "ssm_log_step", "loss")
_SMALL_GROUPS = (
    ("vec", (8, D_MODEL), tuple((name, r) for r, name in enumerate(_VEC_ROWS))),
    ("lam", (2 * SSM_GROUPS, SSM_STATE), (("ssm_lam_re", 0), ("ssm_lam_im", SSM_GROUPS))),
)
_SMALL_EARLY = ("ssm_b_re", "ssm_b_im", "ssm_c_re", "ssm_c_im")
_SMALL_ORDER = tuple(name for _, _, members in _SMALL_GROUPS for name, _ in members) + _SMALL_EARLY
_WEIGHT_ORDER = ("pre_norm_g", "w_in", "ssm_lam_re", "ssm_lam_im", "ssm_log_step", "ssm_b_re", "ssm_b_im", "ssm_c_re",
                 "ssm_c_im", "ssm_d", "ssm_w_glu", "ssm_b_glu", "attn_sinks", "w_out", "post_norm_g", "pl_w_proj",
                 "pl_w_gate", "pl_b_gate")


def _small_shape(name):
    return (1, 1) if name == "loss" else _SMALL[name]


def _to_kernel_form(name, a):
    a = a[0]
    if name == "w_in":
        return a.T
    if name in ("ssm_b_re", "ssm_b_im"):
        a = a.transpose(0, 2, 1)
    return a.reshape(_SMALL[name]) if name in _SMALL else a


def _from_kernel_form(name, a, shape):
    if name == "w_in":
        a = a.T
    if name in ("ssm_b_re", "ssm_b_im"):
        a = a.reshape(SSM_GROUPS, SSM_GROUP_CH, SSM_STATE).transpose(0, 2, 1)
    return a.reshape(shape)


def _mesh_place():
    x, y, c = lax.axis_index("x"), lax.axis_index("y"), lax.axis_index("c")
    other_chips = ((1 - x, y), (x, 1 - y), (1 - x, 1 - y))
    return x, y, c, other_chips


def _gather_copies(s_refs, g_refs, send_sems, recv_sems, local_sems):
    x, y, c, other_chips = _mesh_place()
    started = []
    for i, (s_ref, g_ref) in enumerate(zip(s_refs, g_refs)):
        rows = s_ref.shape[0]
        half = rows // 2

        def block(chip, g_ref=g_ref, rows=rows, half=half):
            return g_ref.at[pl.ds((2 * chip[0] + chip[1]) * rows + c * half, half), :]

        def copy(k, chip, to, src=None, i=i, block=block):
            return pltpu.make_async_remote_copy(
                src_ref=block(chip) if src is None else src, dst_ref=block(chip), send_sem=send_sems.at[6 * i + k],
                recv_sem=recv_sems.at[6 * i + k], device_id=to, device_id_type=MESH)

        own = pltpu.make_async_copy(s_ref, g_ref.at[pl.ds((2 * x + y) * rows, rows), :], local_sems.at[i])
        own.start()
        first = [copy(k, (x, y), (*chip, c), src=s_ref.at[pl.ds(c * half, half), :])
                 for k, chip in enumerate(other_chips)]
        for cp in first:
            cp.start()
        passed = [copy(3 + k, chip, (x, y, 1 - c)) for k, chip in enumerate(other_chips)]
        started.append((own, first, passed))
    for own, first, passed in started:
        for k in range(3):
            first[k].wait_recv()
            passed[k].start()
    for own, first, passed in started:
        for k in range(3):
            passed[k].wait_recv()
        for cp in first + passed:
            cp.wait_send()
        own.wait()


def _gather_semaphores(n_t):
    return [pltpu.SemaphoreType.DMA((6 * n_t,)), pltpu.SemaphoreType.DMA((6 * n_t,)), pltpu.SemaphoreType.DMA((n_t,))]


def _gather_weights(shards):
    n_t = len(shards)

    def body(*refs):
        _gather_copies(refs[:n_t], refs[n_t:2 * n_t], *refs[2 * n_t + 1:])
        refs[2 * n_t][...] = jnp.zeros_like(refs[2 * n_t])

    any_spec = pl.BlockSpec(memory_space=pl.ANY)
    *full, done = _call(
        body, name="gather_weights", in_specs=[any_spec] * n_t,
        out_specs=[any_spec] * n_t + [pl.BlockSpec(memory_space=pltpu.VMEM)],
        out_shape=[_sds((N_CHIPS * s.shape[0], s.shape[1]), s.dtype) for s in shards]
        + [jax.ShapeDtypeStruct((8, LANES), F32)],
        scratch_shapes=_gather_semaphores(n_t),
    )(*shards)
    return full, done[0, 0]


def _gather_weights_beside(shards):
    n_t = len(shards)
    hbm = pltpu.MemorySpace.HBM
    s_refs = [jax.new_ref(s, memory_space=hbm) for s in shards]
    g_refs = [jax.empty_ref(jax.ShapeDtypeStruct((N_CHIPS * s.shape[0], s.shape[1]), s.dtype), memory_space=hbm)
              for s in shards]

    def launch(send_sems, recv_sems, local_sems):
        x, y, c, other_chips = _mesh_place()
        peers = [(*chip, c) for chip in other_chips] + [(x, y, 1 - c)]
        barrier = pltpu.get_barrier_semaphore()
        for peer in peers:
            pl.semaphore_signal(barrier, inc=1, device_id=peer, device_id_type=MESH)
        pl.semaphore_wait(barrier, len(peers))
        _gather_copies(s_refs, g_refs, send_sems, recv_sems, local_sems)

    pl.kernel(launch, mesh=plsc.ScalarSubcoreMesh(axis_name="sequencer", num_cores=1), name="gather_weights_beside",
              scratch_types=_gather_semaphores(n_t), compiler_params=pltpu.CompilerParams(collective_id=1))()
    return [g[...] for g in g_refs]


_RELATIONS = tuple(((r >> 2) & 1, (r >> 1) & 1, r & 1) for r in range(1, 8))


def _related(place, relation):
    return tuple(1 - a if flip else a for a, flip in zip(place, relation))


def _scatter_beside(mats):
    hbm = pltpu.MemorySpace.HBM
    src_refs = [jax.new_ref(a, memory_space=hbm) for a in mats]
    land_refs = [jax.empty_ref(jax.ShapeDtypeStruct((7, a.shape[0] // 8, a.shape[1]), a.dtype), memory_space=hbm)
                 for a in mats]

    def launch(send_sems, recv_sems):
        me = (lax.axis_index("x"), lax.axis_index("y"), lax.axis_index("c"))
        peers = [_related(me, rel) for rel in _RELATIONS]
        barrier = pltpu.get_barrier_semaphore()
        for peer in peers:
            pl.semaphore_signal(barrier, inc=1, device_id=peer, device_id_type=MESH)
        pl.semaphore_wait(barrier, len(peers))
        copies = []
        for i, (src, land) in enumerate(zip(src_refs, land_refs)):
            hr = land.shape[1]
            for k, (tx, ty, tc) in enumerate(peers):
                rows = pl.ds((2 * tx + ty) * 2 * hr + tc * hr, hr)
                copies.append(pltpu.make_async_remote_copy(
                    src_ref=src.at[rows, :], dst_ref=land.at[k], send_sem=send_sems.at[7 * i + k],
                    recv_sem=recv_sems.at[7 * i + k], device_id=(tx, ty, tc), device_id_type=MESH))
                copies[-1].start()
        for cp in copies:
            cp.wait()

    n_sems = 7 * len(mats)
    pl.kernel(launch, mesh=plsc.ScalarSubcoreMesh(axis_name="sequencer", num_cores=1), name="scatter_beside",
              scratch_types=[pltpu.SemaphoreType.DMA((n_sems,)), pltpu.SemaphoreType.DMA((n_sems,))],
              compiler_params=pltpu.CompilerParams(collective_id=2))()
    return [ref[...] for ref in land_refs]


def _broadcast_beside(arrays):
    hbm = pltpu.MemorySpace.HBM
    src_refs = [jax.new_ref(a, memory_space=hbm) for a in arrays]
    land_refs = [jax.empty_ref(jax.ShapeDtypeStruct((len(_RELATIONS),) + a.shape, a.dtype), memory_space=hbm)
                 for a in arrays]

    def launch(send_sems, recv_sems):
        me = (lax.axis_index("x"), lax.axis_index("y"), lax.axis_index("c"))
        peers = [_related(me, rel) for rel in _RELATIONS]
        barrier = pltpu.get_barrier_semaphore()
        for peer in peers:
            pl.semaphore_signal(barrier, inc=1, device_id=peer, device_id_type=MESH)
        pl.semaphore_wait(barrier, len(peers))
        copies = []
        for i, (src, land) in enumerate(zip(src_refs, land_refs)):
            for k, peer in enumerate(peers):
                copies.append(pltpu.make_async_remote_copy(
                    src_ref=src, dst_ref=land.at[k], send_sem=send_sems.at[7 * i + k],
                    recv_sem=recv_sems.at[7 * i + k], device_id=peer, device_id_type=MESH))
                copies[-1].start()
        for cp in copies:
            cp.wait()

    n_sems = 7 * len(arrays)
    pl.kernel(launch, mesh=plsc.ScalarSubcoreMesh(axis_name="sequencer", num_cores=1), name="broadcast_beside",
              scratch_types=[pltpu.SemaphoreType.DMA((n_sems,)), pltpu.SemaphoreType.DMA((n_sems,))],
              compiler_params=pltpu.CompilerParams(collective_id=3))()
    return [ref[...] for ref in land_refs]


def _exchange_grads(big, small, landed, landed_small):
    n_t = len(big)
    n_g = len(_SMALL_GROUPS)
    names = _SMALL_ORDER
    halves = [(b.shape[0] // N_CHIPS // 2, b.shape[1]) for b in big]
    early = sorted(landed)
    late = [i for i in range(n_t) if i not in landed]
    n_sems = 4 * n_g + 7 * len(late) + n_t
    small_sem0, block_sem0 = n_t, n_t + len(names)
    early_sem0 = block_sem0 + N_CHIPS * len(late)
    landed_sem0 = early_sem0 + 2 * len(early)
    early_small = [n for n in names if n in landed_small]

    def body(*refs):
        pos = 0

        def take(n):
            nonlocal pos
            pos += n
            return refs[pos - n:pos]

        big_refs, small_refs = take(n_t), dict(zip(names, take(len(names))))
        land_refs = dict(zip(early, take(len(early))))
        land_small_refs = dict(zip(early_small, take(len(early_small))))
        out_refs, small_out_refs = take(n_t), dict(zip(names, take(len(names))))
        per_late = lambda: dict(zip(late, take(len(late))))
        ga, gb, pme, send_b, recv_b = per_late(), per_late(), take(n_t), per_late(), per_late()
        own_e, land_e = dict(zip(early, take(len(early)))), dict(zip(early, take(len(early))))
        land_s = dict(zip(early_small, take(len(early_small))))
        s_own, s_sib, s_chips, s_pair = take(n_g), take(n_g), take(n_g), take(n_g)
        stage = dict(zip(names, take(len(names))))
        send_sems, recv_sems, local_sems = take(3)
        x, y, c, other_chips = _mesh_place()
        me = 2 * x + y
        sibling = (x, y, 1 - c)
        sem_at = iter(range(n_sems))

        def remote(src, dst, to):
            k = next(sem_at)
            return pltpu.make_async_remote_copy(src_ref=src, dst_ref=dst, send_sem=send_sems.at[k],
                                                recv_sem=recv_sems.at[k], device_id=to, device_id_type=MESH)

        loads = [pltpu.make_async_copy(small_refs[name], stage[name], local_sems.at[small_sem0 + a])
                 for a, name in enumerate(names)]
        landed_loads = [pltpu.make_async_copy(land_small_refs[name], land_s[name], local_sems.at[landed_sem0 + a])
                        for a, name in enumerate(early_small)]
        for cp in loads + landed_loads:
            cp.start()
        for cp in loads:
            cp.wait()
        small_swaps = []
        for gi, (_, _, members) in enumerate(_SMALL_GROUPS):
            s_own[gi][...] = jnp.zeros_like(s_own[gi])
            for name, r0 in members:
                r, n = _small_shape(name)
                s_own[gi][r0:r0 + r, 0:n] = stage[name][...]
            small_swaps.append(remote(s_own[gi], s_sib[gi], sibling))
            small_swaps[gi].start()
        order = sorted(late, key=lambda i: halves[i][0] * halves[i][1])
        own_loads, big_swaps = {}, {}
        for i in order:
            hr = halves[i][0]
            own_loads[i], big_swaps[i] = [], []
            for j in range(N_CHIPS):
                mine = big_refs[i].at[pl.ds(j * 2 * hr + c * hr, hr), :]
                theirs = big_refs[i].at[pl.ds(j * 2 * hr + (1 - c) * hr, hr), :]
                sem = local_sems.at[block_sem0 + N_CHIPS * late.index(i) + j]
                own_loads[i].append(pltpu.make_async_copy(mine, ga[i].at[j], sem))
                own_loads[i][j].start()
                big_swaps[i].append(remote(theirs, gb[i].at[j], sibling))
                big_swaps[i][j].start()
        early_loads = {}
        for e, i in enumerate(early):
            hr = halves[i][0]
            mine = big_refs[i].at[pl.ds(me * 2 * hr + c * hr, hr), :]
            early_loads[i] = [pltpu.make_async_copy(mine, own_e[i], local_sems.at[early_sem0 + 2 * e]),
                              pltpu.make_async_copy(land_refs[i], land_e[i], local_sems.at[early_sem0 + 2 * e + 1])]
            for cp in early_loads[i]:
                cp.start()
        small_sends = []
        for gi in range(n_g):
            small_swaps[gi].wait_recv()
            s_pair[gi][...] = s_own[gi][...] + s_sib[gi][...]
            small_sends.append([remote(s_pair[gi], s_chips[gi].at[k], (*chip, c)) for k, chip in enumerate(other_chips)])
            for cp in small_sends[gi]:
                cp.start()

        def pair_sum(i, j):
            return ga[i][j] + gb[i][j]

        big_sends = {}
        for i in order:
            for j in range(N_CHIPS):
                own_loads[i][j].wait()
                big_swaps[i][j].wait_recv()
            big_sends[i] = []
            for k, chip in enumerate(other_chips):
                send_b[i][k] = pair_sum(i, 2 * chip[0] + chip[1]).astype(BF16)
                big_sends[i].append(remote(send_b[i].at[k], recv_b[i].at[k], (*chip, c)))
                big_sends[i][k].start()
        last_swaps, keeps = {}, {}
        for i in early + order:
            hr = halves[i][0]
            if i in landed:
                for cp in early_loads[i]:
                    cp.wait()
                total = own_e[i][...]
                for k in range(len(_RELATIONS)):
                    total = total + land_e[i][k]
                pme[i][...] = total
            else:
                for k in range(3):
                    big_sends[i][k].wait_recv()
                pme[i][...] = ((pair_sum(i, me) + recv_b[i][0].astype(F32)) + recv_b[i][1].astype(F32)) + recv_b[i][2].astype(F32)
            mine = out_refs[i].at[pl.ds(c * hr, hr), :]
            keeps[i] = pltpu.make_async_copy(pme[i], mine, local_sems.at[i])
            keeps[i].start()
            last_swaps[i] = remote(pme[i], mine, sibling)
            last_swaps[i].start()

        for gi, (_, _, members) in enumerate(_SMALL_GROUPS):
            for k in range(3):
                small_sends[gi][k].wait_recv()
            total = None
            for j in range(N_CHIPS):
                rel = jnp.bitwise_xor(j, me)
                term = jnp.where(rel == 0, s_pair[gi][...], jnp.where(
                    rel == 2, s_chips[gi][0], jnp.where(rel == 1, s_chips[gi][1], s_chips[gi][2])))
                total = term if total is None else total + term
            s_sib[gi][...] = total
            for name, r0 in members:
                r, n = _small_shape(name)
                stage[name][...] = s_sib[gi][r0:r0 + r, 0:n]
        my_index = 4 * x + 2 * y + c
        for cp in landed_loads:
            cp.wait()
        for name in early_small:
            total = None
            for d in range(2 * N_CHIPS):
                rel = jnp.bitwise_xor(d, my_index)
                term = stage[name][...]
                for k in range(len(_RELATIONS)):
                    term = jnp.where(rel == k + 1, land_s[name][k], term)
                total = term if total is None else total + term
            stage[name][...] = total
        stores = [pltpu.make_async_copy(stage[name], small_out_refs[name], local_sems.at[small_sem0 + a])
                  for a, name in enumerate(names)]
        for cp in stores:
            cp.start()

        for i in range(n_t):
            last_swaps[i].wait_recv()
            keeps[i].wait()
        for cp in stores:
            cp.wait()
        groups = list(big_swaps.values()) + small_sends + list(big_sends.values())
        for cp in small_swaps + [cp for group in groups for cp in group] + list(last_swaps.values()):
            cp.wait_send()

    any_spec = pl.BlockSpec(memory_space=pl.ANY)
    small_shapes = [_sds(_small_shape(n)) for n in names]
    group_shapes = [shape for _, shape, _ in _SMALL_GROUPS]
    vmem = lambda which, dtype, lead=(): [pltpu.VMEM(lead + halves[i], dtype) for i in which]
    outs = _call(
        body, name="exchange_grads",
        in_specs=[any_spec] * (n_t + len(names) + len(early) + len(early_small)),
        out_specs=[any_spec] * (n_t + len(names)),
        out_shape=[_sds((b.shape[0] // N_CHIPS, b.shape[1])) for b in big] + small_shapes,
        scratch_shapes=(vmem(late, F32, (N_CHIPS,)) + vmem(late, F32, (N_CHIPS,)) + vmem(range(n_t), F32)
                        + vmem(late, BF16, (3,)) + vmem(late, BF16, (3,))
                        + vmem(early, F32) + vmem(early, F32, (len(_RELATIONS),))
                        + [pltpu.VMEM((len(_RELATIONS),) + _small_shape(n), F32) for n in early_small]
                        + [pltpu.VMEM(s, F32) for s in group_shapes] * 2 + [pltpu.VMEM((3,) + s, F32) for s in group_shapes]
                        + [pltpu.VMEM(s, F32) for s in group_shapes]
                        + [pltpu.VMEM(_small_shape(n), F32) for n in names]
                        + [pltpu.SemaphoreType.DMA((n_sems,)), pltpu.SemaphoreType.DMA((n_sems,)),
                           pltpu.SemaphoreType.DMA((landed_sem0 + len(early_small),))]),
        compiler_params=_params(48),
    )(*big, *[small[n] for n in names], *[landed[i] for i in early], *[landed_small[n] for n in early_small])
    return list(outs[:n_t]), dict(zip(names, outs[n_t:n_t + len(names)]))


def _adamw_update(w, g, m, v):
    m = ADAM_B1 * m + (1.0 - ADAM_B1) * g
    v = ADAM_B2 * v + (1.0 - ADAM_B2) * (g * g)
    m_hat = m / (1.0 - ADAM_B1 ** ADAM_STEP)
    v_hat = v / (1.0 - ADAM_B2 ** ADAM_STEP)
    return -ADAM_LR * (m_hat / (jnp.sqrt(v_hat) + ADAM_EPS) + ADAM_WD * w), m, v


def _adamw(w, g, m, v, grid, name):
    n_t = len(w)

    def body(*refs):
        ins, outs = refs[:4 * n_t], refs[4 * n_t:]
        for i in range(n_t):
            w_, g_, m_, v_ = [ins[a * n_t + i][...] for a in range(4)]
            vals = (g_,) + _adamw_update(w_, g_, m_, v_)
            for a in range(4):
                outs[a * n_t + i][...] = vals[a]

    specs = [pl.BlockSpec((a.shape[0] // grid, a.shape[1]), lambda i: (i, 0)) for a in w]
    shapes = [_sds(a.shape) for a in w]
    outs = _call(
        body, name=name, grid=(grid,), in_specs=specs * 4, out_specs=specs * 4, out_shape=shapes * 4,
        compiler_params=_params(40, ("arbitrary",)),
    )(*w, *g, *m, *v)
    return [outs[a * n_t:(a + 1) * n_t] for a in range(4)]


def kernel(x, p, pre_norm_g, w_in, ssm_lam_re, ssm_lam_im, ssm_log_step, ssm_b_re, ssm_b_im, ssm_c_re, ssm_c_im, ssm_d, ssm_w_glu, ssm_b_glu, attn_sinks, w_out, post_norm_g, pl_w_proj, pl_w_gate, pl_b_gate, loss_target, m_pre_norm_g, m_w_in, m_ssm_lam_re, m_ssm_lam_im, m_ssm_log_step, m_ssm_b_re, m_ssm_b_im, m_ssm_c_re, m_ssm_c_im, m_ssm_d, m_ssm_w_glu, m_ssm_b_glu, m_attn_sinks, m_w_out, m_post_norm_g, m_pl_w_proj, m_pl_w_gate, m_pl_b_gate, v_pre_norm_g, v_w_in, v_ssm_lam_re, v_ssm_lam_im, v_ssm_log_step, v_ssm_b_re, v_ssm_b_im, v_ssm_c_re, v_ssm_c_im, v_ssm_d, v_ssm_w_glu, v_ssm_b_glu, v_attn_sinks, v_w_out, v_post_norm_g, v_pl_w_proj, v_pl_w_gate, v_pl_b_gate):
    weights = dict(pre_norm_g=pre_norm_g, w_in=w_in, ssm_lam_re=ssm_lam_re, ssm_lam_im=ssm_lam_im,
                   ssm_log_step=ssm_log_step, ssm_b_re=ssm_b_re, ssm_b_im=ssm_b_im, ssm_c_re=ssm_c_re,
                   ssm_c_im=ssm_c_im, ssm_d=ssm_d, ssm_w_glu=ssm_w_glu, ssm_b_glu=ssm_b_glu, attn_sinks=attn_sinks,
                   w_out=w_out, post_norm_g=post_norm_g, pl_w_proj=pl_w_proj, pl_w_gate=pl_w_gate, pl_b_gate=pl_b_gate)
    m_in = dict(pre_norm_g=m_pre_norm_g, w_in=m_w_in, ssm_lam_re=m_ssm_lam_re, ssm_lam_im=m_ssm_lam_im,
                ssm_log_step=m_ssm_log_step, ssm_b_re=m_ssm_b_re, ssm_b_im=m_ssm_b_im, ssm_c_re=m_ssm_c_re,
                ssm_c_im=m_ssm_c_im, ssm_d=m_ssm_d, ssm_w_glu=m_ssm_w_glu, ssm_b_glu=m_ssm_b_glu,
                attn_sinks=m_attn_sinks, w_out=m_w_out, post_norm_g=m_post_norm_g, pl_w_proj=m_pl_w_proj,
                pl_w_gate=m_pl_w_gate, pl_b_gate=m_pl_b_gate)
    v_in = dict(pre_norm_g=v_pre_norm_g, w_in=v_w_in, ssm_lam_re=v_ssm_lam_re, ssm_lam_im=v_ssm_lam_im,
                ssm_log_step=v_ssm_log_step, ssm_b_re=v_ssm_b_re, ssm_b_im=v_ssm_b_im, ssm_c_re=v_ssm_c_re,
                ssm_c_im=v_ssm_c_im, ssm_d=v_ssm_d, ssm_w_glu=v_ssm_w_glu, ssm_b_glu=v_ssm_b_glu,
                attn_sinks=v_attn_sinks, w_out=v_w_out, post_norm_g=v_post_norm_g, pl_w_proj=v_pl_w_proj,
                pl_w_gate=v_pl_w_gate, pl_b_gate=v_pl_b_gate)

    def two_d(tree):
        return {k: _to_kernel_form(k, a) for k, a in tree.items()}

    w2, m2, v2 = two_d(weights), two_d(m_in), two_d(v_in)

    (w_in_full,), gathered = _gather_weights([w2["w_in"].astype(BF16)])
    rest = _gather_weights_beside([(w2[n] + gathered).astype(BF16) for n in _BIG[1:]])
    full = dict(zip(_BIG, [w_in_full] + rest))
    s5_params = tuple(w2[n] for n in ("ssm_lam_re", "ssm_lam_im", "ssm_log_step", "ssm_b_re", "ssm_b_im", "ssm_c_re",
                                      "ssm_c_im"))
    grad_x, loss, grads = _local_step(
        x, p, loss_target, w2["pre_norm_g"], full["w_in"], s5_params, w2["ssm_d"], full["ssm_w_glu"], w2["ssm_b_glu"],
        w2["attn_sinks"], full["w_out"], w2["post_norm_g"], full["pl_w_proj"], full["pl_w_gate"], w2["pl_b_gate"])

    sent_early = ("w_out", "pl_w_gate", "pl_w_proj")
    landed = dict(zip([_BIG.index(n) for n in sent_early], _scatter_beside([grads[n] for n in sent_early])))
    landed_small = dict(zip(_SMALL_EARLY, _broadcast_beside([grads[n] for n in _SMALL_EARLY])))
    g_big, g_small = _exchange_grads([grads[n] for n in _BIG], {**{n: grads[n] for n in _SMALL}, "loss": loss}, landed,
                                     landed_small)
    g_big = dict(zip(_BIG, g_big))
    total_loss = g_small.pop("loss")

    big_out = _adamw([w2[n] for n in _BIG], [g_big[n] for n in _BIG], [m2[n] for n in _BIG], [v2[n] for n in _BIG],
                     8, "adamw_matrices")
    small_names = tuple(_SMALL)
    small_out = _adamw([w2[n] for n in small_names], [g_small[n] for n in small_names], [m2[n] for n in small_names],
                       [v2[n] for n in small_names], 1, "adamw_small")

    results = [{**dict(zip(_BIG, big_part)), **dict(zip(small_names, small_part))}
               for big_part, small_part in zip(big_out, small_out)]
    flat = [_from_kernel_form(name, r[name], weights[name].shape) for r in results for name in _WEIGHT_ORDER]
    return (total_loss.reshape(()), grad_x, *flat)
```

```python
import math

import jax
import jax.numpy as jnp
from jax import lax
from jax.experimental import pallas as pl
from jax.experimental.pallas import tpu as pltpu
from jax.experimental.pallas import tpu_sc as plsc

F32 = jnp.float32
BF16 = jnp.bfloat16

D_MODEL = 1024
D_SSM = 512
D_ATTN = 512
SSM_GROUPS = 32
SSM_GROUP_CH = 16
SSM_STATE = 64
SSM_LANES = SSM_GROUPS * SSM_STATE
HEAD_DIM = 64
N_HEADS = 8
KV_HEADS = 2
Q_PER_KV = 4
WINDOW = 128
BLOCK = 128
D_PLE = 256
D_IN = 2304
EPS = 1e-6
ATTN_SCALE = 1.0 / math.sqrt(HEAD_DIM)

ADAM_LR = 0.001
ADAM_B1 = 0.9
ADAM_B2 = 0.999
ADAM_EPS = 1e-08
ADAM_WD = 0.01
ADAM_STEP = 10

N_CHIPS = 4
LANES = 128
SCAN_CHUNKS = 8
SCAN_TILE_STEPS = 32
SCAN_LANE_CHUNK = 512
MIB = 2 ** 20
MESH = pl.DeviceIdType.MESH


def _dot(a, b):
    return jnp.dot(a, b, preferred_element_type=F32)


def _dot_nt(a, b):
    return lax.dot_general(a, b, (((1,), (1,)), ((), ())), preferred_element_type=F32)


def _dot_tn(a, b):
    return lax.dot_general(a, b, (((0,), (0,)), ((), ())), preferred_element_type=F32)


def _params(vmem_mib, semantics=None):
    kw = dict(vmem_limit_bytes=vmem_mib * MIB)
    if semantics is not None:
        kw["dimension_semantics"] = semantics
    return pltpu.CompilerParams(**kw)


def _full(shape):
    nd = len(shape)
    return pl.BlockSpec(shape, lambda *_: (0,) * nd, pipeline_mode=pl.Buffered(1))


def _rows(tm, width):
    return pl.BlockSpec((tm, width), lambda i: (i, 0))


def _sds(shape, dtype=F32):
    return pltpu.HBM(shape, dtype)


def _call(body, **kw):
    fn = pl.pallas_call(body, **kw)
    return lambda *args: fn(*[pltpu.with_memory_space_constraint(a, pltpu.HBM) for a in args])


def _silu(z):
    return z * jax.nn.sigmoid(z)


def _in_proj(x2d, g1, w_in_t, n_seq, seq):
    rows = x2d.shape[0]
    tm = 512
    slab, steps, _, _ = _scan_geometry(n_seq, seq)

    def body(x_ref, g_ref, w_ref, *out_refs):
        u_parts, (zs_ref, q_ref, k_ref, v_ref, za_ref) = out_refs[:_SCAN_PARTS], out_refs[_SCAN_PARTS:]
        x = x_ref[...]
        r = lax.rsqrt(jnp.mean(x * x, axis=-1, keepdims=True) + EPS)
        hn = (x * r * g_ref[...]).astype(BF16)

        def proj(a, b):
            return _dot_nt(hn, w_ref[a:b, :])

        _store_chunks(u_parts, pl.program_id(0) * (tm // steps), proj(0, 512), steps, slab)
        zs_ref[...] = proj(512, 1024)
        q_ref[...] = proj(1024, 1536).astype(BF16)
        k_ref[...] = proj(1536, 1664).astype(BF16)
        v_ref[...] = proj(1664, 1792).astype(BF16)
        za_ref[...] = proj(1792, 2304)

    *u_parts, zs, q, k, v, za = _call(
        body, name="in_proj", grid=(rows // tm,),
        in_specs=[_rows(tm, D_MODEL), _full((1, D_MODEL)), _full((D_IN, D_MODEL))],
        out_specs=_whole_parts(rows) + [_rows(tm, 512), _rows(tm, 512), _rows(tm, 128), _rows(tm, 128), _rows(tm, 512)],
        out_shape=_part_shapes(rows) + [_sds((rows, 512)), _sds((rows, 512), BF16), _sds((rows, 128), BF16),
                                        _sds((rows, 128), BF16), _sds((rows, 512))],
        compiler_params=_params(48, ("arbitrary",)),
    )(x2d, g1, w_in_t)
    return u_parts, zs, q, k, v, za


def _in_proj_bwd(x2d, dh1, g1, w_in_t, du_parts, dzs, dq, dk, dv, dza, n_seq, seq):
    rows = x2d.shape[0]
    tm = 256
    slab, steps, _, _ = _scan_geometry(n_seq, seq)
    pieces = ((0, 512), (512, 1024), (1024, 1536), (1536, 1664), (1664, 1792), (1792, 2304))

    def body(x_ref, dh1_ref, g_ref, w_ref, *refs):
        du_parts, (dzs_ref, dq_ref, dk_ref, dv_ref, dza_ref, gx_ref, dw_ref, dg_ref) = refs[:_SCAN_PARTS], refs[_SCAN_PARTS:]

        @pl.when(pl.program_id(0) == 0)
        def _():
            dw_ref[...] = jnp.zeros_like(dw_ref)
            dg_ref[...] = jnp.zeros_like(dg_ref)

        x = x_ref[...]
        g = g_ref[...]
        r = lax.rsqrt(jnp.mean(x * x, axis=-1, keepdims=True) + EPS)
        xr = x * r
        hn = (xr * g).astype(BF16)
        dhn = jnp.zeros((tm, D_MODEL), F32)
        du = _load_chunks(du_parts, pl.program_id(0) * (tm // steps), tm // steps, steps, slab)
        for (a, b), piece in zip(pieces, (du, dzs_ref[...], dq_ref[...], dk_ref[...], dv_ref[...], dza_ref[...])):
            piece = piece.astype(BF16)
            dhn = dhn + _dot(piece, w_ref[a:b, :])
            dw_ref[a:b, :] += _dot_tn(piece, hn)
        dg_ref[...] += jnp.sum(dhn * xr, axis=0, keepdims=True)
        a_ = dhn * g
        gx_ref[...] = dh1_ref[...] + r * a_ - xr * (r * jnp.mean(a_ * xr, axis=-1, keepdims=True))

    return _call(
        body, name="in_proj_bwd", grid=(rows // tm,),
        in_specs=[_rows(tm, D_MODEL), _rows(tm, D_MODEL), _full((1, D_MODEL)), _full((D_IN, D_MODEL))]
        + _whole_parts(rows) + [_rows(tm, 512), _rows(tm, 512), _rows(tm, 128), _rows(tm, 128), _rows(tm, 512)],
        out_specs=[_rows(tm, D_MODEL), _full((D_IN, D_MODEL)), _full((1, D_MODEL))],
        out_shape=[_sds((rows, D_MODEL)), _sds((D_IN, D_MODEL)), _sds((1, D_MODEL))],
        compiler_params=_params(52, ("arbitrary",)),
    )(x2d, dh1, g1, w_in_t, *du_parts, dzs, dq, dk, dv, dza)


def _iota(shape, axis):
    return lax.broadcasted_iota(jnp.int32, shape, axis)


def _exact_dot(a, b):
    return jnp.dot(a, b, precision=lax.Precision.HIGHEST, preferred_element_type=F32)


_HALF_GROUPS = SSM_GROUPS // 2
_N_SHIFT = SSM_STATE.bit_length() - 1
_P_SHIFT = SSM_GROUP_CH.bit_length() - 1


def _s5_operands(lam_re, lam_im, log_step, b_re, b_im, c_re, c_im):
    g, n, p = SSM_GROUPS, SSM_STATE, SSM_GROUP_CH
    gn, gp, hn_, hp = g * n, g * p, _HALF_GROUPS * n, _HALF_GROUPS * p
    eye_g = _iota((g, g), 0) == _iota((g, g), 1)
    step = jnp.sum(jnp.where(eye_g, jnp.exp(log_step), 0.0), axis=1, keepdims=True)
    a_re = lam_re * step
    a_im = lam_im * step
    mag = jnp.exp(a_re)
    lbar_re = mag * jnp.cos(a_im)
    lbar_im = mag * jnp.sin(a_im)
    n_re = lbar_re - 1.0
    den = lam_re * lam_re + lam_im * lam_im
    f_re = (n_re * lam_re + lbar_im * lam_im) / den
    f_im = (lbar_im * lam_re - n_re * lam_im) / den

    spread_n = (_iota((n, gn), 0) == (_iota((n, gn), 1) & (n - 1))).astype(F32)
    own_g = _iota((g, gn), 0) == (_iota((g, gn), 1) >> _N_SHIFT)

    def to_row(a):
        return jnp.sum(jnp.where(own_g, _exact_dot(a, spread_n), 0.0), axis=0, keepdims=True)

    per_group = ((_iota((gp, g), 0) >> _P_SHIFT) == _iota((gp, g), 1)).astype(F32)
    fx_re, fx_im = _exact_dot(per_group, f_re), _exact_dot(per_group, f_im)
    bbar_re = fx_re * b_re - fx_im * b_im
    bbar_im = fx_re * b_im + fx_im * b_re

    tile_n = (_iota((n, hn_), 0) == (_iota((n, hn_), 1) & (n - 1))).astype(F32)
    same_group = (_iota((hp, hn_), 0) >> _P_SHIFT) == (_iota((hp, hn_), 1) >> _N_SHIFT)

    def embed(a, hf):
        return jnp.where(same_group, _exact_dot(a[hf * hp:(hf + 1) * hp], tile_n), 0.0)

    return (to_row(lbar_re), to_row(lbar_im), embed(bbar_re, 0), embed(bbar_re, 1), embed(bbar_im, 0),
            embed(bbar_im, 1), embed(c_re, 0), embed(c_re, 1), embed(c_im, 0), embed(c_im, 1))


_S5_PARAM_SHAPES = ((SSM_GROUPS, SSM_STATE), (SSM_GROUPS, SSM_STATE), (1, SSM_GROUPS),
                    (D_SSM, SSM_STATE), (D_SSM, SSM_STATE), (D_SSM, SSM_STATE), (D_SSM, SSM_STATE))
_CM_SHAPE = (2, _HALF_GROUPS * SSM_GROUP_CH, _HALF_GROUPS * SSM_STATE)
_S5_OPERAND_SHAPES = ((1, SSM_LANES), (1, SSM_LANES), _CM_SHAPE, _CM_SHAPE, _CM_SHAPE, _CM_SHAPE)


def _s5_params_fwd(*params):
    def body(*refs):
        ins, (lre_ref, lim_ref, btre_ref, btim_ref, cmre_ref, cmim_ref) = refs[:7], refs[7:]
        vals = _s5_operands(*[r[...] for r in ins])
        lre_ref[...] = vals[0]
        lim_ref[...] = vals[1]
        for ref, pair in zip((btre_ref, btim_ref, cmre_ref, cmim_ref), (vals[2:4], vals[4:6], vals[6:8], vals[8:10])):
            ref[0] = pair[0].astype(BF16)
            ref[1] = pair[1].astype(BF16)

    dtypes = (F32, F32, BF16, BF16, BF16, BF16)
    return _call(
        body, name="s5_params_fwd",
        in_specs=[_full(s) for s in _S5_PARAM_SHAPES], out_specs=[_full(s) for s in _S5_OPERAND_SHAPES],
        out_shape=[_sds(s, d) for s, d in zip(_S5_OPERAND_SHAPES, dtypes)], compiler_params=_params(32),
    )(*params)


def _s5_params_bwd(params, cotangents):
    def body(*refs):
        ins, (dlre, dlim, dbtre, dbtim, dcmre, dcmim), outs = refs[:7], refs[7:13], refs[13:]
        _, vjp = jax.vjp(_s5_operands, *[r[...] for r in ins])
        cts = (dlre[...], dlim[...], dbtre[0], dbtre[1], dbtim[0], dbtim[1], dcmre[0], dcmre[1], dcmim[0], dcmim[1])
        for ref, val in zip(outs, vjp(cts)):
            ref[...] = val

    return _call(
        body, name="s5_params_bwd",
        in_specs=[_full(s) for s in _S5_PARAM_SHAPES + _S5_OPERAND_SHAPES],
        out_specs=[_full(s) for s in _S5_PARAM_SHAPES],
        out_shape=[_sds(s) for s in _S5_PARAM_SHAPES], compiler_params=_params(48),
    )(*params, *cotangents)


def _scan_geometry(n_seq, seq):
    slab = n_seq * SCAN_CHUNKS
    steps = seq // SCAN_CHUNKS
    tile_rows = slab * SCAN_TILE_STEPS
    n_tiles = steps // SCAN_TILE_STEPS
    return slab, steps, tile_rows, n_tiles


_SCAN_PARTS = D_SSM // LANES


def _whole_parts(rows):
    return [_full((rows, LANES))] * _SCAN_PARTS


def _part_shapes(rows):
    return [_sds((rows, LANES))] * _SCAN_PARTS


def _load_chunks(parts, first_chunk, n_chunks, steps, slab):
    return jnp.concatenate([
        jnp.concatenate([ref[pl.ds(first_chunk + q, steps, stride=slab), :] for ref in parts], axis=1)
        for q in range(n_chunks)], axis=0)


def _store_chunks(parts, first_chunk, value, steps, slab):
    for q in range(value.shape[0] // steps):
        for j, ref in enumerate(parts):
            ref[pl.ds(first_chunk + q, steps, stride=slab), :] = value[q * steps:(q + 1) * steps,
                                                                     j * LANES:(j + 1) * LANES]


def _join_parts(parts):
    return jnp.concatenate([ref[...] for ref in parts], axis=1)


def _split_parts(parts, value):
    for j, ref in enumerate(parts):
        ref[...] = value[:, j * LANES:(j + 1) * LANES]


def _complex_power(re, im, n):
    out = None
    while n:
        if n & 1:
            out = (re, im) if out is None else (out[0] * re - out[1] * im, out[0] * im + out[1] * re)
        n >>= 1
        if n:
            re, im = re * re - im * im, 2.0 * re * im
    return out


def _chunk_carry(sum_re, sum_im, carry_re, carry_im, a_re, a_im, n_seq, reverse):
    carry_re[...] = jnp.zeros_like(carry_re)
    carry_im[...] = jnp.zeros_like(carry_im)
    for s in range(n_seq):
        order = range(SCAN_CHUNKS - 2, -1, -1) if reverse else range(1, SCAN_CHUNKS)
        for c in order:
            r = s * SCAN_CHUNKS + c
            p = r + 1 if reverse else r - 1
            p_re, p_im = carry_re[p:p + 1, :], carry_im[p:p + 1, :]
            carry_re[r:r + 1, :] = a_re * p_re - a_im * p_im + sum_re[p:p + 1, :]
            carry_im[r:r + 1, :] = a_re * p_im + a_im * p_re + sum_im[p:p + 1, :]


def _s5_scan_fwd(u_parts, bt_re, bt_im, cm_re, cm_im, lbar_re, lbar_im, d_row, n_seq, seq):
    slab, steps, tile_rows, n_tiles = _scan_geometry(n_seq, seq)
    rows = u_parts[0].shape[0]

    def body(*refs):
        u_refs, refs = refs[:_SCAN_PARTS], refs[_SCAN_PARTS:]
        (bre_ref, bim_ref, cre_ref, cim_ref, lre_ref, lim_ref, d_ref), refs = refs[:7], refs[7:]
        y_refs, (hre_ref, him_ref, st_re, st_im, h0_re, h0_im, buf_re, buf_im) = refs[:_SCAN_PARTS], refs[_SCAN_PARTS:]
        second = pl.program_id(0) == 1
        i = pl.program_id(1)

        @pl.when(jnp.logical_and(i == 0, jnp.logical_not(second)))
        def _():
            st_re[...] = jnp.zeros_like(st_re)
            st_im[...] = jnp.zeros_like(st_im)

        u = _join_parts(u_refs)
        ub = u.astype(BF16)
        for hf in range(2):
            cols = slice(hf * 1024, (hf + 1) * 1024)
            buf_re[:, cols] = _dot(ub[:, hf * 256:(hf + 1) * 256], bre_ref[hf])
            buf_im[:, cols] = _dot(ub[:, hf * 256:(hf + 1) * 256], bim_ref[hf])

        for lc in range(SSM_LANES // SCAN_LANE_CHUNK):
            cols = slice(lc * SCAN_LANE_CHUNK, (lc + 1) * SCAN_LANE_CHUNK)
            l_re = jnp.broadcast_to(lre_ref[:, cols], (slab, SCAN_LANE_CHUNK))
            l_im = jnp.broadcast_to(lim_ref[:, cols], (slab, SCAN_LANE_CHUNK))

            def scan_tile(keep_states):
                def step(t, carry):
                    s_re, s_im = carry
                    r0 = pl.multiple_of(t * slab, slab)
                    n_re = l_re * s_re - l_im * s_im + buf_re[pl.ds(r0, slab), cols]
                    n_im = l_re * s_im + l_im * s_re + buf_im[pl.ds(r0, slab), cols]
                    if keep_states:
                        buf_re[pl.ds(r0, slab), cols] = n_re
                        buf_im[pl.ds(r0, slab), cols] = n_im
                    return n_re, n_im

                s_re, s_im = lax.fori_loop(0, SCAN_TILE_STEPS, step, (st_re[:, cols], st_im[:, cols]), unroll=True)
                st_re[:, cols] = s_re
                st_im[:, cols] = s_im

            pl.when(jnp.logical_not(second))(lambda: scan_tile(False))
            pl.when(second)(lambda: scan_tile(True))

        @pl.when(jnp.logical_and(i == n_tiles - 1, jnp.logical_not(second)))
        def _():
            a_re, a_im = _complex_power(lre_ref[...], lim_ref[...], steps)
            _chunk_carry(st_re, st_im, h0_re, h0_im, a_re, a_im, n_seq, reverse=False)
            st_re[...] = h0_re[...]
            st_im[...] = h0_im[...]

        @pl.when(second)
        def _():
            h_re = buf_re[...].astype(BF16)
            h_im = buf_im[...].astype(BF16)
            hre_ref[...] = h_re
            him_ref[...] = h_im
            for hf in range(2):
                cols = slice(hf * 1024, (hf + 1) * 1024)
                ycols = slice(hf * 256, (hf + 1) * 256)
                y_half = (_dot_nt(h_re[:, cols], cre_ref[hf]) - _dot_nt(h_im[:, cols], cim_ref[hf])
                          + d_ref[:, ycols] * u[:, ycols])
                _split_parts(y_refs[2 * hf:2 * hf + 2], y_half)

    tile = lambda w: pl.BlockSpec((tile_rows, w), lambda p, i: (i, 0))
    out_tile = lambda w: pl.BlockSpec((tile_rows, w), lambda p, i: (i * p, 0))
    cm = _full(_CM_SHAPE)
    outs = _call(
        body, name="s5_scan_fwd", grid=(2, n_tiles),
        in_specs=[tile(LANES)] * _SCAN_PARTS + [cm, cm, cm, cm, _full((1, SSM_LANES)), _full((1, SSM_LANES)),
                                                _full((1, 512))],
        out_specs=[out_tile(LANES)] * _SCAN_PARTS + [out_tile(SSM_LANES), out_tile(SSM_LANES)],
        out_shape=_part_shapes(rows) + [_sds((rows, SSM_LANES), BF16), _sds((rows, SSM_LANES), BF16)],
        scratch_shapes=[pltpu.VMEM((slab, SSM_LANES), F32)] * 4 + [pltpu.VMEM((tile_rows, SSM_LANES), F32)] * 2,
        compiler_params=_params(40, ("arbitrary", "arbitrary")),
    )(*u_parts, bt_re, bt_im, cm_re, cm_im, lbar_re, lbar_im, d_row)
    return outs[:_SCAN_PARTS], outs[_SCAN_PARTS], outs[_SCAN_PARTS + 1]


def _s5_scan_bwd(dy_parts, u_parts, h_re, h_im, bt_re, bt_im, cm_re, cm_im, lbar_re, lbar_im, d_row, n_seq, seq):
    slab, steps, tile_rows, n_tiles = _scan_geometry(n_seq, seq)
    rows = u_parts[0].shape[0]

    def body(*refs):
        dy_refs, u_refs, refs = refs[:_SCAN_PARTS], refs[_SCAN_PARTS:2 * _SCAN_PARTS], refs[2 * _SCAN_PARTS:]
        (hre_ref, him_ref, bre_ref, bim_ref, cre_ref, cim_ref, lre_ref, lim_ref, d_ref), refs = refs[:9], refs[9:]
        du_refs, refs = refs[:_SCAN_PARTS], refs[_SCAN_PARTS:]
        (dbre_ref, dbim_ref, dcre_ref, dcim_ref, dlre_ref, dlim_ref, dd_ref,
         st_re, st_im, g0_re, g0_im, acc_re, acc_im, buf_re, buf_im) = refs
        second = pl.program_id(0) == 1
        i = pl.program_id(1)

        @pl.when(jnp.logical_and(i == 0, jnp.logical_not(second)))
        def _():
            st_re[...] = jnp.zeros_like(st_re)
            st_im[...] = jnp.zeros_like(st_im)
            acc_re[...] = jnp.zeros_like(acc_re)
            acc_im[...] = jnp.zeros_like(acc_im)
            for ref in (dbre_ref, dbim_ref, dcre_ref, dcim_ref, dd_ref):
                ref[...] = jnp.zeros_like(ref)

        dy = _join_parts(dy_refs)
        dyb = dy.astype(BF16)
        for hf in range(2):
            cols = slice(hf * 1024, (hf + 1) * 1024)
            buf_re[:, cols] = _dot(dyb[:, hf * 256:(hf + 1) * 256], cre_ref[hf])
            buf_im[:, cols] = -_dot(dyb[:, hf * 256:(hf + 1) * 256], cim_ref[hf])

        for lc in range(SSM_LANES // SCAN_LANE_CHUNK):
            cols = slice(lc * SCAN_LANE_CHUNK, (lc + 1) * SCAN_LANE_CHUNK)
            l_re = jnp.broadcast_to(lre_ref[:, cols], (slab, SCAN_LANE_CHUNK))
            l_im = jnp.broadcast_to(lim_ref[:, cols], (slab, SCAN_LANE_CHUNK))

            def advance(r0, s_re, s_im):
                n_re = l_re * s_re + l_im * s_im + buf_re[pl.ds(r0, slab), cols]
                n_im = l_re * s_im - l_im * s_re + buf_im[pl.ds(r0, slab), cols]
                buf_re[pl.ds(r0, slab), cols] = n_re
                buf_im[pl.ds(r0, slab), cols] = n_im
                return n_re, n_im

            def row0(k):
                return pl.multiple_of((SCAN_TILE_STEPS - 1 - k) * slab, slab)

            @pl.when(jnp.logical_not(second))
            def _():
                s_re, s_im = lax.fori_loop(0, SCAN_TILE_STEPS, lambda k, s: advance(row0(k), *s),
                                           (st_re[:, cols], st_im[:, cols]), unroll=True)
                st_re[:, cols] = s_re
                st_im[:, cols] = s_im

            @pl.when(second)
            def _():
                def step(k, carry):
                    s_re, s_im, a_re, a_im = carry
                    r0 = row0(k)
                    hr = hre_ref[pl.ds(r0, slab), cols].astype(F32)
                    hi = him_ref[pl.ds(r0, slab), cols].astype(F32)
                    a_re = a_re + s_re * hr + s_im * hi
                    a_im = a_im + s_im * hr - s_re * hi
                    return advance(r0, s_re, s_im) + (a_re, a_im)

                zero = jnp.zeros((slab, SCAN_LANE_CHUNK), F32)
                s_re, s_im, a_re, a_im = lax.fori_loop(
                    0, SCAN_TILE_STEPS, step, (st_re[:, cols], st_im[:, cols], zero, zero), unroll=True)
                st_re[:, cols] = s_re
                st_im[:, cols] = s_im
                acc_re[:, cols] += a_re
                acc_im[:, cols] += a_im

        @pl.when(jnp.logical_and(i == n_tiles - 1, jnp.logical_not(second)))
        def _():
            p_re, p_im = _complex_power(lre_ref[...], lim_ref[...], steps)
            _chunk_carry(st_re, st_im, g0_re, g0_im, p_re, -p_im, n_seq, reverse=True)
            st_re[...] = g0_re[...]
            st_im[...] = g0_im[...]

        @pl.when(second)
        def _():
            u = _join_parts(u_refs)
            ub = u.astype(BF16)
            g_re = buf_re[...].astype(BF16)
            g_im = buf_im[...].astype(BF16)
            dd_ref[...] += jnp.sum(dy * u, axis=0, keepdims=True)
            for hf in range(2):
                cols = slice(hf * 1024, (hf + 1) * 1024)
                ycols = slice(hf * 256, (hf + 1) * 256)
                du_half = (_dot_nt(g_re[:, cols], bre_ref[hf]) + _dot_nt(g_im[:, cols], bim_ref[hf])
                           + d_ref[:, ycols] * dy[:, ycols])
                _split_parts(du_refs[2 * hf:2 * hf + 2], du_half)
                dbre_ref[hf] += _dot_tn(ub[:, ycols], g_re[:, cols])
                dbim_ref[hf] += _dot_tn(ub[:, ycols], g_im[:, cols])
                dcre_ref[hf] += _dot_tn(dyb[:, ycols], hre_ref[:, cols])
                dcim_ref[hf] -= _dot_tn(dyb[:, ycols], him_ref[:, cols])

        @pl.when(jnp.logical_and(i == n_tiles - 1, second))
        def _():
            dlre_ref[...] = jnp.sum(acc_re[...], axis=0, keepdims=True)
            dlim_ref[...] = jnp.sum(acc_im[...], axis=0, keepdims=True)

    tile = lambda w: pl.BlockSpec((tile_rows, w), lambda p, i: (n_tiles - 1 - i, 0))
    second_tile = lambda w: pl.BlockSpec((tile_rows, w), lambda p, i: (n_tiles - 1 - i * p, 0))
    cm = _full(_CM_SHAPE)
    row = _full((1, SSM_LANES))
    outs = _call(
        body, name="s5_scan_bwd", grid=(2, n_tiles),
        in_specs=[tile(LANES)] * _SCAN_PARTS + [second_tile(LANES)] * _SCAN_PARTS
        + [second_tile(SSM_LANES), second_tile(SSM_LANES), cm, cm, cm, cm, row, row, _full((1, 512))],
        out_specs=[second_tile(LANES)] * _SCAN_PARTS + [cm, cm, cm, cm, row, row, _full((1, 512))],
        out_shape=(_part_shapes(rows) + [_sds(_CM_SHAPE)] * 4 + [_sds((1, SSM_LANES))] * 2 + [_sds((1, 512))]),
        scratch_shapes=[pltpu.VMEM((slab, SSM_LANES), F32)] * 6 + [pltpu.VMEM((tile_rows, SSM_LANES), F32)] * 2,
        compiler_params=_params(48, ("arbitrary", "arbitrary")),
    )(*dy_parts, *u_parts, h_re, h_im, bt_re, bt_im, cm_re, cm_im, lbar_re, lbar_im, d_row)
    return (outs[:_SCAN_PARTS],) + tuple(outs[_SCAN_PARTS:])


def _glu_gate(gl, a, zs):
    return gl * jax.nn.sigmoid(a) * _silu(zs)


def _glu_fwd(y_parts, zs, w_glu, b_glu, n_seq, seq):
    rows = zs.shape[0]
    tm = 512
    slab, steps, _, _ = _scan_geometry(n_seq, seq)

    def body(*refs):
        y_refs, (zs_ref, w_ref, b_ref, o_ref) = refs[:_SCAN_PARTS], refs[_SCAN_PARTS:]
        y = _load_chunks(y_refs, pl.program_id(0) * (tm // steps), tm // steps, steps, slab)
        gl = jax.nn.gelu(y)
        a = _dot(gl.astype(BF16), w_ref[...]) + b_ref[...]
        o_ref[...] = _glu_gate(gl, a, zs_ref[...]).astype(BF16)

    return _call(
        body, name="glu_fwd", grid=(rows // tm,),
        in_specs=_whole_parts(rows) + [_rows(tm, 512), _full((512, 512)), _full((1, 512))],
        out_specs=_rows(tm, 512), out_shape=_sds((rows, 512), BF16),
        compiler_params=_params(32, ("arbitrary",)),
    )(*y_parts, zs, w_glu, b_glu)


def _glu_bwd(y_parts, zs, d_out, w_glu, b_glu, n_seq, seq):
    rows = zs.shape[0]
    tm = 512
    slab, steps, _, _ = _scan_geometry(n_seq, seq)

    def body(*refs):
        y_refs, (zs_ref, d_ref, w_ref, b_ref), refs = refs[:_SCAN_PARTS], refs[_SCAN_PARTS:_SCAN_PARTS + 4], refs[_SCAN_PARTS + 4:]
        dy_refs, (dzs_ref, dw_ref, db_ref) = refs[:_SCAN_PARTS], refs[_SCAN_PARTS:]
        first_chunk = pl.program_id(0) * (tm // steps)

        @pl.when(pl.program_id(0) == 0)
        def _():
            dw_ref[...] = jnp.zeros_like(dw_ref)
            db_ref[...] = jnp.zeros_like(db_ref)

        gl, gelu_vjp = jax.vjp(jax.nn.gelu, _load_chunks(y_refs, first_chunk, tm // steps, steps, slab))
        glb = gl.astype(BF16)
        a = _dot(glb, w_ref[...]) + b_ref[...]
        _, gate_vjp = jax.vjp(_glu_gate, gl, a, zs_ref[...])
        d_gl, d_a, d_zs = gate_vjp(d_ref[...])
        dab = d_a.astype(BF16)
        d_gl = d_gl + _dot_nt(dab, w_ref[...])
        _store_chunks(dy_refs, first_chunk, gelu_vjp(d_gl)[0], steps, slab)
        dzs_ref[...] = d_zs.astype(BF16)
        dw_ref[...] += _dot_tn(glb, dab)
        db_ref[...] += jnp.sum(d_a, axis=0, keepdims=True)

    *dy_parts, dzs, dw, db = _call(
        body, name="glu_bwd", grid=(rows // tm,),
        in_specs=_whole_parts(rows) + [_rows(tm, 512), _rows(tm, 512), _full((512, 512)), _full((1, 512))],
        out_specs=_whole_parts(rows) + [_rows(tm, 512), _full((512, 512)), _full((1, 512))],
        out_shape=_part_shapes(rows) + [_sds((rows, 512), BF16), _sds((512, 512)), _sds((1, 512))],
        compiler_params=_params(40, ("arbitrary",)),
    )(*y_parts, zs, d_out, w_glu, b_glu)
    return dy_parts, dzs, dw, db


_GROUP_ROWS = Q_PER_KV * BLOCK
_BLOCK_SHIFT = BLOCK.bit_length() - 1


def _attn_bias(j):
    row = _iota((_GROUP_ROWS, BLOCK), 0)
    dist_cur = (row & (BLOCK - 1)) - _iota((_GROUP_ROWS, BLOCK), 1)
    dist_prev = dist_cur + BLOCK
    head = row >> _BLOCK_SHIFT
    slope = jnp.zeros((_GROUP_ROWS, BLOCK), F32)
    for g in range(Q_PER_KV):
        slope = jnp.where(head == g, 2.0 ** (-(j * Q_PER_KV + g + 1)), slope)
    bias_cur = jnp.where(dist_cur >= 0, -slope * dist_cur.astype(F32), -jnp.inf)
    bias_prev = jnp.where(dist_prev < WINDOW, -slope * dist_prev.astype(F32), -jnp.inf)
    return bias_cur, bias_prev


_ATTN_BIAS_SCRATCH = pltpu.VMEM((KV_HEADS, 2, _GROUP_ROWS, BLOCK), F32)


def _fill_attn_bias(bias_ref):
    @pl.when(jnp.logical_and(pl.program_id(0) == 0, pl.program_id(1) == 0))
    def _():
        for j in range(KV_HEADS):
            bias_ref[j, 0], bias_ref[j, 1] = _attn_bias(j)


def _stack_heads(x, j):
    heads = range(j * Q_PER_KV, (j + 1) * Q_PER_KV)
    return jnp.concatenate([x[:, h * HEAD_DIM:(h + 1) * HEAD_DIM] for h in heads], axis=0)


def _stack_columns(x, j):
    heads = range(j * Q_PER_KV, (j + 1) * Q_PER_KV)
    return jnp.concatenate([jnp.broadcast_to(x[:, h:h + 1], (BLOCK, 1)) for h in heads], axis=0)


def _attn_fwd(q, k, v, za, sinks, n_seq, seq):
    nb = seq // BLOCK
    rows = q.shape[0]

    def body(q_ref, kc_ref, kp_ref, vc_ref, vp_ref, za_ref, sk_ref, o_ref, ao_ref, lse_ref, bias_ref):
        _fill_attn_bias(bias_ref)
        has_prev = pl.program_id(1) > 0
        q_all = q_ref[...]
        for j in range(KV_HEADS):
            js = slice(j * HEAD_DIM, (j + 1) * HEAD_DIM)
            bias_c, bias_p = bias_ref[j, 0], bias_ref[j, 1]
            q4 = _stack_heads(q_all, j)
            sc = _dot_nt(q4, kc_ref[:, js]) * ATTN_SCALE + bias_c
            sp = _dot_nt(q4, kp_ref[:, js]) * ATTN_SCALE + jnp.where(has_prev, bias_p, -jnp.inf)
            sink = _stack_columns(sk_ref[...], j)
            m = jnp.maximum(jnp.maximum(jnp.max(sc, axis=-1, keepdims=True), jnp.max(sp, axis=-1, keepdims=True)), sink)
            ec = jnp.exp(sc - m)
            ep = jnp.exp(sp - m)
            den = jnp.sum(ec, axis=-1, keepdims=True) + jnp.sum(ep, axis=-1, keepdims=True) + jnp.exp(sink - m)
            inv = 1.0 / den
            o4 = _dot((ec * inv).astype(BF16), vc_ref[:, js]) + _dot((ep * inv).astype(BF16), vp_ref[:, js])
            lse4 = m + jnp.log(den)
            for g in range(Q_PER_KV):
                h = j * Q_PER_KV + g
                o_ref[:, h * HEAD_DIM:(h + 1) * HEAD_DIM] = o4[g * BLOCK:(g + 1) * BLOCK]
                lse_ref[:, h:h + 1] = lse4[g * BLOCK:(g + 1) * BLOCK]
        ao_ref[...] = (o_ref[...] * _silu(za_ref[...])).astype(BF16)

    cur = lambda w: pl.BlockSpec((BLOCK, w), lambda b, n: (b * nb + n, 0))
    prev = lambda w: pl.BlockSpec((BLOCK, w), lambda b, n: (b * nb + jnp.maximum(n - 1, 0), 0))
    return _call(
        body, name="attn_fwd", grid=(n_seq, nb),
        in_specs=[cur(512), cur(128), prev(128), cur(128), prev(128), cur(512), _full((1, N_HEADS))],
        out_specs=[cur(512), cur(512), cur(N_HEADS)],
        out_shape=[_sds((rows, 512)), _sds((rows, 512), BF16), _sds((rows, N_HEADS))],
        scratch_shapes=[_ATTN_BIAS_SCRATCH], compiler_params=_params(32, ("arbitrary", "arbitrary")),
    )(q, k, k, v, v, za, sinks)


def _attn_bwd(q, k, v, za, o, lse, d_ao, sinks, n_seq, seq):
    nb = seq // BLOCK
    rows = q.shape[0]

    def body(q_ref, q2_ref, kc_ref, kp_ref, vc_ref, vp_ref, za_ref, za2_ref, o_ref, lse_ref, lse2_ref,
             d_ref, d2_ref, sk_ref, dq_ref, dk_ref, dv_ref, dza_ref, dsk_ref, bias_ref, delta_ref):
        n = nb - 1 - pl.program_id(1)
        _fill_attn_bias(bias_ref)

        @pl.when(jnp.logical_and(pl.program_id(0) == 0, pl.program_id(1) == 0))
        def _():
            dsk_ref[...] = jnp.zeros_like(dsk_ref)
            delta_ref[...] = jnp.zeros_like(delta_ref)

        has_prev = n > 0
        has_next = n + 1 < nb

        _, gate_vjp = jax.vjp(lambda o_, z_: o_ * _silu(z_), o_ref[...], za_ref[...])
        d_o, d_za = gate_vjp(d_ref[...])
        dza_ref[...] = d_za.astype(BF16)
        d_o2 = d2_ref[...] * _silu(za2_ref[...])
        q_all, q2_all = q_ref[...], q2_ref[...]
        lse_all, lse2_all = lse_ref[...], lse2_ref[...]

        for j in range(KV_HEADS):
            js = slice(j * HEAD_DIM, (j + 1) * HEAD_DIM)
            kc, kp, vc, vp = kc_ref[:, js], kp_ref[:, js], vc_ref[:, js], vp_ref[:, js]
            bias_c, bias_p = bias_ref[j, 0], bias_ref[j, 1]
            q4 = _stack_heads(q_all, j)
            do4b = _stack_heads(d_o, j).astype(BF16)
            lse4 = _stack_columns(lse_all, j)
            pc = jnp.exp(_dot_nt(q4, kc) * ATTN_SCALE + bias_c - lse4)
            pp = jnp.exp(_dot_nt(q4, kp) * ATTN_SCALE + jnp.where(has_prev, bias_p, -jnp.inf) - lse4)
            dpc = _dot_nt(do4b, vc)
            dpp = _dot_nt(do4b, vp)
            delta = jnp.sum(pc * dpc, axis=-1, keepdims=True) + jnp.sum(pp * dpp, axis=-1, keepdims=True)
            delta2 = jnp.where(has_next, delta_ref[j], 0.0)
            delta_ref[j] = delta
            dsc = (pc * (dpc - delta)).astype(BF16)
            dsp = (pp * (dpp - delta)).astype(BF16)
            dq4 = ((_dot(dsc, kc) + _dot(dsp, kp)) * ATTN_SCALE).astype(BF16)
            sink_loss = jnp.exp(_stack_columns(sk_ref[...], j) - lse4) * delta
            for g in range(Q_PER_KV):
                h = j * Q_PER_KV + g
                dq_ref[:, h * HEAD_DIM:(h + 1) * HEAD_DIM] = dq4[g * BLOCK:(g + 1) * BLOCK]
                dsk_ref[0:1, h:h + 1] -= jnp.sum(sink_loss[g * BLOCK:(g + 1) * BLOCK], axis=0, keepdims=True)
            dk = _dot_tn(dsc, q4)
            dv = _dot_tn(pc.astype(BF16), do4b)
            q4n = _stack_heads(q2_all, j)
            do4nb = _stack_heads(d_o2, j).astype(BF16)
            p2 = jnp.exp(_dot_nt(q4n, kc) * ATTN_SCALE + jnp.where(has_next, bias_p, -jnp.inf)
                         - _stack_columns(lse2_all, j))
            ds2 = (p2 * (_dot_nt(do4nb, vc) - delta2)).astype(BF16)
            dk = dk + _dot_tn(ds2, q4n)
            dv = dv + _dot_tn(p2.astype(BF16), do4nb)
            dk_ref[:, js] = (dk * ATTN_SCALE).astype(BF16)
            dv_ref[:, js] = dv.astype(BF16)

    cur = lambda w: pl.BlockSpec((BLOCK, w), lambda b, s: (b * nb + nb - 1 - s, 0))
    prev = lambda w: pl.BlockSpec((BLOCK, w), lambda b, s: (b * nb + jnp.maximum(nb - 2 - s, 0), 0))
    nxt = lambda w: pl.BlockSpec((BLOCK, w), lambda b, s: (b * nb + jnp.minimum(nb - s, nb - 1), 0))
    return _call(
        body, name="attn_bwd", grid=(n_seq, nb),
        in_specs=[cur(512), nxt(512), cur(128), prev(128), cur(128), prev(128), cur(512), nxt(512),
                  cur(512), cur(N_HEADS), nxt(N_HEADS), cur(512), nxt(512), _full((1, N_HEADS))],
        out_specs=[cur(512), cur(128), cur(128), cur(512), _full((1, N_HEADS))],
        out_shape=[_sds((rows, 512), BF16), _sds((rows, 128), BF16), _sds((rows, 128), BF16),
                   _sds((rows, 512), BF16), _sds((1, N_HEADS))],
        scratch_shapes=[_ATTN_BIAS_SCRATCH, pltpu.VMEM((KV_HEADS, _GROUP_ROWS, 1), F32)],
        compiler_params=_params(32, ("arbitrary", "arbitrary")),
    )(q, q, k, k, v, v, za, za, o, lse, lse, d_ao, d_ao, sinks)


def _tail(ssm_out, attn_out, x2d, p2d, target, w_out, g2, w_gate, b_gate, w_proj):
    rows = x2d.shape[0]
    tm = 512

    def body(so_ref, ao_ref, x_ref, p_ref, t_ref, wo_ref, g2_ref, wg_ref, bg_ref, wp_ref,
             dh1_ref, dso_ref, dao_ref, dwo_ref, dwg_ref, dwp_ref, dbg_ref, dg2_ref, loss_ref):
        @pl.when(pl.program_id(0) == 0)
        def _():
            for ref in (dwo_ref, dwg_ref, dwp_ref, dbg_ref, dg2_ref, loss_ref):
                ref[...] = jnp.zeros_like(ref)

        so = so_ref[...]
        ao = ao_ref[...]
        g2 = g2_ref[...]
        mixed = _dot(so, wo_ref[0:512, :]) + _dot(ao, wo_ref[512:1024, :])
        r = lax.rsqrt(jnp.mean(mixed * mixed, axis=-1, keepdims=True) + EPS)
        mr = mixed * r
        h1 = x_ref[...] + mr * g2
        h1b = h1.astype(BF16)
        gate = jax.nn.sigmoid(_dot(h1b, wg_ref[...]) + bg_ref[...])
        pb = p_ref[...].astype(BF16)
        wp_blocks = [slice(j * D_PLE, (j + 1) * D_PLE) for j in range(N_CHIPS)]
        pp = jnp.concatenate([_dot(pb, wp_ref[blk, :]) for blk in wp_blocks], axis=1)
        err = h1 + gate * pp - t_ref[...]
        loss_ref[...] += 0.5 * jnp.sum(jnp.mean(err * err, axis=-1, keepdims=True), axis=0, keepdims=True)

        dh2 = err * (1.0 / D_MODEL)
        d_glin = dh2 * pp * gate * (1.0 - gate)
        d_glin_b = d_glin.astype(BF16)
        dwg_ref[...] += _dot_tn(h1b, d_glin_b)
        dbg_ref[...] += jnp.sum(d_glin, axis=0, keepdims=True)
        d_pp = (dh2 * gate).astype(BF16)
        for blk in wp_blocks:
            dwp_ref[blk, :] += _dot_tn(pb, d_pp[:, blk])
        dh1 = dh2 + _dot_nt(d_glin_b, wg_ref[...])
        dh1_ref[...] = dh1
        dg2_ref[...] += jnp.sum(dh1 * mr, axis=0, keepdims=True)
        a_ = dh1 * g2
        d_mixed = (r * a_ - mr * (r * jnp.mean(a_ * mr, axis=-1, keepdims=True))).astype(BF16)
        dwo_ref[0:512, :] += _dot_tn(so, d_mixed)
        dwo_ref[512:1024, :] += _dot_tn(ao, d_mixed)
        dso_ref[...] = _dot_nt(d_mixed, wo_ref[0:512, :])
        dao_ref[...] = _dot_nt(d_mixed, wo_ref[512:1024, :])

    return _call(
        body, name="tail_fwd_bwd", grid=(rows // tm,),
        in_specs=[_rows(tm, 512), _rows(tm, 512), _rows(tm, D_MODEL), _rows(tm, D_PLE), _rows(tm, D_MODEL),
                  _full((D_MODEL, D_MODEL)), _full((1, D_MODEL)), _full((D_MODEL, D_MODEL)), _full((1, D_MODEL)),
                  _full((N_CHIPS * D_PLE, D_PLE))],
        out_specs=[_rows(tm, D_MODEL), _rows(tm, 512), _rows(tm, 512), _full((D_MODEL, D_MODEL)),
                   _full((D_MODEL, D_MODEL)), _full((N_CHIPS * D_PLE, D_PLE)), _full((1, D_MODEL)), _full((1, D_MODEL)),
                   _full((1, 1))],
        out_shape=[_sds((rows, D_MODEL)), _sds((rows, 512)), _sds((rows, 512)), _sds((D_MODEL, D_MODEL)),
                   _sds((D_MODEL, D_MODEL)), _sds((N_CHIPS * D_PLE, D_PLE)), _sds((1, D_MODEL)), _sds((1, D_MODEL)),
                   _sds((1, 1))],
        compiler_params=_params(52, ("arbitrary",)),
    )(ssm_out, attn_out, x2d, p2d, target, w_out, g2, w_gate, b_gate, w_proj)


def _local_step(x, p, target, pre_norm_g, w_in_t, s5_params, ssm_d, w_glu, b_glu, sinks, w_out, post_norm_g, w_proj,
                w_gate, b_gate):
    n_seq, seq, _ = x.shape
    rows = n_seq * seq
    x2d = x.reshape(rows, D_MODEL)
    p2d = p.reshape(rows, D_PLE)
    t2d = target.reshape(rows, D_MODEL)

    l_re, l_im, bt_re, bt_im, cm_re, cm_im = _s5_params_fwd(*s5_params)

    u_scan, zs, q, k, v, za = _in_proj(x2d, pre_norm_g, w_in_t, n_seq, seq)
    y_scan, h_re, h_im = _s5_scan_fwd(u_scan, bt_re, bt_im, cm_re, cm_im, l_re, l_im, ssm_d, n_seq, seq)
    ssm_out = _glu_fwd(y_scan, zs, w_glu, b_glu, n_seq, seq)
    o, attn_out, lse = _attn_fwd(q, k, v, za, sinks, n_seq, seq)

    dh1, d_so, d_ao, d_w_out, d_w_gate, d_w_proj, d_b_gate, d_g2, loss = _tail(
        ssm_out, attn_out, x2d, p2d, t2d, w_out, post_norm_g, w_gate, b_gate, w_proj)

    dq, dk, dv, dza, d_sinks = _attn_bwd(q, k, v, za, o, lse, d_ao, sinks, n_seq, seq)
    dy_scan, dzs, d_w_glu, d_b_glu = _glu_bwd(y_scan, zs, d_so, w_glu, b_glu, n_seq, seq)
    du_scan, d_bt_re, d_bt_im, d_cm_re, d_cm_im, d_l_re, d_l_im, d_d = _s5_scan_bwd(
        dy_scan, u_scan, h_re, h_im, bt_re, bt_im, cm_re, cm_im, l_re, l_im, ssm_d, n_seq, seq)
    d_lam_re, d_lam_im, d_log_step, d_b_re, d_b_im, d_c_re, d_c_im = _s5_params_bwd(
        s5_params, (d_l_re, d_l_im, d_bt_re, d_bt_im, d_cm_re, d_cm_im))

    grad_x, d_w_in_t, d_g1 = _in_proj_bwd(x2d, dh1, pre_norm_g, w_in_t, du_scan, dzs, dq, dk, dv, dza, n_seq, seq)
    grads = dict(
        pre_norm_g=d_g1, w_in=d_w_in_t, ssm_lam_re=d_lam_re, ssm_lam_im=d_lam_im, ssm_log_step=d_log_step,
        ssm_b_re=d_b_re, ssm_b_im=d_b_im, ssm_c_re=d_c_re, ssm_c_im=d_c_im, ssm_d=d_d, ssm_w_glu=d_w_glu,
        ssm_b_glu=d_b_glu, attn_sinks=d_sinks, w_out=d_w_out, post_norm_g=d_g2, pl_w_proj=d_w_proj,
        pl_w_gate=d_w_gate, pl_b_gate=d_b_gate)
    return grad_x.reshape(x.shape), loss, grads


_BIG = ("w_in", "ssm_w_glu", "w_out", "pl_w_proj", "pl_w_gate")
_BIG_SHARD = {"w_in": (D_IN // N_CHIPS, D_MODEL), "ssm_w_glu": (D_SSM // N_CHIPS, D_SSM),
              "w_out": (D_MODEL // N_CHIPS, D_MODEL), "pl_w_proj": (D_PLE, D_MODEL // N_CHIPS),
              "pl_w_gate": (D_MODEL // N_CHIPS, D_MODEL)}
_SMALL = {"pre_norm_g": (1, D_MODEL), "ssm_lam_re": (SSM_GROUPS, SSM_STATE), "ssm_lam_im": (SSM_GROUPS, SSM_STATE),
          "ssm_log_step": (1, SSM_GROUPS), "ssm_b_re": (D_SSM, SSM_STATE), "ssm_b_im": (D_SSM, SSM_STATE),
          "ssm_c_re": (D_SSM, SSM_STATE), "ssm_c_im": (D_SSM, SSM_STATE), "ssm_d": (1, D_SSM), "ssm_b_glu": (1, D_SSM),
          "attn_sinks": (1, N_HEADS), "post_norm_g": (1, D_MODEL), "pl_b_gate": (1, D_MODEL)}
_VEC_ROWS = ("pre_norm_g", "post_norm_g", "pl_b_gate", "ssm_d", "ssm_b_glu", "attn_sinks", "ssm_log_step", "loss")
_SMALL_GROUPS = (
    ("vec", (8, D_MODEL), tuple((name, r) for r, name in enumerate(_VEC_ROWS))),
    ("lam", (2 * SSM_GROUPS, SSM_STATE), (("ssm_lam_re", 0), ("ssm_lam_im", SSM_GROUPS))),
)
_SMALL_EARLY = ("ssm_b_re", "ssm_b_im", "ssm_c_re", "ssm_c_im")
_SMALL_ORDER = tuple(name for _, _, members in _SMALL_GROUPS for name, _ in members) + _SMALL_EARLY
_WEIGHT_ORDER = ("pre_norm_g", "w_in", "ssm_lam_re", "ssm_lam_im", "ssm_log_step", "ssm_b_re", "ssm_b_im", "ssm_c_re",
                 "ssm_c_im", "ssm_d", "ssm_w_glu", "ssm_b_glu", "attn_sinks", "w_out", "post_norm_g", "pl_w_proj",
                 "pl_w_gate", "pl_b_gate")


def _small_shape(name):
    return (1, 1) if name == "loss" else _SMALL[name]


def _to_kernel_form(name, a):
    a = a[0]
    if name == "w_in":
        return a.T
    if name in ("ssm_b_re", "ssm_b_im"):
        a = a.transpose(0, 2, 1)
    return a.reshape(_SMALL[name]) if name in _SMALL else a


def _from_kernel_form(name, a, shape):
    if name == "w_in":
        a = a.T
    if name in ("ssm_b_re", "ssm_b_im"):
        a = a.reshape(SSM_GROUPS, SSM_GROUP_CH, SSM_STATE).transpose(0, 2, 1)
    return a.reshape(shape)


def _mesh_place():
    x, y, c = lax.axis_index("x"), lax.axis_index("y"), lax.axis_index("c")
    other_chips = ((1 - x, y), (x, 1 - y), (1 - x, 1 - y))
    return x, y, c, other_chips


def _gather_copies(s_refs, g_refs, send_sems, recv_sems, local_sems):
    x, y, c, other_chips = _mesh_place()
    started = []
    for i, (s_ref, g_ref) in enumerate(zip(s_refs, g_refs)):
        rows = s_ref.shape[0]
        half = rows // 2

        def block(chip, g_ref=g_ref, rows=rows, half=half):
            return g_ref.at[pl.ds((2 * chip[0] + chip[1]) * rows + c * half, half), :]

        def copy(k, chip, to, src=None, i=i, block=block):
            return pltpu.make_async_remote_copy(
                src_ref=block(chip) if src is None else src, dst_ref=block(chip), send_sem=send_sems.at[6 * i + k],
                recv_sem=recv_sems.at[6 * i + k], device_id=to, device_id_type=MESH)

        own = pltpu.make_async_copy(s_ref, g_ref.at[pl.ds((2 * x + y) * rows, rows), :], local_sems.at[i])
        own.start()
        first = [copy(k, (x, y), (*chip, c), src=s_ref.at[pl.ds(c * half, half), :])
                 for k, chip in enumerate(other_chips)]
        for cp in first:
            cp.start()
        passed = [copy(3 + k, chip, (x, y, 1 - c)) for k, chip in enumerate(other_chips)]
        started.append((own, first, passed))
    for own, first, passed in started:
        for k in range(3):
            first[k].wait_recv()
            passed[k].start()
    for own, first, passed in started:
        for k in range(3):
            passed[k].wait_recv()
        for cp in first + passed:
            cp.wait_send()
        own.wait()


def _gather_semaphores(n_t):
    return [pltpu.SemaphoreType.DMA((6 * n_t,)), pltpu.SemaphoreType.DMA((6 * n_t,)), pltpu.SemaphoreType.DMA((n_t,))]


def _gather_weights(shards):
    n_t = len(shards)

    def body(*refs):
        _gather_copies(refs[:n_t], refs[n_t:2 * n_t], *refs[2 * n_t + 1:])
        refs[2 * n_t][...] = jnp.zeros_like(refs[2 * n_t])

    any_spec = pl.BlockSpec(memory_space=pl.ANY)
    *full, done = _call(
        body, name="gather_weights", in_specs=[any_spec] * n_t,
        out_specs=[any_spec] * n_t + [pl.BlockSpec(memory_space=pltpu.VMEM)],
        out_shape=[_sds((N_CHIPS * s.shape[0], s.shape[1]), s.dtype) for s in shards]
        + [jax.ShapeDtypeStruct((8, LANES), F32)],
        scratch_shapes=_gather_semaphores(n_t),
    )(*shards)
    return full, done[0, 0]


def _gather_weights_beside(shards):
    n_t = len(shards)
    hbm = pltpu.MemorySpace.HBM
    s_refs = [jax.new_ref(s, memory_space=hbm) for s in shards]
    g_refs = [jax.empty_ref(jax.ShapeDtypeStruct((N_CHIPS * s.shape[0], s.shape[1]), s.dtype), memory_space=hbm)
              for s in shards]

    def launch(send_sems, recv_sems, local_sems):
        x, y, c, other_chips = _mesh_place()
        peers = [(*chip, c) for chip in other_chips] + [(x, y, 1 - c)]
        barrier = pltpu.get_barrier_semaphore()
        for peer in peers:
            pl.semaphore_signal(barrier, inc=1, device_id=peer, device_id_type=MESH)
        pl.semaphore_wait(barrier, len(peers))
        _gather_copies(s_refs, g_refs, send_sems, recv_sems, local_sems)

    pl.kernel(launch, mesh=plsc.ScalarSubcoreMesh(axis_name="sequencer", num_cores=1), name="gather_weights_beside",
              scratch_types=_gather_semaphores(n_t), compiler_params=pltpu.CompilerParams(collective_id=1))()
    return [g[...] for g in g_refs]


_RELATIONS = tuple(((r >> 2) & 1, (r >> 1) & 1, r & 1) for r in range(1, 8))


def _related(place, relation):
    return tuple(1 - a if flip else a for a, flip in zip(place, relation))


def _scatter_beside(mats):
    hbm = pltpu.MemorySpace.HBM
    src_refs = [jax.new_ref(a, memory_space=hbm) for a in mats]
    land_refs = [jax.empty_ref(jax.ShapeDtypeStruct((7, a.shape[0] // 8, a.shape[1]), a.dtype), memory_space=hbm)
                 for a in mats]

    def launch(send_sems, recv_sems):
        me = (lax.axis_index("x"), lax.axis_index("y"), lax.axis_index("c"))
        peers = [_related(me, rel) for rel in _RELATIONS]
        barrier = pltpu.get_barrier_semaphore()
        for peer in peers:
            pl.semaphore_signal(barrier, inc=1, device_id=peer, device_id_type=MESH)
        pl.semaphore_wait(barrier, len(peers))
        copies = []
        for i, (src, land) in enumerate(zip(src_refs, land_refs)):
            hr = land.shape[1]
            for k, (tx, ty, tc) in enumerate(peers):
                rows = pl.ds((2 * tx + ty) * 2 * hr + tc * hr, hr)
                copies.append(pltpu.make_async_remote_copy(
                    src_ref=src.at[rows, :], dst_ref=land.at[k], send_sem=send_sems.at[7 * i + k],
                    recv_sem=recv_sems.at[7 * i + k], device_id=(tx, ty, tc), device_id_type=MESH))
                copies[-1].start()
        for cp in copies:
            cp.wait()

    n_sems = 7 * len(mats)
    pl.kernel(launch, mesh=plsc.ScalarSubcoreMesh(axis_name="sequencer", num_cores=1), name="scatter_beside",
              scratch_types=[pltpu.SemaphoreType.DMA((n_sems,)), pltpu.SemaphoreType.DMA((n_sems,))],
              compiler_params=pltpu.CompilerParams(collective_id=2))()
    return [ref[...] for ref in land_refs]


def _broadcast_beside(arrays):
    hbm = pltpu.MemorySpace.HBM
    src_refs = [jax.new_ref(a, memory_space=hbm) for a in arrays]
    land_refs = [jax.empty_ref(jax.ShapeDtypeStruct((len(_RELATIONS),) + a.shape, a.dtype), memory_space=hbm)
                 for a in arrays]

    def launch(send_sems, recv_sems):
        me = (lax.axis_index("x"), lax.axis_index("y"), lax.axis_index("c"))
        peers = [_related(me, rel) for rel in _RELATIONS]
        barrier = pltpu.get_barrier_semaphore()
        for peer in peers:
            pl.semaphore_signal(barrier, inc=1, device_id=peer, device_id_type=MESH)
        pl.semaphore_wait(barrier, len(peers))
        copies = []
        for i, (src, land) in enumerate(zip(src_refs, land_refs)):
            for k, peer in enumerate(peers):
                copies.append(pltpu.make_async_remote_copy(
                    src_ref=src, dst_ref=land.at[k], send_sem=send_sems.at[7 * i + k],
                    recv_sem=recv_sems.at[7 * i + k], device_id=peer, device_id_type=MESH))
                copies[-1].start()
        for cp in copies:
            cp.wait()

    n_sems = 7 * len(arrays)
    pl.kernel(launch, mesh=plsc.ScalarSubcoreMesh(axis_name="sequencer", num_cores=1), name="broadcast_beside",
              scratch_types=[pltpu.SemaphoreType.DMA((n_sems,)), pltpu.SemaphoreType.DMA((n_sems,))],
              compiler_params=pltpu.CompilerParams(collective_id=3))()
    return [ref[...] for ref in land_refs]


def _exchange_grads(big, small, landed, landed_small):
    n_t = len(big)
    n_g = len(_SMALL_GROUPS)
    names = _SMALL_ORDER
    halves = [(b.shape[0] // N_CHIPS // 2, b.shape[1]) for b in big]
    early = sorted(landed)
    late = [i for i in range(n_t) if i not in landed]
    n_sems = 4 * n_g + 7 * len(late) + n_t
    small_sem0, block_sem0 = n_t, n_t + len(names)
    early_sem0 = block_sem0 + N_CHIPS * len(late)
    landed_sem0 = early_sem0 + 2 * len(early)
    early_small = [n for n in names if n in landed_small]

    def body(*refs):
        pos = 0

        def take(n):
            nonlocal pos
            pos += n
            return refs[pos - n:pos]

        big_refs, small_refs = take(n_t), dict(zip(names, take(len(names))))
        land_refs = dict(zip(early, take(len(early))))
        land_small_refs = dict(zip(early_small, take(len(early_small))))
        out_refs, small_out_refs = take(n_t), dict(zip(names, take(len(names))))
        per_late = lambda: dict(zip(late, take(len(late))))
        ga, gb, pme, send_b, recv_b = per_late(), per_late(), take(n_t), per_late(), per_late()
        own_e, land_e = dict(zip(early, take(len(early)))), dict(zip(early, take(len(early))))
        land_s = dict(zip(early_small, take(len(early_small))))
        s_own, s_sib, s_chips, s_pair = take(n_g), take(n_g), take(n_g), take(n_g)
        stage = dict(zip(names, take(len(names))))
        send_sems, recv_sems, local_sems = take(3)
        x, y, c, other_chips = _mesh_place()
        me = 2 * x + y
        sibling = (x, y, 1 - c)
        sem_at = iter(range(n_sems))

        def remote(src, dst, to):
            k = next(sem_at)
            return pltpu.make_async_remote_copy(src_ref=src, dst_ref=dst, send_sem=send_sems.at[k],
                                                recv_sem=recv_sems.at[k], device_id=to, device_id_type=MESH)

        loads = [pltpu.make_async_copy(small_refs[name], stage[name], local_sems.at[small_sem0 + a])
                 for a, name in enumerate(names)]
        landed_loads = [pltpu.make_async_copy(land_small_refs[name], land_s[name], local_sems.at[landed_sem0 + a])
                        for a, name in enumerate(early_small)]
        for cp in loads + landed_loads:
            cp.start()
        for cp in loads:
            cp.wait()
        small_swaps = []
        for gi, (_, _, members) in enumerate(_SMALL_GROUPS):
            s_own[gi][...] = jnp.zeros_like(s_own[gi])
            for name, r0 in members:
                r, n = _small_shape(name)
                s_own[gi][r0:r0 + r, 0:n] = stage[name][...]
            small_swaps.append(remote(s_own[gi], s_sib[gi], sibling))
            small_swaps[gi].start()
        order = sorted(late, key=lambda i: halves[i][0] * halves[i][1])
        own_loads, big_swaps = {}, {}
        for i in order:
            hr = halves[i][0]
            own_loads[i], big_swaps[i] = [], []
            for j in range(N_CHIPS):
                mine = big_refs[i].at[pl.ds(j * 2 * hr + c * hr, hr), :]
                theirs = big_refs[i].at[pl.ds(j * 2 * hr + (1 - c) * hr, hr), :]
                sem = local_sems.at[block_sem0 + N_CHIPS * late.index(i) + j]
                own_loads[i].append(pltpu.make_async_copy(mine, ga[i].at[j], sem))
                own_loads[i][j].start()
                big_swaps[i].append(remote(theirs, gb[i].at[j], sibling))
                big_swaps[i][j].start()
        early_loads = {}
        for e, i in enumerate(early):
            hr = halves[i][0]
            mine = big_refs[i].at[pl.ds(me * 2 * hr + c * hr, hr), :]
            early_loads[i] = [pltpu.make_async_copy(mine, own_e[i], local_sems.at[early_sem0 + 2 * e]),
                              pltpu.make_async_copy(land_refs[i], land_e[i], local_sems.at[early_sem0 + 2 * e + 1])]
            for cp in early_loads[i]:
                cp.start()
        small_sends = []
        for gi in range(n_g):
            small_swaps[gi].wait_recv()
            s_pair[gi][...] = s_own[gi][...] + s_sib[gi][...]
            small_sends.append([remote(s_pair[gi], s_chips[gi].at[k], (*chip, c)) for k, chip in enumerate(other_chips)])
            for cp in small_sends[gi]:
                cp.start()

        def pair_sum(i, j):
            return ga[i][j] + gb[i][j]

        big_sends = {}
        for i in order:
            for j in range(N_CHIPS):
                own_loads[i][j].wait()
                big_swaps[i][j].wait_recv()
            big_sends[i] = []
            for k, chip in enumerate(other_chips):
                send_b[i][k] = pair_sum(i, 2 * chip[0] + chip[1]).astype(BF16)
                big_sends[i].append(remote(send_b[i].at[k], recv_b[i].at[k], (*chip, c)))
                big_sends[i][k].start()
        last_swaps, keeps = {}, {}
        for i in early + order:
            hr = halves[i][0]
            if i in landed:
                for cp in early_loads[i]:
                    cp.wait()
                total = own_e[i][...]
                for k in range(len(_RELATIONS)):
                    total = total + land_e[i][k]
                pme[i][...] = total
            else:
                for k in range(3):
                    big_sends[i][k].wait_recv()
                pme[i][...] = ((pair_sum(i, me) + recv_b[i][0].astype(F32)) + recv_b[i][1].astype(F32)) + recv_b[i][2].astype(F32)
            mine = out_refs[i].at[pl.ds(c * hr, hr), :]
            keeps[i] = pltpu.make_async_copy(pme[i], mine, local_sems.at[i])
            keeps[i].start()
            last_swaps[i] = remote(pme[i], mine, sibling)
            last_swaps[i].start()

        for gi, (_, _, members) in enumerate(_SMALL_GROUPS):
            for k in range(3):
                small_sends[gi][k].wait_recv()
            total = None
            for j in range(N_CHIPS):
                rel = jnp.bitwise_xor(j, me)
                term = jnp.where(rel == 0, s_pair[gi][...], jnp.where(
                    rel == 2, s_chips[gi][0], jnp.where(rel == 1, s_chips[gi][1], s_chips[gi][2])))
                total = term if total is None else total + term
            s_sib[gi][...] = total
            for name, r0 in members:
                r, n = _small_shape(name)
                stage[name][...] = s_sib[gi][r0:r0 + r, 0:n]
        my_index = 4 * x + 2 * y + c
        for cp in landed_loads:
            cp.wait()
        for name in early_small:
            total = None
            for d in range(2 * N_CHIPS):
                rel = jnp.bitwise_xor(d, my_index)
                term = stage[name][...]
                for k in range(len(_RELATIONS)):
                    term = jnp.where(rel == k + 1, land_s[name][k], term)
                total = term if total is None else total + term
            stage[name][...] = total
        stores = [pltpu.make_async_copy(stage[name], small_out_refs[name], local_sems.at[small_sem0 + a])
                  for a, name in enumerate(names)]
        for cp in stores:
            cp.start()

        for i in range(n_t):
            last_swaps[i].wait_recv()
            keeps[i].wait()
        for cp in stores:
            cp.wait()
        groups = list(big_swaps.values()) + small_sends + list(big_sends.values())
        for cp in small_swaps + [cp for group in groups for cp in group] + list(last_swaps.values()):
            cp.wait_send()

    any_spec = pl.BlockSpec(memory_space=pl.ANY)
    small_shapes = [_sds(_small_shape(n)) for n in names]
    group_shapes = [shape for _, shape, _ in _SMALL_GROUPS]
    vmem = lambda which, dtype, lead=(): [pltpu.VMEM(lead + halves[i], dtype) for i in which]
    outs = _call(
        body, name="exchange_grads",
        in_specs=[any_spec] * (n_t + len(names) + len(early) + len(early_small)),
        out_specs=[any_spec] * (n_t + len(names)),
        out_shape=[_sds((b.shape[0] // N_CHIPS, b.shape[1])) for b in big] + small_shapes,
        scratch_shapes=(vmem(late, F32, (N_CHIPS,)) + vmem(late, F32, (N_CHIPS,)) + vmem(range(n_t), F32)
                        + vmem(late, BF16, (3,)) + vmem(late, BF16, (3,))
                        + vmem(early, F32) + vmem(early, F32, (len(_RELATIONS),))
                        + [pltpu.VMEM((len(_RELATIONS),) + _small_shape(n), F32) for n in early_small]
                        + [pltpu.VMEM(s, F32) for s in group_shapes] * 2 + [pltpu.VMEM((3,) + s, F32) for s in group_shapes]
                        + [pltpu.VMEM(s, F32) for s in group_shapes]
                        + [pltpu.VMEM(_small_shape(n), F32) for n in names]
                        + [pltpu.SemaphoreType.DMA((n_sems,)), pltpu.SemaphoreType.DMA((n_sems,)),
                           pltpu.SemaphoreType.DMA((landed_sem0 + len(early_small),))]),
        compiler_params=_params(48),
    )(*big, *[small[n] for n in names], *[landed[i] for i in early], *[landed_small[n] for n in early_small])
    return list(outs[:n_t]), dict(zip(names, outs[n_t:n_t + len(names)]))


def _adamw_update(w, g, m, v):
    m = ADAM_B1 * m + (1.0 - ADAM_B1) * g
    v = ADAM_B2 * v + (1.0 - ADAM_B2) * (g * g)
    m_hat = m / (1.0 - ADAM_B1 ** ADAM_STEP)
    v_hat = v / (1.0 - ADAM_B2 ** ADAM_STEP)
    return -ADAM_LR * (m_hat / (jnp.sqrt(v_hat) + ADAM_EPS) + ADAM_WD * w), m, v


def _adamw(w, g, m, v, grid, name):
    n_t = len(w)

    def body(*refs):
        ins, outs = refs[:4 * n_t], refs[4 * n_t:]
        for i in range(n_t):
            w_, g_, m_, v_ = [ins[a * n_t + i][...] for a in range(4)]
            vals = (g_,) + _adamw_update(w_, g_, m_, v_)
            for a in range(4):
                outs[a * n_t + i][...] = vals[a]

    specs = [pl.BlockSpec((a.shape[0] // grid, a.shape[1]), lambda i: (i, 0)) for a in w]
    shapes = [_sds(a.shape) for a in w]
    outs = _call(
        body, name=name, grid=(grid,), in_specs=specs * 4, out_specs=specs * 4, out_shape=shapes * 4,
        compiler_params=_params(40, ("arbitrary",)),
    )(*w, *g, *m, *v)
    return [outs[a * n_t:(a + 1) * n_t] for a in range(4)]


def kernel(x, p, pre_norm_g, w_in, ssm_lam_re, ssm_lam_im, ssm_log_step, ssm_b_re, ssm_b_im, ssm_c_re, ssm_c_im, ssm_d, ssm_w_glu, ssm_b_glu, attn_sinks, w_out, post_norm_g, pl_w_proj, pl_w_gate, pl_b_gate, loss_target, m_pre_norm_g, m_w_in, m_ssm_lam_re, m_ssm_lam_im, m_ssm_log_step, m_ssm_b_re, m_ssm_b_im, m_ssm_c_re, m_ssm_c_im, m_ssm_d, m_ssm_w_glu, m_ssm_b_glu, m_attn_sinks, m_w_out, m_post_norm_g, m_pl_w_proj, m_pl_w_gate, m_pl_b_gate, v_pre_norm_g, v_w_in, v_ssm_lam_re, v_ssm_lam_im, v_ssm_log_step, v_ssm_b_re, v_ssm_b_im, v_ssm_c_re, v_ssm_c_im, v_ssm_d, v_ssm_w_glu, v_ssm_b_glu, v_attn_sinks, v_w_out, v_post_norm_g, v_pl_w_proj, v_pl_w_gate, v_pl_b_gate):
    weights = dict(pre_norm_g=pre_norm_g, w_in=w_in, ssm_lam_re=ssm_lam_re, ssm_lam_im=ssm_lam_im,
                   ssm_log_step=ssm_log_step, ssm_b_re=ssm_b_re, ssm_b_im=ssm_b_im, ssm_c_re=ssm_c_re,
                   ssm_c_im=ssm_c_im, ssm_d=ssm_d, ssm_w_glu=ssm_w_glu, ssm_b_glu=ssm_b_glu, attn_sinks=attn_sinks,
                   w_out=w_out, post_norm_g=post_norm_g, pl_w_proj=pl_w_proj, pl_w_gate=pl_w_gate, pl_b_gate=pl_b_gate)
    m_in = dict(pre_norm_g=m_pre_norm_g, w_in=m_w_in, ssm_lam_re=m_ssm_lam_re, ssm_lam_im=m_ssm_lam_im,
                ssm_log_step=m_ssm_log_step, ssm_b_re=m_ssm_b_re, ssm_b_im=m_ssm_b_im, ssm_c_re=m_ssm_c_re,
                ssm_c_im=m_ssm_c_im, ssm_d=m_ssm_d, ssm_w_glu=m_ssm_w_glu, ssm_b_glu=m_ssm_b_glu,
                attn_sinks=m_attn_sinks, w_out=m_w_out, post_norm_g=m_post_norm_g, pl_w_proj=m_pl_w_proj,
                pl_w_gate=m_pl_w_gate, pl_b_gate=m_pl_b_gate)
    v_in = dict(pre_norm_g=v_pre_norm_g, w_in=v_w_in, ssm_lam_re=v_ssm_lam_re, ssm_lam_im=v_ssm_lam_im,
                ssm_log_step=v_ssm_log_step, ssm_b_re=v_ssm_b_re, ssm_b_im=v_ssm_b_im, ssm_c_re=v_ssm_c_re,
                ssm_c_im=v_ssm_c_im, ssm_d=v_ssm_d, ssm_w_glu=v_ssm_w_glu, ssm_b_glu=v_ssm_b_glu,
                attn_sinks=v_attn_sinks, w_out=v_w_out, post_norm_g=v_post_norm_g, pl_w_proj=v_pl_w_proj,
                pl_w_gate=v_pl_w_gate, pl_b_gate=v_pl_b_gate)

    def two_d(tree):
        return {k: _to_kernel_form(k, a) for k, a in tree.items()}

    w2, m2, v2 = two_d(weights), two_d(m_in), two_d(v_in)

    (w_in_full,), gathered = _gather_weights([w2["w_in"].astype(BF16)])
    rest = _gather_weights_beside([(w2[n] + gathered).astype(BF16) for n in _BIG[1:]])
    full = dict(zip(_BIG, [w_in_full] + rest))
    s5_params = tuple(w2[n] for n in ("ssm_lam_re", "ssm_lam_im", "ssm_log_step", "ssm_b_re", "ssm_b_im", "ssm_c_re",
                                      "ssm_c_im"))
    grad_x, loss, grads = _local_step(
        x, p, loss_target, w2["pre_norm_g"], full["w_in"], s5_params, w2["ssm_d"], full["ssm_w_glu"], w2["ssm_b_glu"],
        w2["attn_sinks"], full["w_out"], w2["post_norm_g"], full["pl_w_proj"], full["pl_w_gate"], w2["pl_b_gate"])

    sent_early = ("w_out", "pl_w_gate", "pl_w_proj")
    landed = dict(zip([_BIG.index(n) for n in sent_early], _scatter_beside([grads[n] for n in sent_early])))
    after_scatter = landed[_BIG.index(sent_early[-1])][0, 0, 0] * 0.0
    landed_small = dict(zip(_SMALL_EARLY, _broadcast_beside([grads[n] + after_scatter for n in _SMALL_EARLY])))
    g_big, g_small = _exchange_grads([grads[n] for n in _BIG], {**{n: grads[n] for n in _SMALL}, "loss": loss}, landed,
                                     landed_small)
    g_big = dict(zip(_BIG, g_big))
    total_loss = g_small.pop("loss")

    big_out = _adamw([w2[n] for n in _BIG], [g_big[n] for n in _BIG], [m2[n] for n in _BIG], [v2[n] for n in _BIG],
                     8, "adamw_matrices")
    small_names = tuple(_SMALL)
    small_out = _adamw([w2[n] for n in small_names], [g_small[n] for n in small_names], [m2[n] for n in small_names],
                       [v2[n] for n in small_names], 1, "adamw_small")

    results = [{**dict(zip(_BIG, big_part)), **dict(zip(small_names, small_part))}
               for big_part, small_part in zip(big_out, small_out)]
    flat = [_from_kernel_form(name, r[name], weights[name].shape) for r in results for name in _WEIGHT_ORDER]
    return (total_loss.reshape(()), grad_x, *flat)
```

```python
import math

import jax
import jax.numpy as jnp
from jax import lax
from jax.experimental import pallas as pl
from jax.experimental.pallas import tpu as pltpu
from jax.experimental.pallas import tpu_sc as plsc

F32 = jnp.float32
BF16 = jnp.bfloat16

D_MODEL = 1024
D_SSM = 512
D_ATTN = 512
SSM_GROUPS = 32
SSM_GROUP_CH = 16
SSM_STATE = 64
SSM_LANES = SSM_GROUPS * SSM_STATE
HEAD_DIM = 64
N_HEADS = 8
KV_HEADS = 2
Q_PER_KV = 4
WINDOW = 128
BLOCK = 128
D_PLE = 256
D_IN = 2304
EPS = 1e-6
ATTN_SCALE = 1.0 / math.sqrt(HEAD_DIM)

ADAM_LR = 0.001
ADAM_B1 = 0.9
ADAM_B2 = 0.999
ADAM_EPS = 1e-08
ADAM_WD = 0.01
ADAM_STEP = 10

N_CHIPS = 4
LANES = 128
SCAN_CHUNKS = 8
SCAN_TILE_STEPS = 32
SCAN_LANE_CHUNK = 512
MIB = 2 ** 20
MESH = pl.DeviceIdType.MESH


def _dot(a, b):
    return jnp.dot(a, b, preferred_element_type=F32)


def _dot_nt(a, b):
    return lax.dot_general(a, b, (((1,), (1,)), ((), ())), preferred_element_type=F32)


def _dot_tn(a, b):
    return lax.dot_general(a, b, (((0,), (0,)), ((), ())), preferred_element_type=F32)


def _params(vmem_mib, semantics=None):
    kw = dict(vmem_limit_bytes=vmem_mib * MIB)
    if semantics is not None:
        kw["dimension_semantics"] = semantics
    return pltpu.CompilerParams(**kw)


def _full(shape):
    nd = len(shape)
    return pl.BlockSpec(shape, lambda *_: (0,) * nd, pipeline_mode=pl.Buffered(1))


def _rows(tm, width):
    return pl.BlockSpec((tm, width), lambda i: (i, 0))


def _sds(shape, dtype=F32):
    return pltpu.HBM(shape, dtype)


def _call(body, **kw):
    fn = pl.pallas_call(body, **kw)
    return lambda *args: fn(*[pltpu.with_memory_space_constraint(a, pltpu.HBM) for a in args])


def _silu(z):
    return z * jax.nn.sigmoid(z)


def _in_proj(x2d, g1, w_in_t, n_seq, seq):
    rows = x2d.shape[0]
    tm = 512
    slab, steps, _, _ = _scan_geometry(n_seq, seq)

    def body(x_ref, g_ref, w_ref, *out_refs):
        u_parts, (zs_ref, q_ref, k_ref, v_ref, za_ref) = out_refs[:_SCAN_PARTS], out_refs[_SCAN_PARTS:]
        x = x_ref[...]
        r = lax.rsqrt(jnp.mean(x * x, axis=-1, keepdims=True) + EPS)
        hn = (x * r * g_ref[...]).astype(BF16)

        def proj(a, b):
            return _dot_nt(hn, w_ref[a:b, :])

        _store_chunks(u_parts, pl.program_id(0) * (tm // steps), proj(0, 512), steps, slab)
        zs_ref[...] = proj(512, 1024)
        q_ref[...] = (proj(1024, 1536) * ATTN_SCALE).astype(BF16)
        k_ref[...] = proj(1536, 1664).astype(BF16)
        v_ref[...] = proj(1664, 1792).astype(BF16)
        za_ref[...] = proj(1792, 2304)

    *u_parts, zs, q, k, v, za = _call(
        body, name="in_proj", grid=(rows // tm,),
        in_specs=[_rows(tm, D_MODEL), _full((1, D_MODEL)), _full((D_IN, D_MODEL))],
        out_specs=_whole_parts(rows) + [_rows(tm, 512), _rows(tm, 512), _rows(tm, 128), _rows(tm, 128), _rows(tm, 512)],
        out_shape=_part_shapes(rows) + [_sds((rows, 512)), _sds((rows, 512), BF16), _sds((rows, 128), BF16),
                                        _sds((rows, 128), BF16), _sds((rows, 512))],
        compiler_params=_params(48, ("arbitrary",)),
    )(x2d, g1, w_in_t)
    return u_parts, zs, q, k, v, za


def _in_proj_bwd(x2d, dh1, g1, w_in_t, du_parts, dzs, dq, dk, dv, dza, n_seq, seq):
    rows = x2d.shape[0]
    tm = 256
    slab, steps, _, _ = _scan_geometry(n_seq, seq)
    pieces = ((0, 512), (512, 1024), (1024, 1536), (1536, 1664), (1664, 1792), (1792, 2304))

    def body(x_ref, dh1_ref, g_ref, w_ref, *refs):
        du_parts, (dzs_ref, dq_ref, dk_ref, dv_ref, dza_ref, gx_ref, dw_ref, dg_ref) = refs[:_SCAN_PARTS], refs[_SCAN_PARTS:]

        @pl.when(pl.program_id(0) == 0)
        def _():
            dw_ref[...] = jnp.zeros_like(dw_ref)
            dg_ref[...] = jnp.zeros_like(dg_ref)

        x = x_ref[...]
        g = g_ref[...]
        r = lax.rsqrt(jnp.mean(x * x, axis=-1, keepdims=True) + EPS)
        xr = x * r
        hn = (xr * g).astype(BF16)
        dhn = jnp.zeros((tm, D_MODEL), F32)
        du = _load_chunks(du_parts, pl.program_id(0) * (tm // steps), tm // steps, steps, slab)
        for (a, b), piece in zip(pieces, (du, dzs_ref[...], dq_ref[...], dk_ref[...], dv_ref[...], dza_ref[...])):
            piece = piece.astype(BF16)
            dhn = dhn + _dot(piece, w_ref[a:b, :])
            dw_ref[a:b, :] += _dot_tn(piece, hn)
        dg_ref[...] += jnp.sum(dhn * xr, axis=0, keepdims=True)
        a_ = dhn * g
        gx_ref[...] = dh1_ref[...] + r * a_ - xr * (r * jnp.mean(a_ * xr, axis=-1, keepdims=True))

    return _call(
        body, name="in_proj_bwd", grid=(rows // tm,),
        in_specs=[_rows(tm, D_MODEL), _rows(tm, D_MODEL), _full((1, D_MODEL)), _full((D_IN, D_MODEL))]
        + _whole_parts(rows) + [_rows(tm, 512), _rows(tm, 512), _rows(tm, 128), _rows(tm, 128), _rows(tm, 512)],
        out_specs=[_rows(tm, D_MODEL), _full((D_IN, D_MODEL)), _full((1, D_MODEL))],
        out_shape=[_sds((rows, D_MODEL)), _sds((D_IN, D_MODEL)), _sds((1, D_MODEL))],
        compiler_params=_params(52, ("arbitrary",)),
    )(x2d, dh1, g1, w_in_t, *du_parts, dzs, dq, dk, dv, dza)


def _iota(shape, axis):
    return lax.broadcasted_iota(jnp.int32, shape, axis)


def _exact_dot(a, b):
    return jnp.dot(a, b, precision=lax.Precision.HIGHEST, preferred_element_type=F32)


_HALF_GROUPS = SSM_GROUPS // 2
_N_SHIFT = SSM_STATE.bit_length() - 1
_P_SHIFT = SSM_GROUP_CH.bit_length() - 1


def _s5_operands(lam_re, lam_im, log_step, b_re, b_im, c_re, c_im):
    g, n, p = SSM_GROUPS, SSM_STATE, SSM_GROUP_CH
    gn, gp, hn_, hp = g * n, g * p, _HALF_GROUPS * n, _HALF_GROUPS * p
    eye_g = _iota((g, g), 0) == _iota((g, g), 1)
    step = jnp.sum(jnp.where(eye_g, jnp.exp(log_step), 0.0), axis=1, keepdims=True)
    a_re = lam_re * step
    a_im = lam_im * step
    mag = jnp.exp(a_re)
    lbar_re = mag * jnp.cos(a_im)
    lbar_im = mag * jnp.sin(a_im)
    n_re = lbar_re - 1.0
    den = lam_re * lam_re + lam_im * lam_im
    f_re = (n_re * lam_re + lbar_im * lam_im) / den
    f_im = (lbar_im * lam_re - n_re * lam_im) / den

    spread_n = (_iota((n, gn), 0) == (_iota((n, gn), 1) & (n - 1))).astype(F32)
    own_g = _iota((g, gn), 0) == (_iota((g, gn), 1) >> _N_SHIFT)

    def to_row(a):
        return jnp.sum(jnp.where(own_g, _exact_dot(a, spread_n), 0.0), axis=0, keepdims=True)

    per_group = ((_iota((gp, g), 0) >> _P_SHIFT) == _iota((gp, g), 1)).astype(F32)
    fx_re, fx_im = _exact_dot(per_group, f_re), _exact_dot(per_group, f_im)
    bbar_re = fx_re * b_re - fx_im * b_im
    bbar_im = fx_re * b_im + fx_im * b_re

    tile_n = (_iota((n, hn_), 0) == (_iota((n, hn_), 1) & (n - 1))).astype(F32)
    same_group = (_iota((hp, hn_), 0) >> _P_SHIFT) == (_iota((hp, hn_), 1) >> _N_SHIFT)

    def embed(a, hf):
        return jnp.where(same_group, _exact_dot(a[hf * hp:(hf + 1) * hp], tile_n), 0.0)

    return (to_row(lbar_re), to_row(lbar_im), embed(bbar_re, 0), embed(bbar_re, 1), embed(bbar_im, 0),
            embed(bbar_im, 1), embed(c_re, 0), embed(c_re, 1), embed(c_im, 0), embed(c_im, 1))


_S5_PARAM_SHAPES = ((SSM_GROUPS, SSM_STATE), (SSM_GROUPS, SSM_STATE), (1, SSM_GROUPS),
                    (D_SSM, SSM_STATE), (D_SSM, SSM_STATE), (D_SSM, SSM_STATE), (D_SSM, SSM_STATE))
_CM_SHAPE = (2, _HALF_GROUPS * SSM_GROUP_CH, _HALF_GROUPS * SSM_STATE)
_S5_OPERAND_SHAPES = ((1, SSM_LANES), (1, SSM_LANES), _CM_SHAPE, _CM_SHAPE, _CM_SHAPE, _CM_SHAPE)


def _s5_params_fwd(*params):
    def body(*refs):
        ins, (lre_ref, lim_ref, btre_ref, btim_ref, cmre_ref, cmim_ref) = refs[:7], refs[7:]
        vals = _s5_operands(*[r[...] for r in ins])
        lre_ref[...] = vals[0]
        lim_ref[...] = vals[1]
        for ref, pair in zip((btre_ref, btim_ref, cmre_ref, cmim_ref), (vals[2:4], vals[4:6], vals[6:8], vals[8:10])):
            ref[0] = pair[0].astype(BF16)
            ref[1] = pair[1].astype(BF16)

    dtypes = (F32, F32, BF16, BF16, BF16, BF16)
    return _call(
        body, name="s5_params_fwd",
        in_specs=[_full(s) for s in _S5_PARAM_SHAPES], out_specs=[_full(s) for s in _S5_OPERAND_SHAPES],
        out_shape=[_sds(s, d) for s, d in zip(_S5_OPERAND_SHAPES, dtypes)], compiler_params=_params(32),
    )(*params)


def _s5_params_bwd(params, cotangents):
    def body(*refs):
        ins, (dlre, dlim, dbtre, dbtim, dcmre, dcmim), outs = refs[:7], refs[7:13], refs[13:]
        _, vjp = jax.vjp(_s5_operands, *[r[...] for r in ins])
        cts = (dlre[...], dlim[...], dbtre[0], dbtre[1], dbtim[0], dbtim[1], dcmre[0], dcmre[1], dcmim[0], dcmim[1])
        for ref, val in zip(outs, vjp(cts)):
            ref[...] = val

    return _call(
        body, name="s5_params_bwd",
        in_specs=[_full(s) for s in _S5_PARAM_SHAPES + _S5_OPERAND_SHAPES],
        out_specs=[_full(s) for s in _S5_PARAM_SHAPES],
        out_shape=[_sds(s) for s in _S5_PARAM_SHAPES], compiler_params=_params(48),
    )(*params, *cotangents)


def _scan_geometry(n_seq, seq):
    slab = n_seq * SCAN_CHUNKS
    steps = seq // SCAN_CHUNKS
    tile_rows = slab * SCAN_TILE_STEPS
    n_tiles = steps // SCAN_TILE_STEPS
    return slab, steps, tile_rows, n_tiles


_SCAN_PARTS = D_SSM // LANES


def _whole_parts(rows):
    return [_full((rows, LANES))] * _SCAN_PARTS


def _part_shapes(rows):
    return [_sds((rows, LANES))] * _SCAN_PARTS


def _load_chunks(parts, first_chunk, n_chunks, steps, slab):
    return jnp.concatenate([
        jnp.concatenate([ref[pl.ds(first_chunk + q, steps, stride=slab), :] for ref in parts], axis=1)
        for q in range(n_chunks)], axis=0)


def _store_chunks(parts, first_chunk, value, steps, slab):
    for q in range(value.shape[0] // steps):
        for j, ref in enumerate(parts):
            ref[pl.ds(first_chunk + q, steps, stride=slab), :] = value[q * steps:(q + 1) * steps,
                                                                     j * LANES:(j + 1) * LANES]


def _join_parts(parts):
    return jnp.concatenate([ref[...] for ref in parts], axis=1)


def _split_parts(parts, value):
    for j, ref in enumerate(parts):
        ref[...] = value[:, j * LANES:(j + 1) * LANES]


def _complex_power(re, im, n):
    out = None
    while n:
        if n & 1:
            out = (re, im) if out is None else (out[0] * re - out[1] * im, out[0] * im + out[1] * re)
        n >>= 1
        if n:
            re, im = re * re - im * im, 2.0 * re * im
    return out


def _chunk_carry(sum_re, sum_im, carry_re, carry_im, a_re, a_im, n_seq, reverse):
    carry_re[...] = jnp.zeros_like(carry_re)
    carry_im[...] = jnp.zeros_like(carry_im)
    for s in range(n_seq):
        order = range(SCAN_CHUNKS - 2, -1, -1) if reverse else range(1, SCAN_CHUNKS)
        for c in order:
            r = s * SCAN_CHUNKS + c
            p = r + 1 if reverse else r - 1
            p_re, p_im = carry_re[p:p + 1, :], carry_im[p:p + 1, :]
            carry_re[r:r + 1, :] = a_re * p_re - a_im * p_im + sum_re[p:p + 1, :]
            carry_im[r:r + 1, :] = a_re * p_im + a_im * p_re + sum_im[p:p + 1, :]


def _s5_scan_fwd(u_parts, bt_re, bt_im, cm_re, cm_im, lbar_re, lbar_im, d_row, n_seq, seq):
    slab, steps, tile_rows, n_tiles = _scan_geometry(n_seq, seq)
    rows = u_parts[0].shape[0]

    def body(*refs):
        u_refs, refs = refs[:_SCAN_PARTS], refs[_SCAN_PARTS:]
        (bre_ref, bim_ref, cre_ref, cim_ref, lre_ref, lim_ref, d_ref), refs = refs[:7], refs[7:]
        y_refs, (hre_ref, him_ref, st_re, st_im, h0_re, h0_im, buf_re, buf_im) = refs[:_SCAN_PARTS], refs[_SCAN_PARTS:]
        second = pl.program_id(0) == 1
        i = pl.program_id(1)

        @pl.when(jnp.logical_and(i == 0, jnp.logical_not(second)))
        def _():
            st_re[...] = jnp.zeros_like(st_re)
            st_im[...] = jnp.zeros_like(st_im)

        u = _join_parts(u_refs)
        ub = u.astype(BF16)
        for hf in range(2):
            cols = slice(hf * 1024, (hf + 1) * 1024)
            buf_re[:, cols] = _dot(ub[:, hf * 256:(hf + 1) * 256], bre_ref[hf])
            buf_im[:, cols] = _dot(ub[:, hf * 256:(hf + 1) * 256], bim_ref[hf])

        for lc in range(SSM_LANES // SCAN_LANE_CHUNK):
            cols = slice(lc * SCAN_LANE_CHUNK, (lc + 1) * SCAN_LANE_CHUNK)
            l_re = jnp.broadcast_to(lre_ref[:, cols], (slab, SCAN_LANE_CHUNK))
            l_im = jnp.broadcast_to(lim_ref[:, cols], (slab, SCAN_LANE_CHUNK))

            def scan_tile(keep_states):
                def step(t, carry):
                    s_re, s_im = carry
                    r0 = pl.multiple_of(t * slab, slab)
                    n_re = l_re * s_re - l_im * s_im + buf_re[pl.ds(r0, slab), cols]
                    n_im = l_re * s_im + l_im * s_re + buf_im[pl.ds(r0, slab), cols]
                    if keep_states:
                        buf_re[pl.ds(r0, slab), cols] = n_re
                        buf_im[pl.ds(r0, slab), cols] = n_im
                    return n_re, n_im

                s_re, s_im = lax.fori_loop(0, SCAN_TILE_STEPS, step, (st_re[:, cols], st_im[:, cols]), unroll=True)
                st_re[:, cols] = s_re
                st_im[:, cols] = s_im

            pl.when(jnp.logical_not(second))(lambda: scan_tile(False))
            pl.when(second)(lambda: scan_tile(True))

        @pl.when(jnp.logical_and(i == n_tiles - 1, jnp.logical_not(second)))
        def _():
            a_re, a_im = _complex_power(lre_ref[...], lim_ref[...], steps)
            _chunk_carry(st_re, st_im, h0_re, h0_im, a_re, a_im, n_seq, reverse=False)
            st_re[...] = h0_re[...]
            st_im[...] = h0_im[...]

        @pl.when(second)
        def _():
            h_re = buf_re[...].astype(BF16)
            h_im = buf_im[...].astype(BF16)
            hre_ref[...] = h_re
            him_ref[...] = h_im
            for hf in range(2):
                cols = slice(hf * 1024, (hf + 1) * 1024)
                ycols = slice(hf * 256, (hf + 1) * 256)
                y_half = (_dot_nt(h_re[:, cols], cre_ref[hf]) - _dot_nt(h_im[:, cols], cim_ref[hf])
                          + d_ref[:, ycols] * u[:, ycols])
                _split_parts(y_refs[2 * hf:2 * hf + 2], y_half)

    tile = lambda w: pl.BlockSpec((tile_rows, w), lambda p, i: (i, 0))
    out_tile = lambda w: pl.BlockSpec((tile_rows, w), lambda p, i: (i * p, 0))
    cm = _full(_CM_SHAPE)
    outs = _call(
        body, name="s5_scan_fwd", grid=(2, n_tiles),
        in_specs=[tile(LANES)] * _SCAN_PARTS + [cm, cm, cm, cm, _full((1, SSM_LANES)), _full((1, SSM_LANES)),
                                                _full((1, 512))],
        out_specs=[out_tile(LANES)] * _SCAN_PARTS + [out_tile(SSM_LANES), out_tile(SSM_LANES)],
        out_shape=_part_shapes(rows) + [_sds((rows, SSM_LANES), BF16), _sds((rows, SSM_LANES), BF16)],
        scratch_shapes=[pltpu.VMEM((slab, SSM_LANES), F32)] * 4 + [pltpu.VMEM((tile_rows, SSM_LANES), F32)] * 2,
        compiler_params=_params(40, ("arbitrary", "arbitrary")),
    )(*u_parts, bt_re, bt_im, cm_re, cm_im, lbar_re, lbar_im, d_row)
    return outs[:_SCAN_PARTS], outs[_SCAN_PARTS], outs[_SCAN_PARTS + 1]


def _s5_scan_bwd(dy_parts, u_parts, h_re, h_im, bt_re, bt_im, cm_re, cm_im, lbar_re, lbar_im, d_row, n_seq, seq):
    slab, steps, tile_rows, n_tiles = _scan_geometry(n_seq, seq)
    rows = u_parts[0].shape[0]

    def body(*refs):
        dy_refs, u_refs, refs = refs[:_SCAN_PARTS], refs[_SCAN_PARTS:2 * _SCAN_PARTS], refs[2 * _SCAN_PARTS:]
        (hre_ref, him_ref, bre_ref, bim_ref, cre_ref, cim_ref, lre_ref, lim_ref, d_ref), refs = refs[:9], refs[9:]
        du_refs, refs = refs[:_SCAN_PARTS], refs[_SCAN_PARTS:]
        (dbre_ref, dbim_ref, dcre_ref, dcim_ref, dlre_ref, dlim_ref, dd_ref,
         st_re, st_im, g0_re, g0_im, acc_re, acc_im, buf_re, buf_im) = refs
        second = pl.program_id(0) == 1
        i = pl.program_id(1)

        @pl.when(jnp.logical_and(i == 0, jnp.logical_not(second)))
        def _():
            st_re[...] = jnp.zeros_like(st_re)
            st_im[...] = jnp.zeros_like(st_im)
            acc_re[...] = jnp.zeros_like(acc_re)
            acc_im[...] = jnp.zeros_like(acc_im)
            for ref in (dbre_ref, dbim_ref, dcre_ref, dcim_ref, dd_ref):
                ref[...] = jnp.zeros_like(ref)

        dy = _join_parts(dy_refs)
        dyb = dy.astype(BF16)
        for hf in range(2):
            cols = slice(hf * 1024, (hf + 1) * 1024)
            buf_re[:, cols] = _dot(dyb[:, hf * 256:(hf + 1) * 256], cre_ref[hf])
            buf_im[:, cols] = -_dot(dyb[:, hf * 256:(hf + 1) * 256], cim_ref[hf])

        for lc in range(SSM_LANES // SCAN_LANE_CHUNK):
            cols = slice(lc * SCAN_LANE_CHUNK, (lc + 1) * SCAN_LANE_CHUNK)
            l_re = jnp.broadcast_to(lre_ref[:, cols], (slab, SCAN_LANE_CHUNK))
            l_im = jnp.broadcast_to(lim_ref[:, cols], (slab, SCAN_LANE_CHUNK))

            def advance(r0, s_re, s_im):
                n_re = l_re * s_re + l_im * s_im + buf_re[pl.ds(r0, slab), cols]
                n_im = l_re * s_im - l_im * s_re + buf_im[pl.ds(r0, slab), cols]
                buf_re[pl.ds(r0, slab), cols] = n_re
                buf_im[pl.ds(r0, slab), cols] = n_im
                return n_re, n_im

            def row0(k):
                return pl.multiple_of((SCAN_TILE_STEPS - 1 - k) * slab, slab)

            @pl.when(jnp.logical_not(second))
            def _():
                s_re, s_im = lax.fori_loop(0, SCAN_TILE_STEPS, lambda k, s: advance(row0(k), *s),
                                           (st_re[:, cols], st_im[:, cols]), unroll=True)
                st_re[:, cols] = s_re
                st_im[:, cols] = s_im

            @pl.when(second)
            def _():
                def step(k, carry):
                    s_re, s_im, a_re, a_im = carry
                    r0 = row0(k)
                    hr = hre_ref[pl.ds(r0, slab), cols].astype(F32)
                    hi = him_ref[pl.ds(r0, slab), cols].astype(F32)
                    a_re = a_re + s_re * hr + s_im * hi
                    a_im = a_im + s_im * hr - s_re * hi
                    return advance(r0, s_re, s_im) + (a_re, a_im)

                zero = jnp.zeros((slab, SCAN_LANE_CHUNK), F32)
                s_re, s_im, a_re, a_im = lax.fori_loop(
                    0, SCAN_TILE_STEPS, step, (st_re[:, cols], st_im[:, cols], zero, zero), unroll=True)
                st_re[:, cols] = s_re
                st_im[:, cols] = s_im
                acc_re[:, cols] += a_re
                acc_im[:, cols] += a_im

        @pl.when(jnp.logical_and(i == n_tiles - 1, jnp.logical_not(second)))
        def _():
            p_re, p_im = _complex_power(lre_ref[...], lim_ref[...], steps)
            _chunk_carry(st_re, st_im, g0_re, g0_im, p_re, -p_im, n_seq, reverse=True)
            st_re[...] = g0_re[...]
            st_im[...] = g0_im[...]

        @pl.when(second)
        def _():
            u = _join_parts(u_refs)
            ub = u.astype(BF16)
            g_re = buf_re[...].astype(BF16)
            g_im = buf_im[...].astype(BF16)
            dd_ref[...] += jnp.sum(dy * u, axis=0, keepdims=True)
            for hf in range(2):
                cols = slice(hf * 1024, (hf + 1) * 1024)
                ycols = slice(hf * 256, (hf + 1) * 256)
                du_half = (_dot_nt(g_re[:, cols], bre_ref[hf]) + _dot_nt(g_im[:, cols], bim_ref[hf])
                           + d_ref[:, ycols] * dy[:, ycols])
                _split_parts(du_refs[2 * hf:2 * hf + 2], du_half)
                dbre_ref[hf] += _dot_tn(ub[:, ycols], g_re[:, cols])
                dbim_ref[hf] += _dot_tn(ub[:, ycols], g_im[:, cols])
                dcre_ref[hf] += _dot_tn(dyb[:, ycols], hre_ref[:, cols])
                dcim_ref[hf] -= _dot_tn(dyb[:, ycols], him_ref[:, cols])

        @pl.when(jnp.logical_and(i == n_tiles - 1, second))
        def _():
            dlre_ref[...] = jnp.sum(acc_re[...], axis=0, keepdims=True)
            dlim_ref[...] = jnp.sum(acc_im[...], axis=0, keepdims=True)

    tile = lambda w: pl.BlockSpec((tile_rows, w), lambda p, i: (n_tiles - 1 - i, 0))
    second_tile = lambda w: pl.BlockSpec((tile_rows, w), lambda p, i: (n_tiles - 1 - i * p, 0))
    cm = _full(_CM_SHAPE)
    row = _full((1, SSM_LANES))
    outs = _call(
        body, name="s5_scan_bwd", grid=(2, n_tiles),
        in_specs=[tile(LANES)] * _SCAN_PARTS + [second_tile(LANES)] * _SCAN_PARTS
        + [second_tile(SSM_LANES), second_tile(SSM_LANES), cm, cm, cm, cm, row, row, _full((1, 512))],
        out_specs=[second_tile(LANES)] * _SCAN_PARTS + [cm, cm, cm, cm, row, row, _full((1, 512))],
        out_shape=(_part_shapes(rows) + [_sds(_CM_SHAPE)] * 4 + [_sds((1, SSM_LANES))] * 2 + [_sds((1, 512))]),
        scratch_shapes=[pltpu.VMEM((slab, SSM_LANES), F32)] * 6 + [pltpu.VMEM((tile_rows, SSM_LANES), F32)] * 2,
        compiler_params=_params(48, ("arbitrary", "arbitrary")),
    )(*dy_parts, *u_parts, h_re, h_im, bt_re, bt_im, cm_re, cm_im, lbar_re, lbar_im, d_row)
    return (outs[:_SCAN_PARTS],) + tuple(outs[_SCAN_PARTS:])


def _glu_gate(gl, a, zs):
    return gl * jax.nn.sigmoid(a) * _silu(zs)


def _glu_fwd(y_parts, zs, w_glu, b_glu, n_seq, seq):
    rows = zs.shape[0]
    tm = 512
    slab, steps, _, _ = _scan_geometry(n_seq, seq)

    def body(*refs):
        y_refs, (zs_ref, w_ref, b_ref, o_ref) = refs[:_SCAN_PARTS], refs[_SCAN_PARTS:]
        y = _load_chunks(y_refs, pl.program_id(0) * (tm // steps), tm // steps, steps, slab)
        gl = jax.nn.gelu(y)
        a = _dot(gl.astype(BF16), w_ref[...]) + b_ref[...]
        o_ref[...] = _glu_gate(gl, a, zs_ref[...]).astype(BF16)

    return _call(
        body, name="glu_fwd", grid=(rows // tm,),
        in_specs=_whole_parts(rows) + [_rows(tm, 512), _full((512, 512)), _full((1, 512))],
        out_specs=_rows(tm, 512), out_shape=_sds((rows, 512), BF16),
        compiler_params=_params(32, ("arbitrary",)),
    )(*y_parts, zs, w_glu, b_glu)


def _glu_bwd(y_parts, zs, d_out, w_glu, b_glu, n_seq, seq):
    rows = zs.shape[0]
    tm = 512
    slab, steps, _, _ = _scan_geometry(n_seq, seq)

    def body(*refs):
        y_refs, (zs_ref, d_ref, w_ref, b_ref), refs = refs[:_SCAN_PARTS], refs[_SCAN_PARTS:_SCAN_PARTS + 4], refs[_SCAN_PARTS + 4:]
        dy_refs, (dzs_ref, dw_ref, db_ref) = refs[:_SCAN_PARTS], refs[_SCAN_PARTS:]
        first_chunk = pl.program_id(0) * (tm // steps)

        @pl.when(pl.program_id(0) == 0)
        def _():
            dw_ref[...] = jnp.zeros_like(dw_ref)
            db_ref[...] = jnp.zeros_like(db_ref)

        gl, gelu_vjp = jax.vjp(jax.nn.gelu, _load_chunks(y_refs, first_chunk, tm // steps, steps, slab))
        glb = gl.astype(BF16)
        a = _dot(glb, w_ref[...]) + b_ref[...]
        _, gate_vjp = jax.vjp(_glu_gate, gl, a, zs_ref[...])
        d_gl, d_a, d_zs = gate_vjp(d_ref[...])
        dab = d_a.astype(BF16)
        d_gl = d_gl + _dot_nt(dab, w_ref[...])
        _store_chunks(dy_refs, first_chunk, gelu_vjp(d_gl)[0], steps, slab)
        dzs_ref[...] = d_zs.astype(BF16)
        dw_ref[...] += _dot_tn(glb, dab)
        db_ref[...] += jnp.sum(d_a, axis=0, keepdims=True)

    *dy_parts, dzs, dw, db = _call(
        body, name="glu_bwd", grid=(rows // tm,),
        in_specs=_whole_parts(rows) + [_rows(tm, 512), _rows(tm, 512), _full((512, 512)), _full((1, 512))],
        out_specs=_whole_parts(rows) + [_rows(tm, 512), _full((512, 512)), _full((1, 512))],
        out_shape=_part_shapes(rows) + [_sds((rows, 512), BF16), _sds((512, 512)), _sds((1, 512))],
        compiler_params=_params(40, ("arbitrary",)),
    )(*y_parts, zs, d_out, w_glu, b_glu)
    return dy_parts, dzs, dw, db


_GROUP_ROWS = Q_PER_KV * BLOCK
_BLOCK_SHIFT = BLOCK.bit_length() - 1


def _attn_bias(j):
    row = _iota((_GROUP_ROWS, BLOCK), 0)
    dist_cur = (row & (BLOCK - 1)) - _iota((_GROUP_ROWS, BLOCK), 1)
    dist_prev = dist_cur + BLOCK
    head = row >> _BLOCK_SHIFT
    slope = jnp.zeros((_GROUP_ROWS, BLOCK), F32)
    for g in range(Q_PER_KV):
        slope = jnp.where(head == g, 2.0 ** (-(j * Q_PER_KV + g + 1)), slope)
    bias_cur = jnp.where(dist_cur >= 0, -slope * dist_cur.astype(F32), -jnp.inf)
    bias_prev = jnp.where(dist_prev < WINDOW, -slope * dist_prev.astype(F32), -jnp.inf)
    return bias_cur, bias_prev


_ATTN_BIAS_SCRATCH = pltpu.VMEM((KV_HEADS, 2, _GROUP_ROWS, BLOCK), F32)


def _fill_attn_bias(bias_ref):
    @pl.when(jnp.logical_and(pl.program_id(0) == 0, pl.program_id(1) == 0))
    def _():
        for j in range(KV_HEADS):
            bias_ref[j, 0], bias_ref[j, 1] = _attn_bias(j)


def _stack_heads(x, j):
    heads = range(j * Q_PER_KV, (j + 1) * Q_PER_KV)
    return jnp.concatenate([x[:, h * HEAD_DIM:(h + 1) * HEAD_DIM] for h in heads], axis=0)


def _stack_columns(x, j):
    heads = range(j * Q_PER_KV, (j + 1) * Q_PER_KV)
    return jnp.concatenate([jnp.broadcast_to(x[:, h:h + 1], (BLOCK, 1)) for h in heads], axis=0)


def _attn_fwd(q, k, v, za, sinks, n_seq, seq):
    nb = seq // BLOCK
    rows = q.shape[0]

    def body(q_ref, kc_ref, kp_ref, vc_ref, vp_ref, za_ref, sk_ref, o_ref, ao_ref, lse_ref, bias_ref):
        _fill_attn_bias(bias_ref)
        has_prev = pl.program_id(1) > 0
        q_all = q_ref[...]
        for j in range(KV_HEADS):
            js = slice(j * HEAD_DIM, (j + 1) * HEAD_DIM)
            bias_c, bias_p = bias_ref[j, 0], bias_ref[j, 1]
            q4 = _stack_heads(q_all, j)
            sc = _dot_nt(q4, kc_ref[:, js]) + bias_c
            sp = _dot_nt(q4, kp_ref[:, js]) + jnp.where(has_prev, bias_p, -jnp.inf)
            sink = _stack_columns(sk_ref[...], j)
            m = jnp.maximum(jnp.max(jnp.maximum(sc, sp), axis=-1, keepdims=True), sink)
            ec = jnp.exp(sc - m)
            ep = jnp.exp(sp - m)
            den = jnp.sum(ec + ep, axis=-1, keepdims=True) + jnp.exp(sink - m)
            inv = 1.0 / den
            o4 = _dot((ec * inv).astype(BF16), vc_ref[:, js]) + _dot((ep * inv).astype(BF16), vp_ref[:, js])
            lse4 = m + jnp.log(den)
            for g in range(Q_PER_KV):
                h = j * Q_PER_KV + g
                o_ref[:, h * HEAD_DIM:(h + 1) * HEAD_DIM] = o4[g * BLOCK:(g + 1) * BLOCK]
                lse_ref[:, h:h + 1] = lse4[g * BLOCK:(g + 1) * BLOCK]
        ao_ref[...] = (o_ref[...] * _silu(za_ref[...])).astype(BF16)

    cur = lambda w: pl.BlockSpec((BLOCK, w), lambda b, n: (b * nb + n, 0))
    prev = lambda w: pl.BlockSpec((BLOCK, w), lambda b, n: (b * nb + jnp.maximum(n - 1, 0), 0))
    return _call(
        body, name="attn_fwd", grid=(n_seq, nb),
        in_specs=[cur(512), cur(128), prev(128), cur(128), prev(128), cur(512), _full((1, N_HEADS))],
        out_specs=[cur(512), cur(512), cur(N_HEADS)],
        out_shape=[_sds((rows, 512)), _sds((rows, 512), BF16), _sds((rows, N_HEADS))],
        scratch_shapes=[_ATTN_BIAS_SCRATCH], compiler_params=_params(32, ("arbitrary", "arbitrary")),
    )(q, k, k, v, v, za, sinks)


def _attn_bwd(q, k, v, za, o, lse, d_ao, sinks, n_seq, seq):
    nb = seq // BLOCK
    rows = q.shape[0]

    def body(q_ref, q2_ref, kc_ref, kp_ref, vc_ref, vp_ref, za_ref, za2_ref, o_ref, lse_ref, lse2_ref,
             d_ref, d2_ref, sk_ref, dq_ref, dk_ref, dv_ref, dza_ref, dsk_ref, bias_ref, delta_ref):
        n = nb - 1 - pl.program_id(1)
        _fill_attn_bias(bias_ref)

        @pl.when(jnp.logical_and(pl.program_id(0) == 0, pl.program_id(1) == 0))
        def _():
            dsk_ref[...] = jnp.zeros_like(dsk_ref)
            delta_ref[...] = jnp.zeros_like(delta_ref)

        has_prev = n > 0
        has_next = n + 1 < nb

        _, gate_vjp = jax.vjp(lambda o_, z_: o_ * _silu(z_), o_ref[...], za_ref[...])
        d_o, d_za = gate_vjp(d_ref[...])
        dza_ref[...] = d_za.astype(BF16)
        d_o2 = d2_ref[...] * _silu(za2_ref[...])
        q_all, q2_all = q_ref[...], q2_ref[...]
        lse_all, lse2_all = lse_ref[...], lse2_ref[...]

        for j in range(KV_HEADS):
            js = slice(j * HEAD_DIM, (j + 1) * HEAD_DIM)
            kc, kp, vc, vp = kc_ref[:, js], kp_ref[:, js], vc_ref[:, js], vp_ref[:, js]
            bias_c, bias_p = bias_ref[j, 0], bias_ref[j, 1]
            q4 = _stack_heads(q_all, j)
            do4b = _stack_heads(d_o, j).astype(BF16)
            lse4 = _stack_columns(lse_all, j)
            pc = jnp.exp(_dot_nt(q4, kc) + bias_c - lse4)
            pp = jnp.exp(_dot_nt(q4, kp) + jnp.where(has_prev, bias_p, -jnp.inf) - lse4)
            dpc = _dot_nt(do4b, vc)
            dpp = _dot_nt(do4b, vp)
            delta = jnp.sum(pc * dpc + pp * dpp, axis=-1, keepdims=True)
            delta2 = jnp.where(has_next, delta_ref[j], 0.0)
            delta_ref[j] = delta
            dsc = (pc * (dpc - delta)).astype(BF16)
            dsp = (pp * (dpp - delta)).astype(BF16)
            dq4 = ((_dot(dsc, kc) + _dot(dsp, kp)) * ATTN_SCALE).astype(BF16)
            sink_loss = jnp.exp(_stack_columns(sk_ref[...], j) - lse4) * delta
            for g in range(Q_PER_KV):
                h = j * Q_PER_KV + g
                dq_ref[:, h * HEAD_DIM:(h + 1) * HEAD_DIM] = dq4[g * BLOCK:(g + 1) * BLOCK]
                dsk_ref[0:1, h:h + 1] -= jnp.sum(sink_loss[g * BLOCK:(g + 1) * BLOCK], axis=0, keepdims=True)
            dk = _dot_tn(dsc, q4)
            dv = _dot_tn(pc.astype(BF16), do4b)
            q4n = _stack_heads(q2_all, j)
            do4nb = _stack_heads(d_o2, j).astype(BF16)
            p2 = jnp.exp(_dot_nt(q4n, kc) + jnp.where(has_next, bias_p, -jnp.inf)
                         - _stack_columns(lse2_all, j))
            ds2 = (p2 * (_dot_nt(do4nb, vc) - delta2)).astype(BF16)
            dk = dk + _dot_tn(ds2, q4n)
            dv = dv + _dot_tn(p2.astype(BF16), do4nb)
            dk_ref[:, js] = dk.astype(BF16)
            dv_ref[:, js] = dv.astype(BF16)

    cur = lambda w: pl.BlockSpec((BLOCK, w), lambda b, s: (b * nb + nb - 1 - s, 0))
    prev = lambda w: pl.BlockSpec((BLOCK, w), lambda b, s: (b * nb + jnp.maximum(nb - 2 - s, 0), 0))
    nxt = lambda w: pl.BlockSpec((BLOCK, w), lambda b, s: (b * nb + jnp.minimum(nb - s, nb - 1), 0))
    return _call(
        body, name="attn_bwd", grid=(n_seq, nb),
        in_specs=[cur(512), nxt(512), cur(128), prev(128), cur(128), prev(128), cur(512), nxt(512),
                  cur(512), cur(N_HEADS), nxt(N_HEADS), cur(512), nxt(512), _full((1, N_HEADS))],
        out_specs=[cur(512), cur(128), cur(128), cur(512), _full((1, N_HEADS))],
        out_shape=[_sds((rows, 512), BF16), _sds((rows, 128), BF16), _sds((rows, 128), BF16),
                   _sds((rows, 512), BF16), _sds((1, N_HEADS))],
        scratch_shapes=[_ATTN_BIAS_SCRATCH, pltpu.VMEM((KV_HEADS, _GROUP_ROWS, 1), F32)],
        compiler_params=_params(32, ("arbitrary", "arbitrary")),
    )(q, q, k, k, v, v, za, za, o, lse, lse, d_ao, d_ao, sinks)


def _tail(ssm_out, attn_out, x2d, p2d, target, w_out, g2, w_gate, b_gate, w_proj):
    rows = x2d.shape[0]
    tm = 512

    def body(so_ref, ao_ref, x_ref, p_ref, t_ref, wo_ref, g2_ref, wg_ref, bg_ref, wp_ref,
             dh1_ref, dso_ref, dao_ref, dwo_ref, dwg_ref, dwp_ref, dbg_ref, dg2_ref, loss_ref):
        @pl.when(pl.program_id(0) == 0)
        def _():
            for ref in (dwo_ref, dwg_ref, dwp_ref, dbg_ref, dg2_ref, loss_ref):
                ref[...] = jnp.zeros_like(ref)

        so = so_ref[...]
        ao = ao_ref[...]
        g2 = g2_ref[...]
        mixed = _dot(so, wo_ref[0:512, :]) + _dot(ao, wo_ref[512:1024, :])
        r = lax.rsqrt(jnp.mean(mixed * mixed, axis=-1, keepdims=True) + EPS)
        mr = mixed * r
        h1 = x_ref[...] + mr * g2
        h1b = h1.astype(BF16)
        gate = jax.nn.sigmoid(_dot(h1b, wg_ref[...]) + bg_ref[...])
        pb = p_ref[...].astype(BF16)
        wp_blocks = [slice(j * D_PLE, (j + 1) * D_PLE) for j in range(N_CHIPS)]
        pp = jnp.concatenate([_dot(pb, wp_ref[blk, :]) for blk in wp_blocks], axis=1)
        err = h1 + gate * pp - t_ref[...]
        loss_ref[...] += 0.5 * jnp.sum(jnp.mean(err * err, axis=-1, keepdims=True), axis=0, keepdims=True)

        dh2 = err * (1.0 / D_MODEL)
        d_glin = dh2 * pp * gate * (1.0 - gate)
        d_glin_b = d_glin.astype(BF16)
        dwg_ref[...] += _dot_tn(h1b, d_glin_b)
        dbg_ref[...] += jnp.sum(d_glin, axis=0, keepdims=True)
        d_pp = (dh2 * gate).astype(BF16)
        for blk in wp_blocks:
            dwp_ref[blk, :] += _dot_tn(pb, d_pp[:, blk])
        dh1 = dh2 + _dot_nt(d_glin_b, wg_ref[...])
        dh1_ref[...] = dh1
        dg2_ref[...] += jnp.sum(dh1 * mr, axis=0, keepdims=True)
        a_ = dh1 * g2
        d_mixed = (r * a_ - mr * (r * jnp.mean(a_ * mr, axis=-1, keepdims=True))).astype(BF16)
        dwo_ref[0:512, :] += _dot_tn(so, d_mixed)
        dwo_ref[512:1024, :] += _dot_tn(ao, d_mixed)
        dso_ref[...] = _dot_nt(d_mixed, wo_ref[0:512, :])
        dao_ref[...] = _dot_nt(d_mixed, wo_ref[512:1024, :])

    return _call(
        body, name="tail_fwd_bwd", grid=(rows // tm,),
        in_specs=[_rows(tm, 512), _rows(tm, 512), _rows(tm, D_MODEL), _rows(tm, D_PLE), _rows(tm, D_MODEL),
                  _full((D_MODEL, D_MODEL)), _full((1, D_MODEL)), _full((D_MODEL, D_MODEL)), _full((1, D_MODEL)),
                  _full((N_CHIPS * D_PLE, D_PLE))],
        out_specs=[_rows(tm, D_MODEL), _rows(tm, 512), _rows(tm, 512), _full((D_MODEL, D_MODEL)),
                   _full((D_MODEL, D_MODEL)), _full((N_CHIPS * D_PLE, D_PLE)), _full((1, D_MODEL)), _full((1, D_MODEL)),
                   _full((1, 1))],
        out_shape=[_sds((rows, D_MODEL)), _sds((rows, 512)), _sds((rows, 512)), _sds((D_MODEL, D_MODEL)),
                   _sds((D_MODEL, D_MODEL)), _sds((N_CHIPS * D_PLE, D_PLE)), _sds((1, D_MODEL)), _sds((1, D_MODEL)),
                   _sds((1, 1))],
        compiler_params=_params(52, ("arbitrary",)),
    )(ssm_out, attn_out, x2d, p2d, target, w_out, g2, w_gate, b_gate, w_proj)


def _local_step(x, p, target, pre_norm_g, w_in_t, s5_params, ssm_d, w_glu, b_glu, sinks, w_out, post_norm_g, w_proj,
                w_gate, b_gate):
    n_seq, seq, _ = x.shape
    rows = n_seq * seq
    x2d = x.reshape(rows, D_MODEL)
    p2d = p.reshape(rows, D_PLE)
    t2d = target.reshape(rows, D_MODEL)

    l_re, l_im, bt_re, bt_im, cm_re, cm_im = _s5_params_fwd(*s5_params)

    u_scan, zs, q, k, v, za = _in_proj(x2d, pre_norm_g, w_in_t, n_seq, seq)
    y_scan, h_re, h_im = _s5_scan_fwd(u_scan, bt_re, bt_im, cm_re, cm_im, l_re, l_im, ssm_d, n_seq, seq)
    ssm_out = _glu_fwd(y_scan, zs, w_glu, b_glu, n_seq, seq)
    o, attn_out, lse = _attn_fwd(q, k, v, za, sinks, n_seq, seq)

    dh1, d_so, d_ao, d_w_out, d_w_gate, d_w_proj, d_b_gate, d_g2, loss = _tail(
        ssm_out, attn_out, x2d, p2d, t2d, w_out, post_norm_g, w_gate, b_gate, w_proj)

    dq, dk, dv, dza, d_sinks = _attn_bwd(q, k, v, za, o, lse, d_ao, sinks, n_seq, seq)
    dy_scan, dzs, d_w_glu, d_b_glu = _glu_bwd(y_scan, zs, d_so, w_glu, b_glu, n_seq, seq)
    du_scan, d_bt_re, d_bt_im, d_cm_re, d_cm_im, d_l_re, d_l_im, d_d = _s5_scan_bwd(
        dy_scan, u_scan, h_re, h_im, bt_re, bt_im, cm_re, cm_im, l_re, l_im, ssm_d, n_seq, seq)
    d_lam_re, d_lam_im, d_log_step, d_b_re, d_b_im, d_c_re, d_c_im = _s5_params_bwd(
        s5_params, (d_l_re, d_l_im, d_bt_re, d_bt_im, d_cm_re, d_cm_im))

    grad_x, d_w_in_t, d_g1 = _in_proj_bwd(x2d, dh1, pre_norm_g, w_in_t, du_scan, dzs, dq, dk, dv, dza, n_seq, seq)
    grads = dict(
        pre_norm_g=d_g1, w_in=d_w_in_t, ssm_lam_re=d_lam_re, ssm_lam_im=d_lam_im, ssm_log_step=d_log_step,
        ssm_b_re=d_b_re, ssm_b_im=d_b_im, ssm_c_re=d_c_re, ssm_c_im=d_c_im, ssm_d=d_d, ssm_w_glu=d_w_glu,
        ssm_b_glu=d_b_glu, attn_sinks=d_sinks, w_out=d_w_out, post_norm_g=d_g2, pl_w_proj=d_w_proj,
        pl_w_gate=d_w_gate, pl_b_gate=d_b_gate)
    return grad_x.reshape(x.shape), loss, grads


_BIG = ("w_in", "ssm_w_glu", "w_out", "pl_w_proj", "pl_w_gate")
_BIG_SHARD = {"w_in": (D_IN // N_CHIPS, D_MODEL), "ssm_w_glu": (D_SSM // N_CHIPS, D_SSM),
              "w_out": (D_MODEL // N_CHIPS, D_MODEL), "pl_w_proj": (D_PLE, D_MODEL // N_CHIPS),
              "pl_w_gate": (D_MODEL // N_CHIPS, D_MODEL)}
_SMALL = {"pre_norm_g": (1, D_MODEL), "ssm_lam_re": (SSM_GROUPS, SSM_STATE), "ssm_lam_im": (SSM_GROUPS, SSM_STATE),
          "ssm_log_step": (1, SSM_GROUPS), "ssm_b_re": (D_SSM, SSM_STATE), "ssm_b_im": (D_SSM, SSM_STATE),
          "ssm_c_re": (D_SSM, SSM_STATE), "ssm_c_im": (D_SSM, SSM_STATE), "ssm_d": (1, D_SSM), "ssm_b_glu": (1, D_SSM),
          "attn_sinks": (1, N_HEADS), "post_norm_g": (1, D_MODEL), "pl_b_gate": (1, D_MODEL)}
_VEC_ROWS = ("pre_norm_g", "post_norm_g", "pl_b_gate", "ssm_d", "ssm_b_glu", "attn_sinks", "ssm_log_step", "loss")
_SMALL_GROUPS = (
    ("vec", (8, D_MODEL), tuple((name, r) for r, name in enumerate(_VEC_ROWS))),
    ("lam", (2 * SSM_GROUPS, SSM_STATE), (("ssm_lam_re", 0), ("ssm_lam_im", SSM_GROUPS))),
)
_SMALL_EARLY = ("ssm_b_re", "ssm_b_im", "ssm_c_re", "ssm_c_im")
_SMALL_ORDER = tuple(name for _, _, members in _SMALL_GROUPS for name, _ in members) + _SMALL_EARLY
_WEIGHT_ORDER = ("pre_norm_g", "w_in", "ssm_lam_re", "ssm_lam_im", "ssm_log_step", "ssm_b_re", "ssm_b_im", "ssm_c_re",
                 "ssm_c_im", "ssm_d", "ssm_w_glu", "ssm_b_glu", "attn_sinks", "w_out", "post_norm_g", "pl_w_proj",
                 "pl_w_gate", "pl_b_gate")


def _small_shape(name):
    return (1, 1) if name == "loss" else _SMALL[name]


def _to_kernel_form(name, a):
    a = a[0]
    if name == "w_in":
        return a.T
    if name in ("ssm_b_re", "ssm_b_im"):
        a = a.transpose(0, 2, 1)
    return a.reshape(_SMALL[name]) if name in _SMALL else a


def _from_kernel_form(name, a, shape):
    if name == "w_in":
        a = a.T
    if name in ("ssm_b_re", "ssm_b_im"):
        a = a.reshape(SSM_GROUPS, SSM_GROUP_CH, SSM_STATE).transpose(0, 2, 1)
    return a.reshape(shape)


def _mesh_place():
    x, y, c = lax.axis_index("x"), lax.axis_index("y"), lax.axis_index("c")
    other_chips = ((1 - x, y), (x, 1 - y), (1 - x, 1 - y))
    return x, y, c, other_chips


def _gather_copies(s_refs, g_refs, send_sems, recv_sems, local_sems):
    x, y, c, other_chips = _mesh_place()
    started = []
    for i, (s_ref, g_ref) in enumerate(zip(s_refs, g_refs)):
        rows = s_ref.shape[0]
        half = rows // 2

        def block(chip, g_ref=g_ref, rows=rows, half=half):
            return g_ref.at[pl.ds((2 * chip[0] + chip[1]) * rows + c * half, half), :]

        def copy(k, chip, to, src=None, i=i, block=block):
            return pltpu.make_async_remote_copy(
                src_ref=block(chip) if src is None else src, dst_ref=block(chip), send_sem=send_sems.at[6 * i + k],
                recv_sem=recv_sems.at[6 * i + k], device_id=to, device_id_type=MESH)

        own = pltpu.make_async_copy(s_ref, g_ref.at[pl.ds((2 * x + y) * rows, rows), :], local_sems.at[i])
        own.start()
        first = [copy(k, (x, y), (*chip, c), src=s_ref.at[pl.ds(c * half, half), :])
                 for k, chip in enumerate(other_chips)]
        for cp in first:
            cp.start()
        passed = [copy(3 + k, chip, (x, y, 1 - c)) for k, chip in enumerate(other_chips)]
        started.append((own, first, passed))
    for own, first, passed in started:
        for k in range(3):
            first[k].wait_recv()
            passed[k].start()
    for own, first, passed in started:
        for k in range(3):
            passed[k].wait_recv()
        for cp in first + passed:
            cp.wait_send()
        own.wait()


def _gather_semaphores(n_t):
    return [pltpu.SemaphoreType.DMA((6 * n_t,)), pltpu.SemaphoreType.DMA((6 * n_t,)), pltpu.SemaphoreType.DMA((n_t,))]


def _gather_weights(shards):
    n_t = len(shards)

    def body(*refs):
        _gather_copies(refs[:n_t], refs[n_t:2 * n_t], *refs[2 * n_t + 1:])
        refs[2 * n_t][...] = jnp.zeros_like(refs[2 * n_t])

    any_spec = pl.BlockSpec(memory_space=pl.ANY)
    *full, done = _call(
        body, name="gather_weights", in_specs=[any_spec] * n_t,
        out_specs=[any_spec] * n_t + [pl.BlockSpec(memory_space=pltpu.VMEM)],
        out_shape=[_sds((N_CHIPS * s.shape[0], s.shape[1]), s.dtype) for s in shards]
        + [jax.ShapeDtypeStruct((8, LANES), F32)],
        scratch_shapes=_gather_semaphores(n_t),
    )(*shards)
    return full, done[0, 0]


def _gather_weights_beside(shards):
    n_t = len(shards)
    hbm = pltpu.MemorySpace.HBM
    s_refs = [jax.new_ref(s, memory_space=hbm) for s in shards]
    g_refs = [jax.empty_ref(jax.ShapeDtypeStruct((N_CHIPS * s.shape[0], s.shape[1]), s.dtype), memory_space=hbm)
              for s in shards]

    def launch(send_sems, recv_sems, local_sems):
        x, y, c, other_chips = _mesh_place()
        peers = [(*chip, c) for chip in other_chips] + [(x, y, 1 - c)]
        barrier = pltpu.get_barrier_semaphore()
        for peer in peers:
            pl.semaphore_signal(barrier, inc=1, device_id=peer, device_id_type=MESH)
        pl.semaphore_wait(barrier, len(peers))
        _gather_copies(s_refs, g_refs, send_sems, recv_sems, local_sems)

    pl.kernel(launch, mesh=plsc.ScalarSubcoreMesh(axis_name="sequencer", num_cores=1), name="gather_weights_beside",
              scratch_types=_gather_semaphores(n_t), compiler_params=pltpu.CompilerParams(collective_id=1))()
    return [g[...] for g in g_refs]


_RELATIONS = tuple(((r >> 2) & 1, (r >> 1) & 1, r & 1) for r in range(1, 8))


def _related(place, relation):
    return tuple(1 - a if flip else a for a, flip in zip(place, relation))


def _scatter_beside(mats):
    hbm = pltpu.MemorySpace.HBM
    src_refs = [jax.new_ref(a, memory_space=hbm) for a in mats]
    land_refs = [jax.empty_ref(jax.ShapeDtypeStruct((7, a.shape[0] // 8, a.shape[1]), a.dtype), memory_space=hbm)
                 for a in mats]

    def launch(send_sems, recv_sems):
        me = (lax.axis_index("x"), lax.axis_index("y"), lax.axis_index("c"))
        peers = [_related(me, rel) for rel in _RELATIONS]
        barrier = pltpu.get_barrier_semaphore()
        for peer in peers:
            pl.semaphore_signal(barrier, inc=1, device_id=peer, device_id_type=MESH)
        pl.semaphore_wait(barrier, len(peers))
        copies = []
        for i, (src, land) in enumerate(zip(src_refs, land_refs)):
            hr = land.shape[1]
            for k, (tx, ty, tc) in enumerate(peers):
                rows = pl.ds((2 * tx + ty) * 2 * hr + tc * hr, hr)
                copies.append(pltpu.make_async_remote_copy(
                    src_ref=src.at[rows, :], dst_ref=land.at[k], send_sem=send_sems.at[7 * i + k],
                    recv_sem=recv_sems.at[7 * i + k], device_id=(tx, ty, tc), device_id_type=MESH))
                copies[-1].start()
        for cp in copies:
            cp.wait()

    n_sems = 7 * len(mats)
    pl.kernel(launch, mesh=plsc.ScalarSubcoreMesh(axis_name="sequencer", num_cores=1), name="scatter_beside",
              scratch_types=[pltpu.SemaphoreType.DMA((n_sems,)), pltpu.SemaphoreType.DMA((n_sems,))],
              compiler_params=pltpu.CompilerParams(collective_id=2))()
    return [ref[...] for ref in land_refs]


def _broadcast_beside(arrays):
    hbm = pltpu.MemorySpace.HBM
    src_refs = [jax.new_ref(a, memory_space=hbm) for a in arrays]
    land_refs = [jax.empty_ref(jax.ShapeDtypeStruct((len(_RELATIONS),) + a.shape, a.dtype), memory_space=hbm)
                 for a in arrays]

    def launch(send_sems, recv_sems):
        me = (lax.axis_index("x"), lax.axis_index("y"), lax.axis_index("c"))
        peers = [_related(me, rel) for rel in _RELATIONS]
        barrier = pltpu.get_barrier_semaphore()
        for peer in peers:
            pl.semaphore_signal(barrier, inc=1, device_id=peer, device_id_type=MESH)
        pl.semaphore_wait(barrier, len(peers))
        copies = []
        for i, (src, land) in enumerate(zip(src_refs, land_refs)):
            for k, peer in enumerate(peers):
                copies.append(pltpu.make_async_remote_copy(
                    src_ref=src, dst_ref=land.at[k], send_sem=send_sems.at[7 * i + k],
                    recv_sem=recv_sems.at[7 * i + k], device_id=peer, device_id_type=MESH))
                copies[-1].start()
        for cp in copies:
            cp.wait()

    n_sems = 7 * len(arrays)
    pl.kernel(launch, mesh=plsc.ScalarSubcoreMesh(axis_name="sequencer", num_cores=1), name="broadcast_beside",
              scratch_types=[pltpu.SemaphoreType.DMA((n_sems,)), pltpu.SemaphoreType.DMA((n_sems,))],
              compiler_params=pltpu.CompilerParams(collective_id=3))()
    return [ref[...] for ref in land_refs]


def _exchange_grads(big, small, landed, landed_small):
    n_t = len(big)
    n_g = len(_SMALL_GROUPS)
    names = _SMALL_ORDER
    halves = [(b.shape[0] // N_CHIPS // 2, b.shape[1]) for b in big]
    early = sorted(landed)
    late = [i for i in range(n_t) if i not in landed]
    n_sems = 4 * n_g + 7 * len(late) + n_t
    small_sem0, block_sem0 = n_t, n_t + len(names)
    early_sem0 = block_sem0 + N_CHIPS * len(late)
    landed_sem0 = early_sem0 + 2 * len(early)
    early_small = [n for n in names if n in landed_small]

    def body(*refs):
        pos = 0

        def take(n):
            nonlocal pos
            pos += n
            return refs[pos - n:pos]

        big_refs, small_refs = take(n_t), dict(zip(names, take(len(names))))
        land_refs = dict(zip(early, take(len(early))))
        land_small_refs = dict(zip(early_small, take(len(early_small))))
        out_refs, small_out_refs = take(n_t), dict(zip(names, take(len(names))))
        per_late = lambda: dict(zip(late, take(len(late))))
        ga, gb, pme, send_b, recv_b = per_late(), per_late(), take(n_t), per_late(), per_late()
        own_e, land_e = dict(zip(early, take(len(early)))), dict(zip(early, take(len(early))))
        land_s = dict(zip(early_small, take(len(early_small))))
        s_own, s_sib, s_chips, s_pair = take(n_g), take(n_g), take(n_g), take(n_g)
        stage = dict(zip(names, take(len(names))))
        send_sems, recv_sems, local_sems = take(3)
        x, y, c, other_chips = _mesh_place()
        me = 2 * x + y
        sibling = (x, y, 1 - c)
        sem_at = iter(range(n_sems))

        def remote(src, dst, to):
            k = next(sem_at)
            return pltpu.make_async_remote_copy(src_ref=src, dst_ref=dst, send_sem=send_sems.at[k],
                                                recv_sem=recv_sems.at[k], device_id=to, device_id_type=MESH)

        loads = [pltpu.make_async_copy(small_refs[name], stage[name], local_sems.at[small_sem0 + a])
                 for a, name in enumerate(names)]
        landed_loads = [pltpu.make_async_copy(land_small_refs[name], land_s[name], local_sems.at[landed_sem0 + a])
                        for a, name in enumerate(early_small)]
        for cp in loads + landed_loads:
            cp.start()
        for cp in loads:
            cp.wait()
        small_swaps = []
        for gi, (_, _, members) in enumerate(_SMALL_GROUPS):
            s_own[gi][...] = jnp.zeros_like(s_own[gi])
            for name, r0 in members:
                r, n = _small_shape(name)
                s_own[gi][r0:r0 + r, 0:n] = stage[name][...]
            small_swaps.append(remote(s_own[gi], s_sib[gi], sibling))
            small_swaps[gi].start()
        order = sorted(late, key=lambda i: halves[i][0] * halves[i][1])
        own_loads, big_swaps = {}, {}
        for i in order:
            hr = halves[i][0]
            own_loads[i], big_swaps[i] = [], []
            for j in range(N_CHIPS):
                mine = big_refs[i].at[pl.ds(j * 2 * hr + c * hr, hr), :]
                theirs = big_refs[i].at[pl.ds(j * 2 * hr + (1 - c) * hr, hr), :]
                sem = local_sems.at[block_sem0 + N_CHIPS * late.index(i) + j]
                own_loads[i].append(pltpu.make_async_copy(mine, ga[i].at[j], sem))
                own_loads[i][j].start()
                big_swaps[i].append(remote(theirs, gb[i].at[j], sibling))
                big_swaps[i][j].start()
        early_loads = {}
        for e, i in enumerate(early):
            hr = halves[i][0]
            mine = big_refs[i].at[pl.ds(me * 2 * hr + c * hr, hr), :]
            early_loads[i] = [pltpu.make_async_copy(mine, own_e[i], local_sems.at[early_sem0 + 2 * e]),
                              pltpu.make_async_copy(land_refs[i], land_e[i], local_sems.at[early_sem0 + 2 * e + 1])]
            for cp in early_loads[i]:
                cp.start()
        small_sends = []
        for gi in range(n_g):
            small_swaps[gi].wait_recv()
            s_pair[gi][...] = s_own[gi][...] + s_sib[gi][...]
            small_sends.append([remote(s_pair[gi], s_chips[gi].at[k], (*chip, c)) for k, chip in enumerate(other_chips)])
            for cp in small_sends[gi]:
                cp.start()

        def pair_sum(i, j):
            return ga[i][j] + gb[i][j]

        big_sends = {}
        for i in order:
            for j in range(N_CHIPS):
                own_loads[i][j].wait()
                big_swaps[i][j].wait_recv()
            big_sends[i] = []
            for k, chip in enumerate(other_chips):
                send_b[i][k] = pair_sum(i, 2 * chip[0] + chip[1]).astype(BF16)
                big_sends[i].append(remote(send_b[i].at[k], recv_b[i].at[k], (*chip, c)))
                big_sends[i][k].start()
        last_swaps, keeps = {}, {}
        for i in early + order:
            hr = halves[i][0]
            if i in landed:
                for cp in early_loads[i]:
                    cp.wait()
                total = own_e[i][...]
                for k in range(len(_RELATIONS)):
                    total = total + land_e[i][k]
                pme[i][...] = total
            else:
                for k in range(3):
                    big_sends[i][k].wait_recv()
                pme[i][...] = ((pair_sum(i, me) + recv_b[i][0].astype(F32)) + recv_b[i][1].astype(F32)) + recv_b[i][2].astype(F32)
            mine = out_refs[i].at[pl.ds(c * hr, hr), :]
            keeps[i] = pltpu.make_async_copy(pme[i], mine, local_sems.at[i])
            keeps[i].start()
            last_swaps[i] = remote(pme[i], mine, sibling)
            last_swaps[i].start()

        for gi, (_, _, members) in enumerate(_SMALL_GROUPS):
            for k in range(3):
                small_sends[gi][k].wait_recv()
            total = None
            for j in range(N_CHIPS):
                rel = jnp.bitwise_xor(j, me)
                term = jnp.where(rel == 0, s_pair[gi][...], jnp.where(
                    rel == 2, s_chips[gi][0], jnp.where(rel == 1, s_chips[gi][1], s_chips[gi][2])))
                total = term if total is None else total + term
            s_sib[gi][...] = total
            for name, r0 in members:
                r, n = _small_shape(name)
                stage[name][...] = s_sib[gi][r0:r0 + r, 0:n]
        my_index = 4 * x + 2 * y + c
        for cp in landed_loads:
            cp.wait()
        for name in early_small:
            total = None
            for d in range(2 * N_CHIPS):
                rel = jnp.bitwise_xor(d, my_index)
                term = stage[name][...]
                for k in range(len(_RELATIONS)):
                    term = jnp.where(rel == k + 1, land_s[name][k], term)
                total = term if total is None else total + term
            stage[name][...] = total
        stores = [pltpu.make_async_copy(stage[name], small_out_refs[name], local_sems.at[small_sem0 + a])
                  for a, name in enumerate(names)]
        for cp in stores:
            cp.start()

        for i in range(n_t):
            last_swaps[i].wait_recv()
            keeps[i].wait()
        for cp in stores:
            cp.wait()
        groups = list(big_swaps.values()) + small_sends + list(big_sends.values())
        for cp in small_swaps + [cp for group in groups for cp in group] + list(last_swaps.values()):
            cp.wait_send()

    any_spec = pl.BlockSpec(memory_space=pl.ANY)
    small_shapes = [_sds(_small_shape(n)) for n in names]
    group_shapes = [shape for _, shape, _ in _SMALL_GROUPS]
    vmem = lambda which, dtype, lead=(): [pltpu.VMEM(lead + halves[i], dtype) for i in which]
    outs = _call(
        body, name="exchange_grads",
        in_specs=[any_spec] * (n_t + len(names) + len(early) + len(early_small)),
        out_specs=[any_spec] * (n_t + len(names)),
        out_shape=[_sds((b.shape[0] // N_CHIPS, b.shape[1])) for b in big] + small_shapes,
        scratch_shapes=(vmem(late, F32, (N_CHIPS,)) + vmem(late, F32, (N_CHIPS,)) + vmem(range(n_t), F32)
                        + vmem(late, BF16, (3,)) + vmem(late, BF16, (3,))
                        + vmem(early, F32) + vmem(early, F32, (len(_RELATIONS),))
                        + [pltpu.VMEM((len(_RELATIONS),) + _small_shape(n), F32) for n in early_small]
                        + [pltpu.VMEM(s, F32) for s in group_shapes] * 2 + [pltpu.VMEM((3,) + s, F32) for s in group_shapes]
                        + [pltpu.VMEM(s, F32) for s in group_shapes]
                        + [pltpu.VMEM(_small_shape(n), F32) for n in names]
                        + [pltpu.SemaphoreType.DMA((n_sems,)), pltpu.SemaphoreType.DMA((n_sems,)),
                           pltpu.SemaphoreType.DMA((landed_sem0 + len(early_small),))]),
        compiler_params=_params(48),
    )(*big, *[small[n] for n in names], *[landed[i] for i in early], *[landed_small[n] for n in early_small])
    return list(outs[:n_t]), dict(zip(names, outs[n_t:n_t + len(names)]))


def _adamw_update(w, g, m, v):
    m = ADAM_B1 * m + (1.0 - ADAM_B1) * g
    v = ADAM_B2 * v + (1.0 - ADAM_B2) * (g * g)
    m_hat = m / (1.0 - ADAM_B1 ** ADAM_STEP)
    v_hat = v / (1.0 - ADAM_B2 ** ADAM_STEP)
    return -ADAM_LR * (m_hat / (jnp.sqrt(v_hat) + ADAM_EPS) + ADAM_WD * w), m, v


def _adamw(w, g, m, v, grid, name):
    n_t = len(w)

    def body(*refs):
        ins, outs = refs[:4 * n_t], refs[4 * n_t:]
        for i in range(n_t):
            w_, g_, m_, v_ = [ins[a * n_t + i][...] for a in range(4)]
            vals = (g_,) + _adamw_update(w_, g_, m_, v_)
            for a in range(4):
                outs[a * n_t + i][...] = vals[a]

    specs = [pl.BlockSpec((a.shape[0] // grid, a.shape[1]), lambda i: (i, 0)) for a in w]
    shapes = [_sds(a.shape) for a in w]
    outs = _call(
        body, name=name, grid=(grid,), in_specs=specs * 4, out_specs=specs * 4, out_shape=shapes * 4,
        compiler_params=_params(40, ("arbitrary",)),
    )(*w, *g, *m, *v)
    return [outs[a * n_t:(a + 1) * n_t] for a in range(4)]


def kernel(x, p, pre_norm_g, w_in, ssm_lam_re, ssm_lam_im, ssm_log_step, ssm_b_re, ssm_b_im, ssm_c_re, ssm_c_im, ssm_d, ssm_w_glu, ssm_b_glu, attn_sinks, w_out, post_norm_g, pl_w_proj, pl_w_gate, pl_b_gate, loss_target, m_pre_norm_g, m_w_in, m_ssm_lam_re, m_ssm_lam_im, m_ssm_log_step, m_ssm_b_re, m_ssm_b_im, m_ssm_c_re, m_ssm_c_im, m_ssm_d, m_ssm_w_glu, m_ssm_b_glu, m_attn_sinks, m_w_out, m_post_norm_g, m_pl_w_proj, m_pl_w_gate, m_pl_b_gate, v_pre_norm_g, v_w_in, v_ssm_lam_re, v_ssm_lam_im, v_ssm_log_step, v_ssm_b_re, v_ssm_b_im, v_ssm_c_re, v_ssm_c_im, v_ssm_d, v_ssm_w_glu, v_ssm_b_glu, v_attn_sinks, v_w_out, v_post_norm_g, v_pl_w_proj, v_pl_w_gate, v_pl_b_gate):
    weights = dict(pre_norm_g=pre_norm_g, w_in=w_in, ssm_lam_re=ssm_lam_re, ssm_lam_im=ssm_lam_im,
                   ssm_log_step=ssm_log_step, ssm_b_re=ssm_b_re, ssm_b_im=ssm_b_im, ssm_c_re=ssm_c_re,
                   ssm_c_im=ssm_c_im, ssm_d=ssm_d, ssm_w_glu=ssm_w_glu, ssm_b_glu=ssm_b_glu, attn_sinks=attn_sinks,
                   w_out=w_out, post_norm_g=post_norm_g, pl_w_proj=pl_w_proj, pl_w_gate=pl_w_gate, pl_b_gate=pl_b_gate)
    m_in = dict(pre_norm_g=m_pre_norm_g, w_in=m_w_in, ssm_lam_re=m_ssm_lam_re, ssm_lam_im=m_ssm_lam_im,
                ssm_log_step=m_ssm_log_step, ssm_b_re=m_ssm_b_re, ssm_b_im=m_ssm_b_im, ssm_c_re=m_ssm_c_re,
                ssm_c_im=m_ssm_c_im, ssm_d=m_ssm_d, ssm_w_glu=m_ssm_w_glu, ssm_b_glu=m_ssm_b_glu,
                attn_sinks=m_attn_sinks, w_out=m_w_out, post_norm_g=m_post_norm_g, pl_w_proj=m_pl_w_proj,
                pl_w_gate=m_pl_w_gate, pl_b_gate=m_pl_b_gate)
    v_in = dict(pre_norm_g=v_pre_norm_g, w_in=v_w_in, ssm_lam_re=v_ssm_lam_re, ssm_lam_im=v_ssm_lam_im,
                ssm_log_step=v_ssm_log_step, ssm_b_re=v_ssm_b_re, ssm_b_im=v_ssm_b_im, ssm_c_re=v_ssm_c_re,
                ssm_c_im=v_ssm_c_im, ssm_d=v_ssm_d, ssm_w_glu=v_ssm_w_glu, ssm_b_glu=v_ssm_b_glu,
                attn_sinks=v_attn_sinks, w_out=v_w_out, post_norm_g=v_post_norm_g, pl_w_proj=v_pl_w_proj,
                pl_w_gate=v_pl_w_gate, pl_b_gate=v_pl_b_gate)

    def two_d(tree):
        return {k: _to_kernel_form(k, a) for k, a in tree.items()}

    w2, m2, v2 = two_d(weights), two_d(m_in), two_d(v_in)

    (w_in_full,), gathered = _gather_weights([w2["w_in"].astype(BF16)])
    rest = _gather_weights_beside([(w2[n] + gathered).astype(BF16) for n in _BIG[1:]])
    full = dict(zip(_BIG, [w_in_full] + rest))
    s5_params = tuple(w2[n] for n in ("ssm_lam_re", "ssm_lam_im", "ssm_log_step", "ssm_b_re", "ssm_b_im", "ssm_c_re",
                                      "ssm_c_im"))
    grad_x, loss, grads = _local_step(
        x, p, loss_target, w2["pre_norm_g"], full["w_in"], s5_params, w2["ssm_d"], full["ssm_w_glu"], w2["ssm_b_glu"],
        w2["attn_sinks"], full["w_out"], w2["post_norm_g"], full["pl_w_proj"], full["pl_w_gate"], w2["pl_b_gate"])

    sent_early = ("w_out", "pl_w_gate", "pl_w_proj")
    landed = dict(zip([_BIG.index(n) for n in sent_early], _scatter_beside([grads[n] for n in sent_early])))
    after_scatter = landed[_BIG.index(sent_early[-1])][0, 0, 0] * 0.0
    landed_small = dict(zip(_SMALL_EARLY, _broadcast_beside([grads[n] + after_scatter for n in _SMALL_EARLY])))
    g_big, g_small = _exchange_grads([grads[n] for n in _BIG], {**{n: grads[n] for n in _SMALL}, "loss": loss}, landed,
                                     landed_small)
    g_big = dict(zip(_BIG, g_big))
    total_loss = g_small.pop("loss")

    big_out = _adamw([w2[n] for n in _BIG], [g_big[n] for n in _BIG], [m2[n] for n in _BIG], [v2[n] for n in _BIG],
                     8, "adamw_matrices")
    small_names = tuple(_SMALL)
    small_out = _adamw([w2[n] for n in small_names], [g_small[n] for n in small_names], [m2[n] for n in small_names],
                       [v2[n] for n in small_names], 1, "adamw_small")

    results = [{**dict(zip(_BIG, big_part)), **dict(zip(small_names, small_part))}
               for big_part, small_part in zip(big_out, small_out)]
    flat = [_from_kernel_form(name, r[name], weights[name].shape) for r in results for name in _WEIGHT_ORDER]
    return (total_loss.reshape(()), grad_x, *flat)
```

```python
import math

import jax
import jax.numpy as jnp
from jax import lax
from jax.experimental import pallas as pl
from jax.experimental.pallas import tpu as pltpu
from jax.experimental.pallas import tpu_sc as plsc

F32 = jnp.float32
BF16 = jnp.bfloat16

D_MODEL = 1024
D_SSM = 512
D_ATTN = 512
SSM_GROUPS = 32
SSM_GROUP_CH = 16
SSM_STATE = 64
SSM_LANES = SSM_GROUPS * SSM_STATE
HEAD_DIM = 64
N_HEADS = 8
KV_HEADS = 2
Q_PER_KV = 4
WINDOW = 128
BLOCK = 128
D_PLE = 256
D_IN = 2304
EPS = 1e-6
ATTN_SCALE = 1.0 / math.sqrt(HEAD_DIM)

ADAM_LR = 0.001
ADAM_B1 = 0.9
ADAM_B2 = 0.999
ADAM_EPS = 1e-08
ADAM_WD = 0.01
ADAM_STEP = 10

N_CHIPS = 4
LANES = 128
SCAN_CHUNKS = 8
SCAN_TILE_STEPS = 32
SCAN_LANE_CHUNK = 512
MIB = 2 ** 20
MESH = pl.DeviceIdType.MESH


def _dot(a, b):
    return jnp.dot(a, b, preferred_element_type=F32)


def _dot_nt(a, b):
    return lax.dot_general(a, b, (((1,), (1,)), ((), ())), preferred_element_type=F32)


def _dot_tn(a, b):
    return lax.dot_general(a, b, (((0,), (0,)), ((), ())), preferred_element_type=F32)


def _params(vmem_mib, semantics=None):
    kw = dict(vmem_limit_bytes=vmem_mib * MIB)
    if semantics is not None:
        kw["dimension_semantics"] = semantics
    return pltpu.CompilerParams(**kw)


def _full(shape):
    nd = len(shape)
    return pl.BlockSpec(shape, lambda *_: (0,) * nd, pipeline_mode=pl.Buffered(1))


def _rows(tm, width):
    return pl.BlockSpec((tm, width), lambda i: (i, 0))


def _sds(shape, dtype=F32):
    return pltpu.HBM(shape, dtype)


def _call(body, **kw):
    fn = pl.pallas_call(body, **kw)
    return lambda *args: fn(*[pltpu.with_memory_space_constraint(a, pltpu.HBM) for a in args])


def _silu(z):
    return z * jax.nn.sigmoid(z)


def _in_proj(x2d, g1, w_in_t, n_seq, seq):
    rows = x2d.shape[0]
    tm = 512
    slab, steps, _, _ = _scan_geometry(n_seq, seq)

    def body(x_ref, g_ref, w_ref, *out_refs):
        u_parts, (zs_ref, q_ref, k_ref, v_ref, za_ref) = out_refs[:_SCAN_PARTS], out_refs[_SCAN_PARTS:]
        x = x_ref[...]
        r = lax.rsqrt(jnp.mean(x * x, axis=-1, keepdims=True) + EPS)
        hn = (x * r * g_ref[...]).astype(BF16)

        def proj(a, b):
            return _dot_nt(hn, w_ref[a:b, :])

        _store_chunks(u_parts, pl.program_id(0) * (tm // steps), proj(0, 512), steps, slab)
        zs_ref[...] = proj(512, 1024)
        q_ref[...] = (proj(1024, 1536) * ATTN_SCALE).astype(BF16)
        k_ref[...] = proj(1536, 1664).astype(BF16)
        v_ref[...] = proj(1664, 1792).astype(BF16)
        za_ref[...] = proj(1792, 2304)

    *u_parts, zs, q, k, v, za = _call(
        body, name="in_proj", grid=(rows // tm,),
        in_specs=[_rows(tm, D_MODEL), _full((1, D_MODEL)), _full((D_IN, D_MODEL))],
        out_specs=_whole_parts(rows) + [_rows(tm, 512), _rows(tm, 512), _rows(tm, 128), _rows(tm, 128), _rows(tm, 512)],
        out_shape=_part_shapes(rows) + [_sds((rows, 512)), _sds((rows, 512), BF16), _sds((rows, 128), BF16),
                                        _sds((rows, 128), BF16), _sds((rows, 512))],
        compiler_params=_params(48, ("arbitrary",)),
    )(x2d, g1, w_in_t)
    return u_parts, zs, q, k, v, za


def _in_proj_bwd(x2d, dh1, g1, w_in_t, du_parts, dzs, dq, dk, dv, dza, n_seq, seq):
    rows = x2d.shape[0]
    tm = 256
    slab, steps, _, _ = _scan_geometry(n_seq, seq)
    pieces = ((0, 512), (512, 1024), (1024, 1536), (1536, 1664), (1664, 1792), (1792, 2304))

    def body(x_ref, dh1_ref, g_ref, w_ref, *refs):
        du_parts, (dzs_ref, dq_ref, dk_ref, dv_ref, dza_ref, gx_ref, dw_ref, dg_ref) = refs[:_SCAN_PARTS], refs[_SCAN_PARTS:]

        @pl.when(pl.program_id(0) == 0)
        def _():
            dw_ref[...] = jnp.zeros_like(dw_ref)
            dg_ref[...] = jnp.zeros_like(dg_ref)

        x = x_ref[...]
        g = g_ref[...]
        r = lax.rsqrt(jnp.mean(x * x, axis=-1, keepdims=True) + EPS)
        xr = x * r
        hn = (xr * g).astype(BF16)
        dhn = jnp.zeros((tm, D_MODEL), F32)
        du = _load_chunks(du_parts, pl.program_id(0) * (tm // steps), tm // steps, steps, slab)
        for (a, b), piece in zip(pieces, (du, dzs_ref[...], dq_ref[...], dk_ref[...], dv_ref[...], dza_ref[...])):
            piece = piece.astype(BF16)
            dhn = dhn + _dot(piece, w_ref[a:b, :])
            dw_ref[a:b, :] += _dot_tn(piece, hn)
        dg_ref[...] += jnp.sum(dhn * xr, axis=0, keepdims=True)
        a_ = dhn * g
        gx_ref[...] = dh1_ref[...] + r * a_ - xr * (r * jnp.mean(a_ * xr, axis=-1, keepdims=True))

    return _call(
        body, name="in_proj_bwd", grid=(rows // tm,),
        in_specs=[_rows(tm, D_MODEL), _rows(tm, D_MODEL), _full((1, D_MODEL)), _full((D_IN, D_MODEL))]
        + _whole_parts(rows) + [_rows(tm, 512), _rows(tm, 512), _rows(tm, 128), _rows(tm, 128), _rows(tm, 512)],
        out_specs=[_rows(tm, D_MODEL), _full((D_IN, D_MODEL)), _full((1, D_MODEL))],
        out_shape=[_sds((rows, D_MODEL)), _sds((D_IN, D_MODEL)), _sds((1, D_MODEL))],
        compiler_params=_params(52, ("arbitrary",)),
    )(x2d, dh1, g1, w_in_t, *du_parts, dzs, dq, dk, dv, dza)


def _iota(shape, axis):
    return lax.broadcasted_iota(jnp.int32, shape, axis)


def _exact_dot(a, b):
    return jnp.dot(a, b, precision=lax.Precision.HIGHEST, preferred_element_type=F32)


_HALF_GROUPS = SSM_GROUPS // 2
_N_SHIFT = SSM_STATE.bit_length() - 1
_P_SHIFT = SSM_GROUP_CH.bit_length() - 1


def _s5_operands(lam_re, lam_im, log_step, b_re, b_im, c_re, c_im):
    g, n, p = SSM_GROUPS, SSM_STATE, SSM_GROUP_CH
    gn, gp, hn_, hp = g * n, g * p, _HALF_GROUPS * n, _HALF_GROUPS * p
    eye_g = _iota((g, g), 0) == _iota((g, g), 1)
    step = jnp.sum(jnp.where(eye_g, jnp.exp(log_step), 0.0), axis=1, keepdims=True)
    a_re = lam_re * step
    a_im = lam_im * step
    mag = jnp.exp(a_re)
    lbar_re = mag * jnp.cos(a_im)
    lbar_im = mag * jnp.sin(a_im)
    n_re = lbar_re - 1.0
    den = lam_re * lam_re + lam_im * lam_im
    f_re = (n_re * lam_re + lbar_im * lam_im) / den
    f_im = (lbar_im * lam_re - n_re * lam_im) / den

    spread_n = (_iota((n, gn), 0) == (_iota((n, gn), 1) & (n - 1))).astype(F32)
    own_g = _iota((g, gn), 0) == (_iota((g, gn), 1) >> _N_SHIFT)

    def to_row(a):
        return jnp.sum(jnp.where(own_g, _exact_dot(a, spread_n), 0.0), axis=0, keepdims=True)

    per_group = ((_iota((gp, g), 0) >> _P_SHIFT) == _iota((gp, g), 1)).astype(F32)
    fx_re, fx_im = _exact_dot(per_group, f_re), _exact_dot(per_group, f_im)
    bbar_re = fx_re * b_re - fx_im * b_im
    bbar_im = fx_re * b_im + fx_im * b_re

    tile_n = (_iota((n, hn_), 0) == (_iota((n, hn_), 1) & (n - 1))).astype(F32)
    same_group = (_iota((hp, hn_), 0) >> _P_SHIFT) == (_iota((hp, hn_), 1) >> _N_SHIFT)

    def embed(a, hf):
        return jnp.where(same_group, _exact_dot(a[hf * hp:(hf + 1) * hp], tile_n), 0.0)

    return (to_row(lbar_re), to_row(lbar_im), embed(bbar_re, 0), embed(bbar_re, 1), embed(bbar_im, 0),
            embed(bbar_im, 1), embed(c_re, 0), embed(c_re, 1), embed(c_im, 0), embed(c_im, 1))


_S5_PARAM_SHAPES = ((SSM_GROUPS, SSM_STATE), (SSM_GROUPS, SSM_STATE), (1, SSM_GROUPS),
                    (D_SSM, SSM_STATE), (D_SSM, SSM_STATE), (D_SSM, SSM_STATE), (D_SSM, SSM_STATE))
_CM_SHAPE = (2, _HALF_GROUPS * SSM_GROUP_CH, _HALF_GROUPS * SSM_STATE)
_S5_OPERAND_SHAPES = ((1, SSM_LANES), (1, SSM_LANES), _CM_SHAPE, _CM_SHAPE, _CM_SHAPE, _CM_SHAPE)


def _s5_params_fwd(*params):
    def body(*refs):
        ins, (lre_ref, lim_ref, btre_ref, btim_ref, cmre_ref, cmim_ref) = refs[:7], refs[7:]
        vals = _s5_operands(*[r[...] for r in ins])
        lre_ref[...] = vals[0]
        lim_ref[...] = vals[1]
        for ref, pair in zip((btre_ref, btim_ref, cmre_ref, cmim_ref), (vals[2:4], vals[4:6], vals[6:8], vals[8:10])):
            ref[0] = pair[0].astype(BF16)
            ref[1] = pair[1].astype(BF16)

    dtypes = (F32, F32, BF16, BF16, BF16, BF16)
    return _call(
        body, name="s5_params_fwd",
        in_specs=[_full(s) for s in _S5_PARAM_SHAPES], out_specs=[_full(s) for s in _S5_OPERAND_SHAPES],
        out_shape=[_sds(s, d) for s, d in zip(_S5_OPERAND_SHAPES, dtypes)], compiler_params=_params(32),
    )(*params)


def _s5_params_bwd(params, cotangents):
    def body(*refs):
        ins, (dlre, dlim, dbtre, dbtim, dcmre, dcmim), outs = refs[:7], refs[7:13], refs[13:]
        _, vjp = jax.vjp(_s5_operands, *[r[...] for r in ins])
        cts = (dlre[...], dlim[...], dbtre[0], dbtre[1], dbtim[0], dbtim[1], dcmre[0], dcmre[1], dcmim[0], dcmim[1])
        for ref, val in zip(outs, vjp(cts)):
            ref[...] = val

    return _call(
        body, name="s5_params_bwd",
        in_specs=[_full(s) for s in _S5_PARAM_SHAPES + _S5_OPERAND_SHAPES],
        out_specs=[_full(s) for s in _S5_PARAM_SHAPES],
        out_shape=[_sds(s) for s in _S5_PARAM_SHAPES], compiler_params=_params(48),
    )(*params, *cotangents)


def _scan_geometry(n_seq, seq):
    slab = n_seq * SCAN_CHUNKS
    steps = seq // SCAN_CHUNKS
    tile_rows = slab * SCAN_TILE_STEPS
    n_tiles = steps // SCAN_TILE_STEPS
    return slab, steps, tile_rows, n_tiles


_SCAN_PARTS = D_SSM // LANES


def _whole_parts(rows):
    return [_full((rows, LANES))] * _SCAN_PARTS


def _part_shapes(rows):
    return [_sds((rows, LANES))] * _SCAN_PARTS


def _load_chunks(parts, first_chunk, n_chunks, steps, slab):
    return jnp.concatenate([
        jnp.concatenate([ref[pl.ds(first_chunk + q, steps, stride=slab), :] for ref in parts], axis=1)
        for q in range(n_chunks)], axis=0)


def _store_chunks(parts, first_chunk, value, steps, slab):
    for q in range(value.shape[0] // steps):
        for j, ref in enumerate(parts):
            ref[pl.ds(first_chunk + q, steps, stride=slab), :] = value[q * steps:(q + 1) * steps,
                                                                     j * LANES:(j + 1) * LANES]


def _join_parts(parts):
    return jnp.concatenate([ref[...] for ref in parts], axis=1)


def _split_parts(parts, value):
    for j, ref in enumerate(parts):
        ref[...] = value[:, j * LANES:(j + 1) * LANES]


def _complex_power(re, im, n):
    out = None
    while n:
        if n & 1:
            out = (re, im) if out is None else (out[0] * re - out[1] * im, out[0] * im + out[1] * re)
        n >>= 1
        if n:
            re, im = re * re - im * im, 2.0 * re * im
    return out


def _chunk_carry(sum_re, sum_im, carry_re, carry_im, a_re, a_im, n_seq, reverse):
    carry_re[...] = jnp.zeros_like(carry_re)
    carry_im[...] = jnp.zeros_like(carry_im)
    for s in range(n_seq):
        order = range(SCAN_CHUNKS - 2, -1, -1) if reverse else range(1, SCAN_CHUNKS)
        for c in order:
            r = s * SCAN_CHUNKS + c
            p = r + 1 if reverse else r - 1
            p_re, p_im = carry_re[p:p + 1, :], carry_im[p:p + 1, :]
            carry_re[r:r + 1, :] = a_re * p_re - a_im * p_im + sum_re[p:p + 1, :]
            carry_im[r:r + 1, :] = a_re * p_im + a_im * p_re + sum_im[p:p + 1, :]


def _s5_scan_fwd(u_parts, bt_re, bt_im, cm_re, cm_im, lbar_re, lbar_im, d_row, n_seq, seq):
    slab, steps, tile_rows, n_tiles = _scan_geometry(n_seq, seq)
    rows = u_parts[0].shape[0]

    def body(*refs):
        u_refs, refs = refs[:_SCAN_PARTS], refs[_SCAN_PARTS:]
        (bre_ref, bim_ref, cre_ref, cim_ref, lre_ref, lim_ref, d_ref), refs = refs[:7], refs[7:]
        y_refs, (hre_ref, him_ref, st_re, st_im, h0_re, h0_im, buf_re, buf_im) = refs[:_SCAN_PARTS], refs[_SCAN_PARTS:]
        second = pl.program_id(0) == 1
        i = pl.program_id(1)

        @pl.when(jnp.logical_and(i == 0, jnp.logical_not(second)))
        def _():
            st_re[...] = jnp.zeros_like(st_re)
            st_im[...] = jnp.zeros_like(st_im)

        u = _join_parts(u_refs)
        ub = u.astype(BF16)
        for hf in range(2):
            cols = slice(hf * 1024, (hf + 1) * 1024)
            buf_re[:, cols] = _dot(ub[:, hf * 256:(hf + 1) * 256], bre_ref[hf])
            buf_im[:, cols] = _dot(ub[:, hf * 256:(hf + 1) * 256], bim_ref[hf])

        for lc in range(SSM_LANES // SCAN_LANE_CHUNK):
            cols = slice(lc * SCAN_LANE_CHUNK, (lc + 1) * SCAN_LANE_CHUNK)
            l_re = jnp.broadcast_to(lre_ref[:, cols], (slab, SCAN_LANE_CHUNK))
            l_im = jnp.broadcast_to(lim_ref[:, cols], (slab, SCAN_LANE_CHUNK))

            def scan_tile(keep_states):
                def step(t, carry):
                    s_re, s_im = carry
                    r0 = pl.multiple_of(t * slab, slab)
                    n_re = l_re * s_re - l_im * s_im + buf_re[pl.ds(r0, slab), cols]
                    n_im = l_re * s_im + l_im * s_re + buf_im[pl.ds(r0, slab), cols]
                    if keep_states:
                        buf_re[pl.ds(r0, slab), cols] = n_re
                        buf_im[pl.ds(r0, slab), cols] = n_im
                    return n_re, n_im

                s_re, s_im = lax.fori_loop(0, SCAN_TILE_STEPS, step, (st_re[:, cols], st_im[:, cols]), unroll=True)
                st_re[:, cols] = s_re
                st_im[:, cols] = s_im

            pl.when(jnp.logical_not(second))(lambda: scan_tile(False))
            pl.when(second)(lambda: scan_tile(True))

        @pl.when(jnp.logical_and(i == n_tiles - 1, jnp.logical_not(second)))
        def _():
            a_re, a_im = _complex_power(lre_ref[...], lim_ref[...], steps)
            _chunk_carry(st_re, st_im, h0_re, h0_im, a_re, a_im, n_seq, reverse=False)
            st_re[...] = h0_re[...]
            st_im[...] = h0_im[...]

        @pl.when(second)
        def _():
            h_re = buf_re[...].astype(BF16)
            h_im = buf_im[...].astype(BF16)
            hre_ref[...] = h_re
            him_ref[...] = h_im
            for hf in range(2):
                cols = slice(hf * 1024, (hf + 1) * 1024)
                ycols = slice(hf * 256, (hf + 1) * 256)
                y_half = (_dot_nt(h_re[:, cols], cre_ref[hf]) - _dot_nt(h_im[:, cols], cim_ref[hf])
                          + d_ref[:, ycols] * u[:, ycols])
                _split_parts(y_refs[2 * hf:2 * hf + 2], y_half)

    tile = lambda w: pl.BlockSpec((tile_rows, w), lambda p, i: (i, 0))
    out_tile = lambda w: pl.BlockSpec((tile_rows, w), lambda p, i: (i * p, 0))
    cm = _full(_CM_SHAPE)
    outs = _call(
        body, name="s5_scan_fwd", grid=(2, n_tiles),
        in_specs=[tile(LANES)] * _SCAN_PARTS + [cm, cm, cm, cm, _full((1, SSM_LANES)), _full((1, SSM_LANES)),
                                                _full((1, 512))],
        out_specs=[out_tile(LANES)] * _SCAN_PARTS + [out_tile(SSM_LANES), out_tile(SSM_LANES)],
        out_shape=_part_shapes(rows) + [_sds((rows, SSM_LANES), BF16), _sds((rows, SSM_LANES), BF16)],
        scratch_shapes=[pltpu.VMEM((slab, SSM_LANES), F32)] * 4 + [pltpu.VMEM((tile_rows, SSM_LANES), F32)] * 2,
        compiler_params=_params(40, ("arbitrary", "arbitrary")),
    )(*u_parts, bt_re, bt_im, cm_re, cm_im, lbar_re, lbar_im, d_row)
    return outs[:_SCAN_PARTS], outs[_SCAN_PARTS], outs[_SCAN_PARTS + 1]


def _s5_scan_bwd(dy_parts, u_parts, h_re, h_im, bt_re, bt_im, cm_re, cm_im, lbar_re, lbar_im, d_row, n_seq, seq):
    slab, steps, tile_rows, n_tiles = _scan_geometry(n_seq, seq)
    rows = u_parts[0].shape[0]

    def body(*refs):
        dy_refs, u_refs, refs = refs[:_SCAN_PARTS], refs[_SCAN_PARTS:2 * _SCAN_PARTS], refs[2 * _SCAN_PARTS:]
        (hre_ref, him_ref, bre_ref, bim_ref, cre_ref, cim_ref, lre_ref, lim_ref, d_ref), refs = refs[:9], refs[9:]
        du_refs, refs = refs[:_SCAN_PARTS], refs[_SCAN_PARTS:]
        (dbre_ref, dbim_ref, dcre_ref, dcim_ref, dlre_ref, dlim_ref, dd_ref,
         st_re, st_im, g0_re, g0_im, acc_re, acc_im, buf_re, buf_im) = refs
        second = pl.program_id(0) == 1
        i = pl.program_id(1)

        @pl.when(jnp.logical_and(i == 0, jnp.logical_not(second)))
        def _():
            st_re[...] = jnp.zeros_like(st_re)
            st_im[...] = jnp.zeros_like(st_im)
            acc_re[...] = jnp.zeros_like(acc_re)
            acc_im[...] = jnp.zeros_like(acc_im)
            for ref in (dbre_ref, dbim_ref, dcre_ref, dcim_ref, dd_ref):
                ref[...] = jnp.zeros_like(ref)

        dy = _join_parts(dy_refs)
        dyb = dy.astype(BF16)
        for hf in range(2):
            cols = slice(hf * 1024, (hf + 1) * 1024)
            buf_re[:, cols] = _dot(dyb[:, hf * 256:(hf + 1) * 256], cre_ref[hf])
            buf_im[:, cols] = -_dot(dyb[:, hf * 256:(hf + 1) * 256], cim_ref[hf])

        for lc in range(SSM_LANES // SCAN_LANE_CHUNK):
            cols = slice(lc * SCAN_LANE_CHUNK, (lc + 1) * SCAN_LANE_CHUNK)
            l_re = jnp.broadcast_to(lre_ref[:, cols], (slab, SCAN_LANE_CHUNK))
            l_im = jnp.broadcast_to(lim_ref[:, cols], (slab, SCAN_LANE_CHUNK))

            def advance(r0, s_re, s_im):
                n_re = l_re * s_re + l_im * s_im + buf_re[pl.ds(r0, slab), cols]
                n_im = l_re * s_im - l_im * s_re + buf_im[pl.ds(r0, slab), cols]
                buf_re[pl.ds(r0, slab), cols] = n_re
                buf_im[pl.ds(r0, slab), cols] = n_im
                return n_re, n_im

            def row0(k):
                return pl.multiple_of((SCAN_TILE_STEPS - 1 - k) * slab, slab)

            @pl.when(jnp.logical_not(second))
            def _():
                s_re, s_im = lax.fori_loop(0, SCAN_TILE_STEPS, lambda k, s: advance(row0(k), *s),
                                           (st_re[:, cols], st_im[:, cols]), unroll=True)
                st_re[:, cols] = s_re
                st_im[:, cols] = s_im

            @pl.when(second)
            def _():
                def step(k, carry):
                    s_re, s_im, a_re, a_im = carry
                    r0 = row0(k)
                    hr = hre_ref[pl.ds(r0, slab), cols].astype(F32)
                    hi = him_ref[pl.ds(r0, slab), cols].astype(F32)
                    a_re = a_re + s_re * hr + s_im * hi
                    a_im = a_im + s_im * hr - s_re * hi
                    return advance(r0, s_re, s_im) + (a_re, a_im)

                zero = jnp.zeros((slab, SCAN_LANE_CHUNK), F32)
                s_re, s_im, a_re, a_im = lax.fori_loop(
                    0, SCAN_TILE_STEPS, step, (st_re[:, cols], st_im[:, cols], zero, zero), unroll=True)
                st_re[:, cols] = s_re
                st_im[:, cols] = s_im
                acc_re[:, cols] += a_re
                acc_im[:, cols] += a_im

        @pl.when(jnp.logical_and(i == n_tiles - 1, jnp.logical_not(second)))
        def _():
            p_re, p_im = _complex_power(lre_ref[...], lim_ref[...], steps)
            _chunk_carry(st_re, st_im, g0_re, g0_im, p_re, -p_im, n_seq, reverse=True)
            st_re[...] = g0_re[...]
            st_im[...] = g0_im[...]

        @pl.when(second)
        def _():
            u = _join_parts(u_refs)
            ub = u.astype(BF16)
            g_re = buf_re[...].astype(BF16)
            g_im = buf_im[...].astype(BF16)
            dd_ref[...] += jnp.sum(dy * u, axis=0, keepdims=True)
            for hf in range(2):
                cols = slice(hf * 1024, (hf + 1) * 1024)
                ycols = slice(hf * 256, (hf + 1) * 256)
                du_half = (_dot_nt(g_re[:, cols], bre_ref[hf]) + _dot_nt(g_im[:, cols], bim_ref[hf])
                           + d_ref[:, ycols] * dy[:, ycols])
                _split_parts(du_refs[2 * hf:2 * hf + 2], du_half)
                dbre_ref[hf] += _dot_tn(ub[:, ycols], g_re[:, cols])
                dbim_ref[hf] += _dot_tn(ub[:, ycols], g_im[:, cols])
                dcre_ref[hf] += _dot_tn(dyb[:, ycols], hre_ref[:, cols])
                dcim_ref[hf] -= _dot_tn(dyb[:, ycols], him_ref[:, cols])

        @pl.when(jnp.logical_and(i == n_tiles - 1, second))
        def _():
            dlre_ref[...] = jnp.sum(acc_re[...], axis=0, keepdims=True)
            dlim_ref[...] = jnp.sum(acc_im[...], axis=0, keepdims=True)

    tile = lambda w: pl.BlockSpec((tile_rows, w), lambda p, i: (n_tiles - 1 - i, 0))
    second_tile = lambda w: pl.BlockSpec((tile_rows, w), lambda p, i: (n_tiles - 1 - i * p, 0))
    cm = _full(_CM_SHAPE)
    row = _full((1, SSM_LANES))
    outs = _call(
        body, name="s5_scan_bwd", grid=(2, n_tiles),
        in_specs=[tile(LANES)] * _SCAN_PARTS + [second_tile(LANES)] * _SCAN_PARTS
        + [second_tile(SSM_LANES), second_tile(SSM_LANES), cm, cm, cm, cm, row, row, _full((1, 512))],
        out_specs=[second_tile(LANES)] * _SCAN_PARTS + [cm, cm, cm, cm, row, row, _full((1, 512))],
        out_shape=(_part_shapes(rows) + [_sds(_CM_SHAPE)] * 4 + [_sds((1, SSM_LANES))] * 2 + [_sds((1, 512))]),
        scratch_shapes=[pltpu.VMEM((slab, SSM_LANES), F32)] * 6 + [pltpu.VMEM((tile_rows, SSM_LANES), F32)] * 2,
        compiler_params=_params(48, ("arbitrary", "arbitrary")),
    )(*dy_parts, *u_parts, h_re, h_im, bt_re, bt_im, cm_re, cm_im, lbar_re, lbar_im, d_row)
    return (outs[:_SCAN_PARTS],) + tuple(outs[_SCAN_PARTS:])


def _glu_gate(gl, a, zs):
    return gl * jax.nn.sigmoid(a) * _silu(zs)


def _glu_fwd(y_parts, zs, w_glu, b_glu, n_seq, seq):
    rows = zs.shape[0]
    tm = 512
    slab, steps, _, _ = _scan_geometry(n_seq, seq)

    def body(*refs):
        y_refs, (zs_ref, w_ref, b_ref, o_ref) = refs[:_SCAN_PARTS], refs[_SCAN_PARTS:]
        y = _load_chunks(y_refs, pl.program_id(0) * (tm // steps), tm // steps, steps, slab)
        gl = jax.nn.gelu(y)
        a = _dot(gl.astype(BF16), w_ref[...]) + b_ref[...]
        o_ref[...] = _glu_gate(gl, a, zs_ref[...]).astype(BF16)

    return _call(
        body, name="glu_fwd", grid=(rows // tm,),
        in_specs=_whole_parts(rows) + [_rows(tm, 512), _full((512, 512)), _full((1, 512))],
        out_specs=_rows(tm, 512), out_shape=_sds((rows, 512), BF16),
        compiler_params=_params(32, ("arbitrary",)),
    )(*y_parts, zs, w_glu, b_glu)


def _glu_bwd(y_parts, zs, d_out, w_glu, b_glu, n_seq, seq):
    rows = zs.shape[0]
    tm = 512
    slab, steps, _, _ = _scan_geometry(n_seq, seq)

    def body(*refs):
        y_refs, (zs_ref, d_ref, w_ref, b_ref), refs = refs[:_SCAN_PARTS], refs[_SCAN_PARTS:_SCAN_PARTS + 4], refs[_SCAN_PARTS + 4:]
        dy_refs, (dzs_ref, dw_ref, db_ref) = refs[:_SCAN_PARTS], refs[_SCAN_PARTS:]
        first_chunk = pl.program_id(0) * (tm // steps)

        @pl.when(pl.program_id(0) == 0)
        def _():
            dw_ref[...] = jnp.zeros_like(dw_ref)
            db_ref[...] = jnp.zeros_like(db_ref)

        gl, gelu_vjp = jax.vjp(jax.nn.gelu, _load_chunks(y_refs, first_chunk, tm // steps, steps, slab))
        glb = gl.astype(BF16)
        a = _dot(glb, w_ref[...]) + b_ref[...]
        _, gate_vjp = jax.vjp(_glu_gate, gl, a, zs_ref[...])
        d_gl, d_a, d_zs = gate_vjp(d_ref[...])
        dab = d_a.astype(BF16)
        d_gl = d_gl + _dot_nt(dab, w_ref[...])
        _store_chunks(dy_refs, first_chunk, gelu_vjp(d_gl)[0], steps, slab)
        dzs_ref[...] = d_zs.astype(BF16)
        dw_ref[...] += _dot_tn(glb, dab)
        db_ref[...] += jnp.sum(d_a, axis=0, keepdims=True)

    *dy_parts, dzs, dw, db = _call(
        body, name="glu_bwd", grid=(rows // tm,),
        in_specs=_whole_parts(rows) + [_rows(tm, 512), _rows(tm, 512), _full((512, 512)), _full((1, 512))],
        out_specs=_whole_parts(rows) + [_rows(tm, 512), _full((512, 512)), _full((1, 512))],
        out_shape=_part_shapes(rows) + [_sds((rows, 512), BF16), _sds((512, 512)), _sds((1, 512))],
        compiler_params=_params(40, ("arbitrary",)),
    )(*y_parts, zs, d_out, w_glu, b_glu)
    return dy_parts, dzs, dw, db


_GROUP_ROWS = Q_PER_KV * BLOCK
_BLOCK_SHIFT = BLOCK.bit_length() - 1


def _attn_bias(j):
    row = _iota((_GROUP_ROWS, BLOCK), 0)
    dist_cur = (row & (BLOCK - 1)) - _iota((_GROUP_ROWS, BLOCK), 1)
    dist_prev = dist_cur + BLOCK
    head = row >> _BLOCK_SHIFT
    slope = jnp.zeros((_GROUP_ROWS, BLOCK), F32)
    for g in range(Q_PER_KV):
        slope = jnp.where(head == g, 2.0 ** (-(j * Q_PER_KV + g + 1)), slope)
    bias_cur = jnp.where(dist_cur >= 0, -slope * dist_cur.astype(F32), -jnp.inf)
    bias_prev = jnp.where(dist_prev < WINDOW, -slope * dist_prev.astype(F32), -jnp.inf)
    return bias_cur, bias_prev


_ATTN_BIAS_SCRATCH = pltpu.VMEM((KV_HEADS, 2, _GROUP_ROWS, BLOCK), F32)


def _fill_attn_bias(bias_ref):
    @pl.when(jnp.logical_and(pl.program_id(0) == 0, pl.program_id(1) == 0))
    def _():
        for j in range(KV_HEADS):
            bias_ref[j, 0], bias_ref[j, 1] = _attn_bias(j)


def _stack_heads(x, j):
    heads = range(j * Q_PER_KV, (j + 1) * Q_PER_KV)
    return jnp.concatenate([x[:, h * HEAD_DIM:(h + 1) * HEAD_DIM] for h in heads], axis=0)


def _stack_columns(x, j):
    heads = range(j * Q_PER_KV, (j + 1) * Q_PER_KV)
    return jnp.concatenate([jnp.broadcast_to(x[:, h:h + 1], (BLOCK, 1)) for h in heads], axis=0)


def _attn_fwd(q, k, v, za, sinks, n_seq, seq):
    nb = seq // BLOCK
    rows = q.shape[0]

    def body(q_ref, kc_ref, kp_ref, vc_ref, vp_ref, za_ref, sk_ref, o_ref, ao_ref, lse_ref, bias_ref):
        _fill_attn_bias(bias_ref)
        has_prev = pl.program_id(1) > 0
        q_all = q_ref[...]
        for j in range(KV_HEADS):
            js = slice(j * HEAD_DIM, (j + 1) * HEAD_DIM)
            bias_c, bias_p = bias_ref[j, 0], bias_ref[j, 1]
            q4 = _stack_heads(q_all, j)
            sc = _dot_nt(q4, kc_ref[:, js]) + bias_c
            sp = _dot_nt(q4, kp_ref[:, js]) + jnp.where(has_prev, bias_p, -jnp.inf)
            sink = _stack_columns(sk_ref[...], j)
            m = jnp.maximum(jnp.max(jnp.maximum(sc, sp), axis=-1, keepdims=True), sink)
            ec = jnp.exp(sc - m)
            ep = jnp.exp(sp - m)
            den = jnp.sum(ec + ep, axis=-1, keepdims=True) + jnp.exp(sink - m)
            inv = 1.0 / den
            o4 = _dot((ec * inv).astype(BF16), vc_ref[:, js]) + _dot((ep * inv).astype(BF16), vp_ref[:, js])
            lse4 = m + jnp.log(den)
            for g in range(Q_PER_KV):
                h = j * Q_PER_KV + g
                o_ref[:, h * HEAD_DIM:(h + 1) * HEAD_DIM] = o4[g * BLOCK:(g + 1) * BLOCK]
                lse_ref[:, h:h + 1] = lse4[g * BLOCK:(g + 1) * BLOCK]
        ao_ref[...] = (o_ref[...] * _silu(za_ref[...])).astype(BF16)

    cur = lambda w: pl.BlockSpec((BLOCK, w), lambda b, n: (b * nb + n, 0))
    prev = lambda w: pl.BlockSpec((BLOCK, w), lambda b, n: (b * nb + jnp.maximum(n - 1, 0), 0))
    return _call(
        body, name="attn_fwd", grid=(n_seq, nb),
        in_specs=[cur(512), cur(128), prev(128), cur(128), prev(128), cur(512), _full((1, N_HEADS))],
        out_specs=[cur(512), cur(512), cur(N_HEADS)],
        out_shape=[_sds((rows, 512)), _sds((rows, 512), BF16), _sds((rows, N_HEADS))],
        scratch_shapes=[_ATTN_BIAS_SCRATCH], compiler_params=_params(32, ("arbitrary", "arbitrary")),
    )(q, k, k, v, v, za, sinks)


def _attn_bwd(q, k, v, za, o, lse, d_ao, sinks, n_seq, seq):
    nb = seq // BLOCK
    rows = q.shape[0]

    def body(q_ref, kc_ref, kp_ref, vc_ref, vp_ref, za_ref, o_ref, lse_ref, d_ref, sk_ref,
             dq_ref, dk_ref, dv_ref, dza_ref, dsk_ref, bias_ref, dk_carry, dv_carry):
        n = nb - 1 - pl.program_id(1)
        _fill_attn_bias(bias_ref)

        @pl.when(jnp.logical_and(pl.program_id(0) == 0, pl.program_id(1) == 0))
        def _():
            dsk_ref[...] = jnp.zeros_like(dsk_ref)
            dk_carry[...] = jnp.zeros_like(dk_carry)
            dv_carry[...] = jnp.zeros_like(dv_carry)

        has_prev = n > 0
        has_next = n + 1 < nb

        _, gate_vjp = jax.vjp(lambda o_, z_: o_ * _silu(z_), o_ref[...], za_ref[...])
        d_o, d_za = gate_vjp(d_ref[...])
        dza_ref[...] = d_za.astype(BF16)
        q_all = q_ref[...]
        lse_all = lse_ref[...]

        for j in range(KV_HEADS):
            js = slice(j * HEAD_DIM, (j + 1) * HEAD_DIM)
            kc, kp, vc, vp = kc_ref[:, js], kp_ref[:, js], vc_ref[:, js], vp_ref[:, js]
            bias_c, bias_p = bias_ref[j, 0], bias_ref[j, 1]
            q4 = _stack_heads(q_all, j)
            do4b = _stack_heads(d_o, j).astype(BF16)
            lse4 = _stack_columns(lse_all, j)
            pc = jnp.exp(_dot_nt(q4, kc) + bias_c - lse4)
            pp = jnp.exp(_dot_nt(q4, kp) + jnp.where(has_prev, bias_p, -jnp.inf) - lse4)
            dpc = _dot_nt(do4b, vc)
            dpp = _dot_nt(do4b, vp)
            delta = jnp.sum(pc * dpc + pp * dpp, axis=-1, keepdims=True)
            dsc = (pc * (dpc - delta)).astype(BF16)
            dsp = (pp * (dpp - delta)).astype(BF16)
            dq4 = ((_dot(dsc, kc) + _dot(dsp, kp)) * ATTN_SCALE).astype(BF16)
            sink_loss = jnp.exp(_stack_columns(sk_ref[...], j) - lse4) * delta
            for g in range(Q_PER_KV):
                h = j * Q_PER_KV + g
                dq_ref[:, h * HEAD_DIM:(h + 1) * HEAD_DIM] = dq4[g * BLOCK:(g + 1) * BLOCK]
                dsk_ref[0:1, h:h + 1] -= jnp.sum(sink_loss[g * BLOCK:(g + 1) * BLOCK], axis=0, keepdims=True)
            dk = _dot_tn(dsc, q4) + jnp.where(has_next, dk_carry[j], 0.0)
            dv = _dot_tn(pc.astype(BF16), do4b) + jnp.where(has_next, dv_carry[j], 0.0)
            dk_carry[j] = _dot_tn(dsp, q4)
            dv_carry[j] = _dot_tn(pp.astype(BF16), do4b)
            dk_ref[:, js] = dk.astype(BF16)
            dv_ref[:, js] = dv.astype(BF16)

    cur = lambda w: pl.BlockSpec((BLOCK, w), lambda b, s: (b * nb + nb - 1 - s, 0))
    prev = lambda w: pl.BlockSpec((BLOCK, w), lambda b, s: (b * nb + jnp.maximum(nb - 2 - s, 0), 0))
    return _call(
        body, name="attn_bwd", grid=(n_seq, nb),
        in_specs=[cur(512), cur(128), prev(128), cur(128), prev(128), cur(512), cur(512), cur(N_HEADS), cur(512),
                  _full((1, N_HEADS))],
        out_specs=[cur(512), cur(128), cur(128), cur(512), _full((1, N_HEADS))],
        out_shape=[_sds((rows, 512), BF16), _sds((rows, 128), BF16), _sds((rows, 128), BF16),
                   _sds((rows, 512), BF16), _sds((1, N_HEADS))],
        scratch_shapes=[_ATTN_BIAS_SCRATCH, pltpu.VMEM((KV_HEADS, BLOCK, HEAD_DIM), F32),
                        pltpu.VMEM((KV_HEADS, BLOCK, HEAD_DIM), F32)],
        compiler_params=_params(32, ("arbitrary", "arbitrary")),
    )(q, k, k, v, v, za, o, lse, d_ao, sinks)


def _tail(ssm_out, attn_out, x2d, p2d, target, w_out, g2, w_gate, b_gate, w_proj):
    rows = x2d.shape[0]
    tm = 512

    def body(so_ref, ao_ref, x_ref, p_ref, t_ref, wo_ref, g2_ref, wg_ref, bg_ref, wp_ref,
             dh1_ref, dso_ref, dao_ref, dwo_ref, dwg_ref, dwp_ref, dbg_ref, dg2_ref, loss_ref):
        @pl.when(pl.program_id(0) == 0)
        def _():
            for ref in (dwo_ref, dwg_ref, dwp_ref, dbg_ref, dg2_ref, loss_ref):
                ref[...] = jnp.zeros_like(ref)

        so = so_ref[...]
        ao = ao_ref[...]
        g2 = g2_ref[...]
        mixed = _dot(so, wo_ref[0:512, :]) + _dot(ao, wo_ref[512:1024, :])
        r = lax.rsqrt(jnp.mean(mixed * mixed, axis=-1, keepdims=True) + EPS)
        mr = mixed * r
        h1 = x_ref[...] + mr * g2
        h1b = h1.astype(BF16)
        gate = jax.nn.sigmoid(_dot(h1b, wg_ref[...]) + bg_ref[...])
        pb = p_ref[...].astype(BF16)
        wp_blocks = [slice(j * D_PLE, (j + 1) * D_PLE) for j in range(N_CHIPS)]
        pp = jnp.concatenate([_dot(pb, wp_ref[blk, :]) for blk in wp_blocks], axis=1)
        err = h1 + gate * pp - t_ref[...]
        loss_ref[...] += 0.5 * jnp.sum(jnp.mean(err * err, axis=-1, keepdims=True), axis=0, keepdims=True)

        dh2 = err * (1.0 / D_MODEL)
        d_glin = dh2 * pp * gate * (1.0 - gate)
        d_glin_b = d_glin.astype(BF16)
        dwg_ref[...] += _dot_tn(h1b, d_glin_b)
        dbg_ref[...] += jnp.sum(d_glin, axis=0, keepdims=True)
        d_pp = (dh2 * gate).astype(BF16)
        for blk in wp_blocks:
            dwp_ref[blk, :] += _dot_tn(pb, d_pp[:, blk])
        dh1 = dh2 + _dot_nt(d_glin_b, wg_ref[...])
        dh1_ref[...] = dh1
        dg2_ref[...] += jnp.sum(dh1 * mr, axis=0, keepdims=True)
        a_ = dh1 * g2
        d_mixed = (r * a_ - mr * (r * jnp.mean(a_ * mr, axis=-1, keepdims=True))).astype(BF16)
        dwo_ref[0:512, :] += _dot_tn(so, d_mixed)
        dwo_ref[512:1024, :] += _dot_tn(ao, d_mixed)
        dso_ref[...] = _dot_nt(d_mixed, wo_ref[0:512, :])
        dao_ref[...] = _dot_nt(d_mixed, wo_ref[512:1024, :])

    return _call(
        body, name="tail_fwd_bwd", grid=(rows // tm,),
        in_specs=[_rows(tm, 512), _rows(tm, 512), _rows(tm, D_MODEL), _rows(tm, D_PLE), _rows(tm, D_MODEL),
                  _full((D_MODEL, D_MODEL)), _full((1, D_MODEL)), _full((D_MODEL, D_MODEL)), _full((1, D_MODEL)),
                  _full((N_CHIPS * D_PLE, D_PLE))],
        out_specs=[_rows(tm, D_MODEL), _rows(tm, 512), _rows(tm, 512), _full((D_MODEL, D_MODEL)),
                   _full((D_MODEL, D_MODEL)), _full((N_CHIPS * D_PLE, D_PLE)), _full((1, D_MODEL)), _full((1, D_MODEL)),
                   _full((1, 1))],
        out_shape=[_sds((rows, D_MODEL)), _sds((rows, 512)), _sds((rows, 512)), _sds((D_MODEL, D_MODEL)),
                   _sds((D_MODEL, D_MODEL)), _sds((N_CHIPS * D_PLE, D_PLE)), _sds((1, D_MODEL)), _sds((1, D_MODEL)),
                   _sds((1, 1))],
        compiler_params=_params(52, ("arbitrary",)),
    )(ssm_out, attn_out, x2d, p2d, target, w_out, g2, w_gate, b_gate, w_proj)


def _local_step(x, p, target, pre_norm_g, w_in_t, s5_params, ssm_d, w_glu, b_glu, sinks, w_out, post_norm_g, w_proj,
                w_gate, b_gate):
    n_seq, seq, _ = x.shape
    rows = n_seq * seq
    x2d = x.reshape(rows, D_MODEL)
    p2d = p.reshape(rows, D_PLE)
    t2d = target.reshape(rows, D_MODEL)

    l_re, l_im, bt_re, bt_im, cm_re, cm_im = _s5_params_fwd(*s5_params)

    u_scan, zs, q, k, v, za = _in_proj(x2d, pre_norm_g, w_in_t, n_seq, seq)
    y_scan, h_re, h_im = _s5_scan_fwd(u_scan, bt_re, bt_im, cm_re, cm_im, l_re, l_im, ssm_d, n_seq, seq)
    ssm_out = _glu_fwd(y_scan, zs, w_glu, b_glu, n_seq, seq)
    o, attn_out, lse = _attn_fwd(q, k, v, za, sinks, n_seq, seq)

    dh1, d_so, d_ao, d_w_out, d_w_gate, d_w_proj, d_b_gate, d_g2, loss = _tail(
        ssm_out, attn_out, x2d, p2d, t2d, w_out, post_norm_g, w_gate, b_gate, w_proj)

    dq, dk, dv, dza, d_sinks = _attn_bwd(q, k, v, za, o, lse, d_ao, sinks, n_seq, seq)
    dy_scan, dzs, d_w_glu, d_b_glu = _glu_bwd(y_scan, zs, d_so, w_glu, b_glu, n_seq, seq)
    du_scan, d_bt_re, d_bt_im, d_cm_re, d_cm_im, d_l_re, d_l_im, d_d = _s5_scan_bwd(
        dy_scan, u_scan, h_re, h_im, bt_re, bt_im, cm_re, cm_im, l_re, l_im, ssm_d, n_seq, seq)
    d_lam_re, d_lam_im, d_log_step, d_b_re, d_b_im, d_c_re, d_c_im = _s5_params_bwd(
        s5_params, (d_l_re, d_l_im, d_bt_re, d_bt_im, d_cm_re, d_cm_im))

    grad_x, d_w_in_t, d_g1 = _in_proj_bwd(x2d, dh1, pre_norm_g, w_in_t, du_scan, dzs, dq, dk, dv, dza, n_seq, seq)
    grads = dict(
        pre_norm_g=d_g1, w_in=d_w_in_t, ssm_lam_re=d_lam_re, ssm_lam_im=d_lam_im, ssm_log_step=d_log_step,
        ssm_b_re=d_b_re, ssm_b_im=d_b_im, ssm_c_re=d_c_re, ssm_c_im=d_c_im, ssm_d=d_d, ssm_w_glu=d_w_glu,
        ssm_b_glu=d_b_glu, attn_sinks=d_sinks, w_out=d_w_out, post_norm_g=d_g2, pl_w_proj=d_w_proj,
        pl_w_gate=d_w_gate, pl_b_gate=d_b_gate)
    return grad_x.reshape(x.shape), loss, grads


_BIG = ("w_in", "ssm_w_glu", "w_out", "pl_w_proj", "pl_w_gate")
_BIG_SHARD = {"w_in": (D_IN // N_CHIPS, D_MODEL), "ssm_w_glu": (D_SSM // N_CHIPS, D_SSM),
              "w_out": (D_MODEL // N_CHIPS, D_MODEL), "pl_w_proj": (D_PLE, D_MODEL // N_CHIPS),
              "pl_w_gate": (D_MODEL // N_CHIPS, D_MODEL)}
_SMALL = {"pre_norm_g": (1, D_MODEL), "ssm_lam_re": (SSM_GROUPS, SSM_STATE), "ssm_lam_im": (SSM_GROUPS, SSM_STATE),
          "ssm_log_step": (1, SSM_GROUPS), "ssm_b_re": (D_SSM, SSM_STATE), "ssm_b_im": (D_SSM, SSM_STATE),
          "ssm_c_re": (D_SSM, SSM_STATE), "ssm_c_im": (D_SSM, SSM_STATE), "ssm_d": (1, D_SSM), "ssm_b_glu": (1, D_SSM),
          "attn_sinks": (1, N_HEADS), "post_norm_g": (1, D_MODEL), "pl_b_gate": (1, D_MODEL)}
_VEC_ROWS = ("pre_norm_g", "post_norm_g", "pl_b_gate", "ssm_d", "ssm_b_glu", "attn_sinks", "ssm_log_step", "loss")
_SMALL_GROUPS = (
    ("vec", (8, D_MODEL), tuple((name, r) for r, name in enumerate(_VEC_ROWS))),
    ("lam", (2 * SSM_GROUPS, SSM_STATE), (("ssm_lam_re", 0), ("ssm_lam_im", SSM_GROUPS))),
)
_SMALL_EARLY = ("ssm_b_re", "ssm_b_im", "ssm_c_re", "ssm_c_im")
_SMALL_ORDER = tuple(name for _, _, members in _SMALL_GROUPS for name, _ in members) + _SMALL_EARLY
_WEIGHT_ORDER = ("pre_norm_g", "w_in", "ssm_lam_re", "ssm_lam_im", "ssm_log_step", "ssm_b_re", "ssm_b_im", "ssm_c_re",
                 "ssm_c_im", "ssm_d", "ssm_w_glu", "ssm_b_glu", "attn_sinks", "w_out", "post_norm_g", "pl_w_proj",
                 "pl_w_gate", "pl_b_gate")


def _small_shape(name):
    return (1, 1) if name == "loss" else _SMALL[name]


def _to_kernel_form(name, a):
    a = a[0]
    if name == "w_in":
        return a.T
    if name in ("ssm_b_re", "ssm_b_im"):
        a = a.transpose(0, 2, 1)
    return a.reshape(_SMALL[name]) if name in _SMALL else a


def _from_kernel_form(name, a, shape):
    if name == "w_in":
        a = a.T
    if name in ("ssm_b_re", "ssm_b_im"):
        a = a.reshape(SSM_GROUPS, SSM_GROUP_CH, SSM_STATE).transpose(0, 2, 1)
    return a.reshape(shape)


def _mesh_place():
    x, y, c = lax.axis_index("x"), lax.axis_index("y"), lax.axis_index("c")
    other_chips = ((1 - x, y), (x, 1 - y), (1 - x, 1 - y))
    return x, y, c, other_chips


def _gather_copies(s_refs, g_refs, send_sems, recv_sems, local_sems):
    x, y, c, other_chips = _mesh_place()
    started = []
    for i, (s_ref, g_ref) in enumerate(zip(s_refs, g_refs)):
        rows = s_ref.shape[0]
        half = rows // 2

        def block(chip, g_ref=g_ref, rows=rows, half=half):
            return g_ref.at[pl.ds((2 * chip[0] + chip[1]) * rows + c * half, half), :]

        def copy(k, chip, to, src=None, i=i, block=block):
            return pltpu.make_async_remote_copy(
                src_ref=block(chip) if src is None else src, dst_ref=block(chip), send_sem=send_sems.at[6 * i + k],
                recv_sem=recv_sems.at[6 * i + k], device_id=to, device_id_type=MESH)

        own = pltpu.make_async_copy(s_ref, g_ref.at[pl.ds((2 * x + y) * rows, rows), :], local_sems.at[i])
        own.start()
        first = [copy(k, (x, y), (*chip, c), src=s_ref.at[pl.ds(c * half, half), :])
                 for k, chip in enumerate(other_chips)]
        for cp in first:
            cp.start()
        passed = [copy(3 + k, chip, (x, y, 1 - c)) for k, chip in enumerate(other_chips)]
        started.append((own, first, passed))
    for own, first, passed in started:
        for k in range(3):
            first[k].wait_recv()
            passed[k].start()
    for own, first, passed in started:
        for k in range(3):
            passed[k].wait_recv()
        for cp in first + passed:
            cp.wait_send()
        own.wait()


def _gather_semaphores(n_t):
    return [pltpu.SemaphoreType.DMA((6 * n_t,)), pltpu.SemaphoreType.DMA((6 * n_t,)), pltpu.SemaphoreType.DMA((n_t,))]


def _gather_weights(shards):
    n_t = len(shards)

    def body(*refs):
        _gather_copies(refs[:n_t], refs[n_t:2 * n_t], *refs[2 * n_t + 1:])
        refs[2 * n_t][...] = jnp.zeros_like(refs[2 * n_t])

    any_spec = pl.BlockSpec(memory_space=pl.ANY)
    *full, done = _call(
        body, name="gather_weights", in_specs=[any_spec] * n_t,
        out_specs=[any_spec] * n_t + [pl.BlockSpec(memory_space=pltpu.VMEM)],
        out_shape=[_sds((N_CHIPS * s.shape[0], s.shape[1]), s.dtype) for s in shards]
        + [jax.ShapeDtypeStruct((8, LANES), F32)],
        scratch_shapes=_gather_semaphores(n_t),
    )(*shards)
    return full, done[0, 0]


def _gather_weights_beside(shards):
    n_t = len(shards)
    hbm = pltpu.MemorySpace.HBM
    s_refs = [jax.new_ref(s, memory_space=hbm) for s in shards]
    g_refs = [jax.empty_ref(jax.ShapeDtypeStruct((N_CHIPS * s.shape[0], s.shape[1]), s.dtype), memory_space=hbm)
              for s in shards]

    def launch(send_sems, recv_sems, local_sems):
        x, y, c, other_chips = _mesh_place()
        peers = [(*chip, c) for chip in other_chips] + [(x, y, 1 - c)]
        barrier = pltpu.get_barrier_semaphore()
        for peer in peers:
            pl.semaphore_signal(barrier, inc=1, device_id=peer, device_id_type=MESH)
        pl.semaphore_wait(barrier, len(peers))
        _gather_copies(s_refs, g_refs, send_sems, recv_sems, local_sems)

    pl.kernel(launch, mesh=plsc.ScalarSubcoreMesh(axis_name="sequencer", num_cores=1), name="gather_weights_beside",
              scratch_types=_gather_semaphores(n_t), compiler_params=pltpu.CompilerParams(collective_id=1))()
    return [g[...] for g in g_refs]


_RELATIONS = tuple(((r >> 2) & 1, (r >> 1) & 1, r & 1) for r in range(1, 8))


def _related(place, relation):
    return tuple(1 - a if flip else a for a, flip in zip(place, relation))


def _scatter_beside(mats):
    hbm = pltpu.MemorySpace.HBM
    src_refs = [jax.new_ref(a, memory_space=hbm) for a in mats]
    land_refs = [jax.empty_ref(jax.ShapeDtypeStruct((7, a.shape[0] // 8, a.shape[1]), a.dtype), memory_space=hbm)
                 for a in mats]

    def launch(send_sems, recv_sems):
        me = (lax.axis_index("x"), lax.axis_index("y"), lax.axis_index("c"))
        peers = [_related(me, rel) for rel in _RELATIONS]
        barrier = pltpu.get_barrier_semaphore()
        for peer in peers:
            pl.semaphore_signal(barrier, inc=1, device_id=peer, device_id_type=MESH)
        pl.semaphore_wait(barrier, len(peers))
        copies = []
        for i, (src, land) in enumerate(zip(src_refs, land_refs)):
            hr = land.shape[1]
            for k, (tx, ty, tc) in enumerate(peers):
                rows = pl.ds((2 * tx + ty) * 2 * hr + tc * hr, hr)
                copies.append(pltpu.make_async_remote_copy(
                    src_ref=src.at[rows, :], dst_ref=land.at[k], send_sem=send_sems.at[7 * i + k],
                    recv_sem=recv_sems.at[7 * i + k], device_id=(tx, ty, tc), device_id_type=MESH))
                copies[-1].start()
        for cp in copies:
            cp.wait()

    n_sems = 7 * len(mats)
    pl.kernel(launch, mesh=plsc.ScalarSubcoreMesh(axis_name="sequencer", num_cores=1), name="scatter_beside",
              scratch_types=[pltpu.SemaphoreType.DMA((n_sems,)), pltpu.SemaphoreType.DMA((n_sems,))],
              compiler_params=pltpu.CompilerParams(collective_id=2))()
    return [ref[...] for ref in land_refs]


def _broadcast_beside(arrays):
    hbm = pltpu.MemorySpace.HBM
    src_refs = [jax.new_ref(a, memory_space=hbm) for a in arrays]
    land_refs = [jax.empty_ref(jax.ShapeDtypeStruct((len(_RELATIONS),) + a.shape, a.dtype), memory_space=hbm)
                 for a in arrays]

    def launch(send_sems, recv_sems):
        me = (lax.axis_index("x"), lax.axis_index("y"), lax.axis_index("c"))
        peers = [_related(me, rel) for rel in _RELATIONS]
        barrier = pltpu.get_barrier_semaphore()
        for peer in peers:
            pl.semaphore_signal(barrier, inc=1, device_id=peer, device_id_type=MESH)
        pl.semaphore_wait(barrier, len(peers))
        copies = []
        for i, (src, land) in enumerate(zip(src_refs, land_refs)):
            for k, peer in enumerate(peers):
                copies.append(pltpu.make_async_remote_copy(
                    src_ref=src, dst_ref=land.at[k], send_sem=send_sems.at[7 * i + k],
                    recv_sem=recv_sems.at[7 * i + k], device_id=peer, device_id_type=MESH))
                copies[-1].start()
        for cp in copies:
            cp.wait()

    n_sems = 7 * len(arrays)
    pl.kernel(launch, mesh=plsc.ScalarSubcoreMesh(axis_name="sequencer", num_cores=1), name="broadcast_beside",
              scratch_types=[pltpu.SemaphoreType.DMA((n_sems,)), pltpu.SemaphoreType.DMA((n_sems,))],
              compiler_params=pltpu.CompilerParams(collective_id=3))()
    return [ref[...] for ref in land_refs]


def _exchange_grads(big, small, landed, landed_small):
    n_t = len(big)
    n_g = len(_SMALL_GROUPS)
    names = _SMALL_ORDER
    halves = [(b.shape[0] // N_CHIPS // 2, b.shape[1]) for b in big]
    early = sorted(landed)
    late = [i for i in range(n_t) if i not in landed]
    n_sems = 4 * n_g + 7 * len(late) + n_t
    small_sem0, block_sem0 = n_t, n_t + len(names)
    early_sem0 = block_sem0 + N_CHIPS * len(late)
    landed_sem0 = early_sem0 + 2 * len(early)
    early_small = [n for n in names if n in landed_small]

    def body(*refs):
        pos = 0

        def take(n):
            nonlocal pos
            pos += n
            return refs[pos - n:pos]

        big_refs, small_refs = take(n_t), dict(zip(names, take(len(names))))
        land_refs = dict(zip(early, take(len(early))))
        land_small_refs = dict(zip(early_small, take(len(early_small))))
        out_refs, small_out_refs = take(n_t), dict(zip(names, take(len(names))))
        per_late = lambda: dict(zip(late, take(len(late))))
        ga, gb, pme, send_b, recv_b = per_late(), per_late(), take(n_t), per_late(), per_late()
        own_e, land_e = dict(zip(early, take(len(early)))), dict(zip(early, take(len(early))))
        land_s = dict(zip(early_small, take(len(early_small))))
        s_own, s_sib, s_chips, s_pair = take(n_g), take(n_g), take(n_g), take(n_g)
        stage = dict(zip(names, take(len(names))))
        send_sems, recv_sems, local_sems = take(3)
        x, y, c, other_chips = _mesh_place()
        me = 2 * x + y
        sibling = (x, y, 1 - c)
        sem_at = iter(range(n_sems))

        def remote(src, dst, to):
            k = next(sem_at)
            return pltpu.make_async_remote_copy(src_ref=src, dst_ref=dst, send_sem=send_sems.at[k],
                                                recv_sem=recv_sems.at[k], device_id=to, device_id_type=MESH)

        loads = [pltpu.make_async_copy(small_refs[name], stage[name], local_sems.at[small_sem0 + a])
                 for a, name in enumerate(names)]
        landed_loads = [pltpu.make_async_copy(land_small_refs[name], land_s[name], local_sems.at[landed_sem0 + a])
                        for a, name in enumerate(early_small)]
        for cp in loads + landed_loads:
            cp.start()
        for cp in loads:
            cp.wait()
        small_swaps = []
        for gi, (_, _, members) in enumerate(_SMALL_GROUPS):
            s_own[gi][...] = jnp.zeros_like(s_own[gi])
            for name, r0 in members:
                r, n = _small_shape(name)
                s_own[gi][r0:r0 + r, 0:n] = stage[name][...]
            small_swaps.append(remote(s_own[gi], s_sib[gi], sibling))
            small_swaps[gi].start()
        order = sorted(late, key=lambda i: halves[i][0] * halves[i][1])
        own_loads, big_swaps = {}, {}
        for i in order:
            hr = halves[i][0]
            own_loads[i], big_swaps[i] = [], []
            for j in range(N_CHIPS):
                mine = big_refs[i].at[pl.ds(j * 2 * hr + c * hr, hr), :]
                theirs = big_refs[i].at[pl.ds(j * 2 * hr + (1 - c) * hr, hr), :]
                sem = local_sems.at[block_sem0 + N_CHIPS * late.index(i) + j]
                own_loads[i].append(pltpu.make_async_copy(mine, ga[i].at[j], sem))
                own_loads[i][j].start()
                big_swaps[i].append(remote(theirs, gb[i].at[j], sibling))
                big_swaps[i][j].start()
        early_loads = {}
        for e, i in enumerate(early):
            hr = halves[i][0]
            mine = big_refs[i].at[pl.ds(me * 2 * hr + c * hr, hr), :]
            early_loads[i] = [pltpu.make_async_copy(mine, own_e[i], local_sems.at[early_sem0 + 2 * e]),
                              pltpu.make_async_copy(land_refs[i], land_e[i], local_sems.at[early_sem0 + 2 * e + 1])]
            for cp in early_loads[i]:
                cp.start()
        small_sends = []
        for gi in range(n_g):
            small_swaps[gi].wait_recv()
            s_pair[gi][...] = s_own[gi][...] + s_sib[gi][...]
            small_sends.append([remote(s_pair[gi], s_chips[gi].at[k], (*chip, c)) for k, chip in enumerate(other_chips)])
            for cp in small_sends[gi]:
                cp.start()

        def pair_sum(i, j):
            return ga[i][j] + gb[i][j]

        big_sends = {}
        for i in order:
            for j in range(N_CHIPS):
                own_loads[i][j].wait()
                big_swaps[i][j].wait_recv()
            big_sends[i] = []
            for k, chip in enumerate(other_chips):
                send_b[i][k] = pair_sum(i, 2 * chip[0] + chip[1]).astype(BF16)
                big_sends[i].append(remote(send_b[i].at[k], recv_b[i].at[k], (*chip, c)))
                big_sends[i][k].start()
        last_swaps, keeps = {}, {}
        for i in early + order:
            hr = halves[i][0]
            if i in landed:
                for cp in early_loads[i]:
                    cp.wait()
                total = own_e[i][...]
                for k in range(len(_RELATIONS)):
                    total = total + land_e[i][k]
                pme[i][...] = total
            else:
                for k in range(3):
                    big_sends[i][k].wait_recv()
                pme[i][...] = ((pair_sum(i, me) + recv_b[i][0].astype(F32)) + recv_b[i][1].astype(F32)) + recv_b[i][2].astype(F32)
            mine = out_refs[i].at[pl.ds(c * hr, hr), :]
            keeps[i] = pltpu.make_async_copy(pme[i], mine, local_sems.at[i])
            keeps[i].start()
            last_swaps[i] = remote(pme[i], mine, sibling)
            last_swaps[i].start()

        for gi, (_, _, members) in enumerate(_SMALL_GROUPS):
            for k in range(3):
                small_sends[gi][k].wait_recv()
            total = None
            for j in range(N_CHIPS):
                rel = jnp.bitwise_xor(j, me)
                term = jnp.where(rel == 0, s_pair[gi][...], jnp.where(
                    rel == 2, s_chips[gi][0], jnp.where(rel == 1, s_chips[gi][1], s_chips[gi][2])))
                total = term if total is None else total + term
            s_sib[gi][...] = total
            for name, r0 in members:
                r, n = _small_shape(name)
                stage[name][...] = s_sib[gi][r0:r0 + r, 0:n]
        my_index = 4 * x + 2 * y + c
        for cp in landed_loads:
            cp.wait()
        for name in early_small:
            total = None
            for d in range(2 * N_CHIPS):
                rel = jnp.bitwise_xor(d, my_index)
                term = stage[name][...]
                for k in range(len(_RELATIONS)):
                    term = jnp.where(rel == k + 1, land_s[name][k], term)
                total = term if total is None else total + term
            stage[name][...] = total
        stores = [pltpu.make_async_copy(stage[name], small_out_refs[name], local_sems.at[small_sem0 + a])
                  for a, name in enumerate(names)]
        for cp in stores:
            cp.start()

        for i in range(n_t):
            last_swaps[i].wait_recv()
            keeps[i].wait()
        for cp in stores:
            cp.wait()
        groups = list(big_swaps.values()) + small_sends + list(big_sends.values())
        for cp in small_swaps + [cp for group in groups for cp in group] + list(last_swaps.values()):
            cp.wait_send()

    any_spec = pl.BlockSpec(memory_space=pl.ANY)
    small_shapes = [_sds(_small_shape(n)) for n in names]
    group_shapes = [shape for _, shape, _ in _SMALL_GROUPS]
    vmem = lambda which, dtype, lead=(): [pltpu.VMEM(lead + halves[i], dtype) for i in which]
    outs = _call(
        body, name="exchange_grads",
        in_specs=[any_spec] * (n_t + len(names) + len(early) + len(early_small)),
        out_specs=[any_spec] * (n_t + len(names)),
        out_shape=[_sds((b.shape[0] // N_CHIPS, b.shape[1])) for b in big] + small_shapes,
        scratch_shapes=(vmem(late, F32, (N_CHIPS,)) + vmem(late, F32, (N_CHIPS,)) + vmem(range(n_t), F32)
                        + vmem(late, BF16, (3,)) + vmem(late, BF16, (3,))
                        + vmem(early, F32) + vmem(early, F32, (len(_RELATIONS),))
                        + [pltpu.VMEM((len(_RELATIONS),) + _small_shape(n), F32) for n in early_small]
                        + [pltpu.VMEM(s, F32) for s in group_shapes] * 2 + [pltpu.VMEM((3,) + s, F32) for s in group_shapes]
                        + [pltpu.VMEM(s, F32) for s in group_shapes]
                        + [pltpu.VMEM(_small_shape(n), F32) for n in names]
                        + [pltpu.SemaphoreType.DMA((n_sems,)), pltpu.SemaphoreType.DMA((n_sems,)),
                           pltpu.SemaphoreType.DMA((landed_sem0 + len(early_small),))]),
        compiler_params=_params(48),
    )(*big, *[small[n] for n in names], *[landed[i] for i in early], *[landed_small[n] for n in early_small])
    return list(outs[:n_t]), dict(zip(names, outs[n_t:n_t + len(names)]))


def _adamw_update(w, g, m, v):
    m = ADAM_B1 * m + (1.0 - ADAM_B1) * g
    v = ADAM_B2 * v + (1.0 - ADAM_B2) * (g * g)
    m_hat = m / (1.0 - ADAM_B1 ** ADAM_STEP)
    v_hat = v / (1.0 - ADAM_B2 ** ADAM_STEP)
    return -ADAM_LR * (m_hat / (jnp.sqrt(v_hat) + ADAM_EPS) + ADAM_WD * w), m, v


def _adamw(w, g, m, v, grid, name):
    n_t = len(w)

    def body(*refs):
        ins, outs = refs[:4 * n_t], refs[4 * n_t:]
        for i in range(n_t):
            w_, g_, m_, v_ = [ins[a * n_t + i][...] for a in range(4)]
            vals = (g_,) + _adamw_update(w_, g_, m_, v_)
            for a in range(4):
                outs[a * n_t + i][...] = vals[a]

    specs = [pl.BlockSpec((a.shape[0] // grid, a.shape[1]), lambda i: (i, 0)) for a in w]
    shapes = [_sds(a.shape) for a in w]
    outs = _call(
        body, name=name, grid=(grid,), in_specs=specs * 4, out_specs=specs * 4, out_shape=shapes * 4,
        compiler_params=_params(40, ("arbitrary",)),
    )(*w, *g, *m, *v)
    return [outs[a * n_t:(a + 1) * n_t] for a in range(4)]


def kernel(x, p, pre_norm_g, w_in, ssm_lam_re, ssm_lam_im, ssm_log_step, ssm_b_re, ssm_b_im, ssm_c_re, ssm_c_im, ssm_d, ssm_w_glu, ssm_b_glu, attn_sinks, w_out, post_norm_g, pl_w_proj, pl_w_gate, pl_b_gate, loss_target, m_pre_norm_g, m_w_in, m_ssm_lam_re, m_ssm_lam_im, m_ssm_log_step, m_ssm_b_re, m_ssm_b_im, m_ssm_c_re, m_ssm_c_im, m_ssm_d, m_ssm_w_glu, m_ssm_b_glu, m_attn_sinks, m_w_out, m_post_norm_g, m_pl_w_proj, m_pl_w_gate, m_pl_b_gate, v_pre_norm_g, v_w_in, v_ssm_lam_re, v_ssm_lam_im, v_ssm_log_step, v_ssm_b_re, v_ssm_b_im, v_ssm_c_re, v_ssm_c_im, v_ssm_d, v_ssm_w_glu, v_ssm_b_glu, v_attn_sinks, v_w_out, v_post_norm_g, v_pl_w_proj, v_pl_w_gate, v_pl_b_gate):
    weights = dict(pre_norm_g=pre_norm_g, w_in=w_in, ssm_lam_re=ssm_lam_re, ssm_lam_im=ssm_lam_im,
                   ssm_log_step=ssm_log_step, ssm_b_re=ssm_b_re, ssm_b_im=ssm_b_im, ssm_c_re=ssm_c_re,
                   ssm_c_im=ssm_c_im, ssm_d=ssm_d, ssm_w_glu=ssm_w_glu, ssm_b_glu=ssm_b_glu, attn_sinks=attn_sinks,
                   w_out=w_out, post_norm_g=post_norm_g, pl_w_proj=pl_w_proj, pl_w_gate=pl_w_gate, pl_b_gate=pl_b_gate)
    m_in = dict(pre_norm_g=m_pre_norm_g, w_in=m_w_in, ssm_lam_re=m_ssm_lam_re, ssm_lam_im=m_ssm_lam_im,
                ssm_log_step=m_ssm_log_step, ssm_b_re=m_ssm_b_re, ssm_b_im=m_ssm_b_im, ssm_c_re=m_ssm_c_re,
                ssm_c_im=m_ssm_c_im, ssm_d=m_ssm_d, ssm_w_glu=m_ssm_w_glu, ssm_b_glu=m_ssm_b_glu,
                attn_sinks=m_attn_sinks, w_out=m_w_out, post_norm_g=m_post_norm_g, pl_w_proj=m_pl_w_proj,
                pl_w_gate=m_pl_w_gate, pl_b_gate=m_pl_b_gate)
    v_in = dict(pre_norm_g=v_pre_norm_g, w_in=v_w_in, ssm_lam_re=v_ssm_lam_re, ssm_lam_im=v_ssm_lam_im,
                ssm_log_step=v_ssm_log_step, ssm_b_re=v_ssm_b_re, ssm_b_im=v_ssm_b_im, ssm_c_re=v_ssm_c_re,
                ssm_c_im=v_ssm_c_im, ssm_d=v_ssm_d, ssm_w_glu=v_ssm_w_glu, ssm_b_glu=v_ssm_b_glu,
                attn_sinks=v_attn_sinks, w_out=v_w_out, post_norm_g=v_post_norm_g, pl_w_proj=v_pl_w_proj,
                pl_w_gate=v_pl_w_gate, pl_b_gate=v_pl_b_gate)

    def two_d(tree):
        return {k: _to_kernel_form(k, a) for k, a in tree.items()}

    w2, m2, v2 = two_d(weights), two_d(m_in), two_d(v_in)

    (w_in_full,), gathered = _gather_weights([w2["w_in"].astype(BF16)])
    rest = _gather_weights_beside([(w2[n] + gathered).astype(BF16) for n in _BIG[1:]])
    full = dict(zip(_BIG, [w_in_full] + rest))
    s5_params = tuple(w2[n] for n in ("ssm_lam_re", "ssm_lam_im", "ssm_log_step", "ssm_b_re", "ssm_b_im", "ssm_c_re",
                                      "ssm_c_im"))
    grad_x, loss, grads = _local_step(
        x, p, loss_target, w2["pre_norm_g"], full["w_in"], s5_params, w2["ssm_d"], full["ssm_w_glu"], w2["ssm_b_glu"],
        w2["attn_sinks"], full["w_out"], w2["post_norm_g"], full["pl_w_proj"], full["pl_w_gate"], w2["pl_b_gate"])

    sent_early = ("w_out", "pl_w_gate", "pl_w_proj")
    landed = dict(zip([_BIG.index(n) for n in sent_early], _scatter_beside([grads[n] for n in sent_early])))
    after_scatter = landed[_BIG.index(sent_early[-1])][0, 0, 0] * 0.0
    landed_small = dict(zip(_SMALL_EARLY, _broadcast_beside([grads[n] + after_scatter for n in _SMALL_EARLY])))
    g_big, g_small = _exchange_grads([grads[n] for n in _BIG], {**{n: grads[n] for n in _SMALL}, "loss": loss}, landed,
                                     landed_small)
    g_big = dict(zip(_BIG, g_big))
    total_loss = g_small.pop("loss")

    big_out = _adamw([w2[n] for n in _BIG], [g_big[n] for n in _BIG], [m2[n] for n in _BIG], [v2[n] for n in _BIG],
                     8, "adamw_matrices")
    small_names = tuple(_SMALL)
    small_out = _adamw([w2[n] for n in small_names], [g_small[n] for n in small_names], [m2[n] for n in small_names],
                       [v2[n] for n in small_names], 1, "adamw_small")

    results = [{**dict(zip(_BIG, big_part)), **dict(zip(small_names, small_part))}
               for big_part, small_part in zip(big_out, small_out)]
    flat = [_from_kernel_form(name, r[name], weights[name].shape) for r in results for name in _WEIGHT_ORDER]
    return (total_loss.reshape(()), grad_x, *flat)
```

```python
import math

import jax
import jax.numpy as jnp
from jax import lax
from jax.experimental import pallas as pl
from jax.experimental.pallas import tpu as pltpu
from jax.experimental.pallas import tpu_sc as plsc

F32 = jnp.float32
BF16 = jnp.bfloat16

D_MODEL = 1024
D_SSM = 512
D_ATTN = 512
SSM_GROUPS = 32
SSM_GROUP_CH = 16
SSM_STATE = 64
SSM_LANES = SSM_GROUPS * SSM_STATE
HEAD_DIM = 64
N_HEADS = 8
KV_HEADS = 2
Q_PER_KV = 4
WINDOW = 128
BLOCK = 128
D_PLE = 256
D_IN = 2304
EPS = 1e-6
ATTN_SCALE = 1.0 / math.sqrt(HEAD_DIM)

ADAM_LR = 0.001
ADAM_B1 = 0.9
ADAM_B2 = 0.999
ADAM_EPS = 1e-08
ADAM_WD = 0.01
ADAM_STEP = 10

N_CHIPS = 4
LANES = 128
SCAN_CHUNKS = 8
SCAN_TILE_STEPS = 32
SCAN_LANE_CHUNK = 512
MIB = 2 ** 20
MESH = pl.DeviceIdType.MESH


def _dot(a, b):
    return jnp.dot(a, b, preferred_element_type=F32)


def _dot_nt(a, b):
    return lax.dot_general(a, b, (((1,), (1,)), ((), ())), preferred_element_type=F32)


def _dot_tn(a, b):
    return lax.dot_general(a, b, (((0,), (0,)), ((), ())), preferred_element_type=F32)


def _params(vmem_mib, semantics=None):
    kw = dict(vmem_limit_bytes=vmem_mib * MIB)
    if semantics is not None:
        kw["dimension_semantics"] = semantics
    return pltpu.CompilerParams(**kw)


def _full(shape):
    nd = len(shape)
    return pl.BlockSpec(shape, lambda *_: (0,) * nd, pipeline_mode=pl.Buffered(1))


def _rows(tm, width):
    return pl.BlockSpec((tm, width), lambda i: (i, 0))


def _sds(shape, dtype=F32):
    return pltpu.HBM(shape, dtype)


def _call(body, **kw):
    fn = pl.pallas_call(body, **kw)
    return lambda *args: fn(*[pltpu.with_memory_space_constraint(a, pltpu.HBM) for a in args])


def _silu(z):
    return z * jax.nn.sigmoid(z)


def _in_proj(x2d, g1, w_in_t, n_seq, seq):
    rows = x2d.shape[0]
    tm = 512
    slab, steps, _, _ = _scan_geometry(n_seq, seq)

    def body(x_ref, g_ref, w_ref, *out_refs):
        u_parts, (zs_ref, q_ref, k_ref, v_ref, za_ref) = out_refs[:_SCAN_PARTS], out_refs[_SCAN_PARTS:]
        x = x_ref[...]
        r = lax.rsqrt(jnp.mean(x * x, axis=-1, keepdims=True) + EPS)
        hn = (x * r * g_ref[...]).astype(BF16)

        def proj(a, b):
            return _dot_nt(hn, w_ref[a:b, :])

        _store_chunks(u_parts, pl.program_id(0) * (tm // steps), proj(0, 512), steps, slab)
        zs_ref[...] = proj(512, 1024)
        q_ref[...] = (proj(1024, 1536) * ATTN_SCALE).astype(BF16)
        k_ref[...] = proj(1536, 1664).astype(BF16)
        v_ref[...] = proj(1664, 1792).astype(BF16)
        za_ref[...] = proj(1792, 2304)

    *u_parts, zs, q, k, v, za = _call(
        body, name="in_proj", grid=(rows // tm,),
        in_specs=[_rows(tm, D_MODEL), _full((1, D_MODEL)), _full((D_IN, D_MODEL))],
        out_specs=_whole_parts(rows) + [_rows(tm, 512), _rows(tm, 512), _rows(tm, 128), _rows(tm, 128), _rows(tm, 512)],
        out_shape=_part_shapes(rows) + [_sds((rows, 512)), _sds((rows, 512), BF16), _sds((rows, 128), BF16),
                                        _sds((rows, 128), BF16), _sds((rows, 512))],
        compiler_params=_params(48, ("arbitrary",)),
    )(x2d, g1, w_in_t)
    return u_parts, zs, q, k, v, za


def _in_proj_bwd(x2d, dh1, g1, w_in_t, du_parts, dzs, dq, dk, dv, dza, n_seq, seq):
    rows = x2d.shape[0]
    tm = 512
    slab, steps, _, _ = _scan_geometry(n_seq, seq)

    def body(x_ref, dh1_ref, g_ref, w_ref, *refs):
        du_parts, (dzs_ref, dq_ref, dk_ref, dv_ref, dza_ref, gx_ref, dw_ref, dg_ref) = refs[:_SCAN_PARTS], refs[_SCAN_PARTS:]

        @pl.when(pl.program_id(0) == 0)
        def _():
            dw_ref[...] = jnp.zeros_like(dw_ref)
            dg_ref[...] = jnp.zeros_like(dg_ref)

        x = x_ref[...]
        g = g_ref[...]
        r = lax.rsqrt(jnp.mean(x * x, axis=-1, keepdims=True) + EPS)
        xr = x * r
        hn = (xr * g).astype(BF16)
        du = _load_chunks(du_parts, pl.program_id(0) * (tm // steps), tm // steps, steps, slab)
        d_proj = jnp.concatenate([du.astype(BF16), dzs_ref[...], dq_ref[...], dk_ref[...], dv_ref[...], dza_ref[...]],
                                 axis=1)
        dhn = _dot(d_proj, w_ref[...])
        dw_ref[...] += _dot_tn(d_proj, hn)
        dg_ref[...] += jnp.sum(dhn * xr, axis=0, keepdims=True)
        a_ = dhn * g
        gx_ref[...] = dh1_ref[...] + r * a_ - xr * (r * jnp.mean(a_ * xr, axis=-1, keepdims=True))

    return _call(
        body, name="in_proj_bwd", grid=(rows // tm,),
        in_specs=[_rows(tm, D_MODEL), _rows(tm, D_MODEL), _full((1, D_MODEL)), _full((D_IN, D_MODEL))]
        + _whole_parts(rows) + [_rows(tm, 512), _rows(tm, 512), _rows(tm, 128), _rows(tm, 128), _rows(tm, 512)],
        out_specs=[_rows(tm, D_MODEL), _full((D_IN, D_MODEL)), _full((1, D_MODEL))],
        out_shape=[_sds((rows, D_MODEL)), _sds((D_IN, D_MODEL)), _sds((1, D_MODEL))],
        compiler_params=_params(56, ("arbitrary",)),
    )(x2d, dh1, g1, w_in_t, *du_parts, dzs, dq, dk, dv, dza)


def _iota(shape, axis):
    return lax.broadcasted_iota(jnp.int32, shape, axis)


def _exact_dot(a, b):
    return jnp.dot(a, b, precision=lax.Precision.HIGHEST, preferred_element_type=F32)


_HALF_GROUPS = SSM_GROUPS // 2
_N_SHIFT = SSM_STATE.bit_length() - 1
_P_SHIFT = SSM_GROUP_CH.bit_length() - 1


def _s5_operands(lam_re, lam_im, log_step, b_re, b_im, c_re, c_im):
    g, n, p = SSM_GROUPS, SSM_STATE, SSM_GROUP_CH
    gn, gp, hn_, hp = g * n, g * p, _HALF_GROUPS * n, _HALF_GROUPS * p
    eye_g = _iota((g, g), 0) == _iota((g, g), 1)
    step = jnp.sum(jnp.where(eye_g, jnp.exp(log_step), 0.0), axis=1, keepdims=True)
    a_re = lam_re * step
    a_im = lam_im * step
    mag = jnp.exp(a_re)
    lbar_re = mag * jnp.cos(a_im)
    lbar_im = mag * jnp.sin(a_im)
    n_re = lbar_re - 1.0
    den = lam_re * lam_re + lam_im * lam_im
    f_re = (n_re * lam_re + lbar_im * lam_im) / den
    f_im = (lbar_im * lam_re - n_re * lam_im) / den

    spread_n = (_iota((n, gn), 0) == (_iota((n, gn), 1) & (n - 1))).astype(F32)
    own_g = _iota((g, gn), 0) == (_iota((g, gn), 1) >> _N_SHIFT)

    def to_row(a):
        return jnp.sum(jnp.where(own_g, _exact_dot(a, spread_n), 0.0), axis=0, keepdims=True)

    per_group = ((_iota((gp, g), 0) >> _P_SHIFT) == _iota((gp, g), 1)).astype(F32)
    fx_re, fx_im = _exact_dot(per_group, f_re), _exact_dot(per_group, f_im)
    bbar_re = fx_re * b_re - fx_im * b_im
    bbar_im = fx_re * b_im + fx_im * b_re

    tile_n = (_iota((n, hn_), 0) == (_iota((n, hn_), 1) & (n - 1))).astype(F32)
    same_group = (_iota((hp, hn_), 0) >> _P_SHIFT) == (_iota((hp, hn_), 1) >> _N_SHIFT)

    def embed(a, hf):
        return jnp.where(same_group, _exact_dot(a[hf * hp:(hf + 1) * hp], tile_n), 0.0)

    return (to_row(lbar_re), to_row(lbar_im), embed(bbar_re, 0), embed(bbar_re, 1), embed(bbar_im, 0),
            embed(bbar_im, 1), embed(c_re, 0), embed(c_re, 1), embed(c_im, 0), embed(c_im, 1))


_S5_PARAM_SHAPES = ((SSM_GROUPS, SSM_STATE), (SSM_GROUPS, SSM_STATE), (1, SSM_GROUPS),
                    (D_SSM, SSM_STATE), (D_SSM, SSM_STATE), (D_SSM, SSM_STATE), (D_SSM, SSM_STATE))
_CM_SHAPE = (2, _HALF_GROUPS * SSM_GROUP_CH, _HALF_GROUPS * SSM_STATE)
_S5_OPERAND_SHAPES = ((1, SSM_LANES), (1, SSM_LANES), _CM_SHAPE, _CM_SHAPE, _CM_SHAPE, _CM_SHAPE)


def _s5_params_fwd(*params):
    def body(*refs):
        ins, (lre_ref, lim_ref, btre_ref, btim_ref, cmre_ref, cmim_ref) = refs[:7], refs[7:]
        vals = _s5_operands(*[r[...] for r in ins])
        lre_ref[...] = vals[0]
        lim_ref[...] = vals[1]
        for ref, pair in zip((btre_ref, btim_ref, cmre_ref, cmim_ref), (vals[2:4], vals[4:6], vals[6:8], vals[8:10])):
            ref[0] = pair[0].astype(BF16)
            ref[1] = pair[1].astype(BF16)

    dtypes = (F32, F32, BF16, BF16, BF16, BF16)
    return _call(
        body, name="s5_params_fwd",
        in_specs=[_full(s) for s in _S5_PARAM_SHAPES], out_specs=[_full(s) for s in _S5_OPERAND_SHAPES],
        out_shape=[_sds(s, d) for s, d in zip(_S5_OPERAND_SHAPES, dtypes)], compiler_params=_params(32),
    )(*params)


def _s5_params_bwd(params, cotangents):
    def body(*refs):
        ins, (dlre, dlim, dbtre, dbtim, dcmre, dcmim), outs = refs[:7], refs[7:13], refs[13:]
        _, vjp = jax.vjp(_s5_operands, *[r[...] for r in ins])
        cts = (dlre[...], dlim[...], dbtre[0], dbtre[1], dbtim[0], dbtim[1], dcmre[0], dcmre[1], dcmim[0], dcmim[1])
        for ref, val in zip(outs, vjp(cts)):
            ref[...] = val

    return _call(
        body, name="s5_params_bwd",
        in_specs=[_full(s) for s in _S5_PARAM_SHAPES + _S5_OPERAND_SHAPES],
        out_specs=[_full(s) for s in _S5_PARAM_SHAPES],
        out_shape=[_sds(s) for s in _S5_PARAM_SHAPES], compiler_params=_params(48),
    )(*params, *cotangents)


def _scan_geometry(n_seq, seq):
    slab = n_seq * SCAN_CHUNKS
    steps = seq // SCAN_CHUNKS
    tile_rows = slab * SCAN_TILE_STEPS
    n_tiles = steps // SCAN_TILE_STEPS
    return slab, steps, tile_rows, n_tiles


_SCAN_PARTS = D_SSM // LANES


def _whole_parts(rows):
    return [_full((rows, LANES))] * _SCAN_PARTS


def _part_shapes(rows):
    return [_sds((rows, LANES))] * _SCAN_PARTS


def _load_chunks(parts, first_chunk, n_chunks, steps, slab):
    return jnp.concatenate([
        jnp.concatenate([ref[pl.ds(first_chunk + q, steps, stride=slab), :] for ref in parts], axis=1)
        for q in range(n_chunks)], axis=0)


def _store_chunks(parts, first_chunk, value, steps, slab):
    for q in range(value.shape[0] // steps):
        for j, ref in enumerate(parts):
            ref[pl.ds(first_chunk + q, steps, stride=slab), :] = value[q * steps:(q + 1) * steps,
                                                                     j * LANES:(j + 1) * LANES]


def _join_parts(parts):
    return jnp.concatenate([ref[...] for ref in parts], axis=1)


def _split_parts(parts, value):
    for j, ref in enumerate(parts):
        ref[...] = value[:, j * LANES:(j + 1) * LANES]


def _complex_power(re, im, n):
    out = None
    while n:
        if n & 1:
            out = (re, im) if out is None else (out[0] * re - out[1] * im, out[0] * im + out[1] * re)
        n >>= 1
        if n:
            re, im = re * re - im * im, 2.0 * re * im
    return out


def _chunk_carry(sum_re, sum_im, carry_re, carry_im, a_re, a_im, n_seq, reverse):
    carry_re[...] = jnp.zeros_like(carry_re)
    carry_im[...] = jnp.zeros_like(carry_im)
    for s in range(n_seq):
        order = range(SCAN_CHUNKS - 2, -1, -1) if reverse else range(1, SCAN_CHUNKS)
        for c in order:
            r = s * SCAN_CHUNKS + c
            p = r + 1 if reverse else r - 1
            p_re, p_im = carry_re[p:p + 1, :], carry_im[p:p + 1, :]
            carry_re[r:r + 1, :] = a_re * p_re - a_im * p_im + sum_re[p:p + 1, :]
            carry_im[r:r + 1, :] = a_re * p_im + a_im * p_re + sum_im[p:p + 1, :]


def _s5_scan_fwd(u_parts, bt_re, bt_im, cm_re, cm_im, lbar_re, lbar_im, d_row, n_seq, seq):
    slab, steps, tile_rows, n_tiles = _scan_geometry(n_seq, seq)
    rows = u_parts[0].shape[0]

    def body(*refs):
        u_refs, refs = refs[:_SCAN_PARTS], refs[_SCAN_PARTS:]
        (bre_ref, bim_ref, cre_ref, cim_ref, lre_ref, lim_ref, d_ref), refs = refs[:7], refs[7:]
        y_refs, (hre_ref, him_ref, st_re, st_im, h0_re, h0_im, buf_re, buf_im) = refs[:_SCAN_PARTS], refs[_SCAN_PARTS:]
        second = pl.program_id(0) == 1
        i = pl.program_id(1)

        @pl.when(jnp.logical_and(i == 0, jnp.logical_not(second)))
        def _():
            st_re[...] = jnp.zeros_like(st_re)
            st_im[...] = jnp.zeros_like(st_im)

        u = _join_parts(u_refs)
        ub = u.astype(BF16)
        for hf in range(2):
            cols = slice(hf * 1024, (hf + 1) * 1024)
            buf_re[:, cols] = _dot(ub[:, hf * 256:(hf + 1) * 256], bre_ref[hf])
            buf_im[:, cols] = _dot(ub[:, hf * 256:(hf + 1) * 256], bim_ref[hf])

        for lc in range(SSM_LANES // SCAN_LANE_CHUNK):
            cols = slice(lc * SCAN_LANE_CHUNK, (lc + 1) * SCAN_LANE_CHUNK)
            l_re = jnp.broadcast_to(lre_ref[:, cols], (slab, SCAN_LANE_CHUNK))
            l_im = jnp.broadcast_to(lim_ref[:, cols], (slab, SCAN_LANE_CHUNK))

            def scan_tile(keep_states):
                def step(t, carry):
                    s_re, s_im = carry
                    r0 = pl.multiple_of(t * slab, slab)
                    n_re = l_re * s_re - l_im * s_im + buf_re[pl.ds(r0, slab), cols]
                    n_im = l_re * s_im + l_im * s_re + buf_im[pl.ds(r0, slab), cols]
                    if keep_states:
                        buf_re[pl.ds(r0, slab), cols] = n_re
                        buf_im[pl.ds(r0, slab), cols] = n_im
                    return n_re, n_im

                s_re, s_im = lax.fori_loop(0, SCAN_TILE_STEPS, step, (st_re[:, cols], st_im[:, cols]), unroll=True)
                st_re[:, cols] = s_re
                st_im[:, cols] = s_im

            pl.when(jnp.logical_not(second))(lambda: scan_tile(False))
            pl.when(second)(lambda: scan_tile(True))

        @pl.when(jnp.logical_and(i == n_tiles - 1, jnp.logical_not(second)))
        def _():
            a_re, a_im = _complex_power(lre_ref[...], lim_ref[...], steps)
            _chunk_carry(st_re, st_im, h0_re, h0_im, a_re, a_im, n_seq, reverse=False)
            st_re[...] = h0_re[...]
            st_im[...] = h0_im[...]

        @pl.when(second)
        def _():
            h_re = buf_re[...].astype(BF16)
            h_im = buf_im[...].astype(BF16)
            hre_ref[...] = h_re
            him_ref[...] = h_im
            for hf in range(2):
                cols = slice(hf * 1024, (hf + 1) * 1024)
                ycols = slice(hf * 256, (hf + 1) * 256)
                y_half = (_dot_nt(h_re[:, cols], cre_ref[hf]) - _dot_nt(h_im[:, cols], cim_ref[hf])
                          + d_ref[:, ycols] * u[:, ycols])
                _split_parts(y_refs[2 * hf:2 * hf + 2], y_half)

    tile = lambda w: pl.BlockSpec((tile_rows, w), lambda p, i: (i, 0))
    out_tile = lambda w: pl.BlockSpec((tile_rows, w), lambda p, i: (i * p, 0))
    cm = _full(_CM_SHAPE)
    outs = _call(
        body, name="s5_scan_fwd", grid=(2, n_tiles),
        in_specs=[tile(LANES)] * _SCAN_PARTS + [cm, cm, cm, cm, _full((1, SSM_LANES)), _full((1, SSM_LANES)),
                                                _full((1, 512))],
        out_specs=[out_tile(LANES)] * _SCAN_PARTS + [out_tile(SSM_LANES), out_tile(SSM_LANES)],
        out_shape=_part_shapes(rows) + [_sds((rows, SSM_LANES), BF16), _sds((rows, SSM_LANES), BF16)],
        scratch_shapes=[pltpu.VMEM((slab, SSM_LANES), F32)] * 4 + [pltpu.VMEM((tile_rows, SSM_LANES), F32)] * 2,
        compiler_params=_params(40, ("arbitrary", "arbitrary")),
    )(*u_parts, bt_re, bt_im, cm_re, cm_im, lbar_re, lbar_im, d_row)
    return outs[:_SCAN_PARTS], outs[_SCAN_PARTS], outs[_SCAN_PARTS + 1]


def _s5_scan_bwd(dy_parts, u_parts, h_re, h_im, bt_re, bt_im, cm_re, cm_im, lbar_re, lbar_im, d_row, n_seq, seq):
    slab, steps, tile_rows, n_tiles = _scan_geometry(n_seq, seq)
    rows = u_parts[0].shape[0]

    def body(*refs):
        dy_refs, u_refs, refs = refs[:_SCAN_PARTS], refs[_SCAN_PARTS:2 * _SCAN_PARTS], refs[2 * _SCAN_PARTS:]
        (hre_ref, him_ref, bre_ref, bim_ref, cre_ref, cim_ref, lre_ref, lim_ref, d_ref), refs = refs[:9], refs[9:]
        du_refs, refs = refs[:_SCAN_PARTS], refs[_SCAN_PARTS:]
        (dbre_ref, dbim_ref, dcre_ref, dcim_ref, dlre_ref, dlim_ref, dd_ref,
         st_re, st_im, g0_re, g0_im, acc_re, acc_im, buf_re, buf_im) = refs
        second = pl.program_id(0) == 1
        i = pl.program_id(1)

        @pl.when(jnp.logical_and(i == 0, jnp.logical_not(second)))
        def _():
            st_re[...] = jnp.zeros_like(st_re)
            st_im[...] = jnp.zeros_like(st_im)
            acc_re[...] = jnp.zeros_like(acc_re)
            acc_im[...] = jnp.zeros_like(acc_im)
            for ref in (dbre_ref, dbim_ref, dcre_ref, dcim_ref, dd_ref):
                ref[...] = jnp.zeros_like(ref)

        dy = _join_parts(dy_refs)
        dyb = dy.astype(BF16)
        for hf in range(2):
            cols = slice(hf * 1024, (hf + 1) * 1024)
            buf_re[:, cols] = _dot(dyb[:, hf * 256:(hf + 1) * 256], cre_ref[hf])
            buf_im[:, cols] = -_dot(dyb[:, hf * 256:(hf + 1) * 256], cim_ref[hf])

        for lc in range(SSM_LANES // SCAN_LANE_CHUNK):
            cols = slice(lc * SCAN_LANE_CHUNK, (lc + 1) * SCAN_LANE_CHUNK)
            l_re = jnp.broadcast_to(lre_ref[:, cols], (slab, SCAN_LANE_CHUNK))
            l_im = jnp.broadcast_to(lim_ref[:, cols], (slab, SCAN_LANE_CHUNK))

            def advance(r0, s_re, s_im):
                n_re = l_re * s_re + l_im * s_im + buf_re[pl.ds(r0, slab), cols]
                n_im = l_re * s_im - l_im * s_re + buf_im[pl.ds(r0, slab), cols]
                buf_re[pl.ds(r0, slab), cols] = n_re
                buf_im[pl.ds(r0, slab), cols] = n_im
                return n_re, n_im

            def row0(k):
                return pl.multiple_of((SCAN_TILE_STEPS - 1 - k) * slab, slab)

            @pl.when(jnp.logical_not(second))
            def _():
                s_re, s_im = lax.fori_loop(0, SCAN_TILE_STEPS, lambda k, s: advance(row0(k), *s),
                                           (st_re[:, cols], st_im[:, cols]), unroll=True)
                st_re[:, cols] = s_re
                st_im[:, cols] = s_im

            @pl.when(second)
            def _():
                def step(k, carry):
                    s_re, s_im, a_re, a_im = carry
                    r0 = row0(k)
                    hr = hre_ref[pl.ds(r0, slab), cols].astype(F32)
                    hi = him_ref[pl.ds(r0, slab), cols].astype(F32)
                    a_re = a_re + s_re * hr + s_im * hi
                    a_im = a_im + s_im * hr - s_re * hi
                    return advance(r0, s_re, s_im) + (a_re, a_im)

                zero = jnp.zeros((slab, SCAN_LANE_CHUNK), F32)
                s_re, s_im, a_re, a_im = lax.fori_loop(
                    0, SCAN_TILE_STEPS, step, (st_re[:, cols], st_im[:, cols], zero, zero), unroll=True)
                st_re[:, cols] = s_re
                st_im[:, cols] = s_im
                acc_re[:, cols] += a_re
                acc_im[:, cols] += a_im

        @pl.when(jnp.logical_and(i == n_tiles - 1, jnp.logical_not(second)))
        def _():
            p_re, p_im = _complex_power(lre_ref[...], lim_ref[...], steps)
            _chunk_carry(st_re, st_im, g0_re, g0_im, p_re, -p_im, n_seq, reverse=True)
            st_re[...] = g0_re[...]
            st_im[...] = g0_im[...]

        @pl.when(second)
        def _():
            u = _join_parts(u_refs)
            ub = u.astype(BF16)
            g_re = buf_re[...].astype(BF16)
            g_im = buf_im[...].astype(BF16)
            dd_ref[...] += jnp.sum(dy * u, axis=0, keepdims=True)
            for hf in range(2):
                cols = slice(hf * 1024, (hf + 1) * 1024)
                ycols = slice(hf * 256, (hf + 1) * 256)
                du_half = (_dot_nt(g_re[:, cols], bre_ref[hf]) + _dot_nt(g_im[:, cols], bim_ref[hf])
                           + d_ref[:, ycols] * dy[:, ycols])
                _split_parts(du_refs[2 * hf:2 * hf + 2], du_half)
                for q4 in range(_HALF_GROUPS // 4):
                    ch = slice(hf * 256 + q4 * 64, hf * 256 + (q4 + 1) * 64)
                    st = slice(hf * 1024 + q4 * 256, hf * 1024 + (q4 + 1) * 256)
                    blk = (hf, slice(q4 * 64, (q4 + 1) * 64), slice(q4 * 256, (q4 + 1) * 256))
                    dbre_ref[blk] += _dot_tn(ub[:, ch], g_re[:, st])
                    dbim_ref[blk] += _dot_tn(ub[:, ch], g_im[:, st])
                    dcre_ref[blk] += _dot_tn(dyb[:, ch], hre_ref[:, st])
                    dcim_ref[blk] -= _dot_tn(dyb[:, ch], him_ref[:, st])

        @pl.when(jnp.logical_and(i == n_tiles - 1, second))
        def _():
            dlre_ref[...] = jnp.sum(acc_re[...], axis=0, keepdims=True)
            dlim_ref[...] = jnp.sum(acc_im[...], axis=0, keepdims=True)

    tile = lambda w: pl.BlockSpec((tile_rows, w), lambda p, i: (n_tiles - 1 - i, 0))
    second_tile = lambda w: pl.BlockSpec((tile_rows, w), lambda p, i: (n_tiles - 1 - i * p, 0))
    cm = _full(_CM_SHAPE)
    row = _full((1, SSM_LANES))
    outs = _call(
        body, name="s5_scan_bwd", grid=(2, n_tiles),
        in_specs=[tile(LANES)] * _SCAN_PARTS + [second_tile(LANES)] * _SCAN_PARTS
        + [second_tile(SSM_LANES), second_tile(SSM_LANES), cm, cm, cm, cm, row, row, _full((1, 512))],
        out_specs=[second_tile(LANES)] * _SCAN_PARTS + [cm, cm, cm, cm, row, row, _full((1, 512))],
        out_shape=(_part_shapes(rows) + [_sds(_CM_SHAPE)] * 4 + [_sds((1, SSM_LANES))] * 2 + [_sds((1, 512))]),
        scratch_shapes=[pltpu.VMEM((slab, SSM_LANES), F32)] * 6 + [pltpu.VMEM((tile_rows, SSM_LANES), F32)] * 2,
        compiler_params=_params(48, ("arbitrary", "arbitrary")),
    )(*dy_parts, *u_parts, h_re, h_im, bt_re, bt_im, cm_re, cm_im, lbar_re, lbar_im, d_row)
    return (outs[:_SCAN_PARTS],) + tuple(outs[_SCAN_PARTS:])


def _glu_gate(gl, a, zs):
    return gl * jax.nn.sigmoid(a) * _silu(zs)


def _glu_fwd(y_parts, zs, w_glu, b_glu, n_seq, seq):
    rows = zs.shape[0]
    tm = 512
    slab, steps, _, _ = _scan_geometry(n_seq, seq)

    def body(*refs):
        y_refs, (zs_ref, w_ref, b_ref, o_ref) = refs[:_SCAN_PARTS], refs[_SCAN_PARTS:]
        y = _load_chunks(y_refs, pl.program_id(0) * (tm // steps), tm // steps, steps, slab)
        gl = jax.nn.gelu(y)
        a = _dot(gl.astype(BF16), w_ref[...]) + b_ref[...]
        o_ref[...] = _glu_gate(gl, a, zs_ref[...]).astype(BF16)

    return _call(
        body, name="glu_fwd", grid=(rows // tm,),
        in_specs=_whole_parts(rows) + [_rows(tm, 512), _full((512, 512)), _full((1, 512))],
        out_specs=_rows(tm, 512), out_shape=_sds((rows, 512), BF16),
        compiler_params=_params(32, ("arbitrary",)),
    )(*y_parts, zs, w_glu, b_glu)


def _glu_bwd(y_parts, zs, d_out, w_glu, b_glu, n_seq, seq):
    rows = zs.shape[0]
    tm = 512
    slab, steps, _, _ = _scan_geometry(n_seq, seq)

    def body(*refs):
        y_refs, (zs_ref, d_ref, w_ref, b_ref), refs = refs[:_SCAN_PARTS], refs[_SCAN_PARTS:_SCAN_PARTS + 4], refs[_SCAN_PARTS + 4:]
        dy_refs, (dzs_ref, dw_ref, db_ref) = refs[:_SCAN_PARTS], refs[_SCAN_PARTS:]
        first_chunk = pl.program_id(0) * (tm // steps)

        @pl.when(pl.program_id(0) == 0)
        def _():
            dw_ref[...] = jnp.zeros_like(dw_ref)
            db_ref[...] = jnp.zeros_like(db_ref)

        gl, gelu_vjp = jax.vjp(jax.nn.gelu, _load_chunks(y_refs, first_chunk, tm // steps, steps, slab))
        glb = gl.astype(BF16)
        a = _dot(glb, w_ref[...]) + b_ref[...]
        _, gate_vjp = jax.vjp(_glu_gate, gl, a, zs_ref[...])
        d_gl, d_a, d_zs = gate_vjp(d_ref[...])
        dab = d_a.astype(BF16)
        d_gl = d_gl + _dot_nt(dab, w_ref[...])
        _store_chunks(dy_refs, first_chunk, gelu_vjp(d_gl)[0], steps, slab)
        dzs_ref[...] = d_zs.astype(BF16)
        dw_ref[...] += _dot_tn(glb, dab)
        db_ref[...] += jnp.sum(d_a, axis=0, keepdims=True)

    *dy_parts, dzs, dw, db = _call(
        body, name="glu_bwd", grid=(rows // tm,),
        in_specs=_whole_parts(rows) + [_rows(tm, 512), _rows(tm, 512), _full((512, 512)), _full((1, 512))],
        out_specs=_whole_parts(rows) + [_rows(tm, 512), _full((512, 512)), _full((1, 512))],
        out_shape=_part_shapes(rows) + [_sds((rows, 512), BF16), _sds((512, 512)), _sds((1, 512))],
        compiler_params=_params(40, ("arbitrary",)),
    )(*y_parts, zs, d_out, w_glu, b_glu)
    return dy_parts, dzs, dw, db


_GROUP_ROWS = Q_PER_KV * BLOCK
_BLOCK_SHIFT = BLOCK.bit_length() - 1


def _attn_bias(j):
    row = _iota((_GROUP_ROWS, BLOCK), 0)
    dist_cur = (row & (BLOCK - 1)) - _iota((_GROUP_ROWS, BLOCK), 1)
    dist_prev = dist_cur + BLOCK
    head = row >> _BLOCK_SHIFT
    slope = jnp.zeros((_GROUP_ROWS, BLOCK), F32)
    for g in range(Q_PER_KV):
        slope = jnp.where(head == g, 2.0 ** (-(j * Q_PER_KV + g + 1)), slope)
    bias_cur = jnp.where(dist_cur >= 0, -slope * dist_cur.astype(F32), -jnp.inf)
    bias_prev = jnp.where(dist_prev < WINDOW, -slope * dist_prev.astype(F32), -jnp.inf)
    return bias_cur, bias_prev


_ATTN_BIAS_SCRATCH = pltpu.VMEM((KV_HEADS, 2, _GROUP_ROWS, BLOCK), F32)


def _fill_attn_bias(bias_ref):
    @pl.when(jnp.logical_and(pl.program_id(0) == 0, pl.program_id(1) == 0))
    def _():
        for j in range(KV_HEADS):
            bias_ref[j, 0], bias_ref[j, 1] = _attn_bias(j)


def _stack_heads(x, j):
    heads = range(j * Q_PER_KV, (j + 1) * Q_PER_KV)
    return jnp.concatenate([x[:, h * HEAD_DIM:(h + 1) * HEAD_DIM] for h in heads], axis=0)


def _stack_columns(x, j):
    heads = range(j * Q_PER_KV, (j + 1) * Q_PER_KV)
    return jnp.concatenate([jnp.broadcast_to(x[:, h:h + 1], (BLOCK, 1)) for h in heads], axis=0)


def _attn_fwd(q, k, v, za, sinks, n_seq, seq):
    nb = seq // BLOCK
    rows = q.shape[0]

    def body(q_ref, kc_ref, kp_ref, vc_ref, vp_ref, za_ref, sk_ref, o_ref, ao_ref, lse_ref, bias_ref):
        _fill_attn_bias(bias_ref)
        has_prev = pl.program_id(1) > 0
        q_all = q_ref[...]
        for j in range(KV_HEADS):
            js = slice(j * HEAD_DIM, (j + 1) * HEAD_DIM)
            bias_c, bias_p = bias_ref[j, 0], bias_ref[j, 1]
            q4 = _stack_heads(q_all, j)
            sc = _dot_nt(q4, kc_ref[:, js]) + bias_c
            sp = _dot_nt(q4, kp_ref[:, js]) + jnp.where(has_prev, bias_p, -jnp.inf)
            sink = _stack_columns(sk_ref[...], j)
            m = jnp.maximum(jnp.max(jnp.maximum(sc, sp), axis=-1, keepdims=True), sink)
            ec = jnp.exp(sc - m)
            ep = jnp.exp(sp - m)
            den = jnp.sum(ec + ep, axis=-1, keepdims=True) + jnp.exp(sink - m)
            inv = 1.0 / den
            o4 = _dot((ec * inv).astype(BF16), vc_ref[:, js]) + _dot((ep * inv).astype(BF16), vp_ref[:, js])
            lse4 = m + jnp.log(den)
            for g in range(Q_PER_KV):
                h = j * Q_PER_KV + g
                o_ref[:, h * HEAD_DIM:(h + 1) * HEAD_DIM] = o4[g * BLOCK:(g + 1) * BLOCK]
                lse_ref[:, h:h + 1] = lse4[g * BLOCK:(g + 1) * BLOCK]
        ao_ref[...] = (o_ref[...] * _silu(za_ref[...])).astype(BF16)

    cur = lambda w: pl.BlockSpec((BLOCK, w), lambda b, n: (b * nb + n, 0))
    prev = lambda w: pl.BlockSpec((BLOCK, w), lambda b, n: (b * nb + jnp.maximum(n - 1, 0), 0))
    return _call(
        body, name="attn_fwd", grid=(n_seq, nb),
        in_specs=[cur(512), cur(128), prev(128), cur(128), prev(128), cur(512), _full((1, N_HEADS))],
        out_specs=[cur(512), cur(512), cur(N_HEADS)],
        out_shape=[_sds((rows, 512)), _sds((rows, 512), BF16), _sds((rows, N_HEADS))],
        scratch_shapes=[_ATTN_BIAS_SCRATCH], compiler_params=_params(32, ("arbitrary", "arbitrary")),
    )(q, k, k, v, v, za, sinks)


def _attn_bwd(q, k, v, za, o, lse, d_ao, sinks, n_seq, seq):
    nb = seq // BLOCK
    rows = q.shape[0]

    def body(q_ref, kc_ref, kp_ref, vc_ref, vp_ref, za_ref, o_ref, lse_ref, d_ref, sk_ref,
             dq_ref, dk_ref, dv_ref, dza_ref, dsk_ref, bias_ref, dk_carry, dv_carry):
        n = nb - 1 - pl.program_id(1)
        _fill_attn_bias(bias_ref)

        @pl.when(jnp.logical_and(pl.program_id(0) == 0, pl.program_id(1) == 0))
        def _():
            dsk_ref[...] = jnp.zeros_like(dsk_ref)
            dk_carry[...] = jnp.zeros_like(dk_carry)
            dv_carry[...] = jnp.zeros_like(dv_carry)

        has_prev = n > 0
        has_next = n + 1 < nb

        _, gate_vjp = jax.vjp(lambda o_, z_: o_ * _silu(z_), o_ref[...], za_ref[...])
        d_o, d_za = gate_vjp(d_ref[...])
        dza_ref[...] = d_za.astype(BF16)
        q_all = q_ref[...]
        lse_all = lse_ref[...]

        for j in range(KV_HEADS):
            js = slice(j * HEAD_DIM, (j + 1) * HEAD_DIM)
            kc, kp, vc, vp = kc_ref[:, js], kp_ref[:, js], vc_ref[:, js], vp_ref[:, js]
            bias_c, bias_p = bias_ref[j, 0], bias_ref[j, 1]
            q4 = _stack_heads(q_all, j)
            do4b = _stack_heads(d_o, j).astype(BF16)
            lse4 = _stack_columns(lse_all, j)
            pc = jnp.exp(_dot_nt(q4, kc) + bias_c - lse4)
            pp = jnp.exp(_dot_nt(q4, kp) + jnp.where(has_prev, bias_p, -jnp.inf) - lse4)
            dpc = _dot_nt(do4b, vc)
            dpp = _dot_nt(do4b, vp)
            delta = jnp.sum(pc * dpc + pp * dpp, axis=-1, keepdims=True)
            dsc = (pc * (dpc - delta)).astype(BF16)
            dsp = (pp * (dpp - delta)).astype(BF16)
            dq4 = ((_dot(dsc, kc) + _dot(dsp, kp)) * ATTN_SCALE).astype(BF16)
            sink_loss = jnp.exp(_stack_columns(sk_ref[...], j) - lse4) * delta
            for g in range(Q_PER_KV):
                h = j * Q_PER_KV + g
                dq_ref[:, h * HEAD_DIM:(h + 1) * HEAD_DIM] = dq4[g * BLOCK:(g + 1) * BLOCK]
                dsk_ref[0:1, h:h + 1] -= jnp.sum(sink_loss[g * BLOCK:(g + 1) * BLOCK], axis=0, keepdims=True)
            dk = _dot_tn(dsc, q4) + jnp.where(has_next, dk_carry[j], 0.0)
            dv = _dot_tn(pc.astype(BF16), do4b) + jnp.where(has_next, dv_carry[j], 0.0)
            dk_carry[j] = _dot_tn(dsp, q4)
            dv_carry[j] = _dot_tn(pp.astype(BF16), do4b)
            dk_ref[:, js] = dk.astype(BF16)
            dv_ref[:, js] = dv.astype(BF16)

    cur = lambda w: pl.BlockSpec((BLOCK, w), lambda b, s: (b * nb + nb - 1 - s, 0))
    prev = lambda w: pl.BlockSpec((BLOCK, w), lambda b, s: (b * nb + jnp.maximum(nb - 2 - s, 0), 0))
    return _call(
        body, name="attn_bwd", grid=(n_seq, nb),
        in_specs=[cur(512), cur(128), prev(128), cur(128), prev(128), cur(512), cur(512), cur(N_HEADS), cur(512),
                  _full((1, N_HEADS))],
        out_specs=[cur(512), cur(128), cur(128), cur(512), _full((1, N_HEADS))],
        out_shape=[_sds((rows, 512), BF16), _sds((rows, 128), BF16), _sds((rows, 128), BF16),
                   _sds((rows, 512), BF16), _sds((1, N_HEADS))],
        scratch_shapes=[_ATTN_BIAS_SCRATCH, pltpu.VMEM((KV_HEADS, BLOCK, HEAD_DIM), F32),
                        pltpu.VMEM((KV_HEADS, BLOCK, HEAD_DIM), F32)],
        compiler_params=_params(32, ("arbitrary", "arbitrary")),
    )(q, k, k, v, v, za, o, lse, d_ao, sinks)


def _tail(ssm_out, attn_out, x2d, p2d, target, w_out, g2, w_gate, b_gate, w_proj):
    rows = x2d.shape[0]
    tm = 512

    def body(so_ref, ao_ref, x_ref, p_ref, t_ref, wo_ref, g2_ref, wg_ref, bg_ref, wp_ref,
             dh1_ref, dso_ref, dao_ref, dwo_ref, dwg_ref, dwp_ref, dbg_ref, dg2_ref, loss_ref):
        @pl.when(pl.program_id(0) == 0)
        def _():
            for ref in (dwo_ref, dwg_ref, dwp_ref, dbg_ref, dg2_ref, loss_ref):
                ref[...] = jnp.zeros_like(ref)

        cat = jnp.concatenate([so_ref[...], ao_ref[...]], axis=1)
        g2 = g2_ref[...]
        mixed = _dot(cat, wo_ref[...])
        r = lax.rsqrt(jnp.mean(mixed * mixed, axis=-1, keepdims=True) + EPS)
        mr = mixed * r
        h1 = x_ref[...] + mr * g2
        h1b = h1.astype(BF16)
        gate = jax.nn.sigmoid(_dot(h1b, wg_ref[...]) + bg_ref[...])
        pb = p_ref[...].astype(BF16)
        wp_blocks = [slice(j * D_PLE, (j + 1) * D_PLE) for j in range(N_CHIPS)]
        pp = jnp.concatenate([_dot(pb, wp_ref[blk, :]) for blk in wp_blocks], axis=1)
        err = h1 + gate * pp - t_ref[...]
        loss_ref[...] += 0.5 * jnp.sum(jnp.mean(err * err, axis=-1, keepdims=True), axis=0, keepdims=True)

        dh2 = err * (1.0 / D_MODEL)
        d_glin = dh2 * pp * gate * (1.0 - gate)
        d_glin_b = d_glin.astype(BF16)
        dwg_ref[...] += _dot_tn(h1b, d_glin_b)
        dbg_ref[...] += jnp.sum(d_glin, axis=0, keepdims=True)
        d_pp = (dh2 * gate).astype(BF16)
        for blk in wp_blocks:
            dwp_ref[blk, :] += _dot_tn(pb, d_pp[:, blk])
        dh1 = dh2 + _dot_nt(d_glin_b, wg_ref[...])
        dh1_ref[...] = dh1
        dg2_ref[...] += jnp.sum(dh1 * mr, axis=0, keepdims=True)
        a_ = dh1 * g2
        d_mixed = (r * a_ - mr * (r * jnp.mean(a_ * mr, axis=-1, keepdims=True))).astype(BF16)
        dwo_ref[...] += _dot_tn(cat, d_mixed)
        d_cat = _dot_nt(d_mixed, wo_ref[...])
        dso_ref[...] = d_cat[:, 0:512]
        dao_ref[...] = d_cat[:, 512:1024]

    return _call(
        body, name="tail_fwd_bwd", grid=(rows // tm,),
        in_specs=[_rows(tm, 512), _rows(tm, 512), _rows(tm, D_MODEL), _rows(tm, D_PLE), _rows(tm, D_MODEL),
                  _full((D_MODEL, D_MODEL)), _full((1, D_MODEL)), _full((D_MODEL, D_MODEL)), _full((1, D_MODEL)),
                  _full((N_CHIPS * D_PLE, D_PLE))],
        out_specs=[_rows(tm, D_MODEL), _rows(tm, 512), _rows(tm, 512), _full((D_MODEL, D_MODEL)),
                   _full((D_MODEL, D_MODEL)), _full((N_CHIPS * D_PLE, D_PLE)), _full((1, D_MODEL)), _full((1, D_MODEL)),
                   _full((1, 1))],
        out_shape=[_sds((rows, D_MODEL)), _sds((rows, 512)), _sds((rows, 512)), _sds((D_MODEL, D_MODEL)),
                   _sds((D_MODEL, D_MODEL)), _sds((N_CHIPS * D_PLE, D_PLE)), _sds((1, D_MODEL)), _sds((1, D_MODEL)),
                   _sds((1, 1))],
        compiler_params=_params(52, ("arbitrary",)),
    )(ssm_out, attn_out, x2d, p2d, target, w_out, g2, w_gate, b_gate, w_proj)


def _local_step(x, p, target, pre_norm_g, w_in_t, s5_params, ssm_d, w_glu, b_glu, sinks, w_out, post_norm_g, w_proj,
                w_gate, b_gate):
    n_seq, seq, _ = x.shape
    rows = n_seq * seq
    x2d = x.reshape(rows, D_MODEL)
    p2d = p.reshape(rows, D_PLE)
    t2d = target.reshape(rows, D_MODEL)

    l_re, l_im, bt_re, bt_im, cm_re, cm_im = _s5_params_fwd(*s5_params)

    u_scan, zs, q, k, v, za = _in_proj(x2d, pre_norm_g, w_in_t, n_seq, seq)
    y_scan, h_re, h_im = _s5_scan_fwd(u_scan, bt_re, bt_im, cm_re, cm_im, l_re, l_im, ssm_d, n_seq, seq)
    ssm_out = _glu_fwd(y_scan, zs, w_glu, b_glu, n_seq, seq)
    o, attn_out, lse = _attn_fwd(q, k, v, za, sinks, n_seq, seq)

    dh1, d_so, d_ao, d_w_out, d_w_gate, d_w_proj, d_b_gate, d_g2, loss = _tail(
        ssm_out, attn_out, x2d, p2d, t2d, w_out, post_norm_g, w_gate, b_gate, w_proj)

    dq, dk, dv, dza, d_sinks = _attn_bwd(q, k, v, za, o, lse, d_ao, sinks, n_seq, seq)
    dy_scan, dzs, d_w_glu, d_b_glu = _glu_bwd(y_scan, zs, d_so, w_glu, b_glu, n_seq, seq)
    du_scan, d_bt_re, d_bt_im, d_cm_re, d_cm_im, d_l_re, d_l_im, d_d = _s5_scan_bwd(
        dy_scan, u_scan, h_re, h_im, bt_re, bt_im, cm_re, cm_im, l_re, l_im, ssm_d, n_seq, seq)
    d_lam_re, d_lam_im, d_log_step, d_b_re, d_b_im, d_c_re, d_c_im = _s5_params_bwd(
        s5_params, (d_l_re, d_l_im, d_bt_re, d_bt_im, d_cm_re, d_cm_im))

    grad_x, d_w_in_t, d_g1 = _in_proj_bwd(x2d, dh1, pre_norm_g, w_in_t, du_scan, dzs, dq, dk, dv, dza, n_seq, seq)
    grads = dict(
        pre_norm_g=d_g1, w_in=d_w_in_t, ssm_lam_re=d_lam_re, ssm_lam_im=d_lam_im, ssm_log_step=d_log_step,
        ssm_b_re=d_b_re, ssm_b_im=d_b_im, ssm_c_re=d_c_re, ssm_c_im=d_c_im, ssm_d=d_d, ssm_w_glu=d_w_glu,
        ssm_b_glu=d_b_glu, attn_sinks=d_sinks, w_out=d_w_out, post_norm_g=d_g2, pl_w_proj=d_w_proj,
        pl_w_gate=d_w_gate, pl_b_gate=d_b_gate)
    return grad_x.reshape(x.shape), loss, grads


_BIG = ("w_in", "ssm_w_glu", "w_out", "pl_w_proj", "pl_w_gate")
_BIG_SHARD = {"w_in": (D_IN // N_CHIPS, D_MODEL), "ssm_w_glu": (D_SSM // N_CHIPS, D_SSM),
              "w_out": (D_MODEL // N_CHIPS, D_MODEL), "pl_w_proj": (D_PLE, D_MODEL // N_CHIPS),
              "pl_w_gate": (D_MODEL // N_CHIPS, D_MODEL)}
_SMALL = {"pre_norm_g": (1, D_MODEL), "ssm_lam_re": (SSM_GROUPS, SSM_STATE), "ssm_lam_im": (SSM_GROUPS, SSM_STATE),
          "ssm_log_step": (1, SSM_GROUPS), "ssm_b_re": (D_SSM, SSM_STATE), "ssm_b_im": (D_SSM, SSM_STATE),
          "ssm_c_re": (D_SSM, SSM_STATE), "ssm_c_im": (D_SSM, SSM_STATE), "ssm_d": (1, D_SSM), "ssm_b_glu": (1, D_SSM),
          "attn_sinks": (1, N_HEADS), "post_norm_g": (1, D_MODEL), "pl_b_gate": (1, D_MODEL)}
_VEC_ROWS = ("pre_norm_g", "post_norm_g", "pl_b_gate", "ssm_d", "ssm_b_glu", "attn_sinks", "ssm_log_step", "loss")
_SMALL_GROUPS = (
    ("vec", (8, D_MODEL), tuple((name, r) for r, name in enumerate(_VEC_ROWS))),
    ("lam", (2 * SSM_GROUPS, SSM_STATE), (("ssm_lam_re", 0), ("ssm_lam_im", SSM_GROUPS))),
)
_SMALL_EARLY = ("ssm_b_re", "ssm_b_im", "ssm_c_re", "ssm_c_im")
_SMALL_ORDER = tuple(name for _, _, members in _SMALL_GROUPS for name, _ in members) + _SMALL_EARLY
_WEIGHT_ORDER = ("pre_norm_g", "w_in", "ssm_lam_re", "ssm_lam_im", "ssm_log_step", "ssm_b_re", "ssm_b_im", "ssm_c_re",
                 "ssm_c_im", "ssm_d", "ssm_w_glu", "ssm_b_glu", "attn_sinks", "w_out", "post_norm_g", "pl_w_proj",
                 "pl_w_gate", "pl_b_gate")


def _small_shape(name):
    return (1, 1) if name == "loss" else _SMALL[name]


def _to_kernel_form(name, a):
    a = a[0]
    if name == "w_in":
        return a.T
    if name in ("ssm_b_re", "ssm_b_im"):
        a = a.transpose(0, 2, 1)
    return a.reshape(_SMALL[name]) if name in _SMALL else a


def _from_kernel_form(name, a, shape):
    if name == "w_in":
        a = a.T
    if name in ("ssm_b_re", "ssm_b_im"):
        a = a.reshape(SSM_GROUPS, SSM_GROUP_CH, SSM_STATE).transpose(0, 2, 1)
    return a.reshape(shape)


def _mesh_place():
    x, y, c = lax.axis_index("x"), lax.axis_index("y"), lax.axis_index("c")
    other_chips = ((1 - x, y), (x, 1 - y), (1 - x, 1 - y))
    return x, y, c, other_chips


def _gather_copies(s_refs, g_refs, send_sems, recv_sems, local_sems):
    x, y, c, other_chips = _mesh_place()
    started = []
    for i, (s_ref, g_ref) in enumerate(zip(s_refs, g_refs)):
        rows = s_ref.shape[0]
        half = rows // 2

        def block(chip, g_ref=g_ref, rows=rows, half=half):
            return g_ref.at[pl.ds((2 * chip[0] + chip[1]) * rows + c * half, half), :]

        def copy(k, chip, to, src=None, i=i, block=block):
            return pltpu.make_async_remote_copy(
                src_ref=block(chip) if src is None else src, dst_ref=block(chip), send_sem=send_sems.at[6 * i + k],
                recv_sem=recv_sems.at[6 * i + k], device_id=to, device_id_type=MESH)

        own = pltpu.make_async_copy(s_ref, g_ref.at[pl.ds((2 * x + y) * rows, rows), :], local_sems.at[i])
        own.start()
        first = [copy(k, (x, y), (*chip, c), src=s_ref.at[pl.ds(c * half, half), :])
                 for k, chip in enumerate(other_chips)]
        for cp in first:
            cp.start()
        passed = [copy(3 + k, chip, (x, y, 1 - c)) for k, chip in enumerate(other_chips)]
        started.append((own, first, passed))
    for own, first, passed in started:
        for k in range(3):
            first[k].wait_recv()
            passed[k].start()
    for own, first, passed in started:
        for k in range(3):
            passed[k].wait_recv()
        for cp in first + passed:
            cp.wait_send()
        own.wait()


def _gather_semaphores(n_t):
    return [pltpu.SemaphoreType.DMA((6 * n_t,)), pltpu.SemaphoreType.DMA((6 * n_t,)), pltpu.SemaphoreType.DMA((n_t,))]


def _gather_weights(shards):
    n_t = len(shards)

    def body(*refs):
        _gather_copies(refs[:n_t], refs[n_t:2 * n_t], *refs[2 * n_t + 1:])
        refs[2 * n_t][...] = jnp.zeros_like(refs[2 * n_t])

    any_spec = pl.BlockSpec(memory_space=pl.ANY)
    *full, done = _call(
        body, name="gather_weights", in_specs=[any_spec] * n_t,
        out_specs=[any_spec] * n_t + [pl.BlockSpec(memory_space=pltpu.VMEM)],
        out_shape=[_sds((N_CHIPS * s.shape[0], s.shape[1]), s.dtype) for s in shards]
        + [jax.ShapeDtypeStruct((8, LANES), F32)],
        scratch_shapes=_gather_semaphores(n_t),
    )(*shards)
    return full, done[0, 0]


def _gather_weights_beside(shards):
    n_t = len(shards)
    hbm = pltpu.MemorySpace.HBM
    s_refs = [jax.new_ref(s, memory_space=hbm) for s in shards]
    g_refs = [jax.empty_ref(jax.ShapeDtypeStruct((N_CHIPS * s.shape[0], s.shape[1]), s.dtype), memory_space=hbm)
              for s in shards]

    def launch(send_sems, recv_sems, local_sems):
        x, y, c, other_chips = _mesh_place()
        peers = [(*chip, c) for chip in other_chips] + [(x, y, 1 - c)]
        barrier = pltpu.get_barrier_semaphore()
        for peer in peers:
            pl.semaphore_signal(barrier, inc=1, device_id=peer, device_id_type=MESH)
        pl.semaphore_wait(barrier, len(peers))
        _gather_copies(s_refs, g_refs, send_sems, recv_sems, local_sems)

    pl.kernel(launch, mesh=plsc.ScalarSubcoreMesh(axis_name="sequencer", num_cores=1), name="gather_weights_beside",
              scratch_types=_gather_semaphores(n_t), compiler_params=pltpu.CompilerParams(collective_id=1))()
    return [g[...] for g in g_refs]


_RELATIONS = tuple(((r >> 2) & 1, (r >> 1) & 1, r & 1) for r in range(1, 8))


def _related(place, relation):
    return tuple(1 - a if flip else a for a, flip in zip(place, relation))


def _scatter_beside(mats):
    hbm = pltpu.MemorySpace.HBM
    src_refs = [jax.new_ref(a, memory_space=hbm) for a in mats]
    land_refs = [jax.empty_ref(jax.ShapeDtypeStruct((7, a.shape[0] // 8, a.shape[1]), a.dtype), memory_space=hbm)
                 for a in mats]

    def launch(send_sems, recv_sems):
        me = (lax.axis_index("x"), lax.axis_index("y"), lax.axis_index("c"))
        peers = [_related(me, rel) for rel in _RELATIONS]
        barrier = pltpu.get_barrier_semaphore()
        for peer in peers:
            pl.semaphore_signal(barrier, inc=1, device_id=peer, device_id_type=MESH)
        pl.semaphore_wait(barrier, len(peers))
        copies = []
        for i, (src, land) in enumerate(zip(src_refs, land_refs)):
            hr = land.shape[1]
            for k, (tx, ty, tc) in enumerate(peers):
                rows = pl.ds((2 * tx + ty) * 2 * hr + tc * hr, hr)
                copies.append(pltpu.make_async_remote_copy(
                    src_ref=src.at[rows, :], dst_ref=land.at[k], send_sem=send_sems.at[7 * i + k],
                    recv_sem=recv_sems.at[7 * i + k], device_id=(tx, ty, tc), device_id_type=MESH))
                copies[-1].start()
        for cp in copies:
            cp.wait()

    n_sems = 7 * len(mats)
    pl.kernel(launch, mesh=plsc.ScalarSubcoreMesh(axis_name="sequencer", num_cores=1), name="scatter_beside",
              scratch_types=[pltpu.SemaphoreType.DMA((n_sems,)), pltpu.SemaphoreType.DMA((n_sems,))],
              compiler_params=pltpu.CompilerParams(collective_id=2))()
    return [ref[...] for ref in land_refs]


def _broadcast_beside(arrays):
    hbm = pltpu.MemorySpace.HBM
    src_refs = [jax.new_ref(a, memory_space=hbm) for a in arrays]
    land_refs = [jax.empty_ref(jax.ShapeDtypeStruct((len(_RELATIONS),) + a.shape, a.dtype), memory_space=hbm)
                 for a in arrays]

    def launch(send_sems, recv_sems):
        me = (lax.axis_index("x"), lax.axis_index("y"), lax.axis_index("c"))
        peers = [_related(me, rel) for rel in _RELATIONS]
        barrier = pltpu.get_barrier_semaphore()
        for peer in peers:
            pl.semaphore_signal(barrier, inc=1, device_id=peer, device_id_type=MESH)
        pl.semaphore_wait(barrier, len(peers))
        copies = []
        for i, (src, land) in enumerate(zip(src_refs, land_refs)):
            for k, peer in enumerate(peers):
                copies.append(pltpu.make_async_remote_copy(
                    src_ref=src, dst_ref=land.at[k], send_sem=send_sems.at[7 * i + k],
                    recv_sem=recv_sems.at[7 * i + k], device_id=peer, device_id_type=MESH))
                copies[-1].start()
        for cp in copies:
            cp.wait()

    n_sems = 7 * len(arrays)
    pl.kernel(launch, mesh=plsc.ScalarSubcoreMesh(axis_name="sequencer", num_cores=1), name="broadcast_beside",
              scratch_types=[pltpu.SemaphoreType.DMA((n_sems,)), pltpu.SemaphoreType.DMA((n_sems,))],
              compiler_params=pltpu.CompilerParams(collective_id=3))()
    return [ref[...] for ref in land_refs]


def _exchange_grads(big, small, landed, landed_small):
    n_t = len(big)
    n_g = len(_SMALL_GROUPS)
    names = _SMALL_ORDER
    halves = [(b.shape[0] // N_CHIPS // 2, b.shape[1]) for b in big]
    early = sorted(landed)
    late = [i for i in range(n_t) if i not in landed]
    n_sems = 4 * n_g + 7 * len(late) + n_t
    small_sem0, block_sem0 = n_t, n_t + len(names)
    early_sem0 = block_sem0 + N_CHIPS * len(late)
    landed_sem0 = early_sem0 + 2 * len(early)
    early_small = [n for n in names if n in landed_small]

    def body(*refs):
        pos = 0

        def take(n):
            nonlocal pos
            pos += n
            return refs[pos - n:pos]

        big_refs, small_refs = take(n_t), dict(zip(names, take(len(names))))
        land_refs = dict(zip(early, take(len(early))))
        land_small_refs = dict(zip(early_small, take(len(early_small))))
        out_refs, small_out_refs = take(n_t), dict(zip(names, take(len(names))))
        per_late = lambda: dict(zip(late, take(len(late))))
        ga, gb, pme, send_b, recv_b = per_late(), per_late(), take(n_t), per_late(), per_late()
        own_e, land_e = dict(zip(early, take(len(early)))), dict(zip(early, take(len(early))))
        land_s = dict(zip(early_small, take(len(early_small))))
        s_own, s_sib, s_chips, s_pair = take(n_g), take(n_g), take(n_g), take(n_g)
        stage = dict(zip(names, take(len(names))))
        send_sems, recv_sems, local_sems = take(3)
        x, y, c, other_chips = _mesh_place()
        me = 2 * x + y
        sibling = (x, y, 1 - c)
        sem_at = iter(range(n_sems))

        def remote(src, dst, to):
            k = next(sem_at)
            return pltpu.make_async_remote_copy(src_ref=src, dst_ref=dst, send_sem=send_sems.at[k],
                                                recv_sem=recv_sems.at[k], device_id=to, device_id_type=MESH)

        loads = [pltpu.make_async_copy(small_refs[name], stage[name], local_sems.at[small_sem0 + a])
                 for a, name in enumerate(names)]
        landed_loads = [pltpu.make_async_copy(land_small_refs[name], land_s[name], local_sems.at[landed_sem0 + a])
                        for a, name in enumerate(early_small)]
        for cp in loads + landed_loads:
            cp.start()
        for cp in loads:
            cp.wait()
        small_swaps = []
        for gi, (_, _, members) in enumerate(_SMALL_GROUPS):
            s_own[gi][...] = jnp.zeros_like(s_own[gi])
            for name, r0 in members:
                r, n = _small_shape(name)
                s_own[gi][r0:r0 + r, 0:n] = stage[name][...]
            small_swaps.append(remote(s_own[gi], s_sib[gi], sibling))
            small_swaps[gi].start()
        order = sorted(late, key=lambda i: halves[i][0] * halves[i][1])
        own_loads, big_swaps = {}, {}
        for i in order:
            hr = halves[i][0]
            own_loads[i], big_swaps[i] = [], []
            for j in range(N_CHIPS):
                mine = big_refs[i].at[pl.ds(j * 2 * hr + c * hr, hr), :]
                theirs = big_refs[i].at[pl.ds(j * 2 * hr + (1 - c) * hr, hr), :]
                sem = local_sems.at[block_sem0 + N_CHIPS * late.index(i) + j]
                own_loads[i].append(pltpu.make_async_copy(mine, ga[i].at[j], sem))
                own_loads[i][j].start()
                big_swaps[i].append(remote(theirs, gb[i].at[j], sibling))
                big_swaps[i][j].start()
        early_loads = {}
        for e, i in enumerate(early):
            hr = halves[i][0]
            mine = big_refs[i].at[pl.ds(me * 2 * hr + c * hr, hr), :]
            early_loads[i] = [pltpu.make_async_copy(mine, own_e[i], local_sems.at[early_sem0 + 2 * e]),
                              pltpu.make_async_copy(land_refs[i], land_e[i], local_sems.at[early_sem0 + 2 * e + 1])]
            for cp in early_loads[i]:
                cp.start()
        small_sends = []
        for gi in range(n_g):
            small_swaps[gi].wait_recv()
            s_pair[gi][...] = s_own[gi][...] + s_sib[gi][...]
            small_sends.append([remote(s_pair[gi], s_chips[gi].at[k], (*chip, c)) for k, chip in enumerate(other_chips)])
            for cp in small_sends[gi]:
                cp.start()

        def pair_sum(i, j):
            return ga[i][j] + gb[i][j]

        big_sends = {}
        for i in order:
            for j in range(N_CHIPS):
                own_loads[i][j].wait()
                big_swaps[i][j].wait_recv()
            big_sends[i] = []
            for k, chip in enumerate(other_chips):
                send_b[i][k] = pair_sum(i, 2 * chip[0] + chip[1]).astype(BF16)
                big_sends[i].append(remote(send_b[i].at[k], recv_b[i].at[k], (*chip, c)))
                big_sends[i][k].start()
        last_swaps, keeps = {}, {}
        for i in early + order:
            hr = halves[i][0]
            if i in landed:
                for cp in early_loads[i]:
                    cp.wait()
                total = own_e[i][...]
                for k in range(len(_RELATIONS)):
                    total = total + land_e[i][k]
                pme[i][...] = total
            else:
                for k in range(3):
                    big_sends[i][k].wait_recv()
                pme[i][...] = ((pair_sum(i, me) + recv_b[i][0].astype(F32)) + recv_b[i][1].astype(F32)) + recv_b[i][2].astype(F32)
            mine = out_refs[i].at[pl.ds(c * hr, hr), :]
            keeps[i] = pltpu.make_async_copy(pme[i], mine, local_sems.at[i])
            keeps[i].start()
            last_swaps[i] = remote(pme[i], mine, sibling)
            last_swaps[i].start()

        for gi, (_, _, members) in enumerate(_SMALL_GROUPS):
            for k in range(3):
                small_sends[gi][k].wait_recv()
            total = None
            for j in range(N_CHIPS):
                rel = jnp.bitwise_xor(j, me)
                term = jnp.where(rel == 0, s_pair[gi][...], jnp.where(
                    rel == 2, s_chips[gi][0], jnp.where(rel == 1, s_chips[gi][1], s_chips[gi][2])))
                total = term if total is None else total + term
            s_sib[gi][...] = total
            for name, r0 in members:
                r, n = _small_shape(name)
                stage[name][...] = s_sib[gi][r0:r0 + r, 0:n]
        my_index = 4 * x + 2 * y + c
        for cp in landed_loads:
            cp.wait()
        for name in early_small:
            total = None
            for d in range(2 * N_CHIPS):
                rel = jnp.bitwise_xor(d, my_index)
                term = stage[name][...]
                for k in range(len(_RELATIONS)):
                    term = jnp.where(rel == k + 1, land_s[name][k], term)
                total = term if total is None else total + term
            stage[name][...] = total
        stores = [pltpu.make_async_copy(stage[name], small_out_refs[name], local_sems.at[small_sem0 + a])
                  for a, name in enumerate(names)]
        for cp in stores:
            cp.start()

        for i in range(n_t):
            last_swaps[i].wait_recv()
            keeps[i].wait()
        for cp in stores:
            cp.wait()
        groups = list(big_swaps.values()) + small_sends + list(big_sends.values())
        for cp in small_swaps + [cp for group in groups for cp in group] + list(last_swaps.values()):
            cp.wait_send()

    any_spec = pl.BlockSpec(memory_space=pl.ANY)
    small_shapes = [_sds(_small_shape(n)) for n in names]
    group_shapes = [shape for _, shape, _ in _SMALL_GROUPS]
    vmem = lambda which, dtype, lead=(): [pltpu.VMEM(lead + halves[i], dtype) for i in which]
    outs = _call(
        body, name="exchange_grads",
        in_specs=[any_spec] * (n_t + len(names) + len(early) + len(early_small)),
        out_specs=[any_spec] * (n_t + len(names)),
        out_shape=[_sds((b.shape[0] // N_CHIPS, b.shape[1])) for b in big] + small_shapes,
        scratch_shapes=(vmem(late, F32, (N_CHIPS,)) + vmem(late, F32, (N_CHIPS,)) + vmem(range(n_t), F32)
                        + vmem(late, BF16, (3,)) + vmem(late, BF16, (3,))
                        + vmem(early, F32) + vmem(early, F32, (len(_RELATIONS),))
                        + [pltpu.VMEM((len(_RELATIONS),) + _small_shape(n), F32) for n in early_small]
                        + [pltpu.VMEM(s, F32) for s in group_shapes] * 2 + [pltpu.VMEM((3,) + s, F32) for s in group_shapes]
                        + [pltpu.VMEM(s, F32) for s in group_shapes]
                        + [pltpu.VMEM(_small_shape(n), F32) for n in names]
                        + [pltpu.SemaphoreType.DMA((n_sems,)), pltpu.SemaphoreType.DMA((n_sems,)),
                           pltpu.SemaphoreType.DMA((landed_sem0 + len(early_small),))]),
        compiler_params=_params(48),
    )(*big, *[small[n] for n in names], *[landed[i] for i in early], *[landed_small[n] for n in early_small])
    return list(outs[:n_t]), dict(zip(names, outs[n_t:n_t + len(names)]))


def _adamw_update(w, g, m, v):
    m = ADAM_B1 * m + (1.0 - ADAM_B1) * g
    v = ADAM_B2 * v + (1.0 - ADAM_B2) * (g * g)
    m_hat = m / (1.0 - ADAM_B1 ** ADAM_STEP)
    v_hat = v / (1.0 - ADAM_B2 ** ADAM_STEP)
    return -ADAM_LR * (m_hat / (jnp.sqrt(v_hat) + ADAM_EPS) + ADAM_WD * w), m, v


def _adamw(w, g, m, v, grid, name):
    n_t = len(w)

    def body(*refs):
        ins, outs = refs[:4 * n_t], refs[4 * n_t:]
        for i in range(n_t):
            w_, g_, m_, v_ = [ins[a * n_t + i][...] for a in range(4)]
            vals = (g_,) + _adamw_update(w_, g_, m_, v_)
            for a in range(4):
                outs[a * n_t + i][...] = vals[a]

    specs = [pl.BlockSpec((a.shape[0] // grid, a.shape[1]), lambda i: (i, 0)) for a in w]
    shapes = [_sds(a.shape) for a in w]
    outs = _call(
        body, name=name, grid=(grid,), in_specs=specs * 4, out_specs=specs * 4, out_shape=shapes * 4,
        compiler_params=_params(40, ("arbitrary",)),
    )(*w, *g, *m, *v)
    return [outs[a * n_t:(a + 1) * n_t] for a in range(4)]


def kernel(x, p, pre_norm_g, w_in, ssm_lam_re, ssm_lam_im, ssm_log_step, ssm_b_re, ssm_b_im, ssm_c_re, ssm_c_im, ssm_d, ssm_w_glu, ssm_b_glu, attn_sinks, w_out, post_norm_g, pl_w_proj, pl_w_gate, pl_b_gate, loss_target, m_pre_norm_g, m_w_in, m_ssm_lam_re, m_ssm_lam_im, m_ssm_log_step, m_ssm_b_re, m_ssm_b_im, m_ssm_c_re, m_ssm_c_im, m_ssm_d, m_ssm_w_glu, m_ssm_b_glu, m_attn_sinks, m_w_out, m_post_norm_g, m_pl_w_proj, m_pl_w_gate, m_pl_b_gate, v_pre_norm_g, v_w_in, v_ssm_lam_re, v_ssm_lam_im, v_ssm_log_step, v_ssm_b_re, v_ssm_b_im, v_ssm_c_re, v_ssm_c_im, v_ssm_d, v_ssm_w_glu, v_ssm_b_glu, v_attn_sinks, v_w_out, v_post_norm_g, v_pl_w_proj, v_pl_w_gate, v_pl_b_gate):
    weights = dict(pre_norm_g=pre_norm_g, w_in=w_in, ssm_lam_re=ssm_lam_re, ssm_lam_im=ssm_lam_im,
                   ssm_log_step=ssm_log_step, ssm_b_re=ssm_b_re, ssm_b_im=ssm_b_im, ssm_c_re=ssm_c_re,
                   ssm_c_im=ssm_c_im, ssm_d=ssm_d, ssm_w_glu=ssm_w_glu, ssm_b_glu=ssm_b_glu, attn_sinks=attn_sinks,
                   w_out=w_out, post_norm_g=post_norm_g, pl_w_proj=pl_w_proj, pl_w_gate=pl_w_gate, pl_b_gate=pl_b_gate)
    m_in = dict(pre_norm_g=m_pre_norm_g, w_in=m_w_in, ssm_lam_re=m_ssm_lam_re, ssm_lam_im=m_ssm_lam_im,
                ssm_log_step=m_ssm_log_step, ssm_b_re=m_ssm_b_re, ssm_b_im=m_ssm_b_im, ssm_c_re=m_ssm_c_re,
                ssm_c_im=m_ssm_c_im, ssm_d=m_ssm_d, ssm_w_glu=m_ssm_w_glu, ssm_b_glu=m_ssm_b_glu,
                attn_sinks=m_attn_sinks, w_out=m_w_out, post_norm_g=m_post_norm_g, pl_w_proj=m_pl_w_proj,
                pl_w_gate=m_pl_w_gate, pl_b_gate=m_pl_b_gate)
    v_in = dict(pre_norm_g=v_pre_norm_g, w_in=v_w_in, ssm_lam_re=v_ssm_lam_re, ssm_lam_im=v_ssm_lam_im,
                ssm_log_step=v_ssm_log_step, ssm_b_re=v_ssm_b_re, ssm_b_im=v_ssm_b_im, ssm_c_re=v_ssm_c_re,
                ssm_c_im=v_ssm_c_im, ssm_d=v_ssm_d, ssm_w_glu=v_ssm_w_glu, ssm_b_glu=v_ssm_b_glu,
                attn_sinks=v_attn_sinks, w_out=v_w_out, post_norm_g=v_post_norm_g, pl_w_proj=v_pl_w_proj,
                pl_w_gate=v_pl_w_gate, pl_b_gate=v_pl_b_gate)

    def two_d(tree):
        return {k: _to_kernel_form(k, a) for k, a in tree.items()}

    w2, m2, v2 = two_d(weights), two_d(m_in), two_d(v_in)

    (w_in_full,), gathered = _gather_weights([w2["w_in"].astype(BF16)])
    rest = _gather_weights_beside([(w2[n] + gathered).astype(BF16) for n in _BIG[1:]])
    full = dict(zip(_BIG, [w_in_full] + rest))
    s5_params = tuple(w2[n] for n in ("ssm_lam_re", "ssm_lam_im", "ssm_log_step", "ssm_b_re", "ssm_b_im", "ssm_c_re",
                                      "ssm_c_im"))
    grad_x, loss, grads = _local_step(
        x, p, loss_target, w2["pre_norm_g"], full["w_in"], s5_params, w2["ssm_d"], full["ssm_w_glu"], w2["ssm_b_glu"],
        w2["attn_sinks"], full["w_out"], w2["post_norm_g"], full["pl_w_proj"], full["pl_w_gate"], w2["pl_b_gate"])

    sent_early = ("w_out", "pl_w_gate", "pl_w_proj")
    landed = dict(zip([_BIG.index(n) for n in sent_early], _scatter_beside([grads[n] for n in sent_early])))
    after_scatter = landed[_BIG.index(sent_early[-1])][0, 0, 0] * 0.0
    landed_small = dict(zip(_SMALL_EARLY, _broadcast_beside([grads[n] + after_scatter for n in _SMALL_EARLY])))
    g_big, g_small = _exchange_grads([grads[n] for n in _BIG], {**{n: grads[n] for n in _SMALL}, "loss": loss}, landed,
                                     landed_small)
    g_big = dict(zip(_BIG, g_big))
    total_loss = g_small.pop("loss")

    big_out = _adamw([w2[n] for n in _BIG], [g_big[n] for n in _BIG], [m2[n] for n in _BIG], [v2[n] for n in _BIG],
                     8, "adamw_matrices")
    small_names = tuple(_SMALL)
    small_out = _adamw([w2[n] for n in small_names], [g_small[n] for n in small_names], [m2[n] for n in small_names],
                       [v2[n] for n in small_names], 1, "adamw_small")

    results = [{**dict(zip(_BIG, big_part)), **dict(zip(small_names, small_part))}
               for big_part, small_part in zip(big_out, small_out)]
    flat = [_from_kernel_form(name, r[name], weights[name].shape) for r in results for name in _WEIGHT_ORDER]
    return (total_loss.reshape(()), grad_x, *flat)
```

```python
import math

import jax
import jax.numpy as jnp
from jax import lax
from jax.experimental import pallas as pl
from jax.experimental.pallas import tpu as pltpu
from jax.experimental.pallas import tpu_sc as plsc

F32 = jnp.float32
BF16 = jnp.bfloat16

D_MODEL = 1024
D_SSM = 512
D_ATTN = 512
SSM_GROUPS = 32
SSM_GROUP_CH = 16
SSM_STATE = 64
SSM_LANES = SSM_GROUPS * SSM_STATE
HEAD_DIM = 64
N_HEADS = 8
KV_HEADS = 2
Q_PER_KV = 4
WINDOW = 128
BLOCK = 128
D_PLE = 256
D_IN = 2304
EPS = 1e-6
ATTN_SCALE = 1.0 / math.sqrt(HEAD_DIM)

ADAM_LR = 0.001
ADAM_B1 = 0.9
ADAM_B2 = 0.999
ADAM_EPS = 1e-08
ADAM_WD = 0.01
ADAM_STEP = 10

N_CHIPS = 4
LANES = 128
SCAN_CHUNKS = 8
SCAN_TILE_STEPS = 32
SCAN_LANE_CHUNK = 512
MIB = 2 ** 20
MESH = pl.DeviceIdType.MESH


def _dot(a, b):
    return jnp.dot(a, b, preferred_element_type=F32)


def _dot_nt(a, b):
    return lax.dot_general(a, b, (((1,), (1,)), ((), ())), preferred_element_type=F32)


def _dot_tn(a, b):
    return lax.dot_general(a, b, (((0,), (0,)), ((), ())), preferred_element_type=F32)


def _params(vmem_mib, semantics=None):
    kw = dict(vmem_limit_bytes=vmem_mib * MIB)
    if semantics is not None:
        kw["dimension_semantics"] = semantics
    return pltpu.CompilerParams(**kw)


def _full(shape):
    nd = len(shape)
    return pl.BlockSpec(shape, lambda *_: (0,) * nd, pipeline_mode=pl.Buffered(1))


def _rows(tm, width):
    return pl.BlockSpec((tm, width), lambda i: (i, 0))


def _sds(shape, dtype=F32):
    return pltpu.HBM(shape, dtype)


def _call(body, **kw):
    fn = pl.pallas_call(body, **kw)
    return lambda *args: fn(*[pltpu.with_memory_space_constraint(a, pltpu.HBM) for a in args])


def _silu(z):
    return z * jax.nn.sigmoid(z)


def _in_proj(x2d, g1, w_in_t, n_seq, seq):
    rows = x2d.shape[0]
    tm = 512
    slab, steps, _, _ = _scan_geometry(n_seq, seq)

    def body(x_ref, g_ref, w_ref, *out_refs):
        u_parts, (zs_ref, q_ref, k_ref, v_ref, za_ref) = out_refs[:_SCAN_PARTS], out_refs[_SCAN_PARTS:]
        x = x_ref[...]
        r = lax.rsqrt(jnp.mean(x * x, axis=-1, keepdims=True) + EPS)
        hn = (x * r * g_ref[...]).astype(BF16)

        whole = _dot_nt(hn, w_ref[...])

        def proj(a, b):
            return whole[:, a:b]

        _store_chunks(u_parts, pl.program_id(0) * (tm // steps), proj(0, 512), steps, slab)
        zs_ref[...] = proj(512, 1024)
        q_ref[...] = (proj(1024, 1536) * ATTN_SCALE).astype(BF16)
        k_ref[...] = proj(1536, 1664).astype(BF16)
        v_ref[...] = proj(1664, 1792).astype(BF16)
        za_ref[...] = proj(1792, 2304)

    *u_parts, zs, q, k, v, za = _call(
        body, name="in_proj", grid=(rows // tm,),
        in_specs=[_rows(tm, D_MODEL), _full((1, D_MODEL)), _full((D_IN, D_MODEL))],
        out_specs=_whole_parts(rows) + [_rows(tm, 512), _rows(tm, 512), _rows(tm, 128), _rows(tm, 128), _rows(tm, 512)],
        out_shape=_part_shapes(rows) + [_sds((rows, 512)), _sds((rows, 512), BF16), _sds((rows, 128), BF16),
                                        _sds((rows, 128), BF16), _sds((rows, 512))],
        compiler_params=_params(48, ("arbitrary",)),
    )(x2d, g1, w_in_t)
    return u_parts, zs, q, k, v, za


def _in_proj_bwd(x2d, dh1, g1, w_in_t, du_parts, dzs, dq, dk, dv, dza, n_seq, seq):
    rows = x2d.shape[0]
    tm = 512
    slab, steps, _, _ = _scan_geometry(n_seq, seq)

    def body(x_ref, dh1_ref, g_ref, w_ref, *refs):
        du_parts, (dzs_ref, dq_ref, dk_ref, dv_ref, dza_ref, gx_ref, dw_ref, dg_ref) = refs[:_SCAN_PARTS], refs[_SCAN_PARTS:]

        @pl.when(pl.program_id(0) == 0)
        def _():
            dw_ref[...] = jnp.zeros_like(dw_ref)
            dg_ref[...] = jnp.zeros_like(dg_ref)

        x = x_ref[...]
        g = g_ref[...]
        r = lax.rsqrt(jnp.mean(x * x, axis=-1, keepdims=True) + EPS)
        xr = x * r
        hn = (xr * g).astype(BF16)
        du = _load_chunks(du_parts, pl.program_id(0) * (tm // steps), tm // steps, steps, slab)
        d_proj = jnp.concatenate([du.astype(BF16), dzs_ref[...], dq_ref[...], dk_ref[...], dv_ref[...], dza_ref[...]],
                                 axis=1)
        dhn = _dot(d_proj, w_ref[...])
        dw_ref[...] += _dot_tn(d_proj, hn)
        dg_ref[...] += jnp.sum(dhn * xr, axis=0, keepdims=True)
        a_ = dhn * g
        gx_ref[...] = dh1_ref[...] + r * a_ - xr * (r * jnp.mean(a_ * xr, axis=-1, keepdims=True))

    return _call(
        body, name="in_proj_bwd", grid=(rows // tm,),
        in_specs=[_rows(tm, D_MODEL), _rows(tm, D_MODEL), _full((1, D_MODEL)), _full((D_IN, D_MODEL))]
        + _whole_parts(rows) + [_rows(tm, 512), _rows(tm, 512), _rows(tm, 128), _rows(tm, 128), _rows(tm, 512)],
        out_specs=[_rows(tm, D_MODEL), _full((D_IN, D_MODEL)), _full((1, D_MODEL))],
        out_shape=[_sds((rows, D_MODEL)), _sds((D_IN, D_MODEL)), _sds((1, D_MODEL))],
        compiler_params=_params(56, ("arbitrary",)),
    )(x2d, dh1, g1, w_in_t, *du_parts, dzs, dq, dk, dv, dza)


def _iota(shape, axis):
    return lax.broadcasted_iota(jnp.int32, shape, axis)


def _sum_of_thirds(f, a):
    hi = a.astype(BF16)
    rest = a - hi.astype(F32)
    mid = rest.astype(BF16)
    low = (rest - mid.astype(F32)).astype(BF16)
    return (f(hi) + f(mid)) + f(low)


@jax.custom_vjp
def _pick_rows(e, a):
    return _sum_of_thirds(lambda part: _dot(e, part), a)


def _pick_rows_fwd(e, a):
    return _pick_rows(e, a), e


def _pick_rows_bwd(e, ct):
    return jnp.zeros_like(e), _sum_of_thirds(lambda part: _dot_tn(e, part), ct)


_pick_rows.defvjp(_pick_rows_fwd, _pick_rows_bwd)


@jax.custom_vjp
def _pick_cols(a, e):
    return _sum_of_thirds(lambda part: _dot(part, e), a)


def _pick_cols_fwd(a, e):
    return _pick_cols(a, e), e


def _pick_cols_bwd(e, ct):
    return _sum_of_thirds(lambda part: _dot_nt(part, e), ct), jnp.zeros_like(e)


_pick_cols.defvjp(_pick_cols_fwd, _pick_cols_bwd)


_HALF_GROUPS = SSM_GROUPS // 2
_N_SHIFT = SSM_STATE.bit_length() - 1
_P_SHIFT = SSM_GROUP_CH.bit_length() - 1


def _s5_operands(lam_re, lam_im, log_step, b_re, b_im, c_re, c_im):
    g, n, p = SSM_GROUPS, SSM_STATE, SSM_GROUP_CH
    gn, gp, hn_, hp = g * n, g * p, _HALF_GROUPS * n, _HALF_GROUPS * p
    eye_g = _iota((g, g), 0) == _iota((g, g), 1)
    step = jnp.sum(jnp.where(eye_g, jnp.exp(log_step), 0.0), axis=1, keepdims=True)
    a_re = lam_re * step
    a_im = lam_im * step
    mag = jnp.exp(a_re)
    lbar_re = mag * jnp.cos(a_im)
    lbar_im = mag * jnp.sin(a_im)
    n_re = lbar_re - 1.0
    den = lam_re * lam_re + lam_im * lam_im
    f_re = (n_re * lam_re + lbar_im * lam_im) / den
    f_im = (lbar_im * lam_re - n_re * lam_im) / den

    spread_n = (_iota((n, gn), 0) == (_iota((n, gn), 1) & (n - 1))).astype(BF16)
    own_g = _iota((g, gn), 0) == (_iota((g, gn), 1) >> _N_SHIFT)

    def to_row(a):
        return jnp.sum(jnp.where(own_g, _pick_cols(a, spread_n), 0.0), axis=0, keepdims=True)

    per_group = ((_iota((gp, g), 0) >> _P_SHIFT) == _iota((gp, g), 1)).astype(BF16)
    fx_re, fx_im = _pick_rows(per_group, f_re), _pick_rows(per_group, f_im)
    bbar_re = fx_re * b_re - fx_im * b_im
    bbar_im = fx_re * b_im + fx_im * b_re

    tile_n = (_iota((n, hn_), 0) == (_iota((n, hn_), 1) & (n - 1))).astype(BF16)
    same_group = (_iota((hp, hn_), 0) >> _P_SHIFT) == (_iota((hp, hn_), 1) >> _N_SHIFT)

    def embed(a, hf):
        return jnp.where(same_group, _pick_cols(a[hf * hp:(hf + 1) * hp], tile_n), 0.0)

    return (to_row(lbar_re), to_row(lbar_im), embed(bbar_re, 0), embed(bbar_re, 1), embed(bbar_im, 0),
            embed(bbar_im, 1), embed(c_re, 0), embed(c_re, 1), embed(c_im, 0), embed(c_im, 1))


_S5_PARAM_SHAPES = ((SSM_GROUPS, SSM_STATE), (SSM_GROUPS, SSM_STATE), (1, SSM_GROUPS),
                    (D_SSM, SSM_STATE), (D_SSM, SSM_STATE), (D_SSM, SSM_STATE), (D_SSM, SSM_STATE))
_CM_SHAPE = (2, _HALF_GROUPS * SSM_GROUP_CH, _HALF_GROUPS * SSM_STATE)
_S5_OPERAND_SHAPES = ((1, SSM_LANES), (1, SSM_LANES), _CM_SHAPE, _CM_SHAPE, _CM_SHAPE, _CM_SHAPE)


def _s5_params_fwd(*params):
    def body(*refs):
        ins, (lre_ref, lim_ref, btre_ref, btim_ref, cmre_ref, cmim_ref) = refs[:7], refs[7:]
        vals = _s5_operands(*[r[...] for r in ins])
        lre_ref[...] = vals[0]
        lim_ref[...] = vals[1]
        for ref, pair in zip((btre_ref, btim_ref, cmre_ref, cmim_ref), (vals[2:4], vals[4:6], vals[6:8], vals[8:10])):
            ref[0] = pair[0].astype(BF16)
            ref[1] = pair[1].astype(BF16)

    dtypes = (F32, F32, BF16, BF16, BF16, BF16)
    return _call(
        body, name="s5_params_fwd",
        in_specs=[_full(s) for s in _S5_PARAM_SHAPES], out_specs=[_full(s) for s in _S5_OPERAND_SHAPES],
        out_shape=[_sds(s, d) for s, d in zip(_S5_OPERAND_SHAPES, dtypes)], compiler_params=_params(32),
    )(*params)


def _s5_params_bwd(params, cotangents):
    def body(*refs):
        ins, (dlre, dlim, dbtre, dbtim, dcmre, dcmim), outs = refs[:7], refs[7:13], refs[13:]
        _, vjp = jax.vjp(_s5_operands, *[r[...] for r in ins])
        cts = (dlre[...], dlim[...], dbtre[0], dbtre[1], dbtim[0], dbtim[1], dcmre[0], dcmre[1], dcmim[0], dcmim[1])
        for ref, val in zip(outs, vjp(cts)):
            ref[...] = val

    return _call(
        body, name="s5_params_bwd",
        in_specs=[_full(s) for s in _S5_PARAM_SHAPES + _S5_OPERAND_SHAPES],
        out_specs=[_full(s) for s in _S5_PARAM_SHAPES],
        out_shape=[_sds(s) for s in _S5_PARAM_SHAPES], compiler_params=_params(48),
    )(*params, *cotangents)


def _scan_geometry(n_seq, seq):
    slab = n_seq * SCAN_CHUNKS
    steps = seq // SCAN_CHUNKS
    tile_rows = slab * SCAN_TILE_STEPS
    n_tiles = steps // SCAN_TILE_STEPS
    return slab, steps, tile_rows, n_tiles


_SCAN_PARTS = D_SSM // LANES


def _whole_parts(rows):
    return [_full((rows, LANES))] * _SCAN_PARTS


def _part_shapes(rows):
    return [_sds((rows, LANES))] * _SCAN_PARTS


def _load_chunks(parts, first_chunk, n_chunks, steps, slab):
    return jnp.concatenate([
        jnp.concatenate([ref[pl.ds(first_chunk + q, steps, stride=slab), :] for ref in parts], axis=1)
        for q in range(n_chunks)], axis=0)


def _store_chunks(parts, first_chunk, value, steps, slab):
    for q in range(value.shape[0] // steps):
        for j, ref in enumerate(parts):
            ref[pl.ds(first_chunk + q, steps, stride=slab), :] = value[q * steps:(q + 1) * steps,
                                                                     j * LANES:(j + 1) * LANES]


def _join_parts(parts):
    return jnp.concatenate([ref[...] for ref in parts], axis=1)


def _split_parts(parts, value):
    for j, ref in enumerate(parts):
        ref[...] = value[:, j * LANES:(j + 1) * LANES]


def _complex_power(re, im, n):
    out = None
    while n:
        if n & 1:
            out = (re, im) if out is None else (out[0] * re - out[1] * im, out[0] * im + out[1] * re)
        n >>= 1
        if n:
            re, im = re * re - im * im, 2.0 * re * im
    return out


def _chunk_carry(sum_re, sum_im, carry_re, carry_im, a_re, a_im, n_seq, reverse):
    carry_re[...] = jnp.zeros_like(carry_re)
    carry_im[...] = jnp.zeros_like(carry_im)
    for s in range(n_seq):
        order = range(SCAN_CHUNKS - 2, -1, -1) if reverse else range(1, SCAN_CHUNKS)
        for c in order:
            r = s * SCAN_CHUNKS + c
            p = r + 1 if reverse else r - 1
            p_re, p_im = carry_re[p:p + 1, :], carry_im[p:p + 1, :]
            carry_re[r:r + 1, :] = a_re * p_re - a_im * p_im + sum_re[p:p + 1, :]
            carry_im[r:r + 1, :] = a_re * p_im + a_im * p_re + sum_im[p:p + 1, :]


def _s5_scan_fwd(u_parts, bt_re, bt_im, cm_re, cm_im, lbar_re, lbar_im, d_row, n_seq, seq):
    slab, steps, tile_rows, n_tiles = _scan_geometry(n_seq, seq)
    rows = u_parts[0].shape[0]

    def body(*refs):
        u_refs, refs = refs[:_SCAN_PARTS], refs[_SCAN_PARTS:]
        (bre_ref, bim_ref, cre_ref, cim_ref, lre_ref, lim_ref, d_ref), refs = refs[:7], refs[7:]
        y_refs, (hre_ref, him_ref, st_re, st_im, h0_re, h0_im, buf_re, buf_im) = refs[:_SCAN_PARTS], refs[_SCAN_PARTS:]
        second = pl.program_id(0) == 1
        i = pl.program_id(1)

        @pl.when(jnp.logical_and(i == 0, jnp.logical_not(second)))
        def _():
            st_re[...] = jnp.zeros_like(st_re)
            st_im[...] = jnp.zeros_like(st_im)

        u = _join_parts(u_refs)
        ub = u.astype(BF16)
        for hf in range(2):
            cols = slice(hf * 1024, (hf + 1) * 1024)
            buf_re[:, cols] = _dot(ub[:, hf * 256:(hf + 1) * 256], bre_ref[hf])
            buf_im[:, cols] = _dot(ub[:, hf * 256:(hf + 1) * 256], bim_ref[hf])

        for lc in range(SSM_LANES // SCAN_LANE_CHUNK):
            cols = slice(lc * SCAN_LANE_CHUNK, (lc + 1) * SCAN_LANE_CHUNK)
            l_re = jnp.broadcast_to(lre_ref[:, cols], (slab, SCAN_LANE_CHUNK))
            l_im = jnp.broadcast_to(lim_ref[:, cols], (slab, SCAN_LANE_CHUNK))

            def scan_tile(keep_states):
                def step(t, carry):
                    s_re, s_im = carry
                    r0 = pl.multiple_of(t * slab, slab)
                    n_re = l_re * s_re - l_im * s_im + buf_re[pl.ds(r0, slab), cols]
                    n_im = l_re * s_im + l_im * s_re + buf_im[pl.ds(r0, slab), cols]
                    if keep_states:
                        buf_re[pl.ds(r0, slab), cols] = n_re
                        buf_im[pl.ds(r0, slab), cols] = n_im
                    return n_re, n_im

                s_re, s_im = lax.fori_loop(0, SCAN_TILE_STEPS, step, (st_re[:, cols], st_im[:, cols]), unroll=True)
                st_re[:, cols] = s_re
                st_im[:, cols] = s_im

            pl.when(jnp.logical_not(second))(lambda: scan_tile(False))
            pl.when(second)(lambda: scan_tile(True))

        @pl.when(jnp.logical_and(i == n_tiles - 1, jnp.logical_not(second)))
        def _():
            a_re, a_im = _complex_power(lre_ref[...], lim_ref[...], steps)
            _chunk_carry(st_re, st_im, h0_re, h0_im, a_re, a_im, n_seq, reverse=False)
            st_re[...] = h0_re[...]
            st_im[...] = h0_im[...]

        @pl.when(second)
        def _():
            h_re = buf_re[...].astype(BF16)
            h_im = buf_im[...].astype(BF16)
            hre_ref[...] = h_re
            him_ref[...] = h_im
            for hf in range(2):
                cols = slice(hf * 1024, (hf + 1) * 1024)
                ycols = slice(hf * 256, (hf + 1) * 256)
                y_half = (_dot_nt(h_re[:, cols], cre_ref[hf]) - _dot_nt(h_im[:, cols], cim_ref[hf])
                          + d_ref[:, ycols] * u[:, ycols])
                _split_parts(y_refs[2 * hf:2 * hf + 2], y_half)

    tile = lambda w: pl.BlockSpec((tile_rows, w), lambda p, i: (i, 0))
    out_tile = lambda w: pl.BlockSpec((tile_rows, w), lambda p, i: (i * p, 0))
    cm = _full(_CM_SHAPE)
    outs = _call(
        body, name="s5_scan_fwd", grid=(2, n_tiles),
        in_specs=[tile(LANES)] * _SCAN_PARTS + [cm, cm, cm, cm, _full((1, SSM_LANES)), _full((1, SSM_LANES)),
                                                _full((1, 512))],
        out_specs=[out_tile(LANES)] * _SCAN_PARTS + [out_tile(SSM_LANES), out_tile(SSM_LANES)],
        out_shape=_part_shapes(rows) + [_sds((rows, SSM_LANES), BF16), _sds((rows, SSM_LANES), BF16)],
        scratch_shapes=[pltpu.VMEM((slab, SSM_LANES), F32)] * 4 + [pltpu.VMEM((tile_rows, SSM_LANES), F32)] * 2,
        compiler_params=_params(40, ("arbitrary", "arbitrary")),
    )(*u_parts, bt_re, bt_im, cm_re, cm_im, lbar_re, lbar_im, d_row)
    return outs[:_SCAN_PARTS], outs[_SCAN_PARTS], outs[_SCAN_PARTS + 1]


def _s5_scan_bwd(dy_parts, u_parts, h_re, h_im, bt_re, bt_im, cm_re, cm_im, lbar_re, lbar_im, d_row, n_seq, seq):
    slab, steps, tile_rows, n_tiles = _scan_geometry(n_seq, seq)
    rows = u_parts[0].shape[0]

    def body(*refs):
        dy_refs, u_refs, refs = refs[:_SCAN_PARTS], refs[_SCAN_PARTS:2 * _SCAN_PARTS], refs[2 * _SCAN_PARTS:]
        (hre_ref, him_ref, bre_ref, bim_ref, cre_ref, cim_ref, lre_ref, lim_ref, d_ref), refs = refs[:9], refs[9:]
        du_refs, refs = refs[:_SCAN_PARTS], refs[_SCAN_PARTS:]
        (dbre_ref, dbim_ref, dcre_ref, dcim_ref, dlre_ref, dlim_ref, dd_ref,
         st_re, st_im, g0_re, g0_im, acc_re, acc_im, buf_re, buf_im) = refs
        second = pl.program_id(0) == 1
        i = pl.program_id(1)

        @pl.when(jnp.logical_and(i == 0, jnp.logical_not(second)))
        def _():
            st_re[...] = jnp.zeros_like(st_re)
            st_im[...] = jnp.zeros_like(st_im)
            acc_re[...] = jnp.zeros_like(acc_re)
            acc_im[...] = jnp.zeros_like(acc_im)
            for ref in (dbre_ref, dbim_ref, dcre_ref, dcim_ref, dd_ref):
                ref[...] = jnp.zeros_like(ref)

        dy = _join_parts(dy_refs)
        dyb = dy.astype(BF16)
        for hf in range(2):
            cols = slice(hf * 1024, (hf + 1) * 1024)
            buf_re[:, cols] = _dot(dyb[:, hf * 256:(hf + 1) * 256], cre_ref[hf])
            buf_im[:, cols] = -_dot(dyb[:, hf * 256:(hf + 1) * 256], cim_ref[hf])

        for lc in range(SSM_LANES // SCAN_LANE_CHUNK):
            cols = slice(lc * SCAN_LANE_CHUNK, (lc + 1) * SCAN_LANE_CHUNK)
            l_re = jnp.broadcast_to(lre_ref[:, cols], (slab, SCAN_LANE_CHUNK))
            l_im = jnp.broadcast_to(lim_ref[:, cols], (slab, SCAN_LANE_CHUNK))

            def advance(r0, s_re, s_im):
                n_re = l_re * s_re + l_im * s_im + buf_re[pl.ds(r0, slab), cols]
                n_im = l_re * s_im - l_im * s_re + buf_im[pl.ds(r0, slab), cols]
                buf_re[pl.ds(r0, slab), cols] = n_re
                buf_im[pl.ds(r0, slab), cols] = n_im
                return n_re, n_im

            def row0(k):
                return pl.multiple_of((SCAN_TILE_STEPS - 1 - k) * slab, slab)

            @pl.when(jnp.logical_not(second))
            def _():
                s_re, s_im = lax.fori_loop(0, SCAN_TILE_STEPS, lambda k, s: advance(row0(k), *s),
                                           (st_re[:, cols], st_im[:, cols]), unroll=True)
                st_re[:, cols] = s_re
                st_im[:, cols] = s_im

            @pl.when(second)
            def _():
                def step(k, carry):
                    s_re, s_im, a_re, a_im = carry
                    r0 = row0(k)
                    hr = hre_ref[pl.ds(r0, slab), cols].astype(F32)
                    hi = him_ref[pl.ds(r0, slab), cols].astype(F32)
                    a_re = a_re + s_re * hr + s_im * hi
                    a_im = a_im + s_im * hr - s_re * hi
                    return advance(r0, s_re, s_im) + (a_re, a_im)

                zero = jnp.zeros((slab, SCAN_LANE_CHUNK), F32)
                s_re, s_im, a_re, a_im = lax.fori_loop(
                    0, SCAN_TILE_STEPS, step, (st_re[:, cols], st_im[:, cols], zero, zero), unroll=True)
                st_re[:, cols] = s_re
                st_im[:, cols] = s_im
                acc_re[:, cols] += a_re
                acc_im[:, cols] += a_im

        @pl.when(jnp.logical_and(i == n_tiles - 1, jnp.logical_not(second)))
        def _():
            p_re, p_im = _complex_power(lre_ref[...], lim_ref[...], steps)
            _chunk_carry(st_re, st_im, g0_re, g0_im, p_re, -p_im, n_seq, reverse=True)
            st_re[...] = g0_re[...]
            st_im[...] = g0_im[...]

        @pl.when(second)
        def _():
            u = _join_parts(u_refs)
            ub = u.astype(BF16)
            g_re = buf_re[...].astype(BF16)
            g_im = buf_im[...].astype(BF16)
            dd_ref[...] += jnp.sum(dy * u, axis=0, keepdims=True)
            for hf in range(2):
                cols = slice(hf * 1024, (hf + 1) * 1024)
                ycols = slice(hf * 256, (hf + 1) * 256)
                du_half = (_dot_nt(g_re[:, cols], bre_ref[hf]) + _dot_nt(g_im[:, cols], bim_ref[hf])
                           + d_ref[:, ycols] * dy[:, ycols])
                _split_parts(du_refs[2 * hf:2 * hf + 2], du_half)
                for q4 in range(_HALF_GROUPS // 4):
                    ch = slice(hf * 256 + q4 * 64, hf * 256 + (q4 + 1) * 64)
                    st = slice(hf * 1024 + q4 * 256, hf * 1024 + (q4 + 1) * 256)
                    blk = (hf, slice(q4 * 64, (q4 + 1) * 64), slice(q4 * 256, (q4 + 1) * 256))
                    dbre_ref[blk] += _dot_tn(ub[:, ch], g_re[:, st])
                    dbim_ref[blk] += _dot_tn(ub[:, ch], g_im[:, st])
                    dcre_ref[blk] += _dot_tn(dyb[:, ch], hre_ref[:, st])
                    dcim_ref[blk] -= _dot_tn(dyb[:, ch], him_ref[:, st])

        @pl.when(jnp.logical_and(i == n_tiles - 1, second))
        def _():
            dlre_ref[...] = jnp.sum(acc_re[...], axis=0, keepdims=True)
            dlim_ref[...] = jnp.sum(acc_im[...], axis=0, keepdims=True)

    tile = lambda w: pl.BlockSpec((tile_rows, w), lambda p, i: (n_tiles - 1 - i, 0))
    second_tile = lambda w: pl.BlockSpec((tile_rows, w), lambda p, i: (n_tiles - 1 - i * p, 0))
    cm = _full(_CM_SHAPE)
    row = _full((1, SSM_LANES))
    outs = _call(
        body, name="s5_scan_bwd", grid=(2, n_tiles),
        in_specs=[tile(LANES)] * _SCAN_PARTS + [second_tile(LANES)] * _SCAN_PARTS
        + [second_tile(SSM_LANES), second_tile(SSM_LANES), cm, cm, cm, cm, row, row, _full((1, 512))],
        out_specs=[second_tile(LANES)] * _SCAN_PARTS + [cm, cm, cm, cm, row, row, _full((1, 512))],
        out_shape=(_part_shapes(rows) + [_sds(_CM_SHAPE)] * 4 + [_sds((1, SSM_LANES))] * 2 + [_sds((1, 512))]),
        scratch_shapes=[pltpu.VMEM((slab, SSM_LANES), F32)] * 6 + [pltpu.VMEM((tile_rows, SSM_LANES), F32)] * 2,
        compiler_params=_params(48, ("arbitrary", "arbitrary")),
    )(*dy_parts, *u_parts, h_re, h_im, bt_re, bt_im, cm_re, cm_im, lbar_re, lbar_im, d_row)
    return (outs[:_SCAN_PARTS],) + tuple(outs[_SCAN_PARTS:])


def _glu_gate(gl, a, zs):
    return gl * jax.nn.sigmoid(a) * _silu(zs)


def _glu_fwd(y_parts, zs, w_glu, b_glu, n_seq, seq):
    rows = zs.shape[0]
    tm = 512
    slab, steps, _, _ = _scan_geometry(n_seq, seq)

    def body(*refs):
        y_refs, (zs_ref, w_ref, b_ref, o_ref) = refs[:_SCAN_PARTS], refs[_SCAN_PARTS:]
        y = _load_chunks(y_refs, pl.program_id(0) * (tm // steps), tm // steps, steps, slab)
        gl = jax.nn.gelu(y)
        a = _dot(gl.astype(BF16), w_ref[...]) + b_ref[...]
        o_ref[...] = _glu_gate(gl, a, zs_ref[...]).astype(BF16)

    return _call(
        body, name="glu_fwd", grid=(rows // tm,),
        in_specs=_whole_parts(rows) + [_rows(tm, 512), _full((512, 512)), _full((1, 512))],
        out_specs=_rows(tm, 512), out_shape=_sds((rows, 512), BF16),
        compiler_params=_params(32, ("arbitrary",)),
    )(*y_parts, zs, w_glu, b_glu)


def _glu_bwd(y_parts, zs, d_out, w_glu, b_glu, n_seq, seq):
    rows = zs.shape[0]
    tm = 512
    slab, steps, _, _ = _scan_geometry(n_seq, seq)

    def body(*refs):
        y_refs, (zs_ref, d_ref, w_ref, b_ref), refs = refs[:_SCAN_PARTS], refs[_SCAN_PARTS:_SCAN_PARTS + 4], refs[_SCAN_PARTS + 4:]
        dy_refs, (dzs_ref, dw_ref, db_ref) = refs[:_SCAN_PARTS], refs[_SCAN_PARTS:]
        first_chunk = pl.program_id(0) * (tm // steps)

        @pl.when(pl.program_id(0) == 0)
        def _():
            dw_ref[...] = jnp.zeros_like(dw_ref)
            db_ref[...] = jnp.zeros_like(db_ref)

        gl, gelu_vjp = jax.vjp(jax.nn.gelu, _load_chunks(y_refs, first_chunk, tm // steps, steps, slab))
        glb = gl.astype(BF16)
        a = _dot(glb, w_ref[...]) + b_ref[...]
        _, gate_vjp = jax.vjp(_glu_gate, gl, a, zs_ref[...])
        d_gl, d_a, d_zs = gate_vjp(d_ref[...])
        dab = d_a.astype(BF16)
        d_gl = d_gl + _dot_nt(dab, w_ref[...])
        _store_chunks(dy_refs, first_chunk, gelu_vjp(d_gl)[0], steps, slab)
        dzs_ref[...] = d_zs.astype(BF16)
        dw_ref[...] += _dot_tn(glb, dab)
        db_ref[...] += jnp.sum(d_a, axis=0, keepdims=True)

    *dy_parts, dzs, dw, db = _call(
        body, name="glu_bwd", grid=(rows // tm,),
        in_specs=_whole_parts(rows) + [_rows(tm, 512), _rows(tm, 512), _full((512, 512)), _full((1, 512))],
        out_specs=_whole_parts(rows) + [_rows(tm, 512), _full((512, 512)), _full((1, 512))],
        out_shape=_part_shapes(rows) + [_sds((rows, 512), BF16), _sds((512, 512)), _sds((1, 512))],
        compiler_params=_params(40, ("arbitrary",)),
    )(*y_parts, zs, d_out, w_glu, b_glu)
    return dy_parts, dzs, dw, db


_GROUP_ROWS = Q_PER_KV * BLOCK
_BLOCK_SHIFT = BLOCK.bit_length() - 1


def _attn_bias(j):
    row = _iota((_GROUP_ROWS, BLOCK), 0)
    dist_cur = (row & (BLOCK - 1)) - _iota((_GROUP_ROWS, BLOCK), 1)
    dist_prev = dist_cur + BLOCK
    head = row >> _BLOCK_SHIFT
    slope = jnp.zeros((_GROUP_ROWS, BLOCK), F32)
    for g in range(Q_PER_KV):
        slope = jnp.where(head == g, 2.0 ** (-(j * Q_PER_KV + g + 1)), slope)
    bias_cur = jnp.where(dist_cur >= 0, -slope * dist_cur.astype(F32), -jnp.inf)
    bias_prev = jnp.where(dist_prev < WINDOW, -slope * dist_prev.astype(F32), -jnp.inf)
    return bias_cur, bias_prev


_ATTN_BIAS_SCRATCH = pltpu.VMEM((KV_HEADS, 2, _GROUP_ROWS, BLOCK), F32)


def _fill_attn_bias(bias_ref):
    @pl.when(jnp.logical_and(pl.program_id(0) == 0, pl.program_id(1) == 0))
    def _():
        for j in range(KV_HEADS):
            bias_ref[j, 0], bias_ref[j, 1] = _attn_bias(j)


def _stack_heads(x, j):
    heads = range(j * Q_PER_KV, (j + 1) * Q_PER_KV)
    return jnp.concatenate([x[:, h * HEAD_DIM:(h + 1) * HEAD_DIM] for h in heads], axis=0)


def _stack_columns(x, j):
    heads = range(j * Q_PER_KV, (j + 1) * Q_PER_KV)
    return jnp.concatenate([jnp.broadcast_to(x[:, h:h + 1], (BLOCK, 1)) for h in heads], axis=0)


def _attn_fwd(q, k, v, za, sinks, n_seq, seq):
    nb = seq // BLOCK
    rows = q.shape[0]

    def body(q_ref, kc_ref, kp_ref, vc_ref, vp_ref, za_ref, sk_ref, o_ref, ao_ref, lse_ref, bias_ref):
        _fill_attn_bias(bias_ref)
        has_prev = pl.program_id(1) > 0
        q_all = q_ref[...]
        for j in range(KV_HEADS):
            js = slice(j * HEAD_DIM, (j + 1) * HEAD_DIM)
            bias_c, bias_p = bias_ref[j, 0], bias_ref[j, 1]
            q4 = _stack_heads(q_all, j)
            sc = _dot_nt(q4, kc_ref[:, js]) + bias_c
            sp = _dot_nt(q4, kp_ref[:, js]) + jnp.where(has_prev, bias_p, -jnp.inf)
            sink = _stack_columns(sk_ref[...], j)
            m = jnp.maximum(jnp.max(jnp.maximum(sc, sp), axis=-1, keepdims=True), sink)
            ec = jnp.exp(sc - m)
            ep = jnp.exp(sp - m)
            den = jnp.sum(ec + ep, axis=-1, keepdims=True) + jnp.exp(sink - m)
            inv = 1.0 / den
            o4 = _dot((ec * inv).astype(BF16), vc_ref[:, js]) + _dot((ep * inv).astype(BF16), vp_ref[:, js])
            lse4 = m + jnp.log(den)
            for g in range(Q_PER_KV):
                h = j * Q_PER_KV + g
                o_ref[:, h * HEAD_DIM:(h + 1) * HEAD_DIM] = o4[g * BLOCK:(g + 1) * BLOCK]
                lse_ref[:, h:h + 1] = lse4[g * BLOCK:(g + 1) * BLOCK]
        ao_ref[...] = (o_ref[...] * _silu(za_ref[...])).astype(BF16)

    cur = lambda w: pl.BlockSpec((BLOCK, w), lambda b, n: (b * nb + n, 0))
    prev = lambda w: pl.BlockSpec((BLOCK, w), lambda b, n: (b * nb + jnp.maximum(n - 1, 0), 0))
    return _call(
        body, name="attn_fwd", grid=(n_seq, nb),
        in_specs=[cur(512), cur(128), prev(128), cur(128), prev(128), cur(512), _full((1, N_HEADS))],
        out_specs=[cur(512), cur(512), cur(N_HEADS)],
        out_shape=[_sds((rows, 512)), _sds((rows, 512), BF16), _sds((rows, N_HEADS))],
        scratch_shapes=[_ATTN_BIAS_SCRATCH], compiler_params=_params(32, ("arbitrary", "arbitrary")),
    )(q, k, k, v, v, za, sinks)


def _attn_bwd(q, k, v, za, o, lse, d_ao, sinks, n_seq, seq):
    nb = seq // BLOCK
    rows = q.shape[0]

    def body(q_ref, kc_ref, kp_ref, vc_ref, vp_ref, za_ref, o_ref, lse_ref, d_ref, sk_ref,
             dq_ref, dk_ref, dv_ref, dza_ref, dsk_ref, bias_ref, dk_carry, dv_carry):
        n = nb - 1 - pl.program_id(1)
        _fill_attn_bias(bias_ref)

        @pl.when(jnp.logical_and(pl.program_id(0) == 0, pl.program_id(1) == 0))
        def _():
            dsk_ref[...] = jnp.zeros_like(dsk_ref)
            dk_carry[...] = jnp.zeros_like(dk_carry)
            dv_carry[...] = jnp.zeros_like(dv_carry)

        has_prev = n > 0
        has_next = n + 1 < nb

        _, gate_vjp = jax.vjp(lambda o_, z_: o_ * _silu(z_), o_ref[...], za_ref[...])
        d_o, d_za = gate_vjp(d_ref[...])
        dza_ref[...] = d_za.astype(BF16)
        q_all = q_ref[...]
        lse_all = lse_ref[...]

        for j in range(KV_HEADS):
            js = slice(j * HEAD_DIM, (j + 1) * HEAD_DIM)
            kc, kp, vc, vp = kc_ref[:, js], kp_ref[:, js], vc_ref[:, js], vp_ref[:, js]
            bias_c, bias_p = bias_ref[j, 0], bias_ref[j, 1]
            q4 = _stack_heads(q_all, j)
            do4b = _stack_heads(d_o, j).astype(BF16)
            lse4 = _stack_columns(lse_all, j)
            pc = jnp.exp(_dot_nt(q4, kc) + bias_c - lse4)
            pp = jnp.exp(_dot_nt(q4, kp) + jnp.where(has_prev, bias_p, -jnp.inf) - lse4)
            dpc = _dot_nt(do4b, vc)
            dpp = _dot_nt(do4b, vp)
            delta = jnp.sum(pc * dpc + pp * dpp, axis=-1, keepdims=True)
            dsc = (pc * (dpc - delta)).astype(BF16)
            dsp = (pp * (dpp - delta)).astype(BF16)
            dq4 = ((_dot(dsc, kc) + _dot(dsp, kp)) * ATTN_SCALE).astype(BF16)
            sink_loss = jnp.exp(_stack_columns(sk_ref[...], j) - lse4) * delta
            for g in range(Q_PER_KV):
                h = j * Q_PER_KV + g
                dq_ref[:, h * HEAD_DIM:(h + 1) * HEAD_DIM] = dq4[g * BLOCK:(g + 1) * BLOCK]
                dsk_ref[0:1, h:h + 1] -= jnp.sum(sink_loss[g * BLOCK:(g + 1) * BLOCK], axis=0, keepdims=True)
            dk = _dot_tn(dsc, q4) + jnp.where(has_next, dk_carry[j], 0.0)
            dv = _dot_tn(pc.astype(BF16), do4b) + jnp.where(has_next, dv_carry[j], 0.0)
            dk_carry[j] = _dot_tn(dsp, q4)
            dv_carry[j] = _dot_tn(pp.astype(BF16), do4b)
            dk_ref[:, js] = dk.astype(BF16)
            dv_ref[:, js] = dv.astype(BF16)

    cur = lambda w: pl.BlockSpec((BLOCK, w), lambda b, s: (b * nb + nb - 1 - s, 0))
    prev = lambda w: pl.BlockSpec((BLOCK, w), lambda b, s: (b * nb + jnp.maximum(nb - 2 - s, 0), 0))
    return _call(
        body, name="attn_bwd", grid=(n_seq, nb),
        in_specs=[cur(512), cur(128), prev(128), cur(128), prev(128), cur(512), cur(512), cur(N_HEADS), cur(512),
                  _full((1, N_HEADS))],
        out_specs=[cur(512), cur(128), cur(128), cur(512), _full((1, N_HEADS))],
        out_shape=[_sds((rows, 512), BF16), _sds((rows, 128), BF16), _sds((rows, 128), BF16),
                   _sds((rows, 512), BF16), _sds((1, N_HEADS))],
        scratch_shapes=[_ATTN_BIAS_SCRATCH, pltpu.VMEM((KV_HEADS, BLOCK, HEAD_DIM), F32),
                        pltpu.VMEM((KV_HEADS, BLOCK, HEAD_DIM), F32)],
        compiler_params=_params(32, ("arbitrary", "arbitrary")),
    )(q, k, k, v, v, za, o, lse, d_ao, sinks)


def _tail(ssm_out, attn_out, x2d, p2d, target, w_out, g2, w_gate, b_gate, w_proj):
    rows = x2d.shape[0]
    tm = 512

    def body(so_ref, ao_ref, x_ref, p_ref, t_ref, wo_ref, g2_ref, wg_ref, bg_ref, wp_ref,
             dh1_ref, dso_ref, dao_ref, dwo_ref, dwg_ref, dwp_ref, dbg_ref, dg2_ref, loss_ref):
        @pl.when(pl.program_id(0) == 0)
        def _():
            for ref in (dwo_ref, dwg_ref, dwp_ref, dbg_ref, dg2_ref, loss_ref):
                ref[...] = jnp.zeros_like(ref)

        cat = jnp.concatenate([so_ref[...], ao_ref[...]], axis=1)
        g2 = g2_ref[...]
        mixed = _dot(cat, wo_ref[...])
        r = lax.rsqrt(jnp.mean(mixed * mixed, axis=-1, keepdims=True) + EPS)
        mr = mixed * r
        h1 = x_ref[...] + mr * g2
        h1b = h1.astype(BF16)
        gate = jax.nn.sigmoid(_dot(h1b, wg_ref[...]) + bg_ref[...])
        pb = p_ref[...].astype(BF16)
        wp_blocks = [slice(j * D_PLE, (j + 1) * D_PLE) for j in range(N_CHIPS)]
        pp = jnp.concatenate([_dot(pb, wp_ref[blk, :]) for blk in wp_blocks], axis=1)
        err = h1 + gate * pp - t_ref[...]
        loss_ref[...] += 0.5 * jnp.sum(jnp.mean(err * err, axis=-1, keepdims=True), axis=0, keepdims=True)

        dh2 = err * (1.0 / D_MODEL)
        d_glin = dh2 * pp * gate * (1.0 - gate)
        d_glin_b = d_glin.astype(BF16)
        dwg_ref[...] += _dot_tn(h1b, d_glin_b)
        dbg_ref[...] += jnp.sum(d_glin, axis=0, keepdims=True)
        d_pp = (dh2 * gate).astype(BF16)
        for blk in wp_blocks:
            dwp_ref[blk, :] += _dot_tn(pb, d_pp[:, blk])
        dh1 = dh2 + _dot_nt(d_glin_b, wg_ref[...])
        dh1_ref[...] = dh1
        dg2_ref[...] += jnp.sum(dh1 * mr, axis=0, keepdims=True)
        a_ = dh1 * g2
        d_mixed = (r * a_ - mr * (r * jnp.mean(a_ * mr, axis=-1, keepdims=True))).astype(BF16)
        dwo_ref[...] += _dot_tn(cat, d_mixed)
        d_cat = _dot_nt(d_mixed, wo_ref[...])
        dso_ref[...] = d_cat[:, 0:512]
        dao_ref[...] = d_cat[:, 512:1024]

    return _call(
        body, name="tail_fwd_bwd", grid=(rows // tm,),
        in_specs=[_rows(tm, 512), _rows(tm, 512), _rows(tm, D_MODEL), _rows(tm, D_PLE), _rows(tm, D_MODEL),
                  _full((D_MODEL, D_MODEL)), _full((1, D_MODEL)), _full((D_MODEL, D_MODEL)), _full((1, D_MODEL)),
                  _full((N_CHIPS * D_PLE, D_PLE))],
        out_specs=[_rows(tm, D_MODEL), _rows(tm, 512), _rows(tm, 512), _full((D_MODEL, D_MODEL)),
                   _full((D_MODEL, D_MODEL)), _full((N_CHIPS * D_PLE, D_PLE)), _full((1, D_MODEL)), _full((1, D_MODEL)),
                   _full((1, 1))],
        out_shape=[_sds((rows, D_MODEL)), _sds((rows, 512)), _sds((rows, 512)), _sds((D_MODEL, D_MODEL)),
                   _sds((D_MODEL, D_MODEL)), _sds((N_CHIPS * D_PLE, D_PLE)), _sds((1, D_MODEL)), _sds((1, D_MODEL)),
                   _sds((1, 1))],
        compiler_params=_params(52, ("arbitrary",)),
    )(ssm_out, attn_out, x2d, p2d, target, w_out, g2, w_gate, b_gate, w_proj)


def _local_step(x, p, target, pre_norm_g, w_in_t, s5_params, ssm_d, w_glu, b_glu, sinks, w_out, post_norm_g, w_proj,
                w_gate, b_gate):
    n_seq, seq, _ = x.shape
    rows = n_seq * seq
    x2d = x.reshape(rows, D_MODEL)
    p2d = p.reshape(rows, D_PLE)
    t2d = target.reshape(rows, D_MODEL)

    l_re, l_im, bt_re, bt_im, cm_re, cm_im = _s5_params_fwd(*s5_params)

    u_scan, zs, q, k, v, za = _in_proj(x2d, pre_norm_g, w_in_t, n_seq, seq)
    y_scan, h_re, h_im = _s5_scan_fwd(u_scan, bt_re, bt_im, cm_re, cm_im, l_re, l_im, ssm_d, n_seq, seq)
    ssm_out = _glu_fwd(y_scan, zs, w_glu, b_glu, n_seq, seq)
    o, attn_out, lse = _attn_fwd(q, k, v, za, sinks, n_seq, seq)

    dh1, d_so, d_ao, d_w_out, d_w_gate, d_w_proj, d_b_gate, d_g2, loss = _tail(
        ssm_out, attn_out, x2d, p2d, t2d, w_out, post_norm_g, w_gate, b_gate, w_proj)

    dq, dk, dv, dza, d_sinks = _attn_bwd(q, k, v, za, o, lse, d_ao, sinks, n_seq, seq)
    dy_scan, dzs, d_w_glu, d_b_glu = _glu_bwd(y_scan, zs, d_so, w_glu, b_glu, n_seq, seq)
    du_scan, d_bt_re, d_bt_im, d_cm_re, d_cm_im, d_l_re, d_l_im, d_d = _s5_scan_bwd(
        dy_scan, u_scan, h_re, h_im, bt_re, bt_im, cm_re, cm_im, l_re, l_im, ssm_d, n_seq, seq)
    d_lam_re, d_lam_im, d_log_step, d_b_re, d_b_im, d_c_re, d_c_im = _s5_params_bwd(
        s5_params, (d_l_re, d_l_im, d_bt_re, d_bt_im, d_cm_re, d_cm_im))

    grad_x, d_w_in_t, d_g1 = _in_proj_bwd(x2d, dh1, pre_norm_g, w_in_t, du_scan, dzs, dq, dk, dv, dza, n_seq, seq)
    grads = dict(
        pre_norm_g=d_g1, w_in=d_w_in_t, ssm_lam_re=d_lam_re, ssm_lam_im=d_lam_im, ssm_log_step=d_log_step,
        ssm_b_re=d_b_re, ssm_b_im=d_b_im, ssm_c_re=d_c_re, ssm_c_im=d_c_im, ssm_d=d_d, ssm_w_glu=d_w_glu,
        ssm_b_glu=d_b_glu, attn_sinks=d_sinks, w_out=d_w_out, post_norm_g=d_g2, pl_w_proj=d_w_proj,
        pl_w_gate=d_w_gate, pl_b_gate=d_b_gate)
    return grad_x.reshape(x.shape), loss, grads


_BIG = ("w_in", "ssm_w_glu", "w_out", "pl_w_proj", "pl_w_gate")
_BIG_SHARD = {"w_in": (D_IN // N_CHIPS, D_MODEL), "ssm_w_glu": (D_SSM // N_CHIPS, D_SSM),
              "w_out": (D_MODEL // N_CHIPS, D_MODEL), "pl_w_proj": (D_PLE, D_MODEL // N_CHIPS),
              "pl_w_gate": (D_MODEL // N_CHIPS, D_MODEL)}
_SMALL = {"pre_norm_g": (1, D_MODEL), "ssm_lam_re": (SSM_GROUPS, SSM_STATE), "ssm_lam_im": (SSM_GROUPS, SSM_STATE),
          "ssm_log_step": (1, SSM_GROUPS), "ssm_b_re": (D_SSM, SSM_STATE), "ssm_b_im": (D_SSM, SSM_STATE),
          "ssm_c_re": (D_SSM, SSM_STATE), "ssm_c_im": (D_SSM, SSM_STATE), "ssm_d": (1, D_SSM), "ssm_b_glu": (1, D_SSM),
          "attn_sinks": (1, N_HEADS), "post_norm_g": (1, D_MODEL), "pl_b_gate": (1, D_MODEL)}
_VEC_ROWS = ("pre_norm_g", "post_norm_g", "pl_b_gate", "ssm_d", "ssm_b_glu", "attn_sinks", "ssm_log_step", "loss")
_SMALL_GROUPS = (
    ("vec", (8, D_MODEL), tuple((name, r) for r, name in enumerate(_VEC_ROWS))),
    ("lam", (2 * SSM_GROUPS, SSM_STATE), (("ssm_lam_re", 0), ("ssm_lam_im", SSM_GROUPS))),
)
_SMALL_EARLY = ("ssm_b_re", "ssm_b_im", "ssm_c_re", "ssm_c_im")
_SMALL_ORDER = tuple(name for _, _, members in _SMALL_GROUPS for name, _ in members) + _SMALL_EARLY
_WEIGHT_ORDER = ("pre_norm_g", "w_in", "ssm_lam_re", "ssm_lam_im", "ssm_log_step", "ssm_b_re", "ssm_b_im", "ssm_c_re",
                 "ssm_c_im", "ssm_d", "ssm_w_glu", "ssm_b_glu", "attn_sinks", "w_out", "post_norm_g", "pl_w_proj",
                 "pl_w_gate", "pl_b_gate")


def _small_shape(name):
    return (1, 1) if name == "loss" else _SMALL[name]


def _to_kernel_form(name, a):
    a = a[0]
    if name == "w_in":
        return a.T
    if name in ("ssm_b_re", "ssm_b_im"):
        a = a.transpose(0, 2, 1)
    return a.reshape(_SMALL[name]) if name in _SMALL else a


def _from_kernel_form(name, a, shape):
    if name == "w_in":
        a = a.T
    if name in ("ssm_b_re", "ssm_b_im"):
        a = a.reshape(SSM_GROUPS, SSM_GROUP_CH, SSM_STATE).transpose(0, 2, 1)
    return a.reshape(shape)


def _mesh_place():
    x, y, c = lax.axis_index("x"), lax.axis_index("y"), lax.axis_index("c")
    other_chips = ((1 - x, y), (x, 1 - y), (1 - x, 1 - y))
    return x, y, c, other_chips


def _gather_copies(s_refs, g_refs, send_sems, recv_sems, local_sems):
    x, y, c, other_chips = _mesh_place()
    started = []
    for i, (s_ref, g_ref) in enumerate(zip(s_refs, g_refs)):
        rows = s_ref.shape[0]
        half = rows // 2

        def block(chip, g_ref=g_ref, rows=rows, half=half):
            return g_ref.at[pl.ds((2 * chip[0] + chip[1]) * rows + c * half, half), :]

        def copy(k, chip, to, src=None, i=i, block=block):
            return pltpu.make_async_remote_copy(
                src_ref=block(chip) if src is None else src, dst_ref=block(chip), send_sem=send_sems.at[6 * i + k],
                recv_sem=recv_sems.at[6 * i + k], device_id=to, device_id_type=MESH)

        own = pltpu.make_async_copy(s_ref, g_ref.at[pl.ds((2 * x + y) * rows, rows), :], local_sems.at[i])
        own.start()
        first = [copy(k, (x, y), (*chip, c), src=s_ref.at[pl.ds(c * half, half), :])
                 for k, chip in enumerate(other_chips)]
        for cp in first:
            cp.start()
        passed = [copy(3 + k, chip, (x, y, 1 - c)) for k, chip in enumerate(other_chips)]
        started.append((own, first, passed))
    for own, first, passed in started:
        for k in range(3):
            first[k].wait_recv()
            passed[k].start()
    for own, first, passed in started:
        for k in range(3):
            passed[k].wait_recv()
        for cp in first + passed:
            cp.wait_send()
        own.wait()


def _gather_semaphores(n_t):
    return [pltpu.SemaphoreType.DMA((6 * n_t,)), pltpu.SemaphoreType.DMA((6 * n_t,)), pltpu.SemaphoreType.DMA((n_t,))]


def _gather_weights(shards):
    n_t = len(shards)

    def body(*refs):
        _gather_copies(refs[:n_t], refs[n_t:2 * n_t], *refs[2 * n_t + 1:])
        refs[2 * n_t][...] = jnp.zeros_like(refs[2 * n_t])

    any_spec = pl.BlockSpec(memory_space=pl.ANY)
    *full, done = _call(
        body, name="gather_weights", in_specs=[any_spec] * n_t,
        out_specs=[any_spec] * n_t + [pl.BlockSpec(memory_space=pltpu.VMEM)],
        out_shape=[_sds((N_CHIPS * s.shape[0], s.shape[1]), s.dtype) for s in shards]
        + [jax.ShapeDtypeStruct((8, LANES), F32)],
        scratch_shapes=_gather_semaphores(n_t),
    )(*shards)
    return full, done[0, 0]


def _gather_weights_beside(shards):
    n_t = len(shards)
    hbm = pltpu.MemorySpace.HBM
    s_refs = [jax.new_ref(s, memory_space=hbm) for s in shards]
    g_refs = [jax.empty_ref(jax.ShapeDtypeStruct((N_CHIPS * s.shape[0], s.shape[1]), s.dtype), memory_space=hbm)
              for s in shards]

    def launch(send_sems, recv_sems, local_sems):
        x, y, c, other_chips = _mesh_place()
        peers = [(*chip, c) for chip in other_chips] + [(x, y, 1 - c)]
        barrier = pltpu.get_barrier_semaphore()
        for peer in peers:
            pl.semaphore_signal(barrier, inc=1, device_id=peer, device_id_type=MESH)
        pl.semaphore_wait(barrier, len(peers))
        _gather_copies(s_refs, g_refs, send_sems, recv_sems, local_sems)

    pl.kernel(launch, mesh=plsc.ScalarSubcoreMesh(axis_name="sequencer", num_cores=1), name="gather_weights_beside",
              scratch_types=_gather_semaphores(n_t), compiler_params=pltpu.CompilerParams(collective_id=1))()
    return [g[...] for g in g_refs]


_RELATIONS = tuple(((r >> 2) & 1, (r >> 1) & 1, r & 1) for r in range(1, 8))


def _related(place, relation):
    return tuple(1 - a if flip else a for a, flip in zip(place, relation))


def _scatter_beside(mats):
    hbm = pltpu.MemorySpace.HBM
    src_refs = [jax.new_ref(a, memory_space=hbm) for a in mats]
    land_refs = [jax.empty_ref(jax.ShapeDtypeStruct((7, a.shape[0] // 8, a.shape[1]), a.dtype), memory_space=hbm)
                 for a in mats]

    def launch(send_sems, recv_sems):
        me = (lax.axis_index("x"), lax.axis_index("y"), lax.axis_index("c"))
        peers = [_related(me, rel) for rel in _RELATIONS]
        barrier = pltpu.get_barrier_semaphore()
        for peer in peers:
            pl.semaphore_signal(barrier, inc=1, device_id=peer, device_id_type=MESH)
        pl.semaphore_wait(barrier, len(peers))
        copies = []
        for i, (src, land) in enumerate(zip(src_refs, land_refs)):
            hr = land.shape[1]
            for k, (tx, ty, tc) in enumerate(peers):
                rows = pl.ds((2 * tx + ty) * 2 * hr + tc * hr, hr)
                copies.append(pltpu.make_async_remote_copy(
                    src_ref=src.at[rows, :], dst_ref=land.at[k], send_sem=send_sems.at[7 * i + k],
                    recv_sem=recv_sems.at[7 * i + k], device_id=(tx, ty, tc), device_id_type=MESH))
                copies[-1].start()
        for cp in copies:
            cp.wait()

    n_sems = 7 * len(mats)
    pl.kernel(launch, mesh=plsc.ScalarSubcoreMesh(axis_name="sequencer", num_cores=1), name="scatter_beside",
              scratch_types=[pltpu.SemaphoreType.DMA((n_sems,)), pltpu.SemaphoreType.DMA((n_sems,))],
              compiler_params=pltpu.CompilerParams(collective_id=2))()
    return [ref[...] for ref in land_refs]


def _broadcast_beside(arrays):
    hbm = pltpu.MemorySpace.HBM
    src_refs = [jax.new_ref(a, memory_space=hbm) for a in arrays]
    land_refs = [jax.empty_ref(jax.ShapeDtypeStruct((len(_RELATIONS),) + a.shape, a.dtype), memory_space=hbm)
                 for a in arrays]

    def launch(send_sems, recv_sems):
        me = (lax.axis_index("x"), lax.axis_index("y"), lax.axis_index("c"))
        peers = [_related(me, rel) for rel in _RELATIONS]
        barrier = pltpu.get_barrier_semaphore()
        for peer in peers:
            pl.semaphore_signal(barrier, inc=1, device_id=peer, device_id_type=MESH)
        pl.semaphore_wait(barrier, len(peers))
        copies = []
        for i, (src, land) in enumerate(zip(src_refs, land_refs)):
            for k, peer in enumerate(peers):
                copies.append(pltpu.make_async_remote_copy(
                    src_ref=src, dst_ref=land.at[k], send_sem=send_sems.at[7 * i + k],
                    recv_sem=recv_sems.at[7 * i + k], device_id=peer, device_id_type=MESH))
                copies[-1].start()
        for cp in copies:
            cp.wait()

    n_sems = 7 * len(arrays)
    pl.kernel(launch, mesh=plsc.ScalarSubcoreMesh(axis_name="sequencer", num_cores=1), name="broadcast_beside",
              scratch_types=[pltpu.SemaphoreType.DMA((n_sems,)), pltpu.SemaphoreType.DMA((n_sems,))],
              compiler_params=pltpu.CompilerParams(collective_id=3))()
    return [ref[...] for ref in land_refs]


def _exchange_grads(big, small, landed, landed_small):
    n_t = len(big)
    n_g = len(_SMALL_GROUPS)
    names = _SMALL_ORDER
    halves = [(b.shape[0] // N_CHIPS // 2, b.shape[1]) for b in big]
    early = sorted(landed)
    late = [i for i in range(n_t) if i not in landed]
    n_sems = 4 * n_g + 7 * len(late) + n_t
    small_sem0, block_sem0 = n_t, n_t + len(names)
    early_sem0 = block_sem0 + N_CHIPS * len(late)
    landed_sem0 = early_sem0 + 2 * len(early)
    early_small = [n for n in names if n in landed_small]

    def body(*refs):
        pos = 0

        def take(n):
            nonlocal pos
            pos += n
            return refs[pos - n:pos]

        big_refs, small_refs = take(n_t), dict(zip(names, take(len(names))))
        land_refs = dict(zip(early, take(len(early))))
        land_small_refs = dict(zip(early_small, take(len(early_small))))
        out_refs, small_out_refs = take(n_t), dict(zip(names, take(len(names))))
        per_late = lambda: dict(zip(late, take(len(late))))
        ga, gb, pme, send_b, recv_b = per_late(), per_late(), take(n_t), per_late(), per_late()
        own_e, land_e = dict(zip(early, take(len(early)))), dict(zip(early, take(len(early))))
        land_s = dict(zip(early_small, take(len(early_small))))
        s_own, s_sib, s_chips, s_pair = take(n_g), take(n_g), take(n_g), take(n_g)
        stage = dict(zip(names, take(len(names))))
        send_sems, recv_sems, local_sems = take(3)
        x, y, c, other_chips = _mesh_place()
        me = 2 * x + y
        sibling = (x, y, 1 - c)
        sem_at = iter(range(n_sems))

        def remote(src, dst, to):
            k = next(sem_at)
            return pltpu.make_async_remote_copy(src_ref=src, dst_ref=dst, send_sem=send_sems.at[k],
                                                recv_sem=recv_sems.at[k], device_id=to, device_id_type=MESH)

        loads = [pltpu.make_async_copy(small_refs[name], stage[name], local_sems.at[small_sem0 + a])
                 for a, name in enumerate(names)]
        landed_loads = [pltpu.make_async_copy(land_small_refs[name], land_s[name], local_sems.at[landed_sem0 + a])
                        for a, name in enumerate(early_small)]
        for cp in loads + landed_loads:
            cp.start()
        for cp in loads:
            cp.wait()
        small_swaps = []
        for gi, (_, _, members) in enumerate(_SMALL_GROUPS):
            s_own[gi][...] = jnp.zeros_like(s_own[gi])
            for name, r0 in members:
                r, n = _small_shape(name)
                s_own[gi][r0:r0 + r, 0:n] = stage[name][...]
            small_swaps.append(remote(s_own[gi], s_sib[gi], sibling))
            small_swaps[gi].start()
        order = sorted(late, key=lambda i: halves[i][0] * halves[i][1])
        own_loads, big_swaps = {}, {}
        for i in order:
            hr = halves[i][0]
            own_loads[i], big_swaps[i] = [], []
            for j in range(N_CHIPS):
                mine = big_refs[i].at[pl.ds(j * 2 * hr + c * hr, hr), :]
                theirs = big_refs[i].at[pl.ds(j * 2 * hr + (1 - c) * hr, hr), :]
                sem = local_sems.at[block_sem0 + N_CHIPS * late.index(i) + j]
                own_loads[i].append(pltpu.make_async_copy(mine, ga[i].at[j], sem))
                own_loads[i][j].start()
                big_swaps[i].append(remote(theirs, gb[i].at[j], sibling))
                big_swaps[i][j].start()
        early_loads = {}
        for e, i in enumerate(early):
            hr = halves[i][0]
            mine = big_refs[i].at[pl.ds(me * 2 * hr + c * hr, hr), :]
            early_loads[i] = [pltpu.make_async_copy(mine, own_e[i], local_sems.at[early_sem0 + 2 * e]),
                              pltpu.make_async_copy(land_refs[i], land_e[i], local_sems.at[early_sem0 + 2 * e + 1])]
            for cp in early_loads[i]:
                cp.start()
        small_sends = []
        for gi in range(n_g):
            small_swaps[gi].wait_recv()
            s_pair[gi][...] = s_own[gi][...] + s_sib[gi][...]
            small_sends.append([remote(s_pair[gi], s_chips[gi].at[k], (*chip, c)) for k, chip in enumerate(other_chips)])
            for cp in small_sends[gi]:
                cp.start()

        def pair_sum(i, j):
            return ga[i][j] + gb[i][j]

        big_sends = {}
        for i in order:
            for j in range(N_CHIPS):
                own_loads[i][j].wait()
                big_swaps[i][j].wait_recv()
            big_sends[i] = []
            for k, chip in enumerate(other_chips):
                send_b[i][k] = pair_sum(i, 2 * chip[0] + chip[1]).astype(BF16)
                big_sends[i].append(remote(send_b[i].at[k], recv_b[i].at[k], (*chip, c)))
                big_sends[i][k].start()
        last_swaps, keeps = {}, {}
        for i in early + order:
            hr = halves[i][0]
            if i in landed:
                for cp in early_loads[i]:
                    cp.wait()
                total = own_e[i][...]
                for k in range(len(_RELATIONS)):
                    total = total + land_e[i][k]
                pme[i][...] = total
            else:
                for k in range(3):
                    big_sends[i][k].wait_recv()
                pme[i][...] = ((pair_sum(i, me) + recv_b[i][0].astype(F32)) + recv_b[i][1].astype(F32)) + recv_b[i][2].astype(F32)
            mine = out_refs[i].at[pl.ds(c * hr, hr), :]
            keeps[i] = pltpu.make_async_copy(pme[i], mine, local_sems.at[i])
            keeps[i].start()
            last_swaps[i] = remote(pme[i], mine, sibling)
            last_swaps[i].start()

        for gi, (_, _, members) in enumerate(_SMALL_GROUPS):
            for k in range(3):
                small_sends[gi][k].wait_recv()
            total = None
            for j in range(N_CHIPS):
                rel = jnp.bitwise_xor(j, me)
                term = jnp.where(rel == 0, s_pair[gi][...], jnp.where(
                    rel == 2, s_chips[gi][0], jnp.where(rel == 1, s_chips[gi][1], s_chips[gi][2])))
                total = term if total is None else total + term
            s_sib[gi][...] = total
            for name, r0 in members:
                r, n = _small_shape(name)
                stage[name][...] = s_sib[gi][r0:r0 + r, 0:n]
        my_index = 4 * x + 2 * y + c
        for cp in landed_loads:
            cp.wait()
        for name in early_small:
            total = None
            for d in range(2 * N_CHIPS):
                rel = jnp.bitwise_xor(d, my_index)
                term = stage[name][...]
                for k in range(len(_RELATIONS)):
                    term = jnp.where(rel == k + 1, land_s[name][k], term)
                total = term if total is None else total + term
            stage[name][...] = total
        stores = [pltpu.make_async_copy(stage[name], small_out_refs[name], local_sems.at[small_sem0 + a])
                  for a, name in enumerate(names)]
        for cp in stores:
            cp.start()

        for i in range(n_t):
            last_swaps[i].wait_recv()
            keeps[i].wait()
        for cp in stores:
            cp.wait()
        groups = list(big_swaps.values()) + small_sends + list(big_sends.values())
        for cp in small_swaps + [cp for group in groups for cp in group] + list(last_swaps.values()):
            cp.wait_send()

    any_spec = pl.BlockSpec(memory_space=pl.ANY)
    small_shapes = [_sds(_small_shape(n)) for n in names]
    group_shapes = [shape for _, shape, _ in _SMALL_GROUPS]
    vmem = lambda which, dtype, lead=(): [pltpu.VMEM(lead + halves[i], dtype) for i in which]
    outs = _call(
        body, name="exchange_grads",
        in_specs=[any_spec] * (n_t + len(names) + len(early) + len(early_small)),
        out_specs=[any_spec] * (n_t + len(names)),
        out_shape=[_sds((b.shape[0] // N_CHIPS, b.shape[1])) for b in big] + small_shapes,
        scratch_shapes=(vmem(late, F32, (N_CHIPS,)) + vmem(late, F32, (N_CHIPS,)) + vmem(range(n_t), F32)
                        + vmem(late, BF16, (3,)) + vmem(late, BF16, (3,))
                        + vmem(early, F32) + vmem(early, F32, (len(_RELATIONS),))
                        + [pltpu.VMEM((len(_RELATIONS),) + _small_shape(n), F32) for n in early_small]
                        + [pltpu.VMEM(s, F32) for s in group_shapes] * 2 + [pltpu.VMEM((3,) + s, F32) for s in group_shapes]
                        + [pltpu.VMEM(s, F32) for s in group_shapes]
                        + [pltpu.VMEM(_small_shape(n), F32) for n in names]
                        + [pltpu.SemaphoreType.DMA((n_sems,)), pltpu.SemaphoreType.DMA((n_sems,)),
                           pltpu.SemaphoreType.DMA((landed_sem0 + len(early_small),))]),
        compiler_params=_params(48),
    )(*big, *[small[n] for n in names], *[landed[i] for i in early], *[landed_small[n] for n in early_small])
    return list(outs[:n_t]), dict(zip(names, outs[n_t:n_t + len(names)]))


def _adamw_update(w, g, m, v):
    m = ADAM_B1 * m + (1.0 - ADAM_B1) * g
    v = ADAM_B2 * v + (1.0 - ADAM_B2) * (g * g)
    m_hat = m / (1.0 - ADAM_B1 ** ADAM_STEP)
    v_hat = v / (1.0 - ADAM_B2 ** ADAM_STEP)
    return -ADAM_LR * (m_hat / (jnp.sqrt(v_hat) + ADAM_EPS) + ADAM_WD * w), m, v


def _adamw(w, g, m, v, grid, name):
    n_t = len(w)

    def body(*refs):
        ins, outs = refs[:4 * n_t], refs[4 * n_t:]
        for i in range(n_t):
            w_, g_, m_, v_ = [ins[a * n_t + i][...] for a in range(4)]
            vals = (g_,) + _adamw_update(w_, g_, m_, v_)
            for a in range(4):
                outs[a * n_t + i][...] = vals[a]

    specs = [pl.BlockSpec((a.shape[0] // grid, a.shape[1]), lambda i: (i, 0)) for a in w]
    shapes = [_sds(a.shape) for a in w]
    outs = _call(
        body, name=name, grid=(grid,), in_specs=specs * 4, out_specs=specs * 4, out_shape=shapes * 4,
        compiler_params=_params(40, ("arbitrary",)),
    )(*w, *g, *m, *v)
    return [outs[a * n_t:(a + 1) * n_t] for a in range(4)]


def kernel(x, p, pre_norm_g, w_in, ssm_lam_re, ssm_lam_im, ssm_log_step, ssm_b_re, ssm_b_im, ssm_c_re, ssm_c_im, ssm_d, ssm_w_glu, ssm_b_glu, attn_sinks, w_out, post_norm_g, pl_w_proj, pl_w_gate, pl_b_gate, loss_target, m_pre_norm_g, m_w_in, m_ssm_lam_re, m_ssm_lam_im, m_ssm_log_step, m_ssm_b_re, m_ssm_b_im, m_ssm_c_re, m_ssm_c_im, m_ssm_d, m_ssm_w_glu, m_ssm_b_glu, m_attn_sinks, m_w_out, m_post_norm_g, m_pl_w_proj, m_pl_w_gate, m_pl_b_gate, v_pre_norm_g, v_w_in, v_ssm_lam_re, v_ssm_lam_im, v_ssm_log_step, v_ssm_b_re, v_ssm_b_im, v_ssm_c_re, v_ssm_c_im, v_ssm_d, v_ssm_w_glu, v_ssm_b_glu, v_attn_sinks, v_w_out, v_post_norm_g, v_pl_w_proj, v_pl_w_gate, v_pl_b_gate):
    weights = dict(pre_norm_g=pre_norm_g, w_in=w_in, ssm_lam_re=ssm_lam_re, ssm_lam_im=ssm_lam_im,
                   ssm_log_step=ssm_log_step, ssm_b_re=ssm_b_re, ssm_b_im=ssm_b_im, ssm_c_re=ssm_c_re,
                   ssm_c_im=ssm_c_im, ssm_d=ssm_d, ssm_w_glu=ssm_w_glu, ssm_b_glu=ssm_b_glu, attn_sinks=attn_sinks,
                   w_out=w_out, post_norm_g=post_norm_g, pl_w_proj=pl_w_proj, pl_w_gate=pl_w_gate, pl_b_gate=pl_b_gate)
    m_in = dict(pre_norm_g=m_pre_norm_g, w_in=m_w_in, ssm_lam_re=m_ssm_lam_re, ssm_lam_im=m_ssm_lam_im,
                ssm_log_step=m_ssm_log_step, ssm_b_re=m_ssm_b_re, ssm_b_im=m_ssm_b_im, ssm_c_re=m_ssm_c_re,
                ssm_c_im=m_ssm_c_im, ssm_d=m_ssm_d, ssm_w_glu=m_ssm_w_glu, ssm_b_glu=m_ssm_b_glu,
                attn_sinks=m_attn_sinks, w_out=m_w_out, post_norm_g=m_post_norm_g, pl_w_proj=m_pl_w_proj,
                pl_w_gate=m_pl_w_gate, pl_b_gate=m_pl_b_gate)
    v_in = dict(pre_norm_g=v_pre_norm_g, w_in=v_w_in, ssm_lam_re=v_ssm_lam_re, ssm_lam_im=v_ssm_lam_im,
                ssm_log_step=v_ssm_log_step, ssm_b_re=v_ssm_b_re, ssm_b_im=v_ssm_b_im, ssm_c_re=v_ssm_c_re,
                ssm_c_im=v_ssm_c_im, ssm_d=v_ssm_d, ssm_w_glu=v_ssm_w_glu, ssm_b_glu=v_ssm_b_glu,
                attn_sinks=v_attn_sinks, w_out=v_w_out, post_norm_g=v_post_norm_g, pl_w_proj=v_pl_w_proj,
                pl_w_gate=v_pl_w_gate, pl_b_gate=v_pl_b_gate)

    def two_d(tree):
        return {k: _to_kernel_form(k, a) for k, a in tree.items()}

    w2, m2, v2 = two_d(weights), two_d(m_in), two_d(v_in)

    (w_in_full,), gathered = _gather_weights([w2["w_in"].astype(BF16)])
    rest = _gather_weights_beside([(w2[n] + gathered).astype(BF16) for n in _BIG[1:]])
    full = dict(zip(_BIG, [w_in_full] + rest))
    s5_params = tuple(w2[n] for n in ("ssm_lam_re", "ssm_lam_im", "ssm_log_step", "ssm_b_re", "ssm_b_im", "ssm_c_re",
                                      "ssm_c_im"))
    grad_x, loss, grads = _local_step(
        x, p, loss_target, w2["pre_norm_g"], full["w_in"], s5_params, w2["ssm_d"], full["ssm_w_glu"], w2["ssm_b_glu"],
        w2["attn_sinks"], full["w_out"], w2["post_norm_g"], full["pl_w_proj"], full["pl_w_gate"], w2["pl_b_gate"])

    sent_early = ("w_out", "pl_w_gate", "pl_w_proj")
    landed = dict(zip([_BIG.index(n) for n in sent_early], _scatter_beside([grads[n] for n in sent_early])))
    after_scatter = landed[_BIG.index(sent_early[-1])][0, 0, 0] * 0.0
    late_operands = [grads[_SMALL_EARLY[0]] + after_scatter] + [grads[n] for n in _SMALL_EARLY[1:]]
    landed_small = dict(zip(_SMALL_EARLY, _broadcast_beside(late_operands)))
    g_big, g_small = _exchange_grads([grads[n] for n in _BIG], {**{n: grads[n] for n in _SMALL}, "loss": loss}, landed,
                                     landed_small)
    g_big = dict(zip(_BIG, g_big))
    total_loss = g_small.pop("loss")

    big_out = _adamw([w2[n] for n in _BIG], [g_big[n] for n in _BIG], [m2[n] for n in _BIG], [v2[n] for n in _BIG],
                     8, "adamw_matrices")
    small_names = tuple(_SMALL)
    small_out = _adamw([w2[n] for n in small_names], [g_small[n] for n in small_names], [m2[n] for n in small_names],
                       [v2[n] for n in small_names], 1, "adamw_small")

    results = [{**dict(zip(_BIG, big_part)), **dict(zip(small_names, small_part))}
               for big_part, small_part in zip(big_out, small_out)]
    flat = [_from_kernel_form(name, r[name], weights[name].shape) for r in results for name in _WEIGHT_ORDER]
    return (total_loss.reshape(()), grad_x, *flat)
```

```python
import math

import jax
import jax.numpy as jnp
from jax import lax
from jax.experimental import pallas as pl
from jax.experimental.pallas import tpu as pltpu
from jax.experimental.pallas import tpu_sc as plsc

F32 = jnp.float32
BF16 = jnp.bfloat16

D_MODEL = 1024
D_SSM = 512
D_ATTN = 512
SSM_GROUPS = 32
SSM_GROUP_CH = 16
SSM_STATE = 64
SSM_LANES = SSM_GROUPS * SSM_STATE
HEAD_DIM = 64
N_HEADS = 8
KV_HEADS = 2
Q_PER_KV = 4
WINDOW = 128
BLOCK = 128
D_PLE = 256
D_IN = 2304
EPS = 1e-6
ATTN_SCALE = 1.0 / math.sqrt(HEAD_DIM)

ADAM_LR = 0.001
ADAM_B1 = 0.9
ADAM_B2 = 0.999
ADAM_EPS = 1e-08
ADAM_WD = 0.01
ADAM_STEP = 10

N_CHIPS = 4
LANES = 128
SCAN_CHUNKS = 8
SCAN_TILE_STEPS = 32
SCAN_LANE_CHUNK = 512
MIB = 2 ** 20
MESH = pl.DeviceIdType.MESH


def _dot(a, b):
    return jnp.dot(a, b, preferred_element_type=F32)


def _dot_nt(a, b):
    return lax.dot_general(a, b, (((1,), (1,)), ((), ())), preferred_element_type=F32)


def _dot_tn(a, b):
    return lax.dot_general(a, b, (((0,), (0,)), ((), ())), preferred_element_type=F32)


def _params(vmem_mib, semantics=None):
    kw = dict(vmem_limit_bytes=vmem_mib * MIB)
    if semantics is not None:
        kw["dimension_semantics"] = semantics
    return pltpu.CompilerParams(**kw)


def _full(shape):
    nd = len(shape)
    return pl.BlockSpec(shape, lambda *_: (0,) * nd, pipeline_mode=pl.Buffered(1))


def _rows(tm, width):
    return pl.BlockSpec((tm, width), lambda i: (i, 0))


def _sds(shape, dtype=F32):
    return pltpu.HBM(shape, dtype)


def _call(body, **kw):
    fn = pl.pallas_call(body, **kw)
    return lambda *args: fn(*[pltpu.with_memory_space_constraint(a, pltpu.HBM) for a in args])


def _silu(z):
    return z * jax.nn.sigmoid(z)


def _in_proj(x2d, g1, w_in_t, n_seq, seq):
    rows = x2d.shape[0]
    tm = 512
    slab, steps, _, _ = _scan_geometry(n_seq, seq)

    def body(x_ref, g_ref, w_ref, *out_refs):
        u_parts, (zs_ref, q_ref, k_ref, v_ref, za_ref) = out_refs[:_SCAN_PARTS], out_refs[_SCAN_PARTS:]
        x = x_ref[...]
        r = lax.rsqrt(jnp.mean(x * x, axis=-1, keepdims=True) + EPS)
        hn = (x * r * g_ref[...]).astype(BF16)

        whole = _dot_nt(hn, w_ref[...])

        def proj(a, b):
            return whole[:, a:b]

        _store_chunks(u_parts, pl.program_id(0) * (tm // steps), proj(0, 512), steps, slab)
        zs_ref[...] = proj(512, 1024)
        q_ref[...] = (proj(1024, 1536) * ATTN_SCALE).astype(BF16)
        k_ref[...] = proj(1536, 1664).astype(BF16)
        v_ref[...] = proj(1664, 1792).astype(BF16)
        za_ref[...] = proj(1792, 2304)

    *u_parts, zs, q, k, v, za = _call(
        body, name="in_proj", grid=(rows // tm,),
        in_specs=[_rows(tm, D_MODEL), _full((1, D_MODEL)), _full((D_IN, D_MODEL))],
        out_specs=_whole_parts(rows) + [_rows(tm, 512), _rows(tm, 512), _rows(tm, 128), _rows(tm, 128), _rows(tm, 512)],
        out_shape=_part_shapes(rows) + [_sds((rows, 512)), _sds((rows, 512), BF16), _sds((rows, 128), BF16),
                                        _sds((rows, 128), BF16), _sds((rows, 512))],
        compiler_params=_params(48, ("arbitrary",)),
    )(x2d, g1, w_in_t)
    return u_parts, zs, q, k, v, za


def _in_proj_bwd(x2d, dh1, g1, w_in_t, du_parts, dzs, dq, dk, dv, dza, n_seq, seq):
    rows = x2d.shape[0]
    tm = 512
    slab, steps, _, _ = _scan_geometry(n_seq, seq)

    def body(x_ref, dh1_ref, g_ref, w_ref, *refs):
        du_parts, (dzs_ref, dq_ref, dk_ref, dv_ref, dza_ref, gx_ref, dw_ref, dg_ref) = refs[:_SCAN_PARTS], refs[_SCAN_PARTS:]

        @pl.when(pl.program_id(0) == 0)
        def _():
            dw_ref[...] = jnp.zeros_like(dw_ref)
            dg_ref[...] = jnp.zeros_like(dg_ref)

        x = x_ref[...]
        g = g_ref[...]
        r = lax.rsqrt(jnp.mean(x * x, axis=-1, keepdims=True) + EPS)
        xr = x * r
        hn = (xr * g).astype(BF16)
        du = _load_chunks(du_parts, pl.program_id(0) * (tm // steps), tm // steps, steps, slab)
        d_proj = jnp.concatenate([du.astype(BF16), dzs_ref[...], dq_ref[...], dk_ref[...], dv_ref[...], dza_ref[...]],
                                 axis=1)
        dhn = _dot(d_proj, w_ref[...])
        dw_ref[...] += _dot_tn(d_proj, hn)
        dg_ref[...] += jnp.sum(dhn * xr, axis=0, keepdims=True)
        a_ = dhn * g
        gx_ref[...] = dh1_ref[...] + r * a_ - xr * (r * jnp.mean(a_ * xr, axis=-1, keepdims=True))

    return _call(
        body, name="in_proj_bwd", grid=(rows // tm,),
        in_specs=[_rows(tm, D_MODEL), _rows(tm, D_MODEL), _full((1, D_MODEL)), _full((D_IN, D_MODEL))]
        + _whole_parts(rows) + [_rows(tm, 512), _rows(tm, 512), _rows(tm, 128), _rows(tm, 128), _rows(tm, 512)],
        out_specs=[_rows(tm, D_MODEL), _full((D_IN, D_MODEL)), _full((1, D_MODEL))],
        out_shape=[_sds((rows, D_MODEL)), _sds((D_IN, D_MODEL)), _sds((1, D_MODEL))],
        compiler_params=_params(56, ("arbitrary",)),
    )(x2d, dh1, g1, w_in_t, *du_parts, dzs, dq, dk, dv, dza)


def _iota(shape, axis):
    return lax.broadcasted_iota(jnp.int32, shape, axis)


def _sum_of_thirds(f, a):
    hi = a.astype(BF16)
    rest = a - hi.astype(F32)
    mid = rest.astype(BF16)
    low = (rest - mid.astype(F32)).astype(BF16)
    return (f(hi) + f(mid)) + f(low)


@jax.custom_vjp
def _pick_rows(e, a):
    return _sum_of_thirds(lambda part: _dot(e, part), a)


def _pick_rows_fwd(e, a):
    return _pick_rows(e, a), e


def _pick_rows_bwd(e, ct):
    return jnp.zeros_like(e), _sum_of_thirds(lambda part: _dot_tn(e, part), ct)


_pick_rows.defvjp(_pick_rows_fwd, _pick_rows_bwd)


@jax.custom_vjp
def _pick_cols(a, e):
    return _sum_of_thirds(lambda part: _dot(part, e), a)


def _pick_cols_fwd(a, e):
    return _pick_cols(a, e), e


def _pick_cols_bwd(e, ct):
    return _sum_of_thirds(lambda part: _dot_nt(part, e), ct), jnp.zeros_like(e)


_pick_cols.defvjp(_pick_cols_fwd, _pick_cols_bwd)


_HALF_GROUPS = SSM_GROUPS // 2
_N_SHIFT = SSM_STATE.bit_length() - 1
_P_SHIFT = SSM_GROUP_CH.bit_length() - 1


def _s5_operands(lam_re, lam_im, log_step, b_re, b_im, c_re, c_im):
    g, n, p = SSM_GROUPS, SSM_STATE, SSM_GROUP_CH
    gn, gp, hn_, hp = g * n, g * p, _HALF_GROUPS * n, _HALF_GROUPS * p
    eye_g = _iota((g, g), 0) == _iota((g, g), 1)
    step = jnp.sum(jnp.where(eye_g, jnp.exp(log_step), 0.0), axis=1, keepdims=True)
    a_re = lam_re * step
    a_im = lam_im * step
    mag = jnp.exp(a_re)
    lbar_re = mag * jnp.cos(a_im)
    lbar_im = mag * jnp.sin(a_im)
    n_re = lbar_re - 1.0
    den = lam_re * lam_re + lam_im * lam_im
    f_re = (n_re * lam_re + lbar_im * lam_im) / den
    f_im = (lbar_im * lam_re - n_re * lam_im) / den

    spread_n = (_iota((n, gn), 0) == (_iota((n, gn), 1) & (n - 1))).astype(BF16)
    own_g = _iota((g, gn), 0) == (_iota((g, gn), 1) >> _N_SHIFT)

    def to_row(a):
        return jnp.sum(jnp.where(own_g, _pick_cols(a, spread_n), 0.0), axis=0, keepdims=True)

    per_group = ((_iota((gp, g), 0) >> _P_SHIFT) == _iota((gp, g), 1)).astype(BF16)
    fx_re, fx_im = _pick_rows(per_group, f_re), _pick_rows(per_group, f_im)
    bbar_re = fx_re * b_re - fx_im * b_im
    bbar_im = fx_re * b_im + fx_im * b_re

    tile_n = (_iota((n, hn_), 0) == (_iota((n, hn_), 1) & (n - 1))).astype(BF16)
    same_group = (_iota((hp, hn_), 0) >> _P_SHIFT) == (_iota((hp, hn_), 1) >> _N_SHIFT)

    def embed(a, hf):
        return jnp.where(same_group, _pick_cols(a[hf * hp:(hf + 1) * hp], tile_n), 0.0)

    return (to_row(lbar_re), to_row(lbar_im), embed(bbar_re, 0), embed(bbar_re, 1), embed(bbar_im, 0),
            embed(bbar_im, 1), embed(c_re, 0), embed(c_re, 1), embed(c_im, 0), embed(c_im, 1))


_S5_PARAM_SHAPES = ((SSM_GROUPS, SSM_STATE), (SSM_GROUPS, SSM_STATE), (1, SSM_GROUPS),
                    (D_SSM, SSM_STATE), (D_SSM, SSM_STATE), (D_SSM, SSM_STATE), (D_SSM, SSM_STATE))
_CM_SHAPE = (2, _HALF_GROUPS * SSM_GROUP_CH, _HALF_GROUPS * SSM_STATE)
_S5_OPERAND_SHAPES = ((1, SSM_LANES), (1, SSM_LANES), _CM_SHAPE, _CM_SHAPE, _CM_SHAPE, _CM_SHAPE)


def _s5_params_fwd(*params):
    def body(*refs):
        ins, (lre_ref, lim_ref, btre_ref, btim_ref, cmre_ref, cmim_ref) = refs[:7], refs[7:]
        vals = _s5_operands(*[r[...] for r in ins])
        lre_ref[...] = vals[0]
        lim_ref[...] = vals[1]
        for ref, pair in zip((btre_ref, btim_ref, cmre_ref, cmim_ref), (vals[2:4], vals[4:6], vals[6:8], vals[8:10])):
            ref[0] = pair[0].astype(BF16)
            ref[1] = pair[1].astype(BF16)

    dtypes = (F32, F32, BF16, BF16, BF16, BF16)
    return _call(
        body, name="s5_params_fwd",
        in_specs=[_full(s) for s in _S5_PARAM_SHAPES], out_specs=[_full(s) for s in _S5_OPERAND_SHAPES],
        out_shape=[_sds(s, d) for s, d in zip(_S5_OPERAND_SHAPES, dtypes)], compiler_params=_params(32),
    )(*params)


def _s5_params_bwd(params, cotangents):
    def body(*refs):
        ins, (dlre, dlim, dbtre, dbtim, dcmre, dcmim), outs = refs[:7], refs[7:13], refs[13:]
        _, vjp = jax.vjp(_s5_operands, *[r[...] for r in ins])
        cts = (dlre[...], dlim[...], dbtre[0], dbtre[1], dbtim[0], dbtim[1], dcmre[0], dcmre[1], dcmim[0], dcmim[1])
        for ref, val in zip(outs, vjp(cts)):
            ref[...] = val

    return _call(
        body, name="s5_params_bwd",
        in_specs=[_full(s) for s in _S5_PARAM_SHAPES + _S5_OPERAND_SHAPES],
        out_specs=[_full(s) for s in _S5_PARAM_SHAPES],
        out_shape=[_sds(s) for s in _S5_PARAM_SHAPES], compiler_params=_params(48),
    )(*params, *cotangents)


def _scan_geometry(n_seq, seq):
    slab = n_seq * SCAN_CHUNKS
    steps = seq // SCAN_CHUNKS
    tile_rows = slab * SCAN_TILE_STEPS
    n_tiles = steps // SCAN_TILE_STEPS
    return slab, steps, tile_rows, n_tiles


_SCAN_PARTS = D_SSM // LANES


def _whole_parts(rows):
    return [_full((rows, LANES))] * _SCAN_PARTS


def _part_shapes(rows):
    return [_sds((rows, LANES))] * _SCAN_PARTS


def _load_chunks(parts, first_chunk, n_chunks, steps, slab):
    return jnp.concatenate([
        jnp.concatenate([ref[pl.ds(first_chunk + q, steps, stride=slab), :] for ref in parts], axis=1)
        for q in range(n_chunks)], axis=0)


def _store_chunks(parts, first_chunk, value, steps, slab):
    for q in range(value.shape[0] // steps):
        for j, ref in enumerate(parts):
            ref[pl.ds(first_chunk + q, steps, stride=slab), :] = value[q * steps:(q + 1) * steps,
                                                                     j * LANES:(j + 1) * LANES]


def _join_parts(parts):
    return jnp.concatenate([ref[...] for ref in parts], axis=1)


def _split_parts(parts, value):
    for j, ref in enumerate(parts):
        ref[...] = value[:, j * LANES:(j + 1) * LANES]


def _complex_power(re, im, n):
    out = None
    while n:
        if n & 1:
            out = (re, im) if out is None else (out[0] * re - out[1] * im, out[0] * im + out[1] * re)
        n >>= 1
        if n:
            re, im = re * re - im * im, 2.0 * re * im
    return out


def _chunk_carry(sum_re, sum_im, carry_re, carry_im, a_re, a_im, n_seq, reverse):
    carry_re[...] = jnp.zeros_like(carry_re)
    carry_im[...] = jnp.zeros_like(carry_im)
    for s in range(n_seq):
        order = range(SCAN_CHUNKS - 2, -1, -1) if reverse else range(1, SCAN_CHUNKS)
        for c in order:
            r = s * SCAN_CHUNKS + c
            p = r + 1 if reverse else r - 1
            p_re, p_im = carry_re[p:p + 1, :], carry_im[p:p + 1, :]
            carry_re[r:r + 1, :] = a_re * p_re - a_im * p_im + sum_re[p:p + 1, :]
            carry_im[r:r + 1, :] = a_re * p_im + a_im * p_re + sum_im[p:p + 1, :]


def _s5_scan_fwd(u_parts, bt_re, bt_im, cm_re, cm_im, lbar_re, lbar_im, d_row, n_seq, seq):
    slab, steps, tile_rows, n_tiles = _scan_geometry(n_seq, seq)
    rows = u_parts[0].shape[0]

    def body(*refs):
        u_refs, refs = refs[:_SCAN_PARTS], refs[_SCAN_PARTS:]
        (bre_ref, bim_ref, cre_ref, cim_ref, lre_ref, lim_ref, d_ref), refs = refs[:7], refs[7:]
        y_refs, (hre_ref, him_ref, st_re, st_im, h0_re, h0_im, buf_re, buf_im) = refs[:_SCAN_PARTS], refs[_SCAN_PARTS:]
        second = pl.program_id(0) == 1
        i = pl.program_id(1)

        @pl.when(jnp.logical_and(i == 0, jnp.logical_not(second)))
        def _():
            st_re[...] = jnp.zeros_like(st_re)
            st_im[...] = jnp.zeros_like(st_im)

        u = _join_parts(u_refs)
        ub = u.astype(BF16)
        for hf in range(2):
            cols = slice(hf * 1024, (hf + 1) * 1024)
            buf_re[:, cols] = _dot(ub[:, hf * 256:(hf + 1) * 256], bre_ref[hf])
            buf_im[:, cols] = _dot(ub[:, hf * 256:(hf + 1) * 256], bim_ref[hf])

        for lc in range(SSM_LANES // SCAN_LANE_CHUNK):
            cols = slice(lc * SCAN_LANE_CHUNK, (lc + 1) * SCAN_LANE_CHUNK)
            l_re = jnp.broadcast_to(lre_ref[:, cols], (slab, SCAN_LANE_CHUNK))
            l_im = jnp.broadcast_to(lim_ref[:, cols], (slab, SCAN_LANE_CHUNK))

            def scan_tile(keep_states):
                def step(t, carry):
                    s_re, s_im = carry
                    r0 = pl.multiple_of(t * slab, slab)
                    n_re = l_re * s_re - l_im * s_im + buf_re[pl.ds(r0, slab), cols]
                    n_im = l_re * s_im + l_im * s_re + buf_im[pl.ds(r0, slab), cols]
                    if keep_states:
                        buf_re[pl.ds(r0, slab), cols] = n_re
                        buf_im[pl.ds(r0, slab), cols] = n_im
                    return n_re, n_im

                s_re, s_im = lax.fori_loop(0, SCAN_TILE_STEPS, step, (st_re[:, cols], st_im[:, cols]), unroll=True)
                st_re[:, cols] = s_re
                st_im[:, cols] = s_im

            pl.when(jnp.logical_not(second))(lambda: scan_tile(False))
            pl.when(second)(lambda: scan_tile(True))

        @pl.when(jnp.logical_and(i == n_tiles - 1, jnp.logical_not(second)))
        def _():
            a_re, a_im = _complex_power(lre_ref[...], lim_ref[...], steps)
            _chunk_carry(st_re, st_im, h0_re, h0_im, a_re, a_im, n_seq, reverse=False)
            st_re[...] = h0_re[...]
            st_im[...] = h0_im[...]

        @pl.when(second)
        def _():
            h_re = buf_re[...].astype(BF16)
            h_im = buf_im[...].astype(BF16)
            hre_ref[...] = h_re
            him_ref[...] = h_im
            for hf in range(2):
                cols = slice(hf * 1024, (hf + 1) * 1024)
                ycols = slice(hf * 256, (hf + 1) * 256)
                y_half = (_dot_nt(h_re[:, cols], cre_ref[hf]) - _dot_nt(h_im[:, cols], cim_ref[hf])
                          + d_ref[:, ycols] * u[:, ycols])
                _split_parts(y_refs[2 * hf:2 * hf + 2], y_half)

    tile = lambda w: pl.BlockSpec((tile_rows, w), lambda p, i: (i, 0))
    out_tile = lambda w: pl.BlockSpec((tile_rows, w), lambda p, i: (i * p, 0))
    cm = _full(_CM_SHAPE)
    outs = _call(
        body, name="s5_scan_fwd", grid=(2, n_tiles),
        in_specs=[tile(LANES)] * _SCAN_PARTS + [cm, cm, cm, cm, _full((1, SSM_LANES)), _full((1, SSM_LANES)),
                                                _full((1, 512))],
        out_specs=[out_tile(LANES)] * _SCAN_PARTS + [out_tile(SSM_LANES), out_tile(SSM_LANES)],
        out_shape=_part_shapes(rows) + [_sds((rows, SSM_LANES), BF16), _sds((rows, SSM_LANES), BF16)],
        scratch_shapes=[pltpu.VMEM((slab, SSM_LANES), F32)] * 4 + [pltpu.VMEM((tile_rows, SSM_LANES), F32)] * 2,
        compiler_params=_params(40, ("arbitrary", "arbitrary")),
    )(*u_parts, bt_re, bt_im, cm_re, cm_im, lbar_re, lbar_im, d_row)
    return outs[:_SCAN_PARTS], outs[_SCAN_PARTS], outs[_SCAN_PARTS + 1]


def _s5_scan_bwd(dy_parts, u_parts, h_re, h_im, bt_re, bt_im, cm_re, cm_im, lbar_re, lbar_im, d_row, n_seq, seq):
    slab, steps, tile_rows, n_tiles = _scan_geometry(n_seq, seq)
    rows = u_parts[0].shape[0]

    def body(*refs):
        dy_refs, u_refs, refs = refs[:_SCAN_PARTS], refs[_SCAN_PARTS:2 * _SCAN_PARTS], refs[2 * _SCAN_PARTS:]
        (hre_ref, him_ref, bre_ref, bim_ref, cre_ref, cim_ref, lre_ref, lim_ref, d_ref), refs = refs[:9], refs[9:]
        du_refs, refs = refs[:_SCAN_PARTS], refs[_SCAN_PARTS:]
        (dbre_ref, dbim_ref, dcre_ref, dcim_ref, dlre_ref, dlim_ref, dd_ref,
         st_re, st_im, g0_re, g0_im, acc_re, acc_im, buf_re, buf_im) = refs
        second = pl.program_id(0) == 1
        i = pl.program_id(1)

        @pl.when(jnp.logical_and(i == 0, jnp.logical_not(second)))
        def _():
            st_re[...] = jnp.zeros_like(st_re)
            st_im[...] = jnp.zeros_like(st_im)
            acc_re[...] = jnp.zeros_like(acc_re)
            acc_im[...] = jnp.zeros_like(acc_im)
            for ref in (dbre_ref, dbim_ref, dcre_ref, dcim_ref, dd_ref):
                ref[...] = jnp.zeros_like(ref)

        dy = _join_parts(dy_refs)
        dyb = dy.astype(BF16)
        for hf in range(2):
            cols = slice(hf * 1024, (hf + 1) * 1024)
            buf_re[:, cols] = _dot(dyb[:, hf * 256:(hf + 1) * 256], cre_ref[hf])
            buf_im[:, cols] = -_dot(dyb[:, hf * 256:(hf + 1) * 256], cim_ref[hf])

        for lc in range(SSM_LANES // SCAN_LANE_CHUNK):
            cols = slice(lc * SCAN_LANE_CHUNK, (lc + 1) * SCAN_LANE_CHUNK)
            l_re = jnp.broadcast_to(lre_ref[:, cols], (slab, SCAN_LANE_CHUNK))
            l_im = jnp.broadcast_to(lim_ref[:, cols], (slab, SCAN_LANE_CHUNK))

            def advance(r0, s_re, s_im):
                n_re = l_re * s_re + l_im * s_im + buf_re[pl.ds(r0, slab), cols]
                n_im = l_re * s_im - l_im * s_re + buf_im[pl.ds(r0, slab), cols]
                buf_re[pl.ds(r0, slab), cols] = n_re
                buf_im[pl.ds(r0, slab), cols] = n_im
                return n_re, n_im

            def row0(k):
                return pl.multiple_of((SCAN_TILE_STEPS - 1 - k) * slab, slab)

            @pl.when(jnp.logical_not(second))
            def _():
                s_re, s_im = lax.fori_loop(0, SCAN_TILE_STEPS, lambda k, s: advance(row0(k), *s),
                                           (st_re[:, cols], st_im[:, cols]), unroll=True)
                st_re[:, cols] = s_re
                st_im[:, cols] = s_im

            @pl.when(second)
            def _():
                def step(k, carry):
                    s_re, s_im, a_re, a_im = carry
                    r0 = row0(k)
                    hr = hre_ref[pl.ds(r0, slab), cols].astype(F32)
                    hi = him_ref[pl.ds(r0, slab), cols].astype(F32)
                    a_re = a_re + s_re * hr + s_im * hi
                    a_im = a_im + s_im * hr - s_re * hi
                    return advance(r0, s_re, s_im) + (a_re, a_im)

                zero = jnp.zeros((slab, SCAN_LANE_CHUNK), F32)
                s_re, s_im, a_re, a_im = lax.fori_loop(
                    0, SCAN_TILE_STEPS, step, (st_re[:, cols], st_im[:, cols], zero, zero), unroll=True)
                st_re[:, cols] = s_re
                st_im[:, cols] = s_im
                acc_re[:, cols] += a_re
                acc_im[:, cols] += a_im

        @pl.when(jnp.logical_and(i == n_tiles - 1, jnp.logical_not(second)))
        def _():
            p_re, p_im = _complex_power(lre_ref[...], lim_ref[...], steps)
            _chunk_carry(st_re, st_im, g0_re, g0_im, p_re, -p_im, n_seq, reverse=True)
            st_re[...] = g0_re[...]
            st_im[...] = g0_im[...]

        @pl.when(second)
        def _():
            u = _join_parts(u_refs)
            ub = u.astype(BF16)
            g_re = buf_re[...].astype(BF16)
            g_im = buf_im[...].astype(BF16)
            dd_ref[...] += jnp.sum(dy * u, axis=0, keepdims=True)
            for hf in range(2):
                cols = slice(hf * 1024, (hf + 1) * 1024)
                ycols = slice(hf * 256, (hf + 1) * 256)
                du_half = (_dot_nt(g_re[:, cols], bre_ref[hf]) + _dot_nt(g_im[:, cols], bim_ref[hf])
                           + d_ref[:, ycols] * dy[:, ycols])
                _split_parts(du_refs[2 * hf:2 * hf + 2], du_half)
                for q4 in range(_HALF_GROUPS // 4):
                    ch = slice(hf * 256 + q4 * 64, hf * 256 + (q4 + 1) * 64)
                    st = slice(hf * 1024 + q4 * 256, hf * 1024 + (q4 + 1) * 256)
                    blk = (hf, slice(q4 * 64, (q4 + 1) * 64), slice(q4 * 256, (q4 + 1) * 256))
                    dbre_ref[blk] += _dot_tn(ub[:, ch], g_re[:, st])
                    dbim_ref[blk] += _dot_tn(ub[:, ch], g_im[:, st])
                    dcre_ref[blk] += _dot_tn(dyb[:, ch], hre_ref[:, st])
                    dcim_ref[blk] -= _dot_tn(dyb[:, ch], him_ref[:, st])

        @pl.when(jnp.logical_and(i == n_tiles - 1, second))
        def _():
            dlre_ref[...] = jnp.sum(acc_re[...], axis=0, keepdims=True)
            dlim_ref[...] = jnp.sum(acc_im[...], axis=0, keepdims=True)

    tile = lambda w: pl.BlockSpec((tile_rows, w), lambda p, i: (n_tiles - 1 - i, 0))
    second_tile = lambda w: pl.BlockSpec((tile_rows, w), lambda p, i: (n_tiles - 1 - i * p, 0))
    cm = _full(_CM_SHAPE)
    row = _full((1, SSM_LANES))
    outs = _call(
        body, name="s5_scan_bwd", grid=(2, n_tiles),
        in_specs=[tile(LANES)] * _SCAN_PARTS + [second_tile(LANES)] * _SCAN_PARTS
        + [second_tile(SSM_LANES), second_tile(SSM_LANES), cm, cm, cm, cm, row, row, _full((1, 512))],
        out_specs=[second_tile(LANES)] * _SCAN_PARTS + [cm, cm, cm, cm, row, row, _full((1, 512))],
        out_shape=(_part_shapes(rows) + [_sds(_CM_SHAPE)] * 4 + [_sds((1, SSM_LANES))] * 2 + [_sds((1, 512))]),
        scratch_shapes=[pltpu.VMEM((slab, SSM_LANES), F32)] * 6 + [pltpu.VMEM((tile_rows, SSM_LANES), F32)] * 2,
        compiler_params=_params(48, ("arbitrary", "arbitrary")),
    )(*dy_parts, *u_parts, h_re, h_im, bt_re, bt_im, cm_re, cm_im, lbar_re, lbar_im, d_row)
    return (outs[:_SCAN_PARTS],) + tuple(outs[_SCAN_PARTS:])


def _glu_gate(gl, a, zs):
    return gl * jax.nn.sigmoid(a) * _silu(zs)


def _glu_fwd(y_parts, zs, w_glu, b_glu, n_seq, seq):
    rows = zs.shape[0]
    tm = 512
    slab, steps, _, _ = _scan_geometry(n_seq, seq)

    def body(*refs):
        y_refs, (zs_ref, w_ref, b_ref, o_ref) = refs[:_SCAN_PARTS], refs[_SCAN_PARTS:]
        y = _load_chunks(y_refs, pl.program_id(0) * (tm // steps), tm // steps, steps, slab)
        gl = jax.nn.gelu(y)
        a = _dot(gl.astype(BF16), w_ref[...]) + b_ref[...]
        o_ref[...] = _glu_gate(gl, a, zs_ref[...]).astype(BF16)

    return _call(
        body, name="glu_fwd", grid=(rows // tm,),
        in_specs=_whole_parts(rows) + [_rows(tm, 512), _full((512, 512)), _full((1, 512))],
        out_specs=_rows(tm, 512), out_shape=_sds((rows, 512), BF16),
        compiler_params=_params(32, ("arbitrary",)),
    )(*y_parts, zs, w_glu, b_glu)


def _glu_bwd(y_parts, zs, d_out, w_glu, b_glu, n_seq, seq):
    rows = zs.shape[0]
    tm = 512
    slab, steps, _, _ = _scan_geometry(n_seq, seq)

    def body(*refs):
        y_refs, (zs_ref, d_ref, w_ref, b_ref), refs = refs[:_SCAN_PARTS], refs[_SCAN_PARTS:_SCAN_PARTS + 4], refs[_SCAN_PARTS + 4:]
        dy_refs, (dzs_ref, dw_ref, db_ref) = refs[:_SCAN_PARTS], refs[_SCAN_PARTS:]
        first_chunk = pl.program_id(0) * (tm // steps)

        @pl.when(pl.program_id(0) == 0)
        def _():
            dw_ref[...] = jnp.zeros_like(dw_ref)
            db_ref[...] = jnp.zeros_like(db_ref)

        gl, gelu_vjp = jax.vjp(jax.nn.gelu, _load_chunks(y_refs, first_chunk, tm // steps, steps, slab))
        glb = gl.astype(BF16)
        a = _dot(glb, w_ref[...]) + b_ref[...]
        _, gate_vjp = jax.vjp(_glu_gate, gl, a, zs_ref[...])
        d_gl, d_a, d_zs = gate_vjp(d_ref[...])
        dab = d_a.astype(BF16)
        d_gl = d_gl + _dot_nt(dab, w_ref[...])
        _store_chunks(dy_refs, first_chunk, gelu_vjp(d_gl)[0], steps, slab)
        dzs_ref[...] = d_zs.astype(BF16)
        dw_ref[...] += _dot_tn(glb, dab)
        db_ref[...] += jnp.sum(d_a, axis=0, keepdims=True)

    *dy_parts, dzs, dw, db = _call(
        body, name="glu_bwd", grid=(rows // tm,),
        in_specs=_whole_parts(rows) + [_rows(tm, 512), _rows(tm, 512), _full((512, 512)), _full((1, 512))],
        out_specs=_whole_parts(rows) + [_rows(tm, 512), _full((512, 512)), _full((1, 512))],
        out_shape=_part_shapes(rows) + [_sds((rows, 512), BF16), _sds((512, 512)), _sds((1, 512))],
        compiler_params=_params(40, ("arbitrary",)),
    )(*y_parts, zs, d_out, w_glu, b_glu)
    return dy_parts, dzs, dw, db


_GROUP_ROWS = Q_PER_KV * BLOCK
_BLOCK_SHIFT = BLOCK.bit_length() - 1


def _attn_bias(j):
    row = _iota((_GROUP_ROWS, BLOCK), 0)
    dist_cur = (row & (BLOCK - 1)) - _iota((_GROUP_ROWS, BLOCK), 1)
    dist_prev = dist_cur + BLOCK
    head = row >> _BLOCK_SHIFT
    slope = jnp.zeros((_GROUP_ROWS, BLOCK), F32)
    for g in range(Q_PER_KV):
        slope = jnp.where(head == g, 2.0 ** (-(j * Q_PER_KV + g + 1)), slope)
    bias_cur = jnp.where(dist_cur >= 0, -slope * dist_cur.astype(F32), -jnp.inf)
    bias_prev = jnp.where(dist_prev < WINDOW, -slope * dist_prev.astype(F32), -jnp.inf)
    return bias_cur, bias_prev


_ATTN_BIAS_SCRATCH = pltpu.VMEM((KV_HEADS, 2, _GROUP_ROWS, BLOCK), F32)


def _fill_attn_bias(bias_ref):
    @pl.when(jnp.logical_and(pl.program_id(0) == 0, pl.program_id(1) == 0))
    def _():
        for j in range(KV_HEADS):
            bias_ref[j, 0], bias_ref[j, 1] = _attn_bias(j)


def _stack_heads(x, j):
    heads = range(j * Q_PER_KV, (j + 1) * Q_PER_KV)
    return jnp.concatenate([x[:, h * HEAD_DIM:(h + 1) * HEAD_DIM] for h in heads], axis=0)


def _stack_columns(x, j):
    heads = range(j * Q_PER_KV, (j + 1) * Q_PER_KV)
    return jnp.concatenate([jnp.broadcast_to(x[:, h:h + 1], (BLOCK, 1)) for h in heads], axis=0)


def _attn_fwd(q, k, v, za, sinks, n_seq, seq):
    nb = seq // BLOCK
    rows = q.shape[0]

    def body(q_ref, kc_ref, kp_ref, vc_ref, vp_ref, za_ref, sk_ref, o_ref, ao_ref, lse_ref, bias_ref):
        _fill_attn_bias(bias_ref)
        has_prev = pl.program_id(1) > 0
        q_all = q_ref[...]
        for j in range(KV_HEADS):
            js = slice(j * HEAD_DIM, (j + 1) * HEAD_DIM)
            bias_c, bias_p = bias_ref[j, 0], bias_ref[j, 1]
            q4 = _stack_heads(q_all, j)
            sc = _dot_nt(q4, kc_ref[:, js]) + bias_c
            sp = _dot_nt(q4, kp_ref[:, js]) + jnp.where(has_prev, bias_p, -jnp.inf)
            sink = _stack_columns(sk_ref[...], j)
            m = jnp.maximum(jnp.max(jnp.maximum(sc, sp), axis=-1, keepdims=True), sink)
            ec = jnp.exp(sc - m)
            ep = jnp.exp(sp - m)
            den = jnp.sum(ec + ep, axis=-1, keepdims=True) + jnp.exp(sink - m)
            inv = 1.0 / den
            o4 = _dot((ec * inv).astype(BF16), vc_ref[:, js]) + _dot((ep * inv).astype(BF16), vp_ref[:, js])
            lse4 = m + jnp.log(den)
            for g in range(Q_PER_KV):
                h = j * Q_PER_KV + g
                o_ref[:, h * HEAD_DIM:(h + 1) * HEAD_DIM] = o4[g * BLOCK:(g + 1) * BLOCK]
                lse_ref[:, h:h + 1] = lse4[g * BLOCK:(g + 1) * BLOCK]
        ao_ref[...] = (o_ref[...] * _silu(za_ref[...])).astype(BF16)

    cur = lambda w: pl.BlockSpec((BLOCK, w), lambda b, n: (b * nb + n, 0))
    prev = lambda w: pl.BlockSpec((BLOCK, w), lambda b, n: (b * nb + jnp.maximum(n - 1, 0), 0))
    return _call(
        body, name="attn_fwd", grid=(n_seq, nb),
        in_specs=[cur(512), cur(128), prev(128), cur(128), prev(128), cur(512), _full((1, N_HEADS))],
        out_specs=[cur(512), cur(512), cur(N_HEADS)],
        out_shape=[_sds((rows, 512)), _sds((rows, 512), BF16), _sds((rows, N_HEADS))],
        scratch_shapes=[_ATTN_BIAS_SCRATCH], compiler_params=_params(32, ("arbitrary", "arbitrary")),
    )(q, k, k, v, v, za, sinks)


def _attn_bwd(q, k, v, za, o, lse, d_ao, sinks, n_seq, seq):
    nb = seq // BLOCK
    rows = q.shape[0]

    def body(q_ref, kc_ref, kp_ref, vc_ref, vp_ref, za_ref, o_ref, lse_ref, d_ref, sk_ref,
             dq_ref, dk_ref, dv_ref, dza_ref, dsk_ref, bias_ref, dk_carry, dv_carry):
        n = nb - 1 - pl.program_id(1)
        _fill_attn_bias(bias_ref)

        @pl.when(jnp.logical_and(pl.program_id(0) == 0, pl.program_id(1) == 0))
        def _():
            dsk_ref[...] = jnp.zeros_like(dsk_ref)
            dk_carry[...] = jnp.zeros_like(dk_carry)
            dv_carry[...] = jnp.zeros_like(dv_carry)

        has_prev = n > 0
        has_next = n + 1 < nb

        _, gate_vjp = jax.vjp(lambda o_, z_: o_ * _silu(z_), o_ref[...], za_ref[...])
        d_o, d_za = gate_vjp(d_ref[...])
        dza_ref[...] = d_za.astype(BF16)
        q_all = q_ref[...]
        lse_all = lse_ref[...]

        for j in range(KV_HEADS):
            js = slice(j * HEAD_DIM, (j + 1) * HEAD_DIM)
            kc, kp, vc, vp = kc_ref[:, js], kp_ref[:, js], vc_ref[:, js], vp_ref[:, js]
            bias_c, bias_p = bias_ref[j, 0], bias_ref[j, 1]
            q4 = _stack_heads(q_all, j)
            do4b = _stack_heads(d_o, j).astype(BF16)
            lse4 = _stack_columns(lse_all, j)
            pc = jnp.exp(_dot_nt(q4, kc) + bias_c - lse4)
            pp = jnp.exp(_dot_nt(q4, kp) + jnp.where(has_prev, bias_p, -jnp.inf) - lse4)
            dpc = _dot_nt(do4b, vc)
            dpp = _dot_nt(do4b, vp)
            delta = jnp.sum(pc * dpc + pp * dpp, axis=-1, keepdims=True)
            dsc = (pc * (dpc - delta)).astype(BF16)
            dsp = (pp * (dpp - delta)).astype(BF16)
            dq4 = ((_dot(dsc, kc) + _dot(dsp, kp)) * ATTN_SCALE).astype(BF16)
            sink_loss = jnp.exp(_stack_columns(sk_ref[...], j) - lse4) * delta
            for g in range(Q_PER_KV):
                h = j * Q_PER_KV + g
                dq_ref[:, h * HEAD_DIM:(h + 1) * HEAD_DIM] = dq4[g * BLOCK:(g + 1) * BLOCK]
                dsk_ref[0:1, h:h + 1] -= jnp.sum(sink_loss[g * BLOCK:(g + 1) * BLOCK], axis=0, keepdims=True)
            dk = _dot_tn(dsc, q4) + jnp.where(has_next, dk_carry[j], 0.0)
            dv = _dot_tn(pc.astype(BF16), do4b) + jnp.where(has_next, dv_carry[j], 0.0)
            dk_carry[j] = _dot_tn(dsp, q4)
            dv_carry[j] = _dot_tn(pp.astype(BF16), do4b)
            dk_ref[:, js] = dk.astype(BF16)
            dv_ref[:, js] = dv.astype(BF16)

    cur = lambda w: pl.BlockSpec((BLOCK, w), lambda b, s: (b * nb + nb - 1 - s, 0))
    prev = lambda w: pl.BlockSpec((BLOCK, w), lambda b, s: (b * nb + jnp.maximum(nb - 2 - s, 0), 0))
    return _call(
        body, name="attn_bwd", grid=(n_seq, nb),
        in_specs=[cur(512), cur(128), prev(128), cur(128), prev(128), cur(512), cur(512), cur(N_HEADS), cur(512),
                  _full((1, N_HEADS))],
        out_specs=[cur(512), cur(128), cur(128), cur(512), _full((1, N_HEADS))],
        out_shape=[_sds((rows, 512), BF16), _sds((rows, 128), BF16), _sds((rows, 128), BF16),
                   _sds((rows, 512), BF16), _sds((1, N_HEADS))],
        scratch_shapes=[_ATTN_BIAS_SCRATCH, pltpu.VMEM((KV_HEADS, BLOCK, HEAD_DIM), F32),
                        pltpu.VMEM((KV_HEADS, BLOCK, HEAD_DIM), F32)],
        compiler_params=_params(32, ("arbitrary", "arbitrary")),
    )(q, k, k, v, v, za, o, lse, d_ao, sinks)


def _tail(ssm_out, attn_out, x2d, p2d, target, w_out, g2, w_gate, b_gate, w_proj):
    rows = x2d.shape[0]
    tm = 512

    def body(so_ref, ao_ref, x_ref, p_ref, t_ref, wo_ref, g2_ref, wg_ref, bg_ref, wp_ref,
             dh1_ref, dso_ref, dao_ref, dwo_ref, dwg_ref, dwp_ref, dbg_ref, dg2_ref, loss_ref):
        @pl.when(pl.program_id(0) == 0)
        def _():
            for ref in (dwo_ref, dwg_ref, dwp_ref, dbg_ref, dg2_ref, loss_ref):
                ref[...] = jnp.zeros_like(ref)

        cat = jnp.concatenate([so_ref[...], ao_ref[...]], axis=1)
        g2 = g2_ref[...]
        mixed = _dot(cat, wo_ref[...])
        r = lax.rsqrt(jnp.mean(mixed * mixed, axis=-1, keepdims=True) + EPS)
        mr = mixed * r
        h1 = x_ref[...] + mr * g2
        h1b = h1.astype(BF16)
        gate = jax.nn.sigmoid(_dot(h1b, wg_ref[...]) + bg_ref[...])
        pb = p_ref[...].astype(BF16)
        wp_blocks = [slice(j * D_PLE, (j + 1) * D_PLE) for j in range(N_CHIPS)]
        pp = jnp.concatenate([_dot(pb, wp_ref[blk, :]) for blk in wp_blocks], axis=1)
        err = h1 + gate * pp - t_ref[...]
        loss_ref[...] += 0.5 * jnp.sum(jnp.mean(err * err, axis=-1, keepdims=True), axis=0, keepdims=True)

        dh2 = err * (1.0 / D_MODEL)
        d_glin = dh2 * pp * gate * (1.0 - gate)
        d_glin_b = d_glin.astype(BF16)
        dwg_ref[...] += _dot_tn(h1b, d_glin_b)
        dbg_ref[...] += jnp.sum(d_glin, axis=0, keepdims=True)
        d_pp = (dh2 * gate).astype(BF16)
        for blk in wp_blocks:
            dwp_ref[blk, :] += _dot_tn(pb, d_pp[:, blk])
        dh1 = dh2 + _dot_nt(d_glin_b, wg_ref[...])
        dh1_ref[...] = dh1
        dg2_ref[...] += jnp.sum(dh1 * mr, axis=0, keepdims=True)
        a_ = dh1 * g2
        d_mixed = (r * a_ - mr * (r * jnp.mean(a_ * mr, axis=-1, keepdims=True))).astype(BF16)
        dwo_ref[...] += _dot_tn(cat, d_mixed)
        d_cat = _dot_nt(d_mixed, wo_ref[...])
        dso_ref[...] = d_cat[:, 0:512]
        dao_ref[...] = d_cat[:, 512:1024]

    return _call(
        body, name="tail_fwd_bwd", grid=(rows // tm,),
        in_specs=[_rows(tm, 512), _rows(tm, 512), _rows(tm, D_MODEL), _rows(tm, D_PLE), _rows(tm, D_MODEL),
                  _full((D_MODEL, D_MODEL)), _full((1, D_MODEL)), _full((D_MODEL, D_MODEL)), _full((1, D_MODEL)),
                  _full((N_CHIPS * D_PLE, D_PLE))],
        out_specs=[_rows(tm, D_MODEL), _rows(tm, 512), _rows(tm, 512), _full((D_MODEL, D_MODEL)),
                   _full((D_MODEL, D_MODEL)), _full((N_CHIPS * D_PLE, D_PLE)), _full((1, D_MODEL)), _full((1, D_MODEL)),
                   _full((1, 1))],
        out_shape=[_sds((rows, D_MODEL)), _sds((rows, 512)), _sds((rows, 512)), _sds((D_MODEL, D_MODEL)),
                   _sds((D_MODEL, D_MODEL)), _sds((N_CHIPS * D_PLE, D_PLE)), _sds((1, D_MODEL)), _sds((1, D_MODEL)),
                   _sds((1, 1))],
        compiler_params=_params(52, ("arbitrary",)),
    )(ssm_out, attn_out, x2d, p2d, target, w_out, g2, w_gate, b_gate, w_proj)


def _local_step(x, p, target, pre_norm_g, w_in_t, s5_params, ssm_d, w_glu, b_glu, sinks, w_out, post_norm_g, w_proj,
                w_gate, b_gate):
    n_seq, seq, _ = x.shape
    rows = n_seq * seq
    x2d = x.reshape(rows, D_MODEL)
    p2d = p.reshape(rows, D_PLE)
    t2d = target.reshape(rows, D_MODEL)

    l_re, l_im, bt_re, bt_im, cm_re, cm_im = _s5_params_fwd(*s5_params)

    u_scan, zs, q, k, v, za = _in_proj(x2d, pre_norm_g, w_in_t, n_seq, seq)
    y_scan, h_re, h_im = _s5_scan_fwd(u_scan, bt_re, bt_im, cm_re, cm_im, l_re, l_im, ssm_d, n_seq, seq)
    ssm_out = _glu_fwd(y_scan, zs, w_glu, b_glu, n_seq, seq)
    o, attn_out, lse = _attn_fwd(q, k, v, za, sinks, n_seq, seq)

    dh1, d_so, d_ao, d_w_out, d_w_gate, d_w_proj, d_b_gate, d_g2, loss = _tail(
        ssm_out, attn_out, x2d, p2d, t2d, w_out, post_norm_g, w_gate, b_gate, w_proj)

    dq, dk, dv, dza, d_sinks = _attn_bwd(q, k, v, za, o, lse, d_ao, sinks, n_seq, seq)
    dy_scan, dzs, d_w_glu, d_b_glu = _glu_bwd(y_scan, zs, d_so, w_glu, b_glu, n_seq, seq)
    du_scan, d_bt_re, d_bt_im, d_cm_re, d_cm_im, d_l_re, d_l_im, d_d = _s5_scan_bwd(
        dy_scan, u_scan, h_re, h_im, bt_re, bt_im, cm_re, cm_im, l_re, l_im, ssm_d, n_seq, seq)
    d_lam_re, d_lam_im, d_log_step, d_b_re, d_b_im, d_c_re, d_c_im = _s5_params_bwd(
        s5_params, (d_l_re, d_l_im, d_bt_re, d_bt_im, d_cm_re, d_cm_im))

    grad_x, d_w_in_t, d_g1 = _in_proj_bwd(x2d, dh1, pre_norm_g, w_in_t, du_scan, dzs, dq, dk, dv, dza, n_seq, seq)
    grads = dict(
        pre_norm_g=d_g1, w_in=d_w_in_t, ssm_lam_re=d_lam_re, ssm_lam_im=d_lam_im, ssm_log_step=d_log_step,
        ssm_b_re=d_b_re, ssm_b_im=d_b_im, ssm_c_re=d_c_re, ssm_c_im=d_c_im, ssm_d=d_d, ssm_w_glu=d_w_glu,
        ssm_b_glu=d_b_glu, attn_sinks=d_sinks, w_out=d_w_out, post_norm_g=d_g2, pl_w_proj=d_w_proj,
        pl_w_gate=d_w_gate, pl_b_gate=d_b_gate)
    return grad_x.reshape(x.shape), loss, grads


_BIG = ("w_in", "ssm_w_glu", "w_out", "pl_w_proj", "pl_w_gate")
_BIG_SHARD = {"w_in": (D_IN // N_CHIPS, D_MODEL), "ssm_w_glu": (D_SSM // N_CHIPS, D_SSM),
              "w_out": (D_MODEL // N_CHIPS, D_MODEL), "pl_w_proj": (D_PLE, D_MODEL // N_CHIPS),
              "pl_w_gate": (D_MODEL // N_CHIPS, D_MODEL)}
_SMALL = {"pre_norm_g": (1, D_MODEL), "ssm_lam_re": (SSM_GROUPS, SSM_STATE), "ssm_lam_im": (SSM_GROUPS, SSM_STATE),
          "ssm_log_step": (1, SSM_GROUPS), "ssm_b_re": (D_SSM, SSM_STATE), "ssm_b_im": (D_SSM, SSM_STATE),
          "ssm_c_re": (D_SSM, SSM_STATE), "ssm_c_im": (D_SSM, SSM_STATE), "ssm_d": (1, D_SSM), "ssm_b_glu": (1, D_SSM),
          "attn_sinks": (1, N_HEADS), "post_norm_g": (1, D_MODEL), "pl_b_gate": (1, D_MODEL)}
_VEC_ROWS = ("pre_norm_g", "post_norm_g", "pl_b_gate", "ssm_d", "ssm_b_glu", "attn_sinks", "ssm_log_step", "loss")
_SMALL_GROUPS = (
    ("vec", (8, D_MODEL), tuple((name, r) for r, name in enumerate(_VEC_ROWS))),
    ("lam", (2 * SSM_GROUPS, SSM_STATE), (("ssm_lam_re", 0), ("ssm_lam_im", SSM_GROUPS))),
)
_SMALL_EARLY = ("ssm_b_re", "ssm_b_im", "ssm_c_re", "ssm_c_im")
_SMALL_ORDER = tuple(name for _, _, members in _SMALL_GROUPS for name, _ in members) + _SMALL_EARLY
_WEIGHT_ORDER = ("pre_norm_g", "w_in", "ssm_lam_re", "ssm_lam_im", "ssm_log_step", "ssm_b_re", "ssm_b_im", "ssm_c_re",
                 "ssm_c_im", "ssm_d", "ssm_w_glu", "ssm_b_glu", "attn_sinks", "w_out", "post_norm_g", "pl_w_proj",
                 "pl_w_gate", "pl_b_gate")


def _small_shape(name):
    return (1, 1) if name == "loss" else _SMALL[name]


def _to_kernel_form(name, a):
    a = a[0]
    if name == "w_in":
        return a.T
    if name in ("ssm_b_re", "ssm_b_im"):
        a = a.transpose(0, 2, 1)
    return a.reshape(_SMALL[name]) if name in _SMALL else a


def _from_kernel_form(name, a, shape):
    if name == "w_in":
        a = a.T
    if name in ("ssm_b_re", "ssm_b_im"):
        a = a.reshape(SSM_GROUPS, SSM_GROUP_CH, SSM_STATE).transpose(0, 2, 1)
    return a.reshape(shape)


def _mesh_place():
    x, y, c = lax.axis_index("x"), lax.axis_index("y"), lax.axis_index("c")
    other_chips = ((1 - x, y), (x, 1 - y), (1 - x, 1 - y))
    return x, y, c, other_chips


def _gather_copies(s_refs, g_refs, send_sems, recv_sems, local_sems):
    x, y, c, other_chips = _mesh_place()
    started = []
    for i, (s_ref, g_ref) in enumerate(zip(s_refs, g_refs)):
        rows = s_ref.shape[0]
        half = rows // 2

        def block(chip, g_ref=g_ref, rows=rows, half=half):
            return g_ref.at[pl.ds((2 * chip[0] + chip[1]) * rows + c * half, half), :]

        def copy(k, chip, to, src=None, i=i, block=block):
            return pltpu.make_async_remote_copy(
                src_ref=block(chip) if src is None else src, dst_ref=block(chip), send_sem=send_sems.at[6 * i + k],
                recv_sem=recv_sems.at[6 * i + k], device_id=to, device_id_type=MESH)

        own = pltpu.make_async_copy(s_ref, g_ref.at[pl.ds((2 * x + y) * rows, rows), :], local_sems.at[i])
        own.start()
        first = [copy(k, (x, y), (*chip, c), src=s_ref.at[pl.ds(c * half, half), :])
                 for k, chip in enumerate(other_chips)]
        for cp in first:
            cp.start()
        passed = [copy(3 + k, chip, (x, y, 1 - c)) for k, chip in enumerate(other_chips)]
        started.append((own, first, passed))
    for own, first, passed in started:
        for k in range(3):
            first[k].wait_recv()
            passed[k].start()
    for own, first, passed in started:
        for k in range(3):
            passed[k].wait_recv()
        for cp in first + passed:
            cp.wait_send()
        own.wait()


def _gather_semaphores(n_t):
    return [pltpu.SemaphoreType.DMA((6 * n_t,)), pltpu.SemaphoreType.DMA((6 * n_t,)), pltpu.SemaphoreType.DMA((n_t,))]


def _gather_weights(shards):
    n_t = len(shards)

    def body(*refs):
        _gather_copies(refs[:n_t], refs[n_t:2 * n_t], *refs[2 * n_t + 1:])
        refs[2 * n_t][...] = jnp.zeros_like(refs[2 * n_t])

    any_spec = pl.BlockSpec(memory_space=pl.ANY)
    *full, done = _call(
        body, name="gather_weights", in_specs=[any_spec] * n_t,
        out_specs=[any_spec] * n_t + [pl.BlockSpec(memory_space=pltpu.VMEM)],
        out_shape=[_sds((N_CHIPS * s.shape[0], s.shape[1]), s.dtype) for s in shards]
        + [jax.ShapeDtypeStruct((8, LANES), F32)],
        scratch_shapes=_gather_semaphores(n_t),
    )(*shards)
    return full, done[0, 0]


def _gather_weights_beside(shards, name, collective_id):
    n_t = len(shards)
    hbm = pltpu.MemorySpace.HBM
    s_refs = [jax.new_ref(s, memory_space=hbm) for s in shards]
    g_refs = [jax.empty_ref(jax.ShapeDtypeStruct((N_CHIPS * s.shape[0], s.shape[1]), s.dtype), memory_space=hbm)
              for s in shards]

    def launch(send_sems, recv_sems, local_sems):
        x, y, c, other_chips = _mesh_place()
        peers = [(*chip, c) for chip in other_chips] + [(x, y, 1 - c)]
        barrier = pltpu.get_barrier_semaphore()
        for peer in peers:
            pl.semaphore_signal(barrier, inc=1, device_id=peer, device_id_type=MESH)
        pl.semaphore_wait(barrier, len(peers))
        _gather_copies(s_refs, g_refs, send_sems, recv_sems, local_sems)

    pl.kernel(launch, mesh=plsc.ScalarSubcoreMesh(axis_name="sequencer", num_cores=1), name=name,
              scratch_types=_gather_semaphores(n_t), compiler_params=pltpu.CompilerParams(collective_id=collective_id))()
    return [g[...] for g in g_refs]


_RELATIONS = tuple(((r >> 2) & 1, (r >> 1) & 1, r & 1) for r in range(1, 8))


def _related(place, relation):
    return tuple(1 - a if flip else a for a, flip in zip(place, relation))


def _scatter_beside(mats):
    hbm = pltpu.MemorySpace.HBM
    src_refs = [jax.new_ref(a, memory_space=hbm) for a in mats]
    land_refs = [jax.empty_ref(jax.ShapeDtypeStruct((7, a.shape[0] // 8, a.shape[1]), a.dtype), memory_space=hbm)
                 for a in mats]

    def launch(send_sems, recv_sems):
        me = (lax.axis_index("x"), lax.axis_index("y"), lax.axis_index("c"))
        peers = [_related(me, rel) for rel in _RELATIONS]
        barrier = pltpu.get_barrier_semaphore()
        for peer in peers:
            pl.semaphore_signal(barrier, inc=1, device_id=peer, device_id_type=MESH)
        pl.semaphore_wait(barrier, len(peers))
        copies = []
        for i, (src, land) in enumerate(zip(src_refs, land_refs)):
            hr = land.shape[1]
            for k, (tx, ty, tc) in enumerate(peers):
                rows = pl.ds((2 * tx + ty) * 2 * hr + tc * hr, hr)
                copies.append(pltpu.make_async_remote_copy(
                    src_ref=src.at[rows, :], dst_ref=land.at[k], send_sem=send_sems.at[7 * i + k],
                    recv_sem=recv_sems.at[7 * i + k], device_id=(tx, ty, tc), device_id_type=MESH))
                copies[-1].start()
        for cp in copies:
            cp.wait()

    n_sems = 7 * len(mats)
    pl.kernel(launch, mesh=plsc.ScalarSubcoreMesh(axis_name="sequencer", num_cores=1), name="scatter_beside",
              scratch_types=[pltpu.SemaphoreType.DMA((n_sems,)), pltpu.SemaphoreType.DMA((n_sems,))],
              compiler_params=pltpu.CompilerParams(collective_id=2))()
    return [ref[...] for ref in land_refs]


def _broadcast_beside(arrays):
    hbm = pltpu.MemorySpace.HBM
    src_refs = [jax.new_ref(a, memory_space=hbm) for a in arrays]
    land_refs = [jax.empty_ref(jax.ShapeDtypeStruct((len(_RELATIONS),) + a.shape, a.dtype), memory_space=hbm)
                 for a in arrays]

    def launch(send_sems, recv_sems):
        me = (lax.axis_index("x"), lax.axis_index("y"), lax.axis_index("c"))
        peers = [_related(me, rel) for rel in _RELATIONS]
        barrier = pltpu.get_barrier_semaphore()
        for peer in peers:
            pl.semaphore_signal(barrier, inc=1, device_id=peer, device_id_type=MESH)
        pl.semaphore_wait(barrier, len(peers))
        copies = []
        for i, (src, land) in enumerate(zip(src_refs, land_refs)):
            for k, peer in enumerate(peers):
                copies.append(pltpu.make_async_remote_copy(
                    src_ref=src, dst_ref=land.at[k], send_sem=send_sems.at[7 * i + k],
                    recv_sem=recv_sems.at[7 * i + k], device_id=peer, device_id_type=MESH))
                copies[-1].start()
        for cp in copies:
            cp.wait()

    n_sems = 7 * len(arrays)
    pl.kernel(launch, mesh=plsc.ScalarSubcoreMesh(axis_name="sequencer", num_cores=1), name="broadcast_beside",
              scratch_types=[pltpu.SemaphoreType.DMA((n_sems,)), pltpu.SemaphoreType.DMA((n_sems,))],
              compiler_params=pltpu.CompilerParams(collective_id=3))()
    return [ref[...] for ref in land_refs]


def _exchange_grads(big, small, landed, landed_small):
    n_t = len(big)
    n_g = len(_SMALL_GROUPS)
    names = _SMALL_ORDER
    halves = [(b.shape[0] // N_CHIPS // 2, b.shape[1]) for b in big]
    early = sorted(landed)
    late = [i for i in range(n_t) if i not in landed]
    n_sems = 4 * n_g + 7 * len(late) + n_t
    small_sem0, block_sem0 = n_t, n_t + len(names)
    early_sem0 = block_sem0 + N_CHIPS * len(late)
    landed_sem0 = early_sem0 + 2 * len(early)
    early_small = [n for n in names if n in landed_small]

    def body(*refs):
        pos = 0

        def take(n):
            nonlocal pos
            pos += n
            return refs[pos - n:pos]

        big_refs, small_refs = take(n_t), dict(zip(names, take(len(names))))
        land_refs = dict(zip(early, take(len(early))))
        land_small_refs = dict(zip(early_small, take(len(early_small))))
        out_refs, small_out_refs = take(n_t), dict(zip(names, take(len(names))))
        per_late = lambda: dict(zip(late, take(len(late))))
        ga, gb, pme, send_b, recv_b = per_late(), per_late(), take(n_t), per_late(), per_late()
        own_e, land_e = dict(zip(early, take(len(early)))), dict(zip(early, take(len(early))))
        land_s = dict(zip(early_small, take(len(early_small))))
        s_own, s_sib, s_chips, s_pair = take(n_g), take(n_g), take(n_g), take(n_g)
        stage = dict(zip(names, take(len(names))))
        send_sems, recv_sems, local_sems = take(3)
        x, y, c, other_chips = _mesh_place()
        me = 2 * x + y
        sibling = (x, y, 1 - c)
        sem_at = iter(range(n_sems))

        def remote(src, dst, to):
            k = next(sem_at)
            return pltpu.make_async_remote_copy(src_ref=src, dst_ref=dst, send_sem=send_sems.at[k],
                                                recv_sem=recv_sems.at[k], device_id=to, device_id_type=MESH)

        loads = [pltpu.make_async_copy(small_refs[name], stage[name], local_sems.at[small_sem0 + a])
                 for a, name in enumerate(names)]
        landed_loads = [pltpu.make_async_copy(land_small_refs[name], land_s[name], local_sems.at[landed_sem0 + a])
                        for a, name in enumerate(early_small)]
        for cp in loads + landed_loads:
            cp.start()
        for cp in loads:
            cp.wait()
        small_swaps = []
        for gi, (_, _, members) in enumerate(_SMALL_GROUPS):
            s_own[gi][...] = jnp.zeros_like(s_own[gi])
            for name, r0 in members:
                r, n = _small_shape(name)
                s_own[gi][r0:r0 + r, 0:n] = stage[name][...]
            small_swaps.append(remote(s_own[gi], s_sib[gi], sibling))
            small_swaps[gi].start()
        order = sorted(late, key=lambda i: halves[i][0] * halves[i][1])
        own_loads, big_swaps = {}, {}
        for i in order:
            hr = halves[i][0]
            own_loads[i], big_swaps[i] = [], []
            for j in range(N_CHIPS):
                mine = big_refs[i].at[pl.ds(j * 2 * hr + c * hr, hr), :]
                theirs = big_refs[i].at[pl.ds(j * 2 * hr + (1 - c) * hr, hr), :]
                sem = local_sems.at[block_sem0 + N_CHIPS * late.index(i) + j]
                own_loads[i].append(pltpu.make_async_copy(mine, ga[i].at[j], sem))
                own_loads[i][j].start()
                big_swaps[i].append(remote(theirs, gb[i].at[j], sibling))
                big_swaps[i][j].start()
        early_loads = {}
        for e, i in enumerate(early):
            hr = halves[i][0]
            mine = big_refs[i].at[pl.ds(me * 2 * hr + c * hr, hr), :]
            early_loads[i] = [pltpu.make_async_copy(mine, own_e[i], local_sems.at[early_sem0 + 2 * e]),
                              pltpu.make_async_copy(land_refs[i], land_e[i], local_sems.at[early_sem0 + 2 * e + 1])]
            for cp in early_loads[i]:
                cp.start()
        small_sends = []
        for gi in range(n_g):
            small_swaps[gi].wait_recv()
            s_pair[gi][...] = s_own[gi][...] + s_sib[gi][...]
            small_sends.append([remote(s_pair[gi], s_chips[gi].at[k], (*chip, c)) for k, chip in enumerate(other_chips)])
            for cp in small_sends[gi]:
                cp.start()

        def pair_sum(i, j):
            return ga[i][j] + gb[i][j]

        big_sends = {}
        for i in order:
            for j in range(N_CHIPS):
                own_loads[i][j].wait()
                big_swaps[i][j].wait_recv()
            big_sends[i] = []
            for k, chip in enumerate(other_chips):
                send_b[i][k] = pair_sum(i, 2 * chip[0] + chip[1]).astype(BF16)
                big_sends[i].append(remote(send_b[i].at[k], recv_b[i].at[k], (*chip, c)))
                big_sends[i][k].start()
        last_swaps, keeps = {}, {}
        for i in early + order:
            hr = halves[i][0]
            if i in landed:
                for cp in early_loads[i]:
                    cp.wait()
                total = own_e[i][...]
                for k in range(len(_RELATIONS)):
                    total = total + land_e[i][k]
                pme[i][...] = total
            else:
                for k in range(3):
                    big_sends[i][k].wait_recv()
                pme[i][...] = ((pair_sum(i, me) + recv_b[i][0].astype(F32)) + recv_b[i][1].astype(F32)) + recv_b[i][2].astype(F32)
            mine = out_refs[i].at[pl.ds(c * hr, hr), :]
            keeps[i] = pltpu.make_async_copy(pme[i], mine, local_sems.at[i])
            keeps[i].start()
            last_swaps[i] = remote(pme[i], mine, sibling)
            last_swaps[i].start()

        for gi, (_, _, members) in enumerate(_SMALL_GROUPS):
            for k in range(3):
                small_sends[gi][k].wait_recv()
            total = None
            for j in range(N_CHIPS):
                rel = jnp.bitwise_xor(j, me)
                term = jnp.where(rel == 0, s_pair[gi][...], jnp.where(
                    rel == 2, s_chips[gi][0], jnp.where(rel == 1, s_chips[gi][1], s_chips[gi][2])))
                total = term if total is None else total + term
            s_sib[gi][...] = total
            for name, r0 in members:
                r, n = _small_shape(name)
                stage[name][...] = s_sib[gi][r0:r0 + r, 0:n]
        my_index = 4 * x + 2 * y + c
        for cp in landed_loads:
            cp.wait()
        for name in early_small:
            total = None
            for d in range(2 * N_CHIPS):
                rel = jnp.bitwise_xor(d, my_index)
                term = stage[name][...]
                for k in range(len(_RELATIONS)):
                    term = jnp.where(rel == k + 1, land_s[name][k], term)
                total = term if total is None else total + term
            stage[name][...] = total
        stores = [pltpu.make_async_copy(stage[name], small_out_refs[name], local_sems.at[small_sem0 + a])
                  for a, name in enumerate(names)]
        for cp in stores:
            cp.start()

        for i in range(n_t):
            last_swaps[i].wait_recv()
            keeps[i].wait()
        for cp in stores:
            cp.wait()
        groups = list(big_swaps.values()) + small_sends + list(big_sends.values())
        for cp in small_swaps + [cp for group in groups for cp in group] + list(last_swaps.values()):
            cp.wait_send()

    any_spec = pl.BlockSpec(memory_space=pl.ANY)
    small_shapes = [_sds(_small_shape(n)) for n in names]
    group_shapes = [shape for _, shape, _ in _SMALL_GROUPS]
    vmem = lambda which, dtype, lead=(): [pltpu.VMEM(lead + halves[i], dtype) for i in which]
    outs = _call(
        body, name="exchange_grads",
        in_specs=[any_spec] * (n_t + len(names) + len(early) + len(early_small)),
        out_specs=[any_spec] * (n_t + len(names)),
        out_shape=[_sds((b.shape[0] // N_CHIPS, b.shape[1])) for b in big] + small_shapes,
        scratch_shapes=(vmem(late, F32, (N_CHIPS,)) + vmem(late, F32, (N_CHIPS,)) + vmem(range(n_t), F32)
                        + vmem(late, BF16, (3,)) + vmem(late, BF16, (3,))
                        + vmem(early, F32) + vmem(early, F32, (len(_RELATIONS),))
                        + [pltpu.VMEM((len(_RELATIONS),) + _small_shape(n), F32) for n in early_small]
                        + [pltpu.VMEM(s, F32) for s in group_shapes] * 2 + [pltpu.VMEM((3,) + s, F32) for s in group_shapes]
                        + [pltpu.VMEM(s, F32) for s in group_shapes]
                        + [pltpu.VMEM(_small_shape(n), F32) for n in names]
                        + [pltpu.SemaphoreType.DMA((n_sems,)), pltpu.SemaphoreType.DMA((n_sems,)),
                           pltpu.SemaphoreType.DMA((landed_sem0 + len(early_small),))]),
        compiler_params=_params(48),
    )(*big, *[small[n] for n in names], *[landed[i] for i in early], *[landed_small[n] for n in early_small])
    return list(outs[:n_t]), dict(zip(names, outs[n_t:n_t + len(names)]))


def _adamw_update(w, g, m, v):
    m = ADAM_B1 * m + (1.0 - ADAM_B1) * g
    v = ADAM_B2 * v + (1.0 - ADAM_B2) * (g * g)
    m_hat = m / (1.0 - ADAM_B1 ** ADAM_STEP)
    v_hat = v / (1.0 - ADAM_B2 ** ADAM_STEP)
    return -ADAM_LR * (m_hat / (jnp.sqrt(v_hat) + ADAM_EPS) + ADAM_WD * w), m, v


def _adamw(w, g, m, v, grid, name):
    n_t = len(w)

    def body(*refs):
        ins, outs = refs[:4 * n_t], refs[4 * n_t:]
        for i in range(n_t):
            w_, g_, m_, v_ = [ins[a * n_t + i][...] for a in range(4)]
            vals = (g_,) + _adamw_update(w_, g_, m_, v_)
            for a in range(4):
                outs[a * n_t + i][...] = vals[a]

    specs = [pl.BlockSpec((a.shape[0] // grid, a.shape[1]), lambda i: (i, 0)) for a in w]
    shapes = [_sds(a.shape) for a in w]
    outs = _call(
        body, name=name, grid=(grid,), in_specs=specs * 4, out_specs=specs * 4, out_shape=shapes * 4,
        compiler_params=_params(40, ("arbitrary",)),
    )(*w, *g, *m, *v)
    return [outs[a * n_t:(a + 1) * n_t] for a in range(4)]


def kernel(x, p, pre_norm_g, w_in, ssm_lam_re, ssm_lam_im, ssm_log_step, ssm_b_re, ssm_b_im, ssm_c_re, ssm_c_im, ssm_d, ssm_w_glu, ssm_b_glu, attn_sinks, w_out, post_norm_g, pl_w_proj, pl_w_gate, pl_b_gate, loss_target, m_pre_norm_g, m_w_in, m_ssm_lam_re, m_ssm_lam_im, m_ssm_log_step, m_ssm_b_re, m_ssm_b_im, m_ssm_c_re, m_ssm_c_im, m_ssm_d, m_ssm_w_glu, m_ssm_b_glu, m_attn_sinks, m_w_out, m_post_norm_g, m_pl_w_proj, m_pl_w_gate, m_pl_b_gate, v_pre_norm_g, v_w_in, v_ssm_lam_re, v_ssm_lam_im, v_ssm_log_step, v_ssm_b_re, v_ssm_b_im, v_ssm_c_re, v_ssm_c_im, v_ssm_d, v_ssm_w_glu, v_ssm_b_glu, v_attn_sinks, v_w_out, v_post_norm_g, v_pl_w_proj, v_pl_w_gate, v_pl_b_gate):
    weights = dict(pre_norm_g=pre_norm_g, w_in=w_in, ssm_lam_re=ssm_lam_re, ssm_lam_im=ssm_lam_im,
                   ssm_log_step=ssm_log_step, ssm_b_re=ssm_b_re, ssm_b_im=ssm_b_im, ssm_c_re=ssm_c_re,
                   ssm_c_im=ssm_c_im, ssm_d=ssm_d, ssm_w_glu=ssm_w_glu, ssm_b_glu=ssm_b_glu, attn_sinks=attn_sinks,
                   w_out=w_out, post_norm_g=post_norm_g, pl_w_proj=pl_w_proj, pl_w_gate=pl_w_gate, pl_b_gate=pl_b_gate)
    m_in = dict(pre_norm_g=m_pre_norm_g, w_in=m_w_in, ssm_lam_re=m_ssm_lam_re, ssm_lam_im=m_ssm_lam_im,
                ssm_log_step=m_ssm_log_step, ssm_b_re=m_ssm_b_re, ssm_b_im=m_ssm_b_im, ssm_c_re=m_ssm_c_re,
                ssm_c_im=m_ssm_c_im, ssm_d=m_ssm_d, ssm_w_glu=m_ssm_w_glu, ssm_b_glu=m_ssm_b_glu,
                attn_sinks=m_attn_sinks, w_out=m_w_out, post_norm_g=m_post_norm_g, pl_w_proj=m_pl_w_proj,
                pl_w_gate=m_pl_w_gate, pl_b_gate=m_pl_b_gate)
    v_in = dict(pre_norm_g=v_pre_norm_g, w_in=v_w_in, ssm_lam_re=v_ssm_lam_re, ssm_lam_im=v_ssm_lam_im,
                ssm_log_step=v_ssm_log_step, ssm_b_re=v_ssm_b_re, ssm_b_im=v_ssm_b_im, ssm_c_re=v_ssm_c_re,
                ssm_c_im=v_ssm_c_im, ssm_d=v_ssm_d, ssm_w_glu=v_ssm_w_glu, ssm_b_glu=v_ssm_b_glu,
                attn_sinks=v_attn_sinks, w_out=v_w_out, post_norm_g=v_post_norm_g, pl_w_proj=v_pl_w_proj,
                pl_w_gate=v_pl_w_gate, pl_b_gate=v_pl_b_gate)

    def two_d(tree):
        return {k: _to_kernel_form(k, a) for k, a in tree.items()}

    w2, m2, v2 = two_d(weights), two_d(m_in), two_d(v_in)

    (w_in_full,) = _gather_weights_beside([w2["w_in"].astype(BF16)], "gather_w_in_beside", 4)
    gathered = w_in_full[0, 0].astype(F32) * 0.0
    rest = _gather_weights_beside([(w2[n] + gathered).astype(BF16) for n in _BIG[1:]], "gather_weights_beside", 1)
    full = dict(zip(_BIG, [w_in_full] + rest))
    s5_params = tuple(w2[n] for n in ("ssm_lam_re", "ssm_lam_im", "ssm_log_step", "ssm_b_re", "ssm_b_im", "ssm_c_re",
                                      "ssm_c_im"))
    grad_x, loss, grads = _local_step(
        x, p, loss_target, w2["pre_norm_g"], full["w_in"], s5_params, w2["ssm_d"], full["ssm_w_glu"], w2["ssm_b_glu"],
        w2["attn_sinks"], full["w_out"], w2["post_norm_g"], full["pl_w_proj"], full["pl_w_gate"], w2["pl_b_gate"])

    sent_early = ("w_out", "pl_w_gate", "pl_w_proj")
    landed = dict(zip([_BIG.index(n) for n in sent_early], _scatter_beside([grads[n] for n in sent_early])))
    after_scatter = landed[_BIG.index(sent_early[-1])][0, 0, 0] * 0.0
    late_operands = [grads[_SMALL_EARLY[0]] + after_scatter] + [grads[n] for n in _SMALL_EARLY[1:]]
    landed_small = dict(zip(_SMALL_EARLY, _broadcast_beside(late_operands)))
    g_big, g_small = _exchange_grads([grads[n] for n in _BIG], {**{n: grads[n] for n in _SMALL}, "loss": loss}, landed,
                                     landed_small)
    g_big = dict(zip(_BIG, g_big))
    total_loss = g_small.pop("loss")

    big_out = _adamw([w2[n] for n in _BIG], [g_big[n] for n in _BIG], [m2[n] for n in _BIG], [v2[n] for n in _BIG],
                     8, "adamw_matrices")
    small_names = tuple(_SMALL)
    small_out = _adamw([w2[n] for n in small_names], [g_small[n] for n in small_names], [m2[n] for n in small_names],
                       [v2[n] for n in small_names], 1, "adamw_small")

    results = [{**dict(zip(_BIG, big_part)), **dict(zip(small_names, small_part))}
               for big_part, small_part in zip(big_out, small_out)]
    flat = [_from_kernel_form(name, r[name], weights[name].shape) for r in results for name in _WEIGHT_ORDER]
    return (total_loss.reshape(()), grad_x, *flat)
```

```python
import math

import jax
import jax.numpy as jnp
from jax import lax
from jax.experimental import pallas as pl
from jax.experimental.pallas import tpu as pltpu
from jax.experimental.pallas import tpu_sc as plsc

F32 = jnp.float32
BF16 = jnp.bfloat16

D_MODEL = 1024
D_SSM = 512
D_ATTN = 512
SSM_GROUPS = 32
SSM_GROUP_CH = 16
SSM_STATE = 64
SSM_LANES = SSM_GROUPS * SSM_STATE
HEAD_DIM = 64
N_HEADS = 8
KV_HEADS = 2
Q_PER_KV = 4
WINDOW = 128
BLOCK = 128
D_PLE = 256
D_IN = 2304
EPS = 1e-6
ATTN_SCALE = 1.0 / math.sqrt(HEAD_DIM)

ADAM_LR = 0.001
ADAM_B1 = 0.9
ADAM_B2 = 0.999
ADAM_EPS = 1e-08
ADAM_WD = 0.01
ADAM_STEP = 10

N_CHIPS = 4
LANES = 128
SCAN_CHUNKS = 8
SCAN_TILE_STEPS = 32
SCAN_LANE_CHUNK = 512
MIB = 2 ** 20
MESH = pl.DeviceIdType.MESH


def _dot(a, b):
    return jnp.dot(a, b, preferred_element_type=F32)


def _dot_nt(a, b):
    return lax.dot_general(a, b, (((1,), (1,)), ((), ())), preferred_element_type=F32)


def _dot_tn(a, b):
    return lax.dot_general(a, b, (((0,), (0,)), ((), ())), preferred_element_type=F32)


def _params(vmem_mib, semantics=None):
    kw = dict(vmem_limit_bytes=vmem_mib * MIB)
    if semantics is not None:
        kw["dimension_semantics"] = semantics
    return pltpu.CompilerParams(**kw)


def _full(shape):
    nd = len(shape)
    return pl.BlockSpec(shape, lambda *_: (0,) * nd, pipeline_mode=pl.Buffered(1))


def _rows(tm, width):
    return pl.BlockSpec((tm, width), lambda i: (i, 0))


def _sds(shape, dtype=F32):
    return pltpu.HBM(shape, dtype)


def _call(body, **kw):
    fn = pl.pallas_call(body, **kw)
    return lambda *args: fn(*[pltpu.with_memory_space_constraint(a, pltpu.HBM) for a in args])


def _silu(z):
    return z * jax.nn.sigmoid(z)


def _in_proj(x2d, g1, w_in_t, n_seq, seq):
    rows = x2d.shape[0]
    tm = 512
    slab, steps, _, _ = _scan_geometry(n_seq, seq)

    def body(x_ref, g_ref, w_ref, *out_refs):
        u_parts, (zs_ref, q_ref, k_ref, v_ref, za_ref) = out_refs[:_SCAN_PARTS], out_refs[_SCAN_PARTS:]
        x = x_ref[...]
        r = lax.rsqrt(jnp.mean(x * x, axis=-1, keepdims=True) + EPS)
        hn = (x * r * g_ref[...]).astype(BF16)

        whole = _dot_nt(hn, w_ref[...])

        def proj(a, b):
            return whole[:, a:b]

        _store_chunks(u_parts, pl.program_id(0) * (tm // steps), proj(0, 512), steps, slab)
        zs_ref[...] = proj(512, 1024)
        q_ref[...] = (proj(1024, 1536) * ATTN_SCALE).astype(BF16)
        k_ref[...] = proj(1536, 1664).astype(BF16)
        v_ref[...] = proj(1664, 1792).astype(BF16)
        za_ref[...] = proj(1792, 2304)

    *u_parts, zs, q, k, v, za = _call(
        body, name="in_proj", grid=(rows // tm,),
        in_specs=[_rows(tm, D_MODEL), _full((1, D_MODEL)), _full((D_IN, D_MODEL))],
        out_specs=_whole_parts(rows) + [_rows(tm, 512), _rows(tm, 512), _rows(tm, 128), _rows(tm, 128), _rows(tm, 512)],
        out_shape=_part_shapes(rows) + [_sds((rows, 512)), _sds((rows, 512), BF16), _sds((rows, 128), BF16),
                                        _sds((rows, 128), BF16), _sds((rows, 512))],
        compiler_params=_params(48, ("arbitrary",)),
    )(x2d, g1, w_in_t)
    return u_parts, zs, q, k, v, za


def _in_proj_bwd(x2d, dh1, g1, w_in_t, du_parts, dzs, dq, dk, dv, dza, n_seq, seq):
    rows = x2d.shape[0]
    tm = 512
    slab, steps, _, _ = _scan_geometry(n_seq, seq)

    def body(x_ref, dh1_ref, g_ref, w_ref, *refs):
        du_parts, (dzs_ref, dq_ref, dk_ref, dv_ref, dza_ref, gx_ref, dw_ref, dg_ref) = refs[:_SCAN_PARTS], refs[_SCAN_PARTS:]

        @pl.when(pl.program_id(0) == 0)
        def _():
            dw_ref[...] = jnp.zeros_like(dw_ref)
            dg_ref[...] = jnp.zeros_like(dg_ref)

        x = x_ref[...]
        g = g_ref[...]
        r = lax.rsqrt(jnp.mean(x * x, axis=-1, keepdims=True) + EPS)
        xr = x * r
        hn = (xr * g).astype(BF16)
        du = _load_chunks(du_parts, pl.program_id(0) * (tm // steps), tm // steps, steps, slab)
        d_proj = jnp.concatenate([du.astype(BF16), dzs_ref[...], dq_ref[...], dk_ref[...], dv_ref[...], dza_ref[...]],
                                 axis=1)
        dhn = _dot(d_proj, w_ref[...])
        dw_ref[...] += _dot_tn(d_proj, hn)
        dg_ref[...] += jnp.sum(dhn * xr, axis=0, keepdims=True)
        a_ = dhn * g
        gx_ref[...] = dh1_ref[...] + r * a_ - xr * (r * jnp.mean(a_ * xr, axis=-1, keepdims=True))

    return _call(
        body, name="in_proj_bwd", grid=(rows // tm,),
        in_specs=[_rows(tm, D_MODEL), _rows(tm, D_MODEL), _full((1, D_MODEL)), _full((D_IN, D_MODEL))]
        + _whole_parts(rows) + [_rows(tm, 512), _rows(tm, 512), _rows(tm, 128), _rows(tm, 128), _rows(tm, 512)],
        out_specs=[_rows(tm, D_MODEL), _full((D_IN, D_MODEL)), _full((1, D_MODEL))],
        out_shape=[_sds((rows, D_MODEL)), _sds((D_IN, D_MODEL)), _sds((1, D_MODEL))],
        compiler_params=_params(56, ("arbitrary",)),
    )(x2d, dh1, g1, w_in_t, *du_parts, dzs, dq, dk, dv, dza)


def _iota(shape, axis):
    return lax.broadcasted_iota(jnp.int32, shape, axis)


def _sum_of_thirds(f, a):
    hi = a.astype(BF16)
    rest = a - hi.astype(F32)
    mid = rest.astype(BF16)
    low = (rest - mid.astype(F32)).astype(BF16)
    return (f(hi) + f(mid)) + f(low)


@jax.custom_vjp
def _pick_rows(e, a):
    return _sum_of_thirds(lambda part: _dot(e, part), a)


def _pick_rows_fwd(e, a):
    return _pick_rows(e, a), e


def _pick_rows_bwd(e, ct):
    return jnp.zeros_like(e), _sum_of_thirds(lambda part: _dot_tn(e, part), ct)


_pick_rows.defvjp(_pick_rows_fwd, _pick_rows_bwd)


@jax.custom_vjp
def _pick_cols(a, e):
    return _sum_of_thirds(lambda part: _dot(part, e), a)


def _pick_cols_fwd(a, e):
    return _pick_cols(a, e), e


def _pick_cols_bwd(e, ct):
    return _sum_of_thirds(lambda part: _dot_nt(part, e), ct), jnp.zeros_like(e)


_pick_cols.defvjp(_pick_cols_fwd, _pick_cols_bwd)


_HALF_GROUPS = SSM_GROUPS // 2
_N_SHIFT = SSM_STATE.bit_length() - 1
_P_SHIFT = SSM_GROUP_CH.bit_length() - 1


def _s5_operands(lam_re, lam_im, log_step, b_re, b_im, c_re, c_im):
    g, n, p = SSM_GROUPS, SSM_STATE, SSM_GROUP_CH
    gn, gp, hn_, hp = g * n, g * p, _HALF_GROUPS * n, _HALF_GROUPS * p
    eye_g = _iota((g, g), 0) == _iota((g, g), 1)
    step = jnp.sum(jnp.where(eye_g, jnp.exp(log_step), 0.0), axis=1, keepdims=True)
    a_re = lam_re * step
    a_im = lam_im * step
    mag = jnp.exp(a_re)
    lbar_re = mag * jnp.cos(a_im)
    lbar_im = mag * jnp.sin(a_im)
    n_re = lbar_re - 1.0
    den = lam_re * lam_re + lam_im * lam_im
    f_re = (n_re * lam_re + lbar_im * lam_im) / den
    f_im = (lbar_im * lam_re - n_re * lam_im) / den

    spread_n = (_iota((n, gn), 0) == (_iota((n, gn), 1) & (n - 1))).astype(BF16)
    own_g = _iota((g, gn), 0) == (_iota((g, gn), 1) >> _N_SHIFT)

    def to_row(a):
        return jnp.sum(jnp.where(own_g, _pick_cols(a, spread_n), 0.0), axis=0, keepdims=True)

    per_group = ((_iota((gp, g), 0) >> _P_SHIFT) == _iota((gp, g), 1)).astype(BF16)
    fx_re, fx_im = _pick_rows(per_group, f_re), _pick_rows(per_group, f_im)
    bbar_re = fx_re * b_re - fx_im * b_im
    bbar_im = fx_re * b_im + fx_im * b_re

    tile_n = (_iota((n, hn_), 0) == (_iota((n, hn_), 1) & (n - 1))).astype(BF16)
    same_group = (_iota((hp, hn_), 0) >> _P_SHIFT) == (_iota((hp, hn_), 1) >> _N_SHIFT)

    def embed(a, hf):
        return jnp.where(same_group, _pick_cols(a[hf * hp:(hf + 1) * hp], tile_n), 0.0)

    return (to_row(lbar_re), to_row(lbar_im), embed(bbar_re, 0), embed(bbar_re, 1), embed(bbar_im, 0),
            embed(bbar_im, 1), embed(c_re, 0), embed(c_re, 1), embed(c_im, 0), embed(c_im, 1))


_S5_PARAM_SHAPES = ((SSM_GROUPS, SSM_STATE), (SSM_GROUPS, SSM_STATE), (1, SSM_GROUPS),
                    (D_SSM, SSM_STATE), (D_SSM, SSM_STATE), (D_SSM, SSM_STATE), (D_SSM, SSM_STATE))
_CM_SHAPE = (2, _HALF_GROUPS * SSM_GROUP_CH, _HALF_GROUPS * SSM_STATE)
_S5_OPERAND_SHAPES = ((1, SSM_LANES), (1, SSM_LANES), _CM_SHAPE, _CM_SHAPE, _CM_SHAPE, _CM_SHAPE)


def _s5_params_fwd(*params):
    def body(*refs):
        ins, (lre_ref, lim_ref, btre_ref, btim_ref, cmre_ref, cmim_ref) = refs[:7], refs[7:]
        vals = _s5_operands(*[r[...] for r in ins])
        lre_ref[...] = vals[0]
        lim_ref[...] = vals[1]
        for ref, pair in zip((btre_ref, btim_ref, cmre_ref, cmim_ref), (vals[2:4], vals[4:6], vals[6:8], vals[8:10])):
            ref[0] = pair[0].astype(BF16)
            ref[1] = pair[1].astype(BF16)

    dtypes = (F32, F32, BF16, BF16, BF16, BF16)
    return _call(
        body, name="s5_params_fwd",
        in_specs=[_full(s) for s in _S5_PARAM_SHAPES], out_specs=[_full(s) for s in _S5_OPERAND_SHAPES],
        out_shape=[_sds(s, d) for s, d in zip(_S5_OPERAND_SHAPES, dtypes)], compiler_params=_params(32),
    )(*params)


def _s5_params_bwd(params, cotangents):
    def body(*refs):
        ins, (dlre, dlim, dbtre, dbtim, dcmre, dcmim), outs = refs[:7], refs[7:13], refs[13:]
        _, vjp = jax.vjp(_s5_operands, *[r[...] for r in ins])
        cts = (dlre[...], dlim[...], dbtre[0], dbtre[1], dbtim[0], dbtim[1], dcmre[0], dcmre[1], dcmim[0], dcmim[1])
        for ref, val in zip(outs, vjp(cts)):
            ref[...] = val

    return _call(
        body, name="s5_params_bwd",
        in_specs=[_full(s) for s in _S5_PARAM_SHAPES + _S5_OPERAND_SHAPES],
        out_specs=[_full(s) for s in _S5_PARAM_SHAPES],
        out_shape=[_sds(s) for s in _S5_PARAM_SHAPES], compiler_params=_params(48),
    )(*params, *cotangents)


def _scan_geometry(n_seq, seq):
    slab = n_seq * SCAN_CHUNKS
    steps = seq // SCAN_CHUNKS
    tile_rows = slab * SCAN_TILE_STEPS
    n_tiles = steps // SCAN_TILE_STEPS
    return slab, steps, tile_rows, n_tiles


_SCAN_PARTS = D_SSM // LANES


def _whole_parts(rows):
    return [_full((rows, LANES))] * _SCAN_PARTS


def _part_shapes(rows):
    return [_sds((rows, LANES))] * _SCAN_PARTS


def _load_chunks(parts, first_chunk, n_chunks, steps, slab):
    return jnp.concatenate([
        jnp.concatenate([ref[pl.ds(first_chunk + q, steps, stride=slab), :] for ref in parts], axis=1)
        for q in range(n_chunks)], axis=0)


def _store_chunks(parts, first_chunk, value, steps, slab):
    for q in range(value.shape[0] // steps):
        for j, ref in enumerate(parts):
            ref[pl.ds(first_chunk + q, steps, stride=slab), :] = value[q * steps:(q + 1) * steps,
                                                                     j * LANES:(j + 1) * LANES]


def _join_parts(parts):
    return jnp.concatenate([ref[...] for ref in parts], axis=1)


def _split_parts(parts, value):
    for j, ref in enumerate(parts):
        ref[...] = value[:, j * LANES:(j + 1) * LANES]


def _complex_power(re, im, n):
    out = None
    while n:
        if n & 1:
            out = (re, im) if out is None else (out[0] * re - out[1] * im, out[0] * im + out[1] * re)
        n >>= 1
        if n:
            re, im = re * re - im * im, 2.0 * re * im
    return out


def _chunk_carry(sum_re, sum_im, carry_re, carry_im, a_re, a_im, n_seq, reverse):
    carry_re[...] = jnp.zeros_like(carry_re)
    carry_im[...] = jnp.zeros_like(carry_im)
    for s in range(n_seq):
        order = range(SCAN_CHUNKS - 2, -1, -1) if reverse else range(1, SCAN_CHUNKS)
        for c in order:
            r = s * SCAN_CHUNKS + c
            p = r + 1 if reverse else r - 1
            p_re, p_im = carry_re[p:p + 1, :], carry_im[p:p + 1, :]
            carry_re[r:r + 1, :] = a_re * p_re - a_im * p_im + sum_re[p:p + 1, :]
            carry_im[r:r + 1, :] = a_re * p_im + a_im * p_re + sum_im[p:p + 1, :]


def _s5_scan_fwd(u_parts, bt_re, bt_im, cm_re, cm_im, lbar_re, lbar_im, d_row, n_seq, seq):
    slab, steps, tile_rows, n_tiles = _scan_geometry(n_seq, seq)
    rows = u_parts[0].shape[0]

    def body(*refs):
        u_refs, refs = refs[:_SCAN_PARTS], refs[_SCAN_PARTS:]
        (bre_ref, bim_ref, cre_ref, cim_ref, lre_ref, lim_ref, d_ref), refs = refs[:7], refs[7:]
        y_refs, (hre_ref, him_ref, st_re, st_im, h0_re, h0_im, buf_re, buf_im) = refs[:_SCAN_PARTS], refs[_SCAN_PARTS:]
        second = pl.program_id(0) == 1
        i = pl.program_id(1)

        @pl.when(jnp.logical_and(i == 0, jnp.logical_not(second)))
        def _():
            st_re[...] = jnp.zeros_like(st_re)
            st_im[...] = jnp.zeros_like(st_im)

        u = _join_parts(u_refs)
        ub = u.astype(BF16)
        for hf in range(2):
            cols = slice(hf * 1024, (hf + 1) * 1024)
            buf_re[:, cols] = _dot(ub[:, hf * 256:(hf + 1) * 256], bre_ref[hf])
            buf_im[:, cols] = _dot(ub[:, hf * 256:(hf + 1) * 256], bim_ref[hf])

        for lc in range(SSM_LANES // SCAN_LANE_CHUNK):
            cols = slice(lc * SCAN_LANE_CHUNK, (lc + 1) * SCAN_LANE_CHUNK)
            l_re = jnp.broadcast_to(lre_ref[:, cols], (slab, SCAN_LANE_CHUNK))
            l_im = jnp.broadcast_to(lim_ref[:, cols], (slab, SCAN_LANE_CHUNK))

            def scan_tile(keep_states):
                def step(t, carry):
                    s_re, s_im = carry
                    r0 = pl.multiple_of(t * slab, slab)
                    n_re = l_re * s_re - l_im * s_im + buf_re[pl.ds(r0, slab), cols]
                    n_im = l_re * s_im + l_im * s_re + buf_im[pl.ds(r0, slab), cols]
                    if keep_states:
                        buf_re[pl.ds(r0, slab), cols] = n_re
                        buf_im[pl.ds(r0, slab), cols] = n_im
                    return n_re, n_im

                s_re, s_im = lax.fori_loop(0, SCAN_TILE_STEPS, step, (st_re[:, cols], st_im[:, cols]), unroll=True)
                st_re[:, cols] = s_re
                st_im[:, cols] = s_im

            pl.when(jnp.logical_not(second))(lambda: scan_tile(False))
            pl.when(second)(lambda: scan_tile(True))

        @pl.when(jnp.logical_and(i == n_tiles - 1, jnp.logical_not(second)))
        def _():
            a_re, a_im = _complex_power(lre_ref[...], lim_ref[...], steps)
            _chunk_carry(st_re, st_im, h0_re, h0_im, a_re, a_im, n_seq, reverse=False)
            st_re[...] = h0_re[...]
            st_im[...] = h0_im[...]

        @pl.when(second)
        def _():
            h_re = buf_re[...].astype(BF16)
            h_im = buf_im[...].astype(BF16)
            hre_ref[...] = h_re
            him_ref[...] = h_im
            for hf in range(2):
                cols = slice(hf * 1024, (hf + 1) * 1024)
                ycols = slice(hf * 256, (hf + 1) * 256)
                y_half = (_dot_nt(h_re[:, cols], cre_ref[hf]) - _dot_nt(h_im[:, cols], cim_ref[hf])
                          + d_ref[:, ycols] * u[:, ycols])
                _split_parts(y_refs[2 * hf:2 * hf + 2], y_half)

    tile = lambda w: pl.BlockSpec((tile_rows, w), lambda p, i: (i, 0))
    out_tile = lambda w: pl.BlockSpec((tile_rows, w), lambda p, i: (i * p, 0))
    cm = _full(_CM_SHAPE)
    outs = _call(
        body, name="s5_scan_fwd", grid=(2, n_tiles),
        in_specs=[tile(LANES)] * _SCAN_PARTS + [cm, cm, cm, cm, _full((1, SSM_LANES)), _full((1, SSM_LANES)),
                                                _full((1, 512))],
        out_specs=[out_tile(LANES)] * _SCAN_PARTS + [out_tile(SSM_LANES), out_tile(SSM_LANES)],
        out_shape=_part_shapes(rows) + [_sds((rows, SSM_LANES), BF16), _sds((rows, SSM_LANES), BF16)],
        scratch_shapes=[pltpu.VMEM((slab, SSM_LANES), F32)] * 4 + [pltpu.VMEM((tile_rows, SSM_LANES), F32)] * 2,
        compiler_params=_params(40, ("arbitrary", "arbitrary")),
    )(*u_parts, bt_re, bt_im, cm_re, cm_im, lbar_re, lbar_im, d_row)
    return outs[:_SCAN_PARTS], outs[_SCAN_PARTS], outs[_SCAN_PARTS + 1]


def _s5_scan_bwd(dy_parts, u_parts, h_re, h_im, bt_re, bt_im, cm_re, cm_im, lbar_re, lbar_im, d_row, n_seq, seq):
    slab, steps, tile_rows, n_tiles = _scan_geometry(n_seq, seq)
    rows = u_parts[0].shape[0]

    def body(*refs):
        dy_refs, u_refs, refs = refs[:_SCAN_PARTS], refs[_SCAN_PARTS:2 * _SCAN_PARTS], refs[2 * _SCAN_PARTS:]
        (hre_ref, him_ref, bre_ref, bim_ref, cre_ref, cim_ref, lre_ref, lim_ref, d_ref), refs = refs[:9], refs[9:]
        du_refs, refs = refs[:_SCAN_PARTS], refs[_SCAN_PARTS:]
        (dbre_ref, dbim_ref, dcre_ref, dcim_ref, dlre_ref, dlim_ref, dd_ref,
         st_re, st_im, g0_re, g0_im, acc_re, acc_im, buf_re, buf_im) = refs
        second = pl.program_id(0) == 1
        i = pl.program_id(1)

        @pl.when(jnp.logical_and(i == 0, jnp.logical_not(second)))
        def _():
            st_re[...] = jnp.zeros_like(st_re)
            st_im[...] = jnp.zeros_like(st_im)
            acc_re[...] = jnp.zeros_like(acc_re)
            acc_im[...] = jnp.zeros_like(acc_im)
            for ref in (dbre_ref, dbim_ref, dcre_ref, dcim_ref, dd_ref):
                ref[...] = jnp.zeros_like(ref)

        dy = _join_parts(dy_refs)
        dyb = dy.astype(BF16)
        for hf in range(2):
            cols = slice(hf * 1024, (hf + 1) * 1024)
            buf_re[:, cols] = _dot(dyb[:, hf * 256:(hf + 1) * 256], cre_ref[hf])
            buf_im[:, cols] = -_dot(dyb[:, hf * 256:(hf + 1) * 256], cim_ref[hf])

        for lc in range(SSM_LANES // SCAN_LANE_CHUNK):
            cols = slice(lc * SCAN_LANE_CHUNK, (lc + 1) * SCAN_LANE_CHUNK)
            l_re = jnp.broadcast_to(lre_ref[:, cols], (slab, SCAN_LANE_CHUNK))
            l_im = jnp.broadcast_to(lim_ref[:, cols], (slab, SCAN_LANE_CHUNK))

            def advance(r0, s_re, s_im):
                n_re = l_re * s_re + l_im * s_im + buf_re[pl.ds(r0, slab), cols]
                n_im = l_re * s_im - l_im * s_re + buf_im[pl.ds(r0, slab), cols]
                buf_re[pl.ds(r0, slab), cols] = n_re
                buf_im[pl.ds(r0, slab), cols] = n_im
                return n_re, n_im

            def row0(k):
                return pl.multiple_of((SCAN_TILE_STEPS - 1 - k) * slab, slab)

            @pl.when(jnp.logical_not(second))
            def _():
                s_re, s_im = lax.fori_loop(0, SCAN_TILE_STEPS, lambda k, s: advance(row0(k), *s),
                                           (st_re[:, cols], st_im[:, cols]), unroll=True)
                st_re[:, cols] = s_re
                st_im[:, cols] = s_im

            @pl.when(second)
            def _():
                def step(k, carry):
                    s_re, s_im, a_re, a_im = carry
                    r0 = row0(k)
                    hr = hre_ref[pl.ds(r0, slab), cols].astype(F32)
                    hi = him_ref[pl.ds(r0, slab), cols].astype(F32)
                    a_re = a_re + s_re * hr + s_im * hi
                    a_im = a_im + s_im * hr - s_re * hi
                    return advance(r0, s_re, s_im) + (a_re, a_im)

                zero = jnp.zeros((slab, SCAN_LANE_CHUNK), F32)
                s_re, s_im, a_re, a_im = lax.fori_loop(
                    0, SCAN_TILE_STEPS, step, (st_re[:, cols], st_im[:, cols], zero, zero), unroll=True)
                st_re[:, cols] = s_re
                st_im[:, cols] = s_im
                acc_re[:, cols] += a_re
                acc_im[:, cols] += a_im

        @pl.when(jnp.logical_and(i == n_tiles - 1, jnp.logical_not(second)))
        def _():
            p_re, p_im = _complex_power(lre_ref[...], lim_ref[...], steps)
            _chunk_carry(st_re, st_im, g0_re, g0_im, p_re, -p_im, n_seq, reverse=True)
            st_re[...] = g0_re[...]
            st_im[...] = g0_im[...]

        @pl.when(second)
        def _():
            u = _join_parts(u_refs)
            ub = u.astype(BF16)
            g_re = buf_re[...].astype(BF16)
            g_im = buf_im[...].astype(BF16)
            dd_ref[...] += jnp.sum(dy * u, axis=0, keepdims=True)
            for hf in range(2):
                cols = slice(hf * 1024, (hf + 1) * 1024)
                ycols = slice(hf * 256, (hf + 1) * 256)
                du_half = (_dot_nt(g_re[:, cols], bre_ref[hf]) + _dot_nt(g_im[:, cols], bim_ref[hf])
                           + d_ref[:, ycols] * dy[:, ycols])
                _split_parts(du_refs[2 * hf:2 * hf + 2], du_half)
                for q4 in range(_HALF_GROUPS // 4):
                    ch = slice(hf * 256 + q4 * 64, hf * 256 + (q4 + 1) * 64)
                    st = slice(hf * 1024 + q4 * 256, hf * 1024 + (q4 + 1) * 256)
                    blk = (hf, slice(q4 * 64, (q4 + 1) * 64), slice(q4 * 256, (q4 + 1) * 256))
                    dbre_ref[blk] += _dot_tn(ub[:, ch], g_re[:, st])
                    dbim_ref[blk] += _dot_tn(ub[:, ch], g_im[:, st])
                    dcre_ref[blk] += _dot_tn(dyb[:, ch], hre_ref[:, st])
                    dcim_ref[blk] -= _dot_tn(dyb[:, ch], him_ref[:, st])

        @pl.when(jnp.logical_and(i == n_tiles - 1, second))
        def _():
            dlre_ref[...] = jnp.sum(acc_re[...], axis=0, keepdims=True)
            dlim_ref[...] = jnp.sum(acc_im[...], axis=0, keepdims=True)

    tile = lambda w: pl.BlockSpec((tile_rows, w), lambda p, i: (n_tiles - 1 - i, 0))
    second_tile = lambda w: pl.BlockSpec((tile_rows, w), lambda p, i: (n_tiles - 1 - i * p, 0))
    cm = _full(_CM_SHAPE)
    row = _full((1, SSM_LANES))
    outs = _call(
        body, name="s5_scan_bwd", grid=(2, n_tiles),
        in_specs=[tile(LANES)] * _SCAN_PARTS + [second_tile(LANES)] * _SCAN_PARTS
        + [second_tile(SSM_LANES), second_tile(SSM_LANES), cm, cm, cm, cm, row, row, _full((1, 512))],
        out_specs=[second_tile(LANES)] * _SCAN_PARTS + [cm, cm, cm, cm, row, row, _full((1, 512))],
        out_shape=(_part_shapes(rows) + [_sds(_CM_SHAPE)] * 4 + [_sds((1, SSM_LANES))] * 2 + [_sds((1, 512))]),
        scratch_shapes=[pltpu.VMEM((slab, SSM_LANES), F32)] * 6 + [pltpu.VMEM((tile_rows, SSM_LANES), F32)] * 2,
        compiler_params=_params(48, ("arbitrary", "arbitrary")),
    )(*dy_parts, *u_parts, h_re, h_im, bt_re, bt_im, cm_re, cm_im, lbar_re, lbar_im, d_row)
    return (outs[:_SCAN_PARTS],) + tuple(outs[_SCAN_PARTS:])


def _glu_gate(gl, a, zs):
    return gl * jax.nn.sigmoid(a) * _silu(zs)


def _glu_fwd(y_parts, zs, w_glu, b_glu, n_seq, seq):
    rows = zs.shape[0]
    tm = 512
    slab, steps, _, _ = _scan_geometry(n_seq, seq)

    def body(*refs):
        y_refs, (zs_ref, w_ref, b_ref, o_ref) = refs[:_SCAN_PARTS], refs[_SCAN_PARTS:]
        y = _load_chunks(y_refs, pl.program_id(0) * (tm // steps), tm // steps, steps, slab)
        gl = jax.nn.gelu(y)
        a = _dot(gl.astype(BF16), w_ref[...]) + b_ref[...]
        o_ref[...] = _glu_gate(gl, a, zs_ref[...]).astype(BF16)

    return _call(
        body, name="glu_fwd", grid=(rows // tm,),
        in_specs=_whole_parts(rows) + [_rows(tm, 512), _full((512, 512)), _full((1, 512))],
        out_specs=_rows(tm, 512), out_shape=_sds((rows, 512), BF16),
        compiler_params=_params(32, ("arbitrary",)),
    )(*y_parts, zs, w_glu, b_glu)


def _glu_bwd(y_parts, zs, d_out, w_glu, b_glu, n_seq, seq):
    rows = zs.shape[0]
    tm = 512
    slab, steps, _, _ = _scan_geometry(n_seq, seq)

    def body(*refs):
        y_refs, (zs_ref, d_ref, w_ref, b_ref), refs = refs[:_SCAN_PARTS], refs[_SCAN_PARTS:_SCAN_PARTS + 4], refs[_SCAN_PARTS + 4:]
        dy_refs, (dzs_ref, dw_ref, db_ref) = refs[:_SCAN_PARTS], refs[_SCAN_PARTS:]
        first_chunk = pl.program_id(0) * (tm // steps)

        @pl.when(pl.program_id(0) == 0)
        def _():
            dw_ref[...] = jnp.zeros_like(dw_ref)
            db_ref[...] = jnp.zeros_like(db_ref)

        gl, gelu_vjp = jax.vjp(jax.nn.gelu, _load_chunks(y_refs, first_chunk, tm // steps, steps, slab))
        glb = gl.astype(BF16)
        a = _dot(glb, w_ref[...]) + b_ref[...]
        _, gate_vjp = jax.vjp(_glu_gate, gl, a, zs_ref[...])
        d_gl, d_a, d_zs = gate_vjp(d_ref[...])
        dab = d_a.astype(BF16)
        d_gl = d_gl + _dot_nt(dab, w_ref[...])
        _store_chunks(dy_refs, first_chunk, gelu_vjp(d_gl)[0], steps, slab)
        dzs_ref[...] = d_zs.astype(BF16)
        dw_ref[...] += _dot_tn(glb, dab)
        db_ref[...] += jnp.sum(d_a, axis=0, keepdims=True)

    *dy_parts, dzs, dw, db = _call(
        body, name="glu_bwd", grid=(rows // tm,),
        in_specs=_whole_parts(rows) + [_rows(tm, 512), _rows(tm, 512), _full((512, 512)), _full((1, 512))],
        out_specs=_whole_parts(rows) + [_rows(tm, 512), _full((512, 512)), _full((1, 512))],
        out_shape=_part_shapes(rows) + [_sds((rows, 512), BF16), _sds((512, 512)), _sds((1, 512))],
        compiler_params=_params(40, ("arbitrary",)),
    )(*y_parts, zs, d_out, w_glu, b_glu)
    return dy_parts, dzs, dw, db


_GROUP_ROWS = Q_PER_KV * BLOCK
_BLOCK_SHIFT = BLOCK.bit_length() - 1


def _attn_bias(j):
    row = _iota((_GROUP_ROWS, BLOCK), 0)
    dist_cur = (row & (BLOCK - 1)) - _iota((_GROUP_ROWS, BLOCK), 1)
    dist_prev = dist_cur + BLOCK
    head = row >> _BLOCK_SHIFT
    slope = jnp.zeros((_GROUP_ROWS, BLOCK), F32)
    for g in range(Q_PER_KV):
        slope = jnp.where(head == g, 2.0 ** (-(j * Q_PER_KV + g + 1)), slope)
    bias_cur = jnp.where(dist_cur >= 0, -slope * dist_cur.astype(F32), -jnp.inf)
    bias_prev = jnp.where(dist_prev < WINDOW, -slope * dist_prev.astype(F32), -jnp.inf)
    return bias_cur, bias_prev


_ATTN_BIAS_SCRATCH = pltpu.VMEM((KV_HEADS, 2, _GROUP_ROWS, BLOCK), F32)


def _fill_attn_bias(bias_ref):
    @pl.when(jnp.logical_and(pl.program_id(0) == 0, pl.program_id(1) == 0))
    def _():
        for j in range(KV_HEADS):
            bias_ref[j, 0], bias_ref[j, 1] = _attn_bias(j)


def _stack_heads(x, j):
    heads = range(j * Q_PER_KV, (j + 1) * Q_PER_KV)
    return jnp.concatenate([x[:, h * HEAD_DIM:(h + 1) * HEAD_DIM] for h in heads], axis=0)


def _stack_columns(x, j):
    heads = range(j * Q_PER_KV, (j + 1) * Q_PER_KV)
    return jnp.concatenate([jnp.broadcast_to(x[:, h:h + 1], (BLOCK, 1)) for h in heads], axis=0)


def _attn_fwd(q, k, v, za, sinks, n_seq, seq):
    nb = seq // BLOCK
    rows = q.shape[0]

    def body(q_ref, kc_ref, kp_ref, vc_ref, vp_ref, za_ref, sk_ref, o_ref, ao_ref, lse_ref, bias_ref):
        _fill_attn_bias(bias_ref)
        has_prev = pl.program_id(1) > 0
        q_all = q_ref[...]
        for j in range(KV_HEADS):
            js = slice(j * HEAD_DIM, (j + 1) * HEAD_DIM)
            bias_c, bias_p = bias_ref[j, 0], bias_ref[j, 1]
            q4 = _stack_heads(q_all, j)
            sc = _dot_nt(q4, kc_ref[:, js]) + bias_c
            sp = _dot_nt(q4, kp_ref[:, js]) + jnp.where(has_prev, bias_p, -jnp.inf)
            sink = _stack_columns(sk_ref[...], j)
            m = jnp.maximum(jnp.max(jnp.maximum(sc, sp), axis=-1, keepdims=True), sink)
            ec = jnp.exp(sc - m)
            ep = jnp.exp(sp - m)
            den = jnp.sum(ec + ep, axis=-1, keepdims=True) + jnp.exp(sink - m)
            inv = 1.0 / den
            o4 = _dot((ec * inv).astype(BF16), vc_ref[:, js]) + _dot((ep * inv).astype(BF16), vp_ref[:, js])
            lse4 = m + jnp.log(den)
            for g in range(Q_PER_KV):
                h = j * Q_PER_KV + g
                o_ref[:, h * HEAD_DIM:(h + 1) * HEAD_DIM] = o4[g * BLOCK:(g + 1) * BLOCK]
                lse_ref[:, h:h + 1] = lse4[g * BLOCK:(g + 1) * BLOCK]
        ao_ref[...] = (o_ref[...] * _silu(za_ref[...])).astype(BF16)

    cur = lambda w: pl.BlockSpec((BLOCK, w), lambda b, n: (b * nb + n, 0))
    prev = lambda w: pl.BlockSpec((BLOCK, w), lambda b, n: (b * nb + jnp.maximum(n - 1, 0), 0))
    return _call(
        body, name="attn_fwd", grid=(n_seq, nb),
        in_specs=[cur(512), cur(128), prev(128), cur(128), prev(128), cur(512), _full((1, N_HEADS))],
        out_specs=[cur(512), cur(512), cur(N_HEADS)],
        out_shape=[_sds((rows, 512)), _sds((rows, 512), BF16), _sds((rows, N_HEADS))],
        scratch_shapes=[_ATTN_BIAS_SCRATCH], compiler_params=_params(32, ("arbitrary", "arbitrary")),
    )(q, k, k, v, v, za, sinks)


def _attn_bwd(q, k, v, za, o, lse, d_ao, sinks, n_seq, seq):
    nb = seq // BLOCK
    rows = q.shape[0]

    def body(q_ref, kc_ref, kp_ref, vc_ref, vp_ref, za_ref, o_ref, lse_ref, d_ref, sk_ref,
             dq_ref, dk_ref, dv_ref, dza_ref, dsk_ref, bias_ref, dk_carry, dv_carry):
        n = nb - 1 - pl.program_id(1)
        _fill_attn_bias(bias_ref)

        @pl.when(jnp.logical_and(pl.program_id(0) == 0, pl.program_id(1) == 0))
        def _():
            dsk_ref[...] = jnp.zeros_like(dsk_ref)
            dk_carry[...] = jnp.zeros_like(dk_carry)
            dv_carry[...] = jnp.zeros_like(dv_carry)

        has_prev = n > 0
        has_next = n + 1 < nb

        _, gate_vjp = jax.vjp(lambda o_, z_: o_ * _silu(z_), o_ref[...], za_ref[...])
        d_o, d_za = gate_vjp(d_ref[...])
        dza_ref[...] = d_za.astype(BF16)
        q_all = q_ref[...]
        lse_all = lse_ref[...]

        for j in range(KV_HEADS):
            js = slice(j * HEAD_DIM, (j + 1) * HEAD_DIM)
            kc, kp, vc, vp = kc_ref[:, js], kp_ref[:, js], vc_ref[:, js], vp_ref[:, js]
            bias_c, bias_p = bias_ref[j, 0], bias_ref[j, 1]
            q4 = _stack_heads(q_all, j)
            do4b = _stack_heads(d_o, j).astype(BF16)
            lse4 = _stack_columns(lse_all, j)
            pc = jnp.exp(_dot_nt(q4, kc) + bias_c - lse4)
            pp = jnp.exp(_dot_nt(q4, kp) + jnp.where(has_prev, bias_p, -jnp.inf) - lse4)
            dpc = _dot_nt(do4b, vc)
            dpp = _dot_nt(do4b, vp)
            delta = jnp.sum(pc * dpc + pp * dpp, axis=-1, keepdims=True)
            dsc = (pc * (dpc - delta)).astype(BF16)
            dsp = (pp * (dpp - delta)).astype(BF16)
            dq4 = ((_dot(dsc, kc) + _dot(dsp, kp)) * ATTN_SCALE).astype(BF16)
            sink_loss = jnp.exp(_stack_columns(sk_ref[...], j) - lse4) * delta
            for g in range(Q_PER_KV):
                h = j * Q_PER_KV + g
                dq_ref[:, h * HEAD_DIM:(h + 1) * HEAD_DIM] = dq4[g * BLOCK:(g + 1) * BLOCK]
                dsk_ref[0:1, h:h + 1] -= jnp.sum(sink_loss[g * BLOCK:(g + 1) * BLOCK], axis=0, keepdims=True)
            dk = _dot_tn(dsc, q4) + jnp.where(has_next, dk_carry[j], 0.0)
            dv = _dot_tn(pc.astype(BF16), do4b) + jnp.where(has_next, dv_carry[j], 0.0)
            dk_carry[j] = _dot_tn(dsp, q4)
            dv_carry[j] = _dot_tn(pp.astype(BF16), do4b)
            dk_ref[:, js] = dk.astype(BF16)
            dv_ref[:, js] = dv.astype(BF16)

    cur = lambda w: pl.BlockSpec((BLOCK, w), lambda b, s: (b * nb + nb - 1 - s, 0))
    prev = lambda w: pl.BlockSpec((BLOCK, w), lambda b, s: (b * nb + jnp.maximum(nb - 2 - s, 0), 0))
    return _call(
        body, name="attn_bwd", grid=(n_seq, nb),
        in_specs=[cur(512), cur(128), prev(128), cur(128), prev(128), cur(512), cur(512), cur(N_HEADS), cur(512),
                  _full((1, N_HEADS))],
        out_specs=[cur(512), cur(128), cur(128), cur(512), _full((1, N_HEADS))],
        out_shape=[_sds((rows, 512), BF16), _sds((rows, 128), BF16), _sds((rows, 128), BF16),
                   _sds((rows, 512), BF16), _sds((1, N_HEADS))],
        scratch_shapes=[_ATTN_BIAS_SCRATCH, pltpu.VMEM((KV_HEADS, BLOCK, HEAD_DIM), F32),
                        pltpu.VMEM((KV_HEADS, BLOCK, HEAD_DIM), F32)],
        compiler_params=_params(32, ("arbitrary", "arbitrary")),
    )(q, k, k, v, v, za, o, lse, d_ao, sinks)


def _tail(ssm_out, attn_out, x2d, p2d, target, w_out, g2, w_gate, b_gate, w_proj):
    rows = x2d.shape[0]
    tm = 512

    def body(so_ref, ao_ref, x_ref, p_ref, t_ref, wo_ref, g2_ref, wg_ref, bg_ref, wp_ref,
             dh1_ref, dso_ref, dao_ref, dwo_ref, dwg_ref, dwp_ref, dbg_ref, dg2_ref, loss_ref):
        @pl.when(pl.program_id(0) == 0)
        def _():
            for ref in (dwo_ref, dwg_ref, dwp_ref, dbg_ref, dg2_ref, loss_ref):
                ref[...] = jnp.zeros_like(ref)

        cat = jnp.concatenate([so_ref[...], ao_ref[...]], axis=1)
        g2 = g2_ref[...]
        mixed = _dot(cat, wo_ref[...])
        r = lax.rsqrt(jnp.mean(mixed * mixed, axis=-1, keepdims=True) + EPS)
        mr = mixed * r
        h1 = x_ref[...] + mr * g2
        h1b = h1.astype(BF16)
        gate = jax.nn.sigmoid(_dot(h1b, wg_ref[...]) + bg_ref[...])
        pb = p_ref[...].astype(BF16)
        wp_blocks = [slice(j * D_PLE, (j + 1) * D_PLE) for j in range(N_CHIPS)]
        pp = jnp.concatenate([_dot(pb, wp_ref[blk, :]) for blk in wp_blocks], axis=1)
        err = h1 + gate * pp - t_ref[...]
        loss_ref[...] += 0.5 * jnp.sum(jnp.mean(err * err, axis=-1, keepdims=True), axis=0, keepdims=True)

        dh2 = err * (1.0 / D_MODEL)
        d_glin = dh2 * pp * gate * (1.0 - gate)
        d_glin_b = d_glin.astype(BF16)
        dwg_ref[...] += _dot_tn(h1b, d_glin_b)
        dbg_ref[...] += jnp.sum(d_glin, axis=0, keepdims=True)
        d_pp = (dh2 * gate).astype(BF16)
        for blk in wp_blocks:
            dwp_ref[blk, :] += _dot_tn(pb, d_pp[:, blk])
        dh1 = dh2 + _dot_nt(d_glin_b, wg_ref[...])
        dh1_ref[...] = dh1
        dg2_ref[...] += jnp.sum(dh1 * mr, axis=0, keepdims=True)
        a_ = dh1 * g2
        d_mixed = (r * a_ - mr * (r * jnp.mean(a_ * mr, axis=-1, keepdims=True))).astype(BF16)
        dwo_ref[...] += _dot_tn(cat, d_mixed)
        d_cat = _dot_nt(d_mixed, wo_ref[...])
        dso_ref[...] = d_cat[:, 0:512]
        dao_ref[...] = d_cat[:, 512:1024]

    return _call(
        body, name="tail_fwd_bwd", grid=(rows // tm,),
        in_specs=[_rows(tm, 512), _rows(tm, 512), _rows(tm, D_MODEL), _rows(tm, D_PLE), _rows(tm, D_MODEL),
                  _full((D_MODEL, D_MODEL)), _full((1, D_MODEL)), _full((D_MODEL, D_MODEL)), _full((1, D_MODEL)),
                  _full((N_CHIPS * D_PLE, D_PLE))],
        out_specs=[_rows(tm, D_MODEL), _rows(tm, 512), _rows(tm, 512), _full((D_MODEL, D_MODEL)),
                   _full((D_MODEL, D_MODEL)), _full((N_CHIPS * D_PLE, D_PLE)), _full((1, D_MODEL)), _full((1, D_MODEL)),
                   _full((1, 1))],
        out_shape=[_sds((rows, D_MODEL)), _sds((rows, 512)), _sds((rows, 512)), _sds((D_MODEL, D_MODEL)),
                   _sds((D_MODEL, D_MODEL)), _sds((N_CHIPS * D_PLE, D_PLE)), _sds((1, D_MODEL)), _sds((1, D_MODEL)),
                   _sds((1, 1))],
        compiler_params=_params(52, ("arbitrary",)),
    )(ssm_out, attn_out, x2d, p2d, target, w_out, g2, w_gate, b_gate, w_proj)


def _local_step(x, p, target, pre_norm_g, w_in_t, s5_params, s5_operands, ssm_d, w_glu, b_glu, sinks, w_out,
                post_norm_g, w_proj, w_gate, b_gate):
    n_seq, seq, _ = x.shape
    rows = n_seq * seq
    x2d = x.reshape(rows, D_MODEL)
    p2d = p.reshape(rows, D_PLE)
    t2d = target.reshape(rows, D_MODEL)

    l_re, l_im, bt_re, bt_im, cm_re, cm_im = s5_operands

    u_scan, zs, q, k, v, za = _in_proj(x2d, pre_norm_g, w_in_t, n_seq, seq)
    y_scan, h_re, h_im = _s5_scan_fwd(u_scan, bt_re, bt_im, cm_re, cm_im, l_re, l_im, ssm_d, n_seq, seq)
    ssm_out = _glu_fwd(y_scan, zs, w_glu, b_glu, n_seq, seq)
    o, attn_out, lse = _attn_fwd(q, k, v, za, sinks, n_seq, seq)

    dh1, d_so, d_ao, d_w_out, d_w_gate, d_w_proj, d_b_gate, d_g2, loss = _tail(
        ssm_out, attn_out, x2d, p2d, t2d, w_out, post_norm_g, w_gate, b_gate, w_proj)

    dq, dk, dv, dza, d_sinks = _attn_bwd(q, k, v, za, o, lse, d_ao, sinks, n_seq, seq)
    dy_scan, dzs, d_w_glu, d_b_glu = _glu_bwd(y_scan, zs, d_so, w_glu, b_glu, n_seq, seq)
    du_scan, d_bt_re, d_bt_im, d_cm_re, d_cm_im, d_l_re, d_l_im, d_d = _s5_scan_bwd(
        dy_scan, u_scan, h_re, h_im, bt_re, bt_im, cm_re, cm_im, l_re, l_im, ssm_d, n_seq, seq)
    d_lam_re, d_lam_im, d_log_step, d_b_re, d_b_im, d_c_re, d_c_im = _s5_params_bwd(
        s5_params, (d_l_re, d_l_im, d_bt_re, d_bt_im, d_cm_re, d_cm_im))

    grad_x, d_w_in_t, d_g1 = _in_proj_bwd(x2d, dh1, pre_norm_g, w_in_t, du_scan, dzs, dq, dk, dv, dza, n_seq, seq)
    grads = dict(
        pre_norm_g=d_g1, w_in=d_w_in_t, ssm_lam_re=d_lam_re, ssm_lam_im=d_lam_im, ssm_log_step=d_log_step,
        ssm_b_re=d_b_re, ssm_b_im=d_b_im, ssm_c_re=d_c_re, ssm_c_im=d_c_im, ssm_d=d_d, ssm_w_glu=d_w_glu,
        ssm_b_glu=d_b_glu, attn_sinks=d_sinks, w_out=d_w_out, post_norm_g=d_g2, pl_w_proj=d_w_proj,
        pl_w_gate=d_w_gate, pl_b_gate=d_b_gate)
    return grad_x.reshape(x.shape), loss, grads


_BIG = ("w_in", "ssm_w_glu", "w_out", "pl_w_proj", "pl_w_gate")
_BIG_SHARD = {"w_in": (D_IN // N_CHIPS, D_MODEL), "ssm_w_glu": (D_SSM // N_CHIPS, D_SSM),
              "w_out": (D_MODEL // N_CHIPS, D_MODEL), "pl_w_proj": (D_PLE, D_MODEL // N_CHIPS),
              "pl_w_gate": (D_MODEL // N_CHIPS, D_MODEL)}
_SMALL = {"pre_norm_g": (1, D_MODEL), "ssm_lam_re": (SSM_GROUPS, SSM_STATE), "ssm_lam_im": (SSM_GROUPS, SSM_STATE),
          "ssm_log_step": (1, SSM_GROUPS), "ssm_b_re": (D_SSM, SSM_STATE), "ssm_b_im": (D_SSM, SSM_STATE),
          "ssm_c_re": (D_SSM, SSM_STATE), "ssm_c_im": (D_SSM, SSM_STATE), "ssm_d": (1, D_SSM), "ssm_b_glu": (1, D_SSM),
          "attn_sinks": (1, N_HEADS), "post_norm_g": (1, D_MODEL), "pl_b_gate": (1, D_MODEL)}
_VEC_ROWS = ("pre_norm_g", "post_norm_g", "pl_b_gate", "ssm_d", "ssm_b_glu", "attn_sinks", "ssm_log_step", "loss")
_SMALL_GROUPS = (
    ("vec", (8, D_MODEL), tuple((name, r) for r, name in enumerate(_VEC_ROWS))),
    ("lam", (2 * SSM_GROUPS, SSM_STATE), (("ssm_lam_re", 0), ("ssm_lam_im", SSM_GROUPS))),
)
_SMALL_EARLY = ("ssm_b_re", "ssm_b_im", "ssm_c_re", "ssm_c_im")
_SMALL_ORDER = tuple(name for _, _, members in _SMALL_GROUPS for name, _ in members) + _SMALL_EARLY
_WEIGHT_ORDER = ("pre_norm_g", "w_in", "ssm_lam_re", "ssm_lam_im", "ssm_log_step", "ssm_b_re", "ssm_b_im", "ssm_c_re",
                 "ssm_c_im", "ssm_d", "ssm_w_glu", "ssm_b_glu", "attn_sinks", "w_out", "post_norm_g", "pl_w_proj",
                 "pl_w_gate", "pl_b_gate")


def _small_shape(name):
    return (1, 1) if name == "loss" else _SMALL[name]


def _to_kernel_form(name, a):
    a = a[0]
    if name == "w_in":
        return a.T
    if name in ("ssm_b_re", "ssm_b_im"):
        a = a.transpose(0, 2, 1)
    return a.reshape(_SMALL[name]) if name in _SMALL else a


def _from_kernel_form(name, a, shape):
    if name == "w_in":
        a = a.T
    if name in ("ssm_b_re", "ssm_b_im"):
        a = a.reshape(SSM_GROUPS, SSM_GROUP_CH, SSM_STATE).transpose(0, 2, 1)
    return a.reshape(shape)


def _mesh_place():
    x, y, c = lax.axis_index("x"), lax.axis_index("y"), lax.axis_index("c")
    other_chips = ((1 - x, y), (x, 1 - y), (1 - x, 1 - y))
    return x, y, c, other_chips


def _gather_copies(s_refs, g_refs, send_sems, recv_sems, local_sems):
    x, y, c, other_chips = _mesh_place()
    started = []
    for i, (s_ref, g_ref) in enumerate(zip(s_refs, g_refs)):
        rows = s_ref.shape[0]
        half = rows // 2

        def block(chip, g_ref=g_ref, rows=rows, half=half):
            return g_ref.at[pl.ds((2 * chip[0] + chip[1]) * rows + c * half, half), :]

        def copy(k, chip, to, src=None, i=i, block=block):
            return pltpu.make_async_remote_copy(
                src_ref=block(chip) if src is None else src, dst_ref=block(chip), send_sem=send_sems.at[6 * i + k],
                recv_sem=recv_sems.at[6 * i + k], device_id=to, device_id_type=MESH)

        own = pltpu.make_async_copy(s_ref, g_ref.at[pl.ds((2 * x + y) * rows, rows), :], local_sems.at[i])
        own.start()
        first = [copy(k, (x, y), (*chip, c), src=s_ref.at[pl.ds(c * half, half), :])
                 for k, chip in enumerate(other_chips)]
        for cp in first:
            cp.start()
        passed = [copy(3 + k, chip, (x, y, 1 - c)) for k, chip in enumerate(other_chips)]
        started.append((own, first, passed))
    for own, first, passed in started:
        for k in range(3):
            first[k].wait_recv()
            passed[k].start()
    for own, first, passed in started:
        for k in range(3):
            passed[k].wait_recv()
        for cp in first + passed:
            cp.wait_send()
        own.wait()


def _gather_semaphores(n_t):
    return [pltpu.SemaphoreType.DMA((6 * n_t,)), pltpu.SemaphoreType.DMA((6 * n_t,)), pltpu.SemaphoreType.DMA((n_t,))]


def _gather_weights(shards):
    n_t = len(shards)

    def body(*refs):
        _gather_copies(refs[:n_t], refs[n_t:2 * n_t], *refs[2 * n_t + 1:])
        refs[2 * n_t][...] = jnp.zeros_like(refs[2 * n_t])

    any_spec = pl.BlockSpec(memory_space=pl.ANY)
    *full, done = _call(
        body, name="gather_weights", in_specs=[any_spec] * n_t,
        out_specs=[any_spec] * n_t + [pl.BlockSpec(memory_space=pltpu.VMEM)],
        out_shape=[_sds((N_CHIPS * s.shape[0], s.shape[1]), s.dtype) for s in shards]
        + [jax.ShapeDtypeStruct((8, LANES), F32)],
        scratch_shapes=_gather_semaphores(n_t),
    )(*shards)
    return full, done[0, 0]


def _gather_weights_beside(shards, name, collective_id):
    n_t = len(shards)
    hbm = pltpu.MemorySpace.HBM
    s_refs = [jax.new_ref(s, memory_space=hbm) for s in shards]
    g_refs = [jax.empty_ref(jax.ShapeDtypeStruct((N_CHIPS * s.shape[0], s.shape[1]), s.dtype), memory_space=hbm)
              for s in shards]

    def launch(send_sems, recv_sems, local_sems):
        x, y, c, other_chips = _mesh_place()
        peers = [(*chip, c) for chip in other_chips] + [(x, y, 1 - c)]
        barrier = pltpu.get_barrier_semaphore()
        for peer in peers:
            pl.semaphore_signal(barrier, inc=1, device_id=peer, device_id_type=MESH)
        pl.semaphore_wait(barrier, len(peers))
        _gather_copies(s_refs, g_refs, send_sems, recv_sems, local_sems)

    pl.kernel(launch, mesh=plsc.ScalarSubcoreMesh(axis_name="sequencer", num_cores=1), name=name,
              scratch_types=_gather_semaphores(n_t), compiler_params=pltpu.CompilerParams(collective_id=collective_id))()
    return [g[...] for g in g_refs]


_RELATIONS = tuple(((r >> 2) & 1, (r >> 1) & 1, r & 1) for r in range(1, 8))


def _related(place, relation):
    return tuple(1 - a if flip else a for a, flip in zip(place, relation))


def _scatter_beside(mats):
    hbm = pltpu.MemorySpace.HBM
    src_refs = [jax.new_ref(a, memory_space=hbm) for a in mats]
    land_refs = [jax.empty_ref(jax.ShapeDtypeStruct((7, a.shape[0] // 8, a.shape[1]), a.dtype), memory_space=hbm)
                 for a in mats]

    def launch(send_sems, recv_sems):
        me = (lax.axis_index("x"), lax.axis_index("y"), lax.axis_index("c"))
        peers = [_related(me, rel) for rel in _RELATIONS]
        barrier = pltpu.get_barrier_semaphore()
        for peer in peers:
            pl.semaphore_signal(barrier, inc=1, device_id=peer, device_id_type=MESH)
        pl.semaphore_wait(barrier, len(peers))
        copies = []
        for i, (src, land) in enumerate(zip(src_refs, land_refs)):
            hr = land.shape[1]
            for k, (tx, ty, tc) in enumerate(peers):
                rows = pl.ds((2 * tx + ty) * 2 * hr + tc * hr, hr)
                copies.append(pltpu.make_async_remote_copy(
                    src_ref=src.at[rows, :], dst_ref=land.at[k], send_sem=send_sems.at[7 * i + k],
                    recv_sem=recv_sems.at[7 * i + k], device_id=(tx, ty, tc), device_id_type=MESH))
                copies[-1].start()
        for cp in copies:
            cp.wait()

    n_sems = 7 * len(mats)
    pl.kernel(launch, mesh=plsc.ScalarSubcoreMesh(axis_name="sequencer", num_cores=1), name="scatter_beside",
              scratch_types=[pltpu.SemaphoreType.DMA((n_sems,)), pltpu.SemaphoreType.DMA((n_sems,))],
              compiler_params=pltpu.CompilerParams(collective_id=2))()
    return [ref[...] for ref in land_refs]


def _broadcast_beside(arrays):
    hbm = pltpu.MemorySpace.HBM
    src_refs = [jax.new_ref(a, memory_space=hbm) for a in arrays]
    land_refs = [jax.empty_ref(jax.ShapeDtypeStruct((len(_RELATIONS),) + a.shape, a.dtype), memory_space=hbm)
                 for a in arrays]

    def launch(send_sems, recv_sems):
        me = (lax.axis_index("x"), lax.axis_index("y"), lax.axis_index("c"))
        peers = [_related(me, rel) for rel in _RELATIONS]
        barrier = pltpu.get_barrier_semaphore()
        for peer in peers:
            pl.semaphore_signal(barrier, inc=1, device_id=peer, device_id_type=MESH)
        pl.semaphore_wait(barrier, len(peers))
        copies = []
        for i, (src, land) in enumerate(zip(src_refs, land_refs)):
            for k, peer in enumerate(peers):
                copies.append(pltpu.make_async_remote_copy(
                    src_ref=src, dst_ref=land.at[k], send_sem=send_sems.at[7 * i + k],
                    recv_sem=recv_sems.at[7 * i + k], device_id=peer, device_id_type=MESH))
                copies[-1].start()
        for cp in copies:
            cp.wait()

    n_sems = 7 * len(arrays)
    pl.kernel(launch, mesh=plsc.ScalarSubcoreMesh(axis_name="sequencer", num_cores=1), name="broadcast_beside",
              scratch_types=[pltpu.SemaphoreType.DMA((n_sems,)), pltpu.SemaphoreType.DMA((n_sems,))],
              compiler_params=pltpu.CompilerParams(collective_id=3))()
    return [ref[...] for ref in land_refs]


def _exchange_grads(big, small, landed, landed_small):
    n_t = len(big)
    n_g = len(_SMALL_GROUPS)
    names = _SMALL_ORDER
    halves = [(b.shape[0] // N_CHIPS // 2, b.shape[1]) for b in big]
    early = sorted(landed)
    late = [i for i in range(n_t) if i not in landed]
    n_sems = 4 * n_g + 7 * len(late) + n_t
    small_sem0, block_sem0 = n_t, n_t + len(names)
    early_sem0 = block_sem0 + N_CHIPS * len(late)
    landed_sem0 = early_sem0 + 2 * len(early)
    early_small = [n for n in names if n in landed_small]

    def body(*refs):
        pos = 0

        def take(n):
            nonlocal pos
            pos += n
            return refs[pos - n:pos]

        big_refs, small_refs = take(n_t), dict(zip(names, take(len(names))))
        land_refs = dict(zip(early, take(len(early))))
        land_small_refs = dict(zip(early_small, take(len(early_small))))
        out_refs, small_out_refs = take(n_t), dict(zip(names, take(len(names))))
        per_late = lambda: dict(zip(late, take(len(late))))
        ga, gb, pme, send_b, recv_b = per_late(), per_late(), take(n_t), per_late(), per_late()
        own_e, land_e = dict(zip(early, take(len(early)))), dict(zip(early, take(len(early))))
        land_s = dict(zip(early_small, take(len(early_small))))
        s_own, s_sib, s_chips, s_pair = take(n_g), take(n_g), take(n_g), take(n_g)
        stage = dict(zip(names, take(len(names))))
        send_sems, recv_sems, local_sems = take(3)
        x, y, c, other_chips = _mesh_place()
        me = 2 * x + y
        sibling = (x, y, 1 - c)
        sem_at = iter(range(n_sems))

        def remote(src, dst, to):
            k = next(sem_at)
            return pltpu.make_async_remote_copy(src_ref=src, dst_ref=dst, send_sem=send_sems.at[k],
                                                recv_sem=recv_sems.at[k], device_id=to, device_id_type=MESH)

        loads = [pltpu.make_async_copy(small_refs[name], stage[name], local_sems.at[small_sem0 + a])
                 for a, name in enumerate(names)]
        landed_loads = [pltpu.make_async_copy(land_small_refs[name], land_s[name], local_sems.at[landed_sem0 + a])
                        for a, name in enumerate(early_small)]
        for cp in loads + landed_loads:
            cp.start()
        for cp in loads:
            cp.wait()
        small_swaps = []
        for gi, (_, _, members) in enumerate(_SMALL_GROUPS):
            s_own[gi][...] = jnp.zeros_like(s_own[gi])
            for name, r0 in members:
                r, n = _small_shape(name)
                s_own[gi][r0:r0 + r, 0:n] = stage[name][...]
            small_swaps.append(remote(s_own[gi], s_sib[gi], sibling))
            small_swaps[gi].start()
        order = sorted(late, key=lambda i: halves[i][0] * halves[i][1])
        own_loads, big_swaps = {}, {}
        for i in order:
            hr = halves[i][0]
            own_loads[i], big_swaps[i] = [], []
            for j in range(N_CHIPS):
                mine = big_refs[i].at[pl.ds(j * 2 * hr + c * hr, hr), :]
                theirs = big_refs[i].at[pl.ds(j * 2 * hr + (1 - c) * hr, hr), :]
                sem = local_sems.at[block_sem0 + N_CHIPS * late.index(i) + j]
                own_loads[i].append(pltpu.make_async_copy(mine, ga[i].at[j], sem))
                own_loads[i][j].start()
                big_swaps[i].append(remote(theirs, gb[i].at[j], sibling))
                big_swaps[i][j].start()
        early_loads = {}
        for e, i in enumerate(early):
            hr = halves[i][0]
            mine = big_refs[i].at[pl.ds(me * 2 * hr + c * hr, hr), :]
            early_loads[i] = [pltpu.make_async_copy(mine, own_e[i], local_sems.at[early_sem0 + 2 * e]),
                              pltpu.make_async_copy(land_refs[i], land_e[i], local_sems.at[early_sem0 + 2 * e + 1])]
            for cp in early_loads[i]:
                cp.start()
        small_sends = []
        for gi in range(n_g):
            small_swaps[gi].wait_recv()
            s_pair[gi][...] = s_own[gi][...] + s_sib[gi][...]
            small_sends.append([remote(s_pair[gi], s_chips[gi].at[k], (*chip, c)) for k, chip in enumerate(other_chips)])
            for cp in small_sends[gi]:
                cp.start()

        def pair_sum(i, j):
            return ga[i][j] + gb[i][j]

        big_sends = {}
        for i in order:
            for j in range(N_CHIPS):
                own_loads[i][j].wait()
                big_swaps[i][j].wait_recv()
            big_sends[i] = []
            for k, chip in enumerate(other_chips):
                send_b[i][k] = pair_sum(i, 2 * chip[0] + chip[1]).astype(BF16)
                big_sends[i].append(remote(send_b[i].at[k], recv_b[i].at[k], (*chip, c)))
                big_sends[i][k].start()
        last_swaps, keeps = {}, {}
        for i in early + order:
            hr = halves[i][0]
            if i in landed:
                for cp in early_loads[i]:
                    cp.wait()
                total = own_e[i][...]
                for k in range(len(_RELATIONS)):
                    total = total + land_e[i][k]
                pme[i][...] = total
            else:
                for k in range(3):
                    big_sends[i][k].wait_recv()
                pme[i][...] = ((pair_sum(i, me) + recv_b[i][0].astype(F32)) + recv_b[i][1].astype(F32)) + recv_b[i][2].astype(F32)
            mine = out_refs[i].at[pl.ds(c * hr, hr), :]
            keeps[i] = pltpu.make_async_copy(pme[i], mine, local_sems.at[i])
            keeps[i].start()
            last_swaps[i] = remote(pme[i], mine, sibling)
            last_swaps[i].start()

        for gi, (_, _, members) in enumerate(_SMALL_GROUPS):
            for k in range(3):
                small_sends[gi][k].wait_recv()
            total = None
            for j in range(N_CHIPS):
                rel = jnp.bitwise_xor(j, me)
                term = jnp.where(rel == 0, s_pair[gi][...], jnp.where(
                    rel == 2, s_chips[gi][0], jnp.where(rel == 1, s_chips[gi][1], s_chips[gi][2])))
                total = term if total is None else total + term
            s_sib[gi][...] = total
            for name, r0 in members:
                r, n = _small_shape(name)
                stage[name][...] = s_sib[gi][r0:r0 + r, 0:n]
        my_index = 4 * x + 2 * y + c
        for cp in landed_loads:
            cp.wait()
        for name in early_small:
            total = None
            for d in range(2 * N_CHIPS):
                rel = jnp.bitwise_xor(d, my_index)
                term = stage[name][...]
                for k in range(len(_RELATIONS)):
                    term = jnp.where(rel == k + 1, land_s[name][k], term)
                total = term if total is None else total + term
            stage[name][...] = total
        stores = [pltpu.make_async_copy(stage[name], small_out_refs[name], local_sems.at[small_sem0 + a])
                  for a, name in enumerate(names)]
        for cp in stores:
            cp.start()

        for i in range(n_t):
            last_swaps[i].wait_recv()
            keeps[i].wait()
        for cp in stores:
            cp.wait()
        groups = list(big_swaps.values()) + small_sends + list(big_sends.values())
        for cp in small_swaps + [cp for group in groups for cp in group] + list(last_swaps.values()):
            cp.wait_send()

    any_spec = pl.BlockSpec(memory_space=pl.ANY)
    small_shapes = [_sds(_small_shape(n)) for n in names]
    group_shapes = [shape for _, shape, _ in _SMALL_GROUPS]
    vmem = lambda which, dtype, lead=(): [pltpu.VMEM(lead + halves[i], dtype) for i in which]
    outs = _call(
        body, name="exchange_grads",
        in_specs=[any_spec] * (n_t + len(names) + len(early) + len(early_small)),
        out_specs=[any_spec] * (n_t + len(names)),
        out_shape=[_sds((b.shape[0] // N_CHIPS, b.shape[1])) for b in big] + small_shapes,
        scratch_shapes=(vmem(late, F32, (N_CHIPS,)) + vmem(late, F32, (N_CHIPS,)) + vmem(range(n_t), F32)
                        + vmem(late, BF16, (3,)) + vmem(late, BF16, (3,))
                        + vmem(early, F32) + vmem(early, F32, (len(_RELATIONS),))
                        + [pltpu.VMEM((len(_RELATIONS),) + _small_shape(n), F32) for n in early_small]
                        + [pltpu.VMEM(s, F32) for s in group_shapes] * 2 + [pltpu.VMEM((3,) + s, F32) for s in group_shapes]
                        + [pltpu.VMEM(s, F32) for s in group_shapes]
                        + [pltpu.VMEM(_small_shape(n), F32) for n in names]
                        + [pltpu.SemaphoreType.DMA((n_sems,)), pltpu.SemaphoreType.DMA((n_sems,)),
                           pltpu.SemaphoreType.DMA((landed_sem0 + len(early_small),))]),
        compiler_params=_params(48),
    )(*big, *[small[n] for n in names], *[landed[i] for i in early], *[landed_small[n] for n in early_small])
    return list(outs[:n_t]), dict(zip(names, outs[n_t:n_t + len(names)]))


def _adamw_update(w, g, m, v):
    m = ADAM_B1 * m + (1.0 - ADAM_B1) * g
    v = ADAM_B2 * v + (1.0 - ADAM_B2) * (g * g)
    m_hat = m / (1.0 - ADAM_B1 ** ADAM_STEP)
    v_hat = v / (1.0 - ADAM_B2 ** ADAM_STEP)
    return -ADAM_LR * (m_hat / (jnp.sqrt(v_hat) + ADAM_EPS) + ADAM_WD * w), m, v


def _adamw(w, g, m, v, grid, name):
    n_t = len(w)

    def body(*refs):
        ins, outs = refs[:4 * n_t], refs[4 * n_t:]
        for i in range(n_t):
            w_, g_, m_, v_ = [ins[a * n_t + i][...] for a in range(4)]
            vals = (g_,) + _adamw_update(w_, g_, m_, v_)
            for a in range(4):
                outs[a * n_t + i][...] = vals[a]

    specs = [pl.BlockSpec((a.shape[0] // grid, a.shape[1]), lambda i: (i, 0)) for a in w]
    shapes = [_sds(a.shape) for a in w]
    outs = _call(
        body, name=name, grid=(grid,), in_specs=specs * 4, out_specs=specs * 4, out_shape=shapes * 4,
        compiler_params=_params(40, ("arbitrary",)),
    )(*w, *g, *m, *v)
    return [outs[a * n_t:(a + 1) * n_t] for a in range(4)]


def kernel(x, p, pre_norm_g, w_in, ssm_lam_re, ssm_lam_im, ssm_log_step, ssm_b_re, ssm_b_im, ssm_c_re, ssm_c_im, ssm_d, ssm_w_glu, ssm_b_glu, attn_sinks, w_out, post_norm_g, pl_w_proj, pl_w_gate, pl_b_gate, loss_target, m_pre_norm_g, m_w_in, m_ssm_lam_re, m_ssm_lam_im, m_ssm_log_step, m_ssm_b_re, m_ssm_b_im, m_ssm_c_re, m_ssm_c_im, m_ssm_d, m_ssm_w_glu, m_ssm_b_glu, m_attn_sinks, m_w_out, m_post_norm_g, m_pl_w_proj, m_pl_w_gate, m_pl_b_gate, v_pre_norm_g, v_w_in, v_ssm_lam_re, v_ssm_lam_im, v_ssm_log_step, v_ssm_b_re, v_ssm_b_im, v_ssm_c_re, v_ssm_c_im, v_ssm_d, v_ssm_w_glu, v_ssm_b_glu, v_attn_sinks, v_w_out, v_post_norm_g, v_pl_w_proj, v_pl_w_gate, v_pl_b_gate):
    weights = dict(pre_norm_g=pre_norm_g, w_in=w_in, ssm_lam_re=ssm_lam_re, ssm_lam_im=ssm_lam_im,
                   ssm_log_step=ssm_log_step, ssm_b_re=ssm_b_re, ssm_b_im=ssm_b_im, ssm_c_re=ssm_c_re,
                   ssm_c_im=ssm_c_im, ssm_d=ssm_d, ssm_w_glu=ssm_w_glu, ssm_b_glu=ssm_b_glu, attn_sinks=attn_sinks,
                   w_out=w_out, post_norm_g=post_norm_g, pl_w_proj=pl_w_proj, pl_w_gate=pl_w_gate, pl_b_gate=pl_b_gate)
    m_in = dict(pre_norm_g=m_pre_norm_g, w_in=m_w_in, ssm_lam_re=m_ssm_lam_re, ssm_lam_im=m_ssm_lam_im,
                ssm_log_step=m_ssm_log_step, ssm_b_re=m_ssm_b_re, ssm_b_im=m_ssm_b_im, ssm_c_re=m_ssm_c_re,
                ssm_c_im=m_ssm_c_im, ssm_d=m_ssm_d, ssm_w_glu=m_ssm_w_glu, ssm_b_glu=m_ssm_b_glu,
                attn_sinks=m_attn_sinks, w_out=m_w_out, post_norm_g=m_post_norm_g, pl_w_proj=m_pl_w_proj,
                pl_w_gate=m_pl_w_gate, pl_b_gate=m_pl_b_gate)
    v_in = dict(pre_norm_g=v_pre_norm_g, w_in=v_w_in, ssm_lam_re=v_ssm_lam_re, ssm_lam_im=v_ssm_lam_im,
                ssm_log_step=v_ssm_log_step, ssm_b_re=v_ssm_b_re, ssm_b_im=v_ssm_b_im, ssm_c_re=v_ssm_c_re,
                ssm_c_im=v_ssm_c_im, ssm_d=v_ssm_d, ssm_w_glu=v_ssm_w_glu, ssm_b_glu=v_ssm_b_glu,
                attn_sinks=v_attn_sinks, w_out=v_w_out, post_norm_g=v_post_norm_g, pl_w_proj=v_pl_w_proj,
                pl_w_gate=v_pl_w_gate, pl_b_gate=v_pl_b_gate)

    def two_d(tree):
        return {k: _to_kernel_form(k, a) for k, a in tree.items()}

    w2, m2, v2 = two_d(weights), two_d(m_in), two_d(v_in)

    (w_in_full,) = _gather_weights_beside([w2["w_in"].astype(BF16)], "gather_w_in_beside", 4)
    s5_params = tuple(w2[n] for n in ("ssm_lam_re", "ssm_lam_im", "ssm_log_step", "ssm_b_re", "ssm_b_im", "ssm_c_re",
                                      "ssm_c_im"))
    s5_operands = _s5_params_fwd(*s5_params)
    behind = s5_operands[0][0, 0] * 0.0
    rest = _gather_weights_beside([(w2[n] + behind).astype(BF16) for n in _BIG[1:]], "gather_weights_beside", 1)
    full = dict(zip(_BIG, [w_in_full] + rest))
    grad_x, loss, grads = _local_step(
        x, p, loss_target, w2["pre_norm_g"], full["w_in"], s5_params, s5_operands, w2["ssm_d"], full["ssm_w_glu"], w2["ssm_b_glu"],
        w2["attn_sinks"], full["w_out"], w2["post_norm_g"], full["pl_w_proj"], full["pl_w_gate"], w2["pl_b_gate"])

    sent_early = ("w_out", "pl_w_gate", "pl_w_proj")
    landed = dict(zip([_BIG.index(n) for n in sent_early], _scatter_beside([grads[n] for n in sent_early])))
    after_scatter = landed[_BIG.index(sent_early[-1])][0, 0, 0] * 0.0
    late_operands = [grads[_SMALL_EARLY[0]] + after_scatter] + [grads[n] for n in _SMALL_EARLY[1:]]
    landed_small = dict(zip(_SMALL_EARLY, _broadcast_beside(late_operands)))
    g_big, g_small = _exchange_grads([grads[n] for n in _BIG], {**{n: grads[n] for n in _SMALL}, "loss": loss}, landed,
                                     landed_small)
    g_big = dict(zip(_BIG, g_big))
    total_loss = g_small.pop("loss")

    big_out = _adamw([w2[n] for n in _BIG], [g_big[n] for n in _BIG], [m2[n] for n in _BIG], [v2[n] for n in _BIG],
                     8, "adamw_matrices")
    small_names = tuple(_SMALL)
    small_out = _adamw([w2[n] for n in small_names], [g_small[n] for n in small_names], [m2[n] for n in small_names],
                       [v2[n] for n in small_names], 1, "adamw_small")

    results = [{**dict(zip(_BIG, big_part)), **dict(zip(small_names, small_part))}
               for big_part, small_part in zip(big_out, small_out)]
    flat = [_from_kernel_form(name, r[name], weights[name].shape) for r in results for name in _WEIGHT_ORDER]
    return (total_loss.reshape(()), grad_x, *flat)
```

```python
import math

import jax
import jax.numpy as jnp
from jax import lax
from jax.experimental import pallas as pl
from jax.experimental.pallas import tpu as pltpu
from jax.experimental.pallas import tpu_sc as plsc

F32 = jnp.float32
BF16 = jnp.bfloat16

D_MODEL = 1024
D_SSM = 512
D_ATTN = 512
SSM_GROUPS = 32
SSM_GROUP_CH = 16
SSM_STATE = 64
SSM_LANES = SSM_GROUPS * SSM_STATE
HEAD_DIM = 64
N_HEADS = 8
KV_HEADS = 2
Q_PER_KV = 4
WINDOW = 128
BLOCK = 128
D_PLE = 256
D_IN = 2304
EPS = 1e-6
ATTN_SCALE = 1.0 / math.sqrt(HEAD_DIM)

ADAM_LR = 0.001
ADAM_B1 = 0.9
ADAM_B2 = 0.999
ADAM_EPS = 1e-08
ADAM_WD = 0.01
ADAM_STEP = 10

N_CHIPS = 4
LANES = 128
SCAN_CHUNKS = 8
SCAN_TILE_STEPS = 32
SCAN_LANE_CHUNK = 512
MIB = 2 ** 20
MESH = pl.DeviceIdType.MESH


def _dot(a, b):
    return jnp.dot(a, b, preferred_element_type=F32)


def _dot_nt(a, b):
    return lax.dot_general(a, b, (((1,), (1,)), ((), ())), preferred_element_type=F32)


def _dot_tn(a, b):
    return lax.dot_general(a, b, (((0,), (0,)), ((), ())), preferred_element_type=F32)


def _params(vmem_mib, semantics=None):
    kw = dict(vmem_limit_bytes=vmem_mib * MIB)
    if semantics is not None:
        kw["dimension_semantics"] = semantics
    return pltpu.CompilerParams(**kw)


def _full(shape):
    nd = len(shape)
    return pl.BlockSpec(shape, lambda *_: (0,) * nd, pipeline_mode=pl.Buffered(1))


def _rows(tm, width):
    return pl.BlockSpec((tm, width), lambda i: (i, 0))


def _sds(shape, dtype=F32):
    return pltpu.HBM(shape, dtype)


def _call(body, **kw):
    fn = pl.pallas_call(body, **kw)
    return lambda *args: fn(*[pltpu.with_memory_space_constraint(a, pltpu.HBM) for a in args])


def _silu(z):
    return z * jax.nn.sigmoid(z)


def _pre_norm(x2d, g1):
    rows = x2d.shape[0]
    tm = 512

    def body(x_ref, g_ref, hn_ref):
        x = x_ref[...]
        r = lax.rsqrt(jnp.mean(x * x, axis=-1, keepdims=True) + EPS)
        hn_ref[...] = (x * r * g_ref[...]).astype(BF16)

    return _call(
        body, name="pre_norm", grid=(rows // tm,), in_specs=[_rows(tm, D_MODEL), _full((1, D_MODEL))],
        out_specs=_rows(tm, D_MODEL), out_shape=_sds((rows, D_MODEL), BF16), compiler_params=_params(32, ("arbitrary",)),
    )(x2d, g1)


def _in_proj(hn, w_in_t, n_seq, seq):
    rows = hn.shape[0]
    tm = 512
    slab, steps, _, _ = _scan_geometry(n_seq, seq)

    def body(hn_ref, w_ref, *out_refs):
        u_parts, (zs_ref, q_ref, k_ref, v_ref, za_ref) = out_refs[:_SCAN_PARTS], out_refs[_SCAN_PARTS:]
        whole = _dot_nt(hn_ref[...], w_ref[...])

        def proj(a, b):
            return whole[:, a:b]

        _store_chunks(u_parts, pl.program_id(0) * (tm // steps), proj(0, 512), steps, slab)
        zs_ref[...] = proj(512, 1024)
        q_ref[...] = (proj(1024, 1536) * ATTN_SCALE).astype(BF16)
        k_ref[...] = proj(1536, 1664).astype(BF16)
        v_ref[...] = proj(1664, 1792).astype(BF16)
        za_ref[...] = proj(1792, 2304)

    *u_parts, zs, q, k, v, za = _call(
        body, name="in_proj", grid=(rows // tm,),
        in_specs=[_rows(tm, D_MODEL), _full((D_IN, D_MODEL))],
        out_specs=_whole_parts(rows) + [_rows(tm, 512), _rows(tm, 512), _rows(tm, 128), _rows(tm, 128), _rows(tm, 512)],
        out_shape=_part_shapes(rows) + [_sds((rows, 512)), _sds((rows, 512), BF16), _sds((rows, 128), BF16),
                                        _sds((rows, 128), BF16), _sds((rows, 512))],
        compiler_params=_params(48, ("arbitrary",)),
    )(hn, w_in_t)
    return u_parts, zs, q, k, v, za


def _in_proj_bwd(x2d, dh1, g1, w_in_t, du_parts, dzs, dq, dk, dv, dza, n_seq, seq):
    rows = x2d.shape[0]
    tm = 512
    slab, steps, _, _ = _scan_geometry(n_seq, seq)

    def body(x_ref, dh1_ref, g_ref, w_ref, *refs):
        du_parts, (dzs_ref, dq_ref, dk_ref, dv_ref, dza_ref, gx_ref, dw_ref, dg_ref) = refs[:_SCAN_PARTS], refs[_SCAN_PARTS:]

        @pl.when(pl.program_id(0) == 0)
        def _():
            dw_ref[...] = jnp.zeros_like(dw_ref)
            dg_ref[...] = jnp.zeros_like(dg_ref)

        x = x_ref[...]
        g = g_ref[...]
        r = lax.rsqrt(jnp.mean(x * x, axis=-1, keepdims=True) + EPS)
        xr = x * r
        hn = (xr * g).astype(BF16)
        du = _load_chunks(du_parts, pl.program_id(0) * (tm // steps), tm // steps, steps, slab)
        d_proj = jnp.concatenate([du.astype(BF16), dzs_ref[...], dq_ref[...], dk_ref[...], dv_ref[...], dza_ref[...]],
                                 axis=1)
        dhn = _dot(d_proj, w_ref[...])
        dw_ref[...] += _dot_tn(d_proj, hn)
        dg_ref[...] += jnp.sum(dhn * xr, axis=0, keepdims=True)
        a_ = dhn * g
        gx_ref[...] = dh1_ref[...] + r * a_ - xr * (r * jnp.mean(a_ * xr, axis=-1, keepdims=True))

    return _call(
        body, name="in_proj_bwd", grid=(rows // tm,),
        in_specs=[_rows(tm, D_MODEL), _rows(tm, D_MODEL), _full((1, D_MODEL)), _full((D_IN, D_MODEL))]
        + _whole_parts(rows) + [_rows(tm, 512), _rows(tm, 512), _rows(tm, 128), _rows(tm, 128), _rows(tm, 512)],
        out_specs=[_rows(tm, D_MODEL), _full((D_IN, D_MODEL)), _full((1, D_MODEL))],
        out_shape=[_sds((rows, D_MODEL)), _sds((D_IN, D_MODEL)), _sds((1, D_MODEL))],
        compiler_params=_params(56, ("arbitrary",)),
    )(x2d, dh1, g1, w_in_t, *du_parts, dzs, dq, dk, dv, dza)


def _iota(shape, axis):
    return lax.broadcasted_iota(jnp.int32, shape, axis)


def _sum_of_thirds(f, a):
    hi = a.astype(BF16)
    rest = a - hi.astype(F32)
    mid = rest.astype(BF16)
    low = (rest - mid.astype(F32)).astype(BF16)
    return (f(hi) + f(mid)) + f(low)


@jax.custom_vjp
def _pick_rows(e, a):
    return _sum_of_thirds(lambda part: _dot(e, part), a)


def _pick_rows_fwd(e, a):
    return _pick_rows(e, a), e


def _pick_rows_bwd(e, ct):
    return jnp.zeros_like(e), _sum_of_thirds(lambda part: _dot_tn(e, part), ct)


_pick_rows.defvjp(_pick_rows_fwd, _pick_rows_bwd)


@jax.custom_vjp
def _pick_cols(a, e):
    return _sum_of_thirds(lambda part: _dot(part, e), a)


def _pick_cols_fwd(a, e):
    return _pick_cols(a, e), e


def _pick_cols_bwd(e, ct):
    return _sum_of_thirds(lambda part: _dot_nt(part, e), ct), jnp.zeros_like(e)


_pick_cols.defvjp(_pick_cols_fwd, _pick_cols_bwd)


_HALF_GROUPS = SSM_GROUPS // 2
_N_SHIFT = SSM_STATE.bit_length() - 1
_P_SHIFT = SSM_GROUP_CH.bit_length() - 1


def _s5_operands(lam_re, lam_im, log_step, b_re, b_im, c_re, c_im):
    g, n, p = SSM_GROUPS, SSM_STATE, SSM_GROUP_CH
    gn, gp, hn_, hp = g * n, g * p, _HALF_GROUPS * n, _HALF_GROUPS * p
    eye_g = _iota((g, g), 0) == _iota((g, g), 1)
    step = jnp.sum(jnp.where(eye_g, jnp.exp(log_step), 0.0), axis=1, keepdims=True)
    a_re = lam_re * step
    a_im = lam_im * step
    mag = jnp.exp(a_re)
    lbar_re = mag * jnp.cos(a_im)
    lbar_im = mag * jnp.sin(a_im)
    n_re = lbar_re - 1.0
    den = lam_re * lam_re + lam_im * lam_im
    f_re = (n_re * lam_re + lbar_im * lam_im) / den
    f_im = (lbar_im * lam_re - n_re * lam_im) / den

    spread_n = (_iota((n, gn), 0) == (_iota((n, gn), 1) & (n - 1))).astype(BF16)
    own_g = _iota((g, gn), 0) == (_iota((g, gn), 1) >> _N_SHIFT)

    def to_row(a):
        return jnp.sum(jnp.where(own_g, _pick_cols(a, spread_n), 0.0), axis=0, keepdims=True)

    per_group = ((_iota((gp, g), 0) >> _P_SHIFT) == _iota((gp, g), 1)).astype(BF16)
    fx_re, fx_im = _pick_rows(per_group, f_re), _pick_rows(per_group, f_im)
    bbar_re = fx_re * b_re - fx_im * b_im
    bbar_im = fx_re * b_im + fx_im * b_re

    tile_n = (_iota((n, hn_), 0) == (_iota((n, hn_), 1) & (n - 1))).astype(BF16)
    same_group = (_iota((hp, hn_), 0) >> _P_SHIFT) == (_iota((hp, hn_), 1) >> _N_SHIFT)

    def embed(a, hf):
        return jnp.where(same_group, _pick_cols(a[hf * hp:(hf + 1) * hp], tile_n), 0.0)

    return (to_row(lbar_re), to_row(lbar_im), embed(bbar_re, 0), embed(bbar_re, 1), embed(bbar_im, 0),
            embed(bbar_im, 1), embed(c_re, 0), embed(c_re, 1), embed(c_im, 0), embed(c_im, 1))


_S5_PARAM_SHAPES = ((SSM_GROUPS, SSM_STATE), (SSM_GROUPS, SSM_STATE), (1, SSM_GROUPS),
                    (D_SSM, SSM_STATE), (D_SSM, SSM_STATE), (D_SSM, SSM_STATE), (D_SSM, SSM_STATE))
_CM_SHAPE = (2, _HALF_GROUPS * SSM_GROUP_CH, _HALF_GROUPS * SSM_STATE)
_S5_OPERAND_SHAPES = ((1, SSM_LANES), (1, SSM_LANES), _CM_SHAPE, _CM_SHAPE, _CM_SHAPE, _CM_SHAPE)


def _s5_params_fwd(*params):
    def body(*refs):
        ins, (lre_ref, lim_ref, btre_ref, btim_ref, cmre_ref, cmim_ref) = refs[:7], refs[7:]
        vals = _s5_operands(*[r[...] for r in ins])
        lre_ref[...] = vals[0]
        lim_ref[...] = vals[1]
        for ref, pair in zip((btre_ref, btim_ref, cmre_ref, cmim_ref), (vals[2:4], vals[4:6], vals[6:8], vals[8:10])):
            ref[0] = pair[0].astype(BF16)
            ref[1] = pair[1].astype(BF16)

    dtypes = (F32, F32, BF16, BF16, BF16, BF16)
    return _call(
        body, name="s5_params_fwd",
        in_specs=[_full(s) for s in _S5_PARAM_SHAPES], out_specs=[_full(s) for s in _S5_OPERAND_SHAPES],
        out_shape=[_sds(s, d) for s, d in zip(_S5_OPERAND_SHAPES, dtypes)], compiler_params=_params(32),
    )(*params)


def _s5_params_bwd(params, cotangents):
    def body(*refs):
        ins, (dlre, dlim, dbtre, dbtim, dcmre, dcmim), outs = refs[:7], refs[7:13], refs[13:]
        _, vjp = jax.vjp(_s5_operands, *[r[...] for r in ins])
        cts = (dlre[...], dlim[...], dbtre[0], dbtre[1], dbtim[0], dbtim[1], dcmre[0], dcmre[1], dcmim[0], dcmim[1])
        for ref, val in zip(outs, vjp(cts)):
            ref[...] = val

    return _call(
        body, name="s5_params_bwd",
        in_specs=[_full(s) for s in _S5_PARAM_SHAPES + _S5_OPERAND_SHAPES],
        out_specs=[_full(s) for s in _S5_PARAM_SHAPES],
        out_shape=[_sds(s) for s in _S5_PARAM_SHAPES], compiler_params=_params(48),
    )(*params, *cotangents)


def _scan_geometry(n_seq, seq):
    slab = n_seq * SCAN_CHUNKS
    steps = seq // SCAN_CHUNKS
    tile_rows = slab * SCAN_TILE_STEPS
    n_tiles = steps // SCAN_TILE_STEPS
    return slab, steps, tile_rows, n_tiles


_SCAN_PARTS = D_SSM // LANES


def _whole_parts(rows):
    return [_full((rows, LANES))] * _SCAN_PARTS


def _part_shapes(rows):
    return [_sds((rows, LANES))] * _SCAN_PARTS


def _load_chunks(parts, first_chunk, n_chunks, steps, slab):
    return jnp.concatenate([
        jnp.concatenate([ref[pl.ds(first_chunk + q, steps, stride=slab), :] for ref in parts], axis=1)
        for q in range(n_chunks)], axis=0)


def _store_chunks(parts, first_chunk, value, steps, slab):
    for q in range(value.shape[0] // steps):
        for j, ref in enumerate(parts):
            ref[pl.ds(first_chunk + q, steps, stride=slab), :] = value[q * steps:(q + 1) * steps,
                                                                     j * LANES:(j + 1) * LANES]


def _join_parts(parts):
    return jnp.concatenate([ref[...] for ref in parts], axis=1)


def _split_parts(parts, value):
    for j, ref in enumerate(parts):
        ref[...] = value[:, j * LANES:(j + 1) * LANES]


def _complex_power(re, im, n):
    out = None
    while n:
        if n & 1:
            out = (re, im) if out is None else (out[0] * re - out[1] * im, out[0] * im + out[1] * re)
        n >>= 1
        if n:
            re, im = re * re - im * im, 2.0 * re * im
    return out


def _chunk_carry(sum_re, sum_im, carry_re, carry_im, a_re, a_im, n_seq, reverse):
    carry_re[...] = jnp.zeros_like(carry_re)
    carry_im[...] = jnp.zeros_like(carry_im)
    for s in range(n_seq):
        order = range(SCAN_CHUNKS - 2, -1, -1) if reverse else range(1, SCAN_CHUNKS)
        for c in order:
            r = s * SCAN_CHUNKS + c
            p = r + 1 if reverse else r - 1
            p_re, p_im = carry_re[p:p + 1, :], carry_im[p:p + 1, :]
            carry_re[r:r + 1, :] = a_re * p_re - a_im * p_im + sum_re[p:p + 1, :]
            carry_im[r:r + 1, :] = a_re * p_im + a_im * p_re + sum_im[p:p + 1, :]


def _s5_scan_fwd(u_parts, bt_re, bt_im, cm_re, cm_im, lbar_re, lbar_im, d_row, n_seq, seq):
    slab, steps, tile_rows, n_tiles = _scan_geometry(n_seq, seq)
    rows = u_parts[0].shape[0]

    def body(*refs):
        u_refs, refs = refs[:_SCAN_PARTS], refs[_SCAN_PARTS:]
        (bre_ref, bim_ref, cre_ref, cim_ref, lre_ref, lim_ref, d_ref), refs = refs[:7], refs[7:]
        y_refs, (hre_ref, him_ref, st_re, st_im, h0_re, h0_im, buf_re, buf_im) = refs[:_SCAN_PARTS], refs[_SCAN_PARTS:]
        second = pl.program_id(0) == 1
        i = pl.program_id(1)

        @pl.when(jnp.logical_and(i == 0, jnp.logical_not(second)))
        def _():
            st_re[...] = jnp.zeros_like(st_re)
            st_im[...] = jnp.zeros_like(st_im)

        u = _join_parts(u_refs)
        ub = u.astype(BF16)
        for hf in range(2):
            cols = slice(hf * 1024, (hf + 1) * 1024)
            buf_re[:, cols] = _dot(ub[:, hf * 256:(hf + 1) * 256], bre_ref[hf])
            buf_im[:, cols] = _dot(ub[:, hf * 256:(hf + 1) * 256], bim_ref[hf])

        for lc in range(SSM_LANES // SCAN_LANE_CHUNK):
            cols = slice(lc * SCAN_LANE_CHUNK, (lc + 1) * SCAN_LANE_CHUNK)
            l_re = jnp.broadcast_to(lre_ref[:, cols], (slab, SCAN_LANE_CHUNK))
            l_im = jnp.broadcast_to(lim_ref[:, cols], (slab, SCAN_LANE_CHUNK))

            def scan_tile(keep_states):
                def step(t, carry):
                    s_re, s_im = carry
                    r0 = pl.multiple_of(t * slab, slab)
                    n_re = l_re * s_re - l_im * s_im + buf_re[pl.ds(r0, slab), cols]
                    n_im = l_re * s_im + l_im * s_re + buf_im[pl.ds(r0, slab), cols]
                    if keep_states:
                        buf_re[pl.ds(r0, slab), cols] = n_re
                        buf_im[pl.ds(r0, slab), cols] = n_im
                    return n_re, n_im

                s_re, s_im = lax.fori_loop(0, SCAN_TILE_STEPS, step, (st_re[:, cols], st_im[:, cols]), unroll=True)
                st_re[:, cols] = s_re
                st_im[:, cols] = s_im

            pl.when(jnp.logical_not(second))(lambda: scan_tile(False))
            pl.when(second)(lambda: scan_tile(True))

        @pl.when(jnp.logical_and(i == n_tiles - 1, jnp.logical_not(second)))
        def _():
            a_re, a_im = _complex_power(lre_ref[...], lim_ref[...], steps)
            _chunk_carry(st_re, st_im, h0_re, h0_im, a_re, a_im, n_seq, reverse=False)
            st_re[...] = h0_re[...]
            st_im[...] = h0_im[...]

        @pl.when(second)
        def _():
            h_re = buf_re[...].astype(BF16)
            h_im = buf_im[...].astype(BF16)
            hre_ref[...] = h_re
            him_ref[...] = h_im
            for hf in range(2):
                cols = slice(hf * 1024, (hf + 1) * 1024)
                ycols = slice(hf * 256, (hf + 1) * 256)
                y_half = (_dot_nt(h_re[:, cols], cre_ref[hf]) - _dot_nt(h_im[:, cols], cim_ref[hf])
                          + d_ref[:, ycols] * u[:, ycols])
                _split_parts(y_refs[2 * hf:2 * hf + 2], y_half)

    tile = lambda w: pl.BlockSpec((tile_rows, w), lambda p, i: (i, 0))
    out_tile = lambda w: pl.BlockSpec((tile_rows, w), lambda p, i: (i * p, 0))
    cm = _full(_CM_SHAPE)
    outs = _call(
        body, name="s5_scan_fwd", grid=(2, n_tiles),
        in_specs=[tile(LANES)] * _SCAN_PARTS + [cm, cm, cm, cm, _full((1, SSM_LANES)), _full((1, SSM_LANES)),
                                                _full((1, 512))],
        out_specs=[out_tile(LANES)] * _SCAN_PARTS + [out_tile(SSM_LANES), out_tile(SSM_LANES)],
        out_shape=_part_shapes(rows) + [_sds((rows, SSM_LANES), BF16), _sds((rows, SSM_LANES), BF16)],
        scratch_shapes=[pltpu.VMEM((slab, SSM_LANES), F32)] * 4 + [pltpu.VMEM((tile_rows, SSM_LANES), F32)] * 2,
        compiler_params=_params(40, ("arbitrary", "arbitrary")),
    )(*u_parts, bt_re, bt_im, cm_re, cm_im, lbar_re, lbar_im, d_row)
    return outs[:_SCAN_PARTS], outs[_SCAN_PARTS], outs[_SCAN_PARTS + 1]


def _s5_scan_bwd(dy_parts, u_parts, h_re, h_im, bt_re, bt_im, cm_re, cm_im, lbar_re, lbar_im, d_row, n_seq, seq):
    slab, steps, tile_rows, n_tiles = _scan_geometry(n_seq, seq)
    rows = u_parts[0].shape[0]

    def body(*refs):
        dy_refs, u_refs, refs = refs[:_SCAN_PARTS], refs[_SCAN_PARTS:2 * _SCAN_PARTS], refs[2 * _SCAN_PARTS:]
        (hre_ref, him_ref, bre_ref, bim_ref, cre_ref, cim_ref, lre_ref, lim_ref, d_ref), refs = refs[:9], refs[9:]
        du_refs, refs = refs[:_SCAN_PARTS], refs[_SCAN_PARTS:]
        (dbre_ref, dbim_ref, dcre_ref, dcim_ref, dlre_ref, dlim_ref, dd_ref,
         st_re, st_im, g0_re, g0_im, acc_re, acc_im, buf_re, buf_im) = refs
        second = pl.program_id(0) == 1
        i = pl.program_id(1)

        @pl.when(jnp.logical_and(i == 0, jnp.logical_not(second)))
        def _():
            st_re[...] = jnp.zeros_like(st_re)
            st_im[...] = jnp.zeros_like(st_im)
            acc_re[...] = jnp.zeros_like(acc_re)
            acc_im[...] = jnp.zeros_like(acc_im)
            for ref in (dbre_ref, dbim_ref, dcre_ref, dcim_ref, dd_ref):
                ref[...] = jnp.zeros_like(ref)

        dy = _join_parts(dy_refs)
        dyb = dy.astype(BF16)
        for hf in range(2):
            cols = slice(hf * 1024, (hf + 1) * 1024)
            buf_re[:, cols] = _dot(dyb[:, hf * 256:(hf + 1) * 256], cre_ref[hf])
            buf_im[:, cols] = -_dot(dyb[:, hf * 256:(hf + 1) * 256], cim_ref[hf])

        for lc in range(SSM_LANES // SCAN_LANE_CHUNK):
            cols = slice(lc * SCAN_LANE_CHUNK, (lc + 1) * SCAN_LANE_CHUNK)
            l_re = jnp.broadcast_to(lre_ref[:, cols], (slab, SCAN_LANE_CHUNK))
            l_im = jnp.broadcast_to(lim_ref[:, cols], (slab, SCAN_LANE_CHUNK))

            def advance(r0, s_re, s_im):
                n_re = l_re * s_re + l_im * s_im + buf_re[pl.ds(r0, slab), cols]
                n_im = l_re * s_im - l_im * s_re + buf_im[pl.ds(r0, slab), cols]
                buf_re[pl.ds(r0, slab), cols] = n_re
                buf_im[pl.ds(r0, slab), cols] = n_im
                return n_re, n_im

            def row0(k):
                return pl.multiple_of((SCAN_TILE_STEPS - 1 - k) * slab, slab)

            @pl.when(jnp.logical_not(second))
            def _():
                s_re, s_im = lax.fori_loop(0, SCAN_TILE_STEPS, lambda k, s: advance(row0(k), *s),
                                           (st_re[:, cols], st_im[:, cols]), unroll=True)
                st_re[:, cols] = s_re
                st_im[:, cols] = s_im

            @pl.when(second)
            def _():
                def step(k, carry):
                    s_re, s_im, a_re, a_im = carry
                    r0 = row0(k)
                    hr = hre_ref[pl.ds(r0, slab), cols].astype(F32)
                    hi = him_ref[pl.ds(r0, slab), cols].astype(F32)
                    a_re = a_re + s_re * hr + s_im * hi
                    a_im = a_im + s_im * hr - s_re * hi
                    return advance(r0, s_re, s_im) + (a_re, a_im)

                zero = jnp.zeros((slab, SCAN_LANE_CHUNK), F32)
                s_re, s_im, a_re, a_im = lax.fori_loop(
                    0, SCAN_TILE_STEPS, step, (st_re[:, cols], st_im[:, cols], zero, zero), unroll=True)
                st_re[:, cols] = s_re
                st_im[:, cols] = s_im
                acc_re[:, cols] += a_re
                acc_im[:, cols] += a_im

        @pl.when(jnp.logical_and(i == n_tiles - 1, jnp.logical_not(second)))
        def _():
            p_re, p_im = _complex_power(lre_ref[...], lim_ref[...], steps)
            _chunk_carry(st_re, st_im, g0_re, g0_im, p_re, -p_im, n_seq, reverse=True)
            st_re[...] = g0_re[...]
            st_im[...] = g0_im[...]

        @pl.when(second)
        def _():
            u = _join_parts(u_refs)
            ub = u.astype(BF16)
            g_re = buf_re[...].astype(BF16)
            g_im = buf_im[...].astype(BF16)
            dd_ref[...] += jnp.sum(dy * u, axis=0, keepdims=True)
            for hf in range(2):
                cols = slice(hf * 1024, (hf + 1) * 1024)
                ycols = slice(hf * 256, (hf + 1) * 256)
                du_half = (_dot_nt(g_re[:, cols], bre_ref[hf]) + _dot_nt(g_im[:, cols], bim_ref[hf])
                           + d_ref[:, ycols] * dy[:, ycols])
                _split_parts(du_refs[2 * hf:2 * hf + 2], du_half)
                for q4 in range(_HALF_GROUPS // 4):
                    ch = slice(hf * 256 + q4 * 64, hf * 256 + (q4 + 1) * 64)
                    st = slice(hf * 1024 + q4 * 256, hf * 1024 + (q4 + 1) * 256)
                    blk = (hf, slice(q4 * 64, (q4 + 1) * 64), slice(q4 * 256, (q4 + 1) * 256))
                    dbre_ref[blk] += _dot_tn(ub[:, ch], g_re[:, st])
                    dbim_ref[blk] += _dot_tn(ub[:, ch], g_im[:, st])
                    dcre_ref[blk] += _dot_tn(dyb[:, ch], hre_ref[:, st])
                    dcim_ref[blk] -= _dot_tn(dyb[:, ch], him_ref[:, st])

        @pl.when(jnp.logical_and(i == n_tiles - 1, second))
        def _():
            dlre_ref[...] = jnp.sum(acc_re[...], axis=0, keepdims=True)
            dlim_ref[...] = jnp.sum(acc_im[...], axis=0, keepdims=True)

    tile = lambda w: pl.BlockSpec((tile_rows, w), lambda p, i: (n_tiles - 1 - i, 0))
    second_tile = lambda w: pl.BlockSpec((tile_rows, w), lambda p, i: (n_tiles - 1 - i * p, 0))
    cm = _full(_CM_SHAPE)
    row = _full((1, SSM_LANES))
    outs = _call(
        body, name="s5_scan_bwd", grid=(2, n_tiles),
        in_specs=[tile(LANES)] * _SCAN_PARTS + [second_tile(LANES)] * _SCAN_PARTS
        + [second_tile(SSM_LANES), second_tile(SSM_LANES), cm, cm, cm, cm, row, row, _full((1, 512))],
        out_specs=[second_tile(LANES)] * _SCAN_PARTS + [cm, cm, cm, cm, row, row, _full((1, 512))],
        out_shape=(_part_shapes(rows) + [_sds(_CM_SHAPE)] * 4 + [_sds((1, SSM_LANES))] * 2 + [_sds((1, 512))]),
        scratch_shapes=[pltpu.VMEM((slab, SSM_LANES), F32)] * 6 + [pltpu.VMEM((tile_rows, SSM_LANES), F32)] * 2,
        compiler_params=_params(48, ("arbitrary", "arbitrary")),
    )(*dy_parts, *u_parts, h_re, h_im, bt_re, bt_im, cm_re, cm_im, lbar_re, lbar_im, d_row)
    return (outs[:_SCAN_PARTS],) + tuple(outs[_SCAN_PARTS:])


def _glu_gate(gl, a, zs):
    return gl * jax.nn.sigmoid(a) * _silu(zs)


def _glu_fwd(y_parts, zs, w_glu, b_glu, n_seq, seq):
    rows = zs.shape[0]
    tm = 512
    slab, steps, _, _ = _scan_geometry(n_seq, seq)

    def body(*refs):
        y_refs, (zs_ref, w_ref, b_ref, o_ref) = refs[:_SCAN_PARTS], refs[_SCAN_PARTS:]
        y = _load_chunks(y_refs, pl.program_id(0) * (tm // steps), tm // steps, steps, slab)
        gl = jax.nn.gelu(y)
        a = _dot(gl.astype(BF16), w_ref[...]) + b_ref[...]
        o_ref[...] = _glu_gate(gl, a, zs_ref[...]).astype(BF16)

    return _call(
        body, name="glu_fwd", grid=(rows // tm,),
        in_specs=_whole_parts(rows) + [_rows(tm, 512), _full((512, 512)), _full((1, 512))],
        out_specs=_rows(tm, 512), out_shape=_sds((rows, 512), BF16),
        compiler_params=_params(32, ("arbitrary",)),
    )(*y_parts, zs, w_glu, b_glu)


def _glu_bwd(y_parts, zs, d_out, w_glu, b_glu, n_seq, seq):
    rows = zs.shape[0]
    tm = 512
    slab, steps, _, _ = _scan_geometry(n_seq, seq)

    def body(*refs):
        y_refs, (zs_ref, d_ref, w_ref, b_ref), refs = refs[:_SCAN_PARTS], refs[_SCAN_PARTS:_SCAN_PARTS + 4], refs[_SCAN_PARTS + 4:]
        dy_refs, (dzs_ref, dw_ref, db_ref) = refs[:_SCAN_PARTS], refs[_SCAN_PARTS:]
        first_chunk = pl.program_id(0) * (tm // steps)

        @pl.when(pl.program_id(0) == 0)
        def _():
            dw_ref[...] = jnp.zeros_like(dw_ref)
            db_ref[...] = jnp.zeros_like(db_ref)

        gl, gelu_vjp = jax.vjp(jax.nn.gelu, _load_chunks(y_refs, first_chunk, tm // steps, steps, slab))
        glb = gl.astype(BF16)
        a = _dot(glb, w_ref[...]) + b_ref[...]
        _, gate_vjp = jax.vjp(_glu_gate, gl, a, zs_ref[...])
        d_gl, d_a, d_zs = gate_vjp(d_ref[...])
        dab = d_a.astype(BF16)
        d_gl = d_gl + _dot_nt(dab, w_ref[...])
        _store_chunks(dy_refs, first_chunk, gelu_vjp(d_gl)[0], steps, slab)
        dzs_ref[...] = d_zs.astype(BF16)
        dw_ref[...] += _dot_tn(glb, dab)
        db_ref[...] += jnp.sum(d_a, axis=0, keepdims=True)

    *dy_parts, dzs, dw, db = _call(
        body, name="glu_bwd", grid=(rows // tm,),
        in_specs=_whole_parts(rows) + [_rows(tm, 512), _rows(tm, 512), _full((512, 512)), _full((1, 512))],
        out_specs=_whole_parts(rows) + [_rows(tm, 512), _full((512, 512)), _full((1, 512))],
        out_shape=_part_shapes(rows) + [_sds((rows, 512), BF16), _sds((512, 512)), _sds((1, 512))],
        compiler_params=_params(40, ("arbitrary",)),
    )(*y_parts, zs, d_out, w_glu, b_glu)
    return dy_parts, dzs, dw, db


_GROUP_ROWS = Q_PER_KV * BLOCK
_BLOCK_SHIFT = BLOCK.bit_length() - 1


def _attn_bias(j):
    row = _iota((_GROUP_ROWS, BLOCK), 0)
    dist_cur = (row & (BLOCK - 1)) - _iota((_GROUP_ROWS, BLOCK), 1)
    dist_prev = dist_cur + BLOCK
    head = row >> _BLOCK_SHIFT
    slope = jnp.zeros((_GROUP_ROWS, BLOCK), F32)
    for g in range(Q_PER_KV):
        slope = jnp.where(head == g, 2.0 ** (-(j * Q_PER_KV + g + 1)), slope)
    bias_cur = jnp.where(dist_cur >= 0, -slope * dist_cur.astype(F32), -jnp.inf)
    bias_prev = jnp.where(dist_prev < WINDOW, -slope * dist_prev.astype(F32), -jnp.inf)
    return bias_cur, bias_prev


_ATTN_BIAS_SCRATCH = pltpu.VMEM((KV_HEADS, 2, _GROUP_ROWS, BLOCK), F32)


def _fill_attn_bias(bias_ref):
    @pl.when(jnp.logical_and(pl.program_id(0) == 0, pl.program_id(1) == 0))
    def _():
        for j in range(KV_HEADS):
            bias_ref[j, 0], bias_ref[j, 1] = _attn_bias(j)


def _stack_heads(x, j):
    heads = range(j * Q_PER_KV, (j + 1) * Q_PER_KV)
    return jnp.concatenate([x[:, h * HEAD_DIM:(h + 1) * HEAD_DIM] for h in heads], axis=0)


def _stack_columns(x, j):
    heads = range(j * Q_PER_KV, (j + 1) * Q_PER_KV)
    return jnp.concatenate([jnp.broadcast_to(x[:, h:h + 1], (BLOCK, 1)) for h in heads], axis=0)


def _attn_fwd(q, k, v, za, sinks, n_seq, seq):
    nb = seq // BLOCK
    rows = q.shape[0]

    def body(q_ref, kc_ref, kp_ref, vc_ref, vp_ref, za_ref, sk_ref, o_ref, ao_ref, lse_ref, bias_ref):
        _fill_attn_bias(bias_ref)
        has_prev = pl.program_id(1) > 0
        q_all = q_ref[...]
        for j in range(KV_HEADS):
            js = slice(j * HEAD_DIM, (j + 1) * HEAD_DIM)
            bias_c, bias_p = bias_ref[j, 0], bias_ref[j, 1]
            q4 = _stack_heads(q_all, j)
            sc = _dot_nt(q4, kc_ref[:, js]) + bias_c
            sp = _dot_nt(q4, kp_ref[:, js]) + jnp.where(has_prev, bias_p, -jnp.inf)
            sink = _stack_columns(sk_ref[...], j)
            m = jnp.maximum(jnp.max(jnp.maximum(sc, sp), axis=-1, keepdims=True), sink)
            ec = jnp.exp(sc - m)
            ep = jnp.exp(sp - m)
            den = jnp.sum(ec + ep, axis=-1, keepdims=True) + jnp.exp(sink - m)
            inv = 1.0 / den
            o4 = _dot((ec * inv).astype(BF16), vc_ref[:, js]) + _dot((ep * inv).astype(BF16), vp_ref[:, js])
            lse4 = m + jnp.log(den)
            for g in range(Q_PER_KV):
                h = j * Q_PER_KV + g
                o_ref[:, h * HEAD_DIM:(h + 1) * HEAD_DIM] = o4[g * BLOCK:(g + 1) * BLOCK]
                lse_ref[:, h:h + 1] = lse4[g * BLOCK:(g + 1) * BLOCK]
        ao_ref[...] = (o_ref[...] * _silu(za_ref[...])).astype(BF16)

    cur = lambda w: pl.BlockSpec((BLOCK, w), lambda b, n: (b * nb + n, 0))
    prev = lambda w: pl.BlockSpec((BLOCK, w), lambda b, n: (b * nb + jnp.maximum(n - 1, 0), 0))
    return _call(
        body, name="attn_fwd", grid=(n_seq, nb),
        in_specs=[cur(512), cur(128), prev(128), cur(128), prev(128), cur(512), _full((1, N_HEADS))],
        out_specs=[cur(512), cur(512), cur(N_HEADS)],
        out_shape=[_sds((rows, 512)), _sds((rows, 512), BF16), _sds((rows, N_HEADS))],
        scratch_shapes=[_ATTN_BIAS_SCRATCH], compiler_params=_params(32, ("arbitrary", "arbitrary")),
    )(q, k, k, v, v, za, sinks)


def _attn_bwd(q, k, v, za, o, lse, d_ao, sinks, n_seq, seq):
    nb = seq // BLOCK
    rows = q.shape[0]

    def body(q_ref, kc_ref, kp_ref, vc_ref, vp_ref, za_ref, o_ref, lse_ref, d_ref, sk_ref,
             dq_ref, dk_ref, dv_ref, dza_ref, dsk_ref, bias_ref, dk_carry, dv_carry):
        n = nb - 1 - pl.program_id(1)
        _fill_attn_bias(bias_ref)

        @pl.when(jnp.logical_and(pl.program_id(0) == 0, pl.program_id(1) == 0))
        def _():
            dsk_ref[...] = jnp.zeros_like(dsk_ref)
            dk_carry[...] = jnp.zeros_like(dk_carry)
            dv_carry[...] = jnp.zeros_like(dv_carry)

        has_prev = n > 0
        has_next = n + 1 < nb

        _, gate_vjp = jax.vjp(lambda o_, z_: o_ * _silu(z_), o_ref[...], za_ref[...])
        d_o, d_za = gate_vjp(d_ref[...])
        dza_ref[...] = d_za.astype(BF16)
        q_all = q_ref[...]
        lse_all = lse_ref[...]

        for j in range(KV_HEADS):
            js = slice(j * HEAD_DIM, (j + 1) * HEAD_DIM)
            kc, kp, vc, vp = kc_ref[:, js], kp_ref[:, js], vc_ref[:, js], vp_ref[:, js]
            bias_c, bias_p = bias_ref[j, 0], bias_ref[j, 1]
            q4 = _stack_heads(q_all, j)
            do4b = _stack_heads(d_o, j).astype(BF16)
            lse4 = _stack_columns(lse_all, j)
            pc = jnp.exp(_dot_nt(q4, kc) + bias_c - lse4)
            pp = jnp.exp(_dot_nt(q4, kp) + jnp.where(has_prev, bias_p, -jnp.inf) - lse4)
            dpc = _dot_nt(do4b, vc)
            dpp = _dot_nt(do4b, vp)
            delta = jnp.sum(pc * dpc + pp * dpp, axis=-1, keepdims=True)
            dsc = (pc * (dpc - delta)).astype(BF16)
            dsp = (pp * (dpp - delta)).astype(BF16)
            dq4 = ((_dot(dsc, kc) + _dot(dsp, kp)) * ATTN_SCALE).astype(BF16)
            sink_loss = jnp.exp(_stack_columns(sk_ref[...], j) - lse4) * delta
            for g in range(Q_PER_KV):
                h = j * Q_PER_KV + g
                dq_ref[:, h * HEAD_DIM:(h + 1) * HEAD_DIM] = dq4[g * BLOCK:(g + 1) * BLOCK]
                dsk_ref[0:1, h:h + 1] -= jnp.sum(sink_loss[g * BLOCK:(g + 1) * BLOCK], axis=0, keepdims=True)
            dk = _dot_tn(dsc, q4) + jnp.where(has_next, dk_carry[j], 0.0)
            dv = _dot_tn(pc.astype(BF16), do4b) + jnp.where(has_next, dv_carry[j], 0.0)
            dk_carry[j] = _dot_tn(dsp, q4)
            dv_carry[j] = _dot_tn(pp.astype(BF16), do4b)
            dk_ref[:, js] = dk.astype(BF16)
            dv_ref[:, js] = dv.astype(BF16)

    cur = lambda w: pl.BlockSpec((BLOCK, w), lambda b, s: (b * nb + nb - 1 - s, 0))
    prev = lambda w: pl.BlockSpec((BLOCK, w), lambda b, s: (b * nb + jnp.maximum(nb - 2 - s, 0), 0))
    return _call(
        body, name="attn_bwd", grid=(n_seq, nb),
        in_specs=[cur(512), cur(128), prev(128), cur(128), prev(128), cur(512), cur(512), cur(N_HEADS), cur(512),
                  _full((1, N_HEADS))],
        out_specs=[cur(512), cur(128), cur(128), cur(512), _full((1, N_HEADS))],
        out_shape=[_sds((rows, 512), BF16), _sds((rows, 128), BF16), _sds((rows, 128), BF16),
                   _sds((rows, 512), BF16), _sds((1, N_HEADS))],
        scratch_shapes=[_ATTN_BIAS_SCRATCH, pltpu.VMEM((KV_HEADS, BLOCK, HEAD_DIM), F32),
                        pltpu.VMEM((KV_HEADS, BLOCK, HEAD_DIM), F32)],
        compiler_params=_params(32, ("arbitrary", "arbitrary")),
    )(q, k, k, v, v, za, o, lse, d_ao, sinks)


def _tail(ssm_out, attn_out, x2d, p2d, target, w_out, g2, w_gate, b_gate, w_proj):
    rows = x2d.shape[0]
    tm = 512

    def body(so_ref, ao_ref, x_ref, p_ref, t_ref, wo_ref, g2_ref, wg_ref, bg_ref, wp_ref,
             dh1_ref, dso_ref, dao_ref, dwo_ref, dwg_ref, dwp_ref, dbg_ref, dg2_ref, loss_ref):
        @pl.when(pl.program_id(0) == 0)
        def _():
            for ref in (dwo_ref, dwg_ref, dwp_ref, dbg_ref, dg2_ref, loss_ref):
                ref[...] = jnp.zeros_like(ref)

        cat = jnp.concatenate([so_ref[...], ao_ref[...]], axis=1)
        g2 = g2_ref[...]
        mixed = _dot(cat, wo_ref[...])
        r = lax.rsqrt(jnp.mean(mixed * mixed, axis=-1, keepdims=True) + EPS)
        mr = mixed * r
        h1 = x_ref[...] + mr * g2
        h1b = h1.astype(BF16)
        gate = jax.nn.sigmoid(_dot(h1b, wg_ref[...]) + bg_ref[...])
        pb = p_ref[...].astype(BF16)
        wp_blocks = [slice(j * D_PLE, (j + 1) * D_PLE) for j in range(N_CHIPS)]
        pp = jnp.concatenate([_dot(pb, wp_ref[blk, :]) for blk in wp_blocks], axis=1)
        err = h1 + gate * pp - t_ref[...]
        loss_ref[...] += 0.5 * jnp.sum(jnp.mean(err * err, axis=-1, keepdims=True), axis=0, keepdims=True)

        dh2 = err * (1.0 / D_MODEL)
        d_glin = dh2 * pp * gate * (1.0 - gate)
        d_glin_b = d_glin.astype(BF16)
        dwg_ref[...] += _dot_tn(h1b, d_glin_b)
        dbg_ref[...] += jnp.sum(d_glin, axis=0, keepdims=True)
        d_pp = (dh2 * gate).astype(BF16)
        for blk in wp_blocks:
            dwp_ref[blk, :] += _dot_tn(pb, d_pp[:, blk])
        dh1 = dh2 + _dot_nt(d_glin_b, wg_ref[...])
        dh1_ref[...] = dh1
        dg2_ref[...] += jnp.sum(dh1 * mr, axis=0, keepdims=True)
        a_ = dh1 * g2
        d_mixed = (r * a_ - mr * (r * jnp.mean(a_ * mr, axis=-1, keepdims=True))).astype(BF16)
        dwo_ref[...] += _dot_tn(cat, d_mixed)
        d_cat = _dot_nt(d_mixed, wo_ref[...])
        dso_ref[...] = d_cat[:, 0:512]
        dao_ref[...] = d_cat[:, 512:1024]

    return _call(
        body, name="tail_fwd_bwd", grid=(rows // tm,),
        in_specs=[_rows(tm, 512), _rows(tm, 512), _rows(tm, D_MODEL), _rows(tm, D_PLE), _rows(tm, D_MODEL),
                  _full((D_MODEL, D_MODEL)), _full((1, D_MODEL)), _full((D_MODEL, D_MODEL)), _full((1, D_MODEL)),
                  _full((N_CHIPS * D_PLE, D_PLE))],
        out_specs=[_rows(tm, D_MODEL), _rows(tm, 512), _rows(tm, 512), _full((D_MODEL, D_MODEL)),
                   _full((D_MODEL, D_MODEL)), _full((N_CHIPS * D_PLE, D_PLE)), _full((1, D_MODEL)), _full((1, D_MODEL)),
                   _full((1, 1))],
        out_shape=[_sds((rows, D_MODEL)), _sds((rows, 512)), _sds((rows, 512)), _sds((D_MODEL, D_MODEL)),
                   _sds((D_MODEL, D_MODEL)), _sds((N_CHIPS * D_PLE, D_PLE)), _sds((1, D_MODEL)), _sds((1, D_MODEL)),
                   _sds((1, 1))],
        compiler_params=_params(52, ("arbitrary",)),
    )(ssm_out, attn_out, x2d, p2d, target, w_out, g2, w_gate, b_gate, w_proj)


def _local_step(x, hn, p, target, pre_norm_g, w_in_t, s5_params, s5_operands, ssm_d, w_glu, b_glu, sinks, w_out,
                post_norm_g, w_proj, w_gate, b_gate):
    n_seq, seq, _ = x.shape
    rows = n_seq * seq
    x2d = x.reshape(rows, D_MODEL)
    p2d = p.reshape(rows, D_PLE)
    t2d = target.reshape(rows, D_MODEL)

    l_re, l_im, bt_re, bt_im, cm_re, cm_im = s5_operands

    u_scan, zs, q, k, v, za = _in_proj(hn, w_in_t, n_seq, seq)
    y_scan, h_re, h_im = _s5_scan_fwd(u_scan, bt_re, bt_im, cm_re, cm_im, l_re, l_im, ssm_d, n_seq, seq)
    ssm_out = _glu_fwd(y_scan, zs, w_glu, b_glu, n_seq, seq)
    o, attn_out, lse = _attn_fwd(q, k, v, za, sinks, n_seq, seq)

    dh1, d_so, d_ao, d_w_out, d_w_gate, d_w_proj, d_b_gate, d_g2, loss = _tail(
        ssm_out, attn_out, x2d, p2d, t2d, w_out, post_norm_g, w_gate, b_gate, w_proj)

    dq, dk, dv, dza, d_sinks = _attn_bwd(q, k, v, za, o, lse, d_ao, sinks, n_seq, seq)
    dy_scan, dzs, d_w_glu, d_b_glu = _glu_bwd(y_scan, zs, d_so, w_glu, b_glu, n_seq, seq)
    du_scan, d_bt_re, d_bt_im, d_cm_re, d_cm_im, d_l_re, d_l_im, d_d = _s5_scan_bwd(
        dy_scan, u_scan, h_re, h_im, bt_re, bt_im, cm_re, cm_im, l_re, l_im, ssm_d, n_seq, seq)
    d_lam_re, d_lam_im, d_log_step, d_b_re, d_b_im, d_c_re, d_c_im = _s5_params_bwd(
        s5_params, (d_l_re, d_l_im, d_bt_re, d_bt_im, d_cm_re, d_cm_im))

    grad_x, d_w_in_t, d_g1 = _in_proj_bwd(x2d, dh1, pre_norm_g, w_in_t, du_scan, dzs, dq, dk, dv, dza, n_seq, seq)
    grads = dict(
        pre_norm_g=d_g1, w_in=d_w_in_t, ssm_lam_re=d_lam_re, ssm_lam_im=d_lam_im, ssm_log_step=d_log_step,
        ssm_b_re=d_b_re, ssm_b_im=d_b_im, ssm_c_re=d_c_re, ssm_c_im=d_c_im, ssm_d=d_d, ssm_w_glu=d_w_glu,
        ssm_b_glu=d_b_glu, attn_sinks=d_sinks, w_out=d_w_out, post_norm_g=d_g2, pl_w_proj=d_w_proj,
        pl_w_gate=d_w_gate, pl_b_gate=d_b_gate)
    return grad_x.reshape(x.shape), loss, grads


_BIG = ("w_in", "ssm_w_glu", "w_out", "pl_w_proj", "pl_w_gate")
_BIG_SHARD = {"w_in": (D_IN // N_CHIPS, D_MODEL), "ssm_w_glu": (D_SSM // N_CHIPS, D_SSM),
              "w_out": (D_MODEL // N_CHIPS, D_MODEL), "pl_w_proj": (D_PLE, D_MODEL // N_CHIPS),
              "pl_w_gate": (D_MODEL // N_CHIPS, D_MODEL)}
_SMALL = {"pre_norm_g": (1, D_MODEL), "ssm_lam_re": (SSM_GROUPS, SSM_STATE), "ssm_lam_im": (SSM_GROUPS, SSM_STATE),
          "ssm_log_step": (1, SSM_GROUPS), "ssm_b_re": (D_SSM, SSM_STATE), "ssm_b_im": (D_SSM, SSM_STATE),
          "ssm_c_re": (D_SSM, SSM_STATE), "ssm_c_im": (D_SSM, SSM_STATE), "ssm_d": (1, D_SSM), "ssm_b_glu": (1, D_SSM),
          "attn_sinks": (1, N_HEADS), "post_norm_g": (1, D_MODEL), "pl_b_gate": (1, D_MODEL)}
_VEC_ROWS = ("pre_norm_g", "post_norm_g", "pl_b_gate", "ssm_d", "ssm_b_glu", "attn_sinks", "ssm_log_step", "loss")
_SMALL_GROUPS = (
    ("vec", (8, D_MODEL), tuple((name, r) for r, name in enumerate(_VEC_ROWS))),
    ("lam", (2 * SSM_GROUPS, SSM_STATE), (("ssm_lam_re", 0), ("ssm_lam_im", SSM_GROUPS))),
)
_SMALL_EARLY = ("ssm_b_re", "ssm_b_im", "ssm_c_re", "ssm_c_im")
_SMALL_ORDER = tuple(name for _, _, members in _SMALL_GROUPS for name, _ in members) + _SMALL_EARLY
_WEIGHT_ORDER = ("pre_norm_g", "w_in", "ssm_lam_re", "ssm_lam_im", "ssm_log_step", "ssm_b_re", "ssm_b_im", "ssm_c_re",
                 "ssm_c_im", "ssm_d", "ssm_w_glu", "ssm_b_glu", "attn_sinks", "w_out", "post_norm_g", "pl_w_proj",
                 "pl_w_gate", "pl_b_gate")


def _small_shape(name):
    return (1, 1) if name == "loss" else _SMALL[name]


def _to_kernel_form(name, a):
    a = a[0]
    if name == "w_in":
        return a.T
    if name in ("ssm_b_re", "ssm_b_im"):
        a = a.transpose(0, 2, 1)
    return a.reshape(_SMALL[name]) if name in _SMALL else a


def _from_kernel_form(name, a, shape):
    if name == "w_in":
        a = a.T
    if name in ("ssm_b_re", "ssm_b_im"):
        a = a.reshape(SSM_GROUPS, SSM_GROUP_CH, SSM_STATE).transpose(0, 2, 1)
    return a.reshape(shape)


def _mesh_place():
    x, y, c = lax.axis_index("x"), lax.axis_index("y"), lax.axis_index("c")
    other_chips = ((1 - x, y), (x, 1 - y), (1 - x, 1 - y))
    return x, y, c, other_chips


def _gather_copies(s_refs, g_refs, send_sems, recv_sems, local_sems):
    x, y, c, other_chips = _mesh_place()
    started = []
    for i, (s_ref, g_ref) in enumerate(zip(s_refs, g_refs)):
        rows = s_ref.shape[0]
        half = rows // 2

        def block(chip, g_ref=g_ref, rows=rows, half=half):
            return g_ref.at[pl.ds((2 * chip[0] + chip[1]) * rows + c * half, half), :]

        def copy(k, chip, to, src=None, i=i, block=block):
            return pltpu.make_async_remote_copy(
                src_ref=block(chip) if src is None else src, dst_ref=block(chip), send_sem=send_sems.at[6 * i + k],
                recv_sem=recv_sems.at[6 * i + k], device_id=to, device_id_type=MESH)

        own = pltpu.make_async_copy(s_ref, g_ref.at[pl.ds((2 * x + y) * rows, rows), :], local_sems.at[i])
        own.start()
        first = [copy(k, (x, y), (*chip, c), src=s_ref.at[pl.ds(c * half, half), :])
                 for k, chip in enumerate(other_chips)]
        for cp in first:
            cp.start()
        passed = [copy(3 + k, chip, (x, y, 1 - c)) for k, chip in enumerate(other_chips)]
        started.append((own, first, passed))
    for own, first, passed in started:
        for k in range(3):
            first[k].wait_recv()
            passed[k].start()
    for own, first, passed in started:
        for k in range(3):
            passed[k].wait_recv()
        for cp in first + passed:
            cp.wait_send()
        own.wait()


def _gather_semaphores(n_t):
    return [pltpu.SemaphoreType.DMA((6 * n_t,)), pltpu.SemaphoreType.DMA((6 * n_t,)), pltpu.SemaphoreType.DMA((n_t,))]


def _gather_weights_beside(shards, name, collective_id):
    n_t = len(shards)
    hbm = pltpu.MemorySpace.HBM
    s_refs = [jax.new_ref(s, memory_space=hbm) for s in shards]
    g_refs = [jax.empty_ref(jax.ShapeDtypeStruct((N_CHIPS * s.shape[0], s.shape[1]), s.dtype), memory_space=hbm)
              for s in shards]

    def launch(send_sems, recv_sems, local_sems):
        x, y, c, other_chips = _mesh_place()
        peers = [(*chip, c) for chip in other_chips] + [(x, y, 1 - c)]
        barrier = pltpu.get_barrier_semaphore()
        for peer in peers:
            pl.semaphore_signal(barrier, inc=1, device_id=peer, device_id_type=MESH)
        pl.semaphore_wait(barrier, len(peers))
        _gather_copies(s_refs, g_refs, send_sems, recv_sems, local_sems)

    pl.kernel(launch, mesh=plsc.ScalarSubcoreMesh(axis_name="sequencer", num_cores=1), name=name,
              scratch_types=_gather_semaphores(n_t), compiler_params=pltpu.CompilerParams(collective_id=collective_id))()
    return [g[...] for g in g_refs]


_RELATIONS = tuple(((r >> 2) & 1, (r >> 1) & 1, r & 1) for r in range(1, 8))


def _related(place, relation):
    return tuple(1 - a if flip else a for a, flip in zip(place, relation))


def _scatter_beside(mats):
    hbm = pltpu.MemorySpace.HBM
    src_refs = [jax.new_ref(a, memory_space=hbm) for a in mats]
    land_refs = [jax.empty_ref(jax.ShapeDtypeStruct((7, a.shape[0] // 8, a.shape[1]), a.dtype), memory_space=hbm)
                 for a in mats]

    def launch(send_sems, recv_sems):
        me = (lax.axis_index("x"), lax.axis_index("y"), lax.axis_index("c"))
        peers = [_related(me, rel) for rel in _RELATIONS]
        barrier = pltpu.get_barrier_semaphore()
        for peer in peers:
            pl.semaphore_signal(barrier, inc=1, device_id=peer, device_id_type=MESH)
        pl.semaphore_wait(barrier, len(peers))
        copies = []
        for i, (src, land) in enumerate(zip(src_refs, land_refs)):
            hr = land.shape[1]
            for k, (tx, ty, tc) in enumerate(peers):
                rows = pl.ds((2 * tx + ty) * 2 * hr + tc * hr, hr)
                copies.append(pltpu.make_async_remote_copy(
                    src_ref=src.at[rows, :], dst_ref=land.at[k], send_sem=send_sems.at[7 * i + k],
                    recv_sem=recv_sems.at[7 * i + k], device_id=(tx, ty, tc), device_id_type=MESH))
                copies[-1].start()
        for cp in copies:
            cp.wait()

    n_sems = 7 * len(mats)
    pl.kernel(launch, mesh=plsc.ScalarSubcoreMesh(axis_name="sequencer", num_cores=1), name="scatter_beside",
              scratch_types=[pltpu.SemaphoreType.DMA((n_sems,)), pltpu.SemaphoreType.DMA((n_sems,))],
              compiler_params=pltpu.CompilerParams(collective_id=2))()
    return [ref[...] for ref in land_refs]


def _broadcast_beside(arrays):
    hbm = pltpu.MemorySpace.HBM
    src_refs = [jax.new_ref(a, memory_space=hbm) for a in arrays]
    land_refs = [jax.empty_ref(jax.ShapeDtypeStruct((len(_RELATIONS),) + a.shape, a.dtype), memory_space=hbm)
                 for a in arrays]

    def launch(send_sems, recv_sems):
        me = (lax.axis_index("x"), lax.axis_index("y"), lax.axis_index("c"))
        peers = [_related(me, rel) for rel in _RELATIONS]
        barrier = pltpu.get_barrier_semaphore()
        for peer in peers:
            pl.semaphore_signal(barrier, inc=1, device_id=peer, device_id_type=MESH)
        pl.semaphore_wait(barrier, len(peers))
        copies = []
        for i, (src, land) in enumerate(zip(src_refs, land_refs)):
            for k, peer in enumerate(peers):
                copies.append(pltpu.make_async_remote_copy(
                    src_ref=src, dst_ref=land.at[k], send_sem=send_sems.at[7 * i + k],
                    recv_sem=recv_sems.at[7 * i + k], device_id=peer, device_id_type=MESH))
                copies[-1].start()
        for cp in copies:
            cp.wait()

    n_sems = 7 * len(arrays)
    pl.kernel(launch, mesh=plsc.ScalarSubcoreMesh(axis_name="sequencer", num_cores=1), name="broadcast_beside",
              scratch_types=[pltpu.SemaphoreType.DMA((n_sems,)), pltpu.SemaphoreType.DMA((n_sems,))],
              compiler_params=pltpu.CompilerParams(collective_id=3))()
    return [ref[...] for ref in land_refs]


def _exchange_grads(big, small, landed, landed_small):
    n_t = len(big)
    n_g = len(_SMALL_GROUPS)
    names = _SMALL_ORDER
    halves = [(b.shape[0] // N_CHIPS // 2, b.shape[1]) for b in big]
    early = sorted(landed)
    late = [i for i in range(n_t) if i not in landed]
    n_sems = 4 * n_g + 7 * len(late) + n_t
    small_sem0, block_sem0 = n_t, n_t + len(names)
    early_sem0 = block_sem0 + N_CHIPS * len(late)
    landed_sem0 = early_sem0 + 2 * len(early)
    early_small = [n for n in names if n in landed_small]

    def body(*refs):
        pos = 0

        def take(n):
            nonlocal pos
            pos += n
            return refs[pos - n:pos]

        big_refs, small_refs = take(n_t), dict(zip(names, take(len(names))))
        land_refs = dict(zip(early, take(len(early))))
        land_small_refs = dict(zip(early_small, take(len(early_small))))
        out_refs, small_out_refs = take(n_t), dict(zip(names, take(len(names))))
        per_late = lambda: dict(zip(late, take(len(late))))
        ga, gb, pme, send_b, recv_b = per_late(), per_late(), take(n_t), per_late(), per_late()
        own_e, land_e = dict(zip(early, take(len(early)))), dict(zip(early, take(len(early))))
        land_s = dict(zip(early_small, take(len(early_small))))
        s_own, s_sib, s_chips, s_pair = take(n_g), take(n_g), take(n_g), take(n_g)
        stage = dict(zip(names, take(len(names))))
        send_sems, recv_sems, local_sems = take(3)
        x, y, c, other_chips = _mesh_place()
        me = 2 * x + y
        sibling = (x, y, 1 - c)
        sem_at = iter(range(n_sems))

        def remote(src, dst, to):
            k = next(sem_at)
            return pltpu.make_async_remote_copy(src_ref=src, dst_ref=dst, send_sem=send_sems.at[k],
                                                recv_sem=recv_sems.at[k], device_id=to, device_id_type=MESH)

        loads = [pltpu.make_async_copy(small_refs[name], stage[name], local_sems.at[small_sem0 + a])
                 for a, name in enumerate(names)]
        landed_loads = [pltpu.make_async_copy(land_small_refs[name], land_s[name], local_sems.at[landed_sem0 + a])
                        for a, name in enumerate(early_small)]
        for cp in loads + landed_loads:
            cp.start()
        for cp in loads:
            cp.wait()
        small_swaps = []
        for gi, (_, _, members) in enumerate(_SMALL_GROUPS):
            s_own[gi][...] = jnp.zeros_like(s_own[gi])
            for name, r0 in members:
                r, n = _small_shape(name)
                s_own[gi][r0:r0 + r, 0:n] = stage[name][...]
            small_swaps.append(remote(s_own[gi], s_sib[gi], sibling))
            small_swaps[gi].start()
        order = sorted(late, key=lambda i: halves[i][0] * halves[i][1])
        own_loads, big_swaps = {}, {}
        for i in order:
            hr = halves[i][0]
            own_loads[i], big_swaps[i] = [], []
            for j in range(N_CHIPS):
                mine = big_refs[i].at[pl.ds(j * 2 * hr + c * hr, hr), :]
                theirs = big_refs[i].at[pl.ds(j * 2 * hr + (1 - c) * hr, hr), :]
                sem = local_sems.at[block_sem0 + N_CHIPS * late.index(i) + j]
                own_loads[i].append(pltpu.make_async_copy(mine, ga[i].at[j], sem))
                own_loads[i][j].start()
                big_swaps[i].append(remote(theirs, gb[i].at[j], sibling))
                big_swaps[i][j].start()
        early_loads = {}
        for e, i in enumerate(early):
            hr = halves[i][0]
            mine = big_refs[i].at[pl.ds(me * 2 * hr + c * hr, hr), :]
            early_loads[i] = [pltpu.make_async_copy(mine, own_e[i], local_sems.at[early_sem0 + 2 * e]),
                              pltpu.make_async_copy(land_refs[i], land_e[i], local_sems.at[early_sem0 + 2 * e + 1])]
            for cp in early_loads[i]:
                cp.start()
        small_sends = []
        for gi in range(n_g):
            small_swaps[gi].wait_recv()
            s_pair[gi][...] = s_own[gi][...] + s_sib[gi][...]
            small_sends.append([remote(s_pair[gi], s_chips[gi].at[k], (*chip, c)) for k, chip in enumerate(other_chips)])
            for cp in small_sends[gi]:
                cp.start()

        def pair_sum(i, j):
            return ga[i][j] + gb[i][j]

        big_sends = {}
        for i in order:
            for j in range(N_CHIPS):
                own_loads[i][j].wait()
                big_swaps[i][j].wait_recv()
            big_sends[i] = []
            for k, chip in enumerate(other_chips):
                send_b[i][k] = pair_sum(i, 2 * chip[0] + chip[1]).astype(BF16)
                big_sends[i].append(remote(send_b[i].at[k], recv_b[i].at[k], (*chip, c)))
                big_sends[i][k].start()
        last_swaps, keeps = {}, {}
        for i in early + order:
            hr = halves[i][0]
            if i in landed:
                for cp in early_loads[i]:
                    cp.wait()
                total = own_e[i][...]
                for k in range(len(_RELATIONS)):
                    total = total + land_e[i][k]
                pme[i][...] = total
            else:
                for k in range(3):
                    big_sends[i][k].wait_recv()
                pme[i][...] = ((pair_sum(i, me) + recv_b[i][0].astype(F32)) + recv_b[i][1].astype(F32)) + recv_b[i][2].astype(F32)
            mine = out_refs[i].at[pl.ds(c * hr, hr), :]
            keeps[i] = pltpu.make_async_copy(pme[i], mine, local_sems.at[i])
            keeps[i].start()
            last_swaps[i] = remote(pme[i], mine, sibling)
            last_swaps[i].start()

        for gi, (_, _, members) in enumerate(_SMALL_GROUPS):
            for k in range(3):
                small_sends[gi][k].wait_recv()
            total = None
            for j in range(N_CHIPS):
                rel = jnp.bitwise_xor(j, me)
                term = jnp.where(rel == 0, s_pair[gi][...], jnp.where(
                    rel == 2, s_chips[gi][0], jnp.where(rel == 1, s_chips[gi][1], s_chips[gi][2])))
                total = term if total is None else total + term
            s_sib[gi][...] = total
            for name, r0 in members:
                r, n = _small_shape(name)
                stage[name][...] = s_sib[gi][r0:r0 + r, 0:n]
        my_index = 4 * x + 2 * y + c
        for cp in landed_loads:
            cp.wait()
        for name in early_small:
            total = None
            for d in range(2 * N_CHIPS):
                rel = jnp.bitwise_xor(d, my_index)
                term = stage[name][...]
                for k in range(len(_RELATIONS)):
                    term = jnp.where(rel == k + 1, land_s[name][k], term)
                total = term if total is None else total + term
            stage[name][...] = total
        stores = [pltpu.make_async_copy(stage[name], small_out_refs[name], local_sems.at[small_sem0 + a])
                  for a, name in enumerate(names)]
        for cp in stores:
            cp.start()

        for i in range(n_t):
            last_swaps[i].wait_recv()
            keeps[i].wait()
        for cp in stores:
            cp.wait()
        groups = list(big_swaps.values()) + small_sends + list(big_sends.values())
        for cp in small_swaps + [cp for group in groups for cp in group] + list(last_swaps.values()):
            cp.wait_send()

    any_spec = pl.BlockSpec(memory_space=pl.ANY)
    small_shapes = [_sds(_small_shape(n)) for n in names]
    group_shapes = [shape for _, shape, _ in _SMALL_GROUPS]
    vmem = lambda which, dtype, lead=(): [pltpu.VMEM(lead + halves[i], dtype) for i in which]
    outs = _call(
        body, name="exchange_grads",
        in_specs=[any_spec] * (n_t + len(names) + len(early) + len(early_small)),
        out_specs=[any_spec] * (n_t + len(names)),
        out_shape=[_sds((b.shape[0] // N_CHIPS, b.shape[1])) for b in big] + small_shapes,
        scratch_shapes=(vmem(late, F32, (N_CHIPS,)) + vmem(late, F32, (N_CHIPS,)) + vmem(range(n_t), F32)
                        + vmem(late, BF16, (3,)) + vmem(late, BF16, (3,))
                        + vmem(early, F32) + vmem(early, F32, (len(_RELATIONS),))
                        + [pltpu.VMEM((len(_RELATIONS),) + _small_shape(n), F32) for n in early_small]
                        + [pltpu.VMEM(s, F32) for s in group_shapes] * 2 + [pltpu.VMEM((3,) + s, F32) for s in group_shapes]
                        + [pltpu.VMEM(s, F32) for s in group_shapes]
                        + [pltpu.VMEM(_small_shape(n), F32) for n in names]
                        + [pltpu.SemaphoreType.DMA((n_sems,)), pltpu.SemaphoreType.DMA((n_sems,)),
                           pltpu.SemaphoreType.DMA((landed_sem0 + len(early_small),))]),
        compiler_params=_params(48),
    )(*big, *[small[n] for n in names], *[landed[i] for i in early], *[landed_small[n] for n in early_small])
    return list(outs[:n_t]), dict(zip(names, outs[n_t:n_t + len(names)]))


def _adamw_update(w, g, m, v):
    m = ADAM_B1 * m + (1.0 - ADAM_B1) * g
    v = ADAM_B2 * v + (1.0 - ADAM_B2) * (g * g)
    m_hat = m / (1.0 - ADAM_B1 ** ADAM_STEP)
    v_hat = v / (1.0 - ADAM_B2 ** ADAM_STEP)
    return -ADAM_LR * (m_hat / (jnp.sqrt(v_hat) + ADAM_EPS) + ADAM_WD * w), m, v


def _adamw(w, g, m, v, grid, name):
    n_t = len(w)

    def body(*refs):
        ins, outs = refs[:4 * n_t], refs[4 * n_t:]
        for i in range(n_t):
            w_, g_, m_, v_ = [ins[a * n_t + i][...] for a in range(4)]
            vals = (g_,) + _adamw_update(w_, g_, m_, v_)
            for a in range(4):
                outs[a * n_t + i][...] = vals[a]

    specs = [pl.BlockSpec((a.shape[0] // grid, a.shape[1]), lambda i: (i, 0)) for a in w]
    shapes = [_sds(a.shape) for a in w]
    outs = _call(
        body, name=name, grid=(grid,), in_specs=specs * 4, out_specs=specs * 4, out_shape=shapes * 4,
        compiler_params=_params(40, ("arbitrary",)),
    )(*w, *g, *m, *v)
    return [outs[a * n_t:(a + 1) * n_t] for a in range(4)]


def kernel(x, p, pre_norm_g, w_in, ssm_lam_re, ssm_lam_im, ssm_log_step, ssm_b_re, ssm_b_im, ssm_c_re, ssm_c_im, ssm_d, ssm_w_glu, ssm_b_glu, attn_sinks, w_out, post_norm_g, pl_w_proj, pl_w_gate, pl_b_gate, loss_target, m_pre_norm_g, m_w_in, m_ssm_lam_re, m_ssm_lam_im, m_ssm_log_step, m_ssm_b_re, m_ssm_b_im, m_ssm_c_re, m_ssm_c_im, m_ssm_d, m_ssm_w_glu, m_ssm_b_glu, m_attn_sinks, m_w_out, m_post_norm_g, m_pl_w_proj, m_pl_w_gate, m_pl_b_gate, v_pre_norm_g, v_w_in, v_ssm_lam_re, v_ssm_lam_im, v_ssm_log_step, v_ssm_b_re, v_ssm_b_im, v_ssm_c_re, v_ssm_c_im, v_ssm_d, v_ssm_w_glu, v_ssm_b_glu, v_attn_sinks, v_w_out, v_post_norm_g, v_pl_w_proj, v_pl_w_gate, v_pl_b_gate):
    weights = dict(pre_norm_g=pre_norm_g, w_in=w_in, ssm_lam_re=ssm_lam_re, ssm_lam_im=ssm_lam_im,
                   ssm_log_step=ssm_log_step, ssm_b_re=ssm_b_re, ssm_b_im=ssm_b_im, ssm_c_re=ssm_c_re,
                   ssm_c_im=ssm_c_im, ssm_d=ssm_d, ssm_w_glu=ssm_w_glu, ssm_b_glu=ssm_b_glu, attn_sinks=attn_sinks,
                   w_out=w_out, post_norm_g=post_norm_g, pl_w_proj=pl_w_proj, pl_w_gate=pl_w_gate, pl_b_gate=pl_b_gate)
    m_in = dict(pre_norm_g=m_pre_norm_g, w_in=m_w_in, ssm_lam_re=m_ssm_lam_re, ssm_lam_im=m_ssm_lam_im,
                ssm_log_step=m_ssm_log_step, ssm_b_re=m_ssm_b_re, ssm_b_im=m_ssm_b_im, ssm_c_re=m_ssm_c_re,
                ssm_c_im=m_ssm_c_im, ssm_d=m_ssm_d, ssm_w_glu=m_ssm_w_glu, ssm_b_glu=m_ssm_b_glu,
                attn_sinks=m_attn_sinks, w_out=m_w_out, post_norm_g=m_post_norm_g, pl_w_proj=m_pl_w_proj,
                pl_w_gate=m_pl_w_gate, pl_b_gate=m_pl_b_gate)
    v_in = dict(pre_norm_g=v_pre_norm_g, w_in=v_w_in, ssm_lam_re=v_ssm_lam_re, ssm_lam_im=v_ssm_lam_im,
                ssm_log_step=v_ssm_log_step, ssm_b_re=v_ssm_b_re, ssm_b_im=v_ssm_b_im, ssm_c_re=v_ssm_c_re,
                ssm_c_im=v_ssm_c_im, ssm_d=v_ssm_d, ssm_w_glu=v_ssm_w_glu, ssm_b_glu=v_ssm_b_glu,
                attn_sinks=v_attn_sinks, w_out=v_w_out, post_norm_g=v_post_norm_g, pl_w_proj=v_pl_w_proj,
                pl_w_gate=v_pl_w_gate, pl_b_gate=v_pl_b_gate)

    def two_d(tree):
        return {k: _to_kernel_form(k, a) for k, a in tree.items()}

    w2, m2, v2 = two_d(weights), two_d(m_in), two_d(v_in)

    (w_in_full,) = _gather_weights_beside([w2["w_in"].astype(BF16)], "gather_w_in_beside", 4)
    s5_params = tuple(w2[n] for n in ("ssm_lam_re", "ssm_lam_im", "ssm_log_step", "ssm_b_re", "ssm_b_im", "ssm_c_re",
                                      "ssm_c_im"))
    s5_operands = _s5_params_fwd(*s5_params)
    hn = _pre_norm(x.reshape(-1, D_MODEL), w2["pre_norm_g"])
    behind = s5_operands[0][0, 0] * 0.0 + hn[0, 0].astype(F32) * 0.0
    rest = _gather_weights_beside([(w2[n] + behind).astype(BF16) for n in _BIG[1:]], "gather_weights_beside", 1)
    full = dict(zip(_BIG, [w_in_full] + rest))
    grad_x, loss, grads = _local_step(
        x, hn, p, loss_target, w2["pre_norm_g"], full["w_in"], s5_params, s5_operands, w2["ssm_d"], full["ssm_w_glu"], w2["ssm_b_glu"],
        w2["attn_sinks"], full["w_out"], w2["post_norm_g"], full["pl_w_proj"], full["pl_w_gate"], w2["pl_b_gate"])

    sent_early = ("w_out", "pl_w_gate", "pl_w_proj")
    landed = dict(zip([_BIG.index(n) for n in sent_early], _scatter_beside([grads[n] for n in sent_early])))
    after_scatter = landed[_BIG.index(sent_early[-1])][0, 0, 0] * 0.0
    late_operands = [grads[_SMALL_EARLY[0]] + after_scatter] + [grads[n] for n in _SMALL_EARLY[1:]]
    landed_small = dict(zip(_SMALL_EARLY, _broadcast_beside(late_operands)))
    g_big, g_small = _exchange_grads([grads[n] for n in _BIG], {**{n: grads[n] for n in _SMALL}, "loss": loss}, landed,
                                     landed_small)
    g_big = dict(zip(_BIG, g_big))
    total_loss = g_small.pop("loss")

    big_out = _adamw([w2[n] for n in _BIG], [g_big[n] for n in _BIG], [m2[n] for n in _BIG], [v2[n] for n in _BIG],
                     8, "adamw_matrices")
    small_names = tuple(_SMALL)
    small_out = _adamw([w2[n] for n in small_names], [g_small[n] for n in small_names], [m2[n] for n in small_names],
                       [v2[n] for n in small_names], 1, "adamw_small")

    results = [{**dict(zip(_BIG, big_part)), **dict(zip(small_names, small_part))}
               for big_part, small_part in zip(big_out, small_out)]
    flat = [_from_kernel_form(name, r[name], weights[name].shape) for r in results for name in _WEIGHT_ORDER]
    return (total_loss.reshape(()), grad_x, *flat)
```

```python
import math

import jax
import jax.numpy as jnp
from jax import lax
from jax.experimental import pallas as pl
from jax.experimental.pallas import tpu as pltpu
from jax.experimental.pallas import tpu_sc as plsc

F32 = jnp.float32
BF16 = jnp.bfloat16

D_MODEL = 1024
D_SSM = 512
D_ATTN = 512
SSM_GROUPS = 32
SSM_GROUP_CH = 16
SSM_STATE = 64
SSM_LANES = SSM_GROUPS * SSM_STATE
HEAD_DIM = 64
N_HEADS = 8
KV_HEADS = 2
Q_PER_KV = 4
WINDOW = 128
BLOCK = 128
D_PLE = 256
D_IN = 2304
EPS = 1e-6
ATTN_SCALE = 1.0 / math.sqrt(HEAD_DIM)

ADAM_LR = 0.001
ADAM_B1 = 0.9
ADAM_B2 = 0.999
ADAM_EPS = 1e-08
ADAM_WD = 0.01
ADAM_STEP = 10

N_CHIPS = 4
LANES = 128
SCAN_CHUNKS = 8
SCAN_TILE_STEPS = 32
SCAN_LANE_CHUNK = 512
MIB = 2 ** 20
MESH = pl.DeviceIdType.MESH


def _dot(a, b):
    return jnp.dot(a, b, preferred_element_type=F32)


def _dot_nt(a, b):
    return lax.dot_general(a, b, (((1,), (1,)), ((), ())), preferred_element_type=F32)


def _dot_tn(a, b):
    return lax.dot_general(a, b, (((0,), (0,)), ((), ())), preferred_element_type=F32)


def _params(vmem_mib, semantics=None):
    kw = dict(vmem_limit_bytes=vmem_mib * MIB)
    if semantics is not None:
        kw["dimension_semantics"] = semantics
    return pltpu.CompilerParams(**kw)


def _full(shape):
    nd = len(shape)
    return pl.BlockSpec(shape, lambda *_: (0,) * nd, pipeline_mode=pl.Buffered(1))


def _rows(tm, width):
    return pl.BlockSpec((tm, width), lambda i: (i, 0))


def _sds(shape, dtype=F32):
    return pltpu.HBM(shape, dtype)


def _call(body, **kw):
    fn = pl.pallas_call(body, **kw)
    return lambda *args: fn(*[pltpu.with_memory_space_constraint(a, pltpu.HBM) for a in args])


def _silu(z):
    return z * jax.nn.sigmoid(z)


def _pre_norm(x2d, g1):
    rows = x2d.shape[0]
    tm = 512

    def body(x_ref, g_ref, hn_ref):
        x = x_ref[...]
        r = lax.rsqrt(jnp.mean(x * x, axis=-1, keepdims=True) + EPS)
        hn_ref[...] = (x * r * g_ref[...]).astype(BF16)

    return _call(
        body, name="pre_norm", grid=(rows // tm,), in_specs=[_rows(tm, D_MODEL), _full((1, D_MODEL))],
        out_specs=_rows(tm, D_MODEL), out_shape=_sds((rows, D_MODEL), BF16), compiler_params=_params(32, ("arbitrary",)),
    )(x2d, g1)


def _in_proj(hn, w_in_t, n_seq, seq):
    rows = hn.shape[0]
    tm = 512
    slab, steps, _, _ = _scan_geometry(n_seq, seq)

    def body(hn_ref, w_ref, *out_refs):
        u_parts, (zs_ref, q_ref, k_ref, v_ref, za_ref) = out_refs[:_SCAN_PARTS], out_refs[_SCAN_PARTS:]
        whole = _dot_nt(hn_ref[...], w_ref[...])

        def proj(a, b):
            return whole[:, a:b]

        _store_chunks(u_parts, pl.program_id(0) * (tm // steps), proj(0, 512), steps, slab)
        zs_ref[...] = proj(512, 1024)
        q_ref[...] = (proj(1024, 1536) * ATTN_SCALE).astype(BF16)
        k_ref[...] = proj(1536, 1664).astype(BF16)
        v_ref[...] = proj(1664, 1792).astype(BF16)
        za_ref[...] = proj(1792, 2304)

    *u_parts, zs, q, k, v, za = _call(
        body, name="in_proj", grid=(rows // tm,),
        in_specs=[_rows(tm, D_MODEL), _full((D_IN, D_MODEL))],
        out_specs=_whole_parts(rows) + [_rows(tm, 512), _rows(tm, 512), _rows(tm, 128), _rows(tm, 128), _rows(tm, 512)],
        out_shape=_part_shapes(rows) + [_sds((rows, 512)), _sds((rows, 512), BF16), _sds((rows, 128), BF16),
                                        _sds((rows, 128), BF16), _sds((rows, 512))],
        compiler_params=_params(48, ("arbitrary",)),
    )(hn, w_in_t)
    return u_parts, zs, q, k, v, za


def _in_proj_bwd(x2d, dh1, g1, w_in_t, du_parts, dzs, dq, dk, dv, dza, n_seq, seq):
    rows = x2d.shape[0]
    tm = 512
    slab, steps, _, _ = _scan_geometry(n_seq, seq)

    def body(x_ref, dh1_ref, g_ref, w_ref, *refs):
        du_parts, (dzs_ref, dq_ref, dk_ref, dv_ref, dza_ref, gx_ref, dw_ref, dg_ref) = refs[:_SCAN_PARTS], refs[_SCAN_PARTS:]

        @pl.when(pl.program_id(0) == 0)
        def _():
            dw_ref[...] = jnp.zeros_like(dw_ref)
            dg_ref[...] = jnp.zeros_like(dg_ref)

        x = x_ref[...]
        g = g_ref[...]
        r = lax.rsqrt(jnp.mean(x * x, axis=-1, keepdims=True) + EPS)
        xr = x * r
        hn = (xr * g).astype(BF16)
        du = _load_chunks(du_parts, pl.program_id(0) * (tm // steps), tm // steps, steps, slab)
        d_proj = jnp.concatenate([du.astype(BF16), dzs_ref[...], dq_ref[...], dk_ref[...], dv_ref[...], dza_ref[...]],
                                 axis=1)
        dhn = _dot(d_proj, w_ref[...])
        dw_ref[...] += _dot_tn(d_proj, hn)
        dg_ref[...] += jnp.sum(dhn * xr, axis=0, keepdims=True)
        a_ = dhn * g
        gx_ref[...] = dh1_ref[...] + r * a_ - xr * (r * jnp.mean(a_ * xr, axis=-1, keepdims=True))

    return _call(
        body, name="in_proj_bwd", grid=(rows // tm,),
        in_specs=[_rows(tm, D_MODEL), _rows(tm, D_MODEL), _full((1, D_MODEL)), _full((D_IN, D_MODEL))]
        + _whole_parts(rows) + [_rows(tm, 512), _rows(tm, 512), _rows(tm, 128), _rows(tm, 128), _rows(tm, 512)],
        out_specs=[_rows(tm, D_MODEL), _full((D_IN, D_MODEL)), _full((1, D_MODEL))],
        out_shape=[_sds((rows, D_MODEL)), _sds((D_IN, D_MODEL)), _sds((1, D_MODEL))],
        compiler_params=_params(56, ("arbitrary",)),
    )(x2d, dh1, g1, w_in_t, *du_parts, dzs, dq, dk, dv, dza)


def _iota(shape, axis):
    return lax.broadcasted_iota(jnp.int32, shape, axis)


def _sum_of_thirds(f, a):
    hi = a.astype(BF16)
    rest = a - hi.astype(F32)
    mid = rest.astype(BF16)
    low = (rest - mid.astype(F32)).astype(BF16)
    return (f(hi) + f(mid)) + f(low)


@jax.custom_vjp
def _pick_rows(e, a):
    return _sum_of_thirds(lambda part: _dot(e, part), a)


def _pick_rows_fwd(e, a):
    return _pick_rows(e, a), e


def _pick_rows_bwd(e, ct):
    return jnp.zeros_like(e), _sum_of_thirds(lambda part: _dot_tn(e, part), ct)


_pick_rows.defvjp(_pick_rows_fwd, _pick_rows_bwd)


@jax.custom_vjp
def _pick_cols(a, e):
    return _sum_of_thirds(lambda part: _dot(part, e), a)


def _pick_cols_fwd(a, e):
    return _pick_cols(a, e), e


def _pick_cols_bwd(e, ct):
    return _sum_of_thirds(lambda part: _dot_nt(part, e), ct), jnp.zeros_like(e)


_pick_cols.defvjp(_pick_cols_fwd, _pick_cols_bwd)


_HALF_GROUPS = SSM_GROUPS // 2
_N_SHIFT = SSM_STATE.bit_length() - 1
_P_SHIFT = SSM_GROUP_CH.bit_length() - 1


def _s5_operands(lam_re, lam_im, log_step, b_re, b_im, c_re, c_im):
    g, n, p = SSM_GROUPS, SSM_STATE, SSM_GROUP_CH
    gn, gp, hn_, hp = g * n, g * p, _HALF_GROUPS * n, _HALF_GROUPS * p
    eye_g = _iota((g, g), 0) == _iota((g, g), 1)
    step = jnp.sum(jnp.where(eye_g, jnp.exp(log_step), 0.0), axis=1, keepdims=True)
    a_re = lam_re * step
    a_im = lam_im * step
    mag = jnp.exp(a_re)
    lbar_re = mag * jnp.cos(a_im)
    lbar_im = mag * jnp.sin(a_im)
    n_re = lbar_re - 1.0
    den = lam_re * lam_re + lam_im * lam_im
    f_re = (n_re * lam_re + lbar_im * lam_im) / den
    f_im = (lbar_im * lam_re - n_re * lam_im) / den

    spread_n = (_iota((n, gn), 0) == (_iota((n, gn), 1) & (n - 1))).astype(BF16)
    own_g = _iota((g, gn), 0) == (_iota((g, gn), 1) >> _N_SHIFT)

    def to_row(a):
        return jnp.sum(jnp.where(own_g, _pick_cols(a, spread_n), 0.0), axis=0, keepdims=True)

    per_group = ((_iota((gp, g), 0) >> _P_SHIFT) == _iota((gp, g), 1)).astype(BF16)
    fx_re, fx_im = _pick_rows(per_group, f_re), _pick_rows(per_group, f_im)
    bbar_re = fx_re * b_re - fx_im * b_im
    bbar_im = fx_re * b_im + fx_im * b_re

    tile_n = (_iota((n, hn_), 0) == (_iota((n, hn_), 1) & (n - 1))).astype(BF16)
    same_group = (_iota((hp, hn_), 0) >> _P_SHIFT) == (_iota((hp, hn_), 1) >> _N_SHIFT)

    def embed(a, hf):
        return jnp.where(same_group, _pick_cols(a[hf * hp:(hf + 1) * hp], tile_n), 0.0)

    return (to_row(lbar_re), to_row(lbar_im), embed(bbar_re, 0), embed(bbar_re, 1), embed(bbar_im, 0),
            embed(bbar_im, 1), embed(c_re, 0), embed(c_re, 1), embed(c_im, 0), embed(c_im, 1))


_S5_PARAM_SHAPES = ((SSM_GROUPS, SSM_STATE), (SSM_GROUPS, SSM_STATE), (1, SSM_GROUPS),
                    (D_SSM, SSM_STATE), (D_SSM, SSM_STATE), (D_SSM, SSM_STATE), (D_SSM, SSM_STATE))
_CM_SHAPE = (2, _HALF_GROUPS * SSM_GROUP_CH, _HALF_GROUPS * SSM_STATE)
_S5_OPERAND_SHAPES = ((1, SSM_LANES), (1, SSM_LANES), _CM_SHAPE, _CM_SHAPE, _CM_SHAPE, _CM_SHAPE)


def _s5_params_fwd(*params):
    def body(*refs):
        ins, (lre_ref, lim_ref, btre_ref, btim_ref, cmre_ref, cmim_ref) = refs[:7], refs[7:]
        vals = _s5_operands(*[r[...] for r in ins])
        lre_ref[...] = vals[0]
        lim_ref[...] = vals[1]
        for ref, pair in zip((btre_ref, btim_ref, cmre_ref, cmim_ref), (vals[2:4], vals[4:6], vals[6:8], vals[8:10])):
            ref[0] = pair[0].astype(BF16)
            ref[1] = pair[1].astype(BF16)

    dtypes = (F32, F32, BF16, BF16, BF16, BF16)
    return _call(
        body, name="s5_params_fwd",
        in_specs=[_full(s) for s in _S5_PARAM_SHAPES], out_specs=[_full(s) for s in _S5_OPERAND_SHAPES],
        out_shape=[_sds(s, d) for s, d in zip(_S5_OPERAND_SHAPES, dtypes)], compiler_params=_params(32),
    )(*params)


def _s5_params_bwd(params, cotangents):
    def body(*refs):
        ins, (dlre, dlim, dbtre, dbtim, dcmre, dcmim), outs = refs[:7], refs[7:13], refs[13:]
        _, vjp = jax.vjp(_s5_operands, *[r[...] for r in ins])
        cts = (dlre[...], dlim[...], dbtre[0], dbtre[1], dbtim[0], dbtim[1], dcmre[0], dcmre[1], dcmim[0], dcmim[1])
        for ref, val in zip(outs, vjp(cts)):
            ref[...] = val

    return _call(
        body, name="s5_params_bwd",
        in_specs=[_full(s) for s in _S5_PARAM_SHAPES + _S5_OPERAND_SHAPES],
        out_specs=[_full(s) for s in _S5_PARAM_SHAPES],
        out_shape=[_sds(s) for s in _S5_PARAM_SHAPES], compiler_params=_params(48),
    )(*params, *cotangents)


def _scan_geometry(n_seq, seq):
    slab = n_seq * SCAN_CHUNKS
    steps = seq // SCAN_CHUNKS
    tile_rows = slab * SCAN_TILE_STEPS
    n_tiles = steps // SCAN_TILE_STEPS
    return slab, steps, tile_rows, n_tiles


_SCAN_PARTS = D_SSM // LANES


def _whole_parts(rows):
    return [_full((rows, LANES))] * _SCAN_PARTS


def _part_shapes(rows):
    return [_sds((rows, LANES))] * _SCAN_PARTS


def _load_chunks(parts, first_chunk, n_chunks, steps, slab):
    return jnp.concatenate([
        jnp.concatenate([ref[pl.ds(first_chunk + q, steps, stride=slab), :] for ref in parts], axis=1)
        for q in range(n_chunks)], axis=0)


def _store_chunks(parts, first_chunk, value, steps, slab):
    for q in range(value.shape[0] // steps):
        for j, ref in enumerate(parts):
            ref[pl.ds(first_chunk + q, steps, stride=slab), :] = value[q * steps:(q + 1) * steps,
                                                                     j * LANES:(j + 1) * LANES]


def _join_parts(parts):
    return jnp.concatenate([ref[...] for ref in parts], axis=1)


def _split_parts(parts, value):
    for j, ref in enumerate(parts):
        ref[...] = value[:, j * LANES:(j + 1) * LANES]


def _complex_power(re, im, n):
    out = None
    while n:
        if n & 1:
            out = (re, im) if out is None else (out[0] * re - out[1] * im, out[0] * im + out[1] * re)
        n >>= 1
        if n:
            re, im = re * re - im * im, 2.0 * re * im
    return out


def _chunk_carry(sum_re, sum_im, carry_re, carry_im, a_re, a_im, n_seq, reverse):
    carry_re[...] = jnp.zeros_like(carry_re)
    carry_im[...] = jnp.zeros_like(carry_im)
    for s in range(n_seq):
        order = range(SCAN_CHUNKS - 2, -1, -1) if reverse else range(1, SCAN_CHUNKS)
        for c in order:
            r = s * SCAN_CHUNKS + c
            p = r + 1 if reverse else r - 1
            p_re, p_im = carry_re[p:p + 1, :], carry_im[p:p + 1, :]
            carry_re[r:r + 1, :] = a_re * p_re - a_im * p_im + sum_re[p:p + 1, :]
            carry_im[r:r + 1, :] = a_re * p_im + a_im * p_re + sum_im[p:p + 1, :]


def _s5_scan_fwd(u_parts, bt_re, bt_im, cm_re, cm_im, lbar_re, lbar_im, d_row, n_seq, seq):
    slab, steps, tile_rows, n_tiles = _scan_geometry(n_seq, seq)
    rows = u_parts[0].shape[0]

    def body(*refs):
        u_refs, refs = refs[:_SCAN_PARTS], refs[_SCAN_PARTS:]
        (bre_ref, bim_ref, cre_ref, cim_ref, lre_ref, lim_ref, d_ref), refs = refs[:7], refs[7:]
        y_refs, (hre_ref, him_ref, st_re, st_im, h0_re, h0_im, buf_re, buf_im) = refs[:_SCAN_PARTS], refs[_SCAN_PARTS:]
        second = pl.program_id(0) == 1
        i = pl.program_id(1)

        @pl.when(jnp.logical_and(i == 0, jnp.logical_not(second)))
        def _():
            st_re[...] = jnp.zeros_like(st_re)
            st_im[...] = jnp.zeros_like(st_im)

        u = _join_parts(u_refs)
        ub = u.astype(BF16)
        for hf in range(2):
            cols = slice(hf * 1024, (hf + 1) * 1024)
            buf_re[:, cols] = _dot(ub[:, hf * 256:(hf + 1) * 256], bre_ref[hf])
            buf_im[:, cols] = _dot(ub[:, hf * 256:(hf + 1) * 256], bim_ref[hf])

        for lc in range(SSM_LANES // SCAN_LANE_CHUNK):
            cols = slice(lc * SCAN_LANE_CHUNK, (lc + 1) * SCAN_LANE_CHUNK)
            l_re = jnp.broadcast_to(lre_ref[:, cols], (slab, SCAN_LANE_CHUNK))
            l_im = jnp.broadcast_to(lim_ref[:, cols], (slab, SCAN_LANE_CHUNK))

            def scan_tile(keep_states):
                def step(t, carry):
                    s_re, s_im = carry
                    r0 = pl.multiple_of(t * slab, slab)
                    n_re = l_re * s_re - l_im * s_im + buf_re[pl.ds(r0, slab), cols]
                    n_im = l_re * s_im + l_im * s_re + buf_im[pl.ds(r0, slab), cols]
                    if keep_states:
                        buf_re[pl.ds(r0, slab), cols] = n_re
                        buf_im[pl.ds(r0, slab), cols] = n_im
                    return n_re, n_im

                s_re, s_im = lax.fori_loop(0, SCAN_TILE_STEPS, step, (st_re[:, cols], st_im[:, cols]), unroll=True)
                st_re[:, cols] = s_re
                st_im[:, cols] = s_im

            pl.when(jnp.logical_not(second))(lambda: scan_tile(False))
            pl.when(second)(lambda: scan_tile(True))

        @pl.when(jnp.logical_and(i == n_tiles - 1, jnp.logical_not(second)))
        def _():
            a_re, a_im = _complex_power(lre_ref[...], lim_ref[...], steps)
            _chunk_carry(st_re, st_im, h0_re, h0_im, a_re, a_im, n_seq, reverse=False)
            st_re[...] = h0_re[...]
            st_im[...] = h0_im[...]

        @pl.when(second)
        def _():
            h_re = buf_re[...].astype(BF16)
            h_im = buf_im[...].astype(BF16)
            hre_ref[...] = h_re
            him_ref[...] = h_im
            for hf in range(2):
                cols = slice(hf * 1024, (hf + 1) * 1024)
                ycols = slice(hf * 256, (hf + 1) * 256)
                y_half = (_dot_nt(h_re[:, cols], cre_ref[hf]) - _dot_nt(h_im[:, cols], cim_ref[hf])
                          + d_ref[:, ycols] * u[:, ycols])
                _split_parts(y_refs[2 * hf:2 * hf + 2], y_half)

    tile = lambda w: pl.BlockSpec((tile_rows, w), lambda p, i: (i, 0))
    out_tile = lambda w: pl.BlockSpec((tile_rows, w), lambda p, i: (i * p, 0))
    cm = _full(_CM_SHAPE)
    outs = _call(
        body, name="s5_scan_fwd", grid=(2, n_tiles),
        in_specs=[tile(LANES)] * _SCAN_PARTS + [cm, cm, cm, cm, _full((1, SSM_LANES)), _full((1, SSM_LANES)),
                                                _full((1, 512))],
        out_specs=[out_tile(LANES)] * _SCAN_PARTS + [out_tile(SSM_LANES), out_tile(SSM_LANES)],
        out_shape=_part_shapes(rows) + [_sds((rows, SSM_LANES), BF16), _sds((rows, SSM_LANES), BF16)],
        scratch_shapes=[pltpu.VMEM((slab, SSM_LANES), F32)] * 4 + [pltpu.VMEM((tile_rows, SSM_LANES), F32)] * 2,
        compiler_params=_params(40, ("arbitrary", "arbitrary")),
    )(*u_parts, bt_re, bt_im, cm_re, cm_im, lbar_re, lbar_im, d_row)
    return outs[:_SCAN_PARTS], outs[_SCAN_PARTS], outs[_SCAN_PARTS + 1]


def _s5_scan_bwd(dy_parts, u_parts, h_re, h_im, bt_re, bt_im, cm_re, cm_im, lbar_re, lbar_im, d_row, n_seq, seq):
    slab, steps, tile_rows, n_tiles = _scan_geometry(n_seq, seq)
    rows = u_parts[0].shape[0]

    def body(*refs):
        dy_refs, u_refs, refs = refs[:_SCAN_PARTS], refs[_SCAN_PARTS:2 * _SCAN_PARTS], refs[2 * _SCAN_PARTS:]
        (hre_ref, him_ref, bre_ref, bim_ref, cre_ref, cim_ref, lre_ref, lim_ref, d_ref), refs = refs[:9], refs[9:]
        du_refs, refs = refs[:_SCAN_PARTS], refs[_SCAN_PARTS:]
        (dbre_ref, dbim_ref, dcre_ref, dcim_ref, dlre_ref, dlim_ref, dd_ref,
         st_re, st_im, g0_re, g0_im, acc_re, acc_im, buf_re, buf_im) = refs
        second = pl.program_id(0) == 1
        i = pl.program_id(1)

        @pl.when(jnp.logical_and(i == 0, jnp.logical_not(second)))
        def _():
            st_re[...] = jnp.zeros_like(st_re)
            st_im[...] = jnp.zeros_like(st_im)
            acc_re[...] = jnp.zeros_like(acc_re)
            acc_im[...] = jnp.zeros_like(acc_im)
            for ref in (dbre_ref, dbim_ref, dcre_ref, dcim_ref, dd_ref):
                ref[...] = jnp.zeros_like(ref)

        dy = _join_parts(dy_refs)
        dyb = dy.astype(BF16)
        for hf in range(2):
            cols = slice(hf * 1024, (hf + 1) * 1024)
            buf_re[:, cols] = _dot(dyb[:, hf * 256:(hf + 1) * 256], cre_ref[hf])
            buf_im[:, cols] = -_dot(dyb[:, hf * 256:(hf + 1) * 256], cim_ref[hf])

        for lc in range(SSM_LANES // SCAN_LANE_CHUNK):
            cols = slice(lc * SCAN_LANE_CHUNK, (lc + 1) * SCAN_LANE_CHUNK)
            l_re = jnp.broadcast_to(lre_ref[:, cols], (slab, SCAN_LANE_CHUNK))
            l_im = jnp.broadcast_to(lim_ref[:, cols], (slab, SCAN_LANE_CHUNK))

            def advance(r0, s_re, s_im):
                n_re = l_re * s_re + l_im * s_im + buf_re[pl.ds(r0, slab), cols]
                n_im = l_re * s_im - l_im * s_re + buf_im[pl.ds(r0, slab), cols]
                buf_re[pl.ds(r0, slab), cols] = n_re
                buf_im[pl.ds(r0, slab), cols] = n_im
                return n_re, n_im

            def row0(k):
                return pl.multiple_of((SCAN_TILE_STEPS - 1 - k) * slab, slab)

            @pl.when(jnp.logical_not(second))
            def _():
                s_re, s_im = lax.fori_loop(0, SCAN_TILE_STEPS, lambda k, s: advance(row0(k), *s),
                                           (st_re[:, cols], st_im[:, cols]), unroll=True)
                st_re[:, cols] = s_re
                st_im[:, cols] = s_im

            @pl.when(second)
            def _():
                def step(k, carry):
                    s_re, s_im, a_re, a_im = carry
                    r0 = row0(k)
                    hr = hre_ref[pl.ds(r0, slab), cols].astype(F32)
                    hi = him_ref[pl.ds(r0, slab), cols].astype(F32)
                    a_re = a_re + s_re * hr + s_im * hi
                    a_im = a_im + s_im * hr - s_re * hi
                    return advance(r0, s_re, s_im) + (a_re, a_im)

                zero = jnp.zeros((slab, SCAN_LANE_CHUNK), F32)
                s_re, s_im, a_re, a_im = lax.fori_loop(
                    0, SCAN_TILE_STEPS, step, (st_re[:, cols], st_im[:, cols], zero, zero), unroll=True)
                st_re[:, cols] = s_re
                st_im[:, cols] = s_im
                acc_re[:, cols] += a_re
                acc_im[:, cols] += a_im

        @pl.when(jnp.logical_and(i == n_tiles - 1, jnp.logical_not(second)))
        def _():
            p_re, p_im = _complex_power(lre_ref[...], lim_ref[...], steps)
            _chunk_carry(st_re, st_im, g0_re, g0_im, p_re, -p_im, n_seq, reverse=True)
            st_re[...] = g0_re[...]
            st_im[...] = g0_im[...]

        @pl.when(second)
        def _():
            u = _join_parts(u_refs)
            ub = u.astype(BF16)
            g_re = buf_re[...].astype(BF16)
            g_im = buf_im[...].astype(BF16)
            dd_ref[...] += jnp.sum(dy * u, axis=0, keepdims=True)
            for hf in range(2):
                cols = slice(hf * 1024, (hf + 1) * 1024)
                ycols = slice(hf * 256, (hf + 1) * 256)
                du_half = (_dot_nt(g_re[:, cols], bre_ref[hf]) + _dot_nt(g_im[:, cols], bim_ref[hf])
                           + d_ref[:, ycols] * dy[:, ycols])
                _split_parts(du_refs[2 * hf:2 * hf + 2], du_half)
                for q4 in range(_HALF_GROUPS // 4):
                    ch = slice(hf * 256 + q4 * 64, hf * 256 + (q4 + 1) * 64)
                    st = slice(hf * 1024 + q4 * 256, hf * 1024 + (q4 + 1) * 256)
                    blk = (hf, slice(q4 * 64, (q4 + 1) * 64), slice(q4 * 256, (q4 + 1) * 256))
                    dbre_ref[blk] += _dot_tn(ub[:, ch], g_re[:, st])
                    dbim_ref[blk] += _dot_tn(ub[:, ch], g_im[:, st])
                    dcre_ref[blk] += _dot_tn(dyb[:, ch], hre_ref[:, st])
                    dcim_ref[blk] -= _dot_tn(dyb[:, ch], him_ref[:, st])

        @pl.when(jnp.logical_and(i == n_tiles - 1, second))
        def _():
            dlre_ref[...] = jnp.sum(acc_re[...], axis=0, keepdims=True)
            dlim_ref[...] = jnp.sum(acc_im[...], axis=0, keepdims=True)

    tile = lambda w: pl.BlockSpec((tile_rows, w), lambda p, i: (n_tiles - 1 - i, 0))
    second_tile = lambda w: pl.BlockSpec((tile_rows, w), lambda p, i: (n_tiles - 1 - i * p, 0))
    cm = _full(_CM_SHAPE)
    row = _full((1, SSM_LANES))
    outs = _call(
        body, name="s5_scan_bwd", grid=(2, n_tiles),
        in_specs=[tile(LANES)] * _SCAN_PARTS + [second_tile(LANES)] * _SCAN_PARTS
        + [second_tile(SSM_LANES), second_tile(SSM_LANES), cm, cm, cm, cm, row, row, _full((1, 512))],
        out_specs=[second_tile(LANES)] * _SCAN_PARTS + [cm, cm, cm, cm, row, row, _full((1, 512))],
        out_shape=(_part_shapes(rows) + [_sds(_CM_SHAPE)] * 4 + [_sds((1, SSM_LANES))] * 2 + [_sds((1, 512))]),
        scratch_shapes=[pltpu.VMEM((slab, SSM_LANES), F32)] * 6 + [pltpu.VMEM((tile_rows, SSM_LANES), F32)] * 2,
        compiler_params=_params(48, ("arbitrary", "arbitrary")),
    )(*dy_parts, *u_parts, h_re, h_im, bt_re, bt_im, cm_re, cm_im, lbar_re, lbar_im, d_row)
    return (outs[:_SCAN_PARTS],) + tuple(outs[_SCAN_PARTS:])


def _glu_gate(gl, a, zs):
    return gl * jax.nn.sigmoid(a) * _silu(zs)


def _glu_fwd(y_parts, zs, w_glu, b_glu, n_seq, seq):
    rows = zs.shape[0]
    tm = 512
    slab, steps, _, _ = _scan_geometry(n_seq, seq)

    def body(*refs):
        y_refs, (zs_ref, w_ref, b_ref, o_ref) = refs[:_SCAN_PARTS], refs[_SCAN_PARTS:]
        y = _load_chunks(y_refs, pl.program_id(0) * (tm // steps), tm // steps, steps, slab)
        gl = jax.nn.gelu(y)
        a = _dot(gl.astype(BF16), w_ref[...]) + b_ref[...]
        o_ref[...] = _glu_gate(gl, a, zs_ref[...]).astype(BF16)

    return _call(
        body, name="glu_fwd", grid=(rows // tm,),
        in_specs=_whole_parts(rows) + [_rows(tm, 512), _full((512, 512)), _full((1, 512))],
        out_specs=_rows(tm, 512), out_shape=_sds((rows, 512), BF16),
        compiler_params=_params(32, ("arbitrary",)),
    )(*y_parts, zs, w_glu, b_glu)


def _glu_bwd(y_parts, zs, d_out, w_glu, b_glu, n_seq, seq):
    rows = zs.shape[0]
    tm = 512
    slab, steps, _, _ = _scan_geometry(n_seq, seq)

    def body(*refs):
        y_refs, (zs_ref, d_ref, w_ref, b_ref), refs = refs[:_SCAN_PARTS], refs[_SCAN_PARTS:_SCAN_PARTS + 4], refs[_SCAN_PARTS + 4:]
        dy_refs, (dzs_ref, dw_ref, db_ref) = refs[:_SCAN_PARTS], refs[_SCAN_PARTS:]
        first_chunk = pl.program_id(0) * (tm // steps)

        @pl.when(pl.program_id(0) == 0)
        def _():
            dw_ref[...] = jnp.zeros_like(dw_ref)
            db_ref[...] = jnp.zeros_like(db_ref)

        gl, gelu_vjp = jax.vjp(jax.nn.gelu, _load_chunks(y_refs, first_chunk, tm // steps, steps, slab))
        glb = gl.astype(BF16)
        a = _dot(glb, w_ref[...]) + b_ref[...]
        _, gate_vjp = jax.vjp(_glu_gate, gl, a, zs_ref[...])
        d_gl, d_a, d_zs = gate_vjp(d_ref[...])
        dab = d_a.astype(BF16)
        d_gl = d_gl + _dot_nt(dab, w_ref[...])
        _store_chunks(dy_refs, first_chunk, gelu_vjp(d_gl)[0], steps, slab)
        dzs_ref[...] = d_zs.astype(BF16)
        dw_ref[...] += _dot_tn(glb, dab)
        db_ref[...] += jnp.sum(d_a, axis=0, keepdims=True)

    *dy_parts, dzs, dw, db = _call(
        body, name="glu_bwd", grid=(rows // tm,),
        in_specs=_whole_parts(rows) + [_rows(tm, 512), _rows(tm, 512), _full((512, 512)), _full((1, 512))],
        out_specs=_whole_parts(rows) + [_rows(tm, 512), _full((512, 512)), _full((1, 512))],
        out_shape=_part_shapes(rows) + [_sds((rows, 512), BF16), _sds((512, 512)), _sds((1, 512))],
        compiler_params=_params(40, ("arbitrary",)),
    )(*y_parts, zs, d_out, w_glu, b_glu)
    return dy_parts, dzs, dw, db


_GROUP_ROWS = Q_PER_KV * BLOCK
_BLOCK_SHIFT = BLOCK.bit_length() - 1


def _attn_bias(j):
    query = _iota((BLOCK, _GROUP_ROWS), 1)
    dist_cur = (query & (BLOCK - 1)) - _iota((BLOCK, _GROUP_ROWS), 0)
    dist_prev = dist_cur + BLOCK
    head = query >> _BLOCK_SHIFT
    slope = jnp.zeros((BLOCK, _GROUP_ROWS), F32)
    for g in range(Q_PER_KV):
        slope = jnp.where(head == g, 2.0 ** (-(j * Q_PER_KV + g + 1)), slope)
    bias_cur = jnp.where(dist_cur >= 0, -slope * dist_cur.astype(F32), -jnp.inf)
    bias_prev = jnp.where(dist_prev < WINDOW, -slope * dist_prev.astype(F32), -jnp.inf)
    return bias_cur, bias_prev


_ATTN_BIAS_SCRATCH = pltpu.VMEM((KV_HEADS, 2, BLOCK, _GROUP_ROWS), F32)


def _fill_attn_bias(bias_ref):
    @pl.when(jnp.logical_and(pl.program_id(0) == 0, pl.program_id(1) == 0))
    def _():
        for j in range(KV_HEADS):
            bias_ref[j, 0], bias_ref[j, 1] = _attn_bias(j)


def _stack_heads(x, j):
    heads = range(j * Q_PER_KV, (j + 1) * Q_PER_KV)
    return jnp.concatenate([x[:, h * HEAD_DIM:(h + 1) * HEAD_DIM] for h in heads], axis=0)


def _head_rows(x, j):
    heads = range(j * Q_PER_KV, (j + 1) * Q_PER_KV)
    return jnp.concatenate([x[h:h + 1, :] for h in heads], axis=1)


def _sink_row(sk_ref, j):
    heads = range(j * Q_PER_KV, (j + 1) * Q_PER_KV)
    return jnp.concatenate([jnp.broadcast_to(sk_ref[0:1, h:h + 1], (1, BLOCK)) for h in heads], axis=1)


def _attn_fwd(q, k, v, za, sinks, n_seq, seq):
    nb = seq // BLOCK
    rows = q.shape[0]

    def body(q_ref, kc_ref, kp_ref, vc_ref, vp_ref, za_ref, sk_ref, o_ref, ao_ref, lse_ref, bias_ref):
        _fill_attn_bias(bias_ref)
        has_prev = pl.program_id(1) > 0
        q_all = q_ref[...]
        for j in range(KV_HEADS):
            js = slice(j * HEAD_DIM, (j + 1) * HEAD_DIM)
            bias_c, bias_p = bias_ref[j, 0], bias_ref[j, 1]
            q4 = _stack_heads(q_all, j)
            sc = _dot_nt(kc_ref[:, js], q4) + bias_c
            sp = _dot_nt(kp_ref[:, js], q4) + jnp.where(has_prev, bias_p, -jnp.inf)
            sink = _sink_row(sk_ref, j)
            m = jnp.maximum(jnp.max(jnp.maximum(sc, sp), axis=0, keepdims=True), sink)
            ec = jnp.exp(sc - m)
            ep = jnp.exp(sp - m)
            den = jnp.sum(ec + ep, axis=0, keepdims=True) + jnp.exp(sink - m)
            inv = 1.0 / den
            o4 = _dot_tn((ec * inv).astype(BF16), vc_ref[:, js]) + _dot_tn((ep * inv).astype(BF16), vp_ref[:, js])
            lse4 = m + jnp.log(den)
            for g in range(Q_PER_KV):
                h = j * Q_PER_KV + g
                o_ref[:, h * HEAD_DIM:(h + 1) * HEAD_DIM] = o4[g * BLOCK:(g + 1) * BLOCK]
                lse_ref[h:h + 1, :] = lse4[:, g * BLOCK:(g + 1) * BLOCK]
        ao_ref[...] = (o_ref[...] * _silu(za_ref[...])).astype(BF16)

    cur = lambda w: pl.BlockSpec((BLOCK, w), lambda b, n: (b * nb + n, 0))
    prev = lambda w: pl.BlockSpec((BLOCK, w), lambda b, n: (b * nb + jnp.maximum(n - 1, 0), 0))
    lse_rows = rows // BLOCK * N_HEADS
    return _call(
        body, name="attn_fwd", grid=(n_seq, nb),
        in_specs=[cur(512), cur(128), prev(128), cur(128), prev(128), cur(512), _full((1, N_HEADS))],
        out_specs=[cur(512), cur(512), pl.BlockSpec((N_HEADS, BLOCK), lambda b, n: (b * nb + n, 0))],
        out_shape=[_sds((rows, 512)), _sds((rows, 512), BF16), _sds((lse_rows, BLOCK))],
        scratch_shapes=[_ATTN_BIAS_SCRATCH], compiler_params=_params(32, ("arbitrary", "arbitrary")),
    )(q, k, k, v, v, za, sinks)


def _attn_bwd(q, k, v, za, o, lse, d_ao, sinks, n_seq, seq):
    nb = seq // BLOCK
    rows = q.shape[0]

    def body(q_ref, kc_ref, kp_ref, vc_ref, vp_ref, za_ref, o_ref, lse_ref, d_ref, sk_ref,
             dq_ref, dk_ref, dv_ref, dza_ref, dsk_ref, bias_ref, dk_carry, dv_carry):
        n = nb - 1 - pl.program_id(1)
        _fill_attn_bias(bias_ref)

        @pl.when(jnp.logical_and(pl.program_id(0) == 0, pl.program_id(1) == 0))
        def _():
            dsk_ref[...] = jnp.zeros_like(dsk_ref)
            dk_carry[...] = jnp.zeros_like(dk_carry)
            dv_carry[...] = jnp.zeros_like(dv_carry)

        has_prev = n > 0
        has_next = n + 1 < nb

        _, gate_vjp = jax.vjp(lambda o_, z_: o_ * _silu(z_), o_ref[...], za_ref[...])
        d_o, d_za = gate_vjp(d_ref[...])
        dza_ref[...] = d_za.astype(BF16)
        q_all = q_ref[...]
        lse_all = lse_ref[...]

        for j in range(KV_HEADS):
            js = slice(j * HEAD_DIM, (j + 1) * HEAD_DIM)
            kc, kp, vc, vp = kc_ref[:, js], kp_ref[:, js], vc_ref[:, js], vp_ref[:, js]
            bias_c, bias_p = bias_ref[j, 0], bias_ref[j, 1]
            q4 = _stack_heads(q_all, j)
            do4b = _stack_heads(d_o, j).astype(BF16)
            lse4 = _head_rows(lse_all, j)
            pc = jnp.exp(_dot_nt(kc, q4) + bias_c - lse4)
            pp = jnp.exp(_dot_nt(kp, q4) + jnp.where(has_prev, bias_p, -jnp.inf) - lse4)
            dpc = _dot_nt(vc, do4b)
            dpp = _dot_nt(vp, do4b)
            delta = jnp.sum(pc * dpc + pp * dpp, axis=0, keepdims=True)
            dsc = (pc * (dpc - delta)).astype(BF16)
            dsp = (pp * (dpp - delta)).astype(BF16)
            dq4 = ((_dot_tn(dsc, kc) + _dot_tn(dsp, kp)) * ATTN_SCALE).astype(BF16)
            sink_loss = jnp.exp(_sink_row(sk_ref, j) - lse4) * delta
            for g in range(Q_PER_KV):
                h = j * Q_PER_KV + g
                dq_ref[:, h * HEAD_DIM:(h + 1) * HEAD_DIM] = dq4[g * BLOCK:(g + 1) * BLOCK]
                dsk_ref[0:1, h:h + 1] -= jnp.sum(sink_loss[:, g * BLOCK:(g + 1) * BLOCK], axis=1, keepdims=True)
            dk = _dot(dsc, q4) + jnp.where(has_next, dk_carry[j], 0.0)
            dv = _dot(pc.astype(BF16), do4b) + jnp.where(has_next, dv_carry[j], 0.0)
            dk_carry[j] = _dot(dsp, q4)
            dv_carry[j] = _dot(pp.astype(BF16), do4b)
            dk_ref[:, js] = dk.astype(BF16)
            dv_ref[:, js] = dv.astype(BF16)

    cur = lambda w: pl.BlockSpec((BLOCK, w), lambda b, s: (b * nb + nb - 1 - s, 0))
    prev = lambda w: pl.BlockSpec((BLOCK, w), lambda b, s: (b * nb + jnp.maximum(nb - 2 - s, 0), 0))
    return _call(
        body, name="attn_bwd", grid=(n_seq, nb),
        in_specs=[cur(512), cur(128), prev(128), cur(128), prev(128), cur(512), cur(512),
                  pl.BlockSpec((N_HEADS, BLOCK), lambda b, s: (b * nb + nb - 1 - s, 0)), cur(512), _full((1, N_HEADS))],
        out_specs=[cur(512), cur(128), cur(128), cur(512), _full((1, N_HEADS))],
        out_shape=[_sds((rows, 512), BF16), _sds((rows, 128), BF16), _sds((rows, 128), BF16),
                   _sds((rows, 512), BF16), _sds((1, N_HEADS))],
        scratch_shapes=[_ATTN_BIAS_SCRATCH, pltpu.VMEM((KV_HEADS, BLOCK, HEAD_DIM), F32),
                        pltpu.VMEM((KV_HEADS, BLOCK, HEAD_DIM), F32)],
        compiler_params=_params(32, ("arbitrary", "arbitrary")),
    )(q, k, k, v, v, za, o, lse, d_ao, sinks)


def _tail(ssm_out, attn_out, x2d, p2d, target, w_out, g2, w_gate, b_gate, w_proj):
    rows = x2d.shape[0]
    tm = 512

    def body(so_ref, ao_ref, x_ref, p_ref, t_ref, wo_ref, g2_ref, wg_ref, bg_ref, wp_ref,
             dh1_ref, dso_ref, dao_ref, dwo_ref, dwg_ref, dwp_ref, dbg_ref, dg2_ref, loss_ref):
        @pl.when(pl.program_id(0) == 0)
        def _():
            for ref in (dwo_ref, dwg_ref, dwp_ref, dbg_ref, dg2_ref, loss_ref):
                ref[...] = jnp.zeros_like(ref)

        cat = jnp.concatenate([so_ref[...], ao_ref[...]], axis=1)
        g2 = g2_ref[...]
        mixed = _dot(cat, wo_ref[...])
        r = lax.rsqrt(jnp.mean(mixed * mixed, axis=-1, keepdims=True) + EPS)
        mr = mixed * r
        h1 = x_ref[...] + mr * g2
        h1b = h1.astype(BF16)
        gate = jax.nn.sigmoid(_dot(h1b, wg_ref[...]) + bg_ref[...])
        pb = p_ref[...].astype(BF16)
        wp_blocks = [slice(j * D_PLE, (j + 1) * D_PLE) for j in range(N_CHIPS)]
        pp = jnp.concatenate([_dot(pb, wp_ref[blk, :]) for blk in wp_blocks], axis=1)
        err = h1 + gate * pp - t_ref[...]
        loss_ref[...] += 0.5 * jnp.sum(jnp.mean(err * err, axis=-1, keepdims=True), axis=0, keepdims=True)

        dh2 = err * (1.0 / D_MODEL)
        d_glin = dh2 * pp * gate * (1.0 - gate)
        d_glin_b = d_glin.astype(BF16)
        dwg_ref[...] += _dot_tn(h1b, d_glin_b)
        dbg_ref[...] += jnp.sum(d_glin, axis=0, keepdims=True)
        d_pp = (dh2 * gate).astype(BF16)
        for blk in wp_blocks:
            dwp_ref[blk, :] += _dot_tn(pb, d_pp[:, blk])
        dh1 = dh2 + _dot_nt(d_glin_b, wg_ref[...])
        dh1_ref[...] = dh1
        dg2_ref[...] += jnp.sum(dh1 * mr, axis=0, keepdims=True)
        a_ = dh1 * g2
        d_mixed = (r * a_ - mr * (r * jnp.mean(a_ * mr, axis=-1, keepdims=True))).astype(BF16)
        dwo_ref[...] += _dot_tn(cat, d_mixed)
        d_cat = _dot_nt(d_mixed, wo_ref[...])
        dso_ref[...] = d_cat[:, 0:512]
        dao_ref[...] = d_cat[:, 512:1024]

    return _call(
        body, name="tail_fwd_bwd", grid=(rows // tm,),
        in_specs=[_rows(tm, 512), _rows(tm, 512), _rows(tm, D_MODEL), _rows(tm, D_PLE), _rows(tm, D_MODEL),
                  _full((D_MODEL, D_MODEL)), _full((1, D_MODEL)), _full((D_MODEL, D_MODEL)), _full((1, D_MODEL)),
                  _full((N_CHIPS * D_PLE, D_PLE))],
        out_specs=[_rows(tm, D_MODEL), _rows(tm, 512), _rows(tm, 512), _full((D_MODEL, D_MODEL)),
                   _full((D_MODEL, D_MODEL)), _full((N_CHIPS * D_PLE, D_PLE)), _full((1, D_MODEL)), _full((1, D_MODEL)),
                   _full((1, 1))],
        out_shape=[_sds((rows, D_MODEL)), _sds((rows, 512)), _sds((rows, 512)), _sds((D_MODEL, D_MODEL)),
                   _sds((D_MODEL, D_MODEL)), _sds((N_CHIPS * D_PLE, D_PLE)), _sds((1, D_MODEL)), _sds((1, D_MODEL)),
                   _sds((1, 1))],
        compiler_params=_params(52, ("arbitrary",)),
    )(ssm_out, attn_out, x2d, p2d, target, w_out, g2, w_gate, b_gate, w_proj)


def _local_step(x, hn, p, target, pre_norm_g, w_in_t, s5_params, s5_operands, ssm_d, w_glu, b_glu, sinks, w_out,
                post_norm_g, w_proj, w_gate, b_gate):
    n_seq, seq, _ = x.shape
    rows = n_seq * seq
    x2d = x.reshape(rows, D_MODEL)
    p2d = p.reshape(rows, D_PLE)
    t2d = target.reshape(rows, D_MODEL)

    l_re, l_im, bt_re, bt_im, cm_re, cm_im = s5_operands

    u_scan, zs, q, k, v, za = _in_proj(hn, w_in_t, n_seq, seq)
    y_scan, h_re, h_im = _s5_scan_fwd(u_scan, bt_re, bt_im, cm_re, cm_im, l_re, l_im, ssm_d, n_seq, seq)
    ssm_out = _glu_fwd(y_scan, zs, w_glu, b_glu, n_seq, seq)
    o, attn_out, lse = _attn_fwd(q, k, v, za, sinks, n_seq, seq)

    dh1, d_so, d_ao, d_w_out, d_w_gate, d_w_proj, d_b_gate, d_g2, loss = _tail(
        ssm_out, attn_out, x2d, p2d, t2d, w_out, post_norm_g, w_gate, b_gate, w_proj)

    dq, dk, dv, dza, d_sinks = _attn_bwd(q, k, v, za, o, lse, d_ao, sinks, n_seq, seq)
    dy_scan, dzs, d_w_glu, d_b_glu = _glu_bwd(y_scan, zs, d_so, w_glu, b_glu, n_seq, seq)
    du_scan, d_bt_re, d_bt_im, d_cm_re, d_cm_im, d_l_re, d_l_im, d_d = _s5_scan_bwd(
        dy_scan, u_scan, h_re, h_im, bt_re, bt_im, cm_re, cm_im, l_re, l_im, ssm_d, n_seq, seq)
    d_lam_re, d_lam_im, d_log_step, d_b_re, d_b_im, d_c_re, d_c_im = _s5_params_bwd(
        s5_params, (d_l_re, d_l_im, d_bt_re, d_bt_im, d_cm_re, d_cm_im))

    grad_x, d_w_in_t, d_g1 = _in_proj_bwd(x2d, dh1, pre_norm_g, w_in_t, du_scan, dzs, dq, dk, dv, dza, n_seq, seq)
    grads = dict(
        pre_norm_g=d_g1, w_in=d_w_in_t, ssm_lam_re=d_lam_re, ssm_lam_im=d_lam_im, ssm_log_step=d_log_step,
        ssm_b_re=d_b_re, ssm_b_im=d_b_im, ssm_c_re=d_c_re, ssm_c_im=d_c_im, ssm_d=d_d, ssm_w_glu=d_w_glu,
        ssm_b_glu=d_b_glu, attn_sinks=d_sinks, w_out=d_w_out, post_norm_g=d_g2, pl_w_proj=d_w_proj,
        pl_w_gate=d_w_gate, pl_b_gate=d_b_gate)
    return grad_x.reshape(x.shape), loss, grads


_BIG = ("w_in", "ssm_w_glu", "w_out", "pl_w_proj", "pl_w_gate")
_BIG_SHARD = {"w_in": (D_IN // N_CHIPS, D_MODEL), "ssm_w_glu": (D_SSM // N_CHIPS, D_SSM),
              "w_out": (D_MODEL // N_CHIPS, D_MODEL), "pl_w_proj": (D_PLE, D_MODEL // N_CHIPS),
              "pl_w_gate": (D_MODEL // N_CHIPS, D_MODEL)}
_SMALL = {"pre_norm_g": (1, D_MODEL), "ssm_lam_re": (SSM_GROUPS, SSM_STATE), "ssm_lam_im": (SSM_GROUPS, SSM_STATE),
          "ssm_log_step": (1, SSM_GROUPS), "ssm_b_re": (D_SSM, SSM_STATE), "ssm_b_im": (D_SSM, SSM_STATE),
          "ssm_c_re": (D_SSM, SSM_STATE), "ssm_c_im": (D_SSM, SSM_STATE), "ssm_d": (1, D_SSM), "ssm_b_glu": (1, D_SSM),
          "attn_sinks": (1, N_HEADS), "post_norm_g": (1, D_MODEL), "pl_b_gate": (1, D_MODEL)}
_VEC_ROWS = ("pre_norm_g", "post_norm_g", "pl_b_gate", "ssm_d", "ssm_b_glu", "attn_sinks", "ssm_log_step", "loss")
_SMALL_GROUPS = (
    ("vec", (8, D_MODEL), tuple((name, r) for r, name in enumerate(_VEC_ROWS))),
    ("lam", (2 * SSM_GROUPS, SSM_STATE), (("ssm_lam_re", 0), ("ssm_lam_im", SSM_GROUPS))),
)
_SMALL_EARLY = ("ssm_b_re", "ssm_b_im", "ssm_c_re", "ssm_c_im")
_SMALL_ORDER = tuple(name for _, _, members in _SMALL_GROUPS for name, _ in members) + _SMALL_EARLY
_WEIGHT_ORDER = ("pre_norm_g", "w_in", "ssm_lam_re", "ssm_lam_im", "ssm_log_step", "ssm_b_re", "ssm_b_im", "ssm_c_re",
                 "ssm_c_im", "ssm_d", "ssm_w_glu", "ssm_b_glu", "attn_sinks", "w_out", "post_norm_g", "pl_w_proj",
                 "pl_w_gate", "pl_b_gate")


def _small_shape(name):
    return (1, 1) if name == "loss" else _SMALL[name]


def _to_kernel_form(name, a):
    a = a[0]
    if name == "w_in":
        return a.T
    if name in ("ssm_b_re", "ssm_b_im"):
        a = a.transpose(0, 2, 1)
    return a.reshape(_SMALL[name]) if name in _SMALL else a


def _from_kernel_form(name, a, shape):
    if name == "w_in":
        a = a.T
    if name in ("ssm_b_re", "ssm_b_im"):
        a = a.reshape(SSM_GROUPS, SSM_GROUP_CH, SSM_STATE).transpose(0, 2, 1)
    return a.reshape(shape)


def _mesh_place():
    x, y, c = lax.axis_index("x"), lax.axis_index("y"), lax.axis_index("c")
    other_chips = ((1 - x, y), (x, 1 - y), (1 - x, 1 - y))
    return x, y, c, other_chips


def _gather_copies(s_refs, g_refs, send_sems, recv_sems, local_sems):
    x, y, c, other_chips = _mesh_place()
    started = []
    for i, (s_ref, g_ref) in enumerate(zip(s_refs, g_refs)):
        rows = s_ref.shape[0]
        half = rows // 2

        def block(chip, g_ref=g_ref, rows=rows, half=half):
            return g_ref.at[pl.ds((2 * chip[0] + chip[1]) * rows + c * half, half), :]

        def copy(k, chip, to, src=None, i=i, block=block):
            return pltpu.make_async_remote_copy(
                src_ref=block(chip) if src is None else src, dst_ref=block(chip), send_sem=send_sems.at[6 * i + k],
                recv_sem=recv_sems.at[6 * i + k], device_id=to, device_id_type=MESH)

        own = pltpu.make_async_copy(s_ref, g_ref.at[pl.ds((2 * x + y) * rows, rows), :], local_sems.at[i])
        own.start()
        first = [copy(k, (x, y), (*chip, c), src=s_ref.at[pl.ds(c * half, half), :])
                 for k, chip in enumerate(other_chips)]
        for cp in first:
            cp.start()
        passed = [copy(3 + k, chip, (x, y, 1 - c)) for k, chip in enumerate(other_chips)]
        started.append((own, first, passed))
    for own, first, passed in started:
        for k in range(3):
            first[k].wait_recv()
            passed[k].start()
    for own, first, passed in started:
        for k in range(3):
            passed[k].wait_recv()
        for cp in first + passed:
            cp.wait_send()
        own.wait()


def _gather_semaphores(n_t):
    return [pltpu.SemaphoreType.DMA((6 * n_t,)), pltpu.SemaphoreType.DMA((6 * n_t,)), pltpu.SemaphoreType.DMA((n_t,))]


def _gather_weights_beside(shards, name, collective_id):
    n_t = len(shards)
    hbm = pltpu.MemorySpace.HBM
    s_refs = [jax.new_ref(s, memory_space=hbm) for s in shards]
    g_refs = [jax.empty_ref(jax.ShapeDtypeStruct((N_CHIPS * s.shape[0], s.shape[1]), s.dtype), memory_space=hbm)
              for s in shards]

    def launch(send_sems, recv_sems, local_sems):
        x, y, c, other_chips = _mesh_place()
        peers = [(*chip, c) for chip in other_chips] + [(x, y, 1 - c)]
        barrier = pltpu.get_barrier_semaphore()
        for peer in peers:
            pl.semaphore_signal(barrier, inc=1, device_id=peer, device_id_type=MESH)
        pl.semaphore_wait(barrier, len(peers))
        _gather_copies(s_refs, g_refs, send_sems, recv_sems, local_sems)

    pl.kernel(launch, mesh=plsc.ScalarSubcoreMesh(axis_name="sequencer", num_cores=1), name=name,
              scratch_types=_gather_semaphores(n_t), compiler_params=pltpu.CompilerParams(collective_id=collective_id))()
    return [g[...] for g in g_refs]


_RELATIONS = tuple(((r >> 2) & 1, (r >> 1) & 1, r & 1) for r in range(1, 8))


def _related(place, relation):
    return tuple(1 - a if flip else a for a, flip in zip(place, relation))


def _scatter_beside(mats):
    hbm = pltpu.MemorySpace.HBM
    src_refs = [jax.new_ref(a, memory_space=hbm) for a in mats]
    land_refs = [jax.empty_ref(jax.ShapeDtypeStruct((7, a.shape[0] // 8, a.shape[1]), a.dtype), memory_space=hbm)
                 for a in mats]

    def launch(send_sems, recv_sems):
        me = (lax.axis_index("x"), lax.axis_index("y"), lax.axis_index("c"))
        peers = [_related(me, rel) for rel in _RELATIONS]
        barrier = pltpu.get_barrier_semaphore()
        for peer in peers:
            pl.semaphore_signal(barrier, inc=1, device_id=peer, device_id_type=MESH)
        pl.semaphore_wait(barrier, len(peers))
        copies = []
        for i, (src, land) in enumerate(zip(src_refs, land_refs)):
            hr = land.shape[1]
            for k, (tx, ty, tc) in enumerate(peers):
                rows = pl.ds((2 * tx + ty) * 2 * hr + tc * hr, hr)
                copies.append(pltpu.make_async_remote_copy(
                    src_ref=src.at[rows, :], dst_ref=land.at[k], send_sem=send_sems.at[7 * i + k],
                    recv_sem=recv_sems.at[7 * i + k], device_id=(tx, ty, tc), device_id_type=MESH))
                copies[-1].start()
        for cp in copies:
            cp.wait()

    n_sems = 7 * len(mats)
    pl.kernel(launch, mesh=plsc.ScalarSubcoreMesh(axis_name="sequencer", num_cores=1), name="scatter_beside",
              scratch_types=[pltpu.SemaphoreType.DMA((n_sems,)), pltpu.SemaphoreType.DMA((n_sems,))],
              compiler_params=pltpu.CompilerParams(collective_id=2))()
    return [ref[...] for ref in land_refs]


def _broadcast_beside(arrays):
    hbm = pltpu.MemorySpace.HBM
    src_refs = [jax.new_ref(a, memory_space=hbm) for a in arrays]
    land_refs = [jax.empty_ref(jax.ShapeDtypeStruct((len(_RELATIONS),) + a.shape, a.dtype), memory_space=hbm)
                 for a in arrays]

    def launch(send_sems, recv_sems):
        me = (lax.axis_index("x"), lax.axis_index("y"), lax.axis_index("c"))
        peers = [_related(me, rel) for rel in _RELATIONS]
        barrier = pltpu.get_barrier_semaphore()
        for peer in peers:
            pl.semaphore_signal(barrier, inc=1, device_id=peer, device_id_type=MESH)
        pl.semaphore_wait(barrier, len(peers))
        copies = []
        for i, (src, land) in enumerate(zip(src_refs, land_refs)):
            for k, peer in enumerate(peers):
                copies.append(pltpu.make_async_remote_copy(
                    src_ref=src, dst_ref=land.at[k], send_sem=send_sems.at[7 * i + k],
                    recv_sem=recv_sems.at[7 * i + k], device_id=peer, device_id_type=MESH))
                copies[-1].start()
        for cp in copies:
            cp.wait()

    n_sems = 7 * len(arrays)
    pl.kernel(launch, mesh=plsc.ScalarSubcoreMesh(axis_name="sequencer", num_cores=1), name="broadcast_beside",
              scratch_types=[pltpu.SemaphoreType.DMA((n_sems,)), pltpu.SemaphoreType.DMA((n_sems,))],
              compiler_params=pltpu.CompilerParams(collective_id=3))()
    return [ref[...] for ref in land_refs]


def _exchange_grads(big, small, landed, landed_small):
    n_t = len(big)
    n_g = len(_SMALL_GROUPS)
    names = _SMALL_ORDER
    halves = [(b.shape[0] // N_CHIPS // 2, b.shape[1]) for b in big]
    early = sorted(landed)
    late = [i for i in range(n_t) if i not in landed]
    n_sems = 4 * n_g + 7 * len(late) + n_t
    small_sem0, block_sem0 = n_t, n_t + len(names)
    early_sem0 = block_sem0 + N_CHIPS * len(late)
    landed_sem0 = early_sem0 + 2 * len(early)
    early_small = [n for n in names if n in landed_small]

    def body(*refs):
        pos = 0

        def take(n):
            nonlocal pos
            pos += n
            return refs[pos - n:pos]

        big_refs, small_refs = take(n_t), dict(zip(names, take(len(names))))
        land_refs = dict(zip(early, take(len(early))))
        land_small_refs = dict(zip(early_small, take(len(early_small))))
        out_refs, small_out_refs = take(n_t), dict(zip(names, take(len(names))))
        per_late = lambda: dict(zip(late, take(len(late))))
        ga, gb, pme, send_b, recv_b = per_late(), per_late(), take(n_t), per_late(), per_late()
        own_e, land_e = dict(zip(early, take(len(early)))), dict(zip(early, take(len(early))))
        land_s = dict(zip(early_small, take(len(early_small))))
        s_own, s_sib, s_chips, s_pair = take(n_g), take(n_g), take(n_g), take(n_g)
        stage = dict(zip(names, take(len(names))))
        send_sems, recv_sems, local_sems = take(3)
        x, y, c, other_chips = _mesh_place()
        me = 2 * x + y
        sibling = (x, y, 1 - c)
        sem_at = iter(range(n_sems))

        def remote(src, dst, to):
            k = next(sem_at)
            return pltpu.make_async_remote_copy(src_ref=src, dst_ref=dst, send_sem=send_sems.at[k],
                                                recv_sem=recv_sems.at[k], device_id=to, device_id_type=MESH)

        loads = [pltpu.make_async_copy(small_refs[name], stage[name], local_sems.at[small_sem0 + a])
                 for a, name in enumerate(names)]
        landed_loads = [pltpu.make_async_copy(land_small_refs[name], land_s[name], local_sems.at[landed_sem0 + a])
                        for a, name in enumerate(early_small)]
        for cp in loads + landed_loads:
            cp.start()
        for cp in loads:
            cp.wait()
        small_swaps = []
        for gi, (_, _, members) in enumerate(_SMALL_GROUPS):
            s_own[gi][...] = jnp.zeros_like(s_own[gi])
            for name, r0 in members:
                r, n = _small_shape(name)
                s_own[gi][r0:r0 + r, 0:n] = stage[name][...]
            small_swaps.append(remote(s_own[gi], s_sib[gi], sibling))
            small_swaps[gi].start()
        order = sorted(late, key=lambda i: halves[i][0] * halves[i][1])
        own_loads, big_swaps = {}, {}
        for i in order:
            hr = halves[i][0]
            own_loads[i], big_swaps[i] = [], []
            for j in range(N_CHIPS):
                mine = big_refs[i].at[pl.ds(j * 2 * hr + c * hr, hr), :]
                theirs = big_refs[i].at[pl.ds(j * 2 * hr + (1 - c) * hr, hr), :]
                sem = local_sems.at[block_sem0 + N_CHIPS * late.index(i) + j]
                own_loads[i].append(pltpu.make_async_copy(mine, ga[i].at[j], sem))
                own_loads[i][j].start()
                big_swaps[i].append(remote(theirs, gb[i].at[j], sibling))
                big_swaps[i][j].start()
        early_loads = {}
        for e, i in enumerate(early):
            hr = halves[i][0]
            mine = big_refs[i].at[pl.ds(me * 2 * hr + c * hr, hr), :]
            early_loads[i] = [pltpu.make_async_copy(mine, own_e[i], local_sems.at[early_sem0 + 2 * e]),
                              pltpu.make_async_copy(land_refs[i], land_e[i], local_sems.at[early_sem0 + 2 * e + 1])]
            for cp in early_loads[i]:
                cp.start()
        small_sends = []
        for gi in range(n_g):
            small_swaps[gi].wait_recv()
            s_pair[gi][...] = s_own[gi][...] + s_sib[gi][...]
            small_sends.append([remote(s_pair[gi], s_chips[gi].at[k], (*chip, c)) for k, chip in enumerate(other_chips)])
            for cp in small_sends[gi]:
                cp.start()

        def pair_sum(i, j):
            return ga[i][j] + gb[i][j]

        big_sends = {}
        for i in order:
            for j in range(N_CHIPS):
                own_loads[i][j].wait()
                big_swaps[i][j].wait_recv()
            big_sends[i] = []
            for k, chip in enumerate(other_chips):
                send_b[i][k] = pair_sum(i, 2 * chip[0] + chip[1]).astype(BF16)
                big_sends[i].append(remote(send_b[i].at[k], recv_b[i].at[k], (*chip, c)))
                big_sends[i][k].start()
        last_swaps, keeps = {}, {}
        for i in early + order:
            hr = halves[i][0]
            if i in landed:
                for cp in early_loads[i]:
                    cp.wait()
                total = own_e[i][...]
                for k in range(len(_RELATIONS)):
                    total = total + land_e[i][k]
                pme[i][...] = total
            else:
                for k in range(3):
                    big_sends[i][k].wait_recv()
                pme[i][...] = ((pair_sum(i, me) + recv_b[i][0].astype(F32)) + recv_b[i][1].astype(F32)) + recv_b[i][2].astype(F32)
            mine = out_refs[i].at[pl.ds(c * hr, hr), :]
            keeps[i] = pltpu.make_async_copy(pme[i], mine, local_sems.at[i])
            keeps[i].start()
            last_swaps[i] = remote(pme[i], mine, sibling)
            last_swaps[i].start()

        for gi, (_, _, members) in enumerate(_SMALL_GROUPS):
            for k in range(3):
                small_sends[gi][k].wait_recv()
            total = None
            for j in range(N_CHIPS):
                rel = jnp.bitwise_xor(j, me)
                term = jnp.where(rel == 0, s_pair[gi][...], jnp.where(
                    rel == 2, s_chips[gi][0], jnp.where(rel == 1, s_chips[gi][1], s_chips[gi][2])))
                total = term if total is None else total + term
            s_sib[gi][...] = total
            for name, r0 in members:
                r, n = _small_shape(name)
                stage[name][...] = s_sib[gi][r0:r0 + r, 0:n]
        my_index = 4 * x + 2 * y + c
        for cp in landed_loads:
            cp.wait()
        for name in early_small:
            total = None
            for d in range(2 * N_CHIPS):
                rel = jnp.bitwise_xor(d, my_index)
                term = stage[name][...]
                for k in range(len(_RELATIONS)):
                    term = jnp.where(rel == k + 1, land_s[name][k], term)
                total = term if total is None else total + term
            stage[name][...] = total
        stores = [pltpu.make_async_copy(stage[name], small_out_refs[name], local_sems.at[small_sem0 + a])
                  for a, name in enumerate(names)]
        for cp in stores:
            cp.start()

        for i in range(n_t):
            last_swaps[i].wait_recv()
            keeps[i].wait()
        for cp in stores:
            cp.wait()
        groups = list(big_swaps.values()) + small_sends + list(big_sends.values())
        for cp in small_swaps + [cp for group in groups for cp in group] + list(last_swaps.values()):
            cp.wait_send()

    any_spec = pl.BlockSpec(memory_space=pl.ANY)
    small_shapes = [_sds(_small_shape(n)) for n in names]
    group_shapes = [shape for _, shape, _ in _SMALL_GROUPS]
    vmem = lambda which, dtype, lead=(): [pltpu.VMEM(lead + halves[i], dtype) for i in which]
    outs = _call(
        body, name="exchange_grads",
        in_specs=[any_spec] * (n_t + len(names) + len(early) + len(early_small)),
        out_specs=[any_spec] * (n_t + len(names)),
        out_shape=[_sds((b.shape[0] // N_CHIPS, b.shape[1])) for b in big] + small_shapes,
        scratch_shapes=(vmem(late, F32, (N_CHIPS,)) + vmem(late, F32, (N_CHIPS,)) + vmem(range(n_t), F32)
                        + vmem(late, BF16, (3,)) + vmem(late, BF16, (3,))
                        + vmem(early, F32) + vmem(early, F32, (len(_RELATIONS),))
                        + [pltpu.VMEM((len(_RELATIONS),) + _small_shape(n), F32) for n in early_small]
                        + [pltpu.VMEM(s, F32) for s in group_shapes] * 2 + [pltpu.VMEM((3,) + s, F32) for s in group_shapes]
                        + [pltpu.VMEM(s, F32) for s in group_shapes]
                        + [pltpu.VMEM(_small_shape(n), F32) for n in names]
                        + [pltpu.SemaphoreType.DMA((n_sems,)), pltpu.SemaphoreType.DMA((n_sems,)),
                           pltpu.SemaphoreType.DMA((landed_sem0 + len(early_small),))]),
        compiler_params=_params(48),
    )(*big, *[small[n] for n in names], *[landed[i] for i in early], *[landed_small[n] for n in early_small])
    return list(outs[:n_t]), dict(zip(names, outs[n_t:n_t + len(names)]))


def _adamw_update(w, g, m, v):
    m = ADAM_B1 * m + (1.0 - ADAM_B1) * g
    v = ADAM_B2 * v + (1.0 - ADAM_B2) * (g * g)
    m_hat = m / (1.0 - ADAM_B1 ** ADAM_STEP)
    v_hat = v / (1.0 - ADAM_B2 ** ADAM_STEP)
    return -ADAM_LR * (m_hat / (jnp.sqrt(v_hat) + ADAM_EPS) + ADAM_WD * w), m, v


def _adamw(w, g, m, v, grid, name):
    n_t = len(w)

    def body(*refs):
        ins, outs = refs[:4 * n_t], refs[4 * n_t:]
        for i in range(n_t):
            w_, g_, m_, v_ = [ins[a * n_t + i][...] for a in range(4)]
            vals = (g_,) + _adamw_update(w_, g_, m_, v_)
            for a in range(4):
                outs[a * n_t + i][...] = vals[a]

    specs = [pl.BlockSpec((a.shape[0] // grid, a.shape[1]), lambda i: (i, 0)) for a in w]
    shapes = [_sds(a.shape) for a in w]
    outs = _call(
        body, name=name, grid=(grid,), in_specs=specs * 4, out_specs=specs * 4, out_shape=shapes * 4,
        compiler_params=_params(40, ("arbitrary",)),
    )(*w, *g, *m, *v)
    return [outs[a * n_t:(a + 1) * n_t] for a in range(4)]


def kernel(x, p, pre_norm_g, w_in, ssm_lam_re, ssm_lam_im, ssm_log_step, ssm_b_re, ssm_b_im, ssm_c_re, ssm_c_im, ssm_d, ssm_w_glu, ssm_b_glu, attn_sinks, w_out, post_norm_g, pl_w_proj, pl_w_gate, pl_b_gate, loss_target, m_pre_norm_g, m_w_in, m_ssm_lam_re, m_ssm_lam_im, m_ssm_log_step, m_ssm_b_re, m_ssm_b_im, m_ssm_c_re, m_ssm_c_im, m_ssm_d, m_ssm_w_glu, m_ssm_b_glu, m_attn_sinks, m_w_out, m_post_norm_g, m_pl_w_proj, m_pl_w_gate, m_pl_b_gate, v_pre_norm_g, v_w_in, v_ssm_lam_re, v_ssm_lam_im, v_ssm_log_step, v_ssm_b_re, v_ssm_b_im, v_ssm_c_re, v_ssm_c_im, v_ssm_d, v_ssm_w_glu, v_ssm_b_glu, v_attn_sinks, v_w_out, v_post_norm_g, v_pl_w_proj, v_pl_w_gate, v_pl_b_gate):
    weights = dict(pre_norm_g=pre_norm_g, w_in=w_in, ssm_lam_re=ssm_lam_re, ssm_lam_im=ssm_lam_im,
                   ssm_log_step=ssm_log_step, ssm_b_re=ssm_b_re, ssm_b_im=ssm_b_im, ssm_c_re=ssm_c_re,
                   ssm_c_im=ssm_c_im, ssm_d=ssm_d, ssm_w_glu=ssm_w_glu, ssm_b_glu=ssm_b_glu, attn_sinks=attn_sinks,
                   w_out=w_out, post_norm_g=post_norm_g, pl_w_proj=pl_w_proj, pl_w_gate=pl_w_gate, pl_b_gate=pl_b_gate)
    m_in = dict(pre_norm_g=m_pre_norm_g, w_in=m_w_in, ssm_lam_re=m_ssm_lam_re, ssm_lam_im=m_ssm_lam_im,
                ssm_log_step=m_ssm_log_step, ssm_b_re=m_ssm_b_re, ssm_b_im=m_ssm_b_im, ssm_c_re=m_ssm_c_re,
                ssm_c_im=m_ssm_c_im, ssm_d=m_ssm_d, ssm_w_glu=m_ssm_w_glu, ssm_b_glu=m_ssm_b_glu,
                attn_sinks=m_attn_sinks, w_out=m_w_out, post_norm_g=m_post_norm_g, pl_w_proj=m_pl_w_proj,
                pl_w_gate=m_pl_w_gate, pl_b_gate=m_pl_b_gate)
    v_in = dict(pre_norm_g=v_pre_norm_g, w_in=v_w_in, ssm_lam_re=v_ssm_lam_re, ssm_lam_im=v_ssm_lam_im,
                ssm_log_step=v_ssm_log_step, ssm_b_re=v_ssm_b_re, ssm_b_im=v_ssm_b_im, ssm_c_re=v_ssm_c_re,
                ssm_c_im=v_ssm_c_im, ssm_d=v_ssm_d, ssm_w_glu=v_ssm_w_glu, ssm_b_glu=v_ssm_b_glu,
                attn_sinks=v_attn_sinks, w_out=v_w_out, post_norm_g=v_post_norm_g, pl_w_proj=v_pl_w_proj,
                pl_w_gate=v_pl_w_gate, pl_b_gate=v_pl_b_gate)

    def two_d(tree):
        return {k: _to_kernel_form(k, a) for k, a in tree.items()}

    w2, m2, v2 = two_d(weights), two_d(m_in), two_d(v_in)

    (w_in_full,) = _gather_weights_beside([w2["w_in"].astype(BF16)], "gather_w_in_beside", 4)
    s5_params = tuple(w2[n] for n in ("ssm_lam_re", "ssm_lam_im", "ssm_log_step", "ssm_b_re", "ssm_b_im", "ssm_c_re",
                                      "ssm_c_im"))
    s5_operands = _s5_params_fwd(*s5_params)
    hn = _pre_norm(x.reshape(-1, D_MODEL), w2["pre_norm_g"])
    behind = s5_operands[0][0, 0] * 0.0 + hn[0, 0].astype(F32) * 0.0
    rest = _gather_weights_beside([(w2[n] + behind).astype(BF16) for n in _BIG[1:]], "gather_weights_beside", 1)
    full = dict(zip(_BIG, [w_in_full] + rest))
    grad_x, loss, grads = _local_step(
        x, hn, p, loss_target, w2["pre_norm_g"], full["w_in"], s5_params, s5_operands, w2["ssm_d"], full["ssm_w_glu"], w2["ssm_b_glu"],
        w2["attn_sinks"], full["w_out"], w2["post_norm_g"], full["pl_w_proj"], full["pl_w_gate"], w2["pl_b_gate"])

    sent_early = ("w_out", "pl_w_gate", "pl_w_proj")
    landed = dict(zip([_BIG.index(n) for n in sent_early], _scatter_beside([grads[n] for n in sent_early])))
    after_scatter = landed[_BIG.index(sent_early[-1])][0, 0, 0] * 0.0
    late_operands = [grads[_SMALL_EARLY[0]] + after_scatter] + [grads[n] for n in _SMALL_EARLY[1:]]
    landed_small = dict(zip(_SMALL_EARLY, _broadcast_beside(late_operands)))
    g_big, g_small = _exchange_grads([grads[n] for n in _BIG], {**{n: grads[n] for n in _SMALL}, "loss": loss}, landed,
                                     landed_small)
    g_big = dict(zip(_BIG, g_big))
    total_loss = g_small.pop("loss")

    big_out = _adamw([w2[n] for n in _BIG], [g_big[n] for n in _BIG], [m2[n] for n in _BIG], [v2[n] for n in _BIG],
                     8, "adamw_matrices")
    small_names = tuple(_SMALL)
    small_out = _adamw([w2[n] for n in small_names], [g_small[n] for n in small_names], [m2[n] for n in small_names],
                       [v2[n] for n in small_names], 1, "adamw_small")

    results = [{**dict(zip(_BIG, big_part)), **dict(zip(small_names, small_part))}
               for big_part, small_part in zip(big_out, small_out)]
    flat = [_from_kernel_form(name, r[name], weights[name].shape) for r in results for name in _WEIGHT_ORDER]
    return (total_loss.reshape(()), grad_x, *flat)
```

```python
import math

import jax
import jax.numpy as jnp
from jax import lax
from jax.experimental import pallas as pl
from jax.experimental.pallas import tpu as pltpu
from jax.experimental.pallas import tpu_sc as plsc

F32 = jnp.float32
BF16 = jnp.bfloat16

D_MODEL = 1024
D_SSM = 512
D_ATTN = 512
SSM_GROUPS = 32
SSM_GROUP_CH = 16
SSM_STATE = 64
SSM_LANES = SSM_GROUPS * SSM_STATE
HEAD_DIM = 64
N_HEADS = 8
KV_HEADS = 2
Q_PER_KV = 4
WINDOW = 128
BLOCK = 128
D_PLE = 256
D_IN = 2304
EPS = 1e-6
ATTN_SCALE = 1.0 / math.sqrt(HEAD_DIM)

ADAM_LR = 0.001
ADAM_B1 = 0.9
ADAM_B2 = 0.999
ADAM_EPS = 1e-08
ADAM_WD = 0.01
ADAM_STEP = 10

N_CHIPS = 4
LANES = 128
SCAN_CHUNKS = 8
SCAN_TILE_STEPS = 32
SCAN_LANE_CHUNK = 512
MIB = 2 ** 20
MESH = pl.DeviceIdType.MESH


def _dot(a, b):
    return jnp.dot(a, b, preferred_element_type=F32)


def _dot_nt(a, b):
    return lax.dot_general(a, b, (((1,), (1,)), ((), ())), preferred_element_type=F32)


def _dot_tn(a, b):
    return lax.dot_general(a, b, (((0,), (0,)), ((), ())), preferred_element_type=F32)


def _params(vmem_mib, semantics=None):
    kw = dict(vmem_limit_bytes=vmem_mib * MIB)
    if semantics is not None:
        kw["dimension_semantics"] = semantics
    return pltpu.CompilerParams(**kw)


def _full(shape):
    nd = len(shape)
    return pl.BlockSpec(shape, lambda *_: (0,) * nd, pipeline_mode=pl.Buffered(1))


def _rows(tm, width):
    return pl.BlockSpec((tm, width), lambda i: (i, 0))


def _sds(shape, dtype=F32):
    return pltpu.HBM(shape, dtype)


def _call(body, **kw):
    fn = pl.pallas_call(body, **kw)
    return lambda *args: fn(*[pltpu.with_memory_space_constraint(a, pltpu.HBM) for a in args])


def _silu(z):
    return z * jax.nn.sigmoid(z)


def _pre_norm(x2d, g1):
    rows = x2d.shape[0]
    tm = 512

    def body(x_ref, g_ref, hn_ref):
        x = x_ref[...]
        r = lax.rsqrt(jnp.mean(x * x, axis=-1, keepdims=True) + EPS)
        hn_ref[...] = (x * r * g_ref[...]).astype(BF16)

    return _call(
        body, name="pre_norm", grid=(rows // tm,), in_specs=[_rows(tm, D_MODEL), _full((1, D_MODEL))],
        out_specs=_rows(tm, D_MODEL), out_shape=_sds((rows, D_MODEL), BF16), compiler_params=_params(32, ("arbitrary",)),
    )(x2d, g1)


def _in_proj(hn, w_in_t, n_seq, seq):
    rows = hn.shape[0]
    tm = 1024
    slab, steps, _, _ = _scan_geometry(n_seq, seq)

    def body(hn_ref, w_ref, *out_refs):
        u_parts, (zs_ref, q_ref, k_ref, v_ref, za_ref) = out_refs[:_SCAN_PARTS], out_refs[_SCAN_PARTS:]
        whole = _dot_nt(hn_ref[...], w_ref[...])

        def proj(a, b):
            return whole[:, a:b]

        _store_chunks(u_parts, pl.program_id(0) * (tm // steps), proj(0, 512), steps, slab)
        zs_ref[...] = proj(512, 1024)
        q_ref[...] = (proj(1024, 1536) * ATTN_SCALE).astype(BF16)
        k_ref[...] = proj(1536, 1664).astype(BF16)
        v_ref[...] = proj(1664, 1792).astype(BF16)
        za_ref[...] = proj(1792, 2304)

    *u_parts, zs, q, k, v, za = _call(
        body, name="in_proj", grid=(rows // tm,),
        in_specs=[_rows(tm, D_MODEL), _full((D_IN, D_MODEL))],
        out_specs=_whole_parts(rows) + [_rows(tm, 512), _rows(tm, 512), _rows(tm, 128), _rows(tm, 128), _rows(tm, 512)],
        out_shape=_part_shapes(rows) + [_sds((rows, 512)), _sds((rows, 512), BF16), _sds((rows, 128), BF16),
                                        _sds((rows, 128), BF16), _sds((rows, 512))],
        compiler_params=_params(48, ("arbitrary",)),
    )(hn, w_in_t)
    return u_parts, zs, q, k, v, za


_EARLY_COLS = D_MODEL // 2


def _in_proj_bwd_early(hn, du_parts, dzs, dq, dk, dv, dza, n_seq, seq):
    rows = hn.shape[0]
    tm = 512
    slab, steps, _, _ = _scan_geometry(n_seq, seq)

    def body(hn_ref, *refs):
        du_parts, (dzs_ref, dq_ref, dk_ref, dv_ref, dza_ref, dproj_ref, dw_ref, dwb_ref) = refs[:_SCAN_PARTS], refs[_SCAN_PARTS:]
        i = pl.program_id(0)

        @pl.when(i == 0)
        def _():
            dw_ref[...] = jnp.zeros_like(dw_ref)

        du = _load_chunks(du_parts, i * (tm // steps), tm // steps, steps, slab)
        d_proj = jnp.concatenate([du.astype(BF16), dzs_ref[...], dq_ref[...], dk_ref[...], dv_ref[...], dza_ref[...]],
                                 axis=1)
        dproj_ref[...] = d_proj
        dw_ref[...] += _dot_tn(d_proj, hn_ref[...])

        @pl.when(i == rows // tm - 1)
        def _():
            dwb_ref[...] = dw_ref[...].astype(BF16)

    return _call(
        body, name="in_proj_bwd_early", grid=(rows // tm,),
        in_specs=[_rows(tm, _EARLY_COLS)] + _whole_parts(rows)
        + [_rows(tm, 512), _rows(tm, 512), _rows(tm, 128), _rows(tm, 128), _rows(tm, 512)],
        out_specs=[_rows(tm, D_IN), _full((D_IN, _EARLY_COLS)), _full((D_IN, _EARLY_COLS))],
        out_shape=[_sds((rows, D_IN), BF16), _sds((D_IN, _EARLY_COLS)), _sds((D_IN, _EARLY_COLS), BF16)],
        compiler_params=_params(48, ("arbitrary",)),
    )(hn, *du_parts, dzs, dq, dk, dv, dza)


def _in_proj_bwd(x2d, dh1, g1, w_in_t, d_proj):
    rows = x2d.shape[0]
    tm = 512

    def body(x_ref, dh1_ref, g_ref, w_ref, dproj_ref, gx_ref, dw_ref, dg_ref):
        @pl.when(pl.program_id(0) == 0)
        def _():
            dw_ref[...] = jnp.zeros_like(dw_ref)
            dg_ref[...] = jnp.zeros_like(dg_ref)

        x = x_ref[...]
        g = g_ref[...]
        r = lax.rsqrt(jnp.mean(x * x, axis=-1, keepdims=True) + EPS)
        xr = x * r
        hn = (xr[:, _EARLY_COLS:] * g[:, _EARLY_COLS:]).astype(BF16)
        d_proj = dproj_ref[...]
        dhn = _dot(d_proj, w_ref[...])
        dw_ref[...] += _dot_tn(d_proj, hn)
        dg_ref[...] += jnp.sum(dhn * xr, axis=0, keepdims=True)
        a_ = dhn * g
        gx_ref[...] = dh1_ref[...] + r * a_ - xr * (r * jnp.mean(a_ * xr, axis=-1, keepdims=True))

    late_cols = D_MODEL - _EARLY_COLS
    return _call(
        body, name="in_proj_bwd", grid=(rows // tm,),
        in_specs=[_rows(tm, D_MODEL), _rows(tm, D_MODEL), _full((1, D_MODEL)), _full((D_IN, D_MODEL)), _rows(tm, D_IN)],
        out_specs=[_rows(tm, D_MODEL), _full((D_IN, late_cols)), _full((1, D_MODEL))],
        out_shape=[_sds((rows, D_MODEL)), _sds((D_IN, late_cols)), _sds((1, D_MODEL))],
        compiler_params=_params(52, ("arbitrary",)),
    )(x2d, dh1, g1, w_in_t, d_proj)


def _iota(shape, axis):
    return lax.broadcasted_iota(jnp.int32, shape, axis)


def _sum_of_thirds(f, a):
    hi = a.astype(BF16)
    rest = a - hi.astype(F32)
    mid = rest.astype(BF16)
    low = (rest - mid.astype(F32)).astype(BF16)
    return (f(hi) + f(mid)) + f(low)


@jax.custom_vjp
def _pick_rows(e, a):
    return _sum_of_thirds(lambda part: _dot(e, part), a)


def _pick_rows_fwd(e, a):
    return _pick_rows(e, a), e


def _pick_rows_bwd(e, ct):
    return jnp.zeros_like(e), _sum_of_thirds(lambda part: _dot_tn(e, part), ct)


_pick_rows.defvjp(_pick_rows_fwd, _pick_rows_bwd)


@jax.custom_vjp
def _pick_cols(a, e):
    return _sum_of_thirds(lambda part: _dot(part, e), a)


def _pick_cols_fwd(a, e):
    return _pick_cols(a, e), e


def _pick_cols_bwd(e, ct):
    return _sum_of_thirds(lambda part: _dot_nt(part, e), ct), jnp.zeros_like(e)


_pick_cols.defvjp(_pick_cols_fwd, _pick_cols_bwd)


_HALF_GROUPS = SSM_GROUPS // 2
_N_SHIFT = SSM_STATE.bit_length() - 1
_P_SHIFT = SSM_GROUP_CH.bit_length() - 1


def _s5_operands(lam_re, lam_im, log_step, b_re, b_im, c_re, c_im):
    g, n, p = SSM_GROUPS, SSM_STATE, SSM_GROUP_CH
    gn, gp, hn_, hp = g * n, g * p, _HALF_GROUPS * n, _HALF_GROUPS * p
    eye_g = _iota((g, g), 0) == _iota((g, g), 1)
    step = jnp.sum(jnp.where(eye_g, jnp.exp(log_step), 0.0), axis=1, keepdims=True)
    a_re = lam_re * step
    a_im = lam_im * step
    mag = jnp.exp(a_re)
    lbar_re = mag * jnp.cos(a_im)
    lbar_im = mag * jnp.sin(a_im)
    n_re = lbar_re - 1.0
    den = lam_re * lam_re + lam_im * lam_im
    f_re = (n_re * lam_re + lbar_im * lam_im) / den
    f_im = (lbar_im * lam_re - n_re * lam_im) / den

    spread_n = (_iota((n, gn), 0) == (_iota((n, gn), 1) & (n - 1))).astype(BF16)
    own_g = _iota((g, gn), 0) == (_iota((g, gn), 1) >> _N_SHIFT)

    def to_row(a):
        return jnp.sum(jnp.where(own_g, _pick_cols(a, spread_n), 0.0), axis=0, keepdims=True)

    per_group = ((_iota((gp, g), 0) >> _P_SHIFT) == _iota((gp, g), 1)).astype(BF16)
    fx_re, fx_im = _pick_rows(per_group, f_re), _pick_rows(per_group, f_im)
    bbar_re = fx_re * b_re - fx_im * b_im
    bbar_im = fx_re * b_im + fx_im * b_re

    tile_n = (_iota((n, hn_), 0) == (_iota((n, hn_), 1) & (n - 1))).astype(BF16)
    same_group = (_iota((hp, hn_), 0) >> _P_SHIFT) == (_iota((hp, hn_), 1) >> _N_SHIFT)

    def embed(a, hf):
        return jnp.where(same_group, _pick_cols(a[hf * hp:(hf + 1) * hp], tile_n), 0.0)

    return (to_row(lbar_re), to_row(lbar_im), embed(bbar_re, 0), embed(bbar_re, 1), embed(bbar_im, 0),
            embed(bbar_im, 1), embed(c_re, 0), embed(c_re, 1), embed(c_im, 0), embed(c_im, 1))


_S5_PARAM_SHAPES = ((SSM_GROUPS, SSM_STATE), (SSM_GROUPS, SSM_STATE), (1, SSM_GROUPS),
                    (D_SSM, SSM_STATE), (D_SSM, SSM_STATE), (D_SSM, SSM_STATE), (D_SSM, SSM_STATE))
_CM_SHAPE = (2, _HALF_GROUPS * SSM_GROUP_CH, _HALF_GROUPS * SSM_STATE)
_S5_OPERAND_SHAPES = ((1, SSM_LANES), (1, SSM_LANES), _CM_SHAPE, _CM_SHAPE, _CM_SHAPE, _CM_SHAPE)


def _s5_params_fwd(*params):
    def body(*refs):
        ins, (lre_ref, lim_ref, btre_ref, btim_ref, cmre_ref, cmim_ref) = refs[:7], refs[7:]
        vals = _s5_operands(*[r[...] for r in ins])
        lre_ref[...] = vals[0]
        lim_ref[...] = vals[1]
        for ref, pair in zip((btre_ref, btim_ref, cmre_ref, cmim_ref), (vals[2:4], vals[4:6], vals[6:8], vals[8:10])):
            ref[0] = pair[0].astype(BF16)
            ref[1] = pair[1].astype(BF16)

    dtypes = (F32, F32, BF16, BF16, BF16, BF16)
    return _call(
        body, name="s5_params_fwd",
        in_specs=[_full(s) for s in _S5_PARAM_SHAPES], out_specs=[_full(s) for s in _S5_OPERAND_SHAPES],
        out_shape=[_sds(s, d) for s, d in zip(_S5_OPERAND_SHAPES, dtypes)], compiler_params=_params(32),
    )(*params)


def _s5_params_bwd(params, cotangents):
    def body(*refs):
        ins, (dlre, dlim, dbtre, dbtim, dcmre, dcmim), outs = refs[:7], refs[7:13], refs[13:]
        _, vjp = jax.vjp(_s5_operands, *[r[...] for r in ins])
        cts = (dlre[...], dlim[...], dbtre[0], dbtre[1], dbtim[0], dbtim[1], dcmre[0], dcmre[1], dcmim[0], dcmim[1])
        for ref, val in zip(outs, vjp(cts)):
            ref[...] = val

    return _call(
        body, name="s5_params_bwd",
        in_specs=[_full(s) for s in _S5_PARAM_SHAPES + _S5_OPERAND_SHAPES],
        out_specs=[_full(s) for s in _S5_PARAM_SHAPES],
        out_shape=[_sds(s) for s in _S5_PARAM_SHAPES], compiler_params=_params(48),
    )(*params, *cotangents)


def _scan_geometry(n_seq, seq):
    slab = n_seq * SCAN_CHUNKS
    steps = seq // SCAN_CHUNKS
    tile_rows = slab * SCAN_TILE_STEPS
    n_tiles = steps // SCAN_TILE_STEPS
    return slab, steps, tile_rows, n_tiles


_SCAN_PARTS = D_SSM // LANES


def _whole_parts(rows):
    return [_full((rows, LANES))] * _SCAN_PARTS


def _part_shapes(rows):
    return [_sds((rows, LANES))] * _SCAN_PARTS


def _load_chunks(parts, first_chunk, n_chunks, steps, slab):
    return jnp.concatenate([
        jnp.concatenate([ref[pl.ds(first_chunk + q, steps, stride=slab), :] for ref in parts], axis=1)
        for q in range(n_chunks)], axis=0)


def _store_chunks(parts, first_chunk, value, steps, slab):
    for q in range(value.shape[0] // steps):
        for j, ref in enumerate(parts):
            ref[pl.ds(first_chunk + q, steps, stride=slab), :] = value[q * steps:(q + 1) * steps,
                                                                     j * LANES:(j + 1) * LANES]


def _join_parts(parts):
    return jnp.concatenate([ref[...] for ref in parts], axis=1)


def _split_parts(parts, value):
    for j, ref in enumerate(parts):
        ref[...] = value[:, j * LANES:(j + 1) * LANES]


def _complex_power(re, im, n):
    out = None
    while n:
        if n & 1:
            out = (re, im) if out is None else (out[0] * re - out[1] * im, out[0] * im + out[1] * re)
        n >>= 1
        if n:
            re, im = re * re - im * im, 2.0 * re * im
    return out


def _chunk_carry(sum_re, sum_im, carry_re, carry_im, a_re, a_im, n_seq, reverse):
    carry_re[...] = jnp.zeros_like(carry_re)
    carry_im[...] = jnp.zeros_like(carry_im)
    for s in range(n_seq):
        order = range(SCAN_CHUNKS - 2, -1, -1) if reverse else range(1, SCAN_CHUNKS)
        for c in order:
            r = s * SCAN_CHUNKS + c
            p = r + 1 if reverse else r - 1
            p_re, p_im = carry_re[p:p + 1, :], carry_im[p:p + 1, :]
            carry_re[r:r + 1, :] = a_re * p_re - a_im * p_im + sum_re[p:p + 1, :]
            carry_im[r:r + 1, :] = a_re * p_im + a_im * p_re + sum_im[p:p + 1, :]


def _s5_scan_fwd(u_parts, bt_re, bt_im, cm_re, cm_im, lbar_re, lbar_im, d_row, n_seq, seq):
    slab, steps, tile_rows, n_tiles = _scan_geometry(n_seq, seq)
    rows = u_parts[0].shape[0]

    def body(*refs):
        u_refs, refs = refs[:_SCAN_PARTS], refs[_SCAN_PARTS:]
        (bre_ref, bim_ref, cre_ref, cim_ref, lre_ref, lim_ref, d_ref), refs = refs[:7], refs[7:]
        y_refs, (hre_ref, him_ref, st_re, st_im, h0_re, h0_im, buf_re, buf_im) = refs[:_SCAN_PARTS], refs[_SCAN_PARTS:]
        second = pl.program_id(0) == 1
        i = pl.program_id(1)

        @pl.when(jnp.logical_and(i == 0, jnp.logical_not(second)))
        def _():
            st_re[...] = jnp.zeros_like(st_re)
            st_im[...] = jnp.zeros_like(st_im)

        u = _join_parts(u_refs)
        ub = u.astype(BF16)
        for hf in range(2):
            cols = slice(hf * 1024, (hf + 1) * 1024)
            buf_re[:, cols] = _dot(ub[:, hf * 256:(hf + 1) * 256], bre_ref[hf])
            buf_im[:, cols] = _dot(ub[:, hf * 256:(hf + 1) * 256], bim_ref[hf])

        for lc in range(SSM_LANES // SCAN_LANE_CHUNK):
            cols = slice(lc * SCAN_LANE_CHUNK, (lc + 1) * SCAN_LANE_CHUNK)
            l_re = jnp.broadcast_to(lre_ref[:, cols], (slab, SCAN_LANE_CHUNK))
            l_im = jnp.broadcast_to(lim_ref[:, cols], (slab, SCAN_LANE_CHUNK))

            def scan_tile(keep_states):
                def step(t, carry):
                    s_re, s_im = carry
                    r0 = pl.multiple_of(t * slab, slab)
                    n_re = l_re * s_re - l_im * s_im + buf_re[pl.ds(r0, slab), cols]
                    n_im = l_re * s_im + l_im * s_re + buf_im[pl.ds(r0, slab), cols]
                    if keep_states:
                        buf_re[pl.ds(r0, slab), cols] = n_re
                        buf_im[pl.ds(r0, slab), cols] = n_im
                    return n_re, n_im

                s_re, s_im = lax.fori_loop(0, SCAN_TILE_STEPS, step, (st_re[:, cols], st_im[:, cols]), unroll=True)
                st_re[:, cols] = s_re
                st_im[:, cols] = s_im

            pl.when(jnp.logical_not(second))(lambda: scan_tile(False))
            pl.when(second)(lambda: scan_tile(True))

        @pl.when(jnp.logical_and(i == n_tiles - 1, jnp.logical_not(second)))
        def _():
            a_re, a_im = _complex_power(lre_ref[...], lim_ref[...], steps)
            _chunk_carry(st_re, st_im, h0_re, h0_im, a_re, a_im, n_seq, reverse=False)
            st_re[...] = h0_re[...]
            st_im[...] = h0_im[...]

        @pl.when(second)
        def _():
            h_re = buf_re[...].astype(BF16)
            h_im = buf_im[...].astype(BF16)
            hre_ref[...] = h_re
            him_ref[...] = h_im
            for hf in range(2):
                cols = slice(hf * 1024, (hf + 1) * 1024)
                ycols = slice(hf * 256, (hf + 1) * 256)
                y_half = (_dot_nt(h_re[:, cols], cre_ref[hf]) - _dot_nt(h_im[:, cols], cim_ref[hf])
                          + d_ref[:, ycols] * u[:, ycols])
                _split_parts(y_refs[2 * hf:2 * hf + 2], y_half)

    tile = lambda w: pl.BlockSpec((tile_rows, w), lambda p, i: (i, 0))
    out_tile = lambda w: pl.BlockSpec((tile_rows, w), lambda p, i: (i * p, 0))
    cm = _full(_CM_SHAPE)
    outs = _call(
        body, name="s5_scan_fwd", grid=(2, n_tiles),
        in_specs=[tile(LANES)] * _SCAN_PARTS + [cm, cm, cm, cm, _full((1, SSM_LANES)), _full((1, SSM_LANES)),
                                                _full((1, 512))],
        out_specs=[out_tile(LANES)] * _SCAN_PARTS + [out_tile(SSM_LANES), out_tile(SSM_LANES)],
        out_shape=_part_shapes(rows) + [_sds((rows, SSM_LANES), BF16), _sds((rows, SSM_LANES), BF16)],
        scratch_shapes=[pltpu.VMEM((slab, SSM_LANES), F32)] * 4 + [pltpu.VMEM((tile_rows, SSM_LANES), F32)] * 2,
        compiler_params=_params(40, ("arbitrary", "arbitrary")),
    )(*u_parts, bt_re, bt_im, cm_re, cm_im, lbar_re, lbar_im, d_row)
    return outs[:_SCAN_PARTS], outs[_SCAN_PARTS], outs[_SCAN_PARTS + 1]


def _s5_scan_bwd(dy_parts, u_parts, h_re, h_im, bt_re, bt_im, cm_re, cm_im, lbar_re, lbar_im, d_row, n_seq, seq):
    slab, steps, tile_rows, n_tiles = _scan_geometry(n_seq, seq)
    rows = u_parts[0].shape[0]

    def body(*refs):
        dy_refs, u_refs, refs = refs[:_SCAN_PARTS], refs[_SCAN_PARTS:2 * _SCAN_PARTS], refs[2 * _SCAN_PARTS:]
        (hre_ref, him_ref, bre_ref, bim_ref, cre_ref, cim_ref, lre_ref, lim_ref, d_ref), refs = refs[:9], refs[9:]
        du_refs, refs = refs[:_SCAN_PARTS], refs[_SCAN_PARTS:]
        (dbre_ref, dbim_ref, dcre_ref, dcim_ref, dlre_ref, dlim_ref, dd_ref,
         st_re, st_im, g0_re, g0_im, acc_re, acc_im, buf_re, buf_im) = refs
        second = pl.program_id(0) == 1
        i = pl.program_id(1)

        @pl.when(jnp.logical_and(i == 0, jnp.logical_not(second)))
        def _():
            st_re[...] = jnp.zeros_like(st_re)
            st_im[...] = jnp.zeros_like(st_im)
            acc_re[...] = jnp.zeros_like(acc_re)
            acc_im[...] = jnp.zeros_like(acc_im)
            for ref in (dbre_ref, dbim_ref, dcre_ref, dcim_ref, dd_ref):
                ref[...] = jnp.zeros_like(ref)

        dy = _join_parts(dy_refs)
        dyb = dy.astype(BF16)
        for hf in range(2):
            cols = slice(hf * 1024, (hf + 1) * 1024)
            buf_re[:, cols] = _dot(dyb[:, hf * 256:(hf + 1) * 256], cre_ref[hf])
            buf_im[:, cols] = -_dot(dyb[:, hf * 256:(hf + 1) * 256], cim_ref[hf])

        for lc in range(SSM_LANES // SCAN_LANE_CHUNK):
            cols = slice(lc * SCAN_LANE_CHUNK, (lc + 1) * SCAN_LANE_CHUNK)
            l_re = jnp.broadcast_to(lre_ref[:, cols], (slab, SCAN_LANE_CHUNK))
            l_im = jnp.broadcast_to(lim_ref[:, cols], (slab, SCAN_LANE_CHUNK))

            def advance(r0, s_re, s_im):
                n_re = l_re * s_re + l_im * s_im + buf_re[pl.ds(r0, slab), cols]
                n_im = l_re * s_im - l_im * s_re + buf_im[pl.ds(r0, slab), cols]
                buf_re[pl.ds(r0, slab), cols] = n_re
                buf_im[pl.ds(r0, slab), cols] = n_im
                return n_re, n_im

            def row0(k):
                return pl.multiple_of((SCAN_TILE_STEPS - 1 - k) * slab, slab)

            @pl.when(jnp.logical_not(second))
            def _():
                s_re, s_im = lax.fori_loop(0, SCAN_TILE_STEPS, lambda k, s: advance(row0(k), *s),
                                           (st_re[:, cols], st_im[:, cols]), unroll=True)
                st_re[:, cols] = s_re
                st_im[:, cols] = s_im

            @pl.when(second)
            def _():
                def step(k, carry):
                    s_re, s_im, a_re, a_im = carry
                    r0 = row0(k)
                    hr = hre_ref[pl.ds(r0, slab), cols].astype(F32)
                    hi = him_ref[pl.ds(r0, slab), cols].astype(F32)
                    a_re = a_re + s_re * hr + s_im * hi
                    a_im = a_im + s_im * hr - s_re * hi
                    return advance(r0, s_re, s_im) + (a_re, a_im)

                zero = jnp.zeros((slab, SCAN_LANE_CHUNK), F32)
                s_re, s_im, a_re, a_im = lax.fori_loop(
                    0, SCAN_TILE_STEPS, step, (st_re[:, cols], st_im[:, cols], zero, zero), unroll=True)
                st_re[:, cols] = s_re
                st_im[:, cols] = s_im
                acc_re[:, cols] += a_re
                acc_im[:, cols] += a_im

        @pl.when(jnp.logical_and(i == n_tiles - 1, jnp.logical_not(second)))
        def _():
            p_re, p_im = _complex_power(lre_ref[...], lim_ref[...], steps)
            _chunk_carry(st_re, st_im, g0_re, g0_im, p_re, -p_im, n_seq, reverse=True)
            st_re[...] = g0_re[...]
            st_im[...] = g0_im[...]

        @pl.when(second)
        def _():
            u = _join_parts(u_refs)
            ub = u.astype(BF16)
            g_re = buf_re[...].astype(BF16)
            g_im = buf_im[...].astype(BF16)
            dd_ref[...] += jnp.sum(dy * u, axis=0, keepdims=True)
            for hf in range(2):
                cols = slice(hf * 1024, (hf + 1) * 1024)
                ycols = slice(hf * 256, (hf + 1) * 256)
                du_half = (_dot_nt(g_re[:, cols], bre_ref[hf]) + _dot_nt(g_im[:, cols], bim_ref[hf])
                           + d_ref[:, ycols] * dy[:, ycols])
                _split_parts(du_refs[2 * hf:2 * hf + 2], du_half)
                for q4 in range(_HALF_GROUPS // 4):
                    ch = slice(hf * 256 + q4 * 64, hf * 256 + (q4 + 1) * 64)
                    st = slice(hf * 1024 + q4 * 256, hf * 1024 + (q4 + 1) * 256)
                    blk = (hf, slice(q4 * 64, (q4 + 1) * 64), slice(q4 * 256, (q4 + 1) * 256))
                    dbre_ref[blk] += _dot_tn(ub[:, ch], g_re[:, st])
                    dbim_ref[blk] += _dot_tn(ub[:, ch], g_im[:, st])
                    dcre_ref[blk] += _dot_tn(dyb[:, ch], hre_ref[:, st])
                    dcim_ref[blk] -= _dot_tn(dyb[:, ch], him_ref[:, st])

        @pl.when(jnp.logical_and(i == n_tiles - 1, second))
        def _():
            dlre_ref[...] = jnp.sum(acc_re[...], axis=0, keepdims=True)
            dlim_ref[...] = jnp.sum(acc_im[...], axis=0, keepdims=True)

    tile = lambda w: pl.BlockSpec((tile_rows, w), lambda p, i: (n_tiles - 1 - i, 0))
    second_tile = lambda w: pl.BlockSpec((tile_rows, w), lambda p, i: (n_tiles - 1 - i * p, 0))
    cm = _full(_CM_SHAPE)
    row = _full((1, SSM_LANES))
    outs = _call(
        body, name="s5_scan_bwd", grid=(2, n_tiles),
        in_specs=[tile(LANES)] * _SCAN_PARTS + [second_tile(LANES)] * _SCAN_PARTS
        + [second_tile(SSM_LANES), second_tile(SSM_LANES), cm, cm, cm, cm, row, row, _full((1, 512))],
        out_specs=[second_tile(LANES)] * _SCAN_PARTS + [cm, cm, cm, cm, row, row, _full((1, 512))],
        out_shape=(_part_shapes(rows) + [_sds(_CM_SHAPE)] * 4 + [_sds((1, SSM_LANES))] * 2 + [_sds((1, 512))]),
        scratch_shapes=[pltpu.VMEM((slab, SSM_LANES), F32)] * 6 + [pltpu.VMEM((tile_rows, SSM_LANES), F32)] * 2,
        compiler_params=_params(48, ("arbitrary", "arbitrary")),
    )(*dy_parts, *u_parts, h_re, h_im, bt_re, bt_im, cm_re, cm_im, lbar_re, lbar_im, d_row)
    return (outs[:_SCAN_PARTS],) + tuple(outs[_SCAN_PARTS:])


def _glu_gate(gl, a, zs):
    return gl * jax.nn.sigmoid(a) * _silu(zs)


def _glu_fwd(y_parts, zs, w_glu, b_glu, n_seq, seq):
    rows = zs.shape[0]
    tm = 512
    slab, steps, _, _ = _scan_geometry(n_seq, seq)

    def body(*refs):
        y_refs, (zs_ref, w_ref, b_ref, o_ref) = refs[:_SCAN_PARTS], refs[_SCAN_PARTS:]
        y = _load_chunks(y_refs, pl.program_id(0) * (tm // steps), tm // steps, steps, slab)
        gl = jax.nn.gelu(y)
        a = _dot(gl.astype(BF16), w_ref[...]) + b_ref[...]
        o_ref[...] = _glu_gate(gl, a, zs_ref[...]).astype(BF16)

    return _call(
        body, name="glu_fwd", grid=(rows // tm,),
        in_specs=_whole_parts(rows) + [_rows(tm, 512), _full((512, 512)), _full((1, 512))],
        out_specs=_rows(tm, 512), out_shape=_sds((rows, 512), BF16),
        compiler_params=_params(32, ("arbitrary",)),
    )(*y_parts, zs, w_glu, b_glu)


def _glu_bwd(y_parts, zs, d_out, w_glu, b_glu, n_seq, seq):
    rows = zs.shape[0]
    tm = 512
    slab, steps, _, _ = _scan_geometry(n_seq, seq)

    def body(*refs):
        y_refs, (zs_ref, d_ref, w_ref, b_ref), refs = refs[:_SCAN_PARTS], refs[_SCAN_PARTS:_SCAN_PARTS + 4], refs[_SCAN_PARTS + 4:]
        dy_refs, (dzs_ref, dw_ref, db_ref) = refs[:_SCAN_PARTS], refs[_SCAN_PARTS:]
        first_chunk = pl.program_id(0) * (tm // steps)

        @pl.when(pl.program_id(0) == 0)
        def _():
            dw_ref[...] = jnp.zeros_like(dw_ref)
            db_ref[...] = jnp.zeros_like(db_ref)

        gl, gelu_vjp = jax.vjp(jax.nn.gelu, _load_chunks(y_refs, first_chunk, tm // steps, steps, slab))
        glb = gl.astype(BF16)
        a = _dot(glb, w_ref[...]) + b_ref[...]
        _, gate_vjp = jax.vjp(_glu_gate, gl, a, zs_ref[...])
        d_gl, d_a, d_zs = gate_vjp(d_ref[...])
        dab = d_a.astype(BF16)
        d_gl = d_gl + _dot_nt(dab, w_ref[...])
        _store_chunks(dy_refs, first_chunk, gelu_vjp(d_gl)[0], steps, slab)
        dzs_ref[...] = d_zs.astype(BF16)
        dw_ref[...] += _dot_tn(glb, dab)
        db_ref[...] += jnp.sum(d_a, axis=0, keepdims=True)

    *dy_parts, dzs, dw, db = _call(
        body, name="glu_bwd", grid=(rows // tm,),
        in_specs=_whole_parts(rows) + [_rows(tm, 512), _rows(tm, 512), _full((512, 512)), _full((1, 512))],
        out_specs=_whole_parts(rows) + [_rows(tm, 512), _full((512, 512)), _full((1, 512))],
        out_shape=_part_shapes(rows) + [_sds((rows, 512), BF16), _sds((512, 512)), _sds((1, 512))],
        compiler_params=_params(40, ("arbitrary",)),
    )(*y_parts, zs, d_out, w_glu, b_glu)
    return dy_parts, dzs, dw, db


_GROUP_ROWS = Q_PER_KV * BLOCK
_BLOCK_SHIFT = BLOCK.bit_length() - 1


def _attn_bias(j):
    query = _iota((BLOCK, _GROUP_ROWS), 1)
    dist_cur = (query & (BLOCK - 1)) - _iota((BLOCK, _GROUP_ROWS), 0)
    dist_prev = dist_cur + BLOCK
    head = query >> _BLOCK_SHIFT
    slope = jnp.zeros((BLOCK, _GROUP_ROWS), F32)
    for g in range(Q_PER_KV):
        slope = jnp.where(head == g, 2.0 ** (-(j * Q_PER_KV + g + 1)), slope)
    bias_cur = jnp.where(dist_cur >= 0, -slope * dist_cur.astype(F32), -jnp.inf)
    bias_prev = jnp.where(dist_prev < WINDOW, -slope * dist_prev.astype(F32), -jnp.inf)
    return bias_cur, bias_prev


_ATTN_BIAS_SCRATCH = pltpu.VMEM((KV_HEADS, 2, BLOCK, _GROUP_ROWS), F32)


def _fill_attn_bias(bias_ref):
    @pl.when(jnp.logical_and(pl.program_id(0) == 0, pl.program_id(1) == 0))
    def _():
        for j in range(KV_HEADS):
            bias_ref[j, 0], bias_ref[j, 1] = _attn_bias(j)


def _stack_heads(x, j):
    heads = range(j * Q_PER_KV, (j + 1) * Q_PER_KV)
    return jnp.concatenate([x[:, h * HEAD_DIM:(h + 1) * HEAD_DIM] for h in heads], axis=0)


def _head_rows(x, j):
    heads = range(j * Q_PER_KV, (j + 1) * Q_PER_KV)
    return jnp.concatenate([x[h:h + 1, :] for h in heads], axis=1)


def _sink_row(sk_ref, j):
    heads = range(j * Q_PER_KV, (j + 1) * Q_PER_KV)
    return jnp.concatenate([jnp.broadcast_to(sk_ref[0:1, h:h + 1], (1, BLOCK)) for h in heads], axis=1)


def _attn_fwd(q, k, v, za, sinks, n_seq, seq):
    nb = seq // BLOCK
    rows = q.shape[0]

    def body(q_ref, kc_ref, kp_ref, vc_ref, vp_ref, za_ref, sk_ref, o_ref, ao_ref, lse_ref, bias_ref):
        _fill_attn_bias(bias_ref)
        has_prev = pl.program_id(1) > 0
        q_all = q_ref[...]
        for j in range(KV_HEADS):
            js = slice(j * HEAD_DIM, (j + 1) * HEAD_DIM)
            bias_c, bias_p = bias_ref[j, 0], bias_ref[j, 1]
            q4 = _stack_heads(q_all, j)
            sc = _dot_nt(kc_ref[:, js], q4) + bias_c
            sp = _dot_nt(kp_ref[:, js], q4) + jnp.where(has_prev, bias_p, -jnp.inf)
            sink = _sink_row(sk_ref, j)
            m = jnp.maximum(jnp.max(jnp.maximum(sc, sp), axis=0, keepdims=True), sink)
            ec = jnp.exp(sc - m)
            ep = jnp.exp(sp - m)
            den = jnp.sum(ec + ep, axis=0, keepdims=True) + jnp.exp(sink - m)
            inv = 1.0 / den
            o4 = _dot_tn((ec * inv).astype(BF16), vc_ref[:, js]) + _dot_tn((ep * inv).astype(BF16), vp_ref[:, js])
            lse4 = m + jnp.log(den)
            for g in range(Q_PER_KV):
                h = j * Q_PER_KV + g
                o_ref[:, h * HEAD_DIM:(h + 1) * HEAD_DIM] = o4[g * BLOCK:(g + 1) * BLOCK]
                lse_ref[h:h + 1, :] = lse4[:, g * BLOCK:(g + 1) * BLOCK]
        ao_ref[...] = (o_ref[...] * _silu(za_ref[...])).astype(BF16)

    cur = lambda w: pl.BlockSpec((BLOCK, w), lambda b, n: (b * nb + n, 0))
    prev = lambda w: pl.BlockSpec((BLOCK, w), lambda b, n: (b * nb + jnp.maximum(n - 1, 0), 0))
    lse_rows = rows // BLOCK * N_HEADS
    return _call(
        body, name="attn_fwd", grid=(n_seq, nb),
        in_specs=[cur(512), cur(128), prev(128), cur(128), prev(128), cur(512), _full((1, N_HEADS))],
        out_specs=[cur(512), cur(512), pl.BlockSpec((N_HEADS, BLOCK), lambda b, n: (b * nb + n, 0))],
        out_shape=[_sds((rows, 512)), _sds((rows, 512), BF16), _sds((lse_rows, BLOCK))],
        scratch_shapes=[_ATTN_BIAS_SCRATCH], compiler_params=_params(32, ("arbitrary", "arbitrary")),
    )(q, k, k, v, v, za, sinks)


def _attn_bwd(q, k, v, za, o, lse, d_ao, sinks, n_seq, seq):
    nb = seq // BLOCK
    rows = q.shape[0]

    def body(q_ref, kc_ref, kp_ref, vc_ref, vp_ref, za_ref, o_ref, lse_ref, d_ref, sk_ref,
             dq_ref, dk_ref, dv_ref, dza_ref, dsk_ref, bias_ref, dk_carry, dv_carry):
        n = nb - 1 - pl.program_id(1)
        _fill_attn_bias(bias_ref)

        @pl.when(jnp.logical_and(pl.program_id(0) == 0, pl.program_id(1) == 0))
        def _():
            dsk_ref[...] = jnp.zeros_like(dsk_ref)
            dk_carry[...] = jnp.zeros_like(dk_carry)
            dv_carry[...] = jnp.zeros_like(dv_carry)

        has_prev = n > 0
        has_next = n + 1 < nb

        _, gate_vjp = jax.vjp(lambda o_, z_: o_ * _silu(z_), o_ref[...], za_ref[...])
        d_o, d_za = gate_vjp(d_ref[...])
        dza_ref[...] = d_za.astype(BF16)
        q_all = q_ref[...]
        lse_all = lse_ref[...]

        for j in range(KV_HEADS):
            js = slice(j * HEAD_DIM, (j + 1) * HEAD_DIM)
            kc, kp, vc, vp = kc_ref[:, js], kp_ref[:, js], vc_ref[:, js], vp_ref[:, js]
            bias_c, bias_p = bias_ref[j, 0], bias_ref[j, 1]
            q4 = _stack_heads(q_all, j)
            do4b = _stack_heads(d_o, j).astype(BF16)
            lse4 = _head_rows(lse_all, j)
            pc = jnp.exp(_dot_nt(kc, q4) + bias_c - lse4)
            pp = jnp.exp(_dot_nt(kp, q4) + jnp.where(has_prev, bias_p, -jnp.inf) - lse4)
            dpc = _dot_nt(vc, do4b)
            dpp = _dot_nt(vp, do4b)
            delta = jnp.sum(pc * dpc + pp * dpp, axis=0, keepdims=True)
            dsc = (pc * (dpc - delta)).astype(BF16)
            dsp = (pp * (dpp - delta)).astype(BF16)
            dq4 = ((_dot_tn(dsc, kc) + _dot_tn(dsp, kp)) * ATTN_SCALE).astype(BF16)
            sink_loss = jnp.exp(_sink_row(sk_ref, j) - lse4) * delta
            for g in range(Q_PER_KV):
                h = j * Q_PER_KV + g
                dq_ref[:, h * HEAD_DIM:(h + 1) * HEAD_DIM] = dq4[g * BLOCK:(g + 1) * BLOCK]
                dsk_ref[0:1, h:h + 1] -= jnp.sum(sink_loss[:, g * BLOCK:(g + 1) * BLOCK], axis=1, keepdims=True)
            dk = _dot(dsc, q4) + jnp.where(has_next, dk_carry[j], 0.0)
            dv = _dot(pc.astype(BF16), do4b) + jnp.where(has_next, dv_carry[j], 0.0)
            dk_carry[j] = _dot(dsp, q4)
            dv_carry[j] = _dot(pp.astype(BF16), do4b)
            dk_ref[:, js] = dk.astype(BF16)
            dv_ref[:, js] = dv.astype(BF16)

    cur = lambda w: pl.BlockSpec((BLOCK, w), lambda b, s: (b * nb + nb - 1 - s, 0))
    prev = lambda w: pl.BlockSpec((BLOCK, w), lambda b, s: (b * nb + jnp.maximum(nb - 2 - s, 0), 0))
    return _call(
        body, name="attn_bwd", grid=(n_seq, nb),
        in_specs=[cur(512), cur(128), prev(128), cur(128), prev(128), cur(512), cur(512),
                  pl.BlockSpec((N_HEADS, BLOCK), lambda b, s: (b * nb + nb - 1 - s, 0)), cur(512), _full((1, N_HEADS))],
        out_specs=[cur(512), cur(128), cur(128), cur(512), _full((1, N_HEADS))],
        out_shape=[_sds((rows, 512), BF16), _sds((rows, 128), BF16), _sds((rows, 128), BF16),
                   _sds((rows, 512), BF16), _sds((1, N_HEADS))],
        scratch_shapes=[_ATTN_BIAS_SCRATCH, pltpu.VMEM((KV_HEADS, BLOCK, HEAD_DIM), F32),
                        pltpu.VMEM((KV_HEADS, BLOCK, HEAD_DIM), F32)],
        compiler_params=_params(32, ("arbitrary", "arbitrary")),
    )(q, k, k, v, v, za, o, lse, d_ao, sinks)


def _tail(ssm_out, attn_out, x2d, p2d, target, w_out, g2, w_gate, b_gate, w_proj):
    rows = x2d.shape[0]
    tm = 512

    def body(so_ref, ao_ref, x_ref, p_ref, t_ref, wo_ref, g2_ref, wg_ref, bg_ref, wp_ref,
             dh1_ref, dso_ref, dao_ref, dwo_ref, dwg_ref, dwp_ref, dbg_ref, dg2_ref, loss_ref):
        @pl.when(pl.program_id(0) == 0)
        def _():
            for ref in (dwo_ref, dwg_ref, dwp_ref, dbg_ref, dg2_ref, loss_ref):
                ref[...] = jnp.zeros_like(ref)

        cat = jnp.concatenate([so_ref[...], ao_ref[...]], axis=1)
        g2 = g2_ref[...]
        mixed = _dot(cat, wo_ref[...])
        r = lax.rsqrt(jnp.mean(mixed * mixed, axis=-1, keepdims=True) + EPS)
        mr = mixed * r
        h1 = x_ref[...] + mr * g2
        h1b = h1.astype(BF16)
        gate = jax.nn.sigmoid(_dot(h1b, wg_ref[...]) + bg_ref[...])
        pb = p_ref[...].astype(BF16)
        wp_blocks = [slice(j * D_PLE, (j + 1) * D_PLE) for j in range(N_CHIPS)]
        pp = jnp.concatenate([_dot(pb, wp_ref[blk, :]) for blk in wp_blocks], axis=1)
        err = h1 + gate * pp - t_ref[...]
        loss_ref[...] += 0.5 * jnp.sum(jnp.mean(err * err, axis=-1, keepdims=True), axis=0, keepdims=True)

        dh2 = err * (1.0 / D_MODEL)
        d_glin = dh2 * pp * gate * (1.0 - gate)
        d_glin_b = d_glin.astype(BF16)
        dwg_ref[...] += _dot_tn(h1b, d_glin_b)
        dbg_ref[...] += jnp.sum(d_glin, axis=0, keepdims=True)
        d_pp = (dh2 * gate).astype(BF16)
        for blk in wp_blocks:
            dwp_ref[blk, :] += _dot_tn(pb, d_pp[:, blk])
        dh1 = dh2 + _dot_nt(d_glin_b, wg_ref[...])
        dh1_ref[...] = dh1
        dg2_ref[...] += jnp.sum(dh1 * mr, axis=0, keepdims=True)
        a_ = dh1 * g2
        d_mixed = (r * a_ - mr * (r * jnp.mean(a_ * mr, axis=-1, keepdims=True))).astype(BF16)
        dwo_ref[...] += _dot_tn(cat, d_mixed)
        d_cat = _dot_nt(d_mixed, wo_ref[...])
        dso_ref[...] = d_cat[:, 0:512]
        dao_ref[...] = d_cat[:, 512:1024]

    return _call(
        body, name="tail_fwd_bwd", grid=(rows // tm,),
        in_specs=[_rows(tm, 512), _rows(tm, 512), _rows(tm, D_MODEL), _rows(tm, D_PLE), _rows(tm, D_MODEL),
                  _full((D_MODEL, D_MODEL)), _full((1, D_MODEL)), _full((D_MODEL, D_MODEL)), _full((1, D_MODEL)),
                  _full((N_CHIPS * D_PLE, D_PLE))],
        out_specs=[_rows(tm, D_MODEL), _rows(tm, 512), _rows(tm, 512), _full((D_MODEL, D_MODEL)),
                   _full((D_MODEL, D_MODEL)), _full((N_CHIPS * D_PLE, D_PLE)), _full((1, D_MODEL)), _full((1, D_MODEL)),
                   _full((1, 1))],
        out_shape=[_sds((rows, D_MODEL)), _sds((rows, 512)), _sds((rows, 512)), _sds((D_MODEL, D_MODEL)),
                   _sds((D_MODEL, D_MODEL)), _sds((N_CHIPS * D_PLE, D_PLE)), _sds((1, D_MODEL)), _sds((1, D_MODEL)),
                   _sds((1, 1))],
        compiler_params=_params(52, ("arbitrary",)),
    )(ssm_out, attn_out, x2d, p2d, target, w_out, g2, w_gate, b_gate, w_proj)


def _local_step(x, hn, p, target, pre_norm_g, w_in_t, s5_params, s5_operands, ssm_d, w_glu, b_glu, sinks, w_out,
                post_norm_g, w_proj, w_gate, b_gate):
    n_seq, seq, _ = x.shape
    rows = n_seq * seq
    x2d = x.reshape(rows, D_MODEL)
    p2d = p.reshape(rows, D_PLE)
    t2d = target.reshape(rows, D_MODEL)

    l_re, l_im, bt_re, bt_im, cm_re, cm_im = s5_operands

    u_scan, zs, q, k, v, za = _in_proj(hn, w_in_t, n_seq, seq)
    y_scan, h_re, h_im = _s5_scan_fwd(u_scan, bt_re, bt_im, cm_re, cm_im, l_re, l_im, ssm_d, n_seq, seq)
    ssm_out = _glu_fwd(y_scan, zs, w_glu, b_glu, n_seq, seq)
    o, attn_out, lse = _attn_fwd(q, k, v, za, sinks, n_seq, seq)

    dh1, d_so, d_ao, d_w_out, d_w_gate, d_w_proj, d_b_gate, d_g2, loss = _tail(
        ssm_out, attn_out, x2d, p2d, t2d, w_out, post_norm_g, w_gate, b_gate, w_proj)

    dq, dk, dv, dza, d_sinks = _attn_bwd(q, k, v, za, o, lse, d_ao, sinks, n_seq, seq)
    dy_scan, dzs, d_w_glu, d_b_glu = _glu_bwd(y_scan, zs, d_so, w_glu, b_glu, n_seq, seq)
    du_scan, d_bt_re, d_bt_im, d_cm_re, d_cm_im, d_l_re, d_l_im, d_d = _s5_scan_bwd(
        dy_scan, u_scan, h_re, h_im, bt_re, bt_im, cm_re, cm_im, l_re, l_im, ssm_d, n_seq, seq)
    d_lam_re, d_lam_im, d_log_step, d_b_re, d_b_im, d_c_re, d_c_im = _s5_params_bwd(
        s5_params, (d_l_re, d_l_im, d_bt_re, d_bt_im, d_cm_re, d_cm_im))

    d_proj, d_w_in_early, d_w_in_early_b = _in_proj_bwd_early(hn, du_scan, dzs, dq, dk, dv, dza, n_seq, seq)
    grad_x, d_w_in_late, d_g1 = _in_proj_bwd(x2d, dh1, pre_norm_g, w_in_t, d_proj)
    grads = dict(
        pre_norm_g=d_g1, w_in_early=d_w_in_early, w_in_early_bf16=d_w_in_early_b, w_in_late=d_w_in_late, ssm_lam_re=d_lam_re, ssm_lam_im=d_lam_im, ssm_log_step=d_log_step,
        ssm_b_re=d_b_re, ssm_b_im=d_b_im, ssm_c_re=d_c_re, ssm_c_im=d_c_im, ssm_d=d_d, ssm_w_glu=d_w_glu,
        ssm_b_glu=d_b_glu, attn_sinks=d_sinks, w_out=d_w_out, post_norm_g=d_g2, pl_w_proj=d_w_proj,
        pl_w_gate=d_w_gate, pl_b_gate=d_b_gate)
    return grad_x.reshape(x.shape), loss, grads


_BIG = ("w_in", "ssm_w_glu", "w_out", "pl_w_proj", "pl_w_gate")
_BIG_SHARD = {"w_in": (D_IN // N_CHIPS, D_MODEL), "ssm_w_glu": (D_SSM // N_CHIPS, D_SSM),
              "w_out": (D_MODEL // N_CHIPS, D_MODEL), "pl_w_proj": (D_PLE, D_MODEL // N_CHIPS),
              "pl_w_gate": (D_MODEL // N_CHIPS, D_MODEL)}
_SMALL = {"pre_norm_g": (1, D_MODEL), "ssm_lam_re": (SSM_GROUPS, SSM_STATE), "ssm_lam_im": (SSM_GROUPS, SSM_STATE),
          "ssm_log_step": (1, SSM_GROUPS), "ssm_b_re": (D_SSM, SSM_STATE), "ssm_b_im": (D_SSM, SSM_STATE),
          "ssm_c_re": (D_SSM, SSM_STATE), "ssm_c_im": (D_SSM, SSM_STATE), "ssm_d": (1, D_SSM), "ssm_b_glu": (1, D_SSM),
          "attn_sinks": (1, N_HEADS), "post_norm_g": (1, D_MODEL), "pl_b_gate": (1, D_MODEL)}
_VEC_ROWS = ("pre_norm_g", "post_norm_g", "pl_b_gate", "ssm_d", "ssm_b_glu", "attn_sinks", "ssm_log_step", "loss")
_SMALL_GROUPS = (
    ("vec", (8, D_MODEL), tuple((name, r) for r, name in enumerate(_VEC_ROWS))),
    ("lam", (2 * SSM_GROUPS, SSM_STATE), (("ssm_lam_re", 0), ("ssm_lam_im", SSM_GROUPS))),
)
_SMALL_EARLY = ("ssm_b_re", "ssm_b_im", "ssm_c_re", "ssm_c_im")
_SMALL_ORDER = tuple(name for _, _, members in _SMALL_GROUPS for name, _ in members) + _SMALL_EARLY
_WEIGHT_ORDER = ("pre_norm_g", "w_in", "ssm_lam_re", "ssm_lam_im", "ssm_log_step", "ssm_b_re", "ssm_b_im", "ssm_c_re",
                 "ssm_c_im", "ssm_d", "ssm_w_glu", "ssm_b_glu", "attn_sinks", "w_out", "post_norm_g", "pl_w_proj",
                 "pl_w_gate", "pl_b_gate")


def _small_shape(name):
    return (1, 1) if name == "loss" else _SMALL[name]


def _to_kernel_form(name, a):
    a = a[0]
    if name == "w_in":
        return a.T
    if name in ("ssm_b_re", "ssm_b_im"):
        a = a.transpose(0, 2, 1)
    return a.reshape(_SMALL[name]) if name in _SMALL else a


def _from_kernel_form(name, a, shape):
    if name == "w_in":
        a = a.T
    if name in ("ssm_b_re", "ssm_b_im"):
        a = a.reshape(SSM_GROUPS, SSM_GROUP_CH, SSM_STATE).transpose(0, 2, 1)
    return a.reshape(shape)


def _mesh_place():
    x, y, c = lax.axis_index("x"), lax.axis_index("y"), lax.axis_index("c")
    other_chips = ((1 - x, y), (x, 1 - y), (1 - x, 1 - y))
    return x, y, c, other_chips


def _gather_copies(s_refs, g_refs, send_sems, recv_sems, local_sems):
    x, y, c, other_chips = _mesh_place()
    started = []
    for i, (s_ref, g_ref) in enumerate(zip(s_refs, g_refs)):
        rows = s_ref.shape[0]
        half = rows // 2

        def block(chip, g_ref=g_ref, rows=rows, half=half):
            return g_ref.at[pl.ds((2 * chip[0] + chip[1]) * rows + c * half, half), :]

        def copy(k, chip, to, src=None, i=i, block=block):
            return pltpu.make_async_remote_copy(
                src_ref=block(chip) if src is None else src, dst_ref=block(chip), send_sem=send_sems.at[6 * i + k],
                recv_sem=recv_sems.at[6 * i + k], device_id=to, device_id_type=MESH)

        own = pltpu.make_async_copy(s_ref, g_ref.at[pl.ds((2 * x + y) * rows, rows), :], local_sems.at[i])
        own.start()
        first = [copy(k, (x, y), (*chip, c), src=s_ref.at[pl.ds(c * half, half), :])
                 for k, chip in enumerate(other_chips)]
        for cp in first:
            cp.start()
        passed = [copy(3 + k, chip, (x, y, 1 - c)) for k, chip in enumerate(other_chips)]
        started.append((own, first, passed))
    for own, first, passed in started:
        for k in range(3):
            first[k].wait_recv()
            passed[k].start()
    for own, first, passed in started:
        for k in range(3):
            passed[k].wait_recv()
        for cp in first + passed:
            cp.wait_send()
        own.wait()


def _gather_semaphores(n_t):
    return [pltpu.SemaphoreType.DMA((6 * n_t,)), pltpu.SemaphoreType.DMA((6 * n_t,)), pltpu.SemaphoreType.DMA((n_t,))]


def _gather_weights_beside(shards, name, collective_id):
    n_t = len(shards)
    hbm = pltpu.MemorySpace.HBM
    s_refs = [jax.new_ref(s, memory_space=hbm) for s in shards]
    g_refs = [jax.empty_ref(jax.ShapeDtypeStruct((N_CHIPS * s.shape[0], s.shape[1]), s.dtype), memory_space=hbm)
              for s in shards]

    def launch(send_sems, recv_sems, local_sems):
        x, y, c, other_chips = _mesh_place()
        peers = [(*chip, c) for chip in other_chips] + [(x, y, 1 - c)]
        barrier = pltpu.get_barrier_semaphore()
        for peer in peers:
            pl.semaphore_signal(barrier, inc=1, device_id=peer, device_id_type=MESH)
        pl.semaphore_wait(barrier, len(peers))
        _gather_copies(s_refs, g_refs, send_sems, recv_sems, local_sems)

    pl.kernel(launch, mesh=plsc.ScalarSubcoreMesh(axis_name="sequencer", num_cores=1), name=name,
              scratch_types=_gather_semaphores(n_t), compiler_params=pltpu.CompilerParams(collective_id=collective_id))()
    return [g[...] for g in g_refs]


_RELATIONS = tuple(((r >> 2) & 1, (r >> 1) & 1, r & 1) for r in range(1, 8))


def _related(place, relation):
    return tuple(1 - a if flip else a for a, flip in zip(place, relation))


def _scatter_beside(mats, name, collective_id):
    hbm = pltpu.MemorySpace.HBM
    src_refs = [jax.new_ref(a, memory_space=hbm) for a in mats]
    land_refs = [jax.empty_ref(jax.ShapeDtypeStruct((7, a.shape[0] // 8, a.shape[1]), a.dtype), memory_space=hbm)
                 for a in mats]

    def launch(send_sems, recv_sems):
        me = (lax.axis_index("x"), lax.axis_index("y"), lax.axis_index("c"))
        peers = [_related(me, rel) for rel in _RELATIONS]
        barrier = pltpu.get_barrier_semaphore()
        for peer in peers:
            pl.semaphore_signal(barrier, inc=1, device_id=peer, device_id_type=MESH)
        pl.semaphore_wait(barrier, len(peers))
        copies = []
        for i, (src, land) in enumerate(zip(src_refs, land_refs)):
            hr = land.shape[1]
            for k, (tx, ty, tc) in enumerate(peers):
                rows = pl.ds((2 * tx + ty) * 2 * hr + tc * hr, hr)
                copies.append(pltpu.make_async_remote_copy(
                    src_ref=src.at[rows, :], dst_ref=land.at[k], send_sem=send_sems.at[7 * i + k],
                    recv_sem=recv_sems.at[7 * i + k], device_id=(tx, ty, tc), device_id_type=MESH))
                copies[-1].start()
        for cp in copies:
            cp.wait()

    n_sems = 7 * len(mats)
    pl.kernel(launch, mesh=plsc.ScalarSubcoreMesh(axis_name="sequencer", num_cores=1), name=name,
              scratch_types=[pltpu.SemaphoreType.DMA((n_sems,)), pltpu.SemaphoreType.DMA((n_sems,))],
              compiler_params=pltpu.CompilerParams(collective_id=collective_id))()
    return [ref[...] for ref in land_refs]


def _broadcast_beside(arrays):
    hbm = pltpu.MemorySpace.HBM
    src_refs = [jax.new_ref(a, memory_space=hbm) for a in arrays]
    land_refs = [jax.empty_ref(jax.ShapeDtypeStruct((len(_RELATIONS),) + a.shape, a.dtype), memory_space=hbm)
                 for a in arrays]

    def launch(send_sems, recv_sems):
        me = (lax.axis_index("x"), lax.axis_index("y"), lax.axis_index("c"))
        peers = [_related(me, rel) for rel in _RELATIONS]
        barrier = pltpu.get_barrier_semaphore()
        for peer in peers:
            pl.semaphore_signal(barrier, inc=1, device_id=peer, device_id_type=MESH)
        pl.semaphore_wait(barrier, len(peers))
        copies = []
        for i, (src, land) in enumerate(zip(src_refs, land_refs)):
            for k, peer in enumerate(peers):
                copies.append(pltpu.make_async_remote_copy(
                    src_ref=src, dst_ref=land.at[k], send_sem=send_sems.at[7 * i + k],
                    recv_sem=recv_sems.at[7 * i + k], device_id=peer, device_id_type=MESH))
                copies[-1].start()
        for cp in copies:
            cp.wait()

    n_sems = 7 * len(arrays)
    pl.kernel(launch, mesh=plsc.ScalarSubcoreMesh(axis_name="sequencer", num_cores=1), name="broadcast_beside",
              scratch_types=[pltpu.SemaphoreType.DMA((n_sems,)), pltpu.SemaphoreType.DMA((n_sems,))],
              compiler_params=pltpu.CompilerParams(collective_id=3))()
    return [ref[...] for ref in land_refs]


def _exchange_grads(big, small, landed, landed_small):
    n_t = len(big)
    n_g = len(_SMALL_GROUPS)
    names = _SMALL_ORDER
    halves = [(b.shape[0] // N_CHIPS // 2, b.shape[1]) for b in big]
    early = sorted(landed)
    late = [i for i in range(n_t) if i not in landed]
    n_sems = 4 * n_g + 7 * len(late) + n_t
    small_sem0, block_sem0 = n_t, n_t + len(names)
    early_sem0 = block_sem0 + N_CHIPS * len(late)
    landed_sem0 = early_sem0 + 2 * len(early)
    early_small = [n for n in names if n in landed_small]

    def body(*refs):
        pos = 0

        def take(n):
            nonlocal pos
            pos += n
            return refs[pos - n:pos]

        big_refs, small_refs = take(n_t), dict(zip(names, take(len(names))))
        land_refs = dict(zip(early, take(len(early))))
        land_small_refs = dict(zip(early_small, take(len(early_small))))
        out_refs, small_out_refs = take(n_t), dict(zip(names, take(len(names))))
        per_late = lambda: dict(zip(late, take(len(late))))
        ga, gb, pme, send_b, recv_b = per_late(), per_late(), take(n_t), per_late(), per_late()
        own_e, land_e = dict(zip(early, take(len(early)))), dict(zip(early, take(len(early))))
        land_s = dict(zip(early_small, take(len(early_small))))
        s_own, s_sib, s_chips, s_pair = take(n_g), take(n_g), take(n_g), take(n_g)
        stage = dict(zip(names, take(len(names))))
        send_sems, recv_sems, local_sems = take(3)
        x, y, c, other_chips = _mesh_place()
        me = 2 * x + y
        sibling = (x, y, 1 - c)
        sem_at = iter(range(n_sems))

        def remote(src, dst, to):
            k = next(sem_at)
            return pltpu.make_async_remote_copy(src_ref=src, dst_ref=dst, send_sem=send_sems.at[k],
                                                recv_sem=recv_sems.at[k], device_id=to, device_id_type=MESH)

        loads = [pltpu.make_async_copy(small_refs[name], stage[name], local_sems.at[small_sem0 + a])
                 for a, name in enumerate(names)]
        landed_loads = [pltpu.make_async_copy(land_small_refs[name], land_s[name], local_sems.at[landed_sem0 + a])
                        for a, name in enumerate(early_small)]
        for cp in loads + landed_loads:
            cp.start()
        for cp in loads:
            cp.wait()
        small_swaps = []
        for gi, (_, _, members) in enumerate(_SMALL_GROUPS):
            s_own[gi][...] = jnp.zeros_like(s_own[gi])
            for name, r0 in members:
                r, n = _small_shape(name)
                s_own[gi][r0:r0 + r, 0:n] = stage[name][...]
            small_swaps.append(remote(s_own[gi], s_sib[gi], sibling))
            small_swaps[gi].start()
        order = sorted(late, key=lambda i: halves[i][0] * halves[i][1])
        own_loads, big_swaps = {}, {}
        for i in order:
            hr = halves[i][0]
            own_loads[i], big_swaps[i] = [], []
            for j in range(N_CHIPS):
                mine = big_refs[i].at[pl.ds(j * 2 * hr + c * hr, hr), :]
                theirs = big_refs[i].at[pl.ds(j * 2 * hr + (1 - c) * hr, hr), :]
                sem = local_sems.at[block_sem0 + N_CHIPS * late.index(i) + j]
                own_loads[i].append(pltpu.make_async_copy(mine, ga[i].at[j], sem))
                own_loads[i][j].start()
                big_swaps[i].append(remote(theirs, gb[i].at[j], sibling))
                big_swaps[i][j].start()
        early_loads = {}
        for e, i in enumerate(early):
            hr = halves[i][0]
            mine = big_refs[i].at[pl.ds(me * 2 * hr + c * hr, hr), :]
            early_loads[i] = [pltpu.make_async_copy(mine, own_e[i], local_sems.at[early_sem0 + 2 * e]),
                              pltpu.make_async_copy(land_refs[i], land_e[i], local_sems.at[early_sem0 + 2 * e + 1])]
            for cp in early_loads[i]:
                cp.start()
        small_sends = []
        for gi in range(n_g):
            small_swaps[gi].wait_recv()
            s_pair[gi][...] = s_own[gi][...] + s_sib[gi][...]
            small_sends.append([remote(s_pair[gi], s_chips[gi].at[k], (*chip, c)) for k, chip in enumerate(other_chips)])
            for cp in small_sends[gi]:
                cp.start()

        def pair_sum(i, j):
            return ga[i][j] + gb[i][j]

        big_sends = {}
        for i in order:
            for j in range(N_CHIPS):
                own_loads[i][j].wait()
                big_swaps[i][j].wait_recv()
            big_sends[i] = []
            for k, chip in enumerate(other_chips):
                send_b[i][k] = pair_sum(i, 2 * chip[0] + chip[1]).astype(BF16)
                big_sends[i].append(remote(send_b[i].at[k], recv_b[i].at[k], (*chip, c)))
                big_sends[i][k].start()
        last_swaps, keeps = {}, {}
        for i in early + order:
            hr = halves[i][0]
            if i in landed:
                for cp in early_loads[i]:
                    cp.wait()
                total = own_e[i][...]
                for k in range(len(_RELATIONS)):
                    total = total + land_e[i][k].astype(F32)
                pme[i][...] = total
            else:
                for k in range(3):
                    big_sends[i][k].wait_recv()
                pme[i][...] = ((pair_sum(i, me) + recv_b[i][0].astype(F32)) + recv_b[i][1].astype(F32)) + recv_b[i][2].astype(F32)
            mine = out_refs[i].at[pl.ds(c * hr, hr), :]
            keeps[i] = pltpu.make_async_copy(pme[i], mine, local_sems.at[i])
            keeps[i].start()
            last_swaps[i] = remote(pme[i], mine, sibling)
            last_swaps[i].start()

        for gi, (_, _, members) in enumerate(_SMALL_GROUPS):
            for k in range(3):
                small_sends[gi][k].wait_recv()
            total = None
            for j in range(N_CHIPS):
                rel = jnp.bitwise_xor(j, me)
                term = jnp.where(rel == 0, s_pair[gi][...], jnp.where(
                    rel == 2, s_chips[gi][0], jnp.where(rel == 1, s_chips[gi][1], s_chips[gi][2])))
                total = term if total is None else total + term
            s_sib[gi][...] = total
            for name, r0 in members:
                r, n = _small_shape(name)
                stage[name][...] = s_sib[gi][r0:r0 + r, 0:n]
        my_index = 4 * x + 2 * y + c
        for cp in landed_loads:
            cp.wait()
        for name in early_small:
            total = None
            for d in range(2 * N_CHIPS):
                rel = jnp.bitwise_xor(d, my_index)
                term = stage[name][...]
                for k in range(len(_RELATIONS)):
                    term = jnp.where(rel == k + 1, land_s[name][k], term)
                total = term if total is None else total + term
            stage[name][...] = total
        stores = [pltpu.make_async_copy(stage[name], small_out_refs[name], local_sems.at[small_sem0 + a])
                  for a, name in enumerate(names)]
        for cp in stores:
            cp.start()

        for i in range(n_t):
            last_swaps[i].wait_recv()
            keeps[i].wait()
        for cp in stores:
            cp.wait()
        groups = list(big_swaps.values()) + small_sends + list(big_sends.values())
        for cp in small_swaps + [cp for group in groups for cp in group] + list(last_swaps.values()):
            cp.wait_send()

    any_spec = pl.BlockSpec(memory_space=pl.ANY)
    small_shapes = [_sds(_small_shape(n)) for n in names]
    group_shapes = [shape for _, shape, _ in _SMALL_GROUPS]
    vmem = lambda which, dtype, lead=(): [pltpu.VMEM(lead + halves[i], dtype) for i in which]
    outs = _call(
        body, name="exchange_grads",
        in_specs=[any_spec] * (n_t + len(names) + len(early) + len(early_small)),
        out_specs=[any_spec] * (n_t + len(names)),
        out_shape=[_sds((b.shape[0] // N_CHIPS, b.shape[1])) for b in big] + small_shapes,
        scratch_shapes=(vmem(late, F32, (N_CHIPS,)) + vmem(late, F32, (N_CHIPS,)) + vmem(range(n_t), F32)
                        + vmem(late, BF16, (3,)) + vmem(late, BF16, (3,))
                        + vmem(early, F32)
                        + [pltpu.VMEM((len(_RELATIONS),) + halves[i], landed[i].dtype) for i in early]
                        + [pltpu.VMEM((len(_RELATIONS),) + _small_shape(n), F32) for n in early_small]
                        + [pltpu.VMEM(s, F32) for s in group_shapes] * 2 + [pltpu.VMEM((3,) + s, F32) for s in group_shapes]
                        + [pltpu.VMEM(s, F32) for s in group_shapes]
                        + [pltpu.VMEM(_small_shape(n), F32) for n in names]
                        + [pltpu.SemaphoreType.DMA((n_sems,)), pltpu.SemaphoreType.DMA((n_sems,)),
                           pltpu.SemaphoreType.DMA((landed_sem0 + len(early_small),))]),
        compiler_params=_params(48),
    )(*big, *[small[n] for n in names], *[landed[i] for i in early], *[landed_small[n] for n in early_small])
    return list(outs[:n_t]), dict(zip(names, outs[n_t:n_t + len(names)]))


def _adamw_update(w, g, m, v):
    m = ADAM_B1 * m + (1.0 - ADAM_B1) * g
    v = ADAM_B2 * v + (1.0 - ADAM_B2) * (g * g)
    m_hat = m / (1.0 - ADAM_B1 ** ADAM_STEP)
    v_hat = v / (1.0 - ADAM_B2 ** ADAM_STEP)
    return -ADAM_LR * (m_hat / (jnp.sqrt(v_hat) + ADAM_EPS) + ADAM_WD * w), m, v


def _adamw(w, g, m, v, grid, name):
    n_t = len(w)

    def body(*refs):
        ins, outs = refs[:4 * n_t], refs[4 * n_t:]
        for i in range(n_t):
            w_, g_, m_, v_ = [ins[a * n_t + i][...] for a in range(4)]
            vals = (g_,) + _adamw_update(w_, g_, m_, v_)
            for a in range(4):
                outs[a * n_t + i][...] = vals[a]

    specs = [pl.BlockSpec((a.shape[0] // grid, a.shape[1]), lambda i: (i, 0)) for a in w]
    shapes = [_sds(a.shape) for a in w]
    outs = _call(
        body, name=name, grid=(grid,), in_specs=specs * 4, out_specs=specs * 4, out_shape=shapes * 4,
        compiler_params=_params(40, ("arbitrary",)),
    )(*w, *g, *m, *v)
    return [outs[a * n_t:(a + 1) * n_t] for a in range(4)]


def kernel(x, p, pre_norm_g, w_in, ssm_lam_re, ssm_lam_im, ssm_log_step, ssm_b_re, ssm_b_im, ssm_c_re, ssm_c_im, ssm_d, ssm_w_glu, ssm_b_glu, attn_sinks, w_out, post_norm_g, pl_w_proj, pl_w_gate, pl_b_gate, loss_target, m_pre_norm_g, m_w_in, m_ssm_lam_re, m_ssm_lam_im, m_ssm_log_step, m_ssm_b_re, m_ssm_b_im, m_ssm_c_re, m_ssm_c_im, m_ssm_d, m_ssm_w_glu, m_ssm_b_glu, m_attn_sinks, m_w_out, m_post_norm_g, m_pl_w_proj, m_pl_w_gate, m_pl_b_gate, v_pre_norm_g, v_w_in, v_ssm_lam_re, v_ssm_lam_im, v_ssm_log_step, v_ssm_b_re, v_ssm_b_im, v_ssm_c_re, v_ssm_c_im, v_ssm_d, v_ssm_w_glu, v_ssm_b_glu, v_attn_sinks, v_w_out, v_post_norm_g, v_pl_w_proj, v_pl_w_gate, v_pl_b_gate):
    weights = dict(pre_norm_g=pre_norm_g, w_in=w_in, ssm_lam_re=ssm_lam_re, ssm_lam_im=ssm_lam_im,
                   ssm_log_step=ssm_log_step, ssm_b_re=ssm_b_re, ssm_b_im=ssm_b_im, ssm_c_re=ssm_c_re,
                   ssm_c_im=ssm_c_im, ssm_d=ssm_d, ssm_w_glu=ssm_w_glu, ssm_b_glu=ssm_b_glu, attn_sinks=attn_sinks,
                   w_out=w_out, post_norm_g=post_norm_g, pl_w_proj=pl_w_proj, pl_w_gate=pl_w_gate, pl_b_gate=pl_b_gate)
    m_in = dict(pre_norm_g=m_pre_norm_g, w_in=m_w_in, ssm_lam_re=m_ssm_lam_re, ssm_lam_im=m_ssm_lam_im,
                ssm_log_step=m_ssm_log_step, ssm_b_re=m_ssm_b_re, ssm_b_im=m_ssm_b_im, ssm_c_re=m_ssm_c_re,
                ssm_c_im=m_ssm_c_im, ssm_d=m_ssm_d, ssm_w_glu=m_ssm_w_glu, ssm_b_glu=m_ssm_b_glu,
                attn_sinks=m_attn_sinks, w_out=m_w_out, post_norm_g=m_post_norm_g, pl_w_proj=m_pl_w_proj,
                pl_w_gate=m_pl_w_gate, pl_b_gate=m_pl_b_gate)
    v_in = dict(pre_norm_g=v_pre_norm_g, w_in=v_w_in, ssm_lam_re=v_ssm_lam_re, ssm_lam_im=v_ssm_lam_im,
                ssm_log_step=v_ssm_log_step, ssm_b_re=v_ssm_b_re, ssm_b_im=v_ssm_b_im, ssm_c_re=v_ssm_c_re,
                ssm_c_im=v_ssm_c_im, ssm_d=v_ssm_d, ssm_w_glu=v_ssm_w_glu, ssm_b_glu=v_ssm_b_glu,
                attn_sinks=v_attn_sinks, w_out=v_w_out, post_norm_g=v_post_norm_g, pl_w_proj=v_pl_w_proj,
                pl_w_gate=v_pl_w_gate, pl_b_gate=v_pl_b_gate)

    def two_d(tree):
        return {k: _to_kernel_form(k, a) for k, a in tree.items()}

    w2, m2, v2 = two_d(weights), two_d(m_in), two_d(v_in)

    (w_in_full,) = _gather_weights_beside([w2["w_in"].astype(BF16)], "gather_w_in_beside", 4)
    s5_params = tuple(w2[n] for n in ("ssm_lam_re", "ssm_lam_im", "ssm_log_step", "ssm_b_re", "ssm_b_im", "ssm_c_re",
                                      "ssm_c_im"))
    s5_operands = _s5_params_fwd(*s5_params)
    hn = _pre_norm(x.reshape(-1, D_MODEL), w2["pre_norm_g"])
    behind = s5_operands[0][0, 0] * 0.0 + hn[0, 0].astype(F32) * 0.0
    rest = _gather_weights_beside([(w2[n] + behind).astype(BF16) for n in _BIG[1:]], "gather_weights_beside", 1)
    full = dict(zip(_BIG, [w_in_full] + rest))
    grad_x, loss, grads = _local_step(
        x, hn, p, loss_target, w2["pre_norm_g"], full["w_in"], s5_params, s5_operands, w2["ssm_d"], full["ssm_w_glu"], w2["ssm_b_glu"],
        w2["attn_sinks"], full["w_out"], w2["post_norm_g"], full["pl_w_proj"], full["pl_w_gate"], w2["pl_b_gate"])

    mats = ("w_in_early", "w_in_late") + _BIG[1:]
    sent_early = ("w_out", "pl_w_gate", "pl_w_proj")
    landed = dict(zip([mats.index(n) for n in sent_early],
                      _scatter_beside([grads[n] for n in sent_early], "scatter_beside", 2)))
    after_scatter = landed[mats.index(sent_early[-1])][0, 0, 0] * 0.0
    late_operands = [grads[_SMALL_EARLY[0]] + after_scatter] + [grads[n] for n in _SMALL_EARLY[1:]]
    landed_small = dict(zip(_SMALL_EARLY, _broadcast_beside(late_operands)))
    landed[0] = _scatter_beside([grads["w_in_early_bf16"]], "scatter_w_in_beside", 5)[0]
    g_big, g_small = _exchange_grads([grads[n] for n in mats], {**{n: grads[n] for n in _SMALL}, "loss": loss}, landed,
                                     landed_small)
    g_big = dict(zip(mats, g_big))
    g_big["w_in"] = jnp.concatenate([g_big.pop("w_in_early"), g_big.pop("w_in_late")], axis=1)
    total_loss = g_small.pop("loss")

    big_out = _adamw([w2[n] for n in _BIG], [g_big[n] for n in _BIG], [m2[n] for n in _BIG], [v2[n] for n in _BIG],
                     8, "adamw_matrices")
    small_names = tuple(_SMALL)
    small_out = _adamw([w2[n] for n in small_names], [g_small[n] for n in small_names], [m2[n] for n in small_names],
                       [v2[n] for n in small_names], 1, "adamw_small")

    results = [{**dict(zip(_BIG, big_part)), **dict(zip(small_names, small_part))}
               for big_part, small_part in zip(big_out, small_out)]
    flat = [_from_kernel_form(name, r[name], weights[name].shape) for r in results for name in _WEIGHT_ORDER]
    return (total_loss.reshape(()), grad_x, *flat)
```

```python
import math

import jax
import jax.numpy as jnp
from jax import lax
from jax.experimental import pallas as pl
from jax.experimental.pallas import tpu as pltpu
from jax.experimental.pallas import tpu_sc as plsc

F32 = jnp.float32
BF16 = jnp.bfloat16

D_MODEL = 1024
D_SSM = 512
D_ATTN = 512
SSM_GROUPS = 32
SSM_GROUP_CH = 16
SSM_STATE = 64
SSM_LANES = SSM_GROUPS * SSM_STATE
HEAD_DIM = 64
N_HEADS = 8
KV_HEADS = 2
Q_PER_KV = 4
WINDOW = 128
BLOCK = 128
D_PLE = 256
D_IN = 2304
EPS = 1e-6
ATTN_SCALE = 1.0 / math.sqrt(HEAD_DIM)

ADAM_LR = 0.001
ADAM_B1 = 0.9
ADAM_B2 = 0.999
ADAM_EPS = 1e-08
ADAM_WD = 0.01
ADAM_STEP = 10

N_CHIPS = 4
LANES = 128
SCAN_CHUNKS = 8
SCAN_TILE_STEPS = 32
SCAN_LANE_CHUNK = 512
MIB = 2 ** 20
MESH = pl.DeviceIdType.MESH


def _dot(a, b):
    return jnp.dot(a, b, preferred_element_type=F32)


def _dot_nt(a, b):
    return lax.dot_general(a, b, (((1,), (1,)), ((), ())), preferred_element_type=F32)


def _dot_tn(a, b):
    return lax.dot_general(a, b, (((0,), (0,)), ((), ())), preferred_element_type=F32)


def _params(vmem_mib, semantics=None):
    kw = dict(vmem_limit_bytes=vmem_mib * MIB)
    if semantics is not None:
        kw["dimension_semantics"] = semantics
    return pltpu.CompilerParams(**kw)


def _full(shape):
    nd = len(shape)
    return pl.BlockSpec(shape, lambda *_: (0,) * nd, pipeline_mode=pl.Buffered(1))


def _rows(tm, width):
    return pl.BlockSpec((tm, width), lambda i: (i, 0))


def _sds(shape, dtype=F32):
    return pltpu.HBM(shape, dtype)


def _call(body, **kw):
    fn = pl.pallas_call(body, **kw)
    return lambda *args: fn(*[pltpu.with_memory_space_constraint(a, pltpu.HBM) for a in args])


def _silu(z):
    return z * jax.nn.sigmoid(z)


def _pre_norm(x2d, g1):
    rows = x2d.shape[0]
    tm = 512

    def body(x_ref, g_ref, hn_ref):
        x = x_ref[...]
        r = lax.rsqrt(jnp.mean(x * x, axis=-1, keepdims=True) + EPS)
        hn_ref[...] = (x * r * g_ref[...]).astype(BF16)

    return _call(
        body, name="pre_norm", grid=(rows // tm,), in_specs=[_rows(tm, D_MODEL), _full((1, D_MODEL))],
        out_specs=_rows(tm, D_MODEL), out_shape=_sds((rows, D_MODEL), BF16), compiler_params=_params(32, ("arbitrary",)),
    )(x2d, g1)


def _in_proj(hn, w_in_t, n_seq, seq):
    rows = hn.shape[0]
    tm = 1024
    slab, steps, _, _ = _scan_geometry(n_seq, seq)

    def body(hn_ref, w_ref, *out_refs):
        u_parts, (zs_ref, q_ref, k_ref, v_ref, za_ref) = out_refs[:_SCAN_PARTS], out_refs[_SCAN_PARTS:]
        whole = _dot_nt(hn_ref[...], w_ref[...])

        def proj(a, b):
            return whole[:, a:b]

        _store_chunks(u_parts, pl.program_id(0) * (tm // steps), proj(0, 512), steps, slab)
        zs_ref[...] = proj(512, 1024)
        q_ref[...] = (proj(1024, 1536) * ATTN_SCALE).astype(BF16)
        k_ref[...] = proj(1536, 1664).astype(BF16)
        v_ref[...] = proj(1664, 1792).astype(BF16)
        za_ref[...] = proj(1792, 2304)

    *u_parts, zs, q, k, v, za = _call(
        body, name="in_proj", grid=(rows // tm,),
        in_specs=[_rows(tm, D_MODEL), _full((D_IN, D_MODEL))],
        out_specs=_whole_parts(rows) + [_rows(tm, 512), _rows(tm, 512), _rows(tm, 128), _rows(tm, 128), _rows(tm, 512)],
        out_shape=_part_shapes(rows) + [_sds((rows, 512)), _sds((rows, 512), BF16), _sds((rows, 128), BF16),
                                        _sds((rows, 128), BF16), _sds((rows, 512))],
        compiler_params=_params(48, ("arbitrary",)),
    )(hn, w_in_t)
    return u_parts, zs, q, k, v, za


_EARLY_COLS = D_MODEL // 2


def _in_proj_bwd_early(hn, du_parts, dzs, dq, dk, dv, dza, runs_after, n_seq, seq):
    rows = hn.shape[0]
    tm = 512
    slab, steps, _, _ = _scan_geometry(n_seq, seq)

    def body(hn_ref, *refs):
        du_parts, (dzs_ref, dq_ref, dk_ref, dv_ref, dza_ref, _, dproj_ref, dw_ref, dwb_ref) = refs[:_SCAN_PARTS], refs[_SCAN_PARTS:]
        i = pl.program_id(0)

        @pl.when(i == 0)
        def _():
            dw_ref[...] = jnp.zeros_like(dw_ref)

        du = _load_chunks(du_parts, i * (tm // steps), tm // steps, steps, slab)
        d_proj = jnp.concatenate([du.astype(BF16), dzs_ref[...], dq_ref[...], dk_ref[...], dv_ref[...], dza_ref[...]],
                                 axis=1)
        dproj_ref[...] = d_proj
        dw_ref[...] += _dot_tn(d_proj, hn_ref[...])

        @pl.when(i == rows // tm - 1)
        def _():
            dwb_ref[...] = dw_ref[...].astype(BF16)

    return _call(
        body, name="in_proj_bwd_early", grid=(rows // tm,),
        in_specs=[_rows(tm, _EARLY_COLS)] + _whole_parts(rows)
        + [_rows(tm, 512), _rows(tm, 512), _rows(tm, 128), _rows(tm, 128), _rows(tm, 512),
           pl.BlockSpec(memory_space=pl.ANY)],
        out_specs=[_rows(tm, D_IN), _full((D_IN, _EARLY_COLS)), _full((D_IN, _EARLY_COLS))],
        out_shape=[_sds((rows, D_IN), BF16), _sds((D_IN, _EARLY_COLS)), _sds((D_IN, _EARLY_COLS), BF16)],
        compiler_params=_params(48, ("arbitrary",)),
    )(hn, *du_parts, dzs, dq, dk, dv, dza, runs_after)


def _in_proj_bwd(x2d, dh1, g1, w_in_t, d_proj):
    rows = x2d.shape[0]
    tm = 512

    def body(x_ref, dh1_ref, g_ref, w_ref, dproj_ref, gx_ref, dw_ref, dg_ref):
        @pl.when(pl.program_id(0) == 0)
        def _():
            dw_ref[...] = jnp.zeros_like(dw_ref)
            dg_ref[...] = jnp.zeros_like(dg_ref)

        x = x_ref[...]
        g = g_ref[...]
        r = lax.rsqrt(jnp.mean(x * x, axis=-1, keepdims=True) + EPS)
        xr = x * r
        hn = (xr[:, _EARLY_COLS:] * g[:, _EARLY_COLS:]).astype(BF16)
        d_proj = dproj_ref[...]
        dhn = _dot(d_proj, w_ref[...])
        dw_ref[...] += _dot_tn(d_proj, hn)
        dg_ref[...] += jnp.sum(dhn * xr, axis=0, keepdims=True)
        a_ = dhn * g
        gx_ref[...] = dh1_ref[...] + r * a_ - xr * (r * jnp.mean(a_ * xr, axis=-1, keepdims=True))

    late_cols = D_MODEL - _EARLY_COLS
    return _call(
        body, name="in_proj_bwd", grid=(rows // tm,),
        in_specs=[_rows(tm, D_MODEL), _rows(tm, D_MODEL), _full((1, D_MODEL)), _full((D_IN, D_MODEL)), _rows(tm, D_IN)],
        out_specs=[_rows(tm, D_MODEL), _full((D_IN, late_cols)), _full((1, D_MODEL))],
        out_shape=[_sds((rows, D_MODEL)), _sds((D_IN, late_cols)), _sds((1, D_MODEL))],
        compiler_params=_params(52, ("arbitrary",)),
    )(x2d, dh1, g1, w_in_t, d_proj)


def _iota(shape, axis):
    return lax.broadcasted_iota(jnp.int32, shape, axis)


def _sum_of_thirds(f, a):
    hi = a.astype(BF16)
    rest = a - hi.astype(F32)
    mid = rest.astype(BF16)
    low = (rest - mid.astype(F32)).astype(BF16)
    return (f(hi) + f(mid)) + f(low)


@jax.custom_vjp
def _pick_rows(e, a):
    return _sum_of_thirds(lambda part: _dot(e, part), a)


def _pick_rows_fwd(e, a):
    return _pick_rows(e, a), e


def _pick_rows_bwd(e, ct):
    return jnp.zeros_like(e), _sum_of_thirds(lambda part: _dot_tn(e, part), ct)


_pick_rows.defvjp(_pick_rows_fwd, _pick_rows_bwd)


@jax.custom_vjp
def _pick_cols(a, e):
    return _sum_of_thirds(lambda part: _dot(part, e), a)


def _pick_cols_fwd(a, e):
    return _pick_cols(a, e), e


def _pick_cols_bwd(e, ct):
    return _sum_of_thirds(lambda part: _dot_nt(part, e), ct), jnp.zeros_like(e)


_pick_cols.defvjp(_pick_cols_fwd, _pick_cols_bwd)


_HALF_GROUPS = SSM_GROUPS // 2
_N_SHIFT = SSM_STATE.bit_length() - 1
_P_SHIFT = SSM_GROUP_CH.bit_length() - 1


def _s5_operands(lam_re, lam_im, log_step, b_re, b_im, c_re, c_im):
    g, n, p = SSM_GROUPS, SSM_STATE, SSM_GROUP_CH
    gn, gp, hn_, hp = g * n, g * p, _HALF_GROUPS * n, _HALF_GROUPS * p
    eye_g = _iota((g, g), 0) == _iota((g, g), 1)
    step = jnp.sum(jnp.where(eye_g, jnp.exp(log_step), 0.0), axis=1, keepdims=True)
    a_re = lam_re * step
    a_im = lam_im * step
    mag = jnp.exp(a_re)
    lbar_re = mag * jnp.cos(a_im)
    lbar_im = mag * jnp.sin(a_im)
    n_re = lbar_re - 1.0
    den = lam_re * lam_re + lam_im * lam_im
    f_re = (n_re * lam_re + lbar_im * lam_im) / den
    f_im = (lbar_im * lam_re - n_re * lam_im) / den

    spread_n = (_iota((n, gn), 0) == (_iota((n, gn), 1) & (n - 1))).astype(BF16)
    own_g = _iota((g, gn), 0) == (_iota((g, gn), 1) >> _N_SHIFT)

    def to_row(a):
        return jnp.sum(jnp.where(own_g, _pick_cols(a, spread_n), 0.0), axis=0, keepdims=True)

    per_group = ((_iota((gp, g), 0) >> _P_SHIFT) == _iota((gp, g), 1)).astype(BF16)
    fx_re, fx_im = _pick_rows(per_group, f_re), _pick_rows(per_group, f_im)
    bbar_re = fx_re * b_re - fx_im * b_im
    bbar_im = fx_re * b_im + fx_im * b_re

    tile_n = (_iota((n, hn_), 0) == (_iota((n, hn_), 1) & (n - 1))).astype(BF16)
    same_group = (_iota((hp, hn_), 0) >> _P_SHIFT) == (_iota((hp, hn_), 1) >> _N_SHIFT)

    def embed(a, hf):
        return jnp.where(same_group, _pick_cols(a[hf * hp:(hf + 1) * hp], tile_n), 0.0)

    return (to_row(lbar_re), to_row(lbar_im), embed(bbar_re, 0), embed(bbar_re, 1), embed(bbar_im, 0),
            embed(bbar_im, 1), embed(c_re, 0), embed(c_re, 1), embed(c_im, 0), embed(c_im, 1))


_S5_PARAM_SHAPES = ((SSM_GROUPS, SSM_STATE), (SSM_GROUPS, SSM_STATE), (1, SSM_GROUPS),
                    (D_SSM, SSM_STATE), (D_SSM, SSM_STATE), (D_SSM, SSM_STATE), (D_SSM, SSM_STATE))
_CM_SHAPE = (2, _HALF_GROUPS * SSM_GROUP_CH, _HALF_GROUPS * SSM_STATE)
_S5_OPERAND_SHAPES = ((1, SSM_LANES), (1, SSM_LANES), _CM_SHAPE, _CM_SHAPE, _CM_SHAPE, _CM_SHAPE)


def _s5_params_fwd(*params):
    def body(*refs):
        ins, (lre_ref, lim_ref, btre_ref, btim_ref, cmre_ref, cmim_ref) = refs[:7], refs[7:]
        vals = _s5_operands(*[r[...] for r in ins])
        lre_ref[...] = vals[0]
        lim_ref[...] = vals[1]
        for ref, pair in zip((btre_ref, btim_ref, cmre_ref, cmim_ref), (vals[2:4], vals[4:6], vals[6:8], vals[8:10])):
            ref[0] = pair[0].astype(BF16)
            ref[1] = pair[1].astype(BF16)

    dtypes = (F32, F32, BF16, BF16, BF16, BF16)
    return _call(
        body, name="s5_params_fwd",
        in_specs=[_full(s) for s in _S5_PARAM_SHAPES], out_specs=[_full(s) for s in _S5_OPERAND_SHAPES],
        out_shape=[_sds(s, d) for s, d in zip(_S5_OPERAND_SHAPES, dtypes)], compiler_params=_params(32),
    )(*params)


_BC_SIDE_BY_SIDE = (D_SSM, 4 * SSM_STATE)


def _s5_params_bwd(params, cotangents):
    def body(*refs):
        ins, (dlre, dlim, dbtre, dbtim, dcmre, dcmim), outs = refs[:7], refs[7:13], refs[13:]
        _, vjp = jax.vjp(_s5_operands, *[r[...] for r in ins])
        cts = (dlre[...], dlim[...], dbtre[0], dbtre[1], dbtim[0], dbtim[1], dcmre[0], dcmre[1], dcmim[0], dcmim[1])
        grads = vjp(cts)
        for ref, val in zip(outs[:3], grads[:3]):
            ref[...] = val
        outs[3][...] = jnp.concatenate(grads[3:], axis=1).astype(BF16)

    out_shapes = _S5_PARAM_SHAPES[:3] + (_BC_SIDE_BY_SIDE,)
    return _call(
        body, name="s5_params_bwd",
        in_specs=[_full(s) for s in _S5_PARAM_SHAPES + _S5_OPERAND_SHAPES],
        out_specs=[_full(s) for s in out_shapes],
        out_shape=[_sds(s, d) for s, d in zip(out_shapes, (F32, F32, F32, BF16))], compiler_params=_params(48),
    )(*params, *cotangents)


def _scan_geometry(n_seq, seq):
    slab = n_seq * SCAN_CHUNKS
    steps = seq // SCAN_CHUNKS
    tile_rows = slab * SCAN_TILE_STEPS
    n_tiles = steps // SCAN_TILE_STEPS
    return slab, steps, tile_rows, n_tiles


_SCAN_PARTS = D_SSM // LANES


def _whole_parts(rows):
    return [_full((rows, LANES))] * _SCAN_PARTS


def _part_shapes(rows):
    return [_sds((rows, LANES))] * _SCAN_PARTS


def _load_chunks(parts, first_chunk, n_chunks, steps, slab):
    return jnp.concatenate([
        jnp.concatenate([ref[pl.ds(first_chunk + q, steps, stride=slab), :] for ref in parts], axis=1)
        for q in range(n_chunks)], axis=0)


def _store_chunks(parts, first_chunk, value, steps, slab):
    for q in range(value.shape[0] // steps):
        for j, ref in enumerate(parts):
            ref[pl.ds(first_chunk + q, steps, stride=slab), :] = value[q * steps:(q + 1) * steps,
                                                                     j * LANES:(j + 1) * LANES]


def _join_parts(parts):
    return jnp.concatenate([ref[...] for ref in parts], axis=1)


def _split_parts(parts, value):
    for j, ref in enumerate(parts):
        ref[...] = value[:, j * LANES:(j + 1) * LANES]


def _complex_power(re, im, n):
    out = None
    while n:
        if n & 1:
            out = (re, im) if out is None else (out[0] * re - out[1] * im, out[0] * im + out[1] * re)
        n >>= 1
        if n:
            re, im = re * re - im * im, 2.0 * re * im
    return out


def _chunk_carry(sum_re, sum_im, carry_re, carry_im, a_re, a_im, n_seq, reverse):
    carry_re[...] = jnp.zeros_like(carry_re)
    carry_im[...] = jnp.zeros_like(carry_im)
    for s in range(n_seq):
        order = range(SCAN_CHUNKS - 2, -1, -1) if reverse else range(1, SCAN_CHUNKS)
        for c in order:
            r = s * SCAN_CHUNKS + c
            p = r + 1 if reverse else r - 1
            p_re, p_im = carry_re[p:p + 1, :], carry_im[p:p + 1, :]
            carry_re[r:r + 1, :] = a_re * p_re - a_im * p_im + sum_re[p:p + 1, :]
            carry_im[r:r + 1, :] = a_re * p_im + a_im * p_re + sum_im[p:p + 1, :]


def _s5_scan_fwd(u_parts, bt_re, bt_im, cm_re, cm_im, lbar_re, lbar_im, d_row, n_seq, seq):
    slab, steps, tile_rows, n_tiles = _scan_geometry(n_seq, seq)
    rows = u_parts[0].shape[0]

    def body(*refs):
        u_refs, refs = refs[:_SCAN_PARTS], refs[_SCAN_PARTS:]
        (bre_ref, bim_ref, cre_ref, cim_ref, lre_ref, lim_ref, d_ref), refs = refs[:7], refs[7:]
        y_refs, (hre_ref, him_ref, st_re, st_im, h0_re, h0_im, buf_re, buf_im) = refs[:_SCAN_PARTS], refs[_SCAN_PARTS:]
        second = pl.program_id(0) == 1
        i = pl.program_id(1)

        @pl.when(jnp.logical_and(i == 0, jnp.logical_not(second)))
        def _():
            st_re[...] = jnp.zeros_like(st_re)
            st_im[...] = jnp.zeros_like(st_im)

        u = _join_parts(u_refs)
        ub = u.astype(BF16)
        for hf in range(2):
            cols = slice(hf * 1024, (hf + 1) * 1024)
            buf_re[:, cols] = _dot(ub[:, hf * 256:(hf + 1) * 256], bre_ref[hf])
            buf_im[:, cols] = _dot(ub[:, hf * 256:(hf + 1) * 256], bim_ref[hf])

        for lc in range(SSM_LANES // SCAN_LANE_CHUNK):
            cols = slice(lc * SCAN_LANE_CHUNK, (lc + 1) * SCAN_LANE_CHUNK)
            l_re = jnp.broadcast_to(lre_ref[:, cols], (slab, SCAN_LANE_CHUNK))
            l_im = jnp.broadcast_to(lim_ref[:, cols], (slab, SCAN_LANE_CHUNK))

            def scan_tile(keep_states):
                def step(t, carry):
                    s_re, s_im = carry
                    r0 = pl.multiple_of(t * slab, slab)
                    n_re = l_re * s_re - l_im * s_im + buf_re[pl.ds(r0, slab), cols]
                    n_im = l_re * s_im + l_im * s_re + buf_im[pl.ds(r0, slab), cols]
                    if keep_states:
                        buf_re[pl.ds(r0, slab), cols] = n_re
                        buf_im[pl.ds(r0, slab), cols] = n_im
                    return n_re, n_im

                s_re, s_im = lax.fori_loop(0, SCAN_TILE_STEPS, step, (st_re[:, cols], st_im[:, cols]), unroll=True)
                st_re[:, cols] = s_re
                st_im[:, cols] = s_im

            pl.when(jnp.logical_not(second))(lambda: scan_tile(False))
            pl.when(second)(lambda: scan_tile(True))

        @pl.when(jnp.logical_and(i == n_tiles - 1, jnp.logical_not(second)))
        def _():
            a_re, a_im = _complex_power(lre_ref[...], lim_ref[...], steps)
            _chunk_carry(st_re, st_im, h0_re, h0_im, a_re, a_im, n_seq, reverse=False)
            st_re[...] = h0_re[...]
            st_im[...] = h0_im[...]

        @pl.when(second)
        def _():
            h_re = buf_re[...].astype(BF16)
            h_im = buf_im[...].astype(BF16)
            hre_ref[...] = h_re
            him_ref[...] = h_im
            for hf in range(2):
                cols = slice(hf * 1024, (hf + 1) * 1024)
                ycols = slice(hf * 256, (hf + 1) * 256)
                y_half = (_dot_nt(h_re[:, cols], cre_ref[hf]) - _dot_nt(h_im[:, cols], cim_ref[hf])
                          + d_ref[:, ycols] * u[:, ycols])
                _split_parts(y_refs[2 * hf:2 * hf + 2], y_half)

    tile = lambda w: pl.BlockSpec((tile_rows, w), lambda p, i: (i, 0))
    out_tile = lambda w: pl.BlockSpec((tile_rows, w), lambda p, i: (i * p, 0))
    cm = _full(_CM_SHAPE)
    outs = _call(
        body, name="s5_scan_fwd", grid=(2, n_tiles),
        in_specs=[tile(LANES)] * _SCAN_PARTS + [cm, cm, cm, cm, _full((1, SSM_LANES)), _full((1, SSM_LANES)),
                                                _full((1, 512))],
        out_specs=[out_tile(LANES)] * _SCAN_PARTS + [out_tile(SSM_LANES), out_tile(SSM_LANES)],
        out_shape=_part_shapes(rows) + [_sds((rows, SSM_LANES), BF16), _sds((rows, SSM_LANES), BF16)],
        scratch_shapes=[pltpu.VMEM((slab, SSM_LANES), F32)] * 4 + [pltpu.VMEM((tile_rows, SSM_LANES), F32)] * 2,
        compiler_params=_params(40, ("arbitrary", "arbitrary")),
    )(*u_parts, bt_re, bt_im, cm_re, cm_im, lbar_re, lbar_im, d_row)
    return outs[:_SCAN_PARTS], outs[_SCAN_PARTS], outs[_SCAN_PARTS + 1]


def _s5_scan_bwd(dy_parts, u_parts, h_re, h_im, bt_re, bt_im, cm_re, cm_im, lbar_re, lbar_im, d_row, n_seq, seq):
    slab, steps, tile_rows, n_tiles = _scan_geometry(n_seq, seq)
    rows = u_parts[0].shape[0]

    def body(*refs):
        dy_refs, u_refs, refs = refs[:_SCAN_PARTS], refs[_SCAN_PARTS:2 * _SCAN_PARTS], refs[2 * _SCAN_PARTS:]
        (hre_ref, him_ref, bre_ref, bim_ref, cre_ref, cim_ref, lre_ref, lim_ref, d_ref), refs = refs[:9], refs[9:]
        du_refs, refs = refs[:_SCAN_PARTS], refs[_SCAN_PARTS:]
        (dbre_ref, dbim_ref, dcre_ref, dcim_ref, dlre_ref, dlim_ref, dd_ref,
         st_re, st_im, g0_re, g0_im, acc_re, acc_im, buf_re, buf_im) = refs
        second = pl.program_id(0) == 1
        i = pl.program_id(1)

        @pl.when(jnp.logical_and(i == 0, jnp.logical_not(second)))
        def _():
            st_re[...] = jnp.zeros_like(st_re)
            st_im[...] = jnp.zeros_like(st_im)
            acc_re[...] = jnp.zeros_like(acc_re)
            acc_im[...] = jnp.zeros_like(acc_im)
            for ref in (dbre_ref, dbim_ref, dcre_ref, dcim_ref, dd_ref):
                ref[...] = jnp.zeros_like(ref)

        dy = _join_parts(dy_refs)
        dyb = dy.astype(BF16)
        for hf in range(2):
            cols = slice(hf * 1024, (hf + 1) * 1024)
            buf_re[:, cols] = _dot(dyb[:, hf * 256:(hf + 1) * 256], cre_ref[hf])
            buf_im[:, cols] = -_dot(dyb[:, hf * 256:(hf + 1) * 256], cim_ref[hf])

        for lc in range(SSM_LANES // SCAN_LANE_CHUNK):
            cols = slice(lc * SCAN_LANE_CHUNK, (lc + 1) * SCAN_LANE_CHUNK)
            l_re = jnp.broadcast_to(lre_ref[:, cols], (slab, SCAN_LANE_CHUNK))
            l_im = jnp.broadcast_to(lim_ref[:, cols], (slab, SCAN_LANE_CHUNK))

            def advance(r0, s_re, s_im):
                n_re = l_re * s_re + l_im * s_im + buf_re[pl.ds(r0, slab), cols]
                n_im = l_re * s_im - l_im * s_re + buf_im[pl.ds(r0, slab), cols]
                buf_re[pl.ds(r0, slab), cols] = n_re
                buf_im[pl.ds(r0, slab), cols] = n_im
                return n_re, n_im

            def row0(k):
                return pl.multiple_of((SCAN_TILE_STEPS - 1 - k) * slab, slab)

            @pl.when(jnp.logical_not(second))
            def _():
                s_re, s_im = lax.fori_loop(0, SCAN_TILE_STEPS, lambda k, s: advance(row0(k), *s),
                                           (st_re[:, cols], st_im[:, cols]), unroll=True)
                st_re[:, cols] = s_re
                st_im[:, cols] = s_im

            @pl.when(second)
            def _():
                def step(k, carry):
                    s_re, s_im, a_re, a_im = carry
                    r0 = row0(k)
                    hr = hre_ref[pl.ds(r0, slab), cols].astype(F32)
                    hi = him_ref[pl.ds(r0, slab), cols].astype(F32)
                    a_re = a_re + s_re * hr + s_im * hi
                    a_im = a_im + s_im * hr - s_re * hi
                    return advance(r0, s_re, s_im) + (a_re, a_im)

                zero = jnp.zeros((slab, SCAN_LANE_CHUNK), F32)
                s_re, s_im, a_re, a_im = lax.fori_loop(
                    0, SCAN_TILE_STEPS, step, (st_re[:, cols], st_im[:, cols], zero, zero), unroll=True)
                st_re[:, cols] = s_re
                st_im[:, cols] = s_im
                acc_re[:, cols] += a_re
                acc_im[:, cols] += a_im

        @pl.when(jnp.logical_and(i == n_tiles - 1, jnp.logical_not(second)))
        def _():
            p_re, p_im = _complex_power(lre_ref[...], lim_ref[...], steps)
            _chunk_carry(st_re, st_im, g0_re, g0_im, p_re, -p_im, n_seq, reverse=True)
            st_re[...] = g0_re[...]
            st_im[...] = g0_im[...]

        @pl.when(second)
        def _():
            u = _join_parts(u_refs)
            ub = u.astype(BF16)
            g_re = buf_re[...].astype(BF16)
            g_im = buf_im[...].astype(BF16)
            dd_ref[...] += jnp.sum(dy * u, axis=0, keepdims=True)
            for hf in range(2):
                cols = slice(hf * 1024, (hf + 1) * 1024)
                ycols = slice(hf * 256, (hf + 1) * 256)
                du_half = (_dot_nt(g_re[:, cols], bre_ref[hf]) + _dot_nt(g_im[:, cols], bim_ref[hf])
                           + d_ref[:, ycols] * dy[:, ycols])
                _split_parts(du_refs[2 * hf:2 * hf + 2], du_half)
                for q4 in range(_HALF_GROUPS // 4):
                    ch = slice(hf * 256 + q4 * 64, hf * 256 + (q4 + 1) * 64)
                    st = slice(hf * 1024 + q4 * 256, hf * 1024 + (q4 + 1) * 256)
                    blk = (hf, slice(q4 * 64, (q4 + 1) * 64), slice(q4 * 256, (q4 + 1) * 256))
                    dbre_ref[blk] += _dot_tn(ub[:, ch], g_re[:, st])
                    dbim_ref[blk] += _dot_tn(ub[:, ch], g_im[:, st])
                    dcre_ref[blk] += _dot_tn(dyb[:, ch], hre_ref[:, st])
                    dcim_ref[blk] -= _dot_tn(dyb[:, ch], him_ref[:, st])

        @pl.when(jnp.logical_and(i == n_tiles - 1, second))
        def _():
            dlre_ref[...] = jnp.sum(acc_re[...], axis=0, keepdims=True)
            dlim_ref[...] = jnp.sum(acc_im[...], axis=0, keepdims=True)

    tile = lambda w: pl.BlockSpec((tile_rows, w), lambda p, i: (n_tiles - 1 - i, 0))
    second_tile = lambda w: pl.BlockSpec((tile_rows, w), lambda p, i: (n_tiles - 1 - i * p, 0))
    cm = _full(_CM_SHAPE)
    row = _full((1, SSM_LANES))
    outs = _call(
        body, name="s5_scan_bwd", grid=(2, n_tiles),
        in_specs=[tile(LANES)] * _SCAN_PARTS + [second_tile(LANES)] * _SCAN_PARTS
        + [second_tile(SSM_LANES), second_tile(SSM_LANES), cm, cm, cm, cm, row, row, _full((1, 512))],
        out_specs=[second_tile(LANES)] * _SCAN_PARTS + [cm, cm, cm, cm, row, row, _full((1, 512))],
        out_shape=(_part_shapes(rows) + [_sds(_CM_SHAPE)] * 4 + [_sds((1, SSM_LANES))] * 2 + [_sds((1, 512))]),
        scratch_shapes=[pltpu.VMEM((slab, SSM_LANES), F32)] * 6 + [pltpu.VMEM((tile_rows, SSM_LANES), F32)] * 2,
        compiler_params=_params(48, ("arbitrary", "arbitrary")),
    )(*dy_parts, *u_parts, h_re, h_im, bt_re, bt_im, cm_re, cm_im, lbar_re, lbar_im, d_row)
    return (outs[:_SCAN_PARTS],) + tuple(outs[_SCAN_PARTS:])


def _glu_gate(gl, a, zs):
    return gl * jax.nn.sigmoid(a) * _silu(zs)


def _glu_fwd(y_parts, zs, w_glu, b_glu, n_seq, seq):
    rows = zs.shape[0]
    tm = 512
    slab, steps, _, _ = _scan_geometry(n_seq, seq)

    def body(*refs):
        y_refs, (zs_ref, w_ref, b_ref, o_ref) = refs[:_SCAN_PARTS], refs[_SCAN_PARTS:]
        y = _load_chunks(y_refs, pl.program_id(0) * (tm // steps), tm // steps, steps, slab)
        gl = jax.nn.gelu(y)
        a = _dot(gl.astype(BF16), w_ref[...]) + b_ref[...]
        o_ref[...] = _glu_gate(gl, a, zs_ref[...]).astype(BF16)

    return _call(
        body, name="glu_fwd", grid=(rows // tm,),
        in_specs=_whole_parts(rows) + [_rows(tm, 512), _full((512, 512)), _full((1, 512))],
        out_specs=_rows(tm, 512), out_shape=_sds((rows, 512), BF16),
        compiler_params=_params(32, ("arbitrary",)),
    )(*y_parts, zs, w_glu, b_glu)


def _glu_bwd(y_parts, zs, d_out, w_glu, b_glu, n_seq, seq):
    rows = zs.shape[0]
    tm = 512
    slab, steps, _, _ = _scan_geometry(n_seq, seq)

    def body(*refs):
        y_refs, (zs_ref, d_ref, w_ref, b_ref), refs = refs[:_SCAN_PARTS], refs[_SCAN_PARTS:_SCAN_PARTS + 4], refs[_SCAN_PARTS + 4:]
        dy_refs, (dzs_ref, dw_ref, db_ref) = refs[:_SCAN_PARTS], refs[_SCAN_PARTS:]
        first_chunk = pl.program_id(0) * (tm // steps)

        @pl.when(pl.program_id(0) == 0)
        def _():
            dw_ref[...] = jnp.zeros_like(dw_ref)
            db_ref[...] = jnp.zeros_like(db_ref)

        gl, gelu_vjp = jax.vjp(jax.nn.gelu, _load_chunks(y_refs, first_chunk, tm // steps, steps, slab))
        glb = gl.astype(BF16)
        a = _dot(glb, w_ref[...]) + b_ref[...]
        _, gate_vjp = jax.vjp(_glu_gate, gl, a, zs_ref[...])
        d_gl, d_a, d_zs = gate_vjp(d_ref[...])
        dab = d_a.astype(BF16)
        d_gl = d_gl + _dot_nt(dab, w_ref[...])
        _store_chunks(dy_refs, first_chunk, gelu_vjp(d_gl)[0], steps, slab)
        dzs_ref[...] = d_zs.astype(BF16)
        dw_ref[...] += _dot_tn(glb, dab)
        db_ref[...] += jnp.sum(d_a, axis=0, keepdims=True)

    *dy_parts, dzs, dw, db = _call(
        body, name="glu_bwd", grid=(rows // tm,),
        in_specs=_whole_parts(rows) + [_rows(tm, 512), _rows(tm, 512), _full((512, 512)), _full((1, 512))],
        out_specs=_whole_parts(rows) + [_rows(tm, 512), _full((512, 512)), _full((1, 512))],
        out_shape=_part_shapes(rows) + [_sds((rows, 512), BF16), _sds((512, 512)), _sds((1, 512))],
        compiler_params=_params(40, ("arbitrary",)),
    )(*y_parts, zs, d_out, w_glu, b_glu)
    return dy_parts, dzs, dw, db


_GROUP_ROWS = Q_PER_KV * BLOCK
_BLOCK_SHIFT = BLOCK.bit_length() - 1


def _attn_bias(j):
    query = _iota((BLOCK, _GROUP_ROWS), 1)
    dist_cur = (query & (BLOCK - 1)) - _iota((BLOCK, _GROUP_ROWS), 0)
    dist_prev = dist_cur + BLOCK
    head = query >> _BLOCK_SHIFT
    slope = jnp.zeros((BLOCK, _GROUP_ROWS), F32)
    for g in range(Q_PER_KV):
        slope = jnp.where(head == g, 2.0 ** (-(j * Q_PER_KV + g + 1)), slope)
    bias_cur = jnp.where(dist_cur >= 0, -slope * dist_cur.astype(F32), -jnp.inf)
    bias_prev = jnp.where(dist_prev < WINDOW, -slope * dist_prev.astype(F32), -jnp.inf)
    return bias_cur, bias_prev


_ATTN_BIAS_SCRATCH = pltpu.VMEM((KV_HEADS, 2, BLOCK, _GROUP_ROWS), F32)


def _fill_attn_bias(bias_ref):
    @pl.when(jnp.logical_and(pl.program_id(0) == 0, pl.program_id(1) == 0))
    def _():
        for j in range(KV_HEADS):
            bias_ref[j, 0], bias_ref[j, 1] = _attn_bias(j)


def _stack_heads(x, j):
    heads = range(j * Q_PER_KV, (j + 1) * Q_PER_KV)
    return jnp.concatenate([x[:, h * HEAD_DIM:(h + 1) * HEAD_DIM] for h in heads], axis=0)


def _head_rows(x, j):
    heads = range(j * Q_PER_KV, (j + 1) * Q_PER_KV)
    return jnp.concatenate([x[h:h + 1, :] for h in heads], axis=1)


def _sink_row(sk_ref, j):
    heads = range(j * Q_PER_KV, (j + 1) * Q_PER_KV)
    return jnp.concatenate([jnp.broadcast_to(sk_ref[0:1, h:h + 1], (1, BLOCK)) for h in heads], axis=1)


def _attn_fwd(q, k, v, za, sinks, n_seq, seq):
    nb = seq // BLOCK
    rows = q.shape[0]

    def body(q_ref, kc_ref, kp_ref, vc_ref, vp_ref, za_ref, sk_ref, o_ref, ao_ref, lse_ref, bias_ref):
        _fill_attn_bias(bias_ref)
        has_prev = pl.program_id(1) > 0
        q_all = q_ref[...]
        for j in range(KV_HEADS):
            js = slice(j * HEAD_DIM, (j + 1) * HEAD_DIM)
            bias_c, bias_p = bias_ref[j, 0], bias_ref[j, 1]
            q4 = _stack_heads(q_all, j)
            sc = _dot_nt(kc_ref[:, js], q4) + bias_c
            sp = _dot_nt(kp_ref[:, js], q4) + jnp.where(has_prev, bias_p, -jnp.inf)
            sink = _sink_row(sk_ref, j)
            m = jnp.maximum(jnp.max(jnp.maximum(sc, sp), axis=0, keepdims=True), sink)
            ec = jnp.exp(sc - m)
            ep = jnp.exp(sp - m)
            den = jnp.sum(ec + ep, axis=0, keepdims=True) + jnp.exp(sink - m)
            inv = 1.0 / den
            o4 = _dot_tn((ec * inv).astype(BF16), vc_ref[:, js]) + _dot_tn((ep * inv).astype(BF16), vp_ref[:, js])
            lse4 = m + jnp.log(den)
            for g in range(Q_PER_KV):
                h = j * Q_PER_KV + g
                o_ref[:, h * HEAD_DIM:(h + 1) * HEAD_DIM] = o4[g * BLOCK:(g + 1) * BLOCK]
                lse_ref[h:h + 1, :] = lse4[:, g * BLOCK:(g + 1) * BLOCK]
        ao_ref[...] = (o_ref[...] * _silu(za_ref[...])).astype(BF16)

    cur = lambda w: pl.BlockSpec((BLOCK, w), lambda b, n: (b * nb + n, 0))
    prev = lambda w: pl.BlockSpec((BLOCK, w), lambda b, n: (b * nb + jnp.maximum(n - 1, 0), 0))
    lse_rows = rows // BLOCK * N_HEADS
    return _call(
        body, name="attn_fwd", grid=(n_seq, nb),
        in_specs=[cur(512), cur(128), prev(128), cur(128), prev(128), cur(512), _full((1, N_HEADS))],
        out_specs=[cur(512), cur(512), pl.BlockSpec((N_HEADS, BLOCK), lambda b, n: (b * nb + n, 0))],
        out_shape=[_sds((rows, 512)), _sds((rows, 512), BF16), _sds((lse_rows, BLOCK))],
        scratch_shapes=[_ATTN_BIAS_SCRATCH], compiler_params=_params(32, ("arbitrary", "arbitrary")),
    )(q, k, k, v, v, za, sinks)


def _attn_bwd(q, k, v, za, o, lse, d_ao, sinks, n_seq, seq):
    nb = seq // BLOCK
    rows = q.shape[0]

    def body(q_ref, kc_ref, kp_ref, vc_ref, vp_ref, za_ref, o_ref, lse_ref, d_ref, sk_ref,
             dq_ref, dk_ref, dv_ref, dza_ref, dsk_ref, bias_ref, dk_carry, dv_carry):
        n = nb - 1 - pl.program_id(1)
        _fill_attn_bias(bias_ref)

        @pl.when(jnp.logical_and(pl.program_id(0) == 0, pl.program_id(1) == 0))
        def _():
            dsk_ref[...] = jnp.zeros_like(dsk_ref)
            dk_carry[...] = jnp.zeros_like(dk_carry)
            dv_carry[...] = jnp.zeros_like(dv_carry)

        has_prev = n > 0
        has_next = n + 1 < nb

        _, gate_vjp = jax.vjp(lambda o_, z_: o_ * _silu(z_), o_ref[...], za_ref[...])
        d_o, d_za = gate_vjp(d_ref[...])
        dza_ref[...] = d_za.astype(BF16)
        q_all = q_ref[...]
        lse_all = lse_ref[...]

        for j in range(KV_HEADS):
            js = slice(j * HEAD_DIM, (j + 1) * HEAD_DIM)
            kc, kp, vc, vp = kc_ref[:, js], kp_ref[:, js], vc_ref[:, js], vp_ref[:, js]
            bias_c, bias_p = bias_ref[j, 0], bias_ref[j, 1]
            q4 = _stack_heads(q_all, j)
            do4b = _stack_heads(d_o, j).astype(BF16)
            lse4 = _head_rows(lse_all, j)
            pc = jnp.exp(_dot_nt(kc, q4) + bias_c - lse4)
            pp = jnp.exp(_dot_nt(kp, q4) + jnp.where(has_prev, bias_p, -jnp.inf) - lse4)
            dpc = _dot_nt(vc, do4b)
            dpp = _dot_nt(vp, do4b)
            delta = jnp.sum(pc * dpc + pp * dpp, axis=0, keepdims=True)
            dsc = (pc * (dpc - delta)).astype(BF16)
            dsp = (pp * (dpp - delta)).astype(BF16)
            dq4 = ((_dot_tn(dsc, kc) + _dot_tn(dsp, kp)) * ATTN_SCALE).astype(BF16)
            sink_loss = jnp.exp(_sink_row(sk_ref, j) - lse4) * delta
            for g in range(Q_PER_KV):
                h = j * Q_PER_KV + g
                dq_ref[:, h * HEAD_DIM:(h + 1) * HEAD_DIM] = dq4[g * BLOCK:(g + 1) * BLOCK]
                dsk_ref[0:1, h:h + 1] -= jnp.sum(sink_loss[:, g * BLOCK:(g + 1) * BLOCK], axis=1, keepdims=True)
            dk = _dot(dsc, q4) + jnp.where(has_next, dk_carry[j], 0.0)
            dv = _dot(pc.astype(BF16), do4b) + jnp.where(has_next, dv_carry[j], 0.0)
            dk_carry[j] = _dot(dsp, q4)
            dv_carry[j] = _dot(pp.astype(BF16), do4b)
            dk_ref[:, js] = dk.astype(BF16)
            dv_ref[:, js] = dv.astype(BF16)

    cur = lambda w: pl.BlockSpec((BLOCK, w), lambda b, s: (b * nb + nb - 1 - s, 0))
    prev = lambda w: pl.BlockSpec((BLOCK, w), lambda b, s: (b * nb + jnp.maximum(nb - 2 - s, 0), 0))
    return _call(
        body, name="attn_bwd", grid=(n_seq, nb),
        in_specs=[cur(512), cur(128), prev(128), cur(128), prev(128), cur(512), cur(512),
                  pl.BlockSpec((N_HEADS, BLOCK), lambda b, s: (b * nb + nb - 1 - s, 0)), cur(512), _full((1, N_HEADS))],
        out_specs=[cur(512), cur(128), cur(128), cur(512), _full((1, N_HEADS))],
        out_shape=[_sds((rows, 512), BF16), _sds((rows, 128), BF16), _sds((rows, 128), BF16),
                   _sds((rows, 512), BF16), _sds((1, N_HEADS))],
        scratch_shapes=[_ATTN_BIAS_SCRATCH, pltpu.VMEM((KV_HEADS, BLOCK, HEAD_DIM), F32),
                        pltpu.VMEM((KV_HEADS, BLOCK, HEAD_DIM), F32)],
        compiler_params=_params(32, ("arbitrary", "arbitrary")),
    )(q, k, k, v, v, za, o, lse, d_ao, sinks)


def _tail(ssm_out, attn_out, x2d, p2d, target, w_out, g2, w_gate, b_gate, w_proj):
    rows = x2d.shape[0]
    tm = 512

    def body(so_ref, ao_ref, x_ref, p_ref, t_ref, wo_ref, g2_ref, wg_ref, bg_ref, wp_ref,
             dh1_ref, dso_ref, dao_ref, dwo_ref, dwg_ref, dwp_ref, dbg_ref, dg2_ref, loss_ref):
        @pl.when(pl.program_id(0) == 0)
        def _():
            for ref in (dwo_ref, dwg_ref, dwp_ref, dbg_ref, dg2_ref, loss_ref):
                ref[...] = jnp.zeros_like(ref)

        cat = jnp.concatenate([so_ref[...], ao_ref[...]], axis=1)
        g2 = g2_ref[...]
        mixed = _dot(cat, wo_ref[...])
        r = lax.rsqrt(jnp.mean(mixed * mixed, axis=-1, keepdims=True) + EPS)
        mr = mixed * r
        h1 = x_ref[...] + mr * g2
        h1b = h1.astype(BF16)
        gate = jax.nn.sigmoid(_dot(h1b, wg_ref[...]) + bg_ref[...])
        pb = p_ref[...].astype(BF16)
        wp_blocks = [slice(j * D_PLE, (j + 1) * D_PLE) for j in range(N_CHIPS)]
        pp = jnp.concatenate([_dot(pb, wp_ref[blk, :]) for blk in wp_blocks], axis=1)
        err = h1 + gate * pp - t_ref[...]
        loss_ref[...] += 0.5 * jnp.sum(jnp.mean(err * err, axis=-1, keepdims=True), axis=0, keepdims=True)

        dh2 = err * (1.0 / D_MODEL)
        d_glin = dh2 * pp * gate * (1.0 - gate)
        d_glin_b = d_glin.astype(BF16)
        dwg_ref[...] += _dot_tn(h1b, d_glin_b)
        dbg_ref[...] += jnp.sum(d_glin, axis=0, keepdims=True)
        d_pp = (dh2 * gate).astype(BF16)
        for blk in wp_blocks:
            dwp_ref[blk, :] += _dot_tn(pb, d_pp[:, blk])
        dh1 = dh2 + _dot_nt(d_glin_b, wg_ref[...])
        dh1_ref[...] = dh1
        dg2_ref[...] += jnp.sum(dh1 * mr, axis=0, keepdims=True)
        a_ = dh1 * g2
        d_mixed = (r * a_ - mr * (r * jnp.mean(a_ * mr, axis=-1, keepdims=True))).astype(BF16)
        dwo_ref[...] += _dot_tn(cat, d_mixed)
        d_cat = _dot_nt(d_mixed, wo_ref[...])
        dso_ref[...] = d_cat[:, 0:512]
        dao_ref[...] = d_cat[:, 512:1024]

    return _call(
        body, name="tail_fwd_bwd", grid=(rows // tm,),
        in_specs=[_rows(tm, 512), _rows(tm, 512), _rows(tm, D_MODEL), _rows(tm, D_PLE), _rows(tm, D_MODEL),
                  _full((D_MODEL, D_MODEL)), _full((1, D_MODEL)), _full((D_MODEL, D_MODEL)), _full((1, D_MODEL)),
                  _full((N_CHIPS * D_PLE, D_PLE))],
        out_specs=[_rows(tm, D_MODEL), _rows(tm, 512), _rows(tm, 512), _full((D_MODEL, D_MODEL)),
                   _full((D_MODEL, D_MODEL)), _full((N_CHIPS * D_PLE, D_PLE)), _full((1, D_MODEL)), _full((1, D_MODEL)),
                   _full((1, 1))],
        out_shape=[_sds((rows, D_MODEL)), _sds((rows, 512)), _sds((rows, 512)), _sds((D_MODEL, D_MODEL)),
                   _sds((D_MODEL, D_MODEL)), _sds((N_CHIPS * D_PLE, D_PLE)), _sds((1, D_MODEL)), _sds((1, D_MODEL)),
                   _sds((1, 1))],
        compiler_params=_params(52, ("arbitrary",)),
    )(ssm_out, attn_out, x2d, p2d, target, w_out, g2, w_gate, b_gate, w_proj)


def _local_step(x, hn, p, target, pre_norm_g, w_in_t, s5_params, s5_operands, ssm_d, w_glu, b_glu, sinks, w_out,
                post_norm_g, w_proj, w_gate, b_gate):
    n_seq, seq, _ = x.shape
    rows = n_seq * seq
    x2d = x.reshape(rows, D_MODEL)
    p2d = p.reshape(rows, D_PLE)
    t2d = target.reshape(rows, D_MODEL)

    l_re, l_im, bt_re, bt_im, cm_re, cm_im = s5_operands

    u_scan, zs, q, k, v, za = _in_proj(hn, w_in_t, n_seq, seq)
    y_scan, h_re, h_im = _s5_scan_fwd(u_scan, bt_re, bt_im, cm_re, cm_im, l_re, l_im, ssm_d, n_seq, seq)
    ssm_out = _glu_fwd(y_scan, zs, w_glu, b_glu, n_seq, seq)
    o, attn_out, lse = _attn_fwd(q, k, v, za, sinks, n_seq, seq)

    dh1, d_so, d_ao, d_w_out, d_w_gate, d_w_proj, d_b_gate, d_g2, loss = _tail(
        ssm_out, attn_out, x2d, p2d, t2d, w_out, post_norm_g, w_gate, b_gate, w_proj)

    dq, dk, dv, dza, d_sinks = _attn_bwd(q, k, v, za, o, lse, d_ao, sinks, n_seq, seq)
    dy_scan, dzs, d_w_glu, d_b_glu = _glu_bwd(y_scan, zs, d_so, w_glu, b_glu, n_seq, seq)
    du_scan, d_bt_re, d_bt_im, d_cm_re, d_cm_im, d_l_re, d_l_im, d_d = _s5_scan_bwd(
        dy_scan, u_scan, h_re, h_im, bt_re, bt_im, cm_re, cm_im, l_re, l_im, ssm_d, n_seq, seq)
    d_lam_re, d_lam_im, d_log_step, d_bc = _s5_params_bwd(
        s5_params, (d_l_re, d_l_im, d_bt_re, d_bt_im, d_cm_re, d_cm_im))

    d_proj, d_w_in_early, d_w_in_early_b = _in_proj_bwd_early(hn, du_scan, dzs, dq, dk, dv, dza, d_bc, n_seq, seq)
    grad_x, d_w_in_late, d_g1 = _in_proj_bwd(x2d, dh1, pre_norm_g, w_in_t, d_proj)
    grads = dict(
        pre_norm_g=d_g1, w_in_early=d_w_in_early, w_in_early_bf16=d_w_in_early_b, w_in_late=d_w_in_late,
        ssm_lam_re=d_lam_re, ssm_lam_im=d_lam_im, ssm_log_step=d_log_step, ssm_bc=d_bc, ssm_d=d_d, ssm_w_glu=d_w_glu,
        ssm_b_glu=d_b_glu, attn_sinks=d_sinks, w_out=d_w_out, post_norm_g=d_g2, pl_w_proj=d_w_proj,
        pl_w_gate=d_w_gate, pl_b_gate=d_b_gate)
    return grad_x.reshape(x.shape), loss, grads


_BIG = ("w_in", "ssm_w_glu", "w_out", "pl_w_proj", "pl_w_gate")
_BIG_SHARD = {"w_in": (D_IN // N_CHIPS, D_MODEL), "ssm_w_glu": (D_SSM // N_CHIPS, D_SSM),
              "w_out": (D_MODEL // N_CHIPS, D_MODEL), "pl_w_proj": (D_PLE, D_MODEL // N_CHIPS),
              "pl_w_gate": (D_MODEL // N_CHIPS, D_MODEL)}
_SMALL = {"pre_norm_g": (1, D_MODEL), "ssm_lam_re": (SSM_GROUPS, SSM_STATE), "ssm_lam_im": (SSM_GROUPS, SSM_STATE),
          "ssm_log_step": (1, SSM_GROUPS), "ssm_b_re": (D_SSM, SSM_STATE), "ssm_b_im": (D_SSM, SSM_STATE),
          "ssm_c_re": (D_SSM, SSM_STATE), "ssm_c_im": (D_SSM, SSM_STATE), "ssm_d": (1, D_SSM), "ssm_b_glu": (1, D_SSM),
          "attn_sinks": (1, N_HEADS), "post_norm_g": (1, D_MODEL), "pl_b_gate": (1, D_MODEL)}
_VEC_ROWS = ("pre_norm_g", "post_norm_g", "pl_b_gate", "ssm_d", "ssm_b_glu", "attn_sinks", "ssm_log_step", "loss")
_SMALL_GROUPS = (
    ("vec", (8, D_MODEL), tuple((name, r) for r, name in enumerate(_VEC_ROWS))),
    ("lam", (2 * SSM_GROUPS, SSM_STATE), (("ssm_lam_re", 0), ("ssm_lam_im", SSM_GROUPS))),
)
_SMALL_EARLY = ("ssm_b_re", "ssm_b_im", "ssm_c_re", "ssm_c_im")
_SMALL_ORDER = tuple(name for _, _, members in _SMALL_GROUPS for name, _ in members) + _SMALL_EARLY
_WEIGHT_ORDER = ("pre_norm_g", "w_in", "ssm_lam_re", "ssm_lam_im", "ssm_log_step", "ssm_b_re", "ssm_b_im", "ssm_c_re",
                 "ssm_c_im", "ssm_d", "ssm_w_glu", "ssm_b_glu", "attn_sinks", "w_out", "post_norm_g", "pl_w_proj",
                 "pl_w_gate", "pl_b_gate")


def _small_shape(name):
    return (1, 1) if name == "loss" else _SMALL[name]


def _to_kernel_form(name, a):
    a = a[0]
    if name == "w_in":
        return a.T
    if name in ("ssm_b_re", "ssm_b_im"):
        a = a.transpose(0, 2, 1)
    return a.reshape(_SMALL[name]) if name in _SMALL else a


def _from_kernel_form(name, a, shape):
    if name == "w_in":
        a = a.T
    if name in ("ssm_b_re", "ssm_b_im"):
        a = a.reshape(SSM_GROUPS, SSM_GROUP_CH, SSM_STATE).transpose(0, 2, 1)
    return a.reshape(shape)


def _mesh_place():
    x, y, c = lax.axis_index("x"), lax.axis_index("y"), lax.axis_index("c")
    other_chips = ((1 - x, y), (x, 1 - y), (1 - x, 1 - y))
    return x, y, c, other_chips


def _gather_copies(s_refs, g_refs, send_sems, recv_sems, local_sems):
    x, y, c, other_chips = _mesh_place()
    started = []
    for i, (s_ref, g_ref) in enumerate(zip(s_refs, g_refs)):
        rows = s_ref.shape[0]
        half = rows // 2

        def block(chip, g_ref=g_ref, rows=rows, half=half):
            return g_ref.at[pl.ds((2 * chip[0] + chip[1]) * rows + c * half, half), :]

        def copy(k, chip, to, src=None, i=i, block=block):
            return pltpu.make_async_remote_copy(
                src_ref=block(chip) if src is None else src, dst_ref=block(chip), send_sem=send_sems.at[6 * i + k],
                recv_sem=recv_sems.at[6 * i + k], device_id=to, device_id_type=MESH)

        own = pltpu.make_async_copy(s_ref, g_ref.at[pl.ds((2 * x + y) * rows, rows), :], local_sems.at[i])
        own.start()
        first = [copy(k, (x, y), (*chip, c), src=s_ref.at[pl.ds(c * half, half), :])
                 for k, chip in enumerate(other_chips)]
        for cp in first:
            cp.start()
        passed = [copy(3 + k, chip, (x, y, 1 - c)) for k, chip in enumerate(other_chips)]
        started.append((own, first, passed))
    for own, first, passed in started:
        for k in range(3):
            first[k].wait_recv()
            passed[k].start()
    for own, first, passed in started:
        for k in range(3):
            passed[k].wait_recv()
        for cp in first + passed:
            cp.wait_send()
        own.wait()


def _gather_semaphores(n_t):
    return [pltpu.SemaphoreType.DMA((6 * n_t,)), pltpu.SemaphoreType.DMA((6 * n_t,)), pltpu.SemaphoreType.DMA((n_t,))]


def _gather_weights_beside(shards, name, collective_id):
    n_t = len(shards)
    hbm = pltpu.MemorySpace.HBM
    s_refs = [jax.new_ref(s, memory_space=hbm) for s in shards]
    g_refs = [jax.empty_ref(jax.ShapeDtypeStruct((N_CHIPS * s.shape[0], s.shape[1]), s.dtype), memory_space=hbm)
              for s in shards]

    def launch(send_sems, recv_sems, local_sems):
        x, y, c, other_chips = _mesh_place()
        peers = [(*chip, c) for chip in other_chips] + [(x, y, 1 - c)]
        barrier = pltpu.get_barrier_semaphore()
        for peer in peers:
            pl.semaphore_signal(barrier, inc=1, device_id=peer, device_id_type=MESH)
        pl.semaphore_wait(barrier, len(peers))
        _gather_copies(s_refs, g_refs, send_sems, recv_sems, local_sems)

    pl.kernel(launch, mesh=plsc.ScalarSubcoreMesh(axis_name="sequencer", num_cores=1), name=name,
              scratch_types=_gather_semaphores(n_t), compiler_params=pltpu.CompilerParams(collective_id=collective_id))()
    return [g[...] for g in g_refs]


_RELATIONS = tuple(((r >> 2) & 1, (r >> 1) & 1, r & 1) for r in range(1, 8))


def _related(place, relation):
    return tuple(1 - a if flip else a for a, flip in zip(place, relation))


def _scatter_beside(mats, name, collective_id):
    hbm = pltpu.MemorySpace.HBM
    src_refs = [jax.new_ref(a, memory_space=hbm) for a in mats]
    land_refs = [jax.empty_ref(jax.ShapeDtypeStruct((7, a.shape[0] // 8, a.shape[1]), a.dtype), memory_space=hbm)
                 for a in mats]

    def launch(send_sems, recv_sems):
        me = (lax.axis_index("x"), lax.axis_index("y"), lax.axis_index("c"))
        peers = [_related(me, rel) for rel in _RELATIONS]
        barrier = pltpu.get_barrier_semaphore()
        for peer in peers:
            pl.semaphore_signal(barrier, inc=1, device_id=peer, device_id_type=MESH)
        pl.semaphore_wait(barrier, len(peers))
        copies = []
        for i, (src, land) in enumerate(zip(src_refs, land_refs)):
            hr = land.shape[1]
            for k, (tx, ty, tc) in enumerate(peers):
                rows = pl.ds((2 * tx + ty) * 2 * hr + tc * hr, hr)
                copies.append(pltpu.make_async_remote_copy(
                    src_ref=src.at[rows, :], dst_ref=land.at[k], send_sem=send_sems.at[7 * i + k],
                    recv_sem=recv_sems.at[7 * i + k], device_id=(tx, ty, tc), device_id_type=MESH))
                copies[-1].start()
        for cp in copies:
            cp.wait()

    n_sems = 7 * len(mats)
    pl.kernel(launch, mesh=plsc.ScalarSubcoreMesh(axis_name="sequencer", num_cores=1), name=name,
              scratch_types=[pltpu.SemaphoreType.DMA((n_sems,)), pltpu.SemaphoreType.DMA((n_sems,))],
              compiler_params=pltpu.CompilerParams(collective_id=collective_id))()
    return [ref[...] for ref in land_refs]


def _broadcast_beside(arrays):
    hbm = pltpu.MemorySpace.HBM
    src_refs = [jax.new_ref(a, memory_space=hbm) for a in arrays]
    land_refs = [jax.empty_ref(jax.ShapeDtypeStruct((len(_RELATIONS),) + a.shape, a.dtype), memory_space=hbm)
                 for a in arrays]

    def launch(send_sems, recv_sems):
        me = (lax.axis_index("x"), lax.axis_index("y"), lax.axis_index("c"))
        peers = [_related(me, rel) for rel in _RELATIONS]
        barrier = pltpu.get_barrier_semaphore()
        for peer in peers:
            pl.semaphore_signal(barrier, inc=1, device_id=peer, device_id_type=MESH)
        pl.semaphore_wait(barrier, len(peers))
        copies = []
        for i, (src, land) in enumerate(zip(src_refs, land_refs)):
            for k, peer in enumerate(peers):
                copies.append(pltpu.make_async_remote_copy(
                    src_ref=src, dst_ref=land.at[k], send_sem=send_sems.at[7 * i + k],
                    recv_sem=recv_sems.at[7 * i + k], device_id=peer, device_id_type=MESH))
                copies[-1].start()
        for cp in copies:
            cp.wait()

    n_sems = 7 * len(arrays)
    pl.kernel(launch, mesh=plsc.ScalarSubcoreMesh(axis_name="sequencer", num_cores=1), name="broadcast_beside",
              scratch_types=[pltpu.SemaphoreType.DMA((n_sems,)), pltpu.SemaphoreType.DMA((n_sems,))],
              compiler_params=pltpu.CompilerParams(collective_id=3))()
    return [ref[...] for ref in land_refs]


def _exchange_grads(big, small, landed, own_bc, landed_bc):
    n_t = len(big)
    n_g = len(_SMALL_GROUPS)
    names = _SMALL_ORDER
    halves = [(b.shape[0] // N_CHIPS // 2, b.shape[1]) for b in big]
    early = sorted(landed)
    late = [i for i in range(n_t) if i not in landed]
    n_sems = 4 * n_g + 7 * len(late) + n_t
    small_sem0, block_sem0 = n_t, n_t + len(names)
    early_sem0 = block_sem0 + N_CHIPS * len(late)
    landed_sem0 = early_sem0 + 2 * len(early)
    sent = [n for n in names if n in small]

    def body(*refs):
        pos = 0

        def take(n):
            nonlocal pos
            pos += n
            return refs[pos - n:pos]

        big_refs, small_refs = take(n_t), dict(zip(sent, take(len(sent))))
        land_refs = dict(zip(early, take(len(early))))
        own_bc_ref, landed_bc_ref = take(2)
        out_refs, small_out_refs = take(n_t), dict(zip(names, take(len(names))))
        per_late = lambda: dict(zip(late, take(len(late))))
        ga, gb, pme, send_b, recv_b = per_late(), per_late(), take(n_t), per_late(), per_late()
        own_e, land_e = dict(zip(early, take(len(early)))), dict(zip(early, take(len(early))))
        own_s, land_s = take(2)
        s_own, s_sib, s_chips, s_pair = take(n_g), take(n_g), take(n_g), take(n_g)
        stage = dict(zip(names, take(len(names))))
        send_sems, recv_sems, local_sems = take(3)
        x, y, c, other_chips = _mesh_place()
        me = 2 * x + y
        sibling = (x, y, 1 - c)
        sem_at = iter(range(n_sems))

        def remote(src, dst, to):
            k = next(sem_at)
            return pltpu.make_async_remote_copy(src_ref=src, dst_ref=dst, send_sem=send_sems.at[k],
                                                recv_sem=recv_sems.at[k], device_id=to, device_id_type=MESH)

        loads = [pltpu.make_async_copy(small_refs[name], stage[name], local_sems.at[small_sem0 + names.index(name)])
                 for name in sent]
        landed_loads = [pltpu.make_async_copy(own_bc_ref, own_s, local_sems.at[landed_sem0]),
                        pltpu.make_async_copy(landed_bc_ref, land_s, local_sems.at[landed_sem0 + 1])]
        for cp in loads + landed_loads:
            cp.start()
        for cp in loads:
            cp.wait()
        small_swaps = []
        for gi, (_, _, members) in enumerate(_SMALL_GROUPS):
            s_own[gi][...] = jnp.zeros_like(s_own[gi])
            for name, r0 in members:
                r, n = _small_shape(name)
                s_own[gi][r0:r0 + r, 0:n] = stage[name][...]
            small_swaps.append(remote(s_own[gi], s_sib[gi], sibling))
            small_swaps[gi].start()
        order = sorted(late, key=lambda i: halves[i][0] * halves[i][1])
        own_loads, big_swaps = {}, {}
        for i in order:
            hr = halves[i][0]
            own_loads[i], big_swaps[i] = [], []
            for j in range(N_CHIPS):
                mine = big_refs[i].at[pl.ds(j * 2 * hr + c * hr, hr), :]
                theirs = big_refs[i].at[pl.ds(j * 2 * hr + (1 - c) * hr, hr), :]
                sem = local_sems.at[block_sem0 + N_CHIPS * late.index(i) + j]
                own_loads[i].append(pltpu.make_async_copy(mine, ga[i].at[j], sem))
                own_loads[i][j].start()
                big_swaps[i].append(remote(theirs, gb[i].at[j], sibling))
                big_swaps[i][j].start()
        early_loads = {}
        for e, i in enumerate(early):
            hr = halves[i][0]
            mine = big_refs[i].at[pl.ds(me * 2 * hr + c * hr, hr), :]
            early_loads[i] = [pltpu.make_async_copy(mine, own_e[i], local_sems.at[early_sem0 + 2 * e]),
                              pltpu.make_async_copy(land_refs[i], land_e[i], local_sems.at[early_sem0 + 2 * e + 1])]
            for cp in early_loads[i]:
                cp.start()
        small_sends = []
        for gi in range(n_g):
            small_swaps[gi].wait_recv()
            s_pair[gi][...] = s_own[gi][...] + s_sib[gi][...]
            small_sends.append([remote(s_pair[gi], s_chips[gi].at[k], (*chip, c)) for k, chip in enumerate(other_chips)])
            for cp in small_sends[gi]:
                cp.start()

        def pair_sum(i, j):
            return ga[i][j] + gb[i][j]

        big_sends = {}
        for i in order:
            for j in range(N_CHIPS):
                own_loads[i][j].wait()
                big_swaps[i][j].wait_recv()
            big_sends[i] = []
            for k, chip in enumerate(other_chips):
                send_b[i][k] = pair_sum(i, 2 * chip[0] + chip[1]).astype(BF16)
                big_sends[i].append(remote(send_b[i].at[k], recv_b[i].at[k], (*chip, c)))
                big_sends[i][k].start()
        last_swaps, keeps = {}, {}
        for i in early + order:
            hr = halves[i][0]
            if i in landed:
                for cp in early_loads[i]:
                    cp.wait()
                total = own_e[i][...]
                for k in range(len(_RELATIONS)):
                    total = total + land_e[i][k].astype(F32)
                pme[i][...] = total
            else:
                for k in range(3):
                    big_sends[i][k].wait_recv()
                pme[i][...] = ((pair_sum(i, me) + recv_b[i][0].astype(F32)) + recv_b[i][1].astype(F32)) + recv_b[i][2].astype(F32)
            mine = out_refs[i].at[pl.ds(c * hr, hr), :]
            keeps[i] = pltpu.make_async_copy(pme[i], mine, local_sems.at[i])
            keeps[i].start()
            last_swaps[i] = remote(pme[i], mine, sibling)
            last_swaps[i].start()

        for gi, (_, _, members) in enumerate(_SMALL_GROUPS):
            for k in range(3):
                small_sends[gi][k].wait_recv()
            total = None
            for j in range(N_CHIPS):
                rel = jnp.bitwise_xor(j, me)
                term = jnp.where(rel == 0, s_pair[gi][...], jnp.where(
                    rel == 2, s_chips[gi][0], jnp.where(rel == 1, s_chips[gi][1], s_chips[gi][2])))
                total = term if total is None else total + term
            s_sib[gi][...] = total
            for name, r0 in members:
                r, n = _small_shape(name)
                stage[name][...] = s_sib[gi][r0:r0 + r, 0:n]
        my_index = 4 * x + 2 * y + c
        for cp in landed_loads:
            cp.wait()
        total = None
        for d in range(2 * N_CHIPS):
            rel = jnp.bitwise_xor(d, my_index)
            term = own_s[...]
            for k in range(len(_RELATIONS)):
                term = jnp.where(rel == k + 1, land_s[k], term)
            total = term.astype(F32) if total is None else total + term.astype(F32)
        for a, name in enumerate(_SMALL_EARLY):
            stage[name][...] = total[:, a * SSM_STATE:(a + 1) * SSM_STATE]
        stores = [pltpu.make_async_copy(stage[name], small_out_refs[name], local_sems.at[small_sem0 + a])
                  for a, name in enumerate(names)]
        for cp in stores:
            cp.start()

        for i in range(n_t):
            last_swaps[i].wait_recv()
            keeps[i].wait()
        for cp in stores:
            cp.wait()
        groups = list(big_swaps.values()) + small_sends + list(big_sends.values())
        for cp in small_swaps + [cp for group in groups for cp in group] + list(last_swaps.values()):
            cp.wait_send()

    any_spec = pl.BlockSpec(memory_space=pl.ANY)
    small_shapes = [_sds(_small_shape(n)) for n in names]
    group_shapes = [shape for _, shape, _ in _SMALL_GROUPS]
    vmem = lambda which, dtype, lead=(): [pltpu.VMEM(lead + halves[i], dtype) for i in which]
    outs = _call(
        body, name="exchange_grads",
        in_specs=[any_spec] * (n_t + len(sent) + len(early) + 2),
        out_specs=[any_spec] * (n_t + len(names)),
        out_shape=[_sds((b.shape[0] // N_CHIPS, b.shape[1])) for b in big] + small_shapes,
        scratch_shapes=(vmem(late, F32, (N_CHIPS,)) + vmem(late, F32, (N_CHIPS,)) + vmem(range(n_t), F32)
                        + vmem(late, BF16, (3,)) + vmem(late, BF16, (3,))
                        + vmem(early, F32)
                        + [pltpu.VMEM((len(_RELATIONS),) + halves[i], landed[i].dtype) for i in early]
                        + [pltpu.VMEM(own_bc.shape, own_bc.dtype), pltpu.VMEM(landed_bc.shape, landed_bc.dtype)]
                        + [pltpu.VMEM(s, F32) for s in group_shapes] * 2 + [pltpu.VMEM((3,) + s, F32) for s in group_shapes]
                        + [pltpu.VMEM(s, F32) for s in group_shapes]
                        + [pltpu.VMEM(_small_shape(n), F32) for n in names]
                        + [pltpu.SemaphoreType.DMA((n_sems,)), pltpu.SemaphoreType.DMA((n_sems,)),
                           pltpu.SemaphoreType.DMA((landed_sem0 + 2,))]),
        compiler_params=_params(48),
    )(*big, *[small[n] for n in sent], *[landed[i] for i in early], own_bc, landed_bc)
    return list(outs[:n_t]), dict(zip(names, outs[n_t:n_t + len(names)]))


def _adamw_update(w, g, m, v):
    m = ADAM_B1 * m + (1.0 - ADAM_B1) * g
    v = ADAM_B2 * v + (1.0 - ADAM_B2) * (g * g)
    m_hat = m / (1.0 - ADAM_B1 ** ADAM_STEP)
    v_hat = v / (1.0 - ADAM_B2 ** ADAM_STEP)
    return -ADAM_LR * (m_hat / (jnp.sqrt(v_hat) + ADAM_EPS) + ADAM_WD * w), m, v


def _adamw(w, g, m, v, grid, name):
    n_t = len(w)

    def body(*refs):
        ins, outs = refs[:4 * n_t], refs[4 * n_t:]
        for i in range(n_t):
            w_, g_, m_, v_ = [ins[a * n_t + i][...] for a in range(4)]
            vals = (g_,) + _adamw_update(w_, g_, m_, v_)
            for a in range(4):
                outs[a * n_t + i][...] = vals[a]

    specs = [pl.BlockSpec((a.shape[0] // grid, a.shape[1]), lambda i: (i, 0)) for a in w]
    shapes = [_sds(a.shape) for a in w]
    outs = _call(
        body, name=name, grid=(grid,), in_specs=specs * 4, out_specs=specs * 4, out_shape=shapes * 4,
        compiler_params=_params(40, ("arbitrary",)),
    )(*w, *g, *m, *v)
    return [outs[a * n_t:(a + 1) * n_t] for a in range(4)]


def kernel(x, p, pre_norm_g, w_in, ssm_lam_re, ssm_lam_im, ssm_log_step, ssm_b_re, ssm_b_im, ssm_c_re, ssm_c_im, ssm_d, ssm_w_glu, ssm_b_glu, attn_sinks, w_out, post_norm_g, pl_w_proj, pl_w_gate, pl_b_gate, loss_target, m_pre_norm_g, m_w_in, m_ssm_lam_re, m_ssm_lam_im, m_ssm_log_step, m_ssm_b_re, m_ssm_b_im, m_ssm_c_re, m_ssm_c_im, m_ssm_d, m_ssm_w_glu, m_ssm_b_glu, m_attn_sinks, m_w_out, m_post_norm_g, m_pl_w_proj, m_pl_w_gate, m_pl_b_gate, v_pre_norm_g, v_w_in, v_ssm_lam_re, v_ssm_lam_im, v_ssm_log_step, v_ssm_b_re, v_ssm_b_im, v_ssm_c_re, v_ssm_c_im, v_ssm_d, v_ssm_w_glu, v_ssm_b_glu, v_attn_sinks, v_w_out, v_post_norm_g, v_pl_w_proj, v_pl_w_gate, v_pl_b_gate):
    weights = dict(pre_norm_g=pre_norm_g, w_in=w_in, ssm_lam_re=ssm_lam_re, ssm_lam_im=ssm_lam_im,
                   ssm_log_step=ssm_log_step, ssm_b_re=ssm_b_re, ssm_b_im=ssm_b_im, ssm_c_re=ssm_c_re,
                   ssm_c_im=ssm_c_im, ssm_d=ssm_d, ssm_w_glu=ssm_w_glu, ssm_b_glu=ssm_b_glu, attn_sinks=attn_sinks,
                   w_out=w_out, post_norm_g=post_norm_g, pl_w_proj=pl_w_proj, pl_w_gate=pl_w_gate, pl_b_gate=pl_b_gate)
    m_in = dict(pre_norm_g=m_pre_norm_g, w_in=m_w_in, ssm_lam_re=m_ssm_lam_re, ssm_lam_im=m_ssm_lam_im,
                ssm_log_step=m_ssm_log_step, ssm_b_re=m_ssm_b_re, ssm_b_im=m_ssm_b_im, ssm_c_re=m_ssm_c_re,
                ssm_c_im=m_ssm_c_im, ssm_d=m_ssm_d, ssm_w_glu=m_ssm_w_glu, ssm_b_glu=m_ssm_b_glu,
                attn_sinks=m_attn_sinks, w_out=m_w_out, post_norm_g=m_post_norm_g, pl_w_proj=m_pl_w_proj,
                pl_w_gate=m_pl_w_gate, pl_b_gate=m_pl_b_gate)
    v_in = dict(pre_norm_g=v_pre_norm_g, w_in=v_w_in, ssm_lam_re=v_ssm_lam_re, ssm_lam_im=v_ssm_lam_im,
                ssm_log_step=v_ssm_log_step, ssm_b_re=v_ssm_b_re, ssm_b_im=v_ssm_b_im, ssm_c_re=v_ssm_c_re,
                ssm_c_im=v_ssm_c_im, ssm_d=v_ssm_d, ssm_w_glu=v_ssm_w_glu, ssm_b_glu=v_ssm_b_glu,
                attn_sinks=v_attn_sinks, w_out=v_w_out, post_norm_g=v_post_norm_g, pl_w_proj=v_pl_w_proj,
                pl_w_gate=v_pl_w_gate, pl_b_gate=v_pl_b_gate)

    def two_d(tree):
        return {k: _to_kernel_form(k, a) for k, a in tree.items()}

    w2, m2, v2 = two_d(weights), two_d(m_in), two_d(v_in)

    (w_in_full,) = _gather_weights_beside([w2["w_in"].astype(BF16)], "gather_w_in_beside", 4)
    s5_params = tuple(w2[n] for n in ("ssm_lam_re", "ssm_lam_im", "ssm_log_step", "ssm_b_re", "ssm_b_im", "ssm_c_re",
                                      "ssm_c_im"))
    s5_operands = _s5_params_fwd(*s5_params)
    hn = _pre_norm(x.reshape(-1, D_MODEL), w2["pre_norm_g"])
    behind = s5_operands[0][0, 0] * 0.0 + hn[0, 0].astype(F32) * 0.0
    rest = _gather_weights_beside([(w2[n] + behind).astype(BF16) for n in _BIG[1:]], "gather_weights_beside", 1)
    full = dict(zip(_BIG, [w_in_full] + rest))
    grad_x, loss, grads = _local_step(
        x, hn, p, loss_target, w2["pre_norm_g"], full["w_in"], s5_params, s5_operands, w2["ssm_d"], full["ssm_w_glu"], w2["ssm_b_glu"],
        w2["attn_sinks"], full["w_out"], w2["post_norm_g"], full["pl_w_proj"], full["pl_w_gate"], w2["pl_b_gate"])

    mats = ("w_in_early", "w_in_late") + _BIG[1:]
    sent_early = ("w_out", "pl_w_gate", "pl_w_proj")
    landed = dict(zip([mats.index(n) for n in sent_early],
                      _scatter_beside([grads[n] for n in sent_early], "scatter_beside", 2)))
    after_scatter = landed[mats.index(sent_early[-1])][0, 0, 0] * 0.0
    own_bc = grads["ssm_bc"] + after_scatter.astype(BF16)
    landed_bc = _broadcast_beside([own_bc])[0]
    landed[0] = _scatter_beside([grads["w_in_early_bf16"]], "scatter_w_in_beside", 5)[0]
    sent_here = {**{n: grads[n] for n in _SMALL if n not in _SMALL_EARLY}, "loss": loss}
    g_big, g_small = _exchange_grads([grads[n] for n in mats], sent_here, landed, own_bc, landed_bc)
    g_big = dict(zip(mats, g_big))
    g_big["w_in"] = jnp.concatenate([g_big.pop("w_in_early"), g_big.pop("w_in_late")], axis=1)
    total_loss = g_small.pop("loss")

    big_out = _adamw([w2[n] for n in _BIG], [g_big[n] for n in _BIG], [m2[n] for n in _BIG], [v2[n] for n in _BIG],
                     8, "adamw_matrices")
    small_names = tuple(_SMALL)
    small_out = _adamw([w2[n] for n in small_names], [g_small[n] for n in small_names], [m2[n] for n in small_names],
                       [v2[n] for n in small_names], 1, "adamw_small")

    results = [{**dict(zip(_BIG, big_part)), **dict(zip(small_names, small_part))}
               for big_part, small_part in zip(big_out, small_out)]
    flat = [_from_kernel_form(name, r[name], weights[name].shape) for r in results for name in _WEIGHT_ORDER]
    return (total_loss.reshape(()), grad_x, *flat)
```

```python
import math

import jax
import jax.numpy as jnp
from jax import lax
from jax.experimental import pallas as pl
from jax.experimental.pallas import tpu as pltpu
from jax.experimental.pallas import tpu_sc as plsc

F32 = jnp.float32
BF16 = jnp.bfloat16

D_MODEL = 1024
D_SSM = 512
D_ATTN = 512
SSM_GROUPS = 32
SSM_GROUP_CH = 16
SSM_STATE = 64
SSM_LANES = SSM_GROUPS * SSM_STATE
HEAD_DIM = 64
N_HEADS = 8
KV_HEADS = 2
Q_PER_KV = 4
WINDOW = 128
BLOCK = 128
D_PLE = 256
D_IN = 2304
EPS = 1e-6
ATTN_SCALE = 1.0 / math.sqrt(HEAD_DIM)

ADAM_LR = 0.001
ADAM_B1 = 0.9
ADAM_B2 = 0.999
ADAM_EPS = 1e-08
ADAM_WD = 0.01
ADAM_STEP = 10

N_CHIPS = 4
LANES = 128
SCAN_CHUNKS = 8
SCAN_TILE_STEPS = 32
SCAN_LANE_CHUNK = 512
MIB = 2 ** 20
MESH = pl.DeviceIdType.MESH


def _dot(a, b):
    return jnp.dot(a, b, preferred_element_type=F32)


def _dot_nt(a, b):
    return lax.dot_general(a, b, (((1,), (1,)), ((), ())), preferred_element_type=F32)


def _dot_tn(a, b):
    return lax.dot_general(a, b, (((0,), (0,)), ((), ())), preferred_element_type=F32)


def _params(vmem_mib, semantics=None):
    kw = dict(vmem_limit_bytes=vmem_mib * MIB)
    if semantics is not None:
        kw["dimension_semantics"] = semantics
    return pltpu.CompilerParams(**kw)


def _full(shape):
    nd = len(shape)
    return pl.BlockSpec(shape, lambda *_: (0,) * nd, pipeline_mode=pl.Buffered(1))


def _rows(tm, width):
    return pl.BlockSpec((tm, width), lambda i: (i, 0))


def _sds(shape, dtype=F32):
    return pltpu.HBM(shape, dtype)


def _call(body, **kw):
    fn = pl.pallas_call(body, **kw)
    return lambda *args: fn(*[pltpu.with_memory_space_constraint(a, pltpu.HBM) for a in args])


def _silu(z):
    return z * jax.nn.sigmoid(z)


def _pre_norm(x2d, g1):
    rows = x2d.shape[0]
    tm = 512

    def body(x_ref, g_ref, hn_ref):
        x = x_ref[...]
        r = lax.rsqrt(jnp.mean(x * x, axis=-1, keepdims=True) + EPS)
        hn_ref[...] = (x * r * g_ref[...]).astype(BF16)

    return _call(
        body, name="pre_norm", grid=(rows // tm,), in_specs=[_rows(tm, D_MODEL), _full((1, D_MODEL))],
        out_specs=_rows(tm, D_MODEL), out_shape=_sds((rows, D_MODEL), BF16), compiler_params=_params(32, ("arbitrary",)),
    )(x2d, g1)


def _in_proj(hn, w_in_t, n_seq, seq):
    rows = hn.shape[0]
    tm = 1024
    slab, steps, _, _ = _scan_geometry(n_seq, seq)

    def body(hn_ref, w_ref, *out_refs):
        u_parts, (zs_ref, q_ref, k_ref, v_ref, za_ref) = out_refs[:_SCAN_PARTS], out_refs[_SCAN_PARTS:]
        whole = _dot_nt(hn_ref[...], w_ref[...])

        def proj(a, b):
            return whole[:, a:b]

        _store_chunks(u_parts, pl.program_id(0) * (tm // steps), proj(0, 512), steps, slab)
        zs_ref[...] = proj(512, 1024)
        q_ref[...] = (proj(1024, 1536) * ATTN_SCALE).astype(BF16)
        k_ref[...] = proj(1536, 1664).astype(BF16)
        v_ref[...] = proj(1664, 1792).astype(BF16)
        za_ref[...] = proj(1792, 2304)

    *u_parts, zs, q, k, v, za = _call(
        body, name="in_proj", grid=(rows // tm,),
        in_specs=[_rows(tm, D_MODEL), _full((D_IN, D_MODEL))],
        out_specs=_whole_parts(rows) + [_rows(tm, 512), _rows(tm, 512), _rows(tm, 128), _rows(tm, 128), _rows(tm, 512)],
        out_shape=_part_shapes(rows) + [_sds((rows, 512)), _sds((rows, 512), BF16), _sds((rows, 128), BF16),
                                        _sds((rows, 128), BF16), _sds((rows, 512))],
        compiler_params=_params(48, ("arbitrary",)),
    )(hn, w_in_t)
    return u_parts, zs, q, k, v, za


_EARLY_COLS = D_MODEL // 2


def _in_proj_bwd_early(hn, du_parts, dzs, dq, dk, dv, dza, runs_after, n_seq, seq):
    rows = hn.shape[0]
    tm = 512
    slab, steps, _, _ = _scan_geometry(n_seq, seq)

    def body(hn_ref, *refs):
        du_parts, (dzs_ref, dq_ref, dk_ref, dv_ref, dza_ref, _, dproj_ref, dw_ref, dwb_ref) = refs[:_SCAN_PARTS], refs[_SCAN_PARTS:]
        i = pl.program_id(0)

        @pl.when(i == 0)
        def _():
            dw_ref[...] = jnp.zeros_like(dw_ref)

        du = _load_chunks(du_parts, i * (tm // steps), tm // steps, steps, slab)
        d_proj = jnp.concatenate([du.astype(BF16), dzs_ref[...], dq_ref[...], dk_ref[...], dv_ref[...], dza_ref[...]],
                                 axis=1)
        dproj_ref[...] = d_proj
        dw_ref[...] += _dot_tn(d_proj, hn_ref[...])

        @pl.when(i == rows // tm - 1)
        def _():
            dwb_ref[...] = dw_ref[...].astype(BF16)

    return _call(
        body, name="in_proj_bwd_early", grid=(rows // tm,),
        in_specs=[_rows(tm, _EARLY_COLS)] + _whole_parts(rows)
        + [_rows(tm, 512), _rows(tm, 512), _rows(tm, 128), _rows(tm, 128), _rows(tm, 512),
           pl.BlockSpec(memory_space=pl.ANY)],
        out_specs=[_rows(tm, D_IN), _full((D_IN, _EARLY_COLS)), _full((D_IN, _EARLY_COLS))],
        out_shape=[_sds((rows, D_IN), BF16), _sds((D_IN, _EARLY_COLS)), _sds((D_IN, _EARLY_COLS), BF16)],
        compiler_params=_params(48, ("arbitrary",)),
    )(hn, *du_parts, dzs, dq, dk, dv, dza, runs_after)


def _in_proj_bwd(x2d, dh1, g1, w_in_t, d_proj):
    rows = x2d.shape[0]
    tm = 512

    def body(x_ref, dh1_ref, g_ref, w_ref, dproj_ref, gx_ref, dw_ref, dg_ref):
        @pl.when(pl.program_id(0) == 0)
        def _():
            dw_ref[...] = jnp.zeros_like(dw_ref)
            dg_ref[...] = jnp.zeros_like(dg_ref)

        x = x_ref[...]
        g = g_ref[...]
        r = lax.rsqrt(jnp.mean(x * x, axis=-1, keepdims=True) + EPS)
        xr = x * r
        hn = (xr[:, _EARLY_COLS:] * g[:, _EARLY_COLS:]).astype(BF16)
        d_proj = dproj_ref[...]
        dhn = _dot(d_proj, w_ref[...])
        dw_ref[...] += _dot_tn(d_proj, hn)
        dg_ref[...] += jnp.sum(dhn * xr, axis=0, keepdims=True)
        a_ = dhn * g
        gx_ref[...] = dh1_ref[...] + r * a_ - xr * (r * jnp.mean(a_ * xr, axis=-1, keepdims=True))

    late_cols = D_MODEL - _EARLY_COLS
    return _call(
        body, name="in_proj_bwd", grid=(rows // tm,),
        in_specs=[_rows(tm, D_MODEL), _rows(tm, D_MODEL), _full((1, D_MODEL)), _full((D_IN, D_MODEL)), _rows(tm, D_IN)],
        out_specs=[_rows(tm, D_MODEL), _full((D_IN, late_cols)), _full((1, D_MODEL))],
        out_shape=[_sds((rows, D_MODEL)), _sds((D_IN, late_cols)), _sds((1, D_MODEL))],
        compiler_params=_params(52, ("arbitrary",)),
    )(x2d, dh1, g1, w_in_t, d_proj)


def _iota(shape, axis):
    return lax.broadcasted_iota(jnp.int32, shape, axis)


def _sum_of_thirds(f, a):
    hi = a.astype(BF16)
    rest = a - hi.astype(F32)
    mid = rest.astype(BF16)
    low = (rest - mid.astype(F32)).astype(BF16)
    return (f(hi) + f(mid)) + f(low)


@jax.custom_vjp
def _pick_rows(e, a):
    return _sum_of_thirds(lambda part: _dot(e, part), a)


def _pick_rows_fwd(e, a):
    return _pick_rows(e, a), e


def _pick_rows_bwd(e, ct):
    return jnp.zeros_like(e), _sum_of_thirds(lambda part: _dot_tn(e, part), ct)


_pick_rows.defvjp(_pick_rows_fwd, _pick_rows_bwd)


@jax.custom_vjp
def _pick_cols(a, e):
    return _sum_of_thirds(lambda part: _dot(part, e), a)


def _pick_cols_fwd(a, e):
    return _pick_cols(a, e), e


def _pick_cols_bwd(e, ct):
    return _sum_of_thirds(lambda part: _dot_nt(part, e), ct), jnp.zeros_like(e)


_pick_cols.defvjp(_pick_cols_fwd, _pick_cols_bwd)


_HALF_GROUPS = SSM_GROUPS // 2
_N_SHIFT = SSM_STATE.bit_length() - 1
_P_SHIFT = SSM_GROUP_CH.bit_length() - 1


def _s5_operands(lam_re, lam_im, log_step, b_re, b_im, c_re, c_im):
    g, n, p = SSM_GROUPS, SSM_STATE, SSM_GROUP_CH
    gn, gp, hn_, hp = g * n, g * p, _HALF_GROUPS * n, _HALF_GROUPS * p
    eye_g = _iota((g, g), 0) == _iota((g, g), 1)
    step = jnp.sum(jnp.where(eye_g, jnp.exp(log_step), 0.0), axis=1, keepdims=True)
    a_re = lam_re * step
    a_im = lam_im * step
    mag = jnp.exp(a_re)
    lbar_re = mag * jnp.cos(a_im)
    lbar_im = mag * jnp.sin(a_im)
    n_re = lbar_re - 1.0
    den = lam_re * lam_re + lam_im * lam_im
    f_re = (n_re * lam_re + lbar_im * lam_im) / den
    f_im = (lbar_im * lam_re - n_re * lam_im) / den

    spread_n = (_iota((n, gn), 0) == (_iota((n, gn), 1) & (n - 1))).astype(BF16)
    own_g = _iota((g, gn), 0) == (_iota((g, gn), 1) >> _N_SHIFT)

    def to_row(a):
        return jnp.sum(jnp.where(own_g, _pick_cols(a, spread_n), 0.0), axis=0, keepdims=True)

    per_group = ((_iota((gp, g), 0) >> _P_SHIFT) == _iota((gp, g), 1)).astype(BF16)
    fx_re, fx_im = _pick_rows(per_group, f_re), _pick_rows(per_group, f_im)
    bbar_re = fx_re * b_re - fx_im * b_im
    bbar_im = fx_re * b_im + fx_im * b_re

    tile_n = (_iota((n, hn_), 0) == (_iota((n, hn_), 1) & (n - 1))).astype(BF16)
    same_group = (_iota((hp, hn_), 0) >> _P_SHIFT) == (_iota((hp, hn_), 1) >> _N_SHIFT)

    def embed(a, hf):
        return jnp.where(same_group, _pick_cols(a[hf * hp:(hf + 1) * hp], tile_n), 0.0)

    return (to_row(lbar_re), to_row(lbar_im), embed(bbar_re, 0), embed(bbar_re, 1), embed(bbar_im, 0),
            embed(bbar_im, 1), embed(c_re, 0), embed(c_re, 1), embed(c_im, 0), embed(c_im, 1))


_S5_PARAM_SHAPES = ((SSM_GROUPS, SSM_STATE), (SSM_GROUPS, SSM_STATE), (1, SSM_GROUPS),
                    (D_SSM, SSM_STATE), (D_SSM, SSM_STATE), (D_SSM, SSM_STATE), (D_SSM, SSM_STATE))
_CM_SHAPE = (2, _HALF_GROUPS * SSM_GROUP_CH, _HALF_GROUPS * SSM_STATE)
_S5_OPERAND_SHAPES = ((1, SSM_LANES), (1, SSM_LANES), _CM_SHAPE, _CM_SHAPE, _CM_SHAPE, _CM_SHAPE)


def _s5_params_fwd(*params):
    def body(*refs):
        ins, (lre_ref, lim_ref, btre_ref, btim_ref, cmre_ref, cmim_ref) = refs[:7], refs[7:]
        vals = _s5_operands(*[r[...] for r in ins])
        lre_ref[...] = vals[0]
        lim_ref[...] = vals[1]
        for ref, pair in zip((btre_ref, btim_ref, cmre_ref, cmim_ref), (vals[2:4], vals[4:6], vals[6:8], vals[8:10])):
            ref[0] = pair[0].astype(BF16)
            ref[1] = pair[1].astype(BF16)

    dtypes = (F32, F32, BF16, BF16, BF16, BF16)
    return _call(
        body, name="s5_params_fwd",
        in_specs=[_full(s) for s in _S5_PARAM_SHAPES], out_specs=[_full(s) for s in _S5_OPERAND_SHAPES],
        out_shape=[_sds(s, d) for s, d in zip(_S5_OPERAND_SHAPES, dtypes)], compiler_params=_params(32),
    )(*params)


_BC_SIDE_BY_SIDE = (D_SSM, 4 * SSM_STATE)


def _s5_params_bwd(params, cotangents):
    def body(*refs):
        ins, (dlre, dlim, dbtre, dbtim, dcmre, dcmim), outs = refs[:7], refs[7:13], refs[13:]
        _, vjp = jax.vjp(_s5_operands, *[r[...] for r in ins])
        cts = (dlre[...], dlim[...], dbtre[0], dbtre[1], dbtim[0], dbtim[1], dcmre[0], dcmre[1], dcmim[0], dcmim[1])
        grads = vjp(cts)
        for ref, val in zip(outs[:3], grads[:3]):
            ref[...] = val
        outs[3][...] = jnp.concatenate(grads[3:], axis=1).astype(BF16)

    out_shapes = _S5_PARAM_SHAPES[:3] + (_BC_SIDE_BY_SIDE,)
    return _call(
        body, name="s5_params_bwd",
        in_specs=[_full(s) for s in _S5_PARAM_SHAPES + _S5_OPERAND_SHAPES],
        out_specs=[_full(s) for s in out_shapes],
        out_shape=[_sds(s, d) for s, d in zip(out_shapes, (F32, F32, F32, BF16))], compiler_params=_params(48),
    )(*params, *cotangents)


def _scan_geometry(n_seq, seq):
    slab = n_seq * SCAN_CHUNKS
    steps = seq // SCAN_CHUNKS
    tile_rows = slab * SCAN_TILE_STEPS
    n_tiles = steps // SCAN_TILE_STEPS
    return slab, steps, tile_rows, n_tiles


_SCAN_PARTS = D_SSM // LANES


def _whole_parts(rows):
    return [_full((rows, LANES))] * _SCAN_PARTS


def _part_shapes(rows):
    return [_sds((rows, LANES))] * _SCAN_PARTS


def _load_chunks(parts, first_chunk, n_chunks, steps, slab):
    return jnp.concatenate([
        jnp.concatenate([ref[pl.ds(first_chunk + q, steps, stride=slab), :] for ref in parts], axis=1)
        for q in range(n_chunks)], axis=0)


def _store_chunks(parts, first_chunk, value, steps, slab):
    for q in range(value.shape[0] // steps):
        for j, ref in enumerate(parts):
            ref[pl.ds(first_chunk + q, steps, stride=slab), :] = value[q * steps:(q + 1) * steps,
                                                                     j * LANES:(j + 1) * LANES]


def _join_parts(parts):
    return jnp.concatenate([ref[...] for ref in parts], axis=1)


def _split_parts(parts, value):
    for j, ref in enumerate(parts):
        ref[...] = value[:, j * LANES:(j + 1) * LANES]


def _complex_power(re, im, n):
    out = None
    while n:
        if n & 1:
            out = (re, im) if out is None else (out[0] * re - out[1] * im, out[0] * im + out[1] * re)
        n >>= 1
        if n:
            re, im = re * re - im * im, 2.0 * re * im
    return out


def _chunk_carry(sum_re, sum_im, carry_re, carry_im, a_re, a_im, n_seq, reverse):
    carry_re[...] = jnp.zeros_like(carry_re)
    carry_im[...] = jnp.zeros_like(carry_im)
    for s in range(n_seq):
        order = range(SCAN_CHUNKS - 2, -1, -1) if reverse else range(1, SCAN_CHUNKS)
        for c in order:
            r = s * SCAN_CHUNKS + c
            p = r + 1 if reverse else r - 1
            p_re, p_im = carry_re[p:p + 1, :], carry_im[p:p + 1, :]
            carry_re[r:r + 1, :] = a_re * p_re - a_im * p_im + sum_re[p:p + 1, :]
            carry_im[r:r + 1, :] = a_re * p_im + a_im * p_re + sum_im[p:p + 1, :]


def _s5_scan_fwd(u_parts, bt_re, bt_im, cm_re, cm_im, lbar_re, lbar_im, d_row, n_seq, seq):
    slab, steps, tile_rows, n_tiles = _scan_geometry(n_seq, seq)
    rows = u_parts[0].shape[0]

    def body(*refs):
        u_refs, refs = refs[:_SCAN_PARTS], refs[_SCAN_PARTS:]
        (bre_ref, bim_ref, cre_ref, cim_ref, lre_ref, lim_ref, d_ref), refs = refs[:7], refs[7:]
        y_refs, (hre_ref, him_ref, st_re, st_im, h0_re, h0_im, buf_re, buf_im) = refs[:_SCAN_PARTS], refs[_SCAN_PARTS:]
        second = pl.program_id(0) == 1
        i = pl.program_id(1)

        @pl.when(jnp.logical_and(i == 0, jnp.logical_not(second)))
        def _():
            st_re[...] = jnp.zeros_like(st_re)
            st_im[...] = jnp.zeros_like(st_im)

        u = _join_parts(u_refs)
        ub = u.astype(BF16)
        for hf in range(2):
            cols = slice(hf * 1024, (hf + 1) * 1024)
            buf_re[:, cols] = _dot(ub[:, hf * 256:(hf + 1) * 256], bre_ref[hf])
            buf_im[:, cols] = _dot(ub[:, hf * 256:(hf + 1) * 256], bim_ref[hf])

        for lc in range(SSM_LANES // SCAN_LANE_CHUNK):
            cols = slice(lc * SCAN_LANE_CHUNK, (lc + 1) * SCAN_LANE_CHUNK)
            l_re = jnp.broadcast_to(lre_ref[:, cols], (slab, SCAN_LANE_CHUNK))
            l_im = jnp.broadcast_to(lim_ref[:, cols], (slab, SCAN_LANE_CHUNK))

            def scan_tile(keep_states):
                def step(t, carry):
                    s_re, s_im = carry
                    r0 = pl.multiple_of(t * slab, slab)
                    n_re = l_re * s_re - l_im * s_im + buf_re[pl.ds(r0, slab), cols]
                    n_im = l_re * s_im + l_im * s_re + buf_im[pl.ds(r0, slab), cols]
                    if keep_states:
                        buf_re[pl.ds(r0, slab), cols] = n_re
                        buf_im[pl.ds(r0, slab), cols] = n_im
                    return n_re, n_im

                s_re, s_im = lax.fori_loop(0, SCAN_TILE_STEPS, step, (st_re[:, cols], st_im[:, cols]), unroll=True)
                st_re[:, cols] = s_re
                st_im[:, cols] = s_im

            pl.when(jnp.logical_not(second))(lambda: scan_tile(False))
            pl.when(second)(lambda: scan_tile(True))

        @pl.when(jnp.logical_and(i == n_tiles - 1, jnp.logical_not(second)))
        def _():
            a_re, a_im = _complex_power(lre_ref[...], lim_ref[...], steps)
            _chunk_carry(st_re, st_im, h0_re, h0_im, a_re, a_im, n_seq, reverse=False)
            st_re[...] = h0_re[...]
            st_im[...] = h0_im[...]

        @pl.when(second)
        def _():
            h_re = buf_re[...].astype(BF16)
            h_im = buf_im[...].astype(BF16)
            hre_ref[...] = h_re
            him_ref[...] = h_im
            for hf in range(2):
                cols = slice(hf * 1024, (hf + 1) * 1024)
                ycols = slice(hf * 256, (hf + 1) * 256)
                y_half = (_dot_nt(h_re[:, cols], cre_ref[hf]) - _dot_nt(h_im[:, cols], cim_ref[hf])
                          + d_ref[:, ycols] * u[:, ycols])
                _split_parts(y_refs[2 * hf:2 * hf + 2], y_half)

    tile = lambda w: pl.BlockSpec((tile_rows, w), lambda p, i: (i, 0))
    out_tile = lambda w: pl.BlockSpec((tile_rows, w), lambda p, i: (i * p, 0))
    cm = _full(_CM_SHAPE)
    outs = _call(
        body, name="s5_scan_fwd", grid=(2, n_tiles),
        in_specs=[tile(LANES)] * _SCAN_PARTS + [cm, cm, cm, cm, _full((1, SSM_LANES)), _full((1, SSM_LANES)),
                                                _full((1, 512))],
        out_specs=[out_tile(LANES)] * _SCAN_PARTS + [out_tile(SSM_LANES), out_tile(SSM_LANES)],
        out_shape=_part_shapes(rows) + [_sds((rows, SSM_LANES), BF16), _sds((rows, SSM_LANES), BF16)],
        scratch_shapes=[pltpu.VMEM((slab, SSM_LANES), F32)] * 4 + [pltpu.VMEM((tile_rows, SSM_LANES), F32)] * 2,
        compiler_params=_params(40, ("arbitrary", "arbitrary")),
    )(*u_parts, bt_re, bt_im, cm_re, cm_im, lbar_re, lbar_im, d_row)
    return outs[:_SCAN_PARTS], outs[_SCAN_PARTS], outs[_SCAN_PARTS + 1]


def _s5_scan_bwd(dy_parts, u_parts, h_re, h_im, bt_re, bt_im, cm_re, cm_im, lbar_re, lbar_im, d_row, n_seq, seq):
    slab, steps, tile_rows, n_tiles = _scan_geometry(n_seq, seq)
    rows = u_parts[0].shape[0]

    def body(*refs):
        dy_refs, u_refs, refs = refs[:_SCAN_PARTS], refs[_SCAN_PARTS:2 * _SCAN_PARTS], refs[2 * _SCAN_PARTS:]
        (hre_ref, him_ref, bre_ref, bim_ref, cre_ref, cim_ref, lre_ref, lim_ref, d_ref), refs = refs[:9], refs[9:]
        du_refs, refs = refs[:_SCAN_PARTS], refs[_SCAN_PARTS:]
        (dbre_ref, dbim_ref, dcre_ref, dcim_ref, dlre_ref, dlim_ref, dd_ref,
         st_re, st_im, g0_re, g0_im, acc_re, acc_im, buf_re, buf_im) = refs
        second = pl.program_id(0) == 1
        i = pl.program_id(1)

        @pl.when(jnp.logical_and(i == 0, jnp.logical_not(second)))
        def _():
            st_re[...] = jnp.zeros_like(st_re)
            st_im[...] = jnp.zeros_like(st_im)
            acc_re[...] = jnp.zeros_like(acc_re)
            acc_im[...] = jnp.zeros_like(acc_im)
            for ref in (dbre_ref, dbim_ref, dcre_ref, dcim_ref, dd_ref):
                ref[...] = jnp.zeros_like(ref)

        dy = _join_parts(dy_refs)
        dyb = dy.astype(BF16)
        for hf in range(2):
            cols = slice(hf * 1024, (hf + 1) * 1024)
            buf_re[:, cols] = _dot(dyb[:, hf * 256:(hf + 1) * 256], cre_ref[hf])
            buf_im[:, cols] = -_dot(dyb[:, hf * 256:(hf + 1) * 256], cim_ref[hf])

        for lc in range(SSM_LANES // SCAN_LANE_CHUNK):
            cols = slice(lc * SCAN_LANE_CHUNK, (lc + 1) * SCAN_LANE_CHUNK)
            l_re = jnp.broadcast_to(lre_ref[:, cols], (slab, SCAN_LANE_CHUNK))
            l_im = jnp.broadcast_to(lim_ref[:, cols], (slab, SCAN_LANE_CHUNK))

            def advance(r0, s_re, s_im):
                n_re = l_re * s_re + l_im * s_im + buf_re[pl.ds(r0, slab), cols]
                n_im = l_re * s_im - l_im * s_re + buf_im[pl.ds(r0, slab), cols]
                buf_re[pl.ds(r0, slab), cols] = n_re
                buf_im[pl.ds(r0, slab), cols] = n_im
                return n_re, n_im

            def row0(k):
                return pl.multiple_of((SCAN_TILE_STEPS - 1 - k) * slab, slab)

            @pl.when(jnp.logical_not(second))
            def _():
                s_re, s_im = lax.fori_loop(0, SCAN_TILE_STEPS, lambda k, s: advance(row0(k), *s),
                                           (st_re[:, cols], st_im[:, cols]), unroll=True)
                st_re[:, cols] = s_re
                st_im[:, cols] = s_im

            @pl.when(second)
            def _():
                def step(k, carry):
                    s_re, s_im, a_re, a_im = carry
                    r0 = row0(k)
                    hr = hre_ref[pl.ds(r0, slab), cols].astype(F32)
                    hi = him_ref[pl.ds(r0, slab), cols].astype(F32)
                    a_re = a_re + s_re * hr + s_im * hi
                    a_im = a_im + s_im * hr - s_re * hi
                    return advance(r0, s_re, s_im) + (a_re, a_im)

                zero = jnp.zeros((slab, SCAN_LANE_CHUNK), F32)
                s_re, s_im, a_re, a_im = lax.fori_loop(
                    0, SCAN_TILE_STEPS, step, (st_re[:, cols], st_im[:, cols], zero, zero), unroll=True)
                st_re[:, cols] = s_re
                st_im[:, cols] = s_im
                acc_re[:, cols] += a_re
                acc_im[:, cols] += a_im

        @pl.when(jnp.logical_and(i == n_tiles - 1, jnp.logical_not(second)))
        def _():
            p_re, p_im = _complex_power(lre_ref[...], lim_ref[...], steps)
            _chunk_carry(st_re, st_im, g0_re, g0_im, p_re, -p_im, n_seq, reverse=True)
            st_re[...] = g0_re[...]
            st_im[...] = g0_im[...]

        @pl.when(second)
        def _():
            u = _join_parts(u_refs)
            ub = u.astype(BF16)
            g_re = buf_re[...].astype(BF16)
            g_im = buf_im[...].astype(BF16)
            dd_ref[...] += jnp.sum(dy * u, axis=0, keepdims=True)
            for hf in range(2):
                cols = slice(hf * 1024, (hf + 1) * 1024)
                ycols = slice(hf * 256, (hf + 1) * 256)
                du_half = (_dot_nt(g_re[:, cols], bre_ref[hf]) + _dot_nt(g_im[:, cols], bim_ref[hf])
                           + d_ref[:, ycols] * dy[:, ycols])
                _split_parts(du_refs[2 * hf:2 * hf + 2], du_half)
                for q4 in range(_HALF_GROUPS // 4):
                    ch = slice(hf * 256 + q4 * 64, hf * 256 + (q4 + 1) * 64)
                    st = slice(hf * 1024 + q4 * 256, hf * 1024 + (q4 + 1) * 256)
                    blk = (hf, slice(q4 * 64, (q4 + 1) * 64), slice(q4 * 256, (q4 + 1) * 256))
                    dbre_ref[blk] += _dot_tn(ub[:, ch], g_re[:, st])
                    dbim_ref[blk] += _dot_tn(ub[:, ch], g_im[:, st])
                    dcre_ref[blk] += _dot_tn(dyb[:, ch], hre_ref[:, st])
                    dcim_ref[blk] -= _dot_tn(dyb[:, ch], him_ref[:, st])

        @pl.when(jnp.logical_and(i == n_tiles - 1, second))
        def _():
            dlre_ref[...] = jnp.sum(acc_re[...], axis=0, keepdims=True)
            dlim_ref[...] = jnp.sum(acc_im[...], axis=0, keepdims=True)

    tile = lambda w: pl.BlockSpec((tile_rows, w), lambda p, i: (n_tiles - 1 - i, 0))
    second_tile = lambda w: pl.BlockSpec((tile_rows, w), lambda p, i: (n_tiles - 1 - i * p, 0))
    cm = _full(_CM_SHAPE)
    row = _full((1, SSM_LANES))
    outs = _call(
        body, name="s5_scan_bwd", grid=(2, n_tiles),
        in_specs=[tile(LANES)] * _SCAN_PARTS + [second_tile(LANES)] * _SCAN_PARTS
        + [second_tile(SSM_LANES), second_tile(SSM_LANES), cm, cm, cm, cm, row, row, _full((1, 512))],
        out_specs=[second_tile(LANES)] * _SCAN_PARTS + [cm, cm, cm, cm, row, row, _full((1, 512))],
        out_shape=(_part_shapes(rows) + [_sds(_CM_SHAPE)] * 4 + [_sds((1, SSM_LANES))] * 2 + [_sds((1, 512))]),
        scratch_shapes=[pltpu.VMEM((slab, SSM_LANES), F32)] * 6 + [pltpu.VMEM((tile_rows, SSM_LANES), F32)] * 2,
        compiler_params=_params(48, ("arbitrary", "arbitrary")),
    )(*dy_parts, *u_parts, h_re, h_im, bt_re, bt_im, cm_re, cm_im, lbar_re, lbar_im, d_row)
    return (outs[:_SCAN_PARTS],) + tuple(outs[_SCAN_PARTS:])


def _glu_gate(gl, a, zs):
    return gl * jax.nn.sigmoid(a) * _silu(zs)


def _glu_fwd(y_parts, zs, w_glu, b_glu, n_seq, seq):
    rows = zs.shape[0]
    tm = 512
    slab, steps, _, _ = _scan_geometry(n_seq, seq)

    def body(*refs):
        y_refs, (zs_ref, w_ref, b_ref, o_ref) = refs[:_SCAN_PARTS], refs[_SCAN_PARTS:]
        y = _load_chunks(y_refs, pl.program_id(0) * (tm // steps), tm // steps, steps, slab)
        gl = jax.nn.gelu(y)
        a = _dot(gl.astype(BF16), w_ref[...]) + b_ref[...]
        o_ref[...] = _glu_gate(gl, a, zs_ref[...]).astype(BF16)

    return _call(
        body, name="glu_fwd", grid=(rows // tm,),
        in_specs=_whole_parts(rows) + [_rows(tm, 512), _full((512, 512)), _full((1, 512))],
        out_specs=_rows(tm, 512), out_shape=_sds((rows, 512), BF16),
        compiler_params=_params(32, ("arbitrary",)),
    )(*y_parts, zs, w_glu, b_glu)


def _glu_bwd(y_parts, zs, d_out, w_glu, b_glu, n_seq, seq):
    rows = zs.shape[0]
    tm = 512
    slab, steps, _, _ = _scan_geometry(n_seq, seq)

    def body(*refs):
        y_refs, (zs_ref, d_ref, w_ref, b_ref), refs = refs[:_SCAN_PARTS], refs[_SCAN_PARTS:_SCAN_PARTS + 4], refs[_SCAN_PARTS + 4:]
        dy_refs, (dzs_ref, dw_ref, db_ref) = refs[:_SCAN_PARTS], refs[_SCAN_PARTS:]
        first_chunk = pl.program_id(0) * (tm // steps)

        @pl.when(pl.program_id(0) == 0)
        def _():
            dw_ref[...] = jnp.zeros_like(dw_ref)
            db_ref[...] = jnp.zeros_like(db_ref)

        gl, gelu_vjp = jax.vjp(jax.nn.gelu, _load_chunks(y_refs, first_chunk, tm // steps, steps, slab))
        glb = gl.astype(BF16)
        a = _dot(glb, w_ref[...]) + b_ref[...]
        _, gate_vjp = jax.vjp(_glu_gate, gl, a, zs_ref[...])
        d_gl, d_a, d_zs = gate_vjp(d_ref[...])
        dab = d_a.astype(BF16)
        d_gl = d_gl + _dot_nt(dab, w_ref[...])
        _store_chunks(dy_refs, first_chunk, gelu_vjp(d_gl)[0], steps, slab)
        dzs_ref[...] = d_zs.astype(BF16)
        dw_ref[...] += _dot_tn(glb, dab)
        db_ref[...] += jnp.sum(d_a, axis=0, keepdims=True)

    *dy_parts, dzs, dw, db = _call(
        body, name="glu_bwd", grid=(rows // tm,),
        in_specs=_whole_parts(rows) + [_rows(tm, 512), _rows(tm, 512), _full((512, 512)), _full((1, 512))],
        out_specs=_whole_parts(rows) + [_rows(tm, 512), _full((512, 512)), _full((1, 512))],
        out_shape=_part_shapes(rows) + [_sds((rows, 512), BF16), _sds((512, 512)), _sds((1, 512))],
        compiler_params=_params(40, ("arbitrary",)),
    )(*y_parts, zs, d_out, w_glu, b_glu)
    return dy_parts, dzs, dw, db


_GROUP_ROWS = Q_PER_KV * BLOCK
_BLOCK_SHIFT = BLOCK.bit_length() - 1


def _attn_bias(j):
    query = _iota((BLOCK, _GROUP_ROWS), 1)
    dist_cur = (query & (BLOCK - 1)) - _iota((BLOCK, _GROUP_ROWS), 0)
    dist_prev = dist_cur + BLOCK
    head = query >> _BLOCK_SHIFT
    slope = jnp.zeros((BLOCK, _GROUP_ROWS), F32)
    for g in range(Q_PER_KV):
        slope = jnp.where(head == g, 2.0 ** (-(j * Q_PER_KV + g + 1)), slope)
    bias_cur = jnp.where(dist_cur >= 0, -slope * dist_cur.astype(F32), -jnp.inf)
    bias_prev = jnp.where(dist_prev < WINDOW, -slope * dist_prev.astype(F32), -jnp.inf)
    return bias_cur, bias_prev


_ATTN_BIAS_SCRATCH = pltpu.VMEM((KV_HEADS, 2, BLOCK, _GROUP_ROWS), F32)


def _fill_attn_bias(bias_ref):
    @pl.when(jnp.logical_and(pl.program_id(0) == 0, pl.program_id(1) == 0))
    def _():
        for j in range(KV_HEADS):
            bias_ref[j, 0], bias_ref[j, 1] = _attn_bias(j)


def _stack_heads(x, j):
    heads = range(j * Q_PER_KV, (j + 1) * Q_PER_KV)
    return jnp.concatenate([x[:, h * HEAD_DIM:(h + 1) * HEAD_DIM] for h in heads], axis=0)


def _head_rows(x, j):
    heads = range(j * Q_PER_KV, (j + 1) * Q_PER_KV)
    return jnp.concatenate([x[h:h + 1, :] for h in heads], axis=1)


def _sink_row(sk_ref, j):
    heads = range(j * Q_PER_KV, (j + 1) * Q_PER_KV)
    return jnp.concatenate([jnp.broadcast_to(sk_ref[0:1, h:h + 1], (1, BLOCK)) for h in heads], axis=1)


def _attn_fwd(q, k, v, za, sinks, n_seq, seq):
    nb = seq // BLOCK
    rows = q.shape[0]

    def body(q_ref, kc_ref, kp_ref, vc_ref, vp_ref, za_ref, sk_ref, o_ref, ao_ref, lse_ref, bias_ref):
        _fill_attn_bias(bias_ref)
        has_prev = pl.program_id(1) > 0
        q_all = q_ref[...]
        for j in range(KV_HEADS):
            js = slice(j * HEAD_DIM, (j + 1) * HEAD_DIM)
            bias_c, bias_p = bias_ref[j, 0], bias_ref[j, 1]
            q4 = _stack_heads(q_all, j)
            sc = _dot_nt(kc_ref[:, js], q4) + bias_c
            sp = _dot_nt(kp_ref[:, js], q4) + jnp.where(has_prev, bias_p, -jnp.inf)
            sink = _sink_row(sk_ref, j)
            m = jnp.maximum(jnp.max(jnp.maximum(sc, sp), axis=0, keepdims=True), sink)
            ec = jnp.exp(sc - m)
            ep = jnp.exp(sp - m)
            den = jnp.sum(ec + ep, axis=0, keepdims=True) + jnp.exp(sink - m)
            inv = 1.0 / den
            o4 = _dot_tn((ec * inv).astype(BF16), vc_ref[:, js]) + _dot_tn((ep * inv).astype(BF16), vp_ref[:, js])
            lse4 = m + jnp.log(den)
            for g in range(Q_PER_KV):
                h = j * Q_PER_KV + g
                o_ref[:, h * HEAD_DIM:(h + 1) * HEAD_DIM] = o4[g * BLOCK:(g + 1) * BLOCK]
                lse_ref[h:h + 1, :] = lse4[:, g * BLOCK:(g + 1) * BLOCK]
        ao_ref[...] = (o_ref[...] * _silu(za_ref[...])).astype(BF16)

    cur = lambda w: pl.BlockSpec((BLOCK, w), lambda b, n: (b * nb + n, 0))
    prev = lambda w: pl.BlockSpec((BLOCK, w), lambda b, n: (b * nb + jnp.maximum(n - 1, 0), 0))
    lse_rows = rows // BLOCK * N_HEADS
    return _call(
        body, name="attn_fwd", grid=(n_seq, nb),
        in_specs=[cur(512), cur(128), prev(128), cur(128), prev(128), cur(512), _full((1, N_HEADS))],
        out_specs=[cur(512), cur(512), pl.BlockSpec((N_HEADS, BLOCK), lambda b, n: (b * nb + n, 0))],
        out_shape=[_sds((rows, 512)), _sds((rows, 512), BF16), _sds((lse_rows, BLOCK))],
        scratch_shapes=[_ATTN_BIAS_SCRATCH], compiler_params=_params(32, ("arbitrary", "arbitrary")),
    )(q, k, k, v, v, za, sinks)


def _attn_bwd(q, k, v, za, o, lse, d_ao, sinks, n_seq, seq):
    nb = seq // BLOCK
    rows = q.shape[0]

    def body(q_ref, kc_ref, kp_ref, vc_ref, vp_ref, za_ref, o_ref, lse_ref, d_ref, sk_ref,
             dq_ref, dk_ref, dv_ref, dza_ref, dsk_ref, bias_ref, dk_carry, dv_carry):
        n = nb - 1 - pl.program_id(1)
        _fill_attn_bias(bias_ref)

        @pl.when(jnp.logical_and(pl.program_id(0) == 0, pl.program_id(1) == 0))
        def _():
            dsk_ref[...] = jnp.zeros_like(dsk_ref)
            dk_carry[...] = jnp.zeros_like(dk_carry)
            dv_carry[...] = jnp.zeros_like(dv_carry)

        has_prev = n > 0
        has_next = n + 1 < nb

        _, gate_vjp = jax.vjp(lambda o_, z_: o_ * _silu(z_), o_ref[...], za_ref[...])
        d_o, d_za = gate_vjp(d_ref[...])
        dza_ref[...] = d_za.astype(BF16)
        q_all = q_ref[...]
        lse_all = lse_ref[...]

        for j in range(KV_HEADS):
            js = slice(j * HEAD_DIM, (j + 1) * HEAD_DIM)
            kc, kp, vc, vp = kc_ref[:, js], kp_ref[:, js], vc_ref[:, js], vp_ref[:, js]
            bias_c, bias_p = bias_ref[j, 0], bias_ref[j, 1]
            q4 = _stack_heads(q_all, j)
            do4b = _stack_heads(d_o, j).astype(BF16)
            lse4 = _head_rows(lse_all, j)
            pc = jnp.exp(_dot_nt(kc, q4) + bias_c - lse4)
            pp = jnp.exp(_dot_nt(kp, q4) + jnp.where(has_prev, bias_p, -jnp.inf) - lse4)
            dpc = _dot_nt(vc, do4b)
            dpp = _dot_nt(vp, do4b)
            delta = jnp.sum(pc * dpc + pp * dpp, axis=0, keepdims=True)
            dsc = (pc * (dpc - delta)).astype(BF16)
            dsp = (pp * (dpp - delta)).astype(BF16)
            dq4 = ((_dot_tn(dsc, kc) + _dot_tn(dsp, kp)) * ATTN_SCALE).astype(BF16)
            sink_loss = jnp.exp(_sink_row(sk_ref, j) - lse4) * delta
            for g in range(Q_PER_KV):
                h = j * Q_PER_KV + g
                dq_ref[:, h * HEAD_DIM:(h + 1) * HEAD_DIM] = dq4[g * BLOCK:(g + 1) * BLOCK]
                dsk_ref[0:1, h:h + 1] -= jnp.sum(sink_loss[:, g * BLOCK:(g + 1) * BLOCK], axis=1, keepdims=True)
            dk = _dot(dsc, q4) + jnp.where(has_next, dk_carry[j], 0.0)
            dv = _dot(pc.astype(BF16), do4b) + jnp.where(has_next, dv_carry[j], 0.0)
            dk_carry[j] = _dot(dsp, q4)
            dv_carry[j] = _dot(pp.astype(BF16), do4b)
            dk_ref[:, js] = dk.astype(BF16)
            dv_ref[:, js] = dv.astype(BF16)

    cur = lambda w: pl.BlockSpec((BLOCK, w), lambda b, s: (b * nb + nb - 1 - s, 0))
    prev = lambda w: pl.BlockSpec((BLOCK, w), lambda b, s: (b * nb + jnp.maximum(nb - 2 - s, 0), 0))
    return _call(
        body, name="attn_bwd", grid=(n_seq, nb),
        in_specs=[cur(512), cur(128), prev(128), cur(128), prev(128), cur(512), cur(512),
                  pl.BlockSpec((N_HEADS, BLOCK), lambda b, s: (b * nb + nb - 1 - s, 0)), cur(512), _full((1, N_HEADS))],
        out_specs=[cur(512), cur(128), cur(128), cur(512), _full((1, N_HEADS))],
        out_shape=[_sds((rows, 512), BF16), _sds((rows, 128), BF16), _sds((rows, 128), BF16),
                   _sds((rows, 512), BF16), _sds((1, N_HEADS))],
        scratch_shapes=[_ATTN_BIAS_SCRATCH, pltpu.VMEM((KV_HEADS, BLOCK, HEAD_DIM), F32),
                        pltpu.VMEM((KV_HEADS, BLOCK, HEAD_DIM), F32)],
        compiler_params=_params(32, ("arbitrary", "arbitrary")),
    )(q, k, k, v, v, za, o, lse, d_ao, sinks)


def _tail(ssm_out, attn_out, x2d, p2d, target, w_out, g2, w_gate, b_gate, w_proj):
    rows = x2d.shape[0]
    tm = 512

    def body(so_ref, ao_ref, x_ref, p_ref, t_ref, wo_ref, g2_ref, wg_ref, bg_ref, wp_ref,
             dh1_ref, dso_ref, dao_ref, dwo_ref, dwg_ref, dwp_ref, dbg_ref, dg2_ref, loss_ref):
        @pl.when(pl.program_id(0) == 0)
        def _():
            for ref in (dwo_ref, dwg_ref, dwp_ref, dbg_ref, dg2_ref, loss_ref):
                ref[...] = jnp.zeros_like(ref)

        cat = jnp.concatenate([so_ref[...], ao_ref[...]], axis=1)
        g2 = g2_ref[...]
        mixed = _dot(cat, wo_ref[...])
        r = lax.rsqrt(jnp.mean(mixed * mixed, axis=-1, keepdims=True) + EPS)
        mr = mixed * r
        h1 = x_ref[...] + mr * g2
        h1b = h1.astype(BF16)
        gate = jax.nn.sigmoid(_dot(h1b, wg_ref[...]) + bg_ref[...])
        pb = p_ref[...].astype(BF16)
        wp_blocks = [slice(j * D_PLE, (j + 1) * D_PLE) for j in range(N_CHIPS)]
        pp = jnp.concatenate([_dot(pb, wp_ref[blk, :]) for blk in wp_blocks], axis=1)
        err = h1 + gate * pp - t_ref[...]
        loss_ref[...] += 0.5 * jnp.sum(jnp.mean(err * err, axis=-1, keepdims=True), axis=0, keepdims=True)

        dh2 = err * (1.0 / D_MODEL)
        d_glin = dh2 * pp * gate * (1.0 - gate)
        d_glin_b = d_glin.astype(BF16)
        dwg_ref[...] += _dot_tn(h1b, d_glin_b)
        dbg_ref[...] += jnp.sum(d_glin, axis=0, keepdims=True)
        d_pp = (dh2 * gate).astype(BF16)
        for blk in wp_blocks:
            dwp_ref[blk, :] += _dot_tn(pb, d_pp[:, blk])
        dh1 = dh2 + _dot_nt(d_glin_b, wg_ref[...])
        dh1_ref[...] = dh1
        dg2_ref[...] += jnp.sum(dh1 * mr, axis=0, keepdims=True)
        a_ = dh1 * g2
        d_mixed = (r * a_ - mr * (r * jnp.mean(a_ * mr, axis=-1, keepdims=True))).astype(BF16)
        dwo_ref[...] += _dot_tn(cat, d_mixed)
        d_cat = _dot_nt(d_mixed, wo_ref[...])
        dso_ref[...] = d_cat[:, 0:512]
        dao_ref[...] = d_cat[:, 512:1024]

    return _call(
        body, name="tail_fwd_bwd", grid=(rows // tm,),
        in_specs=[_rows(tm, 512), _rows(tm, 512), _rows(tm, D_MODEL), _rows(tm, D_PLE), _rows(tm, D_MODEL),
                  _full((D_MODEL, D_MODEL)), _full((1, D_MODEL)), _full((D_MODEL, D_MODEL)), _full((1, D_MODEL)),
                  _full((N_CHIPS * D_PLE, D_PLE))],
        out_specs=[_rows(tm, D_MODEL), _rows(tm, 512), _rows(tm, 512), _full((D_MODEL, D_MODEL)),
                   _full((D_MODEL, D_MODEL)), _full((N_CHIPS * D_PLE, D_PLE)), _full((1, D_MODEL)), _full((1, D_MODEL)),
                   _full((1, 1))],
        out_shape=[_sds((rows, D_MODEL)), _sds((rows, 512)), _sds((rows, 512)), _sds((D_MODEL, D_MODEL)),
                   _sds((D_MODEL, D_MODEL)), _sds((N_CHIPS * D_PLE, D_PLE)), _sds((1, D_MODEL)), _sds((1, D_MODEL)),
                   _sds((1, 1))],
        compiler_params=_params(52, ("arbitrary",)),
    )(ssm_out, attn_out, x2d, p2d, target, w_out, g2, w_gate, b_gate, w_proj)


def _local_step(x, hn, p, target, pre_norm_g, w_in_t, s5_params, s5_operands, ssm_d, w_glu, b_glu, sinks, w_out,
                post_norm_g, w_proj, w_gate, b_gate, send_beside=lambda ready: ready["ssm_bc"]):
    n_seq, seq, _ = x.shape
    rows = n_seq * seq
    x2d = x.reshape(rows, D_MODEL)
    p2d = p.reshape(rows, D_PLE)
    t2d = target.reshape(rows, D_MODEL)

    l_re, l_im, bt_re, bt_im, cm_re, cm_im = s5_operands

    u_scan, zs, q, k, v, za = _in_proj(hn, w_in_t, n_seq, seq)
    y_scan, h_re, h_im = _s5_scan_fwd(u_scan, bt_re, bt_im, cm_re, cm_im, l_re, l_im, ssm_d, n_seq, seq)
    ssm_out = _glu_fwd(y_scan, zs, w_glu, b_glu, n_seq, seq)
    o, attn_out, lse = _attn_fwd(q, k, v, za, sinks, n_seq, seq)

    dh1, d_so, d_ao, d_w_out, d_w_gate, d_w_proj, d_b_gate, d_g2, loss = _tail(
        ssm_out, attn_out, x2d, p2d, t2d, w_out, post_norm_g, w_gate, b_gate, w_proj)

    dq, dk, dv, dza, d_sinks = _attn_bwd(q, k, v, za, o, lse, d_ao, sinks, n_seq, seq)
    dy_scan, dzs, d_w_glu, d_b_glu = _glu_bwd(y_scan, zs, d_so, w_glu, b_glu, n_seq, seq)
    du_scan, d_bt_re, d_bt_im, d_cm_re, d_cm_im, d_l_re, d_l_im, d_d = _s5_scan_bwd(
        dy_scan, u_scan, h_re, h_im, bt_re, bt_im, cm_re, cm_im, l_re, l_im, ssm_d, n_seq, seq)
    d_lam_re, d_lam_im, d_log_step, d_bc = _s5_params_bwd(
        s5_params, (d_l_re, d_l_im, d_bt_re, d_bt_im, d_cm_re, d_cm_im))

    runs_after = send_beside(dict(w_out=d_w_out, pl_w_gate=d_w_gate, pl_w_proj=d_w_proj, ssm_bc=d_bc))
    d_proj, d_w_in_early, d_w_in_early_b = _in_proj_bwd_early(hn, du_scan, dzs, dq, dk, dv, dza, runs_after, n_seq, seq)
    grad_x, d_w_in_late, d_g1 = _in_proj_bwd(x2d, dh1, pre_norm_g, w_in_t, d_proj)
    grads = dict(
        pre_norm_g=d_g1, w_in_early=d_w_in_early, w_in_early_bf16=d_w_in_early_b, w_in_late=d_w_in_late,
        ssm_lam_re=d_lam_re, ssm_lam_im=d_lam_im, ssm_log_step=d_log_step, ssm_bc=d_bc, ssm_d=d_d, ssm_w_glu=d_w_glu,
        ssm_b_glu=d_b_glu, attn_sinks=d_sinks, w_out=d_w_out, post_norm_g=d_g2, pl_w_proj=d_w_proj,
        pl_w_gate=d_w_gate, pl_b_gate=d_b_gate)
    return grad_x.reshape(x.shape), loss, grads


_BIG = ("w_in", "ssm_w_glu", "w_out", "pl_w_proj", "pl_w_gate")
_BIG_SHARD = {"w_in": (D_IN // N_CHIPS, D_MODEL), "ssm_w_glu": (D_SSM // N_CHIPS, D_SSM),
              "w_out": (D_MODEL // N_CHIPS, D_MODEL), "pl_w_proj": (D_PLE, D_MODEL // N_CHIPS),
              "pl_w_gate": (D_MODEL // N_CHIPS, D_MODEL)}
_SMALL = {"pre_norm_g": (1, D_MODEL), "ssm_lam_re": (SSM_GROUPS, SSM_STATE), "ssm_lam_im": (SSM_GROUPS, SSM_STATE),
          "ssm_log_step": (1, SSM_GROUPS), "ssm_b_re": (D_SSM, SSM_STATE), "ssm_b_im": (D_SSM, SSM_STATE),
          "ssm_c_re": (D_SSM, SSM_STATE), "ssm_c_im": (D_SSM, SSM_STATE), "ssm_d": (1, D_SSM), "ssm_b_glu": (1, D_SSM),
          "attn_sinks": (1, N_HEADS), "post_norm_g": (1, D_MODEL), "pl_b_gate": (1, D_MODEL)}
_VEC_ROWS = ("pre_norm_g", "post_norm_g", "pl_b_gate", "ssm_d", "ssm_b_glu", "attn_sinks", "ssm_log_step", "loss")
_SMALL_GROUPS = (
    ("vec", (8, D_MODEL), tuple((name, r) for r, name in enumerate(_VEC_ROWS))),
    ("lam", (2 * SSM_GROUPS, SSM_STATE), (("ssm_lam_re", 0), ("ssm_lam_im", SSM_GROUPS))),
)
_SMALL_EARLY = ("ssm_b_re", "ssm_b_im", "ssm_c_re", "ssm_c_im")
_SMALL_ORDER = tuple(name for _, _, members in _SMALL_GROUPS for name, _ in members) + _SMALL_EARLY
_WEIGHT_ORDER = ("pre_norm_g", "w_in", "ssm_lam_re", "ssm_lam_im", "ssm_log_step", "ssm_b_re", "ssm_b_im", "ssm_c_re",
                 "ssm_c_im", "ssm_d", "ssm_w_glu", "ssm_b_glu", "attn_sinks", "w_out", "post_norm_g", "pl_w_proj",
                 "pl_w_gate", "pl_b_gate")


def _small_shape(name):
    return (1, 1) if name == "loss" else _SMALL[name]


def _to_kernel_form(name, a):
    a = a[0]
    if name == "w_in":
        return a.T
    if name in ("ssm_b_re", "ssm_b_im"):
        a = a.transpose(0, 2, 1)
    return a.reshape(_SMALL[name]) if name in _SMALL else a


def _from_kernel_form(name, a, shape):
    if name == "w_in":
        a = a.T
    if name in ("ssm_b_re", "ssm_b_im"):
        a = a.reshape(SSM_GROUPS, SSM_GROUP_CH, SSM_STATE).transpose(0, 2, 1)
    return a.reshape(shape)


def _mesh_place():
    x, y, c = lax.axis_index("x"), lax.axis_index("y"), lax.axis_index("c")
    other_chips = ((1 - x, y), (x, 1 - y), (1 - x, 1 - y))
    return x, y, c, other_chips


def _gather_copies(s_refs, g_refs, send_sems, recv_sems, local_sems):
    x, y, c, other_chips = _mesh_place()
    started = []
    for i, (s_ref, g_ref) in enumerate(zip(s_refs, g_refs)):
        rows = s_ref.shape[0]
        half = rows // 2

        def block(chip, g_ref=g_ref, rows=rows, half=half):
            return g_ref.at[pl.ds((2 * chip[0] + chip[1]) * rows + c * half, half), :]

        def copy(k, chip, to, src=None, i=i, block=block):
            return pltpu.make_async_remote_copy(
                src_ref=block(chip) if src is None else src, dst_ref=block(chip), send_sem=send_sems.at[6 * i + k],
                recv_sem=recv_sems.at[6 * i + k], device_id=to, device_id_type=MESH)

        own = pltpu.make_async_copy(s_ref, g_ref.at[pl.ds((2 * x + y) * rows, rows), :], local_sems.at[i])
        own.start()
        first = [copy(k, (x, y), (*chip, c), src=s_ref.at[pl.ds(c * half, half), :])
                 for k, chip in enumerate(other_chips)]
        for cp in first:
            cp.start()
        passed = [copy(3 + k, chip, (x, y, 1 - c)) for k, chip in enumerate(other_chips)]
        started.append((own, first, passed))
    for own, first, passed in started:
        for k in range(3):
            first[k].wait_recv()
            passed[k].start()
    for own, first, passed in started:
        for k in range(3):
            passed[k].wait_recv()
        for cp in first + passed:
            cp.wait_send()
        own.wait()


def _gather_semaphores(n_t):
    return [pltpu.SemaphoreType.DMA((6 * n_t,)), pltpu.SemaphoreType.DMA((6 * n_t,)), pltpu.SemaphoreType.DMA((n_t,))]


def _gather_weights_beside(shards, name, collective_id):
    n_t = len(shards)
    hbm = pltpu.MemorySpace.HBM
    s_refs = [jax.new_ref(s, memory_space=hbm) for s in shards]
    g_refs = [jax.empty_ref(jax.ShapeDtypeStruct((N_CHIPS * s.shape[0], s.shape[1]), s.dtype), memory_space=hbm)
              for s in shards]

    def launch(send_sems, recv_sems, local_sems):
        x, y, c, other_chips = _mesh_place()
        peers = [(*chip, c) for chip in other_chips] + [(x, y, 1 - c)]
        barrier = pltpu.get_barrier_semaphore()
        for peer in peers:
            pl.semaphore_signal(barrier, inc=1, device_id=peer, device_id_type=MESH)
        pl.semaphore_wait(barrier, len(peers))
        _gather_copies(s_refs, g_refs, send_sems, recv_sems, local_sems)

    pl.kernel(launch, mesh=plsc.ScalarSubcoreMesh(axis_name="sequencer", num_cores=1), name=name,
              scratch_types=_gather_semaphores(n_t), compiler_params=pltpu.CompilerParams(collective_id=collective_id))()
    return [g[...] for g in g_refs]


_RELATIONS = tuple(((r >> 2) & 1, (r >> 1) & 1, r & 1) for r in range(1, 8))


def _related(place, relation):
    return tuple(1 - a if flip else a for a, flip in zip(place, relation))


def _scatter_beside(mats, name, collective_id):
    hbm = pltpu.MemorySpace.HBM
    src_refs = [jax.new_ref(a, memory_space=hbm) for a in mats]
    land_refs = [jax.empty_ref(jax.ShapeDtypeStruct((7, a.shape[0] // 8, a.shape[1]), a.dtype), memory_space=hbm)
                 for a in mats]

    def launch(send_sems, recv_sems):
        me = (lax.axis_index("x"), lax.axis_index("y"), lax.axis_index("c"))
        peers = [_related(me, rel) for rel in _RELATIONS]
        barrier = pltpu.get_barrier_semaphore()
        for peer in peers:
            pl.semaphore_signal(barrier, inc=1, device_id=peer, device_id_type=MESH)
        pl.semaphore_wait(barrier, len(peers))
        copies = []
        for i, (src, land) in enumerate(zip(src_refs, land_refs)):
            hr = land.shape[1]
            for k, (tx, ty, tc) in enumerate(peers):
                rows = pl.ds((2 * tx + ty) * 2 * hr + tc * hr, hr)
                copies.append(pltpu.make_async_remote_copy(
                    src_ref=src.at[rows, :], dst_ref=land.at[k], send_sem=send_sems.at[7 * i + k],
                    recv_sem=recv_sems.at[7 * i + k], device_id=(tx, ty, tc), device_id_type=MESH))
                copies[-1].start()
        for cp in copies:
            cp.wait()

    n_sems = 7 * len(mats)
    pl.kernel(launch, mesh=plsc.ScalarSubcoreMesh(axis_name="sequencer", num_cores=1), name=name,
              scratch_types=[pltpu.SemaphoreType.DMA((n_sems,)), pltpu.SemaphoreType.DMA((n_sems,))],
              compiler_params=pltpu.CompilerParams(collective_id=collective_id))()
    return [ref[...] for ref in land_refs]


def _broadcast_beside(arrays):
    hbm = pltpu.MemorySpace.HBM
    src_refs = [jax.new_ref(a, memory_space=hbm) for a in arrays]
    land_refs = [jax.empty_ref(jax.ShapeDtypeStruct((len(_RELATIONS),) + a.shape, a.dtype), memory_space=hbm)
                 for a in arrays]

    def launch(send_sems, recv_sems):
        me = (lax.axis_index("x"), lax.axis_index("y"), lax.axis_index("c"))
        peers = [_related(me, rel) for rel in _RELATIONS]
        barrier = pltpu.get_barrier_semaphore()
        for peer in peers:
            pl.semaphore_signal(barrier, inc=1, device_id=peer, device_id_type=MESH)
        pl.semaphore_wait(barrier, len(peers))
        copies = []
        for i, (src, land) in enumerate(zip(src_refs, land_refs)):
            for k, peer in enumerate(peers):
                copies.append(pltpu.make_async_remote_copy(
                    src_ref=src, dst_ref=land.at[k], send_sem=send_sems.at[7 * i + k],
                    recv_sem=recv_sems.at[7 * i + k], device_id=peer, device_id_type=MESH))
                copies[-1].start()
        for cp in copies:
            cp.wait()

    n_sems = 7 * len(arrays)
    pl.kernel(launch, mesh=plsc.ScalarSubcoreMesh(axis_name="sequencer", num_cores=1), name="broadcast_beside",
              scratch_types=[pltpu.SemaphoreType.DMA((n_sems,)), pltpu.SemaphoreType.DMA((n_sems,))],
              compiler_params=pltpu.CompilerParams(collective_id=3))()
    return [ref[...] for ref in land_refs]


def _exchange_grads(big, small, landed, own_bc, landed_bc):
    n_t = len(big)
    n_g = len(_SMALL_GROUPS)
    names = _SMALL_ORDER
    halves = [(b.shape[0] // N_CHIPS // 2, b.shape[1]) for b in big]
    early = sorted(landed)
    late = [i for i in range(n_t) if i not in landed]
    n_sems = 4 * n_g + 7 * len(late) + n_t
    small_sem0, block_sem0 = n_t, n_t + len(names)
    early_sem0 = block_sem0 + N_CHIPS * len(late)
    landed_sem0 = early_sem0 + 2 * len(early)
    sent = [n for n in names if n in small]

    def body(*refs):
        pos = 0

        def take(n):
            nonlocal pos
            pos += n
            return refs[pos - n:pos]

        big_refs, small_refs = take(n_t), dict(zip(sent, take(len(sent))))
        land_refs = dict(zip(early, take(len(early))))
        own_bc_ref, landed_bc_ref = take(2)
        out_refs, small_out_refs = take(n_t), dict(zip(names, take(len(names))))
        per_late = lambda: dict(zip(late, take(len(late))))
        ga, gb, pme, send_b, recv_b = per_late(), per_late(), take(n_t), per_late(), per_late()
        own_e, land_e = dict(zip(early, take(len(early)))), dict(zip(early, take(len(early))))
        own_s, land_s = take(2)
        s_own, s_sib, s_chips, s_pair = take(n_g), take(n_g), take(n_g), take(n_g)
        stage = dict(zip(names, take(len(names))))
        send_sems, recv_sems, local_sems = take(3)
        x, y, c, other_chips = _mesh_place()
        me = 2 * x + y
        sibling = (x, y, 1 - c)
        sem_at = iter(range(n_sems))

        def remote(src, dst, to):
            k = next(sem_at)
            return pltpu.make_async_remote_copy(src_ref=src, dst_ref=dst, send_sem=send_sems.at[k],
                                                recv_sem=recv_sems.at[k], device_id=to, device_id_type=MESH)

        loads = [pltpu.make_async_copy(small_refs[name], stage[name], local_sems.at[small_sem0 + names.index(name)])
                 for name in sent]
        landed_loads = [pltpu.make_async_copy(own_bc_ref, own_s, local_sems.at[landed_sem0]),
                        pltpu.make_async_copy(landed_bc_ref, land_s, local_sems.at[landed_sem0 + 1])]
        for cp in loads + landed_loads:
            cp.start()
        for cp in loads:
            cp.wait()
        small_swaps = []
        for gi, (_, _, members) in enumerate(_SMALL_GROUPS):
            s_own[gi][...] = jnp.zeros_like(s_own[gi])
            for name, r0 in members:
                r, n = _small_shape(name)
                s_own[gi][r0:r0 + r, 0:n] = stage[name][...]
            small_swaps.append(remote(s_own[gi], s_sib[gi], sibling))
            small_swaps[gi].start()
        order = sorted(late, key=lambda i: halves[i][0] * halves[i][1])
        own_loads, big_swaps = {}, {}
        for i in order:
            hr = halves[i][0]
            own_loads[i], big_swaps[i] = [], []
            for j in range(N_CHIPS):
                mine = big_refs[i].at[pl.ds(j * 2 * hr + c * hr, hr), :]
                theirs = big_refs[i].at[pl.ds(j * 2 * hr + (1 - c) * hr, hr), :]
                sem = local_sems.at[block_sem0 + N_CHIPS * late.index(i) + j]
                own_loads[i].append(pltpu.make_async_copy(mine, ga[i].at[j], sem))
                own_loads[i][j].start()
                big_swaps[i].append(remote(theirs, gb[i].at[j], sibling))
                big_swaps[i][j].start()
        early_loads = {}
        for e, i in enumerate(early):
            hr = halves[i][0]
            mine = big_refs[i].at[pl.ds(me * 2 * hr + c * hr, hr), :]
            early_loads[i] = [pltpu.make_async_copy(mine, own_e[i], local_sems.at[early_sem0 + 2 * e]),
                              pltpu.make_async_copy(land_refs[i], land_e[i], local_sems.at[early_sem0 + 2 * e + 1])]
            for cp in early_loads[i]:
                cp.start()
        small_sends = []
        for gi in range(n_g):
            small_swaps[gi].wait_recv()
            s_pair[gi][...] = s_own[gi][...] + s_sib[gi][...]
            small_sends.append([remote(s_pair[gi], s_chips[gi].at[k], (*chip, c)) for k, chip in enumerate(other_chips)])
            for cp in small_sends[gi]:
                cp.start()

        def pair_sum(i, j):
            return ga[i][j] + gb[i][j]

        big_sends = {}
        for i in order:
            for j in range(N_CHIPS):
                own_loads[i][j].wait()
                big_swaps[i][j].wait_recv()
            big_sends[i] = []
            for k, chip in enumerate(other_chips):
                send_b[i][k] = pair_sum(i, 2 * chip[0] + chip[1]).astype(BF16)
                big_sends[i].append(remote(send_b[i].at[k], recv_b[i].at[k], (*chip, c)))
                big_sends[i][k].start()
        last_swaps, keeps = {}, {}
        for i in early + order:
            hr = halves[i][0]
            if i in landed:
                for cp in early_loads[i]:
                    cp.wait()
                total = own_e[i][...]
                for k in range(len(_RELATIONS)):
                    total = total + land_e[i][k].astype(F32)
                pme[i][...] = total
            else:
                for k in range(3):
                    big_sends[i][k].wait_recv()
                pme[i][...] = ((pair_sum(i, me) + recv_b[i][0].astype(F32)) + recv_b[i][1].astype(F32)) + recv_b[i][2].astype(F32)
            mine = out_refs[i].at[pl.ds(c * hr, hr), :]
            keeps[i] = pltpu.make_async_copy(pme[i], mine, local_sems.at[i])
            keeps[i].start()
            last_swaps[i] = remote(pme[i], mine, sibling)
            last_swaps[i].start()

        for gi, (_, _, members) in enumerate(_SMALL_GROUPS):
            for k in range(3):
                small_sends[gi][k].wait_recv()
            total = None
            for j in range(N_CHIPS):
                rel = jnp.bitwise_xor(j, me)
                term = jnp.where(rel == 0, s_pair[gi][...], jnp.where(
                    rel == 2, s_chips[gi][0], jnp.where(rel == 1, s_chips[gi][1], s_chips[gi][2])))
                total = term if total is None else total + term
            s_sib[gi][...] = total
            for name, r0 in members:
                r, n = _small_shape(name)
                stage[name][...] = s_sib[gi][r0:r0 + r, 0:n]
        my_index = 4 * x + 2 * y + c
        for cp in landed_loads:
            cp.wait()
        total = None
        for d in range(2 * N_CHIPS):
            rel = jnp.bitwise_xor(d, my_index)
            term = own_s[...]
            for k in range(len(_RELATIONS)):
                term = jnp.where(rel == k + 1, land_s[k], term)
            total = term.astype(F32) if total is None else total + term.astype(F32)
        for a, name in enumerate(_SMALL_EARLY):
            stage[name][...] = total[:, a * SSM_STATE:(a + 1) * SSM_STATE]
        stores = [pltpu.make_async_copy(stage[name], small_out_refs[name], local_sems.at[small_sem0 + a])
                  for a, name in enumerate(names)]
        for cp in stores:
            cp.start()

        for i in range(n_t):
            last_swaps[i].wait_recv()
            keeps[i].wait()
        for cp in stores:
            cp.wait()
        groups = list(big_swaps.values()) + small_sends + list(big_sends.values())
        for cp in small_swaps + [cp for group in groups for cp in group] + list(last_swaps.values()):
            cp.wait_send()

    any_spec = pl.BlockSpec(memory_space=pl.ANY)
    small_shapes = [_sds(_small_shape(n)) for n in names]
    group_shapes = [shape for _, shape, _ in _SMALL_GROUPS]
    vmem = lambda which, dtype, lead=(): [pltpu.VMEM(lead + halves[i], dtype) for i in which]
    outs = _call(
        body, name="exchange_grads",
        in_specs=[any_spec] * (n_t + len(sent) + len(early) + 2),
        out_specs=[any_spec] * (n_t + len(names)),
        out_shape=[_sds((b.shape[0] // N_CHIPS, b.shape[1])) for b in big] + small_shapes,
        scratch_shapes=(vmem(late, F32, (N_CHIPS,)) + vmem(late, F32, (N_CHIPS,)) + vmem(range(n_t), F32)
                        + vmem(late, BF16, (3,)) + vmem(late, BF16, (3,))
                        + vmem(early, F32)
                        + [pltpu.VMEM((len(_RELATIONS),) + halves[i], landed[i].dtype) for i in early]
                        + [pltpu.VMEM(own_bc.shape, own_bc.dtype), pltpu.VMEM(landed_bc.shape, landed_bc.dtype)]
                        + [pltpu.VMEM(s, F32) for s in group_shapes] * 2 + [pltpu.VMEM((3,) + s, F32) for s in group_shapes]
                        + [pltpu.VMEM(s, F32) for s in group_shapes]
                        + [pltpu.VMEM(_small_shape(n), F32) for n in names]
                        + [pltpu.SemaphoreType.DMA((n_sems,)), pltpu.SemaphoreType.DMA((n_sems,)),
                           pltpu.SemaphoreType.DMA((landed_sem0 + 2,))]),
        compiler_params=_params(48),
    )(*big, *[small[n] for n in sent], *[landed[i] for i in early], own_bc, landed_bc)
    return list(outs[:n_t]), dict(zip(names, outs[n_t:n_t + len(names)]))


def _adamw_update(w, g, m, v):
    m = ADAM_B1 * m + (1.0 - ADAM_B1) * g
    v = ADAM_B2 * v + (1.0 - ADAM_B2) * (g * g)
    m_hat = m / (1.0 - ADAM_B1 ** ADAM_STEP)
    v_hat = v / (1.0 - ADAM_B2 ** ADAM_STEP)
    return -ADAM_LR * (m_hat / (jnp.sqrt(v_hat) + ADAM_EPS) + ADAM_WD * w), m, v


def _adamw(w, g, m, v, grid, name):
    n_t = len(w)

    def body(*refs):
        ins, outs = refs[:4 * n_t], refs[4 * n_t:]
        for i in range(n_t):
            w_, g_, m_, v_ = [ins[a * n_t + i][...] for a in range(4)]
            vals = (g_,) + _adamw_update(w_, g_, m_, v_)
            for a in range(4):
                outs[a * n_t + i][...] = vals[a]

    specs = [pl.BlockSpec((a.shape[0] // grid, a.shape[1]), lambda i: (i, 0)) for a in w]
    shapes = [_sds(a.shape) for a in w]
    outs = _call(
        body, name=name, grid=(grid,), in_specs=specs * 4, out_specs=specs * 4, out_shape=shapes * 4,
        compiler_params=_params(40, ("arbitrary",)),
    )(*w, *g, *m, *v)
    return [outs[a * n_t:(a + 1) * n_t] for a in range(4)]


def kernel(x, p, pre_norm_g, w_in, ssm_lam_re, ssm_lam_im, ssm_log_step, ssm_b_re, ssm_b_im, ssm_c_re, ssm_c_im, ssm_d, ssm_w_glu, ssm_b_glu, attn_sinks, w_out, post_norm_g, pl_w_proj, pl_w_gate, pl_b_gate, loss_target, m_pre_norm_g, m_w_in, m_ssm_lam_re, m_ssm_lam_im, m_ssm_log_step, m_ssm_b_re, m_ssm_b_im, m_ssm_c_re, m_ssm_c_im, m_ssm_d, m_ssm_w_glu, m_ssm_b_glu, m_attn_sinks, m_w_out, m_post_norm_g, m_pl_w_proj, m_pl_w_gate, m_pl_b_gate, v_pre_norm_g, v_w_in, v_ssm_lam_re, v_ssm_lam_im, v_ssm_log_step, v_ssm_b_re, v_ssm_b_im, v_ssm_c_re, v_ssm_c_im, v_ssm_d, v_ssm_w_glu, v_ssm_b_glu, v_attn_sinks, v_w_out, v_post_norm_g, v_pl_w_proj, v_pl_w_gate, v_pl_b_gate):
    weights = dict(pre_norm_g=pre_norm_g, w_in=w_in, ssm_lam_re=ssm_lam_re, ssm_lam_im=ssm_lam_im,
                   ssm_log_step=ssm_log_step, ssm_b_re=ssm_b_re, ssm_b_im=ssm_b_im, ssm_c_re=ssm_c_re,
                   ssm_c_im=ssm_c_im, ssm_d=ssm_d, ssm_w_glu=ssm_w_glu, ssm_b_glu=ssm_b_glu, attn_sinks=attn_sinks,
                   w_out=w_out, post_norm_g=post_norm_g, pl_w_proj=pl_w_proj, pl_w_gate=pl_w_gate, pl_b_gate=pl_b_gate)
    m_in = dict(pre_norm_g=m_pre_norm_g, w_in=m_w_in, ssm_lam_re=m_ssm_lam_re, ssm_lam_im=m_ssm_lam_im,
                ssm_log_step=m_ssm_log_step, ssm_b_re=m_ssm_b_re, ssm_b_im=m_ssm_b_im, ssm_c_re=m_ssm_c_re,
                ssm_c_im=m_ssm_c_im, ssm_d=m_ssm_d, ssm_w_glu=m_ssm_w_glu, ssm_b_glu=m_ssm_b_glu,
                attn_sinks=m_attn_sinks, w_out=m_w_out, post_norm_g=m_post_norm_g, pl_w_proj=m_pl_w_proj,
                pl_w_gate=m_pl_w_gate, pl_b_gate=m_pl_b_gate)
    v_in = dict(pre_norm_g=v_pre_norm_g, w_in=v_w_in, ssm_lam_re=v_ssm_lam_re, ssm_lam_im=v_ssm_lam_im,
                ssm_log_step=v_ssm_log_step, ssm_b_re=v_ssm_b_re, ssm_b_im=v_ssm_b_im, ssm_c_re=v_ssm_c_re,
                ssm_c_im=v_ssm_c_im, ssm_d=v_ssm_d, ssm_w_glu=v_ssm_w_glu, ssm_b_glu=v_ssm_b_glu,
                attn_sinks=v_attn_sinks, w_out=v_w_out, post_norm_g=v_post_norm_g, pl_w_proj=v_pl_w_proj,
                pl_w_gate=v_pl_w_gate, pl_b_gate=v_pl_b_gate)

    def two_d(tree):
        return {k: _to_kernel_form(k, a) for k, a in tree.items()}

    w2, m2, v2 = two_d(weights), two_d(m_in), two_d(v_in)

    (w_in_full,) = _gather_weights_beside([w2["w_in"].astype(BF16)], "gather_w_in_beside", 4)
    s5_params = tuple(w2[n] for n in ("ssm_lam_re", "ssm_lam_im", "ssm_log_step", "ssm_b_re", "ssm_b_im", "ssm_c_re",
                                      "ssm_c_im"))
    s5_operands = _s5_params_fwd(*s5_params)
    hn = _pre_norm(x.reshape(-1, D_MODEL), w2["pre_norm_g"])
    behind = s5_operands[0][0, 0] * 0.0 + hn[0, 0].astype(F32) * 0.0
    rest = _gather_weights_beside([(w2[n] + behind).astype(BF16) for n in _BIG[1:]], "gather_weights_beside", 1)
    full = dict(zip(_BIG, [w_in_full] + rest))
    mats = ("w_in_early", "w_in_late") + _BIG[1:]
    landed, bc = {}, {}

    def send_beside(ready):
        sent_early = ("w_out", "pl_w_gate", "pl_w_proj")
        landed.update(zip([mats.index(n) for n in sent_early],
                          _scatter_beside([ready[n] for n in sent_early], "scatter_beside", 2)))
        after_scatter = landed[mats.index(sent_early[-1])][0, 0, 0] * 0.0
        bc["own"] = ready["ssm_bc"] + after_scatter.astype(BF16)
        bc["landed"] = _broadcast_beside([bc["own"]])[0]
        return bc["own"]

    grad_x, loss, grads = _local_step(
        x, hn, p, loss_target, w2["pre_norm_g"], full["w_in"], s5_params, s5_operands, w2["ssm_d"], full["ssm_w_glu"], w2["ssm_b_glu"],
        w2["attn_sinks"], full["w_out"], w2["post_norm_g"], full["pl_w_proj"], full["pl_w_gate"], w2["pl_b_gate"], send_beside)

    landed[0] = _scatter_beside([grads["w_in_early_bf16"]], "scatter_w_in_beside", 5)[0]
    sent_here = {**{n: grads[n] for n in _SMALL if n not in _SMALL_EARLY}, "loss": loss}
    g_big, g_small = _exchange_grads([grads[n] for n in mats], sent_here, landed, bc["own"], bc["landed"])
    g_big = dict(zip(mats, g_big))
    g_big["w_in"] = jnp.concatenate([g_big.pop("w_in_early"), g_big.pop("w_in_late")], axis=1)
    total_loss = g_small.pop("loss")

    big_out = _adamw([w2[n] for n in _BIG], [g_big[n] for n in _BIG], [m2[n] for n in _BIG], [v2[n] for n in _BIG],
                     8, "adamw_matrices")
    small_names = tuple(_SMALL)
    small_out = _adamw([w2[n] for n in small_names], [g_small[n] for n in small_names], [m2[n] for n in small_names],
                       [v2[n] for n in small_names], 1, "adamw_small")

    results = [{**dict(zip(_BIG, big_part)), **dict(zip(small_names, small_part))}
               for big_part, small_part in zip(big_out, small_out)]
    flat = [_from_kernel_form(name, r[name], weights[name].shape) for r in results for name in _WEIGHT_ORDER]
    return (total_loss.reshape(()), grad_x, *flat)
```

```python
import math

import jax
import jax.numpy as jnp
from jax import lax
from jax.experimental import pallas as pl
from jax.experimental.pallas import tpu as pltpu
from jax.experimental.pallas import tpu_sc as plsc

F32 = jnp.float32
BF16 = jnp.bfloat16

D_MODEL = 1024
D_SSM = 512
D_ATTN = 512
SSM_GROUPS = 32
SSM_GROUP_CH = 16
SSM_STATE = 64
SSM_LANES = SSM_GROUPS * SSM_STATE
HEAD_DIM = 64
N_HEADS = 8
KV_HEADS = 2
Q_PER_KV = 4
WINDOW = 128
BLOCK = 128
D_PLE = 256
D_IN = 2304
EPS = 1e-6
ATTN_SCALE = 1.0 / math.sqrt(HEAD_DIM)

ADAM_LR = 0.001
ADAM_B1 = 0.9
ADAM_B2 = 0.999
ADAM_EPS = 1e-08
ADAM_WD = 0.01
ADAM_STEP = 10

N_CHIPS = 4
LANES = 128
SCAN_CHUNKS = 8
SCAN_TILE_STEPS = 32
SCAN_LANE_CHUNK = 512
MIB = 2 ** 20
MESH = pl.DeviceIdType.MESH


def _dot(a, b):
    return jnp.dot(a, b, preferred_element_type=F32)


def _dot_nt(a, b):
    return lax.dot_general(a, b, (((1,), (1,)), ((), ())), preferred_element_type=F32)


def _dot_tn(a, b):
    return lax.dot_general(a, b, (((0,), (0,)), ((), ())), preferred_element_type=F32)


def _params(vmem_mib, semantics=None):
    kw = dict(vmem_limit_bytes=vmem_mib * MIB)
    if semantics is not None:
        kw["dimension_semantics"] = semantics
    return pltpu.CompilerParams(**kw)


def _full(shape):
    nd = len(shape)
    return pl.BlockSpec(shape, lambda *_: (0,) * nd, pipeline_mode=pl.Buffered(1))


def _rows(tm, width):
    return pl.BlockSpec((tm, width), lambda i: (i, 0))


def _sds(shape, dtype=F32):
    return pltpu.HBM(shape, dtype)


def _call(body, **kw):
    fn = pl.pallas_call(body, **kw)
    return lambda *args: fn(*[pltpu.with_memory_space_constraint(a, pltpu.HBM) for a in args])


def _silu(z):
    return z * jax.nn.sigmoid(z)


def _pre_norm(x2d, g1):
    rows = x2d.shape[0]
    tm = 512

    def body(x_ref, g_ref, hn_ref):
        x = x_ref[...]
        r = lax.rsqrt(jnp.mean(x * x, axis=-1, keepdims=True) + EPS)
        hn_ref[...] = (x * r * g_ref[...]).astype(BF16)

    return _call(
        body, name="pre_norm", grid=(rows // tm,), in_specs=[_rows(tm, D_MODEL), _full((1, D_MODEL))],
        out_specs=_rows(tm, D_MODEL), out_shape=_sds((rows, D_MODEL), BF16), compiler_params=_params(32, ("arbitrary",)),
    )(x2d, g1)


def _in_proj(hn, w_in_t, n_seq, seq):
    rows = hn.shape[0]
    tm = 1024
    slab, steps, _, _ = _scan_geometry(n_seq, seq)

    def body(hn_ref, w_ref, *out_refs):
        u_parts, (zs_ref, q_ref, k_ref, v_ref, za_ref) = out_refs[:_SCAN_PARTS], out_refs[_SCAN_PARTS:]
        whole = _dot_nt(hn_ref[...], w_ref[...])

        def proj(a, b):
            return whole[:, a:b]

        _store_chunks(u_parts, pl.program_id(0) * (tm // steps), proj(0, 512), steps, slab)
        zs_ref[...] = proj(512, 1024)
        q_ref[...] = (proj(1024, 1536) * ATTN_SCALE).astype(BF16)
        k_ref[...] = proj(1536, 1664).astype(BF16)
        v_ref[...] = proj(1664, 1792).astype(BF16)
        za_ref[...] = proj(1792, 2304)

    *u_parts, zs, q, k, v, za = _call(
        body, name="in_proj", grid=(rows // tm,),
        in_specs=[_rows(tm, D_MODEL), _full((D_IN, D_MODEL))],
        out_specs=_whole_parts(rows) + [_rows(tm, 512), _rows(tm, 512), _rows(tm, 128), _rows(tm, 128), _rows(tm, 512)],
        out_shape=_part_shapes(rows) + [_sds((rows, 512)), _sds((rows, 512), BF16), _sds((rows, 128), BF16),
                                        _sds((rows, 128), BF16), _sds((rows, 512))],
        compiler_params=_params(48, ("arbitrary",)),
    )(hn, w_in_t)
    return u_parts, zs, q, k, v, za


_EARLY_COLS = D_MODEL // 2


def _in_proj_bwd_early(hn, du_parts, dzs, dq, dk, dv, dza, runs_after, n_seq, seq):
    rows = hn.shape[0]
    tm = 512
    slab, steps, _, _ = _scan_geometry(n_seq, seq)

    def body(hn_ref, *refs):
        du_parts, (dzs_ref, dq_ref, dk_ref, dv_ref, dza_ref, _, dproj_ref, dw_ref, dwb_ref) = refs[:_SCAN_PARTS], refs[_SCAN_PARTS:]
        i = pl.program_id(0)

        @pl.when(i == 0)
        def _():
            dw_ref[...] = jnp.zeros_like(dw_ref)

        du = _load_chunks(du_parts, i * (tm // steps), tm // steps, steps, slab)
        d_proj = jnp.concatenate([du.astype(BF16), dzs_ref[...], dq_ref[...], dk_ref[...], dv_ref[...], dza_ref[...]],
                                 axis=1)
        dproj_ref[...] = d_proj
        dw_ref[...] += _dot_tn(d_proj, hn_ref[...])

        @pl.when(i == rows // tm - 1)
        def _():
            dwb_ref[...] = dw_ref[...].astype(BF16)

    return _call(
        body, name="in_proj_bwd_early", grid=(rows // tm,),
        in_specs=[_rows(tm, _EARLY_COLS)] + _whole_parts(rows)
        + [_rows(tm, 512), _rows(tm, 512), _rows(tm, 128), _rows(tm, 128), _rows(tm, 512),
           pl.BlockSpec(memory_space=pl.ANY)],
        out_specs=[_rows(tm, D_IN), _full((D_IN, _EARLY_COLS)), _full((D_IN, _EARLY_COLS))],
        out_shape=[_sds((rows, D_IN), BF16), _sds((D_IN, _EARLY_COLS)), _sds((D_IN, _EARLY_COLS), BF16)],
        compiler_params=_params(48, ("arbitrary",)),
    )(hn, *du_parts, dzs, dq, dk, dv, dza, runs_after)


def _in_proj_bwd(x2d, dh1, g1, w_in_t, d_proj, runs_after):
    rows = x2d.shape[0]
    tm = 512

    def body(x_ref, dh1_ref, g_ref, w_ref, dproj_ref, _, gx_ref, dw_ref, dg_ref):
        @pl.when(pl.program_id(0) == 0)
        def _():
            dw_ref[...] = jnp.zeros_like(dw_ref)
            dg_ref[...] = jnp.zeros_like(dg_ref)

        x = x_ref[...]
        g = g_ref[...]
        r = lax.rsqrt(jnp.mean(x * x, axis=-1, keepdims=True) + EPS)
        xr = x * r
        hn = (xr[:, _EARLY_COLS:] * g[:, _EARLY_COLS:]).astype(BF16)
        d_proj = dproj_ref[...]
        dhn = _dot(d_proj, w_ref[...])
        dw_ref[...] += _dot_tn(d_proj, hn)
        dg_ref[...] += jnp.sum(dhn * xr, axis=0, keepdims=True)
        a_ = dhn * g
        gx_ref[...] = dh1_ref[...] + r * a_ - xr * (r * jnp.mean(a_ * xr, axis=-1, keepdims=True))

    late_cols = D_MODEL - _EARLY_COLS
    return _call(
        body, name="in_proj_bwd", grid=(rows // tm,),
        in_specs=[_rows(tm, D_MODEL), _rows(tm, D_MODEL), _full((1, D_MODEL)), _full((D_IN, D_MODEL)), _rows(tm, D_IN),
                  pl.BlockSpec(memory_space=pl.ANY)],
        out_specs=[_rows(tm, D_MODEL), _full((D_IN, late_cols)), _full((1, D_MODEL))],
        out_shape=[_sds((rows, D_MODEL)), _sds((D_IN, late_cols)), _sds((1, D_MODEL))],
        compiler_params=_params(52, ("arbitrary",)),
    )(x2d, dh1, g1, w_in_t, d_proj, runs_after)


def _iota(shape, axis):
    return lax.broadcasted_iota(jnp.int32, shape, axis)


def _sum_of_thirds(f, a):
    hi = a.astype(BF16)
    rest = a - hi.astype(F32)
    mid = rest.astype(BF16)
    low = (rest - mid.astype(F32)).astype(BF16)
    return (f(hi) + f(mid)) + f(low)


@jax.custom_vjp
def _pick_rows(e, a):
    return _sum_of_thirds(lambda part: _dot(e, part), a)


def _pick_rows_fwd(e, a):
    return _pick_rows(e, a), e


def _pick_rows_bwd(e, ct):
    return jnp.zeros_like(e), _sum_of_thirds(lambda part: _dot_tn(e, part), ct)


_pick_rows.defvjp(_pick_rows_fwd, _pick_rows_bwd)


@jax.custom_vjp
def _pick_cols(a, e):
    return _sum_of_thirds(lambda part: _dot(part, e), a)


def _pick_cols_fwd(a, e):
    return _pick_cols(a, e), e


def _pick_cols_bwd(e, ct):
    return _sum_of_thirds(lambda part: _dot_nt(part, e), ct), jnp.zeros_like(e)


_pick_cols.defvjp(_pick_cols_fwd, _pick_cols_bwd)


_HALF_GROUPS = SSM_GROUPS // 2
_N_SHIFT = SSM_STATE.bit_length() - 1
_P_SHIFT = SSM_GROUP_CH.bit_length() - 1


def _s5_operands(lam_re, lam_im, log_step, b_re, b_im, c_re, c_im):
    g, n, p = SSM_GROUPS, SSM_STATE, SSM_GROUP_CH
    gn, gp, hn_, hp = g * n, g * p, _HALF_GROUPS * n, _HALF_GROUPS * p
    eye_g = _iota((g, g), 0) == _iota((g, g), 1)
    step = jnp.sum(jnp.where(eye_g, jnp.exp(log_step), 0.0), axis=1, keepdims=True)
    a_re = lam_re * step
    a_im = lam_im * step
    mag = jnp.exp(a_re)
    lbar_re = mag * jnp.cos(a_im)
    lbar_im = mag * jnp.sin(a_im)
    n_re = lbar_re - 1.0
    den = lam_re * lam_re + lam_im * lam_im
    f_re = (n_re * lam_re + lbar_im * lam_im) / den
    f_im = (lbar_im * lam_re - n_re * lam_im) / den

    spread_n = (_iota((n, gn), 0) == (_iota((n, gn), 1) & (n - 1))).astype(BF16)
    own_g = _iota((g, gn), 0) == (_iota((g, gn), 1) >> _N_SHIFT)

    def to_row(a):
        return jnp.sum(jnp.where(own_g, _pick_cols(a, spread_n), 0.0), axis=0, keepdims=True)

    per_group = ((_iota((gp, g), 0) >> _P_SHIFT) == _iota((gp, g), 1)).astype(BF16)
    fx_re, fx_im = _pick_rows(per_group, f_re), _pick_rows(per_group, f_im)
    bbar_re = fx_re * b_re - fx_im * b_im
    bbar_im = fx_re * b_im + fx_im * b_re

    tile_n = (_iota((n, hn_), 0) == (_iota((n, hn_), 1) & (n - 1))).astype(BF16)
    same_group = (_iota((hp, hn_), 0) >> _P_SHIFT) == (_iota((hp, hn_), 1) >> _N_SHIFT)

    def embed(a, hf):
        return jnp.where(same_group, _pick_cols(a[hf * hp:(hf + 1) * hp], tile_n), 0.0)

    return (to_row(lbar_re), to_row(lbar_im), embed(bbar_re, 0), embed(bbar_re, 1), embed(bbar_im, 0),
            embed(bbar_im, 1), embed(c_re, 0), embed(c_re, 1), embed(c_im, 0), embed(c_im, 1))


_S5_PARAM_SHAPES = ((SSM_GROUPS, SSM_STATE), (SSM_GROUPS, SSM_STATE), (1, SSM_GROUPS),
                    (D_SSM, SSM_STATE), (D_SSM, SSM_STATE), (D_SSM, SSM_STATE), (D_SSM, SSM_STATE))
_CM_SHAPE = (2, _HALF_GROUPS * SSM_GROUP_CH, _HALF_GROUPS * SSM_STATE)
_S5_OPERAND_SHAPES = ((1, SSM_LANES), (1, SSM_LANES), _CM_SHAPE, _CM_SHAPE, _CM_SHAPE, _CM_SHAPE)


def _s5_params_fwd(*params):
    def body(*refs):
        ins, (lre_ref, lim_ref, btre_ref, btim_ref, cmre_ref, cmim_ref) = refs[:7], refs[7:]
        vals = _s5_operands(*[r[...] for r in ins])
        lre_ref[...] = vals[0]
        lim_ref[...] = vals[1]
        for ref, pair in zip((btre_ref, btim_ref, cmre_ref, cmim_ref), (vals[2:4], vals[4:6], vals[6:8], vals[8:10])):
            ref[0] = pair[0].astype(BF16)
            ref[1] = pair[1].astype(BF16)

    dtypes = (F32, F32, BF16, BF16, BF16, BF16)
    return _call(
        body, name="s5_params_fwd",
        in_specs=[_full(s) for s in _S5_PARAM_SHAPES], out_specs=[_full(s) for s in _S5_OPERAND_SHAPES],
        out_shape=[_sds(s, d) for s, d in zip(_S5_OPERAND_SHAPES, dtypes)], compiler_params=_params(32),
    )(*params)


_BC_SIDE_BY_SIDE = (D_SSM, 4 * SSM_STATE)


def _s5_params_bwd(params, cotangents):
    def body(*refs):
        ins, (dlre, dlim, dbtre, dbtim, dcmre, dcmim), outs = refs[:7], refs[7:13], refs[13:]
        _, vjp = jax.vjp(_s5_operands, *[r[...] for r in ins])
        cts = (dlre[...], dlim[...], dbtre[0], dbtre[1], dbtim[0], dbtim[1], dcmre[0], dcmre[1], dcmim[0], dcmim[1])
        grads = vjp(cts)
        for ref, val in zip(outs[:3], grads[:3]):
            ref[...] = val
        outs[3][...] = jnp.concatenate(grads[3:], axis=1).astype(BF16)

    out_shapes = _S5_PARAM_SHAPES[:3] + (_BC_SIDE_BY_SIDE,)
    return _call(
        body, name="s5_params_bwd",
        in_specs=[_full(s) for s in _S5_PARAM_SHAPES + _S5_OPERAND_SHAPES],
        out_specs=[_full(s) for s in out_shapes],
        out_shape=[_sds(s, d) for s, d in zip(out_shapes, (F32, F32, F32, BF16))], compiler_params=_params(48),
    )(*params, *cotangents)


def _scan_geometry(n_seq, seq):
    slab = n_seq * SCAN_CHUNKS
    steps = seq // SCAN_CHUNKS
    tile_rows = slab * SCAN_TILE_STEPS
    n_tiles = steps // SCAN_TILE_STEPS
    return slab, steps, tile_rows, n_tiles


_SCAN_PARTS = D_SSM // LANES


def _whole_parts(rows):
    return [_full((rows, LANES))] * _SCAN_PARTS


def _part_shapes(rows):
    return [_sds((rows, LANES))] * _SCAN_PARTS


def _load_chunks(parts, first_chunk, n_chunks, steps, slab):
    return jnp.concatenate([
        jnp.concatenate([ref[pl.ds(first_chunk + q, steps, stride=slab), :] for ref in parts], axis=1)
        for q in range(n_chunks)], axis=0)


def _store_chunks(parts, first_chunk, value, steps, slab):
    for q in range(value.shape[0] // steps):
        for j, ref in enumerate(parts):
            ref[pl.ds(first_chunk + q, steps, stride=slab), :] = value[q * steps:(q + 1) * steps,
                                                                     j * LANES:(j + 1) * LANES]


def _join_parts(parts):
    return jnp.concatenate([ref[...] for ref in parts], axis=1)


def _split_parts(parts, value):
    for j, ref in enumerate(parts):
        ref[...] = value[:, j * LANES:(j + 1) * LANES]


def _complex_power(re, im, n):
    out = None
    while n:
        if n & 1:
            out = (re, im) if out is None else (out[0] * re - out[1] * im, out[0] * im + out[1] * re)
        n >>= 1
        if n:
            re, im = re * re - im * im, 2.0 * re * im
    return out


def _chunk_carry(sum_re, sum_im, carry_re, carry_im, a_re, a_im, n_seq, reverse):
    carry_re[...] = jnp.zeros_like(carry_re)
    carry_im[...] = jnp.zeros_like(carry_im)
    for s in range(n_seq):
        order = range(SCAN_CHUNKS - 2, -1, -1) if reverse else range(1, SCAN_CHUNKS)
        for c in order:
            r = s * SCAN_CHUNKS + c
            p = r + 1 if reverse else r - 1
            p_re, p_im = carry_re[p:p + 1, :], carry_im[p:p + 1, :]
            carry_re[r:r + 1, :] = a_re * p_re - a_im * p_im + sum_re[p:p + 1, :]
            carry_im[r:r + 1, :] = a_re * p_im + a_im * p_re + sum_im[p:p + 1, :]


def _s5_scan_fwd(u_parts, bt_re, bt_im, cm_re, cm_im, lbar_re, lbar_im, d_row, n_seq, seq):
    slab, steps, tile_rows, n_tiles = _scan_geometry(n_seq, seq)
    rows = u_parts[0].shape[0]

    def body(*refs):
        u_refs, refs = refs[:_SCAN_PARTS], refs[_SCAN_PARTS:]
        (bre_ref, bim_ref, cre_ref, cim_ref, lre_ref, lim_ref, d_ref), refs = refs[:7], refs[7:]
        y_refs, (hre_ref, him_ref, st_re, st_im, h0_re, h0_im, buf_re, buf_im) = refs[:_SCAN_PARTS], refs[_SCAN_PARTS:]
        second = pl.program_id(0) == 1
        i = pl.program_id(1)

        @pl.when(jnp.logical_and(i == 0, jnp.logical_not(second)))
        def _():
            st_re[...] = jnp.zeros_like(st_re)
            st_im[...] = jnp.zeros_like(st_im)

        u = _join_parts(u_refs)
        ub = u.astype(BF16)
        for hf in range(2):
            cols = slice(hf * 1024, (hf + 1) * 1024)
            buf_re[:, cols] = _dot(ub[:, hf * 256:(hf + 1) * 256], bre_ref[hf])
            buf_im[:, cols] = _dot(ub[:, hf * 256:(hf + 1) * 256], bim_ref[hf])

        for lc in range(SSM_LANES // SCAN_LANE_CHUNK):
            cols = slice(lc * SCAN_LANE_CHUNK, (lc + 1) * SCAN_LANE_CHUNK)
            l_re = jnp.broadcast_to(lre_ref[:, cols], (slab, SCAN_LANE_CHUNK))
            l_im = jnp.broadcast_to(lim_ref[:, cols], (slab, SCAN_LANE_CHUNK))

            def scan_tile(keep_states):
                def step(t, carry):
                    s_re, s_im = carry
                    r0 = pl.multiple_of(t * slab, slab)
                    n_re = l_re * s_re - l_im * s_im + buf_re[pl.ds(r0, slab), cols]
                    n_im = l_re * s_im + l_im * s_re + buf_im[pl.ds(r0, slab), cols]
                    if keep_states:
                        buf_re[pl.ds(r0, slab), cols] = n_re
                        buf_im[pl.ds(r0, slab), cols] = n_im
                    return n_re, n_im

                s_re, s_im = lax.fori_loop(0, SCAN_TILE_STEPS, step, (st_re[:, cols], st_im[:, cols]), unroll=True)
                st_re[:, cols] = s_re
                st_im[:, cols] = s_im

            pl.when(jnp.logical_not(second))(lambda: scan_tile(False))
            pl.when(second)(lambda: scan_tile(True))

        @pl.when(jnp.logical_and(i == n_tiles - 1, jnp.logical_not(second)))
        def _():
            a_re, a_im = _complex_power(lre_ref[...], lim_ref[...], steps)
            _chunk_carry(st_re, st_im, h0_re, h0_im, a_re, a_im, n_seq, reverse=False)
            st_re[...] = h0_re[...]
            st_im[...] = h0_im[...]

        @pl.when(second)
        def _():
            h_re = buf_re[...].astype(BF16)
            h_im = buf_im[...].astype(BF16)
            hre_ref[...] = h_re
            him_ref[...] = h_im
            for hf in range(2):
                cols = slice(hf * 1024, (hf + 1) * 1024)
                ycols = slice(hf * 256, (hf + 1) * 256)
                y_half = (_dot_nt(h_re[:, cols], cre_ref[hf]) - _dot_nt(h_im[:, cols], cim_ref[hf])
                          + d_ref[:, ycols] * u[:, ycols])
                _split_parts(y_refs[2 * hf:2 * hf + 2], y_half)

    tile = lambda w: pl.BlockSpec((tile_rows, w), lambda p, i: (i, 0))
    out_tile = lambda w: pl.BlockSpec((tile_rows, w), lambda p, i: (i * p, 0))
    cm = _full(_CM_SHAPE)
    outs = _call(
        body, name="s5_scan_fwd", grid=(2, n_tiles),
        in_specs=[tile(LANES)] * _SCAN_PARTS + [cm, cm, cm, cm, _full((1, SSM_LANES)), _full((1, SSM_LANES)),
                                                _full((1, 512))],
        out_specs=[out_tile(LANES)] * _SCAN_PARTS + [out_tile(SSM_LANES), out_tile(SSM_LANES)],
        out_shape=_part_shapes(rows) + [_sds((rows, SSM_LANES), BF16), _sds((rows, SSM_LANES), BF16)],
        scratch_shapes=[pltpu.VMEM((slab, SSM_LANES), F32)] * 4 + [pltpu.VMEM((tile_rows, SSM_LANES), F32)] * 2,
        compiler_params=_params(40, ("arbitrary", "arbitrary")),
    )(*u_parts, bt_re, bt_im, cm_re, cm_im, lbar_re, lbar_im, d_row)
    return outs[:_SCAN_PARTS], outs[_SCAN_PARTS], outs[_SCAN_PARTS + 1]


def _s5_scan_bwd(dy_parts, u_parts, h_re, h_im, bt_re, bt_im, cm_re, cm_im, lbar_re, lbar_im, d_row, n_seq, seq):
    slab, steps, tile_rows, n_tiles = _scan_geometry(n_seq, seq)
    rows = u_parts[0].shape[0]

    def body(*refs):
        dy_refs, u_refs, refs = refs[:_SCAN_PARTS], refs[_SCAN_PARTS:2 * _SCAN_PARTS], refs[2 * _SCAN_PARTS:]
        (hre_ref, him_ref, bre_ref, bim_ref, cre_ref, cim_ref, lre_ref, lim_ref, d_ref), refs = refs[:9], refs[9:]
        du_refs, refs = refs[:_SCAN_PARTS], refs[_SCAN_PARTS:]
        (dbre_ref, dbim_ref, dcre_ref, dcim_ref, dlre_ref, dlim_ref, dd_ref,
         st_re, st_im, g0_re, g0_im, acc_re, acc_im, buf_re, buf_im) = refs
        second = pl.program_id(0) == 1
        i = pl.program_id(1)

        @pl.when(jnp.logical_and(i == 0, jnp.logical_not(second)))
        def _():
            st_re[...] = jnp.zeros_like(st_re)
            st_im[...] = jnp.zeros_like(st_im)
            acc_re[...] = jnp.zeros_like(acc_re)
            acc_im[...] = jnp.zeros_like(acc_im)
            for ref in (dbre_ref, dbim_ref, dcre_ref, dcim_ref, dd_ref):
                ref[...] = jnp.zeros_like(ref)

        dy = _join_parts(dy_refs)
        dyb = dy.astype(BF16)
        for hf in range(2):
            cols = slice(hf * 1024, (hf + 1) * 1024)
            buf_re[:, cols] = _dot(dyb[:, hf * 256:(hf + 1) * 256], cre_ref[hf])
            buf_im[:, cols] = -_dot(dyb[:, hf * 256:(hf + 1) * 256], cim_ref[hf])

        for lc in range(SSM_LANES // SCAN_LANE_CHUNK):
            cols = slice(lc * SCAN_LANE_CHUNK, (lc + 1) * SCAN_LANE_CHUNK)
            l_re = jnp.broadcast_to(lre_ref[:, cols], (slab, SCAN_LANE_CHUNK))
            l_im = jnp.broadcast_to(lim_ref[:, cols], (slab, SCAN_LANE_CHUNK))

            def advance(r0, s_re, s_im):
                n_re = l_re * s_re + l_im * s_im + buf_re[pl.ds(r0, slab), cols]
                n_im = l_re * s_im - l_im * s_re + buf_im[pl.ds(r0, slab), cols]
                buf_re[pl.ds(r0, slab), cols] = n_re
                buf_im[pl.ds(r0, slab), cols] = n_im
                return n_re, n_im

            def row0(k):
                return pl.multiple_of((SCAN_TILE_STEPS - 1 - k) * slab, slab)

            @pl.when(jnp.logical_not(second))
            def _():
                s_re, s_im = lax.fori_loop(0, SCAN_TILE_STEPS, lambda k, s: advance(row0(k), *s),
                                           (st_re[:, cols], st_im[:, cols]), unroll=True)
                st_re[:, cols] = s_re
                st_im[:, cols] = s_im

            @pl.when(second)
            def _():
                def step(k, carry):
                    s_re, s_im, a_re, a_im = carry
                    r0 = row0(k)
                    hr = hre_ref[pl.ds(r0, slab), cols].astype(F32)
                    hi = him_ref[pl.ds(r0, slab), cols].astype(F32)
                    a_re = a_re + s_re * hr + s_im * hi
                    a_im = a_im + s_im * hr - s_re * hi
                    return advance(r0, s_re, s_im) + (a_re, a_im)

                zero = jnp.zeros((slab, SCAN_LANE_CHUNK), F32)
                s_re, s_im, a_re, a_im = lax.fori_loop(
                    0, SCAN_TILE_STEPS, step, (st_re[:, cols], st_im[:, cols], zero, zero), unroll=True)
                st_re[:, cols] = s_re
                st_im[:, cols] = s_im
                acc_re[:, cols] += a_re
                acc_im[:, cols] += a_im

        @pl.when(jnp.logical_and(i == n_tiles - 1, jnp.logical_not(second)))
        def _():
            p_re, p_im = _complex_power(lre_ref[...], lim_ref[...], steps)
            _chunk_carry(st_re, st_im, g0_re, g0_im, p_re, -p_im, n_seq, reverse=True)
            st_re[...] = g0_re[...]
            st_im[...] = g0_im[...]

        @pl.when(second)
        def _():
            u = _join_parts(u_refs)
            ub = u.astype(BF16)
            g_re = buf_re[...].astype(BF16)
            g_im = buf_im[...].astype(BF16)
            dd_ref[...] += jnp.sum(dy * u, axis=0, keepdims=True)
            for hf in range(2):
                cols = slice(hf * 1024, (hf + 1) * 1024)
                ycols = slice(hf * 256, (hf + 1) * 256)
                du_half = (_dot_nt(g_re[:, cols], bre_ref[hf]) + _dot_nt(g_im[:, cols], bim_ref[hf])
                           + d_ref[:, ycols] * dy[:, ycols])
                _split_parts(du_refs[2 * hf:2 * hf + 2], du_half)
                for q4 in range(_HALF_GROUPS // 4):
                    ch = slice(hf * 256 + q4 * 64, hf * 256 + (q4 + 1) * 64)
                    st = slice(hf * 1024 + q4 * 256, hf * 1024 + (q4 + 1) * 256)
                    blk = (hf, slice(q4 * 64, (q4 + 1) * 64), slice(q4 * 256, (q4 + 1) * 256))
                    dbre_ref[blk] += _dot_tn(ub[:, ch], g_re[:, st])
                    dbim_ref[blk] += _dot_tn(ub[:, ch], g_im[:, st])
                    dcre_ref[blk] += _dot_tn(dyb[:, ch], hre_ref[:, st])
                    dcim_ref[blk] -= _dot_tn(dyb[:, ch], him_ref[:, st])

        @pl.when(jnp.logical_and(i == n_tiles - 1, second))
        def _():
            dlre_ref[...] = jnp.sum(acc_re[...], axis=0, keepdims=True)
            dlim_ref[...] = jnp.sum(acc_im[...], axis=0, keepdims=True)

    tile = lambda w: pl.BlockSpec((tile_rows, w), lambda p, i: (n_tiles - 1 - i, 0))
    second_tile = lambda w: pl.BlockSpec((tile_rows, w), lambda p, i: (n_tiles - 1 - i * p, 0))
    cm = _full(_CM_SHAPE)
    row = _full((1, SSM_LANES))
    outs = _call(
        body, name="s5_scan_bwd", grid=(2, n_tiles),
        in_specs=[tile(LANES)] * _SCAN_PARTS + [second_tile(LANES)] * _SCAN_PARTS
        + [second_tile(SSM_LANES), second_tile(SSM_LANES), cm, cm, cm, cm, row, row, _full((1, 512))],
        out_specs=[second_tile(LANES)] * _SCAN_PARTS + [cm, cm, cm, cm, row, row, _full((1, 512))],
        out_shape=(_part_shapes(rows) + [_sds(_CM_SHAPE)] * 4 + [_sds((1, SSM_LANES))] * 2 + [_sds((1, 512))]),
        scratch_shapes=[pltpu.VMEM((slab, SSM_LANES), F32)] * 6 + [pltpu.VMEM((tile_rows, SSM_LANES), F32)] * 2,
        compiler_params=_params(48, ("arbitrary", "arbitrary")),
    )(*dy_parts, *u_parts, h_re, h_im, bt_re, bt_im, cm_re, cm_im, lbar_re, lbar_im, d_row)
    return (outs[:_SCAN_PARTS],) + tuple(outs[_SCAN_PARTS:])


def _glu_gate(gl, a, zs):
    return gl * jax.nn.sigmoid(a) * _silu(zs)


def _glu_fwd(y_parts, zs, w_glu, b_glu, n_seq, seq):
    rows = zs.shape[0]
    tm = 512
    slab, steps, _, _ = _scan_geometry(n_seq, seq)

    def body(*refs):
        y_refs, (zs_ref, w_ref, b_ref, o_ref) = refs[:_SCAN_PARTS], refs[_SCAN_PARTS:]
        y = _load_chunks(y_refs, pl.program_id(0) * (tm // steps), tm // steps, steps, slab)
        gl = jax.nn.gelu(y)
        a = _dot(gl.astype(BF16), w_ref[...]) + b_ref[...]
        o_ref[...] = _glu_gate(gl, a, zs_ref[...]).astype(BF16)

    return _call(
        body, name="glu_fwd", grid=(rows // tm,),
        in_specs=_whole_parts(rows) + [_rows(tm, 512), _full((512, 512)), _full((1, 512))],
        out_specs=_rows(tm, 512), out_shape=_sds((rows, 512), BF16),
        compiler_params=_params(32, ("arbitrary",)),
    )(*y_parts, zs, w_glu, b_glu)


def _glu_bwd(y_parts, zs, d_out, w_glu, b_glu, n_seq, seq):
    rows = zs.shape[0]
    tm = 512
    slab, steps, _, _ = _scan_geometry(n_seq, seq)

    def body(*refs):
        y_refs, (zs_ref, d_ref, w_ref, b_ref), refs = refs[:_SCAN_PARTS], refs[_SCAN_PARTS:_SCAN_PARTS + 4], refs[_SCAN_PARTS + 4:]
        dy_refs, (dzs_ref, dw_ref, db_ref) = refs[:_SCAN_PARTS], refs[_SCAN_PARTS:]
        first_chunk = pl.program_id(0) * (tm // steps)

        @pl.when(pl.program_id(0) == 0)
        def _():
            dw_ref[...] = jnp.zeros_like(dw_ref)
            db_ref[...] = jnp.zeros_like(db_ref)

        gl, gelu_vjp = jax.vjp(jax.nn.gelu, _load_chunks(y_refs, first_chunk, tm // steps, steps, slab))
        glb = gl.astype(BF16)
        a = _dot(glb, w_ref[...]) + b_ref[...]
        _, gate_vjp = jax.vjp(_glu_gate, gl, a, zs_ref[...])
        d_gl, d_a, d_zs = gate_vjp(d_ref[...])
        dab = d_a.astype(BF16)
        d_gl = d_gl + _dot_nt(dab, w_ref[...])
        _store_chunks(dy_refs, first_chunk, gelu_vjp(d_gl)[0], steps, slab)
        dzs_ref[...] = d_zs.astype(BF16)
        dw_ref[...] += _dot_tn(glb, dab)
        db_ref[...] += jnp.sum(d_a, axis=0, keepdims=True)

    *dy_parts, dzs, dw, db = _call(
        body, name="glu_bwd", grid=(rows // tm,),
        in_specs=_whole_parts(rows) + [_rows(tm, 512), _rows(tm, 512), _full((512, 512)), _full((1, 512))],
        out_specs=_whole_parts(rows) + [_rows(tm, 512), _full((512, 512)), _full((1, 512))],
        out_shape=_part_shapes(rows) + [_sds((rows, 512), BF16), _sds((512, 512)), _sds((1, 512))],
        compiler_params=_params(40, ("arbitrary",)),
    )(*y_parts, zs, d_out, w_glu, b_glu)
    return dy_parts, dzs, dw, db


_GROUP_ROWS = Q_PER_KV * BLOCK
_BLOCK_SHIFT = BLOCK.bit_length() - 1


def _attn_bias(j):
    query = _iota((BLOCK, _GROUP_ROWS), 1)
    dist_cur = (query & (BLOCK - 1)) - _iota((BLOCK, _GROUP_ROWS), 0)
    dist_prev = dist_cur + BLOCK
    head = query >> _BLOCK_SHIFT
    slope = jnp.zeros((BLOCK, _GROUP_ROWS), F32)
    for g in range(Q_PER_KV):
        slope = jnp.where(head == g, 2.0 ** (-(j * Q_PER_KV + g + 1)), slope)
    bias_cur = jnp.where(dist_cur >= 0, -slope * dist_cur.astype(F32), -jnp.inf)
    bias_prev = jnp.where(dist_prev < WINDOW, -slope * dist_prev.astype(F32), -jnp.inf)
    return bias_cur, bias_prev


_ATTN_BIAS_SCRATCH = pltpu.VMEM((KV_HEADS, 2, BLOCK, _GROUP_ROWS), F32)


def _fill_attn_bias(bias_ref):
    @pl.when(jnp.logical_and(pl.program_id(0) == 0, pl.program_id(1) == 0))
    def _():
        for j in range(KV_HEADS):
            bias_ref[j, 0], bias_ref[j, 1] = _attn_bias(j)


def _stack_heads(x, j):
    heads = range(j * Q_PER_KV, (j + 1) * Q_PER_KV)
    return jnp.concatenate([x[:, h * HEAD_DIM:(h + 1) * HEAD_DIM] for h in heads], axis=0)


def _head_rows(x, j):
    heads = range(j * Q_PER_KV, (j + 1) * Q_PER_KV)
    return jnp.concatenate([x[h:h + 1, :] for h in heads], axis=1)


def _sink_row(sk_ref, j):
    heads = range(j * Q_PER_KV, (j + 1) * Q_PER_KV)
    return jnp.concatenate([jnp.broadcast_to(sk_ref[0:1, h:h + 1], (1, BLOCK)) for h in heads], axis=1)


def _attn_fwd(q, k, v, za, sinks, n_seq, seq):
    nb = seq // BLOCK
    rows = q.shape[0]

    def body(q_ref, kc_ref, kp_ref, vc_ref, vp_ref, za_ref, sk_ref, o_ref, ao_ref, lse_ref, bias_ref):
        _fill_attn_bias(bias_ref)
        has_prev = pl.program_id(1) > 0
        q_all = q_ref[...]
        for j in range(KV_HEADS):
            js = slice(j * HEAD_DIM, (j + 1) * HEAD_DIM)
            bias_c, bias_p = bias_ref[j, 0], bias_ref[j, 1]
            q4 = _stack_heads(q_all, j)
            sc = _dot_nt(kc_ref[:, js], q4) + bias_c
            sp = _dot_nt(kp_ref[:, js], q4) + jnp.where(has_prev, bias_p, -jnp.inf)
            sink = _sink_row(sk_ref, j)
            m = jnp.maximum(jnp.max(jnp.maximum(sc, sp), axis=0, keepdims=True), sink)
            ec = jnp.exp(sc - m)
            ep = jnp.exp(sp - m)
            den = jnp.sum(ec + ep, axis=0, keepdims=True) + jnp.exp(sink - m)
            inv = 1.0 / den
            o4 = _dot_tn((ec * inv).astype(BF16), vc_ref[:, js]) + _dot_tn((ep * inv).astype(BF16), vp_ref[:, js])
            lse4 = m + jnp.log(den)
            for g in range(Q_PER_KV):
                h = j * Q_PER_KV + g
                o_ref[:, h * HEAD_DIM:(h + 1) * HEAD_DIM] = o4[g * BLOCK:(g + 1) * BLOCK]
                lse_ref[h:h + 1, :] = lse4[:, g * BLOCK:(g + 1) * BLOCK]
        ao_ref[...] = (o_ref[...] * _silu(za_ref[...])).astype(BF16)

    cur = lambda w: pl.BlockSpec((BLOCK, w), lambda b, n: (b * nb + n, 0))
    prev = lambda w: pl.BlockSpec((BLOCK, w), lambda b, n: (b * nb + jnp.maximum(n - 1, 0), 0))
    lse_rows = rows // BLOCK * N_HEADS
    return _call(
        body, name="attn_fwd", grid=(n_seq, nb),
        in_specs=[cur(512), cur(128), prev(128), cur(128), prev(128), cur(512), _full((1, N_HEADS))],
        out_specs=[cur(512), cur(512), pl.BlockSpec((N_HEADS, BLOCK), lambda b, n: (b * nb + n, 0))],
        out_shape=[_sds((rows, 512)), _sds((rows, 512), BF16), _sds((lse_rows, BLOCK))],
        scratch_shapes=[_ATTN_BIAS_SCRATCH], compiler_params=_params(32, ("arbitrary", "arbitrary")),
    )(q, k, k, v, v, za, sinks)


def _attn_bwd(q, k, v, za, o, lse, d_ao, sinks, n_seq, seq):
    nb = seq // BLOCK
    rows = q.shape[0]

    def body(q_ref, kc_ref, kp_ref, vc_ref, vp_ref, za_ref, o_ref, lse_ref, d_ref, sk_ref,
             dq_ref, dk_ref, dv_ref, dza_ref, dsk_ref, bias_ref, dk_carry, dv_carry):
        n = nb - 1 - pl.program_id(1)
        _fill_attn_bias(bias_ref)

        @pl.when(jnp.logical_and(pl.program_id(0) == 0, pl.program_id(1) == 0))
        def _():
            dsk_ref[...] = jnp.zeros_like(dsk_ref)
            dk_carry[...] = jnp.zeros_like(dk_carry)
            dv_carry[...] = jnp.zeros_like(dv_carry)

        has_prev = n > 0
        has_next = n + 1 < nb

        _, gate_vjp = jax.vjp(lambda o_, z_: o_ * _silu(z_), o_ref[...], za_ref[...])
        d_o, d_za = gate_vjp(d_ref[...])
        dza_ref[...] = d_za.astype(BF16)
        q_all = q_ref[...]
        lse_all = lse_ref[...]

        for j in range(KV_HEADS):
            js = slice(j * HEAD_DIM, (j + 1) * HEAD_DIM)
            kc, kp, vc, vp = kc_ref[:, js], kp_ref[:, js], vc_ref[:, js], vp_ref[:, js]
            bias_c, bias_p = bias_ref[j, 0], bias_ref[j, 1]
            q4 = _stack_heads(q_all, j)
            do4b = _stack_heads(d_o, j).astype(BF16)
            lse4 = _head_rows(lse_all, j)
            pc = jnp.exp(_dot_nt(kc, q4) + bias_c - lse4)
            pp = jnp.exp(_dot_nt(kp, q4) + jnp.where(has_prev, bias_p, -jnp.inf) - lse4)
            dpc = _dot_nt(vc, do4b)
            dpp = _dot_nt(vp, do4b)
            delta = jnp.sum(pc * dpc + pp * dpp, axis=0, keepdims=True)
            dsc = (pc * (dpc - delta)).astype(BF16)
            dsp = (pp * (dpp - delta)).astype(BF16)
            dq4 = ((_dot_tn(dsc, kc) + _dot_tn(dsp, kp)) * ATTN_SCALE).astype(BF16)
            sink_loss = jnp.exp(_sink_row(sk_ref, j) - lse4) * delta
            for g in range(Q_PER_KV):
                h = j * Q_PER_KV + g
                dq_ref[:, h * HEAD_DIM:(h + 1) * HEAD_DIM] = dq4[g * BLOCK:(g + 1) * BLOCK]
                dsk_ref[0:1, h:h + 1] -= jnp.sum(sink_loss[:, g * BLOCK:(g + 1) * BLOCK], axis=1, keepdims=True)
            dk = _dot(dsc, q4) + jnp.where(has_next, dk_carry[j], 0.0)
            dv = _dot(pc.astype(BF16), do4b) + jnp.where(has_next, dv_carry[j], 0.0)
            dk_carry[j] = _dot(dsp, q4)
            dv_carry[j] = _dot(pp.astype(BF16), do4b)
            dk_ref[:, js] = dk.astype(BF16)
            dv_ref[:, js] = dv.astype(BF16)

    cur = lambda w: pl.BlockSpec((BLOCK, w), lambda b, s: (b * nb + nb - 1 - s, 0))
    prev = lambda w: pl.BlockSpec((BLOCK, w), lambda b, s: (b * nb + jnp.maximum(nb - 2 - s, 0), 0))
    return _call(
        body, name="attn_bwd", grid=(n_seq, nb),
        in_specs=[cur(512), cur(128), prev(128), cur(128), prev(128), cur(512), cur(512),
                  pl.BlockSpec((N_HEADS, BLOCK), lambda b, s: (b * nb + nb - 1 - s, 0)), cur(512), _full((1, N_HEADS))],
        out_specs=[cur(512), cur(128), cur(128), cur(512), _full((1, N_HEADS))],
        out_shape=[_sds((rows, 512), BF16), _sds((rows, 128), BF16), _sds((rows, 128), BF16),
                   _sds((rows, 512), BF16), _sds((1, N_HEADS))],
        scratch_shapes=[_ATTN_BIAS_SCRATCH, pltpu.VMEM((KV_HEADS, BLOCK, HEAD_DIM), F32),
                        pltpu.VMEM((KV_HEADS, BLOCK, HEAD_DIM), F32)],
        compiler_params=_params(32, ("arbitrary", "arbitrary")),
    )(q, k, k, v, v, za, o, lse, d_ao, sinks)


def _tail(ssm_out, attn_out, x2d, p2d, target, w_out, g2, w_gate, b_gate, w_proj):
    rows = x2d.shape[0]
    tm = 512

    def body(so_ref, ao_ref, x_ref, p_ref, t_ref, wo_ref, g2_ref, wg_ref, bg_ref, wp_ref,
             dh1_ref, dso_ref, dao_ref, dwo_ref, dwg_ref, dwp_ref, dbg_ref, dg2_ref, loss_ref):
        @pl.when(pl.program_id(0) == 0)
        def _():
            for ref in (dwo_ref, dwg_ref, dwp_ref, dbg_ref, dg2_ref, loss_ref):
                ref[...] = jnp.zeros_like(ref)

        cat = jnp.concatenate([so_ref[...], ao_ref[...]], axis=1)
        g2 = g2_ref[...]
        mixed = _dot(cat, wo_ref[...])
        r = lax.rsqrt(jnp.mean(mixed * mixed, axis=-1, keepdims=True) + EPS)
        mr = mixed * r
        h1 = x_ref[...] + mr * g2
        h1b = h1.astype(BF16)
        gate = jax.nn.sigmoid(_dot(h1b, wg_ref[...]) + bg_ref[...])
        pb = p_ref[...].astype(BF16)
        wp_blocks = [slice(j * D_PLE, (j + 1) * D_PLE) for j in range(N_CHIPS)]
        pp = jnp.concatenate([_dot(pb, wp_ref[blk, :]) for blk in wp_blocks], axis=1)
        err = h1 + gate * pp - t_ref[...]
        loss_ref[...] += 0.5 * jnp.sum(jnp.mean(err * err, axis=-1, keepdims=True), axis=0, keepdims=True)

        dh2 = err * (1.0 / D_MODEL)
        d_glin = dh2 * pp * gate * (1.0 - gate)
        d_glin_b = d_glin.astype(BF16)
        dwg_ref[...] += _dot_tn(h1b, d_glin_b)
        dbg_ref[...] += jnp.sum(d_glin, axis=0, keepdims=True)
        d_pp = (dh2 * gate).astype(BF16)
        for blk in wp_blocks:
            dwp_ref[blk, :] += _dot_tn(pb, d_pp[:, blk])
        dh1 = dh2 + _dot_nt(d_glin_b, wg_ref[...])
        dh1_ref[...] = dh1
        dg2_ref[...] += jnp.sum(dh1 * mr, axis=0, keepdims=True)
        a_ = dh1 * g2
        d_mixed = (r * a_ - mr * (r * jnp.mean(a_ * mr, axis=-1, keepdims=True))).astype(BF16)
        dwo_ref[...] += _dot_tn(cat, d_mixed)
        d_cat = _dot_nt(d_mixed, wo_ref[...])
        dso_ref[...] = d_cat[:, 0:512]
        dao_ref[...] = d_cat[:, 512:1024]

    return _call(
        body, name="tail_fwd_bwd", grid=(rows // tm,),
        in_specs=[_rows(tm, 512), _rows(tm, 512), _rows(tm, D_MODEL), _rows(tm, D_PLE), _rows(tm, D_MODEL),
                  _full((D_MODEL, D_MODEL)), _full((1, D_MODEL)), _full((D_MODEL, D_MODEL)), _full((1, D_MODEL)),
                  _full((N_CHIPS * D_PLE, D_PLE))],
        out_specs=[_rows(tm, D_MODEL), _rows(tm, 512), _rows(tm, 512), _full((D_MODEL, D_MODEL)),
                   _full((D_MODEL, D_MODEL)), _full((N_CHIPS * D_PLE, D_PLE)), _full((1, D_MODEL)), _full((1, D_MODEL)),
                   _full((1, 1))],
        out_shape=[_sds((rows, D_MODEL)), _sds((rows, 512)), _sds((rows, 512)), _sds((D_MODEL, D_MODEL)),
                   _sds((D_MODEL, D_MODEL)), _sds((N_CHIPS * D_PLE, D_PLE)), _sds((1, D_MODEL)), _sds((1, D_MODEL)),
                   _sds((1, 1))],
        compiler_params=_params(52, ("arbitrary",)),
    )(ssm_out, attn_out, x2d, p2d, target, w_out, g2, w_gate, b_gate, w_proj)


def _local_step(x, hn, p, target, pre_norm_g, w_in_t, s5_params, s5_operands, ssm_d, w_glu, b_glu, sinks, w_out,
                post_norm_g, w_proj, w_gate, b_gate, send_beside=lambda ready: (ready["ssm_bc"], ready["ssm_bc"])):
    n_seq, seq, _ = x.shape
    rows = n_seq * seq
    x2d = x.reshape(rows, D_MODEL)
    p2d = p.reshape(rows, D_PLE)
    t2d = target.reshape(rows, D_MODEL)

    l_re, l_im, bt_re, bt_im, cm_re, cm_im = s5_operands

    u_scan, zs, q, k, v, za = _in_proj(hn, w_in_t, n_seq, seq)
    y_scan, h_re, h_im = _s5_scan_fwd(u_scan, bt_re, bt_im, cm_re, cm_im, l_re, l_im, ssm_d, n_seq, seq)
    ssm_out = _glu_fwd(y_scan, zs, w_glu, b_glu, n_seq, seq)
    o, attn_out, lse = _attn_fwd(q, k, v, za, sinks, n_seq, seq)

    dh1, d_so, d_ao, d_w_out, d_w_gate, d_w_proj, d_b_gate, d_g2, loss = _tail(
        ssm_out, attn_out, x2d, p2d, t2d, w_out, post_norm_g, w_gate, b_gate, w_proj)

    dq, dk, dv, dza, d_sinks = _attn_bwd(q, k, v, za, o, lse, d_ao, sinks, n_seq, seq)
    dy_scan, dzs, d_w_glu, d_b_glu = _glu_bwd(y_scan, zs, d_so, w_glu, b_glu, n_seq, seq)
    du_scan, d_bt_re, d_bt_im, d_cm_re, d_cm_im, d_l_re, d_l_im, d_d = _s5_scan_bwd(
        dy_scan, u_scan, h_re, h_im, bt_re, bt_im, cm_re, cm_im, l_re, l_im, ssm_d, n_seq, seq)
    d_lam_re, d_lam_im, d_log_step, d_bc = _s5_params_bwd(
        s5_params, (d_l_re, d_l_im, d_bt_re, d_bt_im, d_cm_re, d_cm_im))

    sent, arrived = send_beside(dict(w_out=d_w_out, pl_w_gate=d_w_gate, pl_w_proj=d_w_proj, ssm_bc=d_bc))
    d_proj, d_w_in_early, d_w_in_early_b = _in_proj_bwd_early(hn, du_scan, dzs, dq, dk, dv, dza, sent, n_seq, seq)
    grad_x, d_w_in_late, d_g1 = _in_proj_bwd(x2d, dh1, pre_norm_g, w_in_t, d_proj, arrived)
    grads = dict(
        pre_norm_g=d_g1, w_in_early=d_w_in_early, w_in_early_bf16=d_w_in_early_b, w_in_late=d_w_in_late,
        ssm_lam_re=d_lam_re, ssm_lam_im=d_lam_im, ssm_log_step=d_log_step, ssm_bc=d_bc, ssm_d=d_d, ssm_w_glu=d_w_glu,
        ssm_b_glu=d_b_glu, attn_sinks=d_sinks, w_out=d_w_out, post_norm_g=d_g2, pl_w_proj=d_w_proj,
        pl_w_gate=d_w_gate, pl_b_gate=d_b_gate)
    return grad_x.reshape(x.shape), loss, grads


_BIG = ("w_in", "ssm_w_glu", "w_out", "pl_w_proj", "pl_w_gate")
_BIG_SHARD = {"w_in": (D_IN // N_CHIPS, D_MODEL), "ssm_w_glu": (D_SSM // N_CHIPS, D_SSM),
              "w_out": (D_MODEL // N_CHIPS, D_MODEL), "pl_w_proj": (D_PLE, D_MODEL // N_CHIPS),
              "pl_w_gate": (D_MODEL // N_CHIPS, D_MODEL)}
_SMALL = {"pre_norm_g": (1, D_MODEL), "ssm_lam_re": (SSM_GROUPS, SSM_STATE), "ssm_lam_im": (SSM_GROUPS, SSM_STATE),
          "ssm_log_step": (1, SSM_GROUPS), "ssm_b_re": (D_SSM, SSM_STATE), "ssm_b_im": (D_SSM, SSM_STATE),
          "ssm_c_re": (D_SSM, SSM_STATE), "ssm_c_im": (D_SSM, SSM_STATE), "ssm_d": (1, D_SSM), "ssm_b_glu": (1, D_SSM),
          "attn_sinks": (1, N_HEADS), "post_norm_g": (1, D_MODEL), "pl_b_gate": (1, D_MODEL)}
_VEC_ROWS = ("pre_norm_g", "post_norm_g", "pl_b_gate", "ssm_d", "ssm_b_glu", "attn_sinks", "ssm_log_step", "loss")
_SMALL_GROUPS = (
    ("vec", (8, D_MODEL), tuple((name, r) for r, name in enumerate(_VEC_ROWS))),
    ("lam", (2 * SSM_GROUPS, SSM_STATE), (("ssm_lam_re", 0), ("ssm_lam_im", SSM_GROUPS))),
)
_SMALL_EARLY = ("ssm_b_re", "ssm_b_im", "ssm_c_re", "ssm_c_im")
_SMALL_ORDER = tuple(name for _, _, members in _SMALL_GROUPS for name, _ in members) + _SMALL_EARLY
_WEIGHT_ORDER = ("pre_norm_g", "w_in", "ssm_lam_re", "ssm_lam_im", "ssm_log_step", "ssm_b_re", "ssm_b_im", "ssm_c_re",
                 "ssm_c_im", "ssm_d", "ssm_w_glu", "ssm_b_glu", "attn_sinks", "w_out", "post_norm_g", "pl_w_proj",
                 "pl_w_gate", "pl_b_gate")


def _small_shape(name):
    return (1, 1) if name == "loss" else _SMALL[name]


def _to_kernel_form(name, a):
    a = a[0]
    if name == "w_in":
        return a.T
    if name in ("ssm_b_re", "ssm_b_im"):
        a = a.transpose(0, 2, 1)
    return a.reshape(_SMALL[name]) if name in _SMALL else a


def _from_kernel_form(name, a, shape):
    if name == "w_in":
        a = a.T
    if name in ("ssm_b_re", "ssm_b_im"):
        a = a.reshape(SSM_GROUPS, SSM_GROUP_CH, SSM_STATE).transpose(0, 2, 1)
    return a.reshape(shape)


def _mesh_place():
    x, y, c = lax.axis_index("x"), lax.axis_index("y"), lax.axis_index("c")
    other_chips = ((1 - x, y), (x, 1 - y), (1 - x, 1 - y))
    return x, y, c, other_chips


def _gather_copies(s_refs, g_refs, send_sems, recv_sems, local_sems):
    x, y, c, other_chips = _mesh_place()
    started = []
    for i, (s_ref, g_ref) in enumerate(zip(s_refs, g_refs)):
        rows = s_ref.shape[0]
        half = rows // 2

        def block(chip, g_ref=g_ref, rows=rows, half=half):
            return g_ref.at[pl.ds((2 * chip[0] + chip[1]) * rows + c * half, half), :]

        def copy(k, chip, to, src=None, i=i, block=block):
            return pltpu.make_async_remote_copy(
                src_ref=block(chip) if src is None else src, dst_ref=block(chip), send_sem=send_sems.at[6 * i + k],
                recv_sem=recv_sems.at[6 * i + k], device_id=to, device_id_type=MESH)

        own = pltpu.make_async_copy(s_ref, g_ref.at[pl.ds((2 * x + y) * rows, rows), :], local_sems.at[i])
        own.start()
        first = [copy(k, (x, y), (*chip, c), src=s_ref.at[pl.ds(c * half, half), :])
                 for k, chip in enumerate(other_chips)]
        for cp in first:
            cp.start()
        passed = [copy(3 + k, chip, (x, y, 1 - c)) for k, chip in enumerate(other_chips)]
        started.append((own, first, passed))
    for own, first, passed in started:
        for k in range(3):
            first[k].wait_recv()
            passed[k].start()
    for own, first, passed in started:
        for k in range(3):
            passed[k].wait_recv()
        for cp in first + passed:
            cp.wait_send()
        own.wait()


def _gather_semaphores(n_t):
    return [pltpu.SemaphoreType.DMA((6 * n_t,)), pltpu.SemaphoreType.DMA((6 * n_t,)), pltpu.SemaphoreType.DMA((n_t,))]


def _gather_weights_beside(shards, name, collective_id):
    n_t = len(shards)
    hbm = pltpu.MemorySpace.HBM
    s_refs = [jax.new_ref(s, memory_space=hbm) for s in shards]
    g_refs = [jax.empty_ref(jax.ShapeDtypeStruct((N_CHIPS * s.shape[0], s.shape[1]), s.dtype), memory_space=hbm)
              for s in shards]

    def launch(send_sems, recv_sems, local_sems):
        x, y, c, other_chips = _mesh_place()
        peers = [(*chip, c) for chip in other_chips] + [(x, y, 1 - c)]
        barrier = pltpu.get_barrier_semaphore()
        for peer in peers:
            pl.semaphore_signal(barrier, inc=1, device_id=peer, device_id_type=MESH)
        pl.semaphore_wait(barrier, len(peers))
        _gather_copies(s_refs, g_refs, send_sems, recv_sems, local_sems)

    pl.kernel(launch, mesh=plsc.ScalarSubcoreMesh(axis_name="sequencer", num_cores=1), name=name,
              scratch_types=_gather_semaphores(n_t), compiler_params=pltpu.CompilerParams(collective_id=collective_id))()
    return [g[...] for g in g_refs]


_RELATIONS = tuple(((r >> 2) & 1, (r >> 1) & 1, r & 1) for r in range(1, 8))


def _related(place, relation):
    return tuple(1 - a if flip else a for a, flip in zip(place, relation))


def _scatter_beside(mats, name, collective_id):
    hbm = pltpu.MemorySpace.HBM
    src_refs = [jax.new_ref(a, memory_space=hbm) for a in mats]
    land_refs = [jax.empty_ref(jax.ShapeDtypeStruct((7, a.shape[0] // 8, a.shape[1]), a.dtype), memory_space=hbm)
                 for a in mats]

    def launch(send_sems, recv_sems):
        me = (lax.axis_index("x"), lax.axis_index("y"), lax.axis_index("c"))
        peers = [_related(me, rel) for rel in _RELATIONS]
        barrier = pltpu.get_barrier_semaphore()
        for peer in peers:
            pl.semaphore_signal(barrier, inc=1, device_id=peer, device_id_type=MESH)
        pl.semaphore_wait(barrier, len(peers))
        copies = []
        for i, (src, land) in enumerate(zip(src_refs, land_refs)):
            hr = land.shape[1]
            for k, (tx, ty, tc) in enumerate(peers):
                rows = pl.ds((2 * tx + ty) * 2 * hr + tc * hr, hr)
                copies.append(pltpu.make_async_remote_copy(
                    src_ref=src.at[rows, :], dst_ref=land.at[k], send_sem=send_sems.at[7 * i + k],
                    recv_sem=recv_sems.at[7 * i + k], device_id=(tx, ty, tc), device_id_type=MESH))
                copies[-1].start()
        for cp in copies:
            cp.wait()

    n_sems = 7 * len(mats)
    pl.kernel(launch, mesh=plsc.ScalarSubcoreMesh(axis_name="sequencer", num_cores=1), name=name,
              scratch_types=[pltpu.SemaphoreType.DMA((n_sems,)), pltpu.SemaphoreType.DMA((n_sems,))],
              compiler_params=pltpu.CompilerParams(collective_id=collective_id))()
    return [ref[...] for ref in land_refs]


def _broadcast_beside(arrays):
    hbm = pltpu.MemorySpace.HBM
    src_refs = [jax.new_ref(a, memory_space=hbm) for a in arrays]
    land_refs = [jax.empty_ref(jax.ShapeDtypeStruct((len(_RELATIONS),) + a.shape, a.dtype), memory_space=hbm)
                 for a in arrays]

    def launch(send_sems, recv_sems):
        me = (lax.axis_index("x"), lax.axis_index("y"), lax.axis_index("c"))
        peers = [_related(me, rel) for rel in _RELATIONS]
        barrier = pltpu.get_barrier_semaphore()
        for peer in peers:
            pl.semaphore_signal(barrier, inc=1, device_id=peer, device_id_type=MESH)
        pl.semaphore_wait(barrier, len(peers))
        copies = []
        for i, (src, land) in enumerate(zip(src_refs, land_refs)):
            for k, peer in enumerate(peers):
                copies.append(pltpu.make_async_remote_copy(
                    src_ref=src, dst_ref=land.at[k], send_sem=send_sems.at[7 * i + k],
                    recv_sem=recv_sems.at[7 * i + k], device_id=peer, device_id_type=MESH))
                copies[-1].start()
        for cp in copies:
            cp.wait()

    n_sems = 7 * len(arrays)
    pl.kernel(launch, mesh=plsc.ScalarSubcoreMesh(axis_name="sequencer", num_cores=1), name="broadcast_beside",
              scratch_types=[pltpu.SemaphoreType.DMA((n_sems,)), pltpu.SemaphoreType.DMA((n_sems,))],
              compiler_params=pltpu.CompilerParams(collective_id=3))()
    return [ref[...] for ref in land_refs]


def _exchange_grads(big, small, landed, own_bc, landed_bc):
    n_t = len(big)
    n_g = len(_SMALL_GROUPS)
    names = _SMALL_ORDER
    halves = [(b.shape[0] // N_CHIPS // 2, b.shape[1]) for b in big]
    early = sorted(landed)
    late = [i for i in range(n_t) if i not in landed]
    n_sems = 4 * n_g + 7 * len(late) + n_t
    small_sem0, block_sem0 = n_t, n_t + len(names)
    early_sem0 = block_sem0 + N_CHIPS * len(late)
    landed_sem0 = early_sem0 + 2 * len(early)
    sent = [n for n in names if n in small]

    def body(*refs):
        pos = 0

        def take(n):
            nonlocal pos
            pos += n
            return refs[pos - n:pos]

        big_refs, small_refs = take(n_t), dict(zip(sent, take(len(sent))))
        land_refs = dict(zip(early, take(len(early))))
        own_bc_ref, landed_bc_ref = take(2)
        out_refs, small_out_refs = take(n_t), dict(zip(names, take(len(names))))
        per_late = lambda: dict(zip(late, take(len(late))))
        ga, gb, pme, send_b, recv_b = per_late(), per_late(), take(n_t), per_late(), per_late()
        own_e, land_e = dict(zip(early, take(len(early)))), dict(zip(early, take(len(early))))
        own_s, land_s = take(2)
        s_own, s_sib, s_chips, s_pair = take(n_g), take(n_g), take(n_g), take(n_g)
        stage = dict(zip(names, take(len(names))))
        send_sems, recv_sems, local_sems = take(3)
        x, y, c, other_chips = _mesh_place()
        me = 2 * x + y
        sibling = (x, y, 1 - c)
        sem_at = iter(range(n_sems))

        def remote(src, dst, to):
            k = next(sem_at)
            return pltpu.make_async_remote_copy(src_ref=src, dst_ref=dst, send_sem=send_sems.at[k],
                                                recv_sem=recv_sems.at[k], device_id=to, device_id_type=MESH)

        loads = [pltpu.make_async_copy(small_refs[name], stage[name], local_sems.at[small_sem0 + names.index(name)])
                 for name in sent]
        landed_loads = [pltpu.make_async_copy(own_bc_ref, own_s, local_sems.at[landed_sem0]),
                        pltpu.make_async_copy(landed_bc_ref, land_s, local_sems.at[landed_sem0 + 1])]
        for cp in loads + landed_loads:
            cp.start()
        for cp in loads:
            cp.wait()
        small_swaps = []
        for gi, (_, _, members) in enumerate(_SMALL_GROUPS):
            s_own[gi][...] = jnp.zeros_like(s_own[gi])
            for name, r0 in members:
                r, n = _small_shape(name)
                s_own[gi][r0:r0 + r, 0:n] = stage[name][...]
            small_swaps.append(remote(s_own[gi], s_sib[gi], sibling))
            small_swaps[gi].start()
        order = sorted(late, key=lambda i: halves[i][0] * halves[i][1])
        own_loads, big_swaps = {}, {}
        for i in order:
            hr = halves[i][0]
            own_loads[i], big_swaps[i] = [], []
            for j in range(N_CHIPS):
                mine = big_refs[i].at[pl.ds(j * 2 * hr + c * hr, hr), :]
                theirs = big_refs[i].at[pl.ds(j * 2 * hr + (1 - c) * hr, hr), :]
                sem = local_sems.at[block_sem0 + N_CHIPS * late.index(i) + j]
                own_loads[i].append(pltpu.make_async_copy(mine, ga[i].at[j], sem))
                own_loads[i][j].start()
                big_swaps[i].append(remote(theirs, gb[i].at[j], sibling))
                big_swaps[i][j].start()
        early_loads = {}
        for e, i in enumerate(early):
            hr = halves[i][0]
            mine = big_refs[i].at[pl.ds(me * 2 * hr + c * hr, hr), :]
            early_loads[i] = [pltpu.make_async_copy(mine, own_e[i], local_sems.at[early_sem0 + 2 * e]),
                              pltpu.make_async_copy(land_refs[i], land_e[i], local_sems.at[early_sem0 + 2 * e + 1])]
            for cp in early_loads[i]:
                cp.start()
        small_sends = []
        for gi in range(n_g):
            small_swaps[gi].wait_recv()
            s_pair[gi][...] = s_own[gi][...] + s_sib[gi][...]
            small_sends.append([remote(s_pair[gi], s_chips[gi].at[k], (*chip, c)) for k, chip in enumerate(other_chips)])
            for cp in small_sends[gi]:
                cp.start()

        def pair_sum(i, j):
            return ga[i][j] + gb[i][j]

        big_sends = {}
        for i in order:
            for j in range(N_CHIPS):
                own_loads[i][j].wait()
                big_swaps[i][j].wait_recv()
            big_sends[i] = []
            for k, chip in enumerate(other_chips):
                send_b[i][k] = pair_sum(i, 2 * chip[0] + chip[1]).astype(BF16)
                big_sends[i].append(remote(send_b[i].at[k], recv_b[i].at[k], (*chip, c)))
                big_sends[i][k].start()
        last_swaps, keeps = {}, {}
        for i in early + order:
            hr = halves[i][0]
            if i in landed:
                for cp in early_loads[i]:
                    cp.wait()
                total = own_e[i][...]
                for k in range(len(_RELATIONS)):
                    total = total + land_e[i][k].astype(F32)
                pme[i][...] = total
            else:
                for k in range(3):
                    big_sends[i][k].wait_recv()
                pme[i][...] = ((pair_sum(i, me) + recv_b[i][0].astype(F32)) + recv_b[i][1].astype(F32)) + recv_b[i][2].astype(F32)
            mine = out_refs[i].at[pl.ds(c * hr, hr), :]
            keeps[i] = pltpu.make_async_copy(pme[i], mine, local_sems.at[i])
            keeps[i].start()
            last_swaps[i] = remote(pme[i], mine, sibling)
            last_swaps[i].start()

        for gi, (_, _, members) in enumerate(_SMALL_GROUPS):
            for k in range(3):
                small_sends[gi][k].wait_recv()
            total = None
            for j in range(N_CHIPS):
                rel = jnp.bitwise_xor(j, me)
                term = jnp.where(rel == 0, s_pair[gi][...], jnp.where(
                    rel == 2, s_chips[gi][0], jnp.where(rel == 1, s_chips[gi][1], s_chips[gi][2])))
                total = term if total is None else total + term
            s_sib[gi][...] = total
            for name, r0 in members:
                r, n = _small_shape(name)
                stage[name][...] = s_sib[gi][r0:r0 + r, 0:n]
        my_index = 4 * x + 2 * y + c
        for cp in landed_loads:
            cp.wait()
        total = None
        for d in range(2 * N_CHIPS):
            rel = jnp.bitwise_xor(d, my_index)
            term = own_s[...]
            for k in range(len(_RELATIONS)):
                term = jnp.where(rel == k + 1, land_s[k], term)
            total = term.astype(F32) if total is None else total + term.astype(F32)
        for a, name in enumerate(_SMALL_EARLY):
            stage[name][...] = total[:, a * SSM_STATE:(a + 1) * SSM_STATE]
        stores = [pltpu.make_async_copy(stage[name], small_out_refs[name], local_sems.at[small_sem0 + a])
                  for a, name in enumerate(names)]
        for cp in stores:
            cp.start()

        for i in range(n_t):
            last_swaps[i].wait_recv()
            keeps[i].wait()
        for cp in stores:
            cp.wait()
        groups = list(big_swaps.values()) + small_sends + list(big_sends.values())
        for cp in small_swaps + [cp for group in groups for cp in group] + list(last_swaps.values()):
            cp.wait_send()

    any_spec = pl.BlockSpec(memory_space=pl.ANY)
    small_shapes = [_sds(_small_shape(n)) for n in names]
    group_shapes = [shape for _, shape, _ in _SMALL_GROUPS]
    vmem = lambda which, dtype, lead=(): [pltpu.VMEM(lead + halves[i], dtype) for i in which]
    outs = _call(
        body, name="exchange_grads",
        in_specs=[any_spec] * (n_t + len(sent) + len(early) + 2),
        out_specs=[any_spec] * (n_t + len(names)),
        out_shape=[_sds((b.shape[0] // N_CHIPS, b.shape[1])) for b in big] + small_shapes,
        scratch_shapes=(vmem(late, F32, (N_CHIPS,)) + vmem(late, F32, (N_CHIPS,)) + vmem(range(n_t), F32)
                        + vmem(late, BF16, (3,)) + vmem(late, BF16, (3,))
                        + vmem(early, F32)
                        + [pltpu.VMEM((len(_RELATIONS),) + halves[i], landed[i].dtype) for i in early]
                        + [pltpu.VMEM(own_bc.shape, own_bc.dtype), pltpu.VMEM(landed_bc.shape, landed_bc.dtype)]
                        + [pltpu.VMEM(s, F32) for s in group_shapes] * 2 + [pltpu.VMEM((3,) + s, F32) for s in group_shapes]
                        + [pltpu.VMEM(s, F32) for s in group_shapes]
                        + [pltpu.VMEM(_small_shape(n), F32) for n in names]
                        + [pltpu.SemaphoreType.DMA((n_sems,)), pltpu.SemaphoreType.DMA((n_sems,)),
                           pltpu.SemaphoreType.DMA((landed_sem0 + 2,))]),
        compiler_params=_params(48),
    )(*big, *[small[n] for n in sent], *[landed[i] for i in early], own_bc, landed_bc)
    return list(outs[:n_t]), dict(zip(names, outs[n_t:n_t + len(names)]))


def _adamw_update(w, g, m, v):
    m = ADAM_B1 * m + (1.0 - ADAM_B1) * g
    v = ADAM_B2 * v + (1.0 - ADAM_B2) * (g * g)
    m_hat = m / (1.0 - ADAM_B1 ** ADAM_STEP)
    v_hat = v / (1.0 - ADAM_B2 ** ADAM_STEP)
    return -ADAM_LR * (m_hat / (jnp.sqrt(v_hat) + ADAM_EPS) + ADAM_WD * w), m, v


def _adamw(w, g, m, v, grid, name):
    n_t = len(w)

    def body(*refs):
        ins, outs = refs[:4 * n_t], refs[4 * n_t:]
        for i in range(n_t):
            w_, g_, m_, v_ = [ins[a * n_t + i][...] for a in range(4)]
            vals = (g_,) + _adamw_update(w_, g_, m_, v_)
            for a in range(4):
                outs[a * n_t + i][...] = vals[a]

    specs = [pl.BlockSpec((a.shape[0] // grid, a.shape[1]), lambda i: (i, 0)) for a in w]
    shapes = [_sds(a.shape) for a in w]
    outs = _call(
        body, name=name, grid=(grid,), in_specs=specs * 4, out_specs=specs * 4, out_shape=shapes * 4,
        compiler_params=_params(40, ("arbitrary",)),
    )(*w, *g, *m, *v)
    return [outs[a * n_t:(a + 1) * n_t] for a in range(4)]


def kernel(x, p, pre_norm_g, w_in, ssm_lam_re, ssm_lam_im, ssm_log_step, ssm_b_re, ssm_b_im, ssm_c_re, ssm_c_im, ssm_d, ssm_w_glu, ssm_b_glu, attn_sinks, w_out, post_norm_g, pl_w_proj, pl_w_gate, pl_b_gate, loss_target, m_pre_norm_g, m_w_in, m_ssm_lam_re, m_ssm_lam_im, m_ssm_log_step, m_ssm_b_re, m_ssm_b_im, m_ssm_c_re, m_ssm_c_im, m_ssm_d, m_ssm_w_glu, m_ssm_b_glu, m_attn_sinks, m_w_out, m_post_norm_g, m_pl_w_proj, m_pl_w_gate, m_pl_b_gate, v_pre_norm_g, v_w_in, v_ssm_lam_re, v_ssm_lam_im, v_ssm_log_step, v_ssm_b_re, v_ssm_b_im, v_ssm_c_re, v_ssm_c_im, v_ssm_d, v_ssm_w_glu, v_ssm_b_glu, v_attn_sinks, v_w_out, v_post_norm_g, v_pl_w_proj, v_pl_w_gate, v_pl_b_gate):
    weights = dict(pre_norm_g=pre_norm_g, w_in=w_in, ssm_lam_re=ssm_lam_re, ssm_lam_im=ssm_lam_im,
                   ssm_log_step=ssm_log_step, ssm_b_re=ssm_b_re, ssm_b_im=ssm_b_im, ssm_c_re=ssm_c_re,
                   ssm_c_im=ssm_c_im, ssm_d=ssm_d, ssm_w_glu=ssm_w_glu, ssm_b_glu=ssm_b_glu, attn_sinks=attn_sinks,
                   w_out=w_out, post_norm_g=post_norm_g, pl_w_proj=pl_w_proj, pl_w_gate=pl_w_gate, pl_b_gate=pl_b_gate)
    m_in = dict(pre_norm_g=m_pre_norm_g, w_in=m_w_in, ssm_lam_re=m_ssm_lam_re, ssm_lam_im=m_ssm_lam_im,
                ssm_log_step=m_ssm_log_step, ssm_b_re=m_ssm_b_re, ssm_b_im=m_ssm_b_im, ssm_c_re=m_ssm_c_re,
                ssm_c_im=m_ssm_c_im, ssm_d=m_ssm_d, ssm_w_glu=m_ssm_w_glu, ssm_b_glu=m_ssm_b_glu,
                attn_sinks=m_attn_sinks, w_out=m_w_out, post_norm_g=m_post_norm_g, pl_w_proj=m_pl_w_proj,
                pl_w_gate=m_pl_w_gate, pl_b_gate=m_pl_b_gate)
    v_in = dict(pre_norm_g=v_pre_norm_g, w_in=v_w_in, ssm_lam_re=v_ssm_lam_re, ssm_lam_im=v_ssm_lam_im,
                ssm_log_step=v_ssm_log_step, ssm_b_re=v_ssm_b_re, ssm_b_im=v_ssm_b_im, ssm_c_re=v_ssm_c_re,
                ssm_c_im=v_ssm_c_im, ssm_d=v_ssm_d, ssm_w_glu=v_ssm_w_glu, ssm_b_glu=v_ssm_b_glu,
                attn_sinks=v_attn_sinks, w_out=v_w_out, post_norm_g=v_post_norm_g, pl_w_proj=v_pl_w_proj,
                pl_w_gate=v_pl_w_gate, pl_b_gate=v_pl_b_gate)

    def two_d(tree):
        return {k: _to_kernel_form(k, a) for k, a in tree.items()}

    w2, m2, v2 = two_d(weights), two_d(m_in), two_d(v_in)

    (w_in_full,) = _gather_weights_beside([w2["w_in"].astype(BF16)], "gather_w_in_beside", 4)
    s5_params = tuple(w2[n] for n in ("ssm_lam_re", "ssm_lam_im", "ssm_log_step", "ssm_b_re", "ssm_b_im", "ssm_c_re",
                                      "ssm_c_im"))
    s5_operands = _s5_params_fwd(*s5_params)
    hn = _pre_norm(x.reshape(-1, D_MODEL), w2["pre_norm_g"])
    behind = s5_operands[0][0, 0] * 0.0 + hn[0, 0].astype(F32) * 0.0
    rest = _gather_weights_beside([(w2[n] + behind).astype(BF16) for n in _BIG[1:]], "gather_weights_beside", 1)
    full = dict(zip(_BIG, [w_in_full] + rest))
    mats = ("w_in_early", "w_in_late") + _BIG[1:]
    landed, bc = {}, {}

    def send_beside(ready):
        sent_early = ("w_out", "pl_w_gate", "pl_w_proj")
        landed.update(zip([mats.index(n) for n in sent_early],
                          _scatter_beside([ready[n] for n in sent_early], "scatter_beside", 2)))
        after_scatter = landed[mats.index(sent_early[-1])][0, 0, 0] * 0.0
        bc["own"] = ready["ssm_bc"] + after_scatter.astype(BF16)
        bc["landed"] = _broadcast_beside([bc["own"]])[0]
        return bc["own"], bc["landed"]

    grad_x, loss, grads = _local_step(
        x, hn, p, loss_target, w2["pre_norm_g"], full["w_in"], s5_params, s5_operands, w2["ssm_d"], full["ssm_w_glu"], w2["ssm_b_glu"],
        w2["attn_sinks"], full["w_out"], w2["post_norm_g"], full["pl_w_proj"], full["pl_w_gate"], w2["pl_b_gate"], send_beside)

    landed[0] = _scatter_beside([grads["w_in_early_bf16"]], "scatter_w_in_beside", 5)[0]
    sent_here = {**{n: grads[n] for n in _SMALL if n not in _SMALL_EARLY}, "loss": loss}
    g_big, g_small = _exchange_grads([grads[n] for n in mats], sent_here, landed, bc["own"], bc["landed"])
    g_big = dict(zip(mats, g_big))
    g_big["w_in"] = jnp.concatenate([g_big.pop("w_in_early"), g_big.pop("w_in_late")], axis=1)
    total_loss = g_small.pop("loss")

    big_out = _adamw([w2[n] for n in _BIG], [g_big[n] for n in _BIG], [m2[n] for n in _BIG], [v2[n] for n in _BIG],
                     8, "adamw_matrices")
    small_names = tuple(_SMALL)
    small_out = _adamw([w2[n] for n in small_names], [g_small[n] for n in small_names], [m2[n] for n in small_names],
                       [v2[n] for n in small_names], 1, "adamw_small")

    results = [{**dict(zip(_BIG, big_part)), **dict(zip(small_names, small_part))}
               for big_part, small_part in zip(big_out, small_out)]
    flat = [_from_kernel_form(name, r[name], weights[name].shape) for r in results for name in _WEIGHT_ORDER]
    return (total_loss.reshape(()), grad_x, *flat)
```

```python
import math

import jax
import jax.numpy as jnp
from jax import lax
from jax.experimental import pallas as pl
from jax.experimental.pallas import tpu as pltpu
from jax.experimental.pallas import tpu_sc as plsc

F32 = jnp.float32
BF16 = jnp.bfloat16

D_MODEL = 1024
D_SSM = 512
D_ATTN = 512
SSM_GROUPS = 32
SSM_GROUP_CH = 16
SSM_STATE = 64
SSM_LANES = SSM_GROUPS * SSM_STATE
HEAD_DIM = 64
N_HEADS = 8
KV_HEADS = 2
Q_PER_KV = 4
WINDOW = 128
BLOCK = 128
D_PLE = 256
D_IN = 2304
EPS = 1e-6
ATTN_SCALE = 1.0 / math.sqrt(HEAD_DIM)

ADAM_LR = 0.001
ADAM_B1 = 0.9
ADAM_B2 = 0.999
ADAM_EPS = 1e-08
ADAM_WD = 0.01
ADAM_STEP = 10

N_CHIPS = 4
LANES = 128
SCAN_CHUNKS = 8
SCAN_TILE_STEPS = 32
SCAN_LANE_CHUNK = 512
MIB = 2 ** 20
MESH = pl.DeviceIdType.MESH


def _dot(a, b):
    return jnp.dot(a, b, preferred_element_type=F32)


def _dot_nt(a, b):
    return lax.dot_general(a, b, (((1,), (1,)), ((), ())), preferred_element_type=F32)


def _dot_tn(a, b):
    return lax.dot_general(a, b, (((0,), (0,)), ((), ())), preferred_element_type=F32)


def _params(vmem_mib, semantics=None):
    kw = dict(vmem_limit_bytes=vmem_mib * MIB)
    if semantics is not None:
        kw["dimension_semantics"] = semantics
    return pltpu.CompilerParams(**kw)


def _full(shape):
    nd = len(shape)
    return pl.BlockSpec(shape, lambda *_: (0,) * nd, pipeline_mode=pl.Buffered(1))


def _rows(tm, width):
    return pl.BlockSpec((tm, width), lambda i: (i, 0))


def _sds(shape, dtype=F32):
    return pltpu.HBM(shape, dtype)


def _call(body, **kw):
    fn = pl.pallas_call(body, **kw)
    return lambda *args: fn(*[pltpu.with_memory_space_constraint(a, pltpu.HBM) for a in args])


def _silu(z):
    return z * jax.nn.sigmoid(z)


def _pre_norm(x2d, g1):
    rows = x2d.shape[0]
    tm = 512

    def body(x_ref, g_ref, hn_ref):
        x = x_ref[...]
        r = lax.rsqrt(jnp.mean(x * x, axis=-1, keepdims=True) + EPS)
        hn_ref[...] = (x * r * g_ref[...]).astype(BF16)

    return _call(
        body, name="pre_norm", grid=(rows // tm,), in_specs=[_rows(tm, D_MODEL), _full((1, D_MODEL))],
        out_specs=_rows(tm, D_MODEL), out_shape=_sds((rows, D_MODEL), BF16), compiler_params=_params(32, ("arbitrary",)),
    )(x2d, g1)


def _in_proj(hn, w_in_t, n_seq, seq):
    rows = hn.shape[0]
    tm = 1024
    slab, steps, _, _ = _scan_geometry(n_seq, seq)

    def body(hn_ref, w_ref, *out_refs):
        u_parts, (zs_ref, q_ref, k_ref, v_ref, za_ref) = out_refs[:_SCAN_PARTS], out_refs[_SCAN_PARTS:]
        whole = _dot_nt(hn_ref[...], w_ref[...])

        def proj(a, b):
            return whole[:, a:b]

        _store_chunks(u_parts, pl.program_id(0) * (tm // steps), proj(0, 512), steps, slab)
        zs_ref[...] = proj(512, 1024)
        q_ref[...] = (proj(1024, 1536) * ATTN_SCALE).astype(BF16)
        k_ref[...] = proj(1536, 1664).astype(BF16)
        v_ref[...] = proj(1664, 1792).astype(BF16)
        za_ref[...] = proj(1792, 2304)

    *u_parts, zs, q, k, v, za = _call(
        body, name="in_proj", grid=(rows // tm,),
        in_specs=[_rows(tm, D_MODEL), _full((D_IN, D_MODEL))],
        out_specs=_whole_parts(rows) + [_rows(tm, 512), _rows(tm, 512), _rows(tm, 128), _rows(tm, 128), _rows(tm, 512)],
        out_shape=_part_shapes(rows) + [_sds((rows, 512)), _sds((rows, 512), BF16), _sds((rows, 128), BF16),
                                        _sds((rows, 128), BF16), _sds((rows, 512))],
        compiler_params=_params(48, ("arbitrary",)),
    )(hn, w_in_t)
    return u_parts, zs, q, k, v, za


_EARLY_COLS = 5 * D_MODEL // 8


def _in_proj_bwd_early(hn, du_parts, dzs, dq, dk, dv, dza, runs_after, n_seq, seq):
    rows = hn.shape[0]
    tm = 512
    slab, steps, _, _ = _scan_geometry(n_seq, seq)

    def body(hn_ref, *refs):
        du_parts, (dzs_ref, dq_ref, dk_ref, dv_ref, dza_ref, _, dproj_ref, dw_ref, dwb_ref) = refs[:_SCAN_PARTS], refs[_SCAN_PARTS:]
        i = pl.program_id(0)

        @pl.when(i == 0)
        def _():
            dw_ref[...] = jnp.zeros_like(dw_ref)

        du = _load_chunks(du_parts, i * (tm // steps), tm // steps, steps, slab)
        d_proj = jnp.concatenate([du.astype(BF16), dzs_ref[...], dq_ref[...], dk_ref[...], dv_ref[...], dza_ref[...]],
                                 axis=1)
        dproj_ref[...] = d_proj
        dw_ref[...] += _dot_tn(d_proj, hn_ref[...])

        @pl.when(i == rows // tm - 1)
        def _():
            dwb_ref[...] = dw_ref[...].astype(BF16)

    return _call(
        body, name="in_proj_bwd_early", grid=(rows // tm,),
        in_specs=[_rows(tm, _EARLY_COLS)] + _whole_parts(rows)
        + [_rows(tm, 512), _rows(tm, 512), _rows(tm, 128), _rows(tm, 128), _rows(tm, 512),
           pl.BlockSpec(memory_space=pl.ANY)],
        out_specs=[_rows(tm, D_IN), _full((D_IN, _EARLY_COLS)), _full((D_IN, _EARLY_COLS))],
        out_shape=[_sds((rows, D_IN), BF16), _sds((D_IN, _EARLY_COLS)), _sds((D_IN, _EARLY_COLS), BF16)],
        compiler_params=_params(48, ("arbitrary",)),
    )(hn, *du_parts, dzs, dq, dk, dv, dza, runs_after)


def _in_proj_bwd(x2d, dh1, g1, w_in_t, d_proj, runs_after):
    rows = x2d.shape[0]
    tm = 512

    def body(x_ref, dh1_ref, g_ref, w_ref, dproj_ref, _, gx_ref, dw_ref, dg_ref):
        @pl.when(pl.program_id(0) == 0)
        def _():
            dw_ref[...] = jnp.zeros_like(dw_ref)
            dg_ref[...] = jnp.zeros_like(dg_ref)

        x = x_ref[...]
        g = g_ref[...]
        r = lax.rsqrt(jnp.mean(x * x, axis=-1, keepdims=True) + EPS)
        xr = x * r
        hn = (xr[:, _EARLY_COLS:] * g[:, _EARLY_COLS:]).astype(BF16)
        d_proj = dproj_ref[...]
        dhn = _dot(d_proj, w_ref[...])
        dw_ref[...] += _dot_tn(d_proj, hn)
        dg_ref[...] += jnp.sum(dhn * xr, axis=0, keepdims=True)
        a_ = dhn * g
        gx_ref[...] = dh1_ref[...] + r * a_ - xr * (r * jnp.mean(a_ * xr, axis=-1, keepdims=True))

    late_cols = D_MODEL - _EARLY_COLS
    return _call(
        body, name="in_proj_bwd", grid=(rows // tm,),
        in_specs=[_rows(tm, D_MODEL), _rows(tm, D_MODEL), _full((1, D_MODEL)), _full((D_IN, D_MODEL)), _rows(tm, D_IN),
                  pl.BlockSpec(memory_space=pl.ANY)],
        out_specs=[_rows(tm, D_MODEL), _full((D_IN, late_cols)), _full((1, D_MODEL))],
        out_shape=[_sds((rows, D_MODEL)), _sds((D_IN, late_cols)), _sds((1, D_MODEL))],
        compiler_params=_params(52, ("arbitrary",)),
    )(x2d, dh1, g1, w_in_t, d_proj, runs_after)


def _iota(shape, axis):
    return lax.broadcasted_iota(jnp.int32, shape, axis)


def _sum_of_thirds(f, a):
    hi = a.astype(BF16)
    rest = a - hi.astype(F32)
    mid = rest.astype(BF16)
    low = (rest - mid.astype(F32)).astype(BF16)
    return (f(hi) + f(mid)) + f(low)


@jax.custom_vjp
def _pick_rows(e, a):
    return _sum_of_thirds(lambda part: _dot(e, part), a)


def _pick_rows_fwd(e, a):
    return _pick_rows(e, a), e


def _pick_rows_bwd(e, ct):
    return jnp.zeros_like(e), _sum_of_thirds(lambda part: _dot_tn(e, part), ct)


_pick_rows.defvjp(_pick_rows_fwd, _pick_rows_bwd)


@jax.custom_vjp
def _pick_cols(a, e):
    return _sum_of_thirds(lambda part: _dot(part, e), a)


def _pick_cols_fwd(a, e):
    return _pick_cols(a, e), e


def _pick_cols_bwd(e, ct):
    return _sum_of_thirds(lambda part: _dot_nt(part, e), ct), jnp.zeros_like(e)


_pick_cols.defvjp(_pick_cols_fwd, _pick_cols_bwd)


_HALF_GROUPS = SSM_GROUPS // 2
_N_SHIFT = SSM_STATE.bit_length() - 1
_P_SHIFT = SSM_GROUP_CH.bit_length() - 1


def _s5_operands(lam_re, lam_im, log_step, b_re, b_im, c_re, c_im):
    g, n, p = SSM_GROUPS, SSM_STATE, SSM_GROUP_CH
    gn, gp, hn_, hp = g * n, g * p, _HALF_GROUPS * n, _HALF_GROUPS * p
    eye_g = _iota((g, g), 0) == _iota((g, g), 1)
    step = jnp.sum(jnp.where(eye_g, jnp.exp(log_step), 0.0), axis=1, keepdims=True)
    a_re = lam_re * step
    a_im = lam_im * step
    mag = jnp.exp(a_re)
    lbar_re = mag * jnp.cos(a_im)
    lbar_im = mag * jnp.sin(a_im)
    n_re = lbar_re - 1.0
    den = lam_re * lam_re + lam_im * lam_im
    f_re = (n_re * lam_re + lbar_im * lam_im) / den
    f_im = (lbar_im * lam_re - n_re * lam_im) / den

    spread_n = (_iota((n, gn), 0) == (_iota((n, gn), 1) & (n - 1))).astype(BF16)
    own_g = _iota((g, gn), 0) == (_iota((g, gn), 1) >> _N_SHIFT)

    def to_row(a):
        return jnp.sum(jnp.where(own_g, _pick_cols(a, spread_n), 0.0), axis=0, keepdims=True)

    per_group = ((_iota((gp, g), 0) >> _P_SHIFT) == _iota((gp, g), 1)).astype(BF16)
    fx_re, fx_im = _pick_rows(per_group, f_re), _pick_rows(per_group, f_im)
    bbar_re = fx_re * b_re - fx_im * b_im
    bbar_im = fx_re * b_im + fx_im * b_re

    tile_n = (_iota((n, hn_), 0) == (_iota((n, hn_), 1) & (n - 1))).astype(BF16)
    same_group = (_iota((hp, hn_), 0) >> _P_SHIFT) == (_iota((hp, hn_), 1) >> _N_SHIFT)

    def embed(a, hf):
        return jnp.where(same_group, _pick_cols(a[hf * hp:(hf + 1) * hp], tile_n), 0.0)

    return (to_row(lbar_re), to_row(lbar_im), embed(bbar_re, 0), embed(bbar_re, 1), embed(bbar_im, 0),
            embed(bbar_im, 1), embed(c_re, 0), embed(c_re, 1), embed(c_im, 0), embed(c_im, 1))


_S5_PARAM_SHAPES = ((SSM_GROUPS, SSM_STATE), (SSM_GROUPS, SSM_STATE), (1, SSM_GROUPS),
                    (D_SSM, SSM_STATE), (D_SSM, SSM_STATE), (D_SSM, SSM_STATE), (D_SSM, SSM_STATE))
_CM_SHAPE = (2, _HALF_GROUPS * SSM_GROUP_CH, _HALF_GROUPS * SSM_STATE)
_S5_OPERAND_SHAPES = ((1, SSM_LANES), (1, SSM_LANES), _CM_SHAPE, _CM_SHAPE, _CM_SHAPE, _CM_SHAPE)


def _s5_params_fwd(*params):
    def body(*refs):
        ins, (lre_ref, lim_ref, btre_ref, btim_ref, cmre_ref, cmim_ref) = refs[:7], refs[7:]
        vals = _s5_operands(*[r[...] for r in ins])
        lre_ref[...] = vals[0]
        lim_ref[...] = vals[1]
        for ref, pair in zip((btre_ref, btim_ref, cmre_ref, cmim_ref), (vals[2:4], vals[4:6], vals[6:8], vals[8:10])):
            ref[0] = pair[0].astype(BF16)
            ref[1] = pair[1].astype(BF16)

    dtypes = (F32, F32, BF16, BF16, BF16, BF16)
    return _call(
        body, name="s5_params_fwd",
        in_specs=[_full(s) for s in _S5_PARAM_SHAPES], out_specs=[_full(s) for s in _S5_OPERAND_SHAPES],
        out_shape=[_sds(s, d) for s, d in zip(_S5_OPERAND_SHAPES, dtypes)], compiler_params=_params(32),
    )(*params)


_BC_SIDE_BY_SIDE = (D_SSM, 4 * SSM_STATE)


def _s5_params_bwd(params, cotangents):
    def body(*refs):
        ins, (dlre, dlim, dbtre, dbtim, dcmre, dcmim), outs = refs[:7], refs[7:13], refs[13:]
        _, vjp = jax.vjp(_s5_operands, *[r[...] for r in ins])
        cts = (dlre[...], dlim[...], dbtre[0], dbtre[1], dbtim[0], dbtim[1], dcmre[0], dcmre[1], dcmim[0], dcmim[1])
        grads = vjp(cts)
        for ref, val in zip(outs[:3], grads[:3]):
            ref[...] = val
        outs[3][...] = jnp.concatenate(grads[3:], axis=1).astype(BF16)

    out_shapes = _S5_PARAM_SHAPES[:3] + (_BC_SIDE_BY_SIDE,)
    return _call(
        body, name="s5_params_bwd",
        in_specs=[_full(s) for s in _S5_PARAM_SHAPES + _S5_OPERAND_SHAPES],
        out_specs=[_full(s) for s in out_shapes],
        out_shape=[_sds(s, d) for s, d in zip(out_shapes, (F32, F32, F32, BF16))], compiler_params=_params(48),
    )(*params, *cotangents)


def _scan_geometry(n_seq, seq):
    slab = n_seq * SCAN_CHUNKS
    steps = seq // SCAN_CHUNKS
    tile_rows = slab * SCAN_TILE_STEPS
    n_tiles = steps // SCAN_TILE_STEPS
    return slab, steps, tile_rows, n_tiles


_SCAN_PARTS = D_SSM // LANES


def _whole_parts(rows):
    return [_full((rows, LANES))] * _SCAN_PARTS


def _part_shapes(rows):
    return [_sds((rows, LANES))] * _SCAN_PARTS


def _load_chunks(parts, first_chunk, n_chunks, steps, slab):
    return jnp.concatenate([
        jnp.concatenate([ref[pl.ds(first_chunk + q, steps, stride=slab), :] for ref in parts], axis=1)
        for q in range(n_chunks)], axis=0)


def _store_chunks(parts, first_chunk, value, steps, slab):
    for q in range(value.shape[0] // steps):
        for j, ref in enumerate(parts):
            ref[pl.ds(first_chunk + q, steps, stride=slab), :] = value[q * steps:(q + 1) * steps,
                                                                     j * LANES:(j + 1) * LANES]


def _join_parts(parts):
    return jnp.concatenate([ref[...] for ref in parts], axis=1)


def _split_parts(parts, value):
    for j, ref in enumerate(parts):
        ref[...] = value[:, j * LANES:(j + 1) * LANES]


def _complex_power(re, im, n):
    out = None
    while n:
        if n & 1:
            out = (re, im) if out is None else (out[0] * re - out[1] * im, out[0] * im + out[1] * re)
        n >>= 1
        if n:
            re, im = re * re - im * im, 2.0 * re * im
    return out


def _chunk_carry(sum_re, sum_im, carry_re, carry_im, a_re, a_im, n_seq, reverse):
    carry_re[...] = jnp.zeros_like(carry_re)
    carry_im[...] = jnp.zeros_like(carry_im)
    for s in range(n_seq):
        order = range(SCAN_CHUNKS - 2, -1, -1) if reverse else range(1, SCAN_CHUNKS)
        for c in order:
            r = s * SCAN_CHUNKS + c
            p = r + 1 if reverse else r - 1
            p_re, p_im = carry_re[p:p + 1, :], carry_im[p:p + 1, :]
            carry_re[r:r + 1, :] = a_re * p_re - a_im * p_im + sum_re[p:p + 1, :]
            carry_im[r:r + 1, :] = a_re * p_im + a_im * p_re + sum_im[p:p + 1, :]


def _s5_scan_fwd(u_parts, bt_re, bt_im, cm_re, cm_im, lbar_re, lbar_im, d_row, n_seq, seq):
    slab, steps, tile_rows, n_tiles = _scan_geometry(n_seq, seq)
    rows = u_parts[0].shape[0]

    def body(*refs):
        u_refs, refs = refs[:_SCAN_PARTS], refs[_SCAN_PARTS:]
        (bre_ref, bim_ref, cre_ref, cim_ref, lre_ref, lim_ref, d_ref), refs = refs[:7], refs[7:]
        y_refs, (hre_ref, him_ref, st_re, st_im, h0_re, h0_im, buf_re, buf_im) = refs[:_SCAN_PARTS], refs[_SCAN_PARTS:]
        second = pl.program_id(0) == 1
        i = pl.program_id(1)

        @pl.when(jnp.logical_and(i == 0, jnp.logical_not(second)))
        def _():
            st_re[...] = jnp.zeros_like(st_re)
            st_im[...] = jnp.zeros_like(st_im)

        u = _join_parts(u_refs)
        ub = u.astype(BF16)
        for hf in range(2):
            cols = slice(hf * 1024, (hf + 1) * 1024)
            buf_re[:, cols] = _dot(ub[:, hf * 256:(hf + 1) * 256], bre_ref[hf])
            buf_im[:, cols] = _dot(ub[:, hf * 256:(hf + 1) * 256], bim_ref[hf])

        for lc in range(SSM_LANES // SCAN_LANE_CHUNK):
            cols = slice(lc * SCAN_LANE_CHUNK, (lc + 1) * SCAN_LANE_CHUNK)
            l_re = jnp.broadcast_to(lre_ref[:, cols], (slab, SCAN_LANE_CHUNK))
            l_im = jnp.broadcast_to(lim_ref[:, cols], (slab, SCAN_LANE_CHUNK))

            def scan_tile(keep_states):
                def step(t, carry):
                    s_re, s_im = carry
                    r0 = pl.multiple_of(t * slab, slab)
                    n_re = l_re * s_re - l_im * s_im + buf_re[pl.ds(r0, slab), cols]
                    n_im = l_re * s_im + l_im * s_re + buf_im[pl.ds(r0, slab), cols]
                    if keep_states:
                        buf_re[pl.ds(r0, slab), cols] = n_re
                        buf_im[pl.ds(r0, slab), cols] = n_im
                    return n_re, n_im

                s_re, s_im = lax.fori_loop(0, SCAN_TILE_STEPS, step, (st_re[:, cols], st_im[:, cols]), unroll=True)
                st_re[:, cols] = s_re
                st_im[:, cols] = s_im

            pl.when(jnp.logical_not(second))(lambda: scan_tile(False))
            pl.when(second)(lambda: scan_tile(True))

        @pl.when(jnp.logical_and(i == n_tiles - 1, jnp.logical_not(second)))
        def _():
            a_re, a_im = _complex_power(lre_ref[...], lim_ref[...], steps)
            _chunk_carry(st_re, st_im, h0_re, h0_im, a_re, a_im, n_seq, reverse=False)
            st_re[...] = h0_re[...]
            st_im[...] = h0_im[...]

        @pl.when(second)
        def _():
            h_re = buf_re[...].astype(BF16)
            h_im = buf_im[...].astype(BF16)
            hre_ref[...] = h_re
            him_ref[...] = h_im
            for hf in range(2):
                cols = slice(hf * 1024, (hf + 1) * 1024)
                ycols = slice(hf * 256, (hf + 1) * 256)
                y_half = (_dot_nt(h_re[:, cols], cre_ref[hf]) - _dot_nt(h_im[:, cols], cim_ref[hf])
                          + d_ref[:, ycols] * u[:, ycols])
                _split_parts(y_refs[2 * hf:2 * hf + 2], y_half)

    tile = lambda w: pl.BlockSpec((tile_rows, w), lambda p, i: (i, 0))
    out_tile = lambda w: pl.BlockSpec((tile_rows, w), lambda p, i: (i * p, 0))
    cm = _full(_CM_SHAPE)
    outs = _call(
        body, name="s5_scan_fwd", grid=(2, n_tiles),
        in_specs=[tile(LANES)] * _SCAN_PARTS + [cm, cm, cm, cm, _full((1, SSM_LANES)), _full((1, SSM_LANES)),
                                                _full((1, 512))],
        out_specs=[out_tile(LANES)] * _SCAN_PARTS + [out_tile(SSM_LANES), out_tile(SSM_LANES)],
        out_shape=_part_shapes(rows) + [_sds((rows, SSM_LANES), BF16), _sds((rows, SSM_LANES), BF16)],
        scratch_shapes=[pltpu.VMEM((slab, SSM_LANES), F32)] * 4 + [pltpu.VMEM((tile_rows, SSM_LANES), F32)] * 2,
        compiler_params=_params(40, ("arbitrary", "arbitrary")),
    )(*u_parts, bt_re, bt_im, cm_re, cm_im, lbar_re, lbar_im, d_row)
    return outs[:_SCAN_PARTS], outs[_SCAN_PARTS], outs[_SCAN_PARTS + 1]


def _s5_scan_bwd(dy_parts, u_parts, h_re, h_im, bt_re, bt_im, cm_re, cm_im, lbar_re, lbar_im, d_row, n_seq, seq):
    slab, steps, tile_rows, n_tiles = _scan_geometry(n_seq, seq)
    rows = u_parts[0].shape[0]

    def body(*refs):
        dy_refs, u_refs, refs = refs[:_SCAN_PARTS], refs[_SCAN_PARTS:2 * _SCAN_PARTS], refs[2 * _SCAN_PARTS:]
        (hre_ref, him_ref, bre_ref, bim_ref, cre_ref, cim_ref, lre_ref, lim_ref, d_ref), refs = refs[:9], refs[9:]
        du_refs, refs = refs[:_SCAN_PARTS], refs[_SCAN_PARTS:]
        (dbre_ref, dbim_ref, dcre_ref, dcim_ref, dlre_ref, dlim_ref, dd_ref,
         st_re, st_im, g0_re, g0_im, acc_re, acc_im, buf_re, buf_im) = refs
        second = pl.program_id(0) == 1
        i = pl.program_id(1)

        @pl.when(jnp.logical_and(i == 0, jnp.logical_not(second)))
        def _():
            st_re[...] = jnp.zeros_like(st_re)
            st_im[...] = jnp.zeros_like(st_im)
            acc_re[...] = jnp.zeros_like(acc_re)
            acc_im[...] = jnp.zeros_like(acc_im)
            for ref in (dbre_ref, dbim_ref, dcre_ref, dcim_ref, dd_ref):
                ref[...] = jnp.zeros_like(ref)

        dy = _join_parts(dy_refs)
        dyb = dy.astype(BF16)
        for hf in range(2):
            cols = slice(hf * 1024, (hf + 1) * 1024)
            buf_re[:, cols] = _dot(dyb[:, hf * 256:(hf + 1) * 256], cre_ref[hf])
            buf_im[:, cols] = -_dot(dyb[:, hf * 256:(hf + 1) * 256], cim_ref[hf])

        for lc in range(SSM_LANES // SCAN_LANE_CHUNK):
            cols = slice(lc * SCAN_LANE_CHUNK, (lc + 1) * SCAN_LANE_CHUNK)
            l_re = jnp.broadcast_to(lre_ref[:, cols], (slab, SCAN_LANE_CHUNK))
            l_im = jnp.broadcast_to(lim_ref[:, cols], (slab, SCAN_LANE_CHUNK))

            def advance(r0, s_re, s_im):
                n_re = l_re * s_re + l_im * s_im + buf_re[pl.ds(r0, slab), cols]
                n_im = l_re * s_im - l_im * s_re + buf_im[pl.ds(r0, slab), cols]
                buf_re[pl.ds(r0, slab), cols] = n_re
                buf_im[pl.ds(r0, slab), cols] = n_im
                return n_re, n_im

            def row0(k):
                return pl.multiple_of((SCAN_TILE_STEPS - 1 - k) * slab, slab)

            @pl.when(jnp.logical_not(second))
            def _():
                s_re, s_im = lax.fori_loop(0, SCAN_TILE_STEPS, lambda k, s: advance(row0(k), *s),
                                           (st_re[:, cols], st_im[:, cols]), unroll=True)
                st_re[:, cols] = s_re
                st_im[:, cols] = s_im

            @pl.when(second)
            def _():
                def step(k, carry):
                    s_re, s_im, a_re, a_im = carry
                    r0 = row0(k)
                    hr = hre_ref[pl.ds(r0, slab), cols].astype(F32)
                    hi = him_ref[pl.ds(r0, slab), cols].astype(F32)
                    a_re = a_re + s_re * hr + s_im * hi
                    a_im = a_im + s_im * hr - s_re * hi
                    return advance(r0, s_re, s_im) + (a_re, a_im)

                zero = jnp.zeros((slab, SCAN_LANE_CHUNK), F32)
                s_re, s_im, a_re, a_im = lax.fori_loop(
                    0, SCAN_TILE_STEPS, step, (st_re[:, cols], st_im[:, cols], zero, zero), unroll=True)
                st_re[:, cols] = s_re
                st_im[:, cols] = s_im
                acc_re[:, cols] += a_re
                acc_im[:, cols] += a_im

        @pl.when(jnp.logical_and(i == n_tiles - 1, jnp.logical_not(second)))
        def _():
            p_re, p_im = _complex_power(lre_ref[...], lim_ref[...], steps)
            _chunk_carry(st_re, st_im, g0_re, g0_im, p_re, -p_im, n_seq, reverse=True)
            st_re[...] = g0_re[...]
            st_im[...] = g0_im[...]

        @pl.when(second)
        def _():
            u = _join_parts(u_refs)
            ub = u.astype(BF16)
            g_re = buf_re[...].astype(BF16)
            g_im = buf_im[...].astype(BF16)
            dd_ref[...] += jnp.sum(dy * u, axis=0, keepdims=True)
            for hf in range(2):
                cols = slice(hf * 1024, (hf + 1) * 1024)
                ycols = slice(hf * 256, (hf + 1) * 256)
                du_half = (_dot_nt(g_re[:, cols], bre_ref[hf]) + _dot_nt(g_im[:, cols], bim_ref[hf])
                           + d_ref[:, ycols] * dy[:, ycols])
                _split_parts(du_refs[2 * hf:2 * hf + 2], du_half)
                for q4 in range(_HALF_GROUPS // 4):
                    ch = slice(hf * 256 + q4 * 64, hf * 256 + (q4 + 1) * 64)
                    st = slice(hf * 1024 + q4 * 256, hf * 1024 + (q4 + 1) * 256)
                    blk = (hf, slice(q4 * 64, (q4 + 1) * 64), slice(q4 * 256, (q4 + 1) * 256))
                    dbre_ref[blk] += _dot_tn(ub[:, ch], g_re[:, st])
                    dbim_ref[blk] += _dot_tn(ub[:, ch], g_im[:, st])
                    dcre_ref[blk] += _dot_tn(dyb[:, ch], hre_ref[:, st])
                    dcim_ref[blk] -= _dot_tn(dyb[:, ch], him_ref[:, st])

        @pl.when(jnp.logical_and(i == n_tiles - 1, second))
        def _():
            dlre_ref[...] = jnp.sum(acc_re[...], axis=0, keepdims=True)
            dlim_ref[...] = jnp.sum(acc_im[...], axis=0, keepdims=True)

    tile = lambda w: pl.BlockSpec((tile_rows, w), lambda p, i: (n_tiles - 1 - i, 0))
    second_tile = lambda w: pl.BlockSpec((tile_rows, w), lambda p, i: (n_tiles - 1 - i * p, 0))
    cm = _full(_CM_SHAPE)
    row = _full((1, SSM_LANES))
    outs = _call(
        body, name="s5_scan_bwd", grid=(2, n_tiles),
        in_specs=[tile(LANES)] * _SCAN_PARTS + [second_tile(LANES)] * _SCAN_PARTS
        + [second_tile(SSM_LANES), second_tile(SSM_LANES), cm, cm, cm, cm, row, row, _full((1, 512))],
        out_specs=[second_tile(LANES)] * _SCAN_PARTS + [cm, cm, cm, cm, row, row, _full((1, 512))],
        out_shape=(_part_shapes(rows) + [_sds(_CM_SHAPE)] * 4 + [_sds((1, SSM_LANES))] * 2 + [_sds((1, 512))]),
        scratch_shapes=[pltpu.VMEM((slab, SSM_LANES), F32)] * 6 + [pltpu.VMEM((tile_rows, SSM_LANES), F32)] * 2,
        compiler_params=_params(48, ("arbitrary", "arbitrary")),
    )(*dy_parts, *u_parts, h_re, h_im, bt_re, bt_im, cm_re, cm_im, lbar_re, lbar_im, d_row)
    return (outs[:_SCAN_PARTS],) + tuple(outs[_SCAN_PARTS:])


def _glu_gate(gl, a, zs):
    return gl * jax.nn.sigmoid(a) * _silu(zs)


def _glu_fwd(y_parts, zs, w_glu, b_glu, n_seq, seq):
    rows = zs.shape[0]
    tm = 512
    slab, steps, _, _ = _scan_geometry(n_seq, seq)

    def body(*refs):
        y_refs, (zs_ref, w_ref, b_ref, o_ref) = refs[:_SCAN_PARTS], refs[_SCAN_PARTS:]
        y = _load_chunks(y_refs, pl.program_id(0) * (tm // steps), tm // steps, steps, slab)
        gl = jax.nn.gelu(y)
        a = _dot(gl.astype(BF16), w_ref[...]) + b_ref[...]
        o_ref[...] = _glu_gate(gl, a, zs_ref[...]).astype(BF16)

    return _call(
        body, name="glu_fwd", grid=(rows // tm,),
        in_specs=_whole_parts(rows) + [_rows(tm, 512), _full((512, 512)), _full((1, 512))],
        out_specs=_rows(tm, 512), out_shape=_sds((rows, 512), BF16),
        compiler_params=_params(32, ("arbitrary",)),
    )(*y_parts, zs, w_glu, b_glu)


def _glu_bwd(y_parts, zs, d_out, w_glu, b_glu, n_seq, seq):
    rows = zs.shape[0]
    tm = 512
    slab, steps, _, _ = _scan_geometry(n_seq, seq)

    def body(*refs):
        y_refs, (zs_ref, d_ref, w_ref, b_ref), refs = refs[:_SCAN_PARTS], refs[_SCAN_PARTS:_SCAN_PARTS + 4], refs[_SCAN_PARTS + 4:]
        dy_refs, (dzs_ref, dw_ref, db_ref) = refs[:_SCAN_PARTS], refs[_SCAN_PARTS:]
        first_chunk = pl.program_id(0) * (tm // steps)

        @pl.when(pl.program_id(0) == 0)
        def _():
            dw_ref[...] = jnp.zeros_like(dw_ref)
            db_ref[...] = jnp.zeros_like(db_ref)

        gl, gelu_vjp = jax.vjp(jax.nn.gelu, _load_chunks(y_refs, first_chunk, tm // steps, steps, slab))
        glb = gl.astype(BF16)
        a = _dot(glb, w_ref[...]) + b_ref[...]
        _, gate_vjp = jax.vjp(_glu_gate, gl, a, zs_ref[...])
        d_gl, d_a, d_zs = gate_vjp(d_ref[...])
        dab = d_a.astype(BF16)
        d_gl = d_gl + _dot_nt(dab, w_ref[...])
        _store_chunks(dy_refs, first_chunk, gelu_vjp(d_gl)[0], steps, slab)
        dzs_ref[...] = d_zs.astype(BF16)
        dw_ref[...] += _dot_tn(glb, dab)
        db_ref[...] += jnp.sum(d_a, axis=0, keepdims=True)

    *dy_parts, dzs, dw, db = _call(
        body, name="glu_bwd", grid=(rows // tm,),
        in_specs=_whole_parts(rows) + [_rows(tm, 512), _rows(tm, 512), _full((512, 512)), _full((1, 512))],
        out_specs=_whole_parts(rows) + [_rows(tm, 512), _full((512, 512)), _full((1, 512))],
        out_shape=_part_shapes(rows) + [_sds((rows, 512), BF16), _sds((512, 512)), _sds((1, 512))],
        compiler_params=_params(40, ("arbitrary",)),
    )(*y_parts, zs, d_out, w_glu, b_glu)
    return dy_parts, dzs, dw, db


_GROUP_ROWS = Q_PER_KV * BLOCK
_BLOCK_SHIFT = BLOCK.bit_length() - 1


def _attn_bias(j):
    query = _iota((BLOCK, _GROUP_ROWS), 1)
    dist_cur = (query & (BLOCK - 1)) - _iota((BLOCK, _GROUP_ROWS), 0)
    dist_prev = dist_cur + BLOCK
    head = query >> _BLOCK_SHIFT
    slope = jnp.zeros((BLOCK, _GROUP_ROWS), F32)
    for g in range(Q_PER_KV):
        slope = jnp.where(head == g, 2.0 ** (-(j * Q_PER_KV + g + 1)), slope)
    bias_cur = jnp.where(dist_cur >= 0, -slope * dist_cur.astype(F32), -jnp.inf)
    bias_prev = jnp.where(dist_prev < WINDOW, -slope * dist_prev.astype(F32), -jnp.inf)
    return bias_cur, bias_prev


_ATTN_BIAS_SCRATCH = pltpu.VMEM((KV_HEADS, 2, BLOCK, _GROUP_ROWS), F32)


def _fill_attn_bias(bias_ref):
    @pl.when(jnp.logical_and(pl.program_id(0) == 0, pl.program_id(1) == 0))
    def _():
        for j in range(KV_HEADS):
            bias_ref[j, 0], bias_ref[j, 1] = _attn_bias(j)


def _stack_heads(x, j):
    heads = range(j * Q_PER_KV, (j + 1) * Q_PER_KV)
    return jnp.concatenate([x[:, h * HEAD_DIM:(h + 1) * HEAD_DIM] for h in heads], axis=0)


def _head_rows(x, j):
    heads = range(j * Q_PER_KV, (j + 1) * Q_PER_KV)
    return jnp.concatenate([x[h:h + 1, :] for h in heads], axis=1)


def _sink_row(sk_ref, j):
    heads = range(j * Q_PER_KV, (j + 1) * Q_PER_KV)
    return jnp.concatenate([jnp.broadcast_to(sk_ref[0:1, h:h + 1], (1, BLOCK)) for h in heads], axis=1)


def _attn_fwd(q, k, v, za, sinks, n_seq, seq):
    nb = seq // BLOCK
    rows = q.shape[0]

    def body(q_ref, kc_ref, kp_ref, vc_ref, vp_ref, za_ref, sk_ref, o_ref, ao_ref, lse_ref, bias_ref):
        _fill_attn_bias(bias_ref)
        has_prev = pl.program_id(1) > 0
        q_all = q_ref[...]
        for j in range(KV_HEADS):
            js = slice(j * HEAD_DIM, (j + 1) * HEAD_DIM)
            bias_c, bias_p = bias_ref[j, 0], bias_ref[j, 1]
            q4 = _stack_heads(q_all, j)
            sc = _dot_nt(kc_ref[:, js], q4) + bias_c
            sp = _dot_nt(kp_ref[:, js], q4) + jnp.where(has_prev, bias_p, -jnp.inf)
            sink = _sink_row(sk_ref, j)
            m = jnp.maximum(jnp.max(jnp.maximum(sc, sp), axis=0, keepdims=True), sink)
            ec = jnp.exp(sc - m)
            ep = jnp.exp(sp - m)
            den = jnp.sum(ec + ep, axis=0, keepdims=True) + jnp.exp(sink - m)
            inv = 1.0 / den
            o4 = _dot_tn((ec * inv).astype(BF16), vc_ref[:, js]) + _dot_tn((ep * inv).astype(BF16), vp_ref[:, js])
            lse4 = m + jnp.log(den)
            for g in range(Q_PER_KV):
                h = j * Q_PER_KV + g
                o_ref[:, h * HEAD_DIM:(h + 1) * HEAD_DIM] = o4[g * BLOCK:(g + 1) * BLOCK]
                lse_ref[h:h + 1, :] = lse4[:, g * BLOCK:(g + 1) * BLOCK]
        ao_ref[...] = (o_ref[...] * _silu(za_ref[...])).astype(BF16)

    cur = lambda w: pl.BlockSpec((BLOCK, w), lambda b, n: (b * nb + n, 0))
    prev = lambda w: pl.BlockSpec((BLOCK, w), lambda b, n: (b * nb + jnp.maximum(n - 1, 0), 0))
    lse_rows = rows // BLOCK * N_HEADS
    return _call(
        body, name="attn_fwd", grid=(n_seq, nb),
        in_specs=[cur(512), cur(128), prev(128), cur(128), prev(128), cur(512), _full((1, N_HEADS))],
        out_specs=[cur(512), cur(512), pl.BlockSpec((N_HEADS, BLOCK), lambda b, n: (b * nb + n, 0))],
        out_shape=[_sds((rows, 512)), _sds((rows, 512), BF16), _sds((lse_rows, BLOCK))],
        scratch_shapes=[_ATTN_BIAS_SCRATCH], compiler_params=_params(32, ("arbitrary", "arbitrary")),
    )(q, k, k, v, v, za, sinks)


def _attn_bwd(q, k, v, za, o, lse, d_ao, sinks, n_seq, seq):
    nb = seq // BLOCK
    rows = q.shape[0]

    def body(q_ref, kc_ref, kp_ref, vc_ref, vp_ref, za_ref, o_ref, lse_ref, d_ref, sk_ref,
             dq_ref, dk_ref, dv_ref, dza_ref, dsk_ref, bias_ref, dk_carry, dv_carry):
        n = nb - 1 - pl.program_id(1)
        _fill_attn_bias(bias_ref)

        @pl.when(jnp.logical_and(pl.program_id(0) == 0, pl.program_id(1) == 0))
        def _():
            dsk_ref[...] = jnp.zeros_like(dsk_ref)
            dk_carry[...] = jnp.zeros_like(dk_carry)
            dv_carry[...] = jnp.zeros_like(dv_carry)

        has_prev = n > 0
        has_next = n + 1 < nb

        _, gate_vjp = jax.vjp(lambda o_, z_: o_ * _silu(z_), o_ref[...], za_ref[...])
        d_o, d_za = gate_vjp(d_ref[...])
        dza_ref[...] = d_za.astype(BF16)
        q_all = q_ref[...]
        lse_all = lse_ref[...]

        for j in range(KV_HEADS):
            js = slice(j * HEAD_DIM, (j + 1) * HEAD_DIM)
            kc, kp, vc, vp = kc_ref[:, js], kp_ref[:, js], vc_ref[:, js], vp_ref[:, js]
            bias_c, bias_p = bias_ref[j, 0], bias_ref[j, 1]
            q4 = _stack_heads(q_all, j)
            do4b = _stack_heads(d_o, j).astype(BF16)
            lse4 = _head_rows(lse_all, j)
            pc = jnp.exp(_dot_nt(kc, q4) + bias_c - lse4)
            pp = jnp.exp(_dot_nt(kp, q4) + jnp.where(has_prev, bias_p, -jnp.inf) - lse4)
            dpc = _dot_nt(vc, do4b)
            dpp = _dot_nt(vp, do4b)
            delta = jnp.sum(pc * dpc + pp * dpp, axis=0, keepdims=True)
            dsc = (pc * (dpc - delta)).astype(BF16)
            dsp = (pp * (dpp - delta)).astype(BF16)
            dq4 = ((_dot_tn(dsc, kc) + _dot_tn(dsp, kp)) * ATTN_SCALE).astype(BF16)
            sink_loss = jnp.exp(_sink_row(sk_ref, j) - lse4) * delta
            for g in range(Q_PER_KV):
                h = j * Q_PER_KV + g
                dq_ref[:, h * HEAD_DIM:(h + 1) * HEAD_DIM] = dq4[g * BLOCK:(g + 1) * BLOCK]
                dsk_ref[0:1, h:h + 1] -= jnp.sum(sink_loss[:, g * BLOCK:(g + 1) * BLOCK], axis=1, keepdims=True)
            dk = _dot(dsc, q4) + jnp.where(has_next, dk_carry[j], 0.0)
            dv = _dot(pc.astype(BF16), do4b) + jnp.where(has_next, dv_carry[j], 0.0)
            dk_carry[j] = _dot(dsp, q4)
            dv_carry[j] = _dot(pp.astype(BF16), do4b)
            dk_ref[:, js] = dk.astype(BF16)
            dv_ref[:, js] = dv.astype(BF16)

    cur = lambda w: pl.BlockSpec((BLOCK, w), lambda b, s: (b * nb + nb - 1 - s, 0))
    prev = lambda w: pl.BlockSpec((BLOCK, w), lambda b, s: (b * nb + jnp.maximum(nb - 2 - s, 0), 0))
    return _call(
        body, name="attn_bwd", grid=(n_seq, nb),
        in_specs=[cur(512), cur(128), prev(128), cur(128), prev(128), cur(512), cur(512),
                  pl.BlockSpec((N_HEADS, BLOCK), lambda b, s: (b * nb + nb - 1 - s, 0)), cur(512), _full((1, N_HEADS))],
        out_specs=[cur(512), cur(128), cur(128), cur(512), _full((1, N_HEADS))],
        out_shape=[_sds((rows, 512), BF16), _sds((rows, 128), BF16), _sds((rows, 128), BF16),
                   _sds((rows, 512), BF16), _sds((1, N_HEADS))],
        scratch_shapes=[_ATTN_BIAS_SCRATCH, pltpu.VMEM((KV_HEADS, BLOCK, HEAD_DIM), F32),
                        pltpu.VMEM((KV_HEADS, BLOCK, HEAD_DIM), F32)],
        compiler_params=_params(32, ("arbitrary", "arbitrary")),
    )(q, k, k, v, v, za, o, lse, d_ao, sinks)


def _tail(ssm_out, attn_out, x2d, p2d, target, w_out, g2, w_gate, b_gate, w_proj):
    rows = x2d.shape[0]
    tm = 512

    def body(so_ref, ao_ref, x_ref, p_ref, t_ref, wo_ref, g2_ref, wg_ref, bg_ref, wp_ref,
             dh1_ref, dso_ref, dao_ref, dwo_ref, dwg_ref, dwp_ref, dbg_ref, dg2_ref, loss_ref):
        @pl.when(pl.program_id(0) == 0)
        def _():
            for ref in (dwo_ref, dwg_ref, dwp_ref, dbg_ref, dg2_ref, loss_ref):
                ref[...] = jnp.zeros_like(ref)

        cat = jnp.concatenate([so_ref[...], ao_ref[...]], axis=1)
        g2 = g2_ref[...]
        mixed = _dot(cat, wo_ref[...])
        r = lax.rsqrt(jnp.mean(mixed * mixed, axis=-1, keepdims=True) + EPS)
        mr = mixed * r
        h1 = x_ref[...] + mr * g2
        h1b = h1.astype(BF16)
        gate = jax.nn.sigmoid(_dot(h1b, wg_ref[...]) + bg_ref[...])
        pb = p_ref[...].astype(BF16)
        wp_blocks = [slice(j * D_PLE, (j + 1) * D_PLE) for j in range(N_CHIPS)]
        pp = jnp.concatenate([_dot(pb, wp_ref[blk, :]) for blk in wp_blocks], axis=1)
        err = h1 + gate * pp - t_ref[...]
        loss_ref[...] += 0.5 * jnp.sum(jnp.mean(err * err, axis=-1, keepdims=True), axis=0, keepdims=True)

        dh2 = err * (1.0 / D_MODEL)
        d_glin = dh2 * pp * gate * (1.0 - gate)
        d_glin_b = d_glin.astype(BF16)
        dwg_ref[...] += _dot_tn(h1b, d_glin_b)
        dbg_ref[...] += jnp.sum(d_glin, axis=0, keepdims=True)
        d_pp = (dh2 * gate).astype(BF16)
        for blk in wp_blocks:
            dwp_ref[blk, :] += _dot_tn(pb, d_pp[:, blk])
        dh1 = dh2 + _dot_nt(d_glin_b, wg_ref[...])
        dh1_ref[...] = dh1
        dg2_ref[...] += jnp.sum(dh1 * mr, axis=0, keepdims=True)
        a_ = dh1 * g2
        d_mixed = (r * a_ - mr * (r * jnp.mean(a_ * mr, axis=-1, keepdims=True))).astype(BF16)
        dwo_ref[...] += _dot_tn(cat, d_mixed)
        d_cat = _dot_nt(d_mixed, wo_ref[...])
        dso_ref[...] = d_cat[:, 0:512]
        dao_ref[...] = d_cat[:, 512:1024]

    return _call(
        body, name="tail_fwd_bwd", grid=(rows // tm,),
        in_specs=[_rows(tm, 512), _rows(tm, 512), _rows(tm, D_MODEL), _rows(tm, D_PLE), _rows(tm, D_MODEL),
                  _full((D_MODEL, D_MODEL)), _full((1, D_MODEL)), _full((D_MODEL, D_MODEL)), _full((1, D_MODEL)),
                  _full((N_CHIPS * D_PLE, D_PLE))],
        out_specs=[_rows(tm, D_MODEL), _rows(tm, 512), _rows(tm, 512), _full((D_MODEL, D_MODEL)),
                   _full((D_MODEL, D_MODEL)), _full((N_CHIPS * D_PLE, D_PLE)), _full((1, D_MODEL)), _full((1, D_MODEL)),
                   _full((1, 1))],
        out_shape=[_sds((rows, D_MODEL)), _sds((rows, 512)), _sds((rows, 512)), _sds((D_MODEL, D_MODEL)),
                   _sds((D_MODEL, D_MODEL)), _sds((N_CHIPS * D_PLE, D_PLE)), _sds((1, D_MODEL)), _sds((1, D_MODEL)),
                   _sds((1, 1))],
        compiler_params=_params(52, ("arbitrary",)),
    )(ssm_out, attn_out, x2d, p2d, target, w_out, g2, w_gate, b_gate, w_proj)


def _local_step(x, hn, p, target, pre_norm_g, w_in_t, s5_params, s5_operands, ssm_d, w_glu, b_glu, sinks, w_out,
                post_norm_g, w_proj, w_gate, b_gate, send_beside=lambda ready: (ready["ssm_bc"], ready["ssm_bc"])):
    n_seq, seq, _ = x.shape
    rows = n_seq * seq
    x2d = x.reshape(rows, D_MODEL)
    p2d = p.reshape(rows, D_PLE)
    t2d = target.reshape(rows, D_MODEL)

    l_re, l_im, bt_re, bt_im, cm_re, cm_im = s5_operands

    u_scan, zs, q, k, v, za = _in_proj(hn, w_in_t, n_seq, seq)
    y_scan, h_re, h_im = _s5_scan_fwd(u_scan, bt_re, bt_im, cm_re, cm_im, l_re, l_im, ssm_d, n_seq, seq)
    ssm_out = _glu_fwd(y_scan, zs, w_glu, b_glu, n_seq, seq)
    o, attn_out, lse = _attn_fwd(q, k, v, za, sinks, n_seq, seq)

    dh1, d_so, d_ao, d_w_out, d_w_gate, d_w_proj, d_b_gate, d_g2, loss = _tail(
        ssm_out, attn_out, x2d, p2d, t2d, w_out, post_norm_g, w_gate, b_gate, w_proj)

    dq, dk, dv, dza, d_sinks = _attn_bwd(q, k, v, za, o, lse, d_ao, sinks, n_seq, seq)
    dy_scan, dzs, d_w_glu, d_b_glu = _glu_bwd(y_scan, zs, d_so, w_glu, b_glu, n_seq, seq)
    du_scan, d_bt_re, d_bt_im, d_cm_re, d_cm_im, d_l_re, d_l_im, d_d = _s5_scan_bwd(
        dy_scan, u_scan, h_re, h_im, bt_re, bt_im, cm_re, cm_im, l_re, l_im, ssm_d, n_seq, seq)
    d_lam_re, d_lam_im, d_log_step, d_bc = _s5_params_bwd(
        s5_params, (d_l_re, d_l_im, d_bt_re, d_bt_im, d_cm_re, d_cm_im))

    sent, arrived = send_beside(dict(ssm_w_glu=d_w_glu, w_out=d_w_out, pl_w_gate=d_w_gate, pl_w_proj=d_w_proj, ssm_bc=d_bc))
    d_proj, d_w_in_early, d_w_in_early_b = _in_proj_bwd_early(hn, du_scan, dzs, dq, dk, dv, dza, sent, n_seq, seq)
    grad_x, d_w_in_late, d_g1 = _in_proj_bwd(x2d, dh1, pre_norm_g, w_in_t, d_proj, arrived)
    grads = dict(
        pre_norm_g=d_g1, w_in_early=d_w_in_early, w_in_early_bf16=d_w_in_early_b, w_in_late=d_w_in_late,
        ssm_lam_re=d_lam_re, ssm_lam_im=d_lam_im, ssm_log_step=d_log_step, ssm_bc=d_bc, ssm_d=d_d, ssm_w_glu=d_w_glu,
        ssm_b_glu=d_b_glu, attn_sinks=d_sinks, w_out=d_w_out, post_norm_g=d_g2, pl_w_proj=d_w_proj,
        pl_w_gate=d_w_gate, pl_b_gate=d_b_gate)
    return grad_x.reshape(x.shape), loss, grads


_BIG = ("w_in", "ssm_w_glu", "w_out", "pl_w_proj", "pl_w_gate")
_BIG_SHARD = {"w_in": (D_IN // N_CHIPS, D_MODEL), "ssm_w_glu": (D_SSM // N_CHIPS, D_SSM),
              "w_out": (D_MODEL // N_CHIPS, D_MODEL), "pl_w_proj": (D_PLE, D_MODEL // N_CHIPS),
              "pl_w_gate": (D_MODEL // N_CHIPS, D_MODEL)}
_SMALL = {"pre_norm_g": (1, D_MODEL), "ssm_lam_re": (SSM_GROUPS, SSM_STATE), "ssm_lam_im": (SSM_GROUPS, SSM_STATE),
          "ssm_log_step": (1, SSM_GROUPS), "ssm_b_re": (D_SSM, SSM_STATE), "ssm_b_im": (D_SSM, SSM_STATE),
          "ssm_c_re": (D_SSM, SSM_STATE), "ssm_c_im": (D_SSM, SSM_STATE), "ssm_d": (1, D_SSM), "ssm_b_glu": (1, D_SSM),
          "attn_sinks": (1, N_HEADS), "post_norm_g": (1, D_MODEL), "pl_b_gate": (1, D_MODEL)}
_VEC_ROWS = ("pre_norm_g", "post_norm_g", "pl_b_gate", "ssm_d", "ssm_b_glu", "attn_sinks", "ssm_log_step", "loss")
_SMALL_GROUPS = (
    ("vec", (8, D_MODEL), tuple((name, r) for r, name in enumerate(_VEC_ROWS))),
    ("lam", (2 * SSM_GROUPS, SSM_STATE), (("ssm_lam_re", 0), ("ssm_lam_im", SSM_GROUPS))),
)
_SMALL_EARLY = ("ssm_b_re", "ssm_b_im", "ssm_c_re", "ssm_c_im")
_SMALL_ORDER = tuple(name for _, _, members in _SMALL_GROUPS for name, _ in members) + _SMALL_EARLY
_WEIGHT_ORDER = ("pre_norm_g", "w_in", "ssm_lam_re", "ssm_lam_im", "ssm_log_step", "ssm_b_re", "ssm_b_im", "ssm_c_re",
                 "ssm_c_im", "ssm_d", "ssm_w_glu", "ssm_b_glu", "attn_sinks", "w_out", "post_norm_g", "pl_w_proj",
                 "pl_w_gate", "pl_b_gate")


def _small_shape(name):
    return (1, 1) if name == "loss" else _SMALL[name]


def _to_kernel_form(name, a):
    a = a[0]
    if name == "w_in":
        return a.T
    if name in ("ssm_b_re", "ssm_b_im"):
        a = a.transpose(0, 2, 1)
    return a.reshape(_SMALL[name]) if name in _SMALL else a


def _from_kernel_form(name, a, shape):
    if name == "w_in":
        a = a.T
    if name in ("ssm_b_re", "ssm_b_im"):
        a = a.reshape(SSM_GROUPS, SSM_GROUP_CH, SSM_STATE).transpose(0, 2, 1)
    return a.reshape(shape)


def _mesh_place():
    x, y, c = lax.axis_index("x"), lax.axis_index("y"), lax.axis_index("c")
    other_chips = ((1 - x, y), (x, 1 - y), (1 - x, 1 - y))
    return x, y, c, other_chips


def _gather_copies(s_refs, g_refs, send_sems, recv_sems, local_sems):
    x, y, c, other_chips = _mesh_place()
    started = []
    for i, (s_ref, g_ref) in enumerate(zip(s_refs, g_refs)):
        rows = s_ref.shape[0]
        half = rows // 2

        def block(chip, g_ref=g_ref, rows=rows, half=half):
            return g_ref.at[pl.ds((2 * chip[0] + chip[1]) * rows + c * half, half), :]

        def copy(k, chip, to, src=None, i=i, block=block):
            return pltpu.make_async_remote_copy(
                src_ref=block(chip) if src is None else src, dst_ref=block(chip), send_sem=send_sems.at[6 * i + k],
                recv_sem=recv_sems.at[6 * i + k], device_id=to, device_id_type=MESH)

        own = pltpu.make_async_copy(s_ref, g_ref.at[pl.ds((2 * x + y) * rows, rows), :], local_sems.at[i])
        own.start()
        first = [copy(k, (x, y), (*chip, c), src=s_ref.at[pl.ds(c * half, half), :])
                 for k, chip in enumerate(other_chips)]
        for cp in first:
            cp.start()
        passed = [copy(3 + k, chip, (x, y, 1 - c)) for k, chip in enumerate(other_chips)]
        started.append((own, first, passed))
    for own, first, passed in started:
        for k in range(3):
            first[k].wait_recv()
            passed[k].start()
    for own, first, passed in started:
        for k in range(3):
            passed[k].wait_recv()
        for cp in first + passed:
            cp.wait_send()
        own.wait()


def _gather_semaphores(n_t):
    return [pltpu.SemaphoreType.DMA((6 * n_t,)), pltpu.SemaphoreType.DMA((6 * n_t,)), pltpu.SemaphoreType.DMA((n_t,))]


def _gather_weights_beside(shards, name, collective_id):
    n_t = len(shards)
    hbm = pltpu.MemorySpace.HBM
    s_refs = [jax.new_ref(s, memory_space=hbm) for s in shards]
    g_refs = [jax.empty_ref(jax.ShapeDtypeStruct((N_CHIPS * s.shape[0], s.shape[1]), s.dtype), memory_space=hbm)
              for s in shards]

    def launch(send_sems, recv_sems, local_sems):
        x, y, c, other_chips = _mesh_place()
        peers = [(*chip, c) for chip in other_chips] + [(x, y, 1 - c)]
        barrier = pltpu.get_barrier_semaphore()
        for peer in peers:
            pl.semaphore_signal(barrier, inc=1, device_id=peer, device_id_type=MESH)
        pl.semaphore_wait(barrier, len(peers))
        _gather_copies(s_refs, g_refs, send_sems, recv_sems, local_sems)

    pl.kernel(launch, mesh=plsc.ScalarSubcoreMesh(axis_name="sequencer", num_cores=1), name=name,
              scratch_types=_gather_semaphores(n_t), compiler_params=pltpu.CompilerParams(collective_id=collective_id))()
    return [g[...] for g in g_refs]


_RELATIONS = tuple(((r >> 2) & 1, (r >> 1) & 1, r & 1) for r in range(1, 8))


def _related(place, relation):
    return tuple(1 - a if flip else a for a, flip in zip(place, relation))


def _scatter_beside(mats, name, collective_id):
    hbm = pltpu.MemorySpace.HBM
    src_refs = [jax.new_ref(a, memory_space=hbm) for a in mats]
    land_refs = [jax.empty_ref(jax.ShapeDtypeStruct((7, a.shape[0] // 8, a.shape[1]), a.dtype), memory_space=hbm)
                 for a in mats]

    def launch(send_sems, recv_sems):
        me = (lax.axis_index("x"), lax.axis_index("y"), lax.axis_index("c"))
        peers = [_related(me, rel) for rel in _RELATIONS]
        barrier = pltpu.get_barrier_semaphore()
        for peer in peers:
            pl.semaphore_signal(barrier, inc=1, device_id=peer, device_id_type=MESH)
        pl.semaphore_wait(barrier, len(peers))
        copies = []
        for i, (src, land) in enumerate(zip(src_refs, land_refs)):
            hr = land.shape[1]
            for k, (tx, ty, tc) in enumerate(peers):
                rows = pl.ds((2 * tx + ty) * 2 * hr + tc * hr, hr)
                copies.append(pltpu.make_async_remote_copy(
                    src_ref=src.at[rows, :], dst_ref=land.at[k], send_sem=send_sems.at[7 * i + k],
                    recv_sem=recv_sems.at[7 * i + k], device_id=(tx, ty, tc), device_id_type=MESH))
                copies[-1].start()
        for cp in copies:
            cp.wait()

    n_sems = 7 * len(mats)
    pl.kernel(launch, mesh=plsc.ScalarSubcoreMesh(axis_name="sequencer", num_cores=1), name=name,
              scratch_types=[pltpu.SemaphoreType.DMA((n_sems,)), pltpu.SemaphoreType.DMA((n_sems,))],
              compiler_params=pltpu.CompilerParams(collective_id=collective_id))()
    return [ref[...] for ref in land_refs]


def _broadcast_beside(arrays):
    hbm = pltpu.MemorySpace.HBM
    src_refs = [jax.new_ref(a, memory_space=hbm) for a in arrays]
    land_refs = [jax.empty_ref(jax.ShapeDtypeStruct((len(_RELATIONS),) + a.shape, a.dtype), memory_space=hbm)
                 for a in arrays]

    def launch(send_sems, recv_sems):
        me = (lax.axis_index("x"), lax.axis_index("y"), lax.axis_index("c"))
        peers = [_related(me, rel) for rel in _RELATIONS]
        barrier = pltpu.get_barrier_semaphore()
        for peer in peers:
            pl.semaphore_signal(barrier, inc=1, device_id=peer, device_id_type=MESH)
        pl.semaphore_wait(barrier, len(peers))
        copies = []
        for i, (src, land) in enumerate(zip(src_refs, land_refs)):
            for k, peer in enumerate(peers):
                copies.append(pltpu.make_async_remote_copy(
                    src_ref=src, dst_ref=land.at[k], send_sem=send_sems.at[7 * i + k],
                    recv_sem=recv_sems.at[7 * i + k], device_id=peer, device_id_type=MESH))
                copies[-1].start()
        for cp in copies:
            cp.wait()

    n_sems = 7 * len(arrays)
    pl.kernel(launch, mesh=plsc.ScalarSubcoreMesh(axis_name="sequencer", num_cores=1), name="broadcast_beside",
              scratch_types=[pltpu.SemaphoreType.DMA((n_sems,)), pltpu.SemaphoreType.DMA((n_sems,))],
              compiler_params=pltpu.CompilerParams(collective_id=3))()
    return [ref[...] for ref in land_refs]


def _exchange_grads(big, outputs, small, landed, own_bc, landed_bc):
    n_t = len(big)
    n_g = len(_SMALL_GROUPS)
    names = _SMALL_ORDER
    halves = [(b.shape[0] // N_CHIPS // 2, b.shape[1]) for b in big]
    early = sorted(landed)
    late = [i for i in range(n_t) if i not in landed]
    n_sems = 4 * n_g + 7 * len(late) + n_t
    small_sem0, block_sem0 = n_t, n_t + len(names)
    early_sem0 = block_sem0 + N_CHIPS * len(late)
    landed_sem0 = early_sem0 + 2 * len(early)
    sent = [n for n in names if n in small]

    def body(*refs):
        pos = 0

        def take(n):
            nonlocal pos
            pos += n
            return refs[pos - n:pos]

        big_refs, small_refs = take(n_t), dict(zip(sent, take(len(sent))))
        land_refs = dict(zip(early, take(len(early))))
        own_bc_ref, landed_bc_ref = take(2)
        out_refs, small_out_refs = take(len(outputs)), dict(zip(names, take(len(names))))
        per_late = lambda: dict(zip(late, take(len(late))))
        ga, gb, pme, send_b, recv_b = per_late(), per_late(), take(n_t), per_late(), per_late()
        own_e, land_e = dict(zip(early, take(len(early)))), dict(zip(early, take(len(early))))
        own_s, land_s = take(2)
        s_own, s_sib, s_chips, s_pair = take(n_g), take(n_g), take(n_g), take(n_g)
        stage = dict(zip(names, take(len(names))))
        send_sems, recv_sems, local_sems = take(3)
        x, y, c, other_chips = _mesh_place()
        me = 2 * x + y
        sibling = (x, y, 1 - c)
        sem_at = iter(range(n_sems))

        def remote(src, dst, to):
            k = next(sem_at)
            return pltpu.make_async_remote_copy(src_ref=src, dst_ref=dst, send_sem=send_sems.at[k],
                                                recv_sem=recv_sems.at[k], device_id=to, device_id_type=MESH)

        loads = [pltpu.make_async_copy(small_refs[name], stage[name], local_sems.at[small_sem0 + names.index(name)])
                 for name in sent]
        landed_loads = [pltpu.make_async_copy(own_bc_ref, own_s, local_sems.at[landed_sem0]),
                        pltpu.make_async_copy(landed_bc_ref, land_s, local_sems.at[landed_sem0 + 1])]
        for cp in loads + landed_loads:
            cp.start()
        for cp in loads:
            cp.wait()
        small_swaps = []
        for gi, (_, _, members) in enumerate(_SMALL_GROUPS):
            s_own[gi][...] = jnp.zeros_like(s_own[gi])
            for name, r0 in members:
                r, n = _small_shape(name)
                s_own[gi][r0:r0 + r, 0:n] = stage[name][...]
            small_swaps.append(remote(s_own[gi], s_sib[gi], sibling))
            small_swaps[gi].start()
        order = sorted(late, key=lambda i: halves[i][0] * halves[i][1])
        own_loads, big_swaps = {}, {}
        for i in order:
            hr = halves[i][0]
            own_loads[i], big_swaps[i] = [], []
            for j in range(N_CHIPS):
                mine = big_refs[i].at[pl.ds(j * 2 * hr + c * hr, hr), :]
                theirs = big_refs[i].at[pl.ds(j * 2 * hr + (1 - c) * hr, hr), :]
                sem = local_sems.at[block_sem0 + N_CHIPS * late.index(i) + j]
                own_loads[i].append(pltpu.make_async_copy(mine, ga[i].at[j], sem))
                own_loads[i][j].start()
                big_swaps[i].append(remote(theirs, gb[i].at[j], sibling))
                big_swaps[i][j].start()
        early_loads = {}
        for e, i in enumerate(early):
            hr = halves[i][0]
            mine = big_refs[i].at[pl.ds(me * 2 * hr + c * hr, hr), :]
            early_loads[i] = [pltpu.make_async_copy(mine, own_e[i], local_sems.at[early_sem0 + 2 * e]),
                              pltpu.make_async_copy(land_refs[i], land_e[i], local_sems.at[early_sem0 + 2 * e + 1])]
            for cp in early_loads[i]:
                cp.start()
        small_sends = []
        for gi in range(n_g):
            small_swaps[gi].wait_recv()
            s_pair[gi][...] = s_own[gi][...] + s_sib[gi][...]
            small_sends.append([remote(s_pair[gi], s_chips[gi].at[k], (*chip, c)) for k, chip in enumerate(other_chips)])
            for cp in small_sends[gi]:
                cp.start()

        def pair_sum(i, j):
            return ga[i][j] + gb[i][j]

        big_sends = {}
        for i in order:
            for j in range(N_CHIPS):
                own_loads[i][j].wait()
                big_swaps[i][j].wait_recv()
            big_sends[i] = []
            for k, chip in enumerate(other_chips):
                send_b[i][k] = pair_sum(i, 2 * chip[0] + chip[1]).astype(BF16)
                big_sends[i].append(remote(send_b[i].at[k], recv_b[i].at[k], (*chip, c)))
                big_sends[i][k].start()
        last_swaps, keeps = {}, {}
        for i in early + order:
            hr = halves[i][0]
            if i in landed:
                for cp in early_loads[i]:
                    cp.wait()
                total = own_e[i][...]
                for k in range(len(_RELATIONS)):
                    total = total + land_e[i][k].astype(F32)
                pme[i][...] = total
            else:
                for k in range(3):
                    big_sends[i][k].wait_recv()
                pme[i][...] = ((pair_sum(i, me) + recv_b[i][0].astype(F32)) + recv_b[i][1].astype(F32)) + recv_b[i][2].astype(F32)
            o, = [o for o, group in enumerate(outputs) if i in group]
            first_col = sum(halves[j][1] for j in outputs[o][:outputs[o].index(i)])
            mine = out_refs[o].at[pl.ds(c * hr, hr), pl.ds(first_col, halves[i][1])]
            keeps[i] = pltpu.make_async_copy(pme[i], mine, local_sems.at[i])
            keeps[i].start()
            last_swaps[i] = remote(pme[i], mine, sibling)
            last_swaps[i].start()

        for gi, (_, _, members) in enumerate(_SMALL_GROUPS):
            for k in range(3):
                small_sends[gi][k].wait_recv()
            total = None
            for j in range(N_CHIPS):
                rel = jnp.bitwise_xor(j, me)
                term = jnp.where(rel == 0, s_pair[gi][...], jnp.where(
                    rel == 2, s_chips[gi][0], jnp.where(rel == 1, s_chips[gi][1], s_chips[gi][2])))
                total = term if total is None else total + term
            s_sib[gi][...] = total
            for name, r0 in members:
                r, n = _small_shape(name)
                stage[name][...] = s_sib[gi][r0:r0 + r, 0:n]
        my_index = 4 * x + 2 * y + c
        for cp in landed_loads:
            cp.wait()
        total = None
        for d in range(2 * N_CHIPS):
            rel = jnp.bitwise_xor(d, my_index)
            term = own_s[...]
            for k in range(len(_RELATIONS)):
                term = jnp.where(rel == k + 1, land_s[k], term)
            total = term.astype(F32) if total is None else total + term.astype(F32)
        for a, name in enumerate(_SMALL_EARLY):
            stage[name][...] = total[:, a * SSM_STATE:(a + 1) * SSM_STATE]
        stores = [pltpu.make_async_copy(stage[name], small_out_refs[name], local_sems.at[small_sem0 + a])
                  for a, name in enumerate(names)]
        for cp in stores:
            cp.start()

        for i in range(n_t):
            last_swaps[i].wait_recv()
            keeps[i].wait()
        for cp in stores:
            cp.wait()
        groups = list(big_swaps.values()) + small_sends + list(big_sends.values())
        for cp in small_swaps + [cp for group in groups for cp in group] + list(last_swaps.values()):
            cp.wait_send()

    any_spec = pl.BlockSpec(memory_space=pl.ANY)
    small_shapes = [_sds(_small_shape(n)) for n in names]
    group_shapes = [shape for _, shape, _ in _SMALL_GROUPS]
    vmem = lambda which, dtype, lead=(): [pltpu.VMEM(lead + halves[i], dtype) for i in which]
    outs = _call(
        body, name="exchange_grads",
        in_specs=[any_spec] * (n_t + len(sent) + len(early) + 2),
        out_specs=[any_spec] * (len(outputs) + len(names)),
        out_shape=[_sds((big[group[0]].shape[0] // N_CHIPS, sum(big[i].shape[1] for i in group))) for group in outputs]
        + small_shapes,
        scratch_shapes=(vmem(late, F32, (N_CHIPS,)) + vmem(late, F32, (N_CHIPS,)) + vmem(range(n_t), F32)
                        + vmem(late, BF16, (3,)) + vmem(late, BF16, (3,))
                        + vmem(early, F32)
                        + [pltpu.VMEM((len(_RELATIONS),) + halves[i], landed[i].dtype) for i in early]
                        + [pltpu.VMEM(own_bc.shape, own_bc.dtype), pltpu.VMEM(landed_bc.shape, landed_bc.dtype)]
                        + [pltpu.VMEM(s, F32) for s in group_shapes] * 2 + [pltpu.VMEM((3,) + s, F32) for s in group_shapes]
                        + [pltpu.VMEM(s, F32) for s in group_shapes]
                        + [pltpu.VMEM(_small_shape(n), F32) for n in names]
                        + [pltpu.SemaphoreType.DMA((n_sems,)), pltpu.SemaphoreType.DMA((n_sems,)),
                           pltpu.SemaphoreType.DMA((landed_sem0 + 2,))]),
        compiler_params=_params(48),
    )(*big, *[small[n] for n in sent], *[landed[i] for i in early], own_bc, landed_bc)
    return list(outs[:len(outputs)]), dict(zip(names, outs[len(outputs):]))


def _adamw_update(w, g, m, v):
    m = ADAM_B1 * m + (1.0 - ADAM_B1) * g
    v = ADAM_B2 * v + (1.0 - ADAM_B2) * (g * g)
    m_hat = m / (1.0 - ADAM_B1 ** ADAM_STEP)
    v_hat = v / (1.0 - ADAM_B2 ** ADAM_STEP)
    return -ADAM_LR * (m_hat / (jnp.sqrt(v_hat) + ADAM_EPS) + ADAM_WD * w), m, v


def _adamw(w, g, m, v, grid, name):
    n_t = len(w)

    def body(*refs):
        ins, outs = refs[:4 * n_t], refs[4 * n_t:]
        for i in range(n_t):
            w_, g_, m_, v_ = [ins[a * n_t + i][...] for a in range(4)]
            vals = (g_,) + _adamw_update(w_, g_, m_, v_)
            for a in range(4):
                outs[a * n_t + i][...] = vals[a]

    specs = [pl.BlockSpec((a.shape[0] // grid, a.shape[1]), lambda i: (i, 0)) for a in w]
    shapes = [_sds(a.shape) for a in w]
    outs = _call(
        body, name=name, grid=(grid,), in_specs=specs * 4, out_specs=specs * 4, out_shape=shapes * 4,
        compiler_params=_params(40, ("arbitrary",)),
    )(*w, *g, *m, *v)
    return [outs[a * n_t:(a + 1) * n_t] for a in range(4)]


def kernel(x, p, pre_norm_g, w_in, ssm_lam_re, ssm_lam_im, ssm_log_step, ssm_b_re, ssm_b_im, ssm_c_re, ssm_c_im, ssm_d, ssm_w_glu, ssm_b_glu, attn_sinks, w_out, post_norm_g, pl_w_proj, pl_w_gate, pl_b_gate, loss_target, m_pre_norm_g, m_w_in, m_ssm_lam_re, m_ssm_lam_im, m_ssm_log_step, m_ssm_b_re, m_ssm_b_im, m_ssm_c_re, m_ssm_c_im, m_ssm_d, m_ssm_w_glu, m_ssm_b_glu, m_attn_sinks, m_w_out, m_post_norm_g, m_pl_w_proj, m_pl_w_gate, m_pl_b_gate, v_pre_norm_g, v_w_in, v_ssm_lam_re, v_ssm_lam_im, v_ssm_log_step, v_ssm_b_re, v_ssm_b_im, v_ssm_c_re, v_ssm_c_im, v_ssm_d, v_ssm_w_glu, v_ssm_b_glu, v_attn_sinks, v_w_out, v_post_norm_g, v_pl_w_proj, v_pl_w_gate, v_pl_b_gate):
    weights = dict(pre_norm_g=pre_norm_g, w_in=w_in, ssm_lam_re=ssm_lam_re, ssm_lam_im=ssm_lam_im,
                   ssm_log_step=ssm_log_step, ssm_b_re=ssm_b_re, ssm_b_im=ssm_b_im, ssm_c_re=ssm_c_re,
                   ssm_c_im=ssm_c_im, ssm_d=ssm_d, ssm_w_glu=ssm_w_glu, ssm_b_glu=ssm_b_glu, attn_sinks=attn_sinks,
                   w_out=w_out, post_norm_g=post_norm_g, pl_w_proj=pl_w_proj, pl_w_gate=pl_w_gate, pl_b_gate=pl_b_gate)
    m_in = dict(pre_norm_g=m_pre_norm_g, w_in=m_w_in, ssm_lam_re=m_ssm_lam_re, ssm_lam_im=m_ssm_lam_im,
                ssm_log_step=m_ssm_log_step, ssm_b_re=m_ssm_b_re, ssm_b_im=m_ssm_b_im, ssm_c_re=m_ssm_c_re,
                ssm_c_im=m_ssm_c_im, ssm_d=m_ssm_d, ssm_w_glu=m_ssm_w_glu, ssm_b_glu=m_ssm_b_glu,
                attn_sinks=m_attn_sinks, w_out=m_w_out, post_norm_g=m_post_norm_g, pl_w_proj=m_pl_w_proj,
                pl_w_gate=m_pl_w_gate, pl_b_gate=m_pl_b_gate)
    v_in = dict(pre_norm_g=v_pre_norm_g, w_in=v_w_in, ssm_lam_re=v_ssm_lam_re, ssm_lam_im=v_ssm_lam_im,
                ssm_log_step=v_ssm_log_step, ssm_b_re=v_ssm_b_re, ssm_b_im=v_ssm_b_im, ssm_c_re=v_ssm_c_re,
                ssm_c_im=v_ssm_c_im, ssm_d=v_ssm_d, ssm_w_glu=v_ssm_w_glu, ssm_b_glu=v_ssm_b_glu,
                attn_sinks=v_attn_sinks, w_out=v_w_out, post_norm_g=v_post_norm_g, pl_w_proj=v_pl_w_proj,
                pl_w_gate=v_pl_w_gate, pl_b_gate=v_pl_b_gate)

    def two_d(tree):
        return {k: _to_kernel_form(k, a) for k, a in tree.items()}

    w2, m2, v2 = two_d(weights), two_d(m_in), two_d(v_in)

    (w_in_full,) = _gather_weights_beside([w2["w_in"].astype(BF16)], "gather_w_in_beside", 4)
    s5_params = tuple(w2[n] for n in ("ssm_lam_re", "ssm_lam_im", "ssm_log_step", "ssm_b_re", "ssm_b_im", "ssm_c_re",
                                      "ssm_c_im"))
    s5_operands = _s5_params_fwd(*s5_params)
    hn = _pre_norm(x.reshape(-1, D_MODEL), w2["pre_norm_g"])
    behind = s5_operands[0][0, 0] * 0.0 + hn[0, 0].astype(F32) * 0.0
    rest = _gather_weights_beside([(w2[n] + behind).astype(BF16) for n in _BIG[1:]], "gather_weights_beside", 1)
    full = dict(zip(_BIG, [w_in_full] + rest))
    mats = ("w_in_early", "w_in_late") + _BIG[1:]
    landed, bc = {}, {}

    def send_beside(ready):
        sent_early = ("ssm_w_glu", "w_out", "pl_w_gate", "pl_w_proj")
        landed.update(zip([mats.index(n) for n in sent_early],
                          _scatter_beside([ready[n] for n in sent_early], "scatter_beside", 2)))
        after_scatter = landed[mats.index(sent_early[-1])][0, 0, 0] * 0.0
        bc["own"] = ready["ssm_bc"] + after_scatter.astype(BF16)
        bc["landed"] = _broadcast_beside([bc["own"]])[0]
        return bc["own"], bc["landed"]

    grad_x, loss, grads = _local_step(
        x, hn, p, loss_target, w2["pre_norm_g"], full["w_in"], s5_params, s5_operands, w2["ssm_d"], full["ssm_w_glu"], w2["ssm_b_glu"],
        w2["attn_sinks"], full["w_out"], w2["post_norm_g"], full["pl_w_proj"], full["pl_w_gate"], w2["pl_b_gate"], send_beside)

    landed[0] = _scatter_beside([grads["w_in_early_bf16"]], "scatter_w_in_beside", 5)[0]
    sent_here = {**{n: grads[n] for n in _SMALL if n not in _SMALL_EARLY}, "loss": loss}
    halves_of_w_in = ((0, 1),) + tuple((i,) for i in range(2, len(mats)))
    g_big, g_small = _exchange_grads([grads[n] for n in mats], halves_of_w_in, sent_here, landed, bc["own"], bc["landed"])
    g_big = dict(zip(_BIG, g_big))
    total_loss = g_small.pop("loss")

    big_out = _adamw([w2[n] for n in _BIG], [g_big[n] for n in _BIG], [m2[n] for n in _BIG], [v2[n] for n in _BIG],
                     8, "adamw_matrices")
    small_names = tuple(_SMALL)
    small_out = _adamw([w2[n] for n in small_names], [g_small[n] for n in small_names], [m2[n] for n in small_names],
                       [v2[n] for n in small_names], 1, "adamw_small")

    results = [{**dict(zip(_BIG, big_part)), **dict(zip(small_names, small_part))}
               for big_part, small_part in zip(big_out, small_out)]
    flat = [_from_kernel_form(name, r[name], weights[name].shape) for r in results for name in _WEIGHT_ORDER]
    return (total_loss.reshape(()), grad_x, *flat)
```

```python
import math

import jax
import jax.numpy as jnp
from jax import lax
from jax.experimental import pallas as pl
from jax.experimental.pallas import tpu as pltpu
from jax.experimental.pallas import tpu_sc as plsc

F32 = jnp.float32
BF16 = jnp.bfloat16

D_MODEL = 1024
D_SSM = 512
D_ATTN = 512
SSM_GROUPS = 32
SSM_GROUP_CH = 16
SSM_STATE = 64
SSM_LANES = SSM_GROUPS * SSM_STATE
HEAD_DIM = 64
N_HEADS = 8
KV_HEADS = 2
Q_PER_KV = 4
WINDOW = 128
BLOCK = 128
D_PLE = 256
D_IN = 2304
EPS = 1e-6
ATTN_SCALE = 1.0 / math.sqrt(HEAD_DIM)

ADAM_LR = 0.001
ADAM_B1 = 0.9
ADAM_B2 = 0.999
ADAM_EPS = 1e-08
ADAM_WD = 0.01
ADAM_STEP = 10

N_CHIPS = 4
LANES = 128
SCAN_CHUNKS = 8
SCAN_TILE_STEPS = 32
SCAN_LANE_CHUNK = 512
MIB = 2 ** 20
MESH = pl.DeviceIdType.MESH


def _dot(a, b):
    return jnp.dot(a, b, preferred_element_type=F32)


def _dot_nt(a, b):
    return lax.dot_general(a, b, (((1,), (1,)), ((), ())), preferred_element_type=F32)


def _dot_tn(a, b):
    return lax.dot_general(a, b, (((0,), (0,)), ((), ())), preferred_element_type=F32)


def _params(vmem_mib, semantics=None):
    kw = dict(vmem_limit_bytes=vmem_mib * MIB)
    if semantics is not None:
        kw["dimension_semantics"] = semantics
    return pltpu.CompilerParams(**kw)


def _full(shape):
    nd = len(shape)
    return pl.BlockSpec(shape, lambda *_: (0,) * nd, pipeline_mode=pl.Buffered(1))


def _rows(tm, width):
    return pl.BlockSpec((tm, width), lambda i: (i, 0))


def _sds(shape, dtype=F32):
    return pltpu.HBM(shape, dtype)


def _call(body, **kw):
    fn = pl.pallas_call(body, **kw)
    return lambda *args: fn(*[pltpu.with_memory_space_constraint(a, pltpu.HBM) for a in args])


def _silu(z):
    return z * jax.nn.sigmoid(z)


def _pre_norm(x2d, g1):
    rows = x2d.shape[0]
    tm = 512

    def body(x_ref, g_ref, hn_ref):
        x = x_ref[...]
        r = lax.rsqrt(jnp.mean(x * x, axis=-1, keepdims=True) + EPS)
        hn_ref[...] = (x * r * g_ref[...]).astype(BF16)

    return _call(
        body, name="pre_norm", grid=(rows // tm,), in_specs=[_rows(tm, D_MODEL), _full((1, D_MODEL))],
        out_specs=_rows(tm, D_MODEL), out_shape=_sds((rows, D_MODEL), BF16), compiler_params=_params(32, ("arbitrary",)),
    )(x2d, g1)


def _in_proj(hn, w_in_t, n_seq, seq):
    rows = hn.shape[0]
    tm = 1024
    slab, steps, _, _ = _scan_geometry(n_seq, seq)

    def body(hn_ref, w_ref, *out_refs):
        u_parts, (zs_ref, q_ref, k_ref, v_ref, za_ref) = out_refs[:_SCAN_PARTS], out_refs[_SCAN_PARTS:]
        whole = _dot_nt(hn_ref[...], w_ref[...])

        def proj(a, b):
            return whole[:, a:b]

        _store_chunks(u_parts, pl.program_id(0) * (tm // steps), proj(0, 512), steps, slab)
        zs_ref[...] = proj(512, 1024)
        q_ref[...] = (proj(1024, 1536) * ATTN_SCALE).astype(BF16)
        k_ref[...] = proj(1536, 1664).astype(BF16)
        v_ref[...] = proj(1664, 1792).astype(BF16)
        za_ref[...] = proj(1792, 2304)

    *u_parts, zs, q, k, v, za = _call(
        body, name="in_proj", grid=(rows // tm,),
        in_specs=[_rows(tm, D_MODEL), _full((D_IN, D_MODEL))],
        out_specs=_whole_parts(rows) + [_rows(tm, 512), _rows(tm, 512), _rows(tm, 128), _rows(tm, 128), _rows(tm, 512)],
        out_shape=_part_shapes(rows) + [_sds((rows, 512)), _sds((rows, 512), BF16), _sds((rows, 128), BF16),
                                        _sds((rows, 128), BF16), _sds((rows, 512))],
        compiler_params=_params(48, ("arbitrary",)),
    )(hn, w_in_t)
    return u_parts, zs, q, k, v, za


_EARLY_COLS = D_MODEL // 2


def _in_proj_bwd_early(hn, du_parts, dzs, dq, dk, dv, dza, runs_after, n_seq, seq):
    rows = hn.shape[0]
    tm = 512
    slab, steps, _, _ = _scan_geometry(n_seq, seq)

    def body(hn_ref, *refs):
        du_parts, (dzs_ref, dq_ref, dk_ref, dv_ref, dza_ref, _, dproj_ref, dw_ref, dwb_ref) = refs[:_SCAN_PARTS], refs[_SCAN_PARTS:]
        i = pl.program_id(0)

        @pl.when(i == 0)
        def _():
            dw_ref[...] = jnp.zeros_like(dw_ref)

        du = _load_chunks(du_parts, i * (tm // steps), tm // steps, steps, slab)
        d_proj = jnp.concatenate([du.astype(BF16), dzs_ref[...], dq_ref[...], dk_ref[...], dv_ref[...], dza_ref[...]],
                                 axis=1)
        dproj_ref[...] = d_proj
        dw_ref[...] += _dot_tn(d_proj, hn_ref[...])

        @pl.when(i == rows // tm - 1)
        def _():
            dwb_ref[...] = dw_ref[...].astype(BF16)

    return _call(
        body, name="in_proj_bwd_early", grid=(rows // tm,),
        in_specs=[_rows(tm, _EARLY_COLS)] + _whole_parts(rows)
        + [_rows(tm, 512), _rows(tm, 512), _rows(tm, 128), _rows(tm, 128), _rows(tm, 512),
           pl.BlockSpec(memory_space=pl.ANY)],
        out_specs=[_rows(tm, D_IN), _full((D_IN, _EARLY_COLS)), _full((D_IN, _EARLY_COLS))],
        out_shape=[_sds((rows, D_IN), BF16), _sds((D_IN, _EARLY_COLS)), _sds((D_IN, _EARLY_COLS), BF16)],
        compiler_params=_params(48, ("arbitrary",)),
    )(hn, *du_parts, dzs, dq, dk, dv, dza, runs_after)


def _in_proj_bwd(x2d, dh1, g1, w_in_t, d_proj, runs_after):
    rows = x2d.shape[0]
    tm = 512

    def body(x_ref, dh1_ref, g_ref, w_ref, dproj_ref, _, gx_ref, dw_ref, dg_ref):
        @pl.when(pl.program_id(0) == 0)
        def _():
            dw_ref[...] = jnp.zeros_like(dw_ref)
            dg_ref[...] = jnp.zeros_like(dg_ref)

        x = x_ref[...]
        g = g_ref[...]
        r = lax.rsqrt(jnp.mean(x * x, axis=-1, keepdims=True) + EPS)
        xr = x * r
        hn = (xr[:, _EARLY_COLS:] * g[:, _EARLY_COLS:]).astype(BF16)
        d_proj = dproj_ref[...]
        dhn = _dot(d_proj, w_ref[...])
        dw_ref[...] += _dot_tn(d_proj, hn)
        dg_ref[...] += jnp.sum(dhn * xr, axis=0, keepdims=True)
        a_ = dhn * g
        gx_ref[...] = dh1_ref[...] + r * a_ - xr * (r * jnp.mean(a_ * xr, axis=-1, keepdims=True))

    late_cols = D_MODEL - _EARLY_COLS
    return _call(
        body, name="in_proj_bwd", grid=(rows // tm,),
        in_specs=[_rows(tm, D_MODEL), _rows(tm, D_MODEL), _full((1, D_MODEL)), _full((D_IN, D_MODEL)), _rows(tm, D_IN),
                  pl.BlockSpec(memory_space=pl.ANY)],
        out_specs=[_rows(tm, D_MODEL), _full((D_IN, late_cols)), _full((1, D_MODEL))],
        out_shape=[_sds((rows, D_MODEL)), _sds((D_IN, late_cols)), _sds((1, D_MODEL))],
        compiler_params=_params(52, ("arbitrary",)),
    )(x2d, dh1, g1, w_in_t, d_proj, runs_after)


def _iota(shape, axis):
    return lax.broadcasted_iota(jnp.int32, shape, axis)


def _sum_of_thirds(f, a):
    hi = a.astype(BF16)
    rest = a - hi.astype(F32)
    mid = rest.astype(BF16)
    low = (rest - mid.astype(F32)).astype(BF16)
    return (f(hi) + f(mid)) + f(low)


@jax.custom_vjp
def _pick_rows(e, a):
    return _sum_of_thirds(lambda part: _dot(e, part), a)


def _pick_rows_fwd(e, a):
    return _pick_rows(e, a), e


def _pick_rows_bwd(e, ct):
    return jnp.zeros_like(e), _sum_of_thirds(lambda part: _dot_tn(e, part), ct)


_pick_rows.defvjp(_pick_rows_fwd, _pick_rows_bwd)


@jax.custom_vjp
def _pick_cols(a, e):
    return _sum_of_thirds(lambda part: _dot(part, e), a)


def _pick_cols_fwd(a, e):
    return _pick_cols(a, e), e


def _pick_cols_bwd(e, ct):
    return _sum_of_thirds(lambda part: _dot_nt(part, e), ct), jnp.zeros_like(e)


_pick_cols.defvjp(_pick_cols_fwd, _pick_cols_bwd)


_HALF_GROUPS = SSM_GROUPS // 2
_N_SHIFT = SSM_STATE.bit_length() - 1
_P_SHIFT = SSM_GROUP_CH.bit_length() - 1


def _s5_operands(lam_re, lam_im, log_step, b_re, b_im, c_re, c_im):
    g, n, p = SSM_GROUPS, SSM_STATE, SSM_GROUP_CH
    gn, gp, hn_, hp = g * n, g * p, _HALF_GROUPS * n, _HALF_GROUPS * p
    eye_g = _iota((g, g), 0) == _iota((g, g), 1)
    step = jnp.sum(jnp.where(eye_g, jnp.exp(log_step), 0.0), axis=1, keepdims=True)
    a_re = lam_re * step
    a_im = lam_im * step
    mag = jnp.exp(a_re)
    lbar_re = mag * jnp.cos(a_im)
    lbar_im = mag * jnp.sin(a_im)
    n_re = lbar_re - 1.0
    den = lam_re * lam_re + lam_im * lam_im
    f_re = (n_re * lam_re + lbar_im * lam_im) / den
    f_im = (lbar_im * lam_re - n_re * lam_im) / den

    spread_n = (_iota((n, gn), 0) == (_iota((n, gn), 1) & (n - 1))).astype(BF16)
    own_g = _iota((g, gn), 0) == (_iota((g, gn), 1) >> _N_SHIFT)

    def to_row(a):
        return jnp.sum(jnp.where(own_g, _pick_cols(a, spread_n), 0.0), axis=0, keepdims=True)

    per_group = ((_iota((gp, g), 0) >> _P_SHIFT) == _iota((gp, g), 1)).astype(BF16)
    fx_re, fx_im = _pick_rows(per_group, f_re), _pick_rows(per_group, f_im)
    bbar_re = fx_re * b_re - fx_im * b_im
    bbar_im = fx_re * b_im + fx_im * b_re

    tile_n = (_iota((n, hn_), 0) == (_iota((n, hn_), 1) & (n - 1))).astype(BF16)
    same_group = (_iota((hp, hn_), 0) >> _P_SHIFT) == (_iota((hp, hn_), 1) >> _N_SHIFT)

    def embed(a, hf):
        return jnp.where(same_group, _pick_cols(a[hf * hp:(hf + 1) * hp], tile_n), 0.0)

    return (to_row(lbar_re), to_row(lbar_im), embed(bbar_re, 0), embed(bbar_re, 1), embed(bbar_im, 0),
            embed(bbar_im, 1), embed(c_re, 0), embed(c_re, 1), embed(c_im, 0), embed(c_im, 1))


_S5_PARAM_SHAPES = ((SSM_GROUPS, SSM_STATE), (SSM_GROUPS, SSM_STATE), (1, SSM_GROUPS),
                    (D_SSM, SSM_STATE), (D_SSM, SSM_STATE), (D_SSM, SSM_STATE), (D_SSM, SSM_STATE))
_CM_SHAPE = (2, _HALF_GROUPS * SSM_GROUP_CH, _HALF_GROUPS * SSM_STATE)
_S5_OPERAND_SHAPES = ((1, SSM_LANES), (1, SSM_LANES), _CM_SHAPE, _CM_SHAPE, _CM_SHAPE, _CM_SHAPE)


def _s5_params_fwd(*params):
    def body(*refs):
        ins, (lre_ref, lim_ref, btre_ref, btim_ref, cmre_ref, cmim_ref) = refs[:7], refs[7:]
        vals = _s5_operands(*[r[...] for r in ins])
        lre_ref[...] = vals[0]
        lim_ref[...] = vals[1]
        for ref, pair in zip((btre_ref, btim_ref, cmre_ref, cmim_ref), (vals[2:4], vals[4:6], vals[6:8], vals[8:10])):
            ref[0] = pair[0].astype(BF16)
            ref[1] = pair[1].astype(BF16)

    dtypes = (F32, F32, BF16, BF16, BF16, BF16)
    return _call(
        body, name="s5_params_fwd",
        in_specs=[_full(s) for s in _S5_PARAM_SHAPES], out_specs=[_full(s) for s in _S5_OPERAND_SHAPES],
        out_shape=[_sds(s, d) for s, d in zip(_S5_OPERAND_SHAPES, dtypes)], compiler_params=_params(32),
    )(*params)


_BC_SIDE_BY_SIDE = (D_SSM, 4 * SSM_STATE)


def _s5_params_bwd(params, cotangents, runs_after):
    def body(*refs):
        ins, (dlre, dlim, dbtre, dbtim, dcmre, dcmim), outs = refs[:7], refs[7:13], refs[14:]
        _, vjp = jax.vjp(_s5_operands, *[r[...] for r in ins])
        cts = (dlre[...], dlim[...], dbtre[0], dbtre[1], dbtim[0], dbtim[1], dcmre[0], dcmre[1], dcmim[0], dcmim[1])
        grads = vjp(cts)
        for ref, val in zip(outs[:3], grads[:3]):
            ref[...] = val
        outs[3][...] = jnp.concatenate(grads[3:], axis=1).astype(BF16)

    out_shapes = _S5_PARAM_SHAPES[:3] + (_BC_SIDE_BY_SIDE,)
    return _call(
        body, name="s5_params_bwd",
        in_specs=[_full(s) for s in _S5_PARAM_SHAPES + _S5_OPERAND_SHAPES] + [pl.BlockSpec(memory_space=pl.ANY)],
        out_specs=[_full(s) for s in out_shapes],
        out_shape=[_sds(s, d) for s, d in zip(out_shapes, (F32, F32, F32, BF16))], compiler_params=_params(48),
    )(*params, *cotangents, runs_after)


def _scan_geometry(n_seq, seq):
    slab = n_seq * SCAN_CHUNKS
    steps = seq // SCAN_CHUNKS
    tile_rows = slab * SCAN_TILE_STEPS
    n_tiles = steps // SCAN_TILE_STEPS
    return slab, steps, tile_rows, n_tiles


_SCAN_PARTS = D_SSM // LANES


def _whole_parts(rows):
    return [_full((rows, LANES))] * _SCAN_PARTS


def _part_shapes(rows):
    return [_sds((rows, LANES))] * _SCAN_PARTS


def _load_chunks(parts, first_chunk, n_chunks, steps, slab):
    return jnp.concatenate([
        jnp.concatenate([ref[pl.ds(first_chunk + q, steps, stride=slab), :] for ref in parts], axis=1)
        for q in range(n_chunks)], axis=0)


def _store_chunks(parts, first_chunk, value, steps, slab):
    for q in range(value.shape[0] // steps):
        for j, ref in enumerate(parts):
            ref[pl.ds(first_chunk + q, steps, stride=slab), :] = value[q * steps:(q + 1) * steps,
                                                                     j * LANES:(j + 1) * LANES]


def _join_parts(parts):
    return jnp.concatenate([ref[...] for ref in parts], axis=1)


def _split_parts(parts, value):
    for j, ref in enumerate(parts):
        ref[...] = value[:, j * LANES:(j + 1) * LANES]


def _complex_power(re, im, n):
    out = None
    while n:
        if n & 1:
            out = (re, im) if out is None else (out[0] * re - out[1] * im, out[0] * im + out[1] * re)
        n >>= 1
        if n:
            re, im = re * re - im * im, 2.0 * re * im
    return out


def _chunk_carry(sum_re, sum_im, carry_re, carry_im, a_re, a_im, n_seq, reverse):
    carry_re[...] = jnp.zeros_like(carry_re)
    carry_im[...] = jnp.zeros_like(carry_im)
    for s in range(n_seq):
        order = range(SCAN_CHUNKS - 2, -1, -1) if reverse else range(1, SCAN_CHUNKS)
        for c in order:
            r = s * SCAN_CHUNKS + c
            p = r + 1 if reverse else r - 1
            p_re, p_im = carry_re[p:p + 1, :], carry_im[p:p + 1, :]
            carry_re[r:r + 1, :] = a_re * p_re - a_im * p_im + sum_re[p:p + 1, :]
            carry_im[r:r + 1, :] = a_re * p_im + a_im * p_re + sum_im[p:p + 1, :]


def _s5_scan_fwd(u_parts, bt_re, bt_im, cm_re, cm_im, lbar_re, lbar_im, d_row, n_seq, seq):
    slab, steps, tile_rows, n_tiles = _scan_geometry(n_seq, seq)
    rows = u_parts[0].shape[0]

    def body(*refs):
        u_refs, refs = refs[:_SCAN_PARTS], refs[_SCAN_PARTS:]
        (bre_ref, bim_ref, cre_ref, cim_ref, lre_ref, lim_ref, d_ref), refs = refs[:7], refs[7:]
        y_refs, (hre_ref, him_ref, st_re, st_im, h0_re, h0_im, buf_re, buf_im) = refs[:_SCAN_PARTS], refs[_SCAN_PARTS:]
        second = pl.program_id(0) == 1
        i = pl.program_id(1)

        @pl.when(jnp.logical_and(i == 0, jnp.logical_not(second)))
        def _():
            st_re[...] = jnp.zeros_like(st_re)
            st_im[...] = jnp.zeros_like(st_im)

        u = _join_parts(u_refs)
        ub = u.astype(BF16)
        for hf in range(2):
            cols = slice(hf * 1024, (hf + 1) * 1024)
            buf_re[:, cols] = _dot(ub[:, hf * 256:(hf + 1) * 256], bre_ref[hf])
            buf_im[:, cols] = _dot(ub[:, hf * 256:(hf + 1) * 256], bim_ref[hf])

        for lc in range(SSM_LANES // SCAN_LANE_CHUNK):
            cols = slice(lc * SCAN_LANE_CHUNK, (lc + 1) * SCAN_LANE_CHUNK)
            l_re = jnp.broadcast_to(lre_ref[:, cols], (slab, SCAN_LANE_CHUNK))
            l_im = jnp.broadcast_to(lim_ref[:, cols], (slab, SCAN_LANE_CHUNK))

            def scan_tile(keep_states):
                def step(t, carry):
                    s_re, s_im = carry
                    r0 = pl.multiple_of(t * slab, slab)
                    n_re = l_re * s_re - l_im * s_im + buf_re[pl.ds(r0, slab), cols]
                    n_im = l_re * s_im + l_im * s_re + buf_im[pl.ds(r0, slab), cols]
                    if keep_states:
                        buf_re[pl.ds(r0, slab), cols] = n_re
                        buf_im[pl.ds(r0, slab), cols] = n_im
                    return n_re, n_im

                s_re, s_im = lax.fori_loop(0, SCAN_TILE_STEPS, step, (st_re[:, cols], st_im[:, cols]), unroll=True)
                st_re[:, cols] = s_re
                st_im[:, cols] = s_im

            pl.when(jnp.logical_not(second))(lambda: scan_tile(False))
            pl.when(second)(lambda: scan_tile(True))

        @pl.when(jnp.logical_and(i == n_tiles - 1, jnp.logical_not(second)))
        def _():
            a_re, a_im = _complex_power(lre_ref[...], lim_ref[...], steps)
            _chunk_carry(st_re, st_im, h0_re, h0_im, a_re, a_im, n_seq, reverse=False)
            st_re[...] = h0_re[...]
            st_im[...] = h0_im[...]

        @pl.when(second)
        def _():
            h_re = buf_re[...].astype(BF16)
            h_im = buf_im[...].astype(BF16)
            hre_ref[...] = h_re
            him_ref[...] = h_im
            for hf in range(2):
                cols = slice(hf * 1024, (hf + 1) * 1024)
                ycols = slice(hf * 256, (hf + 1) * 256)
                y_half = (_dot_nt(h_re[:, cols], cre_ref[hf]) - _dot_nt(h_im[:, cols], cim_ref[hf])
                          + d_ref[:, ycols] * u[:, ycols])
                _split_parts(y_refs[2 * hf:2 * hf + 2], y_half)

    tile = lambda w: pl.BlockSpec((tile_rows, w), lambda p, i: (i, 0))
    out_tile = lambda w: pl.BlockSpec((tile_rows, w), lambda p, i: (i * p, 0))
    cm = _full(_CM_SHAPE)
    outs = _call(
        body, name="s5_scan_fwd", grid=(2, n_tiles),
        in_specs=[tile(LANES)] * _SCAN_PARTS + [cm, cm, cm, cm, _full((1, SSM_LANES)), _full((1, SSM_LANES)),
                                                _full((1, 512))],
        out_specs=[out_tile(LANES)] * _SCAN_PARTS + [out_tile(SSM_LANES), out_tile(SSM_LANES)],
        out_shape=_part_shapes(rows) + [_sds((rows, SSM_LANES), BF16), _sds((rows, SSM_LANES), BF16)],
        scratch_shapes=[pltpu.VMEM((slab, SSM_LANES), F32)] * 4 + [pltpu.VMEM((tile_rows, SSM_LANES), F32)] * 2,
        compiler_params=_params(40, ("arbitrary", "arbitrary")),
    )(*u_parts, bt_re, bt_im, cm_re, cm_im, lbar_re, lbar_im, d_row)
    return outs[:_SCAN_PARTS], outs[_SCAN_PARTS], outs[_SCAN_PARTS + 1]


def _s5_scan_bwd(dy_parts, u_parts, h_re, h_im, bt_re, bt_im, cm_re, cm_im, lbar_re, lbar_im, d_row, n_seq, seq):
    slab, steps, tile_rows, n_tiles = _scan_geometry(n_seq, seq)
    rows = u_parts[0].shape[0]

    def body(*refs):
        dy_refs, u_refs, refs = refs[:_SCAN_PARTS], refs[_SCAN_PARTS:2 * _SCAN_PARTS], refs[2 * _SCAN_PARTS:]
        (hre_ref, him_ref, bre_ref, bim_ref, cre_ref, cim_ref, lre_ref, lim_ref, d_ref), refs = refs[:9], refs[9:]
        du_refs, refs = refs[:_SCAN_PARTS], refs[_SCAN_PARTS:]
        (dbre_ref, dbim_ref, dcre_ref, dcim_ref, dlre_ref, dlim_ref, dd_ref,
         st_re, st_im, g0_re, g0_im, acc_re, acc_im, buf_re, buf_im) = refs
        second = pl.program_id(0) == 1
        i = pl.program_id(1)

        @pl.when(jnp.logical_and(i == 0, jnp.logical_not(second)))
        def _():
            st_re[...] = jnp.zeros_like(st_re)
            st_im[...] = jnp.zeros_like(st_im)
            acc_re[...] = jnp.zeros_like(acc_re)
            acc_im[...] = jnp.zeros_like(acc_im)
            for ref in (dbre_ref, dbim_ref, dcre_ref, dcim_ref, dd_ref):
                ref[...] = jnp.zeros_like(ref)

        dy = _join_parts(dy_refs)
        dyb = dy.astype(BF16)
        for hf in range(2):
            cols = slice(hf * 1024, (hf + 1) * 1024)
            buf_re[:, cols] = _dot(dyb[:, hf * 256:(hf + 1) * 256], cre_ref[hf])
            buf_im[:, cols] = -_dot(dyb[:, hf * 256:(hf + 1) * 256], cim_ref[hf])

        for lc in range(SSM_LANES // SCAN_LANE_CHUNK):
            cols = slice(lc * SCAN_LANE_CHUNK, (lc + 1) * SCAN_LANE_CHUNK)
            l_re = jnp.broadcast_to(lre_ref[:, cols], (slab, SCAN_LANE_CHUNK))
            l_im = jnp.broadcast_to(lim_ref[:, cols], (slab, SCAN_LANE_CHUNK))

            def advance(r0, s_re, s_im):
                n_re = l_re * s_re + l_im * s_im + buf_re[pl.ds(r0, slab), cols]
                n_im = l_re * s_im - l_im * s_re + buf_im[pl.ds(r0, slab), cols]
                buf_re[pl.ds(r0, slab), cols] = n_re
                buf_im[pl.ds(r0, slab), cols] = n_im
                return n_re, n_im

            def row0(k):
                return pl.multiple_of((SCAN_TILE_STEPS - 1 - k) * slab, slab)

            @pl.when(jnp.logical_not(second))
            def _():
                s_re, s_im = lax.fori_loop(0, SCAN_TILE_STEPS, lambda k, s: advance(row0(k), *s),
                                           (st_re[:, cols], st_im[:, cols]), unroll=True)
                st_re[:, cols] = s_re
                st_im[:, cols] = s_im

            @pl.when(second)
            def _():
                def step(k, carry):
                    s_re, s_im, a_re, a_im = carry
                    r0 = row0(k)
                    hr = hre_ref[pl.ds(r0, slab), cols].astype(F32)
                    hi = him_ref[pl.ds(r0, slab), cols].astype(F32)
                    a_re = a_re + s_re * hr + s_im * hi
                    a_im = a_im + s_im * hr - s_re * hi
                    return advance(r0, s_re, s_im) + (a_re, a_im)

                zero = jnp.zeros((slab, SCAN_LANE_CHUNK), F32)
                s_re, s_im, a_re, a_im = lax.fori_loop(
                    0, SCAN_TILE_STEPS, step, (st_re[:, cols], st_im[:, cols], zero, zero), unroll=True)
                st_re[:, cols] = s_re
                st_im[:, cols] = s_im
                acc_re[:, cols] += a_re
                acc_im[:, cols] += a_im

        @pl.when(jnp.logical_and(i == n_tiles - 1, jnp.logical_not(second)))
        def _():
            p_re, p_im = _complex_power(lre_ref[...], lim_ref[...], steps)
            _chunk_carry(st_re, st_im, g0_re, g0_im, p_re, -p_im, n_seq, reverse=True)
            st_re[...] = g0_re[...]
            st_im[...] = g0_im[...]

        @pl.when(second)
        def _():
            u = _join_parts(u_refs)
            ub = u.astype(BF16)
            g_re = buf_re[...].astype(BF16)
            g_im = buf_im[...].astype(BF16)
            dd_ref[...] += jnp.sum(dy * u, axis=0, keepdims=True)
            for hf in range(2):
                cols = slice(hf * 1024, (hf + 1) * 1024)
                ycols = slice(hf * 256, (hf + 1) * 256)
                du_half = (_dot_nt(g_re[:, cols], bre_ref[hf]) + _dot_nt(g_im[:, cols], bim_ref[hf])
                           + d_ref[:, ycols] * dy[:, ycols])
                _split_parts(du_refs[2 * hf:2 * hf + 2], du_half)
                for q4 in range(_HALF_GROUPS // 4):
                    ch = slice(hf * 256 + q4 * 64, hf * 256 + (q4 + 1) * 64)
                    st = slice(hf * 1024 + q4 * 256, hf * 1024 + (q4 + 1) * 256)
                    blk = (hf, slice(q4 * 64, (q4 + 1) * 64), slice(q4 * 256, (q4 + 1) * 256))
                    dbre_ref[blk] += _dot_tn(ub[:, ch], g_re[:, st])
                    dbim_ref[blk] += _dot_tn(ub[:, ch], g_im[:, st])
                    dcre_ref[blk] += _dot_tn(dyb[:, ch], hre_ref[:, st])
                    dcim_ref[blk] -= _dot_tn(dyb[:, ch], him_ref[:, st])

        @pl.when(jnp.logical_and(i == n_tiles - 1, second))
        def _():
            dlre_ref[...] = jnp.sum(acc_re[...], axis=0, keepdims=True)
            dlim_ref[...] = jnp.sum(acc_im[...], axis=0, keepdims=True)

    tile = lambda w: pl.BlockSpec((tile_rows, w), lambda p, i: (n_tiles - 1 - i, 0))
    second_tile = lambda w: pl.BlockSpec((tile_rows, w), lambda p, i: (n_tiles - 1 - i * p, 0))
    cm = _full(_CM_SHAPE)
    row = _full((1, SSM_LANES))
    outs = _call(
        body, name="s5_scan_bwd", grid=(2, n_tiles),
        in_specs=[tile(LANES)] * _SCAN_PARTS + [second_tile(LANES)] * _SCAN_PARTS
        + [second_tile(SSM_LANES), second_tile(SSM_LANES), cm, cm, cm, cm, row, row, _full((1, 512))],
        out_specs=[second_tile(LANES)] * _SCAN_PARTS + [cm, cm, cm, cm, row, row, _full((1, 512))],
        out_shape=(_part_shapes(rows) + [_sds(_CM_SHAPE)] * 4 + [_sds((1, SSM_LANES))] * 2 + [_sds((1, 512))]),
        scratch_shapes=[pltpu.VMEM((slab, SSM_LANES), F32)] * 6 + [pltpu.VMEM((tile_rows, SSM_LANES), F32)] * 2,
        compiler_params=_params(48, ("arbitrary", "arbitrary")),
    )(*dy_parts, *u_parts, h_re, h_im, bt_re, bt_im, cm_re, cm_im, lbar_re, lbar_im, d_row)
    return (outs[:_SCAN_PARTS],) + tuple(outs[_SCAN_PARTS:])


def _glu_gate(gl, a, zs):
    return gl * jax.nn.sigmoid(a) * _silu(zs)


def _glu_fwd(y_parts, zs, w_glu, b_glu, n_seq, seq):
    rows = zs.shape[0]
    tm = 512
    slab, steps, _, _ = _scan_geometry(n_seq, seq)

    def body(*refs):
        y_refs, (zs_ref, w_ref, b_ref, o_ref) = refs[:_SCAN_PARTS], refs[_SCAN_PARTS:]
        y = _load_chunks(y_refs, pl.program_id(0) * (tm // steps), tm // steps, steps, slab)
        gl = jax.nn.gelu(y)
        a = _dot(gl.astype(BF16), w_ref[...]) + b_ref[...]
        o_ref[...] = _glu_gate(gl, a, zs_ref[...]).astype(BF16)

    return _call(
        body, name="glu_fwd", grid=(rows // tm,),
        in_specs=_whole_parts(rows) + [_rows(tm, 512), _full((512, 512)), _full((1, 512))],
        out_specs=_rows(tm, 512), out_shape=_sds((rows, 512), BF16),
        compiler_params=_params(32, ("arbitrary",)),
    )(*y_parts, zs, w_glu, b_glu)


def _glu_bwd(y_parts, zs, d_out, w_glu, b_glu, n_seq, seq):
    rows = zs.shape[0]
    tm = 512
    slab, steps, _, _ = _scan_geometry(n_seq, seq)

    def body(*refs):
        y_refs, (zs_ref, d_ref, w_ref, b_ref), refs = refs[:_SCAN_PARTS], refs[_SCAN_PARTS:_SCAN_PARTS + 4], refs[_SCAN_PARTS + 4:]
        dy_refs, (dzs_ref, dw_ref, db_ref) = refs[:_SCAN_PARTS], refs[_SCAN_PARTS:]
        first_chunk = pl.program_id(0) * (tm // steps)

        @pl.when(pl.program_id(0) == 0)
        def _():
            dw_ref[...] = jnp.zeros_like(dw_ref)
            db_ref[...] = jnp.zeros_like(db_ref)

        gl, gelu_vjp = jax.vjp(jax.nn.gelu, _load_chunks(y_refs, first_chunk, tm // steps, steps, slab))
        glb = gl.astype(BF16)
        a = _dot(glb, w_ref[...]) + b_ref[...]
        _, gate_vjp = jax.vjp(_glu_gate, gl, a, zs_ref[...])
        d_gl, d_a, d_zs = gate_vjp(d_ref[...])
        dab = d_a.astype(BF16)
        d_gl = d_gl + _dot_nt(dab, w_ref[...])
        _store_chunks(dy_refs, first_chunk, gelu_vjp(d_gl)[0], steps, slab)
        dzs_ref[...] = d_zs.astype(BF16)
        dw_ref[...] += _dot_tn(glb, dab)
        db_ref[...] += jnp.sum(d_a, axis=0, keepdims=True)

    *dy_parts, dzs, dw, db = _call(
        body, name="glu_bwd", grid=(rows // tm,),
        in_specs=_whole_parts(rows) + [_rows(tm, 512), _rows(tm, 512), _full((512, 512)), _full((1, 512))],
        out_specs=_whole_parts(rows) + [_rows(tm, 512), _full((512, 512)), _full((1, 512))],
        out_shape=_part_shapes(rows) + [_sds((rows, 512), BF16), _sds((512, 512)), _sds((1, 512))],
        compiler_params=_params(40, ("arbitrary",)),
    )(*y_parts, zs, d_out, w_glu, b_glu)
    return dy_parts, dzs, dw, db


_GROUP_ROWS = Q_PER_KV * BLOCK
_BLOCK_SHIFT = BLOCK.bit_length() - 1


def _attn_bias(j):
    query = _iota((BLOCK, _GROUP_ROWS), 1)
    dist_cur = (query & (BLOCK - 1)) - _iota((BLOCK, _GROUP_ROWS), 0)
    dist_prev = dist_cur + BLOCK
    head = query >> _BLOCK_SHIFT
    slope = jnp.zeros((BLOCK, _GROUP_ROWS), F32)
    for g in range(Q_PER_KV):
        slope = jnp.where(head == g, 2.0 ** (-(j * Q_PER_KV + g + 1)), slope)
    bias_cur = jnp.where(dist_cur >= 0, -slope * dist_cur.astype(F32), -jnp.inf)
    bias_prev = jnp.where(dist_prev < WINDOW, -slope * dist_prev.astype(F32), -jnp.inf)
    return bias_cur, bias_prev


_ATTN_BIAS_SCRATCH = pltpu.VMEM((KV_HEADS, 2, BLOCK, _GROUP_ROWS), F32)


def _fill_attn_bias(bias_ref):
    @pl.when(jnp.logical_and(pl.program_id(0) == 0, pl.program_id(1) == 0))
    def _():
        for j in range(KV_HEADS):
            bias_ref[j, 0], bias_ref[j, 1] = _attn_bias(j)


def _stack_heads(x, j):
    heads = range(j * Q_PER_KV, (j + 1) * Q_PER_KV)
    return jnp.concatenate([x[:, h * HEAD_DIM:(h + 1) * HEAD_DIM] for h in heads], axis=0)


def _head_rows(x, j):
    heads = range(j * Q_PER_KV, (j + 1) * Q_PER_KV)
    return jnp.concatenate([x[h:h + 1, :] for h in heads], axis=1)


def _sink_row(sk_ref, j):
    heads = range(j * Q_PER_KV, (j + 1) * Q_PER_KV)
    return jnp.concatenate([jnp.broadcast_to(sk_ref[0:1, h:h + 1], (1, BLOCK)) for h in heads], axis=1)


def _attn_fwd(q, k, v, za, sinks, n_seq, seq):
    nb = seq // BLOCK
    rows = q.shape[0]

    def body(q_ref, kc_ref, kp_ref, vc_ref, vp_ref, za_ref, sk_ref, o_ref, ao_ref, lse_ref, bias_ref):
        _fill_attn_bias(bias_ref)
        has_prev = pl.program_id(1) > 0
        q_all = q_ref[...]
        for j in range(KV_HEADS):
            js = slice(j * HEAD_DIM, (j + 1) * HEAD_DIM)
            bias_c, bias_p = bias_ref[j, 0], bias_ref[j, 1]
            q4 = _stack_heads(q_all, j)
            sc = _dot_nt(kc_ref[:, js], q4) + bias_c
            sp = _dot_nt(kp_ref[:, js], q4) + jnp.where(has_prev, bias_p, -jnp.inf)
            sink = _sink_row(sk_ref, j)
            m = jnp.maximum(jnp.max(jnp.maximum(sc, sp), axis=0, keepdims=True), sink)
            ec = jnp.exp(sc - m)
            ep = jnp.exp(sp - m)
            den = jnp.sum(ec + ep, axis=0, keepdims=True) + jnp.exp(sink - m)
            inv = 1.0 / den
            o4 = _dot_tn((ec * inv).astype(BF16), vc_ref[:, js]) + _dot_tn((ep * inv).astype(BF16), vp_ref[:, js])
            lse4 = m + jnp.log(den)
            for g in range(Q_PER_KV):
                h = j * Q_PER_KV + g
                o_ref[:, h * HEAD_DIM:(h + 1) * HEAD_DIM] = o4[g * BLOCK:(g + 1) * BLOCK]
                lse_ref[h:h + 1, :] = lse4[:, g * BLOCK:(g + 1) * BLOCK]
        ao_ref[...] = (o_ref[...] * _silu(za_ref[...])).astype(BF16)

    cur = lambda w: pl.BlockSpec((BLOCK, w), lambda b, n: (b * nb + n, 0))
    prev = lambda w: pl.BlockSpec((BLOCK, w), lambda b, n: (b * nb + jnp.maximum(n - 1, 0), 0))
    lse_rows = rows // BLOCK * N_HEADS
    return _call(
        body, name="attn_fwd", grid=(n_seq, nb),
        in_specs=[cur(512), cur(128), prev(128), cur(128), prev(128), cur(512), _full((1, N_HEADS))],
        out_specs=[cur(512), cur(512), pl.BlockSpec((N_HEADS, BLOCK), lambda b, n: (b * nb + n, 0))],
        out_shape=[_sds((rows, 512)), _sds((rows, 512), BF16), _sds((lse_rows, BLOCK))],
        scratch_shapes=[_ATTN_BIAS_SCRATCH], compiler_params=_params(32, ("arbitrary", "arbitrary")),
    )(q, k, k, v, v, za, sinks)


def _attn_bwd(q, k, v, za, o, lse, d_ao, sinks, n_seq, seq):
    nb = seq // BLOCK
    rows = q.shape[0]

    def body(q_ref, kc_ref, kp_ref, vc_ref, vp_ref, za_ref, o_ref, lse_ref, d_ref, sk_ref,
             dq_ref, dk_ref, dv_ref, dza_ref, dsk_ref, bias_ref, dk_carry, dv_carry):
        n = nb - 1 - pl.program_id(1)
        _fill_attn_bias(bias_ref)

        @pl.when(jnp.logical_and(pl.program_id(0) == 0, pl.program_id(1) == 0))
        def _():
            dsk_ref[...] = jnp.zeros_like(dsk_ref)
            dk_carry[...] = jnp.zeros_like(dk_carry)
            dv_carry[...] = jnp.zeros_like(dv_carry)

        has_prev = n > 0
        has_next = n + 1 < nb

        _, gate_vjp = jax.vjp(lambda o_, z_: o_ * _silu(z_), o_ref[...], za_ref[...])
        d_o, d_za = gate_vjp(d_ref[...])
        dza_ref[...] = d_za.astype(BF16)
        q_all = q_ref[...]
        lse_all = lse_ref[...]

        for j in range(KV_HEADS):
            js = slice(j * HEAD_DIM, (j + 1) * HEAD_DIM)
            kc, kp, vc, vp = kc_ref[:, js], kp_ref[:, js], vc_ref[:, js], vp_ref[:, js]
            bias_c, bias_p = bias_ref[j, 0], bias_ref[j, 1]
            q4 = _stack_heads(q_all, j)
            do4b = _stack_heads(d_o, j).astype(BF16)
            lse4 = _head_rows(lse_all, j)
            pc = jnp.exp(_dot_nt(kc, q4) + bias_c - lse4)
            pp = jnp.exp(_dot_nt(kp, q4) + jnp.where(has_prev, bias_p, -jnp.inf) - lse4)
            dpc = _dot_nt(vc, do4b)
            dpp = _dot_nt(vp, do4b)
            delta = jnp.sum(pc * dpc + pp * dpp, axis=0, keepdims=True)
            dsc = (pc * (dpc - delta)).astype(BF16)
            dsp = (pp * (dpp - delta)).astype(BF16)
            dq4 = ((_dot_tn(dsc, kc) + _dot_tn(dsp, kp)) * ATTN_SCALE).astype(BF16)
            sink_loss = jnp.exp(_sink_row(sk_ref, j) - lse4) * delta
            for g in range(Q_PER_KV):
                h = j * Q_PER_KV + g
                dq_ref[:, h * HEAD_DIM:(h + 1) * HEAD_DIM] = dq4[g * BLOCK:(g + 1) * BLOCK]
                dsk_ref[0:1, h:h + 1] -= jnp.sum(sink_loss[:, g * BLOCK:(g + 1) * BLOCK], axis=1, keepdims=True)
            dk = _dot(dsc, q4) + jnp.where(has_next, dk_carry[j], 0.0)
            dv = _dot(pc.astype(BF16), do4b) + jnp.where(has_next, dv_carry[j], 0.0)
            dk_carry[j] = _dot(dsp, q4)
            dv_carry[j] = _dot(pp.astype(BF16), do4b)
            dk_ref[:, js] = dk.astype(BF16)
            dv_ref[:, js] = dv.astype(BF16)

    cur = lambda w: pl.BlockSpec((BLOCK, w), lambda b, s: (b * nb + nb - 1 - s, 0))
    prev = lambda w: pl.BlockSpec((BLOCK, w), lambda b, s: (b * nb + jnp.maximum(nb - 2 - s, 0), 0))
    return _call(
        body, name="attn_bwd", grid=(n_seq, nb),
        in_specs=[cur(512), cur(128), prev(128), cur(128), prev(128), cur(512), cur(512),
                  pl.BlockSpec((N_HEADS, BLOCK), lambda b, s: (b * nb + nb - 1 - s, 0)), cur(512), _full((1, N_HEADS))],
        out_specs=[cur(512), cur(128), cur(128), cur(512), _full((1, N_HEADS))],
        out_shape=[_sds((rows, 512), BF16), _sds((rows, 128), BF16), _sds((rows, 128), BF16),
                   _sds((rows, 512), BF16), _sds((1, N_HEADS))],
        scratch_shapes=[_ATTN_BIAS_SCRATCH, pltpu.VMEM((KV_HEADS, BLOCK, HEAD_DIM), F32),
                        pltpu.VMEM((KV_HEADS, BLOCK, HEAD_DIM), F32)],
        compiler_params=_params(32, ("arbitrary", "arbitrary")),
    )(q, k, k, v, v, za, o, lse, d_ao, sinks)


def _tail(ssm_out, attn_out, x2d, p2d, target, w_out, g2, w_gate, b_gate, w_proj):
    rows = x2d.shape[0]
    tm = 512

    def body(so_ref, ao_ref, x_ref, p_ref, t_ref, wo_ref, g2_ref, wg_ref, bg_ref, wp_ref,
             dh1_ref, dso_ref, dao_ref, dwo_ref, dwg_ref, dwp_ref, dbg_ref, dg2_ref, loss_ref):
        @pl.when(pl.program_id(0) == 0)
        def _():
            for ref in (dwo_ref, dwg_ref, dwp_ref, dbg_ref, dg2_ref, loss_ref):
                ref[...] = jnp.zeros_like(ref)

        cat = jnp.concatenate([so_ref[...], ao_ref[...]], axis=1)
        g2 = g2_ref[...]
        mixed = _dot(cat, wo_ref[...])
        r = lax.rsqrt(jnp.mean(mixed * mixed, axis=-1, keepdims=True) + EPS)
        mr = mixed * r
        h1 = x_ref[...] + mr * g2
        h1b = h1.astype(BF16)
        gate = jax.nn.sigmoid(_dot(h1b, wg_ref[...]) + bg_ref[...])
        pb = p_ref[...].astype(BF16)
        wp_blocks = [slice(j * D_PLE, (j + 1) * D_PLE) for j in range(N_CHIPS)]
        pp = jnp.concatenate([_dot(pb, wp_ref[blk, :]) for blk in wp_blocks], axis=1)
        err = h1 + gate * pp - t_ref[...]
        loss_ref[...] += 0.5 * jnp.sum(jnp.mean(err * err, axis=-1, keepdims=True), axis=0, keepdims=True)

        dh2 = err * (1.0 / D_MODEL)
        d_glin = dh2 * pp * gate * (1.0 - gate)
        d_glin_b = d_glin.astype(BF16)
        dwg_ref[...] += _dot_tn(h1b, d_glin_b)
        dbg_ref[...] += jnp.sum(d_glin, axis=0, keepdims=True)
        d_pp = (dh2 * gate).astype(BF16)
        for blk in wp_blocks:
            dwp_ref[blk, :] += _dot_tn(pb, d_pp[:, blk])
        dh1 = dh2 + _dot_nt(d_glin_b, wg_ref[...])
        dh1_ref[...] = dh1
        dg2_ref[...] += jnp.sum(dh1 * mr, axis=0, keepdims=True)
        a_ = dh1 * g2
        d_mixed = (r * a_ - mr * (r * jnp.mean(a_ * mr, axis=-1, keepdims=True))).astype(BF16)
        dwo_ref[...] += _dot_tn(cat, d_mixed)
        d_cat = _dot_nt(d_mixed, wo_ref[...])
        dso_ref[...] = d_cat[:, 0:512]
        dao_ref[...] = d_cat[:, 512:1024]

    return _call(
        body, name="tail_fwd_bwd", grid=(rows // tm,),
        in_specs=[_rows(tm, 512), _rows(tm, 512), _rows(tm, D_MODEL), _rows(tm, D_PLE), _rows(tm, D_MODEL),
                  _full((D_MODEL, D_MODEL)), _full((1, D_MODEL)), _full((D_MODEL, D_MODEL)), _full((1, D_MODEL)),
                  _full((N_CHIPS * D_PLE, D_PLE))],
        out_specs=[_rows(tm, D_MODEL), _rows(tm, 512), _rows(tm, 512), _full((D_MODEL, D_MODEL)),
                   _full((D_MODEL, D_MODEL)), _full((N_CHIPS * D_PLE, D_PLE)), _full((1, D_MODEL)), _full((1, D_MODEL)),
                   _full((1, 1))],
        out_shape=[_sds((rows, D_MODEL)), _sds((rows, 512)), _sds((rows, 512)), _sds((D_MODEL, D_MODEL)),
                   _sds((D_MODEL, D_MODEL)), _sds((N_CHIPS * D_PLE, D_PLE)), _sds((1, D_MODEL)), _sds((1, D_MODEL)),
                   _sds((1, 1))],
        compiler_params=_params(52, ("arbitrary",)),
    )(ssm_out, attn_out, x2d, p2d, target, w_out, g2, w_gate, b_gate, w_proj)


def _local_step(x, hn, p, target, pre_norm_g, w_in_t, s5_params, s5_operands, ssm_d, w_glu, b_glu, sinks, w_out,
                post_norm_g, w_proj, w_gate, b_gate, send_tail_grads=lambda ready: ready["w_out"],
                send_bc=lambda d_bc: (d_bc, d_bc)):
    n_seq, seq, _ = x.shape
    rows = n_seq * seq
    x2d = x.reshape(rows, D_MODEL)
    p2d = p.reshape(rows, D_PLE)
    t2d = target.reshape(rows, D_MODEL)

    l_re, l_im, bt_re, bt_im, cm_re, cm_im = s5_operands

    u_scan, zs, q, k, v, za = _in_proj(hn, w_in_t, n_seq, seq)
    y_scan, h_re, h_im = _s5_scan_fwd(u_scan, bt_re, bt_im, cm_re, cm_im, l_re, l_im, ssm_d, n_seq, seq)
    ssm_out = _glu_fwd(y_scan, zs, w_glu, b_glu, n_seq, seq)
    o, attn_out, lse = _attn_fwd(q, k, v, za, sinks, n_seq, seq)

    dh1, d_so, d_ao, d_w_out, d_w_gate, d_w_proj, d_b_gate, d_g2, loss = _tail(
        ssm_out, attn_out, x2d, p2d, t2d, w_out, post_norm_g, w_gate, b_gate, w_proj)

    dq, dk, dv, dza, d_sinks = _attn_bwd(q, k, v, za, o, lse, d_ao, sinks, n_seq, seq)
    dy_scan, dzs, d_w_glu, d_b_glu = _glu_bwd(y_scan, zs, d_so, w_glu, b_glu, n_seq, seq)
    du_scan, d_bt_re, d_bt_im, d_cm_re, d_cm_im, d_l_re, d_l_im, d_d = _s5_scan_bwd(
        dy_scan, u_scan, h_re, h_im, bt_re, bt_im, cm_re, cm_im, l_re, l_im, ssm_d, n_seq, seq)
    tail_grads_arrived = send_tail_grads(dict(w_out=d_w_out, pl_w_gate=d_w_gate, pl_w_proj=d_w_proj))
    d_lam_re, d_lam_im, d_log_step, d_bc = _s5_params_bwd(
        s5_params, (d_l_re, d_l_im, d_bt_re, d_bt_im, d_cm_re, d_cm_im), tail_grads_arrived)

    sent, arrived = send_bc(d_bc)
    d_proj, d_w_in_early, d_w_in_early_b = _in_proj_bwd_early(hn, du_scan, dzs, dq, dk, dv, dza, sent, n_seq, seq)
    grad_x, d_w_in_late, d_g1 = _in_proj_bwd(x2d, dh1, pre_norm_g, w_in_t, d_proj, arrived)
    grads = dict(
        pre_norm_g=d_g1, w_in_early=d_w_in_early, w_in_early_bf16=d_w_in_early_b, w_in_late=d_w_in_late,
        ssm_lam_re=d_lam_re, ssm_lam_im=d_lam_im, ssm_log_step=d_log_step, ssm_bc=d_bc, ssm_d=d_d, ssm_w_glu=d_w_glu,
        ssm_b_glu=d_b_glu, attn_sinks=d_sinks, w_out=d_w_out, post_norm_g=d_g2, pl_w_proj=d_w_proj,
        pl_w_gate=d_w_gate, pl_b_gate=d_b_gate)
    return grad_x.reshape(x.shape), loss, grads


_BIG = ("w_in", "ssm_w_glu", "w_out", "pl_w_proj", "pl_w_gate")
_BIG_SHARD = {"w_in": (D_IN // N_CHIPS, D_MODEL), "ssm_w_glu": (D_SSM // N_CHIPS, D_SSM),
              "w_out": (D_MODEL // N_CHIPS, D_MODEL), "pl_w_proj": (D_PLE, D_MODEL // N_CHIPS),
              "pl_w_gate": (D_MODEL // N_CHIPS, D_MODEL)}
_SMALL = {"pre_norm_g": (1, D_MODEL), "ssm_lam_re": (SSM_GROUPS, SSM_STATE), "ssm_lam_im": (SSM_GROUPS, SSM_STATE),
          "ssm_log_step": (1, SSM_GROUPS), "ssm_b_re": (D_SSM, SSM_STATE), "ssm_b_im": (D_SSM, SSM_STATE),
          "ssm_c_re": (D_SSM, SSM_STATE), "ssm_c_im": (D_SSM, SSM_STATE), "ssm_d": (1, D_SSM), "ssm_b_glu": (1, D_SSM),
          "attn_sinks": (1, N_HEADS), "post_norm_g": (1, D_MODEL), "pl_b_gate": (1, D_MODEL)}
_VEC_ROWS = ("pre_norm_g", "post_norm_g", "pl_b_gate", "ssm_d", "ssm_b_glu", "attn_sinks", "ssm_log_step", "loss")
_SMALL_GROUPS = (
    ("vec", (8, D_MODEL), tuple((name, r) for r, name in enumerate(_VEC_ROWS))),
    ("lam", (2 * SSM_GROUPS, SSM_STATE), (("ssm_lam_re", 0), ("ssm_lam_im", SSM_GROUPS))),
)
_SMALL_EARLY = ("ssm_b_re", "ssm_b_im", "ssm_c_re", "ssm_c_im")
_SMALL_ORDER = tuple(name for _, _, members in _SMALL_GROUPS for name, _ in members) + _SMALL_EARLY
_WEIGHT_ORDER = ("pre_norm_g", "w_in", "ssm_lam_re", "ssm_lam_im", "ssm_log_step", "ssm_b_re", "ssm_b_im", "ssm_c_re",
                 "ssm_c_im", "ssm_d", "ssm_w_glu", "ssm_b_glu", "attn_sinks", "w_out", "post_norm_g", "pl_w_proj",
                 "pl_w_gate", "pl_b_gate")


def _small_shape(name):
    return (1, 1) if name == "loss" else _SMALL[name]


def _to_kernel_form(name, a):
    a = a[0]
    if name == "w_in":
        return a.T
    if name in ("ssm_b_re", "ssm_b_im"):
        a = a.transpose(0, 2, 1)
    return a.reshape(_SMALL[name]) if name in _SMALL else a


def _from_kernel_form(name, a, shape):
    if name == "w_in":
        a = a.T
    if name in ("ssm_b_re", "ssm_b_im"):
        a = a.reshape(SSM_GROUPS, SSM_GROUP_CH, SSM_STATE).transpose(0, 2, 1)
    return a.reshape(shape)


def _mesh_place():
    x, y, c = lax.axis_index("x"), lax.axis_index("y"), lax.axis_index("c")
    other_chips = ((1 - x, y), (x, 1 - y), (1 - x, 1 - y))
    return x, y, c, other_chips


def _gather_copies(s_refs, g_refs, send_sems, recv_sems, local_sems):
    x, y, c, other_chips = _mesh_place()
    started = []
    for i, (s_ref, g_ref) in enumerate(zip(s_refs, g_refs)):
        rows = s_ref.shape[0]
        half = rows // 2

        def block(chip, g_ref=g_ref, rows=rows, half=half):
            return g_ref.at[pl.ds((2 * chip[0] + chip[1]) * rows + c * half, half), :]

        def copy(k, chip, to, src=None, i=i, block=block):
            return pltpu.make_async_remote_copy(
                src_ref=block(chip) if src is None else src, dst_ref=block(chip), send_sem=send_sems.at[6 * i + k],
                recv_sem=recv_sems.at[6 * i + k], device_id=to, device_id_type=MESH)

        own = pltpu.make_async_copy(s_ref, g_ref.at[pl.ds((2 * x + y) * rows, rows), :], local_sems.at[i])
        own.start()
        first = [copy(k, (x, y), (*chip, c), src=s_ref.at[pl.ds(c * half, half), :])
                 for k, chip in enumerate(other_chips)]
        for cp in first:
            cp.start()
        passed = [copy(3 + k, chip, (x, y, 1 - c)) for k, chip in enumerate(other_chips)]
        started.append((own, first, passed))
    for own, first, passed in started:
        for k in range(3):
            first[k].wait_recv()
            passed[k].start()
    for own, first, passed in started:
        for k in range(3):
            passed[k].wait_recv()
        for cp in first + passed:
            cp.wait_send()
        own.wait()


def _gather_semaphores(n_t):
    return [pltpu.SemaphoreType.DMA((6 * n_t,)), pltpu.SemaphoreType.DMA((6 * n_t,)), pltpu.SemaphoreType.DMA((n_t,))]


def _gather_weights_beside(shards, name, collective_id):
    n_t = len(shards)
    hbm = pltpu.MemorySpace.HBM
    s_refs = [jax.new_ref(s, memory_space=hbm) for s in shards]
    g_refs = [jax.empty_ref(jax.ShapeDtypeStruct((N_CHIPS * s.shape[0], s.shape[1]), s.dtype), memory_space=hbm)
              for s in shards]

    def launch(send_sems, recv_sems, local_sems):
        x, y, c, other_chips = _mesh_place()
        peers = [(*chip, c) for chip in other_chips] + [(x, y, 1 - c)]
        barrier = pltpu.get_barrier_semaphore()
        for peer in peers:
            pl.semaphore_signal(barrier, inc=1, device_id=peer, device_id_type=MESH)
        pl.semaphore_wait(barrier, len(peers))
        _gather_copies(s_refs, g_refs, send_sems, recv_sems, local_sems)

    pl.kernel(launch, mesh=plsc.ScalarSubcoreMesh(axis_name="sequencer", num_cores=1), name=name,
              scratch_types=_gather_semaphores(n_t), compiler_params=pltpu.CompilerParams(collective_id=collective_id))()
    return [g[...] for g in g_refs]


_RELATIONS = tuple(((r >> 2) & 1, (r >> 1) & 1, r & 1) for r in range(1, 8))


def _related(place, relation):
    return tuple(1 - a if flip else a for a, flip in zip(place, relation))


def _scatter_beside(mats, name, collective_id):
    hbm = pltpu.MemorySpace.HBM
    src_refs = [jax.new_ref(a, memory_space=hbm) for a in mats]
    land_refs = [jax.empty_ref(jax.ShapeDtypeStruct((7, a.shape[0] // 8, a.shape[1]), a.dtype), memory_space=hbm)
                 for a in mats]

    def launch(send_sems, recv_sems):
        me = (lax.axis_index("x"), lax.axis_index("y"), lax.axis_index("c"))
        peers = [_related(me, rel) for rel in _RELATIONS]
        barrier = pltpu.get_barrier_semaphore()
        for peer in peers:
            pl.semaphore_signal(barrier, inc=1, device_id=peer, device_id_type=MESH)
        pl.semaphore_wait(barrier, len(peers))
        copies = []
        for i, (src, land) in enumerate(zip(src_refs, land_refs)):
            hr = land.shape[1]
            for k, (tx, ty, tc) in enumerate(peers):
                rows = pl.ds((2 * tx + ty) * 2 * hr + tc * hr, hr)
                copies.append(pltpu.make_async_remote_copy(
                    src_ref=src.at[rows, :], dst_ref=land.at[k], send_sem=send_sems.at[7 * i + k],
                    recv_sem=recv_sems.at[7 * i + k], device_id=(tx, ty, tc), device_id_type=MESH))
                copies[-1].start()
        for cp in copies:
            cp.wait()

    n_sems = 7 * len(mats)
    pl.kernel(launch, mesh=plsc.ScalarSubcoreMesh(axis_name="sequencer", num_cores=1), name=name,
              scratch_types=[pltpu.SemaphoreType.DMA((n_sems,)), pltpu.SemaphoreType.DMA((n_sems,))],
              compiler_params=pltpu.CompilerParams(collective_id=collective_id))()
    return [ref[...] for ref in land_refs]


def _broadcast_beside(arrays):
    hbm = pltpu.MemorySpace.HBM
    src_refs = [jax.new_ref(a, memory_space=hbm) for a in arrays]
    land_refs = [jax.empty_ref(jax.ShapeDtypeStruct((len(_RELATIONS),) + a.shape, a.dtype), memory_space=hbm)
                 for a in arrays]

    def launch(send_sems, recv_sems):
        me = (lax.axis_index("x"), lax.axis_index("y"), lax.axis_index("c"))
        peers = [_related(me, rel) for rel in _RELATIONS]
        barrier = pltpu.get_barrier_semaphore()
        for peer in peers:
            pl.semaphore_signal(barrier, inc=1, device_id=peer, device_id_type=MESH)
        pl.semaphore_wait(barrier, len(peers))
        copies = []
        for i, (src, land) in enumerate(zip(src_refs, land_refs)):
            for k, peer in enumerate(peers):
                copies.append(pltpu.make_async_remote_copy(
                    src_ref=src, dst_ref=land.at[k], send_sem=send_sems.at[7 * i + k],
                    recv_sem=recv_sems.at[7 * i + k], device_id=peer, device_id_type=MESH))
                copies[-1].start()
        for cp in copies:
            cp.wait()

    n_sems = 7 * len(arrays)
    pl.kernel(launch, mesh=plsc.ScalarSubcoreMesh(axis_name="sequencer", num_cores=1), name="broadcast_beside",
              scratch_types=[pltpu.SemaphoreType.DMA((n_sems,)), pltpu.SemaphoreType.DMA((n_sems,))],
              compiler_params=pltpu.CompilerParams(collective_id=3))()
    return [ref[...] for ref in land_refs]


def _exchange_grads(big, outputs, small, landed, own_bc, landed_bc):
    n_t = len(big)
    n_g = len(_SMALL_GROUPS)
    names = _SMALL_ORDER
    halves = [(b.shape[0] // N_CHIPS // 2, b.shape[1]) for b in big]
    early = sorted(landed)
    late = [i for i in range(n_t) if i not in landed]
    n_sems = 4 * n_g + 7 * len(late) + n_t
    small_sem0, block_sem0 = n_t, n_t + len(names)
    early_sem0 = block_sem0 + N_CHIPS * len(late)
    landed_sem0 = early_sem0 + 2 * len(early)
    sent = [n for n in names if n in small]

    def body(*refs):
        pos = 0

        def take(n):
            nonlocal pos
            pos += n
            return refs[pos - n:pos]

        big_refs, small_refs = take(n_t), dict(zip(sent, take(len(sent))))
        land_refs = dict(zip(early, take(len(early))))
        own_bc_ref, landed_bc_ref = take(2)
        out_refs, small_out_refs = take(len(outputs)), dict(zip(names, take(len(names))))
        per_late = lambda: dict(zip(late, take(len(late))))
        ga, gb, pme, send_b, recv_b = per_late(), per_late(), take(n_t), per_late(), per_late()
        own_e, land_e = dict(zip(early, take(len(early)))), dict(zip(early, take(len(early))))
        own_s, land_s = take(2)
        s_own, s_sib, s_chips, s_pair = take(n_g), take(n_g), take(n_g), take(n_g)
        stage = dict(zip(names, take(len(names))))
        send_sems, recv_sems, local_sems = take(3)
        x, y, c, other_chips = _mesh_place()
        me = 2 * x + y
        sibling = (x, y, 1 - c)
        sem_at = iter(range(n_sems))

        def remote(src, dst, to):
            k = next(sem_at)
            return pltpu.make_async_remote_copy(src_ref=src, dst_ref=dst, send_sem=send_sems.at[k],
                                                recv_sem=recv_sems.at[k], device_id=to, device_id_type=MESH)

        loads = [pltpu.make_async_copy(small_refs[name], stage[name], local_sems.at[small_sem0 + names.index(name)])
                 for name in sent]
        landed_loads = [pltpu.make_async_copy(own_bc_ref, own_s, local_sems.at[landed_sem0]),
                        pltpu.make_async_copy(landed_bc_ref, land_s, local_sems.at[landed_sem0 + 1])]
        for cp in loads + landed_loads:
            cp.start()
        for cp in loads:
            cp.wait()
        small_swaps = []
        for gi, (_, _, members) in enumerate(_SMALL_GROUPS):
            s_own[gi][...] = jnp.zeros_like(s_own[gi])
            for name, r0 in members:
                r, n = _small_shape(name)
                s_own[gi][r0:r0 + r, 0:n] = stage[name][...]
            small_swaps.append(remote(s_own[gi], s_sib[gi], sibling))
            small_swaps[gi].start()
        order = sorted(late, key=lambda i: halves[i][0] * halves[i][1])
        own_loads, big_swaps = {}, {}
        for i in order:
            hr = halves[i][0]
            own_loads[i], big_swaps[i] = [], []
            for j in range(N_CHIPS):
                mine = big_refs[i].at[pl.ds(j * 2 * hr + c * hr, hr), :]
                theirs = big_refs[i].at[pl.ds(j * 2 * hr + (1 - c) * hr, hr), :]
                sem = local_sems.at[block_sem0 + N_CHIPS * late.index(i) + j]
                own_loads[i].append(pltpu.make_async_copy(mine, ga[i].at[j], sem))
                own_loads[i][j].start()
                big_swaps[i].append(remote(theirs, gb[i].at[j], sibling))
                big_swaps[i][j].start()
        early_loads = {}
        for e, i in enumerate(early):
            hr = halves[i][0]
            mine = big_refs[i].at[pl.ds(me * 2 * hr + c * hr, hr), :]
            early_loads[i] = [pltpu.make_async_copy(mine, own_e[i], local_sems.at[early_sem0 + 2 * e]),
                              pltpu.make_async_copy(land_refs[i], land_e[i], local_sems.at[early_sem0 + 2 * e + 1])]
            for cp in early_loads[i]:
                cp.start()
        small_sends = []
        for gi in range(n_g):
            small_swaps[gi].wait_recv()
            s_pair[gi][...] = s_own[gi][...] + s_sib[gi][...]
            small_sends.append([remote(s_pair[gi], s_chips[gi].at[k], (*chip, c)) for k, chip in enumerate(other_chips)])
            for cp in small_sends[gi]:
                cp.start()

        def pair_sum(i, j):
            return ga[i][j] + gb[i][j]

        big_sends = {}
        for i in order:
            for j in range(N_CHIPS):
                own_loads[i][j].wait()
                big_swaps[i][j].wait_recv()
            big_sends[i] = []
            for k, chip in enumerate(other_chips):
                send_b[i][k] = pair_sum(i, 2 * chip[0] + chip[1]).astype(BF16)
                big_sends[i].append(remote(send_b[i].at[k], recv_b[i].at[k], (*chip, c)))
                big_sends[i][k].start()
        last_swaps, keeps = {}, {}
        for i in early + order:
            hr = halves[i][0]
            if i in landed:
                for cp in early_loads[i]:
                    cp.wait()
                total = own_e[i][...]
                for k in range(len(_RELATIONS)):
                    total = total + land_e[i][k].astype(F32)
                pme[i][...] = total
            else:
                for k in range(3):
                    big_sends[i][k].wait_recv()
                pme[i][...] = ((pair_sum(i, me) + recv_b[i][0].astype(F32)) + recv_b[i][1].astype(F32)) + recv_b[i][2].astype(F32)
            o, = [o for o, group in enumerate(outputs) if i in group]
            first_col = sum(halves[j][1] for j in outputs[o][:outputs[o].index(i)])
            mine = out_refs[o].at[pl.ds(c * hr, hr), pl.ds(first_col, halves[i][1])]
            keeps[i] = pltpu.make_async_copy(pme[i], mine, local_sems.at[i])
            keeps[i].start()
            last_swaps[i] = remote(pme[i], mine, sibling)
            last_swaps[i].start()

        for gi, (_, _, members) in enumerate(_SMALL_GROUPS):
            for k in range(3):
                small_sends[gi][k].wait_recv()
            total = None
            for j in range(N_CHIPS):
                rel = jnp.bitwise_xor(j, me)
                term = jnp.where(rel == 0, s_pair[gi][...], jnp.where(
                    rel == 2, s_chips[gi][0], jnp.where(rel == 1, s_chips[gi][1], s_chips[gi][2])))
                total = term if total is None else total + term
            s_sib[gi][...] = total
            for name, r0 in members:
                r, n = _small_shape(name)
                stage[name][...] = s_sib[gi][r0:r0 + r, 0:n]
        my_index = 4 * x + 2 * y + c
        for cp in landed_loads:
            cp.wait()
        total = None
        for d in range(2 * N_CHIPS):
            rel = jnp.bitwise_xor(d, my_index)
            term = own_s[...]
            for k in range(len(_RELATIONS)):
                term = jnp.where(rel == k + 1, land_s[k], term)
            total = term.astype(F32) if total is None else total + term.astype(F32)
        for a, name in enumerate(_SMALL_EARLY):
            stage[name][...] = total[:, a * SSM_STATE:(a + 1) * SSM_STATE]
        stores = [pltpu.make_async_copy(stage[name], small_out_refs[name], local_sems.at[small_sem0 + a])
                  for a, name in enumerate(names)]
        for cp in stores:
            cp.start()

        for i in range(n_t):
            last_swaps[i].wait_recv()
            keeps[i].wait()
        for cp in stores:
            cp.wait()
        groups = list(big_swaps.values()) + small_sends + list(big_sends.values())
        for cp in small_swaps + [cp for group in groups for cp in group] + list(last_swaps.values()):
            cp.wait_send()

    any_spec = pl.BlockSpec(memory_space=pl.ANY)
    small_shapes = [_sds(_small_shape(n)) for n in names]
    group_shapes = [shape for _, shape, _ in _SMALL_GROUPS]
    vmem = lambda which, dtype, lead=(): [pltpu.VMEM(lead + halves[i], dtype) for i in which]
    outs = _call(
        body, name="exchange_grads",
        in_specs=[any_spec] * (n_t + len(sent) + len(early) + 2),
        out_specs=[any_spec] * (len(outputs) + len(names)),
        out_shape=[_sds((big[group[0]].shape[0] // N_CHIPS, sum(big[i].shape[1] for i in group))) for group in outputs]
        + small_shapes,
        scratch_shapes=(vmem(late, F32, (N_CHIPS,)) + vmem(late, F32, (N_CHIPS,)) + vmem(range(n_t), F32)
                        + vmem(late, BF16, (3,)) + vmem(late, BF16, (3,))
                        + vmem(early, F32)
                        + [pltpu.VMEM((len(_RELATIONS),) + halves[i], landed[i].dtype) for i in early]
                        + [pltpu.VMEM(own_bc.shape, own_bc.dtype), pltpu.VMEM(landed_bc.shape, landed_bc.dtype)]
                        + [pltpu.VMEM(s, F32) for s in group_shapes] * 2 + [pltpu.VMEM((3,) + s, F32) for s in group_shapes]
                        + [pltpu.VMEM(s, F32) for s in group_shapes]
                        + [pltpu.VMEM(_small_shape(n), F32) for n in names]
                        + [pltpu.SemaphoreType.DMA((n_sems,)), pltpu.SemaphoreType.DMA((n_sems,)),
                           pltpu.SemaphoreType.DMA((landed_sem0 + 2,))]),
        compiler_params=_params(48),
    )(*big, *[small[n] for n in sent], *[landed[i] for i in early], own_bc, landed_bc)
    return list(outs[:len(outputs)]), dict(zip(names, outs[len(outputs):]))


def _adamw_update(w, g, m, v):
    m = ADAM_B1 * m + (1.0 - ADAM_B1) * g
    v = ADAM_B2 * v + (1.0 - ADAM_B2) * (g * g)
    m_hat = m / (1.0 - ADAM_B1 ** ADAM_STEP)
    v_hat = v / (1.0 - ADAM_B2 ** ADAM_STEP)
    return -ADAM_LR * (m_hat / (jnp.sqrt(v_hat) + ADAM_EPS) + ADAM_WD * w), m, v


def _adamw(w, g, m, v, grid, name):
    n_t = len(w)

    def body(*refs):
        ins, outs = refs[:4 * n_t], refs[4 * n_t:]
        for i in range(n_t):
            w_, g_, m_, v_ = [ins[a * n_t + i][...] for a in range(4)]
            vals = (g_,) + _adamw_update(w_, g_, m_, v_)
            for a in range(4):
                outs[a * n_t + i][...] = vals[a]

    specs = [pl.BlockSpec((a.shape[0] // grid, a.shape[1]), lambda i: (i, 0)) for a in w]
    shapes = [_sds(a.shape) for a in w]
    outs = _call(
        body, name=name, grid=(grid,), in_specs=specs * 4, out_specs=specs * 4, out_shape=shapes * 4,
        compiler_params=_params(40, ("arbitrary",)),
    )(*w, *g, *m, *v)
    return [outs[a * n_t:(a + 1) * n_t] for a in range(4)]


def kernel(x, p, pre_norm_g, w_in, ssm_lam_re, ssm_lam_im, ssm_log_step, ssm_b_re, ssm_b_im, ssm_c_re, ssm_c_im, ssm_d, ssm_w_glu, ssm_b_glu, attn_sinks, w_out, post_norm_g, pl_w_proj, pl_w_gate, pl_b_gate, loss_target, m_pre_norm_g, m_w_in, m_ssm_lam_re, m_ssm_lam_im, m_ssm_log_step, m_ssm_b_re, m_ssm_b_im, m_ssm_c_re, m_ssm_c_im, m_ssm_d, m_ssm_w_glu, m_ssm_b_glu, m_attn_sinks, m_w_out, m_post_norm_g, m_pl_w_proj, m_pl_w_gate, m_pl_b_gate, v_pre_norm_g, v_w_in, v_ssm_lam_re, v_ssm_lam_im, v_ssm_log_step, v_ssm_b_re, v_ssm_b_im, v_ssm_c_re, v_ssm_c_im, v_ssm_d, v_ssm_w_glu, v_ssm_b_glu, v_attn_sinks, v_w_out, v_post_norm_g, v_pl_w_proj, v_pl_w_gate, v_pl_b_gate):
    weights = dict(pre_norm_g=pre_norm_g, w_in=w_in, ssm_lam_re=ssm_lam_re, ssm_lam_im=ssm_lam_im,
                   ssm_log_step=ssm_log_step, ssm_b_re=ssm_b_re, ssm_b_im=ssm_b_im, ssm_c_re=ssm_c_re,
                   ssm_c_im=ssm_c_im, ssm_d=ssm_d, ssm_w_glu=ssm_w_glu, ssm_b_glu=ssm_b_glu, attn_sinks=attn_sinks,
                   w_out=w_out, post_norm_g=post_norm_g, pl_w_proj=pl_w_proj, pl_w_gate=pl_w_gate, pl_b_gate=pl_b_gate)
    m_in = dict(pre_norm_g=m_pre_norm_g, w_in=m_w_in, ssm_lam_re=m_ssm_lam_re, ssm_lam_im=m_ssm_lam_im,
                ssm_log_step=m_ssm_log_step, ssm_b_re=m_ssm_b_re, ssm_b_im=m_ssm_b_im, ssm_c_re=m_ssm_c_re,
                ssm_c_im=m_ssm_c_im, ssm_d=m_ssm_d, ssm_w_glu=m_ssm_w_glu, ssm_b_glu=m_ssm_b_glu,
                attn_sinks=m_attn_sinks, w_out=m_w_out, post_norm_g=m_post_norm_g, pl_w_proj=m_pl_w_proj,
                pl_w_gate=m_pl_w_gate, pl_b_gate=m_pl_b_gate)
    v_in = dict(pre_norm_g=v_pre_norm_g, w_in=v_w_in, ssm_lam_re=v_ssm_lam_re, ssm_lam_im=v_ssm_lam_im,
                ssm_log_step=v_ssm_log_step, ssm_b_re=v_ssm_b_re, ssm_b_im=v_ssm_b_im, ssm_c_re=v_ssm_c_re,
                ssm_c_im=v_ssm_c_im, ssm_d=v_ssm_d, ssm_w_glu=v_ssm_w_glu, ssm_b_glu=v_ssm_b_glu,
                attn_sinks=v_attn_sinks, w_out=v_w_out, post_norm_g=v_post_norm_g, pl_w_proj=v_pl_w_proj,
                pl_w_gate=v_pl_w_gate, pl_b_gate=v_pl_b_gate)

    def two_d(tree):
        return {k: _to_kernel_form(k, a) for k, a in tree.items()}

    w2, m2, v2 = two_d(weights), two_d(m_in), two_d(v_in)

    (w_in_full,) = _gather_weights_beside([w2["w_in"].astype(BF16)], "gather_w_in_beside", 4)
    s5_params = tuple(w2[n] for n in ("ssm_lam_re", "ssm_lam_im", "ssm_log_step", "ssm_b_re", "ssm_b_im", "ssm_c_re",
                                      "ssm_c_im"))
    s5_operands = _s5_params_fwd(*s5_params)
    hn = _pre_norm(x.reshape(-1, D_MODEL), w2["pre_norm_g"])
    behind = s5_operands[0][0, 0] * 0.0 + hn[0, 0].astype(F32) * 0.0
    rest = _gather_weights_beside([(w2[n] + behind).astype(BF16) for n in _BIG[1:]], "gather_weights_beside", 1)
    full = dict(zip(_BIG, [w_in_full] + rest))
    mats = ("w_in_early", "w_in_late") + _BIG[1:]
    landed, bc = {}, {}

    def send_tail_grads(ready):
        sent_early = ("w_out", "pl_w_gate", "pl_w_proj")
        landed.update(zip([mats.index(n) for n in sent_early],
                          _scatter_beside([ready[n] for n in sent_early], "scatter_beside", 2)))
        return landed[mats.index(sent_early[-1])]

    def send_bc(d_bc):
        bc["own"] = d_bc
        bc["landed"] = _broadcast_beside([d_bc])[0]
        return d_bc, bc["landed"]

    grad_x, loss, grads = _local_step(
        x, hn, p, loss_target, w2["pre_norm_g"], full["w_in"], s5_params, s5_operands, w2["ssm_d"], full["ssm_w_glu"], w2["ssm_b_glu"],
        w2["attn_sinks"], full["w_out"], w2["post_norm_g"], full["pl_w_proj"], full["pl_w_gate"], w2["pl_b_gate"], send_tail_grads, send_bc)

    landed[0] = _scatter_beside([grads["w_in_early_bf16"]], "scatter_w_in_beside", 5)[0]
    sent_here = {**{n: grads[n] for n in _SMALL if n not in _SMALL_EARLY}, "loss": loss}
    halves_of_w_in = ((0, 1),) + tuple((i,) for i in range(2, len(mats)))
    g_big, g_small = _exchange_grads([grads[n] for n in mats], halves_of_w_in, sent_here, landed, bc["own"], bc["landed"])
    g_big = dict(zip(_BIG, g_big))
    total_loss = g_small.pop("loss")

    big_out = _adamw([w2[n] for n in _BIG], [g_big[n] for n in _BIG], [m2[n] for n in _BIG], [v2[n] for n in _BIG],
                     8, "adamw_matrices")
    small_names = tuple(_SMALL)
    small_out = _adamw([w2[n] for n in small_names], [g_small[n] for n in small_names], [m2[n] for n in small_names],
                       [v2[n] for n in small_names], 1, "adamw_small")

    results = [{**dict(zip(_BIG, big_part)), **dict(zip(small_names, small_part))}
               for big_part, small_part in zip(big_out, small_out)]
    flat = [_from_kernel_form(name, r[name], weights[name].shape) for r in results for name in _WEIGHT_ORDER]
    return (total_loss.reshape(()), grad_x, *flat)
```

```python
import math

import jax
import jax.numpy as jnp
from jax import lax
from jax.experimental import pallas as pl
from jax.experimental.pallas import tpu as pltpu
from jax.experimental.pallas import tpu_sc as plsc

F32 = jnp.float32
BF16 = jnp.bfloat16

D_MODEL = 1024
D_SSM = 512
D_ATTN = 512
SSM_GROUPS = 32
SSM_GROUP_CH = 16
SSM_STATE = 64
SSM_LANES = SSM_GROUPS * SSM_STATE
HEAD_DIM = 64
N_HEADS = 8
KV_HEADS = 2
Q_PER_KV = 4
WINDOW = 128
BLOCK = 128
D_PLE = 256
D_IN = 2304
EPS = 1e-6
ATTN_SCALE = 1.0 / math.sqrt(HEAD_DIM)

ADAM_LR = 0.001
ADAM_B1 = 0.9
ADAM_B2 = 0.999
ADAM_EPS = 1e-08
ADAM_WD = 0.01
ADAM_STEP = 10

N_CHIPS = 4
LANES = 128
SCAN_CHUNKS = 8
SCAN_TILE_STEPS = 32
SCAN_LANE_CHUNK = 512
MIB = 2 ** 20
MESH = pl.DeviceIdType.MESH


def _dot(a, b):
    return jnp.dot(a, b, preferred_element_type=F32)


def _dot_nt(a, b):
    return lax.dot_general(a, b, (((1,), (1,)), ((), ())), preferred_element_type=F32)


def _dot_tn(a, b):
    return lax.dot_general(a, b, (((0,), (0,)), ((), ())), preferred_element_type=F32)


def _params(vmem_mib, semantics=None):
    kw = dict(vmem_limit_bytes=vmem_mib * MIB)
    if semantics is not None:
        kw["dimension_semantics"] = semantics
    return pltpu.CompilerParams(**kw)


def _full(shape):
    nd = len(shape)
    return pl.BlockSpec(shape, lambda *_: (0,) * nd, pipeline_mode=pl.Buffered(1))


def _rows(tm, width):
    return pl.BlockSpec((tm, width), lambda i: (i, 0))


def _sds(shape, dtype=F32):
    return pltpu.HBM(shape, dtype)


def _call(body, **kw):
    fn = pl.pallas_call(body, **kw)
    return lambda *args: fn(*[pltpu.with_memory_space_constraint(a, pltpu.HBM) for a in args])


def _silu(z):
    return z * jax.nn.sigmoid(z)


def _pre_norm(x2d, g1):
    rows = x2d.shape[0]
    tm = 512

    def body(x_ref, g_ref, hn_ref):
        x = x_ref[...]
        r = lax.rsqrt(jnp.mean(x * x, axis=-1, keepdims=True) + EPS)
        hn_ref[...] = (x * r * g_ref[...]).astype(BF16)

    return _call(
        body, name="pre_norm", grid=(rows // tm,), in_specs=[_rows(tm, D_MODEL), _full((1, D_MODEL))],
        out_specs=_rows(tm, D_MODEL), out_shape=_sds((rows, D_MODEL), BF16), compiler_params=_params(32, ("arbitrary",)),
    )(x2d, g1)


def _in_proj(hn, w_in_t, n_seq, seq):
    rows = hn.shape[0]
    tm = 1024
    slab, steps, _, _ = _scan_geometry(n_seq, seq)

    def body(hn_ref, w_ref, *out_refs):
        u_parts, (zs_ref, q_ref, k_ref, v_ref, za_ref) = out_refs[:_SCAN_PARTS], out_refs[_SCAN_PARTS:]
        whole = _dot_nt(hn_ref[...], w_ref[...])

        def proj(a, b):
            return whole[:, a:b]

        _store_chunks(u_parts, pl.program_id(0) * (tm // steps), proj(0, 512), steps, slab)
        zs_ref[...] = proj(512, 1024)
        q_ref[...] = (proj(1024, 1536) * ATTN_SCALE).astype(BF16)
        k_ref[...] = proj(1536, 1664).astype(BF16)
        v_ref[...] = proj(1664, 1792).astype(BF16)
        za_ref[...] = proj(1792, 2304)

    *u_parts, zs, q, k, v, za = _call(
        body, name="in_proj", grid=(rows // tm,),
        in_specs=[_rows(tm, D_MODEL), _full((D_IN, D_MODEL))],
        out_specs=_whole_parts(rows) + [_rows(tm, 512), _rows(tm, 512), _rows(tm, 128), _rows(tm, 128), _rows(tm, 512)],
        out_shape=_part_shapes(rows) + [_sds((rows, 512)), _sds((rows, 512), BF16), _sds((rows, 128), BF16),
                                        _sds((rows, 128), BF16), _sds((rows, 512))],
        compiler_params=_params(48, ("arbitrary",)),
    )(hn, w_in_t)
    return u_parts, zs, q, k, v, za


_EARLY_COLS = D_MODEL // 2


def _in_proj_bwd_early(hn, du_parts, dzs, dq, dk, dv, dza, runs_after, n_seq, seq):
    rows = hn.shape[0]
    tm = 1024
    slab, steps, _, _ = _scan_geometry(n_seq, seq)

    def body(hn_ref, *refs):
        du_parts, (dzs_ref, dq_ref, dk_ref, dv_ref, dza_ref, _, dproj_ref, dw_ref, dwb_ref) = refs[:_SCAN_PARTS], refs[_SCAN_PARTS:]
        i = pl.program_id(0)

        @pl.when(i == 0)
        def _():
            dw_ref[...] = jnp.zeros_like(dw_ref)

        du = _load_chunks(du_parts, i * (tm // steps), tm // steps, steps, slab)
        d_proj = jnp.concatenate([du.astype(BF16), dzs_ref[...], dq_ref[...], dk_ref[...], dv_ref[...], dza_ref[...]],
                                 axis=1)
        dproj_ref[...] = d_proj
        dw_ref[...] += _dot_tn(d_proj, hn_ref[...])

        @pl.when(i == rows // tm - 1)
        def _():
            dwb_ref[...] = dw_ref[...].astype(BF16)

    return _call(
        body, name="in_proj_bwd_early", grid=(rows // tm,),
        in_specs=[_rows(tm, _EARLY_COLS)] + _whole_parts(rows)
        + [_rows(tm, 512), _rows(tm, 512), _rows(tm, 128), _rows(tm, 128), _rows(tm, 512),
           pl.BlockSpec(memory_space=pl.ANY)],
        out_specs=[_rows(tm, D_IN), _full((D_IN, _EARLY_COLS)), _full((D_IN, _EARLY_COLS))],
        out_shape=[_sds((rows, D_IN), BF16), _sds((D_IN, _EARLY_COLS)), _sds((D_IN, _EARLY_COLS), BF16)],
        compiler_params=_params(48, ("arbitrary",)),
    )(hn, *du_parts, dzs, dq, dk, dv, dza, runs_after)


def _in_proj_bwd(x2d, dh1, g1, w_in_t, d_proj, runs_after):
    rows = x2d.shape[0]
    tm = 512

    def body(x_ref, dh1_ref, g_ref, w_ref, dproj_ref, _, gx_ref, dw_ref, dg_ref):
        @pl.when(pl.program_id(0) == 0)
        def _():
            dw_ref[...] = jnp.zeros_like(dw_ref)
            dg_ref[...] = jnp.zeros_like(dg_ref)

        x = x_ref[...]
        g = g_ref[...]
        r = lax.rsqrt(jnp.mean(x * x, axis=-1, keepdims=True) + EPS)
        xr = x * r
        hn = (xr[:, _EARLY_COLS:] * g[:, _EARLY_COLS:]).astype(BF16)
        d_proj = dproj_ref[...]
        dhn = _dot(d_proj, w_ref[...])
        dw_ref[...] += _dot_tn(d_proj, hn)
        dg_ref[...] += jnp.sum(dhn * xr, axis=0, keepdims=True)
        a_ = dhn * g
        gx_ref[...] = dh1_ref[...] + r * a_ - xr * (r * jnp.mean(a_ * xr, axis=-1, keepdims=True))

    late_cols = D_MODEL - _EARLY_COLS
    return _call(
        body, name="in_proj_bwd", grid=(rows // tm,),
        in_specs=[_rows(tm, D_MODEL), _rows(tm, D_MODEL), _full((1, D_MODEL)), _full((D_IN, D_MODEL)), _rows(tm, D_IN),
                  pl.BlockSpec(memory_space=pl.ANY)],
        out_specs=[_rows(tm, D_MODEL), _full((D_IN, late_cols)), _full((1, D_MODEL))],
        out_shape=[_sds((rows, D_MODEL)), _sds((D_IN, late_cols)), _sds((1, D_MODEL))],
        compiler_params=_params(52, ("arbitrary",)),
    )(x2d, dh1, g1, w_in_t, d_proj, runs_after)


def _iota(shape, axis):
    return lax.broadcasted_iota(jnp.int32, shape, axis)


def _sum_of_thirds(f, a):
    hi = a.astype(BF16)
    rest = a - hi.astype(F32)
    mid = rest.astype(BF16)
    low = (rest - mid.astype(F32)).astype(BF16)
    return (f(hi) + f(mid)) + f(low)


@jax.custom_vjp
def _pick_rows(e, a):
    return _sum_of_thirds(lambda part: _dot(e, part), a)


def _pick_rows_fwd(e, a):
    return _pick_rows(e, a), e


def _pick_rows_bwd(e, ct):
    return jnp.zeros_like(e), _sum_of_thirds(lambda part: _dot_tn(e, part), ct)


_pick_rows.defvjp(_pick_rows_fwd, _pick_rows_bwd)


@jax.custom_vjp
def _pick_cols(a, e):
    return _sum_of_thirds(lambda part: _dot(part, e), a)


def _pick_cols_fwd(a, e):
    return _pick_cols(a, e), e


def _pick_cols_bwd(e, ct):
    return _sum_of_thirds(lambda part: _dot_nt(part, e), ct), jnp.zeros_like(e)


_pick_cols.defvjp(_pick_cols_fwd, _pick_cols_bwd)


_HALF_GROUPS = SSM_GROUPS // 2
_N_SHIFT = SSM_STATE.bit_length() - 1
_P_SHIFT = SSM_GROUP_CH.bit_length() - 1


def _s5_operands(lam_re, lam_im, log_step, b_re, b_im, c_re, c_im):
    g, n, p = SSM_GROUPS, SSM_STATE, SSM_GROUP_CH
    gn, gp, hn_, hp = g * n, g * p, _HALF_GROUPS * n, _HALF_GROUPS * p
    eye_g = _iota((g, g), 0) == _iota((g, g), 1)
    step = jnp.sum(jnp.where(eye_g, jnp.exp(log_step), 0.0), axis=1, keepdims=True)
    a_re = lam_re * step
    a_im = lam_im * step
    mag = jnp.exp(a_re)
    lbar_re = mag * jnp.cos(a_im)
    lbar_im = mag * jnp.sin(a_im)
    n_re = lbar_re - 1.0
    den = lam_re * lam_re + lam_im * lam_im
    f_re = (n_re * lam_re + lbar_im * lam_im) / den
    f_im = (lbar_im * lam_re - n_re * lam_im) / den

    spread_n = (_iota((n, gn), 0) == (_iota((n, gn), 1) & (n - 1))).astype(BF16)
    own_g = _iota((g, gn), 0) == (_iota((g, gn), 1) >> _N_SHIFT)

    def to_row(a):
        return jnp.sum(jnp.where(own_g, _pick_cols(a, spread_n), 0.0), axis=0, keepdims=True)

    per_group = ((_iota((gp, g), 0) >> _P_SHIFT) == _iota((gp, g), 1)).astype(BF16)
    fx_re, fx_im = _pick_rows(per_group, f_re), _pick_rows(per_group, f_im)
    bbar_re = fx_re * b_re - fx_im * b_im
    bbar_im = fx_re * b_im + fx_im * b_re

    tile_n = (_iota((n, hn_), 0) == (_iota((n, hn_), 1) & (n - 1))).astype(BF16)
    same_group = (_iota((hp, hn_), 0) >> _P_SHIFT) == (_iota((hp, hn_), 1) >> _N_SHIFT)

    def embed(a, hf):
        return jnp.where(same_group, _pick_cols(a[hf * hp:(hf + 1) * hp], tile_n), 0.0)

    return (to_row(lbar_re), to_row(lbar_im), embed(bbar_re, 0), embed(bbar_re, 1), embed(bbar_im, 0),
            embed(bbar_im, 1), embed(c_re, 0), embed(c_re, 1), embed(c_im, 0), embed(c_im, 1))


_S5_PARAM_SHAPES = ((SSM_GROUPS, SSM_STATE), (SSM_GROUPS, SSM_STATE), (1, SSM_GROUPS),
                    (D_SSM, SSM_STATE), (D_SSM, SSM_STATE), (D_SSM, SSM_STATE), (D_SSM, SSM_STATE))
_CM_SHAPE = (2, _HALF_GROUPS * SSM_GROUP_CH, _HALF_GROUPS * SSM_STATE)
_S5_OPERAND_SHAPES = ((1, SSM_LANES), (1, SSM_LANES), _CM_SHAPE, _CM_SHAPE, _CM_SHAPE, _CM_SHAPE)


def _s5_params_fwd(*params):
    def body(*refs):
        ins, (lre_ref, lim_ref, btre_ref, btim_ref, cmre_ref, cmim_ref) = refs[:7], refs[7:]
        vals = _s5_operands(*[r[...] for r in ins])
        lre_ref[...] = vals[0]
        lim_ref[...] = vals[1]
        for ref, pair in zip((btre_ref, btim_ref, cmre_ref, cmim_ref), (vals[2:4], vals[4:6], vals[6:8], vals[8:10])):
            ref[0] = pair[0].astype(BF16)
            ref[1] = pair[1].astype(BF16)

    dtypes = (F32, F32, BF16, BF16, BF16, BF16)
    return _call(
        body, name="s5_params_fwd",
        in_specs=[_full(s) for s in _S5_PARAM_SHAPES], out_specs=[_full(s) for s in _S5_OPERAND_SHAPES],
        out_shape=[_sds(s, d) for s, d in zip(_S5_OPERAND_SHAPES, dtypes)], compiler_params=_params(32),
    )(*params)


_BC_SIDE_BY_SIDE = (D_SSM, 4 * SSM_STATE)


def _s5_params_bwd(params, cotangents, runs_after):
    def body(*refs):
        ins, (dlre, dlim, dbtre, dbtim, dcmre, dcmim), outs = refs[:7], refs[7:13], refs[14:]
        _, vjp = jax.vjp(_s5_operands, *[r[...] for r in ins])
        cts = (dlre[...], dlim[...], dbtre[0], dbtre[1], dbtim[0], dbtim[1], dcmre[0], dcmre[1], dcmim[0], dcmim[1])
        grads = vjp(cts)
        for ref, val in zip(outs[:3], grads[:3]):
            ref[...] = val
        outs[3][...] = jnp.concatenate(grads[3:], axis=1).astype(BF16)

    out_shapes = _S5_PARAM_SHAPES[:3] + (_BC_SIDE_BY_SIDE,)
    return _call(
        body, name="s5_params_bwd",
        in_specs=[_full(s) for s in _S5_PARAM_SHAPES + _S5_OPERAND_SHAPES] + [pl.BlockSpec(memory_space=pl.ANY)],
        out_specs=[_full(s) for s in out_shapes],
        out_shape=[_sds(s, d) for s, d in zip(out_shapes, (F32, F32, F32, BF16))], compiler_params=_params(48),
    )(*params, *cotangents, runs_after)


def _scan_geometry(n_seq, seq):
    slab = n_seq * SCAN_CHUNKS
    steps = seq // SCAN_CHUNKS
    tile_rows = slab * SCAN_TILE_STEPS
    n_tiles = steps // SCAN_TILE_STEPS
    return slab, steps, tile_rows, n_tiles


_SCAN_PARTS = D_SSM // LANES


def _whole_parts(rows):
    return [_full((rows, LANES))] * _SCAN_PARTS


def _part_shapes(rows):
    return [_sds((rows, LANES))] * _SCAN_PARTS


def _load_chunks(parts, first_chunk, n_chunks, steps, slab):
    return jnp.concatenate([
        jnp.concatenate([ref[pl.ds(first_chunk + q, steps, stride=slab), :] for ref in parts], axis=1)
        for q in range(n_chunks)], axis=0)


def _store_chunks(parts, first_chunk, value, steps, slab):
    for q in range(value.shape[0] // steps):
        for j, ref in enumerate(parts):
            ref[pl.ds(first_chunk + q, steps, stride=slab), :] = value[q * steps:(q + 1) * steps,
                                                                     j * LANES:(j + 1) * LANES]


def _join_parts(parts):
    return jnp.concatenate([ref[...] for ref in parts], axis=1)


def _split_parts(parts, value):
    for j, ref in enumerate(parts):
        ref[...] = value[:, j * LANES:(j + 1) * LANES]


def _complex_power(re, im, n):
    out = None
    while n:
        if n & 1:
            out = (re, im) if out is None else (out[0] * re - out[1] * im, out[0] * im + out[1] * re)
        n >>= 1
        if n:
            re, im = re * re - im * im, 2.0 * re * im
    return out


def _chunk_carry(sum_re, sum_im, carry_re, carry_im, a_re, a_im, n_seq, reverse):
    carry_re[...] = jnp.zeros_like(carry_re)
    carry_im[...] = jnp.zeros_like(carry_im)
    for s in range(n_seq):
        order = range(SCAN_CHUNKS - 2, -1, -1) if reverse else range(1, SCAN_CHUNKS)
        for c in order:
            r = s * SCAN_CHUNKS + c
            p = r + 1 if reverse else r - 1
            p_re, p_im = carry_re[p:p + 1, :], carry_im[p:p + 1, :]
            carry_re[r:r + 1, :] = a_re * p_re - a_im * p_im + sum_re[p:p + 1, :]
            carry_im[r:r + 1, :] = a_re * p_im + a_im * p_re + sum_im[p:p + 1, :]


def _s5_scan_fwd(u_parts, bt_re, bt_im, cm_re, cm_im, lbar_re, lbar_im, d_row, n_seq, seq):
    slab, steps, tile_rows, n_tiles = _scan_geometry(n_seq, seq)
    rows = u_parts[0].shape[0]

    def body(*refs):
        u_refs, refs = refs[:_SCAN_PARTS], refs[_SCAN_PARTS:]
        (bre_ref, bim_ref, cre_ref, cim_ref, lre_ref, lim_ref, d_ref), refs = refs[:7], refs[7:]
        y_refs, (hre_ref, him_ref, st_re, st_im, h0_re, h0_im, buf_re, buf_im) = refs[:_SCAN_PARTS], refs[_SCAN_PARTS:]
        second = pl.program_id(0) == 1
        i = pl.program_id(1)

        @pl.when(jnp.logical_and(i == 0, jnp.logical_not(second)))
        def _():
            st_re[...] = jnp.zeros_like(st_re)
            st_im[...] = jnp.zeros_like(st_im)

        u = _join_parts(u_refs)
        ub = u.astype(BF16)
        for hf in range(2):
            cols = slice(hf * 1024, (hf + 1) * 1024)
            buf_re[:, cols] = _dot(ub[:, hf * 256:(hf + 1) * 256], bre_ref[hf])
            buf_im[:, cols] = _dot(ub[:, hf * 256:(hf + 1) * 256], bim_ref[hf])

        for lc in range(SSM_LANES // SCAN_LANE_CHUNK):
            cols = slice(lc * SCAN_LANE_CHUNK, (lc + 1) * SCAN_LANE_CHUNK)
            l_re = jnp.broadcast_to(lre_ref[:, cols], (slab, SCAN_LANE_CHUNK))
            l_im = jnp.broadcast_to(lim_ref[:, cols], (slab, SCAN_LANE_CHUNK))

            def scan_tile(keep_states):
                def step(t, carry):
                    s_re, s_im = carry
                    r0 = pl.multiple_of(t * slab, slab)
                    n_re = l_re * s_re - l_im * s_im + buf_re[pl.ds(r0, slab), cols]
                    n_im = l_re * s_im + l_im * s_re + buf_im[pl.ds(r0, slab), cols]
                    if keep_states:
                        buf_re[pl.ds(r0, slab), cols] = n_re
                        buf_im[pl.ds(r0, slab), cols] = n_im
                    return n_re, n_im

                s_re, s_im = lax.fori_loop(0, SCAN_TILE_STEPS, step, (st_re[:, cols], st_im[:, cols]), unroll=True)
                st_re[:, cols] = s_re
                st_im[:, cols] = s_im

            pl.when(jnp.logical_not(second))(lambda: scan_tile(False))
            pl.when(second)(lambda: scan_tile(True))

        @pl.when(jnp.logical_and(i == n_tiles - 1, jnp.logical_not(second)))
        def _():
            a_re, a_im = _complex_power(lre_ref[...], lim_ref[...], steps)
            _chunk_carry(st_re, st_im, h0_re, h0_im, a_re, a_im, n_seq, reverse=False)
            st_re[...] = h0_re[...]
            st_im[...] = h0_im[...]

        @pl.when(second)
        def _():
            h_re = buf_re[...].astype(BF16)
            h_im = buf_im[...].astype(BF16)
            hre_ref[...] = h_re
            him_ref[...] = h_im
            for hf in range(2):
                cols = slice(hf * 1024, (hf + 1) * 1024)
                ycols = slice(hf * 256, (hf + 1) * 256)
                y_half = (_dot_nt(h_re[:, cols], cre_ref[hf]) - _dot_nt(h_im[:, cols], cim_ref[hf])
                          + d_ref[:, ycols] * u[:, ycols])
                _split_parts(y_refs[2 * hf:2 * hf + 2], y_half)

    tile = lambda w: pl.BlockSpec((tile_rows, w), lambda p, i: (i, 0))
    out_tile = lambda w: pl.BlockSpec((tile_rows, w), lambda p, i: (i * p, 0))
    cm = _full(_CM_SHAPE)
    outs = _call(
        body, name="s5_scan_fwd", grid=(2, n_tiles),
        in_specs=[tile(LANES)] * _SCAN_PARTS + [cm, cm, cm, cm, _full((1, SSM_LANES)), _full((1, SSM_LANES)),
                                                _full((1, 512))],
        out_specs=[out_tile(LANES)] * _SCAN_PARTS + [out_tile(SSM_LANES), out_tile(SSM_LANES)],
        out_shape=_part_shapes(rows) + [_sds((rows, SSM_LANES), BF16), _sds((rows, SSM_LANES), BF16)],
        scratch_shapes=[pltpu.VMEM((slab, SSM_LANES), F32)] * 4 + [pltpu.VMEM((tile_rows, SSM_LANES), F32)] * 2,
        compiler_params=_params(40, ("arbitrary", "arbitrary")),
    )(*u_parts, bt_re, bt_im, cm_re, cm_im, lbar_re, lbar_im, d_row)
    return outs[:_SCAN_PARTS], outs[_SCAN_PARTS], outs[_SCAN_PARTS + 1]


def _s5_scan_bwd(dy_parts, u_parts, h_re, h_im, bt_re, bt_im, cm_re, cm_im, lbar_re, lbar_im, d_row, n_seq, seq):
    slab, steps, tile_rows, n_tiles = _scan_geometry(n_seq, seq)
    rows = u_parts[0].shape[0]

    def body(*refs):
        dy_refs, u_refs, refs = refs[:_SCAN_PARTS], refs[_SCAN_PARTS:2 * _SCAN_PARTS], refs[2 * _SCAN_PARTS:]
        (hre_ref, him_ref, bre_ref, bim_ref, cre_ref, cim_ref, lre_ref, lim_ref, d_ref), refs = refs[:9], refs[9:]
        du_refs, refs = refs[:_SCAN_PARTS], refs[_SCAN_PARTS:]
        (dbre_ref, dbim_ref, dcre_ref, dcim_ref, dlre_ref, dlim_ref, dd_ref,
         st_re, st_im, g0_re, g0_im, acc_re, acc_im, buf_re, buf_im) = refs
        second = pl.program_id(0) == 1
        i = pl.program_id(1)

        @pl.when(jnp.logical_and(i == 0, jnp.logical_not(second)))
        def _():
            st_re[...] = jnp.zeros_like(st_re)
            st_im[...] = jnp.zeros_like(st_im)
            acc_re[...] = jnp.zeros_like(acc_re)
            acc_im[...] = jnp.zeros_like(acc_im)
            for ref in (dbre_ref, dbim_ref, dcre_ref, dcim_ref, dd_ref):
                ref[...] = jnp.zeros_like(ref)

        dy = _join_parts(dy_refs)
        dyb = dy.astype(BF16)
        for hf in range(2):
            cols = slice(hf * 1024, (hf + 1) * 1024)
            buf_re[:, cols] = _dot(dyb[:, hf * 256:(hf + 1) * 256], cre_ref[hf])
            buf_im[:, cols] = -_dot(dyb[:, hf * 256:(hf + 1) * 256], cim_ref[hf])

        for lc in range(SSM_LANES // SCAN_LANE_CHUNK):
            cols = slice(lc * SCAN_LANE_CHUNK, (lc + 1) * SCAN_LANE_CHUNK)
            l_re = jnp.broadcast_to(lre_ref[:, cols], (slab, SCAN_LANE_CHUNK))
            l_im = jnp.broadcast_to(lim_ref[:, cols], (slab, SCAN_LANE_CHUNK))

            def advance(r0, s_re, s_im):
                n_re = l_re * s_re + l_im * s_im + buf_re[pl.ds(r0, slab), cols]
                n_im = l_re * s_im - l_im * s_re + buf_im[pl.ds(r0, slab), cols]
                buf_re[pl.ds(r0, slab), cols] = n_re
                buf_im[pl.ds(r0, slab), cols] = n_im
                return n_re, n_im

            def row0(k):
                return pl.multiple_of((SCAN_TILE_STEPS - 1 - k) * slab, slab)

            @pl.when(jnp.logical_not(second))
            def _():
                s_re, s_im = lax.fori_loop(0, SCAN_TILE_STEPS, lambda k, s: advance(row0(k), *s),
                                           (st_re[:, cols], st_im[:, cols]), unroll=True)
                st_re[:, cols] = s_re
                st_im[:, cols] = s_im

            @pl.when(second)
            def _():
                def step(k, carry):
                    s_re, s_im, a_re, a_im = carry
                    r0 = row0(k)
                    hr = hre_ref[pl.ds(r0, slab), cols].astype(F32)
                    hi = him_ref[pl.ds(r0, slab), cols].astype(F32)
                    a_re = a_re + s_re * hr + s_im * hi
                    a_im = a_im + s_im * hr - s_re * hi
                    return advance(r0, s_re, s_im) + (a_re, a_im)

                zero = jnp.zeros((slab, SCAN_LANE_CHUNK), F32)
                s_re, s_im, a_re, a_im = lax.fori_loop(
                    0, SCAN_TILE_STEPS, step, (st_re[:, cols], st_im[:, cols], zero, zero), unroll=True)
                st_re[:, cols] = s_re
                st_im[:, cols] = s_im
                acc_re[:, cols] += a_re
                acc_im[:, cols] += a_im

        @pl.when(jnp.logical_and(i == n_tiles - 1, jnp.logical_not(second)))
        def _():
            p_re, p_im = _complex_power(lre_ref[...], lim_ref[...], steps)
            _chunk_carry(st_re, st_im, g0_re, g0_im, p_re, -p_im, n_seq, reverse=True)
            st_re[...] = g0_re[...]
            st_im[...] = g0_im[...]

        @pl.when(second)
        def _():
            u = _join_parts(u_refs)
            ub = u.astype(BF16)
            g_re = buf_re[...].astype(BF16)
            g_im = buf_im[...].astype(BF16)
            dd_ref[...] += jnp.sum(dy * u, axis=0, keepdims=True)
            for hf in range(2):
                cols = slice(hf * 1024, (hf + 1) * 1024)
                ycols = slice(hf * 256, (hf + 1) * 256)
                du_half = (_dot_nt(g_re[:, cols], bre_ref[hf]) + _dot_nt(g_im[:, cols], bim_ref[hf])
                           + d_ref[:, ycols] * dy[:, ycols])
                _split_parts(du_refs[2 * hf:2 * hf + 2], du_half)
                for q4 in range(_HALF_GROUPS // 4):
                    ch = slice(hf * 256 + q4 * 64, hf * 256 + (q4 + 1) * 64)
                    st = slice(hf * 1024 + q4 * 256, hf * 1024 + (q4 + 1) * 256)
                    blk = (hf, slice(q4 * 64, (q4 + 1) * 64), slice(q4 * 256, (q4 + 1) * 256))
                    dbre_ref[blk] += _dot_tn(ub[:, ch], g_re[:, st])
                    dbim_ref[blk] += _dot_tn(ub[:, ch], g_im[:, st])
                    dcre_ref[blk] += _dot_tn(dyb[:, ch], hre_ref[:, st])
                    dcim_ref[blk] -= _dot_tn(dyb[:, ch], him_ref[:, st])

        @pl.when(jnp.logical_and(i == n_tiles - 1, second))
        def _():
            dlre_ref[...] = jnp.sum(acc_re[...], axis=0, keepdims=True)
            dlim_ref[...] = jnp.sum(acc_im[...], axis=0, keepdims=True)

    tile = lambda w: pl.BlockSpec((tile_rows, w), lambda p, i: (n_tiles - 1 - i, 0))
    second_tile = lambda w: pl.BlockSpec((tile_rows, w), lambda p, i: (n_tiles - 1 - i * p, 0))
    cm = _full(_CM_SHAPE)
    row = _full((1, SSM_LANES))
    outs = _call(
        body, name="s5_scan_bwd", grid=(2, n_tiles),
        in_specs=[tile(LANES)] * _SCAN_PARTS + [second_tile(LANES)] * _SCAN_PARTS
        + [second_tile(SSM_LANES), second_tile(SSM_LANES), cm, cm, cm, cm, row, row, _full((1, 512))],
        out_specs=[second_tile(LANES)] * _SCAN_PARTS + [cm, cm, cm, cm, row, row, _full((1, 512))],
        out_shape=(_part_shapes(rows) + [_sds(_CM_SHAPE)] * 4 + [_sds((1, SSM_LANES))] * 2 + [_sds((1, 512))]),
        scratch_shapes=[pltpu.VMEM((slab, SSM_LANES), F32)] * 6 + [pltpu.VMEM((tile_rows, SSM_LANES), F32)] * 2,
        compiler_params=_params(48, ("arbitrary", "arbitrary")),
    )(*dy_parts, *u_parts, h_re, h_im, bt_re, bt_im, cm_re, cm_im, lbar_re, lbar_im, d_row)
    return (outs[:_SCAN_PARTS],) + tuple(outs[_SCAN_PARTS:])


def _glu_gate(gl, a, zs):
    return gl * jax.nn.sigmoid(a) * _silu(zs)


def _glu_fwd(y_parts, zs, w_glu, b_glu, n_seq, seq):
    rows = zs.shape[0]
    tm = 512
    slab, steps, _, _ = _scan_geometry(n_seq, seq)

    def body(*refs):
        y_refs, (zs_ref, w_ref, b_ref, o_ref) = refs[:_SCAN_PARTS], refs[_SCAN_PARTS:]
        y = _load_chunks(y_refs, pl.program_id(0) * (tm // steps), tm // steps, steps, slab)
        gl = jax.nn.gelu(y)
        a = _dot(gl.astype(BF16), w_ref[...]) + b_ref[...]
        o_ref[...] = _glu_gate(gl, a, zs_ref[...]).astype(BF16)

    return _call(
        body, name="glu_fwd", grid=(rows // tm,),
        in_specs=_whole_parts(rows) + [_rows(tm, 512), _full((512, 512)), _full((1, 512))],
        out_specs=_rows(tm, 512), out_shape=_sds((rows, 512), BF16),
        compiler_params=_params(32, ("arbitrary",)),
    )(*y_parts, zs, w_glu, b_glu)


def _glu_bwd(y_parts, zs, d_out, w_glu, b_glu, n_seq, seq):
    rows = zs.shape[0]
    tm = 512
    slab, steps, _, _ = _scan_geometry(n_seq, seq)

    def body(*refs):
        y_refs, (zs_ref, d_ref, w_ref, b_ref), refs = refs[:_SCAN_PARTS], refs[_SCAN_PARTS:_SCAN_PARTS + 4], refs[_SCAN_PARTS + 4:]
        dy_refs, (dzs_ref, dw_ref, db_ref) = refs[:_SCAN_PARTS], refs[_SCAN_PARTS:]
        first_chunk = pl.program_id(0) * (tm // steps)

        @pl.when(pl.program_id(0) == 0)
        def _():
            dw_ref[...] = jnp.zeros_like(dw_ref)
            db_ref[...] = jnp.zeros_like(db_ref)

        gl, gelu_vjp = jax.vjp(jax.nn.gelu, _load_chunks(y_refs, first_chunk, tm // steps, steps, slab))
        glb = gl.astype(BF16)
        a = _dot(glb, w_ref[...]) + b_ref[...]
        _, gate_vjp = jax.vjp(_glu_gate, gl, a, zs_ref[...])
        d_gl, d_a, d_zs = gate_vjp(d_ref[...])
        dab = d_a.astype(BF16)
        d_gl = d_gl + _dot_nt(dab, w_ref[...])
        _store_chunks(dy_refs, first_chunk, gelu_vjp(d_gl)[0], steps, slab)
        dzs_ref[...] = d_zs.astype(BF16)
        dw_ref[...] += _dot_tn(glb, dab)
        db_ref[...] += jnp.sum(d_a, axis=0, keepdims=True)

    *dy_parts, dzs, dw, db = _call(
        body, name="glu_bwd", grid=(rows // tm,),
        in_specs=_whole_parts(rows) + [_rows(tm, 512), _rows(tm, 512), _full((512, 512)), _full((1, 512))],
        out_specs=_whole_parts(rows) + [_rows(tm, 512), _full((512, 512)), _full((1, 512))],
        out_shape=_part_shapes(rows) + [_sds((rows, 512), BF16), _sds((512, 512)), _sds((1, 512))],
        compiler_params=_params(40, ("arbitrary",)),
    )(*y_parts, zs, d_out, w_glu, b_glu)
    return dy_parts, dzs, dw, db


_GROUP_ROWS = Q_PER_KV * BLOCK
_BLOCK_SHIFT = BLOCK.bit_length() - 1


def _attn_bias(j):
    query = _iota((BLOCK, _GROUP_ROWS), 1)
    dist_cur = (query & (BLOCK - 1)) - _iota((BLOCK, _GROUP_ROWS), 0)
    dist_prev = dist_cur + BLOCK
    head = query >> _BLOCK_SHIFT
    slope = jnp.zeros((BLOCK, _GROUP_ROWS), F32)
    for g in range(Q_PER_KV):
        slope = jnp.where(head == g, 2.0 ** (-(j * Q_PER_KV + g + 1)), slope)
    bias_cur = jnp.where(dist_cur >= 0, -slope * dist_cur.astype(F32), -jnp.inf)
    bias_prev = jnp.where(dist_prev < WINDOW, -slope * dist_prev.astype(F32), -jnp.inf)
    return bias_cur, bias_prev


_ATTN_BIAS_SCRATCH = pltpu.VMEM((KV_HEADS, 2, BLOCK, _GROUP_ROWS), F32)


def _fill_attn_bias(bias_ref):
    @pl.when(jnp.logical_and(pl.program_id(0) == 0, pl.program_id(1) == 0))
    def _():
        for j in range(KV_HEADS):
            bias_ref[j, 0], bias_ref[j, 1] = _attn_bias(j)


def _stack_heads(x, j):
    heads = range(j * Q_PER_KV, (j + 1) * Q_PER_KV)
    return jnp.concatenate([x[:, h * HEAD_DIM:(h + 1) * HEAD_DIM] for h in heads], axis=0)


def _head_rows(x, j):
    heads = range(j * Q_PER_KV, (j + 1) * Q_PER_KV)
    return jnp.concatenate([x[h:h + 1, :] for h in heads], axis=1)


def _sink_row(sk_ref, j):
    heads = range(j * Q_PER_KV, (j + 1) * Q_PER_KV)
    return jnp.concatenate([jnp.broadcast_to(sk_ref[0:1, h:h + 1], (1, BLOCK)) for h in heads], axis=1)


def _attn_fwd(q, k, v, za, sinks, n_seq, seq):
    nb = seq // BLOCK
    rows = q.shape[0]

    def body(q_ref, kc_ref, kp_ref, vc_ref, vp_ref, za_ref, sk_ref, o_ref, ao_ref, lse_ref, bias_ref):
        _fill_attn_bias(bias_ref)
        has_prev = pl.program_id(1) > 0
        q_all = q_ref[...]
        for j in range(KV_HEADS):
            js = slice(j * HEAD_DIM, (j + 1) * HEAD_DIM)
            bias_c, bias_p = bias_ref[j, 0], bias_ref[j, 1]
            q4 = _stack_heads(q_all, j)
            sc = _dot_nt(kc_ref[:, js], q4) + bias_c
            sp = _dot_nt(kp_ref[:, js], q4) + jnp.where(has_prev, bias_p, -jnp.inf)
            sink = _sink_row(sk_ref, j)
            m = jnp.maximum(jnp.max(jnp.maximum(sc, sp), axis=0, keepdims=True), sink)
            ec = jnp.exp(sc - m)
            ep = jnp.exp(sp - m)
            den = jnp.sum(ec + ep, axis=0, keepdims=True) + jnp.exp(sink - m)
            inv = 1.0 / den
            o4 = _dot_tn((ec * inv).astype(BF16), vc_ref[:, js]) + _dot_tn((ep * inv).astype(BF16), vp_ref[:, js])
            lse4 = m + jnp.log(den)
            for g in range(Q_PER_KV):
                h = j * Q_PER_KV + g
                o_ref[:, h * HEAD_DIM:(h + 1) * HEAD_DIM] = o4[g * BLOCK:(g + 1) * BLOCK]
                lse_ref[h:h + 1, :] = lse4[:, g * BLOCK:(g + 1) * BLOCK]
        ao_ref[...] = (o_ref[...] * _silu(za_ref[...])).astype(BF16)

    cur = lambda w: pl.BlockSpec((BLOCK, w), lambda b, n: (b * nb + n, 0))
    prev = lambda w: pl.BlockSpec((BLOCK, w), lambda b, n: (b * nb + jnp.maximum(n - 1, 0), 0))
    lse_rows = rows // BLOCK * N_HEADS
    return _call(
        body, name="attn_fwd", grid=(n_seq, nb),
        in_specs=[cur(512), cur(128), prev(128), cur(128), prev(128), cur(512), _full((1, N_HEADS))],
        out_specs=[cur(512), cur(512), pl.BlockSpec((N_HEADS, BLOCK), lambda b, n: (b * nb + n, 0))],
        out_shape=[_sds((rows, 512)), _sds((rows, 512), BF16), _sds((lse_rows, BLOCK))],
        scratch_shapes=[_ATTN_BIAS_SCRATCH], compiler_params=_params(32, ("arbitrary", "arbitrary")),
    )(q, k, k, v, v, za, sinks)


def _attn_bwd(q, k, v, za, o, lse, d_ao, sinks, n_seq, seq):
    nb = seq // BLOCK
    rows = q.shape[0]

    def body(q_ref, kc_ref, kp_ref, vc_ref, vp_ref, za_ref, o_ref, lse_ref, d_ref, sk_ref,
             dq_ref, dk_ref, dv_ref, dza_ref, dsk_ref, bias_ref, dk_carry, dv_carry):
        n = nb - 1 - pl.program_id(1)
        _fill_attn_bias(bias_ref)

        @pl.when(jnp.logical_and(pl.program_id(0) == 0, pl.program_id(1) == 0))
        def _():
            dsk_ref[...] = jnp.zeros_like(dsk_ref)
            dk_carry[...] = jnp.zeros_like(dk_carry)
            dv_carry[...] = jnp.zeros_like(dv_carry)

        has_prev = n > 0
        has_next = n + 1 < nb

        _, gate_vjp = jax.vjp(lambda o_, z_: o_ * _silu(z_), o_ref[...], za_ref[...])
        d_o, d_za = gate_vjp(d_ref[...])
        dza_ref[...] = d_za.astype(BF16)
        q_all = q_ref[...]
        lse_all = lse_ref[...]

        for j in range(KV_HEADS):
            js = slice(j * HEAD_DIM, (j + 1) * HEAD_DIM)
            kc, kp, vc, vp = kc_ref[:, js], kp_ref[:, js], vc_ref[:, js], vp_ref[:, js]
            bias_c, bias_p = bias_ref[j, 0], bias_ref[j, 1]
            q4 = _stack_heads(q_all, j)
            do4b = _stack_heads(d_o, j).astype(BF16)
            lse4 = _head_rows(lse_all, j)
            pc = jnp.exp(_dot_nt(kc, q4) + bias_c - lse4)
            pp = jnp.exp(_dot_nt(kp, q4) + jnp.where(has_prev, bias_p, -jnp.inf) - lse4)
            dpc = _dot_nt(vc, do4b)
            dpp = _dot_nt(vp, do4b)
            delta = jnp.sum(pc * dpc + pp * dpp, axis=0, keepdims=True)
            dsc = (pc * (dpc - delta)).astype(BF16)
            dsp = (pp * (dpp - delta)).astype(BF16)
            dq4 = ((_dot_tn(dsc, kc) + _dot_tn(dsp, kp)) * ATTN_SCALE).astype(BF16)
            sink_loss = jnp.exp(_sink_row(sk_ref, j) - lse4) * delta
            for g in range(Q_PER_KV):
                h = j * Q_PER_KV + g
                dq_ref[:, h * HEAD_DIM:(h + 1) * HEAD_DIM] = dq4[g * BLOCK:(g + 1) * BLOCK]
                dsk_ref[0:1, h:h + 1] -= jnp.sum(sink_loss[:, g * BLOCK:(g + 1) * BLOCK], axis=1, keepdims=True)
            dk = _dot(dsc, q4) + jnp.where(has_next, dk_carry[j], 0.0)
            dv = _dot(pc.astype(BF16), do4b) + jnp.where(has_next, dv_carry[j], 0.0)
            dk_carry[j] = _dot(dsp, q4)
            dv_carry[j] = _dot(pp.astype(BF16), do4b)
            dk_ref[:, js] = dk.astype(BF16)
            dv_ref[:, js] = dv.astype(BF16)

    cur = lambda w: pl.BlockSpec((BLOCK, w), lambda b, s: (b * nb + nb - 1 - s, 0))
    prev = lambda w: pl.BlockSpec((BLOCK, w), lambda b, s: (b * nb + jnp.maximum(nb - 2 - s, 0), 0))
    return _call(
        body, name="attn_bwd", grid=(n_seq, nb),
        in_specs=[cur(512), cur(128), prev(128), cur(128), prev(128), cur(512), cur(512),
                  pl.BlockSpec((N_HEADS, BLOCK), lambda b, s: (b * nb + nb - 1 - s, 0)), cur(512), _full((1, N_HEADS))],
        out_specs=[cur(512), cur(128), cur(128), cur(512), _full((1, N_HEADS))],
        out_shape=[_sds((rows, 512), BF16), _sds((rows, 128), BF16), _sds((rows, 128), BF16),
                   _sds((rows, 512), BF16), _sds((1, N_HEADS))],
        scratch_shapes=[_ATTN_BIAS_SCRATCH, pltpu.VMEM((KV_HEADS, BLOCK, HEAD_DIM), F32),
                        pltpu.VMEM((KV_HEADS, BLOCK, HEAD_DIM), F32)],
        compiler_params=_params(32, ("arbitrary", "arbitrary")),
    )(q, k, k, v, v, za, o, lse, d_ao, sinks)


def _tail(ssm_out, attn_out, x2d, p2d, target, w_out, g2, w_gate, b_gate, w_proj):
    rows = x2d.shape[0]
    tm = 512

    def body(so_ref, ao_ref, x_ref, p_ref, t_ref, wo_ref, g2_ref, wg_ref, bg_ref, wp_ref,
             dh1_ref, dso_ref, dao_ref, dwo_ref, dwg_ref, dwp_ref, dbg_ref, dg2_ref, loss_ref):
        @pl.when(pl.program_id(0) == 0)
        def _():
            for ref in (dwo_ref, dwg_ref, dwp_ref, dbg_ref, dg2_ref, loss_ref):
                ref[...] = jnp.zeros_like(ref)

        cat = jnp.concatenate([so_ref[...], ao_ref[...]], axis=1)
        g2 = g2_ref[...]
        mixed = _dot(cat, wo_ref[...])
        r = lax.rsqrt(jnp.mean(mixed * mixed, axis=-1, keepdims=True) + EPS)
        mr = mixed * r
        h1 = x_ref[...] + mr * g2
        h1b = h1.astype(BF16)
        gate = jax.nn.sigmoid(_dot(h1b, wg_ref[...]) + bg_ref[...])
        pb = p_ref[...].astype(BF16)
        wp_blocks = [slice(j * D_PLE, (j + 1) * D_PLE) for j in range(N_CHIPS)]
        pp = jnp.concatenate([_dot(pb, wp_ref[blk, :]) for blk in wp_blocks], axis=1)
        err = h1 + gate * pp - t_ref[...]
        loss_ref[...] += 0.5 * jnp.sum(jnp.mean(err * err, axis=-1, keepdims=True), axis=0, keepdims=True)

        dh2 = err * (1.0 / D_MODEL)
        d_glin = dh2 * pp * gate * (1.0 - gate)
        d_glin_b = d_glin.astype(BF16)
        dwg_ref[...] += _dot_tn(h1b, d_glin_b)
        dbg_ref[...] += jnp.sum(d_glin, axis=0, keepdims=True)
        d_pp = (dh2 * gate).astype(BF16)
        for blk in wp_blocks:
            dwp_ref[blk, :] += _dot_tn(pb, d_pp[:, blk])
        dh1 = dh2 + _dot_nt(d_glin_b, wg_ref[...])
        dh1_ref[...] = dh1
        dg2_ref[...] += jnp.sum(dh1 * mr, axis=0, keepdims=True)
        a_ = dh1 * g2
        d_mixed = (r * a_ - mr * (r * jnp.mean(a_ * mr, axis=-1, keepdims=True))).astype(BF16)
        dwo_ref[...] += _dot_tn(cat, d_mixed)
        d_cat = _dot_nt(d_mixed, wo_ref[...])
        dso_ref[...] = d_cat[:, 0:512]
        dao_ref[...] = d_cat[:, 512:1024]

    return _call(
        body, name="tail_fwd_bwd", grid=(rows // tm,),
        in_specs=[_rows(tm, 512), _rows(tm, 512), _rows(tm, D_MODEL), _rows(tm, D_PLE), _rows(tm, D_MODEL),
                  _full((D_MODEL, D_MODEL)), _full((1, D_MODEL)), _full((D_MODEL, D_MODEL)), _full((1, D_MODEL)),
                  _full((N_CHIPS * D_PLE, D_PLE))],
        out_specs=[_rows(tm, D_MODEL), _rows(tm, 512), _rows(tm, 512), _full((D_MODEL, D_MODEL)),
                   _full((D_MODEL, D_MODEL)), _full((N_CHIPS * D_PLE, D_PLE)), _full((1, D_MODEL)), _full((1, D_MODEL)),
                   _full((1, 1))],
        out_shape=[_sds((rows, D_MODEL)), _sds((rows, 512)), _sds((rows, 512)), _sds((D_MODEL, D_MODEL)),
                   _sds((D_MODEL, D_MODEL)), _sds((N_CHIPS * D_PLE, D_PLE)), _sds((1, D_MODEL)), _sds((1, D_MODEL)),
                   _sds((1, 1))],
        compiler_params=_params(52, ("arbitrary",)),
    )(ssm_out, attn_out, x2d, p2d, target, w_out, g2, w_gate, b_gate, w_proj)


def _local_step(x, hn, p, target, pre_norm_g, w_in_t, s5_params, s5_operands, ssm_d, w_glu, b_glu, sinks, w_out,
                post_norm_g, w_proj, w_gate, b_gate, send_tail_grads=lambda ready: ready["w_out"],
                send_bc=lambda d_bc: (d_bc, d_bc)):
    n_seq, seq, _ = x.shape
    rows = n_seq * seq
    x2d = x.reshape(rows, D_MODEL)
    p2d = p.reshape(rows, D_PLE)
    t2d = target.reshape(rows, D_MODEL)

    l_re, l_im, bt_re, bt_im, cm_re, cm_im = s5_operands

    u_scan, zs, q, k, v, za = _in_proj(hn, w_in_t, n_seq, seq)
    y_scan, h_re, h_im = _s5_scan_fwd(u_scan, bt_re, bt_im, cm_re, cm_im, l_re, l_im, ssm_d, n_seq, seq)
    ssm_out = _glu_fwd(y_scan, zs, w_glu, b_glu, n_seq, seq)
    o, attn_out, lse = _attn_fwd(q, k, v, za, sinks, n_seq, seq)

    dh1, d_so, d_ao, d_w_out, d_w_gate, d_w_proj, d_b_gate, d_g2, loss = _tail(
        ssm_out, attn_out, x2d, p2d, t2d, w_out, post_norm_g, w_gate, b_gate, w_proj)

    dq, dk, dv, dza, d_sinks = _attn_bwd(q, k, v, za, o, lse, d_ao, sinks, n_seq, seq)
    dy_scan, dzs, d_w_glu, d_b_glu = _glu_bwd(y_scan, zs, d_so, w_glu, b_glu, n_seq, seq)
    du_scan, d_bt_re, d_bt_im, d_cm_re, d_cm_im, d_l_re, d_l_im, d_d = _s5_scan_bwd(
        dy_scan, u_scan, h_re, h_im, bt_re, bt_im, cm_re, cm_im, l_re, l_im, ssm_d, n_seq, seq)
    tail_grads_arrived = send_tail_grads(dict(w_out=d_w_out, pl_w_gate=d_w_gate, pl_w_proj=d_w_proj))
    d_lam_re, d_lam_im, d_log_step, d_bc = _s5_params_bwd(
        s5_params, (d_l_re, d_l_im, d_bt_re, d_bt_im, d_cm_re, d_cm_im), tail_grads_arrived)

    sent, arrived = send_bc(d_bc)
    d_proj, d_w_in_early, d_w_in_early_b = _in_proj_bwd_early(hn, du_scan, dzs, dq, dk, dv, dza, sent, n_seq, seq)
    grad_x, d_w_in_late, d_g1 = _in_proj_bwd(x2d, dh1, pre_norm_g, w_in_t, d_proj, arrived)
    grads = dict(
        pre_norm_g=d_g1, w_in_early=d_w_in_early, w_in_early_bf16=d_w_in_early_b, w_in_late=d_w_in_late,
        ssm_lam_re=d_lam_re, ssm_lam_im=d_lam_im, ssm_log_step=d_log_step, ssm_bc=d_bc, ssm_d=d_d, ssm_w_glu=d_w_glu,
        ssm_b_glu=d_b_glu, attn_sinks=d_sinks, w_out=d_w_out, post_norm_g=d_g2, pl_w_proj=d_w_proj,
        pl_w_gate=d_w_gate, pl_b_gate=d_b_gate)
    return grad_x.reshape(x.shape), loss, grads


_BIG = ("w_in", "ssm_w_glu", "w_out", "pl_w_proj", "pl_w_gate")
_BIG_SHARD = {"w_in": (D_IN // N_CHIPS, D_MODEL), "ssm_w_glu": (D_SSM // N_CHIPS, D_SSM),
              "w_out": (D_MODEL // N_CHIPS, D_MODEL), "pl_w_proj": (D_PLE, D_MODEL // N_CHIPS),
              "pl_w_gate": (D_MODEL // N_CHIPS, D_MODEL)}
_SMALL = {"pre_norm_g": (1, D_MODEL), "ssm_lam_re": (SSM_GROUPS, SSM_STATE), "ssm_lam_im": (SSM_GROUPS, SSM_STATE),
          "ssm_log_step": (1, SSM_GROUPS), "ssm_b_re": (D_SSM, SSM_STATE), "ssm_b_im": (D_SSM, SSM_STATE),
          "ssm_c_re": (D_SSM, SSM_STATE), "ssm_c_im": (D_SSM, SSM_STATE), "ssm_d": (1, D_SSM), "ssm_b_glu": (1, D_SSM),
          "attn_sinks": (1, N_HEADS), "post_norm_g": (1, D_MODEL), "pl_b_gate": (1, D_MODEL)}
_VEC_ROWS = ("pre_norm_g", "post_norm_g", "pl_b_gate", "ssm_d", "ssm_b_glu", "attn_sinks", "ssm_log_step", "loss")
_SMALL_GROUPS = (
    ("vec", (8, D_MODEL), tuple((name, r) for r, name in enumerate(_VEC_ROWS))),
    ("lam", (2 * SSM_GROUPS, SSM_STATE), (("ssm_lam_re", 0), ("ssm_lam_im", SSM_GROUPS))),
)
_SMALL_EARLY = ("ssm_b_re", "ssm_b_im", "ssm_c_re", "ssm_c_im")
_SMALL_ORDER = tuple(name for _, _, members in _SMALL_GROUPS for name, _ in members) + _SMALL_EARLY
_WEIGHT_ORDER = ("pre_norm_g", "w_in", "ssm_lam_re", "ssm_lam_im", "ssm_log_step", "ssm_b_re", "ssm_b_im", "ssm_c_re",
                 "ssm_c_im", "ssm_d", "ssm_w_glu", "ssm_b_glu", "attn_sinks", "w_out", "post_norm_g", "pl_w_proj",
                 "pl_w_gate", "pl_b_gate")


def _small_shape(name):
    return (1, 1) if name == "loss" else _SMALL[name]


def _to_kernel_form(name, a):
    a = a[0]
    if name == "w_in":
        return a.T
    if name in ("ssm_b_re", "ssm_b_im"):
        a = a.transpose(0, 2, 1)
    return a.reshape(_SMALL[name]) if name in _SMALL else a


def _from_kernel_form(name, a, shape):
    if name == "w_in":
        a = a.T
    if name in ("ssm_b_re", "ssm_b_im"):
        a = a.reshape(SSM_GROUPS, SSM_GROUP_CH, SSM_STATE).transpose(0, 2, 1)
    return a.reshape(shape)


def _mesh_place():
    x, y, c = lax.axis_index("x"), lax.axis_index("y"), lax.axis_index("c")
    other_chips = ((1 - x, y), (x, 1 - y), (1 - x, 1 - y))
    return x, y, c, other_chips


def _gather_copies(s_refs, g_refs, send_sems, recv_sems, local_sems):
    x, y, c, other_chips = _mesh_place()
    started = []
    for i, (s_ref, g_ref) in enumerate(zip(s_refs, g_refs)):
        rows = s_ref.shape[0]
        half = rows // 2

        def block(chip, g_ref=g_ref, rows=rows, half=half):
            return g_ref.at[pl.ds((2 * chip[0] + chip[1]) * rows + c * half, half), :]

        def copy(k, chip, to, src=None, i=i, block=block):
            return pltpu.make_async_remote_copy(
                src_ref=block(chip) if src is None else src, dst_ref=block(chip), send_sem=send_sems.at[6 * i + k],
                recv_sem=recv_sems.at[6 * i + k], device_id=to, device_id_type=MESH)

        own = pltpu.make_async_copy(s_ref, g_ref.at[pl.ds((2 * x + y) * rows, rows), :], local_sems.at[i])
        own.start()
        first = [copy(k, (x, y), (*chip, c), src=s_ref.at[pl.ds(c * half, half), :])
                 for k, chip in enumerate(other_chips)]
        for cp in first:
            cp.start()
        passed = [copy(3 + k, chip, (x, y, 1 - c)) for k, chip in enumerate(other_chips)]
        started.append((own, first, passed))
    for own, first, passed in started:
        for k in range(3):
            first[k].wait_recv()
            passed[k].start()
    for own, first, passed in started:
        for k in range(3):
            passed[k].wait_recv()
        for cp in first + passed:
            cp.wait_send()
        own.wait()


def _gather_semaphores(n_t):
    return [pltpu.SemaphoreType.DMA((6 * n_t,)), pltpu.SemaphoreType.DMA((6 * n_t,)), pltpu.SemaphoreType.DMA((n_t,))]


def _gather_weights_beside(shards, name, collective_id):
    n_t = len(shards)
    hbm = pltpu.MemorySpace.HBM
    s_refs = [jax.new_ref(s, memory_space=hbm) for s in shards]
    g_refs = [jax.empty_ref(jax.ShapeDtypeStruct((N_CHIPS * s.shape[0], s.shape[1]), s.dtype), memory_space=hbm)
              for s in shards]

    def launch(send_sems, recv_sems, local_sems):
        x, y, c, other_chips = _mesh_place()
        peers = [(*chip, c) for chip in other_chips] + [(x, y, 1 - c)]
        barrier = pltpu.get_barrier_semaphore()
        for peer in peers:
            pl.semaphore_signal(barrier, inc=1, device_id=peer, device_id_type=MESH)
        pl.semaphore_wait(barrier, len(peers))
        _gather_copies(s_refs, g_refs, send_sems, recv_sems, local_sems)

    pl.kernel(launch, mesh=plsc.ScalarSubcoreMesh(axis_name="sequencer", num_cores=1), name=name,
              scratch_types=_gather_semaphores(n_t), compiler_params=pltpu.CompilerParams(collective_id=collective_id))()
    return [g[...] for g in g_refs]


_RELATIONS = tuple(((r >> 2) & 1, (r >> 1) & 1, r & 1) for r in range(1, 8))


def _related(place, relation):
    return tuple(1 - a if flip else a for a, flip in zip(place, relation))


def _scatter_beside(mats, name, collective_id):
    hbm = pltpu.MemorySpace.HBM
    src_refs = [jax.new_ref(a, memory_space=hbm) for a in mats]
    land_refs = [jax.empty_ref(jax.ShapeDtypeStruct((7, a.shape[0] // 8, a.shape[1]), a.dtype), memory_space=hbm)
                 for a in mats]

    def launch(send_sems, recv_sems):
        me = (lax.axis_index("x"), lax.axis_index("y"), lax.axis_index("c"))
        peers = [_related(me, rel) for rel in _RELATIONS]
        barrier = pltpu.get_barrier_semaphore()
        for peer in peers:
            pl.semaphore_signal(barrier, inc=1, device_id=peer, device_id_type=MESH)
        pl.semaphore_wait(barrier, len(peers))
        copies = []
        for i, (src, land) in enumerate(zip(src_refs, land_refs)):
            hr = land.shape[1]
            for k, (tx, ty, tc) in enumerate(peers):
                rows = pl.ds((2 * tx + ty) * 2 * hr + tc * hr, hr)
                copies.append(pltpu.make_async_remote_copy(
                    src_ref=src.at[rows, :], dst_ref=land.at[k], send_sem=send_sems.at[7 * i + k],
                    recv_sem=recv_sems.at[7 * i + k], device_id=(tx, ty, tc), device_id_type=MESH))
                copies[-1].start()
        for cp in copies:
            cp.wait()

    n_sems = 7 * len(mats)
    pl.kernel(launch, mesh=plsc.ScalarSubcoreMesh(axis_name="sequencer", num_cores=1), name=name,
              scratch_types=[pltpu.SemaphoreType.DMA((n_sems,)), pltpu.SemaphoreType.DMA((n_sems,))],
              compiler_params=pltpu.CompilerParams(collective_id=collective_id))()
    return [ref[...] for ref in land_refs]


def _broadcast_beside(arrays):
    hbm = pltpu.MemorySpace.HBM
    src_refs = [jax.new_ref(a, memory_space=hbm) for a in arrays]
    land_refs = [jax.empty_ref(jax.ShapeDtypeStruct((len(_RELATIONS),) + a.shape, a.dtype), memory_space=hbm)
                 for a in arrays]

    def launch(send_sems, recv_sems):
        me = (lax.axis_index("x"), lax.axis_index("y"), lax.axis_index("c"))
        peers = [_related(me, rel) for rel in _RELATIONS]
        barrier = pltpu.get_barrier_semaphore()
        for peer in peers:
            pl.semaphore_signal(barrier, inc=1, device_id=peer, device_id_type=MESH)
        pl.semaphore_wait(barrier, len(peers))
        copies = []
        for i, (src, land) in enumerate(zip(src_refs, land_refs)):
            for k, peer in enumerate(peers):
                copies.append(pltpu.make_async_remote_copy(
                    src_ref=src, dst_ref=land.at[k], send_sem=send_sems.at[7 * i + k],
                    recv_sem=recv_sems.at[7 * i + k], device_id=peer, device_id_type=MESH))
                copies[-1].start()
        for cp in copies:
            cp.wait()

    n_sems = 7 * len(arrays)
    pl.kernel(launch, mesh=plsc.ScalarSubcoreMesh(axis_name="sequencer", num_cores=1), name="broadcast_beside",
              scratch_types=[pltpu.SemaphoreType.DMA((n_sems,)), pltpu.SemaphoreType.DMA((n_sems,))],
              compiler_params=pltpu.CompilerParams(collective_id=3))()
    return [ref[...] for ref in land_refs]


def _exchange_grads(big, outputs, small, landed, own_bc, landed_bc):
    n_t = len(big)
    n_g = len(_SMALL_GROUPS)
    names = _SMALL_ORDER
    halves = [(b.shape[0] // N_CHIPS // 2, b.shape[1]) for b in big]
    early = sorted(landed)
    late = [i for i in range(n_t) if i not in landed]
    n_sems = 4 * n_g + 7 * len(late) + n_t
    small_sem0, block_sem0 = n_t, n_t + len(names)
    early_sem0 = block_sem0 + N_CHIPS * len(late)
    landed_sem0 = early_sem0 + 2 * len(early)
    sent = [n for n in names if n in small]

    def body(*refs):
        pos = 0

        def take(n):
            nonlocal pos
            pos += n
            return refs[pos - n:pos]

        big_refs, small_refs = take(n_t), dict(zip(sent, take(len(sent))))
        land_refs = dict(zip(early, take(len(early))))
        own_bc_ref, landed_bc_ref = take(2)
        out_refs, small_out_refs = take(len(outputs)), dict(zip(names, take(len(names))))
        per_late = lambda: dict(zip(late, take(len(late))))
        ga, gb, pme, send_b, recv_b = per_late(), per_late(), take(n_t), per_late(), per_late()
        own_e, land_e = dict(zip(early, take(len(early)))), dict(zip(early, take(len(early))))
        own_s, land_s = take(2)
        s_own, s_sib, s_chips, s_pair = take(n_g), take(n_g), take(n_g), take(n_g)
        stage = dict(zip(names, take(len(names))))
        send_sems, recv_sems, local_sems = take(3)
        x, y, c, other_chips = _mesh_place()
        me = 2 * x + y
        sibling = (x, y, 1 - c)
        sem_at = iter(range(n_sems))

        def remote(src, dst, to):
            k = next(sem_at)
            return pltpu.make_async_remote_copy(src_ref=src, dst_ref=dst, send_sem=send_sems.at[k],
                                                recv_sem=recv_sems.at[k], device_id=to, device_id_type=MESH)

        loads = [pltpu.make_async_copy(small_refs[name], stage[name], local_sems.at[small_sem0 + names.index(name)])
                 for name in sent]
        landed_loads = [pltpu.make_async_copy(own_bc_ref, own_s, local_sems.at[landed_sem0]),
                        pltpu.make_async_copy(landed_bc_ref, land_s, local_sems.at[landed_sem0 + 1])]
        for cp in loads + landed_loads:
            cp.start()
        for cp in loads:
            cp.wait()
        small_swaps = []
        for gi, (_, _, members) in enumerate(_SMALL_GROUPS):
            s_own[gi][...] = jnp.zeros_like(s_own[gi])
            for name, r0 in members:
                r, n = _small_shape(name)
                s_own[gi][r0:r0 + r, 0:n] = stage[name][...]
            small_swaps.append(remote(s_own[gi], s_sib[gi], sibling))
            small_swaps[gi].start()
        order = sorted(late, key=lambda i: halves[i][0] * halves[i][1])
        own_loads, big_swaps = {}, {}
        for i in order:
            hr = halves[i][0]
            own_loads[i], big_swaps[i] = [], []
            for j in range(N_CHIPS):
                mine = big_refs[i].at[pl.ds(j * 2 * hr + c * hr, hr), :]
                theirs = big_refs[i].at[pl.ds(j * 2 * hr + (1 - c) * hr, hr), :]
                sem = local_sems.at[block_sem0 + N_CHIPS * late.index(i) + j]
                own_loads[i].append(pltpu.make_async_copy(mine, ga[i].at[j], sem))
                own_loads[i][j].start()
                big_swaps[i].append(remote(theirs, gb[i].at[j], sibling))
                big_swaps[i][j].start()
        early_loads = {}
        for e, i in enumerate(early):
            hr = halves[i][0]
            mine = big_refs[i].at[pl.ds(me * 2 * hr + c * hr, hr), :]
            early_loads[i] = [pltpu.make_async_copy(mine, own_e[i], local_sems.at[early_sem0 + 2 * e]),
                              pltpu.make_async_copy(land_refs[i], land_e[i], local_sems.at[early_sem0 + 2 * e + 1])]
            for cp in early_loads[i]:
                cp.start()
        small_sends = []
        for gi in range(n_g):
            small_swaps[gi].wait_recv()
            s_pair[gi][...] = s_own[gi][...] + s_sib[gi][...]
            small_sends.append([remote(s_pair[gi], s_chips[gi].at[k], (*chip, c)) for k, chip in enumerate(other_chips)])
            for cp in small_sends[gi]:
                cp.start()

        def pair_sum(i, j):
            return ga[i][j] + gb[i][j]

        big_sends = {}
        for i in order:
            for j in range(N_CHIPS):
                own_loads[i][j].wait()
                big_swaps[i][j].wait_recv()
            big_sends[i] = []
            for k, chip in enumerate(other_chips):
                send_b[i][k] = pair_sum(i, 2 * chip[0] + chip[1]).astype(BF16)
                big_sends[i].append(remote(send_b[i].at[k], recv_b[i].at[k], (*chip, c)))
                big_sends[i][k].start()
        last_swaps, keeps = {}, {}
        for i in early + order:
            hr = halves[i][0]
            if i in landed:
                for cp in early_loads[i]:
                    cp.wait()
                total = own_e[i][...]
                for k in range(len(_RELATIONS)):
                    total = total + land_e[i][k].astype(F32)
                pme[i][...] = total
            else:
                for k in range(3):
                    big_sends[i][k].wait_recv()
                pme[i][...] = ((pair_sum(i, me) + recv_b[i][0].astype(F32)) + recv_b[i][1].astype(F32)) + recv_b[i][2].astype(F32)
            o, = [o for o, group in enumerate(outputs) if i in group]
            first_col = sum(halves[j][1] for j in outputs[o][:outputs[o].index(i)])
            mine = out_refs[o].at[pl.ds(c * hr, hr), pl.ds(first_col, halves[i][1])]
            keeps[i] = pltpu.make_async_copy(pme[i], mine, local_sems.at[i])
            keeps[i].start()
            last_swaps[i] = remote(pme[i], mine, sibling)
            last_swaps[i].start()

        for gi, (_, _, members) in enumerate(_SMALL_GROUPS):
            for k in range(3):
                small_sends[gi][k].wait_recv()
            total = None
            for j in range(N_CHIPS):
                rel = jnp.bitwise_xor(j, me)
                term = jnp.where(rel == 0, s_pair[gi][...], jnp.where(
                    rel == 2, s_chips[gi][0], jnp.where(rel == 1, s_chips[gi][1], s_chips[gi][2])))
                total = term if total is None else total + term
            s_sib[gi][...] = total
            for name, r0 in members:
                r, n = _small_shape(name)
                stage[name][...] = s_sib[gi][r0:r0 + r, 0:n]
        my_index = 4 * x + 2 * y + c
        for cp in landed_loads:
            cp.wait()
        total = None
        for d in range(2 * N_CHIPS):
            rel = jnp.bitwise_xor(d, my_index)
            term = own_s[...]
            for k in range(len(_RELATIONS)):
                term = jnp.where(rel == k + 1, land_s[k], term)
            total = term.astype(F32) if total is None else total + term.astype(F32)
        for a, name in enumerate(_SMALL_EARLY):
            stage[name][...] = total[:, a * SSM_STATE:(a + 1) * SSM_STATE]
        stores = [pltpu.make_async_copy(stage[name], small_out_refs[name], local_sems.at[small_sem0 + a])
                  for a, name in enumerate(names)]
        for cp in stores:
            cp.start()

        for i in range(n_t):
            last_swaps[i].wait_recv()
            keeps[i].wait()
        for cp in stores:
            cp.wait()
        groups = list(big_swaps.values()) + small_sends + list(big_sends.values())
        for cp in small_swaps + [cp for group in groups for cp in group] + list(last_swaps.values()):
            cp.wait_send()

    any_spec = pl.BlockSpec(memory_space=pl.ANY)
    small_shapes = [_sds(_small_shape(n)) for n in names]
    group_shapes = [shape for _, shape, _ in _SMALL_GROUPS]
    vmem = lambda which, dtype, lead=(): [pltpu.VMEM(lead + halves[i], dtype) for i in which]
    outs = _call(
        body, name="exchange_grads",
        in_specs=[any_spec] * (n_t + len(sent) + len(early) + 2),
        out_specs=[any_spec] * (len(outputs) + len(names)),
        out_shape=[_sds((big[group[0]].shape[0] // N_CHIPS, sum(big[i].shape[1] for i in group))) for group in outputs]
        + small_shapes,
        scratch_shapes=(vmem(late, F32, (N_CHIPS,)) + vmem(late, F32, (N_CHIPS,)) + vmem(range(n_t), F32)
                        + vmem(late, BF16, (3,)) + vmem(late, BF16, (3,))
                        + vmem(early, F32)
                        + [pltpu.VMEM((len(_RELATIONS),) + halves[i], landed[i].dtype) for i in early]
                        + [pltpu.VMEM(own_bc.shape, own_bc.dtype), pltpu.VMEM(landed_bc.shape, landed_bc.dtype)]
                        + [pltpu.VMEM(s, F32) for s in group_shapes] * 2 + [pltpu.VMEM((3,) + s, F32) for s in group_shapes]
                        + [pltpu.VMEM(s, F32) for s in group_shapes]
                        + [pltpu.VMEM(_small_shape(n), F32) for n in names]
                        + [pltpu.SemaphoreType.DMA((n_sems,)), pltpu.SemaphoreType.DMA((n_sems,)),
                           pltpu.SemaphoreType.DMA((landed_sem0 + 2,))]),
        compiler_params=_params(48),
    )(*big, *[small[n] for n in sent], *[landed[i] for i in early], own_bc, landed_bc)
    return list(outs[:len(outputs)]), dict(zip(names, outs[len(outputs):]))


def _adamw_update(w, g, m, v):
    m = ADAM_B1 * m + (1.0 - ADAM_B1) * g
    v = ADAM_B2 * v + (1.0 - ADAM_B2) * (g * g)
    m_hat = m / (1.0 - ADAM_B1 ** ADAM_STEP)
    v_hat = v / (1.0 - ADAM_B2 ** ADAM_STEP)
    return -ADAM_LR * (m_hat / (jnp.sqrt(v_hat) + ADAM_EPS) + ADAM_WD * w), m, v


def _adamw(w, g, m, v, grid, name):
    n_t = len(w)

    def body(*refs):
        ins, outs = refs[:4 * n_t], refs[4 * n_t:]
        for i in range(n_t):
            w_, g_, m_, v_ = [ins[a * n_t + i][...] for a in range(4)]
            vals = (g_,) + _adamw_update(w_, g_, m_, v_)
            for a in range(4):
                outs[a * n_t + i][...] = vals[a]

    specs = [pl.BlockSpec((a.shape[0] // grid, a.shape[1]), lambda i: (i, 0)) for a in w]
    shapes = [_sds(a.shape) for a in w]
    outs = _call(
        body, name=name, grid=(grid,), in_specs=specs * 4, out_specs=specs * 4, out_shape=shapes * 4,
        compiler_params=_params(40, ("arbitrary",)),
    )(*w, *g, *m, *v)
    return [outs[a * n_t:(a + 1) * n_t] for a in range(4)]


def kernel(x, p, pre_norm_g, w_in, ssm_lam_re, ssm_lam_im, ssm_log_step, ssm_b_re, ssm_b_im, ssm_c_re, ssm_c_im, ssm_d, ssm_w_glu, ssm_b_glu, attn_sinks, w_out, post_norm_g, pl_w_proj, pl_w_gate, pl_b_gate, loss_target, m_pre_norm_g, m_w_in, m_ssm_lam_re, m_ssm_lam_im, m_ssm_log_step, m_ssm_b_re, m_ssm_b_im, m_ssm_c_re, m_ssm_c_im, m_ssm_d, m_ssm_w_glu, m_ssm_b_glu, m_attn_sinks, m_w_out, m_post_norm_g, m_pl_w_proj, m_pl_w_gate, m_pl_b_gate, v_pre_norm_g, v_w_in, v_ssm_lam_re, v_ssm_lam_im, v_ssm_log_step, v_ssm_b_re, v_ssm_b_im, v_ssm_c_re, v_ssm_c_im, v_ssm_d, v_ssm_w_glu, v_ssm_b_glu, v_attn_sinks, v_w_out, v_post_norm_g, v_pl_w_proj, v_pl_w_gate, v_pl_b_gate):
    weights = dict(pre_norm_g=pre_norm_g, w_in=w_in, ssm_lam_re=ssm_lam_re, ssm_lam_im=ssm_lam_im,
                   ssm_log_step=ssm_log_step, ssm_b_re=ssm_b_re, ssm_b_im=ssm_b_im, ssm_c_re=ssm_c_re,
                   ssm_c_im=ssm_c_im, ssm_d=ssm_d, ssm_w_glu=ssm_w_glu, ssm_b_glu=ssm_b_glu, attn_sinks=attn_sinks,
                   w_out=w_out, post_norm_g=post_norm_g, pl_w_proj=pl_w_proj, pl_w_gate=pl_w_gate, pl_b_gate=pl_b_gate)
    m_in = dict(pre_norm_g=m_pre_norm_g, w_in=m_w_in, ssm_lam_re=m_ssm_lam_re, ssm_lam_im=m_ssm_lam_im,
                ssm_log_step=m_ssm_log_step, ssm_b_re=m_ssm_b_re, ssm_b_im=m_ssm_b_im, ssm_c_re=m_ssm_c_re,
                ssm_c_im=m_ssm_c_im, ssm_d=m_ssm_d, ssm_w_glu=m_ssm_w_glu, ssm_b_glu=m_ssm_b_glu,
                attn_sinks=m_attn_sinks, w_out=m_w_out, post_norm_g=m_post_norm_g, pl_w_proj=m_pl_w_proj,
                pl_w_gate=m_pl_w_gate, pl_b_gate=m_pl_b_gate)
    v_in = dict(pre_norm_g=v_pre_norm_g, w_in=v_w_in, ssm_lam_re=v_ssm_lam_re, ssm_lam_im=v_ssm_lam_im,
                ssm_log_step=v_ssm_log_step, ssm_b_re=v_ssm_b_re, ssm_b_im=v_ssm_b_im, ssm_c_re=v_ssm_c_re,
                ssm_c_im=v_ssm_c_im, ssm_d=v_ssm_d, ssm_w_glu=v_ssm_w_glu, ssm_b_glu=v_ssm_b_glu,
                attn_sinks=v_attn_sinks, w_out=v_w_out, post_norm_g=v_post_norm_g, pl_w_proj=v_pl_w_proj,
                pl_w_gate=v_pl_w_gate, pl_b_gate=v_pl_b_gate)

    def two_d(tree):
        return {k: _to_kernel_form(k, a) for k, a in tree.items()}

    w2, m2, v2 = two_d(weights), two_d(m_in), two_d(v_in)

    (w_in_full,) = _gather_weights_beside([w2["w_in"].astype(BF16)], "gather_w_in_beside", 4)
    s5_params = tuple(w2[n] for n in ("ssm_lam_re", "ssm_lam_im", "ssm_log_step", "ssm_b_re", "ssm_b_im", "ssm_c_re",
                                      "ssm_c_im"))
    s5_operands = _s5_params_fwd(*s5_params)
    hn = _pre_norm(x.reshape(-1, D_MODEL), w2["pre_norm_g"])
    behind = s5_operands[0][0, 0] * 0.0 + hn[0, 0].astype(F32) * 0.0
    rest = _gather_weights_beside([(w2[n] + behind).astype(BF16) for n in _BIG[1:]], "gather_weights_beside", 1)
    full = dict(zip(_BIG, [w_in_full] + rest))
    mats = ("w_in_early", "w_in_late") + _BIG[1:]
    landed, bc = {}, {}

    def send_tail_grads(ready):
        sent_early = ("w_out", "pl_w_gate", "pl_w_proj")
        landed.update(zip([mats.index(n) for n in sent_early],
                          _scatter_beside([ready[n] for n in sent_early], "scatter_beside", 2)))
        return landed[mats.index(sent_early[-1])]

    def send_bc(d_bc):
        bc["own"] = d_bc
        bc["landed"] = _broadcast_beside([d_bc])[0]
        return d_bc, bc["landed"]

    grad_x, loss, grads = _local_step(
        x, hn, p, loss_target, w2["pre_norm_g"], full["w_in"], s5_params, s5_operands, w2["ssm_d"], full["ssm_w_glu"], w2["ssm_b_glu"],
        w2["attn_sinks"], full["w_out"], w2["post_norm_g"], full["pl_w_proj"], full["pl_w_gate"], w2["pl_b_gate"], send_tail_grads, send_bc)

    landed[0] = _scatter_beside([grads["w_in_early_bf16"]], "scatter_w_in_beside", 5)[0]
    sent_here = {**{n: grads[n] for n in _SMALL if n not in _SMALL_EARLY}, "loss": loss}
    halves_of_w_in = ((0, 1),) + tuple((i,) for i in range(2, len(mats)))
    g_big, g_small = _exchange_grads([grads[n] for n in mats], halves_of_w_in, sent_here, landed, bc["own"], bc["landed"])
    g_big = dict(zip(_BIG, g_big))
    total_loss = g_small.pop("loss")

    big_out = _adamw([w2[n] for n in _BIG], [g_big[n] for n in _BIG], [m2[n] for n in _BIG], [v2[n] for n in _BIG],
                     8, "adamw_matrices")
    small_names = tuple(_SMALL)
    small_out = _adamw([w2[n] for n in small_names], [g_small[n] for n in small_names], [m2[n] for n in small_names],
                       [v2[n] for n in small_names], 1, "adamw_small")

    results = [{**dict(zip(_BIG, big_part)), **dict(zip(small_names, small_part))}
               for big_part, small_part in zip(big_out, small_out)]
    flat = [_from_kernel_form(name, r[name], weights[name].shape) for r in results for name in _WEIGHT_ORDER]
    return (total_loss.reshape(()), grad_x, *flat)
```

```python
import math

import jax
import jax.numpy as jnp
from jax import lax
from jax.experimental import pallas as pl
from jax.experimental.pallas import tpu as pltpu
from jax.experimental.pallas import tpu_sc as plsc

F32 = jnp.float32
BF16 = jnp.bfloat16

D_MODEL = 1024
D_SSM = 512
D_ATTN = 512
SSM_GROUPS = 32
SSM_GROUP_CH = 16
SSM_STATE = 64
SSM_LANES = SSM_GROUPS * SSM_STATE
HEAD_DIM = 64
N_HEADS = 8
KV_HEADS = 2
Q_PER_KV = 4
WINDOW = 128
BLOCK = 128
D_PLE = 256
D_IN = 2304
EPS = 1e-6
ATTN_SCALE = 1.0 / math.sqrt(HEAD_DIM)

ADAM_LR = 0.001
ADAM_B1 = 0.9
ADAM_B2 = 0.999
ADAM_EPS = 1e-08
ADAM_WD = 0.01
ADAM_STEP = 10

N_CHIPS = 4
LANES = 128
SCAN_CHUNKS = 8
SCAN_TILE_STEPS = 32
SCAN_LANE_CHUNK = 512
MIB = 2 ** 20
MESH = pl.DeviceIdType.MESH


def _dot(a, b):
    return jnp.dot(a, b, preferred_element_type=F32)


def _dot_nt(a, b):
    return lax.dot_general(a, b, (((1,), (1,)), ((), ())), preferred_element_type=F32)


def _dot_tn(a, b):
    return lax.dot_general(a, b, (((0,), (0,)), ((), ())), preferred_element_type=F32)


def _params(vmem_mib, semantics=None):
    kw = dict(vmem_limit_bytes=vmem_mib * MIB)
    if semantics is not None:
        kw["dimension_semantics"] = semantics
    return pltpu.CompilerParams(**kw)


def _full(shape):
    nd = len(shape)
    return pl.BlockSpec(shape, lambda *_: (0,) * nd, pipeline_mode=pl.Buffered(1))


def _rows(tm, width):
    return pl.BlockSpec((tm, width), lambda i: (i, 0))


def _sds(shape, dtype=F32):
    return pltpu.HBM(shape, dtype)


def _call(body, **kw):
    fn = pl.pallas_call(body, **kw)
    return lambda *args: fn(*[pltpu.with_memory_space_constraint(a, pltpu.HBM) for a in args])


def _silu(z):
    return z * jax.nn.sigmoid(z)


def _pre_norm(x2d, g1):
    rows = x2d.shape[0]
    tm = 512

    def body(x_ref, g_ref, hn_ref):
        x = x_ref[...]
        r = lax.rsqrt(jnp.mean(x * x, axis=-1, keepdims=True) + EPS)
        hn_ref[...] = (x * r * g_ref[...]).astype(BF16)

    return _call(
        body, name="pre_norm", grid=(rows // tm,), in_specs=[_rows(tm, D_MODEL), _full((1, D_MODEL))],
        out_specs=_rows(tm, D_MODEL), out_shape=_sds((rows, D_MODEL), BF16), compiler_params=_params(32, ("arbitrary",)),
    )(x2d, g1)


def _in_proj(hn, w_in_t, n_seq, seq):
    rows = hn.shape[0]
    tm = 1024
    slab, steps, _, _ = _scan_geometry(n_seq, seq)

    def body(hn_ref, w_ref, *out_refs):
        u_parts, (zs_ref, q_ref, k_ref, v_ref, za_ref) = out_refs[:_SCAN_PARTS], out_refs[_SCAN_PARTS:]
        whole = _dot_nt(hn_ref[...], w_ref[...])

        def proj(a, b):
            return whole[:, a:b]

        _store_chunks(u_parts, pl.program_id(0) * (tm // steps), proj(0, 512), steps, slab)
        zs_ref[...] = proj(512, 1024)
        q_ref[...] = (proj(1024, 1536) * ATTN_SCALE).astype(BF16)
        k_ref[...] = proj(1536, 1664).astype(BF16)
        v_ref[...] = proj(1664, 1792).astype(BF16)
        za_ref[...] = proj(1792, 2304)

    *u_parts, zs, q, k, v, za = _call(
        body, name="in_proj", grid=(rows // tm,),
        in_specs=[_rows(tm, D_MODEL), _full((D_IN, D_MODEL))],
        out_specs=_whole_parts(rows) + [_rows(tm, 512), _rows(tm, 512), _rows(tm, 128), _rows(tm, 128), _rows(tm, 512)],
        out_shape=_part_shapes(rows) + [_sds((rows, 512)), _sds((rows, 512), BF16), _sds((rows, 128), BF16),
                                        _sds((rows, 128), BF16), _sds((rows, 512))],
        compiler_params=_params(48, ("arbitrary",)),
    )(hn, w_in_t)
    return u_parts, zs, q, k, v, za


_EARLY_COLS = D_MODEL // 2


def _in_proj_bwd_early(hn, du_parts, dzs, dq, dk, dv, dza, runs_after, n_seq, seq):
    rows = hn.shape[0]
    tm = 1024
    slab, steps, _, _ = _scan_geometry(n_seq, seq)

    def body(hn_ref, *refs):
        du_parts, (dzs_ref, dq_ref, dk_ref, dv_ref, dza_ref, _, dproj_ref, dw_ref, dwb_ref) = refs[:_SCAN_PARTS], refs[_SCAN_PARTS:]
        i = pl.program_id(0)

        @pl.when(i == 0)
        def _():
            dw_ref[...] = jnp.zeros_like(dw_ref)

        du = _load_chunks(du_parts, i * (tm // steps), tm // steps, steps, slab)
        d_proj = jnp.concatenate([du.astype(BF16), dzs_ref[...], dq_ref[...], dk_ref[...], dv_ref[...], dza_ref[...]],
                                 axis=1)
        dproj_ref[...] = d_proj
        dw_ref[...] += _dot_tn(d_proj, hn_ref[...])

        @pl.when(i == rows // tm - 1)
        def _():
            dwb_ref[...] = dw_ref[...].astype(BF16)

    return _call(
        body, name="in_proj_bwd_early", grid=(rows // tm,),
        in_specs=[_rows(tm, _EARLY_COLS)] + _whole_parts(rows)
        + [_rows(tm, 512), _rows(tm, 512), _rows(tm, 128), _rows(tm, 128), _rows(tm, 512),
           pl.BlockSpec(memory_space=pl.ANY)],
        out_specs=[_rows(tm, D_IN), _full((D_IN, _EARLY_COLS)), _full((D_IN, _EARLY_COLS))],
        out_shape=[_sds((rows, D_IN), BF16), _sds((D_IN, _EARLY_COLS)), _sds((D_IN, _EARLY_COLS), BF16)],
        compiler_params=_params(48, ("arbitrary",)),
    )(hn, *du_parts, dzs, dq, dk, dv, dza, runs_after)


def _in_proj_bwd(x2d, dh1, g1, w_in_t, d_proj, runs_after):
    rows = x2d.shape[0]
    tm = 512

    def body(x_ref, dh1_ref, g_ref, w_ref, dproj_ref, _, gx_ref, dw_ref, dg_ref):
        @pl.when(pl.program_id(0) == 0)
        def _():
            dw_ref[...] = jnp.zeros_like(dw_ref)
            dg_ref[...] = jnp.zeros_like(dg_ref)

        x = x_ref[...]
        g = g_ref[...]
        r = lax.rsqrt(jnp.mean(x * x, axis=-1, keepdims=True) + EPS)
        xr = x * r
        hn = (xr[:, _EARLY_COLS:] * g[:, _EARLY_COLS:]).astype(BF16)
        d_proj = dproj_ref[...]
        dhn = _dot(d_proj, w_ref[...])
        dw_ref[...] += _dot_tn(d_proj, hn)
        dg_ref[...] += jnp.sum(dhn * xr, axis=0, keepdims=True)
        a_ = dhn * g
        gx_ref[...] = dh1_ref[...] + r * a_ - xr * (r * jnp.mean(a_ * xr, axis=-1, keepdims=True))

    late_cols = D_MODEL - _EARLY_COLS
    return _call(
        body, name="in_proj_bwd", grid=(rows // tm,),
        in_specs=[_rows(tm, D_MODEL), _rows(tm, D_MODEL), _full((1, D_MODEL)), _full((D_IN, D_MODEL)), _rows(tm, D_IN),
                  pl.BlockSpec(memory_space=pl.ANY)],
        out_specs=[_rows(tm, D_MODEL), _full((D_IN, late_cols)), _full((1, D_MODEL))],
        out_shape=[_sds((rows, D_MODEL)), _sds((D_IN, late_cols)), _sds((1, D_MODEL))],
        compiler_params=_params(52, ("arbitrary",)),
    )(x2d, dh1, g1, w_in_t, d_proj, runs_after)


def _iota(shape, axis):
    return lax.broadcasted_iota(jnp.int32, shape, axis)


def _sum_of_thirds(f, a):
    hi = a.astype(BF16)
    rest = a - hi.astype(F32)
    mid = rest.astype(BF16)
    low = (rest - mid.astype(F32)).astype(BF16)
    return (f(hi) + f(mid)) + f(low)


@jax.custom_vjp
def _pick_rows(e, a):
    return _sum_of_thirds(lambda part: _dot(e, part), a)


def _pick_rows_fwd(e, a):
    return _pick_rows(e, a), e


def _pick_rows_bwd(e, ct):
    return jnp.zeros_like(e), _sum_of_thirds(lambda part: _dot_tn(e, part), ct)


_pick_rows.defvjp(_pick_rows_fwd, _pick_rows_bwd)


@jax.custom_vjp
def _pick_cols(a, e):
    return _sum_of_thirds(lambda part: _dot(part, e), a)


def _pick_cols_fwd(a, e):
    return _pick_cols(a, e), e


def _pick_cols_bwd(e, ct):
    return _sum_of_thirds(lambda part: _dot_nt(part, e), ct), jnp.zeros_like(e)


_pick_cols.defvjp(_pick_cols_fwd, _pick_cols_bwd)


_HALF_GROUPS = SSM_GROUPS // 2
_N_SHIFT = SSM_STATE.bit_length() - 1
_P_SHIFT = SSM_GROUP_CH.bit_length() - 1


def _s5_operands(lam_re, lam_im, log_step, b_re, b_im, c_re, c_im):
    g, n, p = SSM_GROUPS, SSM_STATE, SSM_GROUP_CH
    gn, gp, hn_, hp = g * n, g * p, _HALF_GROUPS * n, _HALF_GROUPS * p
    eye_g = _iota((g, g), 0) == _iota((g, g), 1)
    step = jnp.sum(jnp.where(eye_g, jnp.exp(log_step), 0.0), axis=1, keepdims=True)
    a_re = lam_re * step
    a_im = lam_im * step
    mag = jnp.exp(a_re)
    lbar_re = mag * jnp.cos(a_im)
    lbar_im = mag * jnp.sin(a_im)
    n_re = lbar_re - 1.0
    den = lam_re * lam_re + lam_im * lam_im
    f_re = (n_re * lam_re + lbar_im * lam_im) / den
    f_im = (lbar_im * lam_re - n_re * lam_im) / den

    spread_n = (_iota((n, gn), 0) == (_iota((n, gn), 1) & (n - 1))).astype(BF16)
    own_g = _iota((g, gn), 0) == (_iota((g, gn), 1) >> _N_SHIFT)

    def to_row(a):
        return jnp.sum(jnp.where(own_g, _pick_cols(a, spread_n), 0.0), axis=0, keepdims=True)

    per_group = ((_iota((gp, g), 0) >> _P_SHIFT) == _iota((gp, g), 1)).astype(BF16)
    fx_re, fx_im = _pick_rows(per_group, f_re), _pick_rows(per_group, f_im)
    bbar_re = fx_re * b_re - fx_im * b_im
    bbar_im = fx_re * b_im + fx_im * b_re

    tile_n = (_iota((n, hn_), 0) == (_iota((n, hn_), 1) & (n - 1))).astype(BF16)
    same_group = (_iota((hp, hn_), 0) >> _P_SHIFT) == (_iota((hp, hn_), 1) >> _N_SHIFT)

    def embed(a, hf):
        return jnp.where(same_group, _pick_cols(a[hf * hp:(hf + 1) * hp], tile_n), 0.0)

    return (to_row(lbar_re), to_row(lbar_im), embed(bbar_re, 0), embed(bbar_re, 1), embed(bbar_im, 0),
            embed(bbar_im, 1), embed(c_re, 0), embed(c_re, 1), embed(c_im, 0), embed(c_im, 1))


_S5_PARAM_SHAPES = ((SSM_GROUPS, SSM_STATE), (SSM_GROUPS, SSM_STATE), (1, SSM_GROUPS),
                    (D_SSM, SSM_STATE), (D_SSM, SSM_STATE), (D_SSM, SSM_STATE), (D_SSM, SSM_STATE))
_CM_SHAPE = (2, _HALF_GROUPS * SSM_GROUP_CH, _HALF_GROUPS * SSM_STATE)
_S5_OPERAND_SHAPES = ((1, SSM_LANES), (1, SSM_LANES), _CM_SHAPE, _CM_SHAPE, _CM_SHAPE, _CM_SHAPE)


def _s5_params_fwd(*params):
    def body(*refs):
        ins, (lre_ref, lim_ref, btre_ref, btim_ref, cmre_ref, cmim_ref) = refs[:7], refs[7:]
        vals = _s5_operands(*[r[...] for r in ins])
        lre_ref[...] = vals[0]
        lim_ref[...] = vals[1]
        for ref, pair in zip((btre_ref, btim_ref, cmre_ref, cmim_ref), (vals[2:4], vals[4:6], vals[6:8], vals[8:10])):
            ref[0] = pair[0].astype(BF16)
            ref[1] = pair[1].astype(BF16)

    dtypes = (F32, F32, BF16, BF16, BF16, BF16)
    return _call(
        body, name="s5_params_fwd",
        in_specs=[_full(s) for s in _S5_PARAM_SHAPES], out_specs=[_full(s) for s in _S5_OPERAND_SHAPES],
        out_shape=[_sds(s, d) for s, d in zip(_S5_OPERAND_SHAPES, dtypes)], compiler_params=_params(32),
    )(*params)


_BC_SIDE_BY_SIDE = (D_SSM, 4 * SSM_STATE)


def _s5_params_bwd(params, cotangents, runs_after):
    def body(*refs):
        ins, (dlre, dlim, dbtre, dbtim, dcmre, dcmim), outs = refs[:7], refs[7:13], refs[14:]
        _, vjp = jax.vjp(_s5_operands, *[r[...] for r in ins])
        cts = (dlre[...], dlim[...], dbtre[0], dbtre[1], dbtim[0], dbtim[1], dcmre[0], dcmre[1], dcmim[0], dcmim[1])
        grads = vjp(cts)
        for ref, val in zip(outs[:3], grads[:3]):
            ref[...] = val
        outs[3][...] = jnp.concatenate(grads[3:], axis=1).astype(BF16)

    out_shapes = _S5_PARAM_SHAPES[:3] + (_BC_SIDE_BY_SIDE,)
    return _call(
        body, name="s5_params_bwd",
        in_specs=[_full(s) for s in _S5_PARAM_SHAPES + _S5_OPERAND_SHAPES] + [pl.BlockSpec(memory_space=pl.ANY)],
        out_specs=[_full(s) for s in out_shapes],
        out_shape=[_sds(s, d) for s, d in zip(out_shapes, (F32, F32, F32, BF16))], compiler_params=_params(48),
    )(*params, *cotangents, runs_after)


def _scan_geometry(n_seq, seq):
    slab = n_seq * SCAN_CHUNKS
    steps = seq // SCAN_CHUNKS
    tile_rows = slab * SCAN_TILE_STEPS
    n_tiles = steps // SCAN_TILE_STEPS
    return slab, steps, tile_rows, n_tiles


_SCAN_PARTS = D_SSM // LANES


def _whole_parts(rows):
    return [_full((rows, LANES))] * _SCAN_PARTS


def _part_shapes(rows):
    return [_sds((rows, LANES))] * _SCAN_PARTS


def _load_chunks(parts, first_chunk, n_chunks, steps, slab):
    return jnp.concatenate([
        jnp.concatenate([ref[pl.ds(first_chunk + q, steps, stride=slab), :] for ref in parts], axis=1)
        for q in range(n_chunks)], axis=0)


def _store_chunks(parts, first_chunk, value, steps, slab):
    for q in range(value.shape[0] // steps):
        for j, ref in enumerate(parts):
            ref[pl.ds(first_chunk + q, steps, stride=slab), :] = value[q * steps:(q + 1) * steps,
                                                                     j * LANES:(j + 1) * LANES]


def _join_parts(parts):
    return jnp.concatenate([ref[...] for ref in parts], axis=1)


def _split_parts(parts, value):
    for j, ref in enumerate(parts):
        ref[...] = value[:, j * LANES:(j + 1) * LANES]


def _complex_power(re, im, n):
    out = None
    while n:
        if n & 1:
            out = (re, im) if out is None else (out[0] * re - out[1] * im, out[0] * im + out[1] * re)
        n >>= 1
        if n:
            re, im = re * re - im * im, 2.0 * re * im
    return out


def _chunk_carry(sum_re, sum_im, carry_re, carry_im, a_re, a_im, n_seq, reverse):
    carry_re[...] = jnp.zeros_like(carry_re)
    carry_im[...] = jnp.zeros_like(carry_im)
    for s in range(n_seq):
        order = range(SCAN_CHUNKS - 2, -1, -1) if reverse else range(1, SCAN_CHUNKS)
        for c in order:
            r = s * SCAN_CHUNKS + c
            p = r + 1 if reverse else r - 1
            p_re, p_im = carry_re[p:p + 1, :], carry_im[p:p + 1, :]
            carry_re[r:r + 1, :] = a_re * p_re - a_im * p_im + sum_re[p:p + 1, :]
            carry_im[r:r + 1, :] = a_re * p_im + a_im * p_re + sum_im[p:p + 1, :]


def _s5_scan_fwd(u_parts, bt_re, bt_im, cm_re, cm_im, lbar_re, lbar_im, d_row, n_seq, seq):
    slab, steps, tile_rows, n_tiles = _scan_geometry(n_seq, seq)
    rows = u_parts[0].shape[0]

    def body(*refs):
        u_refs, refs = refs[:_SCAN_PARTS], refs[_SCAN_PARTS:]
        (bre_ref, bim_ref, cre_ref, cim_ref, lre_ref, lim_ref, d_ref), refs = refs[:7], refs[7:]
        y_refs, (hre_ref, him_ref, st_re, st_im, h0_re, h0_im, buf_re, buf_im) = refs[:_SCAN_PARTS], refs[_SCAN_PARTS:]
        second = pl.program_id(0) == 1
        i = pl.program_id(1)

        @pl.when(jnp.logical_and(i == 0, jnp.logical_not(second)))
        def _():
            st_re[...] = jnp.zeros_like(st_re)
            st_im[...] = jnp.zeros_like(st_im)

        u = _join_parts(u_refs)
        ub = u.astype(BF16)
        for hf in range(2):
            cols = slice(hf * 1024, (hf + 1) * 1024)
            buf_re[:, cols] = _dot(ub[:, hf * 256:(hf + 1) * 256], bre_ref[hf])
            buf_im[:, cols] = _dot(ub[:, hf * 256:(hf + 1) * 256], bim_ref[hf])

        for lc in range(SSM_LANES // SCAN_LANE_CHUNK):
            cols = slice(lc * SCAN_LANE_CHUNK, (lc + 1) * SCAN_LANE_CHUNK)
            l_re = jnp.broadcast_to(lre_ref[:, cols], (slab, SCAN_LANE_CHUNK))
            l_im = jnp.broadcast_to(lim_ref[:, cols], (slab, SCAN_LANE_CHUNK))

            def scan_tile(keep_states):
                def step(t, carry):
                    s_re, s_im = carry
                    r0 = pl.multiple_of(t * slab, slab)
                    n_re = l_re * s_re - l_im * s_im + buf_re[pl.ds(r0, slab), cols]
                    n_im = l_re * s_im + l_im * s_re + buf_im[pl.ds(r0, slab), cols]
                    if keep_states:
                        buf_re[pl.ds(r0, slab), cols] = n_re
                        buf_im[pl.ds(r0, slab), cols] = n_im
                    return n_re, n_im

                s_re, s_im = lax.fori_loop(0, SCAN_TILE_STEPS, step, (st_re[:, cols], st_im[:, cols]), unroll=True)
                st_re[:, cols] = s_re
                st_im[:, cols] = s_im

            pl.when(jnp.logical_not(second))(lambda: scan_tile(False))
            pl.when(second)(lambda: scan_tile(True))

        @pl.when(jnp.logical_and(i == n_tiles - 1, jnp.logical_not(second)))
        def _():
            a_re, a_im = _complex_power(lre_ref[...], lim_ref[...], steps)
            _chunk_carry(st_re, st_im, h0_re, h0_im, a_re, a_im, n_seq, reverse=False)
            st_re[...] = h0_re[...]
            st_im[...] = h0_im[...]

        @pl.when(second)
        def _():
            h_re = buf_re[...].astype(BF16)
            h_im = buf_im[...].astype(BF16)
            hre_ref[...] = h_re
            him_ref[...] = h_im
            for hf in range(2):
                cols = slice(hf * 1024, (hf + 1) * 1024)
                ycols = slice(hf * 256, (hf + 1) * 256)
                y_half = (_dot_nt(h_re[:, cols], cre_ref[hf]) - _dot_nt(h_im[:, cols], cim_ref[hf])
                          + d_ref[:, ycols] * u[:, ycols])
                _split_parts(y_refs[2 * hf:2 * hf + 2], y_half)

    tile = lambda w: pl.BlockSpec((tile_rows, w), lambda p, i: (i, 0))
    out_tile = lambda w: pl.BlockSpec((tile_rows, w), lambda p, i: (i * p, 0))
    cm = _full(_CM_SHAPE)
    outs = _call(
        body, name="s5_scan_fwd", grid=(2, n_tiles),
        in_specs=[tile(LANES)] * _SCAN_PARTS + [cm, cm, cm, cm, _full((1, SSM_LANES)), _full((1, SSM_LANES)),
                                                _full((1, 512))],
        out_specs=[out_tile(LANES)] * _SCAN_PARTS + [out_tile(SSM_LANES), out_tile(SSM_LANES)],
        out_shape=_part_shapes(rows) + [_sds((rows, SSM_LANES), BF16), _sds((rows, SSM_LANES), BF16)],
        scratch_shapes=[pltpu.VMEM((slab, SSM_LANES), F32)] * 4 + [pltpu.VMEM((tile_rows, SSM_LANES), F32)] * 2,
        compiler_params=_params(40, ("arbitrary", "arbitrary")),
    )(*u_parts, bt_re, bt_im, cm_re, cm_im, lbar_re, lbar_im, d_row)
    return outs[:_SCAN_PARTS], outs[_SCAN_PARTS], outs[_SCAN_PARTS + 1]


def _s5_scan_bwd(dy_parts, u_parts, h_re, h_im, bt_re, bt_im, cm_re, cm_im, lbar_re, lbar_im, d_row, n_seq, seq):
    slab, steps, tile_rows, n_tiles = _scan_geometry(n_seq, seq)
    rows = u_parts[0].shape[0]

    def body(*refs):
        dy_refs, u_refs, refs = refs[:_SCAN_PARTS], refs[_SCAN_PARTS:2 * _SCAN_PARTS], refs[2 * _SCAN_PARTS:]
        (hre_ref, him_ref, bre_ref, bim_ref, cre_ref, cim_ref, lre_ref, lim_ref, d_ref), refs = refs[:9], refs[9:]
        du_refs, refs = refs[:_SCAN_PARTS], refs[_SCAN_PARTS:]
        (dbre_ref, dbim_ref, dcre_ref, dcim_ref, dlre_ref, dlim_ref, dd_ref,
         st_re, st_im, g0_re, g0_im, acc_re, acc_im, buf_re, buf_im) = refs
        second = pl.program_id(0) == 1
        i = pl.program_id(1)

        @pl.when(jnp.logical_and(i == 0, jnp.logical_not(second)))
        def _():
            st_re[...] = jnp.zeros_like(st_re)
            st_im[...] = jnp.zeros_like(st_im)
            acc_re[...] = jnp.zeros_like(acc_re)
            acc_im[...] = jnp.zeros_like(acc_im)
            for ref in (dbre_ref, dbim_ref, dcre_ref, dcim_ref, dd_ref):
                ref[...] = jnp.zeros_like(ref)

        dy = _join_parts(dy_refs)
        dyb = dy.astype(BF16)
        for hf in range(2):
            cols = slice(hf * 1024, (hf + 1) * 1024)
            buf_re[:, cols] = _dot(dyb[:, hf * 256:(hf + 1) * 256], cre_ref[hf])
            buf_im[:, cols] = -_dot(dyb[:, hf * 256:(hf + 1) * 256], cim_ref[hf])

        for lc in range(SSM_LANES // SCAN_LANE_CHUNK):
            cols = slice(lc * SCAN_LANE_CHUNK, (lc + 1) * SCAN_LANE_CHUNK)
            l_re = jnp.broadcast_to(lre_ref[:, cols], (slab, SCAN_LANE_CHUNK))
            l_im = jnp.broadcast_to(lim_ref[:, cols], (slab, SCAN_LANE_CHUNK))

            def advance(r0, s_re, s_im):
                n_re = l_re * s_re + l_im * s_im + buf_re[pl.ds(r0, slab), cols]
                n_im = l_re * s_im - l_im * s_re + buf_im[pl.ds(r0, slab), cols]
                buf_re[pl.ds(r0, slab), cols] = n_re
                buf_im[pl.ds(r0, slab), cols] = n_im
                return n_re, n_im

            def row0(k):
                return pl.multiple_of((SCAN_TILE_STEPS - 1 - k) * slab, slab)

            @pl.when(jnp.logical_not(second))
            def _():
                s_re, s_im = lax.fori_loop(0, SCAN_TILE_STEPS, lambda k, s: advance(row0(k), *s),
                                           (st_re[:, cols], st_im[:, cols]), unroll=True)
                st_re[:, cols] = s_re
                st_im[:, cols] = s_im

            @pl.when(second)
            def _():
                def step(k, carry):
                    s_re, s_im, a_re, a_im = carry
                    r0 = row0(k)
                    hr = hre_ref[pl.ds(r0, slab), cols].astype(F32)
                    hi = him_ref[pl.ds(r0, slab), cols].astype(F32)
                    a_re = a_re + s_re * hr + s_im * hi
                    a_im = a_im + s_im * hr - s_re * hi
                    return advance(r0, s_re, s_im) + (a_re, a_im)

                zero = jnp.zeros((slab, SCAN_LANE_CHUNK), F32)
                s_re, s_im, a_re, a_im = lax.fori_loop(
                    0, SCAN_TILE_STEPS, step, (st_re[:, cols], st_im[:, cols], zero, zero), unroll=True)
                st_re[:, cols] = s_re
                st_im[:, cols] = s_im
                acc_re[:, cols] += a_re
                acc_im[:, cols] += a_im

        @pl.when(jnp.logical_and(i == n_tiles - 1, jnp.logical_not(second)))
        def _():
            p_re, p_im = _complex_power(lre_ref[...], lim_ref[...], steps)
            _chunk_carry(st_re, st_im, g0_re, g0_im, p_re, -p_im, n_seq, reverse=True)
            st_re[...] = g0_re[...]
            st_im[...] = g0_im[...]

        @pl.when(second)
        def _():
            u = _join_parts(u_refs)
            ub = u.astype(BF16)
            g_re = buf_re[...].astype(BF16)
            g_im = buf_im[...].astype(BF16)
            dd_ref[...] += jnp.sum(dy * u, axis=0, keepdims=True)
            for hf in range(2):
                cols = slice(hf * 1024, (hf + 1) * 1024)
                ycols = slice(hf * 256, (hf + 1) * 256)
                du_half = (_dot_nt(g_re[:, cols], bre_ref[hf]) + _dot_nt(g_im[:, cols], bim_ref[hf])
                           + d_ref[:, ycols] * dy[:, ycols])
                _split_parts(du_refs[2 * hf:2 * hf + 2], du_half)
                for q4 in range(_HALF_GROUPS // 4):
                    ch = slice(hf * 256 + q4 * 64, hf * 256 + (q4 + 1) * 64)
                    st = slice(hf * 1024 + q4 * 256, hf * 1024 + (q4 + 1) * 256)
                    blk = (hf, slice(q4 * 64, (q4 + 1) * 64), slice(q4 * 256, (q4 + 1) * 256))
                    dbre_ref[blk] += _dot_tn(ub[:, ch], g_re[:, st])
                    dbim_ref[blk] += _dot_tn(ub[:, ch], g_im[:, st])
                    dcre_ref[blk] += _dot_tn(dyb[:, ch], hre_ref[:, st])
                    dcim_ref[blk] -= _dot_tn(dyb[:, ch], him_ref[:, st])

        @pl.when(jnp.logical_and(i == n_tiles - 1, second))
        def _():
            dlre_ref[...] = jnp.sum(acc_re[...], axis=0, keepdims=True)
            dlim_ref[...] = jnp.sum(acc_im[...], axis=0, keepdims=True)

    tile = lambda w: pl.BlockSpec((tile_rows, w), lambda p, i: (n_tiles - 1 - i, 0))
    second_tile = lambda w: pl.BlockSpec((tile_rows, w), lambda p, i: (n_tiles - 1 - i * p, 0))
    cm = _full(_CM_SHAPE)
    row = _full((1, SSM_LANES))
    outs = _call(
        body, name="s5_scan_bwd", grid=(2, n_tiles),
        in_specs=[tile(LANES)] * _SCAN_PARTS + [second_tile(LANES)] * _SCAN_PARTS
        + [second_tile(SSM_LANES), second_tile(SSM_LANES), cm, cm, cm, cm, row, row, _full((1, 512))],
        out_specs=[second_tile(LANES)] * _SCAN_PARTS + [cm, cm, cm, cm, row, row, _full((1, 512))],
        out_shape=(_part_shapes(rows) + [_sds(_CM_SHAPE)] * 4 + [_sds((1, SSM_LANES))] * 2 + [_sds((1, 512))]),
        scratch_shapes=[pltpu.VMEM((slab, SSM_LANES), F32)] * 6 + [pltpu.VMEM((tile_rows, SSM_LANES), F32)] * 2,
        compiler_params=_params(48, ("arbitrary", "arbitrary")),
    )(*dy_parts, *u_parts, h_re, h_im, bt_re, bt_im, cm_re, cm_im, lbar_re, lbar_im, d_row)
    return (outs[:_SCAN_PARTS],) + tuple(outs[_SCAN_PARTS:])


def _glu_gate(gl, a, zs):
    return gl * jax.nn.sigmoid(a) * _silu(zs)


def _glu_fwd(y_parts, zs, w_glu, b_glu, n_seq, seq):
    rows = zs.shape[0]
    tm = 512
    slab, steps, _, _ = _scan_geometry(n_seq, seq)

    def body(*refs):
        y_refs, (zs_ref, w_ref, b_ref, o_ref) = refs[:_SCAN_PARTS], refs[_SCAN_PARTS:]
        y = _load_chunks(y_refs, pl.program_id(0) * (tm // steps), tm // steps, steps, slab)
        gl = jax.nn.gelu(y)
        a = _dot(gl.astype(BF16), w_ref[...]) + b_ref[...]
        o_ref[...] = _glu_gate(gl, a, zs_ref[...]).astype(BF16)

    return _call(
        body, name="glu_fwd", grid=(rows // tm,),
        in_specs=_whole_parts(rows) + [_rows(tm, 512), _full((512, 512)), _full((1, 512))],
        out_specs=_rows(tm, 512), out_shape=_sds((rows, 512), BF16),
        compiler_params=_params(32, ("arbitrary",)),
    )(*y_parts, zs, w_glu, b_glu)


def _glu_bwd(y_parts, zs, d_out, w_glu, b_glu, n_seq, seq):
    rows = zs.shape[0]
    tm = 512
    slab, steps, _, _ = _scan_geometry(n_seq, seq)

    def body(*refs):
        y_refs, (zs_ref, d_ref, w_ref, b_ref), refs = refs[:_SCAN_PARTS], refs[_SCAN_PARTS:_SCAN_PARTS + 4], refs[_SCAN_PARTS + 4:]
        dy_refs, (dzs_ref, dw_ref, db_ref) = refs[:_SCAN_PARTS], refs[_SCAN_PARTS:]
        first_chunk = pl.program_id(0) * (tm // steps)

        @pl.when(pl.program_id(0) == 0)
        def _():
            dw_ref[...] = jnp.zeros_like(dw_ref)
            db_ref[...] = jnp.zeros_like(db_ref)

        gl, gelu_vjp = jax.vjp(jax.nn.gelu, _load_chunks(y_refs, first_chunk, tm // steps, steps, slab))
        glb = gl.astype(BF16)
        a = _dot(glb, w_ref[...]) + b_ref[...]
        _, gate_vjp = jax.vjp(_glu_gate, gl, a, zs_ref[...])
        d_gl, d_a, d_zs = gate_vjp(d_ref[...])
        dab = d_a.astype(BF16)
        d_gl = d_gl + _dot_nt(dab, w_ref[...])
        _store_chunks(dy_refs, first_chunk, gelu_vjp(d_gl)[0], steps, slab)
        dzs_ref[...] = d_zs.astype(BF16)
        dw_ref[...] += _dot_tn(glb, dab)
        db_ref[...] += jnp.sum(d_a, axis=0, keepdims=True)

    *dy_parts, dzs, dw, db = _call(
        body, name="glu_bwd", grid=(rows // tm,),
        in_specs=_whole_parts(rows) + [_rows(tm, 512), _rows(tm, 512), _full((512, 512)), _full((1, 512))],
        out_specs=_whole_parts(rows) + [_rows(tm, 512), _full((512, 512)), _full((1, 512))],
        out_shape=_part_shapes(rows) + [_sds((rows, 512), BF16), _sds((512, 512)), _sds((1, 512))],
        compiler_params=_params(40, ("arbitrary",)),
    )(*y_parts, zs, d_out, w_glu, b_glu)
    return dy_parts, dzs, dw, db


_GROUP_ROWS = Q_PER_KV * BLOCK
_BLOCK_SHIFT = BLOCK.bit_length() - 1


def _attn_bias(j):
    query = _iota((BLOCK, _GROUP_ROWS), 1)
    dist_cur = (query & (BLOCK - 1)) - _iota((BLOCK, _GROUP_ROWS), 0)
    dist_prev = dist_cur + BLOCK
    head = query >> _BLOCK_SHIFT
    slope = jnp.zeros((BLOCK, _GROUP_ROWS), F32)
    for g in range(Q_PER_KV):
        slope = jnp.where(head == g, 2.0 ** (-(j * Q_PER_KV + g + 1)), slope)
    bias_cur = jnp.where(dist_cur >= 0, -slope * dist_cur.astype(F32), -jnp.inf)
    bias_prev = jnp.where(dist_prev < WINDOW, -slope * dist_prev.astype(F32), -jnp.inf)
    return bias_cur, bias_prev


_ATTN_BIAS_SCRATCH = pltpu.VMEM((KV_HEADS, 2, BLOCK, _GROUP_ROWS), F32)


def _fill_attn_bias(bias_ref):
    @pl.when(jnp.logical_and(pl.program_id(0) == 0, pl.program_id(1) == 0))
    def _():
        for j in range(KV_HEADS):
            bias_ref[j, 0], bias_ref[j, 1] = _attn_bias(j)


def _stack_heads(x, j):
    heads = range(j * Q_PER_KV, (j + 1) * Q_PER_KV)
    return jnp.concatenate([x[:, h * HEAD_DIM:(h + 1) * HEAD_DIM] for h in heads], axis=0)


def _head_rows(x, j):
    heads = range(j * Q_PER_KV, (j + 1) * Q_PER_KV)
    return jnp.concatenate([x[h:h + 1, :] for h in heads], axis=1)


def _sink_row(sk_ref, j):
    heads = range(j * Q_PER_KV, (j + 1) * Q_PER_KV)
    return jnp.concatenate([jnp.broadcast_to(sk_ref[0:1, h:h + 1], (1, BLOCK)) for h in heads], axis=1)


def _attn_fwd(q, k, v, za, sinks, n_seq, seq):
    nb = seq // BLOCK
    rows = q.shape[0]

    def body(q_ref, kc_ref, kp_ref, vc_ref, vp_ref, za_ref, sk_ref, o_ref, ao_ref, lse_ref, bias_ref):
        _fill_attn_bias(bias_ref)
        has_prev = pl.program_id(1) > 0
        q_all = q_ref[...]
        for j in range(KV_HEADS):
            js = slice(j * HEAD_DIM, (j + 1) * HEAD_DIM)
            bias_c, bias_p = bias_ref[j, 0], bias_ref[j, 1]
            q4 = _stack_heads(q_all, j)
            sc = _dot_nt(kc_ref[:, js], q4) + bias_c
            sp = _dot_nt(kp_ref[:, js], q4) + jnp.where(has_prev, bias_p, -jnp.inf)
            sink = _sink_row(sk_ref, j)
            m = jnp.maximum(jnp.max(jnp.maximum(sc, sp), axis=0, keepdims=True), sink)
            ec = jnp.exp(sc - m)
            ep = jnp.exp(sp - m)
            den = jnp.sum(ec + ep, axis=0, keepdims=True) + jnp.exp(sink - m)
            inv = 1.0 / den
            o4 = _dot_tn((ec * inv).astype(BF16), vc_ref[:, js]) + _dot_tn((ep * inv).astype(BF16), vp_ref[:, js])
            lse4 = m + jnp.log(den)
            for g in range(Q_PER_KV):
                h = j * Q_PER_KV + g
                o_ref[:, h * HEAD_DIM:(h + 1) * HEAD_DIM] = o4[g * BLOCK:(g + 1) * BLOCK]
                lse_ref[h:h + 1, :] = lse4[:, g * BLOCK:(g + 1) * BLOCK]
        ao_ref[...] = (o_ref[...] * _silu(za_ref[...])).astype(BF16)

    cur = lambda w: pl.BlockSpec((BLOCK, w), lambda b, n: (b * nb + n, 0))
    prev = lambda w: pl.BlockSpec((BLOCK, w), lambda b, n: (b * nb + jnp.maximum(n - 1, 0), 0))
    lse_rows = rows // BLOCK * N_HEADS
    return _call(
        body, name="attn_fwd", grid=(n_seq, nb),
        in_specs=[cur(512), cur(128), prev(128), cur(128), prev(128), cur(512), _full((1, N_HEADS))],
        out_specs=[cur(512), cur(512), pl.BlockSpec((N_HEADS, BLOCK), lambda b, n: (b * nb + n, 0))],
        out_shape=[_sds((rows, 512)), _sds((rows, 512), BF16), _sds((lse_rows, BLOCK))],
        scratch_shapes=[_ATTN_BIAS_SCRATCH], compiler_params=_params(32, ("arbitrary", "arbitrary")),
    )(q, k, k, v, v, za, sinks)


def _attn_bwd(q, k, v, za, o, lse, d_ao, sinks, n_seq, seq):
    nb = seq // BLOCK
    rows = q.shape[0]

    def body(q_ref, kc_ref, kp_ref, vc_ref, vp_ref, za_ref, o_ref, lse_ref, d_ref, sk_ref,
             dq_ref, dk_ref, dv_ref, dza_ref, dsk_ref, bias_ref, dk_carry, dv_carry):
        n = nb - 1 - pl.program_id(1)
        _fill_attn_bias(bias_ref)

        @pl.when(jnp.logical_and(pl.program_id(0) == 0, pl.program_id(1) == 0))
        def _():
            dsk_ref[...] = jnp.zeros_like(dsk_ref)
            dk_carry[...] = jnp.zeros_like(dk_carry)
            dv_carry[...] = jnp.zeros_like(dv_carry)

        has_prev = n > 0
        has_next = n + 1 < nb

        _, gate_vjp = jax.vjp(lambda o_, z_: o_ * _silu(z_), o_ref[...], za_ref[...])
        d_o, d_za = gate_vjp(d_ref[...])
        dza_ref[...] = d_za.astype(BF16)
        q_all = q_ref[...]
        lse_all = lse_ref[...]

        for j in range(KV_HEADS):
            js = slice(j * HEAD_DIM, (j + 1) * HEAD_DIM)
            kc, kp, vc, vp = kc_ref[:, js], kp_ref[:, js], vc_ref[:, js], vp_ref[:, js]
            bias_c, bias_p = bias_ref[j, 0], bias_ref[j, 1]
            q4 = _stack_heads(q_all, j)
            do4b = _stack_heads(d_o, j).astype(BF16)
            lse4 = _head_rows(lse_all, j)
            pc = jnp.exp(_dot_nt(kc, q4) + bias_c - lse4)
            pp = jnp.exp(_dot_nt(kp, q4) + jnp.where(has_prev, bias_p, -jnp.inf) - lse4)
            dpc = _dot_nt(vc, do4b)
            dpp = _dot_nt(vp, do4b)
            delta = jnp.sum(pc * dpc + pp * dpp, axis=0, keepdims=True)
            dsc = (pc * (dpc - delta)).astype(BF16)
            dsp = (pp * (dpp - delta)).astype(BF16)
            dq4 = ((_dot_tn(dsc, kc) + _dot_tn(dsp, kp)) * ATTN_SCALE).astype(BF16)
            sink_loss = jnp.exp(_sink_row(sk_ref, j) - lse4) * delta
            for g in range(Q_PER_KV):
                h = j * Q_PER_KV + g
                dq_ref[:, h * HEAD_DIM:(h + 1) * HEAD_DIM] = dq4[g * BLOCK:(g + 1) * BLOCK]
                dsk_ref[0:1, h:h + 1] -= jnp.sum(sink_loss[:, g * BLOCK:(g + 1) * BLOCK], axis=1, keepdims=True)
            dk = _dot(dsc, q4) + jnp.where(has_next, dk_carry[j], 0.0)
            dv = _dot(pc.astype(BF16), do4b) + jnp.where(has_next, dv_carry[j], 0.0)
            dk_carry[j] = _dot(dsp, q4)
            dv_carry[j] = _dot(pp.astype(BF16), do4b)
            dk_ref[:, js] = dk.astype(BF16)
            dv_ref[:, js] = dv.astype(BF16)

    cur = lambda w: pl.BlockSpec((BLOCK, w), lambda b, s: (b * nb + nb - 1 - s, 0))
    prev = lambda w: pl.BlockSpec((BLOCK, w), lambda b, s: (b * nb + jnp.maximum(nb - 2 - s, 0), 0))
    return _call(
        body, name="attn_bwd", grid=(n_seq, nb),
        in_specs=[cur(512), cur(128), prev(128), cur(128), prev(128), cur(512), cur(512),
                  pl.BlockSpec((N_HEADS, BLOCK), lambda b, s: (b * nb + nb - 1 - s, 0)), cur(512), _full((1, N_HEADS))],
        out_specs=[cur(512), cur(128), cur(128), cur(512), _full((1, N_HEADS))],
        out_shape=[_sds((rows, 512), BF16), _sds((rows, 128), BF16), _sds((rows, 128), BF16),
                   _sds((rows, 512), BF16), _sds((1, N_HEADS))],
        scratch_shapes=[_ATTN_BIAS_SCRATCH, pltpu.VMEM((KV_HEADS, BLOCK, HEAD_DIM), F32),
                        pltpu.VMEM((KV_HEADS, BLOCK, HEAD_DIM), F32)],
        compiler_params=_params(32, ("arbitrary", "arbitrary")),
    )(q, k, k, v, v, za, o, lse, d_ao, sinks)


def _tail(ssm_out, attn_out, x2d, p2d, target, w_out, g2, w_gate, b_gate, w_proj):
    rows = x2d.shape[0]
    tm = 512

    def body(so_ref, ao_ref, x_ref, p_ref, t_ref, wo_ref, g2_ref, wg_ref, bg_ref, wp_ref,
             dh1_ref, dso_ref, dao_ref, dwo_ref, dwg_ref, dwp_ref, dbg_ref, dg2_ref, loss_ref):
        @pl.when(pl.program_id(0) == 0)
        def _():
            for ref in (dwo_ref, dwg_ref, dwp_ref, dbg_ref, dg2_ref, loss_ref):
                ref[...] = jnp.zeros_like(ref)

        cat = jnp.concatenate([so_ref[...], ao_ref[...]], axis=1)
        g2 = g2_ref[...]
        mixed = _dot(cat, wo_ref[...])
        r = lax.rsqrt(jnp.mean(mixed * mixed, axis=-1, keepdims=True) + EPS)
        mr = mixed * r
        h1 = x_ref[...] + mr * g2
        h1b = h1.astype(BF16)
        gate = jax.nn.sigmoid(_dot(h1b, wg_ref[...]) + bg_ref[...])
        pb = p_ref[...].astype(BF16)
        wp_blocks = [slice(j * D_PLE, (j + 1) * D_PLE) for j in range(N_CHIPS)]
        pp = jnp.concatenate([_dot(pb, wp_ref[blk, :]) for blk in wp_blocks], axis=1)
        err = h1 + gate * pp - t_ref[...]
        loss_ref[...] += 0.5 * jnp.sum(jnp.mean(err * err, axis=-1, keepdims=True), axis=0, keepdims=True)

        dh2 = err * (1.0 / D_MODEL)
        d_glin = dh2 * pp * gate * (1.0 - gate)
        d_glin_b = d_glin.astype(BF16)
        dwg_ref[...] += _dot_tn(h1b, d_glin_b)
        dbg_ref[...] += jnp.sum(d_glin, axis=0, keepdims=True)
        d_pp = (dh2 * gate).astype(BF16)
        for blk in wp_blocks:
            dwp_ref[blk, :] += _dot_tn(pb, d_pp[:, blk])
        dh1 = dh2 + _dot_nt(d_glin_b, wg_ref[...])
        dh1_ref[...] = dh1
        dg2_ref[...] += jnp.sum(dh1 * mr, axis=0, keepdims=True)
        a_ = dh1 * g2
        d_mixed = (r * a_ - mr * (r * jnp.mean(a_ * mr, axis=-1, keepdims=True))).astype(BF16)
        dwo_ref[...] += _dot_tn(cat, d_mixed)
        d_cat = _dot_nt(d_mixed, wo_ref[...])
        dso_ref[...] = d_cat[:, 0:512]
        dao_ref[...] = d_cat[:, 512:1024]

    return _call(
        body, name="tail_fwd_bwd", grid=(rows // tm,),
        in_specs=[_rows(tm, 512), _rows(tm, 512), _rows(tm, D_MODEL), _rows(tm, D_PLE), _rows(tm, D_MODEL),
                  _full((D_MODEL, D_MODEL)), _full((1, D_MODEL)), _full((D_MODEL, D_MODEL)), _full((1, D_MODEL)),
                  _full((N_CHIPS * D_PLE, D_PLE))],
        out_specs=[_rows(tm, D_MODEL), _rows(tm, 512), _rows(tm, 512), _full((D_MODEL, D_MODEL)),
                   _full((D_MODEL, D_MODEL)), _full((N_CHIPS * D_PLE, D_PLE)), _full((1, D_MODEL)), _full((1, D_MODEL)),
                   _full((1, 1))],
        out_shape=[_sds((rows, D_MODEL)), _sds((rows, 512)), _sds((rows, 512)), _sds((D_MODEL, D_MODEL)),
                   _sds((D_MODEL, D_MODEL)), _sds((N_CHIPS * D_PLE, D_PLE)), _sds((1, D_MODEL)), _sds((1, D_MODEL)),
                   _sds((1, 1))],
        compiler_params=_params(52, ("arbitrary",)),
    )(ssm_out, attn_out, x2d, p2d, target, w_out, g2, w_gate, b_gate, w_proj)


def _local_step(x, hn, p, target, pre_norm_g, w_in_t, s5_params, s5_operands, ssm_d, w_glu, b_glu, sinks, w_out,
                post_norm_g, w_proj, w_gate, b_gate, send_tail_grads=lambda ready: ready["w_out"],
                send_bc=lambda d_bc: (d_bc, d_bc)):
    n_seq, seq, _ = x.shape
    rows = n_seq * seq
    x2d = x.reshape(rows, D_MODEL)
    p2d = p.reshape(rows, D_PLE)
    t2d = target.reshape(rows, D_MODEL)

    l_re, l_im, bt_re, bt_im, cm_re, cm_im = s5_operands

    u_scan, zs, q, k, v, za = _in_proj(hn, w_in_t, n_seq, seq)
    y_scan, h_re, h_im = _s5_scan_fwd(u_scan, bt_re, bt_im, cm_re, cm_im, l_re, l_im, ssm_d, n_seq, seq)
    ssm_out = _glu_fwd(y_scan, zs, w_glu, b_glu, n_seq, seq)
    o, attn_out, lse = _attn_fwd(q, k, v, za, sinks, n_seq, seq)

    dh1, d_so, d_ao, d_w_out, d_w_gate, d_w_proj, d_b_gate, d_g2, loss = _tail(
        ssm_out, attn_out, x2d, p2d, t2d, w_out, post_norm_g, w_gate, b_gate, w_proj)

    dq, dk, dv, dza, d_sinks = _attn_bwd(q, k, v, za, o, lse, d_ao, sinks, n_seq, seq)
    dy_scan, dzs, d_w_glu, d_b_glu = _glu_bwd(y_scan, zs, d_so, w_glu, b_glu, n_seq, seq)
    du_scan, d_bt_re, d_bt_im, d_cm_re, d_cm_im, d_l_re, d_l_im, d_d = _s5_scan_bwd(
        dy_scan, u_scan, h_re, h_im, bt_re, bt_im, cm_re, cm_im, l_re, l_im, ssm_d, n_seq, seq)
    tail_grads_arrived = send_tail_grads(dict(w_out=d_w_out, pl_w_gate=d_w_gate, pl_w_proj=d_w_proj))
    d_lam_re, d_lam_im, d_log_step, d_bc = _s5_params_bwd(
        s5_params, (d_l_re, d_l_im, d_bt_re, d_bt_im, d_cm_re, d_cm_im), tail_grads_arrived)

    sent, arrived = send_bc(d_bc)
    d_proj, d_w_in_early, d_w_in_early_b = _in_proj_bwd_early(hn, du_scan, dzs, dq, dk, dv, dza, sent, n_seq, seq)
    grad_x, d_w_in_late, d_g1 = _in_proj_bwd(x2d, dh1, pre_norm_g, w_in_t, d_proj, arrived)
    grads = dict(
        pre_norm_g=d_g1, w_in_early=d_w_in_early, w_in_early_bf16=d_w_in_early_b, w_in_late=d_w_in_late,
        ssm_lam_re=d_lam_re, ssm_lam_im=d_lam_im, ssm_log_step=d_log_step, ssm_bc=d_bc, ssm_d=d_d, ssm_w_glu=d_w_glu,
        ssm_b_glu=d_b_glu, attn_sinks=d_sinks, w_out=d_w_out, post_norm_g=d_g2, pl_w_proj=d_w_proj,
        pl_w_gate=d_w_gate, pl_b_gate=d_b_gate)
    return grad_x.reshape(x.shape), loss, grads


_BIG = ("w_in", "ssm_w_glu", "w_out", "pl_w_proj", "pl_w_gate")
_BIG_SHARD = {"w_in": (D_IN // N_CHIPS, D_MODEL), "ssm_w_glu": (D_SSM // N_CHIPS, D_SSM),
              "w_out": (D_MODEL // N_CHIPS, D_MODEL), "pl_w_proj": (D_PLE, D_MODEL // N_CHIPS),
              "pl_w_gate": (D_MODEL // N_CHIPS, D_MODEL)}
_SMALL = {"pre_norm_g": (1, D_MODEL), "ssm_lam_re": (SSM_GROUPS, SSM_STATE), "ssm_lam_im": (SSM_GROUPS, SSM_STATE),
          "ssm_log_step": (1, SSM_GROUPS), "ssm_b_re": (D_SSM, SSM_STATE), "ssm_b_im": (D_SSM, SSM_STATE),
          "ssm_c_re": (D_SSM, SSM_STATE), "ssm_c_im": (D_SSM, SSM_STATE), "ssm_d": (1, D_SSM), "ssm_b_glu": (1, D_SSM),
          "attn_sinks": (1, N_HEADS), "post_norm_g": (1, D_MODEL), "pl_b_gate": (1, D_MODEL)}
_VEC_ROWS = ("pre_norm_g", "post_norm_g", "pl_b_gate", "ssm_d", "ssm_b_glu", "attn_sinks", "ssm_log_step", "loss")
_SMALL_GROUPS = (
    ("vec", (8, D_MODEL), tuple((name, r) for r, name in enumerate(_VEC_ROWS))),
    ("lam", (2 * SSM_GROUPS, SSM_STATE), (("ssm_lam_re", 0), ("ssm_lam_im", SSM_GROUPS))),
)
_SMALL_EARLY = ("ssm_b_re", "ssm_b_im", "ssm_c_re", "ssm_c_im")
_SMALL_ORDER = tuple(name for _, _, members in _SMALL_GROUPS for name, _ in members) + _SMALL_EARLY
_WEIGHT_ORDER = ("pre_norm_g", "w_in", "ssm_lam_re", "ssm_lam_im", "ssm_log_step", "ssm_b_re", "ssm_b_im", "ssm_c_re",
                 "ssm_c_im", "ssm_d", "ssm_w_glu", "ssm_b_glu", "attn_sinks", "w_out", "post_norm_g", "pl_w_proj",
                 "pl_w_gate", "pl_b_gate")


def _small_shape(name):
    return (1, 1) if name == "loss" else _SMALL[name]


def _to_kernel_form(name, a):
    a = a[0]
    if name == "w_in":
        return a.T
    if name in ("ssm_b_re", "ssm_b_im"):
        a = a.transpose(0, 2, 1)
    return a.reshape(_SMALL[name]) if name in _SMALL else a


def _from_kernel_form(name, a, shape):
    if name == "w_in":
        a = a.T
    if name in ("ssm_b_re", "ssm_b_im"):
        a = a.reshape(SSM_GROUPS, SSM_GROUP_CH, SSM_STATE).transpose(0, 2, 1)
    return a.reshape(shape)


def _mesh_place():
    x, y, c = lax.axis_index("x"), lax.axis_index("y"), lax.axis_index("c")
    other_chips = ((1 - x, y), (x, 1 - y), (1 - x, 1 - y))
    return x, y, c, other_chips


def _gather_copies(s_refs, g_refs, send_sems, recv_sems, local_sems):
    x, y, c, other_chips = _mesh_place()
    started = []
    for i, (s_ref, g_ref) in enumerate(zip(s_refs, g_refs)):
        rows = s_ref.shape[0]
        half = rows // 2

        def block(chip, g_ref=g_ref, rows=rows, half=half):
            return g_ref.at[pl.ds((2 * chip[0] + chip[1]) * rows + c * half, half), :]

        def copy(k, chip, to, src=None, i=i, block=block):
            return pltpu.make_async_remote_copy(
                src_ref=block(chip) if src is None else src, dst_ref=block(chip), send_sem=send_sems.at[6 * i + k],
                recv_sem=recv_sems.at[6 * i + k], device_id=to, device_id_type=MESH)

        own = pltpu.make_async_copy(s_ref, g_ref.at[pl.ds((2 * x + y) * rows, rows), :], local_sems.at[i])
        own.start()
        first = [copy(k, (x, y), (*chip, c), src=s_ref.at[pl.ds(c * half, half), :])
                 for k, chip in enumerate(other_chips)]
        for cp in first:
            cp.start()
        passed = [copy(3 + k, chip, (x, y, 1 - c)) for k, chip in enumerate(other_chips)]
        started.append((own, first, passed))
    for own, first, passed in started:
        for k in range(3):
            first[k].wait_recv()
            passed[k].start()
    for own, first, passed in started:
        for k in range(3):
            passed[k].wait_recv()
        for cp in first + passed:
            cp.wait_send()
        own.wait()


def _gather_semaphores(n_t):
    return [pltpu.SemaphoreType.DMA((6 * n_t,)), pltpu.SemaphoreType.DMA((6 * n_t,)), pltpu.SemaphoreType.DMA((n_t,))]


def _gather_weights_beside(shards, name, collective_id):
    n_t = len(shards)
    hbm = pltpu.MemorySpace.HBM
    s_refs = [jax.new_ref(s, memory_space=hbm) for s in shards]
    g_refs = [jax.empty_ref(jax.ShapeDtypeStruct((N_CHIPS * s.shape[0], s.shape[1]), s.dtype), memory_space=hbm)
              for s in shards]

    def launch(send_sems, recv_sems, local_sems):
        x, y, c, other_chips = _mesh_place()
        peers = [(*chip, c) for chip in other_chips] + [(x, y, 1 - c)]
        barrier = pltpu.get_barrier_semaphore()
        for peer in peers:
            pl.semaphore_signal(barrier, inc=1, device_id=peer, device_id_type=MESH)
        pl.semaphore_wait(barrier, len(peers))
        _gather_copies(s_refs, g_refs, send_sems, recv_sems, local_sems)

    pl.kernel(launch, mesh=plsc.ScalarSubcoreMesh(axis_name="sequencer", num_cores=1), name=name,
              scratch_types=_gather_semaphores(n_t), compiler_params=pltpu.CompilerParams(collective_id=collective_id))()
    return [g[...] for g in g_refs]


_RELATIONS = tuple(((r >> 2) & 1, (r >> 1) & 1, r & 1) for r in range(1, 8))


def _related(place, relation):
    return tuple(1 - a if flip else a for a, flip in zip(place, relation))


def _scatter_beside(mats, name, collective_id):
    hbm = pltpu.MemorySpace.HBM
    src_refs = [jax.new_ref(a, memory_space=hbm) for a in mats]
    land_refs = [jax.empty_ref(jax.ShapeDtypeStruct((7, a.shape[0] // 8, a.shape[1]), a.dtype), memory_space=hbm)
                 for a in mats]

    def launch(send_sems, recv_sems):
        me = (lax.axis_index("x"), lax.axis_index("y"), lax.axis_index("c"))
        peers = [_related(me, rel) for rel in _RELATIONS]
        barrier = pltpu.get_barrier_semaphore()
        for peer in peers:
            pl.semaphore_signal(barrier, inc=1, device_id=peer, device_id_type=MESH)
        pl.semaphore_wait(barrier, len(peers))
        copies = []
        for i, (src, land) in enumerate(zip(src_refs, land_refs)):
            hr = land.shape[1]
            for k, (tx, ty, tc) in enumerate(peers):
                rows = pl.ds((2 * tx + ty) * 2 * hr + tc * hr, hr)
                copies.append(pltpu.make_async_remote_copy(
                    src_ref=src.at[rows, :], dst_ref=land.at[k], send_sem=send_sems.at[7 * i + k],
                    recv_sem=recv_sems.at[7 * i + k], device_id=(tx, ty, tc), device_id_type=MESH))
                copies[-1].start()
        for cp in copies:
            cp.wait()

    n_sems = 7 * len(mats)
    pl.kernel(launch, mesh=plsc.ScalarSubcoreMesh(axis_name="sequencer", num_cores=1), name=name,
              scratch_types=[pltpu.SemaphoreType.DMA((n_sems,)), pltpu.SemaphoreType.DMA((n_sems,))],
              compiler_params=pltpu.CompilerParams(collective_id=collective_id))()
    return [ref[...] for ref in land_refs]


def _broadcast_beside(arrays):
    hbm = pltpu.MemorySpace.HBM
    src_refs = [jax.new_ref(a, memory_space=hbm) for a in arrays]
    land_refs = [jax.empty_ref(jax.ShapeDtypeStruct((len(_RELATIONS),) + a.shape, a.dtype), memory_space=hbm)
                 for a in arrays]

    def launch(send_sems, recv_sems):
        me = (lax.axis_index("x"), lax.axis_index("y"), lax.axis_index("c"))
        peers = [_related(me, rel) for rel in _RELATIONS]
        barrier = pltpu.get_barrier_semaphore()
        for peer in peers:
            pl.semaphore_signal(barrier, inc=1, device_id=peer, device_id_type=MESH)
        pl.semaphore_wait(barrier, len(peers))
        copies = []
        for i, (src, land) in enumerate(zip(src_refs, land_refs)):
            for k, peer in enumerate(peers):
                copies.append(pltpu.make_async_remote_copy(
                    src_ref=src, dst_ref=land.at[k], send_sem=send_sems.at[7 * i + k],
                    recv_sem=recv_sems.at[7 * i + k], device_id=peer, device_id_type=MESH))
                copies[-1].start()
        for cp in copies:
            cp.wait()

    n_sems = 7 * len(arrays)
    pl.kernel(launch, mesh=plsc.ScalarSubcoreMesh(axis_name="sequencer", num_cores=1), name="broadcast_beside",
              scratch_types=[pltpu.SemaphoreType.DMA((n_sems,)), pltpu.SemaphoreType.DMA((n_sems,))],
              compiler_params=pltpu.CompilerParams(collective_id=3))()
    return [ref[...] for ref in land_refs]


def _exchange_grads(big, outputs, small, landed, own_bc, landed_bc):
    n_t = len(big)
    n_g = len(_SMALL_GROUPS)
    names = _SMALL_ORDER
    halves = [(b.shape[0] // N_CHIPS // 2, b.shape[1]) for b in big]
    early = sorted(landed)
    late = [i for i in range(n_t) if i not in landed]
    n_sems = 4 * n_g + 7 * len(late) + n_t
    small_sem0, block_sem0 = n_t, n_t + len(names)
    early_sem0 = block_sem0 + N_CHIPS * len(late)
    landed_sem0 = early_sem0 + 2 * len(early)
    sent = [n for n in names if n in small]

    def body(*refs):
        pos = 0

        def take(n):
            nonlocal pos
            pos += n
            return refs[pos - n:pos]

        big_refs, small_refs = take(n_t), dict(zip(sent, take(len(sent))))
        land_refs = dict(zip(early, take(len(early))))
        own_bc_ref, landed_bc_ref = take(2)
        out_refs, small_out_refs = take(len(outputs)), dict(zip(names, take(len(names))))
        per_late = lambda: dict(zip(late, take(len(late))))
        ga, gb, pme, send_b, recv_b = per_late(), per_late(), take(n_t), per_late(), per_late()
        own_e, land_e = dict(zip(early, take(len(early)))), dict(zip(early, take(len(early))))
        own_s, land_s = take(2)
        s_own, s_sib, s_chips, s_pair = take(n_g), take(n_g), take(n_g), take(n_g)
        stage = dict(zip(names, take(len(names))))
        send_sems, recv_sems, local_sems = take(3)
        x, y, c, other_chips = _mesh_place()
        me = 2 * x + y
        sibling = (x, y, 1 - c)
        sem_at = iter(range(n_sems))

        def remote(src, dst, to):
            k = next(sem_at)
            return pltpu.make_async_remote_copy(src_ref=src, dst_ref=dst, send_sem=send_sems.at[k],
                                                recv_sem=recv_sems.at[k], device_id=to, device_id_type=MESH)

        loads = [pltpu.make_async_copy(small_refs[name], stage[name], local_sems.at[small_sem0 + names.index(name)])
                 for name in sent]
        landed_loads = [pltpu.make_async_copy(own_bc_ref, own_s, local_sems.at[landed_sem0]),
                        pltpu.make_async_copy(landed_bc_ref, land_s, local_sems.at[landed_sem0 + 1])]
        for cp in loads + landed_loads:
            cp.start()
        for cp in loads:
            cp.wait()
        small_swaps = []
        for gi, (_, _, members) in enumerate(_SMALL_GROUPS):
            s_own[gi][...] = jnp.zeros_like(s_own[gi])
            for name, r0 in members:
                r, n = _small_shape(name)
                s_own[gi][r0:r0 + r, 0:n] = stage[name][...]
            small_swaps.append(remote(s_own[gi], s_sib[gi], sibling))
            small_swaps[gi].start()
        order = sorted(late, key=lambda i: halves[i][0] * halves[i][1])
        own_loads, big_swaps = {}, {}
        for i in order:
            hr = halves[i][0]
            own_loads[i], big_swaps[i] = [], []
            for j in range(N_CHIPS):
                mine = big_refs[i].at[pl.ds(j * 2 * hr + c * hr, hr), :]
                theirs = big_refs[i].at[pl.ds(j * 2 * hr + (1 - c) * hr, hr), :]
                sem = local_sems.at[block_sem0 + N_CHIPS * late.index(i) + j]
                own_loads[i].append(pltpu.make_async_copy(mine, ga[i].at[j], sem))
                own_loads[i][j].start()
                big_swaps[i].append(remote(theirs, gb[i].at[j], sibling))
                big_swaps[i][j].start()
        early_loads = {}
        for e, i in enumerate(early):
            hr = halves[i][0]
            mine = big_refs[i].at[pl.ds(me * 2 * hr + c * hr, hr), :]
            early_loads[i] = [pltpu.make_async_copy(mine, own_e[i], local_sems.at[early_sem0 + 2 * e]),
                              pltpu.make_async_copy(land_refs[i], land_e[i], local_sems.at[early_sem0 + 2 * e + 1])]
            for cp in early_loads[i]:
                cp.start()
        small_sends = []
        for gi in range(n_g):
            small_swaps[gi].wait_recv()
            s_pair[gi][...] = s_own[gi][...] + s_sib[gi][...]
            small_sends.append([remote(s_pair[gi], s_chips[gi].at[k], (*chip, c)) for k, chip in enumerate(other_chips)])
            for cp in small_sends[gi]:
                cp.start()

        def pair_sum(i, j):
            return ga[i][j] + gb[i][j]

        big_sends = {}
        for i in order:
            for j in range(N_CHIPS):
                own_loads[i][j].wait()
                big_swaps[i][j].wait_recv()
            big_sends[i] = []
            for k, chip in enumerate(other_chips):
                send_b[i][k] = pair_sum(i, 2 * chip[0] + chip[1]).astype(BF16)
                big_sends[i].append(remote(send_b[i].at[k], recv_b[i].at[k], (*chip, c)))
                big_sends[i][k].start()
        last_swaps, keeps = {}, {}
        for i in early + order:
            hr = halves[i][0]
            if i in landed:
                for cp in early_loads[i]:
                    cp.wait()
                total = own_e[i][...]
                for k in range(len(_RELATIONS)):
                    total = total + land_e[i][k].astype(F32)
                pme[i][...] = total
            else:
                for k in range(3):
                    big_sends[i][k].wait_recv()
                pme[i][...] = ((pair_sum(i, me) + recv_b[i][0].astype(F32)) + recv_b[i][1].astype(F32)) + recv_b[i][2].astype(F32)
            o, = [o for o, group in enumerate(outputs) if i in group]
            first_col = sum(halves[j][1] for j in outputs[o][:outputs[o].index(i)])
            mine = out_refs[o].at[pl.ds(c * hr, hr), pl.ds(first_col, halves[i][1])]
            keeps[i] = pltpu.make_async_copy(pme[i], mine, local_sems.at[i])
            keeps[i].start()
            last_swaps[i] = remote(pme[i], mine, sibling)
            last_swaps[i].start()

        for gi, (_, _, members) in enumerate(_SMALL_GROUPS):
            for k in range(3):
                small_sends[gi][k].wait_recv()
            total = None
            for j in range(N_CHIPS):
                rel = jnp.bitwise_xor(j, me)
                term = jnp.where(rel == 0, s_pair[gi][...], jnp.where(
                    rel == 2, s_chips[gi][0], jnp.where(rel == 1, s_chips[gi][1], s_chips[gi][2])))
                total = term if total is None else total + term
            s_sib[gi][...] = total
            for name, r0 in members:
                r, n = _small_shape(name)
                stage[name][...] = s_sib[gi][r0:r0 + r, 0:n]
        my_index = 4 * x + 2 * y + c
        for cp in landed_loads:
            cp.wait()
        total = None
        for d in range(2 * N_CHIPS):
            rel = jnp.bitwise_xor(d, my_index)
            term = own_s[...]
            for k in range(len(_RELATIONS)):
                term = jnp.where(rel == k + 1, land_s[k], term)
            total = term.astype(F32) if total is None else total + term.astype(F32)
        for a, name in enumerate(_SMALL_EARLY):
            stage[name][...] = total[:, a * SSM_STATE:(a + 1) * SSM_STATE]
        stores = [pltpu.make_async_copy(stage[name], small_out_refs[name], local_sems.at[small_sem0 + a])
                  for a, name in enumerate(names)]
        for cp in stores:
            cp.start()

        for i in range(n_t):
            last_swaps[i].wait_recv()
            keeps[i].wait()
        for cp in stores:
            cp.wait()
        groups = list(big_swaps.values()) + small_sends + list(big_sends.values())
        for cp in small_swaps + [cp for group in groups for cp in group] + list(last_swaps.values()):
            cp.wait_send()

    any_spec = pl.BlockSpec(memory_space=pl.ANY)
    small_shapes = [_sds(_small_shape(n)) for n in names]
    group_shapes = [shape for _, shape, _ in _SMALL_GROUPS]
    vmem = lambda which, dtype, lead=(): [pltpu.VMEM(lead + halves[i], dtype) for i in which]
    outs = _call(
        body, name="exchange_grads",
        in_specs=[any_spec] * (n_t + len(sent) + len(early) + 2),
        out_specs=[any_spec] * (len(outputs) + len(names)),
        out_shape=[_sds((big[group[0]].shape[0] // N_CHIPS, sum(big[i].shape[1] for i in group))) for group in outputs]
        + small_shapes,
        scratch_shapes=(vmem(late, F32, (N_CHIPS,)) + vmem(late, F32, (N_CHIPS,)) + vmem(range(n_t), F32)
                        + vmem(late, BF16, (3,)) + vmem(late, BF16, (3,))
                        + vmem(early, F32)
                        + [pltpu.VMEM((len(_RELATIONS),) + halves[i], landed[i].dtype) for i in early]
                        + [pltpu.VMEM(own_bc.shape, own_bc.dtype), pltpu.VMEM(landed_bc.shape, landed_bc.dtype)]
                        + [pltpu.VMEM(s, F32) for s in group_shapes] * 2 + [pltpu.VMEM((3,) + s, F32) for s in group_shapes]
                        + [pltpu.VMEM(s, F32) for s in group_shapes]
                        + [pltpu.VMEM(_small_shape(n), F32) for n in names]
                        + [pltpu.SemaphoreType.DMA((n_sems,)), pltpu.SemaphoreType.DMA((n_sems,)),
                           pltpu.SemaphoreType.DMA((landed_sem0 + 2,))]),
        compiler_params=_params(48),
    )(*big, *[small[n] for n in sent], *[landed[i] for i in early], own_bc, landed_bc)
    return list(outs[:len(outputs)]), dict(zip(names, outs[len(outputs):]))


def _adamw_update(w, g, m, v):
    m = ADAM_B1 * m + (1.0 - ADAM_B1) * g
    v = ADAM_B2 * v + (1.0 - ADAM_B2) * (g * g)
    m_hat = m / (1.0 - ADAM_B1 ** ADAM_STEP)
    v_hat = v / (1.0 - ADAM_B2 ** ADAM_STEP)
    return -ADAM_LR * (m_hat / (jnp.sqrt(v_hat) + ADAM_EPS) + ADAM_WD * w), m, v


def _adamw(walked, whole, grid):
    n_walked, n_whole = len(walked[0]), len(whole[0])

    def update(ins, outs, count):
        for i in range(count):
            w_, g_, m_, v_ = [ins[a * count + i][...] for a in range(4)]
            vals = (g_,) + _adamw_update(w_, g_, m_, v_)
            for a in range(4):
                outs[a * count + i][...] = vals[a]

    def body(*refs):
        ins, outs = refs[:len(refs) // 2], refs[len(refs) // 2:]
        update(ins[:4 * n_walked], outs[:4 * n_walked], n_walked)

        @pl.when(pl.program_id(0) == 0)
        def _():
            update(ins[4 * n_walked:], outs[4 * n_walked:], n_whole)

    specs = ([pl.BlockSpec((a.shape[0] // grid, a.shape[1]), lambda i: (i, 0)) for a in walked[0]] * 4
             + [_full(a.shape) for a in whole[0]] * 4)
    shapes = [_sds(a.shape) for a in walked[0]] * 4 + [_sds(a.shape) for a in whole[0]] * 4
    outs = _call(
        body, name="adamw", grid=(grid,), in_specs=specs, out_specs=specs, out_shape=shapes,
        compiler_params=_params(40, ("arbitrary",)),
    )(*[a for group in walked for a in group], *[a for group in whole for a in group])
    walked_outs, whole_outs = outs[:4 * n_walked], outs[4 * n_walked:]
    return ([walked_outs[a * n_walked:(a + 1) * n_walked] for a in range(4)],
            [whole_outs[a * n_whole:(a + 1) * n_whole] for a in range(4)])


def kernel(x, p, pre_norm_g, w_in, ssm_lam_re, ssm_lam_im, ssm_log_step, ssm_b_re, ssm_b_im, ssm_c_re, ssm_c_im, ssm_d, ssm_w_glu, ssm_b_glu, attn_sinks, w_out, post_norm_g, pl_w_proj, pl_w_gate, pl_b_gate, loss_target, m_pre_norm_g, m_w_in, m_ssm_lam_re, m_ssm_lam_im, m_ssm_log_step, m_ssm_b_re, m_ssm_b_im, m_ssm_c_re, m_ssm_c_im, m_ssm_d, m_ssm_w_glu, m_ssm_b_glu, m_attn_sinks, m_w_out, m_post_norm_g, m_pl_w_proj, m_pl_w_gate, m_pl_b_gate, v_pre_norm_g, v_w_in, v_ssm_lam_re, v_ssm_lam_im, v_ssm_log_step, v_ssm_b_re, v_ssm_b_im, v_ssm_c_re, v_ssm_c_im, v_ssm_d, v_ssm_w_glu, v_ssm_b_glu, v_attn_sinks, v_w_out, v_post_norm_g, v_pl_w_proj, v_pl_w_gate, v_pl_b_gate):
    weights = dict(pre_norm_g=pre_norm_g, w_in=w_in, ssm_lam_re=ssm_lam_re, ssm_lam_im=ssm_lam_im,
                   ssm_log_step=ssm_log_step, ssm_b_re=ssm_b_re, ssm_b_im=ssm_b_im, ssm_c_re=ssm_c_re,
                   ssm_c_im=ssm_c_im, ssm_d=ssm_d, ssm_w_glu=ssm_w_glu, ssm_b_glu=ssm_b_glu, attn_sinks=attn_sinks,
                   w_out=w_out, post_norm_g=post_norm_g, pl_w_proj=pl_w_proj, pl_w_gate=pl_w_gate, pl_b_gate=pl_b_gate)
    m_in = dict(pre_norm_g=m_pre_norm_g, w_in=m_w_in, ssm_lam_re=m_ssm_lam_re, ssm_lam_im=m_ssm_lam_im,
                ssm_log_step=m_ssm_log_step, ssm_b_re=m_ssm_b_re, ssm_b_im=m_ssm_b_im, ssm_c_re=m_ssm_c_re,
                ssm_c_im=m_ssm_c_im, ssm_d=m_ssm_d, ssm_w_glu=m_ssm_w_glu, ssm_b_glu=m_ssm_b_glu,
                attn_sinks=m_attn_sinks, w_out=m_w_out, post_norm_g=m_post_norm_g, pl_w_proj=m_pl_w_proj,
                pl_w_gate=m_pl_w_gate, pl_b_gate=m_pl_b_gate)
    v_in = dict(pre_norm_g=v_pre_norm_g, w_in=v_w_in, ssm_lam_re=v_ssm_lam_re, ssm_lam_im=v_ssm_lam_im,
                ssm_log_step=v_ssm_log_step, ssm_b_re=v_ssm_b_re, ssm_b_im=v_ssm_b_im, ssm_c_re=v_ssm_c_re,
                ssm_c_im=v_ssm_c_im, ssm_d=v_ssm_d, ssm_w_glu=v_ssm_w_glu, ssm_b_glu=v_ssm_b_glu,
                attn_sinks=v_attn_sinks, w_out=v_w_out, post_norm_g=v_post_norm_g, pl_w_proj=v_pl_w_proj,
                pl_w_gate=v_pl_w_gate, pl_b_gate=v_pl_b_gate)

    def two_d(tree):
        return {k: _to_kernel_form(k, a) for k, a in tree.items()}

    w2, m2, v2 = two_d(weights), two_d(m_in), two_d(v_in)

    (w_in_full,) = _gather_weights_beside([w2["w_in"].astype(BF16)], "gather_w_in_beside", 4)
    s5_params = tuple(w2[n] for n in ("ssm_lam_re", "ssm_lam_im", "ssm_log_step", "ssm_b_re", "ssm_b_im", "ssm_c_re",
                                      "ssm_c_im"))
    s5_operands = _s5_params_fwd(*s5_params)
    hn = _pre_norm(x.reshape(-1, D_MODEL), w2["pre_norm_g"])
    behind = s5_operands[0][0, 0] * 0.0 + hn[0, 0].astype(F32) * 0.0
    rest = _gather_weights_beside([(w2[n] + behind).astype(BF16) for n in _BIG[1:]], "gather_weights_beside", 1)
    full = dict(zip(_BIG, [w_in_full] + rest))
    mats = ("w_in_early", "w_in_late") + _BIG[1:]
    landed, bc = {}, {}

    def send_tail_grads(ready):
        sent_early = ("w_out", "pl_w_gate", "pl_w_proj")
        landed.update(zip([mats.index(n) for n in sent_early],
                          _scatter_beside([ready[n] for n in sent_early], "scatter_beside", 2)))
        return landed[mats.index(sent_early[-1])]

    def send_bc(d_bc):
        bc["own"] = d_bc
        bc["landed"] = _broadcast_beside([d_bc])[0]
        return d_bc, bc["landed"]

    grad_x, loss, grads = _local_step(
        x, hn, p, loss_target, w2["pre_norm_g"], full["w_in"], s5_params, s5_operands, w2["ssm_d"], full["ssm_w_glu"], w2["ssm_b_glu"],
        w2["attn_sinks"], full["w_out"], w2["post_norm_g"], full["pl_w_proj"], full["pl_w_gate"], w2["pl_b_gate"], send_tail_grads, send_bc)

    landed[0] = _scatter_beside([grads["w_in_early_bf16"]], "scatter_w_in_beside", 5)[0]
    sent_here = {**{n: grads[n] for n in _SMALL if n not in _SMALL_EARLY}, "loss": loss}
    halves_of_w_in = ((0, 1),) + tuple((i,) for i in range(2, len(mats)))
    g_big, g_small = _exchange_grads([grads[n] for n in mats], halves_of_w_in, sent_here, landed, bc["own"], bc["landed"])
    g_big = dict(zip(_BIG, g_big))
    total_loss = g_small.pop("loss")

    small_names = tuple(_SMALL)
    big_out, small_out = _adamw(
        [[a[n] for n in _BIG] for a in (w2, g_big, m2, v2)], [[a[n] for n in small_names] for a in (w2, g_small, m2, v2)], 8)

    results = [{**dict(zip(_BIG, big_part)), **dict(zip(small_names, small_part))}
               for big_part, small_part in zip(big_out, small_out)]
    flat = [_from_kernel_form(name, r[name], weights[name].shape) for r in results for name in _WEIGHT_ORDER]
    return (total_loss.reshape(()), grad_x, *flat)
```

```python
import math

import jax
import jax.numpy as jnp
from jax import lax
from jax.experimental import pallas as pl
from jax.experimental.pallas import tpu as pltpu
from jax.experimental.pallas import tpu_sc as plsc

F32 = jnp.float32
BF16 = jnp.bfloat16

D_MODEL = 1024
D_SSM = 512
D_ATTN = 512
SSM_GROUPS = 32
SSM_GROUP_CH = 16
SSM_STATE = 64
SSM_LANES = SSM_GROUPS * SSM_STATE
HEAD_DIM = 64
N_HEADS = 8
KV_HEADS = 2
Q_PER_KV = 4
WINDOW = 128
BLOCK = 128
D_PLE = 256
D_IN = 2304
EPS = 1e-6
ATTN_SCALE = 1.0 / math.sqrt(HEAD_DIM)

ADAM_LR = 0.001
ADAM_B1 = 0.9
ADAM_B2 = 0.999
ADAM_EPS = 1e-08
ADAM_WD = 0.01
ADAM_STEP = 10

N_CHIPS = 4
LANES = 128
SCAN_CHUNKS = 8
SCAN_TILE_STEPS = 32
SCAN_LANE_CHUNK = 512
MIB = 2 ** 20
MESH = pl.DeviceIdType.MESH


def _dot(a, b):
    return jnp.dot(a, b, preferred_element_type=F32)


def _dot_nt(a, b):
    return lax.dot_general(a, b, (((1,), (1,)), ((), ())), preferred_element_type=F32)


def _dot_tn(a, b):
    return lax.dot_general(a, b, (((0,), (0,)), ((), ())), preferred_element_type=F32)


def _params(vmem_mib, semantics=None):
    kw = dict(vmem_limit_bytes=vmem_mib * MIB)
    if semantics is not None:
        kw["dimension_semantics"] = semantics
    return pltpu.CompilerParams(**kw)


def _full(shape):
    nd = len(shape)
    return pl.BlockSpec(shape, lambda *_: (0,) * nd, pipeline_mode=pl.Buffered(1))


def _rows(tm, width):
    return pl.BlockSpec((tm, width), lambda i: (i, 0))


def _sds(shape, dtype=F32):
    return pltpu.HBM(shape, dtype)


def _call(body, **kw):
    fn = pl.pallas_call(body, **kw)
    return lambda *args: fn(*[pltpu.with_memory_space_constraint(a, pltpu.HBM) for a in args])


def _silu(z):
    return z * jax.nn.sigmoid(z)


def _pre_norm(x2d, g1):
    rows = x2d.shape[0]
    tm = 512

    def body(x_ref, g_ref, hn_ref):
        x = x_ref[...]
        r = lax.rsqrt(jnp.mean(x * x, axis=-1, keepdims=True) + EPS)
        hn_ref[...] = (x * r * g_ref[...]).astype(BF16)

    return _call(
        body, name="pre_norm", grid=(rows // tm,), in_specs=[_rows(tm, D_MODEL), _full((1, D_MODEL))],
        out_specs=_rows(tm, D_MODEL), out_shape=_sds((rows, D_MODEL), BF16), compiler_params=_params(32, ("arbitrary",)),
    )(x2d, g1)


def _in_proj(hn, w_in_t, n_seq, seq):
    rows = hn.shape[0]
    tm = 1024
    slab, steps, _, _ = _scan_geometry(n_seq, seq)

    def body(hn_ref, w_ref, *out_refs):
        u_parts, (zs_ref, q_ref, k_ref, v_ref, za_ref) = out_refs[:_SCAN_PARTS], out_refs[_SCAN_PARTS:]
        whole = _dot_nt(hn_ref[...], w_ref[...])

        def proj(a, b):
            return whole[:, a:b]

        _store_chunks(u_parts, pl.program_id(0) * (tm // steps), proj(0, 512), steps, slab)
        zs_ref[...] = proj(512, 1024)
        q_ref[...] = (proj(1024, 1536) * ATTN_SCALE).astype(BF16)
        k_ref[...] = proj(1536, 1664).astype(BF16)
        v_ref[...] = proj(1664, 1792).astype(BF16)
        za_ref[...] = proj(1792, 2304)

    *u_parts, zs, q, k, v, za = _call(
        body, name="in_proj", grid=(rows // tm,),
        in_specs=[_rows(tm, D_MODEL), _full((D_IN, D_MODEL))],
        out_specs=_whole_parts(rows) + [_rows(tm, 512), _rows(tm, 512), _rows(tm, 128), _rows(tm, 128), _rows(tm, 512)],
        out_shape=_part_shapes(rows) + [_sds((rows, 512)), _sds((rows, 512), BF16), _sds((rows, 128), BF16),
                                        _sds((rows, 128), BF16), _sds((rows, 512))],
        compiler_params=_params(48, ("arbitrary",)),
    )(hn, w_in_t)
    return u_parts, zs, q, k, v, za


_EARLY_COLS = D_MODEL // 2


def _in_proj_bwd_early(hn, du_parts, dzs, dq, dk, dv, dza, runs_after, n_seq, seq):
    rows = hn.shape[0]
    tm = 1024
    slab, steps, _, _ = _scan_geometry(n_seq, seq)

    def body(hn_ref, *refs):
        du_parts, (dzs_ref, dq_ref, dk_ref, dv_ref, dza_ref, _, dproj_ref, dw_ref, dwb_ref) = refs[:_SCAN_PARTS], refs[_SCAN_PARTS:]
        i = pl.program_id(0)

        @pl.when(i == 0)
        def _():
            dw_ref[...] = jnp.zeros_like(dw_ref)

        du = _load_chunks(du_parts, i * (tm // steps), tm // steps, steps, slab)
        d_proj = jnp.concatenate([du.astype(BF16), dzs_ref[...], dq_ref[...], dk_ref[...], dv_ref[...], dza_ref[...]],
                                 axis=1)
        dproj_ref[...] = d_proj
        dw_ref[...] += _dot_tn(d_proj, hn_ref[...])

        @pl.when(i == rows // tm - 1)
        def _():
            dwb_ref[...] = dw_ref[...].astype(BF16)

    return _call(
        body, name="in_proj_bwd_early", grid=(rows // tm,),
        in_specs=[_rows(tm, _EARLY_COLS)] + _whole_parts(rows)
        + [_rows(tm, 512), _rows(tm, 512), _rows(tm, 128), _rows(tm, 128), _rows(tm, 512),
           pl.BlockSpec(memory_space=pl.ANY)],
        out_specs=[_rows(tm, D_IN), _full((D_IN, _EARLY_COLS)), _full((D_IN, _EARLY_COLS))],
        out_shape=[_sds((rows, D_IN), BF16), _sds((D_IN, _EARLY_COLS)), _sds((D_IN, _EARLY_COLS), BF16)],
        compiler_params=_params(48, ("arbitrary",)),
    )(hn, *du_parts, dzs, dq, dk, dv, dza, runs_after)


def _in_proj_bwd(x2d, dh1, g1, w_in_t, d_proj, runs_after):
    rows = x2d.shape[0]
    tm = 512

    def body(x_ref, dh1_ref, g_ref, w_ref, dproj_ref, _, gx_ref, dw_ref, dg_ref):
        @pl.when(pl.program_id(0) == 0)
        def _():
            dw_ref[...] = jnp.zeros_like(dw_ref)
            dg_ref[...] = jnp.zeros_like(dg_ref)

        x = x_ref[...]
        g = g_ref[...]
        r = lax.rsqrt(jnp.mean(x * x, axis=-1, keepdims=True) + EPS)
        xr = x * r
        hn = (xr[:, _EARLY_COLS:] * g[:, _EARLY_COLS:]).astype(BF16)
        d_proj = dproj_ref[...]
        dhn = _dot(d_proj, w_ref[...])
        dw_ref[...] += _dot_tn(d_proj, hn)
        dg_ref[...] += jnp.sum(dhn * xr, axis=0, keepdims=True)
        a_ = dhn * g
        gx_ref[...] = dh1_ref[...] + r * a_ - xr * (r * jnp.mean(a_ * xr, axis=-1, keepdims=True))

    late_cols = D_MODEL - _EARLY_COLS
    return _call(
        body, name="in_proj_bwd", grid=(rows // tm,),
        in_specs=[_rows(tm, D_MODEL), _rows(tm, D_MODEL), _full((1, D_MODEL)), _full((D_IN, D_MODEL)), _rows(tm, D_IN),
                  pl.BlockSpec(memory_space=pl.ANY)],
        out_specs=[_rows(tm, D_MODEL), _full((D_IN, late_cols)), _full((1, D_MODEL))],
        out_shape=[_sds((rows, D_MODEL)), _sds((D_IN, late_cols)), _sds((1, D_MODEL))],
        compiler_params=_params(52, ("arbitrary",)),
    )(x2d, dh1, g1, w_in_t, d_proj, runs_after)


def _iota(shape, axis):
    return lax.broadcasted_iota(jnp.int32, shape, axis)


def _sum_of_thirds(f, a):
    hi = a.astype(BF16)
    rest = a - hi.astype(F32)
    mid = rest.astype(BF16)
    low = (rest - mid.astype(F32)).astype(BF16)
    return (f(hi) + f(mid)) + f(low)


@jax.custom_vjp
def _pick_rows(e, a):
    return _sum_of_thirds(lambda part: _dot(e, part), a)


def _pick_rows_fwd(e, a):
    return _pick_rows(e, a), e


def _pick_rows_bwd(e, ct):
    return jnp.zeros_like(e), _sum_of_thirds(lambda part: _dot_tn(e, part), ct)


_pick_rows.defvjp(_pick_rows_fwd, _pick_rows_bwd)


@jax.custom_vjp
def _pick_cols(a, e):
    return _sum_of_thirds(lambda part: _dot(part, e), a)


def _pick_cols_fwd(a, e):
    return _pick_cols(a, e), e


def _pick_cols_bwd(e, ct):
    return _sum_of_thirds(lambda part: _dot_nt(part, e), ct), jnp.zeros_like(e)


_pick_cols.defvjp(_pick_cols_fwd, _pick_cols_bwd)


_HALF_GROUPS = SSM_GROUPS // 2
_N_SHIFT = SSM_STATE.bit_length() - 1
_P_SHIFT = SSM_GROUP_CH.bit_length() - 1


def _s5_operands(lam_re, lam_im, log_step, b_re, b_im, c_re, c_im):
    g, n, p = SSM_GROUPS, SSM_STATE, SSM_GROUP_CH
    gn, gp, hn_, hp = g * n, g * p, _HALF_GROUPS * n, _HALF_GROUPS * p
    eye_g = _iota((g, g), 0) == _iota((g, g), 1)
    step = jnp.sum(jnp.where(eye_g, jnp.exp(log_step), 0.0), axis=1, keepdims=True)
    a_re = lam_re * step
    a_im = lam_im * step
    mag = jnp.exp(a_re)
    lbar_re = mag * jnp.cos(a_im)
    lbar_im = mag * jnp.sin(a_im)
    n_re = lbar_re - 1.0
    den = lam_re * lam_re + lam_im * lam_im
    f_re = (n_re * lam_re + lbar_im * lam_im) / den
    f_im = (lbar_im * lam_re - n_re * lam_im) / den

    spread_n = (_iota((n, gn), 0) == (_iota((n, gn), 1) & (n - 1))).astype(BF16)
    own_g = _iota((g, gn), 0) == (_iota((g, gn), 1) >> _N_SHIFT)

    def to_row(a):
        return jnp.sum(jnp.where(own_g, _pick_cols(a, spread_n), 0.0), axis=0, keepdims=True)

    per_group = ((_iota((gp, g), 0) >> _P_SHIFT) == _iota((gp, g), 1)).astype(BF16)
    fx_re, fx_im = _pick_rows(per_group, f_re), _pick_rows(per_group, f_im)
    bbar_re = fx_re * b_re - fx_im * b_im
    bbar_im = fx_re * b_im + fx_im * b_re

    tile_n = (_iota((n, hn_), 0) == (_iota((n, hn_), 1) & (n - 1))).astype(BF16)
    same_group = (_iota((hp, hn_), 0) >> _P_SHIFT) == (_iota((hp, hn_), 1) >> _N_SHIFT)

    def embed(a, hf):
        return jnp.where(same_group, _pick_cols(a[hf * hp:(hf + 1) * hp], tile_n), 0.0)

    return (to_row(lbar_re), to_row(lbar_im), embed(bbar_re, 0), embed(bbar_re, 1), embed(bbar_im, 0),
            embed(bbar_im, 1), embed(c_re, 0), embed(c_re, 1), embed(c_im, 0), embed(c_im, 1))


_S5_PARAM_SHAPES = ((SSM_GROUPS, SSM_STATE), (SSM_GROUPS, SSM_STATE), (1, SSM_GROUPS),
                    (D_SSM, SSM_STATE), (D_SSM, SSM_STATE), (D_SSM, SSM_STATE), (D_SSM, SSM_STATE))
_CM_SHAPE = (2, _HALF_GROUPS * SSM_GROUP_CH, _HALF_GROUPS * SSM_STATE)
_S5_OPERAND_SHAPES = ((1, SSM_LANES), (1, SSM_LANES), _CM_SHAPE, _CM_SHAPE, _CM_SHAPE, _CM_SHAPE)


def _s5_params_fwd(*params):
    def body(*refs):
        ins, (lre_ref, lim_ref, btre_ref, btim_ref, cmre_ref, cmim_ref) = refs[:7], refs[7:]
        vals = _s5_operands(*[r[...] for r in ins])
        lre_ref[...] = vals[0]
        lim_ref[...] = vals[1]
        for ref, pair in zip((btre_ref, btim_ref, cmre_ref, cmim_ref), (vals[2:4], vals[4:6], vals[6:8], vals[8:10])):
            ref[0] = pair[0].astype(BF16)
            ref[1] = pair[1].astype(BF16)

    dtypes = (F32, F32, BF16, BF16, BF16, BF16)
    return _call(
        body, name="s5_params_fwd",
        in_specs=[_full(s) for s in _S5_PARAM_SHAPES], out_specs=[_full(s) for s in _S5_OPERAND_SHAPES],
        out_shape=[_sds(s, d) for s, d in zip(_S5_OPERAND_SHAPES, dtypes)], compiler_params=_params(32),
    )(*params)


_BC_SIDE_BY_SIDE = (D_SSM, 4 * SSM_STATE)


def _s5_params_bwd(params, cotangents, runs_after):
    def body(*refs):
        ins, (dlre, dlim, dbtre, dbtim, dcmre, dcmim), outs = refs[:7], refs[7:13], refs[14:]
        _, vjp = jax.vjp(_s5_operands, *[r[...] for r in ins])
        cts = (dlre[...], dlim[...], dbtre[0], dbtre[1], dbtim[0], dbtim[1], dcmre[0], dcmre[1], dcmim[0], dcmim[1])
        grads = vjp(cts)
        for ref, val in zip(outs[:3], grads[:3]):
            ref[...] = val
        outs[3][...] = jnp.concatenate(grads[3:], axis=1).astype(BF16)

    out_shapes = _S5_PARAM_SHAPES[:3] + (_BC_SIDE_BY_SIDE,)
    return _call(
        body, name="s5_params_bwd",
        in_specs=[_full(s) for s in _S5_PARAM_SHAPES + _S5_OPERAND_SHAPES] + [pl.BlockSpec(memory_space=pl.ANY)],
        out_specs=[_full(s) for s in out_shapes],
        out_shape=[_sds(s, d) for s, d in zip(out_shapes, (F32, F32, F32, BF16))], compiler_params=_params(48),
    )(*params, *cotangents, runs_after)


def _scan_geometry(n_seq, seq):
    slab = n_seq * SCAN_CHUNKS
    steps = seq // SCAN_CHUNKS
    tile_rows = slab * SCAN_TILE_STEPS
    n_tiles = steps // SCAN_TILE_STEPS
    return slab, steps, tile_rows, n_tiles


_SCAN_PARTS = D_SSM // LANES


def _whole_parts(rows):
    return [_full((rows, LANES))] * _SCAN_PARTS


def _part_shapes(rows):
    return [_sds((rows, LANES))] * _SCAN_PARTS


def _load_chunks(parts, first_chunk, n_chunks, steps, slab):
    return jnp.concatenate([
        jnp.concatenate([ref[pl.ds(first_chunk + q, steps, stride=slab), :] for ref in parts], axis=1)
        for q in range(n_chunks)], axis=0)


def _store_chunks(parts, first_chunk, value, steps, slab):
    for q in range(value.shape[0] // steps):
        for j, ref in enumerate(parts):
            ref[pl.ds(first_chunk + q, steps, stride=slab), :] = value[q * steps:(q + 1) * steps,
                                                                     j * LANES:(j + 1) * LANES]


def _join_parts(parts):
    return jnp.concatenate([ref[...] for ref in parts], axis=1)


def _split_parts(parts, value):
    for j, ref in enumerate(parts):
        ref[...] = value[:, j * LANES:(j + 1) * LANES]


def _complex_power(re, im, n):
    out = None
    while n:
        if n & 1:
            out = (re, im) if out is None else (out[0] * re - out[1] * im, out[0] * im + out[1] * re)
        n >>= 1
        if n:
            re, im = re * re - im * im, 2.0 * re * im
    return out


def _chunk_carry(sum_re, sum_im, carry_re, carry_im, a_re, a_im, n_seq, reverse):
    carry_re[...] = jnp.zeros_like(carry_re)
    carry_im[...] = jnp.zeros_like(carry_im)
    for s in range(n_seq):
        order = range(SCAN_CHUNKS - 2, -1, -1) if reverse else range(1, SCAN_CHUNKS)
        for c in order:
            r = s * SCAN_CHUNKS + c
            p = r + 1 if reverse else r - 1
            p_re, p_im = carry_re[p:p + 1, :], carry_im[p:p + 1, :]
            carry_re[r:r + 1, :] = a_re * p_re - a_im * p_im + sum_re[p:p + 1, :]
            carry_im[r:r + 1, :] = a_re * p_im + a_im * p_re + sum_im[p:p + 1, :]


def _s5_scan_fwd(u_parts, bt_re, bt_im, cm_re, cm_im, lbar_re, lbar_im, d_row, n_seq, seq):
    slab, steps, tile_rows, n_tiles = _scan_geometry(n_seq, seq)
    rows = u_parts[0].shape[0]

    def body(*refs):
        u_refs, refs = refs[:_SCAN_PARTS], refs[_SCAN_PARTS:]
        (bre_ref, bim_ref, cre_ref, cim_ref, lre_ref, lim_ref, d_ref), refs = refs[:7], refs[7:]
        y_refs, (hre_ref, him_ref, st_re, st_im, h0_re, h0_im, buf_re, buf_im) = refs[:_SCAN_PARTS], refs[_SCAN_PARTS:]
        second = pl.program_id(0) == 1
        i = pl.program_id(1)

        @pl.when(jnp.logical_and(i == 0, jnp.logical_not(second)))
        def _():
            st_re[...] = jnp.zeros_like(st_re)
            st_im[...] = jnp.zeros_like(st_im)

        u = _join_parts(u_refs)
        ub = u.astype(BF16)
        for hf in range(2):
            cols = slice(hf * 1024, (hf + 1) * 1024)
            buf_re[:, cols] = _dot(ub[:, hf * 256:(hf + 1) * 256], bre_ref[hf])
            buf_im[:, cols] = _dot(ub[:, hf * 256:(hf + 1) * 256], bim_ref[hf])

        for lc in range(SSM_LANES // SCAN_LANE_CHUNK):
            cols = slice(lc * SCAN_LANE_CHUNK, (lc + 1) * SCAN_LANE_CHUNK)
            l_re = jnp.broadcast_to(lre_ref[:, cols], (slab, SCAN_LANE_CHUNK))
            l_im = jnp.broadcast_to(lim_ref[:, cols], (slab, SCAN_LANE_CHUNK))

            def scan_tile(keep_states):
                def step(t, carry):
                    s_re, s_im = carry
                    r0 = pl.multiple_of(t * slab, slab)
                    n_re = l_re * s_re - l_im * s_im + buf_re[pl.ds(r0, slab), cols]
                    n_im = l_re * s_im + l_im * s_re + buf_im[pl.ds(r0, slab), cols]
                    if keep_states:
                        buf_re[pl.ds(r0, slab), cols] = n_re
                        buf_im[pl.ds(r0, slab), cols] = n_im
                    return n_re, n_im

                s_re, s_im = lax.fori_loop(0, SCAN_TILE_STEPS, step, (st_re[:, cols], st_im[:, cols]), unroll=True)
                st_re[:, cols] = s_re
                st_im[:, cols] = s_im

            pl.when(jnp.logical_not(second))(lambda: scan_tile(False))
            pl.when(second)(lambda: scan_tile(True))

        @pl.when(jnp.logical_and(i == n_tiles - 1, jnp.logical_not(second)))
        def _():
            a_re, a_im = _complex_power(lre_ref[...], lim_ref[...], steps)
            _chunk_carry(st_re, st_im, h0_re, h0_im, a_re, a_im, n_seq, reverse=False)
            st_re[...] = h0_re[...]
            st_im[...] = h0_im[...]

        @pl.when(second)
        def _():
            h_re = buf_re[...].astype(BF16)
            h_im = buf_im[...].astype(BF16)
            hre_ref[...] = h_re
            him_ref[...] = h_im
            for hf in range(2):
                cols = slice(hf * 1024, (hf + 1) * 1024)
                ycols = slice(hf * 256, (hf + 1) * 256)
                y_half = (_dot_nt(h_re[:, cols], cre_ref[hf]) - _dot_nt(h_im[:, cols], cim_ref[hf])
                          + d_ref[:, ycols] * u[:, ycols])
                _split_parts(y_refs[2 * hf:2 * hf + 2], y_half)

    tile = lambda w: pl.BlockSpec((tile_rows, w), lambda p, i: (i, 0))
    out_tile = lambda w: pl.BlockSpec((tile_rows, w), lambda p, i: (i * p, 0))
    cm = _full(_CM_SHAPE)
    outs = _call(
        body, name="s5_scan_fwd", grid=(2, n_tiles),
        in_specs=[tile(LANES)] * _SCAN_PARTS + [cm, cm, cm, cm, _full((1, SSM_LANES)), _full((1, SSM_LANES)),
                                                _full((1, 512))],
        out_specs=[out_tile(LANES)] * _SCAN_PARTS + [out_tile(SSM_LANES), out_tile(SSM_LANES)],
        out_shape=_part_shapes(rows) + [_sds((rows, SSM_LANES), BF16), _sds((rows, SSM_LANES), BF16)],
        scratch_shapes=[pltpu.VMEM((slab, SSM_LANES), F32)] * 4 + [pltpu.VMEM((tile_rows, SSM_LANES), F32)] * 2,
        compiler_params=_params(40, ("arbitrary", "arbitrary")),
    )(*u_parts, bt_re, bt_im, cm_re, cm_im, lbar_re, lbar_im, d_row)
    return outs[:_SCAN_PARTS], outs[_SCAN_PARTS], outs[_SCAN_PARTS + 1]


def _s5_scan_bwd(dy_parts, u_parts, h_re, h_im, bt_re, bt_im, cm_re, cm_im, lbar_re, lbar_im, d_row, n_seq, seq):
    slab, steps, tile_rows, n_tiles = _scan_geometry(n_seq, seq)
    rows = u_parts[0].shape[0]

    def body(*refs):
        dy_refs, u_refs, refs = refs[:_SCAN_PARTS], refs[_SCAN_PARTS:2 * _SCAN_PARTS], refs[2 * _SCAN_PARTS:]
        (hre_ref, him_ref, bre_ref, bim_ref, cre_ref, cim_ref, lre_ref, lim_ref, d_ref), refs = refs[:9], refs[9:]
        du_refs, refs = refs[:_SCAN_PARTS], refs[_SCAN_PARTS:]
        (dbre_ref, dbim_ref, dcre_ref, dcim_ref, dlre_ref, dlim_ref, dd_ref,
         st_re, st_im, g0_re, g0_im, acc_re, acc_im, buf_re, buf_im) = refs
        second = pl.program_id(0) == 1
        i = pl.program_id(1)

        @pl.when(jnp.logical_and(i == 0, jnp.logical_not(second)))
        def _():
            st_re[...] = jnp.zeros_like(st_re)
            st_im[...] = jnp.zeros_like(st_im)
            acc_re[...] = jnp.zeros_like(acc_re)
            acc_im[...] = jnp.zeros_like(acc_im)
            for ref in (dbre_ref, dbim_ref, dcre_ref, dcim_ref, dd_ref):
                ref[...] = jnp.zeros_like(ref)

        dy = _join_parts(dy_refs)
        dyb = dy.astype(BF16)
        for hf in range(2):
            cols = slice(hf * 1024, (hf + 1) * 1024)
            buf_re[:, cols] = _dot(dyb[:, hf * 256:(hf + 1) * 256], cre_ref[hf])
            buf_im[:, cols] = -_dot(dyb[:, hf * 256:(hf + 1) * 256], cim_ref[hf])

        for lc in range(SSM_LANES // SCAN_LANE_CHUNK):
            cols = slice(lc * SCAN_LANE_CHUNK, (lc + 1) * SCAN_LANE_CHUNK)
            l_re = jnp.broadcast_to(lre_ref[:, cols], (slab, SCAN_LANE_CHUNK))
            l_im = jnp.broadcast_to(lim_ref[:, cols], (slab, SCAN_LANE_CHUNK))

            def advance(r0, s_re, s_im):
                n_re = l_re * s_re + l_im * s_im + buf_re[pl.ds(r0, slab), cols]
                n_im = l_re * s_im - l_im * s_re + buf_im[pl.ds(r0, slab), cols]
                buf_re[pl.ds(r0, slab), cols] = n_re
                buf_im[pl.ds(r0, slab), cols] = n_im
                return n_re, n_im

            def row0(k):
                return pl.multiple_of((SCAN_TILE_STEPS - 1 - k) * slab, slab)

            @pl.when(jnp.logical_not(second))
            def _():
                s_re, s_im = lax.fori_loop(0, SCAN_TILE_STEPS, lambda k, s: advance(row0(k), *s),
                                           (st_re[:, cols], st_im[:, cols]), unroll=True)
                st_re[:, cols] = s_re
                st_im[:, cols] = s_im

            @pl.when(second)
            def _():
                def step(k, carry):
                    s_re, s_im, a_re, a_im = carry
                    r0 = row0(k)
                    hr = hre_ref[pl.ds(r0, slab), cols].astype(F32)
                    hi = him_ref[pl.ds(r0, slab), cols].astype(F32)
                    a_re = a_re + s_re * hr + s_im * hi
                    a_im = a_im + s_im * hr - s_re * hi
                    return advance(r0, s_re, s_im) + (a_re, a_im)

                zero = jnp.zeros((slab, SCAN_LANE_CHUNK), F32)
                s_re, s_im, a_re, a_im = lax.fori_loop(
                    0, SCAN_TILE_STEPS, step, (st_re[:, cols], st_im[:, cols], zero, zero), unroll=True)
                st_re[:, cols] = s_re
                st_im[:, cols] = s_im
                acc_re[:, cols] += a_re
                acc_im[:, cols] += a_im

        @pl.when(jnp.logical_and(i == n_tiles - 1, jnp.logical_not(second)))
        def _():
            p_re, p_im = _complex_power(lre_ref[...], lim_ref[...], steps)
            _chunk_carry(st_re, st_im, g0_re, g0_im, p_re, -p_im, n_seq, reverse=True)
            st_re[...] = g0_re[...]
            st_im[...] = g0_im[...]

        @pl.when(second)
        def _():
            u = _join_parts(u_refs)
            ub = u.astype(BF16)
            g_re = buf_re[...].astype(BF16)
            g_im = buf_im[...].astype(BF16)
            dd_ref[...] += jnp.sum(dy * u, axis=0, keepdims=True)
            for hf in range(2):
                cols = slice(hf * 1024, (hf + 1) * 1024)
                ycols = slice(hf * 256, (hf + 1) * 256)
                du_half = (_dot_nt(g_re[:, cols], bre_ref[hf]) + _dot_nt(g_im[:, cols], bim_ref[hf])
                           + d_ref[:, ycols] * dy[:, ycols])
                _split_parts(du_refs[2 * hf:2 * hf + 2], du_half)
                for q4 in range(_HALF_GROUPS // 4):
                    ch = slice(hf * 256 + q4 * 64, hf * 256 + (q4 + 1) * 64)
                    st = slice(hf * 1024 + q4 * 256, hf * 1024 + (q4 + 1) * 256)
                    blk = (hf, slice(q4 * 64, (q4 + 1) * 64), slice(q4 * 256, (q4 + 1) * 256))
                    dbre_ref[blk] += _dot_tn(ub[:, ch], g_re[:, st])
                    dbim_ref[blk] += _dot_tn(ub[:, ch], g_im[:, st])
                    dcre_ref[blk] += _dot_tn(dyb[:, ch], hre_ref[:, st])
                    dcim_ref[blk] -= _dot_tn(dyb[:, ch], him_ref[:, st])

        @pl.when(jnp.logical_and(i == n_tiles - 1, second))
        def _():
            dlre_ref[...] = jnp.sum(acc_re[...], axis=0, keepdims=True)
            dlim_ref[...] = jnp.sum(acc_im[...], axis=0, keepdims=True)

    tile = lambda w: pl.BlockSpec((tile_rows, w), lambda p, i: (n_tiles - 1 - i, 0))
    second_tile = lambda w: pl.BlockSpec((tile_rows, w), lambda p, i: (n_tiles - 1 - i * p, 0))
    cm = _full(_CM_SHAPE)
    row = _full((1, SSM_LANES))
    outs = _call(
        body, name="s5_scan_bwd", grid=(2, n_tiles),
        in_specs=[tile(LANES)] * _SCAN_PARTS + [second_tile(LANES)] * _SCAN_PARTS
        + [second_tile(SSM_LANES), second_tile(SSM_LANES), cm, cm, cm, cm, row, row, _full((1, 512))],
        out_specs=[second_tile(LANES)] * _SCAN_PARTS + [cm, cm, cm, cm, row, row, _full((1, 512))],
        out_shape=(_part_shapes(rows) + [_sds(_CM_SHAPE)] * 4 + [_sds((1, SSM_LANES))] * 2 + [_sds((1, 512))]),
        scratch_shapes=[pltpu.VMEM((slab, SSM_LANES), F32)] * 6 + [pltpu.VMEM((tile_rows, SSM_LANES), F32)] * 2,
        compiler_params=_params(48, ("arbitrary", "arbitrary")),
    )(*dy_parts, *u_parts, h_re, h_im, bt_re, bt_im, cm_re, cm_im, lbar_re, lbar_im, d_row)
    return (outs[:_SCAN_PARTS],) + tuple(outs[_SCAN_PARTS:])


def _glu_gate(gl, a, zs):
    return gl * jax.nn.sigmoid(a) * _silu(zs)


def _glu_fwd(y_parts, zs, w_glu, b_glu, n_seq, seq):
    rows = zs.shape[0]
    tm = 512
    slab, steps, _, _ = _scan_geometry(n_seq, seq)

    def body(*refs):
        y_refs, (zs_ref, w_ref, b_ref, o_ref) = refs[:_SCAN_PARTS], refs[_SCAN_PARTS:]
        y = _load_chunks(y_refs, pl.program_id(0) * (tm // steps), tm // steps, steps, slab)
        gl = jax.nn.gelu(y)
        a = _dot(gl.astype(BF16), w_ref[...]) + b_ref[...]
        o_ref[...] = _glu_gate(gl, a, zs_ref[...]).astype(BF16)

    return _call(
        body, name="glu_fwd", grid=(rows // tm,),
        in_specs=_whole_parts(rows) + [_rows(tm, 512), _full((512, 512)), _full((1, 512))],
        out_specs=_rows(tm, 512), out_shape=_sds((rows, 512), BF16),
        compiler_params=_params(32, ("arbitrary",)),
    )(*y_parts, zs, w_glu, b_glu)


def _glu_bwd(y_parts, zs, d_out, w_glu, b_glu, n_seq, seq):
    rows = zs.shape[0]
    tm = 512
    slab, steps, _, _ = _scan_geometry(n_seq, seq)

    def body(*refs):
        y_refs, (zs_ref, d_ref, w_ref, b_ref), refs = refs[:_SCAN_PARTS], refs[_SCAN_PARTS:_SCAN_PARTS + 4], refs[_SCAN_PARTS + 4:]
        dy_refs, (dzs_ref, dw_ref, db_ref) = refs[:_SCAN_PARTS], refs[_SCAN_PARTS:]
        first_chunk = pl.program_id(0) * (tm // steps)

        @pl.when(pl.program_id(0) == 0)
        def _():
            dw_ref[...] = jnp.zeros_like(dw_ref)
            db_ref[...] = jnp.zeros_like(db_ref)

        gl, gelu_vjp = jax.vjp(jax.nn.gelu, _load_chunks(y_refs, first_chunk, tm // steps, steps, slab))
        glb = gl.astype(BF16)
        a = _dot(glb, w_ref[...]) + b_ref[...]
        _, gate_vjp = jax.vjp(_glu_gate, gl, a, zs_ref[...])
        d_gl, d_a, d_zs = gate_vjp(d_ref[...])
        dab = d_a.astype(BF16)
        d_gl = d_gl + _dot_nt(dab, w_ref[...])
        _store_chunks(dy_refs, first_chunk, gelu_vjp(d_gl)[0], steps, slab)
        dzs_ref[...] = d_zs.astype(BF16)
        dw_ref[...] += _dot_tn(glb, dab)
        db_ref[...] += jnp.sum(d_a, axis=0, keepdims=True)

    *dy_parts, dzs, dw, db = _call(
        body, name="glu_bwd", grid=(rows // tm,),
        in_specs=_whole_parts(rows) + [_rows(tm, 512), _rows(tm, 512), _full((512, 512)), _full((1, 512))],
        out_specs=_whole_parts(rows) + [_rows(tm, 512), _full((512, 512)), _full((1, 512))],
        out_shape=_part_shapes(rows) + [_sds((rows, 512), BF16), _sds((512, 512)), _sds((1, 512))],
        compiler_params=_params(40, ("arbitrary",)),
    )(*y_parts, zs, d_out, w_glu, b_glu)
    return dy_parts, dzs, dw, db


_GROUP_ROWS = Q_PER_KV * BLOCK
_BLOCK_SHIFT = BLOCK.bit_length() - 1


def _attn_bias(j):
    query = _iota((BLOCK, _GROUP_ROWS), 1)
    dist_cur = (query & (BLOCK - 1)) - _iota((BLOCK, _GROUP_ROWS), 0)
    dist_prev = dist_cur + BLOCK
    head = query >> _BLOCK_SHIFT
    slope = jnp.zeros((BLOCK, _GROUP_ROWS), F32)
    for g in range(Q_PER_KV):
        slope = jnp.where(head == g, 2.0 ** (-(j * Q_PER_KV + g + 1)), slope)
    bias_cur = jnp.where(dist_cur >= 0, -slope * dist_cur.astype(F32), -jnp.inf)
    bias_prev = jnp.where(dist_prev < WINDOW, -slope * dist_prev.astype(F32), -jnp.inf)
    return bias_cur, bias_prev


_ATTN_BIAS_SCRATCH = pltpu.VMEM((KV_HEADS, 2, BLOCK, _GROUP_ROWS), F32)


def _fill_attn_bias(bias_ref):
    @pl.when(jnp.logical_and(pl.program_id(0) == 0, pl.program_id(1) == 0))
    def _():
        for j in range(KV_HEADS):
            bias_ref[j, 0], bias_ref[j, 1] = _attn_bias(j)


def _stack_heads(x, j):
    heads = range(j * Q_PER_KV, (j + 1) * Q_PER_KV)
    return jnp.concatenate([x[:, h * HEAD_DIM:(h + 1) * HEAD_DIM] for h in heads], axis=0)


def _head_rows(x, j):
    heads = range(j * Q_PER_KV, (j + 1) * Q_PER_KV)
    return jnp.concatenate([x[h:h + 1, :] for h in heads], axis=1)


def _sink_row(sk_ref, j):
    heads = range(j * Q_PER_KV, (j + 1) * Q_PER_KV)
    return jnp.concatenate([jnp.broadcast_to(sk_ref[0:1, h:h + 1], (1, BLOCK)) for h in heads], axis=1)


_ATTN_FWD_BLOCKS = 8


def _attn_fwd(q, k, v, za, sinks, n_seq, seq):
    nb = seq // BLOCK
    steps = nb // _ATTN_FWD_BLOCKS
    rows = q.shape[0]

    def body(q_ref, kc_ref, kp_ref, vc_ref, vp_ref, za_ref, sk_ref, o_ref, ao_ref, lse_ref, bias_ref):
        _fill_attn_bias(bias_ref)
        for t in range(_ATTN_FWD_BLOCKS):
            at = slice(t * BLOCK, (t + 1) * BLOCK)
            before = slice((t - 1) * BLOCK, t * BLOCK)
            q_all = q_ref[at, :]
            for j in range(KV_HEADS):
                js = slice(j * HEAD_DIM, (j + 1) * HEAD_DIM)
                bias_c, bias_p = bias_ref[j, 0], bias_ref[j, 1]
                q4 = _stack_heads(q_all, j)
                sc = _dot_nt(kc_ref[at, js], q4) + bias_c
                if t == 0:
                    sp = _dot_nt(kp_ref[:, js], q4) + jnp.where(pl.program_id(1) > 0, bias_p, -jnp.inf)
                    v_prev = vp_ref[:, js]
                else:
                    sp = _dot_nt(kc_ref[before, js], q4) + bias_p
                    v_prev = vc_ref[before, js]
                sink = _sink_row(sk_ref, j)
                m = jnp.maximum(jnp.max(jnp.maximum(sc, sp), axis=0, keepdims=True), sink)
                ec = jnp.exp(sc - m)
                ep = jnp.exp(sp - m)
                den = jnp.sum(ec + ep, axis=0, keepdims=True) + jnp.exp(sink - m)
                inv = 1.0 / den
                o4 = _dot_tn((ec * inv).astype(BF16), vc_ref[at, js]) + _dot_tn((ep * inv).astype(BF16), v_prev)
                lse4 = m + jnp.log(den)
                for g in range(Q_PER_KV):
                    h = j * Q_PER_KV + g
                    o_ref[at, h * HEAD_DIM:(h + 1) * HEAD_DIM] = o4[g * BLOCK:(g + 1) * BLOCK]
                    lse_ref[t * N_HEADS + h:t * N_HEADS + h + 1, :] = lse4[:, g * BLOCK:(g + 1) * BLOCK]
        ao_ref[...] = (o_ref[...] * _silu(za_ref[...])).astype(BF16)

    cur = lambda w: pl.BlockSpec((_ATTN_FWD_BLOCKS * BLOCK, w), lambda b, n: (b * steps + n, 0))
    prev = lambda w: pl.BlockSpec((BLOCK, w), lambda b, n: (b * nb + jnp.maximum(_ATTN_FWD_BLOCKS * n - 1, 0), 0))
    lse_rows = rows // BLOCK * N_HEADS
    return _call(
        body, name="attn_fwd", grid=(n_seq, steps),
        in_specs=[cur(512), cur(128), prev(128), cur(128), prev(128), cur(512), _full((1, N_HEADS))],
        out_specs=[cur(512), cur(512),
                   pl.BlockSpec((_ATTN_FWD_BLOCKS * N_HEADS, BLOCK), lambda b, n: (b * steps + n, 0))],
        out_shape=[_sds((rows, 512)), _sds((rows, 512), BF16), _sds((lse_rows, BLOCK))],
        scratch_shapes=[_ATTN_BIAS_SCRATCH], compiler_params=_params(32, ("arbitrary", "arbitrary")),
    )(q, k, k, v, v, za, sinks)


_ATTN_BWD_BLOCKS = 4


def _attn_bwd(q, k, v, za, o, lse, d_ao, sinks, n_seq, seq):
    nb = seq // BLOCK
    per_step = _ATTN_BWD_BLOCKS
    steps = nb // per_step
    rows = q.shape[0]

    def body(q_ref, kc_ref, kp_ref, vc_ref, vp_ref, za_ref, o_ref, lse_ref, d_ref, sk_ref,
             dq_ref, dk_ref, dv_ref, dza_ref, dsk_ref, bias_ref, dk_carry, dv_carry):
        step = pl.program_id(1)
        first_block = nb - per_step * (step + 1)
        _fill_attn_bias(bias_ref)

        @pl.when(jnp.logical_and(pl.program_id(0) == 0, step == 0))
        def _():
            dsk_ref[...] = jnp.zeros_like(dsk_ref)
            dk_carry[...] = jnp.zeros_like(dk_carry)
            dv_carry[...] = jnp.zeros_like(dv_carry)

        _, gate_vjp = jax.vjp(lambda o_, z_: o_ * _silu(z_), o_ref[...], za_ref[...])
        d_o, d_za = gate_vjp(d_ref[...])
        dza_ref[...] = d_za.astype(BF16)

        for j in range(KV_HEADS):
            js = slice(j * HEAD_DIM, (j + 1) * HEAD_DIM)
            bias_c, bias_p = bias_ref[j, 0], bias_ref[j, 1]
            sink = _sink_row(sk_ref, j)
            sink_loss = jnp.zeros((1, _GROUP_ROWS), F32)
            dk_from_next = jnp.where(step > 0, dk_carry[j], 0.0)
            dv_from_next = jnp.where(step > 0, dv_carry[j], 0.0)
            for t in reversed(range(per_step)):
                at = slice(t * BLOCK, (t + 1) * BLOCK)
                kc, vc = kc_ref[at, js], vc_ref[at, js]
                if t > 0:
                    before = slice((t - 1) * BLOCK, t * BLOCK)
                    kp, vp, bias_before = kc_ref[before, js], vc_ref[before, js], bias_p
                else:
                    kp, vp, bias_before = kp_ref[:, js], vp_ref[:, js], jnp.where(first_block > 0, bias_p, -jnp.inf)
                q4 = _stack_heads(q_ref[at, :], j)
                do4b = _stack_heads(d_o[at], j).astype(BF16)
                lse4 = _head_rows(lse_ref[t * N_HEADS:(t + 1) * N_HEADS, :], j)
                pc = jnp.exp(_dot_nt(kc, q4) + bias_c - lse4)
                pp = jnp.exp(_dot_nt(kp, q4) + bias_before - lse4)
                dpc = _dot_nt(vc, do4b)
                dpp = _dot_nt(vp, do4b)
                delta = jnp.sum(pc * dpc + pp * dpp, axis=0, keepdims=True)
                dsc = (pc * (dpc - delta)).astype(BF16)
                dsp = (pp * (dpp - delta)).astype(BF16)
                dq4 = ((_dot_tn(dsc, kc) + _dot_tn(dsp, kp)) * ATTN_SCALE).astype(BF16)
                sink_loss = sink_loss + jnp.exp(sink - lse4) * delta
                for g in range(Q_PER_KV):
                    h = j * Q_PER_KV + g
                    dq_ref[at, h * HEAD_DIM:(h + 1) * HEAD_DIM] = dq4[g * BLOCK:(g + 1) * BLOCK]
                dk_ref[at, js] = (_dot(dsc, q4) + dk_from_next).astype(BF16)
                dv_ref[at, js] = (_dot(pc.astype(BF16), do4b) + dv_from_next).astype(BF16)
                dk_from_next = _dot(dsp, q4)
                dv_from_next = _dot(pp.astype(BF16), do4b)
            dk_carry[j] = dk_from_next
            dv_carry[j] = dv_from_next
            for g in range(Q_PER_KV):
                h = j * Q_PER_KV + g
                dsk_ref[0:1, h:h + 1] -= jnp.sum(sink_loss[:, g * BLOCK:(g + 1) * BLOCK], axis=1, keepdims=True)

    cur = lambda w: pl.BlockSpec((per_step * BLOCK, w), lambda b, s: (b * steps + steps - 1 - s, 0))
    prev = lambda w: pl.BlockSpec((BLOCK, w), lambda b, s: (b * nb + jnp.maximum(nb - per_step * (s + 1) - 1, 0), 0))
    return _call(
        body, name="attn_bwd", grid=(n_seq, steps),
        in_specs=[cur(512), cur(128), prev(128), cur(128), prev(128), cur(512), cur(512),
                  pl.BlockSpec((per_step * N_HEADS, BLOCK), lambda b, s: (b * steps + steps - 1 - s, 0)), cur(512),
                  _full((1, N_HEADS))],
        out_specs=[cur(512), cur(128), cur(128), cur(512), _full((1, N_HEADS))],
        out_shape=[_sds((rows, 512), BF16), _sds((rows, 128), BF16), _sds((rows, 128), BF16),
                   _sds((rows, 512), BF16), _sds((1, N_HEADS))],
        scratch_shapes=[_ATTN_BIAS_SCRATCH, pltpu.VMEM((KV_HEADS, BLOCK, HEAD_DIM), F32),
                        pltpu.VMEM((KV_HEADS, BLOCK, HEAD_DIM), F32)],
        compiler_params=_params(32, ("arbitrary", "arbitrary")),
    )(q, k, k, v, v, za, o, lse, d_ao, sinks)


def _tail(ssm_out, attn_out, x2d, p2d, target, w_out, g2, w_gate, b_gate, w_proj):
    rows = x2d.shape[0]
    tm = 512

    def body(so_ref, ao_ref, x_ref, p_ref, t_ref, wo_ref, g2_ref, wg_ref, bg_ref, wp_ref,
             dh1_ref, dso_ref, dao_ref, dwo_ref, dwg_ref, dwp_ref, dbg_ref, dg2_ref, loss_ref):
        @pl.when(pl.program_id(0) == 0)
        def _():
            for ref in (dwo_ref, dwg_ref, dwp_ref, dbg_ref, dg2_ref, loss_ref):
                ref[...] = jnp.zeros_like(ref)

        cat = jnp.concatenate([so_ref[...], ao_ref[...]], axis=1)
        g2 = g2_ref[...]
        mixed = _dot(cat, wo_ref[...])
        r = lax.rsqrt(jnp.mean(mixed * mixed, axis=-1, keepdims=True) + EPS)
        mr = mixed * r
        h1 = x_ref[...] + mr * g2
        h1b = h1.astype(BF16)
        gate = jax.nn.sigmoid(_dot(h1b, wg_ref[...]) + bg_ref[...])
        pb = p_ref[...].astype(BF16)
        wp_blocks = [slice(j * D_PLE, (j + 1) * D_PLE) for j in range(N_CHIPS)]
        pp = jnp.concatenate([_dot(pb, wp_ref[blk, :]) for blk in wp_blocks], axis=1)
        err = h1 + gate * pp - t_ref[...]
        loss_ref[...] += 0.5 * jnp.sum(jnp.mean(err * err, axis=-1, keepdims=True), axis=0, keepdims=True)

        dh2 = err * (1.0 / D_MODEL)
        d_glin = dh2 * pp * gate * (1.0 - gate)
        d_glin_b = d_glin.astype(BF16)
        dwg_ref[...] += _dot_tn(h1b, d_glin_b)
        dbg_ref[...] += jnp.sum(d_glin, axis=0, keepdims=True)
        d_pp = (dh2 * gate).astype(BF16)
        for blk in wp_blocks:
            dwp_ref[blk, :] += _dot_tn(pb, d_pp[:, blk])
        dh1 = dh2 + _dot_nt(d_glin_b, wg_ref[...])
        dh1_ref[...] = dh1
        dg2_ref[...] += jnp.sum(dh1 * mr, axis=0, keepdims=True)
        a_ = dh1 * g2
        d_mixed = (r * a_ - mr * (r * jnp.mean(a_ * mr, axis=-1, keepdims=True))).astype(BF16)
        dwo_ref[...] += _dot_tn(cat, d_mixed)
        d_cat = _dot_nt(d_mixed, wo_ref[...])
        dso_ref[...] = d_cat[:, 0:512]
        dao_ref[...] = d_cat[:, 512:1024]

    return _call(
        body, name="tail_fwd_bwd", grid=(rows // tm,),
        in_specs=[_rows(tm, 512), _rows(tm, 512), _rows(tm, D_MODEL), _rows(tm, D_PLE), _rows(tm, D_MODEL),
                  _full((D_MODEL, D_MODEL)), _full((1, D_MODEL)), _full((D_MODEL, D_MODEL)), _full((1, D_MODEL)),
                  _full((N_CHIPS * D_PLE, D_PLE))],
        out_specs=[_rows(tm, D_MODEL), _rows(tm, 512), _rows(tm, 512), _full((D_MODEL, D_MODEL)),
                   _full((D_MODEL, D_MODEL)), _full((N_CHIPS * D_PLE, D_PLE)), _full((1, D_MODEL)), _full((1, D_MODEL)),
                   _full((1, 1))],
        out_shape=[_sds((rows, D_MODEL)), _sds((rows, 512)), _sds((rows, 512)), _sds((D_MODEL, D_MODEL)),
                   _sds((D_MODEL, D_MODEL)), _sds((N_CHIPS * D_PLE, D_PLE)), _sds((1, D_MODEL)), _sds((1, D_MODEL)),
                   _sds((1, 1))],
        compiler_params=_params(52, ("arbitrary",)),
    )(ssm_out, attn_out, x2d, p2d, target, w_out, g2, w_gate, b_gate, w_proj)


def _local_step(x, hn, p, target, pre_norm_g, w_in_t, s5_params, s5_operands, ssm_d, w_glu, b_glu, sinks, w_out,
                post_norm_g, w_proj, w_gate, b_gate, send_tail_grads=lambda ready: ready["w_out"],
                send_bc=lambda d_bc: (d_bc, d_bc)):
    n_seq, seq, _ = x.shape
    rows = n_seq * seq
    x2d = x.reshape(rows, D_MODEL)
    p2d = p.reshape(rows, D_PLE)
    t2d = target.reshape(rows, D_MODEL)

    l_re, l_im, bt_re, bt_im, cm_re, cm_im = s5_operands

    u_scan, zs, q, k, v, za = _in_proj(hn, w_in_t, n_seq, seq)
    y_scan, h_re, h_im = _s5_scan_fwd(u_scan, bt_re, bt_im, cm_re, cm_im, l_re, l_im, ssm_d, n_seq, seq)
    ssm_out = _glu_fwd(y_scan, zs, w_glu, b_glu, n_seq, seq)
    o, attn_out, lse = _attn_fwd(q, k, v, za, sinks, n_seq, seq)

    dh1, d_so, d_ao, d_w_out, d_w_gate, d_w_proj, d_b_gate, d_g2, loss = _tail(
        ssm_out, attn_out, x2d, p2d, t2d, w_out, post_norm_g, w_gate, b_gate, w_proj)

    dq, dk, dv, dza, d_sinks = _attn_bwd(q, k, v, za, o, lse, d_ao, sinks, n_seq, seq)
    dy_scan, dzs, d_w_glu, d_b_glu = _glu_bwd(y_scan, zs, d_so, w_glu, b_glu, n_seq, seq)
    du_scan, d_bt_re, d_bt_im, d_cm_re, d_cm_im, d_l_re, d_l_im, d_d = _s5_scan_bwd(
        dy_scan, u_scan, h_re, h_im, bt_re, bt_im, cm_re, cm_im, l_re, l_im, ssm_d, n_seq, seq)
    tail_grads_arrived = send_tail_grads(dict(w_out=d_w_out, pl_w_gate=d_w_gate, pl_w_proj=d_w_proj))
    d_lam_re, d_lam_im, d_log_step, d_bc = _s5_params_bwd(
        s5_params, (d_l_re, d_l_im, d_bt_re, d_bt_im, d_cm_re, d_cm_im), tail_grads_arrived)

    sent, arrived = send_bc(d_bc)
    d_proj, d_w_in_early, d_w_in_early_b = _in_proj_bwd_early(hn, du_scan, dzs, dq, dk, dv, dza, sent, n_seq, seq)
    grad_x, d_w_in_late, d_g1 = _in_proj_bwd(x2d, dh1, pre_norm_g, w_in_t, d_proj, arrived)
    grads = dict(
        pre_norm_g=d_g1, w_in_early=d_w_in_early, w_in_early_bf16=d_w_in_early_b, w_in_late=d_w_in_late,
        ssm_lam_re=d_lam_re, ssm_lam_im=d_lam_im, ssm_log_step=d_log_step, ssm_bc=d_bc, ssm_d=d_d, ssm_w_glu=d_w_glu,
        ssm_b_glu=d_b_glu, attn_sinks=d_sinks, w_out=d_w_out, post_norm_g=d_g2, pl_w_proj=d_w_proj,
        pl_w_gate=d_w_gate, pl_b_gate=d_b_gate)
    return grad_x.reshape(x.shape), loss, grads


_BIG = ("w_in", "ssm_w_glu", "w_out", "pl_w_proj", "pl_w_gate")
_BIG_SHARD = {"w_in": (D_IN // N_CHIPS, D_MODEL), "ssm_w_glu": (D_SSM // N_CHIPS, D_SSM),
              "w_out": (D_MODEL // N_CHIPS, D_MODEL), "pl_w_proj": (D_PLE, D_MODEL // N_CHIPS),
              "pl_w_gate": (D_MODEL // N_CHIPS, D_MODEL)}
_SMALL = {"pre_norm_g": (1, D_MODEL), "ssm_lam_re": (SSM_GROUPS, SSM_STATE), "ssm_lam_im": (SSM_GROUPS, SSM_STATE),
          "ssm_log_step": (1, SSM_GROUPS), "ssm_b_re": (D_SSM, SSM_STATE), "ssm_b_im": (D_SSM, SSM_STATE),
          "ssm_c_re": (D_SSM, SSM_STATE), "ssm_c_im": (D_SSM, SSM_STATE), "ssm_d": (1, D_SSM), "ssm_b_glu": (1, D_SSM),
          "attn_sinks": (1, N_HEADS), "post_norm_g": (1, D_MODEL), "pl_b_gate": (1, D_MODEL)}
_VEC_ROWS = ("pre_norm_g", "post_norm_g", "pl_b_gate", "ssm_d", "ssm_b_glu", "attn_sinks", "ssm_log_step", "loss")
_SMALL_GROUPS = (
    ("vec", (8, D_MODEL), tuple((name, r) for r, name in enumerate(_VEC_ROWS))),
    ("lam", (2 * SSM_GROUPS, SSM_STATE), (("ssm_lam_re", 0), ("ssm_lam_im", SSM_GROUPS))),
)
_SMALL_EARLY = ("ssm_b_re", "ssm_b_im", "ssm_c_re", "ssm_c_im")
_SMALL_ORDER = tuple(name for _, _, members in _SMALL_GROUPS for name, _ in members) + _SMALL_EARLY
_WEIGHT_ORDER = ("pre_norm_g", "w_in", "ssm_lam_re", "ssm_lam_im", "ssm_log_step", "ssm_b_re", "ssm_b_im", "ssm_c_re",
                 "ssm_c_im", "ssm_d", "ssm_w_glu", "ssm_b_glu", "attn_sinks", "w_out", "post_norm_g", "pl_w_proj",
                 "pl_w_gate", "pl_b_gate")


def _small_shape(name):
    return (1, 1) if name == "loss" else _SMALL[name]


def _to_kernel_form(name, a):
    a = a[0]
    if name == "w_in":
        return a.T
    if name in ("ssm_b_re", "ssm_b_im"):
        a = a.transpose(0, 2, 1)
    return a.reshape(_SMALL[name]) if name in _SMALL else a


def _from_kernel_form(name, a, shape):
    if name == "w_in":
        a = a.T
    if name in ("ssm_b_re", "ssm_b_im"):
        a = a.reshape(SSM_GROUPS, SSM_GROUP_CH, SSM_STATE).transpose(0, 2, 1)
    return a.reshape(shape)


def _mesh_place():
    x, y, c = lax.axis_index("x"), lax.axis_index("y"), lax.axis_index("c")
    other_chips = ((1 - x, y), (x, 1 - y), (1 - x, 1 - y))
    return x, y, c, other_chips


def _gather_copies(s_refs, g_refs, send_sems, recv_sems, local_sems):
    x, y, c, other_chips = _mesh_place()
    started = []
    for i, (s_ref, g_ref) in enumerate(zip(s_refs, g_refs)):
        rows = s_ref.shape[0]
        half = rows // 2

        def block(chip, g_ref=g_ref, rows=rows, half=half):
            return g_ref.at[pl.ds((2 * chip[0] + chip[1]) * rows + c * half, half), :]

        def copy(k, chip, to, src=None, i=i, block=block):
            return pltpu.make_async_remote_copy(
                src_ref=block(chip) if src is None else src, dst_ref=block(chip), send_sem=send_sems.at[6 * i + k],
                recv_sem=recv_sems.at[6 * i + k], device_id=to, device_id_type=MESH)

        own = pltpu.make_async_copy(s_ref, g_ref.at[pl.ds((2 * x + y) * rows, rows), :], local_sems.at[i])
        own.start()
        first = [copy(k, (x, y), (*chip, c), src=s_ref.at[pl.ds(c * half, half), :])
                 for k, chip in enumerate(other_chips)]
        for cp in first:
            cp.start()
        passed = [copy(3 + k, chip, (x, y, 1 - c)) for k, chip in enumerate(other_chips)]
        started.append((own, first, passed))
    for own, first, passed in started:
        for k in range(3):
            first[k].wait_recv()
            passed[k].start()
    for own, first, passed in started:
        for k in range(3):
            passed[k].wait_recv()
        for cp in first + passed:
            cp.wait_send()
        own.wait()


def _gather_semaphores(n_t):
    return [pltpu.SemaphoreType.DMA((6 * n_t,)), pltpu.SemaphoreType.DMA((6 * n_t,)), pltpu.SemaphoreType.DMA((n_t,))]


def _gather_weights_beside(shards, name, collective_id):
    n_t = len(shards)
    hbm = pltpu.MemorySpace.HBM
    s_refs = [jax.new_ref(s, memory_space=hbm) for s in shards]
    g_refs = [jax.empty_ref(jax.ShapeDtypeStruct((N_CHIPS * s.shape[0], s.shape[1]), s.dtype), memory_space=hbm)
              for s in shards]

    def launch(send_sems, recv_sems, local_sems):
        x, y, c, other_chips = _mesh_place()
        peers = [(*chip, c) for chip in other_chips] + [(x, y, 1 - c)]
        barrier = pltpu.get_barrier_semaphore()
        for peer in peers:
            pl.semaphore_signal(barrier, inc=1, device_id=peer, device_id_type=MESH)
        pl.semaphore_wait(barrier, len(peers))
        _gather_copies(s_refs, g_refs, send_sems, recv_sems, local_sems)

    pl.kernel(launch, mesh=plsc.ScalarSubcoreMesh(axis_name="sequencer", num_cores=1), name=name,
              scratch_types=_gather_semaphores(n_t), compiler_params=pltpu.CompilerParams(collective_id=collective_id))()
    return [g[...] for g in g_refs]


_RELATIONS = tuple(((r >> 2) & 1, (r >> 1) & 1, r & 1) for r in range(1, 8))


def _related(place, relation):
    return tuple(1 - a if flip else a for a, flip in zip(place, relation))


def _scatter_beside(mats, name, collective_id):
    hbm = pltpu.MemorySpace.HBM
    src_refs = [jax.new_ref(a, memory_space=hbm) for a in mats]
    land_refs = [jax.empty_ref(jax.ShapeDtypeStruct((7, a.shape[0] // 8, a.shape[1]), a.dtype), memory_space=hbm)
                 for a in mats]

    def launch(send_sems, recv_sems):
        me = (lax.axis_index("x"), lax.axis_index("y"), lax.axis_index("c"))
        peers = [_related(me, rel) for rel in _RELATIONS]
        barrier = pltpu.get_barrier_semaphore()
        for peer in peers:
            pl.semaphore_signal(barrier, inc=1, device_id=peer, device_id_type=MESH)
        pl.semaphore_wait(barrier, len(peers))
        copies = []
        for i, (src, land) in enumerate(zip(src_refs, land_refs)):
            hr = land.shape[1]
            for k, (tx, ty, tc) in enumerate(peers):
                rows = pl.ds((2 * tx + ty) * 2 * hr + tc * hr, hr)
                copies.append(pltpu.make_async_remote_copy(
                    src_ref=src.at[rows, :], dst_ref=land.at[k], send_sem=send_sems.at[7 * i + k],
                    recv_sem=recv_sems.at[7 * i + k], device_id=(tx, ty, tc), device_id_type=MESH))
                copies[-1].start()
        for cp in copies:
            cp.wait()

    n_sems = 7 * len(mats)
    pl.kernel(launch, mesh=plsc.ScalarSubcoreMesh(axis_name="sequencer", num_cores=1), name=name,
              scratch_types=[pltpu.SemaphoreType.DMA((n_sems,)), pltpu.SemaphoreType.DMA((n_sems,))],
              compiler_params=pltpu.CompilerParams(collective_id=collective_id))()
    return [ref[...] for ref in land_refs]


def _broadcast_beside(arrays):
    hbm = pltpu.MemorySpace.HBM
    src_refs = [jax.new_ref(a, memory_space=hbm) for a in arrays]
    land_refs = [jax.empty_ref(jax.ShapeDtypeStruct((len(_RELATIONS),) + a.shape, a.dtype), memory_space=hbm)
                 for a in arrays]

    def launch(send_sems, recv_sems):
        me = (lax.axis_index("x"), lax.axis_index("y"), lax.axis_index("c"))
        peers = [_related(me, rel) for rel in _RELATIONS]
        barrier = pltpu.get_barrier_semaphore()
        for peer in peers:
            pl.semaphore_signal(barrier, inc=1, device_id=peer, device_id_type=MESH)
        pl.semaphore_wait(barrier, len(peers))
        copies = []
        for i, (src, land) in enumerate(zip(src_refs, land_refs)):
            for k, peer in enumerate(peers):
                copies.append(pltpu.make_async_remote_copy(
                    src_ref=src, dst_ref=land.at[k], send_sem=send_sems.at[7 * i + k],
                    recv_sem=recv_sems.at[7 * i + k], device_id=peer, device_id_type=MESH))
                copies[-1].start()
        for cp in copies:
            cp.wait()

    n_sems = 7 * len(arrays)
    pl.kernel(launch, mesh=plsc.ScalarSubcoreMesh(axis_name="sequencer", num_cores=1), name="broadcast_beside",
              scratch_types=[pltpu.SemaphoreType.DMA((n_sems,)), pltpu.SemaphoreType.DMA((n_sems,))],
              compiler_params=pltpu.CompilerParams(collective_id=3))()
    return [ref[...] for ref in land_refs]


def _exchange_grads(big, outputs, small, landed, own_bc, landed_bc):
    n_t = len(big)
    n_g = len(_SMALL_GROUPS)
    names = _SMALL_ORDER
    halves = [(b.shape[0] // N_CHIPS // 2, b.shape[1]) for b in big]
    early = sorted(landed)
    late = [i for i in range(n_t) if i not in landed]
    n_sems = 4 * n_g + 7 * len(late) + n_t
    small_sem0, block_sem0 = n_t, n_t + len(names)
    early_sem0 = block_sem0 + N_CHIPS * len(late)
    landed_sem0 = early_sem0 + 2 * len(early)
    sent = [n for n in names if n in small]

    def body(*refs):
        pos = 0

        def take(n):
            nonlocal pos
            pos += n
            return refs[pos - n:pos]

        big_refs, small_refs = take(n_t), dict(zip(sent, take(len(sent))))
        land_refs = dict(zip(early, take(len(early))))
        own_bc_ref, landed_bc_ref = take(2)
        out_refs, small_out_refs = take(len(outputs)), dict(zip(names, take(len(names))))
        per_late = lambda: dict(zip(late, take(len(late))))
        ga, gb, pme, send_b, recv_b = per_late(), per_late(), take(n_t), per_late(), per_late()
        own_e, land_e = dict(zip(early, take(len(early)))), dict(zip(early, take(len(early))))
        own_s, land_s = take(2)
        s_own, s_sib, s_chips, s_pair = take(n_g), take(n_g), take(n_g), take(n_g)
        stage = dict(zip(names, take(len(names))))
        send_sems, recv_sems, local_sems = take(3)
        x, y, c, other_chips = _mesh_place()
        me = 2 * x + y
        sibling = (x, y, 1 - c)
        sem_at = iter(range(n_sems))

        def remote(src, dst, to):
            k = next(sem_at)
            return pltpu.make_async_remote_copy(src_ref=src, dst_ref=dst, send_sem=send_sems.at[k],
                                                recv_sem=recv_sems.at[k], device_id=to, device_id_type=MESH)

        loads = [pltpu.make_async_copy(small_refs[name], stage[name], local_sems.at[small_sem0 + names.index(name)])
                 for name in sent]
        landed_loads = [pltpu.make_async_copy(own_bc_ref, own_s, local_sems.at[landed_sem0]),
                        pltpu.make_async_copy(landed_bc_ref, land_s, local_sems.at[landed_sem0 + 1])]
        for cp in loads + landed_loads:
            cp.start()
        for cp in loads:
            cp.wait()
        small_swaps = []
        for gi, (_, _, members) in enumerate(_SMALL_GROUPS):
            s_own[gi][...] = jnp.zeros_like(s_own[gi])
            for name, r0 in members:
                r, n = _small_shape(name)
                s_own[gi][r0:r0 + r, 0:n] = stage[name][...]
            small_swaps.append(remote(s_own[gi], s_sib[gi], sibling))
            small_swaps[gi].start()
        order = sorted(late, key=lambda i: halves[i][0] * halves[i][1])
        own_loads, big_swaps = {}, {}
        for i in order:
            hr = halves[i][0]
            own_loads[i], big_swaps[i] = [], []
            for j in range(N_CHIPS):
                mine = big_refs[i].at[pl.ds(j * 2 * hr + c * hr, hr), :]
                theirs = big_refs[i].at[pl.ds(j * 2 * hr + (1 - c) * hr, hr), :]
                sem = local_sems.at[block_sem0 + N_CHIPS * late.index(i) + j]
                own_loads[i].append(pltpu.make_async_copy(mine, ga[i].at[j], sem))
                own_loads[i][j].start()
                big_swaps[i].append(remote(theirs, gb[i].at[j], sibling))
                big_swaps[i][j].start()
        early_loads = {}
        for e, i in enumerate(early):
            hr = halves[i][0]
            mine = big_refs[i].at[pl.ds(me * 2 * hr + c * hr, hr), :]
            early_loads[i] = [pltpu.make_async_copy(mine, own_e[i], local_sems.at[early_sem0 + 2 * e]),
                              pltpu.make_async_copy(land_refs[i], land_e[i], local_sems.at[early_sem0 + 2 * e + 1])]
            for cp in early_loads[i]:
                cp.start()
        small_sends = []
        for gi in range(n_g):
            small_swaps[gi].wait_recv()
            s_pair[gi][...] = s_own[gi][...] + s_sib[gi][...]
            small_sends.append([remote(s_pair[gi], s_chips[gi].at[k], (*chip, c)) for k, chip in enumerate(other_chips)])
            for cp in small_sends[gi]:
                cp.start()

        def pair_sum(i, j):
            return ga[i][j] + gb[i][j]

        big_sends = {}
        for i in order:
            for j in range(N_CHIPS):
                own_loads[i][j].wait()
                big_swaps[i][j].wait_recv()
            big_sends[i] = []
            for k, chip in enumerate(other_chips):
                send_b[i][k] = pair_sum(i, 2 * chip[0] + chip[1]).astype(BF16)
                big_sends[i].append(remote(send_b[i].at[k], recv_b[i].at[k], (*chip, c)))
                big_sends[i][k].start()
        last_swaps, keeps = {}, {}
        for i in early + order:
            hr = halves[i][0]
            if i in landed:
                for cp in early_loads[i]:
                    cp.wait()
                total = own_e[i][...]
                for k in range(len(_RELATIONS)):
                    total = total + land_e[i][k].astype(F32)
                pme[i][...] = total
            else:
                for k in range(3):
                    big_sends[i][k].wait_recv()
                pme[i][...] = ((pair_sum(i, me) + recv_b[i][0].astype(F32)) + recv_b[i][1].astype(F32)) + recv_b[i][2].astype(F32)
            o, = [o for o, group in enumerate(outputs) if i in group]
            first_col = sum(halves[j][1] for j in outputs[o][:outputs[o].index(i)])
            mine = out_refs[o].at[pl.ds(c * hr, hr), pl.ds(first_col, halves[i][1])]
            keeps[i] = pltpu.make_async_copy(pme[i], mine, local_sems.at[i])
            keeps[i].start()
            last_swaps[i] = remote(pme[i], mine, sibling)
            last_swaps[i].start()

        for gi, (_, _, members) in enumerate(_SMALL_GROUPS):
            for k in range(3):
                small_sends[gi][k].wait_recv()
            total = None
            for j in range(N_CHIPS):
                rel = jnp.bitwise_xor(j, me)
                term = jnp.where(rel == 0, s_pair[gi][...], jnp.where(
                    rel == 2, s_chips[gi][0], jnp.where(rel == 1, s_chips[gi][1], s_chips[gi][2])))
                total = term if total is None else total + term
            s_sib[gi][...] = total
            for name, r0 in members:
                r, n = _small_shape(name)
                stage[name][...] = s_sib[gi][r0:r0 + r, 0:n]
        my_index = 4 * x + 2 * y + c
        for cp in landed_loads:
            cp.wait()
        total = None
        for d in range(2 * N_CHIPS):
            rel = jnp.bitwise_xor(d, my_index)
            term = own_s[...]
            for k in range(len(_RELATIONS)):
                term = jnp.where(rel == k + 1, land_s[k], term)
            total = term.astype(F32) if total is None else total + term.astype(F32)
        for a, name in enumerate(_SMALL_EARLY):
            stage[name][...] = total[:, a * SSM_STATE:(a + 1) * SSM_STATE]
        stores = [pltpu.make_async_copy(stage[name], small_out_refs[name], local_sems.at[small_sem0 + a])
                  for a, name in enumerate(names)]
        for cp in stores:
            cp.start()

        for i in range(n_t):
            last_swaps[i].wait_recv()
            keeps[i].wait()
        for cp in stores:
            cp.wait()
        groups = list(big_swaps.values()) + small_sends + list(big_sends.values())
        for cp in small_swaps + [cp for group in groups for cp in group] + list(last_swaps.values()):
            cp.wait_send()

    any_spec = pl.BlockSpec(memory_space=pl.ANY)
    small_shapes = [_sds(_small_shape(n)) for n in names]
    group_shapes = [shape for _, shape, _ in _SMALL_GROUPS]
    vmem = lambda which, dtype, lead=(): [pltpu.VMEM(lead + halves[i], dtype) for i in which]
    outs = _call(
        body, name="exchange_grads",
        in_specs=[any_spec] * (n_t + len(sent) + len(early) + 2),
        out_specs=[any_spec] * (len(outputs) + len(names)),
        out_shape=[_sds((big[group[0]].shape[0] // N_CHIPS, sum(big[i].shape[1] for i in group))) for group in outputs]
        + small_shapes,
        scratch_shapes=(vmem(late, F32, (N_CHIPS,)) + vmem(late, F32, (N_CHIPS,)) + vmem(range(n_t), F32)
                        + vmem(late, BF16, (3,)) + vmem(late, BF16, (3,))
                        + vmem(early, F32)
                        + [pltpu.VMEM((len(_RELATIONS),) + halves[i], landed[i].dtype) for i in early]
                        + [pltpu.VMEM(own_bc.shape, own_bc.dtype), pltpu.VMEM(landed_bc.shape, landed_bc.dtype)]
                        + [pltpu.VMEM(s, F32) for s in group_shapes] * 2 + [pltpu.VMEM((3,) + s, F32) for s in group_shapes]
                        + [pltpu.VMEM(s, F32) for s in group_shapes]
                        + [pltpu.VMEM(_small_shape(n), F32) for n in names]
                        + [pltpu.SemaphoreType.DMA((n_sems,)), pltpu.SemaphoreType.DMA((n_sems,)),
                           pltpu.SemaphoreType.DMA((landed_sem0 + 2,))]),
        compiler_params=_params(48),
    )(*big, *[small[n] for n in sent], *[landed[i] for i in early], own_bc, landed_bc)
    return list(outs[:len(outputs)]), dict(zip(names, outs[len(outputs):]))


def _adamw_update(w, g, m, v):
    m = ADAM_B1 * m + (1.0 - ADAM_B1) * g
    v = ADAM_B2 * v + (1.0 - ADAM_B2) * (g * g)
    m_hat = m / (1.0 - ADAM_B1 ** ADAM_STEP)
    v_hat = v / (1.0 - ADAM_B2 ** ADAM_STEP)
    return -ADAM_LR * (m_hat / (jnp.sqrt(v_hat) + ADAM_EPS) + ADAM_WD * w), m, v


def _adamw(w, g, m, v, grid, name):
    n_t = len(w)

    def body(*refs):
        ins, outs = refs[:4 * n_t], refs[4 * n_t:]
        for i in range(n_t):
            w_, g_, m_, v_ = [ins[a * n_t + i][...] for a in range(4)]
            vals = (g_,) + _adamw_update(w_, g_, m_, v_)
            for a in range(4):
                outs[a * n_t + i][...] = vals[a]

    specs = [pl.BlockSpec((a.shape[0] // grid, a.shape[1]), lambda i: (i, 0)) for a in w]
    shapes = [_sds(a.shape) for a in w]
    outs = _call(
        body, name=name, grid=(grid,), in_specs=specs * 4, out_specs=specs * 4, out_shape=shapes * 4,
        compiler_params=_params(40, ("arbitrary",)),
    )(*w, *g, *m, *v)
    return [outs[a * n_t:(a + 1) * n_t] for a in range(4)]


def kernel(x, p, pre_norm_g, w_in, ssm_lam_re, ssm_lam_im, ssm_log_step, ssm_b_re, ssm_b_im, ssm_c_re, ssm_c_im, ssm_d, ssm_w_glu, ssm_b_glu, attn_sinks, w_out, post_norm_g, pl_w_proj, pl_w_gate, pl_b_gate, loss_target, m_pre_norm_g, m_w_in, m_ssm_lam_re, m_ssm_lam_im, m_ssm_log_step, m_ssm_b_re, m_ssm_b_im, m_ssm_c_re, m_ssm_c_im, m_ssm_d, m_ssm_w_glu, m_ssm_b_glu, m_attn_sinks, m_w_out, m_post_norm_g, m_pl_w_proj, m_pl_w_gate, m_pl_b_gate, v_pre_norm_g, v_w_in, v_ssm_lam_re, v_ssm_lam_im, v_ssm_log_step, v_ssm_b_re, v_ssm_b_im, v_ssm_c_re, v_ssm_c_im, v_ssm_d, v_ssm_w_glu, v_ssm_b_glu, v_attn_sinks, v_w_out, v_post_norm_g, v_pl_w_proj, v_pl_w_gate, v_pl_b_gate):
    weights = dict(pre_norm_g=pre_norm_g, w_in=w_in, ssm_lam_re=ssm_lam_re, ssm_lam_im=ssm_lam_im,
                   ssm_log_step=ssm_log_step, ssm_b_re=ssm_b_re, ssm_b_im=ssm_b_im, ssm_c_re=ssm_c_re,
                   ssm_c_im=ssm_c_im, ssm_d=ssm_d, ssm_w_glu=ssm_w_glu, ssm_b_glu=ssm_b_glu, attn_sinks=attn_sinks,
                   w_out=w_out, post_norm_g=post_norm_g, pl_w_proj=pl_w_proj, pl_w_gate=pl_w_gate, pl_b_gate=pl_b_gate)
    m_in = dict(pre_norm_g=m_pre_norm_g, w_in=m_w_in, ssm_lam_re=m_ssm_lam_re, ssm_lam_im=m_ssm_lam_im,
                ssm_log_step=m_ssm_log_step, ssm_b_re=m_ssm_b_re, ssm_b_im=m_ssm_b_im, ssm_c_re=m_ssm_c_re,
                ssm_c_im=m_ssm_c_im, ssm_d=m_ssm_d, ssm_w_glu=m_ssm_w_glu, ssm_b_glu=m_ssm_b_glu,
                attn_sinks=m_attn_sinks, w_out=m_w_out, post_norm_g=m_post_norm_g, pl_w_proj=m_pl_w_proj,
                pl_w_gate=m_pl_w_gate, pl_b_gate=m_pl_b_gate)
    v_in = dict(pre_norm_g=v_pre_norm_g, w_in=v_w_in, ssm_lam_re=v_ssm_lam_re, ssm_lam_im=v_ssm_lam_im,
                ssm_log_step=v_ssm_log_step, ssm_b_re=v_ssm_b_re, ssm_b_im=v_ssm_b_im, ssm_c_re=v_ssm_c_re,
                ssm_c_im=v_ssm_c_im, ssm_d=v_ssm_d, ssm_w_glu=v_ssm_w_glu, ssm_b_glu=v_ssm_b_glu,
                attn_sinks=v_attn_sinks, w_out=v_w_out, post_norm_g=v_post_norm_g, pl_w_proj=v_pl_w_proj,
                pl_w_gate=v_pl_w_gate, pl_b_gate=v_pl_b_gate)

    def two_d(tree):
        return {k: _to_kernel_form(k, a) for k, a in tree.items()}

    w2, m2, v2 = two_d(weights), two_d(m_in), two_d(v_in)

    (w_in_full,) = _gather_weights_beside([w2["w_in"].astype(BF16)], "gather_w_in_beside", 4)
    s5_params = tuple(w2[n] for n in ("ssm_lam_re", "ssm_lam_im", "ssm_log_step", "ssm_b_re", "ssm_b_im", "ssm_c_re",
                                      "ssm_c_im"))
    s5_operands = _s5_params_fwd(*s5_params)
    hn = _pre_norm(x.reshape(-1, D_MODEL), w2["pre_norm_g"])
    behind = s5_operands[0][0, 0] * 0.0 + hn[0, 0].astype(F32) * 0.0
    rest = _gather_weights_beside([(w2[n] + behind).astype(BF16) for n in _BIG[1:]], "gather_weights_beside", 1)
    full = dict(zip(_BIG, [w_in_full] + rest))
    mats = ("w_in_early", "w_in_late") + _BIG[1:]
    landed, bc = {}, {}

    def send_tail_grads(ready):
        sent_early = ("w_out", "pl_w_gate", "pl_w_proj")
        landed.update(zip([mats.index(n) for n in sent_early],
                          _scatter_beside([ready[n] for n in sent_early], "scatter_beside", 2)))
        return landed[mats.index(sent_early[-1])]

    def send_bc(d_bc):
        bc["own"] = d_bc
        bc["landed"] = _broadcast_beside([d_bc])[0]
        return d_bc, bc["landed"]

    grad_x, loss, grads = _local_step(
        x, hn, p, loss_target, w2["pre_norm_g"], full["w_in"], s5_params, s5_operands, w2["ssm_d"], full["ssm_w_glu"], w2["ssm_b_glu"],
        w2["attn_sinks"], full["w_out"], w2["post_norm_g"], full["pl_w_proj"], full["pl_w_gate"], w2["pl_b_gate"], send_tail_grads, send_bc)

    landed[0] = _scatter_beside([grads["w_in_early_bf16"]], "scatter_w_in_beside", 5)[0]
    sent_here = {**{n: grads[n] for n in _SMALL if n not in _SMALL_EARLY}, "loss": loss}
    halves_of_w_in = ((0, 1),) + tuple((i,) for i in range(2, len(mats)))
    g_big, g_small = _exchange_grads([grads[n] for n in mats], halves_of_w_in, sent_here, landed, bc["own"], bc["landed"])
    g_big = dict(zip(_BIG, g_big))
    total_loss = g_small.pop("loss")

    big_out = _adamw([w2[n] for n in _BIG], [g_big[n] for n in _BIG], [m2[n] for n in _BIG], [v2[n] for n in _BIG],
                     8, "adamw_matrices")
    small_names = tuple(_SMALL)
    small_out = _adamw([w2[n] for n in small_names], [g_small[n] for n in small_names], [m2[n] for n in small_names],
                       [v2[n] for n in small_names], 1, "adamw_small")

    results = [{**dict(zip(_BIG, big_part)), **dict(zip(small_names, small_part))}
               for big_part, small_part in zip(big_out, small_out)]
    flat = [_from_kernel_form(name, r[name], weights[name].shape) for r in results for name in _WEIGHT_ORDER]
    return (total_loss.reshape(()), grad_x, *flat)
```

```python
import math

import jax
import jax.numpy as jnp
from jax import lax
from jax.experimental import pallas as pl
from jax.experimental.pallas import tpu as pltpu
from jax.experimental.pallas import tpu_sc as plsc

F32 = jnp.float32
BF16 = jnp.bfloat16

D_MODEL = 1024
D_SSM = 512
D_ATTN = 512
SSM_GROUPS = 32
SSM_GROUP_CH = 16
SSM_STATE = 64
SSM_LANES = SSM_GROUPS * SSM_STATE
HEAD_DIM = 64
N_HEADS = 8
KV_HEADS = 2
Q_PER_KV = 4
WINDOW = 128
BLOCK = 128
D_PLE = 256
D_IN = 2304
EPS = 1e-6
ATTN_SCALE = 1.0 / math.sqrt(HEAD_DIM)

ADAM_LR = 0.001
ADAM_B1 = 0.9
ADAM_B2 = 0.999
ADAM_EPS = 1e-08
ADAM_WD = 0.01
ADAM_STEP = 10

N_CHIPS = 4
LANES = 128
SCAN_CHUNKS = 8
SCAN_TILE_STEPS = 32
SCAN_LANE_CHUNK = 512
MIB = 2 ** 20
MESH = pl.DeviceIdType.MESH


def _dot(a, b):
    return jnp.dot(a, b, preferred_element_type=F32)


def _dot_nt(a, b):
    return lax.dot_general(a, b, (((1,), (1,)), ((), ())), preferred_element_type=F32)


def _dot_tn(a, b):
    return lax.dot_general(a, b, (((0,), (0,)), ((), ())), preferred_element_type=F32)


def _params(vmem_mib, semantics=None):
    kw = dict(vmem_limit_bytes=vmem_mib * MIB)
    if semantics is not None:
        kw["dimension_semantics"] = semantics
    return pltpu.CompilerParams(**kw)


def _full(shape):
    nd = len(shape)
    return pl.BlockSpec(shape, lambda *_: (0,) * nd, pipeline_mode=pl.Buffered(1))


def _rows(tm, width):
    return pl.BlockSpec((tm, width), lambda i: (i, 0))


def _sds(shape, dtype=F32):
    return pltpu.HBM(shape, dtype)


def _call(body, **kw):
    fn = pl.pallas_call(body, **kw)
    return lambda *args: fn(*[pltpu.with_memory_space_constraint(a, pltpu.HBM) for a in args])


def _silu(z):
    return z * jax.nn.sigmoid(z)


def _pre_norm(x2d, g1):
    rows = x2d.shape[0]
    tm = 512

    def body(x_ref, g_ref, hn_ref):
        x = x_ref[...]
        r = lax.rsqrt(jnp.mean(x * x, axis=-1, keepdims=True) + EPS)
        hn_ref[...] = (x * r * g_ref[...]).astype(BF16)

    return _call(
        body, name="pre_norm", grid=(rows // tm,), in_specs=[_rows(tm, D_MODEL), _full((1, D_MODEL))],
        out_specs=_rows(tm, D_MODEL), out_shape=_sds((rows, D_MODEL), BF16), compiler_params=_params(32, ("arbitrary",)),
    )(x2d, g1)


def _in_proj(hn, w_in_t, n_seq, seq):
    rows = hn.shape[0]
    tm = 1024
    slab, steps, _, _ = _scan_geometry(n_seq, seq)

    def body(hn_ref, w_ref, *out_refs):
        u_parts, (zs_ref, q_ref, k_ref, v_ref, za_ref) = out_refs[:_SCAN_PARTS], out_refs[_SCAN_PARTS:]
        whole = _dot_nt(hn_ref[...], w_ref[...])

        def proj(a, b):
            return whole[:, a:b]

        _store_chunks(u_parts, pl.program_id(0) * (tm // steps), proj(0, 512), steps, slab)
        zs_ref[...] = proj(512, 1024)
        q_ref[...] = (proj(1024, 1536) * ATTN_SCALE).astype(BF16)
        k_ref[...] = proj(1536, 1664).astype(BF16)
        v_ref[...] = proj(1664, 1792).astype(BF16)
        za_ref[...] = proj(1792, 2304)

    *u_parts, zs, q, k, v, za = _call(
        body, name="in_proj", grid=(rows // tm,),
        in_specs=[_rows(tm, D_MODEL), _full((D_IN, D_MODEL))],
        out_specs=_whole_parts(rows) + [_rows(tm, 512), _rows(tm, 512), _rows(tm, 128), _rows(tm, 128), _rows(tm, 512)],
        out_shape=_part_shapes(rows) + [_sds((rows, 512)), _sds((rows, 512), BF16), _sds((rows, 128), BF16),
                                        _sds((rows, 128), BF16), _sds((rows, 512))],
        compiler_params=_params(48, ("arbitrary",)),
    )(hn, w_in_t)
    return u_parts, zs, q, k, v, za


_EARLY_COLS = D_MODEL // 2


def _in_proj_bwd_early(hn, du_parts, dzs, dq, dk, dv, dza, runs_after, n_seq, seq):
    rows = hn.shape[0]
    tm = 1024
    slab, steps, _, _ = _scan_geometry(n_seq, seq)

    def body(hn_ref, *refs):
        du_parts, (dzs_ref, dq_ref, dk_ref, dv_ref, dza_ref, _, dproj_ref, dw_ref, dwb_ref) = refs[:_SCAN_PARTS], refs[_SCAN_PARTS:]
        i = pl.program_id(0)

        @pl.when(i == 0)
        def _():
            dw_ref[...] = jnp.zeros_like(dw_ref)

        du = _load_chunks(du_parts, i * (tm // steps), tm // steps, steps, slab)
        d_proj = jnp.concatenate([du.astype(BF16), dzs_ref[...], dq_ref[...], dk_ref[...], dv_ref[...], dza_ref[...]],
                                 axis=1)
        dproj_ref[...] = d_proj
        dw_ref[...] += _dot_tn(d_proj, hn_ref[...])

        @pl.when(i == rows // tm - 1)
        def _():
            dwb_ref[...] = dw_ref[...].astype(BF16)

    return _call(
        body, name="in_proj_bwd_early", grid=(rows // tm,),
        in_specs=[_rows(tm, _EARLY_COLS)] + _whole_parts(rows)
        + [_rows(tm, 512), _rows(tm, 512), _rows(tm, 128), _rows(tm, 128), _rows(tm, 512),
           pl.BlockSpec(memory_space=pl.ANY)],
        out_specs=[_rows(tm, D_IN), _full((D_IN, _EARLY_COLS)), _full((D_IN, _EARLY_COLS))],
        out_shape=[_sds((rows, D_IN), BF16), _sds((D_IN, _EARLY_COLS)), _sds((D_IN, _EARLY_COLS), BF16)],
        compiler_params=_params(48, ("arbitrary",)),
    )(hn, *du_parts, dzs, dq, dk, dv, dza, runs_after)


def _in_proj_bwd(x2d, dh1, g1, w_in_t, d_proj, runs_after):
    rows = x2d.shape[0]
    tm = 512

    def body(x_ref, dh1_ref, g_ref, w_ref, dproj_ref, _, gx_ref, dw_ref, dg_ref):
        @pl.when(pl.program_id(0) == 0)
        def _():
            dw_ref[...] = jnp.zeros_like(dw_ref)
            dg_ref[...] = jnp.zeros_like(dg_ref)

        x = x_ref[...]
        g = g_ref[...]
        r = lax.rsqrt(jnp.mean(x * x, axis=-1, keepdims=True) + EPS)
        xr = x * r
        hn = (xr[:, _EARLY_COLS:] * g[:, _EARLY_COLS:]).astype(BF16)
        d_proj = dproj_ref[...]
        dhn = _dot(d_proj, w_ref[...])
        dw_ref[...] += _dot_tn(d_proj, hn)
        dg_ref[...] += jnp.sum(dhn * xr, axis=0, keepdims=True)
        a_ = dhn * g
        gx_ref[...] = dh1_ref[...] + r * a_ - xr * (r * jnp.mean(a_ * xr, axis=-1, keepdims=True))

    late_cols = D_MODEL - _EARLY_COLS
    return _call(
        body, name="in_proj_bwd", grid=(rows // tm,),
        in_specs=[_rows(tm, D_MODEL), _rows(tm, D_MODEL), _full((1, D_MODEL)), _full((D_IN, D_MODEL)), _rows(tm, D_IN),
                  pl.BlockSpec(memory_space=pl.ANY)],
        out_specs=[_rows(tm, D_MODEL), _full((D_IN, late_cols)), _full((1, D_MODEL))],
        out_shape=[_sds((rows, D_MODEL)), _sds((D_IN, late_cols)), _sds((1, D_MODEL))],
        compiler_params=_params(52, ("arbitrary",)),
    )(x2d, dh1, g1, w_in_t, d_proj, runs_after)


def _iota(shape, axis):
    return lax.broadcasted_iota(jnp.int32, shape, axis)


def _sum_of_thirds(f, a):
    hi = a.astype(BF16)
    rest = a - hi.astype(F32)
    mid = rest.astype(BF16)
    low = (rest - mid.astype(F32)).astype(BF16)
    return (f(hi) + f(mid)) + f(low)


@jax.custom_vjp
def _pick_rows(e, a):
    return _sum_of_thirds(lambda part: _dot(e, part), a)


def _pick_rows_fwd(e, a):
    return _pick_rows(e, a), e


def _pick_rows_bwd(e, ct):
    return jnp.zeros_like(e), _sum_of_thirds(lambda part: _dot_tn(e, part), ct)


_pick_rows.defvjp(_pick_rows_fwd, _pick_rows_bwd)


@jax.custom_vjp
def _pick_cols(a, e):
    return _sum_of_thirds(lambda part: _dot(part, e), a)


def _pick_cols_fwd(a, e):
    return _pick_cols(a, e), e


def _pick_cols_bwd(e, ct):
    return _sum_of_thirds(lambda part: _dot_nt(part, e), ct), jnp.zeros_like(e)


_pick_cols.defvjp(_pick_cols_fwd, _pick_cols_bwd)


_HALF_GROUPS = SSM_GROUPS // 2
_N_SHIFT = SSM_STATE.bit_length() - 1
_P_SHIFT = SSM_GROUP_CH.bit_length() - 1


def _s5_operands(lam_re, lam_im, log_step, b_re, b_im, c_re, c_im):
    g, n, p = SSM_GROUPS, SSM_STATE, SSM_GROUP_CH
    gn, gp, hn_, hp = g * n, g * p, _HALF_GROUPS * n, _HALF_GROUPS * p
    eye_g = _iota((g, g), 0) == _iota((g, g), 1)
    step = jnp.sum(jnp.where(eye_g, jnp.exp(log_step), 0.0), axis=1, keepdims=True)
    a_re = lam_re * step
    a_im = lam_im * step
    mag = jnp.exp(a_re)
    lbar_re = mag * jnp.cos(a_im)
    lbar_im = mag * jnp.sin(a_im)
    n_re = lbar_re - 1.0
    den = lam_re * lam_re + lam_im * lam_im
    f_re = (n_re * lam_re + lbar_im * lam_im) / den
    f_im = (lbar_im * lam_re - n_re * lam_im) / den

    spread_n = (_iota((n, gn), 0) == (_iota((n, gn), 1) & (n - 1))).astype(BF16)
    own_g = _iota((g, gn), 0) == (_iota((g, gn), 1) >> _N_SHIFT)

    def to_row(a):
        return jnp.sum(jnp.where(own_g, _pick_cols(a, spread_n), 0.0), axis=0, keepdims=True)

    per_group = ((_iota((gp, g), 0) >> _P_SHIFT) == _iota((gp, g), 1)).astype(BF16)
    fx_re, fx_im = _pick_rows(per_group, f_re), _pick_rows(per_group, f_im)
    bbar_re = fx_re * b_re - fx_im * b_im
    bbar_im = fx_re * b_im + fx_im * b_re

    tile_n = (_iota((n, hn_), 0) == (_iota((n, hn_), 1) & (n - 1))).astype(BF16)
    same_group = (_iota((hp, hn_), 0) >> _P_SHIFT) == (_iota((hp, hn_), 1) >> _N_SHIFT)

    def embed(a, hf):
        return jnp.where(same_group, _pick_cols(a[hf * hp:(hf + 1) * hp], tile_n), 0.0)

    return (to_row(lbar_re), to_row(lbar_im), embed(bbar_re, 0), embed(bbar_re, 1), embed(bbar_im, 0),
            embed(bbar_im, 1), embed(c_re, 0), embed(c_re, 1), embed(c_im, 0), embed(c_im, 1))


_S5_PARAM_SHAPES = ((SSM_GROUPS, SSM_STATE), (SSM_GROUPS, SSM_STATE), (1, SSM_GROUPS),
                    (D_SSM, SSM_STATE), (D_SSM, SSM_STATE), (D_SSM, SSM_STATE), (D_SSM, SSM_STATE))
_CM_SHAPE = (2, _HALF_GROUPS * SSM_GROUP_CH, _HALF_GROUPS * SSM_STATE)
_S5_OPERAND_SHAPES = ((1, SSM_LANES), (1, SSM_LANES), _CM_SHAPE, _CM_SHAPE, _CM_SHAPE, _CM_SHAPE)


def _s5_params_fwd(*params):
    def body(*refs):
        ins, (lre_ref, lim_ref, btre_ref, btim_ref, cmre_ref, cmim_ref) = refs[:7], refs[7:]
        vals = _s5_operands(*[r[...] for r in ins])
        lre_ref[...] = vals[0]
        lim_ref[...] = vals[1]
        for ref, pair in zip((btre_ref, btim_ref, cmre_ref, cmim_ref), (vals[2:4], vals[4:6], vals[6:8], vals[8:10])):
            ref[0] = pair[0].astype(BF16)
            ref[1] = pair[1].astype(BF16)

    dtypes = (F32, F32, BF16, BF16, BF16, BF16)
    return _call(
        body, name="s5_params_fwd",
        in_specs=[_full(s) for s in _S5_PARAM_SHAPES], out_specs=[_full(s) for s in _S5_OPERAND_SHAPES],
        out_shape=[_sds(s, d) for s, d in zip(_S5_OPERAND_SHAPES, dtypes)], compiler_params=_params(32),
    )(*params)


_BC_SIDE_BY_SIDE = (D_SSM, 4 * SSM_STATE)


def _s5_params_bwd(params, cotangents, runs_after):
    def body(*refs):
        ins, (dlre, dlim, dbtre, dbtim, dcmre, dcmim), outs = refs[:7], refs[7:13], refs[14:]
        _, vjp = jax.vjp(_s5_operands, *[r[...] for r in ins])
        cts = (dlre[...], dlim[...], dbtre[0], dbtre[1], dbtim[0], dbtim[1], dcmre[0], dcmre[1], dcmim[0], dcmim[1])
        grads = vjp(cts)
        for ref, val in zip(outs[:3], grads[:3]):
            ref[...] = val
        outs[3][...] = jnp.concatenate(grads[3:], axis=1).astype(BF16)

    out_shapes = _S5_PARAM_SHAPES[:3] + (_BC_SIDE_BY_SIDE,)
    return _call(
        body, name="s5_params_bwd",
        in_specs=[_full(s) for s in _S5_PARAM_SHAPES + _S5_OPERAND_SHAPES] + [pl.BlockSpec(memory_space=pl.ANY)],
        out_specs=[_full(s) for s in out_shapes],
        out_shape=[_sds(s, d) for s, d in zip(out_shapes, (F32, F32, F32, BF16))], compiler_params=_params(48),
    )(*params, *cotangents, runs_after)


def _scan_geometry(n_seq, seq):
    slab = n_seq * SCAN_CHUNKS
    steps = seq // SCAN_CHUNKS
    tile_rows = slab * SCAN_TILE_STEPS
    n_tiles = steps // SCAN_TILE_STEPS
    return slab, steps, tile_rows, n_tiles


_SCAN_PARTS = D_SSM // LANES


def _whole_parts(rows):
    return [_full((rows, LANES))] * _SCAN_PARTS


def _part_shapes(rows):
    return [_sds((rows, LANES))] * _SCAN_PARTS


def _load_chunks(parts, first_chunk, n_chunks, steps, slab):
    return jnp.concatenate([
        jnp.concatenate([ref[pl.ds(first_chunk + q, steps, stride=slab), :] for ref in parts], axis=1)
        for q in range(n_chunks)], axis=0)


def _store_chunks(parts, first_chunk, value, steps, slab):
    for q in range(value.shape[0] // steps):
        for j, ref in enumerate(parts):
            ref[pl.ds(first_chunk + q, steps, stride=slab), :] = value[q * steps:(q + 1) * steps,
                                                                     j * LANES:(j + 1) * LANES]


def _join_parts(parts):
    return jnp.concatenate([ref[...] for ref in parts], axis=1)


def _split_parts(parts, value):
    for j, ref in enumerate(parts):
        ref[...] = value[:, j * LANES:(j + 1) * LANES]


def _complex_power(re, im, n):
    out = None
    while n:
        if n & 1:
            out = (re, im) if out is None else (out[0] * re - out[1] * im, out[0] * im + out[1] * re)
        n >>= 1
        if n:
            re, im = re * re - im * im, 2.0 * re * im
    return out


def _chunk_carry(sum_re, sum_im, carry_re, carry_im, a_re, a_im, n_seq, reverse):
    carry_re[...] = jnp.zeros_like(carry_re)
    carry_im[...] = jnp.zeros_like(carry_im)
    for s in range(n_seq):
        order = range(SCAN_CHUNKS - 2, -1, -1) if reverse else range(1, SCAN_CHUNKS)
        for c in order:
            r = s * SCAN_CHUNKS + c
            p = r + 1 if reverse else r - 1
            p_re, p_im = carry_re[p:p + 1, :], carry_im[p:p + 1, :]
            carry_re[r:r + 1, :] = a_re * p_re - a_im * p_im + sum_re[p:p + 1, :]
            carry_im[r:r + 1, :] = a_re * p_im + a_im * p_re + sum_im[p:p + 1, :]


def _s5_scan_fwd(u_parts, bt_re, bt_im, cm_re, cm_im, lbar_re, lbar_im, d_row, n_seq, seq):
    slab, steps, tile_rows, n_tiles = _scan_geometry(n_seq, seq)
    rows = u_parts[0].shape[0]

    def body(*refs):
        u_refs, refs = refs[:_SCAN_PARTS], refs[_SCAN_PARTS:]
        (bre_ref, bim_ref, cre_ref, cim_ref, lre_ref, lim_ref, d_ref), refs = refs[:7], refs[7:]
        y_refs, (hre_ref, him_ref, st_re, st_im, h0_re, h0_im, buf_re, buf_im) = refs[:_SCAN_PARTS], refs[_SCAN_PARTS:]
        second = pl.program_id(0) == 1
        i = pl.program_id(1)

        @pl.when(jnp.logical_and(i == 0, jnp.logical_not(second)))
        def _():
            st_re[...] = jnp.zeros_like(st_re)
            st_im[...] = jnp.zeros_like(st_im)

        u = _join_parts(u_refs)
        ub = u.astype(BF16)
        for hf in range(2):
            cols = slice(hf * 1024, (hf + 1) * 1024)
            buf_re[:, cols] = _dot(ub[:, hf * 256:(hf + 1) * 256], bre_ref[hf])
            buf_im[:, cols] = _dot(ub[:, hf * 256:(hf + 1) * 256], bim_ref[hf])

        for lc in range(SSM_LANES // SCAN_LANE_CHUNK):
            cols = slice(lc * SCAN_LANE_CHUNK, (lc + 1) * SCAN_LANE_CHUNK)
            l_re = jnp.broadcast_to(lre_ref[:, cols], (slab, SCAN_LANE_CHUNK))
            l_im = jnp.broadcast_to(lim_ref[:, cols], (slab, SCAN_LANE_CHUNK))

            def scan_tile(keep_states):
                def step(t, carry):
                    s_re, s_im = carry
                    r0 = pl.multiple_of(t * slab, slab)
                    n_re = l_re * s_re - l_im * s_im + buf_re[pl.ds(r0, slab), cols]
                    n_im = l_re * s_im + l_im * s_re + buf_im[pl.ds(r0, slab), cols]
                    if keep_states:
                        buf_re[pl.ds(r0, slab), cols] = n_re
                        buf_im[pl.ds(r0, slab), cols] = n_im
                    return n_re, n_im

                s_re, s_im = lax.fori_loop(0, SCAN_TILE_STEPS, step, (st_re[:, cols], st_im[:, cols]), unroll=True)
                st_re[:, cols] = s_re
                st_im[:, cols] = s_im

            pl.when(jnp.logical_not(second))(lambda: scan_tile(False))
            pl.when(second)(lambda: scan_tile(True))

        @pl.when(jnp.logical_and(i == n_tiles - 1, jnp.logical_not(second)))
        def _():
            a_re, a_im = _complex_power(lre_ref[...], lim_ref[...], steps)
            _chunk_carry(st_re, st_im, h0_re, h0_im, a_re, a_im, n_seq, reverse=False)
            st_re[...] = h0_re[...]
            st_im[...] = h0_im[...]

        @pl.when(second)
        def _():
            h_re = buf_re[...].astype(BF16)
            h_im = buf_im[...].astype(BF16)
            hre_ref[...] = h_re
            him_ref[...] = h_im
            for hf in range(2):
                cols = slice(hf * 1024, (hf + 1) * 1024)
                ycols = slice(hf * 256, (hf + 1) * 256)
                y_half = (_dot_nt(h_re[:, cols], cre_ref[hf]) - _dot_nt(h_im[:, cols], cim_ref[hf])
                          + d_ref[:, ycols] * u[:, ycols])
                _split_parts(y_refs[2 * hf:2 * hf + 2], y_half)

    tile = lambda w: pl.BlockSpec((tile_rows, w), lambda p, i: (i, 0))
    out_tile = lambda w: pl.BlockSpec((tile_rows, w), lambda p, i: (i * p, 0))
    cm = _full(_CM_SHAPE)
    outs = _call(
        body, name="s5_scan_fwd", grid=(2, n_tiles),
        in_specs=[tile(LANES)] * _SCAN_PARTS + [cm, cm, cm, cm, _full((1, SSM_LANES)), _full((1, SSM_LANES)),
                                                _full((1, 512))],
        out_specs=[out_tile(LANES)] * _SCAN_PARTS + [out_tile(SSM_LANES), out_tile(SSM_LANES)],
        out_shape=_part_shapes(rows) + [_sds((rows, SSM_LANES), BF16), _sds((rows, SSM_LANES), BF16)],
        scratch_shapes=[pltpu.VMEM((slab, SSM_LANES), F32)] * 4 + [pltpu.VMEM((tile_rows, SSM_LANES), F32)] * 2,
        compiler_params=_params(40, ("arbitrary", "arbitrary")),
    )(*u_parts, bt_re, bt_im, cm_re, cm_im, lbar_re, lbar_im, d_row)
    return outs[:_SCAN_PARTS], outs[_SCAN_PARTS], outs[_SCAN_PARTS + 1]


def _s5_scan_bwd(dy_parts, u_parts, h_re, h_im, bt_re, bt_im, cm_re, cm_im, lbar_re, lbar_im, d_row, n_seq, seq):
    slab, steps, tile_rows, n_tiles = _scan_geometry(n_seq, seq)
    rows = u_parts[0].shape[0]

    def body(*refs):
        dy_refs, u_refs, refs = refs[:_SCAN_PARTS], refs[_SCAN_PARTS:2 * _SCAN_PARTS], refs[2 * _SCAN_PARTS:]
        (hre_ref, him_ref, bre_ref, bim_ref, cre_ref, cim_ref, lre_ref, lim_ref, d_ref), refs = refs[:9], refs[9:]
        du_refs, refs = refs[:_SCAN_PARTS], refs[_SCAN_PARTS:]
        (dbre_ref, dbim_ref, dcre_ref, dcim_ref, dlre_ref, dlim_ref, dd_ref,
         st_re, st_im, g0_re, g0_im, acc_re, acc_im, buf_re, buf_im) = refs
        second = pl.program_id(0) == 1
        i = pl.program_id(1)

        @pl.when(jnp.logical_and(i == 0, jnp.logical_not(second)))
        def _():
            st_re[...] = jnp.zeros_like(st_re)
            st_im[...] = jnp.zeros_like(st_im)
            acc_re[...] = jnp.zeros_like(acc_re)
            acc_im[...] = jnp.zeros_like(acc_im)
            for ref in (dbre_ref, dbim_ref, dcre_ref, dcim_ref, dd_ref):
                ref[...] = jnp.zeros_like(ref)

        dy = _join_parts(dy_refs)
        dyb = dy.astype(BF16)
        for hf in range(2):
            cols = slice(hf * 1024, (hf + 1) * 1024)
            buf_re[:, cols] = _dot(dyb[:, hf * 256:(hf + 1) * 256], cre_ref[hf])
            buf_im[:, cols] = -_dot(dyb[:, hf * 256:(hf + 1) * 256], cim_ref[hf])

        for lc in range(SSM_LANES // SCAN_LANE_CHUNK):
            cols = slice(lc * SCAN_LANE_CHUNK, (lc + 1) * SCAN_LANE_CHUNK)
            l_re = jnp.broadcast_to(lre_ref[:, cols], (slab, SCAN_LANE_CHUNK))
            l_im = jnp.broadcast_to(lim_ref[:, cols], (slab, SCAN_LANE_CHUNK))

            def advance(r0, s_re, s_im):
                n_re = l_re * s_re + l_im * s_im + buf_re[pl.ds(r0, slab), cols]
                n_im = l_re * s_im - l_im * s_re + buf_im[pl.ds(r0, slab), cols]
                buf_re[pl.ds(r0, slab), cols] = n_re
                buf_im[pl.ds(r0, slab), cols] = n_im
                return n_re, n_im

            def row0(k):
                return pl.multiple_of((SCAN_TILE_STEPS - 1 - k) * slab, slab)

            @pl.when(jnp.logical_not(second))
            def _():
                s_re, s_im = lax.fori_loop(0, SCAN_TILE_STEPS, lambda k, s: advance(row0(k), *s),
                                           (st_re[:, cols], st_im[:, cols]), unroll=True)
                st_re[:, cols] = s_re
                st_im[:, cols] = s_im

            @pl.when(second)
            def _():
                def step(k, carry):
                    s_re, s_im, a_re, a_im = carry
                    r0 = row0(k)
                    hr = hre_ref[pl.ds(r0, slab), cols].astype(F32)
                    hi = him_ref[pl.ds(r0, slab), cols].astype(F32)
                    a_re = a_re + s_re * hr + s_im * hi
                    a_im = a_im + s_im * hr - s_re * hi
                    return advance(r0, s_re, s_im) + (a_re, a_im)

                zero = jnp.zeros((slab, SCAN_LANE_CHUNK), F32)
                s_re, s_im, a_re, a_im = lax.fori_loop(
                    0, SCAN_TILE_STEPS, step, (st_re[:, cols], st_im[:, cols], zero, zero), unroll=True)
                st_re[:, cols] = s_re
                st_im[:, cols] = s_im
                acc_re[:, cols] += a_re
                acc_im[:, cols] += a_im

        @pl.when(jnp.logical_and(i == n_tiles - 1, jnp.logical_not(second)))
        def _():
            p_re, p_im = _complex_power(lre_ref[...], lim_ref[...], steps)
            _chunk_carry(st_re, st_im, g0_re, g0_im, p_re, -p_im, n_seq, reverse=True)
            st_re[...] = g0_re[...]
            st_im[...] = g0_im[...]

        @pl.when(second)
        def _():
            u = _join_parts(u_refs)
            ub = u.astype(BF16)
            g_re = buf_re[...].astype(BF16)
            g_im = buf_im[...].astype(BF16)
            dd_ref[...] += jnp.sum(dy * u, axis=0, keepdims=True)
            for hf in range(2):
                cols = slice(hf * 1024, (hf + 1) * 1024)
                ycols = slice(hf * 256, (hf + 1) * 256)
                du_half = (_dot_nt(g_re[:, cols], bre_ref[hf]) + _dot_nt(g_im[:, cols], bim_ref[hf])
                           + d_ref[:, ycols] * dy[:, ycols])
                _split_parts(du_refs[2 * hf:2 * hf + 2], du_half)
                for q4 in range(_HALF_GROUPS // 4):
                    ch = slice(hf * 256 + q4 * 64, hf * 256 + (q4 + 1) * 64)
                    st = slice(hf * 1024 + q4 * 256, hf * 1024 + (q4 + 1) * 256)
                    blk = (hf, slice(q4 * 64, (q4 + 1) * 64), slice(q4 * 256, (q4 + 1) * 256))
                    dbre_ref[blk] += _dot_tn(ub[:, ch], g_re[:, st])
                    dbim_ref[blk] += _dot_tn(ub[:, ch], g_im[:, st])
                    dcre_ref[blk] += _dot_tn(dyb[:, ch], hre_ref[:, st])
                    dcim_ref[blk] -= _dot_tn(dyb[:, ch], him_ref[:, st])

        @pl.when(jnp.logical_and(i == n_tiles - 1, second))
        def _():
            dlre_ref[...] = jnp.sum(acc_re[...], axis=0, keepdims=True)
            dlim_ref[...] = jnp.sum(acc_im[...], axis=0, keepdims=True)

    tile = lambda w: pl.BlockSpec((tile_rows, w), lambda p, i: (n_tiles - 1 - i, 0))
    second_tile = lambda w: pl.BlockSpec((tile_rows, w), lambda p, i: (n_tiles - 1 - i * p, 0))
    cm = _full(_CM_SHAPE)
    row = _full((1, SSM_LANES))
    outs = _call(
        body, name="s5_scan_bwd", grid=(2, n_tiles),
        in_specs=[tile(LANES)] * _SCAN_PARTS + [second_tile(LANES)] * _SCAN_PARTS
        + [second_tile(SSM_LANES), second_tile(SSM_LANES), cm, cm, cm, cm, row, row, _full((1, 512))],
        out_specs=[second_tile(LANES)] * _SCAN_PARTS + [cm, cm, cm, cm, row, row, _full((1, 512))],
        out_shape=(_part_shapes(rows) + [_sds(_CM_SHAPE)] * 4 + [_sds((1, SSM_LANES))] * 2 + [_sds((1, 512))]),
        scratch_shapes=[pltpu.VMEM((slab, SSM_LANES), F32)] * 6 + [pltpu.VMEM((tile_rows, SSM_LANES), F32)] * 2,
        compiler_params=_params(48, ("arbitrary", "arbitrary")),
    )(*dy_parts, *u_parts, h_re, h_im, bt_re, bt_im, cm_re, cm_im, lbar_re, lbar_im, d_row)
    return (outs[:_SCAN_PARTS],) + tuple(outs[_SCAN_PARTS:])


def _glu_gate(gl, a, zs):
    return gl * jax.nn.sigmoid(a) * _silu(zs)


def _glu_fwd(y_parts, zs, w_glu, b_glu, n_seq, seq):
    rows = zs.shape[0]
    tm = 512
    slab, steps, _, _ = _scan_geometry(n_seq, seq)

    def body(*refs):
        y_refs, (zs_ref, w_ref, b_ref, o_ref) = refs[:_SCAN_PARTS], refs[_SCAN_PARTS:]
        y = _load_chunks(y_refs, pl.program_id(0) * (tm // steps), tm // steps, steps, slab)
        gl = jax.nn.gelu(y)
        a = _dot(gl.astype(BF16), w_ref[...]) + b_ref[...]
        o_ref[...] = _glu_gate(gl, a, zs_ref[...]).astype(BF16)

    return _call(
        body, name="glu_fwd", grid=(rows // tm,),
        in_specs=_whole_parts(rows) + [_rows(tm, 512), _full((512, 512)), _full((1, 512))],
        out_specs=_rows(tm, 512), out_shape=_sds((rows, 512), BF16),
        compiler_params=_params(32, ("arbitrary",)),
    )(*y_parts, zs, w_glu, b_glu)


def _glu_bwd(y_parts, zs, d_out, w_glu, b_glu, n_seq, seq):
    rows = zs.shape[0]
    tm = 512
    slab, steps, _, _ = _scan_geometry(n_seq, seq)

    def body(*refs):
        y_refs, (zs_ref, d_ref, w_ref, b_ref), refs = refs[:_SCAN_PARTS], refs[_SCAN_PARTS:_SCAN_PARTS + 4], refs[_SCAN_PARTS + 4:]
        dy_refs, (dzs_ref, dw_ref, db_ref) = refs[:_SCAN_PARTS], refs[_SCAN_PARTS:]
        first_chunk = pl.program_id(0) * (tm // steps)

        @pl.when(pl.program_id(0) == 0)
        def _():
            dw_ref[...] = jnp.zeros_like(dw_ref)
            db_ref[...] = jnp.zeros_like(db_ref)

        gl, gelu_vjp = jax.vjp(jax.nn.gelu, _load_chunks(y_refs, first_chunk, tm // steps, steps, slab))
        glb = gl.astype(BF16)
        a = _dot(glb, w_ref[...]) + b_ref[...]
        _, gate_vjp = jax.vjp(_glu_gate, gl, a, zs_ref[...])
        d_gl, d_a, d_zs = gate_vjp(d_ref[...])
        dab = d_a.astype(BF16)
        d_gl = d_gl + _dot_nt(dab, w_ref[...])
        _store_chunks(dy_refs, first_chunk, gelu_vjp(d_gl)[0], steps, slab)
        dzs_ref[...] = d_zs.astype(BF16)
        dw_ref[...] += _dot_tn(glb, dab)
        db_ref[...] += jnp.sum(d_a, axis=0, keepdims=True)

    *dy_parts, dzs, dw, db = _call(
        body, name="glu_bwd", grid=(rows // tm,),
        in_specs=_whole_parts(rows) + [_rows(tm, 512), _rows(tm, 512), _full((512, 512)), _full((1, 512))],
        out_specs=_whole_parts(rows) + [_rows(tm, 512), _full((512, 512)), _full((1, 512))],
        out_shape=_part_shapes(rows) + [_sds((rows, 512), BF16), _sds((512, 512)), _sds((1, 512))],
        compiler_params=_params(40, ("arbitrary",)),
    )(*y_parts, zs, d_out, w_glu, b_glu)
    return dy_parts, dzs, dw, db


_GROUP_ROWS = Q_PER_KV * BLOCK
_BLOCK_SHIFT = BLOCK.bit_length() - 1


def _attn_bias(j):
    query = _iota((BLOCK, _GROUP_ROWS), 1)
    dist_cur = (query & (BLOCK - 1)) - _iota((BLOCK, _GROUP_ROWS), 0)
    dist_prev = dist_cur + BLOCK
    head = query >> _BLOCK_SHIFT
    slope = jnp.zeros((BLOCK, _GROUP_ROWS), F32)
    for g in range(Q_PER_KV):
        slope = jnp.where(head == g, 2.0 ** (-(j * Q_PER_KV + g + 1)), slope)
    bias_cur = jnp.where(dist_cur >= 0, -slope * dist_cur.astype(F32), -jnp.inf)
    bias_prev = jnp.where(dist_prev < WINDOW, -slope * dist_prev.astype(F32), -jnp.inf)
    return bias_cur, bias_prev


_ATTN_BIAS_SCRATCH = pltpu.VMEM((KV_HEADS, 2, BLOCK, _GROUP_ROWS), F32)


def _fill_attn_bias(bias_ref):
    @pl.when(jnp.logical_and(pl.program_id(0) == 0, pl.program_id(1) == 0))
    def _():
        for j in range(KV_HEADS):
            bias_ref[j, 0], bias_ref[j, 1] = _attn_bias(j)


def _stack_heads(x, j):
    heads = range(j * Q_PER_KV, (j + 1) * Q_PER_KV)
    return jnp.concatenate([x[:, h * HEAD_DIM:(h + 1) * HEAD_DIM] for h in heads], axis=0)


def _head_rows(x, j):
    heads = range(j * Q_PER_KV, (j + 1) * Q_PER_KV)
    return jnp.concatenate([x[h:h + 1, :] for h in heads], axis=1)


def _sink_row(sk_ref, j):
    heads = range(j * Q_PER_KV, (j + 1) * Q_PER_KV)
    return jnp.concatenate([jnp.broadcast_to(sk_ref[0:1, h:h + 1], (1, BLOCK)) for h in heads], axis=1)


_ATTN_FWD_BLOCKS = 8


def _attn_fwd(q, k, v, za, sinks, n_seq, seq):
    nb = seq // BLOCK
    steps = nb // _ATTN_FWD_BLOCKS
    rows = q.shape[0]

    def body(q_ref, kc_ref, kp_ref, vc_ref, vp_ref, za_ref, sk_ref, o_ref, ao_ref, lse_ref, bias_ref):
        _fill_attn_bias(bias_ref)
        for t in range(_ATTN_FWD_BLOCKS):
            at = slice(t * BLOCK, (t + 1) * BLOCK)
            before = slice((t - 1) * BLOCK, t * BLOCK)
            q_all = q_ref[at, :]
            for j in range(KV_HEADS):
                js = slice(j * HEAD_DIM, (j + 1) * HEAD_DIM)
                bias_c, bias_p = bias_ref[j, 0], bias_ref[j, 1]
                q4 = _stack_heads(q_all, j)
                sc = _dot_nt(kc_ref[at, js], q4) + bias_c
                if t == 0:
                    sp = _dot_nt(kp_ref[:, js], q4) + jnp.where(pl.program_id(1) > 0, bias_p, -jnp.inf)
                    v_prev = vp_ref[:, js]
                else:
                    sp = _dot_nt(kc_ref[before, js], q4) + bias_p
                    v_prev = vc_ref[before, js]
                sink = _sink_row(sk_ref, j)
                m = jnp.maximum(jnp.max(jnp.maximum(sc, sp), axis=0, keepdims=True), sink)
                ec = jnp.exp(sc - m)
                ep = jnp.exp(sp - m)
                den = jnp.sum(ec + ep, axis=0, keepdims=True) + jnp.exp(sink - m)
                inv = 1.0 / den
                o4 = _dot_tn((ec * inv).astype(BF16), vc_ref[at, js]) + _dot_tn((ep * inv).astype(BF16), v_prev)
                lse4 = m + jnp.log(den)
                for g in range(Q_PER_KV):
                    h = j * Q_PER_KV + g
                    o_ref[at, h * HEAD_DIM:(h + 1) * HEAD_DIM] = o4[g * BLOCK:(g + 1) * BLOCK]
                    lse_ref[t * N_HEADS + h:t * N_HEADS + h + 1, :] = lse4[:, g * BLOCK:(g + 1) * BLOCK]
        ao_ref[...] = (o_ref[...] * _silu(za_ref[...])).astype(BF16)

    cur = lambda w: pl.BlockSpec((_ATTN_FWD_BLOCKS * BLOCK, w), lambda b, n: (b * steps + n, 0))
    prev = lambda w: pl.BlockSpec((BLOCK, w), lambda b, n: (b * nb + jnp.maximum(_ATTN_FWD_BLOCKS * n - 1, 0), 0))
    lse_rows = rows // BLOCK * N_HEADS
    return _call(
        body, name="attn_fwd", grid=(n_seq, steps),
        in_specs=[cur(512), cur(128), prev(128), cur(128), prev(128), cur(512), _full((1, N_HEADS))],
        out_specs=[cur(512), cur(512),
                   pl.BlockSpec((_ATTN_FWD_BLOCKS * N_HEADS, BLOCK), lambda b, n: (b * steps + n, 0))],
        out_shape=[_sds((rows, 512)), _sds((rows, 512), BF16), _sds((lse_rows, BLOCK))],
        scratch_shapes=[_ATTN_BIAS_SCRATCH], compiler_params=_params(32, ("arbitrary", "arbitrary")),
    )(q, k, k, v, v, za, sinks)


_ATTN_BWD_BLOCKS = 4


def _attn_bwd(q, k, v, za, o, lse, d_ao, sinks, n_seq, seq):
    nb = seq // BLOCK
    per_step = _ATTN_BWD_BLOCKS
    steps = nb // per_step
    rows = q.shape[0]

    def body(q_ref, kc_ref, kp_ref, vc_ref, vp_ref, za_ref, o_ref, lse_ref, d_ref, sk_ref,
             dq_ref, dk_ref, dv_ref, dza_ref, dsk_ref, bias_ref, dk_carry, dv_carry):
        step = pl.program_id(1)
        first_block = nb - per_step * (step + 1)
        _fill_attn_bias(bias_ref)

        @pl.when(jnp.logical_and(pl.program_id(0) == 0, step == 0))
        def _():
            dsk_ref[...] = jnp.zeros_like(dsk_ref)
            dk_carry[...] = jnp.zeros_like(dk_carry)
            dv_carry[...] = jnp.zeros_like(dv_carry)

        _, gate_vjp = jax.vjp(lambda o_, z_: o_ * _silu(z_), o_ref[...], za_ref[...])
        d_o, d_za = gate_vjp(d_ref[...])
        dza_ref[...] = d_za.astype(BF16)

        for j in range(KV_HEADS):
            js = slice(j * HEAD_DIM, (j + 1) * HEAD_DIM)
            bias_c, bias_p = bias_ref[j, 0], bias_ref[j, 1]
            sink = _sink_row(sk_ref, j)
            sink_loss = jnp.zeros((1, _GROUP_ROWS), F32)
            dk_from_next = jnp.where(step > 0, dk_carry[j], 0.0)
            dv_from_next = jnp.where(step > 0, dv_carry[j], 0.0)
            for t in reversed(range(per_step)):
                at = slice(t * BLOCK, (t + 1) * BLOCK)
                kc, vc = kc_ref[at, js], vc_ref[at, js]
                if t > 0:
                    before = slice((t - 1) * BLOCK, t * BLOCK)
                    kp, vp, bias_before = kc_ref[before, js], vc_ref[before, js], bias_p
                else:
                    kp, vp, bias_before = kp_ref[:, js], vp_ref[:, js], jnp.where(first_block > 0, bias_p, -jnp.inf)
                q4 = _stack_heads(q_ref[at, :], j)
                do4b = _stack_heads(d_o[at], j).astype(BF16)
                lse4 = _head_rows(lse_ref[t * N_HEADS:(t + 1) * N_HEADS, :], j)
                pc = jnp.exp(_dot_nt(kc, q4) + bias_c - lse4)
                pp = jnp.exp(_dot_nt(kp, q4) + bias_before - lse4)
                dpc = _dot_nt(vc, do4b)
                dpp = _dot_nt(vp, do4b)
                delta = jnp.sum(pc * dpc + pp * dpp, axis=0, keepdims=True)
                dsc = (pc * (dpc - delta)).astype(BF16)
                dsp = (pp * (dpp - delta)).astype(BF16)
                dq4 = ((_dot_tn(dsc, kc) + _dot_tn(dsp, kp)) * ATTN_SCALE).astype(BF16)
                sink_loss = sink_loss + jnp.exp(sink - lse4) * delta
                for g in range(Q_PER_KV):
                    h = j * Q_PER_KV + g
                    dq_ref[at, h * HEAD_DIM:(h + 1) * HEAD_DIM] = dq4[g * BLOCK:(g + 1) * BLOCK]
                dk_ref[at, js] = (_dot(dsc, q4) + dk_from_next).astype(BF16)
                dv_ref[at, js] = (_dot(pc.astype(BF16), do4b) + dv_from_next).astype(BF16)
                dk_from_next = _dot(dsp, q4)
                dv_from_next = _dot(pp.astype(BF16), do4b)
            dk_carry[j] = dk_from_next
            dv_carry[j] = dv_from_next
            for g in range(Q_PER_KV):
                h = j * Q_PER_KV + g
                dsk_ref[0:1, h:h + 1] -= jnp.sum(sink_loss[:, g * BLOCK:(g + 1) * BLOCK], axis=1, keepdims=True)

    cur = lambda w: pl.BlockSpec((per_step * BLOCK, w), lambda b, s: (b * steps + steps - 1 - s, 0))
    prev = lambda w: pl.BlockSpec((BLOCK, w), lambda b, s: (b * nb + jnp.maximum(nb - per_step * (s + 1) - 1, 0), 0))
    return _call(
        body, name="attn_bwd", grid=(n_seq, steps),
        in_specs=[cur(512), cur(128), prev(128), cur(128), prev(128), cur(512), cur(512),
                  pl.BlockSpec((per_step * N_HEADS, BLOCK), lambda b, s: (b * steps + steps - 1 - s, 0)), cur(512),
                  _full((1, N_HEADS))],
        out_specs=[cur(512), cur(128), cur(128), cur(512), _full((1, N_HEADS))],
        out_shape=[_sds((rows, 512), BF16), _sds((rows, 128), BF16), _sds((rows, 128), BF16),
                   _sds((rows, 512), BF16), _sds((1, N_HEADS))],
        scratch_shapes=[_ATTN_BIAS_SCRATCH, pltpu.VMEM((KV_HEADS, BLOCK, HEAD_DIM), F32),
                        pltpu.VMEM((KV_HEADS, BLOCK, HEAD_DIM), F32)],
        compiler_params=_params(32, ("arbitrary", "arbitrary")),
    )(q, k, k, v, v, za, o, lse, d_ao, sinks)


def _tail(ssm_out, attn_out, x2d, p2d, target, w_out, g2, w_gate, b_gate, w_proj):
    rows = x2d.shape[0]
    tm = 512

    def body(so_ref, ao_ref, x_ref, p_ref, t_ref, wo_ref, g2_ref, wg_ref, bg_ref, wp_ref,
             dh1_ref, dso_ref, dao_ref, dwo_ref, dwg_ref, dwp_ref, dbg_ref, dg2_ref, loss_ref):
        @pl.when(pl.program_id(0) == 0)
        def _():
            for ref in (dwo_ref, dwg_ref, dwp_ref, dbg_ref, dg2_ref, loss_ref):
                ref[...] = jnp.zeros_like(ref)

        cat = jnp.concatenate([so_ref[...], ao_ref[...]], axis=1)
        g2 = g2_ref[...]
        mixed = _dot(cat, wo_ref[...])
        r = lax.rsqrt(jnp.mean(mixed * mixed, axis=-1, keepdims=True) + EPS)
        mr = mixed * r
        h1 = x_ref[...] + mr * g2
        h1b = h1.astype(BF16)
        gate = jax.nn.sigmoid(_dot(h1b, wg_ref[...]) + bg_ref[...])
        pb = p_ref[...].astype(BF16)
        wp_blocks = [slice(j * D_PLE, (j + 1) * D_PLE) for j in range(N_CHIPS)]
        pp = jnp.concatenate([_dot(pb, wp_ref[blk, :]) for blk in wp_blocks], axis=1)
        err = h1 + gate * pp - t_ref[...]
        loss_ref[...] += 0.5 * jnp.sum(jnp.mean(err * err, axis=-1, keepdims=True), axis=0, keepdims=True)

        dh2 = err * (1.0 / D_MODEL)
        d_glin = dh2 * pp * gate * (1.0 - gate)
        d_glin_b = d_glin.astype(BF16)
        dwg_ref[...] += _dot_tn(h1b, d_glin_b)
        dbg_ref[...] += jnp.sum(d_glin, axis=0, keepdims=True)
        d_pp = (dh2 * gate).astype(BF16)
        for blk in wp_blocks:
            dwp_ref[blk, :] += _dot_tn(pb, d_pp[:, blk])
        dh1 = dh2 + _dot_nt(d_glin_b, wg_ref[...])
        dh1_ref[...] = dh1
        dg2_ref[...] += jnp.sum(dh1 * mr, axis=0, keepdims=True)
        a_ = dh1 * g2
        d_mixed = (r * a_ - mr * (r * jnp.mean(a_ * mr, axis=-1, keepdims=True))).astype(BF16)
        dwo_ref[...] += _dot_tn(cat, d_mixed)
        d_cat = _dot_nt(d_mixed, wo_ref[...])
        dso_ref[...] = d_cat[:, 0:512]
        dao_ref[...] = d_cat[:, 512:1024]

    return _call(
        body, name="tail_fwd_bwd", grid=(rows // tm,),
        in_specs=[_rows(tm, 512), _rows(tm, 512), _rows(tm, D_MODEL), _rows(tm, D_PLE), _rows(tm, D_MODEL),
                  _full((D_MODEL, D_MODEL)), _full((1, D_MODEL)), _full((D_MODEL, D_MODEL)), _full((1, D_MODEL)),
                  _full((N_CHIPS * D_PLE, D_PLE))],
        out_specs=[_rows(tm, D_MODEL), _rows(tm, 512), _rows(tm, 512), _full((D_MODEL, D_MODEL)),
                   _full((D_MODEL, D_MODEL)), _full((N_CHIPS * D_PLE, D_PLE)), _full((1, D_MODEL)), _full((1, D_MODEL)),
                   _full((1, 1))],
        out_shape=[_sds((rows, D_MODEL)), _sds((rows, 512)), _sds((rows, 512)), _sds((D_MODEL, D_MODEL)),
                   _sds((D_MODEL, D_MODEL)), _sds((N_CHIPS * D_PLE, D_PLE)), _sds((1, D_MODEL)), _sds((1, D_MODEL)),
                   _sds((1, 1))],
        compiler_params=_params(52, ("arbitrary",)),
    )(ssm_out, attn_out, x2d, p2d, target, w_out, g2, w_gate, b_gate, w_proj)


def _local_step(x, hn, p, target, pre_norm_g, w_in_t, s5_params, s5_operands, ssm_d, w_glu, b_glu, sinks, w_out,
                post_norm_g, w_proj, w_gate, b_gate, send_tail_grads=lambda ready: ready["w_out"],
                send_bc=lambda d_bc: (d_bc, d_bc)):
    n_seq, seq, _ = x.shape
    rows = n_seq * seq
    x2d = x.reshape(rows, D_MODEL)
    p2d = p.reshape(rows, D_PLE)
    t2d = target.reshape(rows, D_MODEL)

    l_re, l_im, bt_re, bt_im, cm_re, cm_im = s5_operands

    u_scan, zs, q, k, v, za = _in_proj(hn, w_in_t, n_seq, seq)
    y_scan, h_re, h_im = _s5_scan_fwd(u_scan, bt_re, bt_im, cm_re, cm_im, l_re, l_im, ssm_d, n_seq, seq)
    ssm_out = _glu_fwd(y_scan, zs, w_glu, b_glu, n_seq, seq)
    o, attn_out, lse = _attn_fwd(q, k, v, za, sinks, n_seq, seq)

    dh1, d_so, d_ao, d_w_out, d_w_gate, d_w_proj, d_b_gate, d_g2, loss = _tail(
        ssm_out, attn_out, x2d, p2d, t2d, w_out, post_norm_g, w_gate, b_gate, w_proj)

    dq, dk, dv, dza, d_sinks = _attn_bwd(q, k, v, za, o, lse, d_ao, sinks, n_seq, seq)
    dy_scan, dzs, d_w_glu, d_b_glu = _glu_bwd(y_scan, zs, d_so, w_glu, b_glu, n_seq, seq)
    du_scan, d_bt_re, d_bt_im, d_cm_re, d_cm_im, d_l_re, d_l_im, d_d = _s5_scan_bwd(
        dy_scan, u_scan, h_re, h_im, bt_re, bt_im, cm_re, cm_im, l_re, l_im, ssm_d, n_seq, seq)
    tail_grads_arrived = send_tail_grads(dict(w_out=d_w_out, pl_w_gate=d_w_gate, pl_w_proj=d_w_proj))
    d_lam_re, d_lam_im, d_log_step, d_bc = _s5_params_bwd(
        s5_params, (d_l_re, d_l_im, d_bt_re, d_bt_im, d_cm_re, d_cm_im), tail_grads_arrived)

    sent, arrived = send_bc(d_bc)
    d_proj, d_w_in_early, d_w_in_early_b = _in_proj_bwd_early(hn, du_scan, dzs, dq, dk, dv, dza, sent, n_seq, seq)
    grad_x, d_w_in_late, d_g1 = _in_proj_bwd(x2d, dh1, pre_norm_g, w_in_t, d_proj, arrived)
    grads = dict(
        pre_norm_g=d_g1, w_in_early=d_w_in_early, w_in_early_bf16=d_w_in_early_b, w_in_late=d_w_in_late,
        ssm_lam_re=d_lam_re, ssm_lam_im=d_lam_im, ssm_log_step=d_log_step, ssm_bc=d_bc, ssm_d=d_d, ssm_w_glu=d_w_glu,
        ssm_b_glu=d_b_glu, attn_sinks=d_sinks, w_out=d_w_out, post_norm_g=d_g2, pl_w_proj=d_w_proj,
        pl_w_gate=d_w_gate, pl_b_gate=d_b_gate)
    return grad_x.reshape(x.shape), loss, grads


_BIG = ("w_in", "ssm_w_glu", "w_out", "pl_w_proj", "pl_w_gate")
_BIG_SHARD = {"w_in": (D_IN // N_CHIPS, D_MODEL), "ssm_w_glu": (D_SSM // N_CHIPS, D_SSM),
              "w_out": (D_MODEL // N_CHIPS, D_MODEL), "pl_w_proj": (D_PLE, D_MODEL // N_CHIPS),
              "pl_w_gate": (D_MODEL // N_CHIPS, D_MODEL)}
_SMALL = {"pre_norm_g": (1, D_MODEL), "ssm_lam_re": (SSM_GROUPS, SSM_STATE), "ssm_lam_im": (SSM_GROUPS, SSM_STATE),
          "ssm_log_step": (1, SSM_GROUPS), "ssm_b_re": (D_SSM, SSM_STATE), "ssm_b_im": (D_SSM, SSM_STATE),
          "ssm_c_re": (D_SSM, SSM_STATE), "ssm_c_im": (D_SSM, SSM_STATE), "ssm_d": (1, D_SSM), "ssm_b_glu": (1, D_SSM),
          "attn_sinks": (1, N_HEADS), "post_norm_g": (1, D_MODEL), "pl_b_gate": (1, D_MODEL)}
_VEC_ROWS = ("pre_norm_g", "post_norm_g", "pl_b_gate", "ssm_d", "ssm_b_glu", "attn_sinks", "ssm_log_step", "loss")
_SMALL_GROUPS = (
    ("vec", (8, D_MODEL), tuple((name, r) for r, name in enumerate(_VEC_ROWS))),
    ("lam", (2 * SSM_GROUPS, SSM_STATE), (("ssm_lam_re", 0), ("ssm_lam_im", SSM_GROUPS))),
)
_SMALL_EARLY = ("ssm_b_re", "ssm_b_im", "ssm_c_re", "ssm_c_im")
_SMALL_ORDER = tuple(name for _, _, members in _SMALL_GROUPS for name, _ in members) + _SMALL_EARLY
_WEIGHT_ORDER = ("pre_norm_g", "w_in", "ssm_lam_re", "ssm_lam_im", "ssm_log_step", "ssm_b_re", "ssm_b_im", "ssm_c_re",
                 "ssm_c_im", "ssm_d", "ssm_w_glu", "ssm_b_glu", "attn_sinks", "w_out", "post_norm_g", "pl_w_proj",
                 "pl_w_gate", "pl_b_gate")


def _small_shape(name):
    return (1, 1) if name == "loss" else _SMALL[name]


def _to_kernel_form(name, a):
    a = a[0]
    if name == "w_in":
        return a.T
    if name in ("ssm_b_re", "ssm_b_im"):
        a = a.transpose(0, 2, 1)
    return a.reshape(_SMALL[name]) if name in _SMALL else a


def _from_kernel_form(name, a, shape):
    if name == "w_in":
        a = a.T
    if name in ("ssm_b_re", "ssm_b_im"):
        a = a.reshape(SSM_GROUPS, SSM_GROUP_CH, SSM_STATE).transpose(0, 2, 1)
    return a.reshape(shape)


def _mesh_place():
    x, y, c = lax.axis_index("x"), lax.axis_index("y"), lax.axis_index("c")
    other_chips = ((1 - x, y), (x, 1 - y), (1 - x, 1 - y))
    return x, y, c, other_chips


def _gather_copies(s_refs, g_refs, send_sems, recv_sems, local_sems):
    x, y, c, other_chips = _mesh_place()
    started = []
    for i, (s_ref, g_ref) in enumerate(zip(s_refs, g_refs)):
        rows = s_ref.shape[0]
        half = rows // 2

        def block(chip, g_ref=g_ref, rows=rows, half=half):
            return g_ref.at[pl.ds((2 * chip[0] + chip[1]) * rows + c * half, half), :]

        def copy(k, chip, to, src=None, i=i, block=block):
            return pltpu.make_async_remote_copy(
                src_ref=block(chip) if src is None else src, dst_ref=block(chip), send_sem=send_sems.at[6 * i + k],
                recv_sem=recv_sems.at[6 * i + k], device_id=to, device_id_type=MESH)

        own = pltpu.make_async_copy(s_ref, g_ref.at[pl.ds((2 * x + y) * rows, rows), :], local_sems.at[i])
        own.start()
        first = [copy(k, (x, y), (*chip, c), src=s_ref.at[pl.ds(c * half, half), :])
                 for k, chip in enumerate(other_chips)]
        for cp in first:
            cp.start()
        passed = [copy(3 + k, chip, (x, y, 1 - c)) for k, chip in enumerate(other_chips)]
        started.append((own, first, passed))
    for own, first, passed in started:
        for k in range(3):
            first[k].wait_recv()
            passed[k].start()
    for own, first, passed in started:
        for k in range(3):
            passed[k].wait_recv()
        for cp in first + passed:
            cp.wait_send()
        own.wait()


def _gather_semaphores(n_t):
    return [pltpu.SemaphoreType.DMA((6 * n_t,)), pltpu.SemaphoreType.DMA((6 * n_t,)), pltpu.SemaphoreType.DMA((n_t,))]


def _gather_weights_beside(shards, name, collective_id):
    n_t = len(shards)
    hbm = pltpu.MemorySpace.HBM
    s_refs = [jax.new_ref(s, memory_space=hbm) for s in shards]
    g_refs = [jax.empty_ref(jax.ShapeDtypeStruct((N_CHIPS * s.shape[0], s.shape[1]), s.dtype), memory_space=hbm)
              for s in shards]

    def launch(send_sems, recv_sems, local_sems):
        x, y, c, other_chips = _mesh_place()
        peers = [(*chip, c) for chip in other_chips] + [(x, y, 1 - c)]
        barrier = pltpu.get_barrier_semaphore()
        for peer in peers:
            pl.semaphore_signal(barrier, inc=1, device_id=peer, device_id_type=MESH)
        pl.semaphore_wait(barrier, len(peers))
        _gather_copies(s_refs, g_refs, send_sems, recv_sems, local_sems)

    pl.kernel(launch, mesh=plsc.ScalarSubcoreMesh(axis_name="sequencer", num_cores=1), name=name,
              scratch_types=_gather_semaphores(n_t), compiler_params=pltpu.CompilerParams(collective_id=collective_id))()
    return [g[...] for g in g_refs]


_RELATIONS = tuple(((r >> 2) & 1, (r >> 1) & 1, r & 1) for r in range(1, 8))


def _related(place, relation):
    return tuple(1 - a if flip else a for a, flip in zip(place, relation))


def _scatter_beside(mats, name, collective_id):
    hbm = pltpu.MemorySpace.HBM
    src_refs = [jax.new_ref(a, memory_space=hbm) for a in mats]
    land_refs = [jax.empty_ref(jax.ShapeDtypeStruct((7, a.shape[0] // 8, a.shape[1]), a.dtype), memory_space=hbm)
                 for a in mats]

    def launch(send_sems, recv_sems):
        me = (lax.axis_index("x"), lax.axis_index("y"), lax.axis_index("c"))
        peers = [_related(me, rel) for rel in _RELATIONS]
        barrier = pltpu.get_barrier_semaphore()
        for peer in peers:
            pl.semaphore_signal(barrier, inc=1, device_id=peer, device_id_type=MESH)
        pl.semaphore_wait(barrier, len(peers))
        copies = []
        for i, (src, land) in enumerate(zip(src_refs, land_refs)):
            hr = land.shape[1]
            for k, (tx, ty, tc) in enumerate(peers):
                rows = pl.ds((2 * tx + ty) * 2 * hr + tc * hr, hr)
                copies.append(pltpu.make_async_remote_copy(
                    src_ref=src.at[rows, :], dst_ref=land.at[k], send_sem=send_sems.at[7 * i + k],
                    recv_sem=recv_sems.at[7 * i + k], device_id=(tx, ty, tc), device_id_type=MESH))
                copies[-1].start()
        for cp in copies:
            cp.wait()

    n_sems = 7 * len(mats)
    pl.kernel(launch, mesh=plsc.ScalarSubcoreMesh(axis_name="sequencer", num_cores=1), name=name,
              scratch_types=[pltpu.SemaphoreType.DMA((n_sems,)), pltpu.SemaphoreType.DMA((n_sems,))],
              compiler_params=pltpu.CompilerParams(collective_id=collective_id))()
    return [ref[...] for ref in land_refs]


def _broadcast_beside(arrays):
    hbm = pltpu.MemorySpace.HBM
    src_refs = [jax.new_ref(a, memory_space=hbm) for a in arrays]
    land_refs = [jax.empty_ref(jax.ShapeDtypeStruct((len(_RELATIONS),) + a.shape, a.dtype), memory_space=hbm)
                 for a in arrays]

    def launch(send_sems, recv_sems):
        me = (lax.axis_index("x"), lax.axis_index("y"), lax.axis_index("c"))
        peers = [_related(me, rel) for rel in _RELATIONS]
        barrier = pltpu.get_barrier_semaphore()
        for peer in peers:
            pl.semaphore_signal(barrier, inc=1, device_id=peer, device_id_type=MESH)
        pl.semaphore_wait(barrier, len(peers))
        copies = []
        for i, (src, land) in enumerate(zip(src_refs, land_refs)):
            for k, peer in enumerate(peers):
                copies.append(pltpu.make_async_remote_copy(
                    src_ref=src, dst_ref=land.at[k], send_sem=send_sems.at[7 * i + k],
                    recv_sem=recv_sems.at[7 * i + k], device_id=peer, device_id_type=MESH))
                copies[-1].start()
        for cp in copies:
            cp.wait()

    n_sems = 7 * len(arrays)
    pl.kernel(launch, mesh=plsc.ScalarSubcoreMesh(axis_name="sequencer", num_cores=1), name="broadcast_beside",
              scratch_types=[pltpu.SemaphoreType.DMA((n_sems,)), pltpu.SemaphoreType.DMA((n_sems,))],
              compiler_params=pltpu.CompilerParams(collective_id=3))()
    return [ref[...] for ref in land_refs]


def _exchange_grads(big, outputs, small, landed, own_bc, landed_bc):
    n_t = len(big)
    n_g = len(_SMALL_GROUPS)
    names = _SMALL_ORDER
    halves = [(b.shape[0] // N_CHIPS // 2, b.shape[1]) for b in big]
    early = sorted(landed)
    late = [i for i in range(n_t) if i not in landed]
    n_sems = 4 * n_g + 7 * len(late) + n_t
    small_sem0, block_sem0 = n_t, n_t + len(names)
    early_sem0 = block_sem0 + N_CHIPS * len(late)
    landed_sem0 = early_sem0 + 2 * len(early)
    sent = [n for n in names if n in small]

    def body(*refs):
        pos = 0

        def take(n):
            nonlocal pos
            pos += n
            return refs[pos - n:pos]

        big_refs, small_refs = take(n_t), dict(zip(sent, take(len(sent))))
        land_refs = dict(zip(early, take(len(early))))
        own_bc_ref, landed_bc_ref = take(2)
        out_refs, small_out_refs = take(len(outputs)), dict(zip(names, take(len(names))))
        per_late = lambda: dict(zip(late, take(len(late))))
        ga, gb, pme, send_b, recv_b = per_late(), per_late(), take(n_t), per_late(), per_late()
        own_e, land_e = dict(zip(early, take(len(early)))), dict(zip(early, take(len(early))))
        own_s, land_s = take(2)
        s_own, s_sib, s_chips, s_pair = take(n_g), take(n_g), take(n_g), take(n_g)
        stage = dict(zip(names, take(len(names))))
        send_sems, recv_sems, local_sems = take(3)
        x, y, c, other_chips = _mesh_place()
        me = 2 * x + y
        sibling = (x, y, 1 - c)
        sem_at = iter(range(n_sems))

        def remote(src, dst, to):
            k = next(sem_at)
            return pltpu.make_async_remote_copy(src_ref=src, dst_ref=dst, send_sem=send_sems.at[k],
                                                recv_sem=recv_sems.at[k], device_id=to, device_id_type=MESH)

        order = sorted(late, key=lambda i: halves[i][0] * halves[i][1])
        own_loads, big_swaps = {}, {}
        for i in order:
            hr = halves[i][0]
            own_loads[i], big_swaps[i] = [], []
            for j in range(N_CHIPS):
                mine = big_refs[i].at[pl.ds(j * 2 * hr + c * hr, hr), :]
                theirs = big_refs[i].at[pl.ds(j * 2 * hr + (1 - c) * hr, hr), :]
                sem = local_sems.at[block_sem0 + N_CHIPS * late.index(i) + j]
                own_loads[i].append(pltpu.make_async_copy(mine, ga[i].at[j], sem))
                own_loads[i][j].start()
                big_swaps[i].append(remote(theirs, gb[i].at[j], sibling))
                big_swaps[i][j].start()
        early_loads = {}
        for e, i in enumerate(early):
            hr = halves[i][0]
            mine = big_refs[i].at[pl.ds(me * 2 * hr + c * hr, hr), :]
            early_loads[i] = [pltpu.make_async_copy(mine, own_e[i], local_sems.at[early_sem0 + 2 * e]),
                              pltpu.make_async_copy(land_refs[i], land_e[i], local_sems.at[early_sem0 + 2 * e + 1])]
            for cp in early_loads[i]:
                cp.start()
        loads = [pltpu.make_async_copy(small_refs[name], stage[name], local_sems.at[small_sem0 + names.index(name)])
                 for name in sent]
        landed_loads = [pltpu.make_async_copy(own_bc_ref, own_s, local_sems.at[landed_sem0]),
                        pltpu.make_async_copy(landed_bc_ref, land_s, local_sems.at[landed_sem0 + 1])]
        for cp in loads + landed_loads:
            cp.start()
        for cp in loads:
            cp.wait()
        small_swaps = []
        for gi, (_, _, members) in enumerate(_SMALL_GROUPS):
            s_own[gi][...] = jnp.zeros_like(s_own[gi])
            for name, r0 in members:
                r, n = _small_shape(name)
                s_own[gi][r0:r0 + r, 0:n] = stage[name][...]
            small_swaps.append(remote(s_own[gi], s_sib[gi], sibling))
            small_swaps[gi].start()
        small_sends = []
        for gi in range(n_g):
            small_swaps[gi].wait_recv()
            s_pair[gi][...] = s_own[gi][...] + s_sib[gi][...]
            small_sends.append([remote(s_pair[gi], s_chips[gi].at[k], (*chip, c)) for k, chip in enumerate(other_chips)])
            for cp in small_sends[gi]:
                cp.start()

        def pair_sum(i, j):
            return ga[i][j] + gb[i][j]

        big_sends = {}
        for i in order:
            for j in range(N_CHIPS):
                own_loads[i][j].wait()
                big_swaps[i][j].wait_recv()
            big_sends[i] = []
            for k, chip in enumerate(other_chips):
                send_b[i][k] = pair_sum(i, 2 * chip[0] + chip[1]).astype(BF16)
                big_sends[i].append(remote(send_b[i].at[k], recv_b[i].at[k], (*chip, c)))
                big_sends[i][k].start()
        last_swaps, keeps = {}, {}
        for i in early + order:
            hr = halves[i][0]
            if i in landed:
                for cp in early_loads[i]:
                    cp.wait()
                total = own_e[i][...]
                for k in range(len(_RELATIONS)):
                    total = total + land_e[i][k].astype(F32)
                pme[i][...] = total
            else:
                for k in range(3):
                    big_sends[i][k].wait_recv()
                pme[i][...] = ((pair_sum(i, me) + recv_b[i][0].astype(F32)) + recv_b[i][1].astype(F32)) + recv_b[i][2].astype(F32)
            o, = [o for o, group in enumerate(outputs) if i in group]
            first_col = sum(halves[j][1] for j in outputs[o][:outputs[o].index(i)])
            mine = out_refs[o].at[pl.ds(c * hr, hr), pl.ds(first_col, halves[i][1])]
            keeps[i] = pltpu.make_async_copy(pme[i], mine, local_sems.at[i])
            keeps[i].start()
            last_swaps[i] = remote(pme[i], mine, sibling)
            last_swaps[i].start()

        for gi, (_, _, members) in enumerate(_SMALL_GROUPS):
            for k in range(3):
                small_sends[gi][k].wait_recv()
            total = None
            for j in range(N_CHIPS):
                rel = jnp.bitwise_xor(j, me)
                term = jnp.where(rel == 0, s_pair[gi][...], jnp.where(
                    rel == 2, s_chips[gi][0], jnp.where(rel == 1, s_chips[gi][1], s_chips[gi][2])))
                total = term if total is None else total + term
            s_sib[gi][...] = total
            for name, r0 in members:
                r, n = _small_shape(name)
                stage[name][...] = s_sib[gi][r0:r0 + r, 0:n]
        my_index = 4 * x + 2 * y + c
        for cp in landed_loads:
            cp.wait()
        total = None
        for d in range(2 * N_CHIPS):
            rel = jnp.bitwise_xor(d, my_index)
            term = own_s[...]
            for k in range(len(_RELATIONS)):
                term = jnp.where(rel == k + 1, land_s[k], term)
            total = term.astype(F32) if total is None else total + term.astype(F32)
        for a, name in enumerate(_SMALL_EARLY):
            stage[name][...] = total[:, a * SSM_STATE:(a + 1) * SSM_STATE]
        stores = [pltpu.make_async_copy(stage[name], small_out_refs[name], local_sems.at[small_sem0 + a])
                  for a, name in enumerate(names)]
        for cp in stores:
            cp.start()

        for i in range(n_t):
            last_swaps[i].wait_recv()
            keeps[i].wait()
        for cp in stores:
            cp.wait()
        groups = list(big_swaps.values()) + small_sends + list(big_sends.values())
        for cp in small_swaps + [cp for group in groups for cp in group] + list(last_swaps.values()):
            cp.wait_send()

    any_spec = pl.BlockSpec(memory_space=pl.ANY)
    small_shapes = [_sds(_small_shape(n)) for n in names]
    group_shapes = [shape for _, shape, _ in _SMALL_GROUPS]
    vmem = lambda which, dtype, lead=(): [pltpu.VMEM(lead + halves[i], dtype) for i in which]
    outs = _call(
        body, name="exchange_grads",
        in_specs=[any_spec] * (n_t + len(sent) + len(early) + 2),
        out_specs=[any_spec] * (len(outputs) + len(names)),
        out_shape=[_sds((big[group[0]].shape[0] // N_CHIPS, sum(big[i].shape[1] for i in group))) for group in outputs]
        + small_shapes,
        scratch_shapes=(vmem(late, F32, (N_CHIPS,)) + vmem(late, F32, (N_CHIPS,)) + vmem(range(n_t), F32)
                        + vmem(late, BF16, (3,)) + vmem(late, BF16, (3,))
                        + vmem(early, F32)
                        + [pltpu.VMEM((len(_RELATIONS),) + halves[i], landed[i].dtype) for i in early]
                        + [pltpu.VMEM(own_bc.shape, own_bc.dtype), pltpu.VMEM(landed_bc.shape, landed_bc.dtype)]
                        + [pltpu.VMEM(s, F32) for s in group_shapes] * 2 + [pltpu.VMEM((3,) + s, F32) for s in group_shapes]
                        + [pltpu.VMEM(s, F32) for s in group_shapes]
                        + [pltpu.VMEM(_small_shape(n), F32) for n in names]
                        + [pltpu.SemaphoreType.DMA((n_sems,)), pltpu.SemaphoreType.DMA((n_sems,)),
                           pltpu.SemaphoreType.DMA((landed_sem0 + 2,))]),
        compiler_params=_params(48),
    )(*big, *[small[n] for n in sent], *[landed[i] for i in early], own_bc, landed_bc)
    return list(outs[:len(outputs)]), dict(zip(names, outs[len(outputs):]))


def _adamw_update(w, g, m, v):
    m = ADAM_B1 * m + (1.0 - ADAM_B1) * g
    v = ADAM_B2 * v + (1.0 - ADAM_B2) * (g * g)
    m_hat = m / (1.0 - ADAM_B1 ** ADAM_STEP)
    v_hat = v / (1.0 - ADAM_B2 ** ADAM_STEP)
    return -ADAM_LR * (m_hat / (jnp.sqrt(v_hat) + ADAM_EPS) + ADAM_WD * w), m, v


def _adamw(w, g, m, v, grid, name):
    n_t = len(w)

    def body(*refs):
        ins, outs = refs[:4 * n_t], refs[4 * n_t:]
        for i in range(n_t):
            w_, g_, m_, v_ = [ins[a * n_t + i][...] for a in range(4)]
            vals = (g_,) + _adamw_update(w_, g_, m_, v_)
            for a in range(4):
                outs[a * n_t + i][...] = vals[a]

    specs = [pl.BlockSpec((a.shape[0] // grid, a.shape[1]), lambda i: (i, 0)) for a in w]
    shapes = [_sds(a.shape) for a in w]
    outs = _call(
        body, name=name, grid=(grid,), in_specs=specs * 4, out_specs=specs * 4, out_shape=shapes * 4,
        compiler_params=_params(40, ("arbitrary",)),
    )(*w, *g, *m, *v)
    return [outs[a * n_t:(a + 1) * n_t] for a in range(4)]


def kernel(x, p, pre_norm_g, w_in, ssm_lam_re, ssm_lam_im, ssm_log_step, ssm_b_re, ssm_b_im, ssm_c_re, ssm_c_im, ssm_d, ssm_w_glu, ssm_b_glu, attn_sinks, w_out, post_norm_g, pl_w_proj, pl_w_gate, pl_b_gate, loss_target, m_pre_norm_g, m_w_in, m_ssm_lam_re, m_ssm_lam_im, m_ssm_log_step, m_ssm_b_re, m_ssm_b_im, m_ssm_c_re, m_ssm_c_im, m_ssm_d, m_ssm_w_glu, m_ssm_b_glu, m_attn_sinks, m_w_out, m_post_norm_g, m_pl_w_proj, m_pl_w_gate, m_pl_b_gate, v_pre_norm_g, v_w_in, v_ssm_lam_re, v_ssm_lam_im, v_ssm_log_step, v_ssm_b_re, v_ssm_b_im, v_ssm_c_re, v_ssm_c_im, v_ssm_d, v_ssm_w_glu, v_ssm_b_glu, v_attn_sinks, v_w_out, v_post_norm_g, v_pl_w_proj, v_pl_w_gate, v_pl_b_gate):
    weights = dict(pre_norm_g=pre_norm_g, w_in=w_in, ssm_lam_re=ssm_lam_re, ssm_lam_im=ssm_lam_im,
                   ssm_log_step=ssm_log_step, ssm_b_re=ssm_b_re, ssm_b_im=ssm_b_im, ssm_c_re=ssm_c_re,
                   ssm_c_im=ssm_c_im, ssm_d=ssm_d, ssm_w_glu=ssm_w_glu, ssm_b_glu=ssm_b_glu, attn_sinks=attn_sinks,
                   w_out=w_out, post_norm_g=post_norm_g, pl_w_proj=pl_w_proj, pl_w_gate=pl_w_gate, pl_b_gate=pl_b_gate)
    m_in = dict(pre_norm_g=m_pre_norm_g, w_in=m_w_in, ssm_lam_re=m_ssm_lam_re, ssm_lam_im=m_ssm_lam_im,
                ssm_log_step=m_ssm_log_step, ssm_b_re=m_ssm_b_re, ssm_b_im=m_ssm_b_im, ssm_c_re=m_ssm_c_re,
                ssm_c_im=m_ssm_c_im, ssm_d=m_ssm_d, ssm_w_glu=m_ssm_w_glu, ssm_b_glu=m_ssm_b_glu,
                attn_sinks=m_attn_sinks, w_out=m_w_out, post_norm_g=m_post_norm_g, pl_w_proj=m_pl_w_proj,
                pl_w_gate=m_pl_w_gate, pl_b_gate=m_pl_b_gate)
    v_in = dict(pre_norm_g=v_pre_norm_g, w_in=v_w_in, ssm_lam_re=v_ssm_lam_re, ssm_lam_im=v_ssm_lam_im,
                ssm_log_step=v_ssm_log_step, ssm_b_re=v_ssm_b_re, ssm_b_im=v_ssm_b_im, ssm_c_re=v_ssm_c_re,
                ssm_c_im=v_ssm_c_im, ssm_d=v_ssm_d, ssm_w_glu=v_ssm_w_glu, ssm_b_glu=v_ssm_b_glu,
                attn_sinks=v_attn_sinks, w_out=v_w_out, post_norm_g=v_post_norm_g, pl_w_proj=v_pl_w_proj,
                pl_w_gate=v_pl_w_gate, pl_b_gate=v_pl_b_gate)

    def two_d(tree):
        return {k: _to_kernel_form(k, a) for k, a in tree.items()}

    w2, m2, v2 = two_d(weights), two_d(m_in), two_d(v_in)

    (w_in_full,) = _gather_weights_beside([w2["w_in"].astype(BF16)], "gather_w_in_beside", 4)
    s5_params = tuple(w2[n] for n in ("ssm_lam_re", "ssm_lam_im", "ssm_log_step", "ssm_b_re", "ssm_b_im", "ssm_c_re",
                                      "ssm_c_im"))
    s5_operands = _s5_params_fwd(*s5_params)
    hn = _pre_norm(x.reshape(-1, D_MODEL), w2["pre_norm_g"])
    behind = s5_operands[0][0, 0] * 0.0 + hn[0, 0].astype(F32) * 0.0
    rest = _gather_weights_beside([(w2[n] + behind).astype(BF16) for n in _BIG[1:]], "gather_weights_beside", 1)
    full = dict(zip(_BIG, [w_in_full] + rest))
    mats = ("w_in_early", "w_in_late") + _BIG[1:]
    landed, bc = {}, {}

    def send_tail_grads(ready):
        sent_early = ("w_out", "pl_w_gate", "pl_w_proj")
        landed.update(zip([mats.index(n) for n in sent_early],
                          _scatter_beside([ready[n] for n in sent_early], "scatter_beside", 2)))
        return landed[mats.index(sent_early[-1])]

    def send_bc(d_bc):
        bc["own"] = d_bc
        bc["landed"] = _broadcast_beside([d_bc])[0]
        return d_bc, bc["landed"]

    grad_x, loss, grads = _local_step(
        x, hn, p, loss_target, w2["pre_norm_g"], full["w_in"], s5_params, s5_operands, w2["ssm_d"], full["ssm_w_glu"], w2["ssm_b_glu"],
        w2["attn_sinks"], full["w_out"], w2["post_norm_g"], full["pl_w_proj"], full["pl_w_gate"], w2["pl_b_gate"], send_tail_grads, send_bc)

    landed[0] = _scatter_beside([grads["w_in_early_bf16"]], "scatter_w_in_beside", 5)[0]
    sent_here = {**{n: grads[n] for n in _SMALL if n not in _SMALL_EARLY}, "loss": loss}
    halves_of_w_in = ((0, 1),) + tuple((i,) for i in range(2, len(mats)))
    g_big, g_small = _exchange_grads([grads[n] for n in mats], halves_of_w_in, sent_here, landed, bc["own"], bc["landed"])
    g_big = dict(zip(_BIG, g_big))
    total_loss = g_small.pop("loss")

    big_out = _adamw([w2[n] for n in _BIG], [g_big[n] for n in _BIG], [m2[n] for n in _BIG], [v2[n] for n in _BIG],
                     8, "adamw_matrices")
    small_names = tuple(_SMALL)
    small_out = _adamw([w2[n] for n in small_names], [g_small[n] for n in small_names], [m2[n] for n in small_names],
                       [v2[n] for n in small_names], 1, "adamw_small")

    results = [{**dict(zip(_BIG, big_part)), **dict(zip(small_names, small_part))}
               for big_part, small_part in zip(big_out, small_out)]
    flat = [_from_kernel_form(name, r[name], weights[name].shape) for r in results for name in _WEIGHT_ORDER]
    return (total_loss.reshape(()), grad_x, *flat)
```

```python
import math

import jax
import jax.numpy as jnp
from jax import lax
from jax.experimental import pallas as pl
from jax.experimental.pallas import tpu as pltpu
from jax.experimental.pallas import tpu_sc as plsc

F32 = jnp.float32
BF16 = jnp.bfloat16

D_MODEL = 1024
D_SSM = 512
D_ATTN = 512
SSM_GROUPS = 32
SSM_GROUP_CH = 16
SSM_STATE = 64
SSM_LANES = SSM_GROUPS * SSM_STATE
HEAD_DIM = 64
N_HEADS = 8
KV_HEADS = 2
Q_PER_KV = 4
WINDOW = 128
BLOCK = 128
D_PLE = 256
D_IN = 2304
EPS = 1e-6
ATTN_SCALE = 1.0 / math.sqrt(HEAD_DIM)

ADAM_LR = 0.001
ADAM_B1 = 0.9
ADAM_B2 = 0.999
ADAM_EPS = 1e-08
ADAM_WD = 0.01
ADAM_STEP = 10

N_CHIPS = 4
LANES = 128
SCAN_CHUNKS = 8
SCAN_TILE_STEPS = 32
SCAN_LANE_CHUNK = 512
MIB = 2 ** 20
MESH = pl.DeviceIdType.MESH


def _dot(a, b):
    return jnp.dot(a, b, preferred_element_type=F32)


def _dot_nt(a, b):
    return lax.dot_general(a, b, (((1,), (1,)), ((), ())), preferred_element_type=F32)


def _dot_tn(a, b):
    return lax.dot_general(a, b, (((0,), (0,)), ((), ())), preferred_element_type=F32)


def _params(vmem_mib, semantics=None):
    kw = dict(vmem_limit_bytes=vmem_mib * MIB)
    if semantics is not None:
        kw["dimension_semantics"] = semantics
    return pltpu.CompilerParams(**kw)


def _full(shape):
    nd = len(shape)
    return pl.BlockSpec(shape, lambda *_: (0,) * nd, pipeline_mode=pl.Buffered(1))


def _rows(tm, width):
    return pl.BlockSpec((tm, width), lambda i: (i, 0))


def _sds(shape, dtype=F32):
    return pltpu.HBM(shape, dtype)


def _call(body, **kw):
    fn = pl.pallas_call(body, **kw)
    return lambda *args: fn(*[pltpu.with_memory_space_constraint(a, pltpu.HBM) for a in args])


def _silu(z):
    return z * jax.nn.sigmoid(z)


def _pre_norm(x2d, g1):
    rows = x2d.shape[0]
    tm = 512

    def body(x_ref, g_ref, hn_ref):
        x = x_ref[...]
        r = lax.rsqrt(jnp.mean(x * x, axis=-1, keepdims=True) + EPS)
        hn_ref[...] = (x * r * g_ref[...]).astype(BF16)

    return _call(
        body, name="pre_norm", grid=(rows // tm,), in_specs=[_rows(tm, D_MODEL), _full((1, D_MODEL))],
        out_specs=_rows(tm, D_MODEL), out_shape=_sds((rows, D_MODEL), BF16), compiler_params=_params(32, ("arbitrary",)),
    )(x2d, g1)


def _in_proj(hn, w_in_t, n_seq, seq):
    rows = hn.shape[0]
    tm = 1024
    slab, steps, _, _ = _scan_geometry(n_seq, seq)

    def body(hn_ref, w_ref, *out_refs):
        u_parts, (zs_ref, q_ref, k_ref, v_ref, za_ref) = out_refs[:_SCAN_PARTS], out_refs[_SCAN_PARTS:]
        whole = _dot_nt(hn_ref[...], w_ref[...])

        def proj(a, b):
            return whole[:, a:b]

        _store_chunks(u_parts, pl.program_id(0) * (tm // steps), proj(0, 512), steps, slab)
        zs_ref[...] = proj(512, 1024)
        q_ref[...] = (proj(1024, 1536) * ATTN_SCALE).astype(BF16)
        k_ref[...] = proj(1536, 1664).astype(BF16)
        v_ref[...] = proj(1664, 1792).astype(BF16)
        za_ref[...] = proj(1792, 2304)

    *u_parts, zs, q, k, v, za = _call(
        body, name="in_proj", grid=(rows // tm,),
        in_specs=[_rows(tm, D_MODEL), _full((D_IN, D_MODEL))],
        out_specs=_whole_parts(rows) + [_rows(tm, 512), _rows(tm, 512), _rows(tm, 128), _rows(tm, 128), _rows(tm, 512)],
        out_shape=_part_shapes(rows) + [_sds((rows, 512)), _sds((rows, 512), BF16), _sds((rows, 128), BF16),
                                        _sds((rows, 128), BF16), _sds((rows, 512))],
        compiler_params=_params(48, ("arbitrary",)),
    )(hn, w_in_t)
    return u_parts, zs, q, k, v, za


_EARLY_COLS = D_MODEL // 2


def _in_proj_bwd_early(hn, du_parts, dzs, dq, dk, dv, dza, runs_after, n_seq, seq):
    rows = hn.shape[0]
    tm = 1024
    slab, steps, _, _ = _scan_geometry(n_seq, seq)

    def body(hn_ref, *refs):
        du_parts, (dzs_ref, dq_ref, dk_ref, dv_ref, dza_ref, _, dproj_ref, dw_ref, dwb_ref) = refs[:_SCAN_PARTS], refs[_SCAN_PARTS:]
        i = pl.program_id(0)

        @pl.when(i == 0)
        def _():
            dw_ref[...] = jnp.zeros_like(dw_ref)

        du = _load_chunks(du_parts, i * (tm // steps), tm // steps, steps, slab)
        d_proj = jnp.concatenate([du.astype(BF16), dzs_ref[...], dq_ref[...], dk_ref[...], dv_ref[...], dza_ref[...]],
                                 axis=1)
        dproj_ref[...] = d_proj
        dw_ref[...] += _dot_tn(d_proj, hn_ref[...])

        @pl.when(i == rows // tm - 1)
        def _():
            dwb_ref[...] = dw_ref[...].astype(BF16)

    return _call(
        body, name="in_proj_bwd_early", grid=(rows // tm,),
        in_specs=[_rows(tm, _EARLY_COLS)] + _whole_parts(rows)
        + [_rows(tm, 512), _rows(tm, 512), _rows(tm, 128), _rows(tm, 128), _rows(tm, 512),
           pl.BlockSpec(memory_space=pl.ANY)],
        out_specs=[_rows(tm, D_IN), _full((D_IN, _EARLY_COLS)), _full((D_IN, _EARLY_COLS))],
        out_shape=[_sds((rows, D_IN), BF16), _sds((D_IN, _EARLY_COLS)), _sds((D_IN, _EARLY_COLS), BF16)],
        compiler_params=_params(48, ("arbitrary",)),
    )(hn, *du_parts, dzs, dq, dk, dv, dza, runs_after)


def _in_proj_bwd(x2d, dh1, g1, w_in_t, d_proj, runs_after):
    rows = x2d.shape[0]
    tm = 512

    def body(x_ref, dh1_ref, g_ref, w_ref, dproj_ref, _, gx_ref, dw_ref, dg_ref):
        @pl.when(pl.program_id(0) == 0)
        def _():
            dw_ref[...] = jnp.zeros_like(dw_ref)
            dg_ref[...] = jnp.zeros_like(dg_ref)

        x = x_ref[...]
        g = g_ref[...]
        r = lax.rsqrt(jnp.mean(x * x, axis=-1, keepdims=True) + EPS)
        xr = x * r
        hn = (xr[:, _EARLY_COLS:] * g[:, _EARLY_COLS:]).astype(BF16)
        d_proj = dproj_ref[...]
        dhn = _dot(d_proj, w_ref[...])
        dw_ref[...] += _dot_tn(d_proj, hn)
        dg_ref[...] += jnp.sum(dhn * xr, axis=0, keepdims=True)
        a_ = dhn * g
        gx_ref[...] = dh1_ref[...] + r * a_ - xr * (r * jnp.mean(a_ * xr, axis=-1, keepdims=True))

    late_cols = D_MODEL - _EARLY_COLS
    return _call(
        body, name="in_proj_bwd", grid=(rows // tm,),
        in_specs=[_rows(tm, D_MODEL), _rows(tm, D_MODEL), _full((1, D_MODEL)), _full((D_IN, D_MODEL)), _rows(tm, D_IN),
                  pl.BlockSpec(memory_space=pl.ANY)],
        out_specs=[_rows(tm, D_MODEL), _full((D_IN, late_cols)), _full((1, D_MODEL))],
        out_shape=[_sds((rows, D_MODEL)), _sds((D_IN, late_cols)), _sds((1, D_MODEL))],
        compiler_params=_params(52, ("arbitrary",)),
    )(x2d, dh1, g1, w_in_t, d_proj, runs_after)


def _iota(shape, axis):
    return lax.broadcasted_iota(jnp.int32, shape, axis)


def _sum_of_thirds(f, a):
    hi = a.astype(BF16)
    rest = a - hi.astype(F32)
    mid = rest.astype(BF16)
    low = (rest - mid.astype(F32)).astype(BF16)
    return (f(hi) + f(mid)) + f(low)


@jax.custom_vjp
def _pick_rows(e, a):
    return _sum_of_thirds(lambda part: _dot(e, part), a)


def _pick_rows_fwd(e, a):
    return _pick_rows(e, a), e


def _pick_rows_bwd(e, ct):
    return jnp.zeros_like(e), _sum_of_thirds(lambda part: _dot_tn(e, part), ct)


_pick_rows.defvjp(_pick_rows_fwd, _pick_rows_bwd)


@jax.custom_vjp
def _pick_cols(a, e):
    return _sum_of_thirds(lambda part: _dot(part, e), a)


def _pick_cols_fwd(a, e):
    return _pick_cols(a, e), e


def _pick_cols_bwd(e, ct):
    return _sum_of_thirds(lambda part: _dot_nt(part, e), ct), jnp.zeros_like(e)


_pick_cols.defvjp(_pick_cols_fwd, _pick_cols_bwd)


_HALF_GROUPS = SSM_GROUPS // 2
_N_SHIFT = SSM_STATE.bit_length() - 1
_P_SHIFT = SSM_GROUP_CH.bit_length() - 1


def _s5_operands(lam_re, lam_im, log_step, b_re, b_im, c_re, c_im):
    g, n, p = SSM_GROUPS, SSM_STATE, SSM_GROUP_CH
    gn, gp, hn_, hp = g * n, g * p, _HALF_GROUPS * n, _HALF_GROUPS * p
    eye_g = _iota((g, g), 0) == _iota((g, g), 1)
    step = jnp.sum(jnp.where(eye_g, jnp.exp(log_step), 0.0), axis=1, keepdims=True)
    a_re = lam_re * step
    a_im = lam_im * step
    mag = jnp.exp(a_re)
    lbar_re = mag * jnp.cos(a_im)
    lbar_im = mag * jnp.sin(a_im)
    n_re = lbar_re - 1.0
    den = lam_re * lam_re + lam_im * lam_im
    f_re = (n_re * lam_re + lbar_im * lam_im) / den
    f_im = (lbar_im * lam_re - n_re * lam_im) / den

    spread_n = (_iota((n, gn), 0) == (_iota((n, gn), 1) & (n - 1))).astype(BF16)
    own_g = _iota((g, gn), 0) == (_iota((g, gn), 1) >> _N_SHIFT)

    def to_row(a):
        return jnp.sum(jnp.where(own_g, _pick_cols(a, spread_n), 0.0), axis=0, keepdims=True)

    per_group = ((_iota((gp, g), 0) >> _P_SHIFT) == _iota((gp, g), 1)).astype(BF16)
    fx_re, fx_im = _pick_rows(per_group, f_re), _pick_rows(per_group, f_im)
    bbar_re = fx_re * b_re - fx_im * b_im
    bbar_im = fx_re * b_im + fx_im * b_re

    tile_n = (_iota((n, hn_), 0) == (_iota((n, hn_), 1) & (n - 1))).astype(BF16)
    same_group = (_iota((hp, hn_), 0) >> _P_SHIFT) == (_iota((hp, hn_), 1) >> _N_SHIFT)

    def embed(a, hf):
        return jnp.where(same_group, _pick_cols(a[hf * hp:(hf + 1) * hp], tile_n), 0.0)

    return (to_row(lbar_re), to_row(lbar_im), embed(bbar_re, 0), embed(bbar_re, 1), embed(bbar_im, 0),
            embed(bbar_im, 1), embed(c_re, 0), embed(c_re, 1), embed(c_im, 0), embed(c_im, 1))


_S5_PARAM_SHAPES = ((SSM_GROUPS, SSM_STATE), (SSM_GROUPS, SSM_STATE), (1, SSM_GROUPS),
                    (D_SSM, SSM_STATE), (D_SSM, SSM_STATE), (D_SSM, SSM_STATE), (D_SSM, SSM_STATE))
_CM_SHAPE = (2, _HALF_GROUPS * SSM_GROUP_CH, _HALF_GROUPS * SSM_STATE)
_S5_OPERAND_SHAPES = ((1, SSM_LANES), (1, SSM_LANES), _CM_SHAPE, _CM_SHAPE, _CM_SHAPE, _CM_SHAPE)


def _s5_params_fwd(*params):
    def body(*refs):
        ins, (lre_ref, lim_ref, btre_ref, btim_ref, cmre_ref, cmim_ref) = refs[:7], refs[7:]
        vals = _s5_operands(*[r[...] for r in ins])
        lre_ref[...] = vals[0]
        lim_ref[...] = vals[1]
        for ref, pair in zip((btre_ref, btim_ref, cmre_ref, cmim_ref), (vals[2:4], vals[4:6], vals[6:8], vals[8:10])):
            ref[0] = pair[0].astype(BF16)
            ref[1] = pair[1].astype(BF16)

    dtypes = (F32, F32, BF16, BF16, BF16, BF16)
    return _call(
        body, name="s5_params_fwd",
        in_specs=[_full(s) for s in _S5_PARAM_SHAPES], out_specs=[_full(s) for s in _S5_OPERAND_SHAPES],
        out_shape=[_sds(s, d) for s, d in zip(_S5_OPERAND_SHAPES, dtypes)], compiler_params=_params(32),
    )(*params)


_BC_SIDE_BY_SIDE = (D_SSM, 4 * SSM_STATE)


def _s5_params_bwd(params, cotangents, runs_after):
    def body(*refs):
        ins, (dlre, dlim, dbtre, dbtim, dcmre, dcmim), outs = refs[:7], refs[7:13], refs[14:]
        _, vjp = jax.vjp(_s5_operands, *[r[...] for r in ins])
        cts = (dlre[...], dlim[...], dbtre[0], dbtre[1], dbtim[0], dbtim[1], dcmre[0], dcmre[1], dcmim[0], dcmim[1])
        grads = vjp(cts)
        for ref, val in zip(outs[:3], grads[:3]):
            ref[...] = val
        outs[3][...] = jnp.concatenate(grads[3:], axis=1).astype(BF16)

    out_shapes = _S5_PARAM_SHAPES[:3] + (_BC_SIDE_BY_SIDE,)
    return _call(
        body, name="s5_params_bwd",
        in_specs=[_full(s) for s in _S5_PARAM_SHAPES + _S5_OPERAND_SHAPES] + [pl.BlockSpec(memory_space=pl.ANY)],
        out_specs=[_full(s) for s in out_shapes],
        out_shape=[_sds(s, d) for s, d in zip(out_shapes, (F32, F32, F32, BF16))], compiler_params=_params(48),
    )(*params, *cotangents, runs_after)


def _scan_geometry(n_seq, seq):
    slab = n_seq * SCAN_CHUNKS
    steps = seq // SCAN_CHUNKS
    tile_rows = slab * SCAN_TILE_STEPS
    n_tiles = steps // SCAN_TILE_STEPS
    return slab, steps, tile_rows, n_tiles


_SCAN_PARTS = D_SSM // LANES


def _whole_parts(rows):
    return [_full((rows, LANES))] * _SCAN_PARTS


def _part_shapes(rows):
    return [_sds((rows, LANES))] * _SCAN_PARTS


def _load_chunks(parts, first_chunk, n_chunks, steps, slab):
    return jnp.concatenate([
        jnp.concatenate([ref[pl.ds(first_chunk + q, steps, stride=slab), :] for ref in parts], axis=1)
        for q in range(n_chunks)], axis=0)


def _store_chunks(parts, first_chunk, value, steps, slab):
    for q in range(value.shape[0] // steps):
        for j, ref in enumerate(parts):
            ref[pl.ds(first_chunk + q, steps, stride=slab), :] = value[q * steps:(q + 1) * steps,
                                                                     j * LANES:(j + 1) * LANES]


def _join_parts(parts):
    return jnp.concatenate([ref[...] for ref in parts], axis=1)


def _split_parts(parts, value):
    for j, ref in enumerate(parts):
        ref[...] = value[:, j * LANES:(j + 1) * LANES]


def _complex_power(re, im, n):
    out = None
    while n:
        if n & 1:
            out = (re, im) if out is None else (out[0] * re - out[1] * im, out[0] * im + out[1] * re)
        n >>= 1
        if n:
            re, im = re * re - im * im, 2.0 * re * im
    return out


def _chunk_carry(sum_re, sum_im, carry_re, carry_im, a_re, a_im, n_seq, reverse):
    carry_re[...] = jnp.zeros_like(carry_re)
    carry_im[...] = jnp.zeros_like(carry_im)
    for s in range(n_seq):
        order = range(SCAN_CHUNKS - 2, -1, -1) if reverse else range(1, SCAN_CHUNKS)
        for c in order:
            r = s * SCAN_CHUNKS + c
            p = r + 1 if reverse else r - 1
            p_re, p_im = carry_re[p:p + 1, :], carry_im[p:p + 1, :]
            carry_re[r:r + 1, :] = a_re * p_re - a_im * p_im + sum_re[p:p + 1, :]
            carry_im[r:r + 1, :] = a_re * p_im + a_im * p_re + sum_im[p:p + 1, :]


def _s5_scan_fwd(u_parts, bt_re, bt_im, cm_re, cm_im, lbar_re, lbar_im, d_row, n_seq, seq):
    slab, steps, tile_rows, n_tiles = _scan_geometry(n_seq, seq)
    rows = u_parts[0].shape[0]

    def body(*refs):
        u_refs, refs = refs[:_SCAN_PARTS], refs[_SCAN_PARTS:]
        (bre_ref, bim_ref, cre_ref, cim_ref, lre_ref, lim_ref, d_ref), refs = refs[:7], refs[7:]
        y_refs, (hre_ref, him_ref, st_re, st_im, h0_re, h0_im, buf_re, buf_im) = refs[:_SCAN_PARTS], refs[_SCAN_PARTS:]
        second = pl.program_id(0) == 1
        i = pl.program_id(1)

        @pl.when(jnp.logical_and(i == 0, jnp.logical_not(second)))
        def _():
            st_re[...] = jnp.zeros_like(st_re)
            st_im[...] = jnp.zeros_like(st_im)

        u = _join_parts(u_refs)
        ub = u.astype(BF16)
        for hf in range(2):
            cols = slice(hf * 1024, (hf + 1) * 1024)
            buf_re[:, cols] = _dot(ub[:, hf * 256:(hf + 1) * 256], bre_ref[hf])
            buf_im[:, cols] = _dot(ub[:, hf * 256:(hf + 1) * 256], bim_ref[hf])

        for lc in range(SSM_LANES // SCAN_LANE_CHUNK):
            cols = slice(lc * SCAN_LANE_CHUNK, (lc + 1) * SCAN_LANE_CHUNK)
            l_re = jnp.broadcast_to(lre_ref[:, cols], (slab, SCAN_LANE_CHUNK))
            l_im = jnp.broadcast_to(lim_ref[:, cols], (slab, SCAN_LANE_CHUNK))

            def scan_tile(keep_states):
                def step(t, carry):
                    s_re, s_im = carry
                    r0 = pl.multiple_of(t * slab, slab)
                    n_re = l_re * s_re - l_im * s_im + buf_re[pl.ds(r0, slab), cols]
                    n_im = l_re * s_im + l_im * s_re + buf_im[pl.ds(r0, slab), cols]
                    if keep_states:
                        buf_re[pl.ds(r0, slab), cols] = n_re
                        buf_im[pl.ds(r0, slab), cols] = n_im
                    return n_re, n_im

                s_re, s_im = lax.fori_loop(0, SCAN_TILE_STEPS, step, (st_re[:, cols], st_im[:, cols]), unroll=True)
                st_re[:, cols] = s_re
                st_im[:, cols] = s_im

            pl.when(jnp.logical_not(second))(lambda: scan_tile(False))
            pl.when(second)(lambda: scan_tile(True))

        @pl.when(jnp.logical_and(i == n_tiles - 1, jnp.logical_not(second)))
        def _():
            a_re, a_im = _complex_power(lre_ref[...], lim_ref[...], steps)
            _chunk_carry(st_re, st_im, h0_re, h0_im, a_re, a_im, n_seq, reverse=False)
            st_re[...] = h0_re[...]
            st_im[...] = h0_im[...]

        @pl.when(second)
        def _():
            h_re = buf_re[...].astype(BF16)
            h_im = buf_im[...].astype(BF16)
            hre_ref[...] = h_re
            him_ref[...] = h_im
            for hf in range(2):
                cols = slice(hf * 1024, (hf + 1) * 1024)
                ycols = slice(hf * 256, (hf + 1) * 256)
                y_half = (_dot_nt(h_re[:, cols], cre_ref[hf]) - _dot_nt(h_im[:, cols], cim_ref[hf])
                          + d_ref[:, ycols] * u[:, ycols])
                _split_parts(y_refs[2 * hf:2 * hf + 2], y_half)

    tile = lambda w: pl.BlockSpec((tile_rows, w), lambda p, i: (i, 0))
    out_tile = lambda w: pl.BlockSpec((tile_rows, w), lambda p, i: (i * p, 0))
    cm = _full(_CM_SHAPE)
    outs = _call(
        body, name="s5_scan_fwd", grid=(2, n_tiles),
        in_specs=[tile(LANES)] * _SCAN_PARTS + [cm, cm, cm, cm, _full((1, SSM_LANES)), _full((1, SSM_LANES)),
                                                _full((1, 512))],
        out_specs=[out_tile(LANES)] * _SCAN_PARTS + [out_tile(SSM_LANES), out_tile(SSM_LANES)],
        out_shape=_part_shapes(rows) + [_sds((rows, SSM_LANES), BF16), _sds((rows, SSM_LANES), BF16)],
        scratch_shapes=[pltpu.VMEM((slab, SSM_LANES), F32)] * 4 + [pltpu.VMEM((tile_rows, SSM_LANES), F32)] * 2,
        compiler_params=_params(40, ("arbitrary", "arbitrary")),
    )(*u_parts, bt_re, bt_im, cm_re, cm_im, lbar_re, lbar_im, d_row)
    return outs[:_SCAN_PARTS], outs[_SCAN_PARTS], outs[_SCAN_PARTS + 1]


def _s5_scan_bwd(dy_parts, u_parts, h_re, h_im, bt_re, bt_im, cm_re, cm_im, lbar_re, lbar_im, d_row, n_seq, seq):
    slab, steps, tile_rows, n_tiles = _scan_geometry(n_seq, seq)
    rows = u_parts[0].shape[0]

    def body(*refs):
        dy_refs, u_refs, refs = refs[:_SCAN_PARTS], refs[_SCAN_PARTS:2 * _SCAN_PARTS], refs[2 * _SCAN_PARTS:]
        (hre_ref, him_ref, bre_ref, bim_ref, cre_ref, cim_ref, lre_ref, lim_ref, d_ref), refs = refs[:9], refs[9:]
        du_refs, refs = refs[:_SCAN_PARTS], refs[_SCAN_PARTS:]
        (dbre_ref, dbim_ref, dcre_ref, dcim_ref, dlre_ref, dlim_ref, dd_ref,
         st_re, st_im, g0_re, g0_im, acc_re, acc_im, buf_re, buf_im) = refs
        second = pl.program_id(0) == 1
        i = pl.program_id(1)

        @pl.when(jnp.logical_and(i == 0, jnp.logical_not(second)))
        def _():
            st_re[...] = jnp.zeros_like(st_re)
            st_im[...] = jnp.zeros_like(st_im)
            acc_re[...] = jnp.zeros_like(acc_re)
            acc_im[...] = jnp.zeros_like(acc_im)
            for ref in (dbre_ref, dbim_ref, dcre_ref, dcim_ref, dd_ref):
                ref[...] = jnp.zeros_like(ref)

        dy = _join_parts(dy_refs)
        dyb = dy.astype(BF16)
        for hf in range(2):
            cols = slice(hf * 1024, (hf + 1) * 1024)
            buf_re[:, cols] = _dot(dyb[:, hf * 256:(hf + 1) * 256], cre_ref[hf])
            buf_im[:, cols] = -_dot(dyb[:, hf * 256:(hf + 1) * 256], cim_ref[hf])

        for lc in range(SSM_LANES // SCAN_LANE_CHUNK):
            cols = slice(lc * SCAN_LANE_CHUNK, (lc + 1) * SCAN_LANE_CHUNK)
            l_re = jnp.broadcast_to(lre_ref[:, cols], (slab, SCAN_LANE_CHUNK))
            l_im = jnp.broadcast_to(lim_ref[:, cols], (slab, SCAN_LANE_CHUNK))

            def advance(r0, s_re, s_im):
                n_re = l_re * s_re + l_im * s_im + buf_re[pl.ds(r0, slab), cols]
                n_im = l_re * s_im - l_im * s_re + buf_im[pl.ds(r0, slab), cols]
                buf_re[pl.ds(r0, slab), cols] = n_re
                buf_im[pl.ds(r0, slab), cols] = n_im
                return n_re, n_im

            def row0(k):
                return pl.multiple_of((SCAN_TILE_STEPS - 1 - k) * slab, slab)

            @pl.when(jnp.logical_not(second))
            def _():
                s_re, s_im = lax.fori_loop(0, SCAN_TILE_STEPS, lambda k, s: advance(row0(k), *s),
                                           (st_re[:, cols], st_im[:, cols]), unroll=True)
                st_re[:, cols] = s_re
                st_im[:, cols] = s_im

            @pl.when(second)
            def _():
                def step(k, carry):
                    s_re, s_im, a_re, a_im = carry
                    r0 = row0(k)
                    hr = hre_ref[pl.ds(r0, slab), cols].astype(F32)
                    hi = him_ref[pl.ds(r0, slab), cols].astype(F32)
                    a_re = a_re + s_re * hr + s_im * hi
                    a_im = a_im + s_im * hr - s_re * hi
                    return advance(r0, s_re, s_im) + (a_re, a_im)

                zero = jnp.zeros((slab, SCAN_LANE_CHUNK), F32)
                s_re, s_im, a_re, a_im = lax.fori_loop(
                    0, SCAN_TILE_STEPS, step, (st_re[:, cols], st_im[:, cols], zero, zero), unroll=True)
                st_re[:, cols] = s_re
                st_im[:, cols] = s_im
                acc_re[:, cols] += a_re
                acc_im[:, cols] += a_im

        @pl.when(jnp.logical_and(i == n_tiles - 1, jnp.logical_not(second)))
        def _():
            p_re, p_im = _complex_power(lre_ref[...], lim_ref[...], steps)
            _chunk_carry(st_re, st_im, g0_re, g0_im, p_re, -p_im, n_seq, reverse=True)
            st_re[...] = g0_re[...]
            st_im[...] = g0_im[...]

        @pl.when(second)
        def _():
            u = _join_parts(u_refs)
            ub = u.astype(BF16)
            g_re = buf_re[...].astype(BF16)
            g_im = buf_im[...].astype(BF16)
            dd_ref[...] += jnp.sum(dy * u, axis=0, keepdims=True)
            for hf in range(2):
                cols = slice(hf * 1024, (hf + 1) * 1024)
                ycols = slice(hf * 256, (hf + 1) * 256)
                du_half = (_dot_nt(g_re[:, cols], bre_ref[hf]) + _dot_nt(g_im[:, cols], bim_ref[hf])
                           + d_ref[:, ycols] * dy[:, ycols])
                _split_parts(du_refs[2 * hf:2 * hf + 2], du_half)
                for q4 in range(_HALF_GROUPS // 4):
                    ch = slice(hf * 256 + q4 * 64, hf * 256 + (q4 + 1) * 64)
                    st = slice(hf * 1024 + q4 * 256, hf * 1024 + (q4 + 1) * 256)
                    blk = (hf, slice(q4 * 64, (q4 + 1) * 64), slice(q4 * 256, (q4 + 1) * 256))
                    dbre_ref[blk] += _dot_tn(ub[:, ch], g_re[:, st])
                    dbim_ref[blk] += _dot_tn(ub[:, ch], g_im[:, st])
                    dcre_ref[blk] += _dot_tn(dyb[:, ch], hre_ref[:, st])
                    dcim_ref[blk] -= _dot_tn(dyb[:, ch], him_ref[:, st])

        @pl.when(jnp.logical_and(i == n_tiles - 1, second))
        def _():
            dlre_ref[...] = jnp.sum(acc_re[...], axis=0, keepdims=True)
            dlim_ref[...] = jnp.sum(acc_im[...], axis=0, keepdims=True)

    tile = lambda w: pl.BlockSpec((tile_rows, w), lambda p, i: (n_tiles - 1 - i, 0))
    second_tile = lambda w: pl.BlockSpec((tile_rows, w), lambda p, i: (n_tiles - 1 - i * p, 0))
    cm = _full(_CM_SHAPE)
    row = _full((1, SSM_LANES))
    outs = _call(
        body, name="s5_scan_bwd", grid=(2, n_tiles),
        in_specs=[tile(LANES)] * _SCAN_PARTS + [second_tile(LANES)] * _SCAN_PARTS
        + [second_tile(SSM_LANES), second_tile(SSM_LANES), cm, cm, cm, cm, row, row, _full((1, 512))],
        out_specs=[second_tile(LANES)] * _SCAN_PARTS + [cm, cm, cm, cm, row, row, _full((1, 512))],
        out_shape=(_part_shapes(rows) + [_sds(_CM_SHAPE)] * 4 + [_sds((1, SSM_LANES))] * 2 + [_sds((1, 512))]),
        scratch_shapes=[pltpu.VMEM((slab, SSM_LANES), F32)] * 6 + [pltpu.VMEM((tile_rows, SSM_LANES), F32)] * 2,
        compiler_params=_params(48, ("arbitrary", "arbitrary")),
    )(*dy_parts, *u_parts, h_re, h_im, bt_re, bt_im, cm_re, cm_im, lbar_re, lbar_im, d_row)
    return (outs[:_SCAN_PARTS],) + tuple(outs[_SCAN_PARTS:])


def _glu_gate(gl, a, zs):
    return gl * jax.nn.sigmoid(a) * _silu(zs)


def _glu_fwd(y_parts, zs, w_glu, b_glu, n_seq, seq):
    rows = zs.shape[0]
    tm = 512
    slab, steps, _, _ = _scan_geometry(n_seq, seq)

    def body(*refs):
        y_refs, (zs_ref, w_ref, b_ref, o_ref) = refs[:_SCAN_PARTS], refs[_SCAN_PARTS:]
        y = _load_chunks(y_refs, pl.program_id(0) * (tm // steps), tm // steps, steps, slab)
        gl = jax.nn.gelu(y)
        a = _dot(gl.astype(BF16), w_ref[...]) + b_ref[...]
        o_ref[...] = _glu_gate(gl, a, zs_ref[...]).astype(BF16)

    return _call(
        body, name="glu_fwd", grid=(rows // tm,),
        in_specs=_whole_parts(rows) + [_rows(tm, 512), _full((512, 512)), _full((1, 512))],
        out_specs=_rows(tm, 512), out_shape=_sds((rows, 512), BF16),
        compiler_params=_params(32, ("arbitrary",)),
    )(*y_parts, zs, w_glu, b_glu)


def _glu_bwd(y_parts, zs, d_out, w_glu, b_glu, n_seq, seq):
    rows = zs.shape[0]
    tm = 512
    slab, steps, _, _ = _scan_geometry(n_seq, seq)

    def body(*refs):
        y_refs, (zs_ref, d_ref, w_ref, b_ref), refs = refs[:_SCAN_PARTS], refs[_SCAN_PARTS:_SCAN_PARTS + 4], refs[_SCAN_PARTS + 4:]
        dy_refs, (dzs_ref, dw_ref, db_ref) = refs[:_SCAN_PARTS], refs[_SCAN_PARTS:]
        first_chunk = pl.program_id(0) * (tm // steps)

        @pl.when(pl.program_id(0) == 0)
        def _():
            dw_ref[...] = jnp.zeros_like(dw_ref)
            db_ref[...] = jnp.zeros_like(db_ref)

        gl, gelu_vjp = jax.vjp(jax.nn.gelu, _load_chunks(y_refs, first_chunk, tm // steps, steps, slab))
        glb = gl.astype(BF16)
        a = _dot(glb, w_ref[...]) + b_ref[...]
        _, gate_vjp = jax.vjp(_glu_gate, gl, a, zs_ref[...])
        d_gl, d_a, d_zs = gate_vjp(d_ref[...])
        dab = d_a.astype(BF16)
        d_gl = d_gl + _dot_nt(dab, w_ref[...])
        _store_chunks(dy_refs, first_chunk, gelu_vjp(d_gl)[0], steps, slab)
        dzs_ref[...] = d_zs.astype(BF16)
        dw_ref[...] += _dot_tn(glb, dab)
        db_ref[...] += jnp.sum(d_a, axis=0, keepdims=True)

    *dy_parts, dzs, dw, db = _call(
        body, name="glu_bwd", grid=(rows // tm,),
        in_specs=_whole_parts(rows) + [_rows(tm, 512), _rows(tm, 512), _full((512, 512)), _full((1, 512))],
        out_specs=_whole_parts(rows) + [_rows(tm, 512), _full((512, 512)), _full((1, 512))],
        out_shape=_part_shapes(rows) + [_sds((rows, 512), BF16), _sds((512, 512)), _sds((1, 512))],
        compiler_params=_params(40, ("arbitrary",)),
    )(*y_parts, zs, d_out, w_glu, b_glu)
    return dy_parts, dzs, dw, db


_GROUP_ROWS = Q_PER_KV * BLOCK
_BLOCK_SHIFT = BLOCK.bit_length() - 1


def _attn_bias(j):
    query = _iota((BLOCK, _GROUP_ROWS), 1)
    dist_cur = (query & (BLOCK - 1)) - _iota((BLOCK, _GROUP_ROWS), 0)
    dist_prev = dist_cur + BLOCK
    head = query >> _BLOCK_SHIFT
    slope = jnp.zeros((BLOCK, _GROUP_ROWS), F32)
    for g in range(Q_PER_KV):
        slope = jnp.where(head == g, 2.0 ** (-(j * Q_PER_KV + g + 1)), slope)
    bias_cur = jnp.where(dist_cur >= 0, -slope * dist_cur.astype(F32), -jnp.inf)
    bias_prev = jnp.where(dist_prev < WINDOW, -slope * dist_prev.astype(F32), -jnp.inf)
    return bias_cur, bias_prev


_ATTN_BIAS_SCRATCH = pltpu.VMEM((KV_HEADS, 2, BLOCK, _GROUP_ROWS), F32)


def _fill_attn_bias(bias_ref):
    @pl.when(jnp.logical_and(pl.program_id(0) == 0, pl.program_id(1) == 0))
    def _():
        for j in range(KV_HEADS):
            bias_ref[j, 0], bias_ref[j, 1] = _attn_bias(j)


def _stack_heads(x, j):
    heads = range(j * Q_PER_KV, (j + 1) * Q_PER_KV)
    return jnp.concatenate([x[:, h * HEAD_DIM:(h + 1) * HEAD_DIM] for h in heads], axis=0)


def _head_rows(x, j):
    heads = range(j * Q_PER_KV, (j + 1) * Q_PER_KV)
    return jnp.concatenate([x[h:h + 1, :] for h in heads], axis=1)


def _sink_row(sk_ref, j):
    heads = range(j * Q_PER_KV, (j + 1) * Q_PER_KV)
    return jnp.concatenate([jnp.broadcast_to(sk_ref[0:1, h:h + 1], (1, BLOCK)) for h in heads], axis=1)


_ATTN_FWD_BLOCKS = 8


def _attn_fwd(q, k, v, za, sinks, n_seq, seq):
    nb = seq // BLOCK
    steps = nb // _ATTN_FWD_BLOCKS
    rows = q.shape[0]

    def body(q_ref, kc_ref, kp_ref, vc_ref, vp_ref, za_ref, sk_ref, o_ref, ao_ref, lse_ref, bias_ref):
        _fill_attn_bias(bias_ref)
        for t in range(_ATTN_FWD_BLOCKS):
            at = slice(t * BLOCK, (t + 1) * BLOCK)
            before = slice((t - 1) * BLOCK, t * BLOCK)
            q_all = q_ref[at, :]
            for j in range(KV_HEADS):
                js = slice(j * HEAD_DIM, (j + 1) * HEAD_DIM)
                bias_c, bias_p = bias_ref[j, 0], bias_ref[j, 1]
                q4 = _stack_heads(q_all, j)
                sc = _dot_nt(kc_ref[at, js], q4) + bias_c
                if t == 0:
                    sp = _dot_nt(kp_ref[:, js], q4) + jnp.where(pl.program_id(1) > 0, bias_p, -jnp.inf)
                    v_prev = vp_ref[:, js]
                else:
                    sp = _dot_nt(kc_ref[before, js], q4) + bias_p
                    v_prev = vc_ref[before, js]
                sink = _sink_row(sk_ref, j)
                m = jnp.maximum(jnp.max(jnp.maximum(sc, sp), axis=0, keepdims=True), sink)
                ec = jnp.exp(sc - m)
                ep = jnp.exp(sp - m)
                den = jnp.sum(ec + ep, axis=0, keepdims=True) + jnp.exp(sink - m)
                inv = 1.0 / den
                o4 = _dot_tn((ec * inv).astype(BF16), vc_ref[at, js]) + _dot_tn((ep * inv).astype(BF16), v_prev)
                lse4 = m + jnp.log(den)
                for g in range(Q_PER_KV):
                    h = j * Q_PER_KV + g
                    o_ref[at, h * HEAD_DIM:(h + 1) * HEAD_DIM] = o4[g * BLOCK:(g + 1) * BLOCK]
                    lse_ref[t * N_HEADS + h:t * N_HEADS + h + 1, :] = lse4[:, g * BLOCK:(g + 1) * BLOCK]
        ao_ref[...] = (o_ref[...] * _silu(za_ref[...])).astype(BF16)

    cur = lambda w: pl.BlockSpec((_ATTN_FWD_BLOCKS * BLOCK, w), lambda b, n: (b * steps + n, 0))
    prev = lambda w: pl.BlockSpec((BLOCK, w), lambda b, n: (b * nb + jnp.maximum(_ATTN_FWD_BLOCKS * n - 1, 0), 0))
    lse_rows = rows // BLOCK * N_HEADS
    return _call(
        body, name="attn_fwd", grid=(n_seq, steps),
        in_specs=[cur(512), cur(128), prev(128), cur(128), prev(128), cur(512), _full((1, N_HEADS))],
        out_specs=[cur(512), cur(512),
                   pl.BlockSpec((_ATTN_FWD_BLOCKS * N_HEADS, BLOCK), lambda b, n: (b * steps + n, 0))],
        out_shape=[_sds((rows, 512)), _sds((rows, 512), BF16), _sds((lse_rows, BLOCK))],
        scratch_shapes=[_ATTN_BIAS_SCRATCH], compiler_params=_params(32, ("arbitrary", "arbitrary")),
    )(q, k, k, v, v, za, sinks)


_ATTN_BWD_BLOCKS = 4


def _attn_bwd(q, k, v, za, o, lse, d_ao, sinks, n_seq, seq):
    nb = seq // BLOCK
    per_step = _ATTN_BWD_BLOCKS
    steps = nb // per_step
    rows = q.shape[0]

    def body(q_ref, kc_ref, kp_ref, vc_ref, vp_ref, za_ref, o_ref, lse_ref, d_ref, sk_ref,
             dq_ref, dk_ref, dv_ref, dza_ref, dsk_ref, bias_ref, dk_carry, dv_carry):
        step = pl.program_id(1)
        first_block = nb - per_step * (step + 1)
        _fill_attn_bias(bias_ref)

        @pl.when(jnp.logical_and(pl.program_id(0) == 0, step == 0))
        def _():
            dsk_ref[...] = jnp.zeros_like(dsk_ref)
            dk_carry[...] = jnp.zeros_like(dk_carry)
            dv_carry[...] = jnp.zeros_like(dv_carry)

        _, gate_vjp = jax.vjp(lambda o_, z_: o_ * _silu(z_), o_ref[...], za_ref[...])
        d_o, d_za = gate_vjp(d_ref[...])
        dza_ref[...] = d_za.astype(BF16)

        for j in range(KV_HEADS):
            js = slice(j * HEAD_DIM, (j + 1) * HEAD_DIM)
            bias_c, bias_p = bias_ref[j, 0], bias_ref[j, 1]
            sink = _sink_row(sk_ref, j)
            sink_loss = jnp.zeros((1, _GROUP_ROWS), F32)
            dk_from_next = jnp.where(step > 0, dk_carry[j], 0.0)
            dv_from_next = jnp.where(step > 0, dv_carry[j], 0.0)
            for t in reversed(range(per_step)):
                at = slice(t * BLOCK, (t + 1) * BLOCK)
                kc, vc = kc_ref[at, js], vc_ref[at, js]
                if t > 0:
                    before = slice((t - 1) * BLOCK, t * BLOCK)
                    kp, vp, bias_before = kc_ref[before, js], vc_ref[before, js], bias_p
                else:
                    kp, vp, bias_before = kp_ref[:, js], vp_ref[:, js], jnp.where(first_block > 0, bias_p, -jnp.inf)
                q4 = _stack_heads(q_ref[at, :], j)
                do4b = _stack_heads(d_o[at], j).astype(BF16)
                lse4 = _head_rows(lse_ref[t * N_HEADS:(t + 1) * N_HEADS, :], j)
                pc = jnp.exp(_dot_nt(kc, q4) + bias_c - lse4)
                pp = jnp.exp(_dot_nt(kp, q4) + bias_before - lse4)
                dpc = _dot_nt(vc, do4b)
                dpp = _dot_nt(vp, do4b)
                delta = jnp.sum(pc * dpc + pp * dpp, axis=0, keepdims=True)
                dsc = (pc * (dpc - delta)).astype(BF16)
                dsp = (pp * (dpp - delta)).astype(BF16)
                dq4 = ((_dot_tn(dsc, kc) + _dot_tn(dsp, kp)) * ATTN_SCALE).astype(BF16)
                sink_loss = sink_loss + jnp.exp(sink - lse4) * delta
                for g in range(Q_PER_KV):
                    h = j * Q_PER_KV + g
                    dq_ref[at, h * HEAD_DIM:(h + 1) * HEAD_DIM] = dq4[g * BLOCK:(g + 1) * BLOCK]
                dk_ref[at, js] = (_dot(dsc, q4) + dk_from_next).astype(BF16)
                dv_ref[at, js] = (_dot(pc.astype(BF16), do4b) + dv_from_next).astype(BF16)
                dk_from_next = _dot(dsp, q4)
                dv_from_next = _dot(pp.astype(BF16), do4b)
            dk_carry[j] = dk_from_next
            dv_carry[j] = dv_from_next
            for g in range(Q_PER_KV):
                h = j * Q_PER_KV + g
                dsk_ref[0:1, h:h + 1] -= jnp.sum(sink_loss[:, g * BLOCK:(g + 1) * BLOCK], axis=1, keepdims=True)

    cur = lambda w: pl.BlockSpec((per_step * BLOCK, w), lambda b, s: (b * steps + steps - 1 - s, 0))
    prev = lambda w: pl.BlockSpec((BLOCK, w), lambda b, s: (b * nb + jnp.maximum(nb - per_step * (s + 1) - 1, 0), 0))
    return _call(
        body, name="attn_bwd", grid=(n_seq, steps),
        in_specs=[cur(512), cur(128), prev(128), cur(128), prev(128), cur(512), cur(512),
                  pl.BlockSpec((per_step * N_HEADS, BLOCK), lambda b, s: (b * steps + steps - 1 - s, 0)), cur(512),
                  _full((1, N_HEADS))],
        out_specs=[cur(512), cur(128), cur(128), cur(512), _full((1, N_HEADS))],
        out_shape=[_sds((rows, 512), BF16), _sds((rows, 128), BF16), _sds((rows, 128), BF16),
                   _sds((rows, 512), BF16), _sds((1, N_HEADS))],
        scratch_shapes=[_ATTN_BIAS_SCRATCH, pltpu.VMEM((KV_HEADS, BLOCK, HEAD_DIM), F32),
                        pltpu.VMEM((KV_HEADS, BLOCK, HEAD_DIM), F32)],
        compiler_params=_params(32, ("arbitrary", "arbitrary")),
    )(q, k, k, v, v, za, o, lse, d_ao, sinks)


def _tail(ssm_out, attn_out, x2d, p2d, target, w_out, g2, w_gate, b_gate, w_proj):
    rows = x2d.shape[0]
    tm = 512

    def body(so_ref, ao_ref, x_ref, p_ref, t_ref, wo_ref, g2_ref, wg_ref, bg_ref, wp_ref,
             dh1_ref, dso_ref, dao_ref, dwo_ref, dwg_ref, dwp_ref, dbg_ref, dg2_ref, loss_ref):
        @pl.when(pl.program_id(0) == 0)
        def _():
            for ref in (dwo_ref, dwg_ref, dwp_ref, dbg_ref, dg2_ref, loss_ref):
                ref[...] = jnp.zeros_like(ref)

        cat = jnp.concatenate([so_ref[...], ao_ref[...]], axis=1)
        g2 = g2_ref[...]
        mixed = _dot(cat, wo_ref[...])
        r = lax.rsqrt(jnp.mean(mixed * mixed, axis=-1, keepdims=True) + EPS)
        mr = mixed * r
        h1 = x_ref[...] + mr * g2
        h1b = h1.astype(BF16)
        gate = jax.nn.sigmoid(_dot(h1b, wg_ref[...]) + bg_ref[...])
        pb = p_ref[...].astype(BF16)
        wp_blocks = [slice(j * D_PLE, (j + 1) * D_PLE) for j in range(N_CHIPS)]
        pp = jnp.concatenate([_dot(pb, wp_ref[blk, :]) for blk in wp_blocks], axis=1)
        err = h1 + gate * pp - t_ref[...]
        loss_ref[...] += 0.5 * jnp.sum(jnp.mean(err * err, axis=-1, keepdims=True), axis=0, keepdims=True)

        dh2 = err * (1.0 / D_MODEL)
        d_glin = dh2 * pp * gate * (1.0 - gate)
        d_glin_b = d_glin.astype(BF16)
        dwg_ref[...] += _dot_tn(h1b, d_glin_b)
        dbg_ref[...] += jnp.sum(d_glin, axis=0, keepdims=True)
        d_pp = (dh2 * gate).astype(BF16)
        for blk in wp_blocks:
            dwp_ref[blk, :] += _dot_tn(pb, d_pp[:, blk])
        dh1 = dh2 + _dot_nt(d_glin_b, wg_ref[...])
        dh1_ref[...] = dh1
        dg2_ref[...] += jnp.sum(dh1 * mr, axis=0, keepdims=True)
        a_ = dh1 * g2
        d_mixed = (r * a_ - mr * (r * jnp.mean(a_ * mr, axis=-1, keepdims=True))).astype(BF16)
        dwo_ref[...] += _dot_tn(cat, d_mixed)
        d_cat = _dot_nt(d_mixed, wo_ref[...])
        dso_ref[...] = d_cat[:, 0:512]
        dao_ref[...] = d_cat[:, 512:1024]

    return _call(
        body, name="tail_fwd_bwd", grid=(rows // tm,),
        in_specs=[_rows(tm, 512), _rows(tm, 512), _rows(tm, D_MODEL), _rows(tm, D_PLE), _rows(tm, D_MODEL),
                  _full((D_MODEL, D_MODEL)), _full((1, D_MODEL)), _full((D_MODEL, D_MODEL)), _full((1, D_MODEL)),
                  _full((N_CHIPS * D_PLE, D_PLE))],
        out_specs=[_rows(tm, D_MODEL), _rows(tm, 512), _rows(tm, 512), _full((D_MODEL, D_MODEL)),
                   _full((D_MODEL, D_MODEL)), _full((N_CHIPS * D_PLE, D_PLE)), _full((1, D_MODEL)), _full((1, D_MODEL)),
                   _full((1, 1))],
        out_shape=[_sds((rows, D_MODEL)), _sds((rows, 512)), _sds((rows, 512)), _sds((D_MODEL, D_MODEL)),
                   _sds((D_MODEL, D_MODEL)), _sds((N_CHIPS * D_PLE, D_PLE)), _sds((1, D_MODEL)), _sds((1, D_MODEL)),
                   _sds((1, 1))],
        compiler_params=_params(52, ("arbitrary",)),
    )(ssm_out, attn_out, x2d, p2d, target, w_out, g2, w_gate, b_gate, w_proj)


def _local_step(x, hn, p, target, pre_norm_g, w_in_t, s5_params, s5_operands, ssm_d, w_glu, b_glu, sinks, w_out,
                post_norm_g, w_proj, w_gate, b_gate, send_tail_grads=lambda ready: ready["w_out"],
                send_bc=lambda d_bc: (d_bc, d_bc)):
    n_seq, seq, _ = x.shape
    rows = n_seq * seq
    x2d = x.reshape(rows, D_MODEL)
    p2d = p.reshape(rows, D_PLE)
    t2d = target.reshape(rows, D_MODEL)

    l_re, l_im, bt_re, bt_im, cm_re, cm_im = s5_operands

    u_scan, zs, q, k, v, za = _in_proj(hn, w_in_t, n_seq, seq)
    y_scan, h_re, h_im = _s5_scan_fwd(u_scan, bt_re, bt_im, cm_re, cm_im, l_re, l_im, ssm_d, n_seq, seq)
    ssm_out = _glu_fwd(y_scan, zs, w_glu, b_glu, n_seq, seq)
    o, attn_out, lse = _attn_fwd(q, k, v, za, sinks, n_seq, seq)

    dh1, d_so, d_ao, d_w_out, d_w_gate, d_w_proj, d_b_gate, d_g2, loss = _tail(
        ssm_out, attn_out, x2d, p2d, t2d, w_out, post_norm_g, w_gate, b_gate, w_proj)

    dq, dk, dv, dza, d_sinks = _attn_bwd(q, k, v, za, o, lse, d_ao, sinks, n_seq, seq)
    dy_scan, dzs, d_w_glu, d_b_glu = _glu_bwd(y_scan, zs, d_so, w_glu, b_glu, n_seq, seq)
    du_scan, d_bt_re, d_bt_im, d_cm_re, d_cm_im, d_l_re, d_l_im, d_d = _s5_scan_bwd(
        dy_scan, u_scan, h_re, h_im, bt_re, bt_im, cm_re, cm_im, l_re, l_im, ssm_d, n_seq, seq)
    tail_grads_arrived = send_tail_grads(dict(w_out=d_w_out, pl_w_gate=d_w_gate, pl_w_proj=d_w_proj))
    d_lam_re, d_lam_im, d_log_step, d_bc = _s5_params_bwd(
        s5_params, (d_l_re, d_l_im, d_bt_re, d_bt_im, d_cm_re, d_cm_im), tail_grads_arrived)

    sent, arrived = send_bc(d_bc)
    d_proj, d_w_in_early, d_w_in_early_b = _in_proj_bwd_early(hn, du_scan, dzs, dq, dk, dv, dza, sent, n_seq, seq)
    grad_x, d_w_in_late, d_g1 = _in_proj_bwd(x2d, dh1, pre_norm_g, w_in_t, d_proj, arrived)
    grads = dict(
        pre_norm_g=d_g1, w_in_early=d_w_in_early, w_in_early_bf16=d_w_in_early_b, w_in_late=d_w_in_late,
        ssm_lam_re=d_lam_re, ssm_lam_im=d_lam_im, ssm_log_step=d_log_step, ssm_bc=d_bc, ssm_d=d_d, ssm_w_glu=d_w_glu,
        ssm_b_glu=d_b_glu, attn_sinks=d_sinks, w_out=d_w_out, post_norm_g=d_g2, pl_w_proj=d_w_proj,
        pl_w_gate=d_w_gate, pl_b_gate=d_b_gate)
    return grad_x.reshape(x.shape), loss, grads


_BIG = ("w_in", "ssm_w_glu", "w_out", "pl_w_proj", "pl_w_gate")
_BIG_SHARD = {"w_in": (D_IN // N_CHIPS, D_MODEL), "ssm_w_glu": (D_SSM // N_CHIPS, D_SSM),
              "w_out": (D_MODEL // N_CHIPS, D_MODEL), "pl_w_proj": (D_PLE, D_MODEL // N_CHIPS),
              "pl_w_gate": (D_MODEL // N_CHIPS, D_MODEL)}
_SMALL = {"pre_norm_g": (1, D_MODEL), "ssm_lam_re": (SSM_GROUPS, SSM_STATE), "ssm_lam_im": (SSM_GROUPS, SSM_STATE),
          "ssm_log_step": (1, SSM_GROUPS), "ssm_b_re": (D_SSM, SSM_STATE), "ssm_b_im": (D_SSM, SSM_STATE),
          "ssm_c_re": (D_SSM, SSM_STATE), "ssm_c_im": (D_SSM, SSM_STATE), "ssm_d": (1, D_SSM), "ssm_b_glu": (1, D_SSM),
          "attn_sinks": (1, N_HEADS), "post_norm_g": (1, D_MODEL), "pl_b_gate": (1, D_MODEL)}
_VEC_ROWS = ("pre_norm_g", "post_norm_g", "pl_b_gate", "ssm_d", "ssm_b_glu", "attn_sinks", "ssm_log_step", "loss")
_SMALL_GROUPS = (
    ("vec", (8, D_MODEL), tuple((name, r) for r, name in enumerate(_VEC_ROWS))),
    ("lam", (2 * SSM_GROUPS, SSM_STATE), (("ssm_lam_re", 0), ("ssm_lam_im", SSM_GROUPS))),
)
_SMALL_EARLY = ("ssm_b_re", "ssm_b_im", "ssm_c_re", "ssm_c_im")
_SMALL_ORDER = tuple(name for _, _, members in _SMALL_GROUPS for name, _ in members) + _SMALL_EARLY
_WEIGHT_ORDER = ("pre_norm_g", "w_in", "ssm_lam_re", "ssm_lam_im", "ssm_log_step", "ssm_b_re", "ssm_b_im", "ssm_c_re",
                 "ssm_c_im", "ssm_d", "ssm_w_glu", "ssm_b_glu", "attn_sinks", "w_out", "post_norm_g", "pl_w_proj",
                 "pl_w_gate", "pl_b_gate")


def _small_shape(name):
    return (1, 1) if name == "loss" else _SMALL[name]


def _to_kernel_form(name, a):
    a = a[0]
    if name == "w_in":
        return a.T
    if name in ("ssm_b_re", "ssm_b_im"):
        a = a.transpose(0, 2, 1)
    return a.reshape(_SMALL[name]) if name in _SMALL else a


def _from_kernel_form(name, a, shape):
    if name == "w_in":
        a = a.T
    if name in ("ssm_b_re", "ssm_b_im"):
        a = a.reshape(SSM_GROUPS, SSM_GROUP_CH, SSM_STATE).transpose(0, 2, 1)
    return a.reshape(shape)


def _mesh_place():
    x, y, c = lax.axis_index("x"), lax.axis_index("y"), lax.axis_index("c")
    other_chips = ((1 - x, y), (x, 1 - y), (1 - x, 1 - y))
    return x, y, c, other_chips


def _gather_copies(s_refs, g_refs, send_sems, recv_sems, local_sems):
    x, y, c, other_chips = _mesh_place()
    started = []
    for i, (s_ref, g_ref) in enumerate(zip(s_refs, g_refs)):
        rows = s_ref.shape[0]
        half = rows // 2

        def block(chip, g_ref=g_ref, rows=rows, half=half):
            return g_ref.at[pl.ds((2 * chip[0] + chip[1]) * rows + c * half, half), :]

        def copy(k, chip, to, src=None, i=i, block=block):
            return pltpu.make_async_remote_copy(
                src_ref=block(chip) if src is None else src, dst_ref=block(chip), send_sem=send_sems.at[6 * i + k],
                recv_sem=recv_sems.at[6 * i + k], device_id=to, device_id_type=MESH)

        own = pltpu.make_async_copy(s_ref, g_ref.at[pl.ds((2 * x + y) * rows, rows), :], local_sems.at[i])
        own.start()
        first = [copy(k, (x, y), (*chip, c), src=s_ref.at[pl.ds(c * half, half), :])
                 for k, chip in enumerate(other_chips)]
        for cp in first:
            cp.start()
        passed = [copy(3 + k, chip, (x, y, 1 - c)) for k, chip in enumerate(other_chips)]
        started.append((own, first, passed))
    for own, first, passed in started:
        for k in range(3):
            first[k].wait_recv()
            passed[k].start()
    for own, first, passed in started:
        for k in range(3):
            passed[k].wait_recv()
        for cp in first + passed:
            cp.wait_send()
        own.wait()


def _gather_semaphores(n_t):
    return [pltpu.SemaphoreType.DMA((6 * n_t,)), pltpu.SemaphoreType.DMA((6 * n_t,)), pltpu.SemaphoreType.DMA((n_t,))]


def _gather_weights_beside(shards, name, collective_id):
    n_t = len(shards)
    hbm = pltpu.MemorySpace.HBM
    s_refs = [jax.new_ref(s, memory_space=hbm) for s in shards]
    g_refs = [jax.empty_ref(jax.ShapeDtypeStruct((N_CHIPS * s.shape[0], s.shape[1]), s.dtype), memory_space=hbm)
              for s in shards]

    def launch(send_sems, recv_sems, local_sems):
        x, y, c, other_chips = _mesh_place()
        peers = [(*chip, c) for chip in other_chips] + [(x, y, 1 - c)]
        barrier = pltpu.get_barrier_semaphore()
        for peer in peers:
            pl.semaphore_signal(barrier, inc=1, device_id=peer, device_id_type=MESH)
        pl.semaphore_wait(barrier, len(peers))
        _gather_copies(s_refs, g_refs, send_sems, recv_sems, local_sems)

    pl.kernel(launch, mesh=plsc.ScalarSubcoreMesh(axis_name="sequencer", num_cores=1), name=name,
              scratch_types=_gather_semaphores(n_t), compiler_params=pltpu.CompilerParams(collective_id=collective_id))()
    return [g[...] for g in g_refs]


_RELATIONS = tuple(((r >> 2) & 1, (r >> 1) & 1, r & 1) for r in range(1, 8))


def _related(place, relation):
    return tuple(1 - a if flip else a for a, flip in zip(place, relation))


def _scatter_beside(mats, name, collective_id):
    hbm = pltpu.MemorySpace.HBM
    src_refs = [jax.new_ref(a, memory_space=hbm) for a in mats]
    land_refs = [jax.empty_ref(jax.ShapeDtypeStruct((7, a.shape[0] // 8, a.shape[1]), a.dtype), memory_space=hbm)
                 for a in mats]

    def launch(send_sems, recv_sems):
        me = (lax.axis_index("x"), lax.axis_index("y"), lax.axis_index("c"))
        peers = [_related(me, rel) for rel in _RELATIONS]
        barrier = pltpu.get_barrier_semaphore()
        for peer in peers:
            pl.semaphore_signal(barrier, inc=1, device_id=peer, device_id_type=MESH)
        pl.semaphore_wait(barrier, len(peers))
        copies = []
        for i, (src, land) in enumerate(zip(src_refs, land_refs)):
            hr = land.shape[1]
            for k, (tx, ty, tc) in enumerate(peers):
                rows = pl.ds((2 * tx + ty) * 2 * hr + tc * hr, hr)
                copies.append(pltpu.make_async_remote_copy(
                    src_ref=src.at[rows, :], dst_ref=land.at[k], send_sem=send_sems.at[7 * i + k],
                    recv_sem=recv_sems.at[7 * i + k], device_id=(tx, ty, tc), device_id_type=MESH))
                copies[-1].start()
        for cp in copies:
            cp.wait()

    n_sems = 7 * len(mats)
    pl.kernel(launch, mesh=plsc.ScalarSubcoreMesh(axis_name="sequencer", num_cores=1), name=name,
              scratch_types=[pltpu.SemaphoreType.DMA((n_sems,)), pltpu.SemaphoreType.DMA((n_sems,))],
              compiler_params=pltpu.CompilerParams(collective_id=collective_id))()
    return [ref[...] for ref in land_refs]


def _broadcast_beside(arrays):
    hbm = pltpu.MemorySpace.HBM
    src_refs = [jax.new_ref(a, memory_space=hbm) for a in arrays]
    land_refs = [jax.empty_ref(jax.ShapeDtypeStruct((len(_RELATIONS),) + a.shape, a.dtype), memory_space=hbm)
                 for a in arrays]

    def launch(send_sems, recv_sems):
        me = (lax.axis_index("x"), lax.axis_index("y"), lax.axis_index("c"))
        peers = [_related(me, rel) for rel in _RELATIONS]
        barrier = pltpu.get_barrier_semaphore()
        for peer in peers:
            pl.semaphore_signal(barrier, inc=1, device_id=peer, device_id_type=MESH)
        pl.semaphore_wait(barrier, len(peers))
        copies = []
        for i, (src, land) in enumerate(zip(src_refs, land_refs)):
            for k, peer in enumerate(peers):
                copies.append(pltpu.make_async_remote_copy(
                    src_ref=src, dst_ref=land.at[k], send_sem=send_sems.at[7 * i + k],
                    recv_sem=recv_sems.at[7 * i + k], device_id=peer, device_id_type=MESH))
                copies[-1].start()
        for cp in copies:
            cp.wait()

    n_sems = 7 * len(arrays)
    pl.kernel(launch, mesh=plsc.ScalarSubcoreMesh(axis_name="sequencer", num_cores=1), name="broadcast_beside",
              scratch_types=[pltpu.SemaphoreType.DMA((n_sems,)), pltpu.SemaphoreType.DMA((n_sems,))],
              compiler_params=pltpu.CompilerParams(collective_id=3))()
    return [ref[...] for ref in land_refs]


def _exchange_grads(big, outputs, small, landed, own_bc, landed_bc):
    n_t = len(big)
    n_g = len(_SMALL_GROUPS)
    names = _SMALL_ORDER
    halves = [(b.shape[0] // N_CHIPS // 2, b.shape[1]) for b in big]
    early = sorted(landed)
    late = [i for i in range(n_t) if i not in landed]
    n_sems = 4 * n_g + 7 * len(late) + n_t
    small_sem0, block_sem0 = n_t, n_t + len(names)
    early_sem0 = block_sem0 + N_CHIPS * len(late)
    landed_sem0 = early_sem0 + 2 * len(early)
    sent = [n for n in names if n in small]

    def body(*refs):
        pos = 0

        def take(n):
            nonlocal pos
            pos += n
            return refs[pos - n:pos]

        big_refs, small_refs = take(n_t), dict(zip(sent, take(len(sent))))
        land_refs = dict(zip(early, take(len(early))))
        own_bc_ref, landed_bc_ref = take(2)
        out_refs, small_out_refs = take(len(outputs)), dict(zip(names, take(len(names))))
        per_late = lambda: dict(zip(late, take(len(late))))
        ga, gb, pme, send_b, recv_b = per_late(), per_late(), take(n_t), per_late(), per_late()
        own_e, land_e = dict(zip(early, take(len(early)))), dict(zip(early, take(len(early))))
        own_s, land_s = take(2)
        s_own, s_sib, s_chips, s_pair = take(n_g), take(n_g), take(n_g), take(n_g)
        stage = dict(zip(names, take(len(names))))
        send_sems, recv_sems, local_sems = take(3)
        x, y, c, other_chips = _mesh_place()
        me = 2 * x + y
        sibling = (x, y, 1 - c)
        sem_at = iter(range(n_sems))

        def remote(src, dst, to):
            k = next(sem_at)
            return pltpu.make_async_remote_copy(src_ref=src, dst_ref=dst, send_sem=send_sems.at[k],
                                                recv_sem=recv_sems.at[k], device_id=to, device_id_type=MESH)

        loads = [pltpu.make_async_copy(small_refs[name], stage[name], local_sems.at[small_sem0 + names.index(name)])
                 for name in sent]
        landed_loads = [pltpu.make_async_copy(own_bc_ref, own_s, local_sems.at[landed_sem0]),
                        pltpu.make_async_copy(landed_bc_ref, land_s, local_sems.at[landed_sem0 + 1])]
        for cp in loads:
            cp.start()
        for cp in loads:
            cp.wait()
        small_swaps = []
        for gi, (_, _, members) in enumerate(_SMALL_GROUPS):
            s_own[gi][...] = jnp.zeros_like(s_own[gi])
            for name, r0 in members:
                r, n = _small_shape(name)
                s_own[gi][r0:r0 + r, 0:n] = stage[name][...]
            small_swaps.append(remote(s_own[gi], s_sib[gi], sibling))
            small_swaps[gi].start()
        order = sorted(late, key=lambda i: halves[i][0] * halves[i][1])
        own_loads, big_swaps = {}, {}
        for i in order:
            hr = halves[i][0]
            own_loads[i], big_swaps[i] = [], []
            for j in range(N_CHIPS):
                mine = big_refs[i].at[pl.ds(j * 2 * hr + c * hr, hr), :]
                theirs = big_refs[i].at[pl.ds(j * 2 * hr + (1 - c) * hr, hr), :]
                sem = local_sems.at[block_sem0 + N_CHIPS * late.index(i) + j]
                own_loads[i].append(pltpu.make_async_copy(mine, ga[i].at[j], sem))
                own_loads[i][j].start()
                big_swaps[i].append(remote(theirs, gb[i].at[j], sibling))
                big_swaps[i][j].start()
        small_sends = []
        for gi in range(n_g):
            small_swaps[gi].wait_recv()
            s_pair[gi][...] = s_own[gi][...] + s_sib[gi][...]
            small_sends.append([remote(s_pair[gi], s_chips[gi].at[k], (*chip, c)) for k, chip in enumerate(other_chips)])
            for cp in small_sends[gi]:
                cp.start()

        def pair_sum(i, j):
            return ga[i][j] + gb[i][j]

        big_sends = {}
        for i in order:
            for j in range(N_CHIPS):
                own_loads[i][j].wait()
                big_swaps[i][j].wait_recv()
            big_sends[i] = []
            for k, chip in enumerate(other_chips):
                send_b[i][k] = pair_sum(i, 2 * chip[0] + chip[1]).astype(BF16)
                big_sends[i].append(remote(send_b[i].at[k], recv_b[i].at[k], (*chip, c)))
                big_sends[i][k].start()
        early_loads = {}
        for e, i in enumerate(early):
            hr = halves[i][0]
            mine = big_refs[i].at[pl.ds(me * 2 * hr + c * hr, hr), :]
            early_loads[i] = [pltpu.make_async_copy(mine, own_e[i], local_sems.at[early_sem0 + 2 * e]),
                              pltpu.make_async_copy(land_refs[i], land_e[i], local_sems.at[early_sem0 + 2 * e + 1])]
            for cp in early_loads[i]:
                cp.start()
        for cp in landed_loads:
            cp.start()
        last_swaps, keeps = {}, {}
        for i in early + order:
            hr = halves[i][0]
            if i in landed:
                for cp in early_loads[i]:
                    cp.wait()
                total = own_e[i][...]
                for k in range(len(_RELATIONS)):
                    total = total + land_e[i][k].astype(F32)
                pme[i][...] = total
            else:
                for k in range(3):
                    big_sends[i][k].wait_recv()
                pme[i][...] = ((pair_sum(i, me) + recv_b[i][0].astype(F32)) + recv_b[i][1].astype(F32)) + recv_b[i][2].astype(F32)
            o, = [o for o, group in enumerate(outputs) if i in group]
            first_col = sum(halves[j][1] for j in outputs[o][:outputs[o].index(i)])
            mine = out_refs[o].at[pl.ds(c * hr, hr), pl.ds(first_col, halves[i][1])]
            keeps[i] = pltpu.make_async_copy(pme[i], mine, local_sems.at[i])
            keeps[i].start()
            last_swaps[i] = remote(pme[i], mine, sibling)
            last_swaps[i].start()

        for gi, (_, _, members) in enumerate(_SMALL_GROUPS):
            for k in range(3):
                small_sends[gi][k].wait_recv()
            total = None
            for j in range(N_CHIPS):
                rel = jnp.bitwise_xor(j, me)
                term = jnp.where(rel == 0, s_pair[gi][...], jnp.where(
                    rel == 2, s_chips[gi][0], jnp.where(rel == 1, s_chips[gi][1], s_chips[gi][2])))
                total = term if total is None else total + term
            s_sib[gi][...] = total
            for name, r0 in members:
                r, n = _small_shape(name)
                stage[name][...] = s_sib[gi][r0:r0 + r, 0:n]
        my_index = 4 * x + 2 * y + c
        for cp in landed_loads:
            cp.wait()
        total = None
        for d in range(2 * N_CHIPS):
            rel = jnp.bitwise_xor(d, my_index)
            term = own_s[...]
            for k in range(len(_RELATIONS)):
                term = jnp.where(rel == k + 1, land_s[k], term)
            total = term.astype(F32) if total is None else total + term.astype(F32)
        for a, name in enumerate(_SMALL_EARLY):
            stage[name][...] = total[:, a * SSM_STATE:(a + 1) * SSM_STATE]
        stores = [pltpu.make_async_copy(stage[name], small_out_refs[name], local_sems.at[small_sem0 + a])
                  for a, name in enumerate(names)]
        for cp in stores:
            cp.start()

        for i in range(n_t):
            last_swaps[i].wait_recv()
            keeps[i].wait()
        for cp in stores:
            cp.wait()
        groups = list(big_swaps.values()) + small_sends + list(big_sends.values())
        for cp in small_swaps + [cp for group in groups for cp in group] + list(last_swaps.values()):
            cp.wait_send()

    any_spec = pl.BlockSpec(memory_space=pl.ANY)
    small_shapes = [_sds(_small_shape(n)) for n in names]
    group_shapes = [shape for _, shape, _ in _SMALL_GROUPS]
    vmem = lambda which, dtype, lead=(): [pltpu.VMEM(lead + halves[i], dtype) for i in which]
    outs = _call(
        body, name="exchange_grads",
        in_specs=[any_spec] * (n_t + len(sent) + len(early) + 2),
        out_specs=[any_spec] * (len(outputs) + len(names)),
        out_shape=[_sds((big[group[0]].shape[0] // N_CHIPS, sum(big[i].shape[1] for i in group))) for group in outputs]
        + small_shapes,
        scratch_shapes=(vmem(late, F32, (N_CHIPS,)) + vmem(late, F32, (N_CHIPS,)) + vmem(range(n_t), F32)
                        + vmem(late, BF16, (3,)) + vmem(late, BF16, (3,))
                        + vmem(early, F32)
                        + [pltpu.VMEM((len(_RELATIONS),) + halves[i], landed[i].dtype) for i in early]
                        + [pltpu.VMEM(own_bc.shape, own_bc.dtype), pltpu.VMEM(landed_bc.shape, landed_bc.dtype)]
                        + [pltpu.VMEM(s, F32) for s in group_shapes] * 2 + [pltpu.VMEM((3,) + s, F32) for s in group_shapes]
                        + [pltpu.VMEM(s, F32) for s in group_shapes]
                        + [pltpu.VMEM(_small_shape(n), F32) for n in names]
                        + [pltpu.SemaphoreType.DMA((n_sems,)), pltpu.SemaphoreType.DMA((n_sems,)),
                           pltpu.SemaphoreType.DMA((landed_sem0 + 2,))]),
        compiler_params=_params(48),
    )(*big, *[small[n] for n in sent], *[landed[i] for i in early], own_bc, landed_bc)
    return list(outs[:len(outputs)]), dict(zip(names, outs[len(outputs):]))


def _adamw_update(w, g, m, v):
    m = ADAM_B1 * m + (1.0 - ADAM_B1) * g
    v = ADAM_B2 * v + (1.0 - ADAM_B2) * (g * g)
    m_hat = m / (1.0 - ADAM_B1 ** ADAM_STEP)
    v_hat = v / (1.0 - ADAM_B2 ** ADAM_STEP)
    return -ADAM_LR * (m_hat / (jnp.sqrt(v_hat) + ADAM_EPS) + ADAM_WD * w), m, v


def _adamw(w, g, m, v, grid, name):
    n_t = len(w)

    def body(*refs):
        ins, outs = refs[:4 * n_t], refs[4 * n_t:]
        for i in range(n_t):
            w_, g_, m_, v_ = [ins[a * n_t + i][...] for a in range(4)]
            vals = (g_,) + _adamw_update(w_, g_, m_, v_)
            for a in range(4):
                outs[a * n_t + i][...] = vals[a]

    specs = [pl.BlockSpec((a.shape[0] // grid, a.shape[1]), lambda i: (i, 0)) for a in w]
    shapes = [_sds(a.shape) for a in w]
    outs = _call(
        body, name=name, grid=(grid,), in_specs=specs * 4, out_specs=specs * 4, out_shape=shapes * 4,
        compiler_params=_params(40, ("arbitrary",)),
    )(*w, *g, *m, *v)
    return [outs[a * n_t:(a + 1) * n_t] for a in range(4)]


def kernel(x, p, pre_norm_g, w_in, ssm_lam_re, ssm_lam_im, ssm_log_step, ssm_b_re, ssm_b_im, ssm_c_re, ssm_c_im, ssm_d, ssm_w_glu, ssm_b_glu, attn_sinks, w_out, post_norm_g, pl_w_proj, pl_w_gate, pl_b_gate, loss_target, m_pre_norm_g, m_w_in, m_ssm_lam_re, m_ssm_lam_im, m_ssm_log_step, m_ssm_b_re, m_ssm_b_im, m_ssm_c_re, m_ssm_c_im, m_ssm_d, m_ssm_w_glu, m_ssm_b_glu, m_attn_sinks, m_w_out, m_post_norm_g, m_pl_w_proj, m_pl_w_gate, m_pl_b_gate, v_pre_norm_g, v_w_in, v_ssm_lam_re, v_ssm_lam_im, v_ssm_log_step, v_ssm_b_re, v_ssm_b_im, v_ssm_c_re, v_ssm_c_im, v_ssm_d, v_ssm_w_glu, v_ssm_b_glu, v_attn_sinks, v_w_out, v_post_norm_g, v_pl_w_proj, v_pl_w_gate, v_pl_b_gate):
    weights = dict(pre_norm_g=pre_norm_g, w_in=w_in, ssm_lam_re=ssm_lam_re, ssm_lam_im=ssm_lam_im,
                   ssm_log_step=ssm_log_step, ssm_b_re=ssm_b_re, ssm_b_im=ssm_b_im, ssm_c_re=ssm_c_re,
                   ssm_c_im=ssm_c_im, ssm_d=ssm_d, ssm_w_glu=ssm_w_glu, ssm_b_glu=ssm_b_glu, attn_sinks=attn_sinks,
                   w_out=w_out, post_norm_g=post_norm_g, pl_w_proj=pl_w_proj, pl_w_gate=pl_w_gate, pl_b_gate=pl_b_gate)
    m_in = dict(pre_norm_g=m_pre_norm_g, w_in=m_w_in, ssm_lam_re=m_ssm_lam_re, ssm_lam_im=m_ssm_lam_im,
                ssm_log_step=m_ssm_log_step, ssm_b_re=m_ssm_b_re, ssm_b_im=m_ssm_b_im, ssm_c_re=m_ssm_c_re,
                ssm_c_im=m_ssm_c_im, ssm_d=m_ssm_d, ssm_w_glu=m_ssm_w_glu, ssm_b_glu=m_ssm_b_glu,
                attn_sinks=m_attn_sinks, w_out=m_w_out, post_norm_g=m_post_norm_g, pl_w_proj=m_pl_w_proj,
                pl_w_gate=m_pl_w_gate, pl_b_gate=m_pl_b_gate)
    v_in = dict(pre_norm_g=v_pre_norm_g, w_in=v_w_in, ssm_lam_re=v_ssm_lam_re, ssm_lam_im=v_ssm_lam_im,
                ssm_log_step=v_ssm_log_step, ssm_b_re=v_ssm_b_re, ssm_b_im=v_ssm_b_im, ssm_c_re=v_ssm_c_re,
                ssm_c_im=v_ssm_c_im, ssm_d=v_ssm_d, ssm_w_glu=v_ssm_w_glu, ssm_b_glu=v_ssm_b_glu,
                attn_sinks=v_attn_sinks, w_out=v_w_out, post_norm_g=v_post_norm_g, pl_w_proj=v_pl_w_proj,
                pl_w_gate=v_pl_w_gate, pl_b_gate=v_pl_b_gate)

    def two_d(tree):
        return {k: _to_kernel_form(k, a) for k, a in tree.items()}

    w2, m2, v2 = two_d(weights), two_d(m_in), two_d(v_in)

    (w_in_full,) = _gather_weights_beside([w2["w_in"].astype(BF16)], "gather_w_in_beside", 4)
    s5_params = tuple(w2[n] for n in ("ssm_lam_re", "ssm_lam_im", "ssm_log_step", "ssm_b_re", "ssm_b_im", "ssm_c_re",
                                      "ssm_c_im"))
    s5_operands = _s5_params_fwd(*s5_params)
    hn = _pre_norm(x.reshape(-1, D_MODEL), w2["pre_norm_g"])
    behind = s5_operands[0][0, 0] * 0.0 + hn[0, 0].astype(F32) * 0.0
    rest = _gather_weights_beside([(w2[n] + behind).astype(BF16) for n in _BIG[1:]], "gather_weights_beside", 1)
    full = dict(zip(_BIG, [w_in_full] + rest))
    mats = ("w_in_early", "w_in_late") + _BIG[1:]
    landed, bc = {}, {}

    def send_tail_grads(ready):
        sent_early = ("w_out", "pl_w_gate", "pl_w_proj")
        landed.update(zip([mats.index(n) for n in sent_early],
                          _scatter_beside([ready[n] for n in sent_early], "scatter_beside", 2)))
        return landed[mats.index(sent_early[-1])]

    def send_bc(d_bc):
        bc["own"] = d_bc
        bc["landed"] = _broadcast_beside([d_bc])[0]
        return d_bc, bc["landed"]

    grad_x, loss, grads = _local_step(
        x, hn, p, loss_target, w2["pre_norm_g"], full["w_in"], s5_params, s5_operands, w2["ssm_d"], full["ssm_w_glu"], w2["ssm_b_glu"],
        w2["attn_sinks"], full["w_out"], w2["post_norm_g"], full["pl_w_proj"], full["pl_w_gate"], w2["pl_b_gate"], send_tail_grads, send_bc)

    landed[0] = _scatter_beside([grads["w_in_early_bf16"]], "scatter_w_in_beside", 5)[0]
    sent_here = {**{n: grads[n] for n in _SMALL if n not in _SMALL_EARLY}, "loss": loss}
    halves_of_w_in = ((0, 1),) + tuple((i,) for i in range(2, len(mats)))
    g_big, g_small = _exchange_grads([grads[n] for n in mats], halves_of_w_in, sent_here, landed, bc["own"], bc["landed"])
    g_big = dict(zip(_BIG, g_big))
    total_loss = g_small.pop("loss")

    big_out = _adamw([w2[n] for n in _BIG], [g_big[n] for n in _BIG], [m2[n] for n in _BIG], [v2[n] for n in _BIG],
                     8, "adamw_matrices")
    small_names = tuple(_SMALL)
    small_out = _adamw([w2[n] for n in small_names], [g_small[n] for n in small_names], [m2[n] for n in small_names],
                       [v2[n] for n in small_names], 1, "adamw_small")

    results = [{**dict(zip(_BIG, big_part)), **dict(zip(small_names, small_part))}
               for big_part, small_part in zip(big_out, small_out)]
    flat = [_from_kernel_form(name, r[name], weights[name].shape) for r in results for name in _WEIGHT_ORDER]
    return (total_loss.reshape(()), grad_x, *flat)
```

```python
import math

import jax
import jax.numpy as jnp
from jax import lax
from jax.experimental import pallas as pl
from jax.experimental.pallas import tpu as pltpu
from jax.experimental.pallas import tpu_sc as plsc

F32 = jnp.float32
BF16 = jnp.bfloat16

D_MODEL = 1024
D_SSM = 512
D_ATTN = 512
SSM_GROUPS = 32
SSM_GROUP_CH = 16
SSM_STATE = 64
SSM_LANES = SSM_GROUPS * SSM_STATE
HEAD_DIM = 64
N_HEADS = 8
KV_HEADS = 2
Q_PER_KV = 4
WINDOW = 128
BLOCK = 128
D_PLE = 256
D_IN = 2304
EPS = 1e-6
ATTN_SCALE = 1.0 / math.sqrt(HEAD_DIM)

ADAM_LR = 0.001
ADAM_B1 = 0.9
ADAM_B2 = 0.999
ADAM_EPS = 1e-08
ADAM_WD = 0.01
ADAM_STEP = 10

N_CHIPS = 4
LANES = 128
SCAN_CHUNKS = 8
SCAN_TILE_STEPS = 32
SCAN_LANE_CHUNK = 512
MIB = 2 ** 20
MESH = pl.DeviceIdType.MESH


def _dot(a, b):
    return jnp.dot(a, b, preferred_element_type=F32)


def _dot_nt(a, b):
    return lax.dot_general(a, b, (((1,), (1,)), ((), ())), preferred_element_type=F32)


def _dot_tn(a, b):
    return lax.dot_general(a, b, (((0,), (0,)), ((), ())), preferred_element_type=F32)


def _params(vmem_mib, semantics=None):
    kw = dict(vmem_limit_bytes=vmem_mib * MIB)
    if semantics is not None:
        kw["dimension_semantics"] = semantics
    return pltpu.CompilerParams(**kw)


def _full(shape):
    nd = len(shape)
    return pl.BlockSpec(shape, lambda *_: (0,) * nd, pipeline_mode=pl.Buffered(1))


def _rows(tm, width):
    return pl.BlockSpec((tm, width), lambda i: (i, 0))


def _sds(shape, dtype=F32):
    return pltpu.HBM(shape, dtype)


def _call(body, **kw):
    fn = pl.pallas_call(body, **kw)
    return lambda *args: fn(*[pltpu.with_memory_space_constraint(a, pltpu.HBM) for a in args])


def _silu(z):
    return z * jax.nn.sigmoid(z)


def _pre_norm(x2d, g1):
    rows = x2d.shape[0]
    tm = 512

    def body(x_ref, g_ref, hn_ref):
        x = x_ref[...]
        r = lax.rsqrt(jnp.mean(x * x, axis=-1, keepdims=True) + EPS)
        hn_ref[...] = (x * r * g_ref[...]).astype(BF16)

    return _call(
        body, name="pre_norm", grid=(rows // tm,), in_specs=[_rows(tm, D_MODEL), _full((1, D_MODEL))],
        out_specs=_rows(tm, D_MODEL), out_shape=_sds((rows, D_MODEL), BF16), compiler_params=_params(32, ("arbitrary",)),
    )(x2d, g1)


def _in_proj(hn, w_in_t, n_seq, seq):
    rows = hn.shape[0]
    tm = 1024
    slab, steps, _, _ = _scan_geometry(n_seq, seq)

    def body(hn_ref, w_ref, *out_refs):
        u_parts, (zs_ref, q_ref, k_ref, v_ref, za_ref) = out_refs[:_SCAN_PARTS], out_refs[_SCAN_PARTS:]
        whole = _dot_nt(hn_ref[...], w_ref[...])

        def proj(a, b):
            return whole[:, a:b]

        _store_chunks(u_parts, pl.program_id(0) * (tm // steps), proj(0, 512), steps, slab)
        zs_ref[...] = proj(512, 1024)
        q_ref[...] = (proj(1024, 1536) * ATTN_SCALE).astype(BF16)
        k_ref[...] = proj(1536, 1664).astype(BF16)
        v_ref[...] = proj(1664, 1792).astype(BF16)
        za_ref[...] = proj(1792, 2304)

    *u_parts, zs, q, k, v, za = _call(
        body, name="in_proj", grid=(rows // tm,),
        in_specs=[_rows(tm, D_MODEL), _full((D_IN, D_MODEL))],
        out_specs=_whole_parts(rows) + [_rows(tm, 512), _rows(tm, 512), _rows(tm, 128), _rows(tm, 128), _rows(tm, 512)],
        out_shape=_part_shapes(rows) + [_sds((rows, 512)), _sds((rows, 512), BF16), _sds((rows, 128), BF16),
                                        _sds((rows, 128), BF16), _sds((rows, 512))],
        compiler_params=_params(48, ("arbitrary",)),
    )(hn, w_in_t)
    return u_parts, zs, q, k, v, za


_EARLY_COLS = D_MODEL // 2


def _in_proj_bwd_early(hn, du_parts, dzs, dq, dk, dv, dza, runs_after, n_seq, seq):
    rows = hn.shape[0]
    tm = 1024
    slab, steps, _, _ = _scan_geometry(n_seq, seq)

    def body(hn_ref, *refs):
        du_parts, (dzs_ref, dq_ref, dk_ref, dv_ref, dza_ref, _, dproj_ref, dw_ref, dwb_ref) = refs[:_SCAN_PARTS], refs[_SCAN_PARTS:]
        i = pl.program_id(0)

        @pl.when(i == 0)
        def _():
            dw_ref[...] = jnp.zeros_like(dw_ref)

        du = _load_chunks(du_parts, i * (tm // steps), tm // steps, steps, slab)
        d_proj = jnp.concatenate([du.astype(BF16), dzs_ref[...], dq_ref[...], dk_ref[...], dv_ref[...], dza_ref[...]],
                                 axis=1)
        dproj_ref[...] = d_proj
        dw_ref[...] += _dot_tn(d_proj, hn_ref[...])

        @pl.when(i == rows // tm - 1)
        def _():
            dwb_ref[...] = dw_ref[...].astype(BF16)

    return _call(
        body, name="in_proj_bwd_early", grid=(rows // tm,),
        in_specs=[_rows(tm, _EARLY_COLS)] + _whole_parts(rows)
        + [_rows(tm, 512), _rows(tm, 512), _rows(tm, 128), _rows(tm, 128), _rows(tm, 512),
           pl.BlockSpec(memory_space=pl.ANY)],
        out_specs=[_rows(tm, D_IN), _full((D_IN, _EARLY_COLS)), _full((D_IN, _EARLY_COLS))],
        out_shape=[_sds((rows, D_IN), BF16), _sds((D_IN, _EARLY_COLS)), _sds((D_IN, _EARLY_COLS), BF16)],
        compiler_params=_params(48, ("arbitrary",)),
    )(hn, *du_parts, dzs, dq, dk, dv, dza, runs_after)


def _in_proj_bwd(x2d, dh1, g1, w_in_t, d_proj, runs_after):
    rows = x2d.shape[0]
    tm = 512

    def body(x_ref, dh1_ref, g_ref, w_ref, dproj_ref, _, gx_ref, dw_ref, dg_ref):
        @pl.when(pl.program_id(0) == 0)
        def _():
            dw_ref[...] = jnp.zeros_like(dw_ref)
            dg_ref[...] = jnp.zeros_like(dg_ref)

        x = x_ref[...]
        g = g_ref[...]
        r = lax.rsqrt(jnp.mean(x * x, axis=-1, keepdims=True) + EPS)
        xr = x * r
        hn = (xr[:, _EARLY_COLS:] * g[:, _EARLY_COLS:]).astype(BF16)
        d_proj = dproj_ref[...]
        dhn = _dot(d_proj, w_ref[...])
        dw_ref[...] += _dot_tn(d_proj, hn)
        dg_ref[...] += jnp.sum(dhn * xr, axis=0, keepdims=True)
        a_ = dhn * g
        gx_ref[...] = dh1_ref[...] + r * a_ - xr * (r * jnp.mean(a_ * xr, axis=-1, keepdims=True))

    late_cols = D_MODEL - _EARLY_COLS
    return _call(
        body, name="in_proj_bwd", grid=(rows // tm,),
        in_specs=[_rows(tm, D_MODEL), _rows(tm, D_MODEL), _full((1, D_MODEL)), _full((D_IN, D_MODEL)), _rows(tm, D_IN),
                  pl.BlockSpec(memory_space=pl.ANY)],
        out_specs=[_rows(tm, D_MODEL), _full((D_IN, late_cols)), _full((1, D_MODEL))],
        out_shape=[_sds((rows, D_MODEL)), _sds((D_IN, late_cols)), _sds((1, D_MODEL))],
        compiler_params=_params(52, ("arbitrary",)),
    )(x2d, dh1, g1, w_in_t, d_proj, runs_after)


def _iota(shape, axis):
    return lax.broadcasted_iota(jnp.int32, shape, axis)


def _sum_of_thirds(f, a):
    hi = a.astype(BF16)
    rest = a - hi.astype(F32)
    mid = rest.astype(BF16)
    low = (rest - mid.astype(F32)).astype(BF16)
    return (f(hi) + f(mid)) + f(low)


@jax.custom_vjp
def _pick_rows(e, a):
    return _sum_of_thirds(lambda part: _dot(e, part), a)


def _pick_rows_fwd(e, a):
    return _pick_rows(e, a), e


def _pick_rows_bwd(e, ct):
    return jnp.zeros_like(e), _sum_of_thirds(lambda part: _dot_tn(e, part), ct)


_pick_rows.defvjp(_pick_rows_fwd, _pick_rows_bwd)


@jax.custom_vjp
def _pick_cols(a, e):
    return _sum_of_thirds(lambda part: _dot(part, e), a)


def _pick_cols_fwd(a, e):
    return _pick_cols(a, e), e


def _pick_cols_bwd(e, ct):
    return _sum_of_thirds(lambda part: _dot_nt(part, e), ct), jnp.zeros_like(e)


_pick_cols.defvjp(_pick_cols_fwd, _pick_cols_bwd)


_HALF_GROUPS = SSM_GROUPS // 2
_N_SHIFT = SSM_STATE.bit_length() - 1
_P_SHIFT = SSM_GROUP_CH.bit_length() - 1


def _s5_operands(lam_re, lam_im, log_step, b_re, b_im, c_re, c_im):
    g, n, p = SSM_GROUPS, SSM_STATE, SSM_GROUP_CH
    gn, gp, hn_, hp = g * n, g * p, _HALF_GROUPS * n, _HALF_GROUPS * p
    eye_g = _iota((g, g), 0) == _iota((g, g), 1)
    step = jnp.sum(jnp.where(eye_g, jnp.exp(log_step), 0.0), axis=1, keepdims=True)
    a_re = lam_re * step
    a_im = lam_im * step
    mag = jnp.exp(a_re)
    lbar_re = mag * jnp.cos(a_im)
    lbar_im = mag * jnp.sin(a_im)
    n_re = lbar_re - 1.0
    den = lam_re * lam_re + lam_im * lam_im
    f_re = (n_re * lam_re + lbar_im * lam_im) / den
    f_im = (lbar_im * lam_re - n_re * lam_im) / den

    spread_n = (_iota((n, gn), 0) == (_iota((n, gn), 1) & (n - 1))).astype(BF16)
    own_g = _iota((g, gn), 0) == (_iota((g, gn), 1) >> _N_SHIFT)

    def to_row(a):
        return jnp.sum(jnp.where(own_g, _pick_cols(a, spread_n), 0.0), axis=0, keepdims=True)

    per_group = ((_iota((gp, g), 0) >> _P_SHIFT) == _iota((gp, g), 1)).astype(BF16)
    fx_re, fx_im = _pick_rows(per_group, f_re), _pick_rows(per_group, f_im)
    bbar_re = fx_re * b_re - fx_im * b_im
    bbar_im = fx_re * b_im + fx_im * b_re

    tile_n = (_iota((n, hn_), 0) == (_iota((n, hn_), 1) & (n - 1))).astype(BF16)
    same_group = (_iota((hp, hn_), 0) >> _P_SHIFT) == (_iota((hp, hn_), 1) >> _N_SHIFT)

    def embed(a, hf):
        return jnp.where(same_group, _pick_cols(a[hf * hp:(hf + 1) * hp], tile_n), 0.0)

    return (to_row(lbar_re), to_row(lbar_im), embed(bbar_re, 0), embed(bbar_re, 1), embed(bbar_im, 0),
            embed(bbar_im, 1), embed(c_re, 0), embed(c_re, 1), embed(c_im, 0), embed(c_im, 1))


_S5_PARAM_SHAPES = ((SSM_GROUPS, SSM_STATE), (SSM_GROUPS, SSM_STATE), (1, SSM_GROUPS),
                    (D_SSM, SSM_STATE), (D_SSM, SSM_STATE), (D_SSM, SSM_STATE), (D_SSM, SSM_STATE))
_CM_SHAPE = (2, _HALF_GROUPS * SSM_GROUP_CH, _HALF_GROUPS * SSM_STATE)
_S5_OPERAND_SHAPES = ((1, SSM_LANES), (1, SSM_LANES), _CM_SHAPE, _CM_SHAPE, _CM_SHAPE, _CM_SHAPE)


def _s5_params_fwd(*params):
    def body(*refs):
        ins, (lre_ref, lim_ref, btre_ref, btim_ref, cmre_ref, cmim_ref) = refs[:7], refs[7:]
        vals = _s5_operands(*[r[...] for r in ins])
        lre_ref[...] = vals[0]
        lim_ref[...] = vals[1]
        for ref, pair in zip((btre_ref, btim_ref, cmre_ref, cmim_ref), (vals[2:4], vals[4:6], vals[6:8], vals[8:10])):
            ref[0] = pair[0].astype(BF16)
            ref[1] = pair[1].astype(BF16)

    dtypes = (F32, F32, BF16, BF16, BF16, BF16)
    return _call(
        body, name="s5_params_fwd",
        in_specs=[_full(s) for s in _S5_PARAM_SHAPES], out_specs=[_full(s) for s in _S5_OPERAND_SHAPES],
        out_shape=[_sds(s, d) for s, d in zip(_S5_OPERAND_SHAPES, dtypes)], compiler_params=_params(32),
    )(*params)


_BC_SIDE_BY_SIDE = (D_SSM, 4 * SSM_STATE)


def _s5_params_bwd(params, cotangents, runs_after):
    def body(*refs):
        ins, (dlre, dlim, dbtre, dbtim, dcmre, dcmim), outs = refs[:7], refs[7:13], refs[14:]
        _, vjp = jax.vjp(_s5_operands, *[r[...] for r in ins])
        cts = (dlre[...], dlim[...], dbtre[0], dbtre[1], dbtim[0], dbtim[1], dcmre[0], dcmre[1], dcmim[0], dcmim[1])
        grads = vjp(cts)
        for ref, val in zip(outs[:3], grads[:3]):
            ref[...] = val
        outs[3][...] = jnp.concatenate(grads[3:], axis=1).astype(BF16)

    out_shapes = _S5_PARAM_SHAPES[:3] + (_BC_SIDE_BY_SIDE,)
    return _call(
        body, name="s5_params_bwd",
        in_specs=[_full(s) for s in _S5_PARAM_SHAPES + _S5_OPERAND_SHAPES] + [pl.BlockSpec(memory_space=pl.ANY)],
        out_specs=[_full(s) for s in out_shapes],
        out_shape=[_sds(s, d) for s, d in zip(out_shapes, (F32, F32, F32, BF16))], compiler_params=_params(48),
    )(*params, *cotangents, runs_after)


def _scan_geometry(n_seq, seq):
    slab = n_seq * SCAN_CHUNKS
    steps = seq // SCAN_CHUNKS
    tile_rows = slab * SCAN_TILE_STEPS
    n_tiles = steps // SCAN_TILE_STEPS
    return slab, steps, tile_rows, n_tiles


_SCAN_PARTS = D_SSM // LANES


def _whole_parts(rows):
    return [_full((rows, LANES))] * _SCAN_PARTS


def _part_shapes(rows):
    return [_sds((rows, LANES))] * _SCAN_PARTS


def _load_chunks(parts, first_chunk, n_chunks, steps, slab):
    return jnp.concatenate([
        jnp.concatenate([ref[pl.ds(first_chunk + q, steps, stride=slab), :] for ref in parts], axis=1)
        for q in range(n_chunks)], axis=0)


def _store_chunks(parts, first_chunk, value, steps, slab):
    for q in range(value.shape[0] // steps):
        for j, ref in enumerate(parts):
            ref[pl.ds(first_chunk + q, steps, stride=slab), :] = value[q * steps:(q + 1) * steps,
                                                                     j * LANES:(j + 1) * LANES]


def _join_parts(parts):
    return jnp.concatenate([ref[...] for ref in parts], axis=1)


def _split_parts(parts, value):
    for j, ref in enumerate(parts):
        ref[...] = value[:, j * LANES:(j + 1) * LANES]


def _complex_power(re, im, n):
    out = None
    while n:
        if n & 1:
            out = (re, im) if out is None else (out[0] * re - out[1] * im, out[0] * im + out[1] * re)
        n >>= 1
        if n:
            re, im = re * re - im * im, 2.0 * re * im
    return out


def _chunk_carry(sum_re, sum_im, carry_re, carry_im, a_re, a_im, n_seq, reverse):
    carry_re[...] = jnp.zeros_like(carry_re)
    carry_im[...] = jnp.zeros_like(carry_im)
    for s in range(n_seq):
        order = range(SCAN_CHUNKS - 2, -1, -1) if reverse else range(1, SCAN_CHUNKS)
        for c in order:
            r = s * SCAN_CHUNKS + c
            p = r + 1 if reverse else r - 1
            p_re, p_im = carry_re[p:p + 1, :], carry_im[p:p + 1, :]
            carry_re[r:r + 1, :] = a_re * p_re - a_im * p_im + sum_re[p:p + 1, :]
            carry_im[r:r + 1, :] = a_re * p_im + a_im * p_re + sum_im[p:p + 1, :]


def _s5_scan_fwd(u_parts, bt_re, bt_im, cm_re, cm_im, lbar_re, lbar_im, d_row, n_seq, seq):
    slab, steps, tile_rows, n_tiles = _scan_geometry(n_seq, seq)
    rows = u_parts[0].shape[0]

    def body(*refs):
        u_refs, refs = refs[:_SCAN_PARTS], refs[_SCAN_PARTS:]
        (bre_ref, bim_ref, cre_ref, cim_ref, lre_ref, lim_ref, d_ref), refs = refs[:7], refs[7:]
        y_refs, (hre_ref, him_ref, st_re, st_im, h0_re, h0_im, buf_re, buf_im) = refs[:_SCAN_PARTS], refs[_SCAN_PARTS:]
        second = pl.program_id(0) == 1
        i = pl.program_id(1)

        @pl.when(jnp.logical_and(i == 0, jnp.logical_not(second)))
        def _():
            st_re[...] = jnp.zeros_like(st_re)
            st_im[...] = jnp.zeros_like(st_im)

        u = _join_parts(u_refs)
        ub = u.astype(BF16)
        for hf in range(2):
            cols = slice(hf * 1024, (hf + 1) * 1024)
            buf_re[:, cols] = _dot(ub[:, hf * 256:(hf + 1) * 256], bre_ref[hf])
            buf_im[:, cols] = _dot(ub[:, hf * 256:(hf + 1) * 256], bim_ref[hf])

        for lc in range(SSM_LANES // SCAN_LANE_CHUNK):
            cols = slice(lc * SCAN_LANE_CHUNK, (lc + 1) * SCAN_LANE_CHUNK)
            l_re = jnp.broadcast_to(lre_ref[:, cols], (slab, SCAN_LANE_CHUNK))
            l_im = jnp.broadcast_to(lim_ref[:, cols], (slab, SCAN_LANE_CHUNK))

            def scan_tile(keep_states):
                def step(t, carry):
                    s_re, s_im = carry
                    r0 = pl.multiple_of(t * slab, slab)
                    n_re = l_re * s_re - l_im * s_im + buf_re[pl.ds(r0, slab), cols]
                    n_im = l_re * s_im + l_im * s_re + buf_im[pl.ds(r0, slab), cols]
                    if keep_states:
                        buf_re[pl.ds(r0, slab), cols] = n_re
                        buf_im[pl.ds(r0, slab), cols] = n_im
                    return n_re, n_im

                s_re, s_im = lax.fori_loop(0, SCAN_TILE_STEPS, step, (st_re[:, cols], st_im[:, cols]), unroll=True)
                st_re[:, cols] = s_re
                st_im[:, cols] = s_im

            pl.when(jnp.logical_not(second))(lambda: scan_tile(False))
            pl.when(second)(lambda: scan_tile(True))

        @pl.when(jnp.logical_and(i == n_tiles - 1, jnp.logical_not(second)))
        def _():
            a_re, a_im = _complex_power(lre_ref[...], lim_ref[...], steps)
            _chunk_carry(st_re, st_im, h0_re, h0_im, a_re, a_im, n_seq, reverse=False)
            st_re[...] = h0_re[...]
            st_im[...] = h0_im[...]

        @pl.when(second)
        def _():
            h_re = buf_re[...].astype(BF16)
            h_im = buf_im[...].astype(BF16)
            hre_ref[...] = h_re
            him_ref[...] = h_im
            for hf in range(2):
                cols = slice(hf * 1024, (hf + 1) * 1024)
                ycols = slice(hf * 256, (hf + 1) * 256)
                y_half = (_dot_nt(h_re[:, cols], cre_ref[hf]) - _dot_nt(h_im[:, cols], cim_ref[hf])
                          + d_ref[:, ycols] * u[:, ycols])
                _split_parts(y_refs[2 * hf:2 * hf + 2], y_half)

    tile = lambda w: pl.BlockSpec((tile_rows, w), lambda p, i: (i, 0))
    out_tile = lambda w: pl.BlockSpec((tile_rows, w), lambda p, i: (i * p, 0))
    cm = _full(_CM_SHAPE)
    outs = _call(
        body, name="s5_scan_fwd", grid=(2, n_tiles),
        in_specs=[tile(LANES)] * _SCAN_PARTS + [cm, cm, cm, cm, _full((1, SSM_LANES)), _full((1, SSM_LANES)),
                                                _full((1, 512))],
        out_specs=[out_tile(LANES)] * _SCAN_PARTS + [out_tile(SSM_LANES), out_tile(SSM_LANES)],
        out_shape=_part_shapes(rows) + [_sds((rows, SSM_LANES), BF16), _sds((rows, SSM_LANES), BF16)],
        scratch_shapes=[pltpu.VMEM((slab, SSM_LANES), F32)] * 4 + [pltpu.VMEM((tile_rows, SSM_LANES), F32)] * 2,
        compiler_params=_params(40, ("arbitrary", "arbitrary")),
    )(*u_parts, bt_re, bt_im, cm_re, cm_im, lbar_re, lbar_im, d_row)
    return outs[:_SCAN_PARTS], outs[_SCAN_PARTS], outs[_SCAN_PARTS + 1]


def _s5_scan_bwd(dy_parts, u_parts, h_re, h_im, bt_re, bt_im, cm_re, cm_im, lbar_re, lbar_im, d_row, n_seq, seq):
    slab, steps, tile_rows, n_tiles = _scan_geometry(n_seq, seq)
    rows = u_parts[0].shape[0]

    def body(*refs):
        dy_refs, u_refs, refs = refs[:_SCAN_PARTS], refs[_SCAN_PARTS:2 * _SCAN_PARTS], refs[2 * _SCAN_PARTS:]
        (hre_ref, him_ref, bre_ref, bim_ref, cre_ref, cim_ref, lre_ref, lim_ref, d_ref), refs = refs[:9], refs[9:]
        du_refs, refs = refs[:_SCAN_PARTS], refs[_SCAN_PARTS:]
        (dbre_ref, dbim_ref, dcre_ref, dcim_ref, dlre_ref, dlim_ref, dd_ref,
         st_re, st_im, g0_re, g0_im, acc_re, acc_im, buf_re, buf_im) = refs
        second = pl.program_id(0) == 1
        i = pl.program_id(1)

        @pl.when(jnp.logical_and(i == 0, jnp.logical_not(second)))
        def _():
            st_re[...] = jnp.zeros_like(st_re)
            st_im[...] = jnp.zeros_like(st_im)
            acc_re[...] = jnp.zeros_like(acc_re)
            acc_im[...] = jnp.zeros_like(acc_im)
            for ref in (dbre_ref, dbim_ref, dcre_ref, dcim_ref, dd_ref):
                ref[...] = jnp.zeros_like(ref)

        dy = _join_parts(dy_refs)
        dyb = dy.astype(BF16)
        for hf in range(2):
            cols = slice(hf * 1024, (hf + 1) * 1024)
            buf_re[:, cols] = _dot(dyb[:, hf * 256:(hf + 1) * 256], cre_ref[hf])
            buf_im[:, cols] = -_dot(dyb[:, hf * 256:(hf + 1) * 256], cim_ref[hf])

        for lc in range(SSM_LANES // SCAN_LANE_CHUNK):
            cols = slice(lc * SCAN_LANE_CHUNK, (lc + 1) * SCAN_LANE_CHUNK)
            l_re = jnp.broadcast_to(lre_ref[:, cols], (slab, SCAN_LANE_CHUNK))
            l_im = jnp.broadcast_to(lim_ref[:, cols], (slab, SCAN_LANE_CHUNK))

            def advance(r0, s_re, s_im):
                n_re = l_re * s_re + l_im * s_im + buf_re[pl.ds(r0, slab), cols]
                n_im = l_re * s_im - l_im * s_re + buf_im[pl.ds(r0, slab), cols]
                buf_re[pl.ds(r0, slab), cols] = n_re
                buf_im[pl.ds(r0, slab), cols] = n_im
                return n_re, n_im

            def row0(k):
                return pl.multiple_of((SCAN_TILE_STEPS - 1 - k) * slab, slab)

            @pl.when(jnp.logical_not(second))
            def _():
                s_re, s_im = lax.fori_loop(0, SCAN_TILE_STEPS, lambda k, s: advance(row0(k), *s),
                                           (st_re[:, cols], st_im[:, cols]), unroll=True)
                st_re[:, cols] = s_re
                st_im[:, cols] = s_im

            @pl.when(second)
            def _():
                def step(k, carry):
                    s_re, s_im, a_re, a_im = carry
                    r0 = row0(k)
                    hr = hre_ref[pl.ds(r0, slab), cols].astype(F32)
                    hi = him_ref[pl.ds(r0, slab), cols].astype(F32)
                    a_re = a_re + s_re * hr + s_im * hi
                    a_im = a_im + s_im * hr - s_re * hi
                    return advance(r0, s_re, s_im) + (a_re, a_im)

                zero = jnp.zeros((slab, SCAN_LANE_CHUNK), F32)
                s_re, s_im, a_re, a_im = lax.fori_loop(
                    0, SCAN_TILE_STEPS, step, (st_re[:, cols], st_im[:, cols], zero, zero), unroll=True)
                st_re[:, cols] = s_re
                st_im[:, cols] = s_im
                acc_re[:, cols] += a_re
                acc_im[:, cols] += a_im

        @pl.when(jnp.logical_and(i == n_tiles - 1, jnp.logical_not(second)))
        def _():
            p_re, p_im = _complex_power(lre_ref[...], lim_ref[...], steps)
            _chunk_carry(st_re, st_im, g0_re, g0_im, p_re, -p_im, n_seq, reverse=True)
            st_re[...] = g0_re[...]
            st_im[...] = g0_im[...]

        @pl.when(second)
        def _():
            u = _join_parts(u_refs)
            ub = u.astype(BF16)
            g_re = buf_re[...].astype(BF16)
            g_im = buf_im[...].astype(BF16)
            dd_ref[...] += jnp.sum(dy * u, axis=0, keepdims=True)
            for hf in range(2):
                cols = slice(hf * 1024, (hf + 1) * 1024)
                ycols = slice(hf * 256, (hf + 1) * 256)
                du_half = (_dot_nt(g_re[:, cols], bre_ref[hf]) + _dot_nt(g_im[:, cols], bim_ref[hf])
                           + d_ref[:, ycols] * dy[:, ycols])
                _split_parts(du_refs[2 * hf:2 * hf + 2], du_half)
                for q4 in range(_HALF_GROUPS // 4):
                    ch = slice(hf * 256 + q4 * 64, hf * 256 + (q4 + 1) * 64)
                    st = slice(hf * 1024 + q4 * 256, hf * 1024 + (q4 + 1) * 256)
                    blk = (hf, slice(q4 * 64, (q4 + 1) * 64), slice(q4 * 256, (q4 + 1) * 256))
                    dbre_ref[blk] += _dot_tn(ub[:, ch], g_re[:, st])
                    dbim_ref[blk] += _dot_tn(ub[:, ch], g_im[:, st])
                    dcre_ref[blk] += _dot_tn(dyb[:, ch], hre_ref[:, st])
                    dcim_ref[blk] -= _dot_tn(dyb[:, ch], him_ref[:, st])

        @pl.when(jnp.logical_and(i == n_tiles - 1, second))
        def _():
            dlre_ref[...] = jnp.sum(acc_re[...], axis=0, keepdims=True)
            dlim_ref[...] = jnp.sum(acc_im[...], axis=0, keepdims=True)

    tile = lambda w: pl.BlockSpec((tile_rows, w), lambda p, i: (n_tiles - 1 - i, 0))
    second_tile = lambda w: pl.BlockSpec((tile_rows, w), lambda p, i: (n_tiles - 1 - i * p, 0))
    cm = _full(_CM_SHAPE)
    row = _full((1, SSM_LANES))
    outs = _call(
        body, name="s5_scan_bwd", grid=(2, n_tiles),
        in_specs=[tile(LANES)] * _SCAN_PARTS + [second_tile(LANES)] * _SCAN_PARTS
        + [second_tile(SSM_LANES), second_tile(SSM_LANES), cm, cm, cm, cm, row, row, _full((1, 512))],
        out_specs=[second_tile(LANES)] * _SCAN_PARTS + [cm, cm, cm, cm, row, row, _full((1, 512))],
        out_shape=(_part_shapes(rows) + [_sds(_CM_SHAPE)] * 4 + [_sds((1, SSM_LANES))] * 2 + [_sds((1, 512))]),
        scratch_shapes=[pltpu.VMEM((slab, SSM_LANES), F32)] * 6 + [pltpu.VMEM((tile_rows, SSM_LANES), F32)] * 2,
        compiler_params=_params(48, ("arbitrary", "arbitrary")),
    )(*dy_parts, *u_parts, h_re, h_im, bt_re, bt_im, cm_re, cm_im, lbar_re, lbar_im, d_row)
    return (outs[:_SCAN_PARTS],) + tuple(outs[_SCAN_PARTS:])


def _glu_gate(gl, a, zs):
    return gl * jax.nn.sigmoid(a) * _silu(zs)


def _glu_fwd(y_parts, zs, w_glu, b_glu, n_seq, seq):
    rows = zs.shape[0]
    tm = 1024
    slab, steps, _, _ = _scan_geometry(n_seq, seq)

    def body(*refs):
        y_refs, (zs_ref, w_ref, b_ref, o_ref) = refs[:_SCAN_PARTS], refs[_SCAN_PARTS:]
        y = _load_chunks(y_refs, pl.program_id(0) * (tm // steps), tm // steps, steps, slab)
        gl = jax.nn.gelu(y)
        a = _dot(gl.astype(BF16), w_ref[...]) + b_ref[...]
        o_ref[...] = _glu_gate(gl, a, zs_ref[...]).astype(BF16)

    return _call(
        body, name="glu_fwd", grid=(rows // tm,),
        in_specs=_whole_parts(rows) + [_rows(tm, 512), _full((512, 512)), _full((1, 512))],
        out_specs=_rows(tm, 512), out_shape=_sds((rows, 512), BF16),
        compiler_params=_params(32, ("arbitrary",)),
    )(*y_parts, zs, w_glu, b_glu)


def _glu_bwd(y_parts, zs, d_out, w_glu, b_glu, n_seq, seq):
    rows = zs.shape[0]
    tm = 512
    slab, steps, _, _ = _scan_geometry(n_seq, seq)

    def body(*refs):
        y_refs, (zs_ref, d_ref, w_ref, b_ref), refs = refs[:_SCAN_PARTS], refs[_SCAN_PARTS:_SCAN_PARTS + 4], refs[_SCAN_PARTS + 4:]
        dy_refs, (dzs_ref, dw_ref, db_ref) = refs[:_SCAN_PARTS], refs[_SCAN_PARTS:]
        first_chunk = pl.program_id(0) * (tm // steps)

        @pl.when(pl.program_id(0) == 0)
        def _():
            dw_ref[...] = jnp.zeros_like(dw_ref)
            db_ref[...] = jnp.zeros_like(db_ref)

        gl, gelu_vjp = jax.vjp(jax.nn.gelu, _load_chunks(y_refs, first_chunk, tm // steps, steps, slab))
        glb = gl.astype(BF16)
        a = _dot(glb, w_ref[...]) + b_ref[...]
        _, gate_vjp = jax.vjp(_glu_gate, gl, a, zs_ref[...])
        d_gl, d_a, d_zs = gate_vjp(d_ref[...])
        dab = d_a.astype(BF16)
        d_gl = d_gl + _dot_nt(dab, w_ref[...])
        _store_chunks(dy_refs, first_chunk, gelu_vjp(d_gl)[0], steps, slab)
        dzs_ref[...] = d_zs.astype(BF16)
        dw_ref[...] += _dot_tn(glb, dab)
        db_ref[...] += jnp.sum(d_a, axis=0, keepdims=True)

    *dy_parts, dzs, dw, db = _call(
        body, name="glu_bwd", grid=(rows // tm,),
        in_specs=_whole_parts(rows) + [_rows(tm, 512), _rows(tm, 512), _full((512, 512)), _full((1, 512))],
        out_specs=_whole_parts(rows) + [_rows(tm, 512), _full((512, 512)), _full((1, 512))],
        out_shape=_part_shapes(rows) + [_sds((rows, 512), BF16), _sds((512, 512)), _sds((1, 512))],
        compiler_params=_params(40, ("arbitrary",)),
    )(*y_parts, zs, d_out, w_glu, b_glu)
    return dy_parts, dzs, dw, db


_GROUP_ROWS = Q_PER_KV * BLOCK
_BLOCK_SHIFT = BLOCK.bit_length() - 1


def _attn_bias(j):
    query = _iota((BLOCK, _GROUP_ROWS), 1)
    dist_cur = (query & (BLOCK - 1)) - _iota((BLOCK, _GROUP_ROWS), 0)
    dist_prev = dist_cur + BLOCK
    head = query >> _BLOCK_SHIFT
    slope = jnp.zeros((BLOCK, _GROUP_ROWS), F32)
    for g in range(Q_PER_KV):
        slope = jnp.where(head == g, 2.0 ** (-(j * Q_PER_KV + g + 1)), slope)
    bias_cur = jnp.where(dist_cur >= 0, -slope * dist_cur.astype(F32), -jnp.inf)
    bias_prev = jnp.where(dist_prev < WINDOW, -slope * dist_prev.astype(F32), -jnp.inf)
    return bias_cur, bias_prev


_ATTN_BIAS_SCRATCH = pltpu.VMEM((KV_HEADS, 2, BLOCK, _GROUP_ROWS), F32)


def _fill_attn_bias(bias_ref):
    @pl.when(jnp.logical_and(pl.program_id(0) == 0, pl.program_id(1) == 0))
    def _():
        for j in range(KV_HEADS):
            bias_ref[j, 0], bias_ref[j, 1] = _attn_bias(j)


def _stack_heads(x, j):
    heads = range(j * Q_PER_KV, (j + 1) * Q_PER_KV)
    return jnp.concatenate([x[:, h * HEAD_DIM:(h + 1) * HEAD_DIM] for h in heads], axis=0)


def _head_rows(x, j):
    heads = range(j * Q_PER_KV, (j + 1) * Q_PER_KV)
    return jnp.concatenate([x[h:h + 1, :] for h in heads], axis=1)


def _sink_row(sk_ref, j):
    heads = range(j * Q_PER_KV, (j + 1) * Q_PER_KV)
    return jnp.concatenate([jnp.broadcast_to(sk_ref[0:1, h:h + 1], (1, BLOCK)) for h in heads], axis=1)


_ATTN_FWD_BLOCKS = 8


def _attn_fwd(q, k, v, za, sinks, n_seq, seq):
    nb = seq // BLOCK
    steps = nb // _ATTN_FWD_BLOCKS
    rows = q.shape[0]

    def body(q_ref, kc_ref, kp_ref, vc_ref, vp_ref, za_ref, sk_ref, o_ref, ao_ref, lse_ref, bias_ref):
        _fill_attn_bias(bias_ref)
        for t in range(_ATTN_FWD_BLOCKS):
            at = slice(t * BLOCK, (t + 1) * BLOCK)
            before = slice((t - 1) * BLOCK, t * BLOCK)
            q_all = q_ref[at, :]
            for j in range(KV_HEADS):
                js = slice(j * HEAD_DIM, (j + 1) * HEAD_DIM)
                bias_c, bias_p = bias_ref[j, 0], bias_ref[j, 1]
                q4 = _stack_heads(q_all, j)
                sc = _dot_nt(kc_ref[at, js], q4) + bias_c
                if t == 0:
                    sp = _dot_nt(kp_ref[:, js], q4) + jnp.where(pl.program_id(1) > 0, bias_p, -jnp.inf)
                    v_prev = vp_ref[:, js]
                else:
                    sp = _dot_nt(kc_ref[before, js], q4) + bias_p
                    v_prev = vc_ref[before, js]
                sink = _sink_row(sk_ref, j)
                m = jnp.maximum(jnp.max(jnp.maximum(sc, sp), axis=0, keepdims=True), sink)
                ec = jnp.exp(sc - m)
                ep = jnp.exp(sp - m)
                den = jnp.sum(ec + ep, axis=0, keepdims=True) + jnp.exp(sink - m)
                inv = 1.0 / den
                o4 = _dot_tn((ec * inv).astype(BF16), vc_ref[at, js]) + _dot_tn((ep * inv).astype(BF16), v_prev)
                lse4 = m + jnp.log(den)
                for g in range(Q_PER_KV):
                    h = j * Q_PER_KV + g
                    o_ref[at, h * HEAD_DIM:(h + 1) * HEAD_DIM] = o4[g * BLOCK:(g + 1) * BLOCK]
                    lse_ref[t * N_HEADS + h:t * N_HEADS + h + 1, :] = lse4[:, g * BLOCK:(g + 1) * BLOCK]
        ao_ref[...] = (o_ref[...] * _silu(za_ref[...])).astype(BF16)

    cur = lambda w: pl.BlockSpec((_ATTN_FWD_BLOCKS * BLOCK, w), lambda b, n: (b * steps + n, 0))
    prev = lambda w: pl.BlockSpec((BLOCK, w), lambda b, n: (b * nb + jnp.maximum(_ATTN_FWD_BLOCKS * n - 1, 0), 0))
    lse_rows = rows // BLOCK * N_HEADS
    return _call(
        body, name="attn_fwd", grid=(n_seq, steps),
        in_specs=[cur(512), cur(128), prev(128), cur(128), prev(128), cur(512), _full((1, N_HEADS))],
        out_specs=[cur(512), cur(512),
                   pl.BlockSpec((_ATTN_FWD_BLOCKS * N_HEADS, BLOCK), lambda b, n: (b * steps + n, 0))],
        out_shape=[_sds((rows, 512)), _sds((rows, 512), BF16), _sds((lse_rows, BLOCK))],
        scratch_shapes=[_ATTN_BIAS_SCRATCH], compiler_params=_params(32, ("arbitrary", "arbitrary")),
    )(q, k, k, v, v, za, sinks)


_ATTN_BWD_BLOCKS = 4


def _attn_bwd(q, k, v, za, o, lse, d_ao, sinks, n_seq, seq):
    nb = seq // BLOCK
    per_step = _ATTN_BWD_BLOCKS
    steps = nb // per_step
    rows = q.shape[0]

    def body(q_ref, kc_ref, kp_ref, vc_ref, vp_ref, za_ref, o_ref, lse_ref, d_ref, sk_ref,
             dq_ref, dk_ref, dv_ref, dza_ref, dsk_ref, bias_ref, dk_carry, dv_carry):
        step = pl.program_id(1)
        first_block = nb - per_step * (step + 1)
        _fill_attn_bias(bias_ref)

        @pl.when(jnp.logical_and(pl.program_id(0) == 0, step == 0))
        def _():
            dsk_ref[...] = jnp.zeros_like(dsk_ref)
            dk_carry[...] = jnp.zeros_like(dk_carry)
            dv_carry[...] = jnp.zeros_like(dv_carry)

        _, gate_vjp = jax.vjp(lambda o_, z_: o_ * _silu(z_), o_ref[...], za_ref[...])
        d_o, d_za = gate_vjp(d_ref[...])
        dza_ref[...] = d_za.astype(BF16)

        for j in range(KV_HEADS):
            js = slice(j * HEAD_DIM, (j + 1) * HEAD_DIM)
            bias_c, bias_p = bias_ref[j, 0], bias_ref[j, 1]
            sink = _sink_row(sk_ref, j)
            sink_loss = jnp.zeros((1, _GROUP_ROWS), F32)
            dk_from_next = jnp.where(step > 0, dk_carry[j], 0.0)
            dv_from_next = jnp.where(step > 0, dv_carry[j], 0.0)
            for t in reversed(range(per_step)):
                at = slice(t * BLOCK, (t + 1) * BLOCK)
                kc, vc = kc_ref[at, js], vc_ref[at, js]
                if t > 0:
                    before = slice((t - 1) * BLOCK, t * BLOCK)
                    kp, vp, bias_before = kc_ref[before, js], vc_ref[before, js], bias_p
                else:
                    kp, vp, bias_before = kp_ref[:, js], vp_ref[:, js], jnp.where(first_block > 0, bias_p, -jnp.inf)
                q4 = _stack_heads(q_ref[at, :], j)
                do4b = _stack_heads(d_o[at], j).astype(BF16)
                lse4 = _head_rows(lse_ref[t * N_HEADS:(t + 1) * N_HEADS, :], j)
                pc = jnp.exp(_dot_nt(kc, q4) + bias_c - lse4)
                pp = jnp.exp(_dot_nt(kp, q4) + bias_before - lse4)
                dpc = _dot_nt(vc, do4b)
                dpp = _dot_nt(vp, do4b)
                delta = jnp.sum(pc * dpc + pp * dpp, axis=0, keepdims=True)
                dsc = (pc * (dpc - delta)).astype(BF16)
                dsp = (pp * (dpp - delta)).astype(BF16)
                dq4 = ((_dot_tn(dsc, kc) + _dot_tn(dsp, kp)) * ATTN_SCALE).astype(BF16)
                sink_loss = sink_loss + jnp.exp(sink - lse4) * delta
                for g in range(Q_PER_KV):
                    h = j * Q_PER_KV + g
                    dq_ref[at, h * HEAD_DIM:(h + 1) * HEAD_DIM] = dq4[g * BLOCK:(g + 1) * BLOCK]
                dk_ref[at, js] = (_dot(dsc, q4) + dk_from_next).astype(BF16)
                dv_ref[at, js] = (_dot(pc.astype(BF16), do4b) + dv_from_next).astype(BF16)
                dk_from_next = _dot(dsp, q4)
                dv_from_next = _dot(pp.astype(BF16), do4b)
            dk_carry[j] = dk_from_next
            dv_carry[j] = dv_from_next
            for g in range(Q_PER_KV):
                h = j * Q_PER_KV + g
                dsk_ref[0:1, h:h + 1] -= jnp.sum(sink_loss[:, g * BLOCK:(g + 1) * BLOCK], axis=1, keepdims=True)

    cur = lambda w: pl.BlockSpec((per_step * BLOCK, w), lambda b, s: (b * steps + steps - 1 - s, 0))
    prev = lambda w: pl.BlockSpec((BLOCK, w), lambda b, s: (b * nb + jnp.maximum(nb - per_step * (s + 1) - 1, 0), 0))
    return _call(
        body, name="attn_bwd", grid=(n_seq, steps),
        in_specs=[cur(512), cur(128), prev(128), cur(128), prev(128), cur(512), cur(512),
                  pl.BlockSpec((per_step * N_HEADS, BLOCK), lambda b, s: (b * steps + steps - 1 - s, 0)), cur(512),
                  _full((1, N_HEADS))],
        out_specs=[cur(512), cur(128), cur(128), cur(512), _full((1, N_HEADS))],
        out_shape=[_sds((rows, 512), BF16), _sds((rows, 128), BF16), _sds((rows, 128), BF16),
                   _sds((rows, 512), BF16), _sds((1, N_HEADS))],
        scratch_shapes=[_ATTN_BIAS_SCRATCH, pltpu.VMEM((KV_HEADS, BLOCK, HEAD_DIM), F32),
                        pltpu.VMEM((KV_HEADS, BLOCK, HEAD_DIM), F32)],
        compiler_params=_params(32, ("arbitrary", "arbitrary")),
    )(q, k, k, v, v, za, o, lse, d_ao, sinks)


def _tail(ssm_out, attn_out, x2d, p2d, target, w_out, g2, w_gate, b_gate, w_proj):
    rows = x2d.shape[0]
    tm = 512

    def body(so_ref, ao_ref, x_ref, p_ref, t_ref, wo_ref, g2_ref, wg_ref, bg_ref, wp_ref,
             dh1_ref, dso_ref, dao_ref, dwo_ref, dwg_ref, dwp_ref, dbg_ref, dg2_ref, loss_ref):
        @pl.when(pl.program_id(0) == 0)
        def _():
            for ref in (dwo_ref, dwg_ref, dwp_ref, dbg_ref, dg2_ref, loss_ref):
                ref[...] = jnp.zeros_like(ref)

        cat = jnp.concatenate([so_ref[...], ao_ref[...]], axis=1)
        g2 = g2_ref[...]
        mixed = _dot(cat, wo_ref[...])
        r = lax.rsqrt(jnp.mean(mixed * mixed, axis=-1, keepdims=True) + EPS)
        mr = mixed * r
        h1 = x_ref[...] + mr * g2
        h1b = h1.astype(BF16)
        gate = jax.nn.sigmoid(_dot(h1b, wg_ref[...]) + bg_ref[...])
        pb = p_ref[...].astype(BF16)
        wp_blocks = [slice(j * D_PLE, (j + 1) * D_PLE) for j in range(N_CHIPS)]
        pp = jnp.concatenate([_dot(pb, wp_ref[blk, :]) for blk in wp_blocks], axis=1)
        err = h1 + gate * pp - t_ref[...]
        loss_ref[...] += 0.5 * jnp.sum(jnp.mean(err * err, axis=-1, keepdims=True), axis=0, keepdims=True)

        dh2 = err * (1.0 / D_MODEL)
        d_glin = dh2 * pp * gate * (1.0 - gate)
        d_glin_b = d_glin.astype(BF16)
        dwg_ref[...] += _dot_tn(h1b, d_glin_b)
        dbg_ref[...] += jnp.sum(d_glin, axis=0, keepdims=True)
        d_pp = (dh2 * gate).astype(BF16)
        for blk in wp_blocks:
            dwp_ref[blk, :] += _dot_tn(pb, d_pp[:, blk])
        dh1 = dh2 + _dot_nt(d_glin_b, wg_ref[...])
        dh1_ref[...] = dh1
        dg2_ref[...] += jnp.sum(dh1 * mr, axis=0, keepdims=True)
        a_ = dh1 * g2
        d_mixed = (r * a_ - mr * (r * jnp.mean(a_ * mr, axis=-1, keepdims=True))).astype(BF16)
        dwo_ref[...] += _dot_tn(cat, d_mixed)
        d_cat = _dot_nt(d_mixed, wo_ref[...])
        dso_ref[...] = d_cat[:, 0:512]
        dao_ref[...] = d_cat[:, 512:1024]

    return _call(
        body, name="tail_fwd_bwd", grid=(rows // tm,),
        in_specs=[_rows(tm, 512), _rows(tm, 512), _rows(tm, D_MODEL), _rows(tm, D_PLE), _rows(tm, D_MODEL),
                  _full((D_MODEL, D_MODEL)), _full((1, D_MODEL)), _full((D_MODEL, D_MODEL)), _full((1, D_MODEL)),
                  _full((N_CHIPS * D_PLE, D_PLE))],
        out_specs=[_rows(tm, D_MODEL), _rows(tm, 512), _rows(tm, 512), _full((D_MODEL, D_MODEL)),
                   _full((D_MODEL, D_MODEL)), _full((N_CHIPS * D_PLE, D_PLE)), _full((1, D_MODEL)), _full((1, D_MODEL)),
                   _full((1, 1))],
        out_shape=[_sds((rows, D_MODEL)), _sds((rows, 512)), _sds((rows, 512)), _sds((D_MODEL, D_MODEL)),
                   _sds((D_MODEL, D_MODEL)), _sds((N_CHIPS * D_PLE, D_PLE)), _sds((1, D_MODEL)), _sds((1, D_MODEL)),
                   _sds((1, 1))],
        compiler_params=_params(52, ("arbitrary",)),
    )(ssm_out, attn_out, x2d, p2d, target, w_out, g2, w_gate, b_gate, w_proj)


def _local_step(x, hn, p, target, pre_norm_g, w_in_t, s5_params, s5_operands, ssm_d, w_glu, b_glu, sinks, w_out,
                post_norm_g, w_proj, w_gate, b_gate, send_tail_grads=lambda ready: ready["w_out"],
                send_bc=lambda d_bc: (d_bc, d_bc)):
    n_seq, seq, _ = x.shape
    rows = n_seq * seq
    x2d = x.reshape(rows, D_MODEL)
    p2d = p.reshape(rows, D_PLE)
    t2d = target.reshape(rows, D_MODEL)

    l_re, l_im, bt_re, bt_im, cm_re, cm_im = s5_operands

    u_scan, zs, q, k, v, za = _in_proj(hn, w_in_t, n_seq, seq)
    y_scan, h_re, h_im = _s5_scan_fwd(u_scan, bt_re, bt_im, cm_re, cm_im, l_re, l_im, ssm_d, n_seq, seq)
    ssm_out = _glu_fwd(y_scan, zs, w_glu, b_glu, n_seq, seq)
    o, attn_out, lse = _attn_fwd(q, k, v, za, sinks, n_seq, seq)

    dh1, d_so, d_ao, d_w_out, d_w_gate, d_w_proj, d_b_gate, d_g2, loss = _tail(
        ssm_out, attn_out, x2d, p2d, t2d, w_out, post_norm_g, w_gate, b_gate, w_proj)

    dq, dk, dv, dza, d_sinks = _attn_bwd(q, k, v, za, o, lse, d_ao, sinks, n_seq, seq)
    dy_scan, dzs, d_w_glu, d_b_glu = _glu_bwd(y_scan, zs, d_so, w_glu, b_glu, n_seq, seq)
    du_scan, d_bt_re, d_bt_im, d_cm_re, d_cm_im, d_l_re, d_l_im, d_d = _s5_scan_bwd(
        dy_scan, u_scan, h_re, h_im, bt_re, bt_im, cm_re, cm_im, l_re, l_im, ssm_d, n_seq, seq)
    tail_grads_arrived = send_tail_grads(dict(w_out=d_w_out, pl_w_gate=d_w_gate, pl_w_proj=d_w_proj))
    d_lam_re, d_lam_im, d_log_step, d_bc = _s5_params_bwd(
        s5_params, (d_l_re, d_l_im, d_bt_re, d_bt_im, d_cm_re, d_cm_im), tail_grads_arrived)

    sent, arrived = send_bc(d_bc)
    d_proj, d_w_in_early, d_w_in_early_b = _in_proj_bwd_early(hn, du_scan, dzs, dq, dk, dv, dza, sent, n_seq, seq)
    grad_x, d_w_in_late, d_g1 = _in_proj_bwd(x2d, dh1, pre_norm_g, w_in_t, d_proj, arrived)
    grads = dict(
        pre_norm_g=d_g1, w_in_early=d_w_in_early, w_in_early_bf16=d_w_in_early_b, w_in_late=d_w_in_late,
        ssm_lam_re=d_lam_re, ssm_lam_im=d_lam_im, ssm_log_step=d_log_step, ssm_bc=d_bc, ssm_d=d_d, ssm_w_glu=d_w_glu,
        ssm_b_glu=d_b_glu, attn_sinks=d_sinks, w_out=d_w_out, post_norm_g=d_g2, pl_w_proj=d_w_proj,
        pl_w_gate=d_w_gate, pl_b_gate=d_b_gate)
    return grad_x.reshape(x.shape), loss, grads


_BIG = ("w_in", "ssm_w_glu", "w_out", "pl_w_proj", "pl_w_gate")
_BIG_SHARD = {"w_in": (D_IN // N_CHIPS, D_MODEL), "ssm_w_glu": (D_SSM // N_CHIPS, D_SSM),
              "w_out": (D_MODEL // N_CHIPS, D_MODEL), "pl_w_proj": (D_PLE, D_MODEL // N_CHIPS),
              "pl_w_gate": (D_MODEL // N_CHIPS, D_MODEL)}
_SMALL = {"pre_norm_g": (1, D_MODEL), "ssm_lam_re": (SSM_GROUPS, SSM_STATE), "ssm_lam_im": (SSM_GROUPS, SSM_STATE),
          "ssm_log_step": (1, SSM_GROUPS), "ssm_b_re": (D_SSM, SSM_STATE), "ssm_b_im": (D_SSM, SSM_STATE),
          "ssm_c_re": (D_SSM, SSM_STATE), "ssm_c_im": (D_SSM, SSM_STATE), "ssm_d": (1, D_SSM), "ssm_b_glu": (1, D_SSM),
          "attn_sinks": (1, N_HEADS), "post_norm_g": (1, D_MODEL), "pl_b_gate": (1, D_MODEL)}
_VEC_ROWS = ("pre_norm_g", "post_norm_g", "pl_b_gate", "ssm_d", "ssm_b_glu", "attn_sinks", "ssm_log_step", "loss")
_SMALL_GROUPS = (
    ("vec", (8, D_MODEL), tuple((name, r) for r, name in enumerate(_VEC_ROWS))),
    ("lam", (2 * SSM_GROUPS, SSM_STATE), (("ssm_lam_re", 0), ("ssm_lam_im", SSM_GROUPS))),
)
_SMALL_EARLY = ("ssm_b_re", "ssm_b_im", "ssm_c_re", "ssm_c_im")
_SMALL_ORDER = tuple(name for _, _, members in _SMALL_GROUPS for name, _ in members) + _SMALL_EARLY
_WEIGHT_ORDER = ("pre_norm_g", "w_in", "ssm_lam_re", "ssm_lam_im", "ssm_log_step", "ssm_b_re", "ssm_b_im", "ssm_c_re",
                 "ssm_c_im", "ssm_d", "ssm_w_glu", "ssm_b_glu", "attn_sinks", "w_out", "post_norm_g", "pl_w_proj",
                 "pl_w_gate", "pl_b_gate")


def _small_shape(name):
    return (1, 1) if name == "loss" else _SMALL[name]


def _to_kernel_form(name, a):
    a = a[0]
    if name == "w_in":
        return a.T
    if name in ("ssm_b_re", "ssm_b_im"):
        a = a.transpose(0, 2, 1)
    return a.reshape(_SMALL[name]) if name in _SMALL else a


def _from_kernel_form(name, a, shape):
    if name == "w_in":
        a = a.T
    if name in ("ssm_b_re", "ssm_b_im"):
        a = a.reshape(SSM_GROUPS, SSM_GROUP_CH, SSM_STATE).transpose(0, 2, 1)
    return a.reshape(shape)


def _mesh_place():
    x, y, c = lax.axis_index("x"), lax.axis_index("y"), lax.axis_index("c")
    other_chips = ((1 - x, y), (x, 1 - y), (1 - x, 1 - y))
    return x, y, c, other_chips


def _gather_copies(s_refs, g_refs, send_sems, recv_sems, local_sems):
    x, y, c, other_chips = _mesh_place()
    started = []
    for i, (s_ref, g_ref) in enumerate(zip(s_refs, g_refs)):
        rows = s_ref.shape[0]
        half = rows // 2

        def block(chip, g_ref=g_ref, rows=rows, half=half):
            return g_ref.at[pl.ds((2 * chip[0] + chip[1]) * rows + c * half, half), :]

        def copy(k, chip, to, src=None, i=i, block=block):
            return pltpu.make_async_remote_copy(
                src_ref=block(chip) if src is None else src, dst_ref=block(chip), send_sem=send_sems.at[6 * i + k],
                recv_sem=recv_sems.at[6 * i + k], device_id=to, device_id_type=MESH)

        own = pltpu.make_async_copy(s_ref, g_ref.at[pl.ds((2 * x + y) * rows, rows), :], local_sems.at[i])
        own.start()
        first = [copy(k, (x, y), (*chip, c), src=s_ref.at[pl.ds(c * half, half), :])
                 for k, chip in enumerate(other_chips)]
        for cp in first:
            cp.start()
        passed = [copy(3 + k, chip, (x, y, 1 - c)) for k, chip in enumerate(other_chips)]
        started.append((own, first, passed))
    for own, first, passed in started:
        for k in range(3):
            first[k].wait_recv()
            passed[k].start()
    for own, first, passed in started:
        for k in range(3):
            passed[k].wait_recv()
        for cp in first + passed:
            cp.wait_send()
        own.wait()


def _gather_semaphores(n_t):
    return [pltpu.SemaphoreType.DMA((6 * n_t,)), pltpu.SemaphoreType.DMA((6 * n_t,)), pltpu.SemaphoreType.DMA((n_t,))]


def _gather_weights_beside(shards, name, collective_id):
    n_t = len(shards)
    hbm = pltpu.MemorySpace.HBM
    s_refs = [jax.new_ref(s, memory_space=hbm) for s in shards]
    g_refs = [jax.empty_ref(jax.ShapeDtypeStruct((N_CHIPS * s.shape[0], s.shape[1]), s.dtype), memory_space=hbm)
              for s in shards]

    def launch(send_sems, recv_sems, local_sems):
        x, y, c, other_chips = _mesh_place()
        peers = [(*chip, c) for chip in other_chips] + [(x, y, 1 - c)]
        barrier = pltpu.get_barrier_semaphore()
        for peer in peers:
            pl.semaphore_signal(barrier, inc=1, device_id=peer, device_id_type=MESH)
        pl.semaphore_wait(barrier, len(peers))
        _gather_copies(s_refs, g_refs, send_sems, recv_sems, local_sems)

    pl.kernel(launch, mesh=plsc.ScalarSubcoreMesh(axis_name="sequencer", num_cores=1), name=name,
              scratch_types=_gather_semaphores(n_t), compiler_params=pltpu.CompilerParams(collective_id=collective_id))()
    return [g[...] for g in g_refs]


_RELATIONS = tuple(((r >> 2) & 1, (r >> 1) & 1, r & 1) for r in range(1, 8))


def _related(place, relation):
    return tuple(1 - a if flip else a for a, flip in zip(place, relation))


def _scatter_beside(mats, name, collective_id):
    hbm = pltpu.MemorySpace.HBM
    src_refs = [jax.new_ref(a, memory_space=hbm) for a in mats]
    land_refs = [jax.empty_ref(jax.ShapeDtypeStruct((7, a.shape[0] // 8, a.shape[1]), a.dtype), memory_space=hbm)
                 for a in mats]

    def launch(send_sems, recv_sems):
        me = (lax.axis_index("x"), lax.axis_index("y"), lax.axis_index("c"))
        peers = [_related(me, rel) for rel in _RELATIONS]
        barrier = pltpu.get_barrier_semaphore()
        for peer in peers:
            pl.semaphore_signal(barrier, inc=1, device_id=peer, device_id_type=MESH)
        pl.semaphore_wait(barrier, len(peers))
        copies = []
        for i, (src, land) in enumerate(zip(src_refs, land_refs)):
            hr = land.shape[1]
            for k, (tx, ty, tc) in enumerate(peers):
                rows = pl.ds((2 * tx + ty) * 2 * hr + tc * hr, hr)
                copies.append(pltpu.make_async_remote_copy(
                    src_ref=src.at[rows, :], dst_ref=land.at[k], send_sem=send_sems.at[7 * i + k],
                    recv_sem=recv_sems.at[7 * i + k], device_id=(tx, ty, tc), device_id_type=MESH))
                copies[-1].start()
        for cp in copies:
            cp.wait()

    n_sems = 7 * len(mats)
    pl.kernel(launch, mesh=plsc.ScalarSubcoreMesh(axis_name="sequencer", num_cores=1), name=name,
              scratch_types=[pltpu.SemaphoreType.DMA((n_sems,)), pltpu.SemaphoreType.DMA((n_sems,))],
              compiler_params=pltpu.CompilerParams(collective_id=collective_id))()
    return [ref[...] for ref in land_refs]


def _broadcast_beside(arrays):
    hbm = pltpu.MemorySpace.HBM
    src_refs = [jax.new_ref(a, memory_space=hbm) for a in arrays]
    land_refs = [jax.empty_ref(jax.ShapeDtypeStruct((len(_RELATIONS),) + a.shape, a.dtype), memory_space=hbm)
                 for a in arrays]

    def launch(send_sems, recv_sems):
        me = (lax.axis_index("x"), lax.axis_index("y"), lax.axis_index("c"))
        peers = [_related(me, rel) for rel in _RELATIONS]
        barrier = pltpu.get_barrier_semaphore()
        for peer in peers:
            pl.semaphore_signal(barrier, inc=1, device_id=peer, device_id_type=MESH)
        pl.semaphore_wait(barrier, len(peers))
        copies = []
        for i, (src, land) in enumerate(zip(src_refs, land_refs)):
            for k, peer in enumerate(peers):
                copies.append(pltpu.make_async_remote_copy(
                    src_ref=src, dst_ref=land.at[k], send_sem=send_sems.at[7 * i + k],
                    recv_sem=recv_sems.at[7 * i + k], device_id=peer, device_id_type=MESH))
                copies[-1].start()
        for cp in copies:
            cp.wait()

    n_sems = 7 * len(arrays)
    pl.kernel(launch, mesh=plsc.ScalarSubcoreMesh(axis_name="sequencer", num_cores=1), name="broadcast_beside",
              scratch_types=[pltpu.SemaphoreType.DMA((n_sems,)), pltpu.SemaphoreType.DMA((n_sems,))],
              compiler_params=pltpu.CompilerParams(collective_id=3))()
    return [ref[...] for ref in land_refs]


def _exchange_grads(big, outputs, small, landed, own_bc, landed_bc):
    n_t = len(big)
    n_g = len(_SMALL_GROUPS)
    names = _SMALL_ORDER
    halves = [(b.shape[0] // N_CHIPS // 2, b.shape[1]) for b in big]
    early = sorted(landed)
    late = [i for i in range(n_t) if i not in landed]
    n_sems = 4 * n_g + 7 * len(late) + n_t
    small_sem0, block_sem0 = n_t, n_t + len(names)
    early_sem0 = block_sem0 + N_CHIPS * len(late)
    landed_sem0 = early_sem0 + 2 * len(early)
    sent = [n for n in names if n in small]

    def body(*refs):
        pos = 0

        def take(n):
            nonlocal pos
            pos += n
            return refs[pos - n:pos]

        big_refs, small_refs = take(n_t), dict(zip(sent, take(len(sent))))
        land_refs = dict(zip(early, take(len(early))))
        own_bc_ref, landed_bc_ref = take(2)
        out_refs, small_out_refs = take(len(outputs)), dict(zip(names, take(len(names))))
        per_late = lambda: dict(zip(late, take(len(late))))
        ga, gb, pme, send_b, recv_b = per_late(), per_late(), take(n_t), per_late(), per_late()
        own_e, land_e = dict(zip(early, take(len(early)))), dict(zip(early, take(len(early))))
        own_s, land_s = take(2)
        s_own, s_sib, s_chips, s_pair = take(n_g), take(n_g), take(n_g), take(n_g)
        stage = dict(zip(names, take(len(names))))
        send_sems, recv_sems, local_sems = take(3)
        x, y, c, other_chips = _mesh_place()
        me = 2 * x + y
        sibling = (x, y, 1 - c)
        sem_at = iter(range(n_sems))

        def remote(src, dst, to):
            k = next(sem_at)
            return pltpu.make_async_remote_copy(src_ref=src, dst_ref=dst, send_sem=send_sems.at[k],
                                                recv_sem=recv_sems.at[k], device_id=to, device_id_type=MESH)

        loads = [pltpu.make_async_copy(small_refs[name], stage[name], local_sems.at[small_sem0 + names.index(name)])
                 for name in sent]
        landed_loads = [pltpu.make_async_copy(own_bc_ref, own_s, local_sems.at[landed_sem0]),
                        pltpu.make_async_copy(landed_bc_ref, land_s, local_sems.at[landed_sem0 + 1])]
        for cp in loads + landed_loads:
            cp.start()
        for cp in loads:
            cp.wait()
        small_swaps = []
        for gi, (_, _, members) in enumerate(_SMALL_GROUPS):
            s_own[gi][...] = jnp.zeros_like(s_own[gi])
            for name, r0 in members:
                r, n = _small_shape(name)
                s_own[gi][r0:r0 + r, 0:n] = stage[name][...]
            small_swaps.append(remote(s_own[gi], s_sib[gi], sibling))
            small_swaps[gi].start()
        order = sorted(late, key=lambda i: halves[i][0] * halves[i][1])
        own_loads, big_swaps = {}, {}
        for i in order:
            hr = halves[i][0]
            own_loads[i], big_swaps[i] = [], []
            for j in range(N_CHIPS):
                mine = big_refs[i].at[pl.ds(j * 2 * hr + c * hr, hr), :]
                theirs = big_refs[i].at[pl.ds(j * 2 * hr + (1 - c) * hr, hr), :]
                sem = local_sems.at[block_sem0 + N_CHIPS * late.index(i) + j]
                own_loads[i].append(pltpu.make_async_copy(mine, ga[i].at[j], sem))
                own_loads[i][j].start()
                big_swaps[i].append(remote(theirs, gb[i].at[j], sibling))
                big_swaps[i][j].start()
        early_loads = {}
        for e, i in enumerate(early):
            hr = halves[i][0]
            mine = big_refs[i].at[pl.ds(me * 2 * hr + c * hr, hr), :]
            early_loads[i] = [pltpu.make_async_copy(mine, own_e[i], local_sems.at[early_sem0 + 2 * e]),
                              pltpu.make_async_copy(land_refs[i], land_e[i], local_sems.at[early_sem0 + 2 * e + 1])]
            for cp in early_loads[i]:
                cp.start()
        small_sends = []
        for gi in range(n_g):
            small_swaps[gi].wait_recv()
            s_pair[gi][...] = s_own[gi][...] + s_sib[gi][...]
            small_sends.append([remote(s_pair[gi], s_chips[gi].at[k], (*chip, c)) for k, chip in enumerate(other_chips)])
            for cp in small_sends[gi]:
                cp.start()

        def pair_sum(i, j):
            return ga[i][j] + gb[i][j]

        big_sends = {}
        for i in order:
            for j in range(N_CHIPS):
                own_loads[i][j].wait()
                big_swaps[i][j].wait_recv()
            big_sends[i] = []
            for k, chip in enumerate(other_chips):
                send_b[i][k] = pair_sum(i, 2 * chip[0] + chip[1]).astype(BF16)
                big_sends[i].append(remote(send_b[i].at[k], recv_b[i].at[k], (*chip, c)))
                big_sends[i][k].start()
        last_swaps, keeps = {}, {}
        for i in early + order:
            hr = halves[i][0]
            if i in landed:
                for cp in early_loads[i]:
                    cp.wait()
                total = own_e[i][...]
                for k in range(len(_RELATIONS)):
                    total = total + land_e[i][k].astype(F32)
                pme[i][...] = total
            else:
                for k in range(3):
                    big_sends[i][k].wait_recv()
                pme[i][...] = ((pair_sum(i, me) + recv_b[i][0].astype(F32)) + recv_b[i][1].astype(F32)) + recv_b[i][2].astype(F32)
            o, = [o for o, group in enumerate(outputs) if i in group]
            first_col = sum(halves[j][1] for j in outputs[o][:outputs[o].index(i)])
            mine = out_refs[o].at[pl.ds(c * hr, hr), pl.ds(first_col, halves[i][1])]
            keeps[i] = pltpu.make_async_copy(pme[i], mine, local_sems.at[i])
            keeps[i].start()
            last_swaps[i] = remote(pme[i], mine, sibling)
            last_swaps[i].start()

        for gi, (_, _, members) in enumerate(_SMALL_GROUPS):
            for k in range(3):
                small_sends[gi][k].wait_recv()
            total = None
            for j in range(N_CHIPS):
                rel = jnp.bitwise_xor(j, me)
                term = jnp.where(rel == 0, s_pair[gi][...], jnp.where(
                    rel == 2, s_chips[gi][0], jnp.where(rel == 1, s_chips[gi][1], s_chips[gi][2])))
                total = term if total is None else total + term
            s_sib[gi][...] = total
            for name, r0 in members:
                r, n = _small_shape(name)
                stage[name][...] = s_sib[gi][r0:r0 + r, 0:n]
        my_index = 4 * x + 2 * y + c
        for cp in landed_loads:
            cp.wait()
        total = None
        for d in range(2 * N_CHIPS):
            rel = jnp.bitwise_xor(d, my_index)
            term = own_s[...]
            for k in range(len(_RELATIONS)):
                term = jnp.where(rel == k + 1, land_s[k], term)
            total = term.astype(F32) if total is None else total + term.astype(F32)
        for a, name in enumerate(_SMALL_EARLY):
            stage[name][...] = total[:, a * SSM_STATE:(a + 1) * SSM_STATE]
        stores = [pltpu.make_async_copy(stage[name], small_out_refs[name], local_sems.at[small_sem0 + a])
                  for a, name in enumerate(names)]
        for cp in stores:
            cp.start()

        for i in range(n_t):
            last_swaps[i].wait_recv()
            keeps[i].wait()
        for cp in stores:
            cp.wait()
        groups = list(big_swaps.values()) + small_sends + list(big_sends.values())
        for cp in small_swaps + [cp for group in groups for cp in group] + list(last_swaps.values()):
            cp.wait_send()

    any_spec = pl.BlockSpec(memory_space=pl.ANY)
    small_shapes = [_sds(_small_shape(n)) for n in names]
    group_shapes = [shape for _, shape, _ in _SMALL_GROUPS]
    vmem = lambda which, dtype, lead=(): [pltpu.VMEM(lead + halves[i], dtype) for i in which]
    outs = _call(
        body, name="exchange_grads",
        in_specs=[any_spec] * (n_t + len(sent) + len(early) + 2),
        out_specs=[any_spec] * (len(outputs) + len(names)),
        out_shape=[_sds((big[group[0]].shape[0] // N_CHIPS, sum(big[i].shape[1] for i in group))) for group in outputs]
        + small_shapes,
        scratch_shapes=(vmem(late, F32, (N_CHIPS,)) + vmem(late, F32, (N_CHIPS,)) + vmem(range(n_t), F32)
                        + vmem(late, BF16, (3,)) + vmem(late, BF16, (3,))
                        + vmem(early, F32)
                        + [pltpu.VMEM((len(_RELATIONS),) + halves[i], landed[i].dtype) for i in early]
                        + [pltpu.VMEM(own_bc.shape, own_bc.dtype), pltpu.VMEM(landed_bc.shape, landed_bc.dtype)]
                        + [pltpu.VMEM(s, F32) for s in group_shapes] * 2 + [pltpu.VMEM((3,) + s, F32) for s in group_shapes]
                        + [pltpu.VMEM(s, F32) for s in group_shapes]
                        + [pltpu.VMEM(_small_shape(n), F32) for n in names]
                        + [pltpu.SemaphoreType.DMA((n_sems,)), pltpu.SemaphoreType.DMA((n_sems,)),
                           pltpu.SemaphoreType.DMA((landed_sem0 + 2,))]),
        compiler_params=_params(48),
    )(*big, *[small[n] for n in sent], *[landed[i] for i in early], own_bc, landed_bc)
    return list(outs[:len(outputs)]), dict(zip(names, outs[len(outputs):]))


def _adamw_update(w, g, m, v):
    m = ADAM_B1 * m + (1.0 - ADAM_B1) * g
    v = ADAM_B2 * v + (1.0 - ADAM_B2) * (g * g)
    m_hat = m / (1.0 - ADAM_B1 ** ADAM_STEP)
    v_hat = v / (1.0 - ADAM_B2 ** ADAM_STEP)
    return -ADAM_LR * (m_hat / (jnp.sqrt(v_hat) + ADAM_EPS) + ADAM_WD * w), m, v


def _adamw(w, g, m, v, grid, name):
    n_t = len(w)

    def body(*refs):
        ins, outs = refs[:4 * n_t], refs[4 * n_t:]
        for i in range(n_t):
            w_, g_, m_, v_ = [ins[a * n_t + i][...] for a in range(4)]
            vals = (g_,) + _adamw_update(w_, g_, m_, v_)
            for a in range(4):
                outs[a * n_t + i][...] = vals[a]

    specs = [pl.BlockSpec((a.shape[0] // grid, a.shape[1]), lambda i: (i, 0)) for a in w]
    shapes = [_sds(a.shape) for a in w]
    outs = _call(
        body, name=name, grid=(grid,), in_specs=specs * 4, out_specs=specs * 4, out_shape=shapes * 4,
        compiler_params=_params(40, ("arbitrary",)),
    )(*w, *g, *m, *v)
    return [outs[a * n_t:(a + 1) * n_t] for a in range(4)]


def kernel(x, p, pre_norm_g, w_in, ssm_lam_re, ssm_lam_im, ssm_log_step, ssm_b_re, ssm_b_im, ssm_c_re, ssm_c_im, ssm_d, ssm_w_glu, ssm_b_glu, attn_sinks, w_out, post_norm_g, pl_w_proj, pl_w_gate, pl_b_gate, loss_target, m_pre_norm_g, m_w_in, m_ssm_lam_re, m_ssm_lam_im, m_ssm_log_step, m_ssm_b_re, m_ssm_b_im, m_ssm_c_re, m_ssm_c_im, m_ssm_d, m_ssm_w_glu, m_ssm_b_glu, m_attn_sinks, m_w_out, m_post_norm_g, m_pl_w_proj, m_pl_w_gate, m_pl_b_gate, v_pre_norm_g, v_w_in, v_ssm_lam_re, v_ssm_lam_im, v_ssm_log_step, v_ssm_b_re, v_ssm_b_im, v_ssm_c_re, v_ssm_c_im, v_ssm_d, v_ssm_w_glu, v_ssm_b_glu, v_attn_sinks, v_w_out, v_post_norm_g, v_pl_w_proj, v_pl_w_gate, v_pl_b_gate):
    weights = dict(pre_norm_g=pre_norm_g, w_in=w_in, ssm_lam_re=ssm_lam_re, ssm_lam_im=ssm_lam_im,
                   ssm_log_step=ssm_log_step, ssm_b_re=ssm_b_re, ssm_b_im=ssm_b_im, ssm_c_re=ssm_c_re,
                   ssm_c_im=ssm_c_im, ssm_d=ssm_d, ssm_w_glu=ssm_w_glu, ssm_b_glu=ssm_b_glu, attn_sinks=attn_sinks,
                   w_out=w_out, post_norm_g=post_norm_g, pl_w_proj=pl_w_proj, pl_w_gate=pl_w_gate, pl_b_gate=pl_b_gate)
    m_in = dict(pre_norm_g=m_pre_norm_g, w_in=m_w_in, ssm_lam_re=m_ssm_lam_re, ssm_lam_im=m_ssm_lam_im,
                ssm_log_step=m_ssm_log_step, ssm_b_re=m_ssm_b_re, ssm_b_im=m_ssm_b_im, ssm_c_re=m_ssm_c_re,
                ssm_c_im=m_ssm_c_im, ssm_d=m_ssm_d, ssm_w_glu=m_ssm_w_glu, ssm_b_glu=m_ssm_b_glu,
                attn_sinks=m_attn_sinks, w_out=m_w_out, post_norm_g=m_post_norm_g, pl_w_proj=m_pl_w_proj,
                pl_w_gate=m_pl_w_gate, pl_b_gate=m_pl_b_gate)
    v_in = dict(pre_norm_g=v_pre_norm_g, w_in=v_w_in, ssm_lam_re=v_ssm_lam_re, ssm_lam_im=v_ssm_lam_im,
                ssm_log_step=v_ssm_log_step, ssm_b_re=v_ssm_b_re, ssm_b_im=v_ssm_b_im, ssm_c_re=v_ssm_c_re,
                ssm_c_im=v_ssm_c_im, ssm_d=v_ssm_d, ssm_w_glu=v_ssm_w_glu, ssm_b_glu=v_ssm_b_glu,
                attn_sinks=v_attn_sinks, w_out=v_w_out, post_norm_g=v_post_norm_g, pl_w_proj=v_pl_w_proj,
                pl_w_gate=v_pl_w_gate, pl_b_gate=v_pl_b_gate)

    def two_d(tree):
        return {k: _to_kernel_form(k, a) for k, a in tree.items()}

    w2, m2, v2 = two_d(weights), two_d(m_in), two_d(v_in)

    (w_in_full,) = _gather_weights_beside([w2["w_in"].astype(BF16)], "gather_w_in_beside", 4)
    s5_params = tuple(w2[n] for n in ("ssm_lam_re", "ssm_lam_im", "ssm_log_step", "ssm_b_re", "ssm_b_im", "ssm_c_re",
                                      "ssm_c_im"))
    s5_operands = _s5_params_fwd(*s5_params)
    hn = _pre_norm(x.reshape(-1, D_MODEL), w2["pre_norm_g"])
    behind = s5_operands[0][0, 0] * 0.0 + hn[0, 0].astype(F32) * 0.0
    rest = _gather_weights_beside([(w2[n] + behind).astype(BF16) for n in _BIG[1:]], "gather_weights_beside", 1)
    full = dict(zip(_BIG, [w_in_full] + rest))
    mats = ("w_in_early", "w_in_late") + _BIG[1:]
    landed, bc = {}, {}

    def send_tail_grads(ready):
        sent_early = ("w_out", "pl_w_gate", "pl_w_proj")
        landed.update(zip([mats.index(n) for n in sent_early],
                          _scatter_beside([ready[n] for n in sent_early], "scatter_beside", 2)))
        return landed[mats.index(sent_early[-1])]

    def send_bc(d_bc):
        bc["own"] = d_bc
        bc["landed"] = _broadcast_beside([d_bc])[0]
        return d_bc, bc["landed"]

    grad_x, loss, grads = _local_step(
        x, hn, p, loss_target, w2["pre_norm_g"], full["w_in"], s5_params, s5_operands, w2["ssm_d"], full["ssm_w_glu"], w2["ssm_b_glu"],
        w2["attn_sinks"], full["w_out"], w2["post_norm_g"], full["pl_w_proj"], full["pl_w_gate"], w2["pl_b_gate"], send_tail_grads, send_bc)

    landed[0] = _scatter_beside([grads["w_in_early_bf16"]], "scatter_w_in_beside", 5)[0]
    sent_here = {**{n: grads[n] for n in _SMALL if n not in _SMALL_EARLY}, "loss": loss}
    halves_of_w_in = ((0, 1),) + tuple((i,) for i in range(2, len(mats)))
    g_big, g_small = _exchange_grads([grads[n] for n in mats], halves_of_w_in, sent_here, landed, bc["own"], bc["landed"])
    g_big = dict(zip(_BIG, g_big))
    total_loss = g_small.pop("loss")

    big_out = _adamw([w2[n] for n in _BIG], [g_big[n] for n in _BIG], [m2[n] for n in _BIG], [v2[n] for n in _BIG],
                     8, "adamw_matrices")
    small_names = tuple(_SMALL)
    small_out = _adamw([w2[n] for n in small_names], [g_small[n] for n in small_names], [m2[n] for n in small_names],
                       [v2[n] for n in small_names], 1, "adamw_small")

    results = [{**dict(zip(_BIG, big_part)), **dict(zip(small_names, small_part))}
               for big_part, small_part in zip(big_out, small_out)]
    flat = [_from_kernel_form(name, r[name], weights[name].shape) for r in results for name in _WEIGHT_ORDER]
    return (total_loss.reshape(()), grad_x, *flat)
```

```python
import math

import jax
import jax.numpy as jnp
from jax import lax
from jax.experimental import pallas as pl
from jax.experimental.pallas import tpu as pltpu
from jax.experimental.pallas import tpu_sc as plsc

F32 = jnp.float32
BF16 = jnp.bfloat16

D_MODEL = 1024
D_SSM = 512
D_ATTN = 512
SSM_GROUPS = 32
SSM_GROUP_CH = 16
SSM_STATE = 64
SSM_LANES = SSM_GROUPS * SSM_STATE
HEAD_DIM = 64
N_HEADS = 8
KV_HEADS = 2
Q_PER_KV = 4
WINDOW = 128
BLOCK = 128
D_PLE = 256
D_IN = 2304
EPS = 1e-6
ATTN_SCALE = 1.0 / math.sqrt(HEAD_DIM)

ADAM_LR = 0.001
ADAM_B1 = 0.9
ADAM_B2 = 0.999
ADAM_EPS = 1e-08
ADAM_WD = 0.01
ADAM_STEP = 10

N_CHIPS = 4
LANES = 128
SCAN_CHUNKS = 8
SCAN_TILE_STEPS = 32
SCAN_LANE_CHUNK = 512
MIB = 2 ** 20
MESH = pl.DeviceIdType.MESH


def _dot(a, b):
    return jnp.dot(a, b, preferred_element_type=F32)


def _dot_nt(a, b):
    return lax.dot_general(a, b, (((1,), (1,)), ((), ())), preferred_element_type=F32)


def _dot_tn(a, b):
    return lax.dot_general(a, b, (((0,), (0,)), ((), ())), preferred_element_type=F32)


def _params(vmem_mib, semantics=None):
    kw = dict(vmem_limit_bytes=vmem_mib * MIB)
    if semantics is not None:
        kw["dimension_semantics"] = semantics
    return pltpu.CompilerParams(**kw)


def _full(shape):
    nd = len(shape)
    return pl.BlockSpec(shape, lambda *_: (0,) * nd, pipeline_mode=pl.Buffered(1))


def _rows(tm, width):
    return pl.BlockSpec((tm, width), lambda i: (i, 0))


def _sds(shape, dtype=F32):
    return pltpu.HBM(shape, dtype)


def _call(body, **kw):
    fn = pl.pallas_call(body, **kw)
    return lambda *args: fn(*[pltpu.with_memory_space_constraint(a, pltpu.HBM) for a in args])


def _silu(z):
    return z * jax.nn.sigmoid(z)


def _pre_norm(x2d, g1):
    rows = x2d.shape[0]
    tm = 512

    def body(x_ref, g_ref, hn_ref):
        x = x_ref[...]
        r = lax.rsqrt(jnp.mean(x * x, axis=-1, keepdims=True) + EPS)
        hn_ref[...] = (x * r * g_ref[...]).astype(BF16)

    return _call(
        body, name="pre_norm", grid=(rows // tm,), in_specs=[_rows(tm, D_MODEL), _full((1, D_MODEL))],
        out_specs=_rows(tm, D_MODEL), out_shape=_sds((rows, D_MODEL), BF16), compiler_params=_params(32, ("arbitrary",)),
    )(x2d, g1)


def _in_proj(hn, w_in_t, n_seq, seq):
    rows = hn.shape[0]
    tm = 1024
    slab, steps, _, _ = _scan_geometry(n_seq, seq)

    def body(hn_ref, w_ref, *out_refs):
        u_parts, (zs_ref, q_ref, k_ref, v_ref, za_ref) = out_refs[:_SCAN_PARTS], out_refs[_SCAN_PARTS:]
        whole = _dot_nt(hn_ref[...], w_ref[...])

        def proj(a, b):
            return whole[:, a:b]

        _store_chunks(u_parts, pl.program_id(0) * (tm // steps), proj(0, 512), steps, slab)
        zs_ref[...] = proj(512, 1024)
        q_ref[...] = (proj(1024, 1536) * ATTN_SCALE).astype(BF16)
        k_ref[...] = proj(1536, 1664).astype(BF16)
        v_ref[...] = proj(1664, 1792).astype(BF16)
        za_ref[...] = proj(1792, 2304)

    *u_parts, zs, q, k, v, za = _call(
        body, name="in_proj", grid=(rows // tm,),
        in_specs=[_rows(tm, D_MODEL), _full((D_IN, D_MODEL))],
        out_specs=_whole_parts(rows) + [_rows(tm, 512), _rows(tm, 512), _rows(tm, 128), _rows(tm, 128), _rows(tm, 512)],
        out_shape=_part_shapes(rows) + [_sds((rows, 512)), _sds((rows, 512), BF16), _sds((rows, 128), BF16),
                                        _sds((rows, 128), BF16), _sds((rows, 512))],
        compiler_params=_params(48, ("arbitrary",)),
    )(hn, w_in_t)
    return u_parts, zs, q, k, v, za


_EARLY_COLS = D_MODEL // 2


def _in_proj_bwd_early(hn, du_parts, dzs, dq, dk, dv, dza, runs_after, n_seq, seq):
    rows = hn.shape[0]
    tm = 1024
    slab, steps, _, _ = _scan_geometry(n_seq, seq)

    def body(hn_ref, *refs):
        du_parts, (dzs_ref, dq_ref, dk_ref, dv_ref, dza_ref, _, dproj_ref, dw_ref, dwb_ref) = refs[:_SCAN_PARTS], refs[_SCAN_PARTS:]
        i = pl.program_id(0)

        @pl.when(i == 0)
        def _():
            dw_ref[...] = jnp.zeros_like(dw_ref)

        du = _load_chunks(du_parts, i * (tm // steps), tm // steps, steps, slab)
        d_proj = jnp.concatenate([du.astype(BF16), dzs_ref[...], dq_ref[...], dk_ref[...], dv_ref[...], dza_ref[...]],
                                 axis=1)
        dproj_ref[...] = d_proj
        dw_ref[...] += _dot_tn(d_proj, hn_ref[...])

        @pl.when(i == rows // tm - 1)
        def _():
            dwb_ref[...] = dw_ref[...].astype(BF16)

    return _call(
        body, name="in_proj_bwd_early", grid=(rows // tm,),
        in_specs=[_rows(tm, _EARLY_COLS)] + _whole_parts(rows)
        + [_rows(tm, 512), _rows(tm, 512), _rows(tm, 128), _rows(tm, 128), _rows(tm, 512),
           pl.BlockSpec(memory_space=pl.ANY)],
        out_specs=[_rows(tm, D_IN), _full((D_IN, _EARLY_COLS)), _full((D_IN, _EARLY_COLS))],
        out_shape=[_sds((rows, D_IN), BF16), _sds((D_IN, _EARLY_COLS)), _sds((D_IN, _EARLY_COLS), BF16)],
        compiler_params=_params(48, ("arbitrary",)),
    )(hn, *du_parts, dzs, dq, dk, dv, dza, runs_after)


def _in_proj_bwd(x2d, dh1, g1, w_in_t, d_proj, runs_after):
    rows = x2d.shape[0]
    tm = 512

    def body(x_ref, dh1_ref, g_ref, w_ref, dproj_ref, _, gx_ref, dw_ref, dg_ref):
        @pl.when(pl.program_id(0) == 0)
        def _():
            dw_ref[...] = jnp.zeros_like(dw_ref)
            dg_ref[...] = jnp.zeros_like(dg_ref)

        x = x_ref[...]
        g = g_ref[...]
        r = lax.rsqrt(jnp.mean(x * x, axis=-1, keepdims=True) + EPS)
        xr = x * r
        hn = (xr[:, _EARLY_COLS:] * g[:, _EARLY_COLS:]).astype(BF16)
        d_proj = dproj_ref[...]
        dhn = _dot(d_proj, w_ref[...])
        dw_ref[...] += _dot_tn(d_proj, hn)
        dg_ref[...] += jnp.sum(dhn * xr, axis=0, keepdims=True)
        a_ = dhn * g
        gx_ref[...] = dh1_ref[...] + r * a_ - xr * (r * jnp.mean(a_ * xr, axis=-1, keepdims=True))

    late_cols = D_MODEL - _EARLY_COLS
    return _call(
        body, name="in_proj_bwd", grid=(rows // tm,),
        in_specs=[_rows(tm, D_MODEL), _rows(tm, D_MODEL), _full((1, D_MODEL)), _full((D_IN, D_MODEL)), _rows(tm, D_IN),
                  pl.BlockSpec(memory_space=pl.ANY)],
        out_specs=[_rows(tm, D_MODEL), _full((D_IN, late_cols)), _full((1, D_MODEL))],
        out_shape=[_sds((rows, D_MODEL)), _sds((D_IN, late_cols)), _sds((1, D_MODEL))],
        compiler_params=_params(52, ("arbitrary",)),
    )(x2d, dh1, g1, w_in_t, d_proj, runs_after)


def _iota(shape, axis):
    return lax.broadcasted_iota(jnp.int32, shape, axis)


def _sum_of_thirds(f, a):
    hi = a.astype(BF16)
    rest = a - hi.astype(F32)
    mid = rest.astype(BF16)
    low = (rest - mid.astype(F32)).astype(BF16)
    return (f(hi) + f(mid)) + f(low)


@jax.custom_vjp
def _pick_rows(e, a):
    return _sum_of_thirds(lambda part: _dot(e, part), a)


def _pick_rows_fwd(e, a):
    return _pick_rows(e, a), e


def _pick_rows_bwd(e, ct):
    return jnp.zeros_like(e), _sum_of_thirds(lambda part: _dot_tn(e, part), ct)


_pick_rows.defvjp(_pick_rows_fwd, _pick_rows_bwd)


@jax.custom_vjp
def _pick_cols(a, e):
    return _sum_of_thirds(lambda part: _dot(part, e), a)


def _pick_cols_fwd(a, e):
    return _pick_cols(a, e), e


def _pick_cols_bwd(e, ct):
    return _sum_of_thirds(lambda part: _dot_nt(part, e), ct), jnp.zeros_like(e)


_pick_cols.defvjp(_pick_cols_fwd, _pick_cols_bwd)


_HALF_GROUPS = SSM_GROUPS // 2
_N_SHIFT = SSM_STATE.bit_length() - 1
_P_SHIFT = SSM_GROUP_CH.bit_length() - 1


def _s5_operands(lam_re, lam_im, log_step, b_re, b_im, c_re, c_im):
    g, n, p = SSM_GROUPS, SSM_STATE, SSM_GROUP_CH
    gn, gp, hn_, hp = g * n, g * p, _HALF_GROUPS * n, _HALF_GROUPS * p
    eye_g = _iota((g, g), 0) == _iota((g, g), 1)
    step = jnp.sum(jnp.where(eye_g, jnp.exp(log_step), 0.0), axis=1, keepdims=True)
    a_re = lam_re * step
    a_im = lam_im * step
    mag = jnp.exp(a_re)
    lbar_re = mag * jnp.cos(a_im)
    lbar_im = mag * jnp.sin(a_im)
    n_re = lbar_re - 1.0
    den = lam_re * lam_re + lam_im * lam_im
    f_re = (n_re * lam_re + lbar_im * lam_im) / den
    f_im = (lbar_im * lam_re - n_re * lam_im) / den

    spread_n = (_iota((n, gn), 0) == (_iota((n, gn), 1) & (n - 1))).astype(BF16)
    own_g = _iota((g, gn), 0) == (_iota((g, gn), 1) >> _N_SHIFT)

    def to_row(a):
        return jnp.sum(jnp.where(own_g, _pick_cols(a, spread_n), 0.0), axis=0, keepdims=True)

    per_group = ((_iota((gp, g), 0) >> _P_SHIFT) == _iota((gp, g), 1)).astype(BF16)
    fx_re, fx_im = _pick_rows(per_group, f_re), _pick_rows(per_group, f_im)
    bbar_re = fx_re * b_re - fx_im * b_im
    bbar_im = fx_re * b_im + fx_im * b_re

    tile_n = (_iota((n, hn_), 0) == (_iota((n, hn_), 1) & (n - 1))).astype(BF16)
    same_group = (_iota((hp, hn_), 0) >> _P_SHIFT) == (_iota((hp, hn_), 1) >> _N_SHIFT)

    def embed(a, hf):
        return jnp.where(same_group, _pick_cols(a[hf * hp:(hf + 1) * hp], tile_n), 0.0)

    return (to_row(lbar_re), to_row(lbar_im), embed(bbar_re, 0), embed(bbar_re, 1), embed(bbar_im, 0),
            embed(bbar_im, 1), embed(c_re, 0), embed(c_re, 1), embed(c_im, 0), embed(c_im, 1))


_S5_PARAM_SHAPES = ((SSM_GROUPS, SSM_STATE), (SSM_GROUPS, SSM_STATE), (1, SSM_GROUPS),
                    (D_SSM, SSM_STATE), (D_SSM, SSM_STATE), (D_SSM, SSM_STATE), (D_SSM, SSM_STATE))
_CM_SHAPE = (2, _HALF_GROUPS * SSM_GROUP_CH, _HALF_GROUPS * SSM_STATE)
_S5_OPERAND_SHAPES = ((1, SSM_LANES), (1, SSM_LANES), _CM_SHAPE, _CM_SHAPE, _CM_SHAPE, _CM_SHAPE)


def _s5_params_fwd(*params):
    def body(*refs):
        ins, (lre_ref, lim_ref, btre_ref, btim_ref, cmre_ref, cmim_ref) = refs[:7], refs[7:]
        vals = _s5_operands(*[r[...] for r in ins])
        lre_ref[...] = vals[0]
        lim_ref[...] = vals[1]
        for ref, pair in zip((btre_ref, btim_ref, cmre_ref, cmim_ref), (vals[2:4], vals[4:6], vals[6:8], vals[8:10])):
            ref[0] = pair[0].astype(BF16)
            ref[1] = pair[1].astype(BF16)

    dtypes = (F32, F32, BF16, BF16, BF16, BF16)
    return _call(
        body, name="s5_params_fwd",
        in_specs=[_full(s) for s in _S5_PARAM_SHAPES], out_specs=[_full(s) for s in _S5_OPERAND_SHAPES],
        out_shape=[_sds(s, d) for s, d in zip(_S5_OPERAND_SHAPES, dtypes)], compiler_params=_params(32),
    )(*params)


_BC_SIDE_BY_SIDE = (D_SSM, 4 * SSM_STATE)


def _s5_params_bwd(params, cotangents, runs_after):
    def body(*refs):
        ins, (dlre, dlim, dbtre, dbtim, dcmre, dcmim), outs = refs[:7], refs[7:13], refs[14:]
        _, vjp = jax.vjp(_s5_operands, *[r[...] for r in ins])
        cts = (dlre[...], dlim[...], dbtre[0], dbtre[1], dbtim[0], dbtim[1], dcmre[0], dcmre[1], dcmim[0], dcmim[1])
        grads = vjp(cts)
        for ref, val in zip(outs[:3], grads[:3]):
            ref[...] = val
        outs[3][...] = jnp.concatenate(grads[3:], axis=1).astype(BF16)

    out_shapes = _S5_PARAM_SHAPES[:3] + (_BC_SIDE_BY_SIDE,)
    return _call(
        body, name="s5_params_bwd",
        in_specs=[_full(s) for s in _S5_PARAM_SHAPES + _S5_OPERAND_SHAPES] + [pl.BlockSpec(memory_space=pl.ANY)],
        out_specs=[_full(s) for s in out_shapes],
        out_shape=[_sds(s, d) for s, d in zip(out_shapes, (F32, F32, F32, BF16))], compiler_params=_params(48),
    )(*params, *cotangents, runs_after)


def _scan_geometry(n_seq, seq):
    slab = n_seq * SCAN_CHUNKS
    steps = seq // SCAN_CHUNKS
    tile_rows = slab * SCAN_TILE_STEPS
    n_tiles = steps // SCAN_TILE_STEPS
    return slab, steps, tile_rows, n_tiles


_SCAN_PARTS = D_SSM // LANES


def _whole_parts(rows):
    return [_full((rows, LANES))] * _SCAN_PARTS


def _part_shapes(rows):
    return [_sds((rows, LANES))] * _SCAN_PARTS


def _load_chunks(parts, first_chunk, n_chunks, steps, slab):
    return jnp.concatenate([
        jnp.concatenate([ref[pl.ds(first_chunk + q, steps, stride=slab), :] for ref in parts], axis=1)
        for q in range(n_chunks)], axis=0)


def _store_chunks(parts, first_chunk, value, steps, slab):
    for q in range(value.shape[0] // steps):
        for j, ref in enumerate(parts):
            ref[pl.ds(first_chunk + q, steps, stride=slab), :] = value[q * steps:(q + 1) * steps,
                                                                     j * LANES:(j + 1) * LANES]


def _join_parts(parts):
    return jnp.concatenate([ref[...] for ref in parts], axis=1)


def _split_parts(parts, value):
    for j, ref in enumerate(parts):
        ref[...] = value[:, j * LANES:(j + 1) * LANES]


def _complex_power(re, im, n):
    out = None
    while n:
        if n & 1:
            out = (re, im) if out is None else (out[0] * re - out[1] * im, out[0] * im + out[1] * re)
        n >>= 1
        if n:
            re, im = re * re - im * im, 2.0 * re * im
    return out


def _chunk_carry(sum_re, sum_im, carry_re, carry_im, a_re, a_im, n_seq, reverse):
    carry_re[...] = jnp.zeros_like(carry_re)
    carry_im[...] = jnp.zeros_like(carry_im)
    for s in range(n_seq):
        order = range(SCAN_CHUNKS - 2, -1, -1) if reverse else range(1, SCAN_CHUNKS)
        for c in order:
            r = s * SCAN_CHUNKS + c
            p = r + 1 if reverse else r - 1
            p_re, p_im = carry_re[p:p + 1, :], carry_im[p:p + 1, :]
            carry_re[r:r + 1, :] = a_re * p_re - a_im * p_im + sum_re[p:p + 1, :]
            carry_im[r:r + 1, :] = a_re * p_im + a_im * p_re + sum_im[p:p + 1, :]


def _s5_scan_fwd(u_parts, bt_re, bt_im, cm_re, cm_im, lbar_re, lbar_im, d_row, n_seq, seq):
    slab, steps, tile_rows, n_tiles = _scan_geometry(n_seq, seq)
    rows = u_parts[0].shape[0]

    def body(*refs):
        u_refs, refs = refs[:_SCAN_PARTS], refs[_SCAN_PARTS:]
        (bre_ref, bim_ref, cre_ref, cim_ref, lre_ref, lim_ref, d_ref), refs = refs[:7], refs[7:]
        y_refs, (hre_ref, him_ref, st_re, st_im, h0_re, h0_im, buf_re, buf_im) = refs[:_SCAN_PARTS], refs[_SCAN_PARTS:]
        second = pl.program_id(0) == 1
        i = pl.program_id(1)

        @pl.when(jnp.logical_and(i == 0, jnp.logical_not(second)))
        def _():
            st_re[...] = jnp.zeros_like(st_re)
            st_im[...] = jnp.zeros_like(st_im)

        u = _join_parts(u_refs)
        ub = u.astype(BF16)
        for hf in range(2):
            cols = slice(hf * 1024, (hf + 1) * 1024)
            buf_re[:, cols] = _dot(ub[:, hf * 256:(hf + 1) * 256], bre_ref[hf])
            buf_im[:, cols] = _dot(ub[:, hf * 256:(hf + 1) * 256], bim_ref[hf])

        for lc in range(SSM_LANES // SCAN_LANE_CHUNK):
            cols = slice(lc * SCAN_LANE_CHUNK, (lc + 1) * SCAN_LANE_CHUNK)
            l_re = jnp.broadcast_to(lre_ref[:, cols], (slab, SCAN_LANE_CHUNK))
            l_im = jnp.broadcast_to(lim_ref[:, cols], (slab, SCAN_LANE_CHUNK))

            def scan_tile(keep_states):
                def step(t, carry):
                    s_re, s_im = carry
                    r0 = pl.multiple_of(t * slab, slab)
                    n_re = l_re * s_re - l_im * s_im + buf_re[pl.ds(r0, slab), cols]
                    n_im = l_re * s_im + l_im * s_re + buf_im[pl.ds(r0, slab), cols]
                    if keep_states:
                        buf_re[pl.ds(r0, slab), cols] = n_re
                        buf_im[pl.ds(r0, slab), cols] = n_im
                    return n_re, n_im

                s_re, s_im = lax.fori_loop(0, SCAN_TILE_STEPS, step, (st_re[:, cols], st_im[:, cols]), unroll=True)
                st_re[:, cols] = s_re
                st_im[:, cols] = s_im

            pl.when(jnp.logical_not(second))(lambda: scan_tile(False))
            pl.when(second)(lambda: scan_tile(True))

        @pl.when(jnp.logical_and(i == n_tiles - 1, jnp.logical_not(second)))
        def _():
            a_re, a_im = _complex_power(lre_ref[...], lim_ref[...], steps)
            _chunk_carry(st_re, st_im, h0_re, h0_im, a_re, a_im, n_seq, reverse=False)
            st_re[...] = h0_re[...]
            st_im[...] = h0_im[...]

        @pl.when(second)
        def _():
            h_re = buf_re[...].astype(BF16)
            h_im = buf_im[...].astype(BF16)
            hre_ref[...] = h_re
            him_ref[...] = h_im
            for hf in range(2):
                cols = slice(hf * 1024, (hf + 1) * 1024)
                ycols = slice(hf * 256, (hf + 1) * 256)
                y_half = (_dot_nt(h_re[:, cols], cre_ref[hf]) - _dot_nt(h_im[:, cols], cim_ref[hf])
                          + d_ref[:, ycols] * u[:, ycols])
                _split_parts(y_refs[2 * hf:2 * hf + 2], y_half)

    tile = lambda w: pl.BlockSpec((tile_rows, w), lambda p, i: (i, 0))
    out_tile = lambda w: pl.BlockSpec((tile_rows, w), lambda p, i: (i * p, 0))
    cm = _full(_CM_SHAPE)
    outs = _call(
        body, name="s5_scan_fwd", grid=(2, n_tiles),
        in_specs=[tile(LANES)] * _SCAN_PARTS + [cm, cm, cm, cm, _full((1, SSM_LANES)), _full((1, SSM_LANES)),
                                                _full((1, 512))],
        out_specs=[out_tile(LANES)] * _SCAN_PARTS + [out_tile(SSM_LANES), out_tile(SSM_LANES)],
        out_shape=_part_shapes(rows) + [_sds((rows, SSM_LANES), BF16), _sds((rows, SSM_LANES), BF16)],
        scratch_shapes=[pltpu.VMEM((slab, SSM_LANES), F32)] * 4 + [pltpu.VMEM((tile_rows, SSM_LANES), F32)] * 2,
        compiler_params=_params(40, ("arbitrary", "arbitrary")),
    )(*u_parts, bt_re, bt_im, cm_re, cm_im, lbar_re, lbar_im, d_row)
    return outs[:_SCAN_PARTS], outs[_SCAN_PARTS], outs[_SCAN_PARTS + 1]


def _s5_scan_bwd(dy_parts, u_parts, h_re, h_im, bt_re, bt_im, cm_re, cm_im, lbar_re, lbar_im, d_row, n_seq, seq):
    slab, steps, tile_rows, n_tiles = _scan_geometry(n_seq, seq)
    rows = u_parts[0].shape[0]

    def body(*refs):
        dy_refs, u_refs, refs = refs[:_SCAN_PARTS], refs[_SCAN_PARTS:2 * _SCAN_PARTS], refs[2 * _SCAN_PARTS:]
        (hre_ref, him_ref, bre_ref, bim_ref, cre_ref, cim_ref, lre_ref, lim_ref, d_ref), refs = refs[:9], refs[9:]
        du_refs, refs = refs[:_SCAN_PARTS], refs[_SCAN_PARTS:]
        (dbre_ref, dbim_ref, dcre_ref, dcim_ref, dlre_ref, dlim_ref, dd_ref,
         st_re, st_im, g0_re, g0_im, acc_re, acc_im, buf_re, buf_im) = refs
        second = pl.program_id(0) == 1
        i = pl.program_id(1)

        @pl.when(jnp.logical_and(i == 0, jnp.logical_not(second)))
        def _():
            st_re[...] = jnp.zeros_like(st_re)
            st_im[...] = jnp.zeros_like(st_im)
            acc_re[...] = jnp.zeros_like(acc_re)
            acc_im[...] = jnp.zeros_like(acc_im)
            for ref in (dbre_ref, dbim_ref, dcre_ref, dcim_ref, dd_ref):
                ref[...] = jnp.zeros_like(ref)

        dy = _join_parts(dy_refs)
        dyb = dy.astype(BF16)
        for hf in range(2):
            cols = slice(hf * 1024, (hf + 1) * 1024)
            buf_re[:, cols] = _dot(dyb[:, hf * 256:(hf + 1) * 256], cre_ref[hf])
            buf_im[:, cols] = -_dot(dyb[:, hf * 256:(hf + 1) * 256], cim_ref[hf])

        for lc in range(SSM_LANES // SCAN_LANE_CHUNK):
            cols = slice(lc * SCAN_LANE_CHUNK, (lc + 1) * SCAN_LANE_CHUNK)
            l_re = jnp.broadcast_to(lre_ref[:, cols], (slab, SCAN_LANE_CHUNK))
            l_im = jnp.broadcast_to(lim_ref[:, cols], (slab, SCAN_LANE_CHUNK))

            def advance(r0, s_re, s_im):
                n_re = l_re * s_re + l_im * s_im + buf_re[pl.ds(r0, slab), cols]
                n_im = l_re * s_im - l_im * s_re + buf_im[pl.ds(r0, slab), cols]
                buf_re[pl.ds(r0, slab), cols] = n_re
                buf_im[pl.ds(r0, slab), cols] = n_im
                return n_re, n_im

            def row0(k):
                return pl.multiple_of((SCAN_TILE_STEPS - 1 - k) * slab, slab)

            @pl.when(jnp.logical_not(second))
            def _():
                s_re, s_im = lax.fori_loop(0, SCAN_TILE_STEPS, lambda k, s: advance(row0(k), *s),
                                           (st_re[:, cols], st_im[:, cols]), unroll=True)
                st_re[:, cols] = s_re
                st_im[:, cols] = s_im

            @pl.when(second)
            def _():
                def step(k, carry):
                    s_re, s_im, a_re, a_im = carry
                    r0 = row0(k)
                    hr = hre_ref[pl.ds(r0, slab), cols].astype(F32)
                    hi = him_ref[pl.ds(r0, slab), cols].astype(F32)
                    a_re = a_re + s_re * hr + s_im * hi
                    a_im = a_im + s_im * hr - s_re * hi
                    return advance(r0, s_re, s_im) + (a_re, a_im)

                zero = jnp.zeros((slab, SCAN_LANE_CHUNK), F32)
                s_re, s_im, a_re, a_im = lax.fori_loop(
                    0, SCAN_TILE_STEPS, step, (st_re[:, cols], st_im[:, cols], zero, zero), unroll=True)
                st_re[:, cols] = s_re
                st_im[:, cols] = s_im
                acc_re[:, cols] += a_re
                acc_im[:, cols] += a_im

        @pl.when(jnp.logical_and(i == n_tiles - 1, jnp.logical_not(second)))
        def _():
            p_re, p_im = _complex_power(lre_ref[...], lim_ref[...], steps)
            _chunk_carry(st_re, st_im, g0_re, g0_im, p_re, -p_im, n_seq, reverse=True)
            st_re[...] = g0_re[...]
            st_im[...] = g0_im[...]

        @pl.when(second)
        def _():
            u = _join_parts(u_refs)
            ub = u.astype(BF16)
            g_re = buf_re[...].astype(BF16)
            g_im = buf_im[...].astype(BF16)
            dd_ref[...] += jnp.sum(dy * u, axis=0, keepdims=True)
            for hf in range(2):
                cols = slice(hf * 1024, (hf + 1) * 1024)
                ycols = slice(hf * 256, (hf + 1) * 256)
                du_half = (_dot_nt(g_re[:, cols], bre_ref[hf]) + _dot_nt(g_im[:, cols], bim_ref[hf])
                           + d_ref[:, ycols] * dy[:, ycols])
                _split_parts(du_refs[2 * hf:2 * hf + 2], du_half)
                for q4 in range(_HALF_GROUPS // 4):
                    ch = slice(hf * 256 + q4 * 64, hf * 256 + (q4 + 1) * 64)
                    st = slice(hf * 1024 + q4 * 256, hf * 1024 + (q4 + 1) * 256)
                    blk = (hf, slice(q4 * 64, (q4 + 1) * 64), slice(q4 * 256, (q4 + 1) * 256))
                    dbre_ref[blk] += _dot_tn(ub[:, ch], g_re[:, st])
                    dbim_ref[blk] += _dot_tn(ub[:, ch], g_im[:, st])
                    dcre_ref[blk] += _dot_tn(dyb[:, ch], hre_ref[:, st])
                    dcim_ref[blk] -= _dot_tn(dyb[:, ch], him_ref[:, st])

        @pl.when(jnp.logical_and(i == n_tiles - 1, second))
        def _():
            dlre_ref[...] = jnp.sum(acc_re[...], axis=0, keepdims=True)
            dlim_ref[...] = jnp.sum(acc_im[...], axis=0, keepdims=True)

    tile = lambda w: pl.BlockSpec((tile_rows, w), lambda p, i: (n_tiles - 1 - i, 0))
    second_tile = lambda w: pl.BlockSpec((tile_rows, w), lambda p, i: (n_tiles - 1 - i * p, 0))
    cm = _full(_CM_SHAPE)
    row = _full((1, SSM_LANES))
    outs = _call(
        body, name="s5_scan_bwd", grid=(2, n_tiles),
        in_specs=[tile(LANES)] * _SCAN_PARTS + [second_tile(LANES)] * _SCAN_PARTS
        + [second_tile(SSM_LANES), second_tile(SSM_LANES), cm, cm, cm, cm, row, row, _full((1, 512))],
        out_specs=[second_tile(LANES)] * _SCAN_PARTS + [cm, cm, cm, cm, row, row, _full((1, 512))],
        out_shape=(_part_shapes(rows) + [_sds(_CM_SHAPE)] * 4 + [_sds((1, SSM_LANES))] * 2 + [_sds((1, 512))]),
        scratch_shapes=[pltpu.VMEM((slab, SSM_LANES), F32)] * 6 + [pltpu.VMEM((tile_rows, SSM_LANES), F32)] * 2,
        compiler_params=_params(48, ("arbitrary", "arbitrary")),
    )(*dy_parts, *u_parts, h_re, h_im, bt_re, bt_im, cm_re, cm_im, lbar_re, lbar_im, d_row)
    return (outs[:_SCAN_PARTS],) + tuple(outs[_SCAN_PARTS:])


def _glu_gate(gl, a, zs):
    return gl * jax.nn.sigmoid(a) * _silu(zs)


def _glu_fwd(y_parts, zs, w_glu, b_glu, n_seq, seq):
    rows = zs.shape[0]
    tm = 1024
    slab, steps, _, _ = _scan_geometry(n_seq, seq)

    def body(*refs):
        y_refs, (zs_ref, w_ref, b_ref, o_ref) = refs[:_SCAN_PARTS], refs[_SCAN_PARTS:]
        y = _load_chunks(y_refs, pl.program_id(0) * (tm // steps), tm // steps, steps, slab)
        gl = jax.nn.gelu(y)
        a = _dot(gl.astype(BF16), w_ref[...]) + b_ref[...]
        o_ref[...] = _glu_gate(gl, a, zs_ref[...]).astype(BF16)

    return _call(
        body, name="glu_fwd", grid=(rows // tm,),
        in_specs=_whole_parts(rows) + [_rows(tm, 512), _full((512, 512)), _full((1, 512))],
        out_specs=_rows(tm, 512), out_shape=_sds((rows, 512), BF16),
        compiler_params=_params(32, ("arbitrary",)),
    )(*y_parts, zs, w_glu, b_glu)


def _glu_bwd(y_parts, zs, d_out, w_glu, b_glu, n_seq, seq):
    rows = zs.shape[0]
    tm = 512
    slab, steps, _, _ = _scan_geometry(n_seq, seq)

    def body(*refs):
        y_refs, (zs_ref, d_ref, w_ref, b_ref), refs = refs[:_SCAN_PARTS], refs[_SCAN_PARTS:_SCAN_PARTS + 4], refs[_SCAN_PARTS + 4:]
        dy_refs, (dzs_ref, dw_ref, db_ref) = refs[:_SCAN_PARTS], refs[_SCAN_PARTS:]
        first_chunk = pl.program_id(0) * (tm // steps)

        @pl.when(pl.program_id(0) == 0)
        def _():
            dw_ref[...] = jnp.zeros_like(dw_ref)
            db_ref[...] = jnp.zeros_like(db_ref)

        gl, gelu_vjp = jax.vjp(jax.nn.gelu, _load_chunks(y_refs, first_chunk, tm // steps, steps, slab))
        glb = gl.astype(BF16)
        a = _dot(glb, w_ref[...]) + b_ref[...]
        _, gate_vjp = jax.vjp(_glu_gate, gl, a, zs_ref[...])
        d_gl, d_a, d_zs = gate_vjp(d_ref[...])
        dab = d_a.astype(BF16)
        d_gl = d_gl + _dot_nt(dab, w_ref[...])
        _store_chunks(dy_refs, first_chunk, gelu_vjp(d_gl)[0], steps, slab)
        dzs_ref[...] = d_zs.astype(BF16)
        dw_ref[...] += _dot_tn(glb, dab)
        db_ref[...] += jnp.sum(d_a, axis=0, keepdims=True)

    *dy_parts, dzs, dw, db = _call(
        body, name="glu_bwd", grid=(rows // tm,),
        in_specs=_whole_parts(rows) + [_rows(tm, 512), _rows(tm, 512), _full((512, 512)), _full((1, 512))],
        out_specs=_whole_parts(rows) + [_rows(tm, 512), _full((512, 512)), _full((1, 512))],
        out_shape=_part_shapes(rows) + [_sds((rows, 512), BF16), _sds((512, 512)), _sds((1, 512))],
        compiler_params=_params(40, ("arbitrary",)),
    )(*y_parts, zs, d_out, w_glu, b_glu)
    return dy_parts, dzs, dw, db


_GROUP_ROWS = Q_PER_KV * BLOCK
_BLOCK_SHIFT = BLOCK.bit_length() - 1


def _attn_bias(j):
    query = _iota((BLOCK, _GROUP_ROWS), 1)
    dist_cur = (query & (BLOCK - 1)) - _iota((BLOCK, _GROUP_ROWS), 0)
    dist_prev = dist_cur + BLOCK
    head = query >> _BLOCK_SHIFT
    slope = jnp.zeros((BLOCK, _GROUP_ROWS), F32)
    for g in range(Q_PER_KV):
        slope = jnp.where(head == g, 2.0 ** (-(j * Q_PER_KV + g + 1)), slope)
    bias_cur = jnp.where(dist_cur >= 0, -slope * dist_cur.astype(F32), -jnp.inf)
    bias_prev = jnp.where(dist_prev < WINDOW, -slope * dist_prev.astype(F32), -jnp.inf)
    return bias_cur, bias_prev


_ATTN_BIAS_SCRATCH = pltpu.VMEM((KV_HEADS, 2, BLOCK, _GROUP_ROWS), F32)


def _fill_attn_bias(bias_ref):
    @pl.when(jnp.logical_and(pl.program_id(0) == 0, pl.program_id(1) == 0))
    def _():
        for j in range(KV_HEADS):
            bias_ref[j, 0], bias_ref[j, 1] = _attn_bias(j)


def _stack_heads(x, j):
    heads = range(j * Q_PER_KV, (j + 1) * Q_PER_KV)
    return jnp.concatenate([x[:, h * HEAD_DIM:(h + 1) * HEAD_DIM] for h in heads], axis=0)


def _head_rows(x, j):
    heads = range(j * Q_PER_KV, (j + 1) * Q_PER_KV)
    return jnp.concatenate([x[h:h + 1, :] for h in heads], axis=1)


def _sink_row(sk_ref, j):
    heads = range(j * Q_PER_KV, (j + 1) * Q_PER_KV)
    return jnp.concatenate([jnp.broadcast_to(sk_ref[0:1, h:h + 1], (1, BLOCK)) for h in heads], axis=1)


_ATTN_FWD_BLOCKS = 8


def _attn_fwd(q, k, v, za, sinks, n_seq, seq):
    nb = seq // BLOCK
    steps = nb // _ATTN_FWD_BLOCKS
    rows = q.shape[0]

    def body(q_ref, kc_ref, kp_ref, vc_ref, vp_ref, za_ref, sk_ref, o_ref, ao_ref, lse_ref, bias_ref):
        _fill_attn_bias(bias_ref)
        for t in range(_ATTN_FWD_BLOCKS):
            at = slice(t * BLOCK, (t + 1) * BLOCK)
            before = slice((t - 1) * BLOCK, t * BLOCK)
            q_all = q_ref[at, :]
            for j in range(KV_HEADS):
                js = slice(j * HEAD_DIM, (j + 1) * HEAD_DIM)
                bias_c, bias_p = bias_ref[j, 0], bias_ref[j, 1]
                q4 = _stack_heads(q_all, j)
                sc = _dot_nt(kc_ref[at, js], q4) + bias_c
                if t == 0:
                    sp = _dot_nt(kp_ref[:, js], q4) + jnp.where(pl.program_id(1) > 0, bias_p, -jnp.inf)
                    v_prev = vp_ref[:, js]
                else:
                    sp = _dot_nt(kc_ref[before, js], q4) + bias_p
                    v_prev = vc_ref[before, js]
                sink = _sink_row(sk_ref, j)
                m = jnp.maximum(jnp.max(jnp.maximum(sc, sp), axis=0, keepdims=True), sink)
                ec = jnp.exp(sc - m)
                ep = jnp.exp(sp - m)
                den = jnp.sum(ec + ep, axis=0, keepdims=True) + jnp.exp(sink - m)
                inv = 1.0 / den
                o4 = _dot_tn((ec * inv).astype(BF16), vc_ref[at, js]) + _dot_tn((ep * inv).astype(BF16), v_prev)
                lse4 = m + jnp.log(den)
                for g in range(Q_PER_KV):
                    h = j * Q_PER_KV + g
                    o_ref[at, h * HEAD_DIM:(h + 1) * HEAD_DIM] = o4[g * BLOCK:(g + 1) * BLOCK]
                    lse_ref[t * N_HEADS + h:t * N_HEADS + h + 1, :] = lse4[:, g * BLOCK:(g + 1) * BLOCK]
        ao_ref[...] = (o_ref[...] * _silu(za_ref[...])).astype(BF16)

    cur = lambda w: pl.BlockSpec((_ATTN_FWD_BLOCKS * BLOCK, w), lambda b, n: (b * steps + n, 0))
    prev = lambda w: pl.BlockSpec((BLOCK, w), lambda b, n: (b * nb + jnp.maximum(_ATTN_FWD_BLOCKS * n - 1, 0), 0))
    lse_rows = rows // BLOCK * N_HEADS
    return _call(
        body, name="attn_fwd", grid=(n_seq, steps),
        in_specs=[cur(512), cur(128), prev(128), cur(128), prev(128), cur(512), _full((1, N_HEADS))],
        out_specs=[cur(512), cur(512),
                   pl.BlockSpec((_ATTN_FWD_BLOCKS * N_HEADS, BLOCK), lambda b, n: (b * steps + n, 0))],
        out_shape=[_sds((rows, 512)), _sds((rows, 512), BF16), _sds((lse_rows, BLOCK))],
        scratch_shapes=[_ATTN_BIAS_SCRATCH], compiler_params=_params(32, ("arbitrary", "arbitrary")),
    )(q, k, k, v, v, za, sinks)


_ATTN_BWD_BLOCKS = 4


def _attn_bwd(q, k, v, za, o, lse, d_ao, sinks, n_seq, seq):
    nb = seq // BLOCK
    per_step = _ATTN_BWD_BLOCKS
    steps = nb // per_step
    rows = q.shape[0]

    def body(q_ref, kc_ref, kp_ref, vc_ref, vp_ref, za_ref, o_ref, lse_ref, d_ref, sk_ref,
             dq_ref, dk_ref, dv_ref, dza_ref, dsk_ref, bias_ref, dk_carry, dv_carry):
        step = pl.program_id(1)
        first_block = nb - per_step * (step + 1)
        _fill_attn_bias(bias_ref)

        @pl.when(jnp.logical_and(pl.program_id(0) == 0, step == 0))
        def _():
            dsk_ref[...] = jnp.zeros_like(dsk_ref)
            dk_carry[...] = jnp.zeros_like(dk_carry)
            dv_carry[...] = jnp.zeros_like(dv_carry)

        _, gate_vjp = jax.vjp(lambda o_, z_: o_ * _silu(z_), o_ref[...], za_ref[...])
        d_o, d_za = gate_vjp(d_ref[...])
        dza_ref[...] = d_za.astype(BF16)

        for j in range(KV_HEADS):
            js = slice(j * HEAD_DIM, (j + 1) * HEAD_DIM)
            bias_c, bias_p = bias_ref[j, 0], bias_ref[j, 1]
            sink = _sink_row(sk_ref, j)
            sink_loss = jnp.zeros((1, _GROUP_ROWS), F32)
            dk_from_next = jnp.where(step > 0, dk_carry[j], 0.0)
            dv_from_next = jnp.where(step > 0, dv_carry[j], 0.0)
            for t in reversed(range(per_step)):
                at = slice(t * BLOCK, (t + 1) * BLOCK)
                kc, vc = kc_ref[at, js], vc_ref[at, js]
                if t > 0:
                    before = slice((t - 1) * BLOCK, t * BLOCK)
                    kp, vp, bias_before = kc_ref[before, js], vc_ref[before, js], bias_p
                else:
                    kp, vp, bias_before = kp_ref[:, js], vp_ref[:, js], jnp.where(first_block > 0, bias_p, -jnp.inf)
                q4 = _stack_heads(q_ref[at, :], j)
                do4b = _stack_heads(d_o[at], j).astype(BF16)
                lse4 = _head_rows(lse_ref[t * N_HEADS:(t + 1) * N_HEADS, :], j)
                pc = jnp.exp(_dot_nt(kc, q4) + bias_c - lse4)
                pp = jnp.exp(_dot_nt(kp, q4) + bias_before - lse4)
                dpc = _dot_nt(vc, do4b)
                dpp = _dot_nt(vp, do4b)
                delta = jnp.sum(pc * dpc + pp * dpp, axis=0, keepdims=True)
                dsc = (pc * (dpc - delta)).astype(BF16)
                dsp = (pp * (dpp - delta)).astype(BF16)
                dq4 = ((_dot_tn(dsc, kc) + _dot_tn(dsp, kp)) * ATTN_SCALE).astype(BF16)
                sink_loss = sink_loss + jnp.exp(sink - lse4) * delta
                for g in range(Q_PER_KV):
                    h = j * Q_PER_KV + g
                    dq_ref[at, h * HEAD_DIM:(h + 1) * HEAD_DIM] = dq4[g * BLOCK:(g + 1) * BLOCK]
                dk_ref[at, js] = (_dot(dsc, q4) + dk_from_next).astype(BF16)
                dv_ref[at, js] = (_dot(pc.astype(BF16), do4b) + dv_from_next).astype(BF16)
                dk_from_next = _dot(dsp, q4)
                dv_from_next = _dot(pp.astype(BF16), do4b)
            dk_carry[j] = dk_from_next
            dv_carry[j] = dv_from_next
            for g in range(Q_PER_KV):
                h = j * Q_PER_KV + g
                dsk_ref[0:1, h:h + 1] -= jnp.sum(sink_loss[:, g * BLOCK:(g + 1) * BLOCK], axis=1, keepdims=True)

    cur = lambda w: pl.BlockSpec((per_step * BLOCK, w), lambda b, s: (b * steps + steps - 1 - s, 0))
    prev = lambda w: pl.BlockSpec((BLOCK, w), lambda b, s: (b * nb + jnp.maximum(nb - per_step * (s + 1) - 1, 0), 0))
    return _call(
        body, name="attn_bwd", grid=(n_seq, steps),
        in_specs=[cur(512), cur(128), prev(128), cur(128), prev(128), cur(512), cur(512),
                  pl.BlockSpec((per_step * N_HEADS, BLOCK), lambda b, s: (b * steps + steps - 1 - s, 0)), cur(512),
                  _full((1, N_HEADS))],
        out_specs=[cur(512), cur(128), cur(128), cur(512), _full((1, N_HEADS))],
        out_shape=[_sds((rows, 512), BF16), _sds((rows, 128), BF16), _sds((rows, 128), BF16),
                   _sds((rows, 512), BF16), _sds((1, N_HEADS))],
        scratch_shapes=[_ATTN_BIAS_SCRATCH, pltpu.VMEM((KV_HEADS, BLOCK, HEAD_DIM), F32),
                        pltpu.VMEM((KV_HEADS, BLOCK, HEAD_DIM), F32)],
        compiler_params=_params(32, ("arbitrary", "arbitrary")),
    )(q, k, k, v, v, za, o, lse, d_ao, sinks)


def _tail(ssm_out, attn_out, x2d, p2d, target, w_out, g2, w_gate, b_gate, w_proj):
    rows = x2d.shape[0]
    tm = 512

    def body(so_ref, ao_ref, x_ref, p_ref, t_ref, wo_ref, g2_ref, wg_ref, bg_ref, wp_ref,
             dh1_ref, dso_ref, dao_ref, dwo_ref, dwg_ref, dwp_ref, dbg_ref, dg2_ref, loss_ref):
        @pl.when(pl.program_id(0) == 0)
        def _():
            for ref in (dwo_ref, dwg_ref, dwp_ref, dbg_ref, dg2_ref, loss_ref):
                ref[...] = jnp.zeros_like(ref)

        cat = jnp.concatenate([so_ref[...], ao_ref[...]], axis=1)
        g2 = g2_ref[...]
        mixed = _dot(cat, wo_ref[...])
        r = lax.rsqrt(jnp.mean(mixed * mixed, axis=-1, keepdims=True) + EPS)
        mr = mixed * r
        h1 = x_ref[...] + mr * g2
        h1b = h1.astype(BF16)
        gate = jax.nn.sigmoid(_dot(h1b, wg_ref[...]) + bg_ref[...])
        pb = p_ref[...].astype(BF16)
        wp_blocks = [slice(j * D_PLE, (j + 1) * D_PLE) for j in range(N_CHIPS)]
        pp = jnp.concatenate([_dot(pb, wp_ref[blk, :]) for blk in wp_blocks], axis=1)
        err = h1 + gate * pp - t_ref[...]
        loss_ref[...] += 0.5 * jnp.sum(jnp.mean(err * err, axis=-1, keepdims=True), axis=0, keepdims=True)

        dh2 = err * (1.0 / D_MODEL)
        d_glin = dh2 * pp * gate * (1.0 - gate)
        d_glin_b = d_glin.astype(BF16)
        dwg_ref[...] += _dot_tn(h1b, d_glin_b)
        dbg_ref[...] += jnp.sum(d_glin, axis=0, keepdims=True)
        d_pp = (dh2 * gate).astype(BF16)
        for blk in wp_blocks:
            dwp_ref[blk, :] += _dot_tn(pb, d_pp[:, blk])
        dh1 = dh2 + _dot_nt(d_glin_b, wg_ref[...])
        dh1_ref[...] = dh1
        dg2_ref[...] += jnp.sum(dh1 * mr, axis=0, keepdims=True)
        a_ = dh1 * g2
        d_mixed = (r * a_ - mr * (r * jnp.mean(a_ * mr, axis=-1, keepdims=True))).astype(BF16)
        dwo_ref[...] += _dot_tn(cat, d_mixed)
        d_cat = _dot_nt(d_mixed, wo_ref[...])
        dso_ref[...] = d_cat[:, 0:512]
        dao_ref[...] = d_cat[:, 512:1024]

    return _call(
        body, name="tail_fwd_bwd", grid=(rows // tm,),
        in_specs=[_rows(tm, 512), _rows(tm, 512), _rows(tm, D_MODEL), _rows(tm, D_PLE), _rows(tm, D_MODEL),
                  _full((D_MODEL, D_MODEL)), _full((1, D_MODEL)), _full((D_MODEL, D_MODEL)), _full((1, D_MODEL)),
                  _full((N_CHIPS * D_PLE, D_PLE))],
        out_specs=[_rows(tm, D_MODEL), _rows(tm, 512), _rows(tm, 512), _full((D_MODEL, D_MODEL)),
                   _full((D_MODEL, D_MODEL)), _full((N_CHIPS * D_PLE, D_PLE)), _full((1, D_MODEL)), _full((1, D_MODEL)),
                   _full((1, 1))],
        out_shape=[_sds((rows, D_MODEL)), _sds((rows, 512)), _sds((rows, 512)), _sds((D_MODEL, D_MODEL)),
                   _sds((D_MODEL, D_MODEL)), _sds((N_CHIPS * D_PLE, D_PLE)), _sds((1, D_MODEL)), _sds((1, D_MODEL)),
                   _sds((1, 1))],
        compiler_params=_params(52, ("arbitrary",)),
    )(ssm_out, attn_out, x2d, p2d, target, w_out, g2, w_gate, b_gate, w_proj)


def _local_step(x, hn, p, target, pre_norm_g, w_in_t, s5_params, s5_operands, ssm_d, w_glu, b_glu, sinks, w_out,
                post_norm_g, w_proj, w_gate, b_gate, send_tail_grads=lambda ready: ready["w_out"],
                send_bc=lambda d_bc: (d_bc, d_bc)):
    n_seq, seq, _ = x.shape
    rows = n_seq * seq
    x2d = x.reshape(rows, D_MODEL)
    p2d = p.reshape(rows, D_PLE)
    t2d = target.reshape(rows, D_MODEL)

    l_re, l_im, bt_re, bt_im, cm_re, cm_im = s5_operands

    u_scan, zs, q, k, v, za = _in_proj(hn, w_in_t, n_seq, seq)
    y_scan, h_re, h_im = _s5_scan_fwd(u_scan, bt_re, bt_im, cm_re, cm_im, l_re, l_im, ssm_d, n_seq, seq)
    ssm_out = _glu_fwd(y_scan, zs, w_glu, b_glu, n_seq, seq)
    o, attn_out, lse = _attn_fwd(q, k, v, za, sinks, n_seq, seq)

    dh1, d_so, d_ao, d_w_out, d_w_gate, d_w_proj, d_b_gate, d_g2, loss = _tail(
        ssm_out, attn_out, x2d, p2d, t2d, w_out, post_norm_g, w_gate, b_gate, w_proj)

    dq, dk, dv, dza, d_sinks = _attn_bwd(q, k, v, za, o, lse, d_ao, sinks, n_seq, seq)
    dy_scan, dzs, d_w_glu, d_b_glu = _glu_bwd(y_scan, zs, d_so, w_glu, b_glu, n_seq, seq)
    du_scan, d_bt_re, d_bt_im, d_cm_re, d_cm_im, d_l_re, d_l_im, d_d = _s5_scan_bwd(
        dy_scan, u_scan, h_re, h_im, bt_re, bt_im, cm_re, cm_im, l_re, l_im, ssm_d, n_seq, seq)
    tail_grads_arrived = send_tail_grads(dict(w_out=d_w_out, pl_w_gate=d_w_gate, pl_w_proj=d_w_proj))
    d_lam_re, d_lam_im, d_log_step, d_bc = _s5_params_bwd(
        s5_params, (d_l_re, d_l_im, d_bt_re, d_bt_im, d_cm_re, d_cm_im), tail_grads_arrived)

    sent, arrived = send_bc(d_bc)
    d_proj, d_w_in_early, d_w_in_early_b = _in_proj_bwd_early(hn, du_scan, dzs, dq, dk, dv, dza, sent, n_seq, seq)
    grad_x, d_w_in_late, d_g1 = _in_proj_bwd(x2d, dh1, pre_norm_g, w_in_t, d_proj, arrived)
    grads = dict(
        pre_norm_g=d_g1, w_in_early=d_w_in_early, w_in_early_bf16=d_w_in_early_b, w_in_late=d_w_in_late,
        ssm_lam_re=d_lam_re, ssm_lam_im=d_lam_im, ssm_log_step=d_log_step, ssm_bc=d_bc, ssm_d=d_d, ssm_w_glu=d_w_glu,
        ssm_b_glu=d_b_glu, attn_sinks=d_sinks, w_out=d_w_out, post_norm_g=d_g2, pl_w_proj=d_w_proj,
        pl_w_gate=d_w_gate, pl_b_gate=d_b_gate)
    return grad_x.reshape(x.shape), loss, grads


_BIG = ("w_in", "ssm_w_glu", "w_out", "pl_w_proj", "pl_w_gate")
_BIG_SHARD = {"w_in": (D_IN // N_CHIPS, D_MODEL), "ssm_w_glu": (D_SSM // N_CHIPS, D_SSM),
              "w_out": (D_MODEL // N_CHIPS, D_MODEL), "pl_w_proj": (D_PLE, D_MODEL // N_CHIPS),
              "pl_w_gate": (D_MODEL // N_CHIPS, D_MODEL)}
_SMALL = {"pre_norm_g": (1, D_MODEL), "ssm_lam_re": (SSM_GROUPS, SSM_STATE), "ssm_lam_im": (SSM_GROUPS, SSM_STATE),
          "ssm_log_step": (1, SSM_GROUPS), "ssm_b_re": (D_SSM, SSM_STATE), "ssm_b_im": (D_SSM, SSM_STATE),
          "ssm_c_re": (D_SSM, SSM_STATE), "ssm_c_im": (D_SSM, SSM_STATE), "ssm_d": (1, D_SSM), "ssm_b_glu": (1, D_SSM),
          "attn_sinks": (1, N_HEADS), "post_norm_g": (1, D_MODEL), "pl_b_gate": (1, D_MODEL)}
_VEC_ROWS = ("pre_norm_g", "post_norm_g", "pl_b_gate", "ssm_d", "ssm_b_glu", "attn_sinks", "ssm_log_step", "loss")
_SMALL_GROUPS = (
    ("vec", (8, D_MODEL), tuple((name, r) for r, name in enumerate(_VEC_ROWS))),
    ("lam", (2 * SSM_GROUPS, SSM_STATE), (("ssm_lam_re", 0), ("ssm_lam_im", SSM_GROUPS))),
)
_SMALL_EARLY = ("ssm_b_re", "ssm_b_im", "ssm_c_re", "ssm_c_im")
_SMALL_ORDER = tuple(name for _, _, members in _SMALL_GROUPS for name, _ in members) + _SMALL_EARLY
_WEIGHT_ORDER = ("pre_norm_g", "w_in", "ssm_lam_re", "ssm_lam_im", "ssm_log_step", "ssm_b_re", "ssm_b_im", "ssm_c_re",
                 "ssm_c_im", "ssm_d", "ssm_w_glu", "ssm_b_glu", "attn_sinks", "w_out", "post_norm_g", "pl_w_proj",
                 "pl_w_gate", "pl_b_gate")


def _small_shape(name):
    return (1, 1) if name == "loss" else _SMALL[name]


def _to_kernel_form(name, a):
    a = a[0]
    if name == "w_in":
        return a.T
    if name in ("ssm_b_re", "ssm_b_im"):
        a = a.transpose(0, 2, 1)
    return a.reshape(_SMALL[name]) if name in _SMALL else a


def _from_kernel_form(name, a, shape):
    if name == "w_in":
        a = a.T
    if name in ("ssm_b_re", "ssm_b_im"):
        a = a.reshape(SSM_GROUPS, SSM_GROUP_CH, SSM_STATE).transpose(0, 2, 1)
    return a.reshape(shape)


def _mesh_place():
    x, y, c = lax.axis_index("x"), lax.axis_index("y"), lax.axis_index("c")
    other_chips = ((1 - x, y), (x, 1 - y), (1 - x, 1 - y))
    return x, y, c, other_chips


def _gather_copies(s_refs, g_refs, send_sems, recv_sems, local_sems):
    x, y, c, other_chips = _mesh_place()
    started = []
    for i, (s_ref, g_ref) in enumerate(zip(s_refs, g_refs)):
        rows = s_ref.shape[0]
        half = rows // 2

        def block(chip, g_ref=g_ref, rows=rows, half=half):
            return g_ref.at[pl.ds((2 * chip[0] + chip[1]) * rows + c * half, half), :]

        def copy(k, chip, to, src=None, i=i, block=block):
            return pltpu.make_async_remote_copy(
                src_ref=block(chip) if src is None else src, dst_ref=block(chip), send_sem=send_sems.at[6 * i + k],
                recv_sem=recv_sems.at[6 * i + k], device_id=to, device_id_type=MESH)

        own = pltpu.make_async_copy(s_ref, g_ref.at[pl.ds((2 * x + y) * rows, rows), :], local_sems.at[i])
        own.start()
        first = [copy(k, (x, y), (*chip, c), src=s_ref.at[pl.ds(c * half, half), :])
                 for k, chip in enumerate(other_chips)]
        for cp in first:
            cp.start()
        passed = [copy(3 + k, chip, (x, y, 1 - c)) for k, chip in enumerate(other_chips)]
        started.append((own, first, passed))
    for own, first, passed in started:
        for k in range(3):
            first[k].wait_recv()
            passed[k].start()
    for own, first, passed in started:
        for k in range(3):
            passed[k].wait_recv()
        for cp in first + passed:
            cp.wait_send()
        own.wait()


def _gather_semaphores(n_t):
    return [pltpu.SemaphoreType.DMA((6 * n_t,)), pltpu.SemaphoreType.DMA((6 * n_t,)), pltpu.SemaphoreType.DMA((n_t,))]


def _gather_weights_beside(shards, name, collective_id):
    n_t = len(shards)
    hbm = pltpu.MemorySpace.HBM
    s_refs = [jax.new_ref(s, memory_space=hbm) for s in shards]
    g_refs = [jax.empty_ref(jax.ShapeDtypeStruct((N_CHIPS * s.shape[0], s.shape[1]), s.dtype), memory_space=hbm)
              for s in shards]

    def launch(send_sems, recv_sems, local_sems):
        x, y, c, other_chips = _mesh_place()
        peers = [(*chip, c) for chip in other_chips] + [(x, y, 1 - c)]
        barrier = pltpu.get_barrier_semaphore()
        for peer in peers:
            pl.semaphore_signal(barrier, inc=1, device_id=peer, device_id_type=MESH)
        pl.semaphore_wait(barrier, len(peers))
        _gather_copies(s_refs, g_refs, send_sems, recv_sems, local_sems)

    pl.kernel(launch, mesh=plsc.ScalarSubcoreMesh(axis_name="sequencer", num_cores=1), name=name,
              scratch_types=_gather_semaphores(n_t), compiler_params=pltpu.CompilerParams(collective_id=collective_id))()
    return [g[...] for g in g_refs]


_RELATIONS = tuple(((r >> 2) & 1, (r >> 1) & 1, r & 1) for r in range(1, 8))


def _related(place, relation):
    return tuple(1 - a if flip else a for a, flip in zip(place, relation))


def _scatter_beside(mats, name, collective_id):
    hbm = pltpu.MemorySpace.HBM
    src_refs = [jax.new_ref(a, memory_space=hbm) for a in mats]
    land_refs = [jax.empty_ref(jax.ShapeDtypeStruct((7, a.shape[0] // 8, a.shape[1]), a.dtype), memory_space=hbm)
                 for a in mats]

    def launch(send_sems, recv_sems):
        me = (lax.axis_index("x"), lax.axis_index("y"), lax.axis_index("c"))
        peers = [_related(me, rel) for rel in _RELATIONS]
        barrier = pltpu.get_barrier_semaphore()
        for peer in peers:
            pl.semaphore_signal(barrier, inc=1, device_id=peer, device_id_type=MESH)
        pl.semaphore_wait(barrier, len(peers))
        copies = []
        for i, (src, land) in enumerate(zip(src_refs, land_refs)):
            hr = land.shape[1]
            for k, (tx, ty, tc) in enumerate(peers):
                rows = pl.ds((2 * tx + ty) * 2 * hr + tc * hr, hr)
                copies.append(pltpu.make_async_remote_copy(
                    src_ref=src.at[rows, :], dst_ref=land.at[k], send_sem=send_sems.at[7 * i + k],
                    recv_sem=recv_sems.at[7 * i + k], device_id=(tx, ty, tc), device_id_type=MESH))
                copies[-1].start()
        for cp in copies:
            cp.wait()

    n_sems = 7 * len(mats)
    pl.kernel(launch, mesh=plsc.ScalarSubcoreMesh(axis_name="sequencer", num_cores=1), name=name,
              scratch_types=[pltpu.SemaphoreType.DMA((n_sems,)), pltpu.SemaphoreType.DMA((n_sems,))],
              compiler_params=pltpu.CompilerParams(collective_id=collective_id))()
    return [ref[...] for ref in land_refs]


def _broadcast_beside(arrays):
    hbm = pltpu.MemorySpace.HBM
    src_refs = [jax.new_ref(a, memory_space=hbm) for a in arrays]
    land_refs = [jax.empty_ref(jax.ShapeDtypeStruct((len(_RELATIONS),) + a.shape, a.dtype), memory_space=hbm)
                 for a in arrays]

    def launch(send_sems, recv_sems):
        me = (lax.axis_index("x"), lax.axis_index("y"), lax.axis_index("c"))
        peers = [_related(me, rel) for rel in _RELATIONS]
        barrier = pltpu.get_barrier_semaphore()
        for peer in peers:
            pl.semaphore_signal(barrier, inc=1, device_id=peer, device_id_type=MESH)
        pl.semaphore_wait(barrier, len(peers))
        copies = []
        for i, (src, land) in enumerate(zip(src_refs, land_refs)):
            for k, peer in enumerate(peers):
                copies.append(pltpu.make_async_remote_copy(
                    src_ref=src, dst_ref=land.at[k], send_sem=send_sems.at[7 * i + k],
                    recv_sem=recv_sems.at[7 * i + k], device_id=peer, device_id_type=MESH))
                copies[-1].start()
        for cp in copies:
            cp.wait()

    n_sems = 7 * len(arrays)
    pl.kernel(launch, mesh=plsc.ScalarSubcoreMesh(axis_name="sequencer", num_cores=1), name="broadcast_beside",
              scratch_types=[pltpu.SemaphoreType.DMA((n_sems,)), pltpu.SemaphoreType.DMA((n_sems,))],
              compiler_params=pltpu.CompilerParams(collective_id=3))()
    return [ref[...] for ref in land_refs]


def _exchange_grads(big, outputs, small, landed, own_bc, landed_bc):
    n_t = len(big)
    n_g = len(_SMALL_GROUPS)
    names = _SMALL_ORDER
    halves = [(b.shape[0] // N_CHIPS // 2, b.shape[1]) for b in big]
    early = sorted(landed)
    late = [i for i in range(n_t) if i not in landed]
    n_sems = 4 * n_g + 7 * len(late) + n_t
    small_sem0, block_sem0 = n_t, n_t + len(names)
    early_sem0 = block_sem0 + N_CHIPS * len(late)
    landed_sem0 = early_sem0 + 2 * len(early)
    sent = [n for n in names if n in small]

    def body(*refs):
        pos = 0

        def take(n):
            nonlocal pos
            pos += n
            return refs[pos - n:pos]

        big_refs, small_refs = take(n_t), dict(zip(sent, take(len(sent))))
        land_refs = dict(zip(early, take(len(early))))
        own_bc_ref, landed_bc_ref = take(2)
        out_refs, small_out_refs = take(len(outputs)), dict(zip(names, take(len(names))))
        per_late = lambda: dict(zip(late, take(len(late))))
        ga, gb, pme, send_b, recv_b = per_late(), per_late(), take(n_t), per_late(), per_late()
        own_e, land_e = dict(zip(early, take(len(early)))), dict(zip(early, take(len(early))))
        own_s, land_s = take(2)
        s_own, s_sib, s_chips, s_pair = take(n_g), take(n_g), take(n_g), take(n_g)
        stage = dict(zip(names, take(len(names))))
        send_sems, recv_sems, local_sems = take(3)
        x, y, c, other_chips = _mesh_place()
        me = 2 * x + y
        sibling = (x, y, 1 - c)
        sem_at = iter(range(n_sems))

        def remote(src, dst, to):
            k = next(sem_at)
            return pltpu.make_async_remote_copy(src_ref=src, dst_ref=dst, send_sem=send_sems.at[k],
                                                recv_sem=recv_sems.at[k], device_id=to, device_id_type=MESH)

        loads = [pltpu.make_async_copy(small_refs[name], stage[name], local_sems.at[small_sem0 + names.index(name)])
                 for name in sent]
        landed_loads = [pltpu.make_async_copy(own_bc_ref, own_s, local_sems.at[landed_sem0]),
                        pltpu.make_async_copy(landed_bc_ref, land_s, local_sems.at[landed_sem0 + 1])]
        for cp in loads + landed_loads:
            cp.start()
        for cp in loads:
            cp.wait()
        small_swaps = []
        for gi, (_, _, members) in enumerate(_SMALL_GROUPS):
            s_own[gi][...] = jnp.zeros_like(s_own[gi])
            for name, r0 in members:
                r, n = _small_shape(name)
                s_own[gi][r0:r0 + r, 0:n] = stage[name][...]
            small_swaps.append(remote(s_own[gi], s_sib[gi], sibling))
            small_swaps[gi].start()
        order = sorted(late, key=lambda i: halves[i][0] * halves[i][1])
        own_loads, big_swaps = {}, {}
        for i in order:
            hr = halves[i][0]
            own_loads[i], big_swaps[i] = [], []
            for j in range(N_CHIPS):
                mine = big_refs[i].at[pl.ds(j * 2 * hr + c * hr, hr), :]
                theirs = big_refs[i].at[pl.ds(j * 2 * hr + (1 - c) * hr, hr), :]
                sem = local_sems.at[block_sem0 + N_CHIPS * late.index(i) + j]
                own_loads[i].append(pltpu.make_async_copy(mine, ga[i].at[j], sem))
                own_loads[i][j].start()
                big_swaps[i].append(remote(theirs, gb[i].at[j], sibling))
                big_swaps[i][j].start()
        early_loads = {}
        for e, i in enumerate(early):
            hr = halves[i][0]
            mine = big_refs[i].at[pl.ds(me * 2 * hr + c * hr, hr), :]
            early_loads[i] = [pltpu.make_async_copy(mine, own_e[i], local_sems.at[early_sem0 + 2 * e]),
                              pltpu.make_async_copy(land_refs[i], land_e[i], local_sems.at[early_sem0 + 2 * e + 1])]
            for cp in early_loads[i]:
                cp.start()
        small_sends = []
        for gi in range(n_g):
            small_swaps[gi].wait_recv()
            s_pair[gi][...] = s_own[gi][...] + s_sib[gi][...]
            small_sends.append([remote(s_pair[gi], s_chips[gi].at[k], (*chip, c)) for k, chip in enumerate(other_chips)])
            for cp in small_sends[gi]:
                cp.start()

        def pair_sum(i, j):
            return ga[i][j] + gb[i][j]

        big_sends = {}
        for i in order:
            for j in range(N_CHIPS):
                own_loads[i][j].wait()
                big_swaps[i][j].wait_recv()
            big_sends[i] = []
            for k, chip in enumerate(other_chips):
                send_b[i][k] = pair_sum(i, 2 * chip[0] + chip[1]).astype(BF16)
                big_sends[i].append(remote(send_b[i].at[k], recv_b[i].at[k], (*chip, c)))
                big_sends[i][k].start()
        last_swaps, keeps = {}, {}
        for i in early + order:
            hr = halves[i][0]
            if i in landed:
                for cp in early_loads[i]:
                    cp.wait()
                total = own_e[i][...]
                for k in range(len(_RELATIONS)):
                    total = total + land_e[i][k].astype(F32)
                pme[i][...] = total
            else:
                for k in range(3):
                    big_sends[i][k].wait_recv()
                pme[i][...] = ((pair_sum(i, me) + recv_b[i][0].astype(F32)) + recv_b[i][1].astype(F32)) + recv_b[i][2].astype(F32)
            o, = [o for o, group in enumerate(outputs) if i in group]
            first_col = sum(halves[j][1] for j in outputs[o][:outputs[o].index(i)])
            mine = out_refs[o].at[pl.ds(c * hr, hr), pl.ds(first_col, halves[i][1])]
            keeps[i] = pltpu.make_async_copy(pme[i], mine, local_sems.at[i])
            keeps[i].start()
            last_swaps[i] = remote(pme[i], mine, sibling)
            last_swaps[i].start()

        for gi, (_, _, members) in enumerate(_SMALL_GROUPS):
            for k in range(3):
                small_sends[gi][k].wait_recv()
            total = None
            for j in range(N_CHIPS):
                rel = jnp.bitwise_xor(j, me)
                term = jnp.where(rel == 0, s_pair[gi][...], jnp.where(
                    rel == 2, s_chips[gi][0], jnp.where(rel == 1, s_chips[gi][1], s_chips[gi][2])))
                total = term if total is None else total + term
            s_sib[gi][...] = total
            for name, r0 in members:
                r, n = _small_shape(name)
                stage[name][...] = s_sib[gi][r0:r0 + r, 0:n]
        my_index = 4 * x + 2 * y + c
        for cp in landed_loads:
            cp.wait()
        total = None
        for d in range(2 * N_CHIPS):
            rel = jnp.bitwise_xor(d, my_index)
            term = own_s[...]
            for k in range(len(_RELATIONS)):
                term = jnp.where(rel == k + 1, land_s[k], term)
            total = term.astype(F32) if total is None else total + term.astype(F32)
        for a, name in enumerate(_SMALL_EARLY):
            stage[name][...] = total[:, a * SSM_STATE:(a + 1) * SSM_STATE]
        stores = [pltpu.make_async_copy(stage[name], small_out_refs[name], local_sems.at[small_sem0 + a])
                  for a, name in enumerate(names)]
        for cp in stores:
            cp.start()

        for i in range(n_t):
            last_swaps[i].wait_recv()
            keeps[i].wait()
        for cp in stores:
            cp.wait()
        groups = list(big_swaps.values()) + small_sends + list(big_sends.values())
        for cp in small_swaps + [cp for group in groups for cp in group] + list(last_swaps.values()):
            cp.wait_send()

    any_spec = pl.BlockSpec(memory_space=pl.ANY)
    small_shapes = [_sds(_small_shape(n)) for n in names]
    group_shapes = [shape for _, shape, _ in _SMALL_GROUPS]
    vmem = lambda which, dtype, lead=(): [pltpu.VMEM(lead + halves[i], dtype) for i in which]
    outs = _call(
        body, name="exchange_grads",
        in_specs=[any_spec] * (n_t + len(sent) + len(early) + 2),
        out_specs=[any_spec] * (len(outputs) + len(names)),
        out_shape=[_sds((big[group[0]].shape[0] // N_CHIPS, sum(big[i].shape[1] for i in group))) for group in outputs]
        + small_shapes,
        scratch_shapes=(vmem(late, F32, (N_CHIPS,)) + vmem(late, F32, (N_CHIPS,)) + vmem(range(n_t), F32)
                        + vmem(late, BF16, (3,)) + vmem(late, BF16, (3,))
                        + vmem(early, F32)
                        + [pltpu.VMEM((len(_RELATIONS),) + halves[i], landed[i].dtype) for i in early]
                        + [pltpu.VMEM(own_bc.shape, own_bc.dtype), pltpu.VMEM(landed_bc.shape, landed_bc.dtype)]
                        + [pltpu.VMEM(s, F32) for s in group_shapes] * 2 + [pltpu.VMEM((3,) + s, F32) for s in group_shapes]
                        + [pltpu.VMEM(s, F32) for s in group_shapes]
                        + [pltpu.VMEM(_small_shape(n), F32) for n in names]
                        + [pltpu.SemaphoreType.DMA((n_sems,)), pltpu.SemaphoreType.DMA((n_sems,)),
                           pltpu.SemaphoreType.DMA((landed_sem0 + 2,))]),
        compiler_params=_params(48),
    )(*big, *[small[n] for n in sent], *[landed[i] for i in early], own_bc, landed_bc)
    return list(outs[:len(outputs)]), dict(zip(names, outs[len(outputs):]))


def _adamw_update(w, g, m, v):
    m = ADAM_B1 * m + (1.0 - ADAM_B1) * g
    v = ADAM_B2 * v + (1.0 - ADAM_B2) * (g * g)
    m_hat = m / (1.0 - ADAM_B1 ** ADAM_STEP)
    v_hat = v / (1.0 - ADAM_B2 ** ADAM_STEP)
    return -ADAM_LR * (m_hat / (jnp.sqrt(v_hat) + ADAM_EPS) + ADAM_WD * w), m, v


def _adamw(w, g, m, v, grid, name):
    n_t = len(w)

    def body(*refs):
        ins, outs = refs[:4 * n_t], refs[4 * n_t:]
        for i in range(n_t):
            w_, g_, m_, v_ = [ins[a * n_t + i][...] for a in range(4)]
            vals = (g_,) + _adamw_update(w_, g_, m_, v_)
            for a in range(4):
                outs[a * n_t + i][...] = vals[a]

    specs = [pl.BlockSpec((a.shape[0] // grid, a.shape[1]), lambda i: (i, 0)) for a in w]
    shapes = [_sds(a.shape) for a in w]
    outs = _call(
        body, name=name, grid=(grid,), in_specs=specs * 4, out_specs=specs * 4, out_shape=shapes * 4,
        compiler_params=_params(40, ("arbitrary",)),
    )(*w, *g, *m, *v)
    return [outs[a * n_t:(a + 1) * n_t] for a in range(4)]


def kernel(x, p, pre_norm_g, w_in, ssm_lam_re, ssm_lam_im, ssm_log_step, ssm_b_re, ssm_b_im, ssm_c_re, ssm_c_im, ssm_d, ssm_w_glu, ssm_b_glu, attn_sinks, w_out, post_norm_g, pl_w_proj, pl_w_gate, pl_b_gate, loss_target, m_pre_norm_g, m_w_in, m_ssm_lam_re, m_ssm_lam_im, m_ssm_log_step, m_ssm_b_re, m_ssm_b_im, m_ssm_c_re, m_ssm_c_im, m_ssm_d, m_ssm_w_glu, m_ssm_b_glu, m_attn_sinks, m_w_out, m_post_norm_g, m_pl_w_proj, m_pl_w_gate, m_pl_b_gate, v_pre_norm_g, v_w_in, v_ssm_lam_re, v_ssm_lam_im, v_ssm_log_step, v_ssm_b_re, v_ssm_b_im, v_ssm_c_re, v_ssm_c_im, v_ssm_d, v_ssm_w_glu, v_ssm_b_glu, v_attn_sinks, v_w_out, v_post_norm_g, v_pl_w_proj, v_pl_w_gate, v_pl_b_gate):
    weights = dict(pre_norm_g=pre_norm_g, w_in=w_in, ssm_lam_re=ssm_lam_re, ssm_lam_im=ssm_lam_im,
                   ssm_log_step=ssm_log_step, ssm_b_re=ssm_b_re, ssm_b_im=ssm_b_im, ssm_c_re=ssm_c_re,
                   ssm_c_im=ssm_c_im, ssm_d=ssm_d, ssm_w_glu=ssm_w_glu, ssm_b_glu=ssm_b_glu, attn_sinks=attn_sinks,
                   w_out=w_out, post_norm_g=post_norm_g, pl_w_proj=pl_w_proj, pl_w_gate=pl_w_gate, pl_b_gate=pl_b_gate)
    m_in = dict(pre_norm_g=m_pre_norm_g, w_in=m_w_in, ssm_lam_re=m_ssm_lam_re, ssm_lam_im=m_ssm_lam_im,
                ssm_log_step=m_ssm_log_step, ssm_b_re=m_ssm_b_re, ssm_b_im=m_ssm_b_im, ssm_c_re=m_ssm_c_re,
                ssm_c_im=m_ssm_c_im, ssm_d=m_ssm_d, ssm_w_glu=m_ssm_w_glu, ssm_b_glu=m_ssm_b_glu,
                attn_sinks=m_attn_sinks, w_out=m_w_out, post_norm_g=m_post_norm_g, pl_w_proj=m_pl_w_proj,
                pl_w_gate=m_pl_w_gate, pl_b_gate=m_pl_b_gate)
    v_in = dict(pre_norm_g=v_pre_norm_g, w_in=v_w_in, ssm_lam_re=v_ssm_lam_re, ssm_lam_im=v_ssm_lam_im,
                ssm_log_step=v_ssm_log_step, ssm_b_re=v_ssm_b_re, ssm_b_im=v_ssm_b_im, ssm_c_re=v_ssm_c_re,
                ssm_c_im=v_ssm_c_im, ssm_d=v_ssm_d, ssm_w_glu=v_ssm_w_glu, ssm_b_glu=v_ssm_b_glu,
                attn_sinks=v_attn_sinks, w_out=v_w_out, post_norm_g=v_post_norm_g, pl_w_proj=v_pl_w_proj,
                pl_w_gate=v_pl_w_gate, pl_b_gate=v_pl_b_gate)

    def two_d(tree):
        return {k: _to_kernel_form(k, a) for k, a in tree.items()}

    w2, m2, v2 = two_d(weights), two_d(m_in), two_d(v_in)

    (w_in_full,) = _gather_weights_beside([w2["w_in"].astype(BF16)], "gather_w_in_beside", 4)
    s5_params = tuple(w2[n] for n in ("ssm_lam_re", "ssm_lam_im", "ssm_log_step", "ssm_b_re", "ssm_b_im", "ssm_c_re",
                                      "ssm_c_im"))
    s5_operands = _s5_params_fwd(*s5_params)
    hn = _pre_norm(x.reshape(-1, D_MODEL), w2["pre_norm_g"])
    behind = s5_operands[0][0, 0] * 0.0 + hn[0, 0].astype(F32) * 0.0
    rest = _gather_weights_beside([(w2[n] + behind).astype(BF16) for n in _BIG[1:]], "gather_weights_beside", 1)
    full = dict(zip(_BIG, [w_in_full] + rest))
    mats = ("w_in_early", "w_in_late") + _BIG[1:]
    landed, bc = {}, {}

    def send_tail_grads(ready):
        sent_early = ("w_out", "pl_w_gate", "pl_w_proj")
        landed.update(zip([mats.index(n) for n in sent_early],
                          _scatter_beside([ready[n] for n in sent_early], "scatter_beside", 2)))
        return landed[mats.index(sent_early[-1])]

    def send_bc(d_bc):
        bc["own"] = d_bc
        bc["landed"] = _broadcast_beside([d_bc])[0]
        return d_bc, bc["landed"]

    grad_x, loss, grads = _local_step(
        x, hn, p, loss_target, w2["pre_norm_g"], full["w_in"], s5_params, s5_operands, w2["ssm_d"], full["ssm_w_glu"], w2["ssm_b_glu"],
        w2["attn_sinks"], full["w_out"], w2["post_norm_g"], full["pl_w_proj"], full["pl_w_gate"], w2["pl_b_gate"], send_tail_grads, send_bc)

    landed[0], landed[mats.index("ssm_w_glu")] = _scatter_beside(
        [grads["w_in_early_bf16"], grads["ssm_w_glu"]], "scatter_w_in_beside", 5)
    sent_here = {**{n: grads[n] for n in _SMALL if n not in _SMALL_EARLY}, "loss": loss}
    halves_of_w_in = ((0, 1),) + tuple((i,) for i in range(2, len(mats)))
    g_big, g_small = _exchange_grads([grads[n] for n in mats], halves_of_w_in, sent_here, landed, bc["own"], bc["landed"])
    g_big = dict(zip(_BIG, g_big))
    total_loss = g_small.pop("loss")

    big_out = _adamw([w2[n] for n in _BIG], [g_big[n] for n in _BIG], [m2[n] for n in _BIG], [v2[n] for n in _BIG],
                     8, "adamw_matrices")
    small_names = tuple(_SMALL)
    small_out = _adamw([w2[n] for n in small_names], [g_small[n] for n in small_names], [m2[n] for n in small_names],
                       [v2[n] for n in small_names], 1, "adamw_small")

    results = [{**dict(zip(_BIG, big_part)), **dict(zip(small_names, small_part))}
               for big_part, small_part in zip(big_out, small_out)]
    flat = [_from_kernel_form(name, r[name], weights[name].shape) for r in results for name in _WEIGHT_ORDER]
    return (total_loss.reshape(()), grad_x, *flat)
```

```python
import math

import jax
import jax.numpy as jnp
from jax import lax
from jax.experimental import pallas as pl
from jax.experimental.pallas import tpu as pltpu
from jax.experimental.pallas import tpu_sc as plsc

F32 = jnp.float32
BF16 = jnp.bfloat16

D_MODEL = 1024
D_SSM = 512
D_ATTN = 512
SSM_GROUPS = 32
SSM_GROUP_CH = 16
SSM_STATE = 64
SSM_LANES = SSM_GROUPS * SSM_STATE
HEAD_DIM = 64
N_HEADS = 8
KV_HEADS = 2
Q_PER_KV = 4
WINDOW = 128
BLOCK = 128
D_PLE = 256
D_IN = 2304
EPS = 1e-6
ATTN_SCALE = 1.0 / math.sqrt(HEAD_DIM)

ADAM_LR = 0.001
ADAM_B1 = 0.9
ADAM_B2 = 0.999
ADAM_EPS = 1e-08
ADAM_WD = 0.01
ADAM_STEP = 10

N_CHIPS = 4
LANES = 128
SCAN_CHUNKS = 8
SCAN_TILE_STEPS = 32
SCAN_LANE_CHUNK = 512
MIB = 2 ** 20
MESH = pl.DeviceIdType.MESH


def _dot(a, b):
    return jnp.dot(a, b, preferred_element_type=F32)


def _dot_nt(a, b):
    return lax.dot_general(a, b, (((1,), (1,)), ((), ())), preferred_element_type=F32)


def _dot_tn(a, b):
    return lax.dot_general(a, b, (((0,), (0,)), ((), ())), preferred_element_type=F32)


def _params(vmem_mib, semantics=None):
    kw = dict(vmem_limit_bytes=vmem_mib * MIB)
    if semantics is not None:
        kw["dimension_semantics"] = semantics
    return pltpu.CompilerParams(**kw)


def _full(shape):
    nd = len(shape)
    return pl.BlockSpec(shape, lambda *_: (0,) * nd, pipeline_mode=pl.Buffered(1))


def _rows(tm, width):
    return pl.BlockSpec((tm, width), lambda i: (i, 0))


def _sds(shape, dtype=F32):
    return pltpu.HBM(shape, dtype)


def _call(body, **kw):
    fn = pl.pallas_call(body, **kw)
    return lambda *args: fn(*[pltpu.with_memory_space_constraint(a, pltpu.HBM) for a in args])


def _silu(z):
    return z * jax.nn.sigmoid(z)


def _pre_norm(x2d, g1):
    rows = x2d.shape[0]
    tm = 512

    def body(x_ref, g_ref, hn_ref):
        x = x_ref[...]
        r = lax.rsqrt(jnp.mean(x * x, axis=-1, keepdims=True) + EPS)
        hn_ref[...] = (x * r * g_ref[...]).astype(BF16)

    return _call(
        body, name="pre_norm", grid=(rows // tm,), in_specs=[_rows(tm, D_MODEL), _full((1, D_MODEL))],
        out_specs=_rows(tm, D_MODEL), out_shape=_sds((rows, D_MODEL), BF16), compiler_params=_params(32, ("arbitrary",)),
    )(x2d, g1)


def _in_proj(hn, w_in_t, n_seq, seq):
    rows = hn.shape[0]
    tm = 1024
    slab, steps, _, _ = _scan_geometry(n_seq, seq)

    def body(hn_ref, w_ref, *out_refs):
        u_parts, (zs_ref, q_ref, k_ref, v_ref, za_ref) = out_refs[:_SCAN_PARTS], out_refs[_SCAN_PARTS:]
        whole = _dot_nt(hn_ref[...], w_ref[...])

        def proj(a, b):
            return whole[:, a:b]

        _store_chunks(u_parts, pl.program_id(0) * (tm // steps), proj(0, 512), steps, slab)
        zs_ref[...] = proj(512, 1024)
        q_ref[...] = (proj(1024, 1536) * ATTN_SCALE).astype(BF16)
        k_ref[...] = proj(1536, 1664).astype(BF16)
        v_ref[...] = proj(1664, 1792).astype(BF16)
        za_ref[...] = proj(1792, 2304)

    *u_parts, zs, q, k, v, za = _call(
        body, name="in_proj", grid=(rows // tm,),
        in_specs=[_rows(tm, D_MODEL), _full((D_IN, D_MODEL))],
        out_specs=_whole_parts(rows) + [_rows(tm, 512), _rows(tm, 512), _rows(tm, 128), _rows(tm, 128), _rows(tm, 512)],
        out_shape=_part_shapes(rows) + [_sds((rows, 512)), _sds((rows, 512), BF16), _sds((rows, 128), BF16),
                                        _sds((rows, 128), BF16), _sds((rows, 512))],
        compiler_params=_params(48, ("arbitrary",)),
    )(hn, w_in_t)
    return u_parts, zs, q, k, v, za


_EARLY_COLS = D_MODEL // 2


def _in_proj_bwd_early(hn, du_parts, dzs, dq, dk, dv, dza, runs_after, n_seq, seq):
    rows = hn.shape[0]
    tm = 1024
    slab, steps, _, _ = _scan_geometry(n_seq, seq)

    def body(hn_ref, *refs):
        du_parts, (dzs_ref, dq_ref, dk_ref, dv_ref, dza_ref, _, dproj_ref, dw_ref, dwb_ref) = refs[:_SCAN_PARTS], refs[_SCAN_PARTS:]
        i = pl.program_id(0)

        @pl.when(i == 0)
        def _():
            dw_ref[...] = jnp.zeros_like(dw_ref)

        du = _load_chunks(du_parts, i * (tm // steps), tm // steps, steps, slab)
        d_proj = jnp.concatenate([du.astype(BF16), dzs_ref[...], dq_ref[...], dk_ref[...], dv_ref[...], dza_ref[...]],
                                 axis=1)
        dproj_ref[...] = d_proj
        dw_ref[...] += _dot_tn(d_proj, hn_ref[...])

        @pl.when(i == rows // tm - 1)
        def _():
            dwb_ref[...] = dw_ref[...].astype(BF16)

    return _call(
        body, name="in_proj_bwd_early", grid=(rows // tm,),
        in_specs=[_rows(tm, _EARLY_COLS)] + _whole_parts(rows)
        + [_rows(tm, 512), _rows(tm, 512), _rows(tm, 128), _rows(tm, 128), _rows(tm, 512),
           pl.BlockSpec(memory_space=pl.ANY)],
        out_specs=[_rows(tm, D_IN), _full((D_IN, _EARLY_COLS)), _full((D_IN, _EARLY_COLS))],
        out_shape=[_sds((rows, D_IN), BF16), _sds((D_IN, _EARLY_COLS)), _sds((D_IN, _EARLY_COLS), BF16)],
        compiler_params=_params(48, ("arbitrary",)),
    )(hn, *du_parts, dzs, dq, dk, dv, dza, runs_after)


def _in_proj_bwd(x2d, dh1, g1, w_in_t, d_proj, runs_after):
    rows = x2d.shape[0]
    tm = 512

    def body(x_ref, dh1_ref, g_ref, w_ref, dproj_ref, _, gx_ref, dw_ref, dg_ref):
        @pl.when(pl.program_id(0) == 0)
        def _():
            dw_ref[...] = jnp.zeros_like(dw_ref)
            dg_ref[...] = jnp.zeros_like(dg_ref)

        x = x_ref[...]
        g = g_ref[...]
        r = lax.rsqrt(jnp.mean(x * x, axis=-1, keepdims=True) + EPS)
        xr = x * r
        hn = (xr[:, _EARLY_COLS:] * g[:, _EARLY_COLS:]).astype(BF16)
        d_proj = dproj_ref[...]
        dhn = _dot(d_proj, w_ref[...])
        dw_ref[...] += _dot_tn(d_proj, hn)
        dg_ref[...] += jnp.sum(dhn * xr, axis=0, keepdims=True)
        a_ = dhn * g
        gx_ref[...] = dh1_ref[...] + r * a_ - xr * (r * jnp.mean(a_ * xr, axis=-1, keepdims=True))

    late_cols = D_MODEL - _EARLY_COLS
    return _call(
        body, name="in_proj_bwd", grid=(rows // tm,),
        in_specs=[_rows(tm, D_MODEL), _rows(tm, D_MODEL), _full((1, D_MODEL)), _full((D_IN, D_MODEL)), _rows(tm, D_IN),
                  pl.BlockSpec(memory_space=pl.ANY)],
        out_specs=[_rows(tm, D_MODEL), _full((D_IN, late_cols)), _full((1, D_MODEL))],
        out_shape=[_sds((rows, D_MODEL)), _sds((D_IN, late_cols)), _sds((1, D_MODEL))],
        compiler_params=_params(52, ("arbitrary",)),
    )(x2d, dh1, g1, w_in_t, d_proj, runs_after)


def _iota(shape, axis):
    return lax.broadcasted_iota(jnp.int32, shape, axis)


def _sum_of_thirds(f, a):
    hi = a.astype(BF16)
    rest = a - hi.astype(F32)
    mid = rest.astype(BF16)
    low = (rest - mid.astype(F32)).astype(BF16)
    return (f(hi) + f(mid)) + f(low)


@jax.custom_vjp
def _pick_rows(e, a):
    return _sum_of_thirds(lambda part: _dot(e, part), a)


def _pick_rows_fwd(e, a):
    return _pick_rows(e, a), e


def _pick_rows_bwd(e, ct):
    return jnp.zeros_like(e), _sum_of_thirds(lambda part: _dot_tn(e, part), ct)


_pick_rows.defvjp(_pick_rows_fwd, _pick_rows_bwd)


@jax.custom_vjp
def _pick_cols(a, e):
    return _sum_of_thirds(lambda part: _dot(part, e), a)


def _pick_cols_fwd(a, e):
    return _pick_cols(a, e), e


def _pick_cols_bwd(e, ct):
    return _sum_of_thirds(lambda part: _dot_nt(part, e), ct), jnp.zeros_like(e)


_pick_cols.defvjp(_pick_cols_fwd, _pick_cols_bwd)


_HALF_GROUPS = SSM_GROUPS // 2
_N_SHIFT = SSM_STATE.bit_length() - 1
_P_SHIFT = SSM_GROUP_CH.bit_length() - 1


def _s5_operands(lam_re, lam_im, log_step, b_re, b_im, c_re, c_im):
    g, n, p = SSM_GROUPS, SSM_STATE, SSM_GROUP_CH
    gn, gp, hn_, hp = g * n, g * p, _HALF_GROUPS * n, _HALF_GROUPS * p
    eye_g = _iota((g, g), 0) == _iota((g, g), 1)
    step = jnp.sum(jnp.where(eye_g, jnp.exp(log_step), 0.0), axis=1, keepdims=True)
    a_re = lam_re * step
    a_im = lam_im * step
    mag = jnp.exp(a_re)
    lbar_re = mag * jnp.cos(a_im)
    lbar_im = mag * jnp.sin(a_im)
    n_re = lbar_re - 1.0
    den = lam_re * lam_re + lam_im * lam_im
    f_re = (n_re * lam_re + lbar_im * lam_im) / den
    f_im = (lbar_im * lam_re - n_re * lam_im) / den

    spread_n = (_iota((n, gn), 0) == (_iota((n, gn), 1) & (n - 1))).astype(BF16)
    own_g = _iota((g, gn), 0) == (_iota((g, gn), 1) >> _N_SHIFT)

    def to_row(a):
        return jnp.sum(jnp.where(own_g, _pick_cols(a, spread_n), 0.0), axis=0, keepdims=True)

    per_group = ((_iota((gp, g), 0) >> _P_SHIFT) == _iota((gp, g), 1)).astype(BF16)
    fx_re, fx_im = _pick_rows(per_group, f_re), _pick_rows(per_group, f_im)
    bbar_re = fx_re * b_re - fx_im * b_im
    bbar_im = fx_re * b_im + fx_im * b_re

    tile_n = (_iota((n, hn_), 0) == (_iota((n, hn_), 1) & (n - 1))).astype(BF16)
    same_group = (_iota((hp, hn_), 0) >> _P_SHIFT) == (_iota((hp, hn_), 1) >> _N_SHIFT)

    def embed(a, hf):
        return jnp.where(same_group, _pick_cols(a[hf * hp:(hf + 1) * hp], tile_n), 0.0)

    return (to_row(lbar_re), to_row(lbar_im), embed(bbar_re, 0), embed(bbar_re, 1), embed(bbar_im, 0),
            embed(bbar_im, 1), embed(c_re, 0), embed(c_re, 1), embed(c_im, 0), embed(c_im, 1))


_S5_PARAM_SHAPES = ((SSM_GROUPS, SSM_STATE), (SSM_GROUPS, SSM_STATE), (1, SSM_GROUPS),
                    (D_SSM, SSM_STATE), (D_SSM, SSM_STATE), (D_SSM, SSM_STATE), (D_SSM, SSM_STATE))
_CM_SHAPE = (2, _HALF_GROUPS * SSM_GROUP_CH, _HALF_GROUPS * SSM_STATE)
_S5_OPERAND_SHAPES = ((1, SSM_LANES), (1, SSM_LANES), _CM_SHAPE, _CM_SHAPE, _CM_SHAPE, _CM_SHAPE)


def _s5_params_fwd(*params):
    def body(*refs):
        ins, (lre_ref, lim_ref, btre_ref, btim_ref, cmre_ref, cmim_ref) = refs[:7], refs[7:]
        vals = _s5_operands(*[r[...] for r in ins])
        lre_ref[...] = vals[0]
        lim_ref[...] = vals[1]
        for ref, pair in zip((btre_ref, btim_ref, cmre_ref, cmim_ref), (vals[2:4], vals[4:6], vals[6:8], vals[8:10])):
            ref[0] = pair[0].astype(BF16)
            ref[1] = pair[1].astype(BF16)

    dtypes = (F32, F32, BF16, BF16, BF16, BF16)
    return _call(
        body, name="s5_params_fwd",
        in_specs=[_full(s) for s in _S5_PARAM_SHAPES], out_specs=[_full(s) for s in _S5_OPERAND_SHAPES],
        out_shape=[_sds(s, d) for s, d in zip(_S5_OPERAND_SHAPES, dtypes)], compiler_params=_params(32),
    )(*params)


_BC_SIDE_BY_SIDE = (D_SSM, 4 * SSM_STATE)


def _s5_params_bwd(params, cotangents, runs_after):
    def body(*refs):
        ins, (dlre, dlim, dbtre, dbtim, dcmre, dcmim), outs = refs[:7], refs[7:13], refs[14:]
        _, vjp = jax.vjp(_s5_operands, *[r[...] for r in ins])
        cts = (dlre[...], dlim[...], dbtre[0], dbtre[1], dbtim[0], dbtim[1], dcmre[0], dcmre[1], dcmim[0], dcmim[1])
        grads = vjp(cts)
        for ref, val in zip(outs[:3], grads[:3]):
            ref[...] = val
        outs[3][...] = jnp.concatenate(grads[3:], axis=1).astype(BF16)

    out_shapes = _S5_PARAM_SHAPES[:3] + (_BC_SIDE_BY_SIDE,)
    return _call(
        body, name="s5_params_bwd",
        in_specs=[_full(s) for s in _S5_PARAM_SHAPES + _S5_OPERAND_SHAPES] + [pl.BlockSpec(memory_space=pl.ANY)],
        out_specs=[_full(s) for s in out_shapes],
        out_shape=[_sds(s, d) for s, d in zip(out_shapes, (F32, F32, F32, BF16))], compiler_params=_params(48),
    )(*params, *cotangents, runs_after)


def _scan_geometry(n_seq, seq):
    slab = n_seq * SCAN_CHUNKS
    steps = seq // SCAN_CHUNKS
    tile_rows = slab * SCAN_TILE_STEPS
    n_tiles = steps // SCAN_TILE_STEPS
    return slab, steps, tile_rows, n_tiles


_SCAN_PARTS = D_SSM // LANES


def _whole_parts(rows):
    return [_full((rows, LANES))] * _SCAN_PARTS


def _part_shapes(rows):
    return [_sds((rows, LANES))] * _SCAN_PARTS


def _load_chunks(parts, first_chunk, n_chunks, steps, slab):
    return jnp.concatenate([
        jnp.concatenate([ref[pl.ds(first_chunk + q, steps, stride=slab), :] for ref in parts], axis=1)
        for q in range(n_chunks)], axis=0)


def _store_chunks(parts, first_chunk, value, steps, slab):
    for q in range(value.shape[0] // steps):
        for j, ref in enumerate(parts):
            ref[pl.ds(first_chunk + q, steps, stride=slab), :] = value[q * steps:(q + 1) * steps,
                                                                     j * LANES:(j + 1) * LANES]


def _join_parts(parts):
    return jnp.concatenate([ref[...] for ref in parts], axis=1)


def _split_parts(parts, value):
    for j, ref in enumerate(parts):
        ref[...] = value[:, j * LANES:(j + 1) * LANES]


def _complex_power(re, im, n):
    out = None
    while n:
        if n & 1:
            out = (re, im) if out is None else (out[0] * re - out[1] * im, out[0] * im + out[1] * re)
        n >>= 1
        if n:
            re, im = re * re - im * im, 2.0 * re * im
    return out


def _chunk_carry(sum_re, sum_im, carry_re, carry_im, a_re, a_im, n_seq, reverse):
    carry_re[...] = jnp.zeros_like(carry_re)
    carry_im[...] = jnp.zeros_like(carry_im)
    for s in range(n_seq):
        order = range(SCAN_CHUNKS - 2, -1, -1) if reverse else range(1, SCAN_CHUNKS)
        for c in order:
            r = s * SCAN_CHUNKS + c
            p = r + 1 if reverse else r - 1
            p_re, p_im = carry_re[p:p + 1, :], carry_im[p:p + 1, :]
            carry_re[r:r + 1, :] = a_re * p_re - a_im * p_im + sum_re[p:p + 1, :]
            carry_im[r:r + 1, :] = a_re * p_im + a_im * p_re + sum_im[p:p + 1, :]


def _s5_scan_fwd(u_parts, bt_re, bt_im, cm_re, cm_im, lbar_re, lbar_im, d_row, n_seq, seq):
    slab, steps, tile_rows, n_tiles = _scan_geometry(n_seq, seq)
    rows = u_parts[0].shape[0]

    def body(*refs):
        u_refs, refs = refs[:_SCAN_PARTS], refs[_SCAN_PARTS:]
        (bre_ref, bim_ref, cre_ref, cim_ref, lre_ref, lim_ref, d_ref), refs = refs[:7], refs[7:]
        y_refs, (hre_ref, him_ref, st_re, st_im, h0_re, h0_im, buf_re, buf_im) = refs[:_SCAN_PARTS], refs[_SCAN_PARTS:]
        second = pl.program_id(0) == 1
        i = pl.program_id(1)

        @pl.when(jnp.logical_and(i == 0, jnp.logical_not(second)))
        def _():
            st_re[...] = jnp.zeros_like(st_re)
            st_im[...] = jnp.zeros_like(st_im)

        u = _join_parts(u_refs)
        ub = u.astype(BF16)
        for hf in range(2):
            cols = slice(hf * 1024, (hf + 1) * 1024)
            buf_re[:, cols] = _dot(ub[:, hf * 256:(hf + 1) * 256], bre_ref[hf])
            buf_im[:, cols] = _dot(ub[:, hf * 256:(hf + 1) * 256], bim_ref[hf])

        for lc in range(SSM_LANES // SCAN_LANE_CHUNK):
            cols = slice(lc * SCAN_LANE_CHUNK, (lc + 1) * SCAN_LANE_CHUNK)
            l_re = jnp.broadcast_to(lre_ref[:, cols], (slab, SCAN_LANE_CHUNK))
            l_im = jnp.broadcast_to(lim_ref[:, cols], (slab, SCAN_LANE_CHUNK))

            def scan_tile(keep_states):
                def step(t, carry):
                    s_re, s_im = carry
                    r0 = pl.multiple_of(t * slab, slab)
                    n_re = l_re * s_re - l_im * s_im + buf_re[pl.ds(r0, slab), cols]
                    n_im = l_re * s_im + l_im * s_re + buf_im[pl.ds(r0, slab), cols]
                    if keep_states:
                        buf_re[pl.ds(r0, slab), cols] = n_re
                        buf_im[pl.ds(r0, slab), cols] = n_im
                    return n_re, n_im

                s_re, s_im = lax.fori_loop(0, SCAN_TILE_STEPS, step, (st_re[:, cols], st_im[:, cols]), unroll=True)
                st_re[:, cols] = s_re
                st_im[:, cols] = s_im

            pl.when(jnp.logical_not(second))(lambda: scan_tile(False))
            pl.when(second)(lambda: scan_tile(True))

        @pl.when(jnp.logical_and(i == n_tiles - 1, jnp.logical_not(second)))
        def _():
            a_re, a_im = _complex_power(lre_ref[...], lim_ref[...], steps)
            _chunk_carry(st_re, st_im, h0_re, h0_im, a_re, a_im, n_seq, reverse=False)
            st_re[...] = h0_re[...]
            st_im[...] = h0_im[...]

        @pl.when(second)
        def _():
            h_re = buf_re[...].astype(BF16)
            h_im = buf_im[...].astype(BF16)
            hre_ref[...] = h_re
            him_ref[...] = h_im
            for hf in range(2):
                cols = slice(hf * 1024, (hf + 1) * 1024)
                ycols = slice(hf * 256, (hf + 1) * 256)
                y_half = (_dot_nt(h_re[:, cols], cre_ref[hf]) - _dot_nt(h_im[:, cols], cim_ref[hf])
                          + d_ref[:, ycols] * u[:, ycols])
                _split_parts(y_refs[2 * hf:2 * hf + 2], y_half)

    tile = lambda w: pl.BlockSpec((tile_rows, w), lambda p, i: (i, 0))
    out_tile = lambda w: pl.BlockSpec((tile_rows, w), lambda p, i: (i * p, 0))
    cm = _full(_CM_SHAPE)
    outs = _call(
        body, name="s5_scan_fwd", grid=(2, n_tiles),
        in_specs=[tile(LANES)] * _SCAN_PARTS + [cm, cm, cm, cm, _full((1, SSM_LANES)), _full((1, SSM_LANES)),
                                                _full((1, 512))],
        out_specs=[out_tile(LANES)] * _SCAN_PARTS + [out_tile(SSM_LANES), out_tile(SSM_LANES)],
        out_shape=_part_shapes(rows) + [_sds((rows, SSM_LANES), BF16), _sds((rows, SSM_LANES), BF16)],
        scratch_shapes=[pltpu.VMEM((slab, SSM_LANES), F32)] * 4 + [pltpu.VMEM((tile_rows, SSM_LANES), F32)] * 2,
        compiler_params=_params(40, ("arbitrary", "arbitrary")),
    )(*u_parts, bt_re, bt_im, cm_re, cm_im, lbar_re, lbar_im, d_row)
    return outs[:_SCAN_PARTS], outs[_SCAN_PARTS], outs[_SCAN_PARTS + 1]


def _s5_scan_bwd(dy_parts, u_parts, h_re, h_im, bt_re, bt_im, cm_re, cm_im, lbar_re, lbar_im, d_row, n_seq, seq):
    slab, steps, tile_rows, n_tiles = _scan_geometry(n_seq, seq)
    rows = u_parts[0].shape[0]

    def body(*refs):
        dy_refs, u_refs, refs = refs[:_SCAN_PARTS], refs[_SCAN_PARTS:2 * _SCAN_PARTS], refs[2 * _SCAN_PARTS:]
        (hre_ref, him_ref, bre_ref, bim_ref, cre_ref, cim_ref, lre_ref, lim_ref, d_ref), refs = refs[:9], refs[9:]
        du_refs, refs = refs[:_SCAN_PARTS], refs[_SCAN_PARTS:]
        (dbre_ref, dbim_ref, dcre_ref, dcim_ref, dlre_ref, dlim_ref, dd_ref,
         st_re, st_im, g0_re, g0_im, acc_re, acc_im, buf_re, buf_im) = refs
        second = pl.program_id(0) == 1
        i = pl.program_id(1)

        @pl.when(jnp.logical_and(i == 0, jnp.logical_not(second)))
        def _():
            st_re[...] = jnp.zeros_like(st_re)
            st_im[...] = jnp.zeros_like(st_im)
            acc_re[...] = jnp.zeros_like(acc_re)
            acc_im[...] = jnp.zeros_like(acc_im)
            for ref in (dbre_ref, dbim_ref, dcre_ref, dcim_ref, dd_ref):
                ref[...] = jnp.zeros_like(ref)

        dy = _join_parts(dy_refs)
        dyb = dy.astype(BF16)
        for hf in range(2):
            cols = slice(hf * 1024, (hf + 1) * 1024)
            buf_re[:, cols] = _dot(dyb[:, hf * 256:(hf + 1) * 256], cre_ref[hf])
            buf_im[:, cols] = -_dot(dyb[:, hf * 256:(hf + 1) * 256], cim_ref[hf])

        for lc in range(SSM_LANES // SCAN_LANE_CHUNK):
            cols = slice(lc * SCAN_LANE_CHUNK, (lc + 1) * SCAN_LANE_CHUNK)
            l_re = jnp.broadcast_to(lre_ref[:, cols], (slab, SCAN_LANE_CHUNK))
            l_im = jnp.broadcast_to(lim_ref[:, cols], (slab, SCAN_LANE_CHUNK))

            def advance(r0, s_re, s_im):
                n_re = l_re * s_re + l_im * s_im + buf_re[pl.ds(r0, slab), cols]
                n_im = l_re * s_im - l_im * s_re + buf_im[pl.ds(r0, slab), cols]
                buf_re[pl.ds(r0, slab), cols] = n_re
                buf_im[pl.ds(r0, slab), cols] = n_im
                return n_re, n_im

            def row0(k):
                return pl.multiple_of((SCAN_TILE_STEPS - 1 - k) * slab, slab)

            @pl.when(jnp.logical_not(second))
            def _():
                s_re, s_im = lax.fori_loop(0, SCAN_TILE_STEPS, lambda k, s: advance(row0(k), *s),
                                           (st_re[:, cols], st_im[:, cols]), unroll=True)
                st_re[:, cols] = s_re
                st_im[:, cols] = s_im

            @pl.when(second)
            def _():
                def step(k, carry):
                    s_re, s_im, a_re, a_im = carry
                    r0 = row0(k)
                    hr = hre_ref[pl.ds(r0, slab), cols].astype(F32)
                    hi = him_ref[pl.ds(r0, slab), cols].astype(F32)
                    a_re = a_re + s_re * hr + s_im * hi
                    a_im = a_im + s_im * hr - s_re * hi
                    return advance(r0, s_re, s_im) + (a_re, a_im)

                zero = jnp.zeros((slab, SCAN_LANE_CHUNK), F32)
                s_re, s_im, a_re, a_im = lax.fori_loop(
                    0, SCAN_TILE_STEPS, step, (st_re[:, cols], st_im[:, cols], zero, zero), unroll=True)
                st_re[:, cols] = s_re
                st_im[:, cols] = s_im
                acc_re[:, cols] += a_re
                acc_im[:, cols] += a_im

        @pl.when(jnp.logical_and(i == n_tiles - 1, jnp.logical_not(second)))
        def _():
            p_re, p_im = _complex_power(lre_ref[...], lim_ref[...], steps)
            _chunk_carry(st_re, st_im, g0_re, g0_im, p_re, -p_im, n_seq, reverse=True)
            st_re[...] = g0_re[...]
            st_im[...] = g0_im[...]

        @pl.when(second)
        def _():
            u = _join_parts(u_refs)
            ub = u.astype(BF16)
            g_re = buf_re[...].astype(BF16)
            g_im = buf_im[...].astype(BF16)
            dd_ref[...] += jnp.sum(dy * u, axis=0, keepdims=True)
            for hf in range(2):
                cols = slice(hf * 1024, (hf + 1) * 1024)
                ycols = slice(hf * 256, (hf + 1) * 256)
                du_half = (_dot_nt(g_re[:, cols], bre_ref[hf]) + _dot_nt(g_im[:, cols], bim_ref[hf])
                           + d_ref[:, ycols] * dy[:, ycols])
                _split_parts(du_refs[2 * hf:2 * hf + 2], du_half)
                for q4 in range(_HALF_GROUPS // 4):
                    ch = slice(hf * 256 + q4 * 64, hf * 256 + (q4 + 1) * 64)
                    st = slice(hf * 1024 + q4 * 256, hf * 1024 + (q4 + 1) * 256)
                    blk = (hf, slice(q4 * 64, (q4 + 1) * 64), slice(q4 * 256, (q4 + 1) * 256))
                    dbre_ref[blk] += _dot_tn(ub[:, ch], g_re[:, st])
                    dbim_ref[blk] += _dot_tn(ub[:, ch], g_im[:, st])
                    dcre_ref[blk] += _dot_tn(dyb[:, ch], hre_ref[:, st])
                    dcim_ref[blk] -= _dot_tn(dyb[:, ch], him_ref[:, st])

        @pl.when(jnp.logical_and(i == n_tiles - 1, second))
        def _():
            dlre_ref[...] = jnp.sum(acc_re[...], axis=0, keepdims=True)
            dlim_ref[...] = jnp.sum(acc_im[...], axis=0, keepdims=True)

    tile = lambda w: pl.BlockSpec((tile_rows, w), lambda p, i: (n_tiles - 1 - i, 0))
    second_tile = lambda w: pl.BlockSpec((tile_rows, w), lambda p, i: (n_tiles - 1 - i * p, 0))
    cm = _full(_CM_SHAPE)
    row = _full((1, SSM_LANES))
    outs = _call(
        body, name="s5_scan_bwd", grid=(2, n_tiles),
        in_specs=[tile(LANES)] * _SCAN_PARTS + [second_tile(LANES)] * _SCAN_PARTS
        + [second_tile(SSM_LANES), second_tile(SSM_LANES), cm, cm, cm, cm, row, row, _full((1, 512))],
        out_specs=[second_tile(LANES)] * _SCAN_PARTS + [cm, cm, cm, cm, row, row, _full((1, 512))],
        out_shape=(_part_shapes(rows) + [_sds(_CM_SHAPE)] * 4 + [_sds((1, SSM_LANES))] * 2 + [_sds((1, 512))]),
        scratch_shapes=[pltpu.VMEM((slab, SSM_LANES), F32)] * 6 + [pltpu.VMEM((tile_rows, SSM_LANES), F32)] * 2,
        compiler_params=_params(48, ("arbitrary", "arbitrary")),
    )(*dy_parts, *u_parts, h_re, h_im, bt_re, bt_im, cm_re, cm_im, lbar_re, lbar_im, d_row)
    return (outs[:_SCAN_PARTS],) + tuple(outs[_SCAN_PARTS:])


def _glu_gate(gl, a, zs):
    return gl * jax.nn.sigmoid(a) * _silu(zs)


def _glu_fwd(y_parts, zs, w_glu, b_glu, n_seq, seq):
    rows = zs.shape[0]
    tm = 1024
    slab, steps, _, _ = _scan_geometry(n_seq, seq)

    def body(*refs):
        y_refs, (zs_ref, w_ref, b_ref, o_ref) = refs[:_SCAN_PARTS], refs[_SCAN_PARTS:]
        y = _load_chunks(y_refs, pl.program_id(0) * (tm // steps), tm // steps, steps, slab)
        gl = jax.nn.gelu(y)
        a = _dot(gl.astype(BF16), w_ref[...]) + b_ref[...]
        o_ref[...] = _glu_gate(gl, a, zs_ref[...]).astype(BF16)

    return _call(
        body, name="glu_fwd", grid=(rows // tm,),
        in_specs=_whole_parts(rows) + [_rows(tm, 512), _full((512, 512)), _full((1, 512))],
        out_specs=_rows(tm, 512), out_shape=_sds((rows, 512), BF16),
        compiler_params=_params(32, ("arbitrary",)),
    )(*y_parts, zs, w_glu, b_glu)


def _glu_bwd(y_parts, zs, d_out, w_glu, b_glu, n_seq, seq):
    rows = zs.shape[0]
    tm = 512
    slab, steps, _, _ = _scan_geometry(n_seq, seq)

    def body(*refs):
        y_refs, (zs_ref, d_ref, w_ref, b_ref), refs = refs[:_SCAN_PARTS], refs[_SCAN_PARTS:_SCAN_PARTS + 4], refs[_SCAN_PARTS + 4:]
        dy_refs, (dzs_ref, dw_ref, db_ref) = refs[:_SCAN_PARTS], refs[_SCAN_PARTS:]
        first_chunk = pl.program_id(0) * (tm // steps)

        @pl.when(pl.program_id(0) == 0)
        def _():
            dw_ref[...] = jnp.zeros_like(dw_ref)
            db_ref[...] = jnp.zeros_like(db_ref)

        gl, gelu_vjp = jax.vjp(jax.nn.gelu, _load_chunks(y_refs, first_chunk, tm // steps, steps, slab))
        glb = gl.astype(BF16)
        a = _dot(glb, w_ref[...]) + b_ref[...]
        _, gate_vjp = jax.vjp(_glu_gate, gl, a, zs_ref[...])
        d_gl, d_a, d_zs = gate_vjp(d_ref[...])
        dab = d_a.astype(BF16)
        d_gl = d_gl + _dot_nt(dab, w_ref[...])
        _store_chunks(dy_refs, first_chunk, gelu_vjp(d_gl)[0], steps, slab)
        dzs_ref[...] = d_zs.astype(BF16)
        dw_ref[...] += _dot_tn(glb, dab)
        db_ref[...] += jnp.sum(d_a, axis=0, keepdims=True)

    *dy_parts, dzs, dw, db = _call(
        body, name="glu_bwd", grid=(rows // tm,),
        in_specs=_whole_parts(rows) + [_rows(tm, 512), _rows(tm, 512), _full((512, 512)), _full((1, 512))],
        out_specs=_whole_parts(rows) + [_rows(tm, 512), _full((512, 512)), _full((1, 512))],
        out_shape=_part_shapes(rows) + [_sds((rows, 512), BF16), _sds((512, 512)), _sds((1, 512))],
        compiler_params=_params(40, ("arbitrary",)),
    )(*y_parts, zs, d_out, w_glu, b_glu)
    return dy_parts, dzs, dw, db


_GROUP_ROWS = Q_PER_KV * BLOCK
_BLOCK_SHIFT = BLOCK.bit_length() - 1


def _attn_bias(j):
    query = _iota((BLOCK, _GROUP_ROWS), 1)
    dist_cur = (query & (BLOCK - 1)) - _iota((BLOCK, _GROUP_ROWS), 0)
    dist_prev = dist_cur + BLOCK
    head = query >> _BLOCK_SHIFT
    slope = jnp.zeros((BLOCK, _GROUP_ROWS), F32)
    for g in range(Q_PER_KV):
        slope = jnp.where(head == g, 2.0 ** (-(j * Q_PER_KV + g + 1)), slope)
    bias_cur = jnp.where(dist_cur >= 0, -slope * dist_cur.astype(F32), -jnp.inf)
    bias_prev = jnp.where(dist_prev < WINDOW, -slope * dist_prev.astype(F32), -jnp.inf)
    return bias_cur, bias_prev


_ATTN_BIAS_SCRATCH = pltpu.VMEM((KV_HEADS, 2, BLOCK, _GROUP_ROWS), F32)


def _fill_attn_bias(bias_ref):
    @pl.when(jnp.logical_and(pl.program_id(0) == 0, pl.program_id(1) == 0))
    def _():
        for j in range(KV_HEADS):
            bias_ref[j, 0], bias_ref[j, 1] = _attn_bias(j)


def _stack_heads(x, j):
    heads = range(j * Q_PER_KV, (j + 1) * Q_PER_KV)
    return jnp.concatenate([x[:, h * HEAD_DIM:(h + 1) * HEAD_DIM] for h in heads], axis=0)


def _head_rows(x, j):
    heads = range(j * Q_PER_KV, (j + 1) * Q_PER_KV)
    return jnp.concatenate([x[h:h + 1, :] for h in heads], axis=1)


def _sink_row(sk_ref, j):
    heads = range(j * Q_PER_KV, (j + 1) * Q_PER_KV)
    return jnp.concatenate([jnp.broadcast_to(sk_ref[0:1, h:h + 1], (1, BLOCK)) for h in heads], axis=1)


_ATTN_FWD_BLOCKS = 8


def _attn_fwd(q, k, v, za, sinks, n_seq, seq):
    nb = seq // BLOCK
    steps = nb // _ATTN_FWD_BLOCKS
    rows = q.shape[0]

    def body(q_ref, kc_ref, kp_ref, vc_ref, vp_ref, za_ref, sk_ref, o_ref, ao_ref, lse_ref, bias_ref):
        _fill_attn_bias(bias_ref)
        for t in range(_ATTN_FWD_BLOCKS):
            at = slice(t * BLOCK, (t + 1) * BLOCK)
            before = slice((t - 1) * BLOCK, t * BLOCK)
            q_all = q_ref[at, :]
            for j in range(KV_HEADS):
                js = slice(j * HEAD_DIM, (j + 1) * HEAD_DIM)
                bias_c, bias_p = bias_ref[j, 0], bias_ref[j, 1]
                q4 = _stack_heads(q_all, j)
                sc = _dot_nt(kc_ref[at, js], q4) + bias_c
                if t == 0:
                    sp = _dot_nt(kp_ref[:, js], q4) + jnp.where(pl.program_id(1) > 0, bias_p, -jnp.inf)
                    v_prev = vp_ref[:, js]
                else:
                    sp = _dot_nt(kc_ref[before, js], q4) + bias_p
                    v_prev = vc_ref[before, js]
                sink = _sink_row(sk_ref, j)
                m = jnp.maximum(jnp.max(jnp.maximum(sc, sp), axis=0, keepdims=True), sink)
                ec = jnp.exp(sc - m)
                ep = jnp.exp(sp - m)
                den = jnp.sum(ec + ep, axis=0, keepdims=True) + jnp.exp(sink - m)
                inv = 1.0 / den
                o4 = _dot_tn((ec * inv).astype(BF16), vc_ref[at, js]) + _dot_tn((ep * inv).astype(BF16), v_prev)
                lse4 = m + jnp.log(den)
                for g in range(Q_PER_KV):
                    h = j * Q_PER_KV + g
                    o_ref[at, h * HEAD_DIM:(h + 1) * HEAD_DIM] = o4[g * BLOCK:(g + 1) * BLOCK]
                    lse_ref[t * N_HEADS + h:t * N_HEADS + h + 1, :] = lse4[:, g * BLOCK:(g + 1) * BLOCK]
        ao_ref[...] = (o_ref[...] * _silu(za_ref[...])).astype(BF16)

    cur = lambda w: pl.BlockSpec((_ATTN_FWD_BLOCKS * BLOCK, w), lambda b, n: (b * steps + n, 0))
    prev = lambda w: pl.BlockSpec((BLOCK, w), lambda b, n: (b * nb + jnp.maximum(_ATTN_FWD_BLOCKS * n - 1, 0), 0))
    lse_rows = rows // BLOCK * N_HEADS
    return _call(
        body, name="attn_fwd", grid=(n_seq, steps),
        in_specs=[cur(512), cur(128), prev(128), cur(128), prev(128), cur(512), _full((1, N_HEADS))],
        out_specs=[cur(512), cur(512),
                   pl.BlockSpec((_ATTN_FWD_BLOCKS * N_HEADS, BLOCK), lambda b, n: (b * steps + n, 0))],
        out_shape=[_sds((rows, 512)), _sds((rows, 512), BF16), _sds((lse_rows, BLOCK))],
        scratch_shapes=[_ATTN_BIAS_SCRATCH], compiler_params=_params(32, ("arbitrary", "arbitrary")),
    )(q, k, k, v, v, za, sinks)


def _in_proj_attn_fwd(hn, w_in_t, sinks, n_seq, seq):
    rows = hn.shape[0]
    tm = _ATTN_FWD_BLOCKS * BLOCK
    tiles_per_seq = seq // tm
    slab, steps, _, _ = _scan_geometry(n_seq, seq)

    def body(hn_ref, w_ref, sk_ref, *refs):
        u_parts = refs[:_SCAN_PARTS]
        (zs_ref, q_ref, k_ref, v_ref, za_ref, o_ref, ao_ref, lse_ref, bias_ref, k_last, v_last) = refs[_SCAN_PARTS:]
        i = pl.program_id(0)

        @pl.when(i == 0)
        def _():
            for j in range(KV_HEADS):
                bias_ref[j, 0], bias_ref[j, 1] = _attn_bias(j)
            k_last[...] = jnp.zeros_like(k_last)
            v_last[...] = jnp.zeros_like(v_last)

        for_attn = _dot_nt(hn_ref[...], w_ref[2 * D_SSM:, :])
        q_ref[...] = (for_attn[:, 0:512] * ATTN_SCALE).astype(BF16)
        k_ref[...] = for_attn[:, 512:640].astype(BF16)
        v_ref[...] = for_attn[:, 640:768].astype(BF16)
        za_ref[...] = for_attn[:, 768:1280]

        has_before = lax.rem(i, tiles_per_seq) > 0
        for t in range(_ATTN_FWD_BLOCKS):
            at = slice(t * BLOCK, (t + 1) * BLOCK)
            before = slice((t - 1) * BLOCK, t * BLOCK)
            q_all = q_ref[at, :]
            for j in range(KV_HEADS):
                js = slice(j * HEAD_DIM, (j + 1) * HEAD_DIM)
                bias_c, bias_p = bias_ref[j, 0], bias_ref[j, 1]
                q4 = _stack_heads(q_all, j)
                sc = _dot_nt(k_ref[at, js], q4) + bias_c
                if t == 0:
                    sp = _dot_nt(k_last[:, js], q4) + jnp.where(has_before, bias_p, -jnp.inf)
                    v_prev = v_last[:, js]
                else:
                    sp = _dot_nt(k_ref[before, js], q4) + bias_p
                    v_prev = v_ref[before, js]
                sink = _sink_row(sk_ref, j)
                m = jnp.maximum(jnp.max(jnp.maximum(sc, sp), axis=0, keepdims=True), sink)
                ec = jnp.exp(sc - m)
                ep = jnp.exp(sp - m)
                den = jnp.sum(ec + ep, axis=0, keepdims=True) + jnp.exp(sink - m)
                inv = 1.0 / den
                o4 = _dot_tn((ec * inv).astype(BF16), v_ref[at, js]) + _dot_tn((ep * inv).astype(BF16), v_prev)
                lse4 = m + jnp.log(den)
                for g in range(Q_PER_KV):
                    h = j * Q_PER_KV + g
                    o_ref[at, h * HEAD_DIM:(h + 1) * HEAD_DIM] = o4[g * BLOCK:(g + 1) * BLOCK]
                    lse_ref[t * N_HEADS + h:t * N_HEADS + h + 1, :] = lse4[:, g * BLOCK:(g + 1) * BLOCK]
        ao_ref[...] = (o_ref[...] * _silu(za_ref[...])).astype(BF16)
        k_last[...] = k_ref[tm - BLOCK:, :]
        v_last[...] = v_ref[tm - BLOCK:, :]

        for_ssm = _dot_nt(hn_ref[...], w_ref[0:2 * D_SSM, :])
        _store_chunks(u_parts, i * (tm // steps), for_ssm[:, 0:D_SSM], steps, slab)
        zs_ref[...] = for_ssm[:, D_SSM:]

    lse_rows = rows // BLOCK * N_HEADS
    *u_parts, zs, q, k, v, za, o, attn_out, lse = _call(
        body, name="in_proj_attn_fwd", grid=(rows // tm,),
        in_specs=[_rows(tm, D_MODEL), _full((D_IN, D_MODEL)), _full((1, N_HEADS))],
        out_specs=_whole_parts(rows) + [_rows(tm, 512), _rows(tm, 512), _rows(tm, 128), _rows(tm, 128), _rows(tm, 512),
                                        _rows(tm, 512), _rows(tm, 512),
                                        pl.BlockSpec((_ATTN_FWD_BLOCKS * N_HEADS, BLOCK), lambda i: (i, 0))],
        out_shape=_part_shapes(rows) + [_sds((rows, 512)), _sds((rows, 512), BF16), _sds((rows, 128), BF16),
                                        _sds((rows, 128), BF16), _sds((rows, 512)), _sds((rows, 512)),
                                        _sds((rows, 512), BF16), _sds((lse_rows, BLOCK))],
        scratch_shapes=[_ATTN_BIAS_SCRATCH, pltpu.VMEM((BLOCK, KV_HEADS * HEAD_DIM), BF16),
                        pltpu.VMEM((BLOCK, KV_HEADS * HEAD_DIM), BF16)],
        compiler_params=_params(56, ("arbitrary",)),
    )(hn, w_in_t, sinks)
    return u_parts, zs, q, k, v, za, o, attn_out, lse


_ATTN_BWD_BLOCKS = 4


def _attn_bwd(q, k, v, za, o, lse, d_ao, sinks, n_seq, seq):
    nb = seq // BLOCK
    per_step = _ATTN_BWD_BLOCKS
    steps = nb // per_step
    rows = q.shape[0]

    def body(q_ref, kc_ref, kp_ref, vc_ref, vp_ref, za_ref, o_ref, lse_ref, d_ref, sk_ref,
             dq_ref, dk_ref, dv_ref, dza_ref, dsk_ref, bias_ref, dk_carry, dv_carry):
        step = pl.program_id(1)
        first_block = nb - per_step * (step + 1)
        _fill_attn_bias(bias_ref)

        @pl.when(jnp.logical_and(pl.program_id(0) == 0, step == 0))
        def _():
            dsk_ref[...] = jnp.zeros_like(dsk_ref)
            dk_carry[...] = jnp.zeros_like(dk_carry)
            dv_carry[...] = jnp.zeros_like(dv_carry)

        _, gate_vjp = jax.vjp(lambda o_, z_: o_ * _silu(z_), o_ref[...], za_ref[...])
        d_o, d_za = gate_vjp(d_ref[...])
        dza_ref[...] = d_za.astype(BF16)

        for j in range(KV_HEADS):
            js = slice(j * HEAD_DIM, (j + 1) * HEAD_DIM)
            bias_c, bias_p = bias_ref[j, 0], bias_ref[j, 1]
            sink = _sink_row(sk_ref, j)
            sink_loss = jnp.zeros((1, _GROUP_ROWS), F32)
            dk_from_next = jnp.where(step > 0, dk_carry[j], 0.0)
            dv_from_next = jnp.where(step > 0, dv_carry[j], 0.0)
            for t in reversed(range(per_step)):
                at = slice(t * BLOCK, (t + 1) * BLOCK)
                kc, vc = kc_ref[at, js], vc_ref[at, js]
                if t > 0:
                    before = slice((t - 1) * BLOCK, t * BLOCK)
                    kp, vp, bias_before = kc_ref[before, js], vc_ref[before, js], bias_p
                else:
                    kp, vp, bias_before = kp_ref[:, js], vp_ref[:, js], jnp.where(first_block > 0, bias_p, -jnp.inf)
                q4 = _stack_heads(q_ref[at, :], j)
                do4b = _stack_heads(d_o[at], j).astype(BF16)
                lse4 = _head_rows(lse_ref[t * N_HEADS:(t + 1) * N_HEADS, :], j)
                pc = jnp.exp(_dot_nt(kc, q4) + bias_c - lse4)
                pp = jnp.exp(_dot_nt(kp, q4) + bias_before - lse4)
                dpc = _dot_nt(vc, do4b)
                dpp = _dot_nt(vp, do4b)
                delta = jnp.sum(pc * dpc + pp * dpp, axis=0, keepdims=True)
                dsc = (pc * (dpc - delta)).astype(BF16)
                dsp = (pp * (dpp - delta)).astype(BF16)
                dq4 = ((_dot_tn(dsc, kc) + _dot_tn(dsp, kp)) * ATTN_SCALE).astype(BF16)
                sink_loss = sink_loss + jnp.exp(sink - lse4) * delta
                for g in range(Q_PER_KV):
                    h = j * Q_PER_KV + g
                    dq_ref[at, h * HEAD_DIM:(h + 1) * HEAD_DIM] = dq4[g * BLOCK:(g + 1) * BLOCK]
                dk_ref[at, js] = (_dot(dsc, q4) + dk_from_next).astype(BF16)
                dv_ref[at, js] = (_dot(pc.astype(BF16), do4b) + dv_from_next).astype(BF16)
                dk_from_next = _dot(dsp, q4)
                dv_from_next = _dot(pp.astype(BF16), do4b)
            dk_carry[j] = dk_from_next
            dv_carry[j] = dv_from_next
            for g in range(Q_PER_KV):
                h = j * Q_PER_KV + g
                dsk_ref[0:1, h:h + 1] -= jnp.sum(sink_loss[:, g * BLOCK:(g + 1) * BLOCK], axis=1, keepdims=True)

    cur = lambda w: pl.BlockSpec((per_step * BLOCK, w), lambda b, s: (b * steps + steps - 1 - s, 0))
    prev = lambda w: pl.BlockSpec((BLOCK, w), lambda b, s: (b * nb + jnp.maximum(nb - per_step * (s + 1) - 1, 0), 0))
    return _call(
        body, name="attn_bwd", grid=(n_seq, steps),
        in_specs=[cur(512), cur(128), prev(128), cur(128), prev(128), cur(512), cur(512),
                  pl.BlockSpec((per_step * N_HEADS, BLOCK), lambda b, s: (b * steps + steps - 1 - s, 0)), cur(512),
                  _full((1, N_HEADS))],
        out_specs=[cur(512), cur(128), cur(128), cur(512), _full((1, N_HEADS))],
        out_shape=[_sds((rows, 512), BF16), _sds((rows, 128), BF16), _sds((rows, 128), BF16),
                   _sds((rows, 512), BF16), _sds((1, N_HEADS))],
        scratch_shapes=[_ATTN_BIAS_SCRATCH, pltpu.VMEM((KV_HEADS, BLOCK, HEAD_DIM), F32),
                        pltpu.VMEM((KV_HEADS, BLOCK, HEAD_DIM), F32)],
        compiler_params=_params(32, ("arbitrary", "arbitrary")),
    )(q, k, k, v, v, za, o, lse, d_ao, sinks)


def _tail(ssm_out, attn_out, x2d, p2d, target, w_out, g2, w_gate, b_gate, w_proj):
    rows = x2d.shape[0]
    tm = 512

    def body(so_ref, ao_ref, x_ref, p_ref, t_ref, wo_ref, g2_ref, wg_ref, bg_ref, wp_ref,
             dh1_ref, dso_ref, dao_ref, dwo_ref, dwg_ref, dwp_ref, dbg_ref, dg2_ref, loss_ref):
        @pl.when(pl.program_id(0) == 0)
        def _():
            for ref in (dwo_ref, dwg_ref, dwp_ref, dbg_ref, dg2_ref, loss_ref):
                ref[...] = jnp.zeros_like(ref)

        cat = jnp.concatenate([so_ref[...], ao_ref[...]], axis=1)
        g2 = g2_ref[...]
        mixed = _dot(cat, wo_ref[...])
        r = lax.rsqrt(jnp.mean(mixed * mixed, axis=-1, keepdims=True) + EPS)
        mr = mixed * r
        h1 = x_ref[...] + mr * g2
        h1b = h1.astype(BF16)
        gate = jax.nn.sigmoid(_dot(h1b, wg_ref[...]) + bg_ref[...])
        pb = p_ref[...].astype(BF16)
        wp_blocks = [slice(j * D_PLE, (j + 1) * D_PLE) for j in range(N_CHIPS)]
        pp = jnp.concatenate([_dot(pb, wp_ref[blk, :]) for blk in wp_blocks], axis=1)
        err = h1 + gate * pp - t_ref[...]
        loss_ref[...] += 0.5 * jnp.sum(jnp.mean(err * err, axis=-1, keepdims=True), axis=0, keepdims=True)

        dh2 = err * (1.0 / D_MODEL)
        d_glin = dh2 * pp * gate * (1.0 - gate)
        d_glin_b = d_glin.astype(BF16)
        dwg_ref[...] += _dot_tn(h1b, d_glin_b)
        dbg_ref[...] += jnp.sum(d_glin, axis=0, keepdims=True)
        d_pp = (dh2 * gate).astype(BF16)
        for blk in wp_blocks:
            dwp_ref[blk, :] += _dot_tn(pb, d_pp[:, blk])
        dh1 = dh2 + _dot_nt(d_glin_b, wg_ref[...])
        dh1_ref[...] = dh1
        dg2_ref[...] += jnp.sum(dh1 * mr, axis=0, keepdims=True)
        a_ = dh1 * g2
        d_mixed = (r * a_ - mr * (r * jnp.mean(a_ * mr, axis=-1, keepdims=True))).astype(BF16)
        dwo_ref[...] += _dot_tn(cat, d_mixed)
        d_cat = _dot_nt(d_mixed, wo_ref[...])
        dso_ref[...] = d_cat[:, 0:512]
        dao_ref[...] = d_cat[:, 512:1024]

    return _call(
        body, name="tail_fwd_bwd", grid=(rows // tm,),
        in_specs=[_rows(tm, 512), _rows(tm, 512), _rows(tm, D_MODEL), _rows(tm, D_PLE), _rows(tm, D_MODEL),
                  _full((D_MODEL, D_MODEL)), _full((1, D_MODEL)), _full((D_MODEL, D_MODEL)), _full((1, D_MODEL)),
                  _full((N_CHIPS * D_PLE, D_PLE))],
        out_specs=[_rows(tm, D_MODEL), _rows(tm, 512), _rows(tm, 512), _full((D_MODEL, D_MODEL)),
                   _full((D_MODEL, D_MODEL)), _full((N_CHIPS * D_PLE, D_PLE)), _full((1, D_MODEL)), _full((1, D_MODEL)),
                   _full((1, 1))],
        out_shape=[_sds((rows, D_MODEL)), _sds((rows, 512)), _sds((rows, 512)), _sds((D_MODEL, D_MODEL)),
                   _sds((D_MODEL, D_MODEL)), _sds((N_CHIPS * D_PLE, D_PLE)), _sds((1, D_MODEL)), _sds((1, D_MODEL)),
                   _sds((1, 1))],
        compiler_params=_params(52, ("arbitrary",)),
    )(ssm_out, attn_out, x2d, p2d, target, w_out, g2, w_gate, b_gate, w_proj)


def _local_step(x, hn, p, target, pre_norm_g, w_in_t, s5_params, s5_operands, ssm_d, w_glu, b_glu, sinks, w_out,
                post_norm_g, w_proj, w_gate, b_gate, send_tail_grads=lambda ready: ready["w_out"],
                send_bc=lambda d_bc: (d_bc, d_bc)):
    n_seq, seq, _ = x.shape
    rows = n_seq * seq
    x2d = x.reshape(rows, D_MODEL)
    p2d = p.reshape(rows, D_PLE)
    t2d = target.reshape(rows, D_MODEL)

    l_re, l_im, bt_re, bt_im, cm_re, cm_im = s5_operands

    u_scan, zs, q, k, v, za, o, attn_out, lse = _in_proj_attn_fwd(hn, w_in_t, sinks, n_seq, seq)
    y_scan, h_re, h_im = _s5_scan_fwd(u_scan, bt_re, bt_im, cm_re, cm_im, l_re, l_im, ssm_d, n_seq, seq)
    ssm_out = _glu_fwd(y_scan, zs, w_glu, b_glu, n_seq, seq)

    dh1, d_so, d_ao, d_w_out, d_w_gate, d_w_proj, d_b_gate, d_g2, loss = _tail(
        ssm_out, attn_out, x2d, p2d, t2d, w_out, post_norm_g, w_gate, b_gate, w_proj)

    dq, dk, dv, dza, d_sinks = _attn_bwd(q, k, v, za, o, lse, d_ao, sinks, n_seq, seq)
    dy_scan, dzs, d_w_glu, d_b_glu = _glu_bwd(y_scan, zs, d_so, w_glu, b_glu, n_seq, seq)
    du_scan, d_bt_re, d_bt_im, d_cm_re, d_cm_im, d_l_re, d_l_im, d_d = _s5_scan_bwd(
        dy_scan, u_scan, h_re, h_im, bt_re, bt_im, cm_re, cm_im, l_re, l_im, ssm_d, n_seq, seq)
    tail_grads_arrived = send_tail_grads(dict(w_out=d_w_out, pl_w_gate=d_w_gate, pl_w_proj=d_w_proj))
    d_lam_re, d_lam_im, d_log_step, d_bc = _s5_params_bwd(
        s5_params, (d_l_re, d_l_im, d_bt_re, d_bt_im, d_cm_re, d_cm_im), tail_grads_arrived)

    sent, arrived = send_bc(d_bc)
    d_proj, d_w_in_early, d_w_in_early_b = _in_proj_bwd_early(hn, du_scan, dzs, dq, dk, dv, dza, sent, n_seq, seq)
    grad_x, d_w_in_late, d_g1 = _in_proj_bwd(x2d, dh1, pre_norm_g, w_in_t, d_proj, arrived)
    grads = dict(
        pre_norm_g=d_g1, w_in_early=d_w_in_early, w_in_early_bf16=d_w_in_early_b, w_in_late=d_w_in_late,
        ssm_lam_re=d_lam_re, ssm_lam_im=d_lam_im, ssm_log_step=d_log_step, ssm_bc=d_bc, ssm_d=d_d, ssm_w_glu=d_w_glu,
        ssm_b_glu=d_b_glu, attn_sinks=d_sinks, w_out=d_w_out, post_norm_g=d_g2, pl_w_proj=d_w_proj,
        pl_w_gate=d_w_gate, pl_b_gate=d_b_gate)
    return grad_x.reshape(x.shape), loss, grads


_BIG = ("w_in", "ssm_w_glu", "w_out", "pl_w_proj", "pl_w_gate")
_BIG_SHARD = {"w_in": (D_IN // N_CHIPS, D_MODEL), "ssm_w_glu": (D_SSM // N_CHIPS, D_SSM),
              "w_out": (D_MODEL // N_CHIPS, D_MODEL), "pl_w_proj": (D_PLE, D_MODEL // N_CHIPS),
              "pl_w_gate": (D_MODEL // N_CHIPS, D_MODEL)}
_SMALL = {"pre_norm_g": (1, D_MODEL), "ssm_lam_re": (SSM_GROUPS, SSM_STATE), "ssm_lam_im": (SSM_GROUPS, SSM_STATE),
          "ssm_log_step": (1, SSM_GROUPS), "ssm_b_re": (D_SSM, SSM_STATE), "ssm_b_im": (D_SSM, SSM_STATE),
          "ssm_c_re": (D_SSM, SSM_STATE), "ssm_c_im": (D_SSM, SSM_STATE), "ssm_d": (1, D_SSM), "ssm_b_glu": (1, D_SSM),
          "attn_sinks": (1, N_HEADS), "post_norm_g": (1, D_MODEL), "pl_b_gate": (1, D_MODEL)}
_VEC_ROWS = ("pre_norm_g", "post_norm_g", "pl_b_gate", "ssm_d", "ssm_b_glu", "attn_sinks", "ssm_log_step", "loss")
_SMALL_GROUPS = (
    ("vec", (8, D_MODEL), tuple((name, r) for r, name in enumerate(_VEC_ROWS))),
    ("lam", (2 * SSM_GROUPS, SSM_STATE), (("ssm_lam_re", 0), ("ssm_lam_im", SSM_GROUPS))),
)
_SMALL_EARLY = ("ssm_b_re", "ssm_b_im", "ssm_c_re", "ssm_c_im")
_SMALL_ORDER = tuple(name for _, _, members in _SMALL_GROUPS for name, _ in members) + _SMALL_EARLY
_WEIGHT_ORDER = ("pre_norm_g", "w_in", "ssm_lam_re", "ssm_lam_im", "ssm_log_step", "ssm_b_re", "ssm_b_im", "ssm_c_re",
                 "ssm_c_im", "ssm_d", "ssm_w_glu", "ssm_b_glu", "attn_sinks", "w_out", "post_norm_g", "pl_w_proj",
                 "pl_w_gate", "pl_b_gate")


def _small_shape(name):
    return (1, 1) if name == "loss" else _SMALL[name]


def _to_kernel_form(name, a):
    a = a[0]
    if name == "w_in":
        return a.T
    if name in ("ssm_b_re", "ssm_b_im"):
        a = a.transpose(0, 2, 1)
    return a.reshape(_SMALL[name]) if name in _SMALL else a


def _from_kernel_form(name, a, shape):
    if name == "w_in":
        a = a.T
    if name in ("ssm_b_re", "ssm_b_im"):
        a = a.reshape(SSM_GROUPS, SSM_GROUP_CH, SSM_STATE).transpose(0, 2, 1)
    return a.reshape(shape)


def _mesh_place():
    x, y, c = lax.axis_index("x"), lax.axis_index("y"), lax.axis_index("c")
    other_chips = ((1 - x, y), (x, 1 - y), (1 - x, 1 - y))
    return x, y, c, other_chips


def _gather_copies(s_refs, g_refs, send_sems, recv_sems, local_sems):
    x, y, c, other_chips = _mesh_place()
    started = []
    for i, (s_ref, g_ref) in enumerate(zip(s_refs, g_refs)):
        rows = s_ref.shape[0]
        half = rows // 2

        def block(chip, g_ref=g_ref, rows=rows, half=half):
            return g_ref.at[pl.ds((2 * chip[0] + chip[1]) * rows + c * half, half), :]

        def copy(k, chip, to, src=None, i=i, block=block):
            return pltpu.make_async_remote_copy(
                src_ref=block(chip) if src is None else src, dst_ref=block(chip), send_sem=send_sems.at[6 * i + k],
                recv_sem=recv_sems.at[6 * i + k], device_id=to, device_id_type=MESH)

        own = pltpu.make_async_copy(s_ref, g_ref.at[pl.ds((2 * x + y) * rows, rows), :], local_sems.at[i])
        own.start()
        first = [copy(k, (x, y), (*chip, c), src=s_ref.at[pl.ds(c * half, half), :])
                 for k, chip in enumerate(other_chips)]
        for cp in first:
            cp.start()
        passed = [copy(3 + k, chip, (x, y, 1 - c)) for k, chip in enumerate(other_chips)]
        started.append((own, first, passed))
    for own, first, passed in started:
        for k in range(3):
            first[k].wait_recv()
            passed[k].start()
    for own, first, passed in started:
        for k in range(3):
            passed[k].wait_recv()
        for cp in first + passed:
            cp.wait_send()
        own.wait()


def _gather_semaphores(n_t):
    return [pltpu.SemaphoreType.DMA((6 * n_t,)), pltpu.SemaphoreType.DMA((6 * n_t,)), pltpu.SemaphoreType.DMA((n_t,))]


def _gather_weights_beside(shards, name, collective_id):
    n_t = len(shards)
    hbm = pltpu.MemorySpace.HBM
    s_refs = [jax.new_ref(s, memory_space=hbm) for s in shards]
    g_refs = [jax.empty_ref(jax.ShapeDtypeStruct((N_CHIPS * s.shape[0], s.shape[1]), s.dtype), memory_space=hbm)
              for s in shards]

    def launch(send_sems, recv_sems, local_sems):
        x, y, c, other_chips = _mesh_place()
        peers = [(*chip, c) for chip in other_chips] + [(x, y, 1 - c)]
        barrier = pltpu.get_barrier_semaphore()
        for peer in peers:
            pl.semaphore_signal(barrier, inc=1, device_id=peer, device_id_type=MESH)
        pl.semaphore_wait(barrier, len(peers))
        _gather_copies(s_refs, g_refs, send_sems, recv_sems, local_sems)

    pl.kernel(launch, mesh=plsc.ScalarSubcoreMesh(axis_name="sequencer", num_cores=1), name=name,
              scratch_types=_gather_semaphores(n_t), compiler_params=pltpu.CompilerParams(collective_id=collective_id))()
    return [g[...] for g in g_refs]


_RELATIONS = tuple(((r >> 2) & 1, (r >> 1) & 1, r & 1) for r in range(1, 8))


def _related(place, relation):
    return tuple(1 - a if flip else a for a, flip in zip(place, relation))


def _scatter_beside(mats, name, collective_id):
    hbm = pltpu.MemorySpace.HBM
    src_refs = [jax.new_ref(a, memory_space=hbm) for a in mats]
    land_refs = [jax.empty_ref(jax.ShapeDtypeStruct((7, a.shape[0] // 8, a.shape[1]), a.dtype), memory_space=hbm)
                 for a in mats]

    def launch(send_sems, recv_sems):
        me = (lax.axis_index("x"), lax.axis_index("y"), lax.axis_index("c"))
        peers = [_related(me, rel) for rel in _RELATIONS]
        barrier = pltpu.get_barrier_semaphore()
        for peer in peers:
            pl.semaphore_signal(barrier, inc=1, device_id=peer, device_id_type=MESH)
        pl.semaphore_wait(barrier, len(peers))
        copies = []
        for i, (src, land) in enumerate(zip(src_refs, land_refs)):
            hr = land.shape[1]
            for k, (tx, ty, tc) in enumerate(peers):
                rows = pl.ds((2 * tx + ty) * 2 * hr + tc * hr, hr)
                copies.append(pltpu.make_async_remote_copy(
                    src_ref=src.at[rows, :], dst_ref=land.at[k], send_sem=send_sems.at[7 * i + k],
                    recv_sem=recv_sems.at[7 * i + k], device_id=(tx, ty, tc), device_id_type=MESH))
                copies[-1].start()
        for cp in copies:
            cp.wait()

    n_sems = 7 * len(mats)
    pl.kernel(launch, mesh=plsc.ScalarSubcoreMesh(axis_name="sequencer", num_cores=1), name=name,
              scratch_types=[pltpu.SemaphoreType.DMA((n_sems,)), pltpu.SemaphoreType.DMA((n_sems,))],
              compiler_params=pltpu.CompilerParams(collective_id=collective_id))()
    return [ref[...] for ref in land_refs]


def _broadcast_beside(arrays):
    hbm = pltpu.MemorySpace.HBM
    src_refs = [jax.new_ref(a, memory_space=hbm) for a in arrays]
    land_refs = [jax.empty_ref(jax.ShapeDtypeStruct((len(_RELATIONS),) + a.shape, a.dtype), memory_space=hbm)
                 for a in arrays]

    def launch(send_sems, recv_sems):
        me = (lax.axis_index("x"), lax.axis_index("y"), lax.axis_index("c"))
        peers = [_related(me, rel) for rel in _RELATIONS]
        barrier = pltpu.get_barrier_semaphore()
        for peer in peers:
            pl.semaphore_signal(barrier, inc=1, device_id=peer, device_id_type=MESH)
        pl.semaphore_wait(barrier, len(peers))
        copies = []
        for i, (src, land) in enumerate(zip(src_refs, land_refs)):
            for k, peer in enumerate(peers):
                copies.append(pltpu.make_async_remote_copy(
                    src_ref=src, dst_ref=land.at[k], send_sem=send_sems.at[7 * i + k],
                    recv_sem=recv_sems.at[7 * i + k], device_id=peer, device_id_type=MESH))
                copies[-1].start()
        for cp in copies:
            cp.wait()

    n_sems = 7 * len(arrays)
    pl.kernel(launch, mesh=plsc.ScalarSubcoreMesh(axis_name="sequencer", num_cores=1), name="broadcast_beside",
              scratch_types=[pltpu.SemaphoreType.DMA((n_sems,)), pltpu.SemaphoreType.DMA((n_sems,))],
              compiler_params=pltpu.CompilerParams(collective_id=3))()
    return [ref[...] for ref in land_refs]


def _exchange_grads(big, outputs, small, landed, own_bc, landed_bc):
    n_t = len(big)
    n_g = len(_SMALL_GROUPS)
    names = _SMALL_ORDER
    halves = [(b.shape[0] // N_CHIPS // 2, b.shape[1]) for b in big]
    early = sorted(landed)
    late = [i for i in range(n_t) if i not in landed]
    n_sems = 4 * n_g + 7 * len(late) + n_t
    small_sem0, block_sem0 = n_t, n_t + len(names)
    early_sem0 = block_sem0 + N_CHIPS * len(late)
    landed_sem0 = early_sem0 + 2 * len(early)
    sent = [n for n in names if n in small]

    def body(*refs):
        pos = 0

        def take(n):
            nonlocal pos
            pos += n
            return refs[pos - n:pos]

        big_refs, small_refs = take(n_t), dict(zip(sent, take(len(sent))))
        land_refs = dict(zip(early, take(len(early))))
        own_bc_ref, landed_bc_ref = take(2)
        out_refs, small_out_refs = take(len(outputs)), dict(zip(names, take(len(names))))
        per_late = lambda: dict(zip(late, take(len(late))))
        ga, gb, pme, send_b, recv_b = per_late(), per_late(), take(n_t), per_late(), per_late()
        own_e, land_e = dict(zip(early, take(len(early)))), dict(zip(early, take(len(early))))
        own_s, land_s = take(2)
        s_own, s_sib, s_chips, s_pair = take(n_g), take(n_g), take(n_g), take(n_g)
        stage = dict(zip(names, take(len(names))))
        send_sems, recv_sems, local_sems = take(3)
        x, y, c, other_chips = _mesh_place()
        me = 2 * x + y
        sibling = (x, y, 1 - c)
        sem_at = iter(range(n_sems))

        def remote(src, dst, to):
            k = next(sem_at)
            return pltpu.make_async_remote_copy(src_ref=src, dst_ref=dst, send_sem=send_sems.at[k],
                                                recv_sem=recv_sems.at[k], device_id=to, device_id_type=MESH)

        loads = [pltpu.make_async_copy(small_refs[name], stage[name], local_sems.at[small_sem0 + names.index(name)])
                 for name in sent]
        landed_loads = [pltpu.make_async_copy(own_bc_ref, own_s, local_sems.at[landed_sem0]),
                        pltpu.make_async_copy(landed_bc_ref, land_s, local_sems.at[landed_sem0 + 1])]
        for cp in loads + landed_loads:
            cp.start()
        for cp in loads:
            cp.wait()
        small_swaps = []
        for gi, (_, _, members) in enumerate(_SMALL_GROUPS):
            s_own[gi][...] = jnp.zeros_like(s_own[gi])
            for name, r0 in members:
                r, n = _small_shape(name)
                s_own[gi][r0:r0 + r, 0:n] = stage[name][...]
            small_swaps.append(remote(s_own[gi], s_sib[gi], sibling))
            small_swaps[gi].start()
        order = sorted(late, key=lambda i: halves[i][0] * halves[i][1])
        own_loads, big_swaps = {}, {}
        for i in order:
            hr = halves[i][0]
            own_loads[i], big_swaps[i] = [], []
            for j in range(N_CHIPS):
                mine = big_refs[i].at[pl.ds(j * 2 * hr + c * hr, hr), :]
                theirs = big_refs[i].at[pl.ds(j * 2 * hr + (1 - c) * hr, hr), :]
                sem = local_sems.at[block_sem0 + N_CHIPS * late.index(i) + j]
                own_loads[i].append(pltpu.make_async_copy(mine, ga[i].at[j], sem))
                own_loads[i][j].start()
                big_swaps[i].append(remote(theirs, gb[i].at[j], sibling))
                big_swaps[i][j].start()
        early_loads = {}
        for e, i in enumerate(early):
            hr = halves[i][0]
            mine = big_refs[i].at[pl.ds(me * 2 * hr + c * hr, hr), :]
            early_loads[i] = [pltpu.make_async_copy(mine, own_e[i], local_sems.at[early_sem0 + 2 * e]),
                              pltpu.make_async_copy(land_refs[i], land_e[i], local_sems.at[early_sem0 + 2 * e + 1])]
            for cp in early_loads[i]:
                cp.start()
        small_sends = []
        for gi in range(n_g):
            small_swaps[gi].wait_recv()
            s_pair[gi][...] = s_own[gi][...] + s_sib[gi][...]
            small_sends.append([remote(s_pair[gi], s_chips[gi].at[k], (*chip, c)) for k, chip in enumerate(other_chips)])
            for cp in small_sends[gi]:
                cp.start()

        def pair_sum(i, j):
            return ga[i][j] + gb[i][j]

        big_sends = {}
        for i in order:
            for j in range(N_CHIPS):
                own_loads[i][j].wait()
                big_swaps[i][j].wait_recv()
            big_sends[i] = []
            for k, chip in enumerate(other_chips):
                send_b[i][k] = pair_sum(i, 2 * chip[0] + chip[1]).astype(BF16)
                big_sends[i].append(remote(send_b[i].at[k], recv_b[i].at[k], (*chip, c)))
                big_sends[i][k].start()
        last_swaps, keeps = {}, {}
        for i in early + order:
            hr = halves[i][0]
            if i in landed:
                for cp in early_loads[i]:
                    cp.wait()
                total = own_e[i][...]
                for k in range(len(_RELATIONS)):
                    total = total + land_e[i][k].astype(F32)
                pme[i][...] = total
            else:
                for k in range(3):
                    big_sends[i][k].wait_recv()
                pme[i][...] = ((pair_sum(i, me) + recv_b[i][0].astype(F32)) + recv_b[i][1].astype(F32)) + recv_b[i][2].astype(F32)
            o, = [o for o, group in enumerate(outputs) if i in group]
            first_col = sum(halves[j][1] for j in outputs[o][:outputs[o].index(i)])
            mine = out_refs[o].at[pl.ds(c * hr, hr), pl.ds(first_col, halves[i][1])]
            keeps[i] = pltpu.make_async_copy(pme[i], mine, local_sems.at[i])
            keeps[i].start()
            last_swaps[i] = remote(pme[i], mine, sibling)
            last_swaps[i].start()

        for gi, (_, _, members) in enumerate(_SMALL_GROUPS):
            for k in range(3):
                small_sends[gi][k].wait_recv()
            total = None
            for j in range(N_CHIPS):
                rel = jnp.bitwise_xor(j, me)
                term = jnp.where(rel == 0, s_pair[gi][...], jnp.where(
                    rel == 2, s_chips[gi][0], jnp.where(rel == 1, s_chips[gi][1], s_chips[gi][2])))
                total = term if total is None else total + term
            s_sib[gi][...] = total
            for name, r0 in members:
                r, n = _small_shape(name)
                stage[name][...] = s_sib[gi][r0:r0 + r, 0:n]
        my_index = 4 * x + 2 * y + c
        for cp in landed_loads:
            cp.wait()
        total = None
        for d in range(2 * N_CHIPS):
            rel = jnp.bitwise_xor(d, my_index)
            term = own_s[...]
            for k in range(len(_RELATIONS)):
                term = jnp.where(rel == k + 1, land_s[k], term)
            total = term.astype(F32) if total is None else total + term.astype(F32)
        for a, name in enumerate(_SMALL_EARLY):
            stage[name][...] = total[:, a * SSM_STATE:(a + 1) * SSM_STATE]
        stores = [pltpu.make_async_copy(stage[name], small_out_refs[name], local_sems.at[small_sem0 + a])
                  for a, name in enumerate(names)]
        for cp in stores:
            cp.start()

        for i in range(n_t):
            last_swaps[i].wait_recv()
            keeps[i].wait()
        for cp in stores:
            cp.wait()
        groups = list(big_swaps.values()) + small_sends + list(big_sends.values())
        for cp in small_swaps + [cp for group in groups for cp in group] + list(last_swaps.values()):
            cp.wait_send()

    any_spec = pl.BlockSpec(memory_space=pl.ANY)
    small_shapes = [_sds(_small_shape(n)) for n in names]
    group_shapes = [shape for _, shape, _ in _SMALL_GROUPS]
    vmem = lambda which, dtype, lead=(): [pltpu.VMEM(lead + halves[i], dtype) for i in which]
    outs = _call(
        body, name="exchange_grads",
        in_specs=[any_spec] * (n_t + len(sent) + len(early) + 2),
        out_specs=[any_spec] * (len(outputs) + len(names)),
        out_shape=[_sds((big[group[0]].shape[0] // N_CHIPS, sum(big[i].shape[1] for i in group))) for group in outputs]
        + small_shapes,
        scratch_shapes=(vmem(late, F32, (N_CHIPS,)) + vmem(late, F32, (N_CHIPS,)) + vmem(range(n_t), F32)
                        + vmem(late, BF16, (3,)) + vmem(late, BF16, (3,))
                        + vmem(early, F32)
                        + [pltpu.VMEM((len(_RELATIONS),) + halves[i], landed[i].dtype) for i in early]
                        + [pltpu.VMEM(own_bc.shape, own_bc.dtype), pltpu.VMEM(landed_bc.shape, landed_bc.dtype)]
                        + [pltpu.VMEM(s, F32) for s in group_shapes] * 2 + [pltpu.VMEM((3,) + s, F32) for s in group_shapes]
                        + [pltpu.VMEM(s, F32) for s in group_shapes]
                        + [pltpu.VMEM(_small_shape(n), F32) for n in names]
                        + [pltpu.SemaphoreType.DMA((n_sems,)), pltpu.SemaphoreType.DMA((n_sems,)),
                           pltpu.SemaphoreType.DMA((landed_sem0 + 2,))]),
        compiler_params=_params(48),
    )(*big, *[small[n] for n in sent], *[landed[i] for i in early], own_bc, landed_bc)
    return list(outs[:len(outputs)]), dict(zip(names, outs[len(outputs):]))


def _adamw_update(w, g, m, v):
    m = ADAM_B1 * m + (1.0 - ADAM_B1) * g
    v = ADAM_B2 * v + (1.0 - ADAM_B2) * (g * g)
    m_hat = m / (1.0 - ADAM_B1 ** ADAM_STEP)
    v_hat = v / (1.0 - ADAM_B2 ** ADAM_STEP)
    return -ADAM_LR * (m_hat / (jnp.sqrt(v_hat) + ADAM_EPS) + ADAM_WD * w), m, v


def _adamw(w, g, m, v, grid, name):
    n_t = len(w)

    def body(*refs):
        ins, outs = refs[:4 * n_t], refs[4 * n_t:]
        for i in range(n_t):
            w_, g_, m_, v_ = [ins[a * n_t + i][...] for a in range(4)]
            vals = (g_,) + _adamw_update(w_, g_, m_, v_)
            for a in range(4):
                outs[a * n_t + i][...] = vals[a]

    specs = [pl.BlockSpec((a.shape[0] // grid, a.shape[1]), lambda i: (i, 0)) for a in w]
    shapes = [_sds(a.shape) for a in w]
    outs = _call(
        body, name=name, grid=(grid,), in_specs=specs * 4, out_specs=specs * 4, out_shape=shapes * 4,
        compiler_params=_params(40, ("arbitrary",)),
    )(*w, *g, *m, *v)
    return [outs[a * n_t:(a + 1) * n_t] for a in range(4)]


def kernel(x, p, pre_norm_g, w_in, ssm_lam_re, ssm_lam_im, ssm_log_step, ssm_b_re, ssm_b_im, ssm_c_re, ssm_c_im, ssm_d, ssm_w_glu, ssm_b_glu, attn_sinks, w_out, post_norm_g, pl_w_proj, pl_w_gate, pl_b_gate, loss_target, m_pre_norm_g, m_w_in, m_ssm_lam_re, m_ssm_lam_im, m_ssm_log_step, m_ssm_b_re, m_ssm_b_im, m_ssm_c_re, m_ssm_c_im, m_ssm_d, m_ssm_w_glu, m_ssm_b_glu, m_attn_sinks, m_w_out, m_post_norm_g, m_pl_w_proj, m_pl_w_gate, m_pl_b_gate, v_pre_norm_g, v_w_in, v_ssm_lam_re, v_ssm_lam_im, v_ssm_log_step, v_ssm_b_re, v_ssm_b_im, v_ssm_c_re, v_ssm_c_im, v_ssm_d, v_ssm_w_glu, v_ssm_b_glu, v_attn_sinks, v_w_out, v_post_norm_g, v_pl_w_proj, v_pl_w_gate, v_pl_b_gate):
    weights = dict(pre_norm_g=pre_norm_g, w_in=w_in, ssm_lam_re=ssm_lam_re, ssm_lam_im=ssm_lam_im,
                   ssm_log_step=ssm_log_step, ssm_b_re=ssm_b_re, ssm_b_im=ssm_b_im, ssm_c_re=ssm_c_re,
                   ssm_c_im=ssm_c_im, ssm_d=ssm_d, ssm_w_glu=ssm_w_glu, ssm_b_glu=ssm_b_glu, attn_sinks=attn_sinks,
                   w_out=w_out, post_norm_g=post_norm_g, pl_w_proj=pl_w_proj, pl_w_gate=pl_w_gate, pl_b_gate=pl_b_gate)
    m_in = dict(pre_norm_g=m_pre_norm_g, w_in=m_w_in, ssm_lam_re=m_ssm_lam_re, ssm_lam_im=m_ssm_lam_im,
                ssm_log_step=m_ssm_log_step, ssm_b_re=m_ssm_b_re, ssm_b_im=m_ssm_b_im, ssm_c_re=m_ssm_c_re,
                ssm_c_im=m_ssm_c_im, ssm_d=m_ssm_d, ssm_w_glu=m_ssm_w_glu, ssm_b_glu=m_ssm_b_glu,
                attn_sinks=m_attn_sinks, w_out=m_w_out, post_norm_g=m_post_norm_g, pl_w_proj=m_pl_w_proj,
                pl_w_gate=m_pl_w_gate, pl_b_gate=m_pl_b_gate)
    v_in = dict(pre_norm_g=v_pre_norm_g, w_in=v_w_in, ssm_lam_re=v_ssm_lam_re, ssm_lam_im=v_ssm_lam_im,
                ssm_log_step=v_ssm_log_step, ssm_b_re=v_ssm_b_re, ssm_b_im=v_ssm_b_im, ssm_c_re=v_ssm_c_re,
                ssm_c_im=v_ssm_c_im, ssm_d=v_ssm_d, ssm_w_glu=v_ssm_w_glu, ssm_b_glu=v_ssm_b_glu,
                attn_sinks=v_attn_sinks, w_out=v_w_out, post_norm_g=v_post_norm_g, pl_w_proj=v_pl_w_proj,
                pl_w_gate=v_pl_w_gate, pl_b_gate=v_pl_b_gate)

    def two_d(tree):
        return {k: _to_kernel_form(k, a) for k, a in tree.items()}

    w2, m2, v2 = two_d(weights), two_d(m_in), two_d(v_in)

    (w_in_full,) = _gather_weights_beside([w2["w_in"].astype(BF16)], "gather_w_in_beside", 4)
    s5_params = tuple(w2[n] for n in ("ssm_lam_re", "ssm_lam_im", "ssm_log_step", "ssm_b_re", "ssm_b_im", "ssm_c_re",
                                      "ssm_c_im"))
    s5_operands = _s5_params_fwd(*s5_params)
    hn = _pre_norm(x.reshape(-1, D_MODEL), w2["pre_norm_g"])
    behind = s5_operands[0][0, 0] * 0.0 + hn[0, 0].astype(F32) * 0.0
    rest = _gather_weights_beside([(w2[n] + behind).astype(BF16) for n in _BIG[1:]], "gather_weights_beside", 1)
    full = dict(zip(_BIG, [w_in_full] + rest))
    mats = ("w_in_early", "w_in_late") + _BIG[1:]
    landed, bc = {}, {}

    def send_tail_grads(ready):
        sent_early = ("w_out", "pl_w_gate", "pl_w_proj")
        landed.update(zip([mats.index(n) for n in sent_early],
                          _scatter_beside([ready[n] for n in sent_early], "scatter_beside", 2)))
        return landed[mats.index(sent_early[-1])]

    def send_bc(d_bc):
        bc["own"] = d_bc
        bc["landed"] = _broadcast_beside([d_bc])[0]
        return d_bc, bc["landed"]

    grad_x, loss, grads = _local_step(
        x, hn, p, loss_target, w2["pre_norm_g"], full["w_in"], s5_params, s5_operands, w2["ssm_d"], full["ssm_w_glu"], w2["ssm_b_glu"],
        w2["attn_sinks"], full["w_out"], w2["post_norm_g"], full["pl_w_proj"], full["pl_w_gate"], w2["pl_b_gate"], send_tail_grads, send_bc)

    landed[0], landed[mats.index("ssm_w_glu")] = _scatter_beside(
        [grads["w_in_early_bf16"], grads["ssm_w_glu"]], "scatter_w_in_beside", 5)
    sent_here = {**{n: grads[n] for n in _SMALL if n not in _SMALL_EARLY}, "loss": loss}
    halves_of_w_in = ((0, 1),) + tuple((i,) for i in range(2, len(mats)))
    g_big, g_small = _exchange_grads([grads[n] for n in mats], halves_of_w_in, sent_here, landed, bc["own"], bc["landed"])
    g_big = dict(zip(_BIG, g_big))
    total_loss = g_small.pop("loss")

    big_out = _adamw([w2[n] for n in _BIG], [g_big[n] for n in _BIG], [m2[n] for n in _BIG], [v2[n] for n in _BIG],
                     8, "adamw_matrices")
    small_names = tuple(_SMALL)
    small_out = _adamw([w2[n] for n in small_names], [g_small[n] for n in small_names], [m2[n] for n in small_names],
                       [v2[n] for n in small_names], 1, "adamw_small")

    results = [{**dict(zip(_BIG, big_part)), **dict(zip(small_names, small_part))}
               for big_part, small_part in zip(big_out, small_out)]
    flat = [_from_kernel_form(name, r[name], weights[name].shape) for r in results for name in _WEIGHT_ORDER]
    return (total_loss.reshape(()), grad_x, *flat)
```

```python
import math

import jax
import jax.numpy as jnp
from jax import lax
from jax.experimental import pallas as pl
from jax.experimental.pallas import tpu as pltpu
from jax.experimental.pallas import tpu_sc as plsc

F32 = jnp.float32
BF16 = jnp.bfloat16

D_MODEL = 1024
D_SSM = 512
D_ATTN = 512
SSM_GROUPS = 32
SSM_GROUP_CH = 16
SSM_STATE = 64
SSM_LANES = SSM_GROUPS * SSM_STATE
HEAD_DIM = 64
N_HEADS = 8
KV_HEADS = 2
Q_PER_KV = 4
WINDOW = 128
BLOCK = 128
D_PLE = 256
D_IN = 2304
EPS = 1e-6
ATTN_SCALE = 1.0 / math.sqrt(HEAD_DIM)

ADAM_LR = 0.001
ADAM_B1 = 0.9
ADAM_B2 = 0.999
ADAM_EPS = 1e-08
ADAM_WD = 0.01
ADAM_STEP = 10

N_CHIPS = 4
LANES = 128
SCAN_CHUNKS = 8
SCAN_TILE_STEPS = 32
SCAN_LANE_CHUNK = 512
MIB = 2 ** 20
MESH = pl.DeviceIdType.MESH


def _dot(a, b):
    return jnp.dot(a, b, preferred_element_type=F32)


def _dot_nt(a, b):
    return lax.dot_general(a, b, (((1,), (1,)), ((), ())), preferred_element_type=F32)


def _dot_tn(a, b):
    return lax.dot_general(a, b, (((0,), (0,)), ((), ())), preferred_element_type=F32)


def _params(vmem_mib, semantics=None):
    kw = dict(vmem_limit_bytes=vmem_mib * MIB)
    if semantics is not None:
        kw["dimension_semantics"] = semantics
    return pltpu.CompilerParams(**kw)


def _full(shape):
    nd = len(shape)
    return pl.BlockSpec(shape, lambda *_: (0,) * nd, pipeline_mode=pl.Buffered(1))


def _rows(tm, width):
    return pl.BlockSpec((tm, width), lambda i: (i, 0))


def _sds(shape, dtype=F32):
    return pltpu.HBM(shape, dtype)


def _call(body, **kw):
    fn = pl.pallas_call(body, **kw)
    return lambda *args: fn(*[pltpu.with_memory_space_constraint(a, pltpu.HBM) for a in args])


def _silu(z):
    return z * jax.nn.sigmoid(z)


def _pre_norm(x2d, g1):
    rows = x2d.shape[0]
    tm = 512

    def body(x_ref, g_ref, hn_ref):
        x = x_ref[...]
        r = lax.rsqrt(jnp.mean(x * x, axis=-1, keepdims=True) + EPS)
        hn_ref[...] = (x * r * g_ref[...]).astype(BF16)

    return _call(
        body, name="pre_norm", grid=(rows // tm,), in_specs=[_rows(tm, D_MODEL), _full((1, D_MODEL))],
        out_specs=_rows(tm, D_MODEL), out_shape=_sds((rows, D_MODEL), BF16), compiler_params=_params(32, ("arbitrary",)),
    )(x2d, g1)


def _in_proj(hn, w_in_t, n_seq, seq):
    rows = hn.shape[0]
    tm = 1024
    slab, steps, _, _ = _scan_geometry(n_seq, seq)

    def body(hn_ref, w_ref, *out_refs):
        u_parts, (zs_ref, q_ref, k_ref, v_ref, za_ref) = out_refs[:_SCAN_PARTS], out_refs[_SCAN_PARTS:]
        whole = _dot_nt(hn_ref[...], w_ref[...])

        def proj(a, b):
            return whole[:, a:b]

        _store_chunks(u_parts, pl.program_id(0) * (tm // steps), proj(0, 512), steps, slab)
        zs_ref[...] = proj(512, 1024)
        q_ref[...] = (proj(1024, 1536) * ATTN_SCALE).astype(BF16)
        k_ref[...] = proj(1536, 1664).astype(BF16)
        v_ref[...] = proj(1664, 1792).astype(BF16)
        za_ref[...] = proj(1792, 2304)

    *u_parts, zs, q, k, v, za = _call(
        body, name="in_proj", grid=(rows // tm,),
        in_specs=[_rows(tm, D_MODEL), _full((D_IN, D_MODEL))],
        out_specs=_whole_parts(rows) + [_rows(tm, 512), _rows(tm, 512), _rows(tm, 128), _rows(tm, 128), _rows(tm, 512)],
        out_shape=_part_shapes(rows) + [_sds((rows, 512)), _sds((rows, 512), BF16), _sds((rows, 128), BF16),
                                        _sds((rows, 128), BF16), _sds((rows, 512))],
        compiler_params=_params(48, ("arbitrary",)),
    )(hn, w_in_t)
    return u_parts, zs, q, k, v, za


_EARLY_COLS = D_MODEL // 2


def _in_proj_bwd_early(hn, du_parts, dzs, dq, dk, dv, dza, runs_after, n_seq, seq):
    rows = hn.shape[0]
    tm = 1024
    slab, steps, _, _ = _scan_geometry(n_seq, seq)

    def body(hn_ref, *refs):
        du_parts, (dzs_ref, dq_ref, dk_ref, dv_ref, dza_ref, _, dproj_ref, dw_ref, dwb_ref) = refs[:_SCAN_PARTS], refs[_SCAN_PARTS:]
        i = pl.program_id(0)

        @pl.when(i == 0)
        def _():
            dw_ref[...] = jnp.zeros_like(dw_ref)

        du = _load_chunks(du_parts, i * (tm // steps), tm // steps, steps, slab)
        d_proj = jnp.concatenate([du.astype(BF16), dzs_ref[...], dq_ref[...], dk_ref[...], dv_ref[...], dza_ref[...]],
                                 axis=1)
        dproj_ref[...] = d_proj
        dw_ref[...] += _dot_tn(d_proj, hn_ref[...])

        @pl.when(i == rows // tm - 1)
        def _():
            dwb_ref[...] = dw_ref[...].astype(BF16)

    return _call(
        body, name="in_proj_bwd_early", grid=(rows // tm,),
        in_specs=[_rows(tm, _EARLY_COLS)] + _whole_parts(rows)
        + [_rows(tm, 512), _rows(tm, 512), _rows(tm, 128), _rows(tm, 128), _rows(tm, 512),
           pl.BlockSpec(memory_space=pl.ANY)],
        out_specs=[_rows(tm, D_IN), _full((D_IN, _EARLY_COLS)), _full((D_IN, _EARLY_COLS))],
        out_shape=[_sds((rows, D_IN), BF16), _sds((D_IN, _EARLY_COLS)), _sds((D_IN, _EARLY_COLS), BF16)],
        compiler_params=_params(48, ("arbitrary",)),
    )(hn, *du_parts, dzs, dq, dk, dv, dza, runs_after)


def _in_proj_bwd(x2d, dh1, g1, w_in_t, d_proj, runs_after):
    rows = x2d.shape[0]
    tm = 512

    def body(x_ref, dh1_ref, g_ref, w_ref, dproj_ref, _, gx_ref, dw_ref, dg_ref):
        @pl.when(pl.program_id(0) == 0)
        def _():
            dw_ref[...] = jnp.zeros_like(dw_ref)
            dg_ref[...] = jnp.zeros_like(dg_ref)

        x = x_ref[...]
        g = g_ref[...]
        r = lax.rsqrt(jnp.mean(x * x, axis=-1, keepdims=True) + EPS)
        xr = x * r
        hn = (xr[:, _EARLY_COLS:] * g[:, _EARLY_COLS:]).astype(BF16)
        d_proj = dproj_ref[...]
        dhn = _dot(d_proj, w_ref[...])
        dw_ref[...] += _dot_tn(d_proj, hn)
        dg_ref[...] += jnp.sum(dhn * xr, axis=0, keepdims=True)
        a_ = dhn * g
        gx_ref[...] = dh1_ref[...] + r * a_ - xr * (r * jnp.mean(a_ * xr, axis=-1, keepdims=True))

    late_cols = D_MODEL - _EARLY_COLS
    return _call(
        body, name="in_proj_bwd", grid=(rows // tm,),
        in_specs=[_rows(tm, D_MODEL), _rows(tm, D_MODEL), _full((1, D_MODEL)), _full((D_IN, D_MODEL)), _rows(tm, D_IN),
                  pl.BlockSpec(memory_space=pl.ANY)],
        out_specs=[_rows(tm, D_MODEL), _full((D_IN, late_cols)), _full((1, D_MODEL))],
        out_shape=[_sds((rows, D_MODEL)), _sds((D_IN, late_cols)), _sds((1, D_MODEL))],
        compiler_params=_params(52, ("arbitrary",)),
    )(x2d, dh1, g1, w_in_t, d_proj, runs_after)


def _iota(shape, axis):
    return lax.broadcasted_iota(jnp.int32, shape, axis)


def _sum_of_thirds(f, a):
    hi = a.astype(BF16)
    rest = a - hi.astype(F32)
    mid = rest.astype(BF16)
    low = (rest - mid.astype(F32)).astype(BF16)
    return (f(hi) + f(mid)) + f(low)


@jax.custom_vjp
def _pick_rows(e, a):
    return _sum_of_thirds(lambda part: _dot(e, part), a)


def _pick_rows_fwd(e, a):
    return _pick_rows(e, a), e


def _pick_rows_bwd(e, ct):
    return jnp.zeros_like(e), _sum_of_thirds(lambda part: _dot_tn(e, part), ct)


_pick_rows.defvjp(_pick_rows_fwd, _pick_rows_bwd)


@jax.custom_vjp
def _pick_cols(a, e):
    return _sum_of_thirds(lambda part: _dot(part, e), a)


def _pick_cols_fwd(a, e):
    return _pick_cols(a, e), e


def _pick_cols_bwd(e, ct):
    return _sum_of_thirds(lambda part: _dot_nt(part, e), ct), jnp.zeros_like(e)


_pick_cols.defvjp(_pick_cols_fwd, _pick_cols_bwd)


_HALF_GROUPS = SSM_GROUPS // 2
_N_SHIFT = SSM_STATE.bit_length() - 1
_P_SHIFT = SSM_GROUP_CH.bit_length() - 1


def _s5_operands(lam_re, lam_im, log_step, b_re, b_im, c_re, c_im):
    g, n, p = SSM_GROUPS, SSM_STATE, SSM_GROUP_CH
    gn, gp, hn_, hp = g * n, g * p, _HALF_GROUPS * n, _HALF_GROUPS * p
    eye_g = _iota((g, g), 0) == _iota((g, g), 1)
    step = jnp.sum(jnp.where(eye_g, jnp.exp(log_step), 0.0), axis=1, keepdims=True)
    a_re = lam_re * step
    a_im = lam_im * step
    mag = jnp.exp(a_re)
    lbar_re = mag * jnp.cos(a_im)
    lbar_im = mag * jnp.sin(a_im)
    n_re = lbar_re - 1.0
    den = lam_re * lam_re + lam_im * lam_im
    f_re = (n_re * lam_re + lbar_im * lam_im) / den
    f_im = (lbar_im * lam_re - n_re * lam_im) / den

    spread_n = (_iota((n, gn), 0) == (_iota((n, gn), 1) & (n - 1))).astype(BF16)
    own_g = _iota((g, gn), 0) == (_iota((g, gn), 1) >> _N_SHIFT)

    def to_row(a):
        return jnp.sum(jnp.where(own_g, _pick_cols(a, spread_n), 0.0), axis=0, keepdims=True)

    per_group = ((_iota((gp, g), 0) >> _P_SHIFT) == _iota((gp, g), 1)).astype(BF16)
    fx_re, fx_im = _pick_rows(per_group, f_re), _pick_rows(per_group, f_im)
    bbar_re = fx_re * b_re - fx_im * b_im
    bbar_im = fx_re * b_im + fx_im * b_re

    tile_n = (_iota((n, hn_), 0) == (_iota((n, hn_), 1) & (n - 1))).astype(BF16)
    same_group = (_iota((hp, hn_), 0) >> _P_SHIFT) == (_iota((hp, hn_), 1) >> _N_SHIFT)

    def embed(a, hf):
        return jnp.where(same_group, _pick_cols(a[hf * hp:(hf + 1) * hp], tile_n), 0.0)

    return (to_row(lbar_re), to_row(lbar_im), embed(bbar_re, 0), embed(bbar_re, 1), embed(bbar_im, 0),
            embed(bbar_im, 1), embed(c_re, 0), embed(c_re, 1), embed(c_im, 0), embed(c_im, 1))


_S5_PARAM_SHAPES = ((SSM_GROUPS, SSM_STATE), (SSM_GROUPS, SSM_STATE), (1, SSM_GROUPS),
                    (D_SSM, SSM_STATE), (D_SSM, SSM_STATE), (D_SSM, SSM_STATE), (D_SSM, SSM_STATE))
_CM_SHAPE = (2, _HALF_GROUPS * SSM_GROUP_CH, _HALF_GROUPS * SSM_STATE)
_S5_OPERAND_SHAPES = ((1, SSM_LANES), (1, SSM_LANES), _CM_SHAPE, _CM_SHAPE, _CM_SHAPE, _CM_SHAPE)


def _s5_params_fwd(*params):
    def body(*refs):
        ins, (lre_ref, lim_ref, btre_ref, btim_ref, cmre_ref, cmim_ref) = refs[:7], refs[7:]
        vals = _s5_operands(*[r[...] for r in ins])
        lre_ref[...] = vals[0]
        lim_ref[...] = vals[1]
        for ref, pair in zip((btre_ref, btim_ref, cmre_ref, cmim_ref), (vals[2:4], vals[4:6], vals[6:8], vals[8:10])):
            ref[0] = pair[0].astype(BF16)
            ref[1] = pair[1].astype(BF16)

    dtypes = (F32, F32, BF16, BF16, BF16, BF16)
    return _call(
        body, name="s5_params_fwd",
        in_specs=[_full(s) for s in _S5_PARAM_SHAPES], out_specs=[_full(s) for s in _S5_OPERAND_SHAPES],
        out_shape=[_sds(s, d) for s, d in zip(_S5_OPERAND_SHAPES, dtypes)], compiler_params=_params(32),
    )(*params)


_BC_SIDE_BY_SIDE = (D_SSM, 4 * SSM_STATE)


def _s5_params_bwd(params, cotangents, runs_after):
    def body(*refs):
        ins, (dlre, dlim, dbtre, dbtim, dcmre, dcmim), outs = refs[:7], refs[7:13], refs[14:]
        _, vjp = jax.vjp(_s5_operands, *[r[...] for r in ins])
        cts = (dlre[...], dlim[...], dbtre[0], dbtre[1], dbtim[0], dbtim[1], dcmre[0], dcmre[1], dcmim[0], dcmim[1])
        grads = vjp(cts)
        for ref, val in zip(outs[:3], grads[:3]):
            ref[...] = val
        outs[3][...] = jnp.concatenate(grads[3:], axis=1).astype(BF16)

    out_shapes = _S5_PARAM_SHAPES[:3] + (_BC_SIDE_BY_SIDE,)
    return _call(
        body, name="s5_params_bwd",
        in_specs=[_full(s) for s in _S5_PARAM_SHAPES + _S5_OPERAND_SHAPES] + [pl.BlockSpec(memory_space=pl.ANY)],
        out_specs=[_full(s) for s in out_shapes],
        out_shape=[_sds(s, d) for s, d in zip(out_shapes, (F32, F32, F32, BF16))], compiler_params=_params(48),
    )(*params, *cotangents, runs_after)


def _scan_geometry(n_seq, seq):
    slab = n_seq * SCAN_CHUNKS
    steps = seq // SCAN_CHUNKS
    tile_rows = slab * SCAN_TILE_STEPS
    n_tiles = steps // SCAN_TILE_STEPS
    return slab, steps, tile_rows, n_tiles


_SCAN_PARTS = D_SSM // LANES


def _whole_parts(rows):
    return [_full((rows, LANES))] * _SCAN_PARTS


def _part_shapes(rows):
    return [_sds((rows, LANES))] * _SCAN_PARTS


def _load_chunks(parts, first_chunk, n_chunks, steps, slab):
    return jnp.concatenate([
        jnp.concatenate([ref[pl.ds(first_chunk + q, steps, stride=slab), :] for ref in parts], axis=1)
        for q in range(n_chunks)], axis=0)


def _store_chunks(parts, first_chunk, value, steps, slab):
    for q in range(value.shape[0] // steps):
        for j, ref in enumerate(parts):
            ref[pl.ds(first_chunk + q, steps, stride=slab), :] = value[q * steps:(q + 1) * steps,
                                                                     j * LANES:(j + 1) * LANES]


def _join_parts(parts):
    return jnp.concatenate([ref[...] for ref in parts], axis=1)


def _split_parts(parts, value):
    for j, ref in enumerate(parts):
        ref[...] = value[:, j * LANES:(j + 1) * LANES]


def _complex_power(re, im, n):
    out = None
    while n:
        if n & 1:
            out = (re, im) if out is None else (out[0] * re - out[1] * im, out[0] * im + out[1] * re)
        n >>= 1
        if n:
            re, im = re * re - im * im, 2.0 * re * im
    return out


def _chunk_carry(sum_re, sum_im, carry_re, carry_im, a_re, a_im, n_seq, reverse):
    carry_re[...] = jnp.zeros_like(carry_re)
    carry_im[...] = jnp.zeros_like(carry_im)
    for s in range(n_seq):
        order = range(SCAN_CHUNKS - 2, -1, -1) if reverse else range(1, SCAN_CHUNKS)
        for c in order:
            r = s * SCAN_CHUNKS + c
            p = r + 1 if reverse else r - 1
            p_re, p_im = carry_re[p:p + 1, :], carry_im[p:p + 1, :]
            carry_re[r:r + 1, :] = a_re * p_re - a_im * p_im + sum_re[p:p + 1, :]
            carry_im[r:r + 1, :] = a_re * p_im + a_im * p_re + sum_im[p:p + 1, :]


def _s5_scan_fwd(u_parts, bt_re, bt_im, cm_re, cm_im, lbar_re, lbar_im, d_row, n_seq, seq):
    slab, steps, tile_rows, n_tiles = _scan_geometry(n_seq, seq)
    rows = u_parts[0].shape[0]

    def body(*refs):
        u_refs, refs = refs[:_SCAN_PARTS], refs[_SCAN_PARTS:]
        (bre_ref, bim_ref, cre_ref, cim_ref, lre_ref, lim_ref, d_ref), refs = refs[:7], refs[7:]
        y_refs, (hre_ref, him_ref, st_re, st_im, h0_re, h0_im, buf_re, buf_im) = refs[:_SCAN_PARTS], refs[_SCAN_PARTS:]
        second = pl.program_id(0) == 1
        i = pl.program_id(1)

        @pl.when(jnp.logical_and(i == 0, jnp.logical_not(second)))
        def _():
            st_re[...] = jnp.zeros_like(st_re)
            st_im[...] = jnp.zeros_like(st_im)

        u = _join_parts(u_refs)
        ub = u.astype(BF16)
        for hf in range(2):
            cols = slice(hf * 1024, (hf + 1) * 1024)
            buf_re[:, cols] = _dot(ub[:, hf * 256:(hf + 1) * 256], bre_ref[hf])
            buf_im[:, cols] = _dot(ub[:, hf * 256:(hf + 1) * 256], bim_ref[hf])

        for lc in range(SSM_LANES // SCAN_LANE_CHUNK):
            cols = slice(lc * SCAN_LANE_CHUNK, (lc + 1) * SCAN_LANE_CHUNK)
            l_re = jnp.broadcast_to(lre_ref[:, cols], (slab, SCAN_LANE_CHUNK))
            l_im = jnp.broadcast_to(lim_ref[:, cols], (slab, SCAN_LANE_CHUNK))

            def scan_tile(keep_states):
                def step(t, carry):
                    s_re, s_im = carry
                    r0 = pl.multiple_of(t * slab, slab)
                    n_re = l_re * s_re - l_im * s_im + buf_re[pl.ds(r0, slab), cols]
                    n_im = l_re * s_im + l_im * s_re + buf_im[pl.ds(r0, slab), cols]
                    if keep_states:
                        buf_re[pl.ds(r0, slab), cols] = n_re
                        buf_im[pl.ds(r0, slab), cols] = n_im
                    return n_re, n_im

                s_re, s_im = lax.fori_loop(0, SCAN_TILE_STEPS, step, (st_re[:, cols], st_im[:, cols]), unroll=True)
                st_re[:, cols] = s_re
                st_im[:, cols] = s_im

            pl.when(jnp.logical_not(second))(lambda: scan_tile(False))
            pl.when(second)(lambda: scan_tile(True))

        @pl.when(jnp.logical_and(i == n_tiles - 1, jnp.logical_not(second)))
        def _():
            a_re, a_im = _complex_power(lre_ref[...], lim_ref[...], steps)
            _chunk_carry(st_re, st_im, h0_re, h0_im, a_re, a_im, n_seq, reverse=False)
            st_re[...] = h0_re[...]
            st_im[...] = h0_im[...]

        @pl.when(second)
        def _():
            h_re = buf_re[...].astype(BF16)
            h_im = buf_im[...].astype(BF16)
            hre_ref[...] = h_re
            him_ref[...] = h_im
            for hf in range(2):
                cols = slice(hf * 1024, (hf + 1) * 1024)
                ycols = slice(hf * 256, (hf + 1) * 256)
                y_half = (_dot_nt(h_re[:, cols], cre_ref[hf]) - _dot_nt(h_im[:, cols], cim_ref[hf])
                          + d_ref[:, ycols] * u[:, ycols])
                _split_parts(y_refs[2 * hf:2 * hf + 2], y_half)

    tile = lambda w: pl.BlockSpec((tile_rows, w), lambda p, i: (i, 0))
    out_tile = lambda w: pl.BlockSpec((tile_rows, w), lambda p, i: (i * p, 0))
    cm = _full(_CM_SHAPE)
    outs = _call(
        body, name="s5_scan_fwd", grid=(2, n_tiles),
        in_specs=[tile(LANES)] * _SCAN_PARTS + [cm, cm, cm, cm, _full((1, SSM_LANES)), _full((1, SSM_LANES)),
                                                _full((1, 512))],
        out_specs=[out_tile(LANES)] * _SCAN_PARTS + [out_tile(SSM_LANES), out_tile(SSM_LANES)],
        out_shape=_part_shapes(rows) + [_sds((rows, SSM_LANES), BF16), _sds((rows, SSM_LANES), BF16)],
        scratch_shapes=[pltpu.VMEM((slab, SSM_LANES), F32)] * 4 + [pltpu.VMEM((tile_rows, SSM_LANES), F32)] * 2,
        compiler_params=_params(40, ("arbitrary", "arbitrary")),
    )(*u_parts, bt_re, bt_im, cm_re, cm_im, lbar_re, lbar_im, d_row)
    return outs[:_SCAN_PARTS], outs[_SCAN_PARTS], outs[_SCAN_PARTS + 1]


def _s5_scan_bwd(dy_parts, u_parts, h_re, h_im, bt_re, bt_im, cm_re, cm_im, lbar_re, lbar_im, d_row, n_seq, seq):
    slab, steps, tile_rows, n_tiles = _scan_geometry(n_seq, seq)
    rows = u_parts[0].shape[0]

    def body(*refs):
        dy_refs, u_refs, refs = refs[:_SCAN_PARTS], refs[_SCAN_PARTS:2 * _SCAN_PARTS], refs[2 * _SCAN_PARTS:]
        (hre_ref, him_ref, bre_ref, bim_ref, cre_ref, cim_ref, lre_ref, lim_ref, d_ref), refs = refs[:9], refs[9:]
        du_refs, refs = refs[:_SCAN_PARTS], refs[_SCAN_PARTS:]
        (dbre_ref, dbim_ref, dcre_ref, dcim_ref, dlre_ref, dlim_ref, dd_ref,
         st_re, st_im, g0_re, g0_im, acc_re, acc_im, buf_re, buf_im) = refs
        second = pl.program_id(0) == 1
        i = pl.program_id(1)

        @pl.when(jnp.logical_and(i == 0, jnp.logical_not(second)))
        def _():
            st_re[...] = jnp.zeros_like(st_re)
            st_im[...] = jnp.zeros_like(st_im)
            acc_re[...] = jnp.zeros_like(acc_re)
            acc_im[...] = jnp.zeros_like(acc_im)
            for ref in (dbre_ref, dbim_ref, dcre_ref, dcim_ref, dd_ref):
                ref[...] = jnp.zeros_like(ref)

        dy = _join_parts(dy_refs)
        dyb = dy.astype(BF16)
        for hf in range(2):
            cols = slice(hf * 1024, (hf + 1) * 1024)
            buf_re[:, cols] = _dot(dyb[:, hf * 256:(hf + 1) * 256], cre_ref[hf])
            buf_im[:, cols] = -_dot(dyb[:, hf * 256:(hf + 1) * 256], cim_ref[hf])

        for lc in range(SSM_LANES // SCAN_LANE_CHUNK):
            cols = slice(lc * SCAN_LANE_CHUNK, (lc + 1) * SCAN_LANE_CHUNK)
            l_re = jnp.broadcast_to(lre_ref[:, cols], (slab, SCAN_LANE_CHUNK))
            l_im = jnp.broadcast_to(lim_ref[:, cols], (slab, SCAN_LANE_CHUNK))

            def advance(r0, s_re, s_im):
                n_re = l_re * s_re + l_im * s_im + buf_re[pl.ds(r0, slab), cols]
                n_im = l_re * s_im - l_im * s_re + buf_im[pl.ds(r0, slab), cols]
                buf_re[pl.ds(r0, slab), cols] = n_re
                buf_im[pl.ds(r0, slab), cols] = n_im
                return n_re, n_im

            def row0(k):
                return pl.multiple_of((SCAN_TILE_STEPS - 1 - k) * slab, slab)

            @pl.when(jnp.logical_not(second))
            def _():
                s_re, s_im = lax.fori_loop(0, SCAN_TILE_STEPS, lambda k, s: advance(row0(k), *s),
                                           (st_re[:, cols], st_im[:, cols]), unroll=True)
                st_re[:, cols] = s_re
                st_im[:, cols] = s_im

            @pl.when(second)
            def _():
                def step(k, carry):
                    s_re, s_im, a_re, a_im = carry
                    r0 = row0(k)
                    hr = hre_ref[pl.ds(r0, slab), cols].astype(F32)
                    hi = him_ref[pl.ds(r0, slab), cols].astype(F32)
                    a_re = a_re + s_re * hr + s_im * hi
                    a_im = a_im + s_im * hr - s_re * hi
                    return advance(r0, s_re, s_im) + (a_re, a_im)

                zero = jnp.zeros((slab, SCAN_LANE_CHUNK), F32)
                s_re, s_im, a_re, a_im = lax.fori_loop(
                    0, SCAN_TILE_STEPS, step, (st_re[:, cols], st_im[:, cols], zero, zero), unroll=True)
                st_re[:, cols] = s_re
                st_im[:, cols] = s_im
                acc_re[:, cols] += a_re
                acc_im[:, cols] += a_im

        @pl.when(jnp.logical_and(i == n_tiles - 1, jnp.logical_not(second)))
        def _():
            p_re, p_im = _complex_power(lre_ref[...], lim_ref[...], steps)
            _chunk_carry(st_re, st_im, g0_re, g0_im, p_re, -p_im, n_seq, reverse=True)
            st_re[...] = g0_re[...]
            st_im[...] = g0_im[...]

        @pl.when(second)
        def _():
            u = _join_parts(u_refs)
            ub = u.astype(BF16)
            g_re = buf_re[...].astype(BF16)
            g_im = buf_im[...].astype(BF16)
            dd_ref[...] += jnp.sum(dy * u, axis=0, keepdims=True)
            for hf in range(2):
                cols = slice(hf * 1024, (hf + 1) * 1024)
                ycols = slice(hf * 256, (hf + 1) * 256)
                du_half = (_dot_nt(g_re[:, cols], bre_ref[hf]) + _dot_nt(g_im[:, cols], bim_ref[hf])
                           + d_ref[:, ycols] * dy[:, ycols])
                _split_parts(du_refs[2 * hf:2 * hf + 2], du_half)
                for q4 in range(_HALF_GROUPS // 4):
                    ch = slice(hf * 256 + q4 * 64, hf * 256 + (q4 + 1) * 64)
                    st = slice(hf * 1024 + q4 * 256, hf * 1024 + (q4 + 1) * 256)
                    blk = (hf, slice(q4 * 64, (q4 + 1) * 64), slice(q4 * 256, (q4 + 1) * 256))
                    dbre_ref[blk] += _dot_tn(ub[:, ch], g_re[:, st])
                    dbim_ref[blk] += _dot_tn(ub[:, ch], g_im[:, st])
                    dcre_ref[blk] += _dot_tn(dyb[:, ch], hre_ref[:, st])
                    dcim_ref[blk] -= _dot_tn(dyb[:, ch], him_ref[:, st])

        @pl.when(jnp.logical_and(i == n_tiles - 1, second))
        def _():
            dlre_ref[...] = jnp.sum(acc_re[...], axis=0, keepdims=True)
            dlim_ref[...] = jnp.sum(acc_im[...], axis=0, keepdims=True)

    tile = lambda w: pl.BlockSpec((tile_rows, w), lambda p, i: (n_tiles - 1 - i, 0))
    second_tile = lambda w: pl.BlockSpec((tile_rows, w), lambda p, i: (n_tiles - 1 - i * p, 0))
    cm = _full(_CM_SHAPE)
    row = _full((1, SSM_LANES))
    outs = _call(
        body, name="s5_scan_bwd", grid=(2, n_tiles),
        in_specs=[tile(LANES)] * _SCAN_PARTS + [second_tile(LANES)] * _SCAN_PARTS
        + [second_tile(SSM_LANES), second_tile(SSM_LANES), cm, cm, cm, cm, row, row, _full((1, 512))],
        out_specs=[second_tile(LANES)] * _SCAN_PARTS + [cm, cm, cm, cm, row, row, _full((1, 512))],
        out_shape=(_part_shapes(rows) + [_sds(_CM_SHAPE)] * 4 + [_sds((1, SSM_LANES))] * 2 + [_sds((1, 512))]),
        scratch_shapes=[pltpu.VMEM((slab, SSM_LANES), F32)] * 6 + [pltpu.VMEM((tile_rows, SSM_LANES), F32)] * 2,
        compiler_params=_params(48, ("arbitrary", "arbitrary")),
    )(*dy_parts, *u_parts, h_re, h_im, bt_re, bt_im, cm_re, cm_im, lbar_re, lbar_im, d_row)
    return (outs[:_SCAN_PARTS],) + tuple(outs[_SCAN_PARTS:])


def _glu_gate(gl, a, zs):
    return gl * jax.nn.sigmoid(a) * _silu(zs)


def _glu_fwd(y_parts, zs, w_glu, b_glu, n_seq, seq):
    rows = zs.shape[0]
    tm = 1024
    slab, steps, _, _ = _scan_geometry(n_seq, seq)

    def body(*refs):
        y_refs, (zs_ref, w_ref, b_ref, o_ref) = refs[:_SCAN_PARTS], refs[_SCAN_PARTS:]
        y = _load_chunks(y_refs, pl.program_id(0) * (tm // steps), tm // steps, steps, slab)
        gl = jax.nn.gelu(y)
        a = _dot(gl.astype(BF16), w_ref[...]) + b_ref[...]
        o_ref[...] = _glu_gate(gl, a, zs_ref[...]).astype(BF16)

    return _call(
        body, name="glu_fwd", grid=(rows // tm,),
        in_specs=_whole_parts(rows) + [_rows(tm, 512), _full((512, 512)), _full((1, 512))],
        out_specs=_rows(tm, 512), out_shape=_sds((rows, 512), BF16),
        compiler_params=_params(32, ("arbitrary",)),
    )(*y_parts, zs, w_glu, b_glu)


def _glu_bwd(y_parts, zs, d_out, w_glu, b_glu, n_seq, seq):
    rows = zs.shape[0]
    tm = 512
    slab, steps, _, _ = _scan_geometry(n_seq, seq)

    def body(*refs):
        y_refs, (zs_ref, d_ref, w_ref, b_ref), refs = refs[:_SCAN_PARTS], refs[_SCAN_PARTS:_SCAN_PARTS + 4], refs[_SCAN_PARTS + 4:]
        dy_refs, (dzs_ref, dw_ref, db_ref) = refs[:_SCAN_PARTS], refs[_SCAN_PARTS:]
        first_chunk = pl.program_id(0) * (tm // steps)

        @pl.when(pl.program_id(0) == 0)
        def _():
            dw_ref[...] = jnp.zeros_like(dw_ref)
            db_ref[...] = jnp.zeros_like(db_ref)

        gl, gelu_vjp = jax.vjp(jax.nn.gelu, _load_chunks(y_refs, first_chunk, tm // steps, steps, slab))
        glb = gl.astype(BF16)
        a = _dot(glb, w_ref[...]) + b_ref[...]
        _, gate_vjp = jax.vjp(_glu_gate, gl, a, zs_ref[...])
        d_gl, d_a, d_zs = gate_vjp(d_ref[...])
        dab = d_a.astype(BF16)
        d_gl = d_gl + _dot_nt(dab, w_ref[...])
        _store_chunks(dy_refs, first_chunk, gelu_vjp(d_gl)[0], steps, slab)
        dzs_ref[...] = d_zs.astype(BF16)
        dw_ref[...] += _dot_tn(glb, dab)
        db_ref[...] += jnp.sum(d_a, axis=0, keepdims=True)

    *dy_parts, dzs, dw, db = _call(
        body, name="glu_bwd", grid=(rows // tm,),
        in_specs=_whole_parts(rows) + [_rows(tm, 512), _rows(tm, 512), _full((512, 512)), _full((1, 512))],
        out_specs=_whole_parts(rows) + [_rows(tm, 512), _full((512, 512)), _full((1, 512))],
        out_shape=_part_shapes(rows) + [_sds((rows, 512), BF16), _sds((512, 512)), _sds((1, 512))],
        compiler_params=_params(40, ("arbitrary",)),
    )(*y_parts, zs, d_out, w_glu, b_glu)
    return dy_parts, dzs, dw, db


_GROUP_ROWS = Q_PER_KV * BLOCK
_BLOCK_SHIFT = BLOCK.bit_length() - 1


def _attn_bias(j):
    query = _iota((BLOCK, _GROUP_ROWS), 1)
    dist_cur = (query & (BLOCK - 1)) - _iota((BLOCK, _GROUP_ROWS), 0)
    dist_prev = dist_cur + BLOCK
    head = query >> _BLOCK_SHIFT
    slope = jnp.zeros((BLOCK, _GROUP_ROWS), F32)
    for g in range(Q_PER_KV):
        slope = jnp.where(head == g, 2.0 ** (-(j * Q_PER_KV + g + 1)), slope)
    bias_cur = jnp.where(dist_cur >= 0, -slope * dist_cur.astype(F32), -jnp.inf)
    bias_prev = jnp.where(dist_prev < WINDOW, -slope * dist_prev.astype(F32), -jnp.inf)
    return bias_cur, bias_prev


_ATTN_BIAS_SCRATCH = pltpu.VMEM((KV_HEADS, 2, BLOCK, _GROUP_ROWS), F32)


def _fill_attn_bias(bias_ref):
    @pl.when(jnp.logical_and(pl.program_id(0) == 0, pl.program_id(1) == 0))
    def _():
        for j in range(KV_HEADS):
            bias_ref[j, 0], bias_ref[j, 1] = _attn_bias(j)


def _stack_heads(x, j):
    heads = range(j * Q_PER_KV, (j + 1) * Q_PER_KV)
    return jnp.concatenate([x[:, h * HEAD_DIM:(h + 1) * HEAD_DIM] for h in heads], axis=0)


def _head_rows(x, j):
    heads = range(j * Q_PER_KV, (j + 1) * Q_PER_KV)
    return jnp.concatenate([x[h:h + 1, :] for h in heads], axis=1)


def _sink_row(sk_ref, j):
    heads = range(j * Q_PER_KV, (j + 1) * Q_PER_KV)
    return jnp.concatenate([jnp.broadcast_to(sk_ref[0:1, h:h + 1], (1, BLOCK)) for h in heads], axis=1)


_ATTN_FWD_BLOCKS = 8


def _attn_fwd(q, k, v, za, sinks, n_seq, seq):
    nb = seq // BLOCK
    steps = nb // _ATTN_FWD_BLOCKS
    rows = q.shape[0]

    def body(q_ref, kc_ref, kp_ref, vc_ref, vp_ref, za_ref, sk_ref, o_ref, ao_ref, lse_ref, bias_ref):
        _fill_attn_bias(bias_ref)
        for t in range(_ATTN_FWD_BLOCKS):
            at = slice(t * BLOCK, (t + 1) * BLOCK)
            before = slice((t - 1) * BLOCK, t * BLOCK)
            q_all = q_ref[at, :]
            for j in range(KV_HEADS):
                js = slice(j * HEAD_DIM, (j + 1) * HEAD_DIM)
                bias_c, bias_p = bias_ref[j, 0], bias_ref[j, 1]
                q4 = _stack_heads(q_all, j)
                sc = _dot_nt(kc_ref[at, js], q4) + bias_c
                if t == 0:
                    sp = _dot_nt(kp_ref[:, js], q4) + jnp.where(pl.program_id(1) > 0, bias_p, -jnp.inf)
                    v_prev = vp_ref[:, js]
                else:
                    sp = _dot_nt(kc_ref[before, js], q4) + bias_p
                    v_prev = vc_ref[before, js]
                sink = _sink_row(sk_ref, j)
                m = jnp.maximum(jnp.max(jnp.maximum(sc, sp), axis=0, keepdims=True), sink)
                ec = jnp.exp(sc - m)
                ep = jnp.exp(sp - m)
                den = jnp.sum(ec + ep, axis=0, keepdims=True) + jnp.exp(sink - m)
                inv = 1.0 / den
                o4 = _dot_tn((ec * inv).astype(BF16), vc_ref[at, js]) + _dot_tn((ep * inv).astype(BF16), v_prev)
                lse4 = m + jnp.log(den)
                for g in range(Q_PER_KV):
                    h = j * Q_PER_KV + g
                    o_ref[at, h * HEAD_DIM:(h + 1) * HEAD_DIM] = o4[g * BLOCK:(g + 1) * BLOCK]
                    lse_ref[t * N_HEADS + h:t * N_HEADS + h + 1, :] = lse4[:, g * BLOCK:(g + 1) * BLOCK]
        ao_ref[...] = (o_ref[...] * _silu(za_ref[...])).astype(BF16)

    cur = lambda w: pl.BlockSpec((_ATTN_FWD_BLOCKS * BLOCK, w), lambda b, n: (b * steps + n, 0))
    prev = lambda w: pl.BlockSpec((BLOCK, w), lambda b, n: (b * nb + jnp.maximum(_ATTN_FWD_BLOCKS * n - 1, 0), 0))
    lse_rows = rows // BLOCK * N_HEADS
    return _call(
        body, name="attn_fwd", grid=(n_seq, steps),
        in_specs=[cur(512), cur(128), prev(128), cur(128), prev(128), cur(512), _full((1, N_HEADS))],
        out_specs=[cur(512), cur(512),
                   pl.BlockSpec((_ATTN_FWD_BLOCKS * N_HEADS, BLOCK), lambda b, n: (b * steps + n, 0))],
        out_shape=[_sds((rows, 512)), _sds((rows, 512), BF16), _sds((lse_rows, BLOCK))],
        scratch_shapes=[_ATTN_BIAS_SCRATCH], compiler_params=_params(32, ("arbitrary", "arbitrary")),
    )(q, k, k, v, v, za, sinks)


def _in_proj_attn_fwd(hn, w_in_t, sinks, n_seq, seq):
    rows = hn.shape[0]
    tm = _ATTN_FWD_BLOCKS * BLOCK
    tiles_per_seq = seq // tm
    slab, steps, _, _ = _scan_geometry(n_seq, seq)

    def body(hn_ref, w_ref, sk_ref, *refs):
        u_parts = refs[:_SCAN_PARTS]
        (zs_ref, q_ref, k_ref, v_ref, za_ref, o_ref, ao_ref, lse_ref, bias_ref, k_last, v_last) = refs[_SCAN_PARTS:]
        i = pl.program_id(0)

        @pl.when(i == 0)
        def _():
            for j in range(KV_HEADS):
                bias_ref[j, 0], bias_ref[j, 1] = _attn_bias(j)
            k_last[...] = jnp.zeros_like(k_last)
            v_last[...] = jnp.zeros_like(v_last)

        for_attn = _dot_nt(hn_ref[...], w_ref[2 * D_SSM:, :])
        q_ref[...] = (for_attn[:, 0:512] * ATTN_SCALE).astype(BF16)
        k_ref[...] = for_attn[:, 512:640].astype(BF16)
        v_ref[...] = for_attn[:, 640:768].astype(BF16)
        za_ref[...] = for_attn[:, 768:1280]

        has_before = lax.rem(i, tiles_per_seq) > 0
        for t in range(_ATTN_FWD_BLOCKS):
            at = slice(t * BLOCK, (t + 1) * BLOCK)
            before = slice((t - 1) * BLOCK, t * BLOCK)
            q_all = q_ref[at, :]
            for j in range(KV_HEADS):
                js = slice(j * HEAD_DIM, (j + 1) * HEAD_DIM)
                bias_c, bias_p = bias_ref[j, 0], bias_ref[j, 1]
                q4 = _stack_heads(q_all, j)
                sc = _dot_nt(k_ref[at, js], q4) + bias_c
                if t == 0:
                    sp = _dot_nt(k_last[:, js], q4) + jnp.where(has_before, bias_p, -jnp.inf)
                    v_prev = v_last[:, js]
                else:
                    sp = _dot_nt(k_ref[before, js], q4) + bias_p
                    v_prev = v_ref[before, js]
                sink = _sink_row(sk_ref, j)
                m = jnp.maximum(jnp.max(jnp.maximum(sc, sp), axis=0, keepdims=True), sink)
                ec = jnp.exp(sc - m)
                ep = jnp.exp(sp - m)
                den = jnp.sum(ec + ep, axis=0, keepdims=True) + jnp.exp(sink - m)
                inv = 1.0 / den
                o4 = _dot_tn((ec * inv).astype(BF16), v_ref[at, js]) + _dot_tn((ep * inv).astype(BF16), v_prev)
                lse4 = m + jnp.log(den)
                for g in range(Q_PER_KV):
                    h = j * Q_PER_KV + g
                    o_ref[at, h * HEAD_DIM:(h + 1) * HEAD_DIM] = o4[g * BLOCK:(g + 1) * BLOCK]
                    lse_ref[t * N_HEADS + h:t * N_HEADS + h + 1, :] = lse4[:, g * BLOCK:(g + 1) * BLOCK]
        ao_ref[...] = (o_ref[...] * _silu(za_ref[...])).astype(BF16)
        k_last[...] = k_ref[tm - BLOCK:, :]
        v_last[...] = v_ref[tm - BLOCK:, :]

        for_ssm = _dot_nt(hn_ref[...], w_ref[0:2 * D_SSM, :])
        _store_chunks(u_parts, i * (tm // steps), for_ssm[:, 0:D_SSM], steps, slab)
        zs_ref[...] = for_ssm[:, D_SSM:]

    lse_rows = rows // BLOCK * N_HEADS
    *u_parts, zs, q, k, v, za, o, attn_out, lse = _call(
        body, name="in_proj_attn_fwd", grid=(rows // tm,),
        in_specs=[_rows(tm, D_MODEL), _full((D_IN, D_MODEL)), _full((1, N_HEADS))],
        out_specs=_whole_parts(rows) + [_rows(tm, 512), _rows(tm, 512), _rows(tm, 128), _rows(tm, 128), _rows(tm, 512),
                                        _rows(tm, 512), _rows(tm, 512),
                                        pl.BlockSpec((_ATTN_FWD_BLOCKS * N_HEADS, BLOCK), lambda i: (i, 0))],
        out_shape=_part_shapes(rows) + [_sds((rows, 512)), _sds((rows, 512), BF16), _sds((rows, 128), BF16),
                                        _sds((rows, 128), BF16), _sds((rows, 512)), _sds((rows, 512)),
                                        _sds((rows, 512), BF16), _sds((lse_rows, BLOCK))],
        scratch_shapes=[_ATTN_BIAS_SCRATCH, pltpu.VMEM((BLOCK, KV_HEADS * HEAD_DIM), BF16),
                        pltpu.VMEM((BLOCK, KV_HEADS * HEAD_DIM), BF16)],
        compiler_params=_params(56, ("arbitrary",)),
    )(hn, w_in_t, sinks)
    return u_parts, zs, q, k, v, za, o, attn_out, lse


_ATTN_BWD_BLOCKS = 4


def _attn_bwd(q, k, v, za, o, lse, d_ao, sinks, n_seq, seq):
    nb = seq // BLOCK
    per_step = _ATTN_BWD_BLOCKS
    steps = nb // per_step
    rows = q.shape[0]

    def body(q_ref, kc_ref, kp_ref, vc_ref, vp_ref, za_ref, o_ref, lse_ref, d_ref, sk_ref,
             dq_ref, dk_ref, dv_ref, dza_ref, dsk_ref, bias_ref, dk_carry, dv_carry):
        step = pl.program_id(1)
        first_block = nb - per_step * (step + 1)
        _fill_attn_bias(bias_ref)

        @pl.when(jnp.logical_and(pl.program_id(0) == 0, step == 0))
        def _():
            dsk_ref[...] = jnp.zeros_like(dsk_ref)
            dk_carry[...] = jnp.zeros_like(dk_carry)
            dv_carry[...] = jnp.zeros_like(dv_carry)

        _, gate_vjp = jax.vjp(lambda o_, z_: o_ * _silu(z_), o_ref[...], za_ref[...])
        d_o, d_za = gate_vjp(d_ref[...])
        dza_ref[...] = d_za.astype(BF16)

        for j in range(KV_HEADS):
            js = slice(j * HEAD_DIM, (j + 1) * HEAD_DIM)
            bias_c, bias_p = bias_ref[j, 0], bias_ref[j, 1]
            sink = _sink_row(sk_ref, j)
            sink_loss = jnp.zeros((1, _GROUP_ROWS), F32)
            dk_from_next = jnp.where(step > 0, dk_carry[j], 0.0)
            dv_from_next = jnp.where(step > 0, dv_carry[j], 0.0)
            for t in reversed(range(per_step)):
                at = slice(t * BLOCK, (t + 1) * BLOCK)
                kc, vc = kc_ref[at, js], vc_ref[at, js]
                if t > 0:
                    before = slice((t - 1) * BLOCK, t * BLOCK)
                    kp, vp, bias_before = kc_ref[before, js], vc_ref[before, js], bias_p
                else:
                    kp, vp, bias_before = kp_ref[:, js], vp_ref[:, js], jnp.where(first_block > 0, bias_p, -jnp.inf)
                q4 = _stack_heads(q_ref[at, :], j)
                do4b = _stack_heads(d_o[at], j).astype(BF16)
                lse4 = _head_rows(lse_ref[t * N_HEADS:(t + 1) * N_HEADS, :], j)
                pc = jnp.exp(_dot_nt(kc, q4) + bias_c - lse4)
                pp = jnp.exp(_dot_nt(kp, q4) + bias_before - lse4)
                dpc = _dot_nt(vc, do4b)
                dpp = _dot_nt(vp, do4b)
                delta = jnp.sum(pc * dpc + pp * dpp, axis=0, keepdims=True)
                dsc = (pc * (dpc - delta)).astype(BF16)
                dsp = (pp * (dpp - delta)).astype(BF16)
                dq4 = ((_dot_tn(dsc, kc) + _dot_tn(dsp, kp)) * ATTN_SCALE).astype(BF16)
                sink_loss = sink_loss + jnp.exp(sink - lse4) * delta
                for g in range(Q_PER_KV):
                    h = j * Q_PER_KV + g
                    dq_ref[at, h * HEAD_DIM:(h + 1) * HEAD_DIM] = dq4[g * BLOCK:(g + 1) * BLOCK]
                dk_ref[at, js] = (_dot(dsc, q4) + dk_from_next).astype(BF16)
                dv_ref[at, js] = (_dot(pc.astype(BF16), do4b) + dv_from_next).astype(BF16)
                dk_from_next = _dot(dsp, q4)
                dv_from_next = _dot(pp.astype(BF16), do4b)
            dk_carry[j] = dk_from_next
            dv_carry[j] = dv_from_next
            for g in range(Q_PER_KV):
                h = j * Q_PER_KV + g
                dsk_ref[0:1, h:h + 1] -= jnp.sum(sink_loss[:, g * BLOCK:(g + 1) * BLOCK], axis=1, keepdims=True)

    cur = lambda w: pl.BlockSpec((per_step * BLOCK, w), lambda b, s: (b * steps + steps - 1 - s, 0))
    prev = lambda w: pl.BlockSpec((BLOCK, w), lambda b, s: (b * nb + jnp.maximum(nb - per_step * (s + 1) - 1, 0), 0))
    return _call(
        body, name="attn_bwd", grid=(n_seq, steps),
        in_specs=[cur(512), cur(128), prev(128), cur(128), prev(128), cur(512), cur(512),
                  pl.BlockSpec((per_step * N_HEADS, BLOCK), lambda b, s: (b * steps + steps - 1 - s, 0)), cur(512),
                  _full((1, N_HEADS))],
        out_specs=[cur(512), cur(128), cur(128), cur(512), _full((1, N_HEADS))],
        out_shape=[_sds((rows, 512), BF16), _sds((rows, 128), BF16), _sds((rows, 128), BF16),
                   _sds((rows, 512), BF16), _sds((1, N_HEADS))],
        scratch_shapes=[_ATTN_BIAS_SCRATCH, pltpu.VMEM((KV_HEADS, BLOCK, HEAD_DIM), F32),
                        pltpu.VMEM((KV_HEADS, BLOCK, HEAD_DIM), F32)],
        compiler_params=_params(32, ("arbitrary", "arbitrary")),
    )(q, k, k, v, v, za, o, lse, d_ao, sinks)


def _tail(ssm_out, attn_out, x2d, p2d, target, w_out, g2, w_gate, b_gate, w_proj):
    rows = x2d.shape[0]
    tm = 512

    def body(so_ref, ao_ref, x_ref, p_ref, t_ref, wo_ref, g2_ref, wg_ref, bg_ref, wp_ref,
             dh1_ref, dso_ref, dao_ref, dwo_ref, dwg_ref, dwp_ref, dbg_ref, dg2_ref, loss_ref):
        @pl.when(pl.program_id(0) == 0)
        def _():
            for ref in (dwo_ref, dwg_ref, dwp_ref, dbg_ref, dg2_ref, loss_ref):
                ref[...] = jnp.zeros_like(ref)

        cat = jnp.concatenate([so_ref[...], ao_ref[...]], axis=1)
        g2 = g2_ref[...]
        mixed = _dot(cat, wo_ref[...])
        r = lax.rsqrt(jnp.mean(mixed * mixed, axis=-1, keepdims=True) + EPS)
        mr = mixed * r
        h1 = x_ref[...] + mr * g2
        h1b = h1.astype(BF16)
        gate = jax.nn.sigmoid(_dot(h1b, wg_ref[...]) + bg_ref[...])
        pb = p_ref[...].astype(BF16)
        wp_blocks = [slice(j * D_PLE, (j + 1) * D_PLE) for j in range(N_CHIPS)]
        pp = jnp.concatenate([_dot(pb, wp_ref[blk, :]) for blk in wp_blocks], axis=1)
        err = h1 + gate * pp - t_ref[...]
        loss_ref[...] += 0.5 * jnp.sum(jnp.mean(err * err, axis=-1, keepdims=True), axis=0, keepdims=True)

        dh2 = err * (1.0 / D_MODEL)
        d_glin = dh2 * pp * gate * (1.0 - gate)
        d_glin_b = d_glin.astype(BF16)
        dwg_ref[...] += _dot_tn(h1b, d_glin_b)
        dbg_ref[...] += jnp.sum(d_glin, axis=0, keepdims=True)
        d_pp = (dh2 * gate).astype(BF16)
        for blk in wp_blocks:
            dwp_ref[blk, :] += _dot_tn(pb, d_pp[:, blk])
        dh1 = dh2 + _dot_nt(d_glin_b, wg_ref[...])
        dh1_ref[...] = dh1
        dg2_ref[...] += jnp.sum(dh1 * mr, axis=0, keepdims=True)
        a_ = dh1 * g2
        d_mixed = (r * a_ - mr * (r * jnp.mean(a_ * mr, axis=-1, keepdims=True))).astype(BF16)
        dwo_ref[...] += _dot_tn(cat, d_mixed)
        d_cat = _dot_nt(d_mixed, wo_ref[...])
        dso_ref[...] = d_cat[:, 0:512]
        dao_ref[...] = d_cat[:, 512:1024]

    return _call(
        body, name="tail_fwd_bwd", grid=(rows // tm,),
        in_specs=[_rows(tm, 512), _rows(tm, 512), _rows(tm, D_MODEL), _rows(tm, D_PLE), _rows(tm, D_MODEL),
                  _full((D_MODEL, D_MODEL)), _full((1, D_MODEL)), _full((D_MODEL, D_MODEL)), _full((1, D_MODEL)),
                  _full((N_CHIPS * D_PLE, D_PLE))],
        out_specs=[_rows(tm, D_MODEL), _rows(tm, 512), _rows(tm, 512), _full((D_MODEL, D_MODEL)),
                   _full((D_MODEL, D_MODEL)), _full((N_CHIPS * D_PLE, D_PLE)), _full((1, D_MODEL)), _full((1, D_MODEL)),
                   _full((1, 1))],
        out_shape=[_sds((rows, D_MODEL)), _sds((rows, 512)), _sds((rows, 512)), _sds((D_MODEL, D_MODEL)),
                   _sds((D_MODEL, D_MODEL)), _sds((N_CHIPS * D_PLE, D_PLE)), _sds((1, D_MODEL)), _sds((1, D_MODEL)),
                   _sds((1, 1))],
        compiler_params=_params(52, ("arbitrary",)),
    )(ssm_out, attn_out, x2d, p2d, target, w_out, g2, w_gate, b_gate, w_proj)


def _attn_bwd_in_proj_bwd_early(q, k, v, za, o, lse, d_ao, sinks, hn, du_parts, dzs, runs_after, n_seq, seq):
    nb = seq // BLOCK
    per_step = _ATTN_BWD_BLOCKS
    steps = nb // per_step
    tm = per_step * BLOCK
    rows = q.shape[0]
    slab, scan_steps, _, _ = _scan_geometry(n_seq, seq)
    q_at, k_at, v_at, za_at = 2 * D_SSM, 2 * D_SSM + 512, 2 * D_SSM + 640, 2 * D_SSM + 768

    def body(q_ref, kc_ref, kp_ref, vc_ref, vp_ref, za_ref, o_ref, lse_ref, d_ref, sk_ref, hn_ref, *refs):
        du_parts, others = refs[:_SCAN_PARTS], refs[_SCAN_PARTS:]
        dzs_ref, _, dproj_ref, dw_ref, dwb_ref, dsk_ref, bias_ref, dk_carry, dv_carry = others
        step = pl.program_id(1)
        first_block = nb - per_step * (step + 1)
        first_step = jnp.logical_and(pl.program_id(0) == 0, step == 0)
        _fill_attn_bias(bias_ref)

        @pl.when(first_step)
        def _():
            dsk_ref[...] = jnp.zeros_like(dsk_ref)
            dk_carry[...] = jnp.zeros_like(dk_carry)
            dv_carry[...] = jnp.zeros_like(dv_carry)
            dw_ref[...] = jnp.zeros_like(dw_ref)

        _, gate_vjp = jax.vjp(lambda o_, z_: o_ * _silu(z_), o_ref[...], za_ref[...])
        d_o, d_za = gate_vjp(d_ref[...])
        dproj_ref[:, za_at:] = d_za.astype(BF16)

        for j in range(KV_HEADS):
            js = slice(j * HEAD_DIM, (j + 1) * HEAD_DIM)
            bias_c, bias_p = bias_ref[j, 0], bias_ref[j, 1]
            sink = _sink_row(sk_ref, j)
            sink_loss = jnp.zeros((1, _GROUP_ROWS), F32)
            dk_from_next = jnp.where(step > 0, dk_carry[j], 0.0)
            dv_from_next = jnp.where(step > 0, dv_carry[j], 0.0)
            for t in reversed(range(per_step)):
                at = slice(t * BLOCK, (t + 1) * BLOCK)
                kc, vc = kc_ref[at, js], vc_ref[at, js]
                if t > 0:
                    before = slice((t - 1) * BLOCK, t * BLOCK)
                    kp, vp, bias_before = kc_ref[before, js], vc_ref[before, js], bias_p
                else:
                    kp, vp, bias_before = kp_ref[:, js], vp_ref[:, js], jnp.where(first_block > 0, bias_p, -jnp.inf)
                q4 = _stack_heads(q_ref[at, :], j)
                do4b = _stack_heads(d_o[at], j).astype(BF16)
                lse4 = _head_rows(lse_ref[t * N_HEADS:(t + 1) * N_HEADS, :], j)
                pc = jnp.exp(_dot_nt(kc, q4) + bias_c - lse4)
                pp = jnp.exp(_dot_nt(kp, q4) + bias_before - lse4)
                dpc = _dot_nt(vc, do4b)
                dpp = _dot_nt(vp, do4b)
                delta = jnp.sum(pc * dpc + pp * dpp, axis=0, keepdims=True)
                dsc = (pc * (dpc - delta)).astype(BF16)
                dsp = (pp * (dpp - delta)).astype(BF16)
                dq4 = ((_dot_tn(dsc, kc) + _dot_tn(dsp, kp)) * ATTN_SCALE).astype(BF16)
                sink_loss = sink_loss + jnp.exp(sink - lse4) * delta
                for g in range(Q_PER_KV):
                    h = j * Q_PER_KV + g
                    dproj_ref[at, q_at + h * HEAD_DIM:q_at + (h + 1) * HEAD_DIM] = dq4[g * BLOCK:(g + 1) * BLOCK]
                dproj_ref[at, k_at + j * HEAD_DIM:k_at + (j + 1) * HEAD_DIM] = (_dot(dsc, q4) + dk_from_next).astype(BF16)
                dproj_ref[at, v_at + j * HEAD_DIM:v_at + (j + 1) * HEAD_DIM] = (
                    _dot(pc.astype(BF16), do4b) + dv_from_next).astype(BF16)
                dk_from_next = _dot(dsp, q4)
                dv_from_next = _dot(pp.astype(BF16), do4b)
            dk_carry[j] = dk_from_next
            dv_carry[j] = dv_from_next
            for g in range(Q_PER_KV):
                h = j * Q_PER_KV + g
                dsk_ref[0:1, h:h + 1] -= jnp.sum(sink_loss[:, g * BLOCK:(g + 1) * BLOCK], axis=1, keepdims=True)

        tile = pl.program_id(0) * steps + steps - 1 - step
        du = _load_chunks(du_parts, tile * (tm // scan_steps), tm // scan_steps, scan_steps, slab)
        dproj_ref[:, 0:D_SSM] = du.astype(BF16)
        dproj_ref[:, D_SSM:2 * D_SSM] = dzs_ref[...]
        dw_ref[...] += _dot_tn(dproj_ref[...], hn_ref[...])

        @pl.when(jnp.logical_and(pl.program_id(0) == n_seq - 1, step == steps - 1))
        def _():
            dwb_ref[...] = dw_ref[...].astype(BF16)

    cur = lambda w: pl.BlockSpec((tm, w), lambda b, s: (b * steps + steps - 1 - s, 0))
    prev = lambda w: pl.BlockSpec((BLOCK, w), lambda b, s: (b * nb + jnp.maximum(nb - per_step * (s + 1) - 1, 0), 0))
    whole = lambda shape: pl.BlockSpec(shape, lambda b, s: (0, 0), pipeline_mode=pl.Buffered(1))
    return _call(
        body, name="attn_bwd_in_proj_bwd_early", grid=(n_seq, steps),
        in_specs=[cur(512), cur(128), prev(128), cur(128), prev(128), cur(512), cur(512),
                  pl.BlockSpec((per_step * N_HEADS, BLOCK), lambda b, s: (b * steps + steps - 1 - s, 0)), cur(512),
                  whole((1, N_HEADS)), cur(_EARLY_COLS)] + [whole((rows, LANES))] * _SCAN_PARTS
        + [cur(512), pl.BlockSpec(memory_space=pl.ANY)],
        out_specs=[cur(D_IN), whole((D_IN, _EARLY_COLS)), whole((D_IN, _EARLY_COLS)), whole((1, N_HEADS))],
        out_shape=[_sds((rows, D_IN), BF16), _sds((D_IN, _EARLY_COLS)), _sds((D_IN, _EARLY_COLS), BF16),
                   _sds((1, N_HEADS))],
        scratch_shapes=[_ATTN_BIAS_SCRATCH, pltpu.VMEM((KV_HEADS, BLOCK, HEAD_DIM), F32),
                        pltpu.VMEM((KV_HEADS, BLOCK, HEAD_DIM), F32)],
        compiler_params=_params(56, ("arbitrary", "arbitrary")),
    )(q, k, k, v, v, za, o, lse, d_ao, sinks, hn, *du_parts, dzs, runs_after)


def _local_step(x, hn, p, target, pre_norm_g, w_in_t, s5_params, s5_operands, ssm_d, w_glu, b_glu, sinks, w_out,
                post_norm_g, w_proj, w_gate, b_gate, send_tail_grads=lambda ready: ready["w_out"],
                send_bc=lambda d_bc: (d_bc, d_bc)):
    n_seq, seq, _ = x.shape
    rows = n_seq * seq
    x2d = x.reshape(rows, D_MODEL)
    p2d = p.reshape(rows, D_PLE)
    t2d = target.reshape(rows, D_MODEL)

    l_re, l_im, bt_re, bt_im, cm_re, cm_im = s5_operands

    u_scan, zs, q, k, v, za, o, attn_out, lse = _in_proj_attn_fwd(hn, w_in_t, sinks, n_seq, seq)
    y_scan, h_re, h_im = _s5_scan_fwd(u_scan, bt_re, bt_im, cm_re, cm_im, l_re, l_im, ssm_d, n_seq, seq)
    ssm_out = _glu_fwd(y_scan, zs, w_glu, b_glu, n_seq, seq)

    dh1, d_so, d_ao, d_w_out, d_w_gate, d_w_proj, d_b_gate, d_g2, loss = _tail(
        ssm_out, attn_out, x2d, p2d, t2d, w_out, post_norm_g, w_gate, b_gate, w_proj)

    dy_scan, dzs, d_w_glu, d_b_glu = _glu_bwd(y_scan, zs, d_so, w_glu, b_glu, n_seq, seq)
    du_scan, d_bt_re, d_bt_im, d_cm_re, d_cm_im, d_l_re, d_l_im, d_d = _s5_scan_bwd(
        dy_scan, u_scan, h_re, h_im, bt_re, bt_im, cm_re, cm_im, l_re, l_im, ssm_d, n_seq, seq)
    tail_grads_arrived = send_tail_grads(dict(w_out=d_w_out, pl_w_gate=d_w_gate, pl_w_proj=d_w_proj))
    d_lam_re, d_lam_im, d_log_step, d_bc = _s5_params_bwd(
        s5_params, (d_l_re, d_l_im, d_bt_re, d_bt_im, d_cm_re, d_cm_im), tail_grads_arrived)

    sent, arrived = send_bc(d_bc)
    d_proj, d_w_in_early, d_w_in_early_b, d_sinks = _attn_bwd_in_proj_bwd_early(
        q, k, v, za, o, lse, d_ao, sinks, hn, du_scan, dzs, sent, n_seq, seq)
    grad_x, d_w_in_late, d_g1 = _in_proj_bwd(x2d, dh1, pre_norm_g, w_in_t, d_proj, arrived)
    grads = dict(
        pre_norm_g=d_g1, w_in_early=d_w_in_early, w_in_early_bf16=d_w_in_early_b, w_in_late=d_w_in_late,
        ssm_lam_re=d_lam_re, ssm_lam_im=d_lam_im, ssm_log_step=d_log_step, ssm_bc=d_bc, ssm_d=d_d, ssm_w_glu=d_w_glu,
        ssm_b_glu=d_b_glu, attn_sinks=d_sinks, w_out=d_w_out, post_norm_g=d_g2, pl_w_proj=d_w_proj,
        pl_w_gate=d_w_gate, pl_b_gate=d_b_gate)
    return grad_x.reshape(x.shape), loss, grads


_BIG = ("w_in", "ssm_w_glu", "w_out", "pl_w_proj", "pl_w_gate")
_BIG_SHARD = {"w_in": (D_IN // N_CHIPS, D_MODEL), "ssm_w_glu": (D_SSM // N_CHIPS, D_SSM),
              "w_out": (D_MODEL // N_CHIPS, D_MODEL), "pl_w_proj": (D_PLE, D_MODEL // N_CHIPS),
              "pl_w_gate": (D_MODEL // N_CHIPS, D_MODEL)}
_SMALL = {"pre_norm_g": (1, D_MODEL), "ssm_lam_re": (SSM_GROUPS, SSM_STATE), "ssm_lam_im": (SSM_GROUPS, SSM_STATE),
          "ssm_log_step": (1, SSM_GROUPS), "ssm_b_re": (D_SSM, SSM_STATE), "ssm_b_im": (D_SSM, SSM_STATE),
          "ssm_c_re": (D_SSM, SSM_STATE), "ssm_c_im": (D_SSM, SSM_STATE), "ssm_d": (1, D_SSM), "ssm_b_glu": (1, D_SSM),
          "attn_sinks": (1, N_HEADS), "post_norm_g": (1, D_MODEL), "pl_b_gate": (1, D_MODEL)}
_VEC_ROWS = ("pre_norm_g", "post_norm_g", "pl_b_gate", "ssm_d", "ssm_b_glu", "attn_sinks", "ssm_log_step", "loss")
_SMALL_GROUPS = (
    ("vec", (8, D_MODEL), tuple((name, r) for r, name in enumerate(_VEC_ROWS))),
    ("lam", (2 * SSM_GROUPS, SSM_STATE), (("ssm_lam_re", 0), ("ssm_lam_im", SSM_GROUPS))),
)
_SMALL_EARLY = ("ssm_b_re", "ssm_b_im", "ssm_c_re", "ssm_c_im")
_SMALL_ORDER = tuple(name for _, _, members in _SMALL_GROUPS for name, _ in members) + _SMALL_EARLY
_WEIGHT_ORDER = ("pre_norm_g", "w_in", "ssm_lam_re", "ssm_lam_im", "ssm_log_step", "ssm_b_re", "ssm_b_im", "ssm_c_re",
                 "ssm_c_im", "ssm_d", "ssm_w_glu", "ssm_b_glu", "attn_sinks", "w_out", "post_norm_g", "pl_w_proj",
                 "pl_w_gate", "pl_b_gate")


def _small_shape(name):
    return (1, 1) if name == "loss" else _SMALL[name]


def _to_kernel_form(name, a):
    a = a[0]
    if name == "w_in":
        return a.T
    if name in ("ssm_b_re", "ssm_b_im"):
        a = a.transpose(0, 2, 1)
    return a.reshape(_SMALL[name]) if name in _SMALL else a


def _from_kernel_form(name, a, shape):
    if name == "w_in":
        a = a.T
    if name in ("ssm_b_re", "ssm_b_im"):
        a = a.reshape(SSM_GROUPS, SSM_GROUP_CH, SSM_STATE).transpose(0, 2, 1)
    return a.reshape(shape)


def _mesh_place():
    x, y, c = lax.axis_index("x"), lax.axis_index("y"), lax.axis_index("c")
    other_chips = ((1 - x, y), (x, 1 - y), (1 - x, 1 - y))
    return x, y, c, other_chips


def _gather_copies(s_refs, g_refs, send_sems, recv_sems, local_sems):
    x, y, c, other_chips = _mesh_place()
    started = []
    for i, (s_ref, g_ref) in enumerate(zip(s_refs, g_refs)):
        rows = s_ref.shape[0]
        half = rows // 2

        def block(chip, g_ref=g_ref, rows=rows, half=half):
            return g_ref.at[pl.ds((2 * chip[0] + chip[1]) * rows + c * half, half), :]

        def copy(k, chip, to, src=None, i=i, block=block):
            return pltpu.make_async_remote_copy(
                src_ref=block(chip) if src is None else src, dst_ref=block(chip), send_sem=send_sems.at[6 * i + k],
                recv_sem=recv_sems.at[6 * i + k], device_id=to, device_id_type=MESH)

        own = pltpu.make_async_copy(s_ref, g_ref.at[pl.ds((2 * x + y) * rows, rows), :], local_sems.at[i])
        own.start()
        first = [copy(k, (x, y), (*chip, c), src=s_ref.at[pl.ds(c * half, half), :])
                 for k, chip in enumerate(other_chips)]
        for cp in first:
            cp.start()
        passed = [copy(3 + k, chip, (x, y, 1 - c)) for k, chip in enumerate(other_chips)]
        started.append((own, first, passed))
    for own, first, passed in started:
        for k in range(3):
            first[k].wait_recv()
            passed[k].start()
    for own, first, passed in started:
        for k in range(3):
            passed[k].wait_recv()
        for cp in first + passed:
            cp.wait_send()
        own.wait()


def _gather_semaphores(n_t):
    return [pltpu.SemaphoreType.DMA((6 * n_t,)), pltpu.SemaphoreType.DMA((6 * n_t,)), pltpu.SemaphoreType.DMA((n_t,))]


def _gather_weights_beside(shards, name, collective_id):
    n_t = len(shards)
    hbm = pltpu.MemorySpace.HBM
    s_refs = [jax.new_ref(s, memory_space=hbm) for s in shards]
    g_refs = [jax.empty_ref(jax.ShapeDtypeStruct((N_CHIPS * s.shape[0], s.shape[1]), s.dtype), memory_space=hbm)
              for s in shards]

    def launch(send_sems, recv_sems, local_sems):
        x, y, c, other_chips = _mesh_place()
        peers = [(*chip, c) for chip in other_chips] + [(x, y, 1 - c)]
        barrier = pltpu.get_barrier_semaphore()
        for peer in peers:
            pl.semaphore_signal(barrier, inc=1, device_id=peer, device_id_type=MESH)
        pl.semaphore_wait(barrier, len(peers))
        _gather_copies(s_refs, g_refs, send_sems, recv_sems, local_sems)

    pl.kernel(launch, mesh=plsc.ScalarSubcoreMesh(axis_name="sequencer", num_cores=1), name=name,
              scratch_types=_gather_semaphores(n_t), compiler_params=pltpu.CompilerParams(collective_id=collective_id))()
    return [g[...] for g in g_refs]


_RELATIONS = tuple(((r >> 2) & 1, (r >> 1) & 1, r & 1) for r in range(1, 8))


def _related(place, relation):
    return tuple(1 - a if flip else a for a, flip in zip(place, relation))


def _scatter_beside(mats, name, collective_id):
    hbm = pltpu.MemorySpace.HBM
    src_refs = [jax.new_ref(a, memory_space=hbm) for a in mats]
    land_refs = [jax.empty_ref(jax.ShapeDtypeStruct((7, a.shape[0] // 8, a.shape[1]), a.dtype), memory_space=hbm)
                 for a in mats]

    def launch(send_sems, recv_sems):
        me = (lax.axis_index("x"), lax.axis_index("y"), lax.axis_index("c"))
        peers = [_related(me, rel) for rel in _RELATIONS]
        barrier = pltpu.get_barrier_semaphore()
        for peer in peers:
            pl.semaphore_signal(barrier, inc=1, device_id=peer, device_id_type=MESH)
        pl.semaphore_wait(barrier, len(peers))
        copies = []
        for i, (src, land) in enumerate(zip(src_refs, land_refs)):
            hr = land.shape[1]
            for k, (tx, ty, tc) in enumerate(peers):
                rows = pl.ds((2 * tx + ty) * 2 * hr + tc * hr, hr)
                copies.append(pltpu.make_async_remote_copy(
                    src_ref=src.at[rows, :], dst_ref=land.at[k], send_sem=send_sems.at[7 * i + k],
                    recv_sem=recv_sems.at[7 * i + k], device_id=(tx, ty, tc), device_id_type=MESH))
                copies[-1].start()
        for cp in copies:
            cp.wait()

    n_sems = 7 * len(mats)
    pl.kernel(launch, mesh=plsc.ScalarSubcoreMesh(axis_name="sequencer", num_cores=1), name=name,
              scratch_types=[pltpu.SemaphoreType.DMA((n_sems,)), pltpu.SemaphoreType.DMA((n_sems,))],
              compiler_params=pltpu.CompilerParams(collective_id=collective_id))()
    return [ref[...] for ref in land_refs]


def _broadcast_beside(arrays):
    hbm = pltpu.MemorySpace.HBM
    src_refs = [jax.new_ref(a, memory_space=hbm) for a in arrays]
    land_refs = [jax.empty_ref(jax.ShapeDtypeStruct((len(_RELATIONS),) + a.shape, a.dtype), memory_space=hbm)
                 for a in arrays]

    def launch(send_sems, recv_sems):
        me = (lax.axis_index("x"), lax.axis_index("y"), lax.axis_index("c"))
        peers = [_related(me, rel) for rel in _RELATIONS]
        barrier = pltpu.get_barrier_semaphore()
        for peer in peers:
            pl.semaphore_signal(barrier, inc=1, device_id=peer, device_id_type=MESH)
        pl.semaphore_wait(barrier, len(peers))
        copies = []
        for i, (src, land) in enumerate(zip(src_refs, land_refs)):
            for k, peer in enumerate(peers):
                copies.append(pltpu.make_async_remote_copy(
                    src_ref=src, dst_ref=land.at[k], send_sem=send_sems.at[7 * i + k],
                    recv_sem=recv_sems.at[7 * i + k], device_id=peer, device_id_type=MESH))
                copies[-1].start()
        for cp in copies:
            cp.wait()

    n_sems = 7 * len(arrays)
    pl.kernel(launch, mesh=plsc.ScalarSubcoreMesh(axis_name="sequencer", num_cores=1), name="broadcast_beside",
              scratch_types=[pltpu.SemaphoreType.DMA((n_sems,)), pltpu.SemaphoreType.DMA((n_sems,))],
              compiler_params=pltpu.CompilerParams(collective_id=3))()
    return [ref[...] for ref in land_refs]


def _exchange_grads(big, outputs, small, landed, own_bc, landed_bc):
    n_t = len(big)
    n_g = len(_SMALL_GROUPS)
    names = _SMALL_ORDER
    halves = [(b.shape[0] // N_CHIPS // 2, b.shape[1]) for b in big]
    early = sorted(landed)
    late = [i for i in range(n_t) if i not in landed]
    n_sems = 4 * n_g + 7 * len(late) + n_t
    small_sem0, block_sem0 = n_t, n_t + len(names)
    early_sem0 = block_sem0 + N_CHIPS * len(late)
    landed_sem0 = early_sem0 + 2 * len(early)
    sent = [n for n in names if n in small]

    def body(*refs):
        pos = 0

        def take(n):
            nonlocal pos
            pos += n
            return refs[pos - n:pos]

        big_refs, small_refs = take(n_t), dict(zip(sent, take(len(sent))))
        land_refs = dict(zip(early, take(len(early))))
        own_bc_ref, landed_bc_ref = take(2)
        out_refs, small_out_refs = take(len(outputs)), dict(zip(names, take(len(names))))
        per_late = lambda: dict(zip(late, take(len(late))))
        ga, gb, pme, send_b, recv_b = per_late(), per_late(), take(n_t), per_late(), per_late()
        own_e, land_e = dict(zip(early, take(len(early)))), dict(zip(early, take(len(early))))
        own_s, land_s = take(2)
        s_own, s_sib, s_chips, s_pair = take(n_g), take(n_g), take(n_g), take(n_g)
        stage = dict(zip(names, take(len(names))))
        send_sems, recv_sems, local_sems = take(3)
        x, y, c, other_chips = _mesh_place()
        me = 2 * x + y
        sibling = (x, y, 1 - c)
        sem_at = iter(range(n_sems))

        def remote(src, dst, to):
            k = next(sem_at)
            return pltpu.make_async_remote_copy(src_ref=src, dst_ref=dst, send_sem=send_sems.at[k],
                                                recv_sem=recv_sems.at[k], device_id=to, device_id_type=MESH)

        loads = [pltpu.make_async_copy(small_refs[name], stage[name], local_sems.at[small_sem0 + names.index(name)])
                 for name in sent]
        landed_loads = [pltpu.make_async_copy(own_bc_ref, own_s, local_sems.at[landed_sem0]),
                        pltpu.make_async_copy(landed_bc_ref, land_s, local_sems.at[landed_sem0 + 1])]
        for cp in loads + landed_loads:
            cp.start()
        for cp in loads:
            cp.wait()
        small_swaps = []
        for gi, (_, _, members) in enumerate(_SMALL_GROUPS):
            s_own[gi][...] = jnp.zeros_like(s_own[gi])
            for name, r0 in members:
                r, n = _small_shape(name)
                s_own[gi][r0:r0 + r, 0:n] = stage[name][...]
            small_swaps.append(remote(s_own[gi], s_sib[gi], sibling))
            small_swaps[gi].start()
        order = sorted(late, key=lambda i: halves[i][0] * halves[i][1])
        own_loads, big_swaps = {}, {}
        for i in order:
            hr = halves[i][0]
            own_loads[i], big_swaps[i] = [], []
            for j in range(N_CHIPS):
                mine = big_refs[i].at[pl.ds(j * 2 * hr + c * hr, hr), :]
                theirs = big_refs[i].at[pl.ds(j * 2 * hr + (1 - c) * hr, hr), :]
                sem = local_sems.at[block_sem0 + N_CHIPS * late.index(i) + j]
                own_loads[i].append(pltpu.make_async_copy(mine, ga[i].at[j], sem))
                own_loads[i][j].start()
                big_swaps[i].append(remote(theirs, gb[i].at[j], sibling))
                big_swaps[i][j].start()
        early_loads = {}
        for e, i in enumerate(early):
            hr = halves[i][0]
            mine = big_refs[i].at[pl.ds(me * 2 * hr + c * hr, hr), :]
            early_loads[i] = [pltpu.make_async_copy(mine, own_e[i], local_sems.at[early_sem0 + 2 * e]),
                              pltpu.make_async_copy(land_refs[i], land_e[i], local_sems.at[early_sem0 + 2 * e + 1])]
            for cp in early_loads[i]:
                cp.start()
        small_sends = []
        for gi in range(n_g):
            small_swaps[gi].wait_recv()
            s_pair[gi][...] = s_own[gi][...] + s_sib[gi][...]
            small_sends.append([remote(s_pair[gi], s_chips[gi].at[k], (*chip, c)) for k, chip in enumerate(other_chips)])
            for cp in small_sends[gi]:
                cp.start()

        def pair_sum(i, j):
            return ga[i][j] + gb[i][j]

        big_sends = {}
        for i in order:
            for j in range(N_CHIPS):
                own_loads[i][j].wait()
                big_swaps[i][j].wait_recv()
            big_sends[i] = []
            for k, chip in enumerate(other_chips):
                send_b[i][k] = pair_sum(i, 2 * chip[0] + chip[1]).astype(BF16)
                big_sends[i].append(remote(send_b[i].at[k], recv_b[i].at[k], (*chip, c)))
                big_sends[i][k].start()
        last_swaps, keeps = {}, {}
        for i in early + order:
            hr = halves[i][0]
            if i in landed:
                for cp in early_loads[i]:
                    cp.wait()
                total = own_e[i][...]
                for k in range(len(_RELATIONS)):
                    total = total + land_e[i][k].astype(F32)
                pme[i][...] = total
            else:
                for k in range(3):
                    big_sends[i][k].wait_recv()
                pme[i][...] = ((pair_sum(i, me) + recv_b[i][0].astype(F32)) + recv_b[i][1].astype(F32)) + recv_b[i][2].astype(F32)
            o, = [o for o, group in enumerate(outputs) if i in group]
            first_col = sum(halves[j][1] for j in outputs[o][:outputs[o].index(i)])
            mine = out_refs[o].at[pl.ds(c * hr, hr), pl.ds(first_col, halves[i][1])]
            keeps[i] = pltpu.make_async_copy(pme[i], mine, local_sems.at[i])
            keeps[i].start()
            last_swaps[i] = remote(pme[i], mine, sibling)
            last_swaps[i].start()

        for gi, (_, _, members) in enumerate(_SMALL_GROUPS):
            for k in range(3):
                small_sends[gi][k].wait_recv()
            total = None
            for j in range(N_CHIPS):
                rel = jnp.bitwise_xor(j, me)
                term = jnp.where(rel == 0, s_pair[gi][...], jnp.where(
                    rel == 2, s_chips[gi][0], jnp.where(rel == 1, s_chips[gi][1], s_chips[gi][2])))
                total = term if total is None else total + term
            s_sib[gi][...] = total
            for name, r0 in members:
                r, n = _small_shape(name)
                stage[name][...] = s_sib[gi][r0:r0 + r, 0:n]
        my_index = 4 * x + 2 * y + c
        for cp in landed_loads:
            cp.wait()
        total = None
        for d in range(2 * N_CHIPS):
            rel = jnp.bitwise_xor(d, my_index)
            term = own_s[...]
            for k in range(len(_RELATIONS)):
                term = jnp.where(rel == k + 1, land_s[k], term)
            total = term.astype(F32) if total is None else total + term.astype(F32)
        for a, name in enumerate(_SMALL_EARLY):
            stage[name][...] = total[:, a * SSM_STATE:(a + 1) * SSM_STATE]
        stores = [pltpu.make_async_copy(stage[name], small_out_refs[name], local_sems.at[small_sem0 + a])
                  for a, name in enumerate(names)]
        for cp in stores:
            cp.start()

        for i in range(n_t):
            last_swaps[i].wait_recv()
            keeps[i].wait()
        for cp in stores:
            cp.wait()
        groups = list(big_swaps.values()) + small_sends + list(big_sends.values())
        for cp in small_swaps + [cp for group in groups for cp in group] + list(last_swaps.values()):
            cp.wait_send()

    any_spec = pl.BlockSpec(memory_space=pl.ANY)
    small_shapes = [_sds(_small_shape(n)) for n in names]
    group_shapes = [shape for _, shape, _ in _SMALL_GROUPS]
    vmem = lambda which, dtype, lead=(): [pltpu.VMEM(lead + halves[i], dtype) for i in which]
    outs = _call(
        body, name="exchange_grads",
        in_specs=[any_spec] * (n_t + len(sent) + len(early) + 2),
        out_specs=[any_spec] * (len(outputs) + len(names)),
        out_shape=[_sds((big[group[0]].shape[0] // N_CHIPS, sum(big[i].shape[1] for i in group))) for group in outputs]
        + small_shapes,
        scratch_shapes=(vmem(late, F32, (N_CHIPS,)) + vmem(late, F32, (N_CHIPS,)) + vmem(range(n_t), F32)
                        + vmem(late, BF16, (3,)) + vmem(late, BF16, (3,))
                        + vmem(early, F32)
                        + [pltpu.VMEM((len(_RELATIONS),) + halves[i], landed[i].dtype) for i in early]
                        + [pltpu.VMEM(own_bc.shape, own_bc.dtype), pltpu.VMEM(landed_bc.shape, landed_bc.dtype)]
                        + [pltpu.VMEM(s, F32) for s in group_shapes] * 2 + [pltpu.VMEM((3,) + s, F32) for s in group_shapes]
                        + [pltpu.VMEM(s, F32) for s in group_shapes]
                        + [pltpu.VMEM(_small_shape(n), F32) for n in names]
                        + [pltpu.SemaphoreType.DMA((n_sems,)), pltpu.SemaphoreType.DMA((n_sems,)),
                           pltpu.SemaphoreType.DMA((landed_sem0 + 2,))]),
        compiler_params=_params(48),
    )(*big, *[small[n] for n in sent], *[landed[i] for i in early], own_bc, landed_bc)
    return list(outs[:len(outputs)]), dict(zip(names, outs[len(outputs):]))


def _adamw_update(w, g, m, v):
    m = ADAM_B1 * m + (1.0 - ADAM_B1) * g
    v = ADAM_B2 * v + (1.0 - ADAM_B2) * (g * g)
    m_hat = m / (1.0 - ADAM_B1 ** ADAM_STEP)
    v_hat = v / (1.0 - ADAM_B2 ** ADAM_STEP)
    return -ADAM_LR * (m_hat / (jnp.sqrt(v_hat) + ADAM_EPS) + ADAM_WD * w), m, v


def _adamw(w, g, m, v, grid, name):
    n_t = len(w)

    def body(*refs):
        ins, outs = refs[:4 * n_t], refs[4 * n_t:]
        for i in range(n_t):
            w_, g_, m_, v_ = [ins[a * n_t + i][...] for a in range(4)]
            vals = (g_,) + _adamw_update(w_, g_, m_, v_)
            for a in range(4):
                outs[a * n_t + i][...] = vals[a]

    specs = [pl.BlockSpec((a.shape[0] // grid, a.shape[1]), lambda i: (i, 0)) for a in w]
    shapes = [_sds(a.shape) for a in w]
    outs = _call(
        body, name=name, grid=(grid,), in_specs=specs * 4, out_specs=specs * 4, out_shape=shapes * 4,
        compiler_params=_params(40, ("arbitrary",)),
    )(*w, *g, *m, *v)
    return [outs[a * n_t:(a + 1) * n_t] for a in range(4)]


def kernel(x, p, pre_norm_g, w_in, ssm_lam_re, ssm_lam_im, ssm_log_step, ssm_b_re, ssm_b_im, ssm_c_re, ssm_c_im, ssm_d, ssm_w_glu, ssm_b_glu, attn_sinks, w_out, post_norm_g, pl_w_proj, pl_w_gate, pl_b_gate, loss_target, m_pre_norm_g, m_w_in, m_ssm_lam_re, m_ssm_lam_im, m_ssm_log_step, m_ssm_b_re, m_ssm_b_im, m_ssm_c_re, m_ssm_c_im, m_ssm_d, m_ssm_w_glu, m_ssm_b_glu, m_attn_sinks, m_w_out, m_post_norm_g, m_pl_w_proj, m_pl_w_gate, m_pl_b_gate, v_pre_norm_g, v_w_in, v_ssm_lam_re, v_ssm_lam_im, v_ssm_log_step, v_ssm_b_re, v_ssm_b_im, v_ssm_c_re, v_ssm_c_im, v_ssm_d, v_ssm_w_glu, v_ssm_b_glu, v_attn_sinks, v_w_out, v_post_norm_g, v_pl_w_proj, v_pl_w_gate, v_pl_b_gate):
    weights = dict(pre_norm_g=pre_norm_g, w_in=w_in, ssm_lam_re=ssm_lam_re, ssm_lam_im=ssm_lam_im,
                   ssm_log_step=ssm_log_step, ssm_b_re=ssm_b_re, ssm_b_im=ssm_b_im, ssm_c_re=ssm_c_re,
                   ssm_c_im=ssm_c_im, ssm_d=ssm_d, ssm_w_glu=ssm_w_glu, ssm_b_glu=ssm_b_glu, attn_sinks=attn_sinks,
                   w_out=w_out, post_norm_g=post_norm_g, pl_w_proj=pl_w_proj, pl_w_gate=pl_w_gate, pl_b_gate=pl_b_gate)
    m_in = dict(pre_norm_g=m_pre_norm_g, w_in=m_w_in, ssm_lam_re=m_ssm_lam_re, ssm_lam_im=m_ssm_lam_im,
                ssm_log_step=m_ssm_log_step, ssm_b_re=m_ssm_b_re, ssm_b_im=m_ssm_b_im, ssm_c_re=m_ssm_c_re,
                ssm_c_im=m_ssm_c_im, ssm_d=m_ssm_d, ssm_w_glu=m_ssm_w_glu, ssm_b_glu=m_ssm_b_glu,
                attn_sinks=m_attn_sinks, w_out=m_w_out, post_norm_g=m_post_norm_g, pl_w_proj=m_pl_w_proj,
                pl_w_gate=m_pl_w_gate, pl_b_gate=m_pl_b_gate)
    v_in = dict(pre_norm_g=v_pre_norm_g, w_in=v_w_in, ssm_lam_re=v_ssm_lam_re, ssm_lam_im=v_ssm_lam_im,
                ssm_log_step=v_ssm_log_step, ssm_b_re=v_ssm_b_re, ssm_b_im=v_ssm_b_im, ssm_c_re=v_ssm_c_re,
                ssm_c_im=v_ssm_c_im, ssm_d=v_ssm_d, ssm_w_glu=v_ssm_w_glu, ssm_b_glu=v_ssm_b_glu,
                attn_sinks=v_attn_sinks, w_out=v_w_out, post_norm_g=v_post_norm_g, pl_w_proj=v_pl_w_proj,
                pl_w_gate=v_pl_w_gate, pl_b_gate=v_pl_b_gate)

    def two_d(tree):
        return {k: _to_kernel_form(k, a) for k, a in tree.items()}

    w2, m2, v2 = two_d(weights), two_d(m_in), two_d(v_in)

    (w_in_full,) = _gather_weights_beside([w2["w_in"].astype(BF16)], "gather_w_in_beside", 4)
    s5_params = tuple(w2[n] for n in ("ssm_lam_re", "ssm_lam_im", "ssm_log_step", "ssm_b_re", "ssm_b_im", "ssm_c_re",
                                      "ssm_c_im"))
    s5_operands = _s5_params_fwd(*s5_params)
    hn = _pre_norm(x.reshape(-1, D_MODEL), w2["pre_norm_g"])
    behind = s5_operands[0][0, 0] * 0.0 + hn[0, 0].astype(F32) * 0.0
    rest = _gather_weights_beside([(w2[n] + behind).astype(BF16) for n in _BIG[1:]], "gather_weights_beside", 1)
    full = dict(zip(_BIG, [w_in_full] + rest))
    mats = ("w_in_early", "w_in_late") + _BIG[1:]
    landed, bc = {}, {}

    def send_tail_grads(ready):
        sent_early = ("w_out", "pl_w_gate", "pl_w_proj")
        landed.update(zip([mats.index(n) for n in sent_early],
                          _scatter_beside([ready[n] for n in sent_early], "scatter_beside", 2)))
        return landed[mats.index(sent_early[-1])]

    def send_bc(d_bc):
        bc["own"] = d_bc
        bc["landed"] = _broadcast_beside([d_bc])[0]
        return d_bc, bc["landed"]

    grad_x, loss, grads = _local_step(
        x, hn, p, loss_target, w2["pre_norm_g"], full["w_in"], s5_params, s5_operands, w2["ssm_d"], full["ssm_w_glu"], w2["ssm_b_glu"],
        w2["attn_sinks"], full["w_out"], w2["post_norm_g"], full["pl_w_proj"], full["pl_w_gate"], w2["pl_b_gate"], send_tail_grads, send_bc)

    landed[0], landed[mats.index("ssm_w_glu")] = _scatter_beside(
        [grads["w_in_early_bf16"], grads["ssm_w_glu"]], "scatter_w_in_beside", 5)
    sent_here = {**{n: grads[n] for n in _SMALL if n not in _SMALL_EARLY}, "loss": loss}
    halves_of_w_in = ((0, 1),) + tuple((i,) for i in range(2, len(mats)))
    g_big, g_small = _exchange_grads([grads[n] for n in mats], halves_of_w_in, sent_here, landed, bc["own"], bc["landed"])
    g_big = dict(zip(_BIG, g_big))
    total_loss = g_small.pop("loss")

    big_out = _adamw([w2[n] for n in _BIG], [g_big[n] for n in _BIG], [m2[n] for n in _BIG], [v2[n] for n in _BIG],
                     8, "adamw_matrices")
    small_names = tuple(_SMALL)
    small_out = _adamw([w2[n] for n in small_names], [g_small[n] for n in small_names], [m2[n] for n in small_names],
                       [v2[n] for n in small_names], 1, "adamw_small")

    results = [{**dict(zip(_BIG, big_part)), **dict(zip(small_names, small_part))}
               for big_part, small_part in zip(big_out, small_out)]
    flat = [_from_kernel_form(name, r[name], weights[name].shape) for r in results for name in _WEIGHT_ORDER]
    return (total_loss.reshape(()), grad_x, *flat)
```

```python
import math

import jax
import jax.numpy as jnp
from jax import lax
from jax.experimental import pallas as pl
from jax.experimental.pallas import tpu as pltpu
from jax.experimental.pallas import tpu_sc as plsc

F32 = jnp.float32
BF16 = jnp.bfloat16

D_MODEL = 1024
D_SSM = 512
D_ATTN = 512
SSM_GROUPS = 32
SSM_GROUP_CH = 16
SSM_STATE = 64
SSM_LANES = SSM_GROUPS * SSM_STATE
HEAD_DIM = 64
N_HEADS = 8
KV_HEADS = 2
Q_PER_KV = 4
WINDOW = 128
BLOCK = 128
D_PLE = 256
D_IN = 2304
EPS = 1e-6
ATTN_SCALE = 1.0 / math.sqrt(HEAD_DIM)

ADAM_LR = 0.001
ADAM_B1 = 0.9
ADAM_B2 = 0.999
ADAM_EPS = 1e-08
ADAM_WD = 0.01
ADAM_STEP = 10

N_CHIPS = 4
LANES = 128
SCAN_CHUNKS = 8
SCAN_TILE_STEPS = 32
SCAN_LANE_CHUNK = 512
MIB = 2 ** 20
MESH = pl.DeviceIdType.MESH


def _dot(a, b):
    return jnp.dot(a, b, preferred_element_type=F32)


def _dot_nt(a, b):
    return lax.dot_general(a, b, (((1,), (1,)), ((), ())), preferred_element_type=F32)


def _dot_tn(a, b):
    return lax.dot_general(a, b, (((0,), (0,)), ((), ())), preferred_element_type=F32)


def _params(vmem_mib, semantics=None):
    kw = dict(vmem_limit_bytes=vmem_mib * MIB)
    if semantics is not None:
        kw["dimension_semantics"] = semantics
    return pltpu.CompilerParams(**kw)


def _full(shape):
    nd = len(shape)
    return pl.BlockSpec(shape, lambda *_: (0,) * nd, pipeline_mode=pl.Buffered(1))


def _rows(tm, width):
    return pl.BlockSpec((tm, width), lambda i: (i, 0))


def _sds(shape, dtype=F32):
    return pltpu.HBM(shape, dtype)


def _call(body, **kw):
    fn = pl.pallas_call(body, **kw)
    return lambda *args: fn(*[pltpu.with_memory_space_constraint(a, pltpu.HBM) for a in args])


def _silu(z):
    return z * jax.nn.sigmoid(z)


def _pre_norm(x2d, g1):
    rows = x2d.shape[0]
    tm = 512

    def body(x_ref, g_ref, hn_ref):
        x = x_ref[...]
        r = lax.rsqrt(jnp.mean(x * x, axis=-1, keepdims=True) + EPS)
        hn_ref[...] = (x * r * g_ref[...]).astype(BF16)

    return _call(
        body, name="pre_norm", grid=(rows // tm,), in_specs=[_rows(tm, D_MODEL), _full((1, D_MODEL))],
        out_specs=_rows(tm, D_MODEL), out_shape=_sds((rows, D_MODEL), BF16), compiler_params=_params(32, ("arbitrary",)),
    )(x2d, g1)


def _in_proj(hn, w_in_t, n_seq, seq):
    rows = hn.shape[0]
    tm = 1024
    slab, steps, _, _ = _scan_geometry(n_seq, seq)

    def body(hn_ref, w_ref, *out_refs):
        u_parts, (zs_ref, q_ref, k_ref, v_ref, za_ref) = out_refs[:_SCAN_PARTS], out_refs[_SCAN_PARTS:]
        whole = _dot_nt(hn_ref[...], w_ref[...])

        def proj(a, b):
            return whole[:, a:b]

        _store_chunks(u_parts, pl.program_id(0) * (tm // steps), proj(0, 512), steps, slab)
        zs_ref[...] = proj(512, 1024)
        q_ref[...] = (proj(1024, 1536) * ATTN_SCALE).astype(BF16)
        k_ref[...] = proj(1536, 1664).astype(BF16)
        v_ref[...] = proj(1664, 1792).astype(BF16)
        za_ref[...] = proj(1792, 2304)

    *u_parts, zs, q, k, v, za = _call(
        body, name="in_proj", grid=(rows // tm,),
        in_specs=[_rows(tm, D_MODEL), _full((D_IN, D_MODEL))],
        out_specs=_whole_parts(rows) + [_rows(tm, 512), _rows(tm, 512), _rows(tm, 128), _rows(tm, 128), _rows(tm, 512)],
        out_shape=_part_shapes(rows) + [_sds((rows, 512)), _sds((rows, 512), BF16), _sds((rows, 128), BF16),
                                        _sds((rows, 128), BF16), _sds((rows, 512))],
        compiler_params=_params(48, ("arbitrary",)),
    )(hn, w_in_t)
    return u_parts, zs, q, k, v, za


_EARLY_COLS = D_MODEL // 2


def _in_proj_bwd_early(hn, du_parts, dzs, dq, dk, dv, dza, runs_after, n_seq, seq):
    rows = hn.shape[0]
    tm = 1024
    slab, steps, _, _ = _scan_geometry(n_seq, seq)

    def body(hn_ref, *refs):
        du_parts, (dzs_ref, dq_ref, dk_ref, dv_ref, dza_ref, _, dproj_ref, dw_ref, dwb_ref) = refs[:_SCAN_PARTS], refs[_SCAN_PARTS:]
        i = pl.program_id(0)

        @pl.when(i == 0)
        def _():
            dw_ref[...] = jnp.zeros_like(dw_ref)

        du = _load_chunks(du_parts, i * (tm // steps), tm // steps, steps, slab)
        d_proj = jnp.concatenate([du.astype(BF16), dzs_ref[...], dq_ref[...], dk_ref[...], dv_ref[...], dza_ref[...]],
                                 axis=1)
        dproj_ref[...] = d_proj
        dw_ref[...] += _dot_tn(d_proj, hn_ref[...])

        @pl.when(i == rows // tm - 1)
        def _():
            dwb_ref[...] = dw_ref[...].astype(BF16)

    return _call(
        body, name="in_proj_bwd_early", grid=(rows // tm,),
        in_specs=[_rows(tm, _EARLY_COLS)] + _whole_parts(rows)
        + [_rows(tm, 512), _rows(tm, 512), _rows(tm, 128), _rows(tm, 128), _rows(tm, 512),
           pl.BlockSpec(memory_space=pl.ANY)],
        out_specs=[_rows(tm, D_IN), _full((D_IN, _EARLY_COLS)), _full((D_IN, _EARLY_COLS))],
        out_shape=[_sds((rows, D_IN), BF16), _sds((D_IN, _EARLY_COLS)), _sds((D_IN, _EARLY_COLS), BF16)],
        compiler_params=_params(48, ("arbitrary",)),
    )(hn, *du_parts, dzs, dq, dk, dv, dza, runs_after)


def _in_proj_bwd(x2d, dh1, g1, w_in_t, d_proj, runs_after):
    rows = x2d.shape[0]
    tm = 512

    def body(x_ref, dh1_ref, g_ref, w_ref, dproj_ref, _, gx_ref, dw_ref, dg_ref):
        @pl.when(pl.program_id(0) == 0)
        def _():
            dw_ref[...] = jnp.zeros_like(dw_ref)
            dg_ref[...] = jnp.zeros_like(dg_ref)

        x = x_ref[...]
        g = g_ref[...]
        r = lax.rsqrt(jnp.mean(x * x, axis=-1, keepdims=True) + EPS)
        xr = x * r
        hn = (xr[:, _EARLY_COLS:] * g[:, _EARLY_COLS:]).astype(BF16)
        d_proj = dproj_ref[...]
        dhn = _dot(d_proj, w_ref[...])
        dw_ref[...] += _dot_tn(d_proj, hn)
        dg_ref[...] += jnp.sum(dhn * xr, axis=0, keepdims=True)
        a_ = dhn * g
        gx_ref[...] = dh1_ref[...] + r * a_ - xr * (r * jnp.mean(a_ * xr, axis=-1, keepdims=True))

    late_cols = D_MODEL - _EARLY_COLS
    return _call(
        body, name="in_proj_bwd", grid=(rows // tm,),
        in_specs=[_rows(tm, D_MODEL), _rows(tm, D_MODEL), _full((1, D_MODEL)), _full((D_IN, D_MODEL)), _rows(tm, D_IN),
                  pl.BlockSpec(memory_space=pl.ANY)],
        out_specs=[_rows(tm, D_MODEL), _full((D_IN, late_cols)), _full((1, D_MODEL))],
        out_shape=[_sds((rows, D_MODEL)), _sds((D_IN, late_cols)), _sds((1, D_MODEL))],
        compiler_params=_params(52, ("arbitrary",)),
    )(x2d, dh1, g1, w_in_t, d_proj, runs_after)


def _iota(shape, axis):
    return lax.broadcasted_iota(jnp.int32, shape, axis)


def _sum_of_thirds(f, a):
    hi = a.astype(BF16)
    rest = a - hi.astype(F32)
    mid = rest.astype(BF16)
    low = (rest - mid.astype(F32)).astype(BF16)
    return (f(hi) + f(mid)) + f(low)


@jax.custom_vjp
def _pick_rows(e, a):
    return _sum_of_thirds(lambda part: _dot(e, part), a)


def _pick_rows_fwd(e, a):
    return _pick_rows(e, a), e


def _pick_rows_bwd(e, ct):
    return jnp.zeros_like(e), _sum_of_thirds(lambda part: _dot_tn(e, part), ct)


_pick_rows.defvjp(_pick_rows_fwd, _pick_rows_bwd)


@jax.custom_vjp
def _pick_cols(a, e):
    return _sum_of_thirds(lambda part: _dot(part, e), a)


def _pick_cols_fwd(a, e):
    return _pick_cols(a, e), e


def _pick_cols_bwd(e, ct):
    return _sum_of_thirds(lambda part: _dot_nt(part, e), ct), jnp.zeros_like(e)


_pick_cols.defvjp(_pick_cols_fwd, _pick_cols_bwd)


_HALF_GROUPS = SSM_GROUPS // 2
_N_SHIFT = SSM_STATE.bit_length() - 1
_P_SHIFT = SSM_GROUP_CH.bit_length() - 1


def _s5_operands(lam_re, lam_im, log_step, b_re, b_im, c_re, c_im):
    g, n, p = SSM_GROUPS, SSM_STATE, SSM_GROUP_CH
    gn, gp, hn_, hp = g * n, g * p, _HALF_GROUPS * n, _HALF_GROUPS * p
    eye_g = _iota((g, g), 0) == _iota((g, g), 1)
    step = jnp.sum(jnp.where(eye_g, jnp.exp(log_step), 0.0), axis=1, keepdims=True)
    a_re = lam_re * step
    a_im = lam_im * step
    mag = jnp.exp(a_re)
    lbar_re = mag * jnp.cos(a_im)
    lbar_im = mag * jnp.sin(a_im)
    n_re = lbar_re - 1.0
    den = lam_re * lam_re + lam_im * lam_im
    f_re = (n_re * lam_re + lbar_im * lam_im) / den
    f_im = (lbar_im * lam_re - n_re * lam_im) / den

    spread_n = (_iota((n, gn), 0) == (_iota((n, gn), 1) & (n - 1))).astype(BF16)
    own_g = _iota((g, gn), 0) == (_iota((g, gn), 1) >> _N_SHIFT)

    def to_row(a):
        return jnp.sum(jnp.where(own_g, _pick_cols(a, spread_n), 0.0), axis=0, keepdims=True)

    per_group = ((_iota((gp, g), 0) >> _P_SHIFT) == _iota((gp, g), 1)).astype(BF16)
    fx_re, fx_im = _pick_rows(per_group, f_re), _pick_rows(per_group, f_im)
    bbar_re = fx_re * b_re - fx_im * b_im
    bbar_im = fx_re * b_im + fx_im * b_re

    tile_n = (_iota((n, hn_), 0) == (_iota((n, hn_), 1) & (n - 1))).astype(BF16)
    same_group = (_iota((hp, hn_), 0) >> _P_SHIFT) == (_iota((hp, hn_), 1) >> _N_SHIFT)

    def embed(a, hf):
        return jnp.where(same_group, _pick_cols(a[hf * hp:(hf + 1) * hp], tile_n), 0.0)

    return (to_row(lbar_re), to_row(lbar_im), embed(bbar_re, 0), embed(bbar_re, 1), embed(bbar_im, 0),
            embed(bbar_im, 1), embed(c_re, 0), embed(c_re, 1), embed(c_im, 0), embed(c_im, 1))


_S5_PARAM_SHAPES = ((SSM_GROUPS, SSM_STATE), (SSM_GROUPS, SSM_STATE), (1, SSM_GROUPS),
                    (D_SSM, SSM_STATE), (D_SSM, SSM_STATE), (D_SSM, SSM_STATE), (D_SSM, SSM_STATE))
_CM_SHAPE = (2, _HALF_GROUPS * SSM_GROUP_CH, _HALF_GROUPS * SSM_STATE)
_S5_OPERAND_SHAPES = ((1, SSM_LANES), (1, SSM_LANES), _CM_SHAPE, _CM_SHAPE, _CM_SHAPE, _CM_SHAPE)


def _s5_params_fwd(*params):
    def body(*refs):
        ins, (lre_ref, lim_ref, btre_ref, btim_ref, cmre_ref, cmim_ref) = refs[:7], refs[7:]
        vals = _s5_operands(*[r[...] for r in ins])
        lre_ref[...] = vals[0]
        lim_ref[...] = vals[1]
        for ref, pair in zip((btre_ref, btim_ref, cmre_ref, cmim_ref), (vals[2:4], vals[4:6], vals[6:8], vals[8:10])):
            ref[0] = pair[0].astype(BF16)
            ref[1] = pair[1].astype(BF16)

    dtypes = (F32, F32, BF16, BF16, BF16, BF16)
    return _call(
        body, name="s5_params_fwd",
        in_specs=[_full(s) for s in _S5_PARAM_SHAPES], out_specs=[_full(s) for s in _S5_OPERAND_SHAPES],
        out_shape=[_sds(s, d) for s, d in zip(_S5_OPERAND_SHAPES, dtypes)], compiler_params=_params(32),
    )(*params)


_BC_SIDE_BY_SIDE = (D_SSM, 4 * SSM_STATE)


def _s5_params_bwd(params, cotangents, runs_after):
    def body(*refs):
        ins, (dlre, dlim, dbtre, dbtim, dcmre, dcmim), outs = refs[:7], refs[7:13], refs[14:]
        _, vjp = jax.vjp(_s5_operands, *[r[...] for r in ins])
        cts = (dlre[...], dlim[...], dbtre[0], dbtre[1], dbtim[0], dbtim[1], dcmre[0], dcmre[1], dcmim[0], dcmim[1])
        grads = vjp(cts)
        for ref, val in zip(outs[:3], grads[:3]):
            ref[...] = val
        outs[3][...] = jnp.concatenate(grads[3:], axis=1).astype(BF16)

    out_shapes = _S5_PARAM_SHAPES[:3] + (_BC_SIDE_BY_SIDE,)
    return _call(
        body, name="s5_params_bwd",
        in_specs=[_full(s) for s in _S5_PARAM_SHAPES + _S5_OPERAND_SHAPES] + [pl.BlockSpec(memory_space=pl.ANY)],
        out_specs=[_full(s) for s in out_shapes],
        out_shape=[_sds(s, d) for s, d in zip(out_shapes, (F32, F32, F32, BF16))], compiler_params=_params(48),
    )(*params, *cotangents, runs_after)


def _scan_geometry(n_seq, seq):
    slab = n_seq * SCAN_CHUNKS
    steps = seq // SCAN_CHUNKS
    tile_rows = slab * SCAN_TILE_STEPS
    n_tiles = steps // SCAN_TILE_STEPS
    return slab, steps, tile_rows, n_tiles


_SCAN_PARTS = D_SSM // LANES


def _whole_parts(rows):
    return [_full((rows, LANES))] * _SCAN_PARTS


def _part_shapes(rows):
    return [_sds((rows, LANES))] * _SCAN_PARTS


def _load_chunks(parts, first_chunk, n_chunks, steps, slab):
    return jnp.concatenate([
        jnp.concatenate([ref[pl.ds(first_chunk + q, steps, stride=slab), :] for ref in parts], axis=1)
        for q in range(n_chunks)], axis=0)


def _store_chunks(parts, first_chunk, value, steps, slab):
    for q in range(value.shape[0] // steps):
        for j, ref in enumerate(parts):
            ref[pl.ds(first_chunk + q, steps, stride=slab), :] = value[q * steps:(q + 1) * steps,
                                                                     j * LANES:(j + 1) * LANES]


def _join_parts(parts):
    return jnp.concatenate([ref[...] for ref in parts], axis=1)


def _split_parts(parts, value):
    for j, ref in enumerate(parts):
        ref[...] = value[:, j * LANES:(j + 1) * LANES]


def _complex_power(re, im, n):
    out = None
    while n:
        if n & 1:
            out = (re, im) if out is None else (out[0] * re - out[1] * im, out[0] * im + out[1] * re)
        n >>= 1
        if n:
            re, im = re * re - im * im, 2.0 * re * im
    return out


def _chunk_carry(sum_re, sum_im, carry_re, carry_im, a_re, a_im, n_seq, reverse):
    carry_re[...] = jnp.zeros_like(carry_re)
    carry_im[...] = jnp.zeros_like(carry_im)
    for s in range(n_seq):
        order = range(SCAN_CHUNKS - 2, -1, -1) if reverse else range(1, SCAN_CHUNKS)
        for c in order:
            r = s * SCAN_CHUNKS + c
            p = r + 1 if reverse else r - 1
            p_re, p_im = carry_re[p:p + 1, :], carry_im[p:p + 1, :]
            carry_re[r:r + 1, :] = a_re * p_re - a_im * p_im + sum_re[p:p + 1, :]
            carry_im[r:r + 1, :] = a_re * p_im + a_im * p_re + sum_im[p:p + 1, :]


def _s5_scan_fwd(u_parts, bt_re, bt_im, cm_re, cm_im, lbar_re, lbar_im, d_row, n_seq, seq):
    slab, steps, tile_rows, n_tiles = _scan_geometry(n_seq, seq)
    rows = u_parts[0].shape[0]

    def body(*refs):
        u_refs, refs = refs[:_SCAN_PARTS], refs[_SCAN_PARTS:]
        (bre_ref, bim_ref, cre_ref, cim_ref, lre_ref, lim_ref, d_ref), refs = refs[:7], refs[7:]
        y_refs, (hre_ref, him_ref, st_re, st_im, h0_re, h0_im, buf_re, buf_im) = refs[:_SCAN_PARTS], refs[_SCAN_PARTS:]
        second = pl.program_id(0) == 1
        i = pl.program_id(1)

        @pl.when(jnp.logical_and(i == 0, jnp.logical_not(second)))
        def _():
            st_re[...] = jnp.zeros_like(st_re)
            st_im[...] = jnp.zeros_like(st_im)

        u = _join_parts(u_refs)
        ub = u.astype(BF16)
        for hf in range(2):
            cols = slice(hf * 1024, (hf + 1) * 1024)
            buf_re[:, cols] = _dot(ub[:, hf * 256:(hf + 1) * 256], bre_ref[hf])
            buf_im[:, cols] = _dot(ub[:, hf * 256:(hf + 1) * 256], bim_ref[hf])

        for lc in range(SSM_LANES // SCAN_LANE_CHUNK):
            cols = slice(lc * SCAN_LANE_CHUNK, (lc + 1) * SCAN_LANE_CHUNK)
            l_re = jnp.broadcast_to(lre_ref[:, cols], (slab, SCAN_LANE_CHUNK))
            l_im = jnp.broadcast_to(lim_ref[:, cols], (slab, SCAN_LANE_CHUNK))

            def scan_tile(keep_states):
                def step(t, carry):
                    s_re, s_im = carry
                    r0 = pl.multiple_of(t * slab, slab)
                    n_re = l_re * s_re - l_im * s_im + buf_re[pl.ds(r0, slab), cols]
                    n_im = l_re * s_im + l_im * s_re + buf_im[pl.ds(r0, slab), cols]
                    if keep_states:
                        buf_re[pl.ds(r0, slab), cols] = n_re
                        buf_im[pl.ds(r0, slab), cols] = n_im
                    return n_re, n_im

                s_re, s_im = lax.fori_loop(0, SCAN_TILE_STEPS, step, (st_re[:, cols], st_im[:, cols]), unroll=True)
                st_re[:, cols] = s_re
                st_im[:, cols] = s_im

            pl.when(jnp.logical_not(second))(lambda: scan_tile(False))
            pl.when(second)(lambda: scan_tile(True))

        @pl.when(jnp.logical_and(i == n_tiles - 1, jnp.logical_not(second)))
        def _():
            a_re, a_im = _complex_power(lre_ref[...], lim_ref[...], steps)
            _chunk_carry(st_re, st_im, h0_re, h0_im, a_re, a_im, n_seq, reverse=False)
            st_re[...] = h0_re[...]
            st_im[...] = h0_im[...]

        @pl.when(second)
        def _():
            h_re = buf_re[...].astype(BF16)
            h_im = buf_im[...].astype(BF16)
            hre_ref[...] = h_re
            him_ref[...] = h_im
            for hf in range(2):
                cols = slice(hf * 1024, (hf + 1) * 1024)
                ycols = slice(hf * 256, (hf + 1) * 256)
                y_half = (_dot_nt(h_re[:, cols], cre_ref[hf]) - _dot_nt(h_im[:, cols], cim_ref[hf])
                          + d_ref[:, ycols] * u[:, ycols])
                _split_parts(y_refs[2 * hf:2 * hf + 2], y_half)

    tile = lambda w: pl.BlockSpec((tile_rows, w), lambda p, i: (i, 0))
    out_tile = lambda w: pl.BlockSpec((tile_rows, w), lambda p, i: (i * p, 0))
    cm = _full(_CM_SHAPE)
    outs = _call(
        body, name="s5_scan_fwd", grid=(2, n_tiles),
        in_specs=[tile(LANES)] * _SCAN_PARTS + [cm, cm, cm, cm, _full((1, SSM_LANES)), _full((1, SSM_LANES)),
                                                _full((1, 512))],
        out_specs=[out_tile(LANES)] * _SCAN_PARTS + [out_tile(SSM_LANES), out_tile(SSM_LANES)],
        out_shape=_part_shapes(rows) + [_sds((rows, SSM_LANES), BF16), _sds((rows, SSM_LANES), BF16)],
        scratch_shapes=[pltpu.VMEM((slab, SSM_LANES), F32)] * 4 + [pltpu.VMEM((tile_rows, SSM_LANES), F32)] * 2,
        compiler_params=_params(40, ("arbitrary", "arbitrary")),
    )(*u_parts, bt_re, bt_im, cm_re, cm_im, lbar_re, lbar_im, d_row)
    return outs[:_SCAN_PARTS], outs[_SCAN_PARTS], outs[_SCAN_PARTS + 1]


def _s5_scan_bwd(dy_parts, u_parts, h_re, h_im, bt_re, bt_im, cm_re, cm_im, lbar_re, lbar_im, d_row, n_seq, seq):
    slab, steps, tile_rows, n_tiles = _scan_geometry(n_seq, seq)
    rows = u_parts[0].shape[0]

    def body(*refs):
        dy_refs, u_refs, refs = refs[:_SCAN_PARTS], refs[_SCAN_PARTS:2 * _SCAN_PARTS], refs[2 * _SCAN_PARTS:]
        (hre_ref, him_ref, bre_ref, bim_ref, cre_ref, cim_ref, lre_ref, lim_ref, d_ref), refs = refs[:9], refs[9:]
        du_refs, refs = refs[:_SCAN_PARTS], refs[_SCAN_PARTS:]
        (dbre_ref, dbim_ref, dcre_ref, dcim_ref, dlre_ref, dlim_ref, dd_ref,
         st_re, st_im, g0_re, g0_im, acc_re, acc_im, buf_re, buf_im) = refs
        second = pl.program_id(0) == 1
        i = pl.program_id(1)

        @pl.when(jnp.logical_and(i == 0, jnp.logical_not(second)))
        def _():
            st_re[...] = jnp.zeros_like(st_re)
            st_im[...] = jnp.zeros_like(st_im)
            acc_re[...] = jnp.zeros_like(acc_re)
            acc_im[...] = jnp.zeros_like(acc_im)
            for ref in (dbre_ref, dbim_ref, dcre_ref, dcim_ref, dd_ref):
                ref[...] = jnp.zeros_like(ref)

        dy = _join_parts(dy_refs)
        dyb = dy.astype(BF16)
        for hf in range(2):
            cols = slice(hf * 1024, (hf + 1) * 1024)
            buf_re[:, cols] = _dot(dyb[:, hf * 256:(hf + 1) * 256], cre_ref[hf])
            buf_im[:, cols] = -_dot(dyb[:, hf * 256:(hf + 1) * 256], cim_ref[hf])

        for lc in range(SSM_LANES // SCAN_LANE_CHUNK):
            cols = slice(lc * SCAN_LANE_CHUNK, (lc + 1) * SCAN_LANE_CHUNK)
            l_re = jnp.broadcast_to(lre_ref[:, cols], (slab, SCAN_LANE_CHUNK))
            l_im = jnp.broadcast_to(lim_ref[:, cols], (slab, SCAN_LANE_CHUNK))

            def advance(r0, s_re, s_im):
                n_re = l_re * s_re + l_im * s_im + buf_re[pl.ds(r0, slab), cols]
                n_im = l_re * s_im - l_im * s_re + buf_im[pl.ds(r0, slab), cols]
                buf_re[pl.ds(r0, slab), cols] = n_re
                buf_im[pl.ds(r0, slab), cols] = n_im
                return n_re, n_im

            def row0(k):
                return pl.multiple_of((SCAN_TILE_STEPS - 1 - k) * slab, slab)

            @pl.when(jnp.logical_not(second))
            def _():
                s_re, s_im = lax.fori_loop(0, SCAN_TILE_STEPS, lambda k, s: advance(row0(k), *s),
                                           (st_re[:, cols], st_im[:, cols]), unroll=True)
                st_re[:, cols] = s_re
                st_im[:, cols] = s_im

            @pl.when(second)
            def _():
                def step(k, carry):
                    s_re, s_im, a_re, a_im = carry
                    r0 = row0(k)
                    hr = hre_ref[pl.ds(r0, slab), cols].astype(F32)
                    hi = him_ref[pl.ds(r0, slab), cols].astype(F32)
                    a_re = a_re + s_re * hr + s_im * hi
                    a_im = a_im + s_im * hr - s_re * hi
                    return advance(r0, s_re, s_im) + (a_re, a_im)

                zero = jnp.zeros((slab, SCAN_LANE_CHUNK), F32)
                s_re, s_im, a_re, a_im = lax.fori_loop(
                    0, SCAN_TILE_STEPS, step, (st_re[:, cols], st_im[:, cols], zero, zero), unroll=True)
                st_re[:, cols] = s_re
                st_im[:, cols] = s_im
                acc_re[:, cols] += a_re
                acc_im[:, cols] += a_im

        @pl.when(jnp.logical_and(i == n_tiles - 1, jnp.logical_not(second)))
        def _():
            p_re, p_im = _complex_power(lre_ref[...], lim_ref[...], steps)
            _chunk_carry(st_re, st_im, g0_re, g0_im, p_re, -p_im, n_seq, reverse=True)
            st_re[...] = g0_re[...]
            st_im[...] = g0_im[...]

        @pl.when(second)
        def _():
            u = _join_parts(u_refs)
            ub = u.astype(BF16)
            g_re = buf_re[...].astype(BF16)
            g_im = buf_im[...].astype(BF16)
            dd_ref[...] += jnp.sum(dy * u, axis=0, keepdims=True)
            for hf in range(2):
                cols = slice(hf * 1024, (hf + 1) * 1024)
                ycols = slice(hf * 256, (hf + 1) * 256)
                du_half = (_dot_nt(g_re[:, cols], bre_ref[hf]) + _dot_nt(g_im[:, cols], bim_ref[hf])
                           + d_ref[:, ycols] * dy[:, ycols])
                _split_parts(du_refs[2 * hf:2 * hf + 2], du_half)
                for q4 in range(_HALF_GROUPS // 4):
                    ch = slice(hf * 256 + q4 * 64, hf * 256 + (q4 + 1) * 64)
                    st = slice(hf * 1024 + q4 * 256, hf * 1024 + (q4 + 1) * 256)
                    blk = (hf, slice(q4 * 64, (q4 + 1) * 64), slice(q4 * 256, (q4 + 1) * 256))
                    dbre_ref[blk] += _dot_tn(ub[:, ch], g_re[:, st])
                    dbim_ref[blk] += _dot_tn(ub[:, ch], g_im[:, st])
                    dcre_ref[blk] += _dot_tn(dyb[:, ch], hre_ref[:, st])
                    dcim_ref[blk] -= _dot_tn(dyb[:, ch], him_ref[:, st])

        @pl.when(jnp.logical_and(i == n_tiles - 1, second))
        def _():
            dlre_ref[...] = jnp.sum(acc_re[...], axis=0, keepdims=True)
            dlim_ref[...] = jnp.sum(acc_im[...], axis=0, keepdims=True)

    tile = lambda w: pl.BlockSpec((tile_rows, w), lambda p, i: (n_tiles - 1 - i, 0))
    second_tile = lambda w: pl.BlockSpec((tile_rows, w), lambda p, i: (n_tiles - 1 - i * p, 0))
    cm = _full(_CM_SHAPE)
    row = _full((1, SSM_LANES))
    outs = _call(
        body, name="s5_scan_bwd", grid=(2, n_tiles),
        in_specs=[tile(LANES)] * _SCAN_PARTS + [second_tile(LANES)] * _SCAN_PARTS
        + [second_tile(SSM_LANES), second_tile(SSM_LANES), cm, cm, cm, cm, row, row, _full((1, 512))],
        out_specs=[second_tile(LANES)] * _SCAN_PARTS + [cm, cm, cm, cm, row, row, _full((1, 512))],
        out_shape=(_part_shapes(rows) + [_sds(_CM_SHAPE)] * 4 + [_sds((1, SSM_LANES))] * 2 + [_sds((1, 512))]),
        scratch_shapes=[pltpu.VMEM((slab, SSM_LANES), F32)] * 6 + [pltpu.VMEM((tile_rows, SSM_LANES), F32)] * 2,
        compiler_params=_params(48, ("arbitrary", "arbitrary")),
    )(*dy_parts, *u_parts, h_re, h_im, bt_re, bt_im, cm_re, cm_im, lbar_re, lbar_im, d_row)
    return (outs[:_SCAN_PARTS],) + tuple(outs[_SCAN_PARTS:])


def _glu_gate(gl, a, zs):
    return gl * jax.nn.sigmoid(a) * _silu(zs)


def _glu_fwd(y_parts, zs, w_glu, b_glu, n_seq, seq):
    rows = zs.shape[0]
    tm = 1024
    slab, steps, _, _ = _scan_geometry(n_seq, seq)

    def body(*refs):
        y_refs, (zs_ref, w_ref, b_ref, o_ref) = refs[:_SCAN_PARTS], refs[_SCAN_PARTS:]
        y = _load_chunks(y_refs, pl.program_id(0) * (tm // steps), tm // steps, steps, slab)
        gl = jax.nn.gelu(y)
        a = _dot(gl.astype(BF16), w_ref[...]) + b_ref[...]
        o_ref[...] = _glu_gate(gl, a, zs_ref[...]).astype(BF16)

    return _call(
        body, name="glu_fwd", grid=(rows // tm,),
        in_specs=_whole_parts(rows) + [_rows(tm, 512), _full((512, 512)), _full((1, 512))],
        out_specs=_rows(tm, 512), out_shape=_sds((rows, 512), BF16),
        compiler_params=_params(32, ("arbitrary",)),
    )(*y_parts, zs, w_glu, b_glu)


def _glu_bwd(y_parts, zs, d_out, w_glu, b_glu, n_seq, seq):
    rows = zs.shape[0]
    tm = 512
    slab, steps, _, _ = _scan_geometry(n_seq, seq)

    def body(*refs):
        y_refs, (zs_ref, d_ref, w_ref, b_ref), refs = refs[:_SCAN_PARTS], refs[_SCAN_PARTS:_SCAN_PARTS + 4], refs[_SCAN_PARTS + 4:]
        dy_refs, (dzs_ref, dw_ref, db_ref) = refs[:_SCAN_PARTS], refs[_SCAN_PARTS:]
        first_chunk = pl.program_id(0) * (tm // steps)

        @pl.when(pl.program_id(0) == 0)
        def _():
            dw_ref[...] = jnp.zeros_like(dw_ref)
            db_ref[...] = jnp.zeros_like(db_ref)

        gl, gelu_vjp = jax.vjp(jax.nn.gelu, _load_chunks(y_refs, first_chunk, tm // steps, steps, slab))
        glb = gl.astype(BF16)
        a = _dot(glb, w_ref[...]) + b_ref[...]
        _, gate_vjp = jax.vjp(_glu_gate, gl, a, zs_ref[...])
        d_gl, d_a, d_zs = gate_vjp(d_ref[...])
        dab = d_a.astype(BF16)
        d_gl = d_gl + _dot_nt(dab, w_ref[...])
        _store_chunks(dy_refs, first_chunk, gelu_vjp(d_gl)[0], steps, slab)
        dzs_ref[...] = d_zs.astype(BF16)
        dw_ref[...] += _dot_tn(glb, dab)
        db_ref[...] += jnp.sum(d_a, axis=0, keepdims=True)

    *dy_parts, dzs, dw, db = _call(
        body, name="glu_bwd", grid=(rows // tm,),
        in_specs=_whole_parts(rows) + [_rows(tm, 512), _rows(tm, 512), _full((512, 512)), _full((1, 512))],
        out_specs=_whole_parts(rows) + [_rows(tm, 512), _full((512, 512)), _full((1, 512))],
        out_shape=_part_shapes(rows) + [_sds((rows, 512), BF16), _sds((512, 512)), _sds((1, 512))],
        compiler_params=_params(40, ("arbitrary",)),
    )(*y_parts, zs, d_out, w_glu, b_glu)
    return dy_parts, dzs, dw, db


_GROUP_ROWS = Q_PER_KV * BLOCK
_BLOCK_SHIFT = BLOCK.bit_length() - 1


def _attn_bias(j):
    query = _iota((BLOCK, _GROUP_ROWS), 1)
    dist_cur = (query & (BLOCK - 1)) - _iota((BLOCK, _GROUP_ROWS), 0)
    dist_prev = dist_cur + BLOCK
    head = query >> _BLOCK_SHIFT
    slope = jnp.zeros((BLOCK, _GROUP_ROWS), F32)
    for g in range(Q_PER_KV):
        slope = jnp.where(head == g, 2.0 ** (-(j * Q_PER_KV + g + 1)), slope)
    bias_cur = jnp.where(dist_cur >= 0, -slope * dist_cur.astype(F32), -jnp.inf)
    bias_prev = jnp.where(dist_prev < WINDOW, -slope * dist_prev.astype(F32), -jnp.inf)
    return bias_cur, bias_prev


_ATTN_BIAS_SCRATCH = pltpu.VMEM((KV_HEADS, 2, BLOCK, _GROUP_ROWS), F32)


def _fill_attn_bias(bias_ref):
    @pl.when(jnp.logical_and(pl.program_id(0) == 0, pl.program_id(1) == 0))
    def _():
        for j in range(KV_HEADS):
            bias_ref[j, 0], bias_ref[j, 1] = _attn_bias(j)


def _stack_heads(x, j):
    heads = range(j * Q_PER_KV, (j + 1) * Q_PER_KV)
    return jnp.concatenate([x[:, h * HEAD_DIM:(h + 1) * HEAD_DIM] for h in heads], axis=0)


def _head_rows(x, j):
    heads = range(j * Q_PER_KV, (j + 1) * Q_PER_KV)
    return jnp.concatenate([x[h:h + 1, :] for h in heads], axis=1)


def _sink_row(sk_ref, j):
    heads = range(j * Q_PER_KV, (j + 1) * Q_PER_KV)
    return jnp.concatenate([jnp.broadcast_to(sk_ref[0:1, h:h + 1], (1, BLOCK)) for h in heads], axis=1)


_ATTN_FWD_BLOCKS = 8


def _attn_fwd(q, k, v, za, sinks, n_seq, seq):
    nb = seq // BLOCK
    steps = nb // _ATTN_FWD_BLOCKS
    rows = q.shape[0]

    def body(q_ref, kc_ref, kp_ref, vc_ref, vp_ref, za_ref, sk_ref, o_ref, ao_ref, lse_ref, bias_ref):
        _fill_attn_bias(bias_ref)
        for t in range(_ATTN_FWD_BLOCKS):
            at = slice(t * BLOCK, (t + 1) * BLOCK)
            before = slice((t - 1) * BLOCK, t * BLOCK)
            q_all = q_ref[at, :]
            for j in range(KV_HEADS):
                js = slice(j * HEAD_DIM, (j + 1) * HEAD_DIM)
                bias_c, bias_p = bias_ref[j, 0], bias_ref[j, 1]
                q4 = _stack_heads(q_all, j)
                sc = _dot_nt(kc_ref[at, js], q4) + bias_c
                if t == 0:
                    sp = _dot_nt(kp_ref[:, js], q4) + jnp.where(pl.program_id(1) > 0, bias_p, -jnp.inf)
                    v_prev = vp_ref[:, js]
                else:
                    sp = _dot_nt(kc_ref[before, js], q4) + bias_p
                    v_prev = vc_ref[before, js]
                sink = _sink_row(sk_ref, j)
                m = jnp.maximum(jnp.max(jnp.maximum(sc, sp), axis=0, keepdims=True), sink)
                ec = jnp.exp(sc - m)
                ep = jnp.exp(sp - m)
                den = jnp.sum(ec + ep, axis=0, keepdims=True) + jnp.exp(sink - m)
                inv = 1.0 / den
                o4 = _dot_tn((ec * inv).astype(BF16), vc_ref[at, js]) + _dot_tn((ep * inv).astype(BF16), v_prev)
                lse4 = m + jnp.log(den)
                for g in range(Q_PER_KV):
                    h = j * Q_PER_KV + g
                    o_ref[at, h * HEAD_DIM:(h + 1) * HEAD_DIM] = o4[g * BLOCK:(g + 1) * BLOCK]
                    lse_ref[t * N_HEADS + h:t * N_HEADS + h + 1, :] = lse4[:, g * BLOCK:(g + 1) * BLOCK]
        ao_ref[...] = (o_ref[...] * _silu(za_ref[...])).astype(BF16)

    cur = lambda w: pl.BlockSpec((_ATTN_FWD_BLOCKS * BLOCK, w), lambda b, n: (b * steps + n, 0))
    prev = lambda w: pl.BlockSpec((BLOCK, w), lambda b, n: (b * nb + jnp.maximum(_ATTN_FWD_BLOCKS * n - 1, 0), 0))
    lse_rows = rows // BLOCK * N_HEADS
    return _call(
        body, name="attn_fwd", grid=(n_seq, steps),
        in_specs=[cur(512), cur(128), prev(128), cur(128), prev(128), cur(512), _full((1, N_HEADS))],
        out_specs=[cur(512), cur(512),
                   pl.BlockSpec((_ATTN_FWD_BLOCKS * N_HEADS, BLOCK), lambda b, n: (b * steps + n, 0))],
        out_shape=[_sds((rows, 512)), _sds((rows, 512), BF16), _sds((lse_rows, BLOCK))],
        scratch_shapes=[_ATTN_BIAS_SCRATCH], compiler_params=_params(32, ("arbitrary", "arbitrary")),
    )(q, k, k, v, v, za, sinks)


def _in_proj_attn_fwd(hn, w_in_t, sinks, n_seq, seq):
    rows = hn.shape[0]
    tm = _ATTN_FWD_BLOCKS * BLOCK
    tiles_per_seq = seq // tm
    slab, steps, _, _ = _scan_geometry(n_seq, seq)

    def body(hn_ref, w_ref, sk_ref, *refs):
        u_parts = refs[:_SCAN_PARTS]
        (zs_ref, q_ref, k_ref, v_ref, za_ref, o_ref, ao_ref, lse_ref, bias_ref, k_last, v_last) = refs[_SCAN_PARTS:]
        i = pl.program_id(0)

        @pl.when(i == 0)
        def _():
            for j in range(KV_HEADS):
                bias_ref[j, 0], bias_ref[j, 1] = _attn_bias(j)
            k_last[...] = jnp.zeros_like(k_last)
            v_last[...] = jnp.zeros_like(v_last)

        for_attn = _dot_nt(hn_ref[...], w_ref[2 * D_SSM:, :])
        q_ref[...] = (for_attn[:, 0:512] * ATTN_SCALE).astype(BF16)
        k_ref[...] = for_attn[:, 512:640].astype(BF16)
        v_ref[...] = for_attn[:, 640:768].astype(BF16)
        za_ref[...] = for_attn[:, 768:1280]

        has_before = lax.rem(i, tiles_per_seq) > 0
        for t in range(_ATTN_FWD_BLOCKS):
            at = slice(t * BLOCK, (t + 1) * BLOCK)
            before = slice((t - 1) * BLOCK, t * BLOCK)
            q_all = q_ref[at, :]
            for j in range(KV_HEADS):
                js = slice(j * HEAD_DIM, (j + 1) * HEAD_DIM)
                bias_c, bias_p = bias_ref[j, 0], bias_ref[j, 1]
                q4 = _stack_heads(q_all, j)
                sc = _dot_nt(k_ref[at, js], q4) + bias_c
                if t == 0:
                    sp = _dot_nt(k_last[:, js], q4) + jnp.where(has_before, bias_p, -jnp.inf)
                    v_prev = v_last[:, js]
                else:
                    sp = _dot_nt(k_ref[before, js], q4) + bias_p
                    v_prev = v_ref[before, js]
                sink = _sink_row(sk_ref, j)
                m = jnp.maximum(jnp.max(jnp.maximum(sc, sp), axis=0, keepdims=True), sink)
                ec = jnp.exp(sc - m)
                ep = jnp.exp(sp - m)
                den = jnp.sum(ec + ep, axis=0, keepdims=True) + jnp.exp(sink - m)
                inv = 1.0 / den
                o4 = _dot_tn((ec * inv).astype(BF16), v_ref[at, js]) + _dot_tn((ep * inv).astype(BF16), v_prev)
                lse4 = m + jnp.log(den)
                for g in range(Q_PER_KV):
                    h = j * Q_PER_KV + g
                    o_ref[at, h * HEAD_DIM:(h + 1) * HEAD_DIM] = o4[g * BLOCK:(g + 1) * BLOCK]
                    lse_ref[t * N_HEADS + h:t * N_HEADS + h + 1, :] = lse4[:, g * BLOCK:(g + 1) * BLOCK]
        ao_ref[...] = (o_ref[...] * _silu(za_ref[...])).astype(BF16)
        k_last[...] = k_ref[tm - BLOCK:, :]
        v_last[...] = v_ref[tm - BLOCK:, :]

        for_ssm = _dot_nt(hn_ref[...], w_ref[0:2 * D_SSM, :])
        _store_chunks(u_parts, i * (tm // steps), for_ssm[:, 0:D_SSM], steps, slab)
        zs_ref[...] = for_ssm[:, D_SSM:]

    lse_rows = rows // BLOCK * N_HEADS
    *u_parts, zs, q, k, v, za, o, attn_out, lse = _call(
        body, name="in_proj_attn_fwd", grid=(rows // tm,),
        in_specs=[_rows(tm, D_MODEL), _full((D_IN, D_MODEL)), _full((1, N_HEADS))],
        out_specs=_whole_parts(rows) + [_rows(tm, 512), _rows(tm, 512), _rows(tm, 128), _rows(tm, 128), _rows(tm, 512),
                                        _rows(tm, 512), _rows(tm, 512),
                                        pl.BlockSpec((_ATTN_FWD_BLOCKS * N_HEADS, BLOCK), lambda i: (i, 0))],
        out_shape=_part_shapes(rows) + [_sds((rows, 512)), _sds((rows, 512), BF16), _sds((rows, 128), BF16),
                                        _sds((rows, 128), BF16), _sds((rows, 512)), _sds((rows, 512)),
                                        _sds((rows, 512), BF16), _sds((lse_rows, BLOCK))],
        scratch_shapes=[_ATTN_BIAS_SCRATCH, pltpu.VMEM((BLOCK, KV_HEADS * HEAD_DIM), BF16),
                        pltpu.VMEM((BLOCK, KV_HEADS * HEAD_DIM), BF16)],
        compiler_params=_params(56, ("arbitrary",)),
    )(hn, w_in_t, sinks)
    return u_parts, zs, q, k, v, za, o, attn_out, lse


_ATTN_BWD_BLOCKS = 8


def _attn_bwd(q, k, v, za, o, lse, d_ao, sinks, n_seq, seq):
    nb = seq // BLOCK
    per_step = _ATTN_BWD_BLOCKS
    steps = nb // per_step
    rows = q.shape[0]

    def body(q_ref, kc_ref, kp_ref, vc_ref, vp_ref, za_ref, o_ref, lse_ref, d_ref, sk_ref,
             dq_ref, dk_ref, dv_ref, dza_ref, dsk_ref, bias_ref, dk_carry, dv_carry):
        step = pl.program_id(1)
        first_block = nb - per_step * (step + 1)
        _fill_attn_bias(bias_ref)

        @pl.when(jnp.logical_and(pl.program_id(0) == 0, step == 0))
        def _():
            dsk_ref[...] = jnp.zeros_like(dsk_ref)
            dk_carry[...] = jnp.zeros_like(dk_carry)
            dv_carry[...] = jnp.zeros_like(dv_carry)

        _, gate_vjp = jax.vjp(lambda o_, z_: o_ * _silu(z_), o_ref[...], za_ref[...])
        d_o, d_za = gate_vjp(d_ref[...])
        dza_ref[...] = d_za.astype(BF16)

        for j in range(KV_HEADS):
            js = slice(j * HEAD_DIM, (j + 1) * HEAD_DIM)
            bias_c, bias_p = bias_ref[j, 0], bias_ref[j, 1]
            sink = _sink_row(sk_ref, j)
            sink_loss = jnp.zeros((1, _GROUP_ROWS), F32)
            dk_from_next = jnp.where(step > 0, dk_carry[j], 0.0)
            dv_from_next = jnp.where(step > 0, dv_carry[j], 0.0)
            for t in reversed(range(per_step)):
                at = slice(t * BLOCK, (t + 1) * BLOCK)
                kc, vc = kc_ref[at, js], vc_ref[at, js]
                if t > 0:
                    before = slice((t - 1) * BLOCK, t * BLOCK)
                    kp, vp, bias_before = kc_ref[before, js], vc_ref[before, js], bias_p
                else:
                    kp, vp, bias_before = kp_ref[:, js], vp_ref[:, js], jnp.where(first_block > 0, bias_p, -jnp.inf)
                q4 = _stack_heads(q_ref[at, :], j)
                do4b = _stack_heads(d_o[at], j).astype(BF16)
                lse4 = _head_rows(lse_ref[t * N_HEADS:(t + 1) * N_HEADS, :], j)
                pc = jnp.exp(_dot_nt(kc, q4) + bias_c - lse4)
                pp = jnp.exp(_dot_nt(kp, q4) + bias_before - lse4)
                dpc = _dot_nt(vc, do4b)
                dpp = _dot_nt(vp, do4b)
                delta = jnp.sum(pc * dpc + pp * dpp, axis=0, keepdims=True)
                dsc = (pc * (dpc - delta)).astype(BF16)
                dsp = (pp * (dpp - delta)).astype(BF16)
                dq4 = ((_dot_tn(dsc, kc) + _dot_tn(dsp, kp)) * ATTN_SCALE).astype(BF16)
                sink_loss = sink_loss + jnp.exp(sink - lse4) * delta
                for g in range(Q_PER_KV):
                    h = j * Q_PER_KV + g
                    dq_ref[at, h * HEAD_DIM:(h + 1) * HEAD_DIM] = dq4[g * BLOCK:(g + 1) * BLOCK]
                dk_ref[at, js] = (_dot(dsc, q4) + dk_from_next).astype(BF16)
                dv_ref[at, js] = (_dot(pc.astype(BF16), do4b) + dv_from_next).astype(BF16)
                dk_from_next = _dot(dsp, q4)
                dv_from_next = _dot(pp.astype(BF16), do4b)
            dk_carry[j] = dk_from_next
            dv_carry[j] = dv_from_next
            for g in range(Q_PER_KV):
                h = j * Q_PER_KV + g
                dsk_ref[0:1, h:h + 1] -= jnp.sum(sink_loss[:, g * BLOCK:(g + 1) * BLOCK], axis=1, keepdims=True)

    cur = lambda w: pl.BlockSpec((per_step * BLOCK, w), lambda b, s: (b * steps + steps - 1 - s, 0))
    prev = lambda w: pl.BlockSpec((BLOCK, w), lambda b, s: (b * nb + jnp.maximum(nb - per_step * (s + 1) - 1, 0), 0))
    return _call(
        body, name="attn_bwd", grid=(n_seq, steps),
        in_specs=[cur(512), cur(128), prev(128), cur(128), prev(128), cur(512), cur(512),
                  pl.BlockSpec((per_step * N_HEADS, BLOCK), lambda b, s: (b * steps + steps - 1 - s, 0)), cur(512),
                  _full((1, N_HEADS))],
        out_specs=[cur(512), cur(128), cur(128), cur(512), _full((1, N_HEADS))],
        out_shape=[_sds((rows, 512), BF16), _sds((rows, 128), BF16), _sds((rows, 128), BF16),
                   _sds((rows, 512), BF16), _sds((1, N_HEADS))],
        scratch_shapes=[_ATTN_BIAS_SCRATCH, pltpu.VMEM((KV_HEADS, BLOCK, HEAD_DIM), F32),
                        pltpu.VMEM((KV_HEADS, BLOCK, HEAD_DIM), F32)],
        compiler_params=_params(32, ("arbitrary", "arbitrary")),
    )(q, k, k, v, v, za, o, lse, d_ao, sinks)


def _tail(ssm_out, attn_out, x2d, p2d, target, w_out, g2, w_gate, b_gate, w_proj):
    rows = x2d.shape[0]
    tm = 512

    def body(so_ref, ao_ref, x_ref, p_ref, t_ref, wo_ref, g2_ref, wg_ref, bg_ref, wp_ref,
             dh1_ref, dso_ref, dao_ref, dwo_ref, dwg_ref, dwp_ref, dbg_ref, dg2_ref, loss_ref):
        @pl.when(pl.program_id(0) == 0)
        def _():
            for ref in (dwo_ref, dwg_ref, dwp_ref, dbg_ref, dg2_ref, loss_ref):
                ref[...] = jnp.zeros_like(ref)

        cat = jnp.concatenate([so_ref[...], ao_ref[...]], axis=1)
        g2 = g2_ref[...]
        mixed = _dot(cat, wo_ref[...])
        r = lax.rsqrt(jnp.mean(mixed * mixed, axis=-1, keepdims=True) + EPS)
        mr = mixed * r
        h1 = x_ref[...] + mr * g2
        h1b = h1.astype(BF16)
        gate = jax.nn.sigmoid(_dot(h1b, wg_ref[...]) + bg_ref[...])
        pb = p_ref[...].astype(BF16)
        wp_blocks = [slice(j * D_PLE, (j + 1) * D_PLE) for j in range(N_CHIPS)]
        pp = jnp.concatenate([_dot(pb, wp_ref[blk, :]) for blk in wp_blocks], axis=1)
        err = h1 + gate * pp - t_ref[...]
        loss_ref[...] += 0.5 * jnp.sum(jnp.mean(err * err, axis=-1, keepdims=True), axis=0, keepdims=True)

        dh2 = err * (1.0 / D_MODEL)
        d_glin = dh2 * pp * gate * (1.0 - gate)
        d_glin_b = d_glin.astype(BF16)
        dwg_ref[...] += _dot_tn(h1b, d_glin_b)
        dbg_ref[...] += jnp.sum(d_glin, axis=0, keepdims=True)
        d_pp = (dh2 * gate).astype(BF16)
        for blk in wp_blocks:
            dwp_ref[blk, :] += _dot_tn(pb, d_pp[:, blk])
        dh1 = dh2 + _dot_nt(d_glin_b, wg_ref[...])
        dh1_ref[...] = dh1
        dg2_ref[...] += jnp.sum(dh1 * mr, axis=0, keepdims=True)
        a_ = dh1 * g2
        d_mixed = (r * a_ - mr * (r * jnp.mean(a_ * mr, axis=-1, keepdims=True))).astype(BF16)
        dwo_ref[...] += _dot_tn(cat, d_mixed)
        d_cat = _dot_nt(d_mixed, wo_ref[...])
        dso_ref[...] = d_cat[:, 0:512]
        dao_ref[...] = d_cat[:, 512:1024]

    return _call(
        body, name="tail_fwd_bwd", grid=(rows // tm,),
        in_specs=[_rows(tm, 512), _rows(tm, 512), _rows(tm, D_MODEL), _rows(tm, D_PLE), _rows(tm, D_MODEL),
                  _full((D_MODEL, D_MODEL)), _full((1, D_MODEL)), _full((D_MODEL, D_MODEL)), _full((1, D_MODEL)),
                  _full((N_CHIPS * D_PLE, D_PLE))],
        out_specs=[_rows(tm, D_MODEL), _rows(tm, 512), _rows(tm, 512), _full((D_MODEL, D_MODEL)),
                   _full((D_MODEL, D_MODEL)), _full((N_CHIPS * D_PLE, D_PLE)), _full((1, D_MODEL)), _full((1, D_MODEL)),
                   _full((1, 1))],
        out_shape=[_sds((rows, D_MODEL)), _sds((rows, 512)), _sds((rows, 512)), _sds((D_MODEL, D_MODEL)),
                   _sds((D_MODEL, D_MODEL)), _sds((N_CHIPS * D_PLE, D_PLE)), _sds((1, D_MODEL)), _sds((1, D_MODEL)),
                   _sds((1, 1))],
        compiler_params=_params(52, ("arbitrary",)),
    )(ssm_out, attn_out, x2d, p2d, target, w_out, g2, w_gate, b_gate, w_proj)


def _attn_bwd_in_proj_bwd_early(q, k, v, za, o, lse, d_ao, sinks, hn, du_parts, dzs, runs_after, n_seq, seq):
    nb = seq // BLOCK
    per_step = _ATTN_BWD_BLOCKS
    steps = nb // per_step
    tm = per_step * BLOCK
    rows = q.shape[0]
    slab, scan_steps, _, _ = _scan_geometry(n_seq, seq)
    q_at, k_at, v_at, za_at = 2 * D_SSM, 2 * D_SSM + 512, 2 * D_SSM + 640, 2 * D_SSM + 768

    def body(q_ref, kc_ref, kp_ref, vc_ref, vp_ref, za_ref, o_ref, lse_ref, d_ref, sk_ref, hn_ref, *refs):
        du_parts, others = refs[:_SCAN_PARTS], refs[_SCAN_PARTS:]
        dzs_ref, _, dproj_ref, dw_ref, dwb_ref, dsk_ref, bias_ref, dk_carry, dv_carry = others
        step = pl.program_id(1)
        first_block = nb - per_step * (step + 1)
        first_step = jnp.logical_and(pl.program_id(0) == 0, step == 0)
        _fill_attn_bias(bias_ref)

        @pl.when(first_step)
        def _():
            dsk_ref[...] = jnp.zeros_like(dsk_ref)
            dk_carry[...] = jnp.zeros_like(dk_carry)
            dv_carry[...] = jnp.zeros_like(dv_carry)
            dw_ref[...] = jnp.zeros_like(dw_ref)

        _, gate_vjp = jax.vjp(lambda o_, z_: o_ * _silu(z_), o_ref[...], za_ref[...])
        d_o, d_za = gate_vjp(d_ref[...])
        dproj_ref[:, za_at:] = d_za.astype(BF16)

        for j in range(KV_HEADS):
            js = slice(j * HEAD_DIM, (j + 1) * HEAD_DIM)
            bias_c, bias_p = bias_ref[j, 0], bias_ref[j, 1]
            sink = _sink_row(sk_ref, j)
            sink_loss = jnp.zeros((1, _GROUP_ROWS), F32)
            dk_from_next = jnp.where(step > 0, dk_carry[j], 0.0)
            dv_from_next = jnp.where(step > 0, dv_carry[j], 0.0)
            for t in reversed(range(per_step)):
                at = slice(t * BLOCK, (t + 1) * BLOCK)
                kc, vc = kc_ref[at, js], vc_ref[at, js]
                if t > 0:
                    before = slice((t - 1) * BLOCK, t * BLOCK)
                    kp, vp, bias_before = kc_ref[before, js], vc_ref[before, js], bias_p
                else:
                    kp, vp, bias_before = kp_ref[:, js], vp_ref[:, js], jnp.where(first_block > 0, bias_p, -jnp.inf)
                q4 = _stack_heads(q_ref[at, :], j)
                do4b = _stack_heads(d_o[at], j).astype(BF16)
                lse4 = _head_rows(lse_ref[t * N_HEADS:(t + 1) * N_HEADS, :], j)
                pc = jnp.exp(_dot_nt(kc, q4) + bias_c - lse4)
                pp = jnp.exp(_dot_nt(kp, q4) + bias_before - lse4)
                dpc = _dot_nt(vc, do4b)
                dpp = _dot_nt(vp, do4b)
                delta = jnp.sum(pc * dpc + pp * dpp, axis=0, keepdims=True)
                dsc = (pc * (dpc - delta)).astype(BF16)
                dsp = (pp * (dpp - delta)).astype(BF16)
                dq4 = ((_dot_tn(dsc, kc) + _dot_tn(dsp, kp)) * ATTN_SCALE).astype(BF16)
                sink_loss = sink_loss + jnp.exp(sink - lse4) * delta
                for g in range(Q_PER_KV):
                    h = j * Q_PER_KV + g
                    dproj_ref[at, q_at + h * HEAD_DIM:q_at + (h + 1) * HEAD_DIM] = dq4[g * BLOCK:(g + 1) * BLOCK]
                dproj_ref[at, k_at + j * HEAD_DIM:k_at + (j + 1) * HEAD_DIM] = (_dot(dsc, q4) + dk_from_next).astype(BF16)
                dproj_ref[at, v_at + j * HEAD_DIM:v_at + (j + 1) * HEAD_DIM] = (
                    _dot(pc.astype(BF16), do4b) + dv_from_next).astype(BF16)
                dk_from_next = _dot(dsp, q4)
                dv_from_next = _dot(pp.astype(BF16), do4b)
            dk_carry[j] = dk_from_next
            dv_carry[j] = dv_from_next
            for g in range(Q_PER_KV):
                h = j * Q_PER_KV + g
                dsk_ref[0:1, h:h + 1] -= jnp.sum(sink_loss[:, g * BLOCK:(g + 1) * BLOCK], axis=1, keepdims=True)

        tile = pl.program_id(0) * steps + steps - 1 - step
        du = _load_chunks(du_parts, tile * (tm // scan_steps), tm // scan_steps, scan_steps, slab)
        dproj_ref[:, 0:D_SSM] = du.astype(BF16)
        dproj_ref[:, D_SSM:2 * D_SSM] = dzs_ref[...]
        dw_ref[...] += _dot_tn(dproj_ref[...], hn_ref[...])

        @pl.when(jnp.logical_and(pl.program_id(0) == n_seq - 1, step == steps - 1))
        def _():
            dwb_ref[...] = dw_ref[...].astype(BF16)

    cur = lambda w: pl.BlockSpec((tm, w), lambda b, s: (b * steps + steps - 1 - s, 0))
    prev = lambda w: pl.BlockSpec((BLOCK, w), lambda b, s: (b * nb + jnp.maximum(nb - per_step * (s + 1) - 1, 0), 0))
    whole = lambda shape: pl.BlockSpec(shape, lambda b, s: (0, 0), pipeline_mode=pl.Buffered(1))
    return _call(
        body, name="attn_bwd_in_proj_bwd_early", grid=(n_seq, steps),
        in_specs=[cur(512), cur(128), prev(128), cur(128), prev(128), cur(512), cur(512),
                  pl.BlockSpec((per_step * N_HEADS, BLOCK), lambda b, s: (b * steps + steps - 1 - s, 0)), cur(512),
                  whole((1, N_HEADS)), cur(_EARLY_COLS)] + [whole((rows, LANES))] * _SCAN_PARTS
        + [cur(512), pl.BlockSpec(memory_space=pl.ANY)],
        out_specs=[cur(D_IN), whole((D_IN, _EARLY_COLS)), whole((D_IN, _EARLY_COLS)), whole((1, N_HEADS))],
        out_shape=[_sds((rows, D_IN), BF16), _sds((D_IN, _EARLY_COLS)), _sds((D_IN, _EARLY_COLS), BF16),
                   _sds((1, N_HEADS))],
        scratch_shapes=[_ATTN_BIAS_SCRATCH, pltpu.VMEM((KV_HEADS, BLOCK, HEAD_DIM), F32),
                        pltpu.VMEM((KV_HEADS, BLOCK, HEAD_DIM), F32)],
        compiler_params=_params(56, ("arbitrary", "arbitrary")),
    )(q, k, k, v, v, za, o, lse, d_ao, sinks, hn, *du_parts, dzs, runs_after)


def _local_step(x, hn, p, target, pre_norm_g, w_in_t, s5_params, s5_operands, ssm_d, w_glu, b_glu, sinks, w_out,
                post_norm_g, w_proj, w_gate, b_gate, send_tail_grads=lambda ready: ready["w_out"],
                send_bc=lambda d_bc: (d_bc, d_bc)):
    n_seq, seq, _ = x.shape
    rows = n_seq * seq
    x2d = x.reshape(rows, D_MODEL)
    p2d = p.reshape(rows, D_PLE)
    t2d = target.reshape(rows, D_MODEL)

    l_re, l_im, bt_re, bt_im, cm_re, cm_im = s5_operands

    u_scan, zs, q, k, v, za, o, attn_out, lse = _in_proj_attn_fwd(hn, w_in_t, sinks, n_seq, seq)
    y_scan, h_re, h_im = _s5_scan_fwd(u_scan, bt_re, bt_im, cm_re, cm_im, l_re, l_im, ssm_d, n_seq, seq)
    ssm_out = _glu_fwd(y_scan, zs, w_glu, b_glu, n_seq, seq)

    dh1, d_so, d_ao, d_w_out, d_w_gate, d_w_proj, d_b_gate, d_g2, loss = _tail(
        ssm_out, attn_out, x2d, p2d, t2d, w_out, post_norm_g, w_gate, b_gate, w_proj)

    dy_scan, dzs, d_w_glu, d_b_glu = _glu_bwd(y_scan, zs, d_so, w_glu, b_glu, n_seq, seq)
    du_scan, d_bt_re, d_bt_im, d_cm_re, d_cm_im, d_l_re, d_l_im, d_d = _s5_scan_bwd(
        dy_scan, u_scan, h_re, h_im, bt_re, bt_im, cm_re, cm_im, l_re, l_im, ssm_d, n_seq, seq)
    tail_grads_arrived = send_tail_grads(dict(w_out=d_w_out, pl_w_gate=d_w_gate, pl_w_proj=d_w_proj))
    d_lam_re, d_lam_im, d_log_step, d_bc = _s5_params_bwd(
        s5_params, (d_l_re, d_l_im, d_bt_re, d_bt_im, d_cm_re, d_cm_im), tail_grads_arrived)

    sent, arrived = send_bc(d_bc)
    d_proj, d_w_in_early, d_w_in_early_b, d_sinks = _attn_bwd_in_proj_bwd_early(
        q, k, v, za, o, lse, d_ao, sinks, hn, du_scan, dzs, sent, n_seq, seq)
    grad_x, d_w_in_late, d_g1 = _in_proj_bwd(x2d, dh1, pre_norm_g, w_in_t, d_proj, arrived)
    grads = dict(
        pre_norm_g=d_g1, w_in_early=d_w_in_early, w_in_early_bf16=d_w_in_early_b, w_in_late=d_w_in_late,
        ssm_lam_re=d_lam_re, ssm_lam_im=d_lam_im, ssm_log_step=d_log_step, ssm_bc=d_bc, ssm_d=d_d, ssm_w_glu=d_w_glu,
        ssm_b_glu=d_b_glu, attn_sinks=d_sinks, w_out=d_w_out, post_norm_g=d_g2, pl_w_proj=d_w_proj,
        pl_w_gate=d_w_gate, pl_b_gate=d_b_gate)
    return grad_x.reshape(x.shape), loss, grads


_BIG = ("w_in", "ssm_w_glu", "w_out", "pl_w_proj", "pl_w_gate")
_BIG_SHARD = {"w_in": (D_IN // N_CHIPS, D_MODEL), "ssm_w_glu": (D_SSM // N_CHIPS, D_SSM),
              "w_out": (D_MODEL // N_CHIPS, D_MODEL), "pl_w_proj": (D_PLE, D_MODEL // N_CHIPS),
              "pl_w_gate": (D_MODEL // N_CHIPS, D_MODEL)}
_SMALL = {"pre_norm_g": (1, D_MODEL), "ssm_lam_re": (SSM_GROUPS, SSM_STATE), "ssm_lam_im": (SSM_GROUPS, SSM_STATE),
          "ssm_log_step": (1, SSM_GROUPS), "ssm_b_re": (D_SSM, SSM_STATE), "ssm_b_im": (D_SSM, SSM_STATE),
          "ssm_c_re": (D_SSM, SSM_STATE), "ssm_c_im": (D_SSM, SSM_STATE), "ssm_d": (1, D_SSM), "ssm_b_glu": (1, D_SSM),
          "attn_sinks": (1, N_HEADS), "post_norm_g": (1, D_MODEL), "pl_b_gate": (1, D_MODEL)}
_VEC_ROWS = ("pre_norm_g", "post_norm_g", "pl_b_gate", "ssm_d", "ssm_b_glu", "attn_sinks", "ssm_log_step", "loss")
_SMALL_GROUPS = (
    ("vec", (8, D_MODEL), tuple((name, r) for r, name in enumerate(_VEC_ROWS))),
    ("lam", (2 * SSM_GROUPS, SSM_STATE), (("ssm_lam_re", 0), ("ssm_lam_im", SSM_GROUPS))),
)
_SMALL_EARLY = ("ssm_b_re", "ssm_b_im", "ssm_c_re", "ssm_c_im")
_SMALL_ORDER = tuple(name for _, _, members in _SMALL_GROUPS for name, _ in members) + _SMALL_EARLY
_WEIGHT_ORDER = ("pre_norm_g", "w_in", "ssm_lam_re", "ssm_lam_im", "ssm_log_step", "ssm_b_re", "ssm_b_im", "ssm_c_re",
                 "ssm_c_im", "ssm_d", "ssm_w_glu", "ssm_b_glu", "attn_sinks", "w_out", "post_norm_g", "pl_w_proj",
                 "pl_w_gate", "pl_b_gate")


def _small_shape(name):
    return (1, 1) if name == "loss" else _SMALL[name]


def _to_kernel_form(name, a):
    a = a[0]
    if name == "w_in":
        return a.T
    if name in ("ssm_b_re", "ssm_b_im"):
        a = a.transpose(0, 2, 1)
    return a.reshape(_SMALL[name]) if name in _SMALL else a


def _from_kernel_form(name, a, shape):
    if name == "w_in":
        a = a.T
    if name in ("ssm_b_re", "ssm_b_im"):
        a = a.reshape(SSM_GROUPS, SSM_GROUP_CH, SSM_STATE).transpose(0, 2, 1)
    return a.reshape(shape)


def _mesh_place():
    x, y, c = lax.axis_index("x"), lax.axis_index("y"), lax.axis_index("c")
    other_chips = ((1 - x, y), (x, 1 - y), (1 - x, 1 - y))
    return x, y, c, other_chips


def _gather_copies(s_refs, g_refs, send_sems, recv_sems, local_sems):
    x, y, c, other_chips = _mesh_place()
    started = []
    for i, (s_ref, g_ref) in enumerate(zip(s_refs, g_refs)):
        rows = s_ref.shape[0]
        half = rows // 2

        def block(chip, g_ref=g_ref, rows=rows, half=half):
            return g_ref.at[pl.ds((2 * chip[0] + chip[1]) * rows + c * half, half), :]

        def copy(k, chip, to, src=None, i=i, block=block):
            return pltpu.make_async_remote_copy(
                src_ref=block(chip) if src is None else src, dst_ref=block(chip), send_sem=send_sems.at[6 * i + k],
                recv_sem=recv_sems.at[6 * i + k], device_id=to, device_id_type=MESH)

        own = pltpu.make_async_copy(s_ref, g_ref.at[pl.ds((2 * x + y) * rows, rows), :], local_sems.at[i])
        own.start()
        first = [copy(k, (x, y), (*chip, c), src=s_ref.at[pl.ds(c * half, half), :])
                 for k, chip in enumerate(other_chips)]
        for cp in first:
            cp.start()
        passed = [copy(3 + k, chip, (x, y, 1 - c)) for k, chip in enumerate(other_chips)]
        started.append((own, first, passed))
    for own, first, passed in started:
        for k in range(3):
            first[k].wait_recv()
            passed[k].start()
    for own, first, passed in started:
        for k in range(3):
            passed[k].wait_recv()
        for cp in first + passed:
            cp.wait_send()
        own.wait()


def _gather_semaphores(n_t):
    return [pltpu.SemaphoreType.DMA((6 * n_t,)), pltpu.SemaphoreType.DMA((6 * n_t,)), pltpu.SemaphoreType.DMA((n_t,))]


def _gather_weights_beside(shards, name, collective_id):
    n_t = len(shards)
    hbm = pltpu.MemorySpace.HBM
    s_refs = [jax.new_ref(s, memory_space=hbm) for s in shards]
    g_refs = [jax.empty_ref(jax.ShapeDtypeStruct((N_CHIPS * s.shape[0], s.shape[1]), s.dtype), memory_space=hbm)
              for s in shards]

    def launch(send_sems, recv_sems, local_sems):
        x, y, c, other_chips = _mesh_place()
        peers = [(*chip, c) for chip in other_chips] + [(x, y, 1 - c)]
        barrier = pltpu.get_barrier_semaphore()
        for peer in peers:
            pl.semaphore_signal(barrier, inc=1, device_id=peer, device_id_type=MESH)
        pl.semaphore_wait(barrier, len(peers))
        _gather_copies(s_refs, g_refs, send_sems, recv_sems, local_sems)

    pl.kernel(launch, mesh=plsc.ScalarSubcoreMesh(axis_name="sequencer", num_cores=1), name=name,
              scratch_types=_gather_semaphores(n_t), compiler_params=pltpu.CompilerParams(collective_id=collective_id))()
    return [g[...] for g in g_refs]


_RELATIONS = tuple(((r >> 2) & 1, (r >> 1) & 1, r & 1) for r in range(1, 8))


def _related(place, relation):
    return tuple(1 - a if flip else a for a, flip in zip(place, relation))


def _scatter_beside(mats, name, collective_id):
    hbm = pltpu.MemorySpace.HBM
    src_refs = [jax.new_ref(a, memory_space=hbm) for a in mats]
    land_refs = [jax.empty_ref(jax.ShapeDtypeStruct((7, a.shape[0] // 8, a.shape[1]), a.dtype), memory_space=hbm)
                 for a in mats]

    def launch(send_sems, recv_sems):
        me = (lax.axis_index("x"), lax.axis_index("y"), lax.axis_index("c"))
        peers = [_related(me, rel) for rel in _RELATIONS]
        barrier = pltpu.get_barrier_semaphore()
        for peer in peers:
            pl.semaphore_signal(barrier, inc=1, device_id=peer, device_id_type=MESH)
        pl.semaphore_wait(barrier, len(peers))
        copies = []
        for i, (src, land) in enumerate(zip(src_refs, land_refs)):
            hr = land.shape[1]
            for k, (tx, ty, tc) in enumerate(peers):
                rows = pl.ds((2 * tx + ty) * 2 * hr + tc * hr, hr)
                copies.append(pltpu.make_async_remote_copy(
                    src_ref=src.at[rows, :], dst_ref=land.at[k], send_sem=send_sems.at[7 * i + k],
                    recv_sem=recv_sems.at[7 * i + k], device_id=(tx, ty, tc), device_id_type=MESH))
                copies[-1].start()
        for cp in copies:
            cp.wait()

    n_sems = 7 * len(mats)
    pl.kernel(launch, mesh=plsc.ScalarSubcoreMesh(axis_name="sequencer", num_cores=1), name=name,
              scratch_types=[pltpu.SemaphoreType.DMA((n_sems,)), pltpu.SemaphoreType.DMA((n_sems,))],
              compiler_params=pltpu.CompilerParams(collective_id=collective_id))()
    return [ref[...] for ref in land_refs]


def _broadcast_beside(arrays):
    hbm = pltpu.MemorySpace.HBM
    src_refs = [jax.new_ref(a, memory_space=hbm) for a in arrays]
    land_refs = [jax.empty_ref(jax.ShapeDtypeStruct((len(_RELATIONS),) + a.shape, a.dtype), memory_space=hbm)
                 for a in arrays]

    def launch(send_sems, recv_sems):
        me = (lax.axis_index("x"), lax.axis_index("y"), lax.axis_index("c"))
        peers = [_related(me, rel) for rel in _RELATIONS]
        barrier = pltpu.get_barrier_semaphore()
        for peer in peers:
            pl.semaphore_signal(barrier, inc=1, device_id=peer, device_id_type=MESH)
        pl.semaphore_wait(barrier, len(peers))
        copies = []
        for i, (src, land) in enumerate(zip(src_refs, land_refs)):
            for k, peer in enumerate(peers):
                copies.append(pltpu.make_async_remote_copy(
                    src_ref=src, dst_ref=land.at[k], send_sem=send_sems.at[7 * i + k],
                    recv_sem=recv_sems.at[7 * i + k], device_id=peer, device_id_type=MESH))
                copies[-1].start()
        for cp in copies:
            cp.wait()

    n_sems = 7 * len(arrays)
    pl.kernel(launch, mesh=plsc.ScalarSubcoreMesh(axis_name="sequencer", num_cores=1), name="broadcast_beside",
              scratch_types=[pltpu.SemaphoreType.DMA((n_sems,)), pltpu.SemaphoreType.DMA((n_sems,))],
              compiler_params=pltpu.CompilerParams(collective_id=3))()
    return [ref[...] for ref in land_refs]


def _exchange_grads(big, outputs, small, landed, own_bc, landed_bc):
    n_t = len(big)
    n_g = len(_SMALL_GROUPS)
    names = _SMALL_ORDER
    halves = [(b.shape[0] // N_CHIPS // 2, b.shape[1]) for b in big]
    early = sorted(landed)
    late = [i for i in range(n_t) if i not in landed]
    n_sems = 4 * n_g + 7 * len(late) + n_t
    small_sem0, block_sem0 = n_t, n_t + len(names)
    early_sem0 = block_sem0 + N_CHIPS * len(late)
    landed_sem0 = early_sem0 + 2 * len(early)
    sent = [n for n in names if n in small]

    def body(*refs):
        pos = 0

        def take(n):
            nonlocal pos
            pos += n
            return refs[pos - n:pos]

        big_refs, small_refs = take(n_t), dict(zip(sent, take(len(sent))))
        land_refs = dict(zip(early, take(len(early))))
        own_bc_ref, landed_bc_ref = take(2)
        out_refs, small_out_refs = take(len(outputs)), dict(zip(names, take(len(names))))
        per_late = lambda: dict(zip(late, take(len(late))))
        ga, gb, pme, send_b, recv_b = per_late(), per_late(), take(n_t), per_late(), per_late()
        own_e, land_e = dict(zip(early, take(len(early)))), dict(zip(early, take(len(early))))
        own_s, land_s = take(2)
        s_own, s_sib, s_chips, s_pair = take(n_g), take(n_g), take(n_g), take(n_g)
        stage = dict(zip(names, take(len(names))))
        send_sems, recv_sems, local_sems = take(3)
        x, y, c, other_chips = _mesh_place()
        me = 2 * x + y
        sibling = (x, y, 1 - c)
        sem_at = iter(range(n_sems))

        def remote(src, dst, to):
            k = next(sem_at)
            return pltpu.make_async_remote_copy(src_ref=src, dst_ref=dst, send_sem=send_sems.at[k],
                                                recv_sem=recv_sems.at[k], device_id=to, device_id_type=MESH)

        loads = [pltpu.make_async_copy(small_refs[name], stage[name], local_sems.at[small_sem0 + names.index(name)])
                 for name in sent]
        landed_loads = [pltpu.make_async_copy(own_bc_ref, own_s, local_sems.at[landed_sem0]),
                        pltpu.make_async_copy(landed_bc_ref, land_s, local_sems.at[landed_sem0 + 1])]
        for cp in loads + landed_loads:
            cp.start()
        for cp in loads:
            cp.wait()
        small_swaps = []
        for gi, (_, _, members) in enumerate(_SMALL_GROUPS):
            s_own[gi][...] = jnp.zeros_like(s_own[gi])
            for name, r0 in members:
                r, n = _small_shape(name)
                s_own[gi][r0:r0 + r, 0:n] = stage[name][...]
            small_swaps.append(remote(s_own[gi], s_sib[gi], sibling))
            small_swaps[gi].start()
        order = sorted(late, key=lambda i: halves[i][0] * halves[i][1])
        own_loads, big_swaps = {}, {}
        for i in order:
            hr = halves[i][0]
            own_loads[i], big_swaps[i] = [], []
            for j in range(N_CHIPS):
                mine = big_refs[i].at[pl.ds(j * 2 * hr + c * hr, hr), :]
                theirs = big_refs[i].at[pl.ds(j * 2 * hr + (1 - c) * hr, hr), :]
                sem = local_sems.at[block_sem0 + N_CHIPS * late.index(i) + j]
                own_loads[i].append(pltpu.make_async_copy(mine, ga[i].at[j], sem))
                own_loads[i][j].start()
                big_swaps[i].append(remote(theirs, gb[i].at[j], sibling))
                big_swaps[i][j].start()
        early_loads = {}
        for e, i in enumerate(early):
            hr = halves[i][0]
            mine = big_refs[i].at[pl.ds(me * 2 * hr + c * hr, hr), :]
            early_loads[i] = [pltpu.make_async_copy(mine, own_e[i], local_sems.at[early_sem0 + 2 * e]),
                              pltpu.make_async_copy(land_refs[i], land_e[i], local_sems.at[early_sem0 + 2 * e + 1])]
            for cp in early_loads[i]:
                cp.start()
        small_sends = []
        for gi in range(n_g):
            small_swaps[gi].wait_recv()
            s_pair[gi][...] = s_own[gi][...] + s_sib[gi][...]
            small_sends.append([remote(s_pair[gi], s_chips[gi].at[k], (*chip, c)) for k, chip in enumerate(other_chips)])
            for cp in small_sends[gi]:
                cp.start()

        def pair_sum(i, j):
            return ga[i][j] + gb[i][j]

        big_sends = {}
        for i in order:
            for j in range(N_CHIPS):
                own_loads[i][j].wait()
                big_swaps[i][j].wait_recv()
            big_sends[i] = []
            for k, chip in enumerate(other_chips):
                send_b[i][k] = pair_sum(i, 2 * chip[0] + chip[1]).astype(BF16)
                big_sends[i].append(remote(send_b[i].at[k], recv_b[i].at[k], (*chip, c)))
                big_sends[i][k].start()
        last_swaps, keeps = {}, {}
        for i in early + order:
            hr = halves[i][0]
            if i in landed:
                for cp in early_loads[i]:
                    cp.wait()
                total = own_e[i][...]
                for k in range(len(_RELATIONS)):
                    total = total + land_e[i][k].astype(F32)
                pme[i][...] = total
            else:
                for k in range(3):
                    big_sends[i][k].wait_recv()
                pme[i][...] = ((pair_sum(i, me) + recv_b[i][0].astype(F32)) + recv_b[i][1].astype(F32)) + recv_b[i][2].astype(F32)
            o, = [o for o, group in enumerate(outputs) if i in group]
            first_col = sum(halves[j][1] for j in outputs[o][:outputs[o].index(i)])
            mine = out_refs[o].at[pl.ds(c * hr, hr), pl.ds(first_col, halves[i][1])]
            keeps[i] = pltpu.make_async_copy(pme[i], mine, local_sems.at[i])
            keeps[i].start()
            last_swaps[i] = remote(pme[i], mine, sibling)
            last_swaps[i].start()

        for gi, (_, _, members) in enumerate(_SMALL_GROUPS):
            for k in range(3):
                small_sends[gi][k].wait_recv()
            total = None
            for j in range(N_CHIPS):
                rel = jnp.bitwise_xor(j, me)
                term = jnp.where(rel == 0, s_pair[gi][...], jnp.where(
                    rel == 2, s_chips[gi][0], jnp.where(rel == 1, s_chips[gi][1], s_chips[gi][2])))
                total = term if total is None else total + term
            s_sib[gi][...] = total
            for name, r0 in members:
                r, n = _small_shape(name)
                stage[name][...] = s_sib[gi][r0:r0 + r, 0:n]
        my_index = 4 * x + 2 * y + c
        for cp in landed_loads:
            cp.wait()
        total = None
        for d in range(2 * N_CHIPS):
            rel = jnp.bitwise_xor(d, my_index)
            term = own_s[...]
            for k in range(len(_RELATIONS)):
                term = jnp.where(rel == k + 1, land_s[k], term)
            total = term.astype(F32) if total is None else total + term.astype(F32)
        for a, name in enumerate(_SMALL_EARLY):
            stage[name][...] = total[:, a * SSM_STATE:(a + 1) * SSM_STATE]
        stores = [pltpu.make_async_copy(stage[name], small_out_refs[name], local_sems.at[small_sem0 + a])
                  for a, name in enumerate(names)]
        for cp in stores:
            cp.start()

        for i in range(n_t):
            last_swaps[i].wait_recv()
            keeps[i].wait()
        for cp in stores:
            cp.wait()
        groups = list(big_swaps.values()) + small_sends + list(big_sends.values())
        for cp in small_swaps + [cp for group in groups for cp in group] + list(last_swaps.values()):
            cp.wait_send()

    any_spec = pl.BlockSpec(memory_space=pl.ANY)
    small_shapes = [_sds(_small_shape(n)) for n in names]
    group_shapes = [shape for _, shape, _ in _SMALL_GROUPS]
    vmem = lambda which, dtype, lead=(): [pltpu.VMEM(lead + halves[i], dtype) for i in which]
    outs = _call(
        body, name="exchange_grads",
        in_specs=[any_spec] * (n_t + len(sent) + len(early) + 2),
        out_specs=[any_spec] * (len(outputs) + len(names)),
        out_shape=[_sds((big[group[0]].shape[0] // N_CHIPS, sum(big[i].shape[1] for i in group))) for group in outputs]
        + small_shapes,
        scratch_shapes=(vmem(late, F32, (N_CHIPS,)) + vmem(late, F32, (N_CHIPS,)) + vmem(range(n_t), F32)
                        + vmem(late, BF16, (3,)) + vmem(late, BF16, (3,))
                        + vmem(early, F32)
                        + [pltpu.VMEM((len(_RELATIONS),) + halves[i], landed[i].dtype) for i in early]
                        + [pltpu.VMEM(own_bc.shape, own_bc.dtype), pltpu.VMEM(landed_bc.shape, landed_bc.dtype)]
                        + [pltpu.VMEM(s, F32) for s in group_shapes] * 2 + [pltpu.VMEM((3,) + s, F32) for s in group_shapes]
                        + [pltpu.VMEM(s, F32) for s in group_shapes]
                        + [pltpu.VMEM(_small_shape(n), F32) for n in names]
                        + [pltpu.SemaphoreType.DMA((n_sems,)), pltpu.SemaphoreType.DMA((n_sems,)),
                           pltpu.SemaphoreType.DMA((landed_sem0 + 2,))]),
        compiler_params=_params(48),
    )(*big, *[small[n] for n in sent], *[landed[i] for i in early], own_bc, landed_bc)
    return list(outs[:len(outputs)]), dict(zip(names, outs[len(outputs):]))


def _adamw_update(w, g, m, v):
    m = ADAM_B1 * m + (1.0 - ADAM_B1) * g
    v = ADAM_B2 * v + (1.0 - ADAM_B2) * (g * g)
    m_hat = m / (1.0 - ADAM_B1 ** ADAM_STEP)
    v_hat = v / (1.0 - ADAM_B2 ** ADAM_STEP)
    return -ADAM_LR * (m_hat / (jnp.sqrt(v_hat) + ADAM_EPS) + ADAM_WD * w), m, v


def _adamw(w, g, m, v, grid, name):
    n_t = len(w)

    def body(*refs):
        ins, outs = refs[:4 * n_t], refs[4 * n_t:]
        for i in range(n_t):
            w_, g_, m_, v_ = [ins[a * n_t + i][...] for a in range(4)]
            vals = (g_,) + _adamw_update(w_, g_, m_, v_)
            for a in range(4):
                outs[a * n_t + i][...] = vals[a]

    specs = [pl.BlockSpec((a.shape[0] // grid, a.shape[1]), lambda i: (i, 0)) for a in w]
    shapes = [_sds(a.shape) for a in w]
    outs = _call(
        body, name=name, grid=(grid,), in_specs=specs * 4, out_specs=specs * 4, out_shape=shapes * 4,
        compiler_params=_params(40, ("arbitrary",)),
    )(*w, *g, *m, *v)
    return [outs[a * n_t:(a + 1) * n_t] for a in range(4)]


def kernel(x, p, pre_norm_g, w_in, ssm_lam_re, ssm_lam_im, ssm_log_step, ssm_b_re, ssm_b_im, ssm_c_re, ssm_c_im, ssm_d, ssm_w_glu, ssm_b_glu, attn_sinks, w_out, post_norm_g, pl_w_proj, pl_w_gate, pl_b_gate, loss_target, m_pre_norm_g, m_w_in, m_ssm_lam_re, m_ssm_lam_im, m_ssm_log_step, m_ssm_b_re, m_ssm_b_im, m_ssm_c_re, m_ssm_c_im, m_ssm_d, m_ssm_w_glu, m_ssm_b_glu, m_attn_sinks, m_w_out, m_post_norm_g, m_pl_w_proj, m_pl_w_gate, m_pl_b_gate, v_pre_norm_g, v_w_in, v_ssm_lam_re, v_ssm_lam_im, v_ssm_log_step, v_ssm_b_re, v_ssm_b_im, v_ssm_c_re, v_ssm_c_im, v_ssm_d, v_ssm_w_glu, v_ssm_b_glu, v_attn_sinks, v_w_out, v_post_norm_g, v_pl_w_proj, v_pl_w_gate, v_pl_b_gate):
    weights = dict(pre_norm_g=pre_norm_g, w_in=w_in, ssm_lam_re=ssm_lam_re, ssm_lam_im=ssm_lam_im,
                   ssm_log_step=ssm_log_step, ssm_b_re=ssm_b_re, ssm_b_im=ssm_b_im, ssm_c_re=ssm_c_re,
                   ssm_c_im=ssm_c_im, ssm_d=ssm_d, ssm_w_glu=ssm_w_glu, ssm_b_glu=ssm_b_glu, attn_sinks=attn_sinks,
                   w_out=w_out, post_norm_g=post_norm_g, pl_w_proj=pl_w_proj, pl_w_gate=pl_w_gate, pl_b_gate=pl_b_gate)
    m_in = dict(pre_norm_g=m_pre_norm_g, w_in=m_w_in, ssm_lam_re=m_ssm_lam_re, ssm_lam_im=m_ssm_lam_im,
                ssm_log_step=m_ssm_log_step, ssm_b_re=m_ssm_b_re, ssm_b_im=m_ssm_b_im, ssm_c_re=m_ssm_c_re,
                ssm_c_im=m_ssm_c_im, ssm_d=m_ssm_d, ssm_w_glu=m_ssm_w_glu, ssm_b_glu=m_ssm_b_glu,
                attn_sinks=m_attn_sinks, w_out=m_w_out, post_norm_g=m_post_norm_g, pl_w_proj=m_pl_w_proj,
                pl_w_gate=m_pl_w_gate, pl_b_gate=m_pl_b_gate)
    v_in = dict(pre_norm_g=v_pre_norm_g, w_in=v_w_in, ssm_lam_re=v_ssm_lam_re, ssm_lam_im=v_ssm_lam_im,
                ssm_log_step=v_ssm_log_step, ssm_b_re=v_ssm_b_re, ssm_b_im=v_ssm_b_im, ssm_c_re=v_ssm_c_re,
                ssm_c_im=v_ssm_c_im, ssm_d=v_ssm_d, ssm_w_glu=v_ssm_w_glu, ssm_b_glu=v_ssm_b_glu,
                attn_sinks=v_attn_sinks, w_out=v_w_out, post_norm_g=v_post_norm_g, pl_w_proj=v_pl_w_proj,
                pl_w_gate=v_pl_w_gate, pl_b_gate=v_pl_b_gate)

    def two_d(tree):
        return {k: _to_kernel_form(k, a) for k, a in tree.items()}

    w2, m2, v2 = two_d(weights), two_d(m_in), two_d(v_in)

    (w_in_full,) = _gather_weights_beside([w2["w_in"].astype(BF16)], "gather_w_in_beside", 4)
    s5_params = tuple(w2[n] for n in ("ssm_lam_re", "ssm_lam_im", "ssm_log_step", "ssm_b_re", "ssm_b_im", "ssm_c_re",
                                      "ssm_c_im"))
    s5_operands = _s5_params_fwd(*s5_params)
    hn = _pre_norm(x.reshape(-1, D_MODEL), w2["pre_norm_g"])
    behind = s5_operands[0][0, 0] * 0.0 + hn[0, 0].astype(F32) * 0.0
    rest = _gather_weights_beside([(w2[n] + behind).astype(BF16) for n in _BIG[1:]], "gather_weights_beside", 1)
    full = dict(zip(_BIG, [w_in_full] + rest))
    mats = ("w_in_early", "w_in_late") + _BIG[1:]
    landed, bc = {}, {}

    def send_tail_grads(ready):
        sent_early = ("w_out", "pl_w_gate", "pl_w_proj")
        landed.update(zip([mats.index(n) for n in sent_early],
                          _scatter_beside([ready[n] for n in sent_early], "scatter_beside", 2)))
        return landed[mats.index(sent_early[-1])]

    def send_bc(d_bc):
        bc["own"] = d_bc
        bc["landed"] = _broadcast_beside([d_bc])[0]
        return d_bc, bc["landed"]

    grad_x, loss, grads = _local_step(
        x, hn, p, loss_target, w2["pre_norm_g"], full["w_in"], s5_params, s5_operands, w2["ssm_d"], full["ssm_w_glu"], w2["ssm_b_glu"],
        w2["attn_sinks"], full["w_out"], w2["post_norm_g"], full["pl_w_proj"], full["pl_w_gate"], w2["pl_b_gate"], send_tail_grads, send_bc)

    landed[0], landed[mats.index("ssm_w_glu")] = _scatter_beside(
        [grads["w_in_early_bf16"], grads["ssm_w_glu"]], "scatter_w_in_beside", 5)
    sent_here = {**{n: grads[n] for n in _SMALL if n not in _SMALL_EARLY}, "loss": loss}
    halves_of_w_in = ((0, 1),) + tuple((i,) for i in range(2, len(mats)))
    g_big, g_small = _exchange_grads([grads[n] for n in mats], halves_of_w_in, sent_here, landed, bc["own"], bc["landed"])
    g_big = dict(zip(_BIG, g_big))
    total_loss = g_small.pop("loss")

    big_out = _adamw([w2[n] for n in _BIG], [g_big[n] for n in _BIG], [m2[n] for n in _BIG], [v2[n] for n in _BIG],
                     8, "adamw_matrices")
    small_names = tuple(_SMALL)
    small_out = _adamw([w2[n] for n in small_names], [g_small[n] for n in small_names], [m2[n] for n in small_names],
                       [v2[n] for n in small_names], 1, "adamw_small")

    results = [{**dict(zip(_BIG, big_part)), **dict(zip(small_names, small_part))}
               for big_part, small_part in zip(big_out, small_out)]
    flat = [_from_kernel_form(name, r[name], weights[name].shape) for r in results for name in _WEIGHT_ORDER]
    return (total_loss.reshape(()), grad_x, *flat)
```
